```python
import jax, jax.numpy as jnp
from jax import lax
import numpy as np

D_MODEL = 2048
BATCH = 8
SEQ = 8192
DEPTH = 1

CTX_LEN = 256
GRID_W = 64
RET_HEADS = 8
RET_DK = 64
RET_DV = 128
RET_CHUNK = 128
ATT_HEADS = 16
ATT_KV_HEADS = 4
ATT_DH = 64
ATT_GROUP = ATT_HEADS // ATT_KV_HEADS
WINDOW = 128
ATT_BLOCK = 128
D_FF = -(-(8 * D_MODEL) // (3 * 256)) * 256
ROPE_BASE = 10000.0
NORM_EPS = 1e-6
PROJ_SIZES = (RET_HEADS * RET_DK, RET_HEADS * RET_DK, RET_HEADS * RET_DV, RET_HEADS * RET_DV,
              ATT_HEADS * ATT_DH, ATT_KV_HEADS * ATT_DH, ATT_KV_HEADS * ATT_DH)
D_PROJ = sum(PROJ_SIZES)
D_MIX_OUT = RET_HEADS * RET_DV + ATT_HEADS * ATT_DH

kernel_name = "hybrid_retention_window_gqa_dit_layer"


def _rmsnorm(x, g):
    xf = x.astype(jnp.float32)
    y = xf * lax.rsqrt(jnp.mean(xf * xf, axis=-1, keepdims=True) + NORM_EPS)
    return (y * g.astype(jnp.float32)).astype(x.dtype)


def _modulate(x, g, shift, scale):
    return _rmsnorm(x, g) * (1.0 + scale) + shift


def _split_proj(p):
    idx = [int(v) for v in np.cumsum(PROJ_SIZES)[:-1]]
    return jnp.split(p, idx, axis=-1)


def _heads(p, h, d):
    return p.reshape(p.shape[0], p.shape[1], h, d)


def _rope(x, pos):
    half = x.shape[-1] // 2
    inv = ROPE_BASE ** (-jnp.arange(half, dtype=jnp.float32) / half)
    ang = pos.astype(jnp.float32)[:, None] * inv[None, :]
    cos = jnp.cos(ang)[:, None, :]
    sin = jnp.sin(ang)[:, None, :]
    x1 = x[..., :half].astype(jnp.float32)
    x2 = x[..., half:].astype(jnp.float32)
    return jnp.concatenate([x1 * cos - x2 * sin, x1 * sin + x2 * cos], axis=-1).astype(x.dtype)


def _axial_rope(x, rows, cols):
    half = x.shape[-1] // 2
    return jnp.concatenate([_rope(x[..., :half], rows), _rope(x[..., half:], cols)], axis=-1)


def _ret_state(k, v, log_g):
    L = k.shape[1]
    w = jnp.exp(log_g[None, :] * (L - 1 - jnp.arange(L, dtype=jnp.float32))[:, None])
    return jnp.einsum('blhd,blhv->bhdv', k * w[None, :, :, None], v).astype(jnp.float32)


def _retention_chunkwise(q, k, v, log_g, s0, include_diag):
    b, L, h, dk = q.shape
    dv = v.shape[-1]
    C = RET_CHUNK
    n = L // C
    qc = q.reshape(b, n, C, h, dk)
    kc = k.reshape(b, n, C, h, dk)
    vc = v.reshape(b, n, C, h, dv)
    pos = jnp.arange(C, dtype=jnp.float32)
    rel = pos[:, None] - pos[None, :]
    mask = (rel >= 0) if include_diag else (rel > 0)
    decay_in = jnp.where(mask[None], jnp.exp(log_g[:, None, None] * jnp.maximum(rel, 0.0)[None]), 0.0)
    scores = jnp.einsum('bnihd,bnjhd->bnhij', qc, kc) * decay_in[None, None]
    o_in = jnp.einsum('bnhij,bnjhv->bnihv', scores, vc)
    k_w = jnp.exp(log_g[None, :] * (C - 1 - pos)[:, None])
    kv_chunk = jnp.einsum('bnjhd,bnjhv->nbhdv', kc * k_w[:, :, None], vc).astype(jnp.float32)
    g_chunk = jnp.exp(log_g * C)[None, :, None, None]

    def step(s, kv):
        return g_chunk * s + kv, s

    _, s_prev = lax.scan(step, s0.astype(jnp.float32), kv_chunk)
    q_w = jnp.exp(log_g[None, :] * (pos + 1.0)[:, None])
    o_x = jnp.einsum('bnihd,nbhdv->bnihv', qc * q_w[:, :, None], s_prev)
    return (o_in + o_x).reshape(b, L, h, dv)


def _bidir_retention(q, k, v, gate, lg_f, lg_b, s_f, s_b):
    o_f = _retention_chunkwise(q, k, v, lg_f, s_f, True)
    o_b = _retention_chunkwise(q[:, ::-1], k[:, ::-1], v[:, ::-1], lg_b, s_b, False)[:, ::-1]
    o = (o_f + o_b).astype(jnp.float32)
    o = o * lax.rsqrt(jnp.mean(o * o, axis=-1, keepdims=True) + NORM_EPS)
    o = o.reshape(o.shape[0], o.shape[1], RET_HEADS * RET_DV)
    return (o * jax.nn.silu(gate.astype(jnp.float32))).astype(gate.dtype)


def _window_attention(q, k, v, k_ctx, v_ctx, sink):
    b, L = q.shape[0], q.shape[1]
    Lc = k_ctx.shape[1]
    Bk = ATT_BLOCK
    n = L // Bk
    qb = q.reshape(b, n, Bk, ATT_KV_HEADS, ATT_GROUP, ATT_DH)
    pad = ((0, 0), (Bk, Bk), (0, 0), (0, 0))
    kp = jnp.pad(k, pad)
    vp = jnp.pad(v, pad)
    kb = jnp.concatenate([kp[:, j * Bk:j * Bk + L].reshape(b, n, Bk, ATT_KV_HEADS, ATT_DH) for j in range(3)], axis=2)
    vb = jnp.concatenate([vp[:, j * Bk:j * Bk + L].reshape(b, n, Bk, ATT_KV_HEADS, ATT_DH) for j in range(3)], axis=2)
    scale = ATT_DH ** -0.5
    s_loc = jnp.einsum('bnqkgd,bnskd->bnkgqs', qb, kb).astype(jnp.float32) * scale
    s_ctx = jnp.einsum('bnqkgd,bckd->bnkgqc', qb, k_ctx).astype(jnp.float32) * scale
    blk = jnp.arange(n)[:, None]
    qpos = (blk * Bk + jnp.arange(Bk)[None, :])[:, :, None]
    kpos = ((blk - 1) * Bk + jnp.arange(3 * Bk)[None, :])[:, None, :]
    valid = (jnp.abs(qpos - kpos) <= WINDOW) & (kpos >= 0) & (kpos < L)
    s_loc = jnp.where(valid[None, :, None, None], s_loc, -jnp.inf)
    sink_b = jnp.broadcast_to(sink.astype(jnp.float32).reshape(ATT_KV_HEADS, ATT_GROUP)[None, None, :, :, None, None],
                              s_loc.shape[:-1] + (1,))
    p = jax.nn.softmax(jnp.concatenate([s_loc, s_ctx, sink_b], axis=-1), axis=-1)
    p_loc = p[..., :3 * Bk].astype(v.dtype)
    p_ctx = p[..., 3 * Bk:3 * Bk + Lc].astype(v.dtype)
    o = jnp.einsum('bnkgqs,bnskd->bnqkgd', p_loc, vb) + jnp.einsum('bnkgqc,bckd->bnqkgd', p_ctx, v_ctx)
    return o.reshape(b, L, ATT_HEADS * ATT_DH)


def _context_attention(q, k, v, sink):
    b, Lc = q.shape[0], q.shape[1]
    qg = q.reshape(b, Lc, ATT_KV_HEADS, ATT_GROUP, ATT_DH)
    s = jnp.einsum('bqkgd,bckd->bkgqc', qg, k).astype(jnp.float32) * (ATT_DH ** -0.5)
    sink_b = jnp.broadcast_to(sink.astype(jnp.float32).reshape(ATT_KV_HEADS, ATT_GROUP)[None, :, :, None, None],
                              s.shape[:-1] + (1,))
    p = jax.nn.softmax(jnp.concatenate([s, sink_b], axis=-1), axis=-1)[..., :Lc].astype(v.dtype)
    o = jnp.einsum('bkgqc,bckd->bqkgd', p, v)
    return o.reshape(b, Lc, ATT_HEADS * ATT_DH)


def _swiglu(h, w_gate, w_up, w_down):
    return (jax.nn.silu(h @ w_gate) * (h @ w_up)) @ w_down


def _fwd_setup_inputs(seed: int = 0) -> dict:
    key = jax.random.key(seed)
    ks = jax.random.split(key, 18)

    def nrm(k, shape, scale):
        return jax.random.normal(k, shape, jnp.float32) * scale

    base_decay = np.log(-np.log1p(-2.0 ** (-5.0 - np.arange(RET_HEADS)))).astype(np.float32)
    return {
        "x": nrm(ks[0], (BATCH, SEQ, D_MODEL), 1.0),
        "c": nrm(ks[1], (BATCH, D_MODEL), 1.0),
        "ctx": nrm(ks[2], (BATCH, CTX_LEN, D_MODEL), 1.0),
        "c_ctx": nrm(ks[3], (D_MODEL,), 1.0),
        "w_mod": nrm(ks[4], (DEPTH, D_MODEL, 6 * D_MODEL), 0.5 * D_MODEL ** -0.5),
        "b_mod": nrm(ks[5], (DEPTH, 6 * D_MODEL), 0.02),
        "norm_mix": 1.0 + nrm(ks[6], (DEPTH, D_MODEL), 0.02),
        "norm_ffn": 1.0 + nrm(ks[7], (DEPTH, D_MODEL), 0.02),
        "w_in": nrm(ks[8], (DEPTH, D_MODEL, D_PROJ), D_MODEL ** -0.5),
        "ret_decay": jnp.asarray(base_decay)[None, None, :] + nrm(ks[9], (DEPTH, 2, RET_HEADS), 0.05),
        "attn_sink": nrm(ks[10], (DEPTH, ATT_HEADS), 0.5),
        "w_out": nrm(ks[11], (DEPTH, D_MIX_OUT, D_MODEL), D_MIX_OUT ** -0.5),
        "w_gate": nrm(ks[12], (DEPTH, D_MODEL, D_FF), D_MODEL ** -0.5),
        "w_up": nrm(ks[13], (DEPTH, D_MODEL, D_FF), D_MODEL ** -0.5),
        "w_down": nrm(ks[14], (DEPTH, D_FF, D_MODEL), D_FF ** -0.5),
        "norm_final": 1.0 + nrm(ks[15], (D_MODEL,), 0.02),
    }


def _fwd_reference(x, c, ctx, c_ctx, w_mod, b_mod, norm_mix, norm_ffn, w_in, ret_decay, attn_sink,
              w_out, w_gate, w_up, w_down, norm_final):
    L = x.shape[1]
    ROWS = L // GRID_W
    t = jnp.arange(L)
    rows = jnp.repeat(jnp.arange(ROWS), GRID_W)
    cols = jnp.tile(jnp.arange(GRID_W), ROWS)
    k_scale = RET_DK ** -0.5
    xc = ctx
    for l in range(DEPTH):
        last = l == DEPTH - 1
        mod = (jax.nn.silu(c) @ w_mod[l] + b_mod[l])[:, None, :]
        mod_c = (jax.nn.silu(c_ctx) @ w_mod[l] + b_mod[l])[None, None, :]
        sh_m, sc_m, gt_m, sh_f, sc_f, gt_f = jnp.split(mod, 6, axis=-1)
        sh_mc, sc_mc, gt_mc, sh_fc, sc_fc, gt_fc = jnp.split(mod_c, 6, axis=-1)
        lg_f = -jnp.exp(ret_decay[l, 0].astype(jnp.float32))
        lg_b = -jnp.exp(ret_decay[l, 1].astype(jnp.float32))

        hx = _modulate(x, norm_mix[l], sh_m, sc_m)
        hc = _modulate(xc, norm_mix[l], sh_mc, sc_mc)
        rq, rk, rv, rg, aq, ak, av = _split_proj(hx @ w_in[l])
        crq, crk, crv, crg, caq, cak, cav = _split_proj(hc @ w_in[l])

        crk = _heads(crk, RET_HEADS, RET_DK) * k_scale
        crv = _heads(crv, RET_HEADS, RET_DV)
        s_f = _ret_state(crk, crv, lg_f)
        s_b = _ret_state(crk[:, ::-1], crv[:, ::-1], lg_b)
        q_r = _rope(_heads(rq, RET_HEADS, RET_DK), t)
        k_r = _rope(_heads(rk, RET_HEADS, RET_DK), t) * k_scale
        v_r = _heads(rv, RET_HEADS, RET_DV)
        y_ret = _bidir_retention(q_r, k_r, v_r, rg, lg_f, lg_b, s_f, s_b)

        cak = _heads(cak, ATT_KV_HEADS, ATT_DH)
        cav = _heads(cav, ATT_KV_HEADS, ATT_DH)
        q_a = _axial_rope(_heads(aq, ATT_HEADS, ATT_DH), rows, cols)
        k_a = _axial_rope(_heads(ak, ATT_KV_HEADS, ATT_DH), rows, cols)
        v_a = _heads(av, ATT_KV_HEADS, ATT_DH)
        y_att = _window_attention(q_a, k_a, v_a, cak, cav, attn_sink[l])

        x = x + gt_m * (jnp.concatenate([y_ret, y_att], axis=-1) @ w_out[l])

        if not last:
            zero = jnp.zeros((xc.shape[0], RET_HEADS, RET_DK, RET_DV), jnp.float32)
            y_ret_c = _bidir_retention(_heads(crq, RET_HEADS, RET_DK), crk, crv, crg, lg_f, lg_b, zero, zero)
            y_att_c = _context_attention(_heads(caq, ATT_HEADS, ATT_DH), cak, cav, attn_sink[l])
            xc = xc + gt_mc * (jnp.concatenate([y_ret_c, y_att_c], axis=-1) @ w_out[l])
            xc = xc + gt_fc * _swiglu(_modulate(xc, norm_ffn[l], sh_fc, sc_fc), w_gate[l], w_up[l], w_down[l])

        x = x + gt_f * _swiglu(_modulate(x, norm_ffn[l], sh_f, sc_f), w_gate[l], w_up[l], w_down[l])
    return _rmsnorm(x, norm_final)


import jax as _jax
import jax.numpy as _jnp

TWIN_FORMAT = 'train_step'
FWD_PARAMS = ['x', 'c', 'ctx', 'c_ctx', 'w_mod', 'b_mod', 'norm_mix', 'norm_ffn', 'w_in', 'ret_decay', 'attn_sink', 'w_out', 'w_gate', 'w_up', 'w_down', 'norm_final']
TWIN_WEIGHTS = ['c_ctx', 'w_mod', 'b_mod', 'norm_mix', 'norm_ffn', 'w_in', 'ret_decay', 'attn_sink', 'w_out', 'w_gate', 'w_up', 'w_down', 'norm_final']
TWIN_DIFF_INPUT = 'x'
TWIN_INPUTS = ['x', 'c', 'ctx', 'c_ctx', 'w_mod', 'b_mod', 'norm_mix', 'norm_ffn', 'w_in', 'ret_decay', 'attn_sink', 'w_out', 'w_gate', 'w_up', 'w_down', 'norm_final', 'loss_target', 'm_c_ctx', 'm_w_mod', 'm_b_mod', 'm_norm_mix', 'm_norm_ffn', 'm_w_in', 'm_ret_decay', 'm_attn_sink', 'm_w_out', 'm_w_gate', 'm_w_up', 'm_w_down', 'm_norm_final', 'v_c_ctx', 'v_w_mod', 'v_b_mod', 'v_norm_mix', 'v_norm_ffn', 'v_w_in', 'v_ret_decay', 'v_attn_sink', 'v_w_out', 'v_w_gate', 'v_w_up', 'v_w_down', 'v_norm_final']
TWIN_OUTPUTS = ['loss', 'grad_x', 'grad_c_ctx', 'grad_w_mod', 'grad_b_mod', 'grad_norm_mix', 'grad_norm_ffn', 'grad_w_in', 'grad_ret_decay', 'grad_attn_sink', 'grad_w_out', 'grad_w_gate', 'grad_w_up', 'grad_w_down', 'grad_norm_final', 'delta_c_ctx', 'delta_w_mod', 'delta_b_mod', 'delta_norm_mix', 'delta_norm_ffn', 'delta_w_in', 'delta_ret_decay', 'delta_attn_sink', 'delta_w_out', 'delta_w_gate', 'delta_w_up', 'delta_w_down', 'delta_norm_final', 'new_m_c_ctx', 'new_m_w_mod', 'new_m_b_mod', 'new_m_norm_mix', 'new_m_norm_ffn', 'new_m_w_in', 'new_m_ret_decay', 'new_m_attn_sink', 'new_m_w_out', 'new_m_w_gate', 'new_m_w_up', 'new_m_w_down', 'new_m_norm_final', 'new_v_c_ctx', 'new_v_w_mod', 'new_v_b_mod', 'new_v_norm_mix', 'new_v_norm_ffn', 'new_v_w_in', 'new_v_ret_decay', 'new_v_attn_sink', 'new_v_w_out', 'new_v_w_gate', 'new_v_w_up', 'new_v_w_down', 'new_v_norm_final']
TWIN_LEAF_KINDS = {'loss': 'loss', 'grad_x': 'grad_x', 'grad_c_ctx': 'grad_w', 'grad_w_mod': 'grad_w', 'grad_b_mod': 'grad_w', 'grad_norm_mix': 'grad_w', 'grad_norm_ffn': 'grad_w', 'grad_w_in': 'grad_w', 'grad_ret_decay': 'grad_w', 'grad_attn_sink': 'grad_w', 'grad_w_out': 'grad_w', 'grad_w_gate': 'grad_w', 'grad_w_up': 'grad_w', 'grad_w_down': 'grad_w', 'grad_norm_final': 'grad_w', 'delta_c_ctx': 'delta_w', 'delta_w_mod': 'delta_w', 'delta_b_mod': 'delta_w', 'delta_norm_mix': 'delta_w', 'delta_norm_ffn': 'delta_w', 'delta_w_in': 'delta_w', 'delta_ret_decay': 'delta_w', 'delta_attn_sink': 'delta_w', 'delta_w_out': 'delta_w', 'delta_w_gate': 'delta_w', 'delta_w_up': 'delta_w', 'delta_w_down': 'delta_w', 'delta_norm_final': 'delta_w', 'new_m_c_ctx': 'new_m', 'new_m_w_mod': 'new_m', 'new_m_b_mod': 'new_m', 'new_m_norm_mix': 'new_m', 'new_m_norm_ffn': 'new_m', 'new_m_w_in': 'new_m', 'new_m_ret_decay': 'new_m', 'new_m_attn_sink': 'new_m', 'new_m_w_out': 'new_m', 'new_m_w_gate': 'new_m', 'new_m_w_up': 'new_m', 'new_m_w_down': 'new_m', 'new_m_norm_final': 'new_m', 'new_v_c_ctx': 'new_v', 'new_v_w_mod': 'new_v', 'new_v_b_mod': 'new_v', 'new_v_norm_mix': 'new_v', 'new_v_norm_ffn': 'new_v', 'new_v_w_in': 'new_v', 'new_v_ret_decay': 'new_v', 'new_v_attn_sink': 'new_v', 'new_v_w_out': 'new_v', 'new_v_w_gate': 'new_v', 'new_v_w_up': 'new_v', 'new_v_w_down': 'new_v', 'new_v_norm_final': 'new_v'}


def _forward(args):
    return _fwd_reference(*[args[k] for k in FWD_PARAMS])


def _output_shape():
    def fwd():
        inp = _fwd_setup_inputs(0)
        return _fwd_reference(*[inp[k] for k in FWD_PARAMS])
    out = _jax.eval_shape(fwd)
    return out.shape, out.dtype

N_MICROBATCH = 1
ADAM_LR = 0.001
ADAM_B1 = 0.9
ADAM_B2 = 0.999
ADAM_EPS = 1e-08
ADAM_WD = 0.01
ADAM_STEP = 10
PER_EXAMPLE_BATCH_AXIS = {'x': 0, 'c': 0, 'ctx': 0, 'loss_target': 0}
SHARED_INPUTS = []
_WEIGHT_DTYPES = {'c_ctx': _jnp.float32, 'w_mod': _jnp.float32, 'b_mod': _jnp.float32, 'norm_mix': _jnp.float32, 'norm_ffn': _jnp.float32, 'w_in': _jnp.float32, 'ret_decay': _jnp.float32, 'attn_sink': _jnp.float32, 'w_out': _jnp.float32, 'w_gate': _jnp.float32, 'w_up': _jnp.float32, 'w_down': _jnp.float32, 'norm_final': _jnp.float32}
MOMENT_SCALE = {'c_ctx': 1.561566e-02, 'w_mod': 3.444925e-02, 'b_mod': 6.020963e-02, 'norm_mix': 3.378472e-02, 'norm_ffn': 3.640764e-02, 'w_in': 2.504088e-02, 'ret_decay': 1.339847e-01, 'attn_sink': 1.001324e-04, 'w_out': 1.852226e-02, 'w_gate': 1.614262e-02, 'w_up': 1.559517e-02, 'w_down': 2.586637e-02, 'norm_final': 3.196016e+01}


def _to_microbatches(a, axis):
    t = _jnp.moveaxis(a, axis, 0)
    t = t.reshape((N_MICROBATCH, t.shape[0] // N_MICROBATCH) + t.shape[1:])
    return _jnp.moveaxis(t, 1, axis + 1)


def setup_inputs(seed: int = 0) -> dict:
    inp = _fwd_setup_inputs(seed)
    key = _jax.random.fold_in(_jax.random.key(seed), 7919)
    shape, _ = _output_shape()
    out = dict(inp)
    out["loss_target"] = _jax.random.normal(_jax.random.fold_in(key, 0), shape, _jnp.float32)
    for i, name in enumerate(TWIN_WEIGHTS):
        w = inp[name].astype(_jnp.float32)
        if MOMENT_SCALE is None:
            s = _jnp.sqrt(_jnp.mean(_jnp.square(w)) + 1e-30)
        else:
            s = MOMENT_SCALE[name]
        km, kv = _jax.random.split(_jax.random.fold_in(key, i + 1))
        out[name] = w
        out["m_" + name] = s * _jax.random.normal(km, w.shape, _jnp.float32)
        out["v_" + name] = (s * s) * _jax.random.uniform(kv, w.shape, _jnp.float32, 0.5, 1.5)
    if N_MICROBATCH > 1:
        for name, axis in PER_EXAMPLE_BATCH_AXIS.items():
            out[name] = _to_microbatches(out[name], axis)
    return {'x': out['x'], 'c': out['c'], 'ctx': out['ctx'], 'c_ctx': out['c_ctx'], 'w_mod': out['w_mod'], 'b_mod': out['b_mod'], 'norm_mix': out['norm_mix'], 'norm_ffn': out['norm_ffn'], 'w_in': out['w_in'], 'ret_decay': out['ret_decay'], 'attn_sink': out['attn_sink'], 'w_out': out['w_out'], 'w_gate': out['w_gate'], 'w_up': out['w_up'], 'w_down': out['w_down'], 'norm_final': out['norm_final'], 'loss_target': out['loss_target'], 'm_c_ctx': out['m_c_ctx'], 'm_w_mod': out['m_w_mod'], 'm_b_mod': out['m_b_mod'], 'm_norm_mix': out['m_norm_mix'], 'm_norm_ffn': out['m_norm_ffn'], 'm_w_in': out['m_w_in'], 'm_ret_decay': out['m_ret_decay'], 'm_attn_sink': out['m_attn_sink'], 'm_w_out': out['m_w_out'], 'm_w_gate': out['m_w_gate'], 'm_w_up': out['m_w_up'], 'm_w_down': out['m_w_down'], 'm_norm_final': out['m_norm_final'], 'v_c_ctx': out['v_c_ctx'], 'v_w_mod': out['v_w_mod'], 'v_b_mod': out['v_b_mod'], 'v_norm_mix': out['v_norm_mix'], 'v_norm_ffn': out['v_norm_ffn'], 'v_w_in': out['v_w_in'], 'v_ret_decay': out['v_ret_decay'], 'v_attn_sink': out['v_attn_sink'], 'v_w_out': out['v_w_out'], 'v_w_gate': out['v_w_gate'], 'v_w_up': out['v_w_up'], 'v_w_down': out['v_w_down'], 'v_norm_final': out['v_norm_final']}


def _loss(weights, diff, rest, loss_target):
    with _jax.named_scope("forward"):
        args = {**rest, TWIN_DIFF_INPUT: diff, **{k: w.astype(_WEIGHT_DTYPES[k]) for k, w in weights.items()}}
        y = _forward(args)
    with _jax.named_scope("loss_head"):
        err = _jnp.square(y.astype(_jnp.float32) - loss_target)
        return 0.5 * _jnp.sum(_jnp.mean(err, axis=-1)) if err.ndim else 0.5 * err


def _adamw(w, g, m, v):
    m = ADAM_B1 * m + (1.0 - ADAM_B1) * g
    v = ADAM_B2 * v + (1.0 - ADAM_B2) * _jnp.square(g)
    m_hat = m / (1.0 - ADAM_B1 ** ADAM_STEP)
    v_hat = v / (1.0 - ADAM_B2 ** ADAM_STEP)
    delta = -ADAM_LR * (m_hat / (_jnp.sqrt(v_hat) + ADAM_EPS) + ADAM_WD * w)
    return delta, m, v


def reference(x, c, ctx, c_ctx, w_mod, b_mod, norm_mix, norm_ffn, w_in, ret_decay, attn_sink, w_out, w_gate, w_up, w_down, norm_final, loss_target, m_c_ctx, m_w_mod, m_b_mod, m_norm_mix, m_norm_ffn, m_w_in, m_ret_decay, m_attn_sink, m_w_out, m_w_gate, m_w_up, m_w_down, m_norm_final, v_c_ctx, v_w_mod, v_b_mod, v_norm_mix, v_norm_ffn, v_w_in, v_ret_decay, v_attn_sink, v_w_out, v_w_gate, v_w_up, v_w_down, v_norm_final):
    given = dict(x=x, c=c, ctx=ctx, c_ctx=c_ctx, w_mod=w_mod, b_mod=b_mod, norm_mix=norm_mix, norm_ffn=norm_ffn, w_in=w_in, ret_decay=ret_decay, attn_sink=attn_sink, w_out=w_out, w_gate=w_gate, w_up=w_up, w_down=w_down, norm_final=norm_final, loss_target=loss_target, m_c_ctx=m_c_ctx, m_w_mod=m_w_mod, m_b_mod=m_b_mod, m_norm_mix=m_norm_mix, m_norm_ffn=m_norm_ffn, m_w_in=m_w_in, m_ret_decay=m_ret_decay, m_attn_sink=m_attn_sink, m_w_out=m_w_out, m_w_gate=m_w_gate, m_w_up=m_w_up, m_w_down=m_w_down, m_norm_final=m_norm_final, v_c_ctx=v_c_ctx, v_w_mod=v_w_mod, v_b_mod=v_b_mod, v_norm_mix=v_norm_mix, v_norm_ffn=v_norm_ffn, v_w_in=v_w_in, v_ret_decay=v_ret_decay, v_attn_sink=v_attn_sink, v_w_out=v_w_out, v_w_gate=v_w_gate, v_w_up=v_w_up, v_w_down=v_w_down, v_norm_final=v_norm_final)
    weights = {n: given[n] for n in TWIN_WEIGHTS}
    shared = {n: given[n] for n in SHARED_INPUTS}
    per_example = {n: given[n] for n in ['x', 'c', 'ctx']}
    grad_fn = _jax.value_and_grad(_loss, argnums=(0, 1))

    def one_microbatch(ex, loss_target):
        ex = dict(ex)
        diff = ex.pop(TWIN_DIFF_INPUT)
        return grad_fn(weights, diff, {**shared, **ex}, loss_target)

    if N_MICROBATCH == 1:
        loss, (grad_w, grad_x) = one_microbatch(per_example, given["loss_target"])
    else:
        def body(carry, xs):
            loss_sum, grad_sum = carry
            l_k, (gw_k, gx_k) = one_microbatch(xs[0], xs[1])
            with _jax.named_scope("update"):
                return (loss_sum + l_k, _jax.tree.map(_jnp.add, grad_sum, gw_k)), gx_k

        init = (_jnp.zeros((), _jnp.float32), _jax.tree.map(_jnp.zeros_like, weights))
        (loss, grad_w), grad_x = _jax.lax.scan(body, init, (per_example, given["loss_target"]))
    with _jax.named_scope("update"):
        delta_w, new_m, new_v = {}, {}, {}
        for n in TWIN_WEIGHTS:
            delta_w[n], new_m[n], new_v[n] = _adamw(weights[n], grad_w[n], given["m_" + n], given["v_" + n])
    return (loss, grad_x, *[grad_w[n] for n in TWIN_WEIGHTS], *[delta_w[n] for n in TWIN_WEIGHTS],
            *[new_m[n] for n in TWIN_WEIGHTS], *[new_v[n] for n in TWIN_WEIGHTS])
```

```python
import functools
import numpy as np
import jax
import jax.numpy as jnp
from jax import lax
from jax.experimental import pallas as pl
from jax.experimental.pallas import tpu as pltpu

F32 = jnp.float32
BF16 = jnp.bfloat16

RET_HEADS = 8
RET_DK = 64
RET_DV = 128
CHUNK = 128
ATT_HEADS = 16
ATT_KV_HEADS = 4
ATT_DH = 64
GRID_W = 64
ROPE_BASE = 10000.0
NORM_EPS = 1e-6
ADAM_LR = 0.001
ADAM_B1 = 0.9
ADAM_B2 = 0.999
ADAM_EPS = 1e-08
ADAM_WD = 0.01
ADAM_STEP = 10
NEG = -1e30
LANES = 128
VMEM_LIMIT = 56 * 1024 * 1024
MESH = pl.DeviceIdType.MESH
N_CHIPS = 4
N_DEV = 8


def _nn(a, b):
    return jnp.dot(a, b, preferred_element_type=F32)


def _nt(a, b):
    return lax.dot_general(a, b, (((1,), (1,)), ((), ())), preferred_element_type=F32)


def _tn(a, b):
    return lax.dot_general(a, b, (((0,), (0,)), ((), ())), preferred_element_type=F32)


def _tile(n, pref, unit=LANES):
    t = min(n, pref)
    t -= t % unit
    while t > unit and n % t:
        t -= unit
    if t <= 0 or n % t:
        return n
    return t


def _params(ndim, vmem=True):
    return pltpu.CompilerParams(dimension_semantics=("arbitrary",) * ndim,
                                vmem_limit_bytes=VMEM_LIMIT if vmem else None)


def _sigmoid(x):
    return 1.0 / (1.0 + jnp.exp(-x))


def _fsum(x):
    return jnp.sum(jnp.sum(x, axis=1, keepdims=True), axis=0, keepdims=True)


def _rope_tables(L):
    lane = np.arange(LANES)
    d = lane % 64
    inv_r = jnp.asarray(ROPE_BASE, F32) ** (-jnp.arange(32, dtype=F32) / 32)
    t = jnp.arange(L)
    ang_r = t.astype(F32)[:, None] * jnp.tile(inv_r, LANES // 32)[None, :]
    Rr = np.zeros((LANES, LANES), np.float32)
    for l in range(LANES):
        if d[l] < 32:
            Rr[l + 32, l] = -1.0
        else:
            Rr[l - 32, l] = 1.0
    inv_a = jnp.asarray(ROPE_BASE, F32) ** (-jnp.arange(16, dtype=F32) / 16)
    rows = (t // GRID_W).astype(F32)
    cols = (t % GRID_W).astype(F32)
    dd = d % 32
    pos = jnp.where(jnp.asarray(d < 32)[None, :], rows[:, None], cols[:, None])
    ang_a = pos * jnp.tile(inv_a, LANES // 16)[None, :]
    Ra = np.zeros((LANES, LANES), np.float32)
    for l in range(LANES):
        if dd[l] < 16:
            Ra[l + 16, l] = -1.0
        else:
            Ra[l - 16, l] = 1.0
    D0 = np.zeros((LANES, LANES), np.float32)
    D1 = np.zeros((LANES, LANES), np.float32)
    for l in range(LANES):
        D0[l % 64, l] = 1.0
        D1[64 + l % 64, l] = 1.0
    return dict(
        Cr=jnp.cos(ang_r), Sr=jnp.sin(ang_r), Rr=jnp.asarray(Rr, BF16), RrT=jnp.asarray(Rr.T, BF16),
        Ca=jnp.cos(ang_a), Sa=jnp.sin(ang_a), Ra=jnp.asarray(Ra, BF16), RaT=jnp.asarray(Ra.T, BF16),
        D0=jnp.asarray(D0, BF16), D1=jnp.asarray(D1, BF16),
        D0T=jnp.asarray(D0.T, BF16), D1T=jnp.asarray(D1.T, BF16))


def _norm_mod(xf, g, sh, sc):
    r = lax.rsqrt(jnp.mean(xf * xf, axis=-1, keepdims=True) + NORM_EPS)
    return (xf * r * g) * (1.0 + sc) + sh


def _norm_mod_matmul(x, g, sh, sc, w, name):
    M, D = x.shape
    N = w.shape[1]
    tm, tn = _tile(M, 512, 8), _tile(N, 512)

    def body(x_ref, g_ref, sh_ref, sc_ref, w_ref, p_ref, h_ref, hs):
        @pl.when(pl.program_id(1) == 0)
        def _():
            hb = _norm_mod(x_ref[...], g_ref[...], sh_ref[...], sc_ref[...]).astype(BF16)
            hs[...] = hb
            h_ref[...] = hb
        p_ref[...] = _nn(hs[...], w_ref[...]).astype(BF16)

    vec = pl.BlockSpec((1, D), lambda i, j: (0, 0))
    return pl.pallas_call(
        body, name=name, grid=(M // tm, N // tn),
        in_specs=[pl.BlockSpec((tm, D), lambda i, j: (i, 0)), vec, vec, vec,
                  pl.BlockSpec((D, tn), lambda i, j: (0, j))],
        out_specs=[pl.BlockSpec((tm, tn), lambda i, j: (i, j)), pl.BlockSpec((tm, D), lambda i, j: (i, 0))],
        out_shape=[jax.ShapeDtypeStruct((M, N), BF16), jax.ShapeDtypeStruct((M, D), BF16)],
        scratch_shapes=[pltpu.VMEM((tm, D), BF16)],
        compiler_params=_params(2))(x, g, sh, sc, w)


def _proj_residual(a, w, xres, gt, name):
    M, K = a.shape
    N = w.shape[1]
    tm, tn = _tile(M, 512, 8), _tile(N, 512)

    def body(a_ref, w_ref, x_ref, gt_ref, xo_ref, o_ref):
        o = _nn(a_ref[...], w_ref[...])
        o_ref[...] = o.astype(BF16)
        xo_ref[...] = x_ref[...] + gt_ref[...] * o

    return pl.pallas_call(
        body, name=name, grid=(M // tm, N // tn),
        in_specs=[pl.BlockSpec((tm, K), lambda i, j: (i, 0)), pl.BlockSpec((K, tn), lambda i, j: (0, j)),
                  pl.BlockSpec((tm, tn), lambda i, j: (i, j)), pl.BlockSpec((1, tn), lambda i, j: (0, j))],
        out_specs=[pl.BlockSpec((tm, tn), lambda i, j: (i, j)), pl.BlockSpec((tm, tn), lambda i, j: (i, j))],
        out_shape=[jax.ShapeDtypeStruct((M, N), F32), jax.ShapeDtypeStruct((M, N), BF16)],
        compiler_params=_params(2))(a, w, xres, gt)


def _ffn_in(x1, g, sh, sc, wg, wu):
    M, D = x1.shape
    N = wg.shape[1]
    tm, tn = _tile(M, 512, 8), _tile(N, 512)

    def body(x_ref, g_ref, sh_ref, sc_ref, wg_ref, wu_ref, G_ref, U_ref, A_ref, h_ref, hs):
        @pl.when(pl.program_id(1) == 0)
        def _():
            hb = _norm_mod(x_ref[...], g_ref[...], sh_ref[...], sc_ref[...]).astype(BF16)
            hs[...] = hb
            h_ref[...] = hb
        G = _nn(hs[...], wg_ref[...])
        U = _nn(hs[...], wu_ref[...])
        G_ref[...] = G.astype(BF16)
        U_ref[...] = U.astype(BF16)
        A_ref[...] = (G * _sigmoid(G) * U).astype(BF16)

    vec = pl.BlockSpec((1, D), lambda i, j: (0, 0))
    wspec = pl.BlockSpec((D, tn), lambda i, j: (0, j))
    ospec = pl.BlockSpec((tm, tn), lambda i, j: (i, j))
    big = jax.ShapeDtypeStruct((M, N), BF16)
    return pl.pallas_call(
        body, name="ffn_in", grid=(M // tm, N // tn),
        in_specs=[pl.BlockSpec((tm, D), lambda i, j: (i, 0)), vec, vec, vec, wspec, wspec],
        out_specs=[ospec, ospec, ospec, pl.BlockSpec((tm, D), lambda i, j: (i, 0))],
        out_shape=[big, big, big, jax.ShapeDtypeStruct((M, D), BF16)],
        scratch_shapes=[pltpu.VMEM((tm, D), BF16)],
        compiler_params=_params(2))(x1, g, sh, sc, wg, wu)


def _final(x2, gn, tgt):
    M, D = x2.shape
    tm = _tile(M, 256, 8)

    def body(x_ref, g_ref, t_ref, dx_ref, loss_ref, dg_ref):
        @pl.when(pl.program_id(0) == 0)
        def _():
            loss_ref[...] = jnp.zeros_like(loss_ref)
            dg_ref[...] = jnp.zeros_like(dg_ref)
        x = x_ref[...]
        g = g_ref[...]
        r = lax.rsqrt(jnp.mean(x * x, axis=-1, keepdims=True) + NORM_EPS)
        xh = x * r
        e = xh * g - t_ref[...]
        loss_ref[...] += (0.5 / D) * _fsum(e * e)
        dy = e * (1.0 / D)
        dg_ref[...] += jnp.sum(dy * xh, axis=0, keepdims=True)
        dxh = dy * g
        dx_ref[...] = r * (dxh - xh * jnp.mean(dxh * xh, axis=-1, keepdims=True))

    row = pl.BlockSpec((tm, D), lambda i: (i, 0))
    return pl.pallas_call(
        body, name="final_loss", grid=(M // tm,),
        in_specs=[row, pl.BlockSpec((1, D), lambda i: (0, 0)), row],
        out_specs=[row, pl.BlockSpec((1, LANES), lambda i: (0, 0)), pl.BlockSpec((1, D), lambda i: (0, 0))],
        out_shape=[jax.ShapeDtypeStruct((M, D), F32), jax.ShapeDtypeStruct((1, LANES), F32),
                   jax.ShapeDtypeStruct((1, D), F32)],
        compiler_params=_params(1))(x2, gn, tgt)


def _rope_cols(src, blk0, nblk, Ct, St, R, scale, rope, name):
    M = src.shape[0]
    tm = _tile(M, 512, 8)

    def body(x_ref, c_ref, s_ref, r_ref, o_ref):
        x = x_ref[...]
        xf = x.astype(F32)
        if rope:
            xf = xf * c_ref[...] + _nn(x.astype(BF16), r_ref[...]) * s_ref[...]
        o_ref[...] = (xf * scale).astype(BF16)

    tab = pl.BlockSpec((tm, LANES), lambda i, j: (i, 0))
    return pl.pallas_call(
        body, name=name, grid=(M // tm, nblk),
        in_specs=[pl.BlockSpec((tm, LANES), lambda i, j: (i, blk0 + j)), tab, tab,
                  pl.BlockSpec((LANES, LANES), lambda i, j: (0, 0))],
        out_specs=pl.BlockSpec((tm, LANES), lambda i, j: (i, j)),
        out_shape=jax.ShapeDtypeStruct((M, nblk * LANES), BF16),
        compiler_params=_params(2, False))(src, Ct, St, R)


def _dup_heads(src, blk0, npair, Ct, St, R, D0, D1, rope, name):
    M = src.shape[0]
    tm = _tile(M, 512, 8)

    def body(x_ref, c_ref, s_ref, r_ref, d0_ref, d1_ref, o_ref):
        x = x_ref[...]
        if rope:
            x = (x.astype(F32) * c_ref[...] + _nn(x, r_ref[...]) * s_ref[...]).astype(BF16)
        o_ref[0] = _nn(x, d0_ref[...]).astype(BF16)
        o_ref[1] = _nn(x, d1_ref[...]).astype(BF16)

    tab = pl.BlockSpec((tm, LANES), lambda i, p: (i, 0))
    mat = pl.BlockSpec((LANES, LANES), lambda i, p: (0, 0))
    return pl.pallas_call(
        body, name=name, grid=(M // tm, npair),
        in_specs=[pl.BlockSpec((tm, LANES), lambda i, p: (i, blk0 + p)), tab, tab, mat, mat, mat],
        out_specs=pl.BlockSpec((2, tm, LANES), lambda i, p: (p, i, 0)),
        out_shape=jax.ShapeDtypeStruct((2 * npair, M, LANES), BF16),
        compiler_params=_params(2, False))(src, Ct, St, R, D0, D1)


def _unrope_cols(dsrc, dst, blk0, nblk, Ct, St, RT, scale, rope, name):
    M = dsrc.shape[0]
    tm = _tile(M, 512, 8)

    def body(x_ref, c_ref, s_ref, r_ref, dst_ref, o_ref):
        del dst_ref
        xf = x_ref[...].astype(F32)
        if rope:
            xf = xf * c_ref[...] + _nn((xf * s_ref[...]).astype(BF16), r_ref[...])
        o_ref[...] = (xf * scale).astype(BF16)

    tab = pl.BlockSpec((tm, LANES), lambda i, j: (i, 0))
    return pl.pallas_call(
        body, name=name, grid=(M // tm, nblk),
        in_specs=[pl.BlockSpec((tm, LANES), lambda i, j: (i, j)), tab, tab,
                  pl.BlockSpec((LANES, LANES), lambda i, j: (0, 0)),
                  pl.BlockSpec(memory_space=pl.ANY)],
        out_specs=pl.BlockSpec((tm, LANES), lambda i, j: (i, blk0 + j)),
        out_shape=jax.ShapeDtypeStruct(dst.shape, dst.dtype),
        input_output_aliases={4: 0},
        compiler_params=_params(2, False))(dsrc, Ct, St, RT, dst)


def _fold_heads(parts, dst, blk0, npair, Ct, St, RT, D0T, D1T, rope, name):
    M = parts[0][0].shape[1]
    nb = M // CHUNK
    nparts = len(parts)

    def body(*refs):
        part_refs = refs[:nparts]
        c_ref, s_ref, r_ref, d0_ref, d1_ref, dst_ref, o_ref = refs[nparts:]
        del dst_ref
        m = pl.program_id(0)
        tot = [jnp.zeros((CHUNK, LANES), F32), jnp.zeros((CHUNK, LANES), F32)]
        for pr, (_, shift) in zip(part_refs, parts):
            ok = jnp.logical_and(m + shift >= 0, m + shift <= nb - 1)
            for e in range(2):
                tot[e] = tot[e] + jnp.where(ok, pr[e].astype(F32), 0.0)
        f = _nn(tot[0].astype(BF16), d0_ref[...]) + _nn(tot[1].astype(BF16), d1_ref[...])
        if rope:
            f = f * c_ref[...] + _nn((f * s_ref[...]).astype(BF16), r_ref[...])
        o_ref[...] = f.astype(BF16)

    def part_spec(shift):
        return pl.BlockSpec((2, CHUNK, LANES), lambda m, p: (p, jnp.clip(m + shift, 0, nb - 1), 0))

    tab = pl.BlockSpec((CHUNK, LANES), lambda m, p: (m, 0))
    mat = pl.BlockSpec((LANES, LANES), lambda m, p: (0, 0))
    return pl.pallas_call(
        body, name=name, grid=(nb, npair),
        in_specs=[part_spec(s) for _, s in parts] + [tab, tab, mat, mat, mat, pl.BlockSpec(memory_space=pl.ANY)],
        out_specs=pl.BlockSpec((CHUNK, LANES), lambda m, p: (m, blk0 + p)),
        out_shape=jax.ShapeDtypeStruct(dst.shape, dst.dtype),
        input_output_aliases={nparts + 5: 0},
        compiler_params=_params(2, False))(*[a for a, _ in parts], Ct, St, RT, D0T, D1T, dst)


def _head_masks():
    lane = lax.broadcasted_iota(jnp.int32, (1, LANES), 1)
    return [lane < 64, lane >= 64]


def _decay_vecs(lam, mu):
    i = lax.broadcasted_iota(jnp.int32, (CHUNK, 1), 0).astype(F32)
    return dict(qf=jnp.exp(lam * (i + 1.0)), kf=jnp.exp(lam * (CHUNK - 1.0 - i)),
                qb=jnp.exp(mu * (CHUNK - i)), kb=jnp.exp(mu * i),
                gf=jnp.exp(lam * float(CHUNK)), gb=jnp.exp(mu * float(CHUNK)), i=i)


def _decay_mask(lam, mu):
    r = lax.broadcasted_iota(jnp.int32, (CHUNK, CHUNK), 0)
    c = lax.broadcasted_iota(jnp.int32, (CHUNK, CHUNK), 1)
    rel = (r - c).astype(F32)
    low = rel >= 0.0
    mf = jnp.exp(lam * jnp.maximum(rel, 0.0))
    mb = jnp.exp(mu * jnp.maximum(-rel, 0.0))
    return jnp.where(low, mf, mb), rel, low


def _lam_of(lg_ref, row, idx):
    return jnp.full((1, 1), lg_ref[row, idx], F32)


def _ret_states_fwd(Kr, P, Krc, Pc, lg, rv_blk, npairs):
    L = Kr.shape[0]
    Lc = Krc.shape[0]
    N, ncc = L // CHUNK, Lc // CHUNK

    def body(lg_ref, k_ref, v_ref, kc_ref, vc_ref, sf_ref, S):
        p, n = pl.program_id(0), pl.program_id(1)
        masks = _head_masks()
        for h in range(2):
            lam = _lam_of(lg_ref, 0, 2 * p + h)
            dv = _decay_vecs(lam, lam)
            hm = masks[h]

            @pl.when(n == 0)
            def _():
                s = jnp.zeros((LANES, LANES), F32)
                for cc in range(ncc):
                    kw = jnp.where(hm, kc_ref[cc * CHUNK:(cc + 1) * CHUNK, :].astype(F32) * dv["kf"], 0.0).astype(BF16)
                    s = dv["gf"] * s + _tn(kw, vc_ref[cc * CHUNK:(cc + 1) * CHUNK, h * LANES:(h + 1) * LANES])
                S[h] = s

            s = S[h]
            sf_ref[0, 0, h] = s.astype(BF16)
            kw = jnp.where(hm, k_ref[...].astype(F32) * dv["kf"], 0.0).astype(BF16)
            S[h] = dv["gf"] * s + _tn(kw, v_ref[:, h * LANES:(h + 1) * LANES])

    return pl.pallas_call(
        body, name="ret_states_fwd", grid=(npairs, N),
        in_specs=[pl.BlockSpec(memory_space=pltpu.SMEM),
                  pl.BlockSpec((CHUNK, LANES), lambda p, n: (n, p)),
                  pl.BlockSpec((CHUNK, 2 * LANES), lambda p, n: (n, rv_blk + p)),
                  pl.BlockSpec((Lc, LANES), lambda p, n: (0, p)),
                  pl.BlockSpec((Lc, 2 * LANES), lambda p, n: (0, rv_blk + p))],
        out_specs=pl.BlockSpec((1, 1, 2, LANES, LANES), lambda p, n: (p, n, 0, 0, 0)),
        out_shape=jax.ShapeDtypeStruct((npairs, N, 2, LANES, LANES), BF16),
        scratch_shapes=[pltpu.VMEM((2, LANES, LANES), F32)],
        compiler_params=_params(2, False))(lg, Kr, P, Krc, Pc)


def _ret_chunk_fwd(q, k, v, sf, sb, hm, lam, mu):
    dv = _decay_vecs(lam, mu)
    Mk, rel, low = _decay_mask(lam, mu)
    qm = jnp.where(hm, q, jnp.zeros_like(q))
    qmf = qm.astype(F32)
    A = _nt(qm, k)
    Am = A * Mk
    Amb = Am.astype(BF16)
    Qf = (qmf * dv["qf"]).astype(BF16)
    Qb = (qmf * dv["qb"]).astype(BF16)
    O = _nn(Amb, v) + _nn(Qf, sf) + _nn(Qb, sb)
    return dict(dv=dv, Mk=Mk, rel=rel, low=low, qm=qm, Am=Am, Amb=Amb, Qf=Qf, Qb=Qb, O=O)


def _ret_out_fwd(Qr, Kr, P, Krc, Pc, SF, lg, rv_blk, rg_blk, npairs, d_mix):
    L = Qr.shape[0]
    Lc = Krc.shape[0]
    N, ncc = L // CHUNK, Lc // CHUNK

    def body(lg_ref, q_ref, k_ref, v_ref, g_ref, sf_ref, kc_ref, vc_ref, y_ref, sb_ref, S):
        p, n = pl.program_id(0), pl.program_id(1)
        masks = _head_masks()
        for h in range(2):
            lam = _lam_of(lg_ref, 0, 2 * p + h)
            mu = _lam_of(lg_ref, 1, 2 * p + h)
            hm = masks[h]
            dvb = _decay_vecs(lam, mu)

            @pl.when(n == 0)
            def _():
                s = jnp.zeros((LANES, LANES), F32)
                for cc in reversed(range(ncc)):
                    kw = jnp.where(hm, kc_ref[cc * CHUNK:(cc + 1) * CHUNK, :].astype(F32) * dvb["kb"], 0.0).astype(BF16)
                    s = dvb["gb"] * s + _tn(kw, vc_ref[cc * CHUNK:(cc + 1) * CHUNK, h * LANES:(h + 1) * LANES])
                S[h] = s

            s = S[h]
            sbb = s.astype(BF16)
            sb_ref[0, 0, h] = sbb
            v = v_ref[:, h * LANES:(h + 1) * LANES]
            f = _ret_chunk_fwd(q_ref[...], k_ref[...], v, sf_ref[0, 0, h], sbb, hm, lam, mu)
            O = f["O"]
            r = lax.rsqrt(jnp.mean(O * O, axis=-1, keepdims=True) + NORM_EPS)
            g = g_ref[:, h * LANES:(h + 1) * LANES].astype(F32)
            y_ref[:, h * LANES:(h + 1) * LANES] = (O * r * (g * _sigmoid(g))).astype(BF16)
            kw = jnp.where(hm, k_ref[...].astype(F32) * dvb["kb"], 0.0).astype(BF16)
            S[h] = dvb["gb"] * s + _tn(kw, v)

    rev = lambda n: N - 1 - n
    return pl.pallas_call(
        body, name="ret_out_fwd", grid=(npairs, N),
        in_specs=[pl.BlockSpec(memory_space=pltpu.SMEM),
                  pl.BlockSpec((CHUNK, LANES), lambda p, n: (rev(n), p)),
                  pl.BlockSpec((CHUNK, LANES), lambda p, n: (rev(n), p)),
                  pl.BlockSpec((CHUNK, 2 * LANES), lambda p, n: (rev(n), rv_blk + p)),
                  pl.BlockSpec((CHUNK, 2 * LANES), lambda p, n: (rev(n), rg_blk + p)),
                  pl.BlockSpec((1, 1, 2, LANES, LANES), lambda p, n: (p, rev(n), 0, 0, 0)),
                  pl.BlockSpec((Lc, LANES), lambda p, n: (0, p)),
                  pl.BlockSpec((Lc, 2 * LANES), lambda p, n: (0, rv_blk + p))],
        out_specs=[pl.BlockSpec((CHUNK, 2 * LANES), lambda p, n: (rev(n), p)),
                   pl.BlockSpec((1, 1, 2, LANES, LANES), lambda p, n: (p, rev(n), 0, 0, 0))],
        out_shape=[jax.ShapeDtypeStruct((L, d_mix), BF16),
                   jax.ShapeDtypeStruct((npairs, N, 2, LANES, LANES), BF16)],
        scratch_shapes=[pltpu.VMEM((2, LANES, LANES), F32)],
        compiler_params=_params(2, False))(lg, Qr, Kr, P, P, SF, Krc, Pc)


ACC_ROWS = 8


def _ret_bwd1(Qr, Kr, P, Krc, Pc, SF, SB, dY, lg, rv_blk, rg_blk, npairs, d_proj):
    L = Qr.shape[0]
    Lc = Krc.shape[0]
    N, ncc = L // CHUNK, Lc // CHUNK

    def body(lg_ref, q_ref, k_ref, v_ref, g_ref, sf_ref, sb_ref, dy_ref, kc_ref, vc_ref,
             dq_ref, dk_ref, dv_ref, dg_ref, do_ref, dkc_ref, dvc_ref, acc_ref, dS, T):
        p, n = pl.program_id(0), pl.program_id(1)
        masks = _head_masks()

        @pl.when(n == 0)
        def _():
            dS[...] = jnp.zeros_like(dS)
            T[...] = jnp.zeros_like(T)
            acc_ref[...] = jnp.zeros_like(acc_ref)

        dq_tot = jnp.zeros((CHUNK, LANES), F32)
        dk_tot = jnp.zeros((CHUNK, LANES), F32)
        for h in range(2):
            lam = _lam_of(lg_ref, 0, 2 * p + h)
            mu = _lam_of(lg_ref, 1, 2 * p + h)
            hm = masks[h]
            hs = slice(h * LANES, (h + 1) * LANES)
            v = v_ref[:, hs]
            k = k_ref[...]
            sf = sf_ref[0, 0, h]
            sb = sb_ref[0, 0, h]
            f = _ret_chunk_fwd(q_ref[...], k, v, sf, sb, hm, lam, mu)
            dv_, O = f["dv"], f["O"]
            r = lax.rsqrt(jnp.mean(O * O, axis=-1, keepdims=True) + NORM_EPS)
            on = O * r
            g = g_ref[:, hs].astype(F32)
            sg = _sigmoid(g)
            dy = dy_ref[:, hs].astype(F32)
            dg_ref[:, hs] = (dy * on * (sg * (1.0 + g * (1.0 - sg)))).astype(BF16)
            don = dy * (g * sg)
            dO = r * (don - on * jnp.mean(don * on, axis=-1, keepdims=True))
            dOb = dO.astype(BF16)
            do_ref[:, hs] = dOb
            dAm = _nt(dOb, v)
            T[h] += dAm * f["Am"]
            dAb = (dAm * f["Mk"]).astype(BF16)
            km = jnp.where(hm, k, jnp.zeros_like(k))
            dq = _nn(dAb, km)
            dk = _tn(dAb, f["qm"])
            dvh = _tn(f["Amb"], dOb)
            dQf = _nt(dOb, sf)
            dQb = _nt(dOb, sb)
            dq = dq + dQf * dv_["qf"] + dQb * dv_["qb"]
            acc_ref[0, h, 0:1, :] += _fsum(dQf * f["Qf"].astype(F32) * (dv_["i"] + 1.0))
            acc_ref[0, h, 1:2, :] += _fsum(dQb * f["Qb"].astype(F32) * (CHUNK - dv_["i"]))
            dSh = dS[h]
            dSb_ = dSh.astype(BF16)
            Kf = (km.astype(F32) * dv_["kf"]).astype(BF16)
            dKf = _nt(v, dSb_)
            dk = dk + jnp.where(hm, dKf * dv_["kf"], 0.0)
            acc_ref[0, h, 2:3, :] += _fsum(jnp.where(hm, dKf, 0.0) * Kf.astype(F32) * (CHUNK - 1.0 - dv_["i"]))
            dvh = dvh + _nn(Kf, dSb_)
            acc_ref[0, h, 3:4, :] += float(CHUNK) * dv_["gf"] * _fsum(dSh * sf.astype(F32))
            dSh = dv_["gf"] * dSh + _tn(f["Qf"], dOb)
            dS[h] = dSh
            dv_ref[:, hs] = dvh
            dq_tot = dq_tot + dq
            dk_tot = dk_tot + dk

        dq_ref[...] = dq_tot
        dk_ref[...] = dk_tot

        @pl.when(n == N - 1)
        def _():
            for h in range(2):
                lam = _lam_of(lg_ref, 0, 2 * p + h)
                dv_ = _decay_vecs(lam, lam)
                hm = masks[h]
                hs = slice(h * LANES, (h + 1) * LANES)
                states = [jnp.zeros((LANES, LANES), F32)]
                kws = []
                for cc in range(ncc):
                    kw = jnp.where(hm, kc_ref[cc * CHUNK:(cc + 1) * CHUNK, :].astype(F32) * dv_["kf"], 0.0).astype(BF16)
                    kws.append(kw)
                    states.append(dv_["gf"] * states[-1] + _tn(kw, vc_ref[cc * CHUNK:(cc + 1) * CHUNK, hs]))
                d = dS[h]
                for cc in reversed(range(ncc)):
                    db = d.astype(BF16)
                    rows = slice(cc * CHUNK, (cc + 1) * CHUNK)
                    dKf_c = jnp.where(hm, _nt(vc_ref[rows, hs], db), 0.0)
                    part = dKf_c * dv_["kf"]
                    if h == 0:
                        dkc_ref[rows, :] = part
                    else:
                        dkc_ref[rows, :] += part
                    acc_ref[0, h, 2:3, :] += _fsum(dKf_c * kws[cc].astype(F32) * (CHUNK - 1.0 - dv_["i"]))
                    dvc_ref[rows, hs] = _nn(kws[cc], db)
                    acc_ref[0, h, 3:4, :] += float(CHUNK) * dv_["gf"] * _fsum(d * states[cc])
                    d = dv_["gf"] * d
                _, rel, low = _decay_mask(lam, lam)
                Th = T[h]
                acc_ref[0, h, 4:5, :] += _fsum(jnp.where(low, Th * rel, 0.0))
                acc_ref[0, h, 5:6, :] += _fsum(jnp.where(low, 0.0, -Th * rel))

    rev = lambda n: N - 1 - n
    st = pl.BlockSpec((1, 1, 2, LANES, LANES), lambda p, n: (p, rev(n), 0, 0, 0))
    pair = pl.BlockSpec((CHUNK, LANES), lambda p, n: (rev(n), p))
    wide = lambda b0: pl.BlockSpec((CHUNK, 2 * LANES), lambda p, n: (rev(n), b0 + p))
    return pl.pallas_call(
        body, name="ret_bwd_desc", grid=(npairs, N),
        in_specs=[pl.BlockSpec(memory_space=pltpu.SMEM), pair, pair, wide(rv_blk), wide(rg_blk), st, st, wide(0),
                  pl.BlockSpec((Lc, LANES), lambda p, n: (0, p)),
                  pl.BlockSpec((Lc, 2 * LANES), lambda p, n: (0, rv_blk + p))],
        out_specs=[pair, pair, wide(0), wide(rg_blk), wide(0),
                   pl.BlockSpec((Lc, LANES), lambda p, n: (0, p)),
                   pl.BlockSpec((Lc, 2 * LANES), lambda p, n: (0, p)),
                   pl.BlockSpec((1, 2, ACC_ROWS, LANES), lambda p, n: (p, 0, 0, 0))],
        out_shape=[jax.ShapeDtypeStruct((L, npairs * LANES), F32),
                   jax.ShapeDtypeStruct((L, npairs * LANES), F32),
                   jax.ShapeDtypeStruct((L, npairs * 2 * LANES), F32),
                   jax.ShapeDtypeStruct((L, d_proj), BF16),
                   jax.ShapeDtypeStruct((L, npairs * 2 * LANES), BF16),
                   jax.ShapeDtypeStruct((Lc, npairs * LANES), F32),
                   jax.ShapeDtypeStruct((Lc, npairs * 2 * LANES), F32),
                   jax.ShapeDtypeStruct((npairs, 2, ACC_ROWS, LANES), F32)],
        scratch_shapes=[pltpu.VMEM((2, LANES, LANES), F32), pltpu.VMEM((2, CHUNK, CHUNK), F32)],
        compiler_params=_params(2, False))(lg, Qr, Kr, P, P, SF, SB, dY, Krc, Pc)


def _ret_bwd2(Qr, Kr, P, Krc, Pc, SB, dO, dKr, dVp, dP, dKc, dVc, lg, rv_blk, npairs):
    L = Qr.shape[0]
    Lc = Krc.shape[0]
    N, ncc = L // CHUNK, Lc // CHUNK

    def body(lg_ref, q_ref, k_ref, v_ref, sb_ref, do_ref, dkin_ref, dvin_ref, kc_ref, vc_ref, dkcin_ref, dvcin_ref,
             dpin_ref, dk_ref, dv_ref, dkc_ref, dvc_ref, acc_ref, dS):
        del dpin_ref
        p, n = pl.program_id(0), pl.program_id(1)
        masks = _head_masks()

        @pl.when(n == 0)
        def _():
            dS[...] = jnp.zeros_like(dS)
            acc_ref[...] = jnp.zeros_like(acc_ref)

        dk_tot = dkin_ref[...]
        for h in range(2):
            mu = _lam_of(lg_ref, 1, 2 * p + h)
            hm = masks[h]
            hs = slice(h * LANES, (h + 1) * LANES)
            dv_ = _decay_vecs(mu, mu)
            v = v_ref[:, hs]
            k = k_ref[...]
            q = q_ref[...]
            dOb = do_ref[:, hs]
            km = jnp.where(hm, k, jnp.zeros_like(k)).astype(F32)
            Kb = (km * dv_["kb"]).astype(BF16)
            Qb = (jnp.where(hm, q, jnp.zeros_like(q)).astype(F32) * dv_["qb"]).astype(BF16)
            dSh = dS[h]
            dSb_ = dSh.astype(BF16)
            dKb = jnp.where(hm, _nt(v, dSb_), 0.0)
            dk_tot = dk_tot + dKb * dv_["kb"]
            acc_ref[0, h, 0:1, :] += _fsum(dKb * Kb.astype(F32) * dv_["i"])
            dv_ref[:, hs] = (dvin_ref[:, hs] + _nn(Kb, dSb_)).astype(BF16)
            acc_ref[0, h, 1:2, :] += float(CHUNK) * dv_["gb"] * _fsum(dSh * sb_ref[0, 0, h].astype(F32))
            dSh = dv_["gb"] * dSh + _tn(Qb, dOb)
            dS[h] = dSh

        dk_ref[...] = dk_tot

        @pl.when(n == N - 1)
        def _():
            for h in range(2):
                mu = _lam_of(lg_ref, 1, 2 * p + h)
                hm = masks[h]
                hs = slice(h * LANES, (h + 1) * LANES)
                dv_ = _decay_vecs(mu, mu)
                states = {}
                kws = {}
                s = jnp.zeros((LANES, LANES), F32)
                for cc in reversed(range(ncc)):
                    states[cc] = s
                    kw = jnp.where(hm, kc_ref[cc * CHUNK:(cc + 1) * CHUNK, :].astype(F32) * dv_["kb"], 0.0).astype(BF16)
                    kws[cc] = kw
                    s = dv_["gb"] * s + _tn(kw, vc_ref[cc * CHUNK:(cc + 1) * CHUNK, hs])
                d = dS[h]
                for cc in range(ncc):
                    db = d.astype(BF16)
                    rows = slice(cc * CHUNK, (cc + 1) * CHUNK)
                    dKb_c = jnp.where(hm, _nt(vc_ref[rows, hs], db), 0.0)
                    part = dKb_c * dv_["kb"]
                    if h == 0:
                        dkc_ref[rows, :] = dkcin_ref[rows, :] + part
                    else:
                        dkc_ref[rows, :] += part
                    acc_ref[0, h, 0:1, :] += _fsum(dKb_c * kws[cc].astype(F32) * dv_["i"])
                    dvc_ref[rows, hs] = dvcin_ref[rows, hs] + _nn(kws[cc], db)
                    acc_ref[0, h, 1:2, :] += float(CHUNK) * dv_["gb"] * _fsum(d * states[cc])
                    d = dv_["gb"] * d

    st = pl.BlockSpec((1, 1, 2, LANES, LANES), lambda p, n: (p, n, 0, 0, 0))
    pair = pl.BlockSpec((CHUNK, LANES), lambda p, n: (n, p))
    wide = lambda b0: pl.BlockSpec((CHUNK, 2 * LANES), lambda p, n: (n, b0 + p))
    ckc = pl.BlockSpec((Lc, LANES), lambda p, n: (0, p))
    cvc = lambda b0: pl.BlockSpec((Lc, 2 * LANES), lambda p, n: (0, b0 + p))
    return pl.pallas_call(
        body, name="ret_bwd_asc", grid=(npairs, N),
        in_specs=[pl.BlockSpec(memory_space=pltpu.SMEM), pair, pair, wide(rv_blk), st, wide(0), pair, wide(0),
                  ckc, cvc(rv_blk), ckc, cvc(0), pl.BlockSpec(memory_space=pl.ANY)],
        out_specs=[pair, wide(rv_blk), ckc, cvc(0),
                   pl.BlockSpec((1, 2, ACC_ROWS, LANES), lambda p, n: (p, 0, 0, 0))],
        out_shape=[jax.ShapeDtypeStruct(dKr.shape, F32),
                   jax.ShapeDtypeStruct(dP.shape, dP.dtype),
                   jax.ShapeDtypeStruct(dKc.shape, F32),
                   jax.ShapeDtypeStruct(dVc.shape, F32),
                   jax.ShapeDtypeStruct((npairs, 2, ACC_ROWS, LANES), F32)],
        input_output_aliases={12: 1},
        scratch_shapes=[pltpu.VMEM((2, LANES, LANES), F32)],
        compiler_params=_params(2, False))(lg, Qr, Kr, P, SB, dO, dKr, dVp, Krc, Pc, dKc, dVc, dP)


def _att_valid(n, N):
    row = lax.broadcasted_iota(jnp.int32, (CHUNK, 3 * CHUNK), 0)
    col = lax.broadcasted_iota(jnp.int32, (CHUNK, 3 * CHUNK), 1)
    ok = jnp.logical_and(col >= row, col <= row + 2 * CHUNK)
    ok = jnp.logical_and(ok, jnp.logical_or(col >= CHUNK, n > 0))
    ok = jnp.logical_and(ok, jnp.logical_or(col < 2 * CHUNK, n < N - 1))
    return ok


def _att_probs(qa, K, Kc, valid, snk):
    s_loc = jnp.where(valid, _nt(qa, K), NEG)
    s_ctx = _nt(qa, Kc)
    mx = jnp.maximum(jnp.maximum(jnp.max(s_loc, axis=1, keepdims=True), jnp.max(s_ctx, axis=1, keepdims=True)), snk)
    p_loc = jnp.exp(s_loc - mx)
    p_ctx = jnp.exp(s_ctx - mx)
    p_snk = jnp.exp(snk - mx)
    inv = 1.0 / (jnp.sum(p_loc, axis=1, keepdims=True) + jnp.sum(p_ctx, axis=1, keepdims=True) + p_snk)
    return p_loc, p_ctx, p_snk, inv


def _att_specs(Lc, N):
    q = pl.BlockSpec((CHUNK, 2 * LANES), lambda g, n: (n, g))
    kv = lambda s: pl.BlockSpec((1, CHUNK, LANES), lambda g, n: (g, jnp.clip(n + s, 0, N - 1), 0))
    ctx = pl.BlockSpec((1, Lc, LANES), lambda g, n: (g, 0, 0))
    return q, kv, ctx


def _att_fwd(Qa, Kd, Vd, Kdc, Vdc, sink, Y, blk0):
    L = Qa.shape[0]
    Lc = Kdc.shape[1]
    N = L // CHUNK
    nkv = Kd.shape[0]

    def body(sink_ref, q_ref, kp, kc_, kn, vp, vc_, vn, kctx, vctx, y_in, o_ref):
        del y_in
        g, n = pl.program_id(0), pl.program_id(1)
        K = jnp.concatenate([kp[0], kc_[0], kn[0]], axis=0)
        V = jnp.concatenate([vp[0], vc_[0], vn[0]], axis=0)
        Kc, Vc = kctx[0], vctx[0]
        valid = _att_valid(n, N)
        masks = _head_masks()
        for pr in range(2):
            qp = q_ref[:, pr * LANES:(pr + 1) * LANES]
            acc = jnp.zeros((CHUNK, LANES), F32)
            for a in range(2):
                hm = masks[a]
                qa = jnp.where(hm, qp, jnp.zeros_like(qp))
                snk = jnp.full((1, 1), sink_ref[0, g * 4 + pr * 2 + a], F32)
                p_loc, p_ctx, _, inv = _att_probs(qa, K, Kc, valid, snk)
                Va = jnp.where(hm, V, jnp.zeros_like(V))
                Vca = jnp.where(hm, Vc, jnp.zeros_like(Vc))
                acc = acc + (_nn(p_loc.astype(BF16), Va) + _nn(p_ctx.astype(BF16), Vca)) * inv
            o_ref[:, pr * LANES:(pr + 1) * LANES] = acc.astype(BF16)

    q, kv, ctx = _att_specs(Lc, N)
    return pl.pallas_call(
        body, name="att_fwd", grid=(nkv, N),
        in_specs=[pl.BlockSpec(memory_space=pltpu.SMEM), q, kv(-1), kv(0), kv(1), kv(-1), kv(0), kv(1), ctx, ctx,
                  pl.BlockSpec(memory_space=pl.ANY)],
        out_specs=pl.BlockSpec((CHUNK, 2 * LANES), lambda g, n: (n, blk0 + g)),
        out_shape=jax.ShapeDtypeStruct(Y.shape, Y.dtype),
        input_output_aliases={10: 0},
        compiler_params=_params(2, False))(sink, Qa, Kd, Kd, Kd, Vd, Vd, Vd, Kdc, Vdc, Y)


def _att_bwd(Qa, Kd, Vd, Kdc, Vdc, sink, dY, blk0):
    L = Qa.shape[0]
    Lc = Kdc.shape[1]
    N = L // CHUNK
    nkv = Kd.shape[0]

    def body(sink_ref, q_ref, kp, kc_, kn, vp, vc_, vn, kctx, vctx, dy_ref,
             dq_ref, dkp, dkc_, dkn, dvp, dvc_, dvn, dkctx, dvctx, dsink_ref):
        g, n = pl.program_id(0), pl.program_id(1)

        @pl.when(n == 0)
        def _():
            dkctx[...] = jnp.zeros_like(dkctx)
            dvctx[...] = jnp.zeros_like(dvctx)
            dsink_ref[...] = jnp.zeros_like(dsink_ref)

        K = jnp.concatenate([kp[0], kc_[0], kn[0]], axis=0)
        V = jnp.concatenate([vp[0], vc_[0], vn[0]], axis=0)
        Kc, Vc = kctx[0], vctx[0]
        valid = _att_valid(n, N)
        masks = _head_masks()
        dK = jnp.zeros((3 * CHUNK, LANES), F32)
        dV = jnp.zeros((3 * CHUNK, LANES), F32)
        dKc = jnp.zeros((Lc, LANES), F32)
        dVc = jnp.zeros((Lc, LANES), F32)
        for pr in range(2):
            qp = q_ref[:, pr * LANES:(pr + 1) * LANES]
            dyp = dy_ref[:, pr * LANES:(pr + 1) * LANES]
            dq = jnp.zeros((CHUNK, LANES), F32)
            for a in range(2):
                hm = masks[a]
                qa = jnp.where(hm, qp, jnp.zeros_like(qp))
                do = jnp.where(hm, dyp, jnp.zeros_like(dyp))
                snk = jnp.full((1, 1), sink_ref[0, g * 4 + pr * 2 + a], F32)
                p_loc, p_ctx, p_snk, inv = _att_probs(qa, K, Kc, valid, snk)
                P_loc, P_ctx = p_loc * inv, p_ctx * inv
                dp_loc = _nt(do, V)
                dp_ctx = _nt(do, Vc)
                delta = jnp.sum(P_loc * dp_loc, axis=1, keepdims=True) + jnp.sum(P_ctx * dp_ctx, axis=1, keepdims=True)
                ds_loc = (P_loc * (dp_loc - delta)).astype(BF16)
                ds_ctx = (P_ctx * (dp_ctx - delta)).astype(BF16)
                row = pr * 2 + a
                dsink_ref[0, row:row + 1, :] += _fsum(-(p_snk * inv) * delta)
                dq = dq + jnp.where(hm, _nn(ds_loc, K) + _nn(ds_ctx, Kc), 0.0)
                dK = dK + _tn(ds_loc, qa)
                dKc = dKc + _tn(ds_ctx, qa)
                dV = dV + _tn(P_loc.astype(BF16), do)
                dVc = dVc + _tn(P_ctx.astype(BF16), do)
            dq_ref[:, pr * LANES:(pr + 1) * LANES] = dq
        for j, (rk, rv) in enumerate([(dkp, dvp), (dkc_, dvc_), (dkn, dvn)]):
            rk[0] = dK[j * CHUNK:(j + 1) * CHUNK].astype(BF16)
            rv[0] = dV[j * CHUNK:(j + 1) * CHUNK].astype(BF16)
        dkctx[0] += dKc
        dvctx[0] += dVc

    q, kv, ctx = _att_specs(Lc, N)
    blk = pl.BlockSpec((1, CHUNK, LANES), lambda g, n: (g, n, 0))
    part = jax.ShapeDtypeStruct((nkv, L, LANES), BF16)
    cshape = jax.ShapeDtypeStruct((nkv, Lc, LANES), F32)
    return pl.pallas_call(
        body, name="att_bwd", grid=(nkv, N),
        in_specs=[pl.BlockSpec(memory_space=pltpu.SMEM), q, kv(-1), kv(0), kv(1), kv(-1), kv(0), kv(1), ctx, ctx,
                  pl.BlockSpec((CHUNK, 2 * LANES), lambda g, n: (n, blk0 + g))],
        out_specs=[q, blk, blk, blk, blk, blk, blk, ctx, ctx,
                   pl.BlockSpec((1, 8, LANES), lambda g, n: (g, 0, 0))],
        out_shape=[jax.ShapeDtypeStruct(Qa.shape, F32), part, part, part, part, part, part, cshape, cshape,
                   jax.ShapeDtypeStruct((nkv, 8, LANES), F32)],
        compiler_params=_params(2, False))(sink, Qa, Kd, Kd, Kd, Vd, Vd, Vd, Kdc, Vdc, dY)


def _bwd_proj(dx, gt, w, saved, G=None, U=None, name="bwd_proj"):
    M, D = dx.shape
    N = w.shape[0]
    tm, tn = _tile(M, 512, 8), _tile(N, 512)
    swiglu = G is not None

    def body(*refs):
        if swiglu:
            dx_ref, gt_ref, w_ref, sv_ref, G_ref, U_ref, dG_ref, dU_ref, dz_ref, dgt_ref, zs = refs
        else:
            dx_ref, gt_ref, w_ref, sv_ref, dA_ref, dz_ref, dgt_ref, zs = refs
        i, j = pl.program_id(0), pl.program_id(1)

        @pl.when(jnp.logical_and(i == 0, j == 0))
        def _():
            dgt_ref[...] = jnp.zeros_like(dgt_ref)

        @pl.when(j == 0)
        def _():
            d = dx_ref[...]
            z = (d * gt_ref[...]).astype(BF16)
            zs[...] = z
            dz_ref[...] = z
            dgt_ref[...] += jnp.sum(d * sv_ref[...].astype(F32), axis=0, keepdims=True)

        dA = _nt(zs[...], w_ref[...])
        if swiglu:
            Gv = G_ref[...].astype(F32)
            Uv = U_ref[...].astype(F32)
            sg = _sigmoid(Gv)
            dU_ref[...] = (dA * Gv * sg).astype(BF16)
            dG_ref[...] = (dA * Uv * (sg * (1.0 + Gv * (1.0 - sg)))).astype(BF16)
        else:
            dA_ref[...] = dA.astype(BF16)

    row = pl.BlockSpec((tm, D), lambda i, j: (i, 0))
    vec = pl.BlockSpec((1, D), lambda i, j: (0, 0))
    tile = pl.BlockSpec((tm, tn), lambda i, j: (i, j))
    big = jax.ShapeDtypeStruct((M, N), BF16)
    in_specs = [row, vec, pl.BlockSpec((tn, D), lambda i, j: (j, 0)), row]
    args = [dx, gt, w, saved]
    if swiglu:
        in_specs += [tile, tile]
        args += [G, U]
        out_specs = [tile, tile, row, vec]
        out_shape = [big, big, jax.ShapeDtypeStruct((M, D), BF16), jax.ShapeDtypeStruct((1, D), F32)]
    else:
        out_specs = [tile, row, vec]
        out_shape = [big, jax.ShapeDtypeStruct((M, D), BF16), jax.ShapeDtypeStruct((1, D), F32)]
    return pl.pallas_call(
        body, name=name, grid=(M // tm, N // tn), in_specs=in_specs, out_specs=out_specs, out_shape=out_shape,
        scratch_shapes=[pltpu.VMEM((tm, D), BF16)], compiler_params=_params(2))(*args)


def _tn_matmul(pairs, name):
    Ka, Nb = pairs[0][0].shape[1], pairs[0][1].shape[1]
    tk, tn = _tile(Ka, 1024), _tile(Nb, 1536)
    tls, nks = [], []
    for a, _ in pairs:
        tl = _tile(a.shape[0], 512, 8)
        tls.append(tl)
        nks.append(a.shape[0] // tl)
    starts = [int(s) for s in np.cumsum([0] + nks[:-1])]
    nk = int(sum(nks))

    def body(*refs):
        out_ref, acc = refs[-2], refs[-1]
        k = pl.program_id(2)

        @pl.when(k == 0)
        def _():
            acc[...] = jnp.zeros_like(acc)

        for idx in range(len(pairs)):
            a_ref, b_ref = refs[2 * idx], refs[2 * idx + 1]

            @pl.when(jnp.logical_and(k >= starts[idx], k < starts[idx] + nks[idx]))
            def _():
                acc[...] += _tn(a_ref[...], b_ref[...])

        @pl.when(k == nk - 1)
        def _():
            out_ref[...] = acc[...].astype(BF16)

    in_specs, args = [], []
    for idx, (a, b) in enumerate(pairs):
        s0, n_ = starts[idx], nks[idx]
        in_specs.append(pl.BlockSpec((tls[idx], tk), lambda i, j, k, s0=s0, n_=n_: (jnp.clip(k - s0, 0, n_ - 1), i)))
        in_specs.append(pl.BlockSpec((tls[idx], tn), lambda i, j, k, s0=s0, n_=n_: (jnp.clip(k - s0, 0, n_ - 1), j)))
        args += [a, b]
    return pl.pallas_call(
        body, name=name, grid=(Ka // tk, Nb // tn, nk), in_specs=in_specs,
        out_specs=pl.BlockSpec((tk, tn), lambda i, j, k: (i, j)),
        out_shape=jax.ShapeDtypeStruct((Ka, Nb), BF16),
        scratch_shapes=[pltpu.VMEM((tk, tn), F32)], compiler_params=_params(3))(*args)


def _bwd_norm_mod(pairs, x, dres, g, sh, sc, name):
    M, D = x.shape
    K = pairs[0][0].shape[1]
    tm, tk = _tile(M, 512, 8), _tile(K, 512)
    nk = K // tk
    npair = len(pairs)
    has_res = dres is not None

    def body(*refs):
        pr = refs[:2 * npair]
        rest = refs[2 * npair:]
        if has_res:
            x_ref, dres_ref, g_ref, sh_ref, sc_ref, dx_ref, st_ref, acc = rest
        else:
            x_ref, g_ref, sh_ref, sc_ref, dx_ref, st_ref, acc = rest
        del sh_ref
        i, k = pl.program_id(0), pl.program_id(1)

        @pl.when(jnp.logical_and(i == 0, k == 0))
        def _():
            st_ref[...] = jnp.zeros_like(st_ref)

        @pl.when(k == 0)
        def _():
            acc[...] = jnp.zeros_like(acc)

        for idx in range(npair):
            acc[...] += _nt(pr[2 * idx][...], pr[2 * idx + 1][...])

        @pl.when(k == nk - 1)
        def _():
            xv = x_ref[...]
            gv = g_ref[...]
            dh = acc[...]
            r = lax.rsqrt(jnp.mean(xv * xv, axis=-1, keepdims=True) + NORM_EPS)
            xh = xv * r
            st_ref[0:1, :] += jnp.sum(dh, axis=0, keepdims=True)
            st_ref[1:2, :] += jnp.sum(dh * (xh * gv), axis=0, keepdims=True)
            dn = dh * (1.0 + sc_ref[...])
            st_ref[2:3, :] += jnp.sum(dn * xh, axis=0, keepdims=True)
            dxh = dn * gv
            d = r * (dxh - xh * jnp.mean(dxh * xh, axis=-1, keepdims=True))
            if has_res:
                d = d + dres_ref[...]
            dx_ref[...] = d

    row = pl.BlockSpec((tm, D), lambda i, k: (i, 0))
    vec = pl.BlockSpec((1, D), lambda i, k: (0, 0))
    in_specs, args = [], []
    for dA, w in pairs:
        in_specs += [pl.BlockSpec((tm, tk), lambda i, k: (i, k)), pl.BlockSpec((D, tk), lambda i, k: (0, k))]
        args += [dA, w]
    in_specs += [row] + ([row] if has_res else []) + [vec, vec, vec]
    args += [x] + ([dres] if has_res else []) + [g, sh, sc]
    return pl.pallas_call(
        body, name=name, grid=(M // tm, nk), in_specs=in_specs,
        out_specs=[row, pl.BlockSpec((8, D), lambda i, k: (0, 0))],
        out_shape=[jax.ShapeDtypeStruct((M, D), F32), jax.ShapeDtypeStruct((8, D), F32)],
        scratch_shapes=[pltpu.VMEM((tm, D), F32)], compiler_params=_params(2))(*args)


def _local_step(x, ctx, tgt, mod, modc, norm_mix, norm_ffn, norm_final, lg, sink, w_in, w_out, w_gate, w_up, w_down):
    L, D = x.shape
    Lc = ctx.shape[0]
    d_proj = w_in.shape[1]
    npairs = RET_HEADS // 2
    nkv = ATT_KV_HEADS
    nkvp = nkv // 2
    o_rq = 0
    o_rk = o_rq + RET_HEADS * RET_DK // LANES
    o_rv = o_rk + RET_HEADS * RET_DK // LANES
    o_rg = o_rv + RET_HEADS * RET_DV // LANES
    o_aq = o_rg + RET_HEADS * RET_DV // LANES
    o_ak = o_aq + ATT_HEADS * ATT_DH // LANES
    o_av = o_ak + nkv * ATT_DH // LANES
    assert (o_av + nkv * ATT_DH // LANES) * LANES == d_proj
    assert o_rv % 2 == 0 and o_rg % 2 == 0 and (RET_HEADS * RET_DV) % (2 * LANES) == 0
    rv_blk, rg_blk = o_rv // 2, o_rg // 2
    d_ret = RET_HEADS * RET_DV
    d_mix = d_ret + ATT_HEADS * ATT_DH
    att_blk = d_ret // (2 * LANES)
    k_scale = RET_DK ** -0.5
    a_scale = ATT_DH ** -0.5

    T = _rope_tables(L)
    Tc = dict(C=jnp.ones((Lc, LANES), F32), S=jnp.zeros((Lc, LANES), F32))
    row = lambda m, i: m[i:i + 1]
    sh_m, sc_m, gt_m, sh_f, sc_f, gt_f = [row(mod, i) for i in range(6)]
    sh_mc, sc_mc = row(modc, 0), row(modc, 1)

    P, hx = _norm_mod_matmul(x, norm_mix, sh_m, sc_m, w_in, "in_proj")
    Pc, hc = _norm_mod_matmul(ctx, norm_mix, sh_mc, sc_mc, w_in, "in_proj_ctx")
    nq = RET_HEADS * RET_DK // LANES
    Qr = _rope_cols(P, o_rq, nq, T["Cr"], T["Sr"], T["Rr"], 1.0, True, "rope_rq")
    Kr = _rope_cols(P, o_rk, nq, T["Cr"], T["Sr"], T["Rr"], k_scale, True, "rope_rk")
    Krc = _rope_cols(Pc, o_rk, nq, Tc["C"], Tc["S"], T["Rr"], k_scale, False, "scale_rk_ctx")
    Qa = _rope_cols(P, o_aq, ATT_HEADS * ATT_DH // LANES, T["Ca"], T["Sa"], T["Ra"], a_scale, True, "rope_aq")
    Kd = _dup_heads(P, o_ak, nkvp, T["Ca"], T["Sa"], T["Ra"], T["D0"], T["D1"], True, "dup_ak")
    Vd = _dup_heads(P, o_av, nkvp, T["Ca"], T["Sa"], T["Ra"], T["D0"], T["D1"], False, "dup_av")
    Kdc = _dup_heads(Pc, o_ak, nkvp, Tc["C"], Tc["S"], T["Ra"], T["D0"], T["D1"], False, "dup_ak_ctx")
    Vdc = _dup_heads(Pc, o_av, nkvp, Tc["C"], Tc["S"], T["Ra"], T["D0"], T["D1"], False, "dup_av_ctx")

    SF = _ret_states_fwd(Kr, P, Krc, Pc, lg, rv_blk, npairs)
    Y, SB = _ret_out_fwd(Qr, Kr, P, Krc, Pc, SF, lg, rv_blk, rg_blk, npairs, d_mix)
    Y = _att_fwd(Qa, Kd, Vd, Kdc, Vdc, sink, Y, att_blk)

    x1, O1 = _proj_residual(Y, w_out, x, gt_m, "out_proj")
    G, U, A, h2 = _ffn_in(x1, norm_ffn, sh_f, sc_f, w_gate, w_up)
    x2, Fo = _proj_residual(A, w_down, x1, gt_f, "ffn_out")
    dx2, loss, d_norm_final = _final(x2, norm_final, tgt)

    dG, dU, dz2, dgt_f = _bwd_proj(dx2, gt_f, w_down, Fo, G, U, name="ffn_out_bwd")
    g_w_down = _tn_matmul([(A, dz2)], "grad_w_down")
    dx1, st_f = _bwd_norm_mod([(dG, w_gate), (dU, w_up)], x1, dx2, norm_ffn, sh_f, sc_f, "ffn_in_bwd")
    g_w_gate = _tn_matmul([(h2, dG)], "grad_w_gate")
    g_w_up = _tn_matmul([(h2, dU)], "grad_w_up")
    dY, dz1, dgt_m = _bwd_proj(dx1, gt_m, w_out, O1, name="out_proj_bwd")
    g_w_out = _tn_matmul([(Y, dz1)], "grad_w_out")

    dQa, dKp, dKs, dKn, dVp, dVs, dVn, dKdc, dVdc, dsink = _att_bwd(Qa, Kd, Vd, Kdc, Vdc, sink, dY, att_blk)
    dQr, dKr, dVr, dP, dO, dKc, dVc, acc1 = _ret_bwd1(Qr, Kr, P, Krc, Pc, SF, SB, dY, lg, rv_blk, rg_blk, npairs, d_proj)
    dKr, dP, dKc, dVc, acc2 = _ret_bwd2(Qr, Kr, P, Krc, Pc, SB, dO, dKr, dVr, dP, dKc, dVc, lg, rv_blk, npairs)

    dP = _unrope_cols(dQr, dP, o_rq, nq, T["Cr"], T["Sr"], T["RrT"], 1.0, True, "unrope_rq")
    dP = _unrope_cols(dKr, dP, o_rk, nq, T["Cr"], T["Sr"], T["RrT"], k_scale, True, "unrope_rk")
    dP = _unrope_cols(dQa, dP, o_aq, ATT_HEADS * ATT_DH // LANES, T["Ca"], T["Sa"], T["RaT"], a_scale, True, "unrope_aq")
    dP = _fold_heads([(dKs, 0), (dKp, 1), (dKn, -1)], dP, o_ak, nkvp, T["Ca"], T["Sa"], T["RaT"], T["D0T"], T["D1T"],
                     True, "fold_ak")
    dP = _fold_heads([(dVs, 0), (dVp, 1), (dVn, -1)], dP, o_av, nkvp, T["Ca"], T["Sa"], T["RaT"], T["D0T"], T["D1T"],
                     False, "fold_av")
    dPc = jnp.zeros((Lc, d_proj), BF16)
    dPc = _unrope_cols(dKc, dPc, o_rk, nq, Tc["C"], Tc["S"], T["RrT"], k_scale, False, "ctx_rk_bwd")
    dPc = _unrope_cols(dVc, dPc, o_rv, RET_HEADS * RET_DV // LANES, Tc["C"], Tc["S"], T["RrT"], 1.0, False, "ctx_rv_bwd")
    dPc = _fold_heads([(dKdc.astype(BF16), 0)], dPc, o_ak, nkvp, Tc["C"], Tc["S"], T["RaT"], T["D0T"], T["D1T"],
                      False, "fold_ak_ctx")
    dPc = _fold_heads([(dVdc.astype(BF16), 0)], dPc, o_av, nkvp, Tc["C"], Tc["S"], T["RaT"], T["D0T"], T["D1T"],
                      False, "fold_av_ctx")

    dx, st_m = _bwd_norm_mod([(dP, w_in)], x, dx1, norm_mix, sh_m, sc_m, "in_proj_bwd")
    _, st_mc = _bwd_norm_mod([(dPc, w_in)], ctx, None, norm_mix, sh_mc, sc_mc, "in_proj_ctx_bwd")
    g_w_in = _tn_matmul([(hx, dP), (hc, dPc)], "grad_w_in")

    a1 = acc1[:, :, :, 0].reshape(RET_HEADS, ACC_ROWS)
    a2 = acc2[:, :, :, 0].reshape(RET_HEADS, ACC_ROWS)
    dlam = a1[:, 0] + a1[:, 2] + a1[:, 3] + a1[:, 4]
    dmu = a1[:, 1] + a1[:, 5] + a2[:, 0] + a2[:, 1]
    d_ret_decay = jnp.stack([dlam, dmu]) * lg
    d_sink = dsink[:, :4, 0].reshape(1, ATT_HEADS)

    zero = jnp.zeros((1, D), F32)
    dmod = jnp.concatenate([st_m[0:1], st_m[1:2], dgt_m, st_f[0:1], st_f[1:2], dgt_f], axis=0)
    dmodc = jnp.concatenate([st_mc[0:1], st_mc[1:2], zero, zero, zero, zero], axis=0)
    return dict(loss=loss[0, 0], grad_x=dx, dmod=dmod, dmodc=dmodc,
                d_norm_mix=st_m[2:3] + st_mc[2:3], d_norm_ffn=st_f[2:3], d_norm_final=d_norm_final,
                d_ret_decay=d_ret_decay, d_sink=d_sink,
                g_w_in=g_w_in, g_w_out=g_w_out, g_w_gate=g_w_gate, g_w_up=g_w_up, g_w_down=g_w_down)


def _my_pos():
    return lax.axis_index("x"), lax.axis_index("y"), lax.axis_index("c")


def _other_chips(x, y):
    return [(1 - x, y), (x, 1 - y), (1 - x, 1 - y)]


def _remote(src, dst, ssem, rsem, dev):
    return pltpu.make_async_remote_copy(src_ref=src, dst_ref=dst, send_sem=ssem, recv_sem=rsem,
                                        device_id=dev, device_id_type=MESH)


def _allgather8(v, name):
    R, Cc = v.shape

    def body(v_ref, out_ref, send_sems, recv_sems):
        x, y, c = _my_pos()
        me = 4 * x + 2 * y + c
        out_ref[pl.ds(me, 1)] = v_ref[...][None]
        peers = []
        for j in range(1, N_DEV):
            peers.append((1 - x if (j >> 2) & 1 else x, 1 - y if (j >> 1) & 1 else y, 1 - c if j & 1 else c))
        copies = []
        for j, peer in enumerate(peers):
            cp = _remote(v_ref, out_ref.at[me], send_sems.at[j], recv_sems.at[j], peer)
            cp.start()
            copies.append(cp)
        for j, peer in enumerate(peers):
            pid = 4 * peer[0] + 2 * peer[1] + peer[2]
            _remote(v_ref, out_ref.at[pid], send_sems.at[j], recv_sems.at[j], peer).wait_recv()
        for cp in copies:
            cp.wait_send()

    return pl.pallas_call(
        body, name=name, out_shape=jax.ShapeDtypeStruct((N_DEV, R, Cc), v.dtype),
        in_specs=[pl.BlockSpec(memory_space=pltpu.VMEM)], out_specs=pl.BlockSpec(memory_space=pltpu.VMEM),
        scratch_shapes=[pltpu.SemaphoreType.DMA((N_DEV - 1,)), pltpu.SemaphoreType.DMA((N_DEV - 1,))])(v)


def _region(ref, k, half, shard_shape, axis):
    r, cs = shard_shape
    hr = r // 2
    if axis == 1:
        return ref.at[pl.ds(pl.multiple_of(half * hr, 16), hr), pl.ds(pl.multiple_of(k * cs, LANES), cs)]
    return ref.at[pl.ds(pl.multiple_of(k * r + half * hr, 16), hr), :]


def _full_shape(shard_shape, axis):
    r, cs = shard_shape
    return (r, N_CHIPS * cs) if axis == 1 else (N_CHIPS * r, cs)


def _allgather_weights(shards, axes):
    nw = len(shards)
    shapes = [s.shape for s in shards]

    def body(*refs):
        sh_refs, out_refs = refs[:nw], refs[nw:2 * nw]
        send1, recv1, send2, recv2, lsem = refs[2 * nw:]
        x, y, c = _my_pos()
        k0 = 2 * x + y
        chips = _other_chips(x, y)
        sends = []
        locals_ = []
        for w in range(nw):
            r, cs = shapes[w]
            hr = r // 2
            if axes[w] == 1:
                own = out_refs[w].at[:, pl.ds(pl.multiple_of(k0 * cs, LANES), cs)]
            else:
                own = out_refs[w].at[pl.ds(pl.multiple_of(k0 * r, 16), r), :]
            lc = pltpu.make_async_copy(sh_refs[w], own, lsem.at[w])
            lc.start()
            locals_.append(lc)
            for j, ch in enumerate(chips):
                cp = _remote(sh_refs[w].at[pl.ds(pl.multiple_of(c * hr, 16), hr), :],
                             _region(out_refs[w], k0, c, shapes[w], axes[w]),
                             send1.at[w, j], recv1.at[w, j], (ch[0], ch[1], c))
                cp.start()
                sends.append(cp)
        for w in range(nw):
            for j, ch in enumerate(chips):
                kj = 2 * ch[0] + ch[1]
                got = _region(out_refs[w], kj, c, shapes[w], axes[w])
                _remote(got, got, send1.at[w, j], recv1.at[w, j], (ch[0], ch[1], c)).wait_recv()
                fw = _remote(got, got, send2.at[w, j], recv2.at[w, j], (x, y, 1 - c))
                fw.start()
                sends.append(fw)
        for w in range(nw):
            for j, ch in enumerate(chips):
                kj = 2 * ch[0] + ch[1]
                got = _region(out_refs[w], kj, 1 - c, shapes[w], axes[w])
                _remote(got, got, send2.at[w, j], recv2.at[w, j], (x, y, 1 - c)).wait_recv()
        for cp in sends:
            cp.wait_send()
        for lc in locals_:
            lc.wait()

    anyspec = pl.BlockSpec(memory_space=pl.ANY)
    sem = lambda: pltpu.SemaphoreType.DMA((nw, 3))
    return pl.pallas_call(
        body, name="allgather_weights",
        out_shape=[jax.ShapeDtypeStruct(_full_shape(s, a), BF16) for s, a in zip(shapes, axes)],
        in_specs=[anyspec] * nw, out_specs=[anyspec] * nw,
        scratch_shapes=[sem(), sem(), sem(), sem(), pltpu.SemaphoreType.DMA((nw,))])(*shards)


def _half_pieces(ref, half, shard_shape, axis):
    r, cs = shard_shape
    hr = r // 2
    if axis == 1:
        return [ref.at[pl.ds(pl.multiple_of(half * hr, 16), hr), :]]
    return [ref.at[pl.ds(pl.multiple_of(k * r + half * hr, 16), hr), :] for k in range(N_CHIPS)]


def _rs_sibling(grads, shapes, axes):
    nw = len(grads)
    npc = max(1 if a == 1 else N_CHIPS for a in axes)

    def body(*refs):
        g_refs, out_refs = refs[:nw], refs[nw:2 * nw]
        send, recv = refs[2 * nw:]
        x, y, c = _my_pos()
        sib = (x, y, 1 - c)
        copies = []
        for w in range(nw):
            src = _half_pieces(g_refs[w], 1 - c, shapes[w], axes[w])
            dst = _half_pieces(out_refs[w], 1 - c, shapes[w], axes[w])
            for i, (s, d) in enumerate(zip(src, dst)):
                cp = _remote(s, d, send.at[w, i], recv.at[w, i], sib)
                cp.start()
                copies.append(cp)
        for w in range(nw):
            mine = _half_pieces(out_refs[w], c, shapes[w], axes[w])
            for i, d in enumerate(mine):
                _remote(d, d, send.at[w, i], recv.at[w, i], sib).wait_recv()
        for cp in copies:
            cp.wait_send()

    anyspec = pl.BlockSpec(memory_space=pl.ANY)
    return pl.pallas_call(
        body, name="rs_sibling",
        out_shape=[jax.ShapeDtypeStruct(_full_shape(s, a), BF16) for s, a in zip(shapes, axes)],
        in_specs=[anyspec] * nw, out_specs=[anyspec] * nw,
        scratch_shapes=[pltpu.SemaphoreType.DMA((nw, npc)), pltpu.SemaphoreType.DMA((nw, npc))])(*grads)


def _half_block_spec(shard_shape, axis, tr):
    r, cs = shard_shape
    hr = r // 2
    if axis == 1:
        return pl.BlockSpec((tr, cs), lambda k, i, c_ref: (c_ref[0] * (hr // tr) + i, k))
    return pl.BlockSpec((tr, cs), lambda k, i, c_ref: (k * (r // tr) + c_ref[0] * (hr // tr) + i, 0))


def _add_halves(g, recv, cvec, shard_shape, axis, name):
    r, cs = shard_shape
    hr = r // 2
    tr = _tile(hr, 256, 16)

    def body(c_ref, a_ref, b_ref, o_ref):
        del c_ref
        o_ref[0] = (a_ref[...].astype(F32) + b_ref[...].astype(F32)).astype(BF16)

    spec = _half_block_spec(shard_shape, axis, tr)
    return pl.pallas_call(
        body, name=name,
        grid_spec=pltpu.PrefetchScalarGridSpec(
            num_scalar_prefetch=1, grid=(N_CHIPS, hr // tr), in_specs=[spec, spec],
            out_specs=pl.BlockSpec((1, tr, cs), lambda k, i, c_ref: (k, i, 0))),
        out_shape=jax.ShapeDtypeStruct((N_CHIPS, hr, cs), BF16),
        compiler_params=_params(2, False))(cvec, g, recv)


def _rs_chips(sums):
    nw = len(sums)

    def body(*refs):
        s_refs, out_refs = refs[:nw], refs[nw:2 * nw]
        send, recv, lsem = refs[2 * nw:]
        x, y, c = _my_pos()
        k0 = 2 * x + y
        chips = _other_chips(x, y)
        copies, locals_ = [], []
        for w in range(nw):
            lc = pltpu.make_async_copy(s_refs[w].at[k0], out_refs[w].at[k0], lsem.at[w])
            lc.start()
            locals_.append(lc)
            for j, ch in enumerate(chips):
                kj = 2 * ch[0] + ch[1]
                cp = _remote(s_refs[w].at[kj], out_refs[w].at[k0], send.at[w, j], recv.at[w, j], (ch[0], ch[1], c))
                cp.start()
                copies.append(cp)
        for w in range(nw):
            for j, ch in enumerate(chips):
                kj = 2 * ch[0] + ch[1]
                d = out_refs[w].at[kj]
                _remote(d, d, send.at[w, j], recv.at[w, j], (ch[0], ch[1], c)).wait_recv()
        for cp in copies:
            cp.wait_send()
        for lc in locals_:
            lc.wait()

    anyspec = pl.BlockSpec(memory_space=pl.ANY)
    return pl.pallas_call(
        body, name="rs_chips",
        out_shape=[jax.ShapeDtypeStruct(s.shape, BF16) for s in sums],
        in_specs=[anyspec] * nw, out_specs=[anyspec] * nw,
        scratch_shapes=[pltpu.SemaphoreType.DMA((nw, 3)), pltpu.SemaphoreType.DMA((nw, 3)),
                        pltpu.SemaphoreType.DMA((nw,))])(*sums)


def _sum_chips(parts, name):
    _, hr, cs = parts.shape
    tr = _tile(hr, 256, 16)

    def body(p_ref, o_ref):
        t = p_ref[0].astype(F32)
        for k in range(1, N_CHIPS):
            t = t + p_ref[k].astype(F32)
        o_ref[...] = t

    return pl.pallas_call(
        body, name=name, grid=(hr // tr,),
        in_specs=[pl.BlockSpec((N_CHIPS, tr, cs), lambda i: (0, i, 0))],
        out_specs=pl.BlockSpec((tr, cs), lambda i: (i, 0)),
        out_shape=jax.ShapeDtypeStruct((hr, cs), F32), compiler_params=_params(1, False))(parts)


def _exchange_halves(halves):
    nw = len(halves)

    def body(*refs):
        h_refs, out_refs = refs[:nw], refs[nw:2 * nw]
        send, recv, lsem = refs[2 * nw:]
        x, y, c = _my_pos()
        sib = (x, y, 1 - c)
        copies, locals_ = [], []
        for w in range(nw):
            hr = halves[w].shape[0]
            mine = out_refs[w].at[pl.ds(pl.multiple_of(c * hr, 8), hr), :]
            lc = pltpu.make_async_copy(h_refs[w], mine, lsem.at[w])
            lc.start()
            locals_.append(lc)
            cp = _remote(h_refs[w], mine, send.at[w], recv.at[w], sib)
            cp.start()
            copies.append(cp)
        for w in range(nw):
            hr = halves[w].shape[0]
            other = out_refs[w].at[pl.ds(pl.multiple_of((1 - c) * hr, 8), hr), :]
            _remote(other, other, send.at[w], recv.at[w], sib).wait_recv()
        for cp in copies:
            cp.wait_send()
        for lc in locals_:
            lc.wait()

    anyspec = pl.BlockSpec(memory_space=pl.ANY)
    return pl.pallas_call(
        body, name="exchange_halves",
        out_shape=[jax.ShapeDtypeStruct((2 * h.shape[0], h.shape[1]), F32) for h in halves],
        in_specs=[anyspec] * nw, out_specs=[anyspec] * nw,
        scratch_shapes=[pltpu.SemaphoreType.DMA((nw,)), pltpu.SemaphoreType.DMA((nw,)),
                        pltpu.SemaphoreType.DMA((nw,))])(*halves)


def _cast_bf16(w, name):
    r, cs = w.shape
    tr = _tile(r, 256, 16)

    def body(w_ref, o_ref):
        o_ref[...] = w_ref[...].astype(BF16)

    spec = pl.BlockSpec((tr, cs), lambda i: (i, 0))
    return pl.pallas_call(body, name=name, grid=(r // tr,), in_specs=[spec], out_specs=spec,
                          out_shape=jax.ShapeDtypeStruct((r, cs), BF16), compiler_params=_params(1, False))(w)


def _adam_math(w, g, m, v):
    m2 = ADAM_B1 * m + (1.0 - ADAM_B1) * g
    v2 = ADAM_B2 * v + (1.0 - ADAM_B2) * (g * g)
    m_hat = m2 / (1.0 - ADAM_B1 ** ADAM_STEP)
    v_hat = v2 / (1.0 - ADAM_B2 ** ADAM_STEP)
    delta = -ADAM_LR * (m_hat / (jnp.sqrt(v_hat) + ADAM_EPS) + ADAM_WD * w)
    return delta, m2, v2


def _adam(w, g, m, v, name):
    r, cs = w.shape
    tr = _tile(r, 256, 8)

    def body(w_ref, g_ref, m_ref, v_ref, d_ref, m2_ref, v2_ref):
        d, m2, v2 = _adam_math(w_ref[...], g_ref[...], m_ref[...], v_ref[...])
        d_ref[...] = d
        m2_ref[...] = m2
        v2_ref[...] = v2

    spec = pl.BlockSpec((tr, cs), lambda i: (i, 0))
    shp = jax.ShapeDtypeStruct((r, cs), F32)
    return pl.pallas_call(body, name=name, grid=(r // tr,), in_specs=[spec] * 4, out_specs=[spec] * 3,
                          out_shape=[shp, shp, shp], compiler_params=_params(1, False))(w, g, m, v)


def _mod_rows(a16, w, b, name):
    D, n = w.shape
    tn = _tile(n, 512)

    def body(a_ref, w_ref, b_ref, o_ref):
        a = a_ref[...]
        o_ref[...] = _nn((a * _sigmoid(a)).astype(BF16), w_ref[...].astype(BF16)) + b_ref[...]

    return pl.pallas_call(
        body, name=name, grid=(n // tn,),
        in_specs=[pl.BlockSpec((16, D), lambda j: (0, 0)), pl.BlockSpec((D, tn), lambda j: (0, j)),
                  pl.BlockSpec((1, tn), lambda j: (0, j))],
        out_specs=pl.BlockSpec((16, tn), lambda j: (0, j)),
        out_shape=jax.ShapeDtypeStruct((16, n), F32), compiler_params=_params(1, False))(a16, w, b)


def _w_mod_update(a16, d16, w, m, v):
    D, n = w.shape
    tn = _tile(n, 256)

    def body(a_ref, d_ref, w_ref, m_ref, v_ref, g_ref, dl_ref, m2_ref, v2_ref, p_ref):
        @pl.when(pl.program_id(0) == 0)
        def _():
            p_ref[...] = jnp.zeros_like(p_ref)
        a = a_ref[...]
        db = d_ref[...].astype(BF16)
        wv = w_ref[...]
        g = _tn((a * _sigmoid(a)).astype(BF16), db)
        g_ref[...] = g
        d, m2, v2 = _adam_math(wv, g, m_ref[...], v_ref[...])
        dl_ref[...] = d
        m2_ref[...] = m2
        v2_ref[...] = v2
        p_ref[...] += _nt(db, wv.astype(BF16))

    wspec = pl.BlockSpec((D, tn), lambda j: (0, j))
    shp = jax.ShapeDtypeStruct((D, n), F32)
    return pl.pallas_call(
        body, name="w_mod_update", grid=(n // tn,),
        in_specs=[pl.BlockSpec((16, D), lambda j: (0, 0)), pl.BlockSpec((16, tn), lambda j: (0, j)), wspec, wspec, wspec],
        out_specs=[wspec, wspec, wspec, wspec, pl.BlockSpec((16, D), lambda j: (0, 0))],
        out_shape=[shp, shp, shp, shp, jax.ShapeDtypeStruct((16, D), F32)],
        compiler_params=_params(1))(a16, d16, w, m, v)


def _sum_devices(g8, name):
    _, R, Cc = g8.shape

    def body(g_ref, o_ref):
        t = g_ref[0]
        for d in range(1, N_DEV):
            t = t + g_ref[d]
        o_ref[...] = t

    return pl.pallas_call(body, name=name, out_shape=jax.ShapeDtypeStruct((R, Cc), F32))(g8)


def _c_ctx_grad(parts, c_ctx):
    D = c_ctx.shape[1]

    def body(p_ref, c_ref, o_ref):
        t = p_ref[0]
        for k in range(1, N_CHIPS):
            t = t + p_ref[2 * k]
        cv = c_ref[...]
        sg = _sigmoid(cv)
        o_ref[...] = t * (sg * (1.0 + cv * (1.0 - sg)))

    return pl.pallas_call(body, name="c_ctx_grad", out_shape=jax.ShapeDtypeStruct((1, D), F32))(parts, c_ctx)


LOSS_LANE = 64


def kernel(x, c, ctx, c_ctx, w_mod, b_mod, norm_mix, norm_ffn, w_in, ret_decay, attn_sink, w_out, w_gate, w_up, w_down, norm_final, loss_target, m_c_ctx, m_w_mod, m_b_mod, m_norm_mix, m_norm_ffn, m_w_in, m_ret_decay, m_attn_sink, m_w_out, m_w_gate, m_w_up, m_w_down, m_norm_final, v_c_ctx, v_w_mod, v_b_mod, v_norm_mix, v_norm_ffn, v_w_in, v_ret_decay, v_attn_sink, v_w_out, v_w_gate, v_w_up, v_w_down, v_norm_final):
    D = x.shape[-1]
    n3 = w_mod.shape[-1]
    xi, yi, ci = _my_pos()
    b = 4 * xi + 2 * yi + ci
    k0 = 2 * xi + yi
    cvec = jnp.reshape(ci, (1,)).astype(jnp.int32)

    dense = [("w_in", w_in[0], 1), ("w_out", w_out[0], 0), ("w_gate", w_gate[0], 1), ("w_up", w_up[0], 1),
             ("w_down", w_down[0], 0)]
    axes = [a for _, _, a in dense]
    shapes = [w.shape for _, w, _ in dense]
    shards16 = [_cast_bf16(w, "cast_" + n) for n, w, _ in dense]
    f_in, f_out, f_gate, f_up, f_down = _allgather_weights(shards16, axes)

    c_all = _allgather8(c, "gather_c").reshape(N_DEV, D)
    c_ctx2 = c_ctx.reshape(1, D)
    a16 = jnp.concatenate([c_all, c_ctx2, jnp.zeros((16 - N_DEV - 1, D), F32)], axis=0)
    b_cols = lax.dynamic_slice_in_dim(b_mod, k0 * n3, n3, axis=1)
    mod16 = _mod_rows(a16, w_mod[0], b_cols, "mod_rows")
    mod_all = _allgather8(mod16, "gather_mod")
    mine = jnp.stack([lax.dynamic_index_in_dim(mod_all, 2 * k + ci, 0, keepdims=False) for k in range(N_CHIPS)])
    mod = lax.dynamic_index_in_dim(mine, b, 1, keepdims=False).reshape(6, D)
    modc = mine[:, N_DEV].reshape(6, D)

    lg = -jnp.exp(ret_decay[0])
    out = _local_step(x[0], ctx[0], loss_target[0], mod, modc, norm_mix, norm_ffn, norm_final.reshape(1, D), lg,
                      attn_sink, f_in, f_out, f_gate, f_up, f_down)

    grads16 = [out["g_w_in"], out["g_w_out"], out["g_w_gate"], out["g_w_up"], out["g_w_down"]]
    from_sib = _rs_sibling(grads16, shapes, axes)
    sums = [_add_halves(g, r, cvec, s, a, "add_halves_" + n)
            for g, r, s, a, (n, _, _) in zip(grads16, from_sib, shapes, axes, dense)]
    landed = _rs_chips(sums)
    halves = [_sum_chips(p, "sum_chips_" + n) for p, (n, _, _) in zip(landed, dense)]
    g_dense = _exchange_halves(halves)

    nh = 2 * RET_HEADS
    assert nh + ATT_HEADS <= LOSS_LANE
    misc = jnp.concatenate([out["d_ret_decay"].reshape(1, nh), out["d_sink"].reshape(1, ATT_HEADS),
                            jnp.zeros((1, LOSS_LANE - nh - ATT_HEADS), F32), out["loss"].reshape(1, 1),
                            jnp.zeros((1, D - LOSS_LANE - 1), F32)], axis=1)
    small = jnp.concatenate([out["dmod"], out["dmodc"], out["d_norm_mix"], out["d_norm_ffn"], out["d_norm_final"],
                             misc], axis=0)
    small_all = _allgather8(small, "gather_small")
    tot = _sum_devices(small_all, "sum_small")
    g_b_mod = (tot[0:6] + tot[6:12]).reshape(1, 6 * D)
    dmodc_tot = tot[6:12].reshape(1, 6 * D)
    dmod_rows = small_all[:, 0:6].reshape(N_DEV, 6 * D)
    d16 = jnp.concatenate([dmod_rows, dmodc_tot, jnp.zeros((16 - N_DEV - 1, 6 * D), F32)], axis=0)
    d16 = lax.dynamic_slice_in_dim(d16, k0 * n3, n3, axis=1)
    g_w_mod, dl_w_mod, m2_w_mod, v2_w_mod, part = _w_mod_update(a16, d16, w_mod[0], m_w_mod[0], v_w_mod[0])
    part_all = _allgather8(part[N_DEV:N_DEV + 1], "gather_c_ctx")
    g_c_ctx = _c_ctx_grad(part_all, c_ctx2)
    loss = tot[15, LOSS_LANE]
    g_ret_decay = tot[15, :nh].reshape(1, 2, RET_HEADS)
    g_sink = tot[15, nh:nh + ATT_HEADS].reshape(1, ATT_HEADS)

    def pad_row(v):
        v = v.reshape(1, -1)
        return jnp.concatenate([v, jnp.zeros((1, D - v.shape[1]), F32)], axis=1)

    def pack(cc, bm, nm, nf, nfin, rd, sk):
        rows = [bm.reshape(6, D), cc.reshape(1, D), nm.reshape(1, D), nf.reshape(1, D), nfin.reshape(1, D),
                pad_row(jnp.concatenate([rd.reshape(-1), sk.reshape(-1)])), jnp.zeros((5, D), F32)]
        return jnp.concatenate(rows, axis=0)

    w_s = pack(c_ctx, b_mod, norm_mix, norm_ffn, norm_final, ret_decay, attn_sink)
    g_s = pack(g_c_ctx, g_b_mod, tot[12], tot[13], tot[14], g_ret_decay, g_sink)
    m_s = pack(m_c_ctx, m_b_mod, m_norm_mix, m_norm_ffn, m_norm_final, m_ret_decay, m_attn_sink)
    v_s = pack(v_c_ctx, v_b_mod, v_norm_mix, v_norm_ffn, v_norm_final, v_ret_decay, v_attn_sink)
    small_upd = _adam(w_s, g_s, m_s, v_s, "adam_small")

    def unpack(t):
        return dict(b_mod=t[0:6].reshape(1, 6 * D), c_ctx=t[6], norm_mix=t[7:8], norm_ffn=t[8:9], norm_final=t[9],
                    ret_decay=t[10, :nh].reshape(1, 2, RET_HEADS), attn_sink=t[10, nh:nh + ATT_HEADS].reshape(1, ATT_HEADS))

    dense_w = dict(w_in=(w_in, m_w_in, v_w_in), w_out=(w_out, m_w_out, v_w_out), w_gate=(w_gate, m_w_gate, v_w_gate),
                   w_up=(w_up, m_w_up, v_w_up), w_down=(w_down, m_w_down, v_w_down))
    grads = dict(unpack(g_s), w_mod=g_w_mod[None])
    upd = [dict(unpack(t)) for t in small_upd]
    upd[0]["w_mod"], upd[1]["w_mod"], upd[2]["w_mod"] = dl_w_mod[None], m2_w_mod[None], v2_w_mod[None]
    for (n, _, _), g in zip(dense, g_dense):
        w_, m_, v_ = dense_w[n]
        res = _adam(w_[0], g, m_[0], v_[0], "adam_" + n)
        grads[n] = g[None]
        for u, r_ in zip(upd, res):
            u[n] = r_[None]

    order = ['c_ctx', 'w_mod', 'b_mod', 'norm_mix', 'norm_ffn', 'w_in', 'ret_decay', 'attn_sink', 'w_out', 'w_gate',
             'w_up', 'w_down', 'norm_final']
    outs = [loss, out["grad_x"][None]] + [grads[n] for n in order]
    for u in upd:
        outs += [u[n] for n in order]
    return tuple(outs)
```

```python
import functools
import numpy as np
import jax
import jax.numpy as jnp
from jax import lax
from jax.experimental import pallas as pl
from jax.experimental.pallas import tpu as pltpu

F32 = jnp.float32
BF16 = jnp.bfloat16

RET_HEADS = 8
RET_DK = 64
RET_DV = 128
CHUNK = 128
ATT_HEADS = 16
ATT_KV_HEADS = 4
ATT_DH = 64
GRID_W = 64
ROPE_BASE = 10000.0
NORM_EPS = 1e-6
ADAM_LR = 0.001
ADAM_B1 = 0.9
ADAM_B2 = 0.999
ADAM_EPS = 1e-08
ADAM_WD = 0.01
ADAM_STEP = 10
NEG = -1e30
LANES = 128
VMEM_LIMIT = 56 * 1024 * 1024
MESH = pl.DeviceIdType.MESH
N_CHIPS = 4
N_DEV = 8


def _nn(a, b):
    return jnp.dot(a, b, preferred_element_type=F32)


def _nt(a, b):
    return lax.dot_general(a, b, (((1,), (1,)), ((), ())), preferred_element_type=F32)


def _tn(a, b):
    return lax.dot_general(a, b, (((0,), (0,)), ((), ())), preferred_element_type=F32)


def _tile(n, pref, unit=LANES):
    t = min(n, pref)
    t -= t % unit
    while t > unit and n % t:
        t -= unit
    if t <= 0 or n % t:
        return n
    return t


def _params(ndim, vmem=True):
    return pltpu.CompilerParams(dimension_semantics=("arbitrary",) * ndim,
                                vmem_limit_bytes=VMEM_LIMIT if vmem else None)


def _sigmoid(x):
    return 1.0 / (1.0 + jnp.exp(-x))


def _fsum(x):
    return jnp.sum(jnp.sum(x, axis=1, keepdims=True), axis=0, keepdims=True)


def _rope_tables(L):
    lane = np.arange(LANES)
    d = lane % 64
    inv_r = jnp.asarray(ROPE_BASE, F32) ** (-jnp.arange(32, dtype=F32) / 32)
    t = jnp.arange(L)
    ang_r = t.astype(F32)[:, None] * jnp.tile(inv_r, LANES // 32)[None, :]
    Rr = np.zeros((LANES, LANES), np.float32)
    for l in range(LANES):
        if d[l] < 32:
            Rr[l + 32, l] = -1.0
        else:
            Rr[l - 32, l] = 1.0
    inv_a = jnp.asarray(ROPE_BASE, F32) ** (-jnp.arange(16, dtype=F32) / 16)
    rows = (t // GRID_W).astype(F32)
    cols = (t % GRID_W).astype(F32)
    dd = d % 32
    pos = jnp.where(jnp.asarray(d < 32)[None, :], rows[:, None], cols[:, None])
    ang_a = pos * jnp.tile(inv_a, LANES // 16)[None, :]
    Ra = np.zeros((LANES, LANES), np.float32)
    for l in range(LANES):
        if dd[l] < 16:
            Ra[l + 16, l] = -1.0
        else:
            Ra[l - 16, l] = 1.0
    D0 = np.zeros((LANES, LANES), np.float32)
    D1 = np.zeros((LANES, LANES), np.float32)
    for l in range(LANES):
        D0[l % 64, l] = 1.0
        D1[64 + l % 64, l] = 1.0
    return dict(
        Cr=jnp.cos(ang_r), Sr=jnp.sin(ang_r), Rr=jnp.asarray(Rr, BF16), RrT=jnp.asarray(Rr.T, BF16),
        Ca=jnp.cos(ang_a), Sa=jnp.sin(ang_a), Ra=jnp.asarray(Ra, BF16), RaT=jnp.asarray(Ra.T, BF16),
        D0=jnp.asarray(D0, BF16), D1=jnp.asarray(D1, BF16),
        D0T=jnp.asarray(D0.T, BF16), D1T=jnp.asarray(D1.T, BF16))


def _norm_mod(xf, g, sh, sc):
    r = lax.rsqrt(jnp.mean(xf * xf, axis=-1, keepdims=True) + NORM_EPS)
    return (xf * r * g) * (1.0 + sc) + sh


def _norm_mod_matmul(x, g, sh, sc, w, name):
    M, D = x.shape
    N = w.shape[1]
    tm, tn = _tile(M, 512, 8), _tile(N, 512)

    def body(x_ref, g_ref, sh_ref, sc_ref, w_ref, p_ref, h_ref, hs):
        @pl.when(pl.program_id(1) == 0)
        def _():
            hb = _norm_mod(x_ref[...], g_ref[...], sh_ref[...], sc_ref[...]).astype(BF16)
            hs[...] = hb
            h_ref[...] = hb
        p_ref[...] = _nn(hs[...], w_ref[...]).astype(BF16)

    vec = pl.BlockSpec((1, D), lambda i, j: (0, 0))
    return pl.pallas_call(
        body, name=name, grid=(M // tm, N // tn),
        in_specs=[pl.BlockSpec((tm, D), lambda i, j: (i, 0)), vec, vec, vec,
                  pl.BlockSpec((D, tn), lambda i, j: (0, j))],
        out_specs=[pl.BlockSpec((tm, tn), lambda i, j: (i, j)), pl.BlockSpec((tm, D), lambda i, j: (i, 0))],
        out_shape=[jax.ShapeDtypeStruct((M, N), BF16), jax.ShapeDtypeStruct((M, D), BF16)],
        scratch_shapes=[pltpu.VMEM((tm, D), BF16)],
        compiler_params=_params(2))(x, g, sh, sc, w)


def _proj_residual(a, w, xres, gt, name):
    M, K = a.shape
    N = w.shape[1]
    tm, tn = _tile(M, 512, 8), _tile(N, 512)

    def body(a_ref, w_ref, x_ref, gt_ref, xo_ref, o_ref):
        o = _nn(a_ref[...], w_ref[...])
        o_ref[...] = o.astype(BF16)
        xo_ref[...] = x_ref[...] + gt_ref[...] * o

    return pl.pallas_call(
        body, name=name, grid=(M // tm, N // tn),
        in_specs=[pl.BlockSpec((tm, K), lambda i, j: (i, 0)), pl.BlockSpec((K, tn), lambda i, j: (0, j)),
                  pl.BlockSpec((tm, tn), lambda i, j: (i, j)), pl.BlockSpec((1, tn), lambda i, j: (0, j))],
        out_specs=[pl.BlockSpec((tm, tn), lambda i, j: (i, j)), pl.BlockSpec((tm, tn), lambda i, j: (i, j))],
        out_shape=[jax.ShapeDtypeStruct((M, N), F32), jax.ShapeDtypeStruct((M, N), BF16)],
        compiler_params=_params(2))(a, w, xres, gt)


def _ffn_in(x1, g, sh, sc, wg, wu):
    M, D = x1.shape
    N = wg.shape[1]
    tm, tn = _tile(M, 512, 8), _tile(N, 512)

    def body(x_ref, g_ref, sh_ref, sc_ref, wg_ref, wu_ref, G_ref, U_ref, A_ref, h_ref, hs):
        @pl.when(pl.program_id(1) == 0)
        def _():
            hb = _norm_mod(x_ref[...], g_ref[...], sh_ref[...], sc_ref[...]).astype(BF16)
            hs[...] = hb
            h_ref[...] = hb
        G = _nn(hs[...], wg_ref[...])
        U = _nn(hs[...], wu_ref[...])
        G_ref[...] = G.astype(BF16)
        U_ref[...] = U.astype(BF16)
        A_ref[...] = (G * _sigmoid(G) * U).astype(BF16)

    vec = pl.BlockSpec((1, D), lambda i, j: (0, 0))
    wspec = pl.BlockSpec((D, tn), lambda i, j: (0, j))
    ospec = pl.BlockSpec((tm, tn), lambda i, j: (i, j))
    big = jax.ShapeDtypeStruct((M, N), BF16)
    return pl.pallas_call(
        body, name="ffn_in", grid=(M // tm, N // tn),
        in_specs=[pl.BlockSpec((tm, D), lambda i, j: (i, 0)), vec, vec, vec, wspec, wspec],
        out_specs=[ospec, ospec, ospec, pl.BlockSpec((tm, D), lambda i, j: (i, 0))],
        out_shape=[big, big, big, jax.ShapeDtypeStruct((M, D), BF16)],
        scratch_shapes=[pltpu.VMEM((tm, D), BF16)],
        compiler_params=_params(2))(x1, g, sh, sc, wg, wu)


def _final(x2, gn, tgt):
    M, D = x2.shape
    tm = _tile(M, 256, 8)

    def body(x_ref, g_ref, t_ref, dx_ref, loss_ref, dg_ref):
        @pl.when(pl.program_id(0) == 0)
        def _():
            loss_ref[...] = jnp.zeros_like(loss_ref)
            dg_ref[...] = jnp.zeros_like(dg_ref)
        x = x_ref[...]
        g = g_ref[...]
        r = lax.rsqrt(jnp.mean(x * x, axis=-1, keepdims=True) + NORM_EPS)
        xh = x * r
        e = xh * g - t_ref[...]
        loss_ref[...] += (0.5 / D) * _fsum(e * e)
        dy = e * (1.0 / D)
        dg_ref[...] += jnp.sum(dy * xh, axis=0, keepdims=True)
        dxh = dy * g
        dx_ref[...] = r * (dxh - xh * jnp.mean(dxh * xh, axis=-1, keepdims=True))

    row = pl.BlockSpec((tm, D), lambda i: (i, 0))
    return pl.pallas_call(
        body, name="final_loss", grid=(M // tm,),
        in_specs=[row, pl.BlockSpec((1, D), lambda i: (0, 0)), row],
        out_specs=[row, pl.BlockSpec((1, LANES), lambda i: (0, 0)), pl.BlockSpec((1, D), lambda i: (0, 0))],
        out_shape=[jax.ShapeDtypeStruct((M, D), F32), jax.ShapeDtypeStruct((1, LANES), F32),
                   jax.ShapeDtypeStruct((1, D), F32)],
        compiler_params=_params(1))(x2, gn, tgt)


def _rope_cols(src, blk0, nblk, Ct, St, R, scale, rope, name):
    M = src.shape[0]
    tm = _tile(M, 512, 8)

    def body(x_ref, c_ref, s_ref, r_ref, o_ref):
        x = x_ref[...]
        xf = x.astype(F32)
        if rope:
            xf = xf * c_ref[...] + _nn(x.astype(BF16), r_ref[...]) * s_ref[...]
        o_ref[...] = (xf * scale).astype(BF16)

    tab = pl.BlockSpec((tm, LANES), lambda i, j: (i, 0))
    return pl.pallas_call(
        body, name=name, grid=(M // tm, nblk),
        in_specs=[pl.BlockSpec((tm, LANES), lambda i, j: (i, blk0 + j)), tab, tab,
                  pl.BlockSpec((LANES, LANES), lambda i, j: (0, 0))],
        out_specs=pl.BlockSpec((tm, LANES), lambda i, j: (i, j)),
        out_shape=jax.ShapeDtypeStruct((M, nblk * LANES), BF16),
        compiler_params=_params(2, False))(src, Ct, St, R)


def _dup_heads(src, blk0, npair, Ct, St, R, D0, D1, rope, name):
    M = src.shape[0]
    tm = _tile(M, 512, 8)

    def body(x_ref, c_ref, s_ref, r_ref, d0_ref, d1_ref, o_ref):
        x = x_ref[...]
        if rope:
            x = (x.astype(F32) * c_ref[...] + _nn(x, r_ref[...]) * s_ref[...]).astype(BF16)
        o_ref[0] = _nn(x, d0_ref[...]).astype(BF16)
        o_ref[1] = _nn(x, d1_ref[...]).astype(BF16)

    tab = pl.BlockSpec((tm, LANES), lambda i, p: (i, 0))
    mat = pl.BlockSpec((LANES, LANES), lambda i, p: (0, 0))
    return pl.pallas_call(
        body, name=name, grid=(M // tm, npair),
        in_specs=[pl.BlockSpec((tm, LANES), lambda i, p: (i, blk0 + p)), tab, tab, mat, mat, mat],
        out_specs=pl.BlockSpec((2, tm, LANES), lambda i, p: (p, i, 0)),
        out_shape=jax.ShapeDtypeStruct((2 * npair, M, LANES), BF16),
        compiler_params=_params(2, False))(src, Ct, St, R, D0, D1)


def _unrope_cols(dsrc, dst, blk0, nblk, Ct, St, RT, scale, rope, name):
    M = dsrc.shape[0]
    tm = _tile(M, 512, 8)

    def body(x_ref, c_ref, s_ref, r_ref, dst_ref, o_ref):
        del dst_ref
        xf = x_ref[...].astype(F32)
        if rope:
            xf = xf * c_ref[...] + _nn((xf * s_ref[...]).astype(BF16), r_ref[...])
        o_ref[...] = (xf * scale).astype(BF16)

    tab = pl.BlockSpec((tm, LANES), lambda i, j: (i, 0))
    return pl.pallas_call(
        body, name=name, grid=(M // tm, nblk),
        in_specs=[pl.BlockSpec((tm, LANES), lambda i, j: (i, j)), tab, tab,
                  pl.BlockSpec((LANES, LANES), lambda i, j: (0, 0)),
                  pl.BlockSpec(memory_space=pl.ANY)],
        out_specs=pl.BlockSpec((tm, LANES), lambda i, j: (i, blk0 + j)),
        out_shape=jax.ShapeDtypeStruct(dst.shape, dst.dtype),
        input_output_aliases={4: 0},
        compiler_params=_params(2, False))(dsrc, Ct, St, RT, dst)


def _fold_heads(parts, dst, blk0, npair, Ct, St, RT, D0T, D1T, rope, name):
    M = parts[0][0].shape[1]
    nb = M // CHUNK
    nparts = len(parts)

    def body(*refs):
        part_refs = refs[:nparts]
        c_ref, s_ref, r_ref, d0_ref, d1_ref, dst_ref, o_ref = refs[nparts:]
        del dst_ref
        m = pl.program_id(0)
        tot = [jnp.zeros((CHUNK, LANES), F32), jnp.zeros((CHUNK, LANES), F32)]
        for pr, (_, shift) in zip(part_refs, parts):
            ok = jnp.logical_and(m + shift >= 0, m + shift <= nb - 1)
            for e in range(2):
                tot[e] = tot[e] + jnp.where(ok, pr[e].astype(F32), 0.0)
        f = _nn(tot[0].astype(BF16), d0_ref[...]) + _nn(tot[1].astype(BF16), d1_ref[...])
        if rope:
            f = f * c_ref[...] + _nn((f * s_ref[...]).astype(BF16), r_ref[...])
        o_ref[...] = f.astype(BF16)

    def part_spec(shift):
        return pl.BlockSpec((2, CHUNK, LANES), lambda m, p: (p, jnp.clip(m + shift, 0, nb - 1), 0))

    tab = pl.BlockSpec((CHUNK, LANES), lambda m, p: (m, 0))
    mat = pl.BlockSpec((LANES, LANES), lambda m, p: (0, 0))
    return pl.pallas_call(
        body, name=name, grid=(nb, npair),
        in_specs=[part_spec(s) for _, s in parts] + [tab, tab, mat, mat, mat, pl.BlockSpec(memory_space=pl.ANY)],
        out_specs=pl.BlockSpec((CHUNK, LANES), lambda m, p: (m, blk0 + p)),
        out_shape=jax.ShapeDtypeStruct(dst.shape, dst.dtype),
        input_output_aliases={nparts + 5: 0},
        compiler_params=_params(2, False))(*[a for a, _ in parts], Ct, St, RT, D0T, D1T, dst)


def _head_masks():
    lane = lax.broadcasted_iota(jnp.int32, (1, LANES), 1)
    return [lane < 64, lane >= 64]


def _decay_vecs(lam, mu):
    i = lax.broadcasted_iota(jnp.int32, (CHUNK, 1), 0).astype(F32)
    return dict(qf=jnp.exp(lam * (i + 1.0)), kf=jnp.exp(lam * (CHUNK - 1.0 - i)),
                qb=jnp.exp(mu * (CHUNK - i)), kb=jnp.exp(mu * i),
                gf=jnp.exp(lam * float(CHUNK)), gb=jnp.exp(mu * float(CHUNK)), i=i)


def _decay_mask(lam, mu):
    r = lax.broadcasted_iota(jnp.int32, (CHUNK, CHUNK), 0)
    c = lax.broadcasted_iota(jnp.int32, (CHUNK, CHUNK), 1)
    rel = (r - c).astype(F32)
    low = rel >= 0.0
    mf = jnp.exp(lam * jnp.maximum(rel, 0.0))
    mb = jnp.exp(mu * jnp.maximum(-rel, 0.0))
    return jnp.where(low, mf, mb), rel, low


def _lam_of(lg_ref, row, idx):
    return jnp.full((1, 1), lg_ref[row, idx], F32)


def _ret_states_fwd(Kr, P, Krc, Pc, lg, rv_blk, npairs):
    L = Kr.shape[0]
    Lc = Krc.shape[0]
    N, ncc = L // CHUNK, Lc // CHUNK

    def body(lg_ref, k_ref, v_ref, kc_ref, vc_ref, sf_ref, S):
        p, n = pl.program_id(0), pl.program_id(1)
        masks = _head_masks()
        for h in range(2):
            lam = _lam_of(lg_ref, 0, 2 * p + h)
            dv = _decay_vecs(lam, lam)
            hm = masks[h]

            @pl.when(n == 0)
            def _():
                s = jnp.zeros((LANES, LANES), F32)
                for cc in range(ncc):
                    kw = jnp.where(hm, kc_ref[cc * CHUNK:(cc + 1) * CHUNK, :].astype(F32) * dv["kf"], 0.0).astype(BF16)
                    s = dv["gf"] * s + _tn(kw, vc_ref[cc * CHUNK:(cc + 1) * CHUNK, h * LANES:(h + 1) * LANES])
                S[h] = s

            s = S[h]
            sf_ref[0, 0, h] = s.astype(BF16)
            kw = jnp.where(hm, k_ref[...].astype(F32) * dv["kf"], 0.0).astype(BF16)
            S[h] = dv["gf"] * s + _tn(kw, v_ref[:, h * LANES:(h + 1) * LANES])

    return pl.pallas_call(
        body, name="ret_states_fwd", grid=(npairs, N),
        in_specs=[pl.BlockSpec(memory_space=pltpu.SMEM),
                  pl.BlockSpec((CHUNK, LANES), lambda p, n: (n, p)),
                  pl.BlockSpec((CHUNK, 2 * LANES), lambda p, n: (n, rv_blk + p)),
                  pl.BlockSpec((Lc, LANES), lambda p, n: (0, p)),
                  pl.BlockSpec((Lc, 2 * LANES), lambda p, n: (0, rv_blk + p))],
        out_specs=pl.BlockSpec((1, 1, 2, LANES, LANES), lambda p, n: (p, n, 0, 0, 0)),
        out_shape=jax.ShapeDtypeStruct((npairs, N, 2, LANES, LANES), BF16),
        scratch_shapes=[pltpu.VMEM((2, LANES, LANES), F32)],
        compiler_params=_params(2, False))(lg, Kr, P, Krc, Pc)


def _ret_chunk_fwd(q, k, v, sf, sb, hm, lam, mu):
    dv = _decay_vecs(lam, mu)
    Mk, rel, low = _decay_mask(lam, mu)
    qm = jnp.where(hm, q, jnp.zeros_like(q))
    qmf = qm.astype(F32)
    A = _nt(qm, k)
    Am = A * Mk
    Amb = Am.astype(BF16)
    Qf = (qmf * dv["qf"]).astype(BF16)
    Qb = (qmf * dv["qb"]).astype(BF16)
    O = _nn(Amb, v) + _nn(Qf, sf) + _nn(Qb, sb)
    return dict(dv=dv, Mk=Mk, rel=rel, low=low, qm=qm, Am=Am, Amb=Amb, Qf=Qf, Qb=Qb, O=O)


def _ret_out_fwd(Qr, Kr, P, Krc, Pc, SF, lg, rv_blk, rg_blk, npairs, d_mix):
    L = Qr.shape[0]
    Lc = Krc.shape[0]
    N, ncc = L // CHUNK, Lc // CHUNK

    def body(lg_ref, q_ref, k_ref, v_ref, g_ref, sf_ref, kc_ref, vc_ref, y_ref, sb_ref, S):
        p, n = pl.program_id(0), pl.program_id(1)
        masks = _head_masks()
        for h in range(2):
            lam = _lam_of(lg_ref, 0, 2 * p + h)
            mu = _lam_of(lg_ref, 1, 2 * p + h)
            hm = masks[h]
            dvb = _decay_vecs(lam, mu)

            @pl.when(n == 0)
            def _():
                s = jnp.zeros((LANES, LANES), F32)
                for cc in reversed(range(ncc)):
                    kw = jnp.where(hm, kc_ref[cc * CHUNK:(cc + 1) * CHUNK, :].astype(F32) * dvb["kb"], 0.0).astype(BF16)
                    s = dvb["gb"] * s + _tn(kw, vc_ref[cc * CHUNK:(cc + 1) * CHUNK, h * LANES:(h + 1) * LANES])
                S[h] = s

            s = S[h]
            sbb = s.astype(BF16)
            sb_ref[0, 0, h] = sbb
            v = v_ref[:, h * LANES:(h + 1) * LANES]
            f = _ret_chunk_fwd(q_ref[...], k_ref[...], v, sf_ref[0, 0, h], sbb, hm, lam, mu)
            O = f["O"]
            r = lax.rsqrt(jnp.mean(O * O, axis=-1, keepdims=True) + NORM_EPS)
            g = g_ref[:, h * LANES:(h + 1) * LANES].astype(F32)
            y_ref[:, h * LANES:(h + 1) * LANES] = (O * r * (g * _sigmoid(g))).astype(BF16)
            kw = jnp.where(hm, k_ref[...].astype(F32) * dvb["kb"], 0.0).astype(BF16)
            S[h] = dvb["gb"] * s + _tn(kw, v)

    rev = lambda n: N - 1 - n
    return pl.pallas_call(
        body, name="ret_out_fwd", grid=(npairs, N),
        in_specs=[pl.BlockSpec(memory_space=pltpu.SMEM),
                  pl.BlockSpec((CHUNK, LANES), lambda p, n: (rev(n), p)),
                  pl.BlockSpec((CHUNK, LANES), lambda p, n: (rev(n), p)),
                  pl.BlockSpec((CHUNK, 2 * LANES), lambda p, n: (rev(n), rv_blk + p)),
                  pl.BlockSpec((CHUNK, 2 * LANES), lambda p, n: (rev(n), rg_blk + p)),
                  pl.BlockSpec((1, 1, 2, LANES, LANES), lambda p, n: (p, rev(n), 0, 0, 0)),
                  pl.BlockSpec((Lc, LANES), lambda p, n: (0, p)),
                  pl.BlockSpec((Lc, 2 * LANES), lambda p, n: (0, rv_blk + p))],
        out_specs=[pl.BlockSpec((CHUNK, 2 * LANES), lambda p, n: (rev(n), p)),
                   pl.BlockSpec((1, 1, 2, LANES, LANES), lambda p, n: (p, rev(n), 0, 0, 0))],
        out_shape=[jax.ShapeDtypeStruct((L, d_mix), BF16),
                   jax.ShapeDtypeStruct((npairs, N, 2, LANES, LANES), BF16)],
        scratch_shapes=[pltpu.VMEM((2, LANES, LANES), F32)],
        compiler_params=_params(2, False))(lg, Qr, Kr, P, P, SF, Krc, Pc)


ACC_ROWS = 8


def _ret_bwd1(Qr, Kr, P, Krc, Pc, SF, SB, dY, lg, rv_blk, rg_blk, npairs, d_proj):
    L = Qr.shape[0]
    Lc = Krc.shape[0]
    N, ncc = L // CHUNK, Lc // CHUNK

    def body(lg_ref, q_ref, k_ref, v_ref, g_ref, sf_ref, sb_ref, dy_ref, kc_ref, vc_ref,
             dq_ref, dk_ref, dv_ref, dg_ref, do_ref, dkc_ref, dvc_ref, acc_ref, dS, T):
        p, n = pl.program_id(0), pl.program_id(1)
        masks = _head_masks()

        @pl.when(n == 0)
        def _():
            dS[...] = jnp.zeros_like(dS)
            T[...] = jnp.zeros_like(T)
            acc_ref[...] = jnp.zeros_like(acc_ref)

        dq_tot = jnp.zeros((CHUNK, LANES), F32)
        dk_tot = jnp.zeros((CHUNK, LANES), F32)
        for h in range(2):
            lam = _lam_of(lg_ref, 0, 2 * p + h)
            mu = _lam_of(lg_ref, 1, 2 * p + h)
            hm = masks[h]
            hs = slice(h * LANES, (h + 1) * LANES)
            v = v_ref[:, hs]
            k = k_ref[...]
            sf = sf_ref[0, 0, h]
            sb = sb_ref[0, 0, h]
            f = _ret_chunk_fwd(q_ref[...], k, v, sf, sb, hm, lam, mu)
            dv_, O = f["dv"], f["O"]
            r = lax.rsqrt(jnp.mean(O * O, axis=-1, keepdims=True) + NORM_EPS)
            on = O * r
            g = g_ref[:, hs].astype(F32)
            sg = _sigmoid(g)
            dy = dy_ref[:, hs].astype(F32)
            dg_ref[:, hs] = (dy * on * (sg * (1.0 + g * (1.0 - sg)))).astype(BF16)
            don = dy * (g * sg)
            dO = r * (don - on * jnp.mean(don * on, axis=-1, keepdims=True))
            dOb = dO.astype(BF16)
            do_ref[:, hs] = dOb
            dAm = _nt(dOb, v)
            T[h] += dAm * f["Am"]
            dAb = (dAm * f["Mk"]).astype(BF16)
            km = jnp.where(hm, k, jnp.zeros_like(k))
            dq = _nn(dAb, km)
            dk = _tn(dAb, f["qm"])
            dvh = _tn(f["Amb"], dOb)
            dQf = _nt(dOb, sf)
            dQb = _nt(dOb, sb)
            dq = dq + dQf * dv_["qf"] + dQb * dv_["qb"]
            acc_ref[0, h, 0:1, :] += _fsum(dQf * f["Qf"].astype(F32) * (dv_["i"] + 1.0))
            acc_ref[0, h, 1:2, :] += _fsum(dQb * f["Qb"].astype(F32) * (CHUNK - dv_["i"]))
            dSh = dS[h]
            dSb_ = dSh.astype(BF16)
            Kf = (km.astype(F32) * dv_["kf"]).astype(BF16)
            dKf = _nt(v, dSb_)
            dk = dk + jnp.where(hm, dKf * dv_["kf"], 0.0)
            acc_ref[0, h, 2:3, :] += _fsum(jnp.where(hm, dKf, 0.0) * Kf.astype(F32) * (CHUNK - 1.0 - dv_["i"]))
            dvh = dvh + _nn(Kf, dSb_)
            acc_ref[0, h, 3:4, :] += float(CHUNK) * dv_["gf"] * _fsum(dSh * sf.astype(F32))
            dSh = dv_["gf"] * dSh + _tn(f["Qf"], dOb)
            dS[h] = dSh
            dv_ref[:, hs] = dvh
            dq_tot = dq_tot + dq
            dk_tot = dk_tot + dk

        dq_ref[...] = dq_tot
        dk_ref[...] = dk_tot

        @pl.when(n == N - 1)
        def _():
            for h in range(2):
                lam = _lam_of(lg_ref, 0, 2 * p + h)
                dv_ = _decay_vecs(lam, lam)
                hm = masks[h]
                hs = slice(h * LANES, (h + 1) * LANES)
                states = [jnp.zeros((LANES, LANES), F32)]
                kws = []
                for cc in range(ncc):
                    kw = jnp.where(hm, kc_ref[cc * CHUNK:(cc + 1) * CHUNK, :].astype(F32) * dv_["kf"], 0.0).astype(BF16)
                    kws.append(kw)
                    states.append(dv_["gf"] * states[-1] + _tn(kw, vc_ref[cc * CHUNK:(cc + 1) * CHUNK, hs]))
                d = dS[h]
                for cc in reversed(range(ncc)):
                    db = d.astype(BF16)
                    rows = slice(cc * CHUNK, (cc + 1) * CHUNK)
                    dKf_c = jnp.where(hm, _nt(vc_ref[rows, hs], db), 0.0)
                    part = dKf_c * dv_["kf"]
                    if h == 0:
                        dkc_ref[rows, :] = part
                    else:
                        dkc_ref[rows, :] += part
                    acc_ref[0, h, 2:3, :] += _fsum(dKf_c * kws[cc].astype(F32) * (CHUNK - 1.0 - dv_["i"]))
                    dvc_ref[rows, hs] = _nn(kws[cc], db)
                    acc_ref[0, h, 3:4, :] += float(CHUNK) * dv_["gf"] * _fsum(d * states[cc])
                    d = dv_["gf"] * d
                _, rel, low = _decay_mask(lam, lam)
                Th = T[h]
                acc_ref[0, h, 4:5, :] += _fsum(jnp.where(low, Th * rel, 0.0))
                acc_ref[0, h, 5:6, :] += _fsum(jnp.where(low, 0.0, -Th * rel))

    rev = lambda n: N - 1 - n
    st = pl.BlockSpec((1, 1, 2, LANES, LANES), lambda p, n: (p, rev(n), 0, 0, 0))
    pair = pl.BlockSpec((CHUNK, LANES), lambda p, n: (rev(n), p))
    wide = lambda b0: pl.BlockSpec((CHUNK, 2 * LANES), lambda p, n: (rev(n), b0 + p))
    return pl.pallas_call(
        body, name="ret_bwd_desc", grid=(npairs, N),
        in_specs=[pl.BlockSpec(memory_space=pltpu.SMEM), pair, pair, wide(rv_blk), wide(rg_blk), st, st, wide(0),
                  pl.BlockSpec((Lc, LANES), lambda p, n: (0, p)),
                  pl.BlockSpec((Lc, 2 * LANES), lambda p, n: (0, rv_blk + p))],
        out_specs=[pair, pair, wide(0), wide(rg_blk), wide(0),
                   pl.BlockSpec((Lc, LANES), lambda p, n: (0, p)),
                   pl.BlockSpec((Lc, 2 * LANES), lambda p, n: (0, p)),
                   pl.BlockSpec((1, 2, ACC_ROWS, LANES), lambda p, n: (p, 0, 0, 0))],
        out_shape=[jax.ShapeDtypeStruct((L, npairs * LANES), F32),
                   jax.ShapeDtypeStruct((L, npairs * LANES), F32),
                   jax.ShapeDtypeStruct((L, npairs * 2 * LANES), F32),
                   jax.ShapeDtypeStruct((L, d_proj), BF16),
                   jax.ShapeDtypeStruct((L, npairs * 2 * LANES), BF16),
                   jax.ShapeDtypeStruct((Lc, npairs * LANES), F32),
                   jax.ShapeDtypeStruct((Lc, npairs * 2 * LANES), F32),
                   jax.ShapeDtypeStruct((npairs, 2, ACC_ROWS, LANES), F32)],
        scratch_shapes=[pltpu.VMEM((2, LANES, LANES), F32), pltpu.VMEM((2, CHUNK, CHUNK), F32)],
        compiler_params=_params(2, False))(lg, Qr, Kr, P, P, SF, SB, dY, Krc, Pc)


def _ret_bwd2(Qr, Kr, P, Krc, Pc, SB, dO, dKr, dVp, dP, dKc, dVc, lg, rv_blk, npairs):
    L = Qr.shape[0]
    Lc = Krc.shape[0]
    N, ncc = L // CHUNK, Lc // CHUNK

    def body(lg_ref, q_ref, k_ref, v_ref, sb_ref, do_ref, dkin_ref, dvin_ref, kc_ref, vc_ref, dkcin_ref, dvcin_ref,
             dpin_ref, dk_ref, dv_ref, dkc_ref, dvc_ref, acc_ref, dS):
        del dpin_ref
        p, n = pl.program_id(0), pl.program_id(1)
        masks = _head_masks()

        @pl.when(n == 0)
        def _():
            dS[...] = jnp.zeros_like(dS)
            acc_ref[...] = jnp.zeros_like(acc_ref)

        dk_tot = dkin_ref[...]
        for h in range(2):
            mu = _lam_of(lg_ref, 1, 2 * p + h)
            hm = masks[h]
            hs = slice(h * LANES, (h + 1) * LANES)
            dv_ = _decay_vecs(mu, mu)
            v = v_ref[:, hs]
            k = k_ref[...]
            q = q_ref[...]
            dOb = do_ref[:, hs]
            km = jnp.where(hm, k, jnp.zeros_like(k)).astype(F32)
            Kb = (km * dv_["kb"]).astype(BF16)
            Qb = (jnp.where(hm, q, jnp.zeros_like(q)).astype(F32) * dv_["qb"]).astype(BF16)
            dSh = dS[h]
            dSb_ = dSh.astype(BF16)
            dKb = jnp.where(hm, _nt(v, dSb_), 0.0)
            dk_tot = dk_tot + dKb * dv_["kb"]
            acc_ref[0, h, 0:1, :] += _fsum(dKb * Kb.astype(F32) * dv_["i"])
            dv_ref[:, hs] = (dvin_ref[:, hs] + _nn(Kb, dSb_)).astype(BF16)
            acc_ref[0, h, 1:2, :] += float(CHUNK) * dv_["gb"] * _fsum(dSh * sb_ref[0, 0, h].astype(F32))
            dSh = dv_["gb"] * dSh + _tn(Qb, dOb)
            dS[h] = dSh

        dk_ref[...] = dk_tot

        @pl.when(n == N - 1)
        def _():
            for h in range(2):
                mu = _lam_of(lg_ref, 1, 2 * p + h)
                hm = masks[h]
                hs = slice(h * LANES, (h + 1) * LANES)
                dv_ = _decay_vecs(mu, mu)
                states = {}
                kws = {}
                s = jnp.zeros((LANES, LANES), F32)
                for cc in reversed(range(ncc)):
                    states[cc] = s
                    kw = jnp.where(hm, kc_ref[cc * CHUNK:(cc + 1) * CHUNK, :].astype(F32) * dv_["kb"], 0.0).astype(BF16)
                    kws[cc] = kw
                    s = dv_["gb"] * s + _tn(kw, vc_ref[cc * CHUNK:(cc + 1) * CHUNK, hs])
                d = dS[h]
                for cc in range(ncc):
                    db = d.astype(BF16)
                    rows = slice(cc * CHUNK, (cc + 1) * CHUNK)
                    dKb_c = jnp.where(hm, _nt(vc_ref[rows, hs], db), 0.0)
                    part = dKb_c * dv_["kb"]
                    if h == 0:
                        dkc_ref[rows, :] = dkcin_ref[rows, :] + part
                    else:
                        dkc_ref[rows, :] += part
                    acc_ref[0, h, 0:1, :] += _fsum(dKb_c * kws[cc].astype(F32) * dv_["i"])
                    dvc_ref[rows, hs] = dvcin_ref[rows, hs] + _nn(kws[cc], db)
                    acc_ref[0, h, 1:2, :] += float(CHUNK) * dv_["gb"] * _fsum(d * states[cc])
                    d = dv_["gb"] * d

    st = pl.BlockSpec((1, 1, 2, LANES, LANES), lambda p, n: (p, n, 0, 0, 0))
    pair = pl.BlockSpec((CHUNK, LANES), lambda p, n: (n, p))
    wide = lambda b0: pl.BlockSpec((CHUNK, 2 * LANES), lambda p, n: (n, b0 + p))
    ckc = pl.BlockSpec((Lc, LANES), lambda p, n: (0, p))
    cvc = lambda b0: pl.BlockSpec((Lc, 2 * LANES), lambda p, n: (0, b0 + p))
    return pl.pallas_call(
        body, name="ret_bwd_asc", grid=(npairs, N),
        in_specs=[pl.BlockSpec(memory_space=pltpu.SMEM), pair, pair, wide(rv_blk), st, wide(0), pair, wide(0),
                  ckc, cvc(rv_blk), ckc, cvc(0), pl.BlockSpec(memory_space=pl.ANY)],
        out_specs=[pair, wide(rv_blk), ckc, cvc(0),
                   pl.BlockSpec((1, 2, ACC_ROWS, LANES), lambda p, n: (p, 0, 0, 0))],
        out_shape=[jax.ShapeDtypeStruct(dKr.shape, F32),
                   jax.ShapeDtypeStruct(dP.shape, dP.dtype),
                   jax.ShapeDtypeStruct(dKc.shape, F32),
                   jax.ShapeDtypeStruct(dVc.shape, F32),
                   jax.ShapeDtypeStruct((npairs, 2, ACC_ROWS, LANES), F32)],
        input_output_aliases={12: 1},
        scratch_shapes=[pltpu.VMEM((2, LANES, LANES), F32)],
        compiler_params=_params(2, False))(lg, Qr, Kr, P, SB, dO, dKr, dVp, Krc, Pc, dKc, dVc, dP)


def _att_valid(n, N):
    row = lax.broadcasted_iota(jnp.int32, (CHUNK, 3 * CHUNK), 0)
    col = lax.broadcasted_iota(jnp.int32, (CHUNK, 3 * CHUNK), 1)
    ok = jnp.logical_and(col >= row, col <= row + 2 * CHUNK)
    ok = jnp.logical_and(ok, jnp.logical_or(col >= CHUNK, n > 0))
    ok = jnp.logical_and(ok, jnp.logical_or(col < 2 * CHUNK, n < N - 1))
    return ok


def _att_probs(qa, K, Kc, valid, snk):
    s_loc = jnp.where(valid, _nt(qa, K), NEG)
    s_ctx = _nt(qa, Kc)
    mx = jnp.maximum(jnp.maximum(jnp.max(s_loc, axis=1, keepdims=True), jnp.max(s_ctx, axis=1, keepdims=True)), snk)
    p_loc = jnp.exp(s_loc - mx)
    p_ctx = jnp.exp(s_ctx - mx)
    p_snk = jnp.exp(snk - mx)
    inv = 1.0 / (jnp.sum(p_loc, axis=1, keepdims=True) + jnp.sum(p_ctx, axis=1, keepdims=True) + p_snk)
    return p_loc, p_ctx, p_snk, inv


def _att_specs(Lc, N):
    q = pl.BlockSpec((CHUNK, 2 * LANES), lambda g, n: (n, g))
    kv = lambda s: pl.BlockSpec((1, CHUNK, LANES), lambda g, n: (g, jnp.clip(n + s, 0, N - 1), 0))
    ctx = pl.BlockSpec((1, Lc, LANES), lambda g, n: (g, 0, 0))
    return q, kv, ctx


def _att_fwd(Qa, Kd, Vd, Kdc, Vdc, sink, Y, blk0):
    L = Qa.shape[0]
    Lc = Kdc.shape[1]
    N = L // CHUNK
    nkv = Kd.shape[0]

    def body(sink_ref, q_ref, kp, kc_, kn, vp, vc_, vn, kctx, vctx, y_in, o_ref):
        del y_in
        g, n = pl.program_id(0), pl.program_id(1)
        K = jnp.concatenate([kp[0], kc_[0], kn[0]], axis=0)
        V = jnp.concatenate([vp[0], vc_[0], vn[0]], axis=0)
        Kc, Vc = kctx[0], vctx[0]
        valid = _att_valid(n, N)
        masks = _head_masks()
        for pr in range(2):
            qp = q_ref[:, pr * LANES:(pr + 1) * LANES]
            acc = jnp.zeros((CHUNK, LANES), F32)
            for a in range(2):
                hm = masks[a]
                qa = jnp.where(hm, qp, jnp.zeros_like(qp))
                snk = jnp.full((1, 1), sink_ref[0, g * 4 + pr * 2 + a], F32)
                p_loc, p_ctx, _, inv = _att_probs(qa, K, Kc, valid, snk)
                Va = jnp.where(hm, V, jnp.zeros_like(V))
                Vca = jnp.where(hm, Vc, jnp.zeros_like(Vc))
                acc = acc + (_nn(p_loc.astype(BF16), Va) + _nn(p_ctx.astype(BF16), Vca)) * inv
            o_ref[:, pr * LANES:(pr + 1) * LANES] = acc.astype(BF16)

    q, kv, ctx = _att_specs(Lc, N)
    return pl.pallas_call(
        body, name="att_fwd", grid=(nkv, N),
        in_specs=[pl.BlockSpec(memory_space=pltpu.SMEM), q, kv(-1), kv(0), kv(1), kv(-1), kv(0), kv(1), ctx, ctx,
                  pl.BlockSpec(memory_space=pl.ANY)],
        out_specs=pl.BlockSpec((CHUNK, 2 * LANES), lambda g, n: (n, blk0 + g)),
        out_shape=jax.ShapeDtypeStruct(Y.shape, Y.dtype),
        input_output_aliases={10: 0},
        compiler_params=_params(2, False))(sink, Qa, Kd, Kd, Kd, Vd, Vd, Vd, Kdc, Vdc, Y)


def _att_bwd(Qa, Kd, Vd, Kdc, Vdc, sink, dY, blk0):
    L = Qa.shape[0]
    Lc = Kdc.shape[1]
    N = L // CHUNK
    nkv = Kd.shape[0]

    def body(sink_ref, q_ref, kp, kc_, kn, vp, vc_, vn, kctx, vctx, dy_ref,
             dq_ref, dkp, dkc_, dkn, dvp, dvc_, dvn, dkctx, dvctx, dsink_ref):
        g, n = pl.program_id(0), pl.program_id(1)

        @pl.when(n == 0)
        def _():
            dkctx[...] = jnp.zeros_like(dkctx)
            dvctx[...] = jnp.zeros_like(dvctx)
            dsink_ref[...] = jnp.zeros_like(dsink_ref)

        K = jnp.concatenate([kp[0], kc_[0], kn[0]], axis=0)
        V = jnp.concatenate([vp[0], vc_[0], vn[0]], axis=0)
        Kc, Vc = kctx[0], vctx[0]
        valid = _att_valid(n, N)
        masks = _head_masks()
        dK = jnp.zeros((3 * CHUNK, LANES), F32)
        dV = jnp.zeros((3 * CHUNK, LANES), F32)
        dKc = jnp.zeros((Lc, LANES), F32)
        dVc = jnp.zeros((Lc, LANES), F32)
        for pr in range(2):
            qp = q_ref[:, pr * LANES:(pr + 1) * LANES]
            dyp = dy_ref[:, pr * LANES:(pr + 1) * LANES]
            dq = jnp.zeros((CHUNK, LANES), F32)
            for a in range(2):
                hm = masks[a]
                qa = jnp.where(hm, qp, jnp.zeros_like(qp))
                do = jnp.where(hm, dyp, jnp.zeros_like(dyp))
                snk = jnp.full((1, 1), sink_ref[0, g * 4 + pr * 2 + a], F32)
                p_loc, p_ctx, p_snk, inv = _att_probs(qa, K, Kc, valid, snk)
                P_loc, P_ctx = p_loc * inv, p_ctx * inv
                dp_loc = _nt(do, V)
                dp_ctx = _nt(do, Vc)
                delta = jnp.sum(P_loc * dp_loc, axis=1, keepdims=True) + jnp.sum(P_ctx * dp_ctx, axis=1, keepdims=True)
                ds_loc = (P_loc * (dp_loc - delta)).astype(BF16)
                ds_ctx = (P_ctx * (dp_ctx - delta)).astype(BF16)
                row = pr * 2 + a
                dsink_ref[0, row:row + 1, :] += _fsum(-(p_snk * inv) * delta)
                dq = dq + jnp.where(hm, _nn(ds_loc, K) + _nn(ds_ctx, Kc), 0.0)
                dK = dK + _tn(ds_loc, qa)
                dKc = dKc + _tn(ds_ctx, qa)
                dV = dV + _tn(P_loc.astype(BF16), do)
                dVc = dVc + _tn(P_ctx.astype(BF16), do)
            dq_ref[:, pr * LANES:(pr + 1) * LANES] = dq
        for j, (rk, rv) in enumerate([(dkp, dvp), (dkc_, dvc_), (dkn, dvn)]):
            rk[0] = dK[j * CHUNK:(j + 1) * CHUNK].astype(BF16)
            rv[0] = dV[j * CHUNK:(j + 1) * CHUNK].astype(BF16)
        dkctx[0] += dKc
        dvctx[0] += dVc

    q, kv, ctx = _att_specs(Lc, N)
    blk = pl.BlockSpec((1, CHUNK, LANES), lambda g, n: (g, n, 0))
    part = jax.ShapeDtypeStruct((nkv, L, LANES), BF16)
    cshape = jax.ShapeDtypeStruct((nkv, Lc, LANES), F32)
    return pl.pallas_call(
        body, name="att_bwd", grid=(nkv, N),
        in_specs=[pl.BlockSpec(memory_space=pltpu.SMEM), q, kv(-1), kv(0), kv(1), kv(-1), kv(0), kv(1), ctx, ctx,
                  pl.BlockSpec((CHUNK, 2 * LANES), lambda g, n: (n, blk0 + g))],
        out_specs=[q, blk, blk, blk, blk, blk, blk, ctx, ctx,
                   pl.BlockSpec((1, 8, LANES), lambda g, n: (g, 0, 0))],
        out_shape=[jax.ShapeDtypeStruct(Qa.shape, F32), part, part, part, part, part, part, cshape, cshape,
                   jax.ShapeDtypeStruct((nkv, 8, LANES), F32)],
        compiler_params=_params(2, False))(sink, Qa, Kd, Kd, Kd, Vd, Vd, Vd, Kdc, Vdc, dY)


def _bwd_proj(dx, gt, w, saved, G=None, U=None, name="bwd_proj"):
    M, D = dx.shape
    N = w.shape[0]
    tm, tn = _tile(M, 512, 8), _tile(N, 512)
    swiglu = G is not None

    def body(*refs):
        if swiglu:
            dx_ref, gt_ref, w_ref, sv_ref, G_ref, U_ref, dG_ref, dU_ref, dz_ref, dgt_ref, zs = refs
        else:
            dx_ref, gt_ref, w_ref, sv_ref, dA_ref, dz_ref, dgt_ref, zs = refs
        i, j = pl.program_id(0), pl.program_id(1)

        @pl.when(jnp.logical_and(i == 0, j == 0))
        def _():
            dgt_ref[...] = jnp.zeros_like(dgt_ref)

        @pl.when(j == 0)
        def _():
            d = dx_ref[...]
            z = (d * gt_ref[...]).astype(BF16)
            zs[...] = z
            dz_ref[...] = z
            dgt_ref[...] += jnp.sum(d * sv_ref[...].astype(F32), axis=0, keepdims=True)

        dA = _nt(zs[...], w_ref[...])
        if swiglu:
            Gv = G_ref[...].astype(F32)
            Uv = U_ref[...].astype(F32)
            sg = _sigmoid(Gv)
            dU_ref[...] = (dA * Gv * sg).astype(BF16)
            dG_ref[...] = (dA * Uv * (sg * (1.0 + Gv * (1.0 - sg)))).astype(BF16)
        else:
            dA_ref[...] = dA.astype(BF16)

    row = pl.BlockSpec((tm, D), lambda i, j: (i, 0))
    vec = pl.BlockSpec((1, D), lambda i, j: (0, 0))
    tile = pl.BlockSpec((tm, tn), lambda i, j: (i, j))
    big = jax.ShapeDtypeStruct((M, N), BF16)
    in_specs = [row, vec, pl.BlockSpec((tn, D), lambda i, j: (j, 0)), row]
    args = [dx, gt, w, saved]
    if swiglu:
        in_specs += [tile, tile]
        args += [G, U]
        out_specs = [tile, tile, row, vec]
        out_shape = [big, big, jax.ShapeDtypeStruct((M, D), BF16), jax.ShapeDtypeStruct((1, D), F32)]
    else:
        out_specs = [tile, row, vec]
        out_shape = [big, jax.ShapeDtypeStruct((M, D), BF16), jax.ShapeDtypeStruct((1, D), F32)]
    return pl.pallas_call(
        body, name=name, grid=(M // tm, N // tn), in_specs=in_specs, out_specs=out_specs, out_shape=out_shape,
        scratch_shapes=[pltpu.VMEM((tm, D), BF16)], compiler_params=_params(2))(*args)


def _tn_matmul(pairs, name):
    Ka, Nb = pairs[0][0].shape[1], pairs[0][1].shape[1]
    tk, tn = _tile(Ka, 1024), _tile(Nb, 1536)
    tls, nks = [], []
    for a, _ in pairs:
        tl = _tile(a.shape[0], 512, 8)
        tls.append(tl)
        nks.append(a.shape[0] // tl)
    starts = [int(s) for s in np.cumsum([0] + nks[:-1])]
    nk = int(sum(nks))

    def body(*refs):
        out_ref, acc = refs[-2], refs[-1]
        k = pl.program_id(2)

        @pl.when(k == 0)
        def _():
            acc[...] = jnp.zeros_like(acc)

        for idx in range(len(pairs)):
            a_ref, b_ref = refs[2 * idx], refs[2 * idx + 1]

            @pl.when(jnp.logical_and(k >= starts[idx], k < starts[idx] + nks[idx]))
            def _():
                acc[...] += _tn(a_ref[...], b_ref[...])

        @pl.when(k == nk - 1)
        def _():
            out_ref[...] = acc[...].astype(BF16)

    in_specs, args = [], []
    for idx, (a, b) in enumerate(pairs):
        s0, n_ = starts[idx], nks[idx]
        in_specs.append(pl.BlockSpec((tls[idx], tk), lambda i, j, k, s0=s0, n_=n_: (jnp.clip(k - s0, 0, n_ - 1), i)))
        in_specs.append(pl.BlockSpec((tls[idx], tn), lambda i, j, k, s0=s0, n_=n_: (jnp.clip(k - s0, 0, n_ - 1), j)))
        args += [a, b]
    return pl.pallas_call(
        body, name=name, grid=(Ka // tk, Nb // tn, nk), in_specs=in_specs,
        out_specs=pl.BlockSpec((tk, tn), lambda i, j, k: (i, j)),
        out_shape=jax.ShapeDtypeStruct((Ka, Nb), BF16),
        scratch_shapes=[pltpu.VMEM((tk, tn), F32)], compiler_params=_params(3))(*args)


def _bwd_norm_mod(pairs, x, dres, g, sh, sc, name):
    M, D = x.shape
    K = pairs[0][0].shape[1]
    tm, tk = _tile(M, 512, 8), _tile(K, 512)
    nk = K // tk
    npair = len(pairs)
    has_res = dres is not None

    def body(*refs):
        pr = refs[:2 * npair]
        rest = refs[2 * npair:]
        if has_res:
            x_ref, dres_ref, g_ref, sh_ref, sc_ref, dx_ref, st_ref, acc = rest
        else:
            x_ref, g_ref, sh_ref, sc_ref, dx_ref, st_ref, acc = rest
        del sh_ref
        i, k = pl.program_id(0), pl.program_id(1)

        @pl.when(jnp.logical_and(i == 0, k == 0))
        def _():
            st_ref[...] = jnp.zeros_like(st_ref)

        @pl.when(k == 0)
        def _():
            acc[...] = jnp.zeros_like(acc)

        for idx in range(npair):
            acc[...] += _nt(pr[2 * idx][...], pr[2 * idx + 1][...])

        @pl.when(k == nk - 1)
        def _():
            xv = x_ref[...]
            gv = g_ref[...]
            dh = acc[...]
            r = lax.rsqrt(jnp.mean(xv * xv, axis=-1, keepdims=True) + NORM_EPS)
            xh = xv * r
            st_ref[0:1, :] += jnp.sum(dh, axis=0, keepdims=True)
            st_ref[1:2, :] += jnp.sum(dh * (xh * gv), axis=0, keepdims=True)
            dn = dh * (1.0 + sc_ref[...])
            st_ref[2:3, :] += jnp.sum(dn * xh, axis=0, keepdims=True)
            dxh = dn * gv
            d = r * (dxh - xh * jnp.mean(dxh * xh, axis=-1, keepdims=True))
            if has_res:
                d = d + dres_ref[...]
            dx_ref[...] = d

    row = pl.BlockSpec((tm, D), lambda i, k: (i, 0))
    vec = pl.BlockSpec((1, D), lambda i, k: (0, 0))
    in_specs, args = [], []
    for dA, w in pairs:
        in_specs += [pl.BlockSpec((tm, tk), lambda i, k: (i, k)), pl.BlockSpec((D, tk), lambda i, k: (0, k))]
        args += [dA, w]
    in_specs += [row] + ([row] if has_res else []) + [vec, vec, vec]
    args += [x] + ([dres] if has_res else []) + [g, sh, sc]
    return pl.pallas_call(
        body, name=name, grid=(M // tm, nk), in_specs=in_specs,
        out_specs=[row, pl.BlockSpec((8, D), lambda i, k: (0, 0))],
        out_shape=[jax.ShapeDtypeStruct((M, D), F32), jax.ShapeDtypeStruct((8, D), F32)],
        scratch_shapes=[pltpu.VMEM((tm, D), F32)], compiler_params=_params(2))(*args)


def _local_step(x, ctx, tgt, mod, modc, norm_mix, norm_ffn, norm_final, lg, sink, w_in, w_out, w_gate, w_up, w_down):
    L, D = x.shape
    Lc = ctx.shape[0]
    d_proj = w_in.shape[1]
    npairs = RET_HEADS // 2
    nkv = ATT_KV_HEADS
    nkvp = nkv // 2
    o_rq = 0
    o_rk = o_rq + RET_HEADS * RET_DK // LANES
    o_rv = o_rk + RET_HEADS * RET_DK // LANES
    o_rg = o_rv + RET_HEADS * RET_DV // LANES
    o_aq = o_rg + RET_HEADS * RET_DV // LANES
    o_ak = o_aq + ATT_HEADS * ATT_DH // LANES
    o_av = o_ak + nkv * ATT_DH // LANES
    assert (o_av + nkv * ATT_DH // LANES) * LANES == d_proj
    assert o_rv % 2 == 0 and o_rg % 2 == 0 and (RET_HEADS * RET_DV) % (2 * LANES) == 0
    rv_blk, rg_blk = o_rv // 2, o_rg // 2
    d_ret = RET_HEADS * RET_DV
    d_mix = d_ret + ATT_HEADS * ATT_DH
    att_blk = d_ret // (2 * LANES)
    k_scale = RET_DK ** -0.5
    a_scale = ATT_DH ** -0.5

    T = _rope_tables(L)
    Tc = dict(C=jnp.ones((Lc, LANES), F32), S=jnp.zeros((Lc, LANES), F32))
    row = lambda m, i: m[i:i + 1]
    sh_m, sc_m, gt_m, sh_f, sc_f, gt_f = [row(mod, i) for i in range(6)]
    sh_mc, sc_mc = row(modc, 0), row(modc, 1)

    P, hx = _norm_mod_matmul(x, norm_mix, sh_m, sc_m, w_in, "in_proj")
    Pc, hc = _norm_mod_matmul(ctx, norm_mix, sh_mc, sc_mc, w_in, "in_proj_ctx")
    nq = RET_HEADS * RET_DK // LANES
    Qr = _rope_cols(P, o_rq, nq, T["Cr"], T["Sr"], T["Rr"], 1.0, True, "rope_rq")
    Kr = _rope_cols(P, o_rk, nq, T["Cr"], T["Sr"], T["Rr"], k_scale, True, "rope_rk")
    Krc = _rope_cols(Pc, o_rk, nq, Tc["C"], Tc["S"], T["Rr"], k_scale, False, "scale_rk_ctx")
    Qa = _rope_cols(P, o_aq, ATT_HEADS * ATT_DH // LANES, T["Ca"], T["Sa"], T["Ra"], a_scale, True, "rope_aq")
    Kd = _dup_heads(P, o_ak, nkvp, T["Ca"], T["Sa"], T["Ra"], T["D0"], T["D1"], True, "dup_ak")
    Vd = _dup_heads(P, o_av, nkvp, T["Ca"], T["Sa"], T["Ra"], T["D0"], T["D1"], False, "dup_av")
    Kdc = _dup_heads(Pc, o_ak, nkvp, Tc["C"], Tc["S"], T["Ra"], T["D0"], T["D1"], False, "dup_ak_ctx")
    Vdc = _dup_heads(Pc, o_av, nkvp, Tc["C"], Tc["S"], T["Ra"], T["D0"], T["D1"], False, "dup_av_ctx")

    SF = _ret_states_fwd(Kr, P, Krc, Pc, lg, rv_blk, npairs)
    Y, SB = _ret_out_fwd(Qr, Kr, P, Krc, Pc, SF, lg, rv_blk, rg_blk, npairs, d_mix)
    Y = _att_fwd(Qa, Kd, Vd, Kdc, Vdc, sink, Y, att_blk)

    x1, O1 = _proj_residual(Y, w_out, x, gt_m, "out_proj")
    G, U, A, h2 = _ffn_in(x1, norm_ffn, sh_f, sc_f, w_gate, w_up)
    x2, Fo = _proj_residual(A, w_down, x1, gt_f, "ffn_out")
    dx2, loss, d_norm_final = _final(x2, norm_final, tgt)

    dG, dU, dz2, dgt_f = _bwd_proj(dx2, gt_f, w_down, Fo, G, U, name="ffn_out_bwd")
    g_w_down = _tn_matmul([(A, dz2)], "grad_w_down")
    dx1, st_f = _bwd_norm_mod([(dG, w_gate), (dU, w_up)], x1, dx2, norm_ffn, sh_f, sc_f, "ffn_in_bwd")
    g_w_gate = _tn_matmul([(h2, dG)], "grad_w_gate")
    g_w_up = _tn_matmul([(h2, dU)], "grad_w_up")
    dY, dz1, dgt_m = _bwd_proj(dx1, gt_m, w_out, O1, name="out_proj_bwd")
    g_w_out = _tn_matmul([(Y, dz1)], "grad_w_out")

    dQa, dKp, dKs, dKn, dVp, dVs, dVn, dKdc, dVdc, dsink = _att_bwd(Qa, Kd, Vd, Kdc, Vdc, sink, dY, att_blk)
    dQr, dKr, dVr, dP, dO, dKc, dVc, acc1 = _ret_bwd1(Qr, Kr, P, Krc, Pc, SF, SB, dY, lg, rv_blk, rg_blk, npairs, d_proj)
    dKr, dP, dKc, dVc, acc2 = _ret_bwd2(Qr, Kr, P, Krc, Pc, SB, dO, dKr, dVr, dP, dKc, dVc, lg, rv_blk, npairs)

    dP = _unrope_cols(dQr, dP, o_rq, nq, T["Cr"], T["Sr"], T["RrT"], 1.0, True, "unrope_rq")
    dP = _unrope_cols(dKr, dP, o_rk, nq, T["Cr"], T["Sr"], T["RrT"], k_scale, True, "unrope_rk")
    dP = _unrope_cols(dQa, dP, o_aq, ATT_HEADS * ATT_DH // LANES, T["Ca"], T["Sa"], T["RaT"], a_scale, True, "unrope_aq")
    dP = _fold_heads([(dKs, 0), (dKp, 1), (dKn, -1)], dP, o_ak, nkvp, T["Ca"], T["Sa"], T["RaT"], T["D0T"], T["D1T"],
                     True, "fold_ak")
    dP = _fold_heads([(dVs, 0), (dVp, 1), (dVn, -1)], dP, o_av, nkvp, T["Ca"], T["Sa"], T["RaT"], T["D0T"], T["D1T"],
                     False, "fold_av")
    dPc = jnp.zeros((Lc, d_proj), BF16)
    dPc = _unrope_cols(dKc, dPc, o_rk, nq, Tc["C"], Tc["S"], T["RrT"], k_scale, False, "ctx_rk_bwd")
    dPc = _unrope_cols(dVc, dPc, o_rv, RET_HEADS * RET_DV // LANES, Tc["C"], Tc["S"], T["RrT"], 1.0, False, "ctx_rv_bwd")
    dPc = _fold_heads([(dKdc.astype(BF16), 0)], dPc, o_ak, nkvp, Tc["C"], Tc["S"], T["RaT"], T["D0T"], T["D1T"],
                      False, "fold_ak_ctx")
    dPc = _fold_heads([(dVdc.astype(BF16), 0)], dPc, o_av, nkvp, Tc["C"], Tc["S"], T["RaT"], T["D0T"], T["D1T"],
                      False, "fold_av_ctx")

    dx, st_m = _bwd_norm_mod([(dP, w_in)], x, dx1, norm_mix, sh_m, sc_m, "in_proj_bwd")
    _, st_mc = _bwd_norm_mod([(dPc, w_in)], ctx, None, norm_mix, sh_mc, sc_mc, "in_proj_ctx_bwd")
    g_w_in = _tn_matmul([(hx, dP), (hc, dPc)], "grad_w_in")

    a1 = acc1[:, :, :, 0].reshape(RET_HEADS, ACC_ROWS)
    a2 = acc2[:, :, :, 0].reshape(RET_HEADS, ACC_ROWS)
    dlam = a1[:, 0] + a1[:, 2] + a1[:, 3] + a1[:, 4]
    dmu = a1[:, 1] + a1[:, 5] + a2[:, 0] + a2[:, 1]
    d_ret_decay = jnp.stack([dlam, dmu]) * lg
    d_sink = dsink[:, :4, 0].reshape(1, ATT_HEADS)

    zero = jnp.zeros((1, D), F32)
    dmod = jnp.concatenate([st_m[0:1], st_m[1:2], dgt_m, st_f[0:1], st_f[1:2], dgt_f], axis=0)
    dmodc = jnp.concatenate([st_mc[0:1], st_mc[1:2], zero, zero, zero, zero], axis=0)
    return dict(loss=loss[0, 0], grad_x=dx, dmod=dmod, dmodc=dmodc,
                d_norm_mix=st_m[2:3] + st_mc[2:3], d_norm_ffn=st_f[2:3], d_norm_final=d_norm_final,
                d_ret_decay=d_ret_decay, d_sink=d_sink,
                g_w_in=g_w_in, g_w_out=g_w_out, g_w_gate=g_w_gate, g_w_up=g_w_up, g_w_down=g_w_down)


def _my_pos():
    return lax.axis_index("x"), lax.axis_index("y"), lax.axis_index("c")


def _other_chips(x, y):
    return [(1 - x, y), (x, 1 - y), (1 - x, 1 - y)]


def _remote(src, dst, ssem, rsem, dev):
    return pltpu.make_async_remote_copy(src_ref=src, dst_ref=dst, send_sem=ssem, recv_sem=rsem,
                                        device_id=dev, device_id_type=MESH)


def _allgather8(v, name):
    R, Cc = v.shape

    def body(v_ref, out_ref, send_sems, recv_sems):
        x, y, c = _my_pos()
        me = 4 * x + 2 * y + c
        out_ref[pl.ds(me, 1)] = v_ref[...][None]
        peers = []
        for j in range(1, N_DEV):
            peers.append((1 - x if (j >> 2) & 1 else x, 1 - y if (j >> 1) & 1 else y, 1 - c if j & 1 else c))
        copies = []
        for j, peer in enumerate(peers):
            cp = _remote(v_ref, out_ref.at[me], send_sems.at[j], recv_sems.at[j], peer)
            cp.start()
            copies.append(cp)
        for j, peer in enumerate(peers):
            pid = 4 * peer[0] + 2 * peer[1] + peer[2]
            _remote(v_ref, out_ref.at[pid], send_sems.at[j], recv_sems.at[j], peer).wait_recv()
        for cp in copies:
            cp.wait_send()

    return pl.pallas_call(
        body, name=name, out_shape=jax.ShapeDtypeStruct((N_DEV, R, Cc), v.dtype),
        in_specs=[pl.BlockSpec(memory_space=pltpu.VMEM)], out_specs=pl.BlockSpec(memory_space=pltpu.VMEM),
        scratch_shapes=[pltpu.SemaphoreType.DMA((N_DEV - 1,)), pltpu.SemaphoreType.DMA((N_DEV - 1,))])(v)


def _region(ref, k, half, shard_shape, axis):
    r, cs = shard_shape
    hr = r // 2
    if axis == 1:
        return ref.at[pl.ds(pl.multiple_of(half * hr, 16), hr), pl.ds(pl.multiple_of(k * cs, LANES), cs)]
    return ref.at[pl.ds(pl.multiple_of(k * r + half * hr, 16), hr), :]


def _full_shape(shard_shape, axis):
    r, cs = shard_shape
    return (r, N_CHIPS * cs) if axis == 1 else (N_CHIPS * r, cs)


def _allgather_weights(fulls, shapes, axes):
    nw = len(fulls)

    def body(*refs):
        out_refs = refs[nw:2 * nw]
        send1, recv1, send2, recv2 = refs[2 * nw:]
        x, y, c = _my_pos()
        k0 = 2 * x + y
        chips = _other_chips(x, y)
        sends = []
        for w in range(nw):
            own = _region(out_refs[w], k0, c, shapes[w], axes[w])
            for j, ch in enumerate(chips):
                cp = _remote(own, own, send1.at[w, j], recv1.at[w, j], (ch[0], ch[1], c))
                cp.start()
                sends.append(cp)
        for w in range(nw):
            for j, ch in enumerate(chips):
                kj = 2 * ch[0] + ch[1]
                got = _region(out_refs[w], kj, c, shapes[w], axes[w])
                _remote(got, got, send1.at[w, j], recv1.at[w, j], (ch[0], ch[1], c)).wait_recv()
                fw = _remote(got, got, send2.at[w, j], recv2.at[w, j], (x, y, 1 - c))
                fw.start()
                sends.append(fw)
        for w in range(nw):
            for j, ch in enumerate(chips):
                kj = 2 * ch[0] + ch[1]
                got = _region(out_refs[w], kj, 1 - c, shapes[w], axes[w])
                _remote(got, got, send2.at[w, j], recv2.at[w, j], (x, y, 1 - c)).wait_recv()
        for cp in sends:
            cp.wait_send()

    anyspec = pl.BlockSpec(memory_space=pl.ANY)
    sem = lambda: pltpu.SemaphoreType.DMA((nw, 3))
    return pl.pallas_call(
        body, name="allgather_weights",
        out_shape=[jax.ShapeDtypeStruct(f.shape, BF16) for f in fulls],
        in_specs=[anyspec] * nw, out_specs=[anyspec] * nw,
        input_output_aliases={w: w for w in range(nw)},
        scratch_shapes=[sem(), sem(), sem(), sem()])(*fulls)


def _half_pieces(ref, half, shard_shape, axis):
    r, cs = shard_shape
    hr = r // 2
    if axis == 1:
        return [ref.at[pl.ds(pl.multiple_of(half * hr, 16), hr), :]]
    return [ref.at[pl.ds(pl.multiple_of(k * r + half * hr, 16), hr), :] for k in range(N_CHIPS)]


def _rs_sibling(grads, shapes, axes):
    nw = len(grads)
    npc = max(1 if a == 1 else N_CHIPS for a in axes)

    def body(*refs):
        g_refs, out_refs = refs[:nw], refs[nw:2 * nw]
        send, recv = refs[2 * nw:]
        x, y, c = _my_pos()
        sib = (x, y, 1 - c)
        copies = []
        for w in range(nw):
            src = _half_pieces(g_refs[w], 1 - c, shapes[w], axes[w])
            dst = _half_pieces(out_refs[w], 1 - c, shapes[w], axes[w])
            for i, (s, d) in enumerate(zip(src, dst)):
                cp = _remote(s, d, send.at[w, i], recv.at[w, i], sib)
                cp.start()
                copies.append(cp)
        for w in range(nw):
            mine = _half_pieces(out_refs[w], c, shapes[w], axes[w])
            for i, d in enumerate(mine):
                _remote(d, d, send.at[w, i], recv.at[w, i], sib).wait_recv()
        for cp in copies:
            cp.wait_send()

    anyspec = pl.BlockSpec(memory_space=pl.ANY)
    return pl.pallas_call(
        body, name="rs_sibling",
        out_shape=[jax.ShapeDtypeStruct(_full_shape(s, a), BF16) for s, a in zip(shapes, axes)],
        in_specs=[anyspec] * nw, out_specs=[anyspec] * nw,
        scratch_shapes=[pltpu.SemaphoreType.DMA((nw, npc)), pltpu.SemaphoreType.DMA((nw, npc))])(*grads)


def _half_block_spec(shard_shape, axis, tr):
    r, cs = shard_shape
    hr = r // 2
    if axis == 1:
        return pl.BlockSpec((tr, cs), lambda k, i, c_ref: (c_ref[0] * (hr // tr) + i, k))
    return pl.BlockSpec((tr, cs), lambda k, i, c_ref: (k * (r // tr) + c_ref[0] * (hr // tr) + i, 0))


def _add_halves(g, recv, cvec, shard_shape, axis, name):
    r, cs = shard_shape
    hr = r // 2
    tr = _tile(hr, 256, 16)

    def body(c_ref, a_ref, b_ref, o_ref):
        del c_ref
        o_ref[0] = (a_ref[...].astype(F32) + b_ref[...].astype(F32)).astype(BF16)

    spec = _half_block_spec(shard_shape, axis, tr)
    return pl.pallas_call(
        body, name=name,
        grid_spec=pltpu.PrefetchScalarGridSpec(
            num_scalar_prefetch=1, grid=(N_CHIPS, hr // tr), in_specs=[spec, spec],
            out_specs=pl.BlockSpec((1, tr, cs), lambda k, i, c_ref: (k, i, 0))),
        out_shape=jax.ShapeDtypeStruct((N_CHIPS, hr, cs), BF16),
        compiler_params=_params(2, False))(cvec, g, recv)


def _rs_chips(sums):
    nw = len(sums)

    def body(*refs):
        s_refs, out_refs = refs[:nw], refs[nw:2 * nw]
        send, recv = refs[2 * nw:]
        x, y, c = _my_pos()
        k0 = 2 * x + y
        chips = _other_chips(x, y)
        copies = []
        for w in range(nw):
            for j, ch in enumerate(chips):
                kj = 2 * ch[0] + ch[1]
                cp = _remote(s_refs[w].at[kj], out_refs[w].at[k0], send.at[w, j], recv.at[w, j], (ch[0], ch[1], c))
                cp.start()
                copies.append(cp)
        for w in range(nw):
            for j, ch in enumerate(chips):
                kj = 2 * ch[0] + ch[1]
                d = out_refs[w].at[kj]
                _remote(d, d, send.at[w, j], recv.at[w, j], (ch[0], ch[1], c)).wait_recv()
        for cp in copies:
            cp.wait_send()

    anyspec = pl.BlockSpec(memory_space=pl.ANY)
    return pl.pallas_call(
        body, name="rs_chips",
        out_shape=[jax.ShapeDtypeStruct(s.shape, BF16) for s in sums],
        in_specs=[anyspec] * nw, out_specs=[anyspec] * nw,
        scratch_shapes=[pltpu.SemaphoreType.DMA((nw, 3)), pltpu.SemaphoreType.DMA((nw, 3))])(*sums)


def _sum_chips(sums, landed, kc, name):
    _, hr, cs = sums.shape
    tr = _tile(hr, 256, 16)

    def body(kc_ref, own_ref, a_ref, b_ref, c_ref, o_ref):
        del kc_ref
        o_ref[...] = (own_ref[0].astype(F32) + a_ref[0].astype(F32)) + (b_ref[0].astype(F32) + c_ref[0].astype(F32))

    slot = lambda j: pl.BlockSpec((1, tr, cs), lambda i, kc_ref: ((kc_ref[0] + j) % N_CHIPS, i, 0))
    return pl.pallas_call(
        body, name=name,
        grid_spec=pltpu.PrefetchScalarGridSpec(
            num_scalar_prefetch=1, grid=(hr // tr,), in_specs=[slot(0), slot(1), slot(2), slot(3)],
            out_specs=pl.BlockSpec((tr, cs), lambda i, kc_ref: (kc_ref[1] * (hr // tr) + i, 0))),
        out_shape=jax.ShapeDtypeStruct((2 * hr, cs), F32),
        compiler_params=_params(1, False))(kc, sums, landed, landed, landed)


def _exchange_halves(shards):
    nw = len(shards)

    def body(*refs):
        out_refs = refs[nw:2 * nw]
        send, recv = refs[2 * nw:]
        x, y, c = _my_pos()
        sib = (x, y, 1 - c)
        copies = []
        for w in range(nw):
            hr = shards[w].shape[0] // 2
            mine = out_refs[w].at[pl.ds(pl.multiple_of(c * hr, 8), hr), :]
            cp = _remote(mine, mine, send.at[w], recv.at[w], sib)
            cp.start()
            copies.append(cp)
        for w in range(nw):
            hr = shards[w].shape[0] // 2
            other = out_refs[w].at[pl.ds(pl.multiple_of((1 - c) * hr, 8), hr), :]
            _remote(other, other, send.at[w], recv.at[w], sib).wait_recv()
        for cp in copies:
            cp.wait_send()

    anyspec = pl.BlockSpec(memory_space=pl.ANY)
    return pl.pallas_call(
        body, name="exchange_halves",
        out_shape=[jax.ShapeDtypeStruct(s.shape, F32) for s in shards],
        in_specs=[anyspec] * nw, out_specs=[anyspec] * nw,
        input_output_aliases={w: w for w in range(nw)},
        scratch_shapes=[pltpu.SemaphoreType.DMA((nw,)), pltpu.SemaphoreType.DMA((nw,))])(*shards)


def _cast_into_full(w, kc, axis, name):
    r, cs = w.shape
    tr = _tile(r, 256, 16)

    def body(kc_ref, w_ref, o_ref):
        del kc_ref
        o_ref[...] = w_ref[...].astype(BF16)

    if axis == 1:
        ospec = pl.BlockSpec((tr, cs), lambda i, kc_ref: (i, kc_ref[0]))
    else:
        ospec = pl.BlockSpec((tr, cs), lambda i, kc_ref: (kc_ref[0] * (r // tr) + i, 0))
    return pl.pallas_call(
        body, name=name,
        grid_spec=pltpu.PrefetchScalarGridSpec(
            num_scalar_prefetch=1, grid=(r // tr,), in_specs=[pl.BlockSpec((tr, cs), lambda i, kc_ref: (i, 0))],
            out_specs=ospec),
        out_shape=jax.ShapeDtypeStruct(_full_shape((r, cs), axis), BF16),
        compiler_params=_params(1, False))(kc, w)


def _adam_math(w, g, m, v):
    m2 = ADAM_B1 * m + (1.0 - ADAM_B1) * g
    v2 = ADAM_B2 * v + (1.0 - ADAM_B2) * (g * g)
    m_hat = m2 / (1.0 - ADAM_B1 ** ADAM_STEP)
    v_hat = v2 / (1.0 - ADAM_B2 ** ADAM_STEP)
    delta = -ADAM_LR * (m_hat / (jnp.sqrt(v_hat) + ADAM_EPS) + ADAM_WD * w)
    return delta, m2, v2


def _adam(w, g, m, v, name):
    r, cs = w.shape
    tr = _tile(r, 256, 8)

    def body(w_ref, g_ref, m_ref, v_ref, d_ref, m2_ref, v2_ref):
        d, m2, v2 = _adam_math(w_ref[...], g_ref[...], m_ref[...], v_ref[...])
        d_ref[...] = d
        m2_ref[...] = m2
        v2_ref[...] = v2

    spec = pl.BlockSpec((tr, cs), lambda i: (i, 0))
    shp = jax.ShapeDtypeStruct((r, cs), F32)
    return pl.pallas_call(body, name=name, grid=(r // tr,), in_specs=[spec] * 4, out_specs=[spec] * 3,
                          out_shape=[shp, shp, shp], compiler_params=_params(1, False))(w, g, m, v)


def _mod_rows(a16, w, b, name):
    D, n = w.shape
    tn = _tile(n, 512)

    def body(a_ref, w_ref, b_ref, o_ref):
        a = a_ref[...]
        o_ref[...] = _nn((a * _sigmoid(a)).astype(BF16), w_ref[...].astype(BF16)) + b_ref[...]

    return pl.pallas_call(
        body, name=name, grid=(n // tn,),
        in_specs=[pl.BlockSpec((16, D), lambda j: (0, 0)), pl.BlockSpec((D, tn), lambda j: (0, j)),
                  pl.BlockSpec((1, tn), lambda j: (0, j))],
        out_specs=pl.BlockSpec((16, tn), lambda j: (0, j)),
        out_shape=jax.ShapeDtypeStruct((16, n), F32), compiler_params=_params(1, False))(a16, w, b)


def _w_mod_update(a16, d16, w, m, v):
    D, n = w.shape
    tn = _tile(n, 256)

    def body(a_ref, d_ref, w_ref, m_ref, v_ref, g_ref, dl_ref, m2_ref, v2_ref, p_ref):
        @pl.when(pl.program_id(0) == 0)
        def _():
            p_ref[...] = jnp.zeros_like(p_ref)
        a = a_ref[...]
        db = d_ref[...].astype(BF16)
        wv = w_ref[...]
        g = _tn((a * _sigmoid(a)).astype(BF16), db)
        g_ref[...] = g
        d, m2, v2 = _adam_math(wv, g, m_ref[...], v_ref[...])
        dl_ref[...] = d
        m2_ref[...] = m2
        v2_ref[...] = v2
        p_ref[...] += _nt(db, wv.astype(BF16))

    wspec = pl.BlockSpec((D, tn), lambda j: (0, j))
    shp = jax.ShapeDtypeStruct((D, n), F32)
    return pl.pallas_call(
        body, name="w_mod_update", grid=(n // tn,),
        in_specs=[pl.BlockSpec((16, D), lambda j: (0, 0)), pl.BlockSpec((16, tn), lambda j: (0, j)), wspec, wspec, wspec],
        out_specs=[wspec, wspec, wspec, wspec, pl.BlockSpec((16, D), lambda j: (0, 0))],
        out_shape=[shp, shp, shp, shp, jax.ShapeDtypeStruct((16, D), F32)],
        compiler_params=_params(1))(a16, d16, w, m, v)


def _sum_devices(g8, name):
    _, R, Cc = g8.shape

    def body(g_ref, o_ref):
        t = g_ref[0]
        for d in range(1, N_DEV):
            t = t + g_ref[d]
        o_ref[...] = t

    return pl.pallas_call(body, name=name, out_shape=jax.ShapeDtypeStruct((R, Cc), F32))(g8)


def _c_ctx_grad(parts, c_ctx):
    D = c_ctx.shape[1]

    def body(p_ref, c_ref, o_ref):
        t = p_ref[0]
        for k in range(1, N_CHIPS):
            t = t + p_ref[2 * k]
        cv = c_ref[...]
        sg = _sigmoid(cv)
        o_ref[...] = t * (sg * (1.0 + cv * (1.0 - sg)))

    return pl.pallas_call(body, name="c_ctx_grad", out_shape=jax.ShapeDtypeStruct((1, D), F32))(parts, c_ctx)


LOSS_LANE = 64


def kernel(x, c, ctx, c_ctx, w_mod, b_mod, norm_mix, norm_ffn, w_in, ret_decay, attn_sink, w_out, w_gate, w_up, w_down, norm_final, loss_target, m_c_ctx, m_w_mod, m_b_mod, m_norm_mix, m_norm_ffn, m_w_in, m_ret_decay, m_attn_sink, m_w_out, m_w_gate, m_w_up, m_w_down, m_norm_final, v_c_ctx, v_w_mod, v_b_mod, v_norm_mix, v_norm_ffn, v_w_in, v_ret_decay, v_attn_sink, v_w_out, v_w_gate, v_w_up, v_w_down, v_norm_final):
    D = x.shape[-1]
    n3 = w_mod.shape[-1]
    xi, yi, ci = _my_pos()
    b = 4 * xi + 2 * yi + ci
    k0 = 2 * xi + yi
    cvec = jnp.reshape(ci, (1,)).astype(jnp.int32)
    kc = jnp.stack([k0, ci]).astype(jnp.int32)

    dense = [("w_in", w_in[0], 1), ("w_out", w_out[0], 0), ("w_gate", w_gate[0], 1), ("w_up", w_up[0], 1),
             ("w_down", w_down[0], 0)]
    axes = [a for _, _, a in dense]
    shapes = [w.shape for _, w, _ in dense]
    own16 = [_cast_into_full(w, kc, a, "cast_" + n) for n, w, a in dense]
    f_in, f_out, f_gate, f_up, f_down = _allgather_weights(own16, shapes, axes)

    c_all = _allgather8(c, "gather_c").reshape(N_DEV, D)
    c_ctx2 = c_ctx.reshape(1, D)
    a16 = jnp.concatenate([c_all, c_ctx2, jnp.zeros((16 - N_DEV - 1, D), F32)], axis=0)
    b_cols = lax.dynamic_slice_in_dim(b_mod, k0 * n3, n3, axis=1)
    mod16 = _mod_rows(a16, w_mod[0], b_cols, "mod_rows")
    mod_all = _allgather8(mod16, "gather_mod")
    mine = jnp.stack([lax.dynamic_index_in_dim(mod_all, 2 * k + ci, 0, keepdims=False) for k in range(N_CHIPS)])
    mod = lax.dynamic_index_in_dim(mine, b, 1, keepdims=False).reshape(6, D)
    modc = mine[:, N_DEV].reshape(6, D)

    lg = -jnp.exp(ret_decay[0])
    out = _local_step(x[0], ctx[0], loss_target[0], mod, modc, norm_mix, norm_ffn, norm_final.reshape(1, D), lg,
                      attn_sink, f_in, f_out, f_gate, f_up, f_down)

    grads16 = [out["g_w_in"], out["g_w_out"], out["g_w_gate"], out["g_w_up"], out["g_w_down"]]
    from_sib = _rs_sibling(grads16, shapes, axes)
    sums = [_add_halves(g, r, cvec, s, a, "add_halves_" + n)
            for g, r, s, a, (n, _, _) in zip(grads16, from_sib, shapes, axes, dense)]
    landed = _rs_chips(sums)
    halves = [_sum_chips(s, p, kc, "sum_chips_" + n) for s, p, (n, _, _) in zip(sums, landed, dense)]
    g_dense = _exchange_halves(halves)

    nh = 2 * RET_HEADS
    assert nh + ATT_HEADS <= LOSS_LANE
    misc = jnp.concatenate([out["d_ret_decay"].reshape(1, nh), out["d_sink"].reshape(1, ATT_HEADS),
                            jnp.zeros((1, LOSS_LANE - nh - ATT_HEADS), F32), out["loss"].reshape(1, 1),
                            jnp.zeros((1, D - LOSS_LANE - 1), F32)], axis=1)
    small = jnp.concatenate([out["dmod"], out["dmodc"], out["d_norm_mix"], out["d_norm_ffn"], out["d_norm_final"],
                             misc], axis=0)
    small_all = _allgather8(small, "gather_small")
    tot = _sum_devices(small_all, "sum_small")
    g_b_mod = (tot[0:6] + tot[6:12]).reshape(1, 6 * D)
    dmodc_tot = tot[6:12].reshape(1, 6 * D)
    dmod_rows = small_all[:, 0:6].reshape(N_DEV, 6 * D)
    d16 = jnp.concatenate([dmod_rows, dmodc_tot, jnp.zeros((16 - N_DEV - 1, 6 * D), F32)], axis=0)
    d16 = lax.dynamic_slice_in_dim(d16, k0 * n3, n3, axis=1)
    g_w_mod, dl_w_mod, m2_w_mod, v2_w_mod, part = _w_mod_update(a16, d16, w_mod[0], m_w_mod[0], v_w_mod[0])
    part_all = _allgather8(part[N_DEV:N_DEV + 1], "gather_c_ctx")
    g_c_ctx = _c_ctx_grad(part_all, c_ctx2)
    loss = tot[15, LOSS_LANE]
    g_ret_decay = tot[15, :nh].reshape(1, 2, RET_HEADS)
    g_sink = tot[15, nh:nh + ATT_HEADS].reshape(1, ATT_HEADS)

    def pad_row(v):
        v = v.reshape(1, -1)
        return jnp.concatenate([v, jnp.zeros((1, D - v.shape[1]), F32)], axis=1)

    def pack(cc, bm, nm, nf, nfin, rd, sk):
        rows = [bm.reshape(6, D), cc.reshape(1, D), nm.reshape(1, D), nf.reshape(1, D), nfin.reshape(1, D),
                pad_row(jnp.concatenate([rd.reshape(-1), sk.reshape(-1)])), jnp.zeros((5, D), F32)]
        return jnp.concatenate(rows, axis=0)

    w_s = pack(c_ctx, b_mod, norm_mix, norm_ffn, norm_final, ret_decay, attn_sink)
    g_s = pack(g_c_ctx, g_b_mod, tot[12], tot[13], tot[14], g_ret_decay, g_sink)
    m_s = pack(m_c_ctx, m_b_mod, m_norm_mix, m_norm_ffn, m_norm_final, m_ret_decay, m_attn_sink)
    v_s = pack(v_c_ctx, v_b_mod, v_norm_mix, v_norm_ffn, v_norm_final, v_ret_decay, v_attn_sink)
    small_upd = _adam(w_s, g_s, m_s, v_s, "adam_small")

    def unpack(t):
        return dict(b_mod=t[0:6].reshape(1, 6 * D), c_ctx=t[6], norm_mix=t[7:8], norm_ffn=t[8:9], norm_final=t[9],
                    ret_decay=t[10, :nh].reshape(1, 2, RET_HEADS), attn_sink=t[10, nh:nh + ATT_HEADS].reshape(1, ATT_HEADS))

    dense_w = dict(w_in=(w_in, m_w_in, v_w_in), w_out=(w_out, m_w_out, v_w_out), w_gate=(w_gate, m_w_gate, v_w_gate),
                   w_up=(w_up, m_w_up, v_w_up), w_down=(w_down, m_w_down, v_w_down))
    grads = dict(unpack(g_s), w_mod=g_w_mod[None])
    upd = [dict(unpack(t)) for t in small_upd]
    upd[0]["w_mod"], upd[1]["w_mod"], upd[2]["w_mod"] = dl_w_mod[None], m2_w_mod[None], v2_w_mod[None]
    for (n, _, _), g in zip(dense, g_dense):
        w_, m_, v_ = dense_w[n]
        res = _adam(w_[0], g, m_[0], v_[0], "adam_" + n)
        grads[n] = g[None]
        for u, r_ in zip(upd, res):
            u[n] = r_[None]

    order = ['c_ctx', 'w_mod', 'b_mod', 'norm_mix', 'norm_ffn', 'w_in', 'ret_decay', 'attn_sink', 'w_out', 'w_gate',
             'w_up', 'w_down', 'norm_final']
    outs = [loss, out["grad_x"][None]] + [grads[n] for n in order]
    for u in upd:
        outs += [u[n] for n in order]
    return tuple(outs)
```

```python
import functools
import numpy as np
import jax
import jax.numpy as jnp
from jax import lax
from jax.experimental import pallas as pl
from jax.experimental.pallas import tpu as pltpu

F32 = jnp.float32
BF16 = jnp.bfloat16

RET_HEADS = 8
RET_DK = 64
RET_DV = 128
CHUNK = 128
ATT_HEADS = 16
ATT_KV_HEADS = 4
ATT_DH = 64
GRID_W = 64
ROPE_BASE = 10000.0
NORM_EPS = 1e-6
ADAM_LR = 0.001
ADAM_B1 = 0.9
ADAM_B2 = 0.999
ADAM_EPS = 1e-08
ADAM_WD = 0.01
ADAM_STEP = 10
NEG = -1e30
LANES = 128
VMEM_LIMIT = 56 * 1024 * 1024
ROWS_PER_LATCH = 1024
MESH = pl.DeviceIdType.MESH
N_CHIPS = 4
N_DEV = 8


def _nn(a, b):
    return jnp.dot(a, b, preferred_element_type=F32)


def _nt(a, b):
    return lax.dot_general(a, b, (((1,), (1,)), ((), ())), preferred_element_type=F32)


def _tn(a, b):
    return lax.dot_general(a, b, (((0,), (0,)), ((), ())), preferred_element_type=F32)


def _tile(n, pref, unit=LANES):
    t = min(n, pref)
    t -= t % unit
    while t > unit and n % t:
        t -= unit
    if t <= 0 or n % t:
        return n
    return t


def _params(ndim, vmem=True):
    return pltpu.CompilerParams(dimension_semantics=("arbitrary",) * ndim,
                                vmem_limit_bytes=VMEM_LIMIT if vmem else None)


def _sigmoid(x):
    return 1.0 / (1.0 + jnp.exp(-x))


def _fsum(x):
    return jnp.sum(jnp.sum(x, axis=1, keepdims=True), axis=0, keepdims=True)


def _rope_tables(L):
    lane = np.arange(LANES)
    d = lane % 64
    inv_r = jnp.asarray(ROPE_BASE, F32) ** (-jnp.arange(32, dtype=F32) / 32)
    t = jnp.arange(L)
    ang_r = t.astype(F32)[:, None] * jnp.tile(inv_r, LANES // 32)[None, :]
    Rr = np.zeros((LANES, LANES), np.float32)
    for l in range(LANES):
        if d[l] < 32:
            Rr[l + 32, l] = -1.0
        else:
            Rr[l - 32, l] = 1.0
    inv_a = jnp.asarray(ROPE_BASE, F32) ** (-jnp.arange(16, dtype=F32) / 16)
    rows = (t // GRID_W).astype(F32)
    cols = (t % GRID_W).astype(F32)
    dd = d % 32
    pos = jnp.where(jnp.asarray(d < 32)[None, :], rows[:, None], cols[:, None])
    ang_a = pos * jnp.tile(inv_a, LANES // 16)[None, :]
    Ra = np.zeros((LANES, LANES), np.float32)
    for l in range(LANES):
        if dd[l] < 16:
            Ra[l + 16, l] = -1.0
        else:
            Ra[l - 16, l] = 1.0
    D0 = np.zeros((LANES, LANES), np.float32)
    D1 = np.zeros((LANES, LANES), np.float32)
    for l in range(LANES):
        D0[l % 64, l] = 1.0
        D1[64 + l % 64, l] = 1.0
    return dict(
        Cr=jnp.cos(ang_r), Sr=jnp.sin(ang_r), Rr=jnp.asarray(Rr, BF16), RrT=jnp.asarray(Rr.T, BF16),
        Ca=jnp.cos(ang_a), Sa=jnp.sin(ang_a), Ra=jnp.asarray(Ra, BF16), RaT=jnp.asarray(Ra.T, BF16),
        D0=jnp.asarray(D0, BF16), D1=jnp.asarray(D1, BF16),
        D0T=jnp.asarray(D0.T, BF16), D1T=jnp.asarray(D1.T, BF16))


def _norm_mod(xf, g, sh, sc):
    r = lax.rsqrt(jnp.mean(xf * xf, axis=-1, keepdims=True) + NORM_EPS)
    return (xf * r * g) * (1.0 + sc) + sh


def _norm_mod_matmul(x, g, sh, sc, w, name):
    M, D = x.shape
    N = w.shape[1]
    tm, tn = _tile(M, ROWS_PER_LATCH, 8), _tile(N, 512)

    def body(x_ref, g_ref, sh_ref, sc_ref, w_ref, p_ref, h_ref, hs):
        @pl.when(pl.program_id(1) == 0)
        def _():
            hb = _norm_mod(x_ref[...], g_ref[...], sh_ref[...], sc_ref[...]).astype(BF16)
            hs[...] = hb
            h_ref[...] = hb
        p_ref[...] = _nn(hs[...], w_ref[...]).astype(BF16)

    vec = pl.BlockSpec((1, D), lambda i, j: (0, 0))
    return pl.pallas_call(
        body, name=name, grid=(M // tm, N // tn),
        in_specs=[pl.BlockSpec((tm, D), lambda i, j: (i, 0)), vec, vec, vec,
                  pl.BlockSpec((D, tn), lambda i, j: (0, j))],
        out_specs=[pl.BlockSpec((tm, tn), lambda i, j: (i, j)), pl.BlockSpec((tm, D), lambda i, j: (i, 0))],
        out_shape=[jax.ShapeDtypeStruct((M, N), BF16), jax.ShapeDtypeStruct((M, D), BF16)],
        scratch_shapes=[pltpu.VMEM((tm, D), BF16)],
        compiler_params=_params(2))(x, g, sh, sc, w)


def _proj_residual(a, w, xres, gt, name):
    M, K = a.shape
    N = w.shape[1]
    tm, tn = _tile(M, ROWS_PER_LATCH, 8), _tile(N, 512)

    def body(a_ref, w_ref, x_ref, gt_ref, xo_ref, o_ref):
        o = _nn(a_ref[...], w_ref[...])
        o_ref[...] = o.astype(BF16)
        xo_ref[...] = x_ref[...] + gt_ref[...] * o

    return pl.pallas_call(
        body, name=name, grid=(M // tm, N // tn),
        in_specs=[pl.BlockSpec((tm, K), lambda i, j: (i, 0)), pl.BlockSpec((K, tn), lambda i, j: (0, j)),
                  pl.BlockSpec((tm, tn), lambda i, j: (i, j)), pl.BlockSpec((1, tn), lambda i, j: (0, j))],
        out_specs=[pl.BlockSpec((tm, tn), lambda i, j: (i, j)), pl.BlockSpec((tm, tn), lambda i, j: (i, j))],
        out_shape=[jax.ShapeDtypeStruct((M, N), F32), jax.ShapeDtypeStruct((M, N), BF16)],
        compiler_params=_params(2))(a, w, xres, gt)


def _ffn_in(x1, g, sh, sc, wg, wu):
    M, D = x1.shape
    N = wg.shape[1]
    tm, tn = _tile(M, ROWS_PER_LATCH, 8), _tile(N, 512)

    def body(x_ref, g_ref, sh_ref, sc_ref, wg_ref, wu_ref, G_ref, U_ref, A_ref, h_ref, hs):
        @pl.when(pl.program_id(1) == 0)
        def _():
            hb = _norm_mod(x_ref[...], g_ref[...], sh_ref[...], sc_ref[...]).astype(BF16)
            hs[...] = hb
            h_ref[...] = hb
        G = _nn(hs[...], wg_ref[...])
        U = _nn(hs[...], wu_ref[...])
        G_ref[...] = G.astype(BF16)
        U_ref[...] = U.astype(BF16)
        A_ref[...] = (G * _sigmoid(G) * U).astype(BF16)

    vec = pl.BlockSpec((1, D), lambda i, j: (0, 0))
    wspec = pl.BlockSpec((D, tn), lambda i, j: (0, j))
    ospec = pl.BlockSpec((tm, tn), lambda i, j: (i, j))
    big = jax.ShapeDtypeStruct((M, N), BF16)
    return pl.pallas_call(
        body, name="ffn_in", grid=(M // tm, N // tn),
        in_specs=[pl.BlockSpec((tm, D), lambda i, j: (i, 0)), vec, vec, vec, wspec, wspec],
        out_specs=[ospec, ospec, ospec, pl.BlockSpec((tm, D), lambda i, j: (i, 0))],
        out_shape=[big, big, big, jax.ShapeDtypeStruct((M, D), BF16)],
        scratch_shapes=[pltpu.VMEM((tm, D), BF16)],
        compiler_params=_params(2))(x1, g, sh, sc, wg, wu)


def _final(x2, gn, tgt):
    M, D = x2.shape
    tm = _tile(M, 256, 8)

    def body(x_ref, g_ref, t_ref, dx_ref, loss_ref, dg_ref):
        @pl.when(pl.program_id(0) == 0)
        def _():
            loss_ref[...] = jnp.zeros_like(loss_ref)
            dg_ref[...] = jnp.zeros_like(dg_ref)
        x = x_ref[...]
        g = g_ref[...]
        r = lax.rsqrt(jnp.mean(x * x, axis=-1, keepdims=True) + NORM_EPS)
        xh = x * r
        e = xh * g - t_ref[...]
        loss_ref[...] += (0.5 / D) * _fsum(e * e)
        dy = e * (1.0 / D)
        dg_ref[...] += jnp.sum(dy * xh, axis=0, keepdims=True)
        dxh = dy * g
        dx_ref[...] = r * (dxh - xh * jnp.mean(dxh * xh, axis=-1, keepdims=True))

    row = pl.BlockSpec((tm, D), lambda i: (i, 0))
    return pl.pallas_call(
        body, name="final_loss", grid=(M // tm,),
        in_specs=[row, pl.BlockSpec((1, D), lambda i: (0, 0)), row],
        out_specs=[row, pl.BlockSpec((1, LANES), lambda i: (0, 0)), pl.BlockSpec((1, D), lambda i: (0, 0))],
        out_shape=[jax.ShapeDtypeStruct((M, D), F32), jax.ShapeDtypeStruct((1, LANES), F32),
                   jax.ShapeDtypeStruct((1, D), F32)],
        compiler_params=_params(1))(x2, gn, tgt)


def _col_group(blk0, nblk):
    return int(np.gcd(blk0, nblk)) if blk0 else nblk


def _rope_cols(src, blk0, nblk, Ct, St, R, scale, rope, name):
    M = src.shape[0]
    tm = _tile(M, 512, 8)
    wb = _col_group(blk0, nblk)

    def body(x_ref, c_ref, s_ref, r_ref, o_ref):
        for j in range(wb):
            cols = slice(j * LANES, (j + 1) * LANES)
            x = x_ref[:, cols]
            xf = x.astype(F32)
            if rope:
                xf = xf * c_ref[...] + _nn(x.astype(BF16), r_ref[...]) * s_ref[...]
            o_ref[:, cols] = (xf * scale).astype(BF16)

    tab = pl.BlockSpec((tm, LANES), lambda i, j: (i, 0))
    return pl.pallas_call(
        body, name=name, grid=(M // tm, nblk // wb),
        in_specs=[pl.BlockSpec((tm, wb * LANES), lambda i, j: (i, blk0 // wb + j)), tab, tab,
                  pl.BlockSpec((LANES, LANES), lambda i, j: (0, 0))],
        out_specs=pl.BlockSpec((tm, wb * LANES), lambda i, j: (i, j)),
        out_shape=jax.ShapeDtypeStruct((M, nblk * LANES), BF16),
        compiler_params=_params(2, False))(src, Ct, St, R)


def _dup_heads(src, blk0, npair, Ct, St, R, D0, D1, rope, name):
    M = src.shape[0]
    tm = _tile(M, 512, 8)

    def body(x_ref, c_ref, s_ref, r_ref, d0_ref, d1_ref, o_ref):
        x = x_ref[...]
        if rope:
            x = (x.astype(F32) * c_ref[...] + _nn(x, r_ref[...]) * s_ref[...]).astype(BF16)
        o_ref[0] = _nn(x, d0_ref[...]).astype(BF16)
        o_ref[1] = _nn(x, d1_ref[...]).astype(BF16)

    tab = pl.BlockSpec((tm, LANES), lambda i, p: (i, 0))
    mat = pl.BlockSpec((LANES, LANES), lambda i, p: (0, 0))
    return pl.pallas_call(
        body, name=name, grid=(M // tm, npair),
        in_specs=[pl.BlockSpec((tm, LANES), lambda i, p: (i, blk0 + p)), tab, tab, mat, mat, mat],
        out_specs=pl.BlockSpec((2, tm, LANES), lambda i, p: (p, i, 0)),
        out_shape=jax.ShapeDtypeStruct((2 * npair, M, LANES), BF16),
        compiler_params=_params(2, False))(src, Ct, St, R, D0, D1)


def _unrope_cols(dsrc, dst, blk0, nblk, Ct, St, RT, scale, rope, name):
    M = dsrc.shape[0]
    tm = _tile(M, 512, 8)
    wb = _col_group(blk0, nblk)

    def body(x_ref, c_ref, s_ref, r_ref, dst_ref, o_ref):
        del dst_ref
        for j in range(wb):
            cols = slice(j * LANES, (j + 1) * LANES)
            xf = x_ref[:, cols].astype(F32)
            if rope:
                xf = xf * c_ref[...] + _nn((xf * s_ref[...]).astype(BF16), r_ref[...])
            o_ref[:, cols] = (xf * scale).astype(BF16)

    tab = pl.BlockSpec((tm, LANES), lambda i, j: (i, 0))
    return pl.pallas_call(
        body, name=name, grid=(M // tm, nblk // wb),
        in_specs=[pl.BlockSpec((tm, wb * LANES), lambda i, j: (i, j)), tab, tab,
                  pl.BlockSpec((LANES, LANES), lambda i, j: (0, 0)),
                  pl.BlockSpec(memory_space=pl.ANY)],
        out_specs=pl.BlockSpec((tm, wb * LANES), lambda i, j: (i, blk0 // wb + j)),
        out_shape=jax.ShapeDtypeStruct(dst.shape, dst.dtype),
        input_output_aliases={4: 0},
        compiler_params=_params(2, False))(dsrc, Ct, St, RT, dst)


def _fold_heads(parts, dst, blk0, npair, Ct, St, RT, D0T, D1T, rope, name):
    M = parts[0][0].shape[1]
    nb = M // CHUNK
    R = _tile(M, 1024, CHUNK)
    rb = R // CHUNK
    nrefs = sum(1 if s == 0 else 2 for _, s in parts)

    def body(*refs):
        part_refs = list(refs[:nrefs])
        c_ref, s_ref, r_ref, d0_ref, d1_ref, dst_ref, o_ref = refs[nrefs:]
        del dst_ref
        i = pl.program_id(0)
        tot = [jnp.zeros((R, LANES), F32), jnp.zeros((R, LANES), F32)]
        for _, shift in parts:
            main = part_refs.pop(0)
            if shift == 0:
                for e in range(2):
                    tot[e] = tot[e] + main[e].astype(F32)
                continue
            edge = part_refs.pop(0)
            ok = (i + 1) * rb <= nb - 1 if shift > 0 else i > 0
            for e in range(2):
                ed = jnp.where(ok, edge[e].astype(F32), 0.0)
                if rb == 1:
                    tot[e] = tot[e] + ed
                elif shift > 0:
                    tot[e] = tot[e] + jnp.concatenate([main[e, CHUNK:, :].astype(F32), ed], axis=0)
                else:
                    tot[e] = tot[e] + jnp.concatenate([ed, main[e, :R - CHUNK, :].astype(F32)], axis=0)
        f = _nn(tot[0].astype(BF16), d0_ref[...]) + _nn(tot[1].astype(BF16), d1_ref[...])
        if rope:
            f = f * c_ref[...] + _nn((f * s_ref[...]).astype(BF16), r_ref[...])
        o_ref[...] = f.astype(BF16)

    in_specs, args = [], []
    for a, shift in parts:
        assert shift in (-1, 0, 1)
        in_specs.append(pl.BlockSpec((2, R, LANES), lambda i, p: (p, i, 0)))
        args.append(a)
        if shift > 0:
            in_specs.append(pl.BlockSpec((2, CHUNK, LANES), lambda i, p: (p, jnp.minimum((i + 1) * rb, nb - 1), 0)))
            args.append(a)
        elif shift < 0:
            in_specs.append(pl.BlockSpec((2, CHUNK, LANES), lambda i, p: (p, jnp.maximum(i * rb - 1, 0), 0)))
            args.append(a)
    tab = pl.BlockSpec((R, LANES), lambda i, p: (i, 0))
    mat = pl.BlockSpec((LANES, LANES), lambda i, p: (0, 0))
    return pl.pallas_call(
        body, name=name, grid=(M // R, npair),
        in_specs=in_specs + [tab, tab, mat, mat, mat, pl.BlockSpec(memory_space=pl.ANY)],
        out_specs=pl.BlockSpec((R, LANES), lambda i, p: (i, blk0 + p)),
        out_shape=jax.ShapeDtypeStruct(dst.shape, dst.dtype),
        input_output_aliases={nrefs + 5: 0},
        compiler_params=_params(2, False))(*args, Ct, St, RT, D0T, D1T, dst)


def _head_masks():
    lane = lax.broadcasted_iota(jnp.int32, (1, LANES), 1)
    return [lane < 64, lane >= 64]


def _decay_vecs(lam, mu):
    i = lax.broadcasted_iota(jnp.int32, (CHUNK, 1), 0).astype(F32)
    return dict(qf=jnp.exp(lam * (i + 1.0)), kf=jnp.exp(lam * (CHUNK - 1.0 - i)),
                qb=jnp.exp(mu * (CHUNK - i)), kb=jnp.exp(mu * i),
                gf=jnp.exp(lam * float(CHUNK)), gb=jnp.exp(mu * float(CHUNK)), i=i)


def _decay_mask(lam, mu):
    r = lax.broadcasted_iota(jnp.int32, (CHUNK, CHUNK), 0)
    c = lax.broadcasted_iota(jnp.int32, (CHUNK, CHUNK), 1)
    rel = (r - c).astype(F32)
    low = rel >= 0.0
    mf = jnp.exp(lam * jnp.maximum(rel, 0.0))
    mb = jnp.exp(mu * jnp.maximum(-rel, 0.0))
    return jnp.where(low, mf, mb), rel, low


def _lam_of(lg_ref, row, idx):
    return jnp.full((1, 1), lg_ref[row, idx], F32)


def _ret_states_fwd(Kr, P, Krc, Pc, lg, rv_blk, npairs):
    L = Kr.shape[0]
    Lc = Krc.shape[0]
    N, ncc = L // CHUNK, Lc // CHUNK

    def body(lg_ref, k_ref, v_ref, kc_ref, vc_ref, sf_ref, S):
        p, n = pl.program_id(0), pl.program_id(1)
        masks = _head_masks()
        for h in range(2):
            lam = _lam_of(lg_ref, 0, 2 * p + h)
            dv = _decay_vecs(lam, lam)
            hm = masks[h]

            @pl.when(n == 0)
            def _():
                s = jnp.zeros((LANES, LANES), F32)
                for cc in range(ncc):
                    kw = jnp.where(hm, kc_ref[cc * CHUNK:(cc + 1) * CHUNK, :].astype(F32) * dv["kf"], 0.0).astype(BF16)
                    s = dv["gf"] * s + _tn(kw, vc_ref[cc * CHUNK:(cc + 1) * CHUNK, h * LANES:(h + 1) * LANES])
                S[h] = s

            s = S[h]
            sf_ref[0, 0, h] = s.astype(BF16)
            kw = jnp.where(hm, k_ref[...].astype(F32) * dv["kf"], 0.0).astype(BF16)
            S[h] = dv["gf"] * s + _tn(kw, v_ref[:, h * LANES:(h + 1) * LANES])

    return pl.pallas_call(
        body, name="ret_states_fwd", grid=(npairs, N),
        in_specs=[pl.BlockSpec(memory_space=pltpu.SMEM),
                  pl.BlockSpec((CHUNK, LANES), lambda p, n: (n, p)),
                  pl.BlockSpec((CHUNK, 2 * LANES), lambda p, n: (n, rv_blk + p)),
                  pl.BlockSpec((Lc, LANES), lambda p, n: (0, p)),
                  pl.BlockSpec((Lc, 2 * LANES), lambda p, n: (0, rv_blk + p))],
        out_specs=pl.BlockSpec((1, 1, 2, LANES, LANES), lambda p, n: (p, n, 0, 0, 0)),
        out_shape=jax.ShapeDtypeStruct((npairs, N, 2, LANES, LANES), BF16),
        scratch_shapes=[pltpu.VMEM((2, LANES, LANES), F32)],
        compiler_params=_params(2, False))(lg, Kr, P, Krc, Pc)


def _ret_chunk_fwd(q, k, v, sf, sb, hm, lam, mu):
    dv = _decay_vecs(lam, mu)
    Mk, rel, low = _decay_mask(lam, mu)
    qm = jnp.where(hm, q, jnp.zeros_like(q))
    qmf = qm.astype(F32)
    A = _nt(qm, k)
    Am = A * Mk
    Amb = Am.astype(BF16)
    Qf = (qmf * dv["qf"]).astype(BF16)
    Qb = (qmf * dv["qb"]).astype(BF16)
    O = _nn(Amb, v) + _nn(Qf, sf) + _nn(Qb, sb)
    return dict(dv=dv, Mk=Mk, rel=rel, low=low, qm=qm, Am=Am, Amb=Amb, Qf=Qf, Qb=Qb, O=O)


def _ret_out_fwd(Qr, Kr, P, Krc, Pc, SF, lg, rv_blk, rg_blk, npairs, d_mix):
    L = Qr.shape[0]
    Lc = Krc.shape[0]
    N, ncc = L // CHUNK, Lc // CHUNK

    def body(lg_ref, q_ref, k_ref, v_ref, g_ref, sf_ref, kc_ref, vc_ref, y_ref, sb_ref, S):
        p, n = pl.program_id(0), pl.program_id(1)
        masks = _head_masks()
        for h in range(2):
            lam = _lam_of(lg_ref, 0, 2 * p + h)
            mu = _lam_of(lg_ref, 1, 2 * p + h)
            hm = masks[h]
            dvb = _decay_vecs(lam, mu)

            @pl.when(n == 0)
            def _():
                s = jnp.zeros((LANES, LANES), F32)
                for cc in reversed(range(ncc)):
                    kw = jnp.where(hm, kc_ref[cc * CHUNK:(cc + 1) * CHUNK, :].astype(F32) * dvb["kb"], 0.0).astype(BF16)
                    s = dvb["gb"] * s + _tn(kw, vc_ref[cc * CHUNK:(cc + 1) * CHUNK, h * LANES:(h + 1) * LANES])
                S[h] = s

            s = S[h]
            sbb = s.astype(BF16)
            sb_ref[0, 0, h] = sbb
            v = v_ref[:, h * LANES:(h + 1) * LANES]
            f = _ret_chunk_fwd(q_ref[...], k_ref[...], v, sf_ref[0, 0, h], sbb, hm, lam, mu)
            O = f["O"]
            r = lax.rsqrt(jnp.mean(O * O, axis=-1, keepdims=True) + NORM_EPS)
            g = g_ref[:, h * LANES:(h + 1) * LANES].astype(F32)
            y_ref[:, h * LANES:(h + 1) * LANES] = (O * r * (g * _sigmoid(g))).astype(BF16)
            kw = jnp.where(hm, k_ref[...].astype(F32) * dvb["kb"], 0.0).astype(BF16)
            S[h] = dvb["gb"] * s + _tn(kw, v)

    rev = lambda n: N - 1 - n
    return pl.pallas_call(
        body, name="ret_out_fwd", grid=(npairs, N),
        in_specs=[pl.BlockSpec(memory_space=pltpu.SMEM),
                  pl.BlockSpec((CHUNK, LANES), lambda p, n: (rev(n), p)),
                  pl.BlockSpec((CHUNK, LANES), lambda p, n: (rev(n), p)),
                  pl.BlockSpec((CHUNK, 2 * LANES), lambda p, n: (rev(n), rv_blk + p)),
                  pl.BlockSpec((CHUNK, 2 * LANES), lambda p, n: (rev(n), rg_blk + p)),
                  pl.BlockSpec((1, 1, 2, LANES, LANES), lambda p, n: (p, rev(n), 0, 0, 0)),
                  pl.BlockSpec((Lc, LANES), lambda p, n: (0, p)),
                  pl.BlockSpec((Lc, 2 * LANES), lambda p, n: (0, rv_blk + p))],
        out_specs=[pl.BlockSpec((CHUNK, 2 * LANES), lambda p, n: (rev(n), p)),
                   pl.BlockSpec((1, 1, 2, LANES, LANES), lambda p, n: (p, rev(n), 0, 0, 0))],
        out_shape=[jax.ShapeDtypeStruct((L, d_mix), BF16),
                   jax.ShapeDtypeStruct((npairs, N, 2, LANES, LANES), BF16)],
        scratch_shapes=[pltpu.VMEM((2, LANES, LANES), F32)],
        compiler_params=_params(2, False))(lg, Qr, Kr, P, P, SF, Krc, Pc)


ACC_ROWS = 8


def _ret_bwd1(Qr, Kr, P, Krc, Pc, SF, SB, dY, lg, rv_blk, rg_blk, npairs, d_proj):
    L = Qr.shape[0]
    Lc = Krc.shape[0]
    N, ncc = L // CHUNK, Lc // CHUNK

    def body(lg_ref, q_ref, k_ref, v_ref, g_ref, sf_ref, sb_ref, dy_ref, kc_ref, vc_ref,
             dq_ref, dk_ref, dv_ref, dg_ref, do_ref, dkc_ref, dvc_ref, acc_ref, dS, T):
        p, n = pl.program_id(0), pl.program_id(1)
        masks = _head_masks()

        @pl.when(n == 0)
        def _():
            dS[...] = jnp.zeros_like(dS)
            T[...] = jnp.zeros_like(T)
            acc_ref[...] = jnp.zeros_like(acc_ref)

        dq_tot = jnp.zeros((CHUNK, LANES), F32)
        dk_tot = jnp.zeros((CHUNK, LANES), F32)
        for h in range(2):
            lam = _lam_of(lg_ref, 0, 2 * p + h)
            mu = _lam_of(lg_ref, 1, 2 * p + h)
            hm = masks[h]
            hs = slice(h * LANES, (h + 1) * LANES)
            v = v_ref[:, hs]
            k = k_ref[...]
            sf = sf_ref[0, 0, h]
            sb = sb_ref[0, 0, h]
            f = _ret_chunk_fwd(q_ref[...], k, v, sf, sb, hm, lam, mu)
            dv_, O = f["dv"], f["O"]
            r = lax.rsqrt(jnp.mean(O * O, axis=-1, keepdims=True) + NORM_EPS)
            on = O * r
            g = g_ref[:, hs].astype(F32)
            sg = _sigmoid(g)
            dy = dy_ref[:, hs].astype(F32)
            dg_ref[:, hs] = (dy * on * (sg * (1.0 + g * (1.0 - sg)))).astype(BF16)
            don = dy * (g * sg)
            dO = r * (don - on * jnp.mean(don * on, axis=-1, keepdims=True))
            dOb = dO.astype(BF16)
            do_ref[:, hs] = dOb
            dAm = _nt(dOb, v)
            T[h] += dAm * f["Am"]
            dAb = (dAm * f["Mk"]).astype(BF16)
            km = jnp.where(hm, k, jnp.zeros_like(k))
            dq = _nn(dAb, km)
            dk = _tn(dAb, f["qm"])
            dvh = _tn(f["Amb"], dOb)
            dQf = _nt(dOb, sf)
            dQb = _nt(dOb, sb)
            dq = dq + dQf * dv_["qf"] + dQb * dv_["qb"]
            acc_ref[0, h, 0:1, :] += _fsum(dQf * f["Qf"].astype(F32) * (dv_["i"] + 1.0))
            acc_ref[0, h, 1:2, :] += _fsum(dQb * f["Qb"].astype(F32) * (CHUNK - dv_["i"]))
            dSh = dS[h]
            dSb_ = dSh.astype(BF16)
            Kf = (km.astype(F32) * dv_["kf"]).astype(BF16)
            dKf = _nt(v, dSb_)
            dk = dk + jnp.where(hm, dKf * dv_["kf"], 0.0)
            acc_ref[0, h, 2:3, :] += _fsum(jnp.where(hm, dKf, 0.0) * Kf.astype(F32) * (CHUNK - 1.0 - dv_["i"]))
            dvh = dvh + _nn(Kf, dSb_)
            acc_ref[0, h, 3:4, :] += float(CHUNK) * dv_["gf"] * _fsum(dSh * sf.astype(F32))
            dSh = dv_["gf"] * dSh + _tn(f["Qf"], dOb)
            dS[h] = dSh
            dv_ref[:, hs] = dvh
            dq_tot = dq_tot + dq
            dk_tot = dk_tot + dk

        dq_ref[...] = dq_tot
        dk_ref[...] = dk_tot

        @pl.when(n == N - 1)
        def _():
            for h in range(2):
                lam = _lam_of(lg_ref, 0, 2 * p + h)
                dv_ = _decay_vecs(lam, lam)
                hm = masks[h]
                hs = slice(h * LANES, (h + 1) * LANES)
                states = [jnp.zeros((LANES, LANES), F32)]
                kws = []
                for cc in range(ncc):
                    kw = jnp.where(hm, kc_ref[cc * CHUNK:(cc + 1) * CHUNK, :].astype(F32) * dv_["kf"], 0.0).astype(BF16)
                    kws.append(kw)
                    states.append(dv_["gf"] * states[-1] + _tn(kw, vc_ref[cc * CHUNK:(cc + 1) * CHUNK, hs]))
                d = dS[h]
                for cc in reversed(range(ncc)):
                    db = d.astype(BF16)
                    rows = slice(cc * CHUNK, (cc + 1) * CHUNK)
                    dKf_c = jnp.where(hm, _nt(vc_ref[rows, hs], db), 0.0)
                    part = dKf_c * dv_["kf"]
                    if h == 0:
                        dkc_ref[rows, :] = part
                    else:
                        dkc_ref[rows, :] += part
                    acc_ref[0, h, 2:3, :] += _fsum(dKf_c * kws[cc].astype(F32) * (CHUNK - 1.0 - dv_["i"]))
                    dvc_ref[rows, hs] = _nn(kws[cc], db)
                    acc_ref[0, h, 3:4, :] += float(CHUNK) * dv_["gf"] * _fsum(d * states[cc])
                    d = dv_["gf"] * d
                _, rel, low = _decay_mask(lam, lam)
                Th = T[h]
                acc_ref[0, h, 4:5, :] += _fsum(jnp.where(low, Th * rel, 0.0))
                acc_ref[0, h, 5:6, :] += _fsum(jnp.where(low, 0.0, -Th * rel))

    rev = lambda n: N - 1 - n
    st = pl.BlockSpec((1, 1, 2, LANES, LANES), lambda p, n: (p, rev(n), 0, 0, 0))
    pair = pl.BlockSpec((CHUNK, LANES), lambda p, n: (rev(n), p))
    wide = lambda b0: pl.BlockSpec((CHUNK, 2 * LANES), lambda p, n: (rev(n), b0 + p))
    return pl.pallas_call(
        body, name="ret_bwd_desc", grid=(npairs, N),
        in_specs=[pl.BlockSpec(memory_space=pltpu.SMEM), pair, pair, wide(rv_blk), wide(rg_blk), st, st, wide(0),
                  pl.BlockSpec((Lc, LANES), lambda p, n: (0, p)),
                  pl.BlockSpec((Lc, 2 * LANES), lambda p, n: (0, rv_blk + p))],
        out_specs=[pair, pair, wide(0), wide(rg_blk), wide(0),
                   pl.BlockSpec((Lc, LANES), lambda p, n: (0, p)),
                   pl.BlockSpec((Lc, 2 * LANES), lambda p, n: (0, p)),
                   pl.BlockSpec((1, 2, ACC_ROWS, LANES), lambda p, n: (p, 0, 0, 0))],
        out_shape=[jax.ShapeDtypeStruct((L, npairs * LANES), F32),
                   jax.ShapeDtypeStruct((L, npairs * LANES), F32),
                   jax.ShapeDtypeStruct((L, npairs * 2 * LANES), F32),
                   jax.ShapeDtypeStruct((L, d_proj), BF16),
                   jax.ShapeDtypeStruct((L, npairs * 2 * LANES), BF16),
                   jax.ShapeDtypeStruct((Lc, npairs * LANES), F32),
                   jax.ShapeDtypeStruct((Lc, npairs * 2 * LANES), F32),
                   jax.ShapeDtypeStruct((npairs, 2, ACC_ROWS, LANES), F32)],
        scratch_shapes=[pltpu.VMEM((2, LANES, LANES), F32), pltpu.VMEM((2, CHUNK, CHUNK), F32)],
        compiler_params=_params(2, False))(lg, Qr, Kr, P, P, SF, SB, dY, Krc, Pc)


def _ret_bwd2(Qr, Kr, P, Krc, Pc, SB, dO, dKr, dVp, dP, dKc, dVc, lg, rv_blk, npairs):
    L = Qr.shape[0]
    Lc = Krc.shape[0]
    N, ncc = L // CHUNK, Lc // CHUNK

    def body(lg_ref, q_ref, k_ref, v_ref, sb_ref, do_ref, dkin_ref, dvin_ref, kc_ref, vc_ref, dkcin_ref, dvcin_ref,
             dpin_ref, dk_ref, dv_ref, dkc_ref, dvc_ref, acc_ref, dS):
        del dpin_ref
        p, n = pl.program_id(0), pl.program_id(1)
        masks = _head_masks()

        @pl.when(n == 0)
        def _():
            dS[...] = jnp.zeros_like(dS)
            acc_ref[...] = jnp.zeros_like(acc_ref)

        dk_tot = dkin_ref[...]
        for h in range(2):
            mu = _lam_of(lg_ref, 1, 2 * p + h)
            hm = masks[h]
            hs = slice(h * LANES, (h + 1) * LANES)
            dv_ = _decay_vecs(mu, mu)
            v = v_ref[:, hs]
            k = k_ref[...]
            q = q_ref[...]
            dOb = do_ref[:, hs]
            km = jnp.where(hm, k, jnp.zeros_like(k)).astype(F32)
            Kb = (km * dv_["kb"]).astype(BF16)
            Qb = (jnp.where(hm, q, jnp.zeros_like(q)).astype(F32) * dv_["qb"]).astype(BF16)
            dSh = dS[h]
            dSb_ = dSh.astype(BF16)
            dKb = jnp.where(hm, _nt(v, dSb_), 0.0)
            dk_tot = dk_tot + dKb * dv_["kb"]
            acc_ref[0, h, 0:1, :] += _fsum(dKb * Kb.astype(F32) * dv_["i"])
            dv_ref[:, hs] = (dvin_ref[:, hs] + _nn(Kb, dSb_)).astype(BF16)
            acc_ref[0, h, 1:2, :] += float(CHUNK) * dv_["gb"] * _fsum(dSh * sb_ref[0, 0, h].astype(F32))
            dSh = dv_["gb"] * dSh + _tn(Qb, dOb)
            dS[h] = dSh

        dk_ref[...] = dk_tot

        @pl.when(n == N - 1)
        def _():
            for h in range(2):
                mu = _lam_of(lg_ref, 1, 2 * p + h)
                hm = masks[h]
                hs = slice(h * LANES, (h + 1) * LANES)
                dv_ = _decay_vecs(mu, mu)
                states = {}
                kws = {}
                s = jnp.zeros((LANES, LANES), F32)
                for cc in reversed(range(ncc)):
                    states[cc] = s
                    kw = jnp.where(hm, kc_ref[cc * CHUNK:(cc + 1) * CHUNK, :].astype(F32) * dv_["kb"], 0.0).astype(BF16)
                    kws[cc] = kw
                    s = dv_["gb"] * s + _tn(kw, vc_ref[cc * CHUNK:(cc + 1) * CHUNK, hs])
                d = dS[h]
                for cc in range(ncc):
                    db = d.astype(BF16)
                    rows = slice(cc * CHUNK, (cc + 1) * CHUNK)
                    dKb_c = jnp.where(hm, _nt(vc_ref[rows, hs], db), 0.0)
                    part = dKb_c * dv_["kb"]
                    if h == 0:
                        dkc_ref[rows, :] = dkcin_ref[rows, :] + part
                    else:
                        dkc_ref[rows, :] += part
                    acc_ref[0, h, 0:1, :] += _fsum(dKb_c * kws[cc].astype(F32) * dv_["i"])
                    dvc_ref[rows, hs] = dvcin_ref[rows, hs] + _nn(kws[cc], db)
                    acc_ref[0, h, 1:2, :] += float(CHUNK) * dv_["gb"] * _fsum(d * states[cc])
                    d = dv_["gb"] * d

    st = pl.BlockSpec((1, 1, 2, LANES, LANES), lambda p, n: (p, n, 0, 0, 0))
    pair = pl.BlockSpec((CHUNK, LANES), lambda p, n: (n, p))
    wide = lambda b0: pl.BlockSpec((CHUNK, 2 * LANES), lambda p, n: (n, b0 + p))
    ckc = pl.BlockSpec((Lc, LANES), lambda p, n: (0, p))
    cvc = lambda b0: pl.BlockSpec((Lc, 2 * LANES), lambda p, n: (0, b0 + p))
    return pl.pallas_call(
        body, name="ret_bwd_asc", grid=(npairs, N),
        in_specs=[pl.BlockSpec(memory_space=pltpu.SMEM), pair, pair, wide(rv_blk), st, wide(0), pair, wide(0),
                  ckc, cvc(rv_blk), ckc, cvc(0), pl.BlockSpec(memory_space=pl.ANY)],
        out_specs=[pair, wide(rv_blk), ckc, cvc(0),
                   pl.BlockSpec((1, 2, ACC_ROWS, LANES), lambda p, n: (p, 0, 0, 0))],
        out_shape=[jax.ShapeDtypeStruct(dKr.shape, F32),
                   jax.ShapeDtypeStruct(dP.shape, dP.dtype),
                   jax.ShapeDtypeStruct(dKc.shape, F32),
                   jax.ShapeDtypeStruct(dVc.shape, F32),
                   jax.ShapeDtypeStruct((npairs, 2, ACC_ROWS, LANES), F32)],
        input_output_aliases={12: 1},
        scratch_shapes=[pltpu.VMEM((2, LANES, LANES), F32)],
        compiler_params=_params(2, False))(lg, Qr, Kr, P, SB, dO, dKr, dVp, Krc, Pc, dKc, dVc, dP)


def _att_valid(n, N):
    row = lax.broadcasted_iota(jnp.int32, (CHUNK, 3 * CHUNK), 0)
    col = lax.broadcasted_iota(jnp.int32, (CHUNK, 3 * CHUNK), 1)
    ok = jnp.logical_and(col >= row, col <= row + 2 * CHUNK)
    ok = jnp.logical_and(ok, jnp.logical_or(col >= CHUNK, n > 0))
    ok = jnp.logical_and(ok, jnp.logical_or(col < 2 * CHUNK, n < N - 1))
    return ok


def _att_probs(qa, K, Kc, valid, snk):
    s_loc = jnp.where(valid, _nt(qa, K), NEG)
    s_ctx = _nt(qa, Kc)
    mx = jnp.maximum(jnp.maximum(jnp.max(s_loc, axis=1, keepdims=True), jnp.max(s_ctx, axis=1, keepdims=True)), snk)
    p_loc = jnp.exp(s_loc - mx)
    p_ctx = jnp.exp(s_ctx - mx)
    p_snk = jnp.exp(snk - mx)
    inv = 1.0 / (jnp.sum(p_loc, axis=1, keepdims=True) + jnp.sum(p_ctx, axis=1, keepdims=True) + p_snk)
    return p_loc, p_ctx, p_snk, inv


def _att_specs(Lc, N):
    q = pl.BlockSpec((CHUNK, 2 * LANES), lambda g, n: (n, g))
    kv = lambda s: pl.BlockSpec((1, CHUNK, LANES), lambda g, n: (g, jnp.clip(n + s, 0, N - 1), 0))
    ctx = pl.BlockSpec((1, Lc, LANES), lambda g, n: (g, 0, 0))
    return q, kv, ctx


def _att_fwd(Qa, Kd, Vd, Kdc, Vdc, sink, Y, blk0):
    L = Qa.shape[0]
    Lc = Kdc.shape[1]
    N = L // CHUNK
    nkv = Kd.shape[0]

    def body(sink_ref, q_ref, kp, kc_, kn, vp, vc_, vn, kctx, vctx, y_in, o_ref):
        del y_in
        g, n = pl.program_id(0), pl.program_id(1)
        K = jnp.concatenate([kp[0], kc_[0], kn[0]], axis=0)
        V = jnp.concatenate([vp[0], vc_[0], vn[0]], axis=0)
        Kc, Vc = kctx[0], vctx[0]
        valid = _att_valid(n, N)
        masks = _head_masks()
        for pr in range(2):
            qp = q_ref[:, pr * LANES:(pr + 1) * LANES]
            acc = jnp.zeros((CHUNK, LANES), F32)
            for a in range(2):
                hm = masks[a]
                qa = jnp.where(hm, qp, jnp.zeros_like(qp))
                snk = jnp.full((1, 1), sink_ref[0, g * 4 + pr * 2 + a], F32)
                p_loc, p_ctx, _, inv = _att_probs(qa, K, Kc, valid, snk)
                Va = jnp.where(hm, V, jnp.zeros_like(V))
                Vca = jnp.where(hm, Vc, jnp.zeros_like(Vc))
                acc = acc + (_nn(p_loc.astype(BF16), Va) + _nn(p_ctx.astype(BF16), Vca)) * inv
            o_ref[:, pr * LANES:(pr + 1) * LANES] = acc.astype(BF16)

    q, kv, ctx = _att_specs(Lc, N)
    return pl.pallas_call(
        body, name="att_fwd", grid=(nkv, N),
        in_specs=[pl.BlockSpec(memory_space=pltpu.SMEM), q, kv(-1), kv(0), kv(1), kv(-1), kv(0), kv(1), ctx, ctx,
                  pl.BlockSpec(memory_space=pl.ANY)],
        out_specs=pl.BlockSpec((CHUNK, 2 * LANES), lambda g, n: (n, blk0 + g)),
        out_shape=jax.ShapeDtypeStruct(Y.shape, Y.dtype),
        input_output_aliases={10: 0},
        compiler_params=_params(2, False))(sink, Qa, Kd, Kd, Kd, Vd, Vd, Vd, Kdc, Vdc, Y)


def _att_bwd(Qa, Kd, Vd, Kdc, Vdc, sink, dY, blk0):
    L = Qa.shape[0]
    Lc = Kdc.shape[1]
    N = L // CHUNK
    nkv = Kd.shape[0]

    def body(sink_ref, q_ref, kp, kc_, kn, vp, vc_, vn, kctx, vctx, dy_ref,
             dq_ref, dkp, dkc_, dkn, dvp, dvc_, dvn, dkctx, dvctx, dsink_ref):
        g, n = pl.program_id(0), pl.program_id(1)

        @pl.when(n == 0)
        def _():
            dkctx[...] = jnp.zeros_like(dkctx)
            dvctx[...] = jnp.zeros_like(dvctx)
            dsink_ref[...] = jnp.zeros_like(dsink_ref)

        K = jnp.concatenate([kp[0], kc_[0], kn[0]], axis=0)
        V = jnp.concatenate([vp[0], vc_[0], vn[0]], axis=0)
        Kc, Vc = kctx[0], vctx[0]
        valid = _att_valid(n, N)
        masks = _head_masks()
        dK = jnp.zeros((3 * CHUNK, LANES), F32)
        dV = jnp.zeros((3 * CHUNK, LANES), F32)
        dKc = jnp.zeros((Lc, LANES), F32)
        dVc = jnp.zeros((Lc, LANES), F32)
        for pr in range(2):
            qp = q_ref[:, pr * LANES:(pr + 1) * LANES]
            dyp = dy_ref[:, pr * LANES:(pr + 1) * LANES]
            dq = jnp.zeros((CHUNK, LANES), F32)
            for a in range(2):
                hm = masks[a]
                qa = jnp.where(hm, qp, jnp.zeros_like(qp))
                do = jnp.where(hm, dyp, jnp.zeros_like(dyp))
                snk = jnp.full((1, 1), sink_ref[0, g * 4 + pr * 2 + a], F32)
                p_loc, p_ctx, p_snk, inv = _att_probs(qa, K, Kc, valid, snk)
                P_loc, P_ctx = p_loc * inv, p_ctx * inv
                dp_loc = _nt(do, V)
                dp_ctx = _nt(do, Vc)
                delta = jnp.sum(P_loc * dp_loc, axis=1, keepdims=True) + jnp.sum(P_ctx * dp_ctx, axis=1, keepdims=True)
                ds_loc = (P_loc * (dp_loc - delta)).astype(BF16)
                ds_ctx = (P_ctx * (dp_ctx - delta)).astype(BF16)
                row = pr * 2 + a
                dsink_ref[0, row:row + 1, :] += _fsum(-(p_snk * inv) * delta)
                dq = dq + jnp.where(hm, _nn(ds_loc, K) + _nn(ds_ctx, Kc), 0.0)
                dK = dK + _tn(ds_loc, qa)
                dKc = dKc + _tn(ds_ctx, qa)
                dV = dV + _tn(P_loc.astype(BF16), do)
                dVc = dVc + _tn(P_ctx.astype(BF16), do)
            dq_ref[:, pr * LANES:(pr + 1) * LANES] = dq
        for j, (rk, rv) in enumerate([(dkp, dvp), (dkc_, dvc_), (dkn, dvn)]):
            rk[0] = dK[j * CHUNK:(j + 1) * CHUNK].astype(BF16)
            rv[0] = dV[j * CHUNK:(j + 1) * CHUNK].astype(BF16)
        dkctx[0] += dKc
        dvctx[0] += dVc

    q, kv, ctx = _att_specs(Lc, N)
    blk = pl.BlockSpec((1, CHUNK, LANES), lambda g, n: (g, n, 0))
    part = jax.ShapeDtypeStruct((nkv, L, LANES), BF16)
    cshape = jax.ShapeDtypeStruct((nkv, Lc, LANES), F32)
    return pl.pallas_call(
        body, name="att_bwd", grid=(nkv, N),
        in_specs=[pl.BlockSpec(memory_space=pltpu.SMEM), q, kv(-1), kv(0), kv(1), kv(-1), kv(0), kv(1), ctx, ctx,
                  pl.BlockSpec((CHUNK, 2 * LANES), lambda g, n: (n, blk0 + g))],
        out_specs=[q, blk, blk, blk, blk, blk, blk, ctx, ctx,
                   pl.BlockSpec((1, 8, LANES), lambda g, n: (g, 0, 0))],
        out_shape=[jax.ShapeDtypeStruct(Qa.shape, F32), part, part, part, part, part, part, cshape, cshape,
                   jax.ShapeDtypeStruct((nkv, 8, LANES), F32)],
        compiler_params=_params(2, False))(sink, Qa, Kd, Kd, Kd, Vd, Vd, Vd, Kdc, Vdc, dY)


def _bwd_proj(dx, gt, w, saved, G=None, U=None, name="bwd_proj"):
    M, D = dx.shape
    N = w.shape[0]
    swiglu = G is not None
    tm, tn = _tile(M, ROWS_PER_LATCH, 8), _tile(N, 256 if swiglu else 512)

    def body(*refs):
        if swiglu:
            dx_ref, gt_ref, w_ref, sv_ref, G_ref, U_ref, dG_ref, dU_ref, dz_ref, dgt_ref, zs = refs
        else:
            dx_ref, gt_ref, w_ref, sv_ref, dA_ref, dz_ref, dgt_ref, zs = refs
        i, j = pl.program_id(0), pl.program_id(1)

        @pl.when(jnp.logical_and(i == 0, j == 0))
        def _():
            dgt_ref[...] = jnp.zeros_like(dgt_ref)

        @pl.when(j == 0)
        def _():
            d = dx_ref[...]
            z = (d * gt_ref[...]).astype(BF16)
            zs[...] = z
            dz_ref[...] = z
            dgt_ref[...] += jnp.sum(d * sv_ref[...].astype(F32), axis=0, keepdims=True)

        dA = _nt(zs[...], w_ref[...])
        if swiglu:
            Gv = G_ref[...].astype(F32)
            Uv = U_ref[...].astype(F32)
            sg = _sigmoid(Gv)
            dU_ref[...] = (dA * Gv * sg).astype(BF16)
            dG_ref[...] = (dA * Uv * (sg * (1.0 + Gv * (1.0 - sg)))).astype(BF16)
        else:
            dA_ref[...] = dA.astype(BF16)

    row = pl.BlockSpec((tm, D), lambda i, j: (i, 0))
    vec = pl.BlockSpec((1, D), lambda i, j: (0, 0))
    tile = pl.BlockSpec((tm, tn), lambda i, j: (i, j))
    big = jax.ShapeDtypeStruct((M, N), BF16)
    in_specs = [row, vec, pl.BlockSpec((tn, D), lambda i, j: (j, 0)), row]
    args = [dx, gt, w, saved]
    if swiglu:
        in_specs += [tile, tile]
        args += [G, U]
        out_specs = [tile, tile, row, vec]
        out_shape = [big, big, jax.ShapeDtypeStruct((M, D), BF16), jax.ShapeDtypeStruct((1, D), F32)]
    else:
        out_specs = [tile, row, vec]
        out_shape = [big, jax.ShapeDtypeStruct((M, D), BF16), jax.ShapeDtypeStruct((1, D), F32)]
    return pl.pallas_call(
        body, name=name, grid=(M // tm, N // tn), in_specs=in_specs, out_specs=out_specs, out_shape=out_shape,
        scratch_shapes=[pltpu.VMEM((tm, D), BF16)], compiler_params=_params(2))(*args)


def _tn_matmul(pairs, name):
    Ka, Nb = pairs[0][0].shape[1], pairs[0][1].shape[1]
    tk, tn = _tile(Ka, 2048), _tile(Nb, 2048)
    tls, nks = [], []
    for a, _ in pairs:
        tl = _tile(a.shape[0], 512, 8)
        tls.append(tl)
        nks.append(a.shape[0] // tl)
    starts = [int(s) for s in np.cumsum([0] + nks[:-1])]
    nk = int(sum(nks))

    def body(*refs):
        out_ref, acc = refs[-2], refs[-1]
        k = pl.program_id(2)

        @pl.when(k == 0)
        def _():
            acc[...] = jnp.zeros_like(acc)

        for idx in range(len(pairs)):
            a_ref, b_ref = refs[2 * idx], refs[2 * idx + 1]

            @pl.when(jnp.logical_and(k >= starts[idx], k < starts[idx] + nks[idx]))
            def _():
                acc[...] += _tn(a_ref[...], b_ref[...])

        @pl.when(k == nk - 1)
        def _():
            out_ref[...] = acc[...].astype(BF16)

    in_specs, args = [], []
    for idx, (a, b) in enumerate(pairs):
        s0, n_ = starts[idx], nks[idx]
        in_specs.append(pl.BlockSpec((tls[idx], tk), lambda i, j, k, s0=s0, n_=n_: (jnp.clip(k - s0, 0, n_ - 1), i)))
        in_specs.append(pl.BlockSpec((tls[idx], tn), lambda i, j, k, s0=s0, n_=n_: (jnp.clip(k - s0, 0, n_ - 1), j)))
        args += [a, b]
    return pl.pallas_call(
        body, name=name, grid=(Ka // tk, Nb // tn, nk), in_specs=in_specs,
        out_specs=pl.BlockSpec((tk, tn), lambda i, j, k: (i, j)),
        out_shape=jax.ShapeDtypeStruct((Ka, Nb), BF16),
        scratch_shapes=[pltpu.VMEM((tk, tn), F32)], compiler_params=_params(3))(*args)


def _bwd_norm_mod(pairs, x, dres, g, sh, sc, name):
    M, D = x.shape
    K = pairs[0][0].shape[1]
    tm, tk = _tile(M, 512, 8), _tile(K, 1152 if len(pairs) == 1 else 512)
    nk = K // tk
    npair = len(pairs)
    has_res = dres is not None

    def body(*refs):
        pr = refs[:2 * npair]
        rest = refs[2 * npair:]
        if has_res:
            x_ref, dres_ref, g_ref, sh_ref, sc_ref, dx_ref, st_ref, acc = rest
        else:
            x_ref, g_ref, sh_ref, sc_ref, dx_ref, st_ref, acc = rest
        del sh_ref
        i, k = pl.program_id(0), pl.program_id(1)

        @pl.when(jnp.logical_and(i == 0, k == 0))
        def _():
            st_ref[...] = jnp.zeros_like(st_ref)

        @pl.when(k == 0)
        def _():
            acc[...] = jnp.zeros_like(acc)

        for idx in range(npair):
            acc[...] += _nt(pr[2 * idx + 1][...], pr[2 * idx][...])

        @pl.when(k == nk - 1)
        def _():
            xv = x_ref[...]
            gv = g_ref[...]
            dh = acc[...].T
            r = lax.rsqrt(jnp.mean(xv * xv, axis=-1, keepdims=True) + NORM_EPS)
            xh = xv * r
            st_ref[0:1, :] += jnp.sum(dh, axis=0, keepdims=True)
            st_ref[1:2, :] += jnp.sum(dh * (xh * gv), axis=0, keepdims=True)
            dn = dh * (1.0 + sc_ref[...])
            st_ref[2:3, :] += jnp.sum(dn * xh, axis=0, keepdims=True)
            dxh = dn * gv
            d = r * (dxh - xh * jnp.mean(dxh * xh, axis=-1, keepdims=True))
            if has_res:
                d = d + dres_ref[...]
            dx_ref[...] = d

    row = pl.BlockSpec((tm, D), lambda i, k: (i, 0))
    vec = pl.BlockSpec((1, D), lambda i, k: (0, 0))
    in_specs, args = [], []
    for dA, w in pairs:
        in_specs += [pl.BlockSpec((tm, tk), lambda i, k: (i, k)), pl.BlockSpec((D, tk), lambda i, k: (0, k))]
        args += [dA, w]
    in_specs += [row] + ([row] if has_res else []) + [vec, vec, vec]
    args += [x] + ([dres] if has_res else []) + [g, sh, sc]
    return pl.pallas_call(
        body, name=name, grid=(M // tm, nk), in_specs=in_specs,
        out_specs=[row, pl.BlockSpec((8, D), lambda i, k: (0, 0))],
        out_shape=[jax.ShapeDtypeStruct((M, D), F32), jax.ShapeDtypeStruct((8, D), F32)],
        scratch_shapes=[pltpu.VMEM((D, tm), F32)], compiler_params=_params(2))(*args)


def _local_step(x, ctx, tgt, mod, modc, norm_mix, norm_ffn, norm_final, lg, sink, w_in, w_out, w_gate, w_up, w_down):
    L, D = x.shape
    Lc = ctx.shape[0]
    d_proj = w_in.shape[1]
    npairs = RET_HEADS // 2
    nkv = ATT_KV_HEADS
    nkvp = nkv // 2
    o_rq = 0
    o_rk = o_rq + RET_HEADS * RET_DK // LANES
    o_rv = o_rk + RET_HEADS * RET_DK // LANES
    o_rg = o_rv + RET_HEADS * RET_DV // LANES
    o_aq = o_rg + RET_HEADS * RET_DV // LANES
    o_ak = o_aq + ATT_HEADS * ATT_DH // LANES
    o_av = o_ak + nkv * ATT_DH // LANES
    assert (o_av + nkv * ATT_DH // LANES) * LANES == d_proj
    assert o_rv % 2 == 0 and o_rg % 2 == 0 and (RET_HEADS * RET_DV) % (2 * LANES) == 0
    rv_blk, rg_blk = o_rv // 2, o_rg // 2
    d_ret = RET_HEADS * RET_DV
    d_mix = d_ret + ATT_HEADS * ATT_DH
    att_blk = d_ret // (2 * LANES)
    k_scale = RET_DK ** -0.5
    a_scale = ATT_DH ** -0.5

    T = _rope_tables(L)
    Tc = dict(C=jnp.ones((Lc, LANES), F32), S=jnp.zeros((Lc, LANES), F32))
    row = lambda m, i: m[i:i + 1]
    sh_m, sc_m, gt_m, sh_f, sc_f, gt_f = [row(mod, i) for i in range(6)]
    sh_mc, sc_mc = row(modc, 0), row(modc, 1)

    P, hx = _norm_mod_matmul(x, norm_mix, sh_m, sc_m, w_in, "in_proj")
    Pc, hc = _norm_mod_matmul(ctx, norm_mix, sh_mc, sc_mc, w_in, "in_proj_ctx")
    nq = RET_HEADS * RET_DK // LANES
    Qr = _rope_cols(P, o_rq, nq, T["Cr"], T["Sr"], T["Rr"], 1.0, True, "rope_rq")
    Kr = _rope_cols(P, o_rk, nq, T["Cr"], T["Sr"], T["Rr"], k_scale, True, "rope_rk")
    Krc = _rope_cols(Pc, o_rk, nq, Tc["C"], Tc["S"], T["Rr"], k_scale, False, "scale_rk_ctx")
    Qa = _rope_cols(P, o_aq, ATT_HEADS * ATT_DH // LANES, T["Ca"], T["Sa"], T["Ra"], a_scale, True, "rope_aq")
    Kd = _dup_heads(P, o_ak, nkvp, T["Ca"], T["Sa"], T["Ra"], T["D0"], T["D1"], True, "dup_ak")
    Vd = _dup_heads(P, o_av, nkvp, T["Ca"], T["Sa"], T["Ra"], T["D0"], T["D1"], False, "dup_av")
    Kdc = _dup_heads(Pc, o_ak, nkvp, Tc["C"], Tc["S"], T["Ra"], T["D0"], T["D1"], False, "dup_ak_ctx")
    Vdc = _dup_heads(Pc, o_av, nkvp, Tc["C"], Tc["S"], T["Ra"], T["D0"], T["D1"], False, "dup_av_ctx")

    SF = _ret_states_fwd(Kr, P, Krc, Pc, lg, rv_blk, npairs)
    Y, SB = _ret_out_fwd(Qr, Kr, P, Krc, Pc, SF, lg, rv_blk, rg_blk, npairs, d_mix)
    Y = _att_fwd(Qa, Kd, Vd, Kdc, Vdc, sink, Y, att_blk)

    x1, O1 = _proj_residual(Y, w_out, x, gt_m, "out_proj")
    G, U, A, h2 = _ffn_in(x1, norm_ffn, sh_f, sc_f, w_gate, w_up)
    x2, Fo = _proj_residual(A, w_down, x1, gt_f, "ffn_out")
    dx2, loss, d_norm_final = _final(x2, norm_final, tgt)

    dG, dU, dz2, dgt_f = _bwd_proj(dx2, gt_f, w_down, Fo, G, U, name="ffn_out_bwd")
    g_w_down = _tn_matmul([(A, dz2)], "grad_w_down")
    dx1, st_f = _bwd_norm_mod([(dG, w_gate), (dU, w_up)], x1, dx2, norm_ffn, sh_f, sc_f, "ffn_in_bwd")
    g_w_gate = _tn_matmul([(h2, dG)], "grad_w_gate")
    g_w_up = _tn_matmul([(h2, dU)], "grad_w_up")
    dY, dz1, dgt_m = _bwd_proj(dx1, gt_m, w_out, O1, name="out_proj_bwd")
    g_w_out = _tn_matmul([(Y, dz1)], "grad_w_out")

    dQa, dKp, dKs, dKn, dVp, dVs, dVn, dKdc, dVdc, dsink = _att_bwd(Qa, Kd, Vd, Kdc, Vdc, sink, dY, att_blk)
    dQr, dKr, dVr, dP, dO, dKc, dVc, acc1 = _ret_bwd1(Qr, Kr, P, Krc, Pc, SF, SB, dY, lg, rv_blk, rg_blk, npairs, d_proj)
    dKr, dP, dKc, dVc, acc2 = _ret_bwd2(Qr, Kr, P, Krc, Pc, SB, dO, dKr, dVr, dP, dKc, dVc, lg, rv_blk, npairs)

    dP = _unrope_cols(dQr, dP, o_rq, nq, T["Cr"], T["Sr"], T["RrT"], 1.0, True, "unrope_rq")
    dP = _unrope_cols(dKr, dP, o_rk, nq, T["Cr"], T["Sr"], T["RrT"], k_scale, True, "unrope_rk")
    dP = _unrope_cols(dQa, dP, o_aq, ATT_HEADS * ATT_DH // LANES, T["Ca"], T["Sa"], T["RaT"], a_scale, True, "unrope_aq")
    dP = _fold_heads([(dKs, 0), (dKp, 1), (dKn, -1)], dP, o_ak, nkvp, T["Ca"], T["Sa"], T["RaT"], T["D0T"], T["D1T"],
                     True, "fold_ak")
    dP = _fold_heads([(dVs, 0), (dVp, 1), (dVn, -1)], dP, o_av, nkvp, T["Ca"], T["Sa"], T["RaT"], T["D0T"], T["D1T"],
                     False, "fold_av")
    dPc = jnp.zeros((Lc, d_proj), BF16)
    dPc = _unrope_cols(dKc, dPc, o_rk, nq, Tc["C"], Tc["S"], T["RrT"], k_scale, False, "ctx_rk_bwd")
    dPc = _unrope_cols(dVc, dPc, o_rv, RET_HEADS * RET_DV // LANES, Tc["C"], Tc["S"], T["RrT"], 1.0, False, "ctx_rv_bwd")
    dPc = _fold_heads([(dKdc.astype(BF16), 0)], dPc, o_ak, nkvp, Tc["C"], Tc["S"], T["RaT"], T["D0T"], T["D1T"],
                      False, "fold_ak_ctx")
    dPc = _fold_heads([(dVdc.astype(BF16), 0)], dPc, o_av, nkvp, Tc["C"], Tc["S"], T["RaT"], T["D0T"], T["D1T"],
                      False, "fold_av_ctx")

    dx, st_m = _bwd_norm_mod([(dP, w_in)], x, dx1, norm_mix, sh_m, sc_m, "in_proj_bwd")
    _, st_mc = _bwd_norm_mod([(dPc, w_in)], ctx, None, norm_mix, sh_mc, sc_mc, "in_proj_ctx_bwd")
    g_w_in = _tn_matmul([(hx, dP), (hc, dPc)], "grad_w_in")

    a1 = acc1[:, :, :, 0].reshape(RET_HEADS, ACC_ROWS)
    a2 = acc2[:, :, :, 0].reshape(RET_HEADS, ACC_ROWS)
    dlam = a1[:, 0] + a1[:, 2] + a1[:, 3] + a1[:, 4]
    dmu = a1[:, 1] + a1[:, 5] + a2[:, 0] + a2[:, 1]
    d_ret_decay = jnp.stack([dlam, dmu]) * lg
    d_sink = dsink[:, :4, 0].reshape(1, ATT_HEADS)

    zero = jnp.zeros((1, D), F32)
    dmod = jnp.concatenate([st_m[0:1], st_m[1:2], dgt_m, st_f[0:1], st_f[1:2], dgt_f], axis=0)
    dmodc = jnp.concatenate([st_mc[0:1], st_mc[1:2], zero, zero, zero, zero], axis=0)
    return dict(loss=loss[0, 0], grad_x=dx, dmod=dmod, dmodc=dmodc,
                d_norm_mix=st_m[2:3] + st_mc[2:3], d_norm_ffn=st_f[2:3], d_norm_final=d_norm_final,
                d_ret_decay=d_ret_decay, d_sink=d_sink,
                g_w_in=g_w_in, g_w_out=g_w_out, g_w_gate=g_w_gate, g_w_up=g_w_up, g_w_down=g_w_down)


def _my_pos():
    return lax.axis_index("x"), lax.axis_index("y"), lax.axis_index("c")


def _other_chips(x, y):
    return [(1 - x, y), (x, 1 - y), (1 - x, 1 - y)]


def _remote(src, dst, ssem, rsem, dev):
    return pltpu.make_async_remote_copy(src_ref=src, dst_ref=dst, send_sem=ssem, recv_sem=rsem,
                                        device_id=dev, device_id_type=MESH)


def _allgather8(v, name):
    R, Cc = v.shape

    def body(v_ref, out_ref, send_sems, recv_sems):
        x, y, c = _my_pos()
        me = 4 * x + 2 * y + c
        out_ref[pl.ds(me, 1)] = v_ref[...][None]
        peers = []
        for j in range(1, N_DEV):
            peers.append((1 - x if (j >> 2) & 1 else x, 1 - y if (j >> 1) & 1 else y, 1 - c if j & 1 else c))
        copies = []
        for j, peer in enumerate(peers):
            cp = _remote(v_ref, out_ref.at[me], send_sems.at[j], recv_sems.at[j], peer)
            cp.start()
            copies.append(cp)
        for j, peer in enumerate(peers):
            pid = 4 * peer[0] + 2 * peer[1] + peer[2]
            _remote(v_ref, out_ref.at[pid], send_sems.at[j], recv_sems.at[j], peer).wait_recv()
        for cp in copies:
            cp.wait_send()

    return pl.pallas_call(
        body, name=name, out_shape=jax.ShapeDtypeStruct((N_DEV, R, Cc), v.dtype),
        in_specs=[pl.BlockSpec(memory_space=pltpu.VMEM)], out_specs=pl.BlockSpec(memory_space=pltpu.VMEM),
        scratch_shapes=[pltpu.SemaphoreType.DMA((N_DEV - 1,)), pltpu.SemaphoreType.DMA((N_DEV - 1,))])(v)


def _region(ref, k, half, shard_shape, axis):
    r, cs = shard_shape
    hr = r // 2
    if axis == 1:
        return ref.at[pl.ds(pl.multiple_of(half * hr, 16), hr), pl.ds(pl.multiple_of(k * cs, LANES), cs)]
    return ref.at[pl.ds(pl.multiple_of(k * r + half * hr, 16), hr), :]


def _full_shape(shard_shape, axis):
    r, cs = shard_shape
    return (r, N_CHIPS * cs) if axis == 1 else (N_CHIPS * r, cs)


def _allgather_weights(fulls, shapes, axes):
    nw = len(fulls)

    def body(*refs):
        out_refs = refs[nw:2 * nw]
        send1, recv1, send2, recv2 = refs[2 * nw:]
        x, y, c = _my_pos()
        k0 = 2 * x + y
        chips = _other_chips(x, y)
        sends = []
        for w in range(nw):
            own = _region(out_refs[w], k0, c, shapes[w], axes[w])
            for j, ch in enumerate(chips):
                cp = _remote(own, own, send1.at[w, j], recv1.at[w, j], (ch[0], ch[1], c))
                cp.start()
                sends.append(cp)
        for w in range(nw):
            for j, ch in enumerate(chips):
                kj = 2 * ch[0] + ch[1]
                got = _region(out_refs[w], kj, c, shapes[w], axes[w])
                _remote(got, got, send1.at[w, j], recv1.at[w, j], (ch[0], ch[1], c)).wait_recv()
                fw = _remote(got, got, send2.at[w, j], recv2.at[w, j], (x, y, 1 - c))
                fw.start()
                sends.append(fw)
        for w in range(nw):
            for j, ch in enumerate(chips):
                kj = 2 * ch[0] + ch[1]
                got = _region(out_refs[w], kj, 1 - c, shapes[w], axes[w])
                _remote(got, got, send2.at[w, j], recv2.at[w, j], (x, y, 1 - c)).wait_recv()
        for cp in sends:
            cp.wait_send()

    anyspec = pl.BlockSpec(memory_space=pl.ANY)
    sem = lambda: pltpu.SemaphoreType.DMA((nw, 3))
    return pl.pallas_call(
        body, name="allgather_weights",
        out_shape=[jax.ShapeDtypeStruct(f.shape, BF16) for f in fulls],
        in_specs=[anyspec] * nw, out_specs=[anyspec] * nw,
        input_output_aliases={w: w for w in range(nw)},
        scratch_shapes=[sem(), sem(), sem(), sem()])(*fulls)


def _half_pieces(ref, half, shard_shape, axis):
    r, cs = shard_shape
    hr = r // 2
    if axis == 1:
        return [ref.at[pl.ds(pl.multiple_of(half * hr, 16), hr), :]]
    return [ref.at[pl.ds(pl.multiple_of(k * r + half * hr, 16), hr), :] for k in range(N_CHIPS)]


def _rs_sibling(grads, shapes, axes):
    nw = len(grads)
    npc = max(1 if a == 1 else N_CHIPS for a in axes)

    def body(*refs):
        g_refs, out_refs = refs[:nw], refs[nw:2 * nw]
        send, recv = refs[2 * nw:]
        x, y, c = _my_pos()
        sib = (x, y, 1 - c)
        copies = []
        for w in range(nw):
            src = _half_pieces(g_refs[w], 1 - c, shapes[w], axes[w])
            dst = _half_pieces(out_refs[w], 1 - c, shapes[w], axes[w])
            for i, (s, d) in enumerate(zip(src, dst)):
                cp = _remote(s, d, send.at[w, i], recv.at[w, i], sib)
                cp.start()
                copies.append(cp)
        for w in range(nw):
            mine = _half_pieces(out_refs[w], c, shapes[w], axes[w])
            for i, d in enumerate(mine):
                _remote(d, d, send.at[w, i], recv.at[w, i], sib).wait_recv()
        for cp in copies:
            cp.wait_send()

    anyspec = pl.BlockSpec(memory_space=pl.ANY)
    return pl.pallas_call(
        body, name="rs_sibling",
        out_shape=[jax.ShapeDtypeStruct(_full_shape(s, a), BF16) for s, a in zip(shapes, axes)],
        in_specs=[anyspec] * nw, out_specs=[anyspec] * nw,
        scratch_shapes=[pltpu.SemaphoreType.DMA((nw, npc)), pltpu.SemaphoreType.DMA((nw, npc))])(*grads)


def _half_block_spec(shard_shape, axis, tr):
    r, cs = shard_shape
    hr = r // 2
    if axis == 1:
        return pl.BlockSpec((tr, cs), lambda k, i, c_ref: (c_ref[0] * (hr // tr) + i, k))
    return pl.BlockSpec((tr, cs), lambda k, i, c_ref: (k * (r // tr) + c_ref[0] * (hr // tr) + i, 0))


def _add_halves(g, recv, cvec, shard_shape, axis, name):
    r, cs = shard_shape
    hr = r // 2
    tr = _tile(hr, 256, 16)

    def body(c_ref, a_ref, b_ref, o_ref):
        del c_ref
        o_ref[0] = (a_ref[...].astype(F32) + b_ref[...].astype(F32)).astype(BF16)

    spec = _half_block_spec(shard_shape, axis, tr)
    return pl.pallas_call(
        body, name=name,
        grid_spec=pltpu.PrefetchScalarGridSpec(
            num_scalar_prefetch=1, grid=(N_CHIPS, hr // tr), in_specs=[spec, spec],
            out_specs=pl.BlockSpec((1, tr, cs), lambda k, i, c_ref: (k, i, 0))),
        out_shape=jax.ShapeDtypeStruct((N_CHIPS, hr, cs), BF16),
        compiler_params=_params(2, False))(cvec, g, recv)


def _rs_chips(sums):
    nw = len(sums)

    def body(*refs):
        s_refs, out_refs = refs[:nw], refs[nw:2 * nw]
        send, recv = refs[2 * nw:]
        x, y, c = _my_pos()
        k0 = 2 * x + y
        chips = _other_chips(x, y)
        copies = []
        for w in range(nw):
            for j, ch in enumerate(chips):
                kj = 2 * ch[0] + ch[1]
                cp = _remote(s_refs[w].at[kj], out_refs[w].at[k0], send.at[w, j], recv.at[w, j], (ch[0], ch[1], c))
                cp.start()
                copies.append(cp)
        for w in range(nw):
            for j, ch in enumerate(chips):
                kj = 2 * ch[0] + ch[1]
                d = out_refs[w].at[kj]
                _remote(d, d, send.at[w, j], recv.at[w, j], (ch[0], ch[1], c)).wait_recv()
        for cp in copies:
            cp.wait_send()

    anyspec = pl.BlockSpec(memory_space=pl.ANY)
    return pl.pallas_call(
        body, name="rs_chips",
        out_shape=[jax.ShapeDtypeStruct(s.shape, BF16) for s in sums],
        in_specs=[anyspec] * nw, out_specs=[anyspec] * nw,
        scratch_shapes=[pltpu.SemaphoreType.DMA((nw, 3)), pltpu.SemaphoreType.DMA((nw, 3))])(*sums)


def _sum_chips(sums, landed, kc, name):
    _, hr, cs = sums.shape
    tr = _tile(hr, 256, 16)

    def body(kc_ref, own_ref, a_ref, b_ref, c_ref, o_ref):
        del kc_ref
        o_ref[...] = (own_ref[0].astype(F32) + a_ref[0].astype(F32)) + (b_ref[0].astype(F32) + c_ref[0].astype(F32))

    slot = lambda j: pl.BlockSpec((1, tr, cs), lambda i, kc_ref: ((kc_ref[0] + j) % N_CHIPS, i, 0))
    return pl.pallas_call(
        body, name=name,
        grid_spec=pltpu.PrefetchScalarGridSpec(
            num_scalar_prefetch=1, grid=(hr // tr,), in_specs=[slot(0), slot(1), slot(2), slot(3)],
            out_specs=pl.BlockSpec((tr, cs), lambda i, kc_ref: (kc_ref[1] * (hr // tr) + i, 0))),
        out_shape=jax.ShapeDtypeStruct((2 * hr, cs), F32),
        compiler_params=_params(1, False))(kc, sums, landed, landed, landed)


def _exchange_halves(shards):
    nw = len(shards)

    def body(*refs):
        out_refs = refs[nw:2 * nw]
        send, recv = refs[2 * nw:]
        x, y, c = _my_pos()
        sib = (x, y, 1 - c)
        copies = []
        for w in range(nw):
            hr = shards[w].shape[0] // 2
            mine = out_refs[w].at[pl.ds(pl.multiple_of(c * hr, 8), hr), :]
            cp = _remote(mine, mine, send.at[w], recv.at[w], sib)
            cp.start()
            copies.append(cp)
        for w in range(nw):
            hr = shards[w].shape[0] // 2
            other = out_refs[w].at[pl.ds(pl.multiple_of((1 - c) * hr, 8), hr), :]
            _remote(other, other, send.at[w], recv.at[w], sib).wait_recv()
        for cp in copies:
            cp.wait_send()

    anyspec = pl.BlockSpec(memory_space=pl.ANY)
    return pl.pallas_call(
        body, name="exchange_halves",
        out_shape=[jax.ShapeDtypeStruct(s.shape, F32) for s in shards],
        in_specs=[anyspec] * nw, out_specs=[anyspec] * nw,
        input_output_aliases={w: w for w in range(nw)},
        scratch_shapes=[pltpu.SemaphoreType.DMA((nw,)), pltpu.SemaphoreType.DMA((nw,))])(*shards)


def _cast_into_full(w, kc, axis, name):
    r, cs = w.shape
    tr = _tile(r, 256, 16)

    def body(kc_ref, w_ref, o_ref):
        del kc_ref
        o_ref[...] = w_ref[...].astype(BF16)

    if axis == 1:
        ospec = pl.BlockSpec((tr, cs), lambda i, kc_ref: (i, kc_ref[0]))
    else:
        ospec = pl.BlockSpec((tr, cs), lambda i, kc_ref: (kc_ref[0] * (r // tr) + i, 0))
    return pl.pallas_call(
        body, name=name,
        grid_spec=pltpu.PrefetchScalarGridSpec(
            num_scalar_prefetch=1, grid=(r // tr,), in_specs=[pl.BlockSpec((tr, cs), lambda i, kc_ref: (i, 0))],
            out_specs=ospec),
        out_shape=jax.ShapeDtypeStruct(_full_shape((r, cs), axis), BF16),
        compiler_params=_params(1, False))(kc, w)


def _adam_math(w, g, m, v):
    m2 = ADAM_B1 * m + (1.0 - ADAM_B1) * g
    v2 = ADAM_B2 * v + (1.0 - ADAM_B2) * (g * g)
    m_hat = m2 / (1.0 - ADAM_B1 ** ADAM_STEP)
    v_hat = v2 / (1.0 - ADAM_B2 ** ADAM_STEP)
    delta = -ADAM_LR * (m_hat / (jnp.sqrt(v_hat) + ADAM_EPS) + ADAM_WD * w)
    return delta, m2, v2


def _adam(w, g, m, v, name):
    r, cs = w.shape
    tr = _tile(r, 256, 8)

    def body(w_ref, g_ref, m_ref, v_ref, d_ref, m2_ref, v2_ref):
        d, m2, v2 = _adam_math(w_ref[...], g_ref[...], m_ref[...], v_ref[...])
        d_ref[...] = d
        m2_ref[...] = m2
        v2_ref[...] = v2

    spec = pl.BlockSpec((tr, cs), lambda i: (i, 0))
    shp = jax.ShapeDtypeStruct((r, cs), F32)
    return pl.pallas_call(body, name=name, grid=(r // tr,), in_specs=[spec] * 4, out_specs=[spec] * 3,
                          out_shape=[shp, shp, shp], compiler_params=_params(1, False))(w, g, m, v)


def _mod_rows(a16, w, b, name):
    D, n = w.shape
    tn = _tile(n, 512)

    def body(a_ref, w_ref, b_ref, o_ref):
        a = a_ref[...]
        o_ref[...] = _nn((a * _sigmoid(a)).astype(BF16), w_ref[...].astype(BF16)) + b_ref[...]

    return pl.pallas_call(
        body, name=name, grid=(n // tn,),
        in_specs=[pl.BlockSpec((16, D), lambda j: (0, 0)), pl.BlockSpec((D, tn), lambda j: (0, j)),
                  pl.BlockSpec((1, tn), lambda j: (0, j))],
        out_specs=pl.BlockSpec((16, tn), lambda j: (0, j)),
        out_shape=jax.ShapeDtypeStruct((16, n), F32), compiler_params=_params(1, False))(a16, w, b)


def _w_mod_update(a16, d16, w, m, v):
    D, n = w.shape
    tn = _tile(n, 256)

    def body(a_ref, d_ref, w_ref, m_ref, v_ref, g_ref, dl_ref, m2_ref, v2_ref, p_ref):
        @pl.when(pl.program_id(0) == 0)
        def _():
            p_ref[...] = jnp.zeros_like(p_ref)
        a = a_ref[...]
        db = d_ref[...].astype(BF16)
        wv = w_ref[...]
        g = _tn((a * _sigmoid(a)).astype(BF16), db)
        g_ref[...] = g
        d, m2, v2 = _adam_math(wv, g, m_ref[...], v_ref[...])
        dl_ref[...] = d
        m2_ref[...] = m2
        v2_ref[...] = v2
        p_ref[...] += _nt(db, wv.astype(BF16))

    wspec = pl.BlockSpec((D, tn), lambda j: (0, j))
    shp = jax.ShapeDtypeStruct((D, n), F32)
    return pl.pallas_call(
        body, name="w_mod_update", grid=(n // tn,),
        in_specs=[pl.BlockSpec((16, D), lambda j: (0, 0)), pl.BlockSpec((16, tn), lambda j: (0, j)), wspec, wspec, wspec],
        out_specs=[wspec, wspec, wspec, wspec, pl.BlockSpec((16, D), lambda j: (0, 0))],
        out_shape=[shp, shp, shp, shp, jax.ShapeDtypeStruct((16, D), F32)],
        compiler_params=_params(1))(a16, d16, w, m, v)


def _sum_devices(g8, name):
    _, R, Cc = g8.shape

    def body(g_ref, o_ref):
        t = g_ref[0]
        for d in range(1, N_DEV):
            t = t + g_ref[d]
        o_ref[...] = t

    return pl.pallas_call(body, name=name, out_shape=jax.ShapeDtypeStruct((R, Cc), F32))(g8)


def _c_ctx_grad(parts, c_ctx):
    D = c_ctx.shape[1]

    def body(p_ref, c_ref, o_ref):
        t = p_ref[0]
        for k in range(1, N_CHIPS):
            t = t + p_ref[2 * k]
        cv = c_ref[...]
        sg = _sigmoid(cv)
        o_ref[...] = t * (sg * (1.0 + cv * (1.0 - sg)))

    return pl.pallas_call(body, name="c_ctx_grad", out_shape=jax.ShapeDtypeStruct((1, D), F32))(parts, c_ctx)


LOSS_LANE = 64


def kernel(x, c, ctx, c_ctx, w_mod, b_mod, norm_mix, norm_ffn, w_in, ret_decay, attn_sink, w_out, w_gate, w_up, w_down, norm_final, loss_target, m_c_ctx, m_w_mod, m_b_mod, m_norm_mix, m_norm_ffn, m_w_in, m_ret_decay, m_attn_sink, m_w_out, m_w_gate, m_w_up, m_w_down, m_norm_final, v_c_ctx, v_w_mod, v_b_mod, v_norm_mix, v_norm_ffn, v_w_in, v_ret_decay, v_attn_sink, v_w_out, v_w_gate, v_w_up, v_w_down, v_norm_final):
    D = x.shape[-1]
    n3 = w_mod.shape[-1]
    xi, yi, ci = _my_pos()
    b = 4 * xi + 2 * yi + ci
    k0 = 2 * xi + yi
    cvec = jnp.reshape(ci, (1,)).astype(jnp.int32)
    kc = jnp.stack([k0, ci]).astype(jnp.int32)

    dense = [("w_in", w_in[0], 1), ("w_out", w_out[0], 0), ("w_gate", w_gate[0], 1), ("w_up", w_up[0], 1),
             ("w_down", w_down[0], 0)]
    axes = [a for _, _, a in dense]
    shapes = [w.shape for _, w, _ in dense]
    own16 = [_cast_into_full(w, kc, a, "cast_" + n) for n, w, a in dense]
    f_in, f_out, f_gate, f_up, f_down = _allgather_weights(own16, shapes, axes)

    c_all = _allgather8(c, "gather_c").reshape(N_DEV, D)
    c_ctx2 = c_ctx.reshape(1, D)
    a16 = jnp.concatenate([c_all, c_ctx2, jnp.zeros((16 - N_DEV - 1, D), F32)], axis=0)
    b_cols = lax.dynamic_slice_in_dim(b_mod, k0 * n3, n3, axis=1)
    mod16 = _mod_rows(a16, w_mod[0], b_cols, "mod_rows")
    mod_all = _allgather8(mod16, "gather_mod")
    mine = jnp.stack([lax.dynamic_index_in_dim(mod_all, 2 * k + ci, 0, keepdims=False) for k in range(N_CHIPS)])
    mod = lax.dynamic_index_in_dim(mine, b, 1, keepdims=False).reshape(6, D)
    modc = mine[:, N_DEV].reshape(6, D)

    lg = -jnp.exp(ret_decay[0])
    out = _local_step(x[0], ctx[0], loss_target[0], mod, modc, norm_mix, norm_ffn, norm_final.reshape(1, D), lg,
                      attn_sink, f_in, f_out, f_gate, f_up, f_down)

    grads16 = [out["g_w_in"], out["g_w_out"], out["g_w_gate"], out["g_w_up"], out["g_w_down"]]
    from_sib = _rs_sibling(grads16, shapes, axes)
    sums = [_add_halves(g, r, cvec, s, a, "add_halves_" + n)
            for g, r, s, a, (n, _, _) in zip(grads16, from_sib, shapes, axes, dense)]
    landed = _rs_chips(sums)
    halves = [_sum_chips(s, p, kc, "sum_chips_" + n) for s, p, (n, _, _) in zip(sums, landed, dense)]
    g_dense = _exchange_halves(halves)

    nh = 2 * RET_HEADS
    assert nh + ATT_HEADS <= LOSS_LANE
    misc = jnp.concatenate([out["d_ret_decay"].reshape(1, nh), out["d_sink"].reshape(1, ATT_HEADS),
                            jnp.zeros((1, LOSS_LANE - nh - ATT_HEADS), F32), out["loss"].reshape(1, 1),
                            jnp.zeros((1, D - LOSS_LANE - 1), F32)], axis=1)
    small = jnp.concatenate([out["dmod"], out["dmodc"], out["d_norm_mix"], out["d_norm_ffn"], out["d_norm_final"],
                             misc], axis=0)
    small_all = _allgather8(small, "gather_small")
    tot = _sum_devices(small_all, "sum_small")
    g_b_mod = (tot[0:6] + tot[6:12]).reshape(1, 6 * D)
    dmodc_tot = tot[6:12].reshape(1, 6 * D)
    dmod_rows = small_all[:, 0:6].reshape(N_DEV, 6 * D)
    d16 = jnp.concatenate([dmod_rows, dmodc_tot, jnp.zeros((16 - N_DEV - 1, 6 * D), F32)], axis=0)
    d16 = lax.dynamic_slice_in_dim(d16, k0 * n3, n3, axis=1)
    g_w_mod, dl_w_mod, m2_w_mod, v2_w_mod, part = _w_mod_update(a16, d16, w_mod[0], m_w_mod[0], v_w_mod[0])
    part_all = _allgather8(part[N_DEV:N_DEV + 1], "gather_c_ctx")
    g_c_ctx = _c_ctx_grad(part_all, c_ctx2)
    loss = tot[15, LOSS_LANE]
    g_ret_decay = tot[15, :nh].reshape(1, 2, RET_HEADS)
    g_sink = tot[15, nh:nh + ATT_HEADS].reshape(1, ATT_HEADS)

    def pad_row(v):
        v = v.reshape(1, -1)
        return jnp.concatenate([v, jnp.zeros((1, D - v.shape[1]), F32)], axis=1)

    def pack(cc, bm, nm, nf, nfin, rd, sk):
        rows = [bm.reshape(6, D), cc.reshape(1, D), nm.reshape(1, D), nf.reshape(1, D), nfin.reshape(1, D),
                pad_row(jnp.concatenate([rd.reshape(-1), sk.reshape(-1)])), jnp.zeros((5, D), F32)]
        return jnp.concatenate(rows, axis=0)

    w_s = pack(c_ctx, b_mod, norm_mix, norm_ffn, norm_final, ret_decay, attn_sink)
    g_s = pack(g_c_ctx, g_b_mod, tot[12], tot[13], tot[14], g_ret_decay, g_sink)
    m_s = pack(m_c_ctx, m_b_mod, m_norm_mix, m_norm_ffn, m_norm_final, m_ret_decay, m_attn_sink)
    v_s = pack(v_c_ctx, v_b_mod, v_norm_mix, v_norm_ffn, v_norm_final, v_ret_decay, v_attn_sink)
    small_upd = _adam(w_s, g_s, m_s, v_s, "adam_small")

    def unpack(t):
        return dict(b_mod=t[0:6].reshape(1, 6 * D), c_ctx=t[6], norm_mix=t[7:8], norm_ffn=t[8:9], norm_final=t[9],
                    ret_decay=t[10, :nh].reshape(1, 2, RET_HEADS), attn_sink=t[10, nh:nh + ATT_HEADS].reshape(1, ATT_HEADS))

    dense_w = dict(w_in=(w_in, m_w_in, v_w_in), w_out=(w_out, m_w_out, v_w_out), w_gate=(w_gate, m_w_gate, v_w_gate),
                   w_up=(w_up, m_w_up, v_w_up), w_down=(w_down, m_w_down, v_w_down))
    grads = dict(unpack(g_s), w_mod=g_w_mod[None])
    upd = [dict(unpack(t)) for t in small_upd]
    upd[0]["w_mod"], upd[1]["w_mod"], upd[2]["w_mod"] = dl_w_mod[None], m2_w_mod[None], v2_w_mod[None]
    for (n, _, _), g in zip(dense, g_dense):
        w_, m_, v_ = dense_w[n]
        res = _adam(w_[0], g, m_[0], v_[0], "adam_" + n)
        grads[n] = g[None]
        for u, r_ in zip(upd, res):
            u[n] = r_[None]

    order = ['c_ctx', 'w_mod', 'b_mod', 'norm_mix', 'norm_ffn', 'w_in', 'ret_decay', 'attn_sink', 'w_out', 'w_gate',
             'w_up', 'w_down', 'norm_final']
    outs = [loss, out["grad_x"][None]] + [grads[n] for n in order]
    for u in upd:
        outs += [u[n] for n in order]
    return tuple(outs)
```

```python
import functools
import numpy as np
import jax
import jax.numpy as jnp
from jax import lax
from jax.experimental import pallas as pl
from jax.experimental.pallas import tpu as pltpu

F32 = jnp.float32
BF16 = jnp.bfloat16

RET_HEADS = 8
RET_DK = 64
RET_DV = 128
CHUNK = 128
ATT_HEADS = 16
ATT_KV_HEADS = 4
ATT_DH = 64
GRID_W = 64
ROPE_BASE = 10000.0
NORM_EPS = 1e-6
ADAM_LR = 0.001
ADAM_B1 = 0.9
ADAM_B2 = 0.999
ADAM_EPS = 1e-08
ADAM_WD = 0.01
ADAM_STEP = 10
NEG = -1e30
LANES = 128
VMEM_LIMIT = 56 * 1024 * 1024
ROWS_PER_LATCH = 1024
MESH = pl.DeviceIdType.MESH
N_CHIPS = 4
N_DEV = 8


def _nn(a, b):
    return jnp.dot(a, b, preferred_element_type=F32)


def _nt(a, b):
    return lax.dot_general(a, b, (((1,), (1,)), ((), ())), preferred_element_type=F32)


def _tn(a, b):
    return lax.dot_general(a, b, (((0,), (0,)), ((), ())), preferred_element_type=F32)


def _tile(n, pref, unit=LANES):
    t = min(n, pref)
    t -= t % unit
    while t > unit and n % t:
        t -= unit
    if t <= 0 or n % t:
        return n
    return t


def _params(ndim, vmem=True):
    return pltpu.CompilerParams(dimension_semantics=("arbitrary",) * ndim,
                                vmem_limit_bytes=VMEM_LIMIT if vmem else None)


def _sigmoid(x):
    return 1.0 / (1.0 + jnp.exp(-x))


def _fsum(x):
    return jnp.sum(jnp.sum(x, axis=1, keepdims=True), axis=0, keepdims=True)


def _rope_tables(L):
    lane = np.arange(LANES)
    d = lane % 64
    inv_r = jnp.asarray(ROPE_BASE, F32) ** (-jnp.arange(32, dtype=F32) / 32)
    t = jnp.arange(L)
    ang_r = t.astype(F32)[:, None] * jnp.tile(inv_r, LANES // 32)[None, :]
    Rr = np.zeros((LANES, LANES), np.float32)
    for l in range(LANES):
        if d[l] < 32:
            Rr[l + 32, l] = -1.0
        else:
            Rr[l - 32, l] = 1.0
    inv_a = jnp.asarray(ROPE_BASE, F32) ** (-jnp.arange(16, dtype=F32) / 16)
    rows = (t // GRID_W).astype(F32)
    cols = (t % GRID_W).astype(F32)
    dd = d % 32
    pos = jnp.where(jnp.asarray(d < 32)[None, :], rows[:, None], cols[:, None])
    ang_a = pos * jnp.tile(inv_a, LANES // 16)[None, :]
    Ra = np.zeros((LANES, LANES), np.float32)
    for l in range(LANES):
        if dd[l] < 16:
            Ra[l + 16, l] = -1.0
        else:
            Ra[l - 16, l] = 1.0
    D0 = np.zeros((LANES, LANES), np.float32)
    D1 = np.zeros((LANES, LANES), np.float32)
    for l in range(LANES):
        D0[l % 64, l] = 1.0
        D1[64 + l % 64, l] = 1.0
    return dict(
        Cr=jnp.cos(ang_r), Sr=jnp.sin(ang_r), Rr=jnp.asarray(Rr, BF16), RrT=jnp.asarray(Rr.T, BF16),
        Ca=jnp.cos(ang_a), Sa=jnp.sin(ang_a), Ra=jnp.asarray(Ra, BF16), RaT=jnp.asarray(Ra.T, BF16),
        D0=jnp.asarray(D0, BF16), D1=jnp.asarray(D1, BF16),
        D0T=jnp.asarray(D0.T, BF16), D1T=jnp.asarray(D1.T, BF16))


def _norm_mod(xf, g, sh, sc):
    r = lax.rsqrt(jnp.mean(xf * xf, axis=-1, keepdims=True) + NORM_EPS)
    return (xf * r * g) * (1.0 + sc) + sh


def _norm_mod_matmul(x, g, sh, sc, w, name):
    M, D = x.shape
    N = w.shape[1]
    tm, tn = _tile(M, ROWS_PER_LATCH, 8), _tile(N, 512)

    def body(x_ref, g_ref, sh_ref, sc_ref, w_ref, p_ref, h_ref, hs):
        @pl.when(pl.program_id(1) == 0)
        def _():
            hb = _norm_mod(x_ref[...], g_ref[...], sh_ref[...], sc_ref[...]).astype(BF16)
            hs[...] = hb
            h_ref[...] = hb
        p_ref[...] = _nn(hs[...], w_ref[...]).astype(BF16)

    vec = pl.BlockSpec((1, D), lambda i, j: (0, 0))
    return pl.pallas_call(
        body, name=name, grid=(M // tm, N // tn),
        in_specs=[pl.BlockSpec((tm, D), lambda i, j: (i, 0)), vec, vec, vec,
                  pl.BlockSpec((D, tn), lambda i, j: (0, j))],
        out_specs=[pl.BlockSpec((tm, tn), lambda i, j: (i, j)), pl.BlockSpec((tm, D), lambda i, j: (i, 0))],
        out_shape=[jax.ShapeDtypeStruct((M, N), BF16), jax.ShapeDtypeStruct((M, D), BF16)],
        scratch_shapes=[pltpu.VMEM((tm, D), BF16)],
        compiler_params=_params(2))(x, g, sh, sc, w)


def _proj_residual(a, w, xres, gt, name):
    M, K = a.shape
    N = w.shape[1]
    tm, tn = _tile(M, ROWS_PER_LATCH, 8), _tile(N, 512)

    def body(a_ref, w_ref, x_ref, gt_ref, xo_ref, o_ref):
        o = _nn(a_ref[...], w_ref[...])
        o_ref[...] = o.astype(BF16)
        xo_ref[...] = x_ref[...] + gt_ref[...] * o

    return pl.pallas_call(
        body, name=name, grid=(M // tm, N // tn),
        in_specs=[pl.BlockSpec((tm, K), lambda i, j: (i, 0)), pl.BlockSpec((K, tn), lambda i, j: (0, j)),
                  pl.BlockSpec((tm, tn), lambda i, j: (i, j)), pl.BlockSpec((1, tn), lambda i, j: (0, j))],
        out_specs=[pl.BlockSpec((tm, tn), lambda i, j: (i, j)), pl.BlockSpec((tm, tn), lambda i, j: (i, j))],
        out_shape=[jax.ShapeDtypeStruct((M, N), F32), jax.ShapeDtypeStruct((M, N), BF16)],
        compiler_params=_params(2))(a, w, xres, gt)


def _ffn_in(x1, g, sh, sc, wg, wu):
    M, D = x1.shape
    N = wg.shape[1]
    tm, tn = _tile(M, ROWS_PER_LATCH, 8), _tile(N, 512)

    def body(x_ref, g_ref, sh_ref, sc_ref, wg_ref, wu_ref, G_ref, U_ref, A_ref, h_ref, hs):
        @pl.when(pl.program_id(1) == 0)
        def _():
            hb = _norm_mod(x_ref[...], g_ref[...], sh_ref[...], sc_ref[...]).astype(BF16)
            hs[...] = hb
            h_ref[...] = hb
        G = _nn(hs[...], wg_ref[...])
        U = _nn(hs[...], wu_ref[...])
        G_ref[...] = G.astype(BF16)
        U_ref[...] = U.astype(BF16)
        A_ref[...] = (G * _sigmoid(G) * U).astype(BF16)

    vec = pl.BlockSpec((1, D), lambda i, j: (0, 0))
    wspec = pl.BlockSpec((D, tn), lambda i, j: (0, j))
    ospec = pl.BlockSpec((tm, tn), lambda i, j: (i, j))
    big = jax.ShapeDtypeStruct((M, N), BF16)
    return pl.pallas_call(
        body, name="ffn_in", grid=(M // tm, N // tn),
        in_specs=[pl.BlockSpec((tm, D), lambda i, j: (i, 0)), vec, vec, vec, wspec, wspec],
        out_specs=[ospec, ospec, ospec, pl.BlockSpec((tm, D), lambda i, j: (i, 0))],
        out_shape=[big, big, big, jax.ShapeDtypeStruct((M, D), BF16)],
        scratch_shapes=[pltpu.VMEM((tm, D), BF16)],
        compiler_params=_params(2))(x1, g, sh, sc, wg, wu)


def _final(x2, gn, tgt):
    M, D = x2.shape
    tm = _tile(M, 256, 8)

    def body(x_ref, g_ref, t_ref, dx_ref, loss_ref, dg_ref):
        @pl.when(pl.program_id(0) == 0)
        def _():
            loss_ref[...] = jnp.zeros_like(loss_ref)
            dg_ref[...] = jnp.zeros_like(dg_ref)
        x = x_ref[...]
        g = g_ref[...]
        r = lax.rsqrt(jnp.mean(x * x, axis=-1, keepdims=True) + NORM_EPS)
        xh = x * r
        e = xh * g - t_ref[...]
        loss_ref[...] += (0.5 / D) * _fsum(e * e)
        dy = e * (1.0 / D)
        dg_ref[...] += jnp.sum(dy * xh, axis=0, keepdims=True)
        dxh = dy * g
        dx_ref[...] = r * (dxh - xh * jnp.mean(dxh * xh, axis=-1, keepdims=True))

    row = pl.BlockSpec((tm, D), lambda i: (i, 0))
    return pl.pallas_call(
        body, name="final_loss", grid=(M // tm,),
        in_specs=[row, pl.BlockSpec((1, D), lambda i: (0, 0)), row],
        out_specs=[row, pl.BlockSpec((1, LANES), lambda i: (0, 0)), pl.BlockSpec((1, D), lambda i: (0, 0))],
        out_shape=[jax.ShapeDtypeStruct((M, D), F32), jax.ShapeDtypeStruct((1, LANES), F32),
                   jax.ShapeDtypeStruct((1, D), F32)],
        compiler_params=_params(1))(x2, gn, tgt)


def _col_group(blk0, nblk):
    return int(np.gcd(blk0, nblk)) if blk0 else nblk


def _rope_cols(src, blk0, nblk, Ct, St, R, scale, rope, name):
    M = src.shape[0]
    tm = _tile(M, 512, 8)
    wb = _col_group(blk0, nblk)

    def body(x_ref, c_ref, s_ref, r_ref, o_ref):
        for j in range(wb):
            cols = slice(j * LANES, (j + 1) * LANES)
            x = x_ref[:, cols]
            xf = x.astype(F32)
            if rope:
                xf = xf * c_ref[...] + _nn(x.astype(BF16), r_ref[...]) * s_ref[...]
            o_ref[:, cols] = (xf * scale).astype(BF16)

    tab = pl.BlockSpec((tm, LANES), lambda i, j: (i, 0))
    return pl.pallas_call(
        body, name=name, grid=(M // tm, nblk // wb),
        in_specs=[pl.BlockSpec((tm, wb * LANES), lambda i, j: (i, blk0 // wb + j)), tab, tab,
                  pl.BlockSpec((LANES, LANES), lambda i, j: (0, 0))],
        out_specs=pl.BlockSpec((tm, wb * LANES), lambda i, j: (i, j)),
        out_shape=jax.ShapeDtypeStruct((M, nblk * LANES), BF16),
        compiler_params=_params(2, False))(src, Ct, St, R)


def _dup_heads(src, blk0, npair, Ct, St, R, D0, D1, rope, name):
    M = src.shape[0]
    tm = _tile(M, 512, 8)

    def body(x_ref, c_ref, s_ref, r_ref, d0_ref, d1_ref, o_ref):
        x = x_ref[...]
        if rope:
            x = (x.astype(F32) * c_ref[...] + _nn(x, r_ref[...]) * s_ref[...]).astype(BF16)
        o_ref[0] = _nn(x, d0_ref[...]).astype(BF16)
        o_ref[1] = _nn(x, d1_ref[...]).astype(BF16)

    tab = pl.BlockSpec((tm, LANES), lambda i, p: (i, 0))
    mat = pl.BlockSpec((LANES, LANES), lambda i, p: (0, 0))
    return pl.pallas_call(
        body, name=name, grid=(M // tm, npair),
        in_specs=[pl.BlockSpec((tm, LANES), lambda i, p: (i, blk0 + p)), tab, tab, mat, mat, mat],
        out_specs=pl.BlockSpec((2, tm, LANES), lambda i, p: (p, i, 0)),
        out_shape=jax.ShapeDtypeStruct((2 * npair, M, LANES), BF16),
        compiler_params=_params(2, False))(src, Ct, St, R, D0, D1)


def _unrope_cols(dsrc, dst, blk0, nblk, Ct, St, RT, scale, rope, name):
    M = dsrc.shape[0]
    tm = _tile(M, 512, 8)
    wb = _col_group(blk0, nblk)

    def body(x_ref, c_ref, s_ref, r_ref, dst_ref, o_ref):
        del dst_ref
        for j in range(wb):
            cols = slice(j * LANES, (j + 1) * LANES)
            xf = x_ref[:, cols].astype(F32)
            if rope:
                xf = xf * c_ref[...] + _nn((xf * s_ref[...]).astype(BF16), r_ref[...])
            o_ref[:, cols] = (xf * scale).astype(BF16)

    tab = pl.BlockSpec((tm, LANES), lambda i, j: (i, 0))
    return pl.pallas_call(
        body, name=name, grid=(M // tm, nblk // wb),
        in_specs=[pl.BlockSpec((tm, wb * LANES), lambda i, j: (i, j)), tab, tab,
                  pl.BlockSpec((LANES, LANES), lambda i, j: (0, 0)),
                  pl.BlockSpec(memory_space=pl.ANY)],
        out_specs=pl.BlockSpec((tm, wb * LANES), lambda i, j: (i, blk0 // wb + j)),
        out_shape=jax.ShapeDtypeStruct(dst.shape, dst.dtype),
        input_output_aliases={4: 0},
        compiler_params=_params(2, False))(dsrc, Ct, St, RT, dst)


def _fold_heads(parts, dst, blk0, npair, Ct, St, RT, D0T, D1T, rope, name):
    M = parts[0][0].shape[1]
    nb = M // CHUNK
    R = _tile(M, 1024, CHUNK)
    rb = R // CHUNK
    nrefs = sum(1 if s == 0 else 2 for _, s in parts)

    def body(*refs):
        part_refs = list(refs[:nrefs])
        c_ref, s_ref, r_ref, d0_ref, d1_ref, dst_ref, o_ref = refs[nrefs:]
        del dst_ref
        i = pl.program_id(0)
        tot = [jnp.zeros((R, LANES), F32), jnp.zeros((R, LANES), F32)]
        for _, shift in parts:
            main = part_refs.pop(0)
            if shift == 0:
                for e in range(2):
                    tot[e] = tot[e] + main[e].astype(F32)
                continue
            edge = part_refs.pop(0)
            ok = (i + 1) * rb <= nb - 1 if shift > 0 else i > 0
            for e in range(2):
                ed = jnp.where(ok, edge[e].astype(F32), 0.0)
                if rb == 1:
                    tot[e] = tot[e] + ed
                elif shift > 0:
                    tot[e] = tot[e] + jnp.concatenate([main[e, CHUNK:, :].astype(F32), ed], axis=0)
                else:
                    tot[e] = tot[e] + jnp.concatenate([ed, main[e, :R - CHUNK, :].astype(F32)], axis=0)
        f = _nn(tot[0].astype(BF16), d0_ref[...]) + _nn(tot[1].astype(BF16), d1_ref[...])
        if rope:
            f = f * c_ref[...] + _nn((f * s_ref[...]).astype(BF16), r_ref[...])
        o_ref[...] = f.astype(BF16)

    in_specs, args = [], []
    for a, shift in parts:
        assert shift in (-1, 0, 1)
        in_specs.append(pl.BlockSpec((2, R, LANES), lambda i, p: (p, i, 0)))
        args.append(a)
        if shift > 0:
            in_specs.append(pl.BlockSpec((2, CHUNK, LANES), lambda i, p: (p, jnp.minimum((i + 1) * rb, nb - 1), 0)))
            args.append(a)
        elif shift < 0:
            in_specs.append(pl.BlockSpec((2, CHUNK, LANES), lambda i, p: (p, jnp.maximum(i * rb - 1, 0), 0)))
            args.append(a)
    tab = pl.BlockSpec((R, LANES), lambda i, p: (i, 0))
    mat = pl.BlockSpec((LANES, LANES), lambda i, p: (0, 0))
    return pl.pallas_call(
        body, name=name, grid=(M // R, npair),
        in_specs=in_specs + [tab, tab, mat, mat, mat, pl.BlockSpec(memory_space=pl.ANY)],
        out_specs=pl.BlockSpec((R, LANES), lambda i, p: (i, blk0 + p)),
        out_shape=jax.ShapeDtypeStruct(dst.shape, dst.dtype),
        input_output_aliases={nrefs + 5: 0},
        compiler_params=_params(2, False))(*args, Ct, St, RT, D0T, D1T, dst)


def _head_masks():
    lane = lax.broadcasted_iota(jnp.int32, (1, LANES), 1)
    return [lane < 64, lane >= 64]


def _decay_vecs(lam, mu):
    i = lax.broadcasted_iota(jnp.int32, (CHUNK, 1), 0).astype(F32)
    return dict(qf=jnp.exp(lam * (i + 1.0)), kf=jnp.exp(lam * (CHUNK - 1.0 - i)),
                qb=jnp.exp(mu * (CHUNK - i)), kb=jnp.exp(mu * i),
                gf=jnp.exp(lam * float(CHUNK)), gb=jnp.exp(mu * float(CHUNK)), i=i)


def _decay_mask(lam, mu):
    r = lax.broadcasted_iota(jnp.int32, (CHUNK, CHUNK), 0)
    c = lax.broadcasted_iota(jnp.int32, (CHUNK, CHUNK), 1)
    rel = (r - c).astype(F32)
    low = rel >= 0.0
    mf = jnp.exp(lam * jnp.maximum(rel, 0.0))
    mb = jnp.exp(mu * jnp.maximum(-rel, 0.0))
    return jnp.where(low, mf, mb), rel, low


def _lam_of(lg_ref, row, idx):
    return jnp.full((1, 1), lg_ref[row, idx], F32)


def _ret_states_fwd(Kr, P, Krc, Pc, lg, rv_blk, npairs):
    L = Kr.shape[0]
    Lc = Krc.shape[0]
    N, ncc = L // CHUNK, Lc // CHUNK

    def body(lg_ref, k_ref, v_ref, kc_ref, vc_ref, sf_ref, S):
        p, n = pl.program_id(0), pl.program_id(1)
        masks = _head_masks()
        for h in range(2):
            lam = _lam_of(lg_ref, 0, 2 * p + h)
            dv = _decay_vecs(lam, lam)
            hm = masks[h]

            @pl.when(n == 0)
            def _():
                s = jnp.zeros((LANES, LANES), F32)
                for cc in range(ncc):
                    kw = jnp.where(hm, kc_ref[cc * CHUNK:(cc + 1) * CHUNK, :].astype(F32) * dv["kf"], 0.0).astype(BF16)
                    s = dv["gf"] * s + _tn(kw, vc_ref[cc * CHUNK:(cc + 1) * CHUNK, h * LANES:(h + 1) * LANES])
                S[h] = s

            s = S[h]
            sf_ref[0, 0, h] = s.astype(BF16)
            kw = jnp.where(hm, k_ref[...].astype(F32) * dv["kf"], 0.0).astype(BF16)
            S[h] = dv["gf"] * s + _tn(kw, v_ref[:, h * LANES:(h + 1) * LANES])

    return pl.pallas_call(
        body, name="ret_states_fwd", grid=(npairs, N),
        in_specs=[pl.BlockSpec(memory_space=pltpu.SMEM),
                  pl.BlockSpec((CHUNK, LANES), lambda p, n: (n, p)),
                  pl.BlockSpec((CHUNK, 2 * LANES), lambda p, n: (n, rv_blk + p)),
                  pl.BlockSpec((Lc, LANES), lambda p, n: (0, p)),
                  pl.BlockSpec((Lc, 2 * LANES), lambda p, n: (0, rv_blk + p))],
        out_specs=pl.BlockSpec((1, 1, 2, LANES, LANES), lambda p, n: (p, n, 0, 0, 0)),
        out_shape=jax.ShapeDtypeStruct((npairs, N, 2, LANES, LANES), BF16),
        scratch_shapes=[pltpu.VMEM((2, LANES, LANES), F32)],
        compiler_params=_params(2, False))(lg, Kr, P, Krc, Pc)


def _ret_chunk_fwd(q, k, v, sf, sb, hm, lam, mu):
    dv = _decay_vecs(lam, mu)
    Mk, rel, low = _decay_mask(lam, mu)
    qm = jnp.where(hm, q, jnp.zeros_like(q))
    qmf = qm.astype(F32)
    A = _nt(qm, k)
    Am = A * Mk
    Amb = Am.astype(BF16)
    Qf = (qmf * dv["qf"]).astype(BF16)
    Qb = (qmf * dv["qb"]).astype(BF16)
    O = _nn(Amb, v) + _nn(Qf, sf) + _nn(Qb, sb)
    return dict(dv=dv, Mk=Mk, rel=rel, low=low, qm=qm, Am=Am, Amb=Amb, Qf=Qf, Qb=Qb, O=O)


def _ret_out_fwd(Qr, Kr, P, Krc, Pc, SF, lg, rv_blk, rg_blk, npairs, d_mix):
    L = Qr.shape[0]
    Lc = Krc.shape[0]
    N, ncc = L // CHUNK, Lc // CHUNK

    def body(lg_ref, q_ref, k_ref, v_ref, g_ref, sf_ref, kc_ref, vc_ref, y_ref, sb_ref, S):
        p, n = pl.program_id(0), pl.program_id(1)
        masks = _head_masks()
        for h in range(2):
            lam = _lam_of(lg_ref, 0, 2 * p + h)
            mu = _lam_of(lg_ref, 1, 2 * p + h)
            hm = masks[h]
            dvb = _decay_vecs(lam, mu)

            @pl.when(n == 0)
            def _():
                s = jnp.zeros((LANES, LANES), F32)
                for cc in reversed(range(ncc)):
                    kw = jnp.where(hm, kc_ref[cc * CHUNK:(cc + 1) * CHUNK, :].astype(F32) * dvb["kb"], 0.0).astype(BF16)
                    s = dvb["gb"] * s + _tn(kw, vc_ref[cc * CHUNK:(cc + 1) * CHUNK, h * LANES:(h + 1) * LANES])
                S[h] = s

            s = S[h]
            sbb = s.astype(BF16)
            sb_ref[0, 0, h] = sbb
            v = v_ref[:, h * LANES:(h + 1) * LANES]
            f = _ret_chunk_fwd(q_ref[...], k_ref[...], v, sf_ref[0, 0, h], sbb, hm, lam, mu)
            O = f["O"]
            r = lax.rsqrt(jnp.mean(O * O, axis=-1, keepdims=True) + NORM_EPS)
            g = g_ref[:, h * LANES:(h + 1) * LANES].astype(F32)
            y_ref[:, h * LANES:(h + 1) * LANES] = (O * r * (g * _sigmoid(g))).astype(BF16)
            kw = jnp.where(hm, k_ref[...].astype(F32) * dvb["kb"], 0.0).astype(BF16)
            S[h] = dvb["gb"] * s + _tn(kw, v)

    rev = lambda n: N - 1 - n
    return pl.pallas_call(
        body, name="ret_out_fwd", grid=(npairs, N),
        in_specs=[pl.BlockSpec(memory_space=pltpu.SMEM),
                  pl.BlockSpec((CHUNK, LANES), lambda p, n: (rev(n), p)),
                  pl.BlockSpec((CHUNK, LANES), lambda p, n: (rev(n), p)),
                  pl.BlockSpec((CHUNK, 2 * LANES), lambda p, n: (rev(n), rv_blk + p)),
                  pl.BlockSpec((CHUNK, 2 * LANES), lambda p, n: (rev(n), rg_blk + p)),
                  pl.BlockSpec((1, 1, 2, LANES, LANES), lambda p, n: (p, rev(n), 0, 0, 0)),
                  pl.BlockSpec((Lc, LANES), lambda p, n: (0, p)),
                  pl.BlockSpec((Lc, 2 * LANES), lambda p, n: (0, rv_blk + p))],
        out_specs=[pl.BlockSpec((CHUNK, 2 * LANES), lambda p, n: (rev(n), p)),
                   pl.BlockSpec((1, 1, 2, LANES, LANES), lambda p, n: (p, rev(n), 0, 0, 0))],
        out_shape=[jax.ShapeDtypeStruct((L, d_mix), BF16),
                   jax.ShapeDtypeStruct((npairs, N, 2, LANES, LANES), BF16)],
        scratch_shapes=[pltpu.VMEM((2, LANES, LANES), F32)],
        compiler_params=_params(2, False))(lg, Qr, Kr, P, P, SF, Krc, Pc)


ACC_ROWS = 8


def _ret_bwd1(Qr, Kr, P, Krc, Pc, SF, SB, dY, lg, rv_blk, rg_blk, npairs, d_proj):
    L = Qr.shape[0]
    Lc = Krc.shape[0]
    N, ncc = L // CHUNK, Lc // CHUNK

    def body(lg_ref, q_ref, k_ref, v_ref, g_ref, sf_ref, sb_ref, dy_ref, kc_ref, vc_ref,
             dq_ref, dk_ref, dv_ref, dg_ref, do_ref, dkc_ref, dvc_ref, acc_ref, dS, T):
        p, n = pl.program_id(0), pl.program_id(1)
        masks = _head_masks()

        @pl.when(n == 0)
        def _():
            dS[...] = jnp.zeros_like(dS)
            T[...] = jnp.zeros_like(T)
            acc_ref[...] = jnp.zeros_like(acc_ref)

        dq_tot = jnp.zeros((CHUNK, LANES), F32)
        dk_tot = jnp.zeros((CHUNK, LANES), F32)
        for h in range(2):
            lam = _lam_of(lg_ref, 0, 2 * p + h)
            mu = _lam_of(lg_ref, 1, 2 * p + h)
            hm = masks[h]
            hs = slice(h * LANES, (h + 1) * LANES)
            v = v_ref[:, hs]
            k = k_ref[...]
            sf = sf_ref[0, 0, h]
            sb = sb_ref[0, 0, h]
            f = _ret_chunk_fwd(q_ref[...], k, v, sf, sb, hm, lam, mu)
            dv_, O = f["dv"], f["O"]
            r = lax.rsqrt(jnp.mean(O * O, axis=-1, keepdims=True) + NORM_EPS)
            on = O * r
            g = g_ref[:, hs].astype(F32)
            sg = _sigmoid(g)
            dy = dy_ref[:, hs].astype(F32)
            dg_ref[:, hs] = (dy * on * (sg * (1.0 + g * (1.0 - sg)))).astype(BF16)
            don = dy * (g * sg)
            dO = r * (don - on * jnp.mean(don * on, axis=-1, keepdims=True))
            dOb = dO.astype(BF16)
            do_ref[:, hs] = dOb
            dAm = _nt(dOb, v)
            T[h] += dAm * f["Am"]
            dAb = (dAm * f["Mk"]).astype(BF16)
            km = jnp.where(hm, k, jnp.zeros_like(k))
            dq = _nn(dAb, km)
            dk = _tn(dAb, f["qm"])
            dvh = _tn(f["Amb"], dOb)
            dQf = _nt(dOb, sf)
            dQb = _nt(dOb, sb)
            dq = dq + dQf * dv_["qf"] + dQb * dv_["qb"]
            acc_ref[0, h, 0:1, :] += _fsum(dQf * f["Qf"].astype(F32) * (dv_["i"] + 1.0))
            acc_ref[0, h, 1:2, :] += _fsum(dQb * f["Qb"].astype(F32) * (CHUNK - dv_["i"]))
            dSh = dS[h]
            dSb_ = dSh.astype(BF16)
            Kf = (km.astype(F32) * dv_["kf"]).astype(BF16)
            dKf = _nt(v, dSb_)
            dk = dk + jnp.where(hm, dKf * dv_["kf"], 0.0)
            acc_ref[0, h, 2:3, :] += _fsum(jnp.where(hm, dKf, 0.0) * Kf.astype(F32) * (CHUNK - 1.0 - dv_["i"]))
            dvh = dvh + _nn(Kf, dSb_)
            acc_ref[0, h, 3:4, :] += float(CHUNK) * dv_["gf"] * _fsum(dSh * sf.astype(F32))
            dSh = dv_["gf"] * dSh + _tn(f["Qf"], dOb)
            dS[h] = dSh
            dv_ref[:, hs] = dvh
            dq_tot = dq_tot + dq
            dk_tot = dk_tot + dk

        dq_ref[...] = dq_tot
        dk_ref[...] = dk_tot

        @pl.when(n == N - 1)
        def _():
            for h in range(2):
                lam = _lam_of(lg_ref, 0, 2 * p + h)
                dv_ = _decay_vecs(lam, lam)
                hm = masks[h]
                hs = slice(h * LANES, (h + 1) * LANES)
                states = [jnp.zeros((LANES, LANES), F32)]
                kws = []
                for cc in range(ncc):
                    kw = jnp.where(hm, kc_ref[cc * CHUNK:(cc + 1) * CHUNK, :].astype(F32) * dv_["kf"], 0.0).astype(BF16)
                    kws.append(kw)
                    states.append(dv_["gf"] * states[-1] + _tn(kw, vc_ref[cc * CHUNK:(cc + 1) * CHUNK, hs]))
                d = dS[h]
                for cc in reversed(range(ncc)):
                    db = d.astype(BF16)
                    rows = slice(cc * CHUNK, (cc + 1) * CHUNK)
                    dKf_c = jnp.where(hm, _nt(vc_ref[rows, hs], db), 0.0)
                    part = dKf_c * dv_["kf"]
                    if h == 0:
                        dkc_ref[rows, :] = part
                    else:
                        dkc_ref[rows, :] += part
                    acc_ref[0, h, 2:3, :] += _fsum(dKf_c * kws[cc].astype(F32) * (CHUNK - 1.0 - dv_["i"]))
                    dvc_ref[rows, hs] = _nn(kws[cc], db)
                    acc_ref[0, h, 3:4, :] += float(CHUNK) * dv_["gf"] * _fsum(d * states[cc])
                    d = dv_["gf"] * d
                _, rel, low = _decay_mask(lam, lam)
                Th = T[h]
                acc_ref[0, h, 4:5, :] += _fsum(jnp.where(low, Th * rel, 0.0))
                acc_ref[0, h, 5:6, :] += _fsum(jnp.where(low, 0.0, -Th * rel))

    rev = lambda n: N - 1 - n
    st = pl.BlockSpec((1, 1, 2, LANES, LANES), lambda p, n: (p, rev(n), 0, 0, 0))
    pair = pl.BlockSpec((CHUNK, LANES), lambda p, n: (rev(n), p))
    wide = lambda b0: pl.BlockSpec((CHUNK, 2 * LANES), lambda p, n: (rev(n), b0 + p))
    return pl.pallas_call(
        body, name="ret_bwd_desc", grid=(npairs, N),
        in_specs=[pl.BlockSpec(memory_space=pltpu.SMEM), pair, pair, wide(rv_blk), wide(rg_blk), st, st, wide(0),
                  pl.BlockSpec((Lc, LANES), lambda p, n: (0, p)),
                  pl.BlockSpec((Lc, 2 * LANES), lambda p, n: (0, rv_blk + p))],
        out_specs=[pair, pair, wide(0), wide(rg_blk), wide(0),
                   pl.BlockSpec((Lc, LANES), lambda p, n: (0, p)),
                   pl.BlockSpec((Lc, 2 * LANES), lambda p, n: (0, p)),
                   pl.BlockSpec((1, 2, ACC_ROWS, LANES), lambda p, n: (p, 0, 0, 0))],
        out_shape=[jax.ShapeDtypeStruct((L, npairs * LANES), F32),
                   jax.ShapeDtypeStruct((L, npairs * LANES), F32),
                   jax.ShapeDtypeStruct((L, npairs * 2 * LANES), F32),
                   jax.ShapeDtypeStruct((L, d_proj), BF16),
                   jax.ShapeDtypeStruct((L, npairs * 2 * LANES), BF16),
                   jax.ShapeDtypeStruct((Lc, npairs * LANES), F32),
                   jax.ShapeDtypeStruct((Lc, npairs * 2 * LANES), F32),
                   jax.ShapeDtypeStruct((npairs, 2, ACC_ROWS, LANES), F32)],
        scratch_shapes=[pltpu.VMEM((2, LANES, LANES), F32), pltpu.VMEM((2, CHUNK, CHUNK), F32)],
        compiler_params=_params(2, False))(lg, Qr, Kr, P, P, SF, SB, dY, Krc, Pc)


def _ret_bwd2(Qr, Kr, P, Krc, Pc, SB, dO, dKr, dVp, dP, dKc, dVc, lg, rv_blk, npairs):
    L = Qr.shape[0]
    Lc = Krc.shape[0]
    N, ncc = L // CHUNK, Lc // CHUNK

    def body(lg_ref, q_ref, k_ref, v_ref, sb_ref, do_ref, dkin_ref, dvin_ref, kc_ref, vc_ref, dkcin_ref, dvcin_ref,
             dpin_ref, dk_ref, dv_ref, dkc_ref, dvc_ref, acc_ref, dS):
        del dpin_ref
        p, n = pl.program_id(0), pl.program_id(1)
        masks = _head_masks()

        @pl.when(n == 0)
        def _():
            dS[...] = jnp.zeros_like(dS)
            acc_ref[...] = jnp.zeros_like(acc_ref)

        dk_tot = dkin_ref[...]
        for h in range(2):
            mu = _lam_of(lg_ref, 1, 2 * p + h)
            hm = masks[h]
            hs = slice(h * LANES, (h + 1) * LANES)
            dv_ = _decay_vecs(mu, mu)
            v = v_ref[:, hs]
            k = k_ref[...]
            q = q_ref[...]
            dOb = do_ref[:, hs]
            km = jnp.where(hm, k, jnp.zeros_like(k)).astype(F32)
            Kb = (km * dv_["kb"]).astype(BF16)
            Qb = (jnp.where(hm, q, jnp.zeros_like(q)).astype(F32) * dv_["qb"]).astype(BF16)
            dSh = dS[h]
            dSb_ = dSh.astype(BF16)
            dKb = jnp.where(hm, _nt(v, dSb_), 0.0)
            dk_tot = dk_tot + dKb * dv_["kb"]
            acc_ref[0, h, 0:1, :] += _fsum(dKb * Kb.astype(F32) * dv_["i"])
            dv_ref[:, hs] = (dvin_ref[:, hs] + _nn(Kb, dSb_)).astype(BF16)
            acc_ref[0, h, 1:2, :] += float(CHUNK) * dv_["gb"] * _fsum(dSh * sb_ref[0, 0, h].astype(F32))
            dSh = dv_["gb"] * dSh + _tn(Qb, dOb)
            dS[h] = dSh

        dk_ref[...] = dk_tot

        @pl.when(n == N - 1)
        def _():
            for h in range(2):
                mu = _lam_of(lg_ref, 1, 2 * p + h)
                hm = masks[h]
                hs = slice(h * LANES, (h + 1) * LANES)
                dv_ = _decay_vecs(mu, mu)
                states = {}
                kws = {}
                s = jnp.zeros((LANES, LANES), F32)
                for cc in reversed(range(ncc)):
                    states[cc] = s
                    kw = jnp.where(hm, kc_ref[cc * CHUNK:(cc + 1) * CHUNK, :].astype(F32) * dv_["kb"], 0.0).astype(BF16)
                    kws[cc] = kw
                    s = dv_["gb"] * s + _tn(kw, vc_ref[cc * CHUNK:(cc + 1) * CHUNK, hs])
                d = dS[h]
                for cc in range(ncc):
                    db = d.astype(BF16)
                    rows = slice(cc * CHUNK, (cc + 1) * CHUNK)
                    dKb_c = jnp.where(hm, _nt(vc_ref[rows, hs], db), 0.0)
                    part = dKb_c * dv_["kb"]
                    if h == 0:
                        dkc_ref[rows, :] = dkcin_ref[rows, :] + part
                    else:
                        dkc_ref[rows, :] += part
                    acc_ref[0, h, 0:1, :] += _fsum(dKb_c * kws[cc].astype(F32) * dv_["i"])
                    dvc_ref[rows, hs] = dvcin_ref[rows, hs] + _nn(kws[cc], db)
                    acc_ref[0, h, 1:2, :] += float(CHUNK) * dv_["gb"] * _fsum(d * states[cc])
                    d = dv_["gb"] * d

    st = pl.BlockSpec((1, 1, 2, LANES, LANES), lambda p, n: (p, n, 0, 0, 0))
    pair = pl.BlockSpec((CHUNK, LANES), lambda p, n: (n, p))
    wide = lambda b0: pl.BlockSpec((CHUNK, 2 * LANES), lambda p, n: (n, b0 + p))
    ckc = pl.BlockSpec((Lc, LANES), lambda p, n: (0, p))
    cvc = lambda b0: pl.BlockSpec((Lc, 2 * LANES), lambda p, n: (0, b0 + p))
    return pl.pallas_call(
        body, name="ret_bwd_asc", grid=(npairs, N),
        in_specs=[pl.BlockSpec(memory_space=pltpu.SMEM), pair, pair, wide(rv_blk), st, wide(0), pair, wide(0),
                  ckc, cvc(rv_blk), ckc, cvc(0), pl.BlockSpec(memory_space=pl.ANY)],
        out_specs=[pair, wide(rv_blk), ckc, cvc(0),
                   pl.BlockSpec((1, 2, ACC_ROWS, LANES), lambda p, n: (p, 0, 0, 0))],
        out_shape=[jax.ShapeDtypeStruct(dKr.shape, F32),
                   jax.ShapeDtypeStruct(dP.shape, dP.dtype),
                   jax.ShapeDtypeStruct(dKc.shape, F32),
                   jax.ShapeDtypeStruct(dVc.shape, F32),
                   jax.ShapeDtypeStruct((npairs, 2, ACC_ROWS, LANES), F32)],
        input_output_aliases={12: 1},
        scratch_shapes=[pltpu.VMEM((2, LANES, LANES), F32)],
        compiler_params=_params(2, False))(lg, Qr, Kr, P, SB, dO, dKr, dVp, Krc, Pc, dKc, dVc, dP)


def _att_valid(n, N):
    row = lax.broadcasted_iota(jnp.int32, (CHUNK, 3 * CHUNK), 0)
    col = lax.broadcasted_iota(jnp.int32, (CHUNK, 3 * CHUNK), 1)
    ok = jnp.logical_and(col >= row, col <= row + 2 * CHUNK)
    ok = jnp.logical_and(ok, jnp.logical_or(col >= CHUNK, n > 0))
    ok = jnp.logical_and(ok, jnp.logical_or(col < 2 * CHUNK, n < N - 1))
    return ok


def _att_probs(qa, K, Kc, valid, snk):
    s_loc = jnp.where(valid, _nt(qa, K), NEG)
    s_ctx = _nt(qa, Kc)
    mx = jnp.maximum(jnp.maximum(jnp.max(s_loc, axis=1, keepdims=True), jnp.max(s_ctx, axis=1, keepdims=True)), snk)
    p_loc = jnp.exp(s_loc - mx)
    p_ctx = jnp.exp(s_ctx - mx)
    p_snk = jnp.exp(snk - mx)
    inv = 1.0 / (jnp.sum(p_loc, axis=1, keepdims=True) + jnp.sum(p_ctx, axis=1, keepdims=True) + p_snk)
    return p_loc, p_ctx, p_snk, inv


def _att_specs(Lc, N):
    q = pl.BlockSpec((CHUNK, 2 * LANES), lambda g, n: (n, g))
    kv = lambda s: pl.BlockSpec((1, CHUNK, LANES), lambda g, n: (g, jnp.clip(n + s, 0, N - 1), 0))
    ctx = pl.BlockSpec((1, Lc, LANES), lambda g, n: (g, 0, 0))
    return q, kv, ctx


def _att_fwd(Qa, Kd, Vd, Kdc, Vdc, sink, Y, blk0):
    L = Qa.shape[0]
    Lc = Kdc.shape[1]
    N = L // CHUNK
    nkv = Kd.shape[0]

    def body(sink_ref, q_ref, kp, kc_, kn, vp, vc_, vn, kctx, vctx, y_in, o_ref):
        del y_in
        g, n = pl.program_id(0), pl.program_id(1)
        K = jnp.concatenate([kp[0], kc_[0], kn[0]], axis=0)
        V = jnp.concatenate([vp[0], vc_[0], vn[0]], axis=0)
        Kc, Vc = kctx[0], vctx[0]
        valid = _att_valid(n, N)
        masks = _head_masks()
        for pr in range(2):
            qp = q_ref[:, pr * LANES:(pr + 1) * LANES]
            acc = jnp.zeros((CHUNK, LANES), F32)
            for a in range(2):
                hm = masks[a]
                qa = jnp.where(hm, qp, jnp.zeros_like(qp))
                snk = jnp.full((1, 1), sink_ref[0, g * 4 + pr * 2 + a], F32)
                p_loc, p_ctx, _, inv = _att_probs(qa, K, Kc, valid, snk)
                Va = jnp.where(hm, V, jnp.zeros_like(V))
                Vca = jnp.where(hm, Vc, jnp.zeros_like(Vc))
                acc = acc + (_nn(p_loc.astype(BF16), Va) + _nn(p_ctx.astype(BF16), Vca)) * inv
            o_ref[:, pr * LANES:(pr + 1) * LANES] = acc.astype(BF16)

    q, kv, ctx = _att_specs(Lc, N)
    return pl.pallas_call(
        body, name="att_fwd", grid=(nkv, N),
        in_specs=[pl.BlockSpec(memory_space=pltpu.SMEM), q, kv(-1), kv(0), kv(1), kv(-1), kv(0), kv(1), ctx, ctx,
                  pl.BlockSpec(memory_space=pl.ANY)],
        out_specs=pl.BlockSpec((CHUNK, 2 * LANES), lambda g, n: (n, blk0 + g)),
        out_shape=jax.ShapeDtypeStruct(Y.shape, Y.dtype),
        input_output_aliases={10: 0},
        compiler_params=_params(2, False))(sink, Qa, Kd, Kd, Kd, Vd, Vd, Vd, Kdc, Vdc, Y)


def _att_bwd(Qa, Kd, Vd, Kdc, Vdc, sink, dY, blk0):
    L = Qa.shape[0]
    Lc = Kdc.shape[1]
    N = L // CHUNK
    nkv = Kd.shape[0]

    def body(sink_ref, q_ref, kp, kc_, kn, vp, vc_, vn, kctx, vctx, dy_ref,
             dq_ref, dkp, dkc_, dkn, dvp, dvc_, dvn, dkctx, dvctx, dsink_ref):
        g, n = pl.program_id(0), pl.program_id(1)

        @pl.when(n == 0)
        def _():
            dkctx[...] = jnp.zeros_like(dkctx)
            dvctx[...] = jnp.zeros_like(dvctx)
            dsink_ref[...] = jnp.zeros_like(dsink_ref)

        K = jnp.concatenate([kp[0], kc_[0], kn[0]], axis=0)
        V = jnp.concatenate([vp[0], vc_[0], vn[0]], axis=0)
        Kc, Vc = kctx[0], vctx[0]
        valid = _att_valid(n, N)
        masks = _head_masks()
        dK = jnp.zeros((3 * CHUNK, LANES), F32)
        dV = jnp.zeros((3 * CHUNK, LANES), F32)
        dKc = jnp.zeros((Lc, LANES), F32)
        dVc = jnp.zeros((Lc, LANES), F32)
        for pr in range(2):
            qp = q_ref[:, pr * LANES:(pr + 1) * LANES]
            dyp = dy_ref[:, pr * LANES:(pr + 1) * LANES]
            dq = jnp.zeros((CHUNK, LANES), F32)
            for a in range(2):
                hm = masks[a]
                qa = jnp.where(hm, qp, jnp.zeros_like(qp))
                do = jnp.where(hm, dyp, jnp.zeros_like(dyp))
                snk = jnp.full((1, 1), sink_ref[0, g * 4 + pr * 2 + a], F32)
                p_loc, p_ctx, p_snk, inv = _att_probs(qa, K, Kc, valid, snk)
                P_loc, P_ctx = p_loc * inv, p_ctx * inv
                dp_loc = _nt(do, V)
                dp_ctx = _nt(do, Vc)
                delta = jnp.sum(P_loc * dp_loc, axis=1, keepdims=True) + jnp.sum(P_ctx * dp_ctx, axis=1, keepdims=True)
                ds_loc = (P_loc * (dp_loc - delta)).astype(BF16)
                ds_ctx = (P_ctx * (dp_ctx - delta)).astype(BF16)
                row = pr * 2 + a
                dsink_ref[0, row:row + 1, :] += _fsum(-(p_snk * inv) * delta)
                dq = dq + jnp.where(hm, _nn(ds_loc, K) + _nn(ds_ctx, Kc), 0.0)
                dK = dK + _tn(ds_loc, qa)
                dKc = dKc + _tn(ds_ctx, qa)
                dV = dV + _tn(P_loc.astype(BF16), do)
                dVc = dVc + _tn(P_ctx.astype(BF16), do)
            dq_ref[:, pr * LANES:(pr + 1) * LANES] = dq
        for j, (rk, rv) in enumerate([(dkp, dvp), (dkc_, dvc_), (dkn, dvn)]):
            rk[0] = dK[j * CHUNK:(j + 1) * CHUNK].astype(BF16)
            rv[0] = dV[j * CHUNK:(j + 1) * CHUNK].astype(BF16)
        dkctx[0] += dKc
        dvctx[0] += dVc

    q, kv, ctx = _att_specs(Lc, N)
    blk = pl.BlockSpec((1, CHUNK, LANES), lambda g, n: (g, n, 0))
    part = jax.ShapeDtypeStruct((nkv, L, LANES), BF16)
    cshape = jax.ShapeDtypeStruct((nkv, Lc, LANES), F32)
    return pl.pallas_call(
        body, name="att_bwd", grid=(nkv, N),
        in_specs=[pl.BlockSpec(memory_space=pltpu.SMEM), q, kv(-1), kv(0), kv(1), kv(-1), kv(0), kv(1), ctx, ctx,
                  pl.BlockSpec((CHUNK, 2 * LANES), lambda g, n: (n, blk0 + g))],
        out_specs=[q, blk, blk, blk, blk, blk, blk, ctx, ctx,
                   pl.BlockSpec((1, 8, LANES), lambda g, n: (g, 0, 0))],
        out_shape=[jax.ShapeDtypeStruct(Qa.shape, F32), part, part, part, part, part, part, cshape, cshape,
                   jax.ShapeDtypeStruct((nkv, 8, LANES), F32)],
        compiler_params=_params(2, False))(sink, Qa, Kd, Kd, Kd, Vd, Vd, Vd, Kdc, Vdc, dY)


def _bwd_proj(dx, gt, w, saved, G=None, U=None, name="bwd_proj"):
    M, D = dx.shape
    N = w.shape[0]
    swiglu = G is not None
    tm, tn = _tile(M, ROWS_PER_LATCH, 8), _tile(N, 256 if swiglu else 512)

    def body(*refs):
        if swiglu:
            dx_ref, gt_ref, w_ref, sv_ref, G_ref, U_ref, dG_ref, dU_ref, dz_ref, dgt_ref, zs = refs
        else:
            dx_ref, gt_ref, w_ref, sv_ref, dA_ref, dz_ref, dgt_ref, zs = refs
        i, j = pl.program_id(0), pl.program_id(1)

        @pl.when(jnp.logical_and(i == 0, j == 0))
        def _():
            dgt_ref[...] = jnp.zeros_like(dgt_ref)

        @pl.when(j == 0)
        def _():
            d = dx_ref[...]
            z = (d * gt_ref[...]).astype(BF16)
            zs[...] = z
            dz_ref[...] = z
            dgt_ref[...] += jnp.sum(d * sv_ref[...].astype(F32), axis=0, keepdims=True)

        dA = _nt(zs[...], w_ref[...])
        if swiglu:
            Gv = G_ref[...].astype(F32)
            Uv = U_ref[...].astype(F32)
            sg = _sigmoid(Gv)
            dU_ref[...] = (dA * Gv * sg).astype(BF16)
            dG_ref[...] = (dA * Uv * (sg * (1.0 + Gv * (1.0 - sg)))).astype(BF16)
        else:
            dA_ref[...] = dA.astype(BF16)

    row = pl.BlockSpec((tm, D), lambda i, j: (i, 0))
    vec = pl.BlockSpec((1, D), lambda i, j: (0, 0))
    tile = pl.BlockSpec((tm, tn), lambda i, j: (i, j))
    big = jax.ShapeDtypeStruct((M, N), BF16)
    in_specs = [row, vec, pl.BlockSpec((tn, D), lambda i, j: (j, 0)), row]
    args = [dx, gt, w, saved]
    if swiglu:
        in_specs += [tile, tile]
        args += [G, U]
        out_specs = [tile, tile, row, vec]
        out_shape = [big, big, jax.ShapeDtypeStruct((M, D), BF16), jax.ShapeDtypeStruct((1, D), F32)]
    else:
        out_specs = [tile, row, vec]
        out_shape = [big, jax.ShapeDtypeStruct((M, D), BF16), jax.ShapeDtypeStruct((1, D), F32)]
    return pl.pallas_call(
        body, name=name, grid=(M // tm, N // tn), in_specs=in_specs, out_specs=out_specs, out_shape=out_shape,
        scratch_shapes=[pltpu.VMEM((tm, D), BF16)], compiler_params=_params(2))(*args)


def _tn_matmul(pairs, name):
    Ka, Nb = pairs[0][0].shape[1], pairs[0][1].shape[1]
    tk, tn = _tile(Ka, 2048), _tile(Nb, 2048)
    tls, nks = [], []
    for a, _ in pairs:
        tl = _tile(a.shape[0], 512, 8)
        tls.append(tl)
        nks.append(a.shape[0] // tl)
    starts = [int(s) for s in np.cumsum([0] + nks[:-1])]
    nk = int(sum(nks))

    def body(*refs):
        out_ref, acc = refs[-2], refs[-1]
        k = pl.program_id(2)

        @pl.when(k == 0)
        def _():
            acc[...] = jnp.zeros_like(acc)

        for idx in range(len(pairs)):
            a_ref, b_ref = refs[2 * idx], refs[2 * idx + 1]

            @pl.when(jnp.logical_and(k >= starts[idx], k < starts[idx] + nks[idx]))
            def _():
                acc[...] += _tn(a_ref[...], b_ref[...])

        @pl.when(k == nk - 1)
        def _():
            out_ref[...] = acc[...].astype(BF16)

    in_specs, args = [], []
    for idx, (a, b) in enumerate(pairs):
        s0, n_ = starts[idx], nks[idx]
        in_specs.append(pl.BlockSpec((tls[idx], tk), lambda i, j, k, s0=s0, n_=n_: (jnp.clip(k - s0, 0, n_ - 1), i)))
        in_specs.append(pl.BlockSpec((tls[idx], tn), lambda i, j, k, s0=s0, n_=n_: (jnp.clip(k - s0, 0, n_ - 1), j)))
        args += [a, b]
    return pl.pallas_call(
        body, name=name, grid=(Ka // tk, Nb // tn, nk), in_specs=in_specs,
        out_specs=pl.BlockSpec((tk, tn), lambda i, j, k: (i, j)),
        out_shape=jax.ShapeDtypeStruct((Ka, Nb), BF16),
        scratch_shapes=[pltpu.VMEM((tk, tn), F32)], compiler_params=_params(3))(*args)


def _bwd_norm_mod(pairs, x, dres, g, sh, sc, name):
    M, D = x.shape
    K = pairs[0][0].shape[1]
    tm, tk = _tile(M, 512, 8), _tile(K, 1152 if len(pairs) == 1 else 512)
    nk = K // tk
    npair = len(pairs)
    has_res = dres is not None

    def body(*refs):
        pr = refs[:2 * npair]
        rest = refs[2 * npair:]
        if has_res:
            x_ref, dres_ref, g_ref, sh_ref, sc_ref, dx_ref, st_ref, acc = rest
        else:
            x_ref, g_ref, sh_ref, sc_ref, dx_ref, st_ref, acc = rest
        del sh_ref
        i, k = pl.program_id(0), pl.program_id(1)

        @pl.when(jnp.logical_and(i == 0, k == 0))
        def _():
            st_ref[...] = jnp.zeros_like(st_ref)

        @pl.when(k == 0)
        def _():
            acc[...] = jnp.zeros_like(acc)

        for idx in range(npair):
            acc[...] += _nt(pr[2 * idx + 1][...], pr[2 * idx][...])

        @pl.when(k == nk - 1)
        def _():
            xv = x_ref[...]
            gv = g_ref[...]
            dh = acc[...].T
            r = lax.rsqrt(jnp.mean(xv * xv, axis=-1, keepdims=True) + NORM_EPS)
            xh = xv * r
            st_ref[0:1, :] += jnp.sum(dh, axis=0, keepdims=True)
            st_ref[1:2, :] += jnp.sum(dh * (xh * gv), axis=0, keepdims=True)
            dn = dh * (1.0 + sc_ref[...])
            st_ref[2:3, :] += jnp.sum(dn * xh, axis=0, keepdims=True)
            dxh = dn * gv
            d = r * (dxh - xh * jnp.mean(dxh * xh, axis=-1, keepdims=True))
            if has_res:
                d = d + dres_ref[...]
            dx_ref[...] = d

    row = pl.BlockSpec((tm, D), lambda i, k: (i, 0))
    vec = pl.BlockSpec((1, D), lambda i, k: (0, 0))
    in_specs, args = [], []
    for dA, w in pairs:
        in_specs += [pl.BlockSpec((tm, tk), lambda i, k: (i, k)), pl.BlockSpec((D, tk), lambda i, k: (0, k))]
        args += [dA, w]
    in_specs += [row] + ([row] if has_res else []) + [vec, vec, vec]
    args += [x] + ([dres] if has_res else []) + [g, sh, sc]
    return pl.pallas_call(
        body, name=name, grid=(M // tm, nk), in_specs=in_specs,
        out_specs=[row, pl.BlockSpec((8, D), lambda i, k: (0, 0))],
        out_shape=[jax.ShapeDtypeStruct((M, D), F32), jax.ShapeDtypeStruct((8, D), F32)],
        scratch_shapes=[pltpu.VMEM((D, tm), F32)], compiler_params=_params(2))(*args)


def _local_step(x, ctx, tgt, mod, modc, norm_mix, norm_ffn, norm_final, lg, sink, w_in, rest_weights, on_grads):
    L, D = x.shape
    Lc = ctx.shape[0]
    d_proj = w_in.shape[1]
    npairs = RET_HEADS // 2
    nkv = ATT_KV_HEADS
    nkvp = nkv // 2
    o_rq = 0
    o_rk = o_rq + RET_HEADS * RET_DK // LANES
    o_rv = o_rk + RET_HEADS * RET_DK // LANES
    o_rg = o_rv + RET_HEADS * RET_DV // LANES
    o_aq = o_rg + RET_HEADS * RET_DV // LANES
    o_ak = o_aq + ATT_HEADS * ATT_DH // LANES
    o_av = o_ak + nkv * ATT_DH // LANES
    assert (o_av + nkv * ATT_DH // LANES) * LANES == d_proj
    assert o_rv % 2 == 0 and o_rg % 2 == 0 and (RET_HEADS * RET_DV) % (2 * LANES) == 0
    rv_blk, rg_blk = o_rv // 2, o_rg // 2
    d_ret = RET_HEADS * RET_DV
    d_mix = d_ret + ATT_HEADS * ATT_DH
    att_blk = d_ret // (2 * LANES)
    k_scale = RET_DK ** -0.5
    a_scale = ATT_DH ** -0.5

    T = _rope_tables(L)
    Tc = dict(C=jnp.ones((Lc, LANES), F32), S=jnp.zeros((Lc, LANES), F32))
    row = lambda m, i: m[i:i + 1]
    sh_m, sc_m, gt_m, sh_f, sc_f, gt_f = [row(mod, i) for i in range(6)]
    sh_mc, sc_mc = row(modc, 0), row(modc, 1)

    P, hx = _norm_mod_matmul(x, norm_mix, sh_m, sc_m, w_in, "in_proj")
    Pc, hc = _norm_mod_matmul(ctx, norm_mix, sh_mc, sc_mc, w_in, "in_proj_ctx")
    nq = RET_HEADS * RET_DK // LANES
    Qr = _rope_cols(P, o_rq, nq, T["Cr"], T["Sr"], T["Rr"], 1.0, True, "rope_rq")
    Kr = _rope_cols(P, o_rk, nq, T["Cr"], T["Sr"], T["Rr"], k_scale, True, "rope_rk")
    Krc = _rope_cols(Pc, o_rk, nq, Tc["C"], Tc["S"], T["Rr"], k_scale, False, "scale_rk_ctx")
    Qa = _rope_cols(P, o_aq, ATT_HEADS * ATT_DH // LANES, T["Ca"], T["Sa"], T["Ra"], a_scale, True, "rope_aq")
    Kd = _dup_heads(P, o_ak, nkvp, T["Ca"], T["Sa"], T["Ra"], T["D0"], T["D1"], True, "dup_ak")
    Vd = _dup_heads(P, o_av, nkvp, T["Ca"], T["Sa"], T["Ra"], T["D0"], T["D1"], False, "dup_av")
    Kdc = _dup_heads(Pc, o_ak, nkvp, Tc["C"], Tc["S"], T["Ra"], T["D0"], T["D1"], False, "dup_ak_ctx")
    Vdc = _dup_heads(Pc, o_av, nkvp, Tc["C"], Tc["S"], T["Ra"], T["D0"], T["D1"], False, "dup_av_ctx")

    SF = _ret_states_fwd(Kr, P, Krc, Pc, lg, rv_blk, npairs)
    Y, SB = _ret_out_fwd(Qr, Kr, P, Krc, Pc, SF, lg, rv_blk, rg_blk, npairs, d_mix)
    Y = _att_fwd(Qa, Kd, Vd, Kdc, Vdc, sink, Y, att_blk)

    w_out, w_gate, w_up, w_down = rest_weights(Y)
    x1, O1 = _proj_residual(Y, w_out, x, gt_m, "out_proj")
    G, U, A, h2 = _ffn_in(x1, norm_ffn, sh_f, sc_f, w_gate, w_up)
    x2, Fo = _proj_residual(A, w_down, x1, gt_f, "ffn_out")
    dx2, loss, d_norm_final = _final(x2, norm_final, tgt)

    dG, dU, dz2, dgt_f = _bwd_proj(dx2, gt_f, w_down, Fo, G, U, name="ffn_out_bwd")
    g_w_down = _tn_matmul([(A, dz2)], "grad_w_down")
    tok = on_grads(["w_down"], [g_w_down])
    dx1, st_f = _bwd_norm_mod([(dG, w_gate), (dU, w_up)], x1, dx2, norm_ffn + tok, sh_f, sc_f, "ffn_in_bwd")
    g_w_gate = _tn_matmul([(h2, dG)], "grad_w_gate")
    g_w_up = _tn_matmul([(h2, dU)], "grad_w_up")
    tok = on_grads(["w_gate", "w_up"], [g_w_gate, g_w_up])
    dY, dz1, dgt_m = _bwd_proj(dx1, gt_m + tok, w_out, O1, name="out_proj_bwd")
    g_w_out = _tn_matmul([(Y, dz1)], "grad_w_out")
    tok = on_grads(["w_out"], [g_w_out])

    dQa, dKp, dKs, dKn, dVp, dVs, dVn, dKdc, dVdc, dsink = _att_bwd(Qa, Kd, Vd, Kdc, Vdc, sink + tok, dY, att_blk)
    dQr, dKr, dVr, dP, dO, dKc, dVc, acc1 = _ret_bwd1(Qr, Kr, P, Krc, Pc, SF, SB, dY, lg, rv_blk, rg_blk, npairs, d_proj)
    dKr, dP, dKc, dVc, acc2 = _ret_bwd2(Qr, Kr, P, Krc, Pc, SB, dO, dKr, dVr, dP, dKc, dVc, lg, rv_blk, npairs)

    dP = _unrope_cols(dQr, dP, o_rq, nq, T["Cr"], T["Sr"], T["RrT"], 1.0, True, "unrope_rq")
    dP = _unrope_cols(dKr, dP, o_rk, nq, T["Cr"], T["Sr"], T["RrT"], k_scale, True, "unrope_rk")
    dP = _unrope_cols(dQa, dP, o_aq, ATT_HEADS * ATT_DH // LANES, T["Ca"], T["Sa"], T["RaT"], a_scale, True, "unrope_aq")
    dP = _fold_heads([(dKs, 0), (dKp, 1), (dKn, -1)], dP, o_ak, nkvp, T["Ca"], T["Sa"], T["RaT"], T["D0T"], T["D1T"],
                     True, "fold_ak")
    dP = _fold_heads([(dVs, 0), (dVp, 1), (dVn, -1)], dP, o_av, nkvp, T["Ca"], T["Sa"], T["RaT"], T["D0T"], T["D1T"],
                     False, "fold_av")
    dPc = jnp.zeros((Lc, d_proj), BF16)
    dPc = _unrope_cols(dKc, dPc, o_rk, nq, Tc["C"], Tc["S"], T["RrT"], k_scale, False, "ctx_rk_bwd")
    dPc = _unrope_cols(dVc, dPc, o_rv, RET_HEADS * RET_DV // LANES, Tc["C"], Tc["S"], T["RrT"], 1.0, False, "ctx_rv_bwd")
    dPc = _fold_heads([(dKdc.astype(BF16), 0)], dPc, o_ak, nkvp, Tc["C"], Tc["S"], T["RaT"], T["D0T"], T["D1T"],
                      False, "fold_ak_ctx")
    dPc = _fold_heads([(dVdc.astype(BF16), 0)], dPc, o_av, nkvp, Tc["C"], Tc["S"], T["RaT"], T["D0T"], T["D1T"],
                      False, "fold_av_ctx")

    dx, st_m = _bwd_norm_mod([(dP, w_in)], x, dx1, norm_mix, sh_m, sc_m, "in_proj_bwd")
    _, st_mc = _bwd_norm_mod([(dPc, w_in)], ctx, None, norm_mix, sh_mc, sc_mc, "in_proj_ctx_bwd")
    g_w_in = _tn_matmul([(hx, dP), (hc, dPc)], "grad_w_in")
    on_grads(["w_in"], [g_w_in])

    a1 = acc1[:, :, :, 0].reshape(RET_HEADS, ACC_ROWS)
    a2 = acc2[:, :, :, 0].reshape(RET_HEADS, ACC_ROWS)
    dlam = (a1[:, 0] + a1[:, 2] + a1[:, 3] + a1[:, 4]) * lg[0]
    dmu = (a1[:, 1] + a1[:, 5] + a2[:, 0] + a2[:, 1]) * lg[1]
    d_sink = dsink[:, :4, 0].reshape(1, ATT_HEADS)

    nh = RET_HEADS
    assert 2 * nh + ATT_HEADS <= LOSS_LANE
    small = _pack_rows(
        [(st_m, 0, 2, 0, 0), (dgt_m, 0, 1, 2, 0), (st_f, 0, 2, 3, 0), (dgt_f, 0, 1, 5, 0), (st_mc, 0, 2, 6, 0),
         (st_m[2:3] + st_mc[2:3], 0, 1, 12, 0), (st_f, 2, 1, 13, 0), (d_norm_final, 0, 1, 14, 0),
         (dlam.reshape(1, nh), 0, 1, 15, 0), (dmu.reshape(1, nh), 0, 1, 15, nh), (d_sink, 0, 1, 15, 2 * nh),
         (loss[:, 0:1], 0, 1, 15, LOSS_LANE)], 16, D, "pack_small")
    return dict(grad_x=dx, small=small)


def _my_pos():
    return lax.axis_index("x"), lax.axis_index("y"), lax.axis_index("c")


def _other_chips(x, y):
    return [(1 - x, y), (x, 1 - y), (1 - x, 1 - y)]


def _remote(src, dst, ssem, rsem, dev):
    return pltpu.make_async_remote_copy(src_ref=src, dst_ref=dst, send_sem=ssem, recv_sem=rsem,
                                        device_id=dev, device_id_type=MESH)


def _allgather8(v, name):
    R, Cc = v.shape

    def body(v_ref, out_ref, send_sems, recv_sems):
        x, y, c = _my_pos()
        me = 4 * x + 2 * y + c
        out_ref[pl.ds(me, 1)] = v_ref[...][None]
        peers = []
        for j in range(1, N_DEV):
            peers.append((1 - x if (j >> 2) & 1 else x, 1 - y if (j >> 1) & 1 else y, 1 - c if j & 1 else c))
        copies = []
        for j, peer in enumerate(peers):
            cp = _remote(v_ref, out_ref.at[me], send_sems.at[j], recv_sems.at[j], peer)
            cp.start()
            copies.append(cp)
        for j, peer in enumerate(peers):
            pid = 4 * peer[0] + 2 * peer[1] + peer[2]
            _remote(v_ref, out_ref.at[pid], send_sems.at[j], recv_sems.at[j], peer).wait_recv()
        for cp in copies:
            cp.wait_send()

    return pl.pallas_call(
        body, name=name, out_shape=jax.ShapeDtypeStruct((N_DEV, R, Cc), v.dtype),
        in_specs=[pl.BlockSpec(memory_space=pltpu.VMEM)], out_specs=pl.BlockSpec(memory_space=pltpu.VMEM),
        scratch_shapes=[pltpu.SemaphoreType.DMA((N_DEV - 1,)), pltpu.SemaphoreType.DMA((N_DEV - 1,))])(v)


def _region(ref, k, half, shard_shape, axis):
    r, cs = shard_shape
    hr = r // 2
    if axis == 1:
        return ref.at[pl.ds(pl.multiple_of(half * hr, 16), hr), pl.ds(pl.multiple_of(k * cs, LANES), cs)]
    return ref.at[pl.ds(pl.multiple_of(k * r + half * hr, 16), hr), :]


def _full_shape(shard_shape, axis):
    r, cs = shard_shape
    return (r, N_CHIPS * cs) if axis == 1 else (N_CHIPS * r, cs)


def _allgather_weights(fulls, shapes, axes):
    nw = len(fulls)

    def body(*refs):
        out_refs = refs[nw:2 * nw]
        send1, recv1, send2, recv2 = refs[2 * nw:]
        x, y, c = _my_pos()
        k0 = 2 * x + y
        chips = _other_chips(x, y)
        sends = []
        for w in range(nw):
            own = _region(out_refs[w], k0, c, shapes[w], axes[w])
            for j, ch in enumerate(chips):
                cp = _remote(own, own, send1.at[w, j], recv1.at[w, j], (ch[0], ch[1], c))
                cp.start()
                sends.append(cp)
        for w in range(nw):
            for j, ch in enumerate(chips):
                kj = 2 * ch[0] + ch[1]
                got = _region(out_refs[w], kj, c, shapes[w], axes[w])
                _remote(got, got, send1.at[w, j], recv1.at[w, j], (ch[0], ch[1], c)).wait_recv()
                fw = _remote(got, got, send2.at[w, j], recv2.at[w, j], (x, y, 1 - c))
                fw.start()
                sends.append(fw)
        for w in range(nw):
            for j, ch in enumerate(chips):
                kj = 2 * ch[0] + ch[1]
                got = _region(out_refs[w], kj, 1 - c, shapes[w], axes[w])
                _remote(got, got, send2.at[w, j], recv2.at[w, j], (x, y, 1 - c)).wait_recv()
        for cp in sends:
            cp.wait_send()

    anyspec = pl.BlockSpec(memory_space=pl.ANY)
    sem = lambda: pltpu.SemaphoreType.DMA((nw, 3))
    return pl.pallas_call(
        body, name="allgather_weights",
        out_shape=[jax.ShapeDtypeStruct(f.shape, BF16) for f in fulls],
        in_specs=[anyspec] * nw, out_specs=[anyspec] * nw,
        input_output_aliases={w: w for w in range(nw)},
        scratch_shapes=[sem(), sem(), sem(), sem()])(*fulls)


def _half_pieces(ref, half, shard_shape, axis):
    r, cs = shard_shape
    hr = r // 2
    if axis == 1:
        return [ref.at[pl.ds(pl.multiple_of(half * hr, 16), hr), :]]
    return [ref.at[pl.ds(pl.multiple_of(k * r + half * hr, 16), hr), :] for k in range(N_CHIPS)]


def _rs_sibling(grads, shapes, axes, name):
    nw = len(grads)
    npc = max(1 if a == 1 else N_CHIPS for a in axes)

    def body(*refs):
        g_refs, out_refs = refs[:nw], refs[nw:2 * nw]
        send, recv = refs[2 * nw:]
        x, y, c = _my_pos()
        sib = (x, y, 1 - c)
        copies = []
        for w in range(nw):
            src = _half_pieces(g_refs[w], 1 - c, shapes[w], axes[w])
            dst = _half_pieces(out_refs[w], 1 - c, shapes[w], axes[w])
            for i, (s, d) in enumerate(zip(src, dst)):
                cp = _remote(s, d, send.at[w, i], recv.at[w, i], sib)
                cp.start()
                copies.append(cp)
        for w in range(nw):
            mine = _half_pieces(out_refs[w], c, shapes[w], axes[w])
            for i, d in enumerate(mine):
                _remote(d, d, send.at[w, i], recv.at[w, i], sib).wait_recv()
        for cp in copies:
            cp.wait_send()

    anyspec = pl.BlockSpec(memory_space=pl.ANY)
    return pl.pallas_call(
        body, name=name,
        out_shape=[jax.ShapeDtypeStruct(_full_shape(s, a), BF16) for s, a in zip(shapes, axes)],
        in_specs=[anyspec] * nw, out_specs=[anyspec] * nw,
        scratch_shapes=[pltpu.SemaphoreType.DMA((nw, npc)), pltpu.SemaphoreType.DMA((nw, npc))])(*grads)


def _half_block_spec(shard_shape, axis, tr):
    r, cs = shard_shape
    hr = r // 2
    if axis == 1:
        return pl.BlockSpec((tr, cs), lambda k, i, c_ref: (c_ref[0] * (hr // tr) + i, k))
    return pl.BlockSpec((tr, cs), lambda k, i, c_ref: (k * (r // tr) + c_ref[0] * (hr // tr) + i, 0))


def _add_halves(g, recv, cvec, shard_shape, axis, name):
    r, cs = shard_shape
    hr = r // 2
    tr = _tile(hr, 256, 16)

    def body(c_ref, a_ref, b_ref, o_ref):
        del c_ref
        o_ref[0] = (a_ref[...].astype(F32) + b_ref[...].astype(F32)).astype(BF16)

    spec = _half_block_spec(shard_shape, axis, tr)
    return pl.pallas_call(
        body, name=name,
        grid_spec=pltpu.PrefetchScalarGridSpec(
            num_scalar_prefetch=1, grid=(N_CHIPS, hr // tr), in_specs=[spec, spec],
            out_specs=pl.BlockSpec((1, tr, cs), lambda k, i, c_ref: (k, i, 0))),
        out_shape=jax.ShapeDtypeStruct((N_CHIPS, hr, cs), BF16),
        compiler_params=_params(2, False))(cvec, g, recv)


def _rs_chips(sums, name):
    nw = len(sums)

    def body(*refs):
        s_refs, out_refs = refs[:nw], refs[nw:2 * nw]
        send, recv = refs[2 * nw:]
        x, y, c = _my_pos()
        k0 = 2 * x + y
        chips = _other_chips(x, y)
        copies = []
        for w in range(nw):
            for j, ch in enumerate(chips):
                kj = 2 * ch[0] + ch[1]
                cp = _remote(s_refs[w].at[kj], out_refs[w].at[k0], send.at[w, j], recv.at[w, j], (ch[0], ch[1], c))
                cp.start()
                copies.append(cp)
        for w in range(nw):
            for j, ch in enumerate(chips):
                kj = 2 * ch[0] + ch[1]
                d = out_refs[w].at[kj]
                _remote(d, d, send.at[w, j], recv.at[w, j], (ch[0], ch[1], c)).wait_recv()
        for cp in copies:
            cp.wait_send()

    anyspec = pl.BlockSpec(memory_space=pl.ANY)
    return pl.pallas_call(
        body, name=name,
        out_shape=[jax.ShapeDtypeStruct(s.shape, BF16) for s in sums],
        in_specs=[anyspec] * nw, out_specs=[anyspec] * nw,
        scratch_shapes=[pltpu.SemaphoreType.DMA((nw, 3)), pltpu.SemaphoreType.DMA((nw, 3))])(*sums)


def _sum_chips(sums, landed, kc, name):
    _, hr, cs = sums.shape
    tr = _tile(hr, 256, 16)

    def body(kc_ref, own_ref, a_ref, b_ref, c_ref, o_ref):
        del kc_ref
        o_ref[...] = (own_ref[0].astype(F32) + a_ref[0].astype(F32)) + (b_ref[0].astype(F32) + c_ref[0].astype(F32))

    slot = lambda j: pl.BlockSpec((1, tr, cs), lambda i, kc_ref: ((kc_ref[0] + j) % N_CHIPS, i, 0))
    return pl.pallas_call(
        body, name=name,
        grid_spec=pltpu.PrefetchScalarGridSpec(
            num_scalar_prefetch=1, grid=(hr // tr,), in_specs=[slot(0), slot(1), slot(2), slot(3)],
            out_specs=pl.BlockSpec((tr, cs), lambda i, kc_ref: (kc_ref[1] * (hr // tr) + i, 0))),
        out_shape=jax.ShapeDtypeStruct((2 * hr, cs), F32),
        compiler_params=_params(1, False))(kc, sums, landed, landed, landed)


def _exchange_halves(shards):
    nw = len(shards)

    def body(*refs):
        out_refs = refs[nw:2 * nw]
        send, recv = refs[2 * nw:]
        x, y, c = _my_pos()
        sib = (x, y, 1 - c)
        copies = []
        for w in range(nw):
            hr = shards[w].shape[0] // 2
            mine = out_refs[w].at[pl.ds(pl.multiple_of(c * hr, 8), hr), :]
            cp = _remote(mine, mine, send.at[w], recv.at[w], sib)
            cp.start()
            copies.append(cp)
        for w in range(nw):
            hr = shards[w].shape[0] // 2
            other = out_refs[w].at[pl.ds(pl.multiple_of((1 - c) * hr, 8), hr), :]
            _remote(other, other, send.at[w], recv.at[w], sib).wait_recv()
        for cp in copies:
            cp.wait_send()

    anyspec = pl.BlockSpec(memory_space=pl.ANY)
    return pl.pallas_call(
        body, name="exchange_halves",
        out_shape=[jax.ShapeDtypeStruct(s.shape, F32) for s in shards],
        in_specs=[anyspec] * nw, out_specs=[anyspec] * nw,
        input_output_aliases={w: w for w in range(nw)},
        scratch_shapes=[pltpu.SemaphoreType.DMA((nw,)), pltpu.SemaphoreType.DMA((nw,))])(*shards)


def _cast_into_full(w, kc, axis, name):
    r, cs = w.shape
    tr = _tile(r, 256, 16)

    def body(kc_ref, w_ref, o_ref):
        del kc_ref
        o_ref[...] = w_ref[...].astype(BF16)

    if axis == 1:
        ospec = pl.BlockSpec((tr, cs), lambda i, kc_ref: (i, kc_ref[0]))
    else:
        ospec = pl.BlockSpec((tr, cs), lambda i, kc_ref: (kc_ref[0] * (r // tr) + i, 0))
    return pl.pallas_call(
        body, name=name,
        grid_spec=pltpu.PrefetchScalarGridSpec(
            num_scalar_prefetch=1, grid=(r // tr,), in_specs=[pl.BlockSpec((tr, cs), lambda i, kc_ref: (i, 0))],
            out_specs=ospec),
        out_shape=jax.ShapeDtypeStruct(_full_shape((r, cs), axis), BF16),
        compiler_params=_params(1, False))(kc, w)


def _adam_math(w, g, m, v):
    m2 = ADAM_B1 * m + (1.0 - ADAM_B1) * g
    v2 = ADAM_B2 * v + (1.0 - ADAM_B2) * (g * g)
    m_hat = m2 / (1.0 - ADAM_B1 ** ADAM_STEP)
    v_hat = v2 / (1.0 - ADAM_B2 ** ADAM_STEP)
    delta = -ADAM_LR * (m_hat / (jnp.sqrt(v_hat) + ADAM_EPS) + ADAM_WD * w)
    return delta, m2, v2


def _adam(w, g, m, v, name):
    r, cs = w.shape
    tr = _tile(r, 256, 8)

    def body(w_ref, g_ref, m_ref, v_ref, d_ref, m2_ref, v2_ref):
        d, m2, v2 = _adam_math(w_ref[...], g_ref[...], m_ref[...], v_ref[...])
        d_ref[...] = d
        m2_ref[...] = m2
        v2_ref[...] = v2

    spec = pl.BlockSpec((tr, cs), lambda i: (i, 0))
    shp = jax.ShapeDtypeStruct((r, cs), F32)
    return pl.pallas_call(body, name=name, grid=(r // tr,), in_specs=[spec] * 4, out_specs=[spec] * 3,
                          out_shape=[shp, shp, shp], compiler_params=_params(1, False))(w, g, m, v)


def _mod_rows(a16, w, b, name):
    D, n = w.shape
    tn = _tile(n, 512)

    def body(a_ref, w_ref, b_ref, o_ref):
        a = a_ref[...]
        o_ref[...] = _nn((a * _sigmoid(a)).astype(BF16), w_ref[...].astype(BF16)) + b_ref[...]

    return pl.pallas_call(
        body, name=name, grid=(n // tn,),
        in_specs=[pl.BlockSpec((16, D), lambda j: (0, 0)), pl.BlockSpec((D, tn), lambda j: (0, j)),
                  pl.BlockSpec((1, tn), lambda j: (0, j))],
        out_specs=pl.BlockSpec((16, tn), lambda j: (0, j)),
        out_shape=jax.ShapeDtypeStruct((16, n), F32), compiler_params=_params(1, False))(a16, w, b)


def _w_mod_update(a16, d16, w, m, v):
    D, n = w.shape
    tn = _tile(n, 256)

    def body(a_ref, d_ref, w_ref, m_ref, v_ref, g_ref, dl_ref, m2_ref, v2_ref, p_ref):
        @pl.when(pl.program_id(0) == 0)
        def _():
            p_ref[...] = jnp.zeros_like(p_ref)
        a = a_ref[...]
        db = d_ref[...].astype(BF16)
        wv = w_ref[...]
        g = _tn((a * _sigmoid(a)).astype(BF16), db)
        g_ref[...] = g
        d, m2, v2 = _adam_math(wv, g, m_ref[...], v_ref[...])
        dl_ref[...] = d
        m2_ref[...] = m2
        v2_ref[...] = v2
        p_ref[...] += _nt(db, wv.astype(BF16))

    wspec = pl.BlockSpec((D, tn), lambda j: (0, j))
    shp = jax.ShapeDtypeStruct((D, n), F32)
    return pl.pallas_call(
        body, name="w_mod_update", grid=(n // tn,),
        in_specs=[pl.BlockSpec((16, D), lambda j: (0, 0)), pl.BlockSpec((16, tn), lambda j: (0, j)), wspec, wspec, wspec],
        out_specs=[wspec, wspec, wspec, wspec, pl.BlockSpec((16, D), lambda j: (0, 0))],
        out_shape=[shp, shp, shp, shp, jax.ShapeDtypeStruct((16, D), F32)],
        compiler_params=_params(1))(a16, d16, w, m, v)


def _sum_devices(g8, name):
    _, R, Cc = g8.shape

    def body(g_ref, o_ref):
        t = g_ref[0]
        for d in range(1, N_DEV):
            t = t + g_ref[d]
        o_ref[...] = t

    return pl.pallas_call(body, name=name, out_shape=jax.ShapeDtypeStruct((R, Cc), F32))(g8)


def _c_ctx_grad(parts, c_ctx):
    D = c_ctx.shape[1]

    def body(p_ref, c_ref, o_ref):
        t = p_ref[0]
        for k in range(1, N_CHIPS):
            t = t + p_ref[2 * k]
        cv = c_ref[...]
        sg = _sigmoid(cv)
        o_ref[...] = t * (sg * (1.0 + cv * (1.0 - sg)))

    return pl.pallas_call(body, name="c_ctx_grad", out_shape=jax.ShapeDtypeStruct((1, D), F32))(parts, c_ctx)


def _pack_rows(items, nrows, width, name):
    arrays, plan = [], []
    for a, r0, nr, d0, c0 in items:
        for ai, b in enumerate(arrays):
            if b is a:
                break
        else:
            ai = len(arrays)
            arrays.append(a)
        plan.append((ai, r0, nr, d0, c0, a.shape[1]))

    def body(*refs):
        o_ref = refs[-1]
        o_ref[...] = jnp.zeros_like(o_ref)
        for ai, r0, nr, d0, c0, w in plan:
            o_ref[d0:d0 + nr, c0:c0 + w] = refs[ai][r0:r0 + nr, :]

    return pl.pallas_call(body, name=name, out_shape=jax.ShapeDtypeStruct((nrows, width), F32))(*arrays)


HBM_SPEC = pl.BlockSpec(memory_space=pltpu.HBM)
SEM_SPEC = pl.BlockSpec(memory_space=pltpu.SEMAPHORE)
SPLIT_PARAMS = pltpu.CompilerParams(has_side_effects=pltpu.SideEffectType.DATAFLOW_SIDE_EFFECTING)


def _in_hbm(a):
    return pltpu.with_memory_space_constraint(a, pltpu.HBM)


def _ag_chips_start(fulls, shapes, axes):
    nw = len(fulls)

    def body(*refs):
        in_refs, send, recv, token = refs[:nw], refs[nw], refs[nw + 1], refs[-1]
        x, y, c = _my_pos()
        k0 = 2 * x + y
        for w in range(nw):
            own = _region(in_refs[w], k0, c, shapes[w], axes[w])
            for j, ch in enumerate(_other_chips(x, y)):
                _remote(own, own, send.at[3 * w + j], recv.at[3 * w + j], (ch[0], ch[1], c)).start()
        token[...] = jnp.zeros_like(token)

    return pl.pallas_call(
        body, name="ag_chips_start",
        out_shape=(pltpu.SemaphoreType.DMA((3 * nw,)), pltpu.SemaphoreType.DMA((3 * nw,)),
                   *[pltpu.HBM(f.shape, f.dtype) for f in fulls], jax.ShapeDtypeStruct((8, LANES), F32)),
        in_specs=[HBM_SPEC] * nw,
        out_specs=(SEM_SPEC, SEM_SPEC, *[HBM_SPEC] * nw, pl.BlockSpec(memory_space=pltpu.VMEM)),
        input_output_aliases={w: 2 + w for w in range(nw)},
        compiler_params=SPLIT_PARAMS)(*[_in_hbm(f) for f in fulls])


def _ag_chips_wait(send, recv, fulls, shapes, axes, after):
    nw = len(fulls)

    def body(*refs):
        in_refs, send_ref, recv_ref = refs[:nw], refs[nw], refs[nw + 1]
        x, y, c = _my_pos()
        k0 = 2 * x + y
        for w in range(nw):
            own = _region(in_refs[w], k0, c, shapes[w], axes[w])
            for j, ch in enumerate(_other_chips(x, y)):
                got = _region(in_refs[w], 2 * ch[0] + ch[1], c, shapes[w], axes[w])
                cp = _remote(own, got, send_ref.at[3 * w + j], recv_ref.at[3 * w + j], (ch[0], ch[1], c))
                cp.wait_send()
                cp.wait_recv()

    return pl.pallas_call(
        body, name="ag_chips_wait",
        out_shape=tuple(pltpu.HBM(f.shape, f.dtype) for f in fulls),
        in_specs=[HBM_SPEC] * nw + [SEM_SPEC, SEM_SPEC, pl.BlockSpec(memory_space=pl.ANY)],
        out_specs=tuple([HBM_SPEC] * nw),
        input_output_aliases={w: w for w in range(nw)},
        compiler_params=SPLIT_PARAMS)(*fulls, send, recv, after)


def _ag_forward(fulls, shapes, axes):
    nw = len(fulls)

    def body(*refs):
        out_refs = refs[nw:2 * nw]
        send, recv = refs[2 * nw:]
        x, y, c = _my_pos()
        sib = (x, y, 1 - c)
        chips = _other_chips(x, y)
        copies = []
        for w in range(nw):
            for j, ch in enumerate(chips):
                got = _region(out_refs[w], 2 * ch[0] + ch[1], c, shapes[w], axes[w])
                cp = _remote(got, got, send.at[w, j], recv.at[w, j], sib)
                cp.start()
                copies.append(cp)
        for w in range(nw):
            for j, ch in enumerate(chips):
                got = _region(out_refs[w], 2 * ch[0] + ch[1], 1 - c, shapes[w], axes[w])
                _remote(got, got, send.at[w, j], recv.at[w, j], sib).wait_recv()
        for cp in copies:
            cp.wait_send()

    anyspec = pl.BlockSpec(memory_space=pl.ANY)
    return pl.pallas_call(
        body, name="ag_forward",
        out_shape=[jax.ShapeDtypeStruct(f.shape, BF16) for f in fulls],
        in_specs=[anyspec] * nw, out_specs=[anyspec] * nw,
        input_output_aliases={w: w for w in range(nw)},
        scratch_shapes=[pltpu.SemaphoreType.DMA((nw, 3)), pltpu.SemaphoreType.DMA((nw, 3))])(*fulls)


def _rs_chips_start(sums, name):
    nw = len(sums)

    def body(*refs):
        s_refs, l_refs, send, recv, token = refs[:nw], refs[nw:2 * nw], refs[2 * nw], refs[2 * nw + 1], refs[-1]
        x, y, c = _my_pos()
        k0 = 2 * x + y
        for w in range(nw):
            for j, ch in enumerate(_other_chips(x, y)):
                _remote(s_refs[w].at[2 * ch[0] + ch[1]], l_refs[w].at[k0], send.at[3 * w + j], recv.at[3 * w + j],
                        (ch[0], ch[1], c)).start()
        token[...] = jnp.zeros_like(token)

    thru = [pltpu.HBM(s.shape, s.dtype) for s in sums]
    return pl.pallas_call(
        body, name=name,
        out_shape=(pltpu.SemaphoreType.DMA((3 * nw,)), pltpu.SemaphoreType.DMA((3 * nw,)), *thru, *thru,
                   jax.ShapeDtypeStruct((8, LANES), F32)),
        in_specs=[HBM_SPEC] * (2 * nw),
        out_specs=(SEM_SPEC, SEM_SPEC, *[HBM_SPEC] * (2 * nw), pl.BlockSpec(memory_space=pltpu.VMEM)),
        input_output_aliases={i: 2 + i for i in range(2 * nw)},
        compiler_params=SPLIT_PARAMS)(*[_in_hbm(s) for s in sums], *[_in_hbm(lax.empty(s.shape, s.dtype)) for s in sums])


def _rs_chips_wait(send, recv, sums, lands, after, name):
    nw = len(sums)

    def body(*refs):
        s_refs, l_refs, send_ref, recv_ref = refs[:nw], refs[nw:2 * nw], refs[2 * nw], refs[2 * nw + 1]
        x, y, c = _my_pos()
        for w in range(nw):
            for j, ch in enumerate(_other_chips(x, y)):
                kj = 2 * ch[0] + ch[1]
                cp = _remote(s_refs[w].at[kj], l_refs[w].at[kj], send_ref.at[3 * w + j], recv_ref.at[3 * w + j],
                             (ch[0], ch[1], c))
                cp.wait_send()
                cp.wait_recv()

    thru = tuple(pltpu.HBM(s.shape, s.dtype) for s in sums)
    return pl.pallas_call(
        body, name=name, out_shape=thru + thru,
        in_specs=[HBM_SPEC] * (2 * nw) + [SEM_SPEC, SEM_SPEC, pl.BlockSpec(memory_space=pl.ANY)],
        out_specs=tuple([HBM_SPEC] * (2 * nw)),
        input_output_aliases={i: i for i in range(2 * nw)},
        compiler_params=SPLIT_PARAMS)(*sums, *lands, send, recv, after)


LOSS_LANE = 64


def kernel(x, c, ctx, c_ctx, w_mod, b_mod, norm_mix, norm_ffn, w_in, ret_decay, attn_sink, w_out, w_gate, w_up, w_down, norm_final, loss_target, m_c_ctx, m_w_mod, m_b_mod, m_norm_mix, m_norm_ffn, m_w_in, m_ret_decay, m_attn_sink, m_w_out, m_w_gate, m_w_up, m_w_down, m_norm_final, v_c_ctx, v_w_mod, v_b_mod, v_norm_mix, v_norm_ffn, v_w_in, v_ret_decay, v_attn_sink, v_w_out, v_w_gate, v_w_up, v_w_down, v_norm_final):
    D = x.shape[-1]
    n3 = w_mod.shape[-1]
    xi, yi, ci = _my_pos()
    b = 4 * xi + 2 * yi + ci
    k0 = 2 * xi + yi
    cvec = jnp.reshape(ci, (1,)).astype(jnp.int32)
    kc = jnp.stack([k0, ci]).astype(jnp.int32)

    dense = [("w_in", w_in[0], 1), ("w_out", w_out[0], 0), ("w_gate", w_gate[0], 1), ("w_up", w_up[0], 1),
             ("w_down", w_down[0], 0)]
    axes = [a for _, _, a in dense]
    shapes = [w.shape for _, w, _ in dense]
    own16 = [_cast_into_full(w, kc, a, "cast_" + n) for n, w, a in dense]
    (f_in,) = _allgather_weights(own16[:1], shapes[:1], axes[:1])
    ag = _ag_chips_start(own16[1:], shapes[1:], axes[1:])
    ag_send, ag_recv, ag_thru, ag_tok = ag[0], ag[1], list(ag[2:-1]), ag[-1][0:1, 0:1]

    def rest_weights(after):
        landed_w = _ag_chips_wait(ag_send, ag_recv, ag_thru, shapes[1:], axes[1:], after)
        return _ag_forward(list(landed_w), shapes[1:], axes[1:])

    c_all = _allgather8(c, "gather_c").reshape(N_DEV, D)
    c_ctx2 = c_ctx.reshape(1, D)
    a16 = _pack_rows([(c_all, 0, N_DEV, 0, 0), (c_ctx2, 0, 1, N_DEV, 0)], 16, D, "pack_cond")
    b_cols = lax.dynamic_slice_in_dim(b_mod, k0 * n3, n3, axis=1)
    mod16 = _mod_rows(a16, w_mod[0], b_cols, "mod_rows")
    mod_all = _allgather8(mod16, "gather_mod")
    mine = jnp.stack([lax.dynamic_index_in_dim(mod_all, 2 * k + ci, 0, keepdims=False) for k in range(N_CHIPS)])
    mod = lax.dynamic_index_in_dim(mine, b, 1, keepdims=False).reshape(6, D)
    modc = mine[:, N_DEV].reshape(6, D)

    lg = -jnp.exp(ret_decay[0])

    index = {n: i for i, (n, _, _) in enumerate(dense)}
    pending, done = [], {}

    def on_grads(names, gs):
        ids = [index[n] for n in names]
        shp, axs = [shapes[i] for i in ids], [axes[i] for i in ids]
        from_sib = _rs_sibling(gs, shp, axs, "rs_sibling_" + names[0])
        sums = [_add_halves(g, r, cvec, s, a, "add_halves_" + n) for g, r, s, a, n in zip(gs, from_sib, shp, axs, names)]
        if names == ["w_in"]:
            for n, s, l in zip(names, sums, _rs_chips(sums, "rs_chips_" + names[0])):
                done[n] = (s, l)
            return None
        st = _rs_chips_start(sums, "rs_chips_start_" + names[0])
        nw = len(names)
        pending.append((names, st[0], st[1], list(st[2:2 + nw]), list(st[2 + nw:2 + 2 * nw])))
        return st[-1][0:1, 0:1]

    out = _local_step(x[0], ctx[0], loss_target[0], mod, modc, norm_mix + ag_tok, norm_ffn, norm_final.reshape(1, D), lg,
                      attn_sink, f_in, rest_weights, on_grads)

    for names, send, recv, sums, lands in pending:
        res = _rs_chips_wait(send, recv, sums, lands, out["grad_x"], "rs_chips_wait_" + names[0])
        for i, n in enumerate(names):
            done[n] = (res[i], res[len(names) + i])
    halves = [_sum_chips(done[n][0], done[n][1], kc, "sum_chips_" + n) for n, _, _ in dense]
    g_dense = _exchange_halves(halves)

    nh = 2 * RET_HEADS
    small_all = _allgather8(out["small"], "gather_small")
    tot = _sum_devices(small_all, "sum_small")
    g_b_mod = (tot[0:6] + tot[6:12]).reshape(1, 6 * D)
    dmodc_tot = tot[6:12].reshape(1, 6 * D)
    dmod_rows = small_all[:, 0:6].reshape(N_DEV, 6 * D)
    d16 = _pack_rows([(dmod_rows, 0, N_DEV, 0, 0), (dmodc_tot, 0, 1, N_DEV, 0)], 16, 6 * D, "pack_dmod")
    d16 = lax.dynamic_slice_in_dim(d16, k0 * n3, n3, axis=1)
    g_w_mod, dl_w_mod, m2_w_mod, v2_w_mod, part = _w_mod_update(a16, d16, w_mod[0], m_w_mod[0], v_w_mod[0])
    part_all = _allgather8(part[N_DEV:N_DEV + 1], "gather_c_ctx")
    g_c_ctx = _c_ctx_grad(part_all, c_ctx2)
    loss = tot[15, LOSS_LANE]
    g_ret_decay = tot[15, :nh].reshape(1, 2, RET_HEADS)
    g_sink = tot[15, nh:nh + ATT_HEADS].reshape(1, ATT_HEADS)

    def pack(cc, bm, nm, nf, nfin, rd, sk, name):
        rd2 = rd.reshape(2, RET_HEADS)
        return _pack_rows([(bm.reshape(6, D), 0, 6, 0, 0), (cc.reshape(1, D), 0, 1, 6, 0), (nm.reshape(1, D), 0, 1, 7, 0),
                           (nf.reshape(1, D), 0, 1, 8, 0), (nfin.reshape(1, D), 0, 1, 9, 0),
                           (rd2, 0, 1, 10, 0), (rd2, 1, 1, 10, RET_HEADS), (sk.reshape(1, ATT_HEADS), 0, 1, 10, nh)],
                          16, D, name)

    w_s = pack(c_ctx, b_mod, norm_mix, norm_ffn, norm_final, ret_decay, attn_sink, "pack_w")
    g_s = _pack_rows([(g_b_mod.reshape(6, D), 0, 6, 0, 0), (g_c_ctx, 0, 1, 6, 0), (tot, 12, 3, 7, 0),
                      (tot[15:16, 0:nh + ATT_HEADS], 0, 1, 10, 0)], 16, D, "pack_g")
    m_s = pack(m_c_ctx, m_b_mod, m_norm_mix, m_norm_ffn, m_norm_final, m_ret_decay, m_attn_sink, "pack_m")
    v_s = pack(v_c_ctx, v_b_mod, v_norm_mix, v_norm_ffn, v_norm_final, v_ret_decay, v_attn_sink, "pack_v")
    small_upd = _adam(w_s, g_s, m_s, v_s, "adam_small")

    def unpack(t):
        return dict(b_mod=t[0:6].reshape(1, 6 * D), c_ctx=t[6], norm_mix=t[7:8], norm_ffn=t[8:9], norm_final=t[9],
                    ret_decay=t[10, :nh].reshape(1, 2, RET_HEADS), attn_sink=t[10, nh:nh + ATT_HEADS].reshape(1, ATT_HEADS))

    dense_w = dict(w_in=(w_in, m_w_in, v_w_in), w_out=(w_out, m_w_out, v_w_out), w_gate=(w_gate, m_w_gate, v_w_gate),
                   w_up=(w_up, m_w_up, v_w_up), w_down=(w_down, m_w_down, v_w_down))
    grads = dict(unpack(g_s), w_mod=g_w_mod[None])
    upd = [dict(unpack(t)) for t in small_upd]
    upd[0]["w_mod"], upd[1]["w_mod"], upd[2]["w_mod"] = dl_w_mod[None], m2_w_mod[None], v2_w_mod[None]
    for (n, _, _), g in zip(dense, g_dense):
        w_, m_, v_ = dense_w[n]
        res = _adam(w_[0], g, m_[0], v_[0], "adam_" + n)
        grads[n] = g[None]
        for u, r_ in zip(upd, res):
            u[n] = r_[None]

    order = ['c_ctx', 'w_mod', 'b_mod', 'norm_mix', 'norm_ffn', 'w_in', 'ret_decay', 'attn_sink', 'w_out', 'w_gate',
             'w_up', 'w_down', 'norm_final']
    outs = [loss, out["grad_x"][None]] + [grads[n] for n in order]
    for u in upd:
        outs += [u[n] for n in order]
    return tuple(outs)
```

```python
import functools
import numpy as np
import jax
import jax.numpy as jnp
from jax import lax
from jax.experimental import pallas as pl
from jax.experimental.pallas import tpu as pltpu

F32 = jnp.float32
BF16 = jnp.bfloat16

RET_HEADS = 8
RET_DK = 64
RET_DV = 128
CHUNK = 128
ATT_HEADS = 16
ATT_KV_HEADS = 4
ATT_DH = 64
GRID_W = 64
ROPE_BASE = 10000.0
NORM_EPS = 1e-6
ADAM_LR = 0.001
ADAM_B1 = 0.9
ADAM_B2 = 0.999
ADAM_EPS = 1e-08
ADAM_WD = 0.01
ADAM_STEP = 10
NEG = -1e30
LANES = 128
VMEM_LIMIT = 56 * 1024 * 1024
ROWS_PER_LATCH = 1024
MESH = pl.DeviceIdType.MESH
N_CHIPS = 4
N_DEV = 8


def _nn(a, b):
    return jnp.dot(a, b, preferred_element_type=F32)


def _nt(a, b):
    return lax.dot_general(a, b, (((1,), (1,)), ((), ())), preferred_element_type=F32)


def _tn(a, b):
    return lax.dot_general(a, b, (((0,), (0,)), ((), ())), preferred_element_type=F32)


def _tile(n, pref, unit=LANES):
    t = min(n, pref)
    t -= t % unit
    while t > unit and n % t:
        t -= unit
    if t <= 0 or n % t:
        return n
    return t


def _params(ndim, vmem=True):
    return pltpu.CompilerParams(dimension_semantics=("arbitrary",) * ndim,
                                vmem_limit_bytes=VMEM_LIMIT if vmem else None)


def _sigmoid(x):
    return 1.0 / (1.0 + jnp.exp(-x))


def _fsum(x):
    return jnp.sum(jnp.sum(x, axis=1, keepdims=True), axis=0, keepdims=True)


def _rope_tables(L):
    lane = np.arange(LANES)
    d = lane % 64
    inv_r = jnp.asarray(ROPE_BASE, F32) ** (-jnp.arange(32, dtype=F32) / 32)
    t = jnp.arange(L)
    ang_r = t.astype(F32)[:, None] * jnp.tile(inv_r, LANES // 32)[None, :]
    Rr = np.zeros((LANES, LANES), np.float32)
    for l in range(LANES):
        if d[l] < 32:
            Rr[l + 32, l] = -1.0
        else:
            Rr[l - 32, l] = 1.0
    inv_a = jnp.asarray(ROPE_BASE, F32) ** (-jnp.arange(16, dtype=F32) / 16)
    rows = (t // GRID_W).astype(F32)
    cols = (t % GRID_W).astype(F32)
    dd = d % 32
    pos = jnp.where(jnp.asarray(d < 32)[None, :], rows[:, None], cols[:, None])
    ang_a = pos * jnp.tile(inv_a, LANES // 16)[None, :]
    Ra = np.zeros((LANES, LANES), np.float32)
    for l in range(LANES):
        if dd[l] < 16:
            Ra[l + 16, l] = -1.0
        else:
            Ra[l - 16, l] = 1.0
    D0 = np.zeros((LANES, LANES), np.float32)
    D1 = np.zeros((LANES, LANES), np.float32)
    for l in range(LANES):
        D0[l % 64, l] = 1.0
        D1[64 + l % 64, l] = 1.0
    return dict(
        Cr=jnp.cos(ang_r), Sr=jnp.sin(ang_r), Rr=jnp.asarray(Rr, BF16), RrT=jnp.asarray(Rr.T, BF16),
        Ca=jnp.cos(ang_a), Sa=jnp.sin(ang_a), Ra=jnp.asarray(Ra, BF16), RaT=jnp.asarray(Ra.T, BF16),
        D0=jnp.asarray(D0, BF16), D1=jnp.asarray(D1, BF16),
        D0T=jnp.asarray(D0.T, BF16), D1T=jnp.asarray(D1.T, BF16))


def _norm_mod(xf, g, sh, sc):
    r = lax.rsqrt(jnp.mean(xf * xf, axis=-1, keepdims=True) + NORM_EPS)
    return (xf * r * g) * (1.0 + sc) + sh


def _norm_mod_matmul(x, g, sh, sc, w, name):
    M, D = x.shape
    N = w.shape[1]
    tm, tn = _tile(M, ROWS_PER_LATCH, 8), _tile(N, 512)

    def body(x_ref, g_ref, sh_ref, sc_ref, w_ref, p_ref, h_ref, hs):
        @pl.when(pl.program_id(1) == 0)
        def _():
            hb = _norm_mod(x_ref[...], g_ref[...], sh_ref[...], sc_ref[...]).astype(BF16)
            hs[...] = hb
            h_ref[...] = hb
        p_ref[...] = _nn(hs[...], w_ref[...]).astype(BF16)

    vec = pl.BlockSpec((1, D), lambda i, j: (0, 0))
    return pl.pallas_call(
        body, name=name, grid=(M // tm, N // tn),
        in_specs=[pl.BlockSpec((tm, D), lambda i, j: (i, 0)), vec, vec, vec,
                  pl.BlockSpec((D, tn), lambda i, j: (0, j))],
        out_specs=[pl.BlockSpec((tm, tn), lambda i, j: (i, j)), pl.BlockSpec((tm, D), lambda i, j: (i, 0))],
        out_shape=[jax.ShapeDtypeStruct((M, N), BF16), jax.ShapeDtypeStruct((M, D), BF16)],
        scratch_shapes=[pltpu.VMEM((tm, D), BF16)],
        compiler_params=_params(2))(x, g, sh, sc, w)


def _proj_residual(a, w, xres, gt, name):
    M, K = a.shape
    N = w.shape[1]
    tm, tn = _tile(M, ROWS_PER_LATCH, 8), _tile(N, 512)

    def body(a_ref, w_ref, x_ref, gt_ref, xo_ref, o_ref):
        o = _nn(a_ref[...], w_ref[...])
        o_ref[...] = o.astype(BF16)
        xo_ref[...] = x_ref[...] + gt_ref[...] * o

    return pl.pallas_call(
        body, name=name, grid=(M // tm, N // tn),
        in_specs=[pl.BlockSpec((tm, K), lambda i, j: (i, 0)), pl.BlockSpec((K, tn), lambda i, j: (0, j)),
                  pl.BlockSpec((tm, tn), lambda i, j: (i, j)), pl.BlockSpec((1, tn), lambda i, j: (0, j))],
        out_specs=[pl.BlockSpec((tm, tn), lambda i, j: (i, j)), pl.BlockSpec((tm, tn), lambda i, j: (i, j))],
        out_shape=[jax.ShapeDtypeStruct((M, N), F32), jax.ShapeDtypeStruct((M, N), BF16)],
        compiler_params=_params(2))(a, w, xres, gt)


def _ffn_in(x1, g, sh, sc, wg, wu):
    M, D = x1.shape
    N = wg.shape[1]
    tm, tn = _tile(M, ROWS_PER_LATCH, 8), _tile(N, 512)

    def body(x_ref, g_ref, sh_ref, sc_ref, wg_ref, wu_ref, G_ref, U_ref, A_ref, h_ref, hs):
        @pl.when(pl.program_id(1) == 0)
        def _():
            hb = _norm_mod(x_ref[...], g_ref[...], sh_ref[...], sc_ref[...]).astype(BF16)
            hs[...] = hb
            h_ref[...] = hb
        G = _nn(hs[...], wg_ref[...])
        U = _nn(hs[...], wu_ref[...])
        G_ref[...] = G.astype(BF16)
        U_ref[...] = U.astype(BF16)
        A_ref[...] = (G * _sigmoid(G) * U).astype(BF16)

    vec = pl.BlockSpec((1, D), lambda i, j: (0, 0))
    wspec = pl.BlockSpec((D, tn), lambda i, j: (0, j))
    ospec = pl.BlockSpec((tm, tn), lambda i, j: (i, j))
    big = jax.ShapeDtypeStruct((M, N), BF16)
    return pl.pallas_call(
        body, name="ffn_in", grid=(M // tm, N // tn),
        in_specs=[pl.BlockSpec((tm, D), lambda i, j: (i, 0)), vec, vec, vec, wspec, wspec],
        out_specs=[ospec, ospec, ospec, pl.BlockSpec((tm, D), lambda i, j: (i, 0))],
        out_shape=[big, big, big, jax.ShapeDtypeStruct((M, D), BF16)],
        scratch_shapes=[pltpu.VMEM((tm, D), BF16)],
        compiler_params=_params(2))(x1, g, sh, sc, wg, wu)


def _final(x2, gn, tgt):
    M, D = x2.shape
    tm = _tile(M, 256, 8)

    def body(x_ref, g_ref, t_ref, dx_ref, loss_ref, dg_ref):
        @pl.when(pl.program_id(0) == 0)
        def _():
            loss_ref[...] = jnp.zeros_like(loss_ref)
            dg_ref[...] = jnp.zeros_like(dg_ref)
        x = x_ref[...]
        g = g_ref[...]
        r = lax.rsqrt(jnp.mean(x * x, axis=-1, keepdims=True) + NORM_EPS)
        xh = x * r
        e = xh * g - t_ref[...]
        loss_ref[...] += (0.5 / D) * _fsum(e * e)
        dy = e * (1.0 / D)
        dg_ref[...] += jnp.sum(dy * xh, axis=0, keepdims=True)
        dxh = dy * g
        dx_ref[...] = r * (dxh - xh * jnp.mean(dxh * xh, axis=-1, keepdims=True))

    row = pl.BlockSpec((tm, D), lambda i: (i, 0))
    return pl.pallas_call(
        body, name="final_loss", grid=(M // tm,),
        in_specs=[row, pl.BlockSpec((1, D), lambda i: (0, 0)), row],
        out_specs=[row, pl.BlockSpec((1, LANES), lambda i: (0, 0)), pl.BlockSpec((1, D), lambda i: (0, 0))],
        out_shape=[jax.ShapeDtypeStruct((M, D), F32), jax.ShapeDtypeStruct((1, LANES), F32),
                   jax.ShapeDtypeStruct((1, D), F32)],
        compiler_params=_params(1))(x2, gn, tgt)


def _col_group(blk0, nblk):
    return int(np.gcd(blk0, nblk)) if blk0 else nblk


def _rope_cols(src, blk0, nblk, Ct, St, R, scale, rope, name):
    M = src.shape[0]
    tm = _tile(M, 512, 8)
    wb = _col_group(blk0, nblk)

    def body(x_ref, c_ref, s_ref, r_ref, o_ref):
        for j in range(wb):
            cols = slice(j * LANES, (j + 1) * LANES)
            x = x_ref[:, cols]
            xf = x.astype(F32)
            if rope:
                xf = xf * c_ref[...] + _nn(x.astype(BF16), r_ref[...]) * s_ref[...]
            o_ref[:, cols] = (xf * scale).astype(BF16)

    tab = pl.BlockSpec((tm, LANES), lambda i, j: (i, 0))
    return pl.pallas_call(
        body, name=name, grid=(M // tm, nblk // wb),
        in_specs=[pl.BlockSpec((tm, wb * LANES), lambda i, j: (i, blk0 // wb + j)), tab, tab,
                  pl.BlockSpec((LANES, LANES), lambda i, j: (0, 0))],
        out_specs=pl.BlockSpec((tm, wb * LANES), lambda i, j: (i, j)),
        out_shape=jax.ShapeDtypeStruct((M, nblk * LANES), BF16),
        compiler_params=_params(2, False))(src, Ct, St, R)


def _dup_heads(src, blk0, npair, Ct, St, R, D0, D1, rope, name):
    M = src.shape[0]
    tm = _tile(M, 512, 8)

    def body(x_ref, c_ref, s_ref, r_ref, d0_ref, d1_ref, o_ref):
        x = x_ref[...]
        if rope:
            x = (x.astype(F32) * c_ref[...] + _nn(x, r_ref[...]) * s_ref[...]).astype(BF16)
        o_ref[0] = _nn(x, d0_ref[...]).astype(BF16)
        o_ref[1] = _nn(x, d1_ref[...]).astype(BF16)

    tab = pl.BlockSpec((tm, LANES), lambda i, p: (i, 0))
    mat = pl.BlockSpec((LANES, LANES), lambda i, p: (0, 0))
    return pl.pallas_call(
        body, name=name, grid=(M // tm, npair),
        in_specs=[pl.BlockSpec((tm, LANES), lambda i, p: (i, blk0 + p)), tab, tab, mat, mat, mat],
        out_specs=pl.BlockSpec((2, tm, LANES), lambda i, p: (p, i, 0)),
        out_shape=jax.ShapeDtypeStruct((2 * npair, M, LANES), BF16),
        compiler_params=_params(2, False))(src, Ct, St, R, D0, D1)


def _unrope_cols(dsrc, dst, blk0, nblk, Ct, St, RT, scale, rope, name):
    M = dsrc.shape[0]
    tm = _tile(M, 512, 8)
    wb = _col_group(blk0, nblk)

    def body(x_ref, c_ref, s_ref, r_ref, dst_ref, o_ref):
        del dst_ref
        for j in range(wb):
            cols = slice(j * LANES, (j + 1) * LANES)
            xf = x_ref[:, cols].astype(F32)
            if rope:
                xf = xf * c_ref[...] + _nn((xf * s_ref[...]).astype(BF16), r_ref[...])
            o_ref[:, cols] = (xf * scale).astype(BF16)

    tab = pl.BlockSpec((tm, LANES), lambda i, j: (i, 0))
    return pl.pallas_call(
        body, name=name, grid=(M // tm, nblk // wb),
        in_specs=[pl.BlockSpec((tm, wb * LANES), lambda i, j: (i, j)), tab, tab,
                  pl.BlockSpec((LANES, LANES), lambda i, j: (0, 0)),
                  pl.BlockSpec(memory_space=pl.ANY)],
        out_specs=pl.BlockSpec((tm, wb * LANES), lambda i, j: (i, blk0 // wb + j)),
        out_shape=jax.ShapeDtypeStruct(dst.shape, dst.dtype),
        input_output_aliases={4: 0},
        compiler_params=_params(2, False))(dsrc, Ct, St, RT, dst)


def _fold_heads(parts, dst, blk0, npair, Ct, St, RT, D0T, D1T, rope, name):
    M = parts[0][0].shape[1]
    nb = M // CHUNK
    R = _tile(M, 1024, CHUNK)
    rb = R // CHUNK
    nrefs = sum(1 if s == 0 else 2 for _, s in parts)

    def body(*refs):
        part_refs = list(refs[:nrefs])
        c_ref, s_ref, r_ref, d0_ref, d1_ref, dst_ref, o_ref = refs[nrefs:]
        del dst_ref
        i = pl.program_id(0)
        tot = [jnp.zeros((R, LANES), F32), jnp.zeros((R, LANES), F32)]
        for _, shift in parts:
            main = part_refs.pop(0)
            if shift == 0:
                for e in range(2):
                    tot[e] = tot[e] + main[e].astype(F32)
                continue
            edge = part_refs.pop(0)
            ok = (i + 1) * rb <= nb - 1 if shift > 0 else i > 0
            for e in range(2):
                ed = jnp.where(ok, edge[e].astype(F32), 0.0)
                if rb == 1:
                    tot[e] = tot[e] + ed
                elif shift > 0:
                    tot[e] = tot[e] + jnp.concatenate([main[e, CHUNK:, :].astype(F32), ed], axis=0)
                else:
                    tot[e] = tot[e] + jnp.concatenate([ed, main[e, :R - CHUNK, :].astype(F32)], axis=0)
        f = _nn(tot[0].astype(BF16), d0_ref[...]) + _nn(tot[1].astype(BF16), d1_ref[...])
        if rope:
            f = f * c_ref[...] + _nn((f * s_ref[...]).astype(BF16), r_ref[...])
        o_ref[...] = f.astype(BF16)

    in_specs, args = [], []
    for a, shift in parts:
        assert shift in (-1, 0, 1)
        in_specs.append(pl.BlockSpec((2, R, LANES), lambda i, p: (p, i, 0)))
        args.append(a)
        if shift > 0:
            in_specs.append(pl.BlockSpec((2, CHUNK, LANES), lambda i, p: (p, jnp.minimum((i + 1) * rb, nb - 1), 0)))
            args.append(a)
        elif shift < 0:
            in_specs.append(pl.BlockSpec((2, CHUNK, LANES), lambda i, p: (p, jnp.maximum(i * rb - 1, 0), 0)))
            args.append(a)
    tab = pl.BlockSpec((R, LANES), lambda i, p: (i, 0))
    mat = pl.BlockSpec((LANES, LANES), lambda i, p: (0, 0))
    return pl.pallas_call(
        body, name=name, grid=(M // R, npair),
        in_specs=in_specs + [tab, tab, mat, mat, mat, pl.BlockSpec(memory_space=pl.ANY)],
        out_specs=pl.BlockSpec((R, LANES), lambda i, p: (i, blk0 + p)),
        out_shape=jax.ShapeDtypeStruct(dst.shape, dst.dtype),
        input_output_aliases={nrefs + 5: 0},
        compiler_params=_params(2, False))(*args, Ct, St, RT, D0T, D1T, dst)


def _head_masks():
    lane = lax.broadcasted_iota(jnp.int32, (1, LANES), 1)
    return [lane < 64, lane >= 64]


def _decay_vecs(lam, mu):
    i = lax.broadcasted_iota(jnp.int32, (CHUNK, 1), 0).astype(F32)
    return dict(qf=jnp.exp(lam * (i + 1.0)), kf=jnp.exp(lam * (CHUNK - 1.0 - i)),
                qb=jnp.exp(mu * (CHUNK - i)), kb=jnp.exp(mu * i),
                gf=jnp.exp(lam * float(CHUNK)), gb=jnp.exp(mu * float(CHUNK)), i=i)


def _decay_mask(lam, mu):
    r = lax.broadcasted_iota(jnp.int32, (CHUNK, CHUNK), 0)
    c = lax.broadcasted_iota(jnp.int32, (CHUNK, CHUNK), 1)
    rel = (r - c).astype(F32)
    low = rel >= 0.0
    mf = jnp.exp(lam * jnp.maximum(rel, 0.0))
    mb = jnp.exp(mu * jnp.maximum(-rel, 0.0))
    return jnp.where(low, mf, mb), rel, low


def _lam_of(lg_ref, row, idx):
    return jnp.full((1, 1), lg_ref[row, idx], F32)


def _ret_states_fwd(Kr, P, Krc, Pc, lg, rv_blk, npairs):
    L = Kr.shape[0]
    Lc = Krc.shape[0]
    N, ncc = L // CHUNK, Lc // CHUNK

    def body(lg_ref, k_ref, v_ref, kc_ref, vc_ref, sf_ref, S):
        p, n = pl.program_id(0), pl.program_id(1)
        masks = _head_masks()
        for h in range(2):
            lam = _lam_of(lg_ref, 0, 2 * p + h)
            dv = _decay_vecs(lam, lam)
            hm = masks[h]

            @pl.when(n == 0)
            def _():
                s = jnp.zeros((LANES, LANES), F32)
                for cc in range(ncc):
                    kw = jnp.where(hm, kc_ref[cc * CHUNK:(cc + 1) * CHUNK, :].astype(F32) * dv["kf"], 0.0).astype(BF16)
                    s = dv["gf"] * s + _tn(kw, vc_ref[cc * CHUNK:(cc + 1) * CHUNK, h * LANES:(h + 1) * LANES])
                S[h] = s

            s = S[h]
            sf_ref[0, 0, h] = s.astype(BF16)
            kw = jnp.where(hm, k_ref[...].astype(F32) * dv["kf"], 0.0).astype(BF16)
            S[h] = dv["gf"] * s + _tn(kw, v_ref[:, h * LANES:(h + 1) * LANES])

    return pl.pallas_call(
        body, name="ret_states_fwd", grid=(npairs, N),
        in_specs=[pl.BlockSpec(memory_space=pltpu.SMEM),
                  pl.BlockSpec((CHUNK, LANES), lambda p, n: (n, p)),
                  pl.BlockSpec((CHUNK, 2 * LANES), lambda p, n: (n, rv_blk + p)),
                  pl.BlockSpec((Lc, LANES), lambda p, n: (0, p)),
                  pl.BlockSpec((Lc, 2 * LANES), lambda p, n: (0, rv_blk + p))],
        out_specs=pl.BlockSpec((1, 1, 2, LANES, LANES), lambda p, n: (p, n, 0, 0, 0)),
        out_shape=jax.ShapeDtypeStruct((npairs, N, 2, LANES, LANES), BF16),
        scratch_shapes=[pltpu.VMEM((2, LANES, LANES), F32)],
        compiler_params=_params(2, False))(lg, Kr, P, Krc, Pc)


def _ret_chunk_fwd(q, k, v, sf, sb, hm, lam, mu):
    dv = _decay_vecs(lam, mu)
    Mk, rel, low = _decay_mask(lam, mu)
    qm = jnp.where(hm, q, jnp.zeros_like(q))
    qmf = qm.astype(F32)
    A = _nt(qm, k)
    Am = A * Mk
    Amb = Am.astype(BF16)
    Qf = (qmf * dv["qf"]).astype(BF16)
    Qb = (qmf * dv["qb"]).astype(BF16)
    O = _nn(Amb, v) + _nn(Qf, sf) + _nn(Qb, sb)
    return dict(dv=dv, Mk=Mk, rel=rel, low=low, qm=qm, Am=Am, Amb=Amb, Qf=Qf, Qb=Qb, O=O)


def _ret_out_fwd(Qr, Kr, P, Krc, Pc, SF, lg, rv_blk, rg_blk, npairs, d_mix):
    L = Qr.shape[0]
    Lc = Krc.shape[0]
    N, ncc = L // CHUNK, Lc // CHUNK

    def body(lg_ref, q_ref, k_ref, v_ref, g_ref, sf_ref, kc_ref, vc_ref, y_ref, sb_ref, S):
        p, n = pl.program_id(0), pl.program_id(1)
        masks = _head_masks()
        for h in range(2):
            lam = _lam_of(lg_ref, 0, 2 * p + h)
            mu = _lam_of(lg_ref, 1, 2 * p + h)
            hm = masks[h]
            dvb = _decay_vecs(lam, mu)

            @pl.when(n == 0)
            def _():
                s = jnp.zeros((LANES, LANES), F32)
                for cc in reversed(range(ncc)):
                    kw = jnp.where(hm, kc_ref[cc * CHUNK:(cc + 1) * CHUNK, :].astype(F32) * dvb["kb"], 0.0).astype(BF16)
                    s = dvb["gb"] * s + _tn(kw, vc_ref[cc * CHUNK:(cc + 1) * CHUNK, h * LANES:(h + 1) * LANES])
                S[h] = s

            s = S[h]
            sbb = s.astype(BF16)
            sb_ref[0, 0, h] = sbb
            v = v_ref[:, h * LANES:(h + 1) * LANES]
            f = _ret_chunk_fwd(q_ref[...], k_ref[...], v, sf_ref[0, 0, h], sbb, hm, lam, mu)
            O = f["O"]
            r = lax.rsqrt(jnp.mean(O * O, axis=-1, keepdims=True) + NORM_EPS)
            g = g_ref[:, h * LANES:(h + 1) * LANES].astype(F32)
            y_ref[:, h * LANES:(h + 1) * LANES] = (O * r * (g * _sigmoid(g))).astype(BF16)
            kw = jnp.where(hm, k_ref[...].astype(F32) * dvb["kb"], 0.0).astype(BF16)
            S[h] = dvb["gb"] * s + _tn(kw, v)

    rev = lambda n: N - 1 - n
    return pl.pallas_call(
        body, name="ret_out_fwd", grid=(npairs, N),
        in_specs=[pl.BlockSpec(memory_space=pltpu.SMEM),
                  pl.BlockSpec((CHUNK, LANES), lambda p, n: (rev(n), p)),
                  pl.BlockSpec((CHUNK, LANES), lambda p, n: (rev(n), p)),
                  pl.BlockSpec((CHUNK, 2 * LANES), lambda p, n: (rev(n), rv_blk + p)),
                  pl.BlockSpec((CHUNK, 2 * LANES), lambda p, n: (rev(n), rg_blk + p)),
                  pl.BlockSpec((1, 1, 2, LANES, LANES), lambda p, n: (p, rev(n), 0, 0, 0)),
                  pl.BlockSpec((Lc, LANES), lambda p, n: (0, p)),
                  pl.BlockSpec((Lc, 2 * LANES), lambda p, n: (0, rv_blk + p))],
        out_specs=[pl.BlockSpec((CHUNK, 2 * LANES), lambda p, n: (rev(n), p)),
                   pl.BlockSpec((1, 1, 2, LANES, LANES), lambda p, n: (p, rev(n), 0, 0, 0))],
        out_shape=[jax.ShapeDtypeStruct((L, d_mix), BF16),
                   jax.ShapeDtypeStruct((npairs, N, 2, LANES, LANES), BF16)],
        scratch_shapes=[pltpu.VMEM((2, LANES, LANES), F32)],
        compiler_params=_params(2, False))(lg, Qr, Kr, P, P, SF, Krc, Pc)


ACC_ROWS = 8


def _ret_bwd1(Qr, Kr, P, Krc, Pc, SF, SB, dY, lg, rv_blk, rg_blk, npairs, d_proj):
    L = Qr.shape[0]
    Lc = Krc.shape[0]
    N, ncc = L // CHUNK, Lc // CHUNK

    def body(lg_ref, q_ref, k_ref, v_ref, g_ref, sf_ref, sb_ref, dy_ref, kc_ref, vc_ref,
             dq_ref, dk_ref, dv_ref, dg_ref, do_ref, dkc_ref, dvc_ref, acc_ref, dS, T):
        p, n = pl.program_id(0), pl.program_id(1)
        masks = _head_masks()

        @pl.when(n == 0)
        def _():
            dS[...] = jnp.zeros_like(dS)
            T[...] = jnp.zeros_like(T)
            acc_ref[...] = jnp.zeros_like(acc_ref)

        dq_tot = jnp.zeros((CHUNK, LANES), F32)
        dk_tot = jnp.zeros((CHUNK, LANES), F32)
        for h in range(2):
            lam = _lam_of(lg_ref, 0, 2 * p + h)
            mu = _lam_of(lg_ref, 1, 2 * p + h)
            hm = masks[h]
            hs = slice(h * LANES, (h + 1) * LANES)
            v = v_ref[:, hs]
            k = k_ref[...]
            sf = sf_ref[0, 0, h]
            sb = sb_ref[0, 0, h]
            f = _ret_chunk_fwd(q_ref[...], k, v, sf, sb, hm, lam, mu)
            dv_, O = f["dv"], f["O"]
            r = lax.rsqrt(jnp.mean(O * O, axis=-1, keepdims=True) + NORM_EPS)
            on = O * r
            g = g_ref[:, hs].astype(F32)
            sg = _sigmoid(g)
            dy = dy_ref[:, hs].astype(F32)
            dg_ref[:, hs] = (dy * on * (sg * (1.0 + g * (1.0 - sg)))).astype(BF16)
            don = dy * (g * sg)
            dO = r * (don - on * jnp.mean(don * on, axis=-1, keepdims=True))
            dOb = dO.astype(BF16)
            do_ref[:, hs] = dOb
            dAm = _nt(dOb, v)
            T[h] += dAm * f["Am"]
            dAb = (dAm * f["Mk"]).astype(BF16)
            km = jnp.where(hm, k, jnp.zeros_like(k))
            dq = _nn(dAb, km)
            dk = _tn(dAb, f["qm"])
            dvh = _tn(f["Amb"], dOb)
            dQf = _nt(dOb, sf)
            dQb = _nt(dOb, sb)
            dq = dq + dQf * dv_["qf"] + dQb * dv_["qb"]
            acc_ref[0, h, 0:1, :] += _fsum(dQf * f["Qf"].astype(F32) * (dv_["i"] + 1.0))
            acc_ref[0, h, 1:2, :] += _fsum(dQb * f["Qb"].astype(F32) * (CHUNK - dv_["i"]))
            dSh = dS[h]
            dSb_ = dSh.astype(BF16)
            Kf = (km.astype(F32) * dv_["kf"]).astype(BF16)
            dKf = _nt(v, dSb_)
            dk = dk + jnp.where(hm, dKf * dv_["kf"], 0.0)
            acc_ref[0, h, 2:3, :] += _fsum(jnp.where(hm, dKf, 0.0) * Kf.astype(F32) * (CHUNK - 1.0 - dv_["i"]))
            dvh = dvh + _nn(Kf, dSb_)
            acc_ref[0, h, 3:4, :] += float(CHUNK) * dv_["gf"] * _fsum(dSh * sf.astype(F32))
            dSh = dv_["gf"] * dSh + _tn(f["Qf"], dOb)
            dS[h] = dSh
            dv_ref[:, hs] = dvh
            dq_tot = dq_tot + dq
            dk_tot = dk_tot + dk

        dq_ref[...] = dq_tot
        dk_ref[...] = dk_tot

        @pl.when(n == N - 1)
        def _():
            for h in range(2):
                lam = _lam_of(lg_ref, 0, 2 * p + h)
                dv_ = _decay_vecs(lam, lam)
                hm = masks[h]
                hs = slice(h * LANES, (h + 1) * LANES)
                states = [jnp.zeros((LANES, LANES), F32)]
                kws = []
                for cc in range(ncc):
                    kw = jnp.where(hm, kc_ref[cc * CHUNK:(cc + 1) * CHUNK, :].astype(F32) * dv_["kf"], 0.0).astype(BF16)
                    kws.append(kw)
                    states.append(dv_["gf"] * states[-1] + _tn(kw, vc_ref[cc * CHUNK:(cc + 1) * CHUNK, hs]))
                d = dS[h]
                for cc in reversed(range(ncc)):
                    db = d.astype(BF16)
                    rows = slice(cc * CHUNK, (cc + 1) * CHUNK)
                    dKf_c = jnp.where(hm, _nt(vc_ref[rows, hs], db), 0.0)
                    part = dKf_c * dv_["kf"]
                    if h == 0:
                        dkc_ref[rows, :] = part
                    else:
                        dkc_ref[rows, :] += part
                    acc_ref[0, h, 2:3, :] += _fsum(dKf_c * kws[cc].astype(F32) * (CHUNK - 1.0 - dv_["i"]))
                    dvc_ref[rows, hs] = _nn(kws[cc], db)
                    acc_ref[0, h, 3:4, :] += float(CHUNK) * dv_["gf"] * _fsum(d * states[cc])
                    d = dv_["gf"] * d
                _, rel, low = _decay_mask(lam, lam)
                Th = T[h]
                acc_ref[0, h, 4:5, :] += _fsum(jnp.where(low, Th * rel, 0.0))
                acc_ref[0, h, 5:6, :] += _fsum(jnp.where(low, 0.0, -Th * rel))

    rev = lambda n: N - 1 - n
    st = pl.BlockSpec((1, 1, 2, LANES, LANES), lambda p, n: (p, rev(n), 0, 0, 0))
    pair = pl.BlockSpec((CHUNK, LANES), lambda p, n: (rev(n), p))
    wide = lambda b0: pl.BlockSpec((CHUNK, 2 * LANES), lambda p, n: (rev(n), b0 + p))
    return pl.pallas_call(
        body, name="ret_bwd_desc", grid=(npairs, N),
        in_specs=[pl.BlockSpec(memory_space=pltpu.SMEM), pair, pair, wide(rv_blk), wide(rg_blk), st, st, wide(0),
                  pl.BlockSpec((Lc, LANES), lambda p, n: (0, p)),
                  pl.BlockSpec((Lc, 2 * LANES), lambda p, n: (0, rv_blk + p))],
        out_specs=[pair, pair, wide(0), wide(rg_blk), wide(0),
                   pl.BlockSpec((Lc, LANES), lambda p, n: (0, p)),
                   pl.BlockSpec((Lc, 2 * LANES), lambda p, n: (0, p)),
                   pl.BlockSpec((1, 2, ACC_ROWS, LANES), lambda p, n: (p, 0, 0, 0))],
        out_shape=[jax.ShapeDtypeStruct((L, npairs * LANES), F32),
                   jax.ShapeDtypeStruct((L, npairs * LANES), F32),
                   jax.ShapeDtypeStruct((L, npairs * 2 * LANES), F32),
                   jax.ShapeDtypeStruct((L, d_proj), BF16),
                   jax.ShapeDtypeStruct((L, npairs * 2 * LANES), BF16),
                   jax.ShapeDtypeStruct((Lc, npairs * LANES), F32),
                   jax.ShapeDtypeStruct((Lc, npairs * 2 * LANES), F32),
                   jax.ShapeDtypeStruct((npairs, 2, ACC_ROWS, LANES), F32)],
        scratch_shapes=[pltpu.VMEM((2, LANES, LANES), F32), pltpu.VMEM((2, CHUNK, CHUNK), F32)],
        compiler_params=_params(2, False))(lg, Qr, Kr, P, P, SF, SB, dY, Krc, Pc)


def _ret_bwd2(Qr, Kr, P, Krc, Pc, SB, dO, dKr, dVp, dP, dKc, dVc, lg, rv_blk, npairs):
    L = Qr.shape[0]
    Lc = Krc.shape[0]
    N, ncc = L // CHUNK, Lc // CHUNK

    def body(lg_ref, q_ref, k_ref, v_ref, sb_ref, do_ref, dkin_ref, dvin_ref, kc_ref, vc_ref, dkcin_ref, dvcin_ref,
             dpin_ref, dk_ref, dv_ref, dkc_ref, dvc_ref, acc_ref, dS):
        del dpin_ref
        p, n = pl.program_id(0), pl.program_id(1)
        masks = _head_masks()

        @pl.when(n == 0)
        def _():
            dS[...] = jnp.zeros_like(dS)
            acc_ref[...] = jnp.zeros_like(acc_ref)

        dk_tot = dkin_ref[...]
        for h in range(2):
            mu = _lam_of(lg_ref, 1, 2 * p + h)
            hm = masks[h]
            hs = slice(h * LANES, (h + 1) * LANES)
            dv_ = _decay_vecs(mu, mu)
            v = v_ref[:, hs]
            k = k_ref[...]
            q = q_ref[...]
            dOb = do_ref[:, hs]
            km = jnp.where(hm, k, jnp.zeros_like(k)).astype(F32)
            Kb = (km * dv_["kb"]).astype(BF16)
            Qb = (jnp.where(hm, q, jnp.zeros_like(q)).astype(F32) * dv_["qb"]).astype(BF16)
            dSh = dS[h]
            dSb_ = dSh.astype(BF16)
            dKb = jnp.where(hm, _nt(v, dSb_), 0.0)
            dk_tot = dk_tot + dKb * dv_["kb"]
            acc_ref[0, h, 0:1, :] += _fsum(dKb * Kb.astype(F32) * dv_["i"])
            dv_ref[:, hs] = (dvin_ref[:, hs] + _nn(Kb, dSb_)).astype(BF16)
            acc_ref[0, h, 1:2, :] += float(CHUNK) * dv_["gb"] * _fsum(dSh * sb_ref[0, 0, h].astype(F32))
            dSh = dv_["gb"] * dSh + _tn(Qb, dOb)
            dS[h] = dSh

        dk_ref[...] = dk_tot

        @pl.when(n == N - 1)
        def _():
            for h in range(2):
                mu = _lam_of(lg_ref, 1, 2 * p + h)
                hm = masks[h]
                hs = slice(h * LANES, (h + 1) * LANES)
                dv_ = _decay_vecs(mu, mu)
                states = {}
                kws = {}
                s = jnp.zeros((LANES, LANES), F32)
                for cc in reversed(range(ncc)):
                    states[cc] = s
                    kw = jnp.where(hm, kc_ref[cc * CHUNK:(cc + 1) * CHUNK, :].astype(F32) * dv_["kb"], 0.0).astype(BF16)
                    kws[cc] = kw
                    s = dv_["gb"] * s + _tn(kw, vc_ref[cc * CHUNK:(cc + 1) * CHUNK, hs])
                d = dS[h]
                for cc in range(ncc):
                    db = d.astype(BF16)
                    rows = slice(cc * CHUNK, (cc + 1) * CHUNK)
                    dKb_c = jnp.where(hm, _nt(vc_ref[rows, hs], db), 0.0)
                    part = dKb_c * dv_["kb"]
                    if h == 0:
                        dkc_ref[rows, :] = dkcin_ref[rows, :] + part
                    else:
                        dkc_ref[rows, :] += part
                    acc_ref[0, h, 0:1, :] += _fsum(dKb_c * kws[cc].astype(F32) * dv_["i"])
                    dvc_ref[rows, hs] = dvcin_ref[rows, hs] + _nn(kws[cc], db)
                    acc_ref[0, h, 1:2, :] += float(CHUNK) * dv_["gb"] * _fsum(d * states[cc])
                    d = dv_["gb"] * d

    st = pl.BlockSpec((1, 1, 2, LANES, LANES), lambda p, n: (p, n, 0, 0, 0))
    pair = pl.BlockSpec((CHUNK, LANES), lambda p, n: (n, p))
    wide = lambda b0: pl.BlockSpec((CHUNK, 2 * LANES), lambda p, n: (n, b0 + p))
    ckc = pl.BlockSpec((Lc, LANES), lambda p, n: (0, p))
    cvc = lambda b0: pl.BlockSpec((Lc, 2 * LANES), lambda p, n: (0, b0 + p))
    return pl.pallas_call(
        body, name="ret_bwd_asc", grid=(npairs, N),
        in_specs=[pl.BlockSpec(memory_space=pltpu.SMEM), pair, pair, wide(rv_blk), st, wide(0), pair, wide(0),
                  ckc, cvc(rv_blk), ckc, cvc(0), pl.BlockSpec(memory_space=pl.ANY)],
        out_specs=[pair, wide(rv_blk), ckc, cvc(0),
                   pl.BlockSpec((1, 2, ACC_ROWS, LANES), lambda p, n: (p, 0, 0, 0))],
        out_shape=[jax.ShapeDtypeStruct(dKr.shape, F32),
                   jax.ShapeDtypeStruct(dP.shape, dP.dtype),
                   jax.ShapeDtypeStruct(dKc.shape, F32),
                   jax.ShapeDtypeStruct(dVc.shape, F32),
                   jax.ShapeDtypeStruct((npairs, 2, ACC_ROWS, LANES), F32)],
        input_output_aliases={12: 1},
        scratch_shapes=[pltpu.VMEM((2, LANES, LANES), F32)],
        compiler_params=_params(2, False))(lg, Qr, Kr, P, SB, dO, dKr, dVp, Krc, Pc, dKc, dVc, dP)


GROUP = 4


def _att_valid(n, N, Lc):
    rows = GROUP * CHUNK
    row = lax.broadcasted_iota(jnp.int32, (rows, 3 * CHUNK + Lc), 0) % CHUNK
    col = lax.broadcasted_iota(jnp.int32, (rows, 3 * CHUNK + Lc), 1)
    ok = jnp.logical_and(col >= row, col <= row + 2 * CHUNK)
    ok = jnp.logical_and(ok, jnp.logical_or(col >= CHUNK, n > 0))
    ok = jnp.logical_and(ok, jnp.logical_or(col < 2 * CHUNK, n < N - 1))
    return jnp.logical_or(ok, col >= 3 * CHUNK)


def _stack_heads(ref):
    masks = _head_masks()
    tiles = []
    for pr in range(2):
        t = ref[:, pr * LANES:(pr + 1) * LANES]
        for a in range(2):
            tiles.append(jnp.where(masks[a], t, jnp.zeros_like(t)))
    return jnp.concatenate(tiles, axis=0)


def _unstack_heads(x4):
    m0 = _head_masks()[0]
    return [jnp.where(m0, x4[(2 * pr) * CHUNK:(2 * pr + 1) * CHUNK], x4[(2 * pr + 1) * CHUNK:(2 * pr + 2) * CHUNK])
            for pr in range(2)]


def _sink_column(sink_ref, g):
    row = lax.broadcasted_iota(jnp.int32, (GROUP * CHUNK, 1), 0) // CHUNK
    col = jnp.zeros((GROUP * CHUNK, 1), F32)
    for h in range(GROUP):
        col = jnp.where(row == h, sink_ref[0, g * GROUP + h], col)
    return col


def _att_probs(q4, Kall, valid, snk):
    s = jnp.where(valid, _nt(q4, Kall), NEG)
    mx = jnp.maximum(jnp.max(s, axis=1, keepdims=True), snk)
    p = jnp.exp(s - mx)
    p_snk = jnp.exp(snk - mx)
    inv = 1.0 / (jnp.sum(p, axis=1, keepdims=True) + p_snk)
    return p, p_snk, inv


def _att_specs(Lc, N):
    q = pl.BlockSpec((CHUNK, 2 * LANES), lambda g, n: (n, g))
    kv = lambda s: pl.BlockSpec((1, CHUNK, LANES), lambda g, n: (g, jnp.clip(n + s, 0, N - 1), 0))
    ctx = pl.BlockSpec((1, Lc, LANES), lambda g, n: (g, 0, 0))
    return q, kv, ctx


def _att_fwd(Qa, Kd, Vd, Kdc, Vdc, sink, Y, blk0):
    L = Qa.shape[0]
    Lc = Kdc.shape[1]
    N = L // CHUNK
    nkv = Kd.shape[0]

    def body(sink_ref, q_ref, kp, kc_, kn, vp, vc_, vn, kctx, vctx, y_in, o_ref):
        del y_in
        g, n = pl.program_id(0), pl.program_id(1)
        Kall = jnp.concatenate([kp[0], kc_[0], kn[0], kctx[0]], axis=0)
        Vall = jnp.concatenate([vp[0], vc_[0], vn[0], vctx[0]], axis=0)
        p, _, inv = _att_probs(_stack_heads(q_ref), Kall, _att_valid(n, N, Lc), _sink_column(sink_ref, g))
        o4 = _nn(p.astype(BF16), Vall) * inv
        for pr, o in enumerate(_unstack_heads(o4)):
            o_ref[:, pr * LANES:(pr + 1) * LANES] = o.astype(BF16)

    q, kv, ctx = _att_specs(Lc, N)
    return pl.pallas_call(
        body, name="att_fwd", grid=(nkv, N),
        in_specs=[pl.BlockSpec(memory_space=pltpu.SMEM), q, kv(-1), kv(0), kv(1), kv(-1), kv(0), kv(1), ctx, ctx,
                  pl.BlockSpec(memory_space=pl.ANY)],
        out_specs=pl.BlockSpec((CHUNK, 2 * LANES), lambda g, n: (n, blk0 + g)),
        out_shape=jax.ShapeDtypeStruct(Y.shape, Y.dtype),
        input_output_aliases={10: 0},
        compiler_params=_params(2))(sink, Qa, Kd, Kd, Kd, Vd, Vd, Vd, Kdc, Vdc, Y)


def _att_bwd(Qa, Kd, Vd, Kdc, Vdc, sink, dY, blk0):
    L = Qa.shape[0]
    Lc = Kdc.shape[1]
    N = L // CHUNK
    nkv = Kd.shape[0]

    def body(sink_ref, q_ref, kp, kc_, kn, vp, vc_, vn, kctx, vctx, dy_ref,
             dq_ref, dkp, dkc_, dkn, dvp, dvc_, dvn, dkctx, dvctx, dsink_ref):
        g, n = pl.program_id(0), pl.program_id(1)

        @pl.when(n == 0)
        def _():
            dkctx[...] = jnp.zeros_like(dkctx)
            dvctx[...] = jnp.zeros_like(dvctx)
            dsink_ref[...] = jnp.zeros_like(dsink_ref)

        Kall = jnp.concatenate([kp[0], kc_[0], kn[0], kctx[0]], axis=0)
        Vall = jnp.concatenate([vp[0], vc_[0], vn[0], vctx[0]], axis=0)
        q4 = _stack_heads(q_ref)
        do4 = _stack_heads(dy_ref)
        p, p_snk, inv = _att_probs(q4, Kall, _att_valid(n, N, Lc), _sink_column(sink_ref, g))
        P = p * inv
        dp = _nt(do4, Vall)
        delta = jnp.sum(P * dp, axis=1, keepdims=True)
        ds = (P * (dp - delta)).astype(BF16)
        dsnk = -(p_snk * inv) * delta
        for h in range(GROUP):
            dsink_ref[0, h:h + 1, :] += _fsum(dsnk[h * CHUNK:(h + 1) * CHUNK])
        for pr, dq in enumerate(_unstack_heads(_nn(ds, Kall))):
            dq_ref[:, pr * LANES:(pr + 1) * LANES] = dq
        dK = _tn(ds, q4)
        dV = _tn(P.astype(BF16), do4)
        for j, (rk, rv) in enumerate([(dkp, dvp), (dkc_, dvc_), (dkn, dvn)]):
            rk[0] = dK[j * CHUNK:(j + 1) * CHUNK].astype(BF16)
            rv[0] = dV[j * CHUNK:(j + 1) * CHUNK].astype(BF16)
        dkctx[0] += dK[3 * CHUNK:]
        dvctx[0] += dV[3 * CHUNK:]

    q, kv, ctx = _att_specs(Lc, N)
    blk = pl.BlockSpec((1, CHUNK, LANES), lambda g, n: (g, n, 0))
    part = jax.ShapeDtypeStruct((nkv, L, LANES), BF16)
    cshape = jax.ShapeDtypeStruct((nkv, Lc, LANES), F32)
    return pl.pallas_call(
        body, name="att_bwd", grid=(nkv, N),
        in_specs=[pl.BlockSpec(memory_space=pltpu.SMEM), q, kv(-1), kv(0), kv(1), kv(-1), kv(0), kv(1), ctx, ctx,
                  pl.BlockSpec((CHUNK, 2 * LANES), lambda g, n: (n, blk0 + g))],
        out_specs=[q, blk, blk, blk, blk, blk, blk, ctx, ctx,
                   pl.BlockSpec((1, 8, LANES), lambda g, n: (g, 0, 0))],
        out_shape=[jax.ShapeDtypeStruct(Qa.shape, F32), part, part, part, part, part, part, cshape, cshape,
                   jax.ShapeDtypeStruct((nkv, 8, LANES), F32)],
        compiler_params=_params(2))(sink, Qa, Kd, Kd, Kd, Vd, Vd, Vd, Kdc, Vdc, dY)


def _bwd_proj(dx, gt, w, saved, G=None, U=None, name="bwd_proj"):
    M, D = dx.shape
    N = w.shape[0]
    swiglu = G is not None
    tm, tn = _tile(M, ROWS_PER_LATCH, 8), _tile(N, 256 if swiglu else 512)

    def body(*refs):
        if swiglu:
            dx_ref, gt_ref, w_ref, sv_ref, G_ref, U_ref, dG_ref, dU_ref, dz_ref, dgt_ref, zs = refs
        else:
            dx_ref, gt_ref, w_ref, sv_ref, dA_ref, dz_ref, dgt_ref, zs = refs
        i, j = pl.program_id(0), pl.program_id(1)

        @pl.when(jnp.logical_and(i == 0, j == 0))
        def _():
            dgt_ref[...] = jnp.zeros_like(dgt_ref)

        @pl.when(j == 0)
        def _():
            d = dx_ref[...]
            z = (d * gt_ref[...]).astype(BF16)
            zs[...] = z
            dz_ref[...] = z
            dgt_ref[...] += jnp.sum(d * sv_ref[...].astype(F32), axis=0, keepdims=True)

        dA = _nt(zs[...], w_ref[...])
        if swiglu:
            Gv = G_ref[...].astype(F32)
            Uv = U_ref[...].astype(F32)
            sg = _sigmoid(Gv)
            dU_ref[...] = (dA * Gv * sg).astype(BF16)
            dG_ref[...] = (dA * Uv * (sg * (1.0 + Gv * (1.0 - sg)))).astype(BF16)
        else:
            dA_ref[...] = dA.astype(BF16)

    row = pl.BlockSpec((tm, D), lambda i, j: (i, 0))
    vec = pl.BlockSpec((1, D), lambda i, j: (0, 0))
    tile = pl.BlockSpec((tm, tn), lambda i, j: (i, j))
    big = jax.ShapeDtypeStruct((M, N), BF16)
    in_specs = [row, vec, pl.BlockSpec((tn, D), lambda i, j: (j, 0)), row]
    args = [dx, gt, w, saved]
    if swiglu:
        in_specs += [tile, tile]
        args += [G, U]
        out_specs = [tile, tile, row, vec]
        out_shape = [big, big, jax.ShapeDtypeStruct((M, D), BF16), jax.ShapeDtypeStruct((1, D), F32)]
    else:
        out_specs = [tile, row, vec]
        out_shape = [big, jax.ShapeDtypeStruct((M, D), BF16), jax.ShapeDtypeStruct((1, D), F32)]
    return pl.pallas_call(
        body, name=name, grid=(M // tm, N // tn), in_specs=in_specs, out_specs=out_specs, out_shape=out_shape,
        scratch_shapes=[pltpu.VMEM((tm, D), BF16)], compiler_params=_params(2))(*args)


def _tn_matmul(pairs, name):
    Ka, Nb = pairs[0][0].shape[1], pairs[0][1].shape[1]
    tk, tn = _tile(Ka, 2048), _tile(Nb, 2048)
    tls, nks = [], []
    for a, _ in pairs:
        tl = _tile(a.shape[0], 512, 8)
        tls.append(tl)
        nks.append(a.shape[0] // tl)
    starts = [int(s) for s in np.cumsum([0] + nks[:-1])]
    nk = int(sum(nks))

    def body(*refs):
        out_ref, acc = refs[-2], refs[-1]
        k = pl.program_id(2)

        @pl.when(k == 0)
        def _():
            acc[...] = jnp.zeros_like(acc)

        for idx in range(len(pairs)):
            a_ref, b_ref = refs[2 * idx], refs[2 * idx + 1]

            @pl.when(jnp.logical_and(k >= starts[idx], k < starts[idx] + nks[idx]))
            def _():
                acc[...] += _tn(a_ref[...], b_ref[...])

        @pl.when(k == nk - 1)
        def _():
            out_ref[...] = acc[...].astype(BF16)

    in_specs, args = [], []
    for idx, (a, b) in enumerate(pairs):
        s0, n_ = starts[idx], nks[idx]
        in_specs.append(pl.BlockSpec((tls[idx], tk), lambda i, j, k, s0=s0, n_=n_: (jnp.clip(k - s0, 0, n_ - 1), i)))
        in_specs.append(pl.BlockSpec((tls[idx], tn), lambda i, j, k, s0=s0, n_=n_: (jnp.clip(k - s0, 0, n_ - 1), j)))
        args += [a, b]
    return pl.pallas_call(
        body, name=name, grid=(Ka // tk, Nb // tn, nk), in_specs=in_specs,
        out_specs=pl.BlockSpec((tk, tn), lambda i, j, k: (i, j)),
        out_shape=jax.ShapeDtypeStruct((Ka, Nb), BF16),
        scratch_shapes=[pltpu.VMEM((tk, tn), F32)], compiler_params=_params(3))(*args)


def _bwd_norm_mod(pairs, x, dres, g, sh, sc, name):
    M, D = x.shape
    K = pairs[0][0].shape[1]
    tm, tk = _tile(M, 512, 8), _tile(K, 1152 if len(pairs) == 1 else 512)
    nk = K // tk
    npair = len(pairs)
    has_res = dres is not None

    def body(*refs):
        pr = refs[:2 * npair]
        rest = refs[2 * npair:]
        if has_res:
            x_ref, dres_ref, g_ref, sh_ref, sc_ref, dx_ref, st_ref, acc = rest
        else:
            x_ref, g_ref, sh_ref, sc_ref, dx_ref, st_ref, acc = rest
        del sh_ref
        i, k = pl.program_id(0), pl.program_id(1)

        @pl.when(jnp.logical_and(i == 0, k == 0))
        def _():
            st_ref[...] = jnp.zeros_like(st_ref)

        @pl.when(k == 0)
        def _():
            acc[...] = jnp.zeros_like(acc)

        for idx in range(npair):
            acc[...] += _nt(pr[2 * idx + 1][...], pr[2 * idx][...])

        @pl.when(k == nk - 1)
        def _():
            xv = x_ref[...]
            gv = g_ref[...]
            dh = acc[...].T
            r = lax.rsqrt(jnp.mean(xv * xv, axis=-1, keepdims=True) + NORM_EPS)
            xh = xv * r
            st_ref[0:1, :] += jnp.sum(dh, axis=0, keepdims=True)
            st_ref[1:2, :] += jnp.sum(dh * (xh * gv), axis=0, keepdims=True)
            dn = dh * (1.0 + sc_ref[...])
            st_ref[2:3, :] += jnp.sum(dn * xh, axis=0, keepdims=True)
            dxh = dn * gv
            d = r * (dxh - xh * jnp.mean(dxh * xh, axis=-1, keepdims=True))
            if has_res:
                d = d + dres_ref[...]
            dx_ref[...] = d

    row = pl.BlockSpec((tm, D), lambda i, k: (i, 0))
    vec = pl.BlockSpec((1, D), lambda i, k: (0, 0))
    in_specs, args = [], []
    for dA, w in pairs:
        in_specs += [pl.BlockSpec((tm, tk), lambda i, k: (i, k)), pl.BlockSpec((D, tk), lambda i, k: (0, k))]
        args += [dA, w]
    in_specs += [row] + ([row] if has_res else []) + [vec, vec, vec]
    args += [x] + ([dres] if has_res else []) + [g, sh, sc]
    return pl.pallas_call(
        body, name=name, grid=(M // tm, nk), in_specs=in_specs,
        out_specs=[row, pl.BlockSpec((8, D), lambda i, k: (0, 0))],
        out_shape=[jax.ShapeDtypeStruct((M, D), F32), jax.ShapeDtypeStruct((8, D), F32)],
        scratch_shapes=[pltpu.VMEM((D, tm), F32)], compiler_params=_params(2))(*args)


def _local_step(x, ctx, tgt, mod, modc, norm_mix, norm_ffn, norm_final, lg, sink, w_in, rest_weights, on_grads):
    L, D = x.shape
    Lc = ctx.shape[0]
    d_proj = w_in.shape[1]
    npairs = RET_HEADS // 2
    nkv = ATT_KV_HEADS
    nkvp = nkv // 2
    o_rq = 0
    o_rk = o_rq + RET_HEADS * RET_DK // LANES
    o_rv = o_rk + RET_HEADS * RET_DK // LANES
    o_rg = o_rv + RET_HEADS * RET_DV // LANES
    o_aq = o_rg + RET_HEADS * RET_DV // LANES
    o_ak = o_aq + ATT_HEADS * ATT_DH // LANES
    o_av = o_ak + nkv * ATT_DH // LANES
    assert (o_av + nkv * ATT_DH // LANES) * LANES == d_proj
    assert o_rv % 2 == 0 and o_rg % 2 == 0 and (RET_HEADS * RET_DV) % (2 * LANES) == 0
    rv_blk, rg_blk = o_rv // 2, o_rg // 2
    d_ret = RET_HEADS * RET_DV
    d_mix = d_ret + ATT_HEADS * ATT_DH
    att_blk = d_ret // (2 * LANES)
    k_scale = RET_DK ** -0.5
    a_scale = ATT_DH ** -0.5

    T = _rope_tables(L)
    Tc = dict(C=jnp.ones((Lc, LANES), F32), S=jnp.zeros((Lc, LANES), F32))
    row = lambda m, i: m[i:i + 1]
    sh_m, sc_m, gt_m, sh_f, sc_f, gt_f = [row(mod, i) for i in range(6)]
    sh_mc, sc_mc = row(modc, 0), row(modc, 1)

    P, hx = _norm_mod_matmul(x, norm_mix, sh_m, sc_m, w_in, "in_proj")
    Pc, hc = _norm_mod_matmul(ctx, norm_mix, sh_mc, sc_mc, w_in, "in_proj_ctx")
    nq = RET_HEADS * RET_DK // LANES
    Qr = _rope_cols(P, o_rq, nq, T["Cr"], T["Sr"], T["Rr"], 1.0, True, "rope_rq")
    Kr = _rope_cols(P, o_rk, nq, T["Cr"], T["Sr"], T["Rr"], k_scale, True, "rope_rk")
    Krc = _rope_cols(Pc, o_rk, nq, Tc["C"], Tc["S"], T["Rr"], k_scale, False, "scale_rk_ctx")
    Qa = _rope_cols(P, o_aq, ATT_HEADS * ATT_DH // LANES, T["Ca"], T["Sa"], T["Ra"], a_scale, True, "rope_aq")
    Kd = _dup_heads(P, o_ak, nkvp, T["Ca"], T["Sa"], T["Ra"], T["D0"], T["D1"], True, "dup_ak")
    Vd = _dup_heads(P, o_av, nkvp, T["Ca"], T["Sa"], T["Ra"], T["D0"], T["D1"], False, "dup_av")
    Kdc = _dup_heads(Pc, o_ak, nkvp, Tc["C"], Tc["S"], T["Ra"], T["D0"], T["D1"], False, "dup_ak_ctx")
    Vdc = _dup_heads(Pc, o_av, nkvp, Tc["C"], Tc["S"], T["Ra"], T["D0"], T["D1"], False, "dup_av_ctx")

    SF = _ret_states_fwd(Kr, P, Krc, Pc, lg, rv_blk, npairs)
    Y, SB = _ret_out_fwd(Qr, Kr, P, Krc, Pc, SF, lg, rv_blk, rg_blk, npairs, d_mix)
    Y = _att_fwd(Qa, Kd, Vd, Kdc, Vdc, sink, Y, att_blk)

    w_out, w_gate, w_up, w_down = rest_weights(Y)
    x1, O1 = _proj_residual(Y, w_out, x, gt_m, "out_proj")
    G, U, A, h2 = _ffn_in(x1, norm_ffn, sh_f, sc_f, w_gate, w_up)
    x2, Fo = _proj_residual(A, w_down, x1, gt_f, "ffn_out")
    dx2, loss, d_norm_final = _final(x2, norm_final, tgt)

    dG, dU, dz2, dgt_f = _bwd_proj(dx2, gt_f, w_down, Fo, G, U, name="ffn_out_bwd")
    g_w_down = _tn_matmul([(A, dz2)], "grad_w_down")
    tok = on_grads(["w_down"], [g_w_down])
    dx1, st_f = _bwd_norm_mod([(dG, w_gate), (dU, w_up)], x1, dx2, norm_ffn + tok, sh_f, sc_f, "ffn_in_bwd")
    g_w_gate = _tn_matmul([(h2, dG)], "grad_w_gate")
    g_w_up = _tn_matmul([(h2, dU)], "grad_w_up")
    tok = on_grads(["w_gate", "w_up"], [g_w_gate, g_w_up])
    dY, dz1, dgt_m = _bwd_proj(dx1, gt_m + tok, w_out, O1, name="out_proj_bwd")
    g_w_out = _tn_matmul([(Y, dz1)], "grad_w_out")
    tok = on_grads(["w_out"], [g_w_out])

    dQa, dKp, dKs, dKn, dVp, dVs, dVn, dKdc, dVdc, dsink = _att_bwd(Qa, Kd, Vd, Kdc, Vdc, sink + tok, dY, att_blk)
    dQr, dKr, dVr, dP, dO, dKc, dVc, acc1 = _ret_bwd1(Qr, Kr, P, Krc, Pc, SF, SB, dY, lg, rv_blk, rg_blk, npairs, d_proj)
    dKr, dP, dKc, dVc, acc2 = _ret_bwd2(Qr, Kr, P, Krc, Pc, SB, dO, dKr, dVr, dP, dKc, dVc, lg, rv_blk, npairs)

    dP = _unrope_cols(dQr, dP, o_rq, nq, T["Cr"], T["Sr"], T["RrT"], 1.0, True, "unrope_rq")
    dP = _unrope_cols(dKr, dP, o_rk, nq, T["Cr"], T["Sr"], T["RrT"], k_scale, True, "unrope_rk")
    dP = _unrope_cols(dQa, dP, o_aq, ATT_HEADS * ATT_DH // LANES, T["Ca"], T["Sa"], T["RaT"], a_scale, True, "unrope_aq")
    dP = _fold_heads([(dKs, 0), (dKp, 1), (dKn, -1)], dP, o_ak, nkvp, T["Ca"], T["Sa"], T["RaT"], T["D0T"], T["D1T"],
                     True, "fold_ak")
    dP = _fold_heads([(dVs, 0), (dVp, 1), (dVn, -1)], dP, o_av, nkvp, T["Ca"], T["Sa"], T["RaT"], T["D0T"], T["D1T"],
                     False, "fold_av")
    dPc = jnp.zeros((Lc, d_proj), BF16)
    dPc = _unrope_cols(dKc, dPc, o_rk, nq, Tc["C"], Tc["S"], T["RrT"], k_scale, False, "ctx_rk_bwd")
    dPc = _unrope_cols(dVc, dPc, o_rv, RET_HEADS * RET_DV // LANES, Tc["C"], Tc["S"], T["RrT"], 1.0, False, "ctx_rv_bwd")
    dPc = _fold_heads([(dKdc.astype(BF16), 0)], dPc, o_ak, nkvp, Tc["C"], Tc["S"], T["RaT"], T["D0T"], T["D1T"],
                      False, "fold_ak_ctx")
    dPc = _fold_heads([(dVdc.astype(BF16), 0)], dPc, o_av, nkvp, Tc["C"], Tc["S"], T["RaT"], T["D0T"], T["D1T"],
                      False, "fold_av_ctx")

    dx, st_m = _bwd_norm_mod([(dP, w_in)], x, dx1, norm_mix, sh_m, sc_m, "in_proj_bwd")
    _, st_mc = _bwd_norm_mod([(dPc, w_in)], ctx, None, norm_mix, sh_mc, sc_mc, "in_proj_ctx_bwd")
    g_w_in = _tn_matmul([(hx, dP), (hc, dPc)], "grad_w_in")
    on_grads(["w_in"], [g_w_in])

    a1 = acc1[:, :, :, 0].reshape(RET_HEADS, ACC_ROWS)
    a2 = acc2[:, :, :, 0].reshape(RET_HEADS, ACC_ROWS)
    dlam = (a1[:, 0] + a1[:, 2] + a1[:, 3] + a1[:, 4]) * lg[0]
    dmu = (a1[:, 1] + a1[:, 5] + a2[:, 0] + a2[:, 1]) * lg[1]
    d_sink = dsink[:, :4, 0].reshape(1, ATT_HEADS)

    nh = RET_HEADS
    assert 2 * nh + ATT_HEADS <= LOSS_LANE
    small = _pack_rows(
        [(st_m, 0, 2, 0, 0), (dgt_m, 0, 1, 2, 0), (st_f, 0, 2, 3, 0), (dgt_f, 0, 1, 5, 0), (st_mc, 0, 2, 6, 0),
         (st_m[2:3] + st_mc[2:3], 0, 1, 12, 0), (st_f, 2, 1, 13, 0), (d_norm_final, 0, 1, 14, 0),
         (dlam.reshape(1, nh), 0, 1, 15, 0), (dmu.reshape(1, nh), 0, 1, 15, nh), (d_sink, 0, 1, 15, 2 * nh),
         (loss[:, 0:1], 0, 1, 15, LOSS_LANE)], 16, D, "pack_small")
    return dict(grad_x=dx, small=small)


def _my_pos():
    return lax.axis_index("x"), lax.axis_index("y"), lax.axis_index("c")


def _other_chips(x, y):
    return [(1 - x, y), (x, 1 - y), (1 - x, 1 - y)]


def _remote(src, dst, ssem, rsem, dev):
    return pltpu.make_async_remote_copy(src_ref=src, dst_ref=dst, send_sem=ssem, recv_sem=rsem,
                                        device_id=dev, device_id_type=MESH)


def _allgather8(v, name):
    R, Cc = v.shape

    def body(v_ref, out_ref, send_sems, recv_sems):
        x, y, c = _my_pos()
        me = 4 * x + 2 * y + c
        out_ref[pl.ds(me, 1)] = v_ref[...][None]
        peers = []
        for j in range(1, N_DEV):
            peers.append((1 - x if (j >> 2) & 1 else x, 1 - y if (j >> 1) & 1 else y, 1 - c if j & 1 else c))
        copies = []
        for j, peer in enumerate(peers):
            cp = _remote(v_ref, out_ref.at[me], send_sems.at[j], recv_sems.at[j], peer)
            cp.start()
            copies.append(cp)
        for j, peer in enumerate(peers):
            pid = 4 * peer[0] + 2 * peer[1] + peer[2]
            _remote(v_ref, out_ref.at[pid], send_sems.at[j], recv_sems.at[j], peer).wait_recv()
        for cp in copies:
            cp.wait_send()

    return pl.pallas_call(
        body, name=name, out_shape=jax.ShapeDtypeStruct((N_DEV, R, Cc), v.dtype),
        in_specs=[pl.BlockSpec(memory_space=pltpu.VMEM)], out_specs=pl.BlockSpec(memory_space=pltpu.VMEM),
        scratch_shapes=[pltpu.SemaphoreType.DMA((N_DEV - 1,)), pltpu.SemaphoreType.DMA((N_DEV - 1,))])(v)


def _region(ref, k, half, shard_shape, axis):
    r, cs = shard_shape
    hr = r // 2
    if axis == 1:
        return ref.at[pl.ds(pl.multiple_of(half * hr, 16), hr), pl.ds(pl.multiple_of(k * cs, LANES), cs)]
    return ref.at[pl.ds(pl.multiple_of(k * r + half * hr, 16), hr), :]


def _full_shape(shard_shape, axis):
    r, cs = shard_shape
    return (r, N_CHIPS * cs) if axis == 1 else (N_CHIPS * r, cs)


def _allgather_weights(fulls, shapes, axes):
    nw = len(fulls)

    def body(*refs):
        out_refs = refs[nw:2 * nw]
        send1, recv1, send2, recv2 = refs[2 * nw:]
        x, y, c = _my_pos()
        k0 = 2 * x + y
        chips = _other_chips(x, y)
        sends = []
        for w in range(nw):
            own = _region(out_refs[w], k0, c, shapes[w], axes[w])
            for j, ch in enumerate(chips):
                cp = _remote(own, own, send1.at[w, j], recv1.at[w, j], (ch[0], ch[1], c))
                cp.start()
                sends.append(cp)
        for w in range(nw):
            for j, ch in enumerate(chips):
                kj = 2 * ch[0] + ch[1]
                got = _region(out_refs[w], kj, c, shapes[w], axes[w])
                _remote(got, got, send1.at[w, j], recv1.at[w, j], (ch[0], ch[1], c)).wait_recv()
                fw = _remote(got, got, send2.at[w, j], recv2.at[w, j], (x, y, 1 - c))
                fw.start()
                sends.append(fw)
        for w in range(nw):
            for j, ch in enumerate(chips):
                kj = 2 * ch[0] + ch[1]
                got = _region(out_refs[w], kj, 1 - c, shapes[w], axes[w])
                _remote(got, got, send2.at[w, j], recv2.at[w, j], (x, y, 1 - c)).wait_recv()
        for cp in sends:
            cp.wait_send()

    anyspec = pl.BlockSpec(memory_space=pl.ANY)
    sem = lambda: pltpu.SemaphoreType.DMA((nw, 3))
    return pl.pallas_call(
        body, name="allgather_weights",
        out_shape=[jax.ShapeDtypeStruct(f.shape, BF16) for f in fulls],
        in_specs=[anyspec] * nw, out_specs=[anyspec] * nw,
        input_output_aliases={w: w for w in range(nw)},
        scratch_shapes=[sem(), sem(), sem(), sem()])(*fulls)


def _half_pieces(ref, half, shard_shape, axis):
    r, cs = shard_shape
    hr = r // 2
    if axis == 1:
        return [ref.at[pl.ds(pl.multiple_of(half * hr, 16), hr), :]]
    return [ref.at[pl.ds(pl.multiple_of(k * r + half * hr, 16), hr), :] for k in range(N_CHIPS)]


def _rs_sibling(grads, shapes, axes, name):
    nw = len(grads)
    npc = max(1 if a == 1 else N_CHIPS for a in axes)

    def body(*refs):
        g_refs, out_refs = refs[:nw], refs[nw:2 * nw]
        send, recv = refs[2 * nw:]
        x, y, c = _my_pos()
        sib = (x, y, 1 - c)
        copies = []
        for w in range(nw):
            src = _half_pieces(g_refs[w], 1 - c, shapes[w], axes[w])
            dst = _half_pieces(out_refs[w], 1 - c, shapes[w], axes[w])
            for i, (s, d) in enumerate(zip(src, dst)):
                cp = _remote(s, d, send.at[w, i], recv.at[w, i], sib)
                cp.start()
                copies.append(cp)
        for w in range(nw):
            mine = _half_pieces(out_refs[w], c, shapes[w], axes[w])
            for i, d in enumerate(mine):
                _remote(d, d, send.at[w, i], recv.at[w, i], sib).wait_recv()
        for cp in copies:
            cp.wait_send()

    anyspec = pl.BlockSpec(memory_space=pl.ANY)
    return pl.pallas_call(
        body, name=name,
        out_shape=[jax.ShapeDtypeStruct(_full_shape(s, a), BF16) for s, a in zip(shapes, axes)],
        in_specs=[anyspec] * nw, out_specs=[anyspec] * nw,
        scratch_shapes=[pltpu.SemaphoreType.DMA((nw, npc)), pltpu.SemaphoreType.DMA((nw, npc))])(*grads)


def _half_block_spec(shard_shape, axis, tr):
    r, cs = shard_shape
    hr = r // 2
    if axis == 1:
        return pl.BlockSpec((tr, cs), lambda k, i, c_ref: (c_ref[0] * (hr // tr) + i, k))
    return pl.BlockSpec((tr, cs), lambda k, i, c_ref: (k * (r // tr) + c_ref[0] * (hr // tr) + i, 0))


def _add_halves(g, recv, cvec, shard_shape, axis, name):
    r, cs = shard_shape
    hr = r // 2
    tr = _tile(hr, 256, 16)

    def body(c_ref, a_ref, b_ref, o_ref):
        del c_ref
        o_ref[0] = (a_ref[...].astype(F32) + b_ref[...].astype(F32)).astype(BF16)

    spec = _half_block_spec(shard_shape, axis, tr)
    return pl.pallas_call(
        body, name=name,
        grid_spec=pltpu.PrefetchScalarGridSpec(
            num_scalar_prefetch=1, grid=(N_CHIPS, hr // tr), in_specs=[spec, spec],
            out_specs=pl.BlockSpec((1, tr, cs), lambda k, i, c_ref: (k, i, 0))),
        out_shape=jax.ShapeDtypeStruct((N_CHIPS, hr, cs), BF16),
        compiler_params=_params(2, False))(cvec, g, recv)


def _rs_chips(sums, name):
    nw = len(sums)

    def body(*refs):
        s_refs, out_refs = refs[:nw], refs[nw:2 * nw]
        send, recv = refs[2 * nw:]
        x, y, c = _my_pos()
        k0 = 2 * x + y
        chips = _other_chips(x, y)
        copies = []
        for w in range(nw):
            for j, ch in enumerate(chips):
                kj = 2 * ch[0] + ch[1]
                cp = _remote(s_refs[w].at[kj], out_refs[w].at[k0], send.at[w, j], recv.at[w, j], (ch[0], ch[1], c))
                cp.start()
                copies.append(cp)
        for w in range(nw):
            for j, ch in enumerate(chips):
                kj = 2 * ch[0] + ch[1]
                d = out_refs[w].at[kj]
                _remote(d, d, send.at[w, j], recv.at[w, j], (ch[0], ch[1], c)).wait_recv()
        for cp in copies:
            cp.wait_send()

    anyspec = pl.BlockSpec(memory_space=pl.ANY)
    return pl.pallas_call(
        body, name=name,
        out_shape=[jax.ShapeDtypeStruct(s.shape, BF16) for s in sums],
        in_specs=[anyspec] * nw, out_specs=[anyspec] * nw,
        scratch_shapes=[pltpu.SemaphoreType.DMA((nw, 3)), pltpu.SemaphoreType.DMA((nw, 3))])(*sums)


def _sum_chips(sums, landed, kc, name):
    _, hr, cs = sums.shape
    tr = _tile(hr, 256, 16)

    def body(kc_ref, own_ref, a_ref, b_ref, c_ref, o_ref):
        del kc_ref
        o_ref[...] = (own_ref[0].astype(F32) + a_ref[0].astype(F32)) + (b_ref[0].astype(F32) + c_ref[0].astype(F32))

    slot = lambda j: pl.BlockSpec((1, tr, cs), lambda i, kc_ref: ((kc_ref[0] + j) % N_CHIPS, i, 0))
    return pl.pallas_call(
        body, name=name,
        grid_spec=pltpu.PrefetchScalarGridSpec(
            num_scalar_prefetch=1, grid=(hr // tr,), in_specs=[slot(0), slot(1), slot(2), slot(3)],
            out_specs=pl.BlockSpec((tr, cs), lambda i, kc_ref: (kc_ref[1] * (hr // tr) + i, 0))),
        out_shape=jax.ShapeDtypeStruct((2 * hr, cs), F32),
        compiler_params=_params(1, False))(kc, sums, landed, landed, landed)


def _exchange_halves(shards):
    nw = len(shards)

    def body(*refs):
        out_refs = refs[nw:2 * nw]
        send, recv = refs[2 * nw:]
        x, y, c = _my_pos()
        sib = (x, y, 1 - c)
        copies = []
        for w in range(nw):
            hr = shards[w].shape[0] // 2
            mine = out_refs[w].at[pl.ds(pl.multiple_of(c * hr, 8), hr), :]
            cp = _remote(mine, mine, send.at[w], recv.at[w], sib)
            cp.start()
            copies.append(cp)
        for w in range(nw):
            hr = shards[w].shape[0] // 2
            other = out_refs[w].at[pl.ds(pl.multiple_of((1 - c) * hr, 8), hr), :]
            _remote(other, other, send.at[w], recv.at[w], sib).wait_recv()
        for cp in copies:
            cp.wait_send()

    anyspec = pl.BlockSpec(memory_space=pl.ANY)
    return pl.pallas_call(
        body, name="exchange_halves",
        out_shape=[jax.ShapeDtypeStruct(s.shape, F32) for s in shards],
        in_specs=[anyspec] * nw, out_specs=[anyspec] * nw,
        input_output_aliases={w: w for w in range(nw)},
        scratch_shapes=[pltpu.SemaphoreType.DMA((nw,)), pltpu.SemaphoreType.DMA((nw,))])(*shards)


def _cast_into_full(w, kc, axis, name):
    r, cs = w.shape
    tr = _tile(r, 256, 16)

    def body(kc_ref, w_ref, o_ref):
        del kc_ref
        o_ref[...] = w_ref[...].astype(BF16)

    if axis == 1:
        ospec = pl.BlockSpec((tr, cs), lambda i, kc_ref: (i, kc_ref[0]))
    else:
        ospec = pl.BlockSpec((tr, cs), lambda i, kc_ref: (kc_ref[0] * (r // tr) + i, 0))
    return pl.pallas_call(
        body, name=name,
        grid_spec=pltpu.PrefetchScalarGridSpec(
            num_scalar_prefetch=1, grid=(r // tr,), in_specs=[pl.BlockSpec((tr, cs), lambda i, kc_ref: (i, 0))],
            out_specs=ospec),
        out_shape=jax.ShapeDtypeStruct(_full_shape((r, cs), axis), BF16),
        compiler_params=_params(1, False))(kc, w)


def _adam_math(w, g, m, v):
    m2 = ADAM_B1 * m + (1.0 - ADAM_B1) * g
    v2 = ADAM_B2 * v + (1.0 - ADAM_B2) * (g * g)
    m_hat = m2 / (1.0 - ADAM_B1 ** ADAM_STEP)
    v_hat = v2 / (1.0 - ADAM_B2 ** ADAM_STEP)
    delta = -ADAM_LR * (m_hat / (jnp.sqrt(v_hat) + ADAM_EPS) + ADAM_WD * w)
    return delta, m2, v2


def _adam(w, g, m, v, name):
    r, cs = w.shape
    tr = _tile(r, 256, 8)

    def body(w_ref, g_ref, m_ref, v_ref, d_ref, m2_ref, v2_ref):
        d, m2, v2 = _adam_math(w_ref[...], g_ref[...], m_ref[...], v_ref[...])
        d_ref[...] = d
        m2_ref[...] = m2
        v2_ref[...] = v2

    spec = pl.BlockSpec((tr, cs), lambda i: (i, 0))
    shp = jax.ShapeDtypeStruct((r, cs), F32)
    return pl.pallas_call(body, name=name, grid=(r // tr,), in_specs=[spec] * 4, out_specs=[spec] * 3,
                          out_shape=[shp, shp, shp], compiler_params=_params(1, False))(w, g, m, v)


def _mod_rows(a16, w, b, name):
    D, n = w.shape
    tn = _tile(n, 512)

    def body(a_ref, w_ref, b_ref, o_ref):
        a = a_ref[...]
        o_ref[...] = _nn((a * _sigmoid(a)).astype(BF16), w_ref[...].astype(BF16)) + b_ref[...]

    return pl.pallas_call(
        body, name=name, grid=(n // tn,),
        in_specs=[pl.BlockSpec((16, D), lambda j: (0, 0)), pl.BlockSpec((D, tn), lambda j: (0, j)),
                  pl.BlockSpec((1, tn), lambda j: (0, j))],
        out_specs=pl.BlockSpec((16, tn), lambda j: (0, j)),
        out_shape=jax.ShapeDtypeStruct((16, n), F32), compiler_params=_params(1, False))(a16, w, b)


def _w_mod_update(a16, d16, w, m, v):
    D, n = w.shape
    tn = _tile(n, 256)

    def body(a_ref, d_ref, w_ref, m_ref, v_ref, g_ref, dl_ref, m2_ref, v2_ref, p_ref):
        @pl.when(pl.program_id(0) == 0)
        def _():
            p_ref[...] = jnp.zeros_like(p_ref)
        a = a_ref[...]
        db = d_ref[...].astype(BF16)
        wv = w_ref[...]
        g = _tn((a * _sigmoid(a)).astype(BF16), db)
        g_ref[...] = g
        d, m2, v2 = _adam_math(wv, g, m_ref[...], v_ref[...])
        dl_ref[...] = d
        m2_ref[...] = m2
        v2_ref[...] = v2
        p_ref[...] += _nt(db, wv.astype(BF16))

    wspec = pl.BlockSpec((D, tn), lambda j: (0, j))
    shp = jax.ShapeDtypeStruct((D, n), F32)
    return pl.pallas_call(
        body, name="w_mod_update", grid=(n // tn,),
        in_specs=[pl.BlockSpec((16, D), lambda j: (0, 0)), pl.BlockSpec((16, tn), lambda j: (0, j)), wspec, wspec, wspec],
        out_specs=[wspec, wspec, wspec, wspec, pl.BlockSpec((16, D), lambda j: (0, 0))],
        out_shape=[shp, shp, shp, shp, jax.ShapeDtypeStruct((16, D), F32)],
        compiler_params=_params(1))(a16, d16, w, m, v)


def _sum_devices(g8, name):
    _, R, Cc = g8.shape

    def body(g_ref, o_ref):
        t = g_ref[0]
        for d in range(1, N_DEV):
            t = t + g_ref[d]
        o_ref[...] = t

    return pl.pallas_call(body, name=name, out_shape=jax.ShapeDtypeStruct((R, Cc), F32))(g8)


def _c_ctx_grad(parts, c_ctx):
    D = c_ctx.shape[1]

    def body(p_ref, c_ref, o_ref):
        t = p_ref[0]
        for k in range(1, N_CHIPS):
            t = t + p_ref[2 * k]
        cv = c_ref[...]
        sg = _sigmoid(cv)
        o_ref[...] = t * (sg * (1.0 + cv * (1.0 - sg)))

    return pl.pallas_call(body, name="c_ctx_grad", out_shape=jax.ShapeDtypeStruct((1, D), F32))(parts, c_ctx)


def _pack_rows(items, nrows, width, name):
    arrays, plan = [], []
    for a, r0, nr, d0, c0 in items:
        for ai, b in enumerate(arrays):
            if b is a:
                break
        else:
            ai = len(arrays)
            arrays.append(a)
        plan.append((ai, r0, nr, d0, c0, a.shape[1]))

    def body(*refs):
        o_ref = refs[-1]
        o_ref[...] = jnp.zeros_like(o_ref)
        for ai, r0, nr, d0, c0, w in plan:
            o_ref[d0:d0 + nr, c0:c0 + w] = refs[ai][r0:r0 + nr, :]

    return pl.pallas_call(body, name=name, out_shape=jax.ShapeDtypeStruct((nrows, width), F32))(*arrays)


HBM_SPEC = pl.BlockSpec(memory_space=pltpu.HBM)
SEM_SPEC = pl.BlockSpec(memory_space=pltpu.SEMAPHORE)
SPLIT_PARAMS = pltpu.CompilerParams(has_side_effects=pltpu.SideEffectType.DATAFLOW_SIDE_EFFECTING)


def _in_hbm(a):
    return pltpu.with_memory_space_constraint(a, pltpu.HBM)


def _ag_chips_start(fulls, shapes, axes, after):
    nw = len(fulls)

    def body(*refs):
        in_refs, send, recv, token = refs[:nw], refs[nw + 1], refs[nw + 2], refs[-1]
        x, y, c = _my_pos()
        k0 = 2 * x + y
        for w in range(nw):
            own = _region(in_refs[w], k0, c, shapes[w], axes[w])
            for j, ch in enumerate(_other_chips(x, y)):
                _remote(own, own, send.at[3 * w + j], recv.at[3 * w + j], (ch[0], ch[1], c)).start()
        token[...] = jnp.zeros_like(token)

    return pl.pallas_call(
        body, name="ag_chips_start",
        out_shape=(pltpu.SemaphoreType.DMA((3 * nw,)), pltpu.SemaphoreType.DMA((3 * nw,)),
                   *[pltpu.HBM(f.shape, f.dtype) for f in fulls], jax.ShapeDtypeStruct((8, LANES), F32)),
        in_specs=[HBM_SPEC] * nw + [pl.BlockSpec(memory_space=pl.ANY)],
        out_specs=(SEM_SPEC, SEM_SPEC, *[HBM_SPEC] * nw, pl.BlockSpec(memory_space=pltpu.VMEM)),
        input_output_aliases={w: 2 + w for w in range(nw)},
        compiler_params=SPLIT_PARAMS)(*[_in_hbm(f) for f in fulls], after)


def _ag_chips_wait(send, recv, fulls, shapes, axes, after):
    nw = len(fulls)

    def body(*refs):
        in_refs, send_ref, recv_ref = refs[:nw], refs[nw], refs[nw + 1]
        x, y, c = _my_pos()
        k0 = 2 * x + y
        for w in range(nw):
            own = _region(in_refs[w], k0, c, shapes[w], axes[w])
            for j, ch in enumerate(_other_chips(x, y)):
                got = _region(in_refs[w], 2 * ch[0] + ch[1], c, shapes[w], axes[w])
                cp = _remote(own, got, send_ref.at[3 * w + j], recv_ref.at[3 * w + j], (ch[0], ch[1], c))
                cp.wait_send()
                cp.wait_recv()

    return pl.pallas_call(
        body, name="ag_chips_wait",
        out_shape=tuple(pltpu.HBM(f.shape, f.dtype) for f in fulls),
        in_specs=[HBM_SPEC] * nw + [SEM_SPEC, SEM_SPEC, pl.BlockSpec(memory_space=pl.ANY)],
        out_specs=tuple([HBM_SPEC] * nw),
        input_output_aliases={w: w for w in range(nw)},
        compiler_params=SPLIT_PARAMS)(*fulls, send, recv, after)


def _ag_forward(fulls, shapes, axes):
    nw = len(fulls)

    def body(*refs):
        out_refs = refs[nw:2 * nw]
        send, recv = refs[2 * nw:]
        x, y, c = _my_pos()
        sib = (x, y, 1 - c)
        chips = _other_chips(x, y)
        copies = []
        for w in range(nw):
            for j, ch in enumerate(chips):
                got = _region(out_refs[w], 2 * ch[0] + ch[1], c, shapes[w], axes[w])
                cp = _remote(got, got, send.at[w, j], recv.at[w, j], sib)
                cp.start()
                copies.append(cp)
        for w in range(nw):
            for j, ch in enumerate(chips):
                got = _region(out_refs[w], 2 * ch[0] + ch[1], 1 - c, shapes[w], axes[w])
                _remote(got, got, send.at[w, j], recv.at[w, j], sib).wait_recv()
        for cp in copies:
            cp.wait_send()

    anyspec = pl.BlockSpec(memory_space=pl.ANY)
    return pl.pallas_call(
        body, name="ag_forward",
        out_shape=[jax.ShapeDtypeStruct(f.shape, BF16) for f in fulls],
        in_specs=[anyspec] * nw, out_specs=[anyspec] * nw,
        input_output_aliases={w: w for w in range(nw)},
        scratch_shapes=[pltpu.SemaphoreType.DMA((nw, 3)), pltpu.SemaphoreType.DMA((nw, 3))])(*fulls)


def _rs_chips_start(sums, name):
    nw = len(sums)

    def body(*refs):
        s_refs, l_refs, send, recv, token = refs[:nw], refs[nw:2 * nw], refs[2 * nw], refs[2 * nw + 1], refs[-1]
        x, y, c = _my_pos()
        k0 = 2 * x + y
        for w in range(nw):
            for j, ch in enumerate(_other_chips(x, y)):
                _remote(s_refs[w].at[2 * ch[0] + ch[1]], l_refs[w].at[k0], send.at[3 * w + j], recv.at[3 * w + j],
                        (ch[0], ch[1], c)).start()
        token[...] = jnp.zeros_like(token)

    thru = [pltpu.HBM(s.shape, s.dtype) for s in sums]
    return pl.pallas_call(
        body, name=name,
        out_shape=(pltpu.SemaphoreType.DMA((3 * nw,)), pltpu.SemaphoreType.DMA((3 * nw,)), *thru, *thru,
                   jax.ShapeDtypeStruct((8, LANES), F32)),
        in_specs=[HBM_SPEC] * (2 * nw),
        out_specs=(SEM_SPEC, SEM_SPEC, *[HBM_SPEC] * (2 * nw), pl.BlockSpec(memory_space=pltpu.VMEM)),
        input_output_aliases={i: 2 + i for i in range(2 * nw)},
        compiler_params=SPLIT_PARAMS)(*[_in_hbm(s) for s in sums], *[_in_hbm(lax.empty(s.shape, s.dtype)) for s in sums])


def _rs_chips_wait(send, recv, sums, lands, after, name):
    nw = len(sums)

    def body(*refs):
        s_refs, l_refs, send_ref, recv_ref = refs[:nw], refs[nw:2 * nw], refs[2 * nw], refs[2 * nw + 1]
        x, y, c = _my_pos()
        for w in range(nw):
            for j, ch in enumerate(_other_chips(x, y)):
                kj = 2 * ch[0] + ch[1]
                cp = _remote(s_refs[w].at[kj], l_refs[w].at[kj], send_ref.at[3 * w + j], recv_ref.at[3 * w + j],
                             (ch[0], ch[1], c))
                cp.wait_send()
                cp.wait_recv()

    thru = tuple(pltpu.HBM(s.shape, s.dtype) for s in sums)
    return pl.pallas_call(
        body, name=name, out_shape=thru + thru,
        in_specs=[HBM_SPEC] * (2 * nw) + [SEM_SPEC, SEM_SPEC, pl.BlockSpec(memory_space=pl.ANY)],
        out_specs=tuple([HBM_SPEC] * (2 * nw)),
        input_output_aliases={i: i for i in range(2 * nw)},
        compiler_params=SPLIT_PARAMS)(*sums, *lands, send, recv, after)


LOSS_LANE = 64


def kernel(x, c, ctx, c_ctx, w_mod, b_mod, norm_mix, norm_ffn, w_in, ret_decay, attn_sink, w_out, w_gate, w_up, w_down, norm_final, loss_target, m_c_ctx, m_w_mod, m_b_mod, m_norm_mix, m_norm_ffn, m_w_in, m_ret_decay, m_attn_sink, m_w_out, m_w_gate, m_w_up, m_w_down, m_norm_final, v_c_ctx, v_w_mod, v_b_mod, v_norm_mix, v_norm_ffn, v_w_in, v_ret_decay, v_attn_sink, v_w_out, v_w_gate, v_w_up, v_w_down, v_norm_final):
    D = x.shape[-1]
    n3 = w_mod.shape[-1]
    xi, yi, ci = _my_pos()
    b = 4 * xi + 2 * yi + ci
    k0 = 2 * xi + yi
    cvec = jnp.reshape(ci, (1,)).astype(jnp.int32)
    kc = jnp.stack([k0, ci]).astype(jnp.int32)

    dense = [("w_in", w_in[0], 1), ("w_out", w_out[0], 0), ("w_gate", w_gate[0], 1), ("w_up", w_up[0], 1),
             ("w_down", w_down[0], 0)]
    axes = [a for _, _, a in dense]
    shapes = [w.shape for _, w, _ in dense]
    own16 = [_cast_into_full(w, kc, a, "cast_" + n) for n, w, a in dense]
    (f_in,) = _allgather_weights(own16[:1], shapes[:1], axes[:1])
    ag = _ag_chips_start(own16[1:], shapes[1:], axes[1:], f_in)
    ag_send, ag_recv, ag_thru, ag_tok = ag[0], ag[1], list(ag[2:-1]), ag[-1][0:1, 0:1]

    def rest_weights(after):
        landed_w = _ag_chips_wait(ag_send, ag_recv, ag_thru, shapes[1:], axes[1:], after)
        return _ag_forward(list(landed_w), shapes[1:], axes[1:])

    c_all = _allgather8(c, "gather_c").reshape(N_DEV, D)
    c_ctx2 = c_ctx.reshape(1, D)
    a16 = _pack_rows([(c_all, 0, N_DEV, 0, 0), (c_ctx2, 0, 1, N_DEV, 0)], 16, D, "pack_cond")
    b_cols = lax.dynamic_slice_in_dim(b_mod, k0 * n3, n3, axis=1)
    mod16 = _mod_rows(a16, w_mod[0], b_cols, "mod_rows")
    mod_all = _allgather8(mod16, "gather_mod")
    mine = jnp.stack([lax.dynamic_index_in_dim(mod_all, 2 * k + ci, 0, keepdims=False) for k in range(N_CHIPS)])
    mod = lax.dynamic_index_in_dim(mine, b, 1, keepdims=False).reshape(6, D)
    modc = mine[:, N_DEV].reshape(6, D)

    lg = -jnp.exp(ret_decay[0])

    index = {n: i for i, (n, _, _) in enumerate(dense)}
    pending, done = [], {}

    def on_grads(names, gs):
        ids = [index[n] for n in names]
        shp, axs = [shapes[i] for i in ids], [axes[i] for i in ids]
        from_sib = _rs_sibling(gs, shp, axs, "rs_sibling_" + names[0])
        sums = [_add_halves(g, r, cvec, s, a, "add_halves_" + n) for g, r, s, a, n in zip(gs, from_sib, shp, axs, names)]
        if names == ["w_in"]:
            for n, s, l in zip(names, sums, _rs_chips(sums, "rs_chips_" + names[0])):
                done[n] = (s, l)
            return None
        st = _rs_chips_start(sums, "rs_chips_start_" + names[0])
        nw = len(names)
        pending.append((names, st[0], st[1], list(st[2:2 + nw]), list(st[2 + nw:2 + 2 * nw])))
        return st[-1][0:1, 0:1]

    out = _local_step(x[0], ctx[0], loss_target[0], mod, modc, norm_mix + ag_tok, norm_ffn, norm_final.reshape(1, D), lg,
                      attn_sink, f_in, rest_weights, on_grads)

    for names, send, recv, sums, lands in pending:
        res = _rs_chips_wait(send, recv, sums, lands, out["grad_x"], "rs_chips_wait_" + names[0])
        for i, n in enumerate(names):
            done[n] = (res[i], res[len(names) + i])
    halves = [_sum_chips(done[n][0], done[n][1], kc, "sum_chips_" + n) for n, _, _ in dense]
    g_dense = _exchange_halves(halves)

    nh = 2 * RET_HEADS
    small_all = _allgather8(out["small"], "gather_small")
    tot = _sum_devices(small_all, "sum_small")
    g_b_mod = (tot[0:6] + tot[6:12]).reshape(1, 6 * D)
    dmodc_tot = tot[6:12].reshape(1, 6 * D)
    dmod_rows = small_all[:, 0:6].reshape(N_DEV, 6 * D)
    d16 = _pack_rows([(dmod_rows, 0, N_DEV, 0, 0), (dmodc_tot, 0, 1, N_DEV, 0)], 16, 6 * D, "pack_dmod")
    d16 = lax.dynamic_slice_in_dim(d16, k0 * n3, n3, axis=1)
    g_w_mod, dl_w_mod, m2_w_mod, v2_w_mod, part = _w_mod_update(a16, d16, w_mod[0], m_w_mod[0], v_w_mod[0])
    part_all = _allgather8(part[N_DEV:N_DEV + 1], "gather_c_ctx")
    g_c_ctx = _c_ctx_grad(part_all, c_ctx2)
    loss = tot[15, LOSS_LANE]
    g_ret_decay = tot[15, :nh].reshape(1, 2, RET_HEADS)
    g_sink = tot[15, nh:nh + ATT_HEADS].reshape(1, ATT_HEADS)

    def pack(cc, bm, nm, nf, nfin, rd, sk, name):
        rd2 = rd.reshape(2, RET_HEADS)
        return _pack_rows([(bm.reshape(6, D), 0, 6, 0, 0), (cc.reshape(1, D), 0, 1, 6, 0), (nm.reshape(1, D), 0, 1, 7, 0),
                           (nf.reshape(1, D), 0, 1, 8, 0), (nfin.reshape(1, D), 0, 1, 9, 0),
                           (rd2, 0, 1, 10, 0), (rd2, 1, 1, 10, RET_HEADS), (sk.reshape(1, ATT_HEADS), 0, 1, 10, nh)],
                          16, D, name)

    w_s = pack(c_ctx, b_mod, norm_mix, norm_ffn, norm_final, ret_decay, attn_sink, "pack_w")
    g_s = _pack_rows([(g_b_mod.reshape(6, D), 0, 6, 0, 0), (g_c_ctx, 0, 1, 6, 0), (tot, 12, 3, 7, 0),
                      (tot[15:16, 0:nh + ATT_HEADS], 0, 1, 10, 0)], 16, D, "pack_g")
    m_s = pack(m_c_ctx, m_b_mod, m_norm_mix, m_norm_ffn, m_norm_final, m_ret_decay, m_attn_sink, "pack_m")
    v_s = pack(v_c_ctx, v_b_mod, v_norm_mix, v_norm_ffn, v_norm_final, v_ret_decay, v_attn_sink, "pack_v")
    small_upd = _adam(w_s, g_s, m_s, v_s, "adam_small")

    def unpack(t):
        return dict(b_mod=t[0:6].reshape(1, 6 * D), c_ctx=t[6], norm_mix=t[7:8], norm_ffn=t[8:9], norm_final=t[9],
                    ret_decay=t[10, :nh].reshape(1, 2, RET_HEADS), attn_sink=t[10, nh:nh + ATT_HEADS].reshape(1, ATT_HEADS))

    dense_w = dict(w_in=(w_in, m_w_in, v_w_in), w_out=(w_out, m_w_out, v_w_out), w_gate=(w_gate, m_w_gate, v_w_gate),
                   w_up=(w_up, m_w_up, v_w_up), w_down=(w_down, m_w_down, v_w_down))
    grads = dict(unpack(g_s), w_mod=g_w_mod[None])
    upd = [dict(unpack(t)) for t in small_upd]
    upd[0]["w_mod"], upd[1]["w_mod"], upd[2]["w_mod"] = dl_w_mod[None], m2_w_mod[None], v2_w_mod[None]
    for (n, _, _), g in zip(dense, g_dense):
        w_, m_, v_ = dense_w[n]
        res = _adam(w_[0], g, m_[0], v_[0], "adam_" + n)
        grads[n] = g[None]
        for u, r_ in zip(upd, res):
            u[n] = r_[None]

    order = ['c_ctx', 'w_mod', 'b_mod', 'norm_mix', 'norm_ffn', 'w_in', 'ret_decay', 'attn_sink', 'w_out', 'w_gate',
             'w_up', 'w_down', 'norm_final']
    outs = [loss, out["grad_x"][None]] + [grads[n] for n in order]
    for u in upd:
        outs += [u[n] for n in order]
    return tuple(outs)
```

```python
import functools
import numpy as np
import jax
import jax.numpy as jnp
from jax import lax
from jax.experimental import pallas as pl
from jax.experimental.pallas import tpu as pltpu

F32 = jnp.float32
BF16 = jnp.bfloat16

RET_HEADS = 8
RET_DK = 64
RET_DV = 128
CHUNK = 128
ATT_HEADS = 16
ATT_KV_HEADS = 4
ATT_DH = 64
GRID_W = 64
ROPE_BASE = 10000.0
NORM_EPS = 1e-6
ADAM_LR = 0.001
ADAM_B1 = 0.9
ADAM_B2 = 0.999
ADAM_EPS = 1e-08
ADAM_WD = 0.01
ADAM_STEP = 10
NEG = -1e30
LANES = 128
VMEM_LIMIT = 56 * 1024 * 1024
ROWS_PER_LATCH = 1024
MESH = pl.DeviceIdType.MESH
N_CHIPS = 4
N_DEV = 8


def _nn(a, b):
    return jnp.dot(a, b, preferred_element_type=F32)


def _nt(a, b):
    return lax.dot_general(a, b, (((1,), (1,)), ((), ())), preferred_element_type=F32)


def _tn(a, b):
    return lax.dot_general(a, b, (((0,), (0,)), ((), ())), preferred_element_type=F32)


def _tile(n, pref, unit=LANES):
    t = min(n, pref)
    t -= t % unit
    while t > unit and n % t:
        t -= unit
    if t <= 0 or n % t:
        return n
    return t


def _params(ndim, vmem=True):
    return pltpu.CompilerParams(dimension_semantics=("arbitrary",) * ndim,
                                vmem_limit_bytes=VMEM_LIMIT if vmem else None)


def _sigmoid(x):
    return 0.5 * jnp.tanh(0.5 * x) + 0.5


def _fsum(x):
    return jnp.sum(jnp.sum(x, axis=1, keepdims=True), axis=0, keepdims=True)


def _rope_tables(L):
    lane = np.arange(LANES)
    d = lane % 64
    inv_r = jnp.asarray(ROPE_BASE, F32) ** (-jnp.arange(32, dtype=F32) / 32)
    t = jnp.arange(L)
    ang_r = t.astype(F32)[:, None] * jnp.tile(inv_r, LANES // 32)[None, :]
    Rr = np.zeros((LANES, LANES), np.float32)
    for l in range(LANES):
        if d[l] < 32:
            Rr[l + 32, l] = -1.0
        else:
            Rr[l - 32, l] = 1.0
    inv_a = jnp.asarray(ROPE_BASE, F32) ** (-jnp.arange(16, dtype=F32) / 16)
    rows = (t // GRID_W).astype(F32)
    cols = (t % GRID_W).astype(F32)
    dd = d % 32
    pos = jnp.where(jnp.asarray(d < 32)[None, :], rows[:, None], cols[:, None])
    ang_a = pos * jnp.tile(inv_a, LANES // 16)[None, :]
    Ra = np.zeros((LANES, LANES), np.float32)
    for l in range(LANES):
        if dd[l] < 16:
            Ra[l + 16, l] = -1.0
        else:
            Ra[l - 16, l] = 1.0
    D0 = np.zeros((LANES, LANES), np.float32)
    D1 = np.zeros((LANES, LANES), np.float32)
    for l in range(LANES):
        D0[l % 64, l] = 1.0
        D1[64 + l % 64, l] = 1.0
    return dict(
        Cr=jnp.cos(ang_r), Sr=jnp.sin(ang_r), Rr=jnp.asarray(Rr, BF16), RrT=jnp.asarray(Rr.T, BF16),
        Ca=jnp.cos(ang_a), Sa=jnp.sin(ang_a), Ra=jnp.asarray(Ra, BF16), RaT=jnp.asarray(Ra.T, BF16),
        D0=jnp.asarray(D0, BF16), D1=jnp.asarray(D1, BF16),
        D0T=jnp.asarray(D0.T, BF16), D1T=jnp.asarray(D1.T, BF16))


def _norm_mod(xf, g, sh, sc):
    r = lax.rsqrt(jnp.mean(xf * xf, axis=-1, keepdims=True) + NORM_EPS)
    return (xf * r * g) * (1.0 + sc) + sh


def _norm_mod_matmul(x, g, sh, sc, w, name):
    M, D = x.shape
    N = w.shape[1]
    tm, tn = _tile(M, ROWS_PER_LATCH, 8), _tile(N, 512)

    def body(x_ref, g_ref, sh_ref, sc_ref, w_ref, p_ref, h_ref, hs):
        @pl.when(pl.program_id(1) == 0)
        def _():
            hb = _norm_mod(x_ref[...], g_ref[...], sh_ref[...], sc_ref[...]).astype(BF16)
            hs[...] = hb
            h_ref[...] = hb
        p_ref[...] = _nn(hs[...], w_ref[...]).astype(BF16)

    vec = pl.BlockSpec((1, D), lambda i, j: (0, 0))
    return pl.pallas_call(
        body, name=name, grid=(M // tm, N // tn),
        in_specs=[pl.BlockSpec((tm, D), lambda i, j: (i, 0)), vec, vec, vec,
                  pl.BlockSpec((D, tn), lambda i, j: (0, j))],
        out_specs=[pl.BlockSpec((tm, tn), lambda i, j: (i, j)), pl.BlockSpec((tm, D), lambda i, j: (i, 0))],
        out_shape=[jax.ShapeDtypeStruct((M, N), BF16), jax.ShapeDtypeStruct((M, D), BF16)],
        scratch_shapes=[pltpu.VMEM((tm, D), BF16)],
        compiler_params=_params(2))(x, g, sh, sc, w)


def _proj_residual(a, w, xres, gt, name):
    M, K = a.shape
    N = w.shape[1]
    tm, tn = _tile(M, ROWS_PER_LATCH, 8), _tile(N, 512)

    def body(a_ref, w_ref, x_ref, gt_ref, xo_ref, o_ref):
        o = _nn(a_ref[...], w_ref[...])
        o_ref[...] = o.astype(BF16)
        xo_ref[...] = x_ref[...] + gt_ref[...] * o

    return pl.pallas_call(
        body, name=name, grid=(M // tm, N // tn),
        in_specs=[pl.BlockSpec((tm, K), lambda i, j: (i, 0)), pl.BlockSpec((K, tn), lambda i, j: (0, j)),
                  pl.BlockSpec((tm, tn), lambda i, j: (i, j)), pl.BlockSpec((1, tn), lambda i, j: (0, j))],
        out_specs=[pl.BlockSpec((tm, tn), lambda i, j: (i, j)), pl.BlockSpec((tm, tn), lambda i, j: (i, j))],
        out_shape=[jax.ShapeDtypeStruct((M, N), F32), jax.ShapeDtypeStruct((M, N), BF16)],
        compiler_params=_params(2))(a, w, xres, gt)


def _ffn_in(x1, g, sh, sc, wg, wu):
    M, D = x1.shape
    N = wg.shape[1]
    tm, tn = _tile(M, ROWS_PER_LATCH, 8), _tile(N, 512)

    def body(x_ref, g_ref, sh_ref, sc_ref, wg_ref, wu_ref, G_ref, U_ref, A_ref, h_ref, hs):
        @pl.when(pl.program_id(1) == 0)
        def _():
            hb = _norm_mod(x_ref[...], g_ref[...], sh_ref[...], sc_ref[...]).astype(BF16)
            hs[...] = hb
            h_ref[...] = hb
        G = _nn(hs[...], wg_ref[...])
        U = _nn(hs[...], wu_ref[...])
        G_ref[...] = G.astype(BF16)
        U_ref[...] = U.astype(BF16)
        A_ref[...] = (G * _sigmoid(G) * U).astype(BF16)

    vec = pl.BlockSpec((1, D), lambda i, j: (0, 0))
    wspec = pl.BlockSpec((D, tn), lambda i, j: (0, j))
    ospec = pl.BlockSpec((tm, tn), lambda i, j: (i, j))
    big = jax.ShapeDtypeStruct((M, N), BF16)
    return pl.pallas_call(
        body, name="ffn_in", grid=(M // tm, N // tn),
        in_specs=[pl.BlockSpec((tm, D), lambda i, j: (i, 0)), vec, vec, vec, wspec, wspec],
        out_specs=[ospec, ospec, ospec, pl.BlockSpec((tm, D), lambda i, j: (i, 0))],
        out_shape=[big, big, big, jax.ShapeDtypeStruct((M, D), BF16)],
        scratch_shapes=[pltpu.VMEM((tm, D), BF16)],
        compiler_params=_params(2))(x1, g, sh, sc, wg, wu)


def _final(x2, gn, tgt):
    M, D = x2.shape
    tm = _tile(M, 256, 8)

    def body(x_ref, g_ref, t_ref, dx_ref, loss_ref, dg_ref):
        @pl.when(pl.program_id(0) == 0)
        def _():
            loss_ref[...] = jnp.zeros_like(loss_ref)
            dg_ref[...] = jnp.zeros_like(dg_ref)
        x = x_ref[...]
        g = g_ref[...]
        r = lax.rsqrt(jnp.mean(x * x, axis=-1, keepdims=True) + NORM_EPS)
        xh = x * r
        e = xh * g - t_ref[...]
        loss_ref[...] += (0.5 / D) * _fsum(e * e)
        dy = e * (1.0 / D)
        dg_ref[...] += jnp.sum(dy * xh, axis=0, keepdims=True)
        dxh = dy * g
        dx_ref[...] = r * (dxh - xh * jnp.mean(dxh * xh, axis=-1, keepdims=True))

    row = pl.BlockSpec((tm, D), lambda i: (i, 0))
    return pl.pallas_call(
        body, name="final_loss", grid=(M // tm,),
        in_specs=[row, pl.BlockSpec((1, D), lambda i: (0, 0)), row],
        out_specs=[row, pl.BlockSpec((1, LANES), lambda i: (0, 0)), pl.BlockSpec((1, D), lambda i: (0, 0))],
        out_shape=[jax.ShapeDtypeStruct((M, D), F32), jax.ShapeDtypeStruct((1, LANES), F32),
                   jax.ShapeDtypeStruct((1, D), F32)],
        compiler_params=_params(1))(x2, gn, tgt)


def _col_group(blk0, nblk):
    return int(np.gcd(blk0, nblk)) if blk0 else nblk


def _rope_cols(src, blk0, nblk, Ct, St, R, scale, rope, name):
    M = src.shape[0]
    tm = _tile(M, 512, 8)
    wb = _col_group(blk0, nblk)

    def body(x_ref, c_ref, s_ref, r_ref, o_ref):
        for j in range(wb):
            cols = slice(j * LANES, (j + 1) * LANES)
            x = x_ref[:, cols]
            xf = x.astype(F32)
            if rope:
                xf = xf * c_ref[...] + _nn(x.astype(BF16), r_ref[...]) * s_ref[...]
            o_ref[:, cols] = (xf * scale).astype(BF16)

    tab = pl.BlockSpec((tm, LANES), lambda i, j: (i, 0))
    return pl.pallas_call(
        body, name=name, grid=(M // tm, nblk // wb),
        in_specs=[pl.BlockSpec((tm, wb * LANES), lambda i, j: (i, blk0 // wb + j)), tab, tab,
                  pl.BlockSpec((LANES, LANES), lambda i, j: (0, 0))],
        out_specs=pl.BlockSpec((tm, wb * LANES), lambda i, j: (i, j)),
        out_shape=jax.ShapeDtypeStruct((M, nblk * LANES), BF16),
        compiler_params=_params(2, False))(src, Ct, St, R)


def _dup_heads(src, blk0, npair, Ct, St, R, D0, D1, rope, name):
    M = src.shape[0]
    tm = _tile(M, 512, 8)

    def body(x_ref, c_ref, s_ref, r_ref, d0_ref, d1_ref, o_ref):
        x = x_ref[...]
        if rope:
            x = (x.astype(F32) * c_ref[...] + _nn(x, r_ref[...]) * s_ref[...]).astype(BF16)
        o_ref[0] = _nn(x, d0_ref[...]).astype(BF16)
        o_ref[1] = _nn(x, d1_ref[...]).astype(BF16)

    tab = pl.BlockSpec((tm, LANES), lambda i, p: (i, 0))
    mat = pl.BlockSpec((LANES, LANES), lambda i, p: (0, 0))
    return pl.pallas_call(
        body, name=name, grid=(M // tm, npair),
        in_specs=[pl.BlockSpec((tm, LANES), lambda i, p: (i, blk0 + p)), tab, tab, mat, mat, mat],
        out_specs=pl.BlockSpec((2, tm, LANES), lambda i, p: (p, i, 0)),
        out_shape=jax.ShapeDtypeStruct((2 * npair, M, LANES), BF16),
        compiler_params=_params(2, False))(src, Ct, St, R, D0, D1)


def _unrope_cols(dsrc, dst, blk0, nblk, Ct, St, RT, scale, rope, name):
    M = dsrc.shape[0]
    tm = _tile(M, 512, 8)
    wb = _col_group(blk0, nblk)

    def body(x_ref, c_ref, s_ref, r_ref, dst_ref, o_ref):
        del dst_ref
        for j in range(wb):
            cols = slice(j * LANES, (j + 1) * LANES)
            xf = x_ref[:, cols].astype(F32)
            if rope:
                xf = xf * c_ref[...] + _nn((xf * s_ref[...]).astype(BF16), r_ref[...])
            o_ref[:, cols] = (xf * scale).astype(BF16)

    tab = pl.BlockSpec((tm, LANES), lambda i, j: (i, 0))
    return pl.pallas_call(
        body, name=name, grid=(M // tm, nblk // wb),
        in_specs=[pl.BlockSpec((tm, wb * LANES), lambda i, j: (i, j)), tab, tab,
                  pl.BlockSpec((LANES, LANES), lambda i, j: (0, 0)),
                  pl.BlockSpec(memory_space=pl.ANY)],
        out_specs=pl.BlockSpec((tm, wb * LANES), lambda i, j: (i, blk0 // wb + j)),
        out_shape=jax.ShapeDtypeStruct(dst.shape, dst.dtype),
        input_output_aliases={4: 0},
        compiler_params=_params(2, False))(dsrc, Ct, St, RT, dst)


def _fold_heads(parts, dst, blk0, npair, Ct, St, RT, D0T, D1T, rope, name):
    M = parts[0][0].shape[1]
    nb = M // CHUNK
    R = _tile(M, 1024, CHUNK)
    rb = R // CHUNK
    nrefs = sum(1 if s == 0 else 2 for _, s in parts)

    def body(*refs):
        part_refs = list(refs[:nrefs])
        c_ref, s_ref, r_ref, d0_ref, d1_ref, dst_ref, o_ref = refs[nrefs:]
        del dst_ref
        i = pl.program_id(0)
        tot = [jnp.zeros((R, LANES), F32), jnp.zeros((R, LANES), F32)]
        for _, shift in parts:
            main = part_refs.pop(0)
            if shift == 0:
                for e in range(2):
                    tot[e] = tot[e] + main[e].astype(F32)
                continue
            edge = part_refs.pop(0)
            ok = (i + 1) * rb <= nb - 1 if shift > 0 else i > 0
            for e in range(2):
                ed = jnp.where(ok, edge[e].astype(F32), 0.0)
                if rb == 1:
                    tot[e] = tot[e] + ed
                elif shift > 0:
                    tot[e] = tot[e] + jnp.concatenate([main[e, CHUNK:, :].astype(F32), ed], axis=0)
                else:
                    tot[e] = tot[e] + jnp.concatenate([ed, main[e, :R - CHUNK, :].astype(F32)], axis=0)
        f = _nn(tot[0].astype(BF16), d0_ref[...]) + _nn(tot[1].astype(BF16), d1_ref[...])
        if rope:
            f = f * c_ref[...] + _nn((f * s_ref[...]).astype(BF16), r_ref[...])
        o_ref[...] = f.astype(BF16)

    in_specs, args = [], []
    for a, shift in parts:
        assert shift in (-1, 0, 1)
        in_specs.append(pl.BlockSpec((2, R, LANES), lambda i, p: (p, i, 0)))
        args.append(a)
        if shift > 0:
            in_specs.append(pl.BlockSpec((2, CHUNK, LANES), lambda i, p: (p, jnp.minimum((i + 1) * rb, nb - 1), 0)))
            args.append(a)
        elif shift < 0:
            in_specs.append(pl.BlockSpec((2, CHUNK, LANES), lambda i, p: (p, jnp.maximum(i * rb - 1, 0), 0)))
            args.append(a)
    tab = pl.BlockSpec((R, LANES), lambda i, p: (i, 0))
    mat = pl.BlockSpec((LANES, LANES), lambda i, p: (0, 0))
    return pl.pallas_call(
        body, name=name, grid=(M // R, npair),
        in_specs=in_specs + [tab, tab, mat, mat, mat, pl.BlockSpec(memory_space=pl.ANY)],
        out_specs=pl.BlockSpec((R, LANES), lambda i, p: (i, blk0 + p)),
        out_shape=jax.ShapeDtypeStruct(dst.shape, dst.dtype),
        input_output_aliases={nrefs + 5: 0},
        compiler_params=_params(2, False))(*args, Ct, St, RT, D0T, D1T, dst)


def _head_masks():
    lane = lax.broadcasted_iota(jnp.int32, (1, LANES), 1)
    return [lane < 64, lane >= 64]


def _decay_vecs(lam, mu):
    i = lax.broadcasted_iota(jnp.int32, (CHUNK, 1), 0).astype(F32)
    return dict(qf=jnp.exp(lam * (i + 1.0)), kf=jnp.exp(lam * (CHUNK - 1.0 - i)),
                qb=jnp.exp(mu * (CHUNK - i)), kb=jnp.exp(mu * i),
                gf=jnp.exp(lam * float(CHUNK)), gb=jnp.exp(mu * float(CHUNK)), i=i)


def _decay_mask(lam, mu):
    r = lax.broadcasted_iota(jnp.int32, (CHUNK, CHUNK), 0)
    c = lax.broadcasted_iota(jnp.int32, (CHUNK, CHUNK), 1)
    rel = (r - c).astype(F32)
    low = rel >= 0.0
    mf = jnp.exp(lam * jnp.maximum(rel, 0.0))
    mb = jnp.exp(mu * jnp.maximum(-rel, 0.0))
    return jnp.where(low, mf, mb), rel, low


def _lam_of(lg_ref, row, idx):
    return jnp.full((1, 1), lg_ref[row, idx], F32)


def _group_index(pair_blk, npairs):
    assert pair_blk % npairs == 0
    return pair_blk // npairs


def _ret_states_fwd(Kr, P, Krc, Pc, lg, rv_blk, npairs):
    L = Kr.shape[0]
    Lc = Krc.shape[0]
    N, ncc = L // CHUNK, Lc // CHUNK
    rv_grp = _group_index(rv_blk, npairs)

    heads = [(p, h) for p in range(npairs) for h in range(2)]
    kcols = lambda p: slice(p * LANES, (p + 1) * LANES)
    vcols = lambda p, h: slice((2 * p + h) * LANES, (2 * p + h + 1) * LANES)

    def body(lg_ref, k_ref, v_ref, kc_ref, vc_ref, sf_ref, S):
        n = pl.program_id(0)
        masks = _head_masks()

        @pl.when(n == 0)
        def _():
            for p, h in heads:
                lam = _lam_of(lg_ref, 0, 2 * p + h)
                dv = _decay_vecs(lam, lam)
                s = jnp.zeros((LANES, LANES), F32)
                for cc in range(ncc):
                    rows = slice(cc * CHUNK, (cc + 1) * CHUNK)
                    kw = jnp.where(masks[h], kc_ref[rows, kcols(p)].astype(F32) * dv["kf"], 0.0).astype(BF16)
                    s = dv["gf"] * s + _tn(kw, vc_ref[rows, vcols(p, h)])
                S[p, h] = s

        for p, h in heads:
            lam = _lam_of(lg_ref, 0, 2 * p + h)
            dv = _decay_vecs(lam, lam)
            s = S[p, h]
            sf_ref[p, 0, h] = s.astype(BF16)
            kw = jnp.where(masks[h], k_ref[:, kcols(p)].astype(F32) * dv["kf"], 0.0).astype(BF16)
            S[p, h] = dv["gf"] * s + _tn(kw, v_ref[:, vcols(p, h)])

    wq, wv = npairs * LANES, npairs * 2 * LANES
    return pl.pallas_call(
        body, name="ret_states_fwd", grid=(N,),
        in_specs=[pl.BlockSpec(memory_space=pltpu.SMEM),
                  pl.BlockSpec((CHUNK, wq), lambda n: (n, 0)),
                  pl.BlockSpec((CHUNK, wv), lambda n: (n, rv_grp)),
                  pl.BlockSpec((Lc, wq), lambda n: (0, 0)),
                  pl.BlockSpec((Lc, wv), lambda n: (0, rv_grp))],
        out_specs=pl.BlockSpec((npairs, 1, 2, LANES, LANES), lambda n: (0, n, 0, 0, 0)),
        out_shape=jax.ShapeDtypeStruct((npairs, N, 2, LANES, LANES), BF16),
        scratch_shapes=[pltpu.VMEM((npairs, 2, LANES, LANES), F32)],
        compiler_params=_params(1, False))(lg, Kr, P, Krc, Pc)


def _ret_chunk_fwd(q, k, v, sf, sb, hm, lam, mu):
    dv = _decay_vecs(lam, mu)
    Mk, rel, low = _decay_mask(lam, mu)
    qm = jnp.where(hm, q, jnp.zeros_like(q))
    qmf = qm.astype(F32)
    A = _nt(qm, k)
    Am = A * Mk
    Amb = Am.astype(BF16)
    Qf = (qmf * dv["qf"]).astype(BF16)
    Qb = (qmf * dv["qb"]).astype(BF16)
    O = _nn(Amb, v) + _nn(Qf, sf) + _nn(Qb, sb)
    return dict(dv=dv, Mk=Mk, rel=rel, low=low, qm=qm, Am=Am, Amb=Amb, Qf=Qf, Qb=Qb, O=O)


def _ret_out_fwd(Qr, Kr, P, Krc, Pc, SF, lg, rv_blk, rg_blk, npairs, d_mix):
    L = Qr.shape[0]
    Lc = Krc.shape[0]
    N, ncc = L // CHUNK, Lc // CHUNK

    rv_grp, rg_grp = _group_index(rv_blk, npairs), _group_index(rg_blk, npairs)
    heads = [(p, h) for p in range(npairs) for h in range(2)]
    kcols = lambda p: slice(p * LANES, (p + 1) * LANES)
    vcols = lambda p, h: slice((2 * p + h) * LANES, (2 * p + h + 1) * LANES)

    def body(lg_ref, q_ref, k_ref, v_ref, g_ref, sf_ref, kc_ref, vc_ref, y_ref, sb_ref, S):
        n = pl.program_id(0)
        masks = _head_masks()

        @pl.when(n == 0)
        def _():
            for p, h in heads:
                mu = _lam_of(lg_ref, 1, 2 * p + h)
                dvb = _decay_vecs(mu, mu)
                s = jnp.zeros((LANES, LANES), F32)
                for cc in reversed(range(ncc)):
                    rows = slice(cc * CHUNK, (cc + 1) * CHUNK)
                    kw = jnp.where(masks[h], kc_ref[rows, kcols(p)].astype(F32) * dvb["kb"], 0.0).astype(BF16)
                    s = dvb["gb"] * s + _tn(kw, vc_ref[rows, vcols(p, h)])
                S[p, h] = s

        for p, h in heads:
            lam = _lam_of(lg_ref, 0, 2 * p + h)
            mu = _lam_of(lg_ref, 1, 2 * p + h)
            hm = masks[h]
            dvb = _decay_vecs(lam, mu)
            s = S[p, h]
            sbb = s.astype(BF16)
            sb_ref[p, 0, h] = sbb
            k = k_ref[:, kcols(p)]
            v = v_ref[:, vcols(p, h)]
            f = _ret_chunk_fwd(q_ref[:, kcols(p)], k, v, sf_ref[p, 0, h], sbb, hm, lam, mu)
            O = f["O"]
            r = lax.rsqrt(jnp.mean(O * O, axis=-1, keepdims=True) + NORM_EPS)
            g = g_ref[:, vcols(p, h)].astype(F32)
            y_ref[:, vcols(p, h)] = (O * r * (g * _sigmoid(g))).astype(BF16)
            kw = jnp.where(hm, k.astype(F32) * dvb["kb"], 0.0).astype(BF16)
            S[p, h] = dvb["gb"] * s + _tn(kw, v)

    rev = lambda n: N - 1 - n
    wq, wv = npairs * LANES, npairs * 2 * LANES
    st = pl.BlockSpec((npairs, 1, 2, LANES, LANES), lambda n: (0, rev(n), 0, 0, 0))
    return pl.pallas_call(
        body, name="ret_out_fwd", grid=(N,),
        in_specs=[pl.BlockSpec(memory_space=pltpu.SMEM),
                  pl.BlockSpec((CHUNK, wq), lambda n: (rev(n), 0)),
                  pl.BlockSpec((CHUNK, wq), lambda n: (rev(n), 0)),
                  pl.BlockSpec((CHUNK, wv), lambda n: (rev(n), rv_grp)),
                  pl.BlockSpec((CHUNK, wv), lambda n: (rev(n), rg_grp)),
                  st,
                  pl.BlockSpec((Lc, wq), lambda n: (0, 0)),
                  pl.BlockSpec((Lc, wv), lambda n: (0, rv_grp))],
        out_specs=[pl.BlockSpec((CHUNK, wv), lambda n: (rev(n), 0)), st],
        out_shape=[jax.ShapeDtypeStruct((L, d_mix), BF16),
                   jax.ShapeDtypeStruct((npairs, N, 2, LANES, LANES), BF16)],
        scratch_shapes=[pltpu.VMEM((npairs, 2, LANES, LANES), F32)],
        compiler_params=_params(1))(lg, Qr, Kr, P, P, SF, Krc, Pc)


ACC_ROWS = 8


def _ret_bwd1(Qr, Kr, P, Krc, Pc, SF, SB, dY, lg, rv_blk, rg_blk, npairs, d_proj):
    L = Qr.shape[0]
    Lc = Krc.shape[0]
    N, ncc = L // CHUNK, Lc // CHUNK
    rv_grp, rg_grp = _group_index(rv_blk, npairs), _group_index(rg_blk, npairs)
    heads = [(p, h) for p in range(npairs) for h in range(2)]
    kcols = lambda p: slice(p * LANES, (p + 1) * LANES)
    vcols = lambda p, h: slice((2 * p + h) * LANES, (2 * p + h + 1) * LANES)

    def body(lg_ref, q_ref, k_ref, v_ref, g_ref, sf_ref, sb_ref, dy_ref, kc_ref, vc_ref,
             dq_ref, dk_ref, dv_ref, dg_ref, do_ref, dkc_ref, dvc_ref, acc_ref, dS, T):
        n = pl.program_id(0)
        masks = _head_masks()

        @pl.when(n == 0)
        def _():
            dS[...] = jnp.zeros_like(dS)
            T[...] = jnp.zeros_like(T)
            acc_ref[...] = jnp.zeros_like(acc_ref)

        def head_main(p, h):
            lam = _lam_of(lg_ref, 0, 2 * p + h)
            mu = _lam_of(lg_ref, 1, 2 * p + h)
            hm = masks[h]
            hs = vcols(p, h)
            v = v_ref[:, hs]
            k = k_ref[:, kcols(p)]
            sf = sf_ref[p, 0, h]
            sb = sb_ref[p, 0, h]
            f = _ret_chunk_fwd(q_ref[:, kcols(p)], k, v, sf, sb, hm, lam, mu)
            dv_, O = f["dv"], f["O"]
            r = lax.rsqrt(jnp.mean(O * O, axis=-1, keepdims=True) + NORM_EPS)
            on = O * r
            g = g_ref[:, hs].astype(F32)
            sg = _sigmoid(g)
            dy = dy_ref[:, hs].astype(F32)
            dg_ref[:, hs] = (dy * on * (sg * (1.0 + g * (1.0 - sg)))).astype(BF16)
            don = dy * (g * sg)
            dO = r * (don - on * jnp.mean(don * on, axis=-1, keepdims=True))
            dOb = dO.astype(BF16)
            do_ref[:, hs] = dOb
            dAm = _nt(dOb, v)
            T[p, h] += dAm * f["Am"]
            dAb = (dAm * f["Mk"]).astype(BF16)
            km = jnp.where(hm, k, jnp.zeros_like(k))
            dq = _nn(dAb, km)
            dk = _tn(dAb, f["qm"])
            dvh = _tn(f["Amb"], dOb)
            dQf = _nt(dOb, sf)
            dQb = _nt(dOb, sb)
            dq = dq + dQf * dv_["qf"] + dQb * dv_["qb"]
            acc_ref[p, h, 0:1, :] += _fsum(dQf * f["Qf"].astype(F32) * (dv_["i"] + 1.0))
            acc_ref[p, h, 1:2, :] += _fsum(dQb * f["Qb"].astype(F32) * (CHUNK - dv_["i"]))
            dSh = dS[p, h]
            dSb_ = dSh.astype(BF16)
            Kf = (km.astype(F32) * dv_["kf"]).astype(BF16)
            dKf = _nt(v, dSb_)
            dk = dk + jnp.where(hm, dKf * dv_["kf"], 0.0)
            acc_ref[p, h, 2:3, :] += _fsum(jnp.where(hm, dKf, 0.0) * Kf.astype(F32) * (CHUNK - 1.0 - dv_["i"]))
            dvh = dvh + _nn(Kf, dSb_)
            acc_ref[p, h, 3:4, :] += float(CHUNK) * dv_["gf"] * _fsum(dSh * sf.astype(F32))
            dSh = dv_["gf"] * dSh + _tn(f["Qf"], dOb)
            dS[p, h] = dSh
            dv_ref[:, hs] = dvh
            return dq, dk

        for p in range(npairs):
            dq0, dk0 = head_main(p, 0)
            dq1, dk1 = head_main(p, 1)
            dq_ref[:, kcols(p)] = dq0 + dq1
            dk_ref[:, kcols(p)] = dk0 + dk1

        @pl.when(n == N - 1)
        def _():
            for p, h in heads:
                lam = _lam_of(lg_ref, 0, 2 * p + h)
                dv_ = _decay_vecs(lam, lam)
                hm = masks[h]
                hs = vcols(p, h)
                states = [jnp.zeros((LANES, LANES), F32)]
                kws = []
                for cc in range(ncc):
                    rows = slice(cc * CHUNK, (cc + 1) * CHUNK)
                    kw = jnp.where(hm, kc_ref[rows, kcols(p)].astype(F32) * dv_["kf"], 0.0).astype(BF16)
                    kws.append(kw)
                    states.append(dv_["gf"] * states[-1] + _tn(kw, vc_ref[rows, hs]))
                d = dS[p, h]
                for cc in reversed(range(ncc)):
                    db = d.astype(BF16)
                    rows = slice(cc * CHUNK, (cc + 1) * CHUNK)
                    dKf_c = jnp.where(hm, _nt(vc_ref[rows, hs], db), 0.0)
                    part = dKf_c * dv_["kf"]
                    if h == 0:
                        dkc_ref[rows, kcols(p)] = part
                    else:
                        dkc_ref[rows, kcols(p)] += part
                    acc_ref[p, h, 2:3, :] += _fsum(dKf_c * kws[cc].astype(F32) * (CHUNK - 1.0 - dv_["i"]))
                    dvc_ref[rows, hs] = _nn(kws[cc], db)
                    acc_ref[p, h, 3:4, :] += float(CHUNK) * dv_["gf"] * _fsum(d * states[cc])
                    d = dv_["gf"] * d
                _, rel, low = _decay_mask(lam, lam)
                Th = T[p, h]
                acc_ref[p, h, 4:5, :] += _fsum(jnp.where(low, Th * rel, 0.0))
                acc_ref[p, h, 5:6, :] += _fsum(jnp.where(low, 0.0, -Th * rel))

    rev = lambda n: N - 1 - n
    wq, wv = npairs * LANES, npairs * 2 * LANES
    st = pl.BlockSpec((npairs, 1, 2, LANES, LANES), lambda n: (0, rev(n), 0, 0, 0))
    pair = pl.BlockSpec((CHUNK, wq), lambda n: (rev(n), 0))
    wide = lambda grp: pl.BlockSpec((CHUNK, wv), lambda n: (rev(n), grp))
    return pl.pallas_call(
        body, name="ret_bwd_desc", grid=(N,),
        in_specs=[pl.BlockSpec(memory_space=pltpu.SMEM), pair, pair, wide(rv_grp), wide(rg_grp), st, st, wide(0),
                  pl.BlockSpec((Lc, wq), lambda n: (0, 0)),
                  pl.BlockSpec((Lc, wv), lambda n: (0, rv_grp))],
        out_specs=[pair, pair, wide(0), wide(rg_grp), wide(0),
                   pl.BlockSpec((Lc, wq), lambda n: (0, 0)),
                   pl.BlockSpec((Lc, wv), lambda n: (0, 0)),
                   pl.BlockSpec((npairs, 2, ACC_ROWS, LANES), lambda n: (0, 0, 0, 0))],
        out_shape=[jax.ShapeDtypeStruct((L, npairs * LANES), F32),
                   jax.ShapeDtypeStruct((L, npairs * LANES), F32),
                   jax.ShapeDtypeStruct((L, npairs * 2 * LANES), F32),
                   jax.ShapeDtypeStruct((L, d_proj), BF16),
                   jax.ShapeDtypeStruct((L, npairs * 2 * LANES), BF16),
                   jax.ShapeDtypeStruct((Lc, npairs * LANES), F32),
                   jax.ShapeDtypeStruct((Lc, npairs * 2 * LANES), F32),
                   jax.ShapeDtypeStruct((npairs, 2, ACC_ROWS, LANES), F32)],
        scratch_shapes=[pltpu.VMEM((npairs, 2, LANES, LANES), F32), pltpu.VMEM((npairs, 2, CHUNK, CHUNK), F32)],
        compiler_params=_params(1))(lg, Qr, Kr, P, P, SF, SB, dY, Krc, Pc)


def _ret_bwd2(Qr, Kr, P, Krc, Pc, SB, dO, dKr, dVp, dP, dKc, dVc, lg, rv_blk, npairs):
    L = Qr.shape[0]
    Lc = Krc.shape[0]
    N, ncc = L // CHUNK, Lc // CHUNK
    rv_grp = _group_index(rv_blk, npairs)
    heads = [(p, h) for p in range(npairs) for h in range(2)]
    kcols = lambda p: slice(p * LANES, (p + 1) * LANES)
    vcols = lambda p, h: slice((2 * p + h) * LANES, (2 * p + h + 1) * LANES)

    def body(lg_ref, q_ref, k_ref, v_ref, sb_ref, do_ref, dkin_ref, dvin_ref, kc_ref, vc_ref, dkcin_ref, dvcin_ref,
             dpin_ref, dk_ref, dv_ref, dkc_ref, dvc_ref, acc_ref, dS):
        del dpin_ref
        n = pl.program_id(0)
        masks = _head_masks()

        @pl.when(n == 0)
        def _():
            dS[...] = jnp.zeros_like(dS)
            acc_ref[...] = jnp.zeros_like(acc_ref)

        def head_main(p, h):
            mu = _lam_of(lg_ref, 1, 2 * p + h)
            hm = masks[h]
            hs = vcols(p, h)
            dv_ = _decay_vecs(mu, mu)
            v = v_ref[:, hs]
            k = k_ref[:, kcols(p)]
            q = q_ref[:, kcols(p)]
            dOb = do_ref[:, hs]
            km = jnp.where(hm, k, jnp.zeros_like(k)).astype(F32)
            Kb = (km * dv_["kb"]).astype(BF16)
            Qb = (jnp.where(hm, q, jnp.zeros_like(q)).astype(F32) * dv_["qb"]).astype(BF16)
            dSh = dS[p, h]
            dSb_ = dSh.astype(BF16)
            dKb = jnp.where(hm, _nt(v, dSb_), 0.0)
            acc_ref[p, h, 0:1, :] += _fsum(dKb * Kb.astype(F32) * dv_["i"])
            dv_ref[:, hs] = (dvin_ref[:, hs] + _nn(Kb, dSb_)).astype(BF16)
            acc_ref[p, h, 1:2, :] += float(CHUNK) * dv_["gb"] * _fsum(dSh * sb_ref[p, 0, h].astype(F32))
            dS[p, h] = dv_["gb"] * dSh + _tn(Qb, dOb)
            return dKb * dv_["kb"]

        for p in range(npairs):
            dk_ref[:, kcols(p)] = dkin_ref[:, kcols(p)] + head_main(p, 0) + head_main(p, 1)

        @pl.when(n == N - 1)
        def _():
            for p, h in heads:
                mu = _lam_of(lg_ref, 1, 2 * p + h)
                hm = masks[h]
                hs = vcols(p, h)
                dv_ = _decay_vecs(mu, mu)
                states = {}
                kws = {}
                s = jnp.zeros((LANES, LANES), F32)
                for cc in reversed(range(ncc)):
                    rows = slice(cc * CHUNK, (cc + 1) * CHUNK)
                    states[cc] = s
                    kw = jnp.where(hm, kc_ref[rows, kcols(p)].astype(F32) * dv_["kb"], 0.0).astype(BF16)
                    kws[cc] = kw
                    s = dv_["gb"] * s + _tn(kw, vc_ref[rows, hs])
                d = dS[p, h]
                for cc in range(ncc):
                    db = d.astype(BF16)
                    rows = slice(cc * CHUNK, (cc + 1) * CHUNK)
                    dKb_c = jnp.where(hm, _nt(vc_ref[rows, hs], db), 0.0)
                    part = dKb_c * dv_["kb"]
                    if h == 0:
                        dkc_ref[rows, kcols(p)] = dkcin_ref[rows, kcols(p)] + part
                    else:
                        dkc_ref[rows, kcols(p)] += part
                    acc_ref[p, h, 0:1, :] += _fsum(dKb_c * kws[cc].astype(F32) * dv_["i"])
                    dvc_ref[rows, hs] = dvcin_ref[rows, hs] + _nn(kws[cc], db)
                    acc_ref[p, h, 1:2, :] += float(CHUNK) * dv_["gb"] * _fsum(d * states[cc])
                    d = dv_["gb"] * d

    wq, wv = npairs * LANES, npairs * 2 * LANES
    st = pl.BlockSpec((npairs, 1, 2, LANES, LANES), lambda n: (0, n, 0, 0, 0))
    pair = pl.BlockSpec((CHUNK, wq), lambda n: (n, 0))
    wide = lambda grp: pl.BlockSpec((CHUNK, wv), lambda n: (n, grp))
    ckc = pl.BlockSpec((Lc, wq), lambda n: (0, 0))
    cvc = lambda grp: pl.BlockSpec((Lc, wv), lambda n: (0, grp))
    return pl.pallas_call(
        body, name="ret_bwd_asc", grid=(N,),
        in_specs=[pl.BlockSpec(memory_space=pltpu.SMEM), pair, pair, wide(rv_grp), st, wide(0), pair, wide(0),
                  ckc, cvc(rv_grp), ckc, cvc(0), pl.BlockSpec(memory_space=pl.ANY)],
        out_specs=[pair, wide(rv_grp), ckc, cvc(0),
                   pl.BlockSpec((npairs, 2, ACC_ROWS, LANES), lambda n: (0, 0, 0, 0))],
        out_shape=[jax.ShapeDtypeStruct(dKr.shape, F32),
                   jax.ShapeDtypeStruct(dP.shape, dP.dtype),
                   jax.ShapeDtypeStruct(dKc.shape, F32),
                   jax.ShapeDtypeStruct(dVc.shape, F32),
                   jax.ShapeDtypeStruct((npairs, 2, ACC_ROWS, LANES), F32)],
        input_output_aliases={12: 1},
        scratch_shapes=[pltpu.VMEM((npairs, 2, LANES, LANES), F32)],
        compiler_params=_params(1))(lg, Qr, Kr, P, SB, dO, dKr, dVp, Krc, Pc, dKc, dVc, dP)


GROUP = 4


def _att_valid(n, N, Lc):
    rows = GROUP * CHUNK
    row = lax.broadcasted_iota(jnp.int32, (rows, 3 * CHUNK + Lc), 0) % CHUNK
    col = lax.broadcasted_iota(jnp.int32, (rows, 3 * CHUNK + Lc), 1)
    ok = jnp.logical_and(col >= row, col <= row + 2 * CHUNK)
    ok = jnp.logical_and(ok, jnp.logical_or(col >= CHUNK, n > 0))
    ok = jnp.logical_and(ok, jnp.logical_or(col < 2 * CHUNK, n < N - 1))
    return jnp.logical_or(ok, col >= 3 * CHUNK)


def _stack_heads(ref):
    masks = _head_masks()
    tiles = []
    for pr in range(2):
        t = ref[:, pr * LANES:(pr + 1) * LANES]
        for a in range(2):
            tiles.append(jnp.where(masks[a], t, jnp.zeros_like(t)))
    return jnp.concatenate(tiles, axis=0)


def _unstack_heads(x4):
    m0 = _head_masks()[0]
    return [jnp.where(m0, x4[(2 * pr) * CHUNK:(2 * pr + 1) * CHUNK], x4[(2 * pr + 1) * CHUNK:(2 * pr + 2) * CHUNK])
            for pr in range(2)]


def _sink_column(sink_ref, g):
    row = lax.broadcasted_iota(jnp.int32, (GROUP * CHUNK, 1), 0) // CHUNK
    col = jnp.zeros((GROUP * CHUNK, 1), F32)
    for h in range(GROUP):
        col = jnp.where(row == h, sink_ref[0, g * GROUP + h], col)
    return col


def _att_probs(q4, Kall, valid, snk):
    s = jnp.where(valid, _nt(q4, Kall), NEG)
    mx = jnp.maximum(jnp.max(s, axis=1, keepdims=True), snk)
    p = jnp.exp(s - mx)
    p_snk = jnp.exp(snk - mx)
    inv = 1.0 / (jnp.sum(p, axis=1, keepdims=True) + p_snk)
    return p, p_snk, inv


def _att_specs(Lc, N):
    q = pl.BlockSpec((CHUNK, 2 * LANES), lambda g, n: (n, g))
    kv = lambda s: pl.BlockSpec((1, CHUNK, LANES), lambda g, n: (g, jnp.clip(n + s, 0, N - 1), 0))
    ctx = pl.BlockSpec((1, Lc, LANES), lambda g, n: (g, 0, 0))
    return q, kv, ctx


def _att_fwd(Qa, Kd, Vd, Kdc, Vdc, sink, Y, blk0):
    L = Qa.shape[0]
    Lc = Kdc.shape[1]
    N = L // CHUNK
    nkv = Kd.shape[0]

    def body(sink_ref, q_ref, kp, kc_, kn, vp, vc_, vn, kctx, vctx, y_in, o_ref):
        del y_in
        g, n = pl.program_id(0), pl.program_id(1)
        Kall = jnp.concatenate([kp[0], kc_[0], kn[0], kctx[0]], axis=0)
        Vall = jnp.concatenate([vp[0], vc_[0], vn[0], vctx[0]], axis=0)
        p, _, inv = _att_probs(_stack_heads(q_ref), Kall, _att_valid(n, N, Lc), _sink_column(sink_ref, g))
        o4 = _nn(p.astype(BF16), Vall) * inv
        for pr, o in enumerate(_unstack_heads(o4)):
            o_ref[:, pr * LANES:(pr + 1) * LANES] = o.astype(BF16)

    q, kv, ctx = _att_specs(Lc, N)
    return pl.pallas_call(
        body, name="att_fwd", grid=(nkv, N),
        in_specs=[pl.BlockSpec(memory_space=pltpu.SMEM), q, kv(-1), kv(0), kv(1), kv(-1), kv(0), kv(1), ctx, ctx,
                  pl.BlockSpec(memory_space=pl.ANY)],
        out_specs=pl.BlockSpec((CHUNK, 2 * LANES), lambda g, n: (n, blk0 + g)),
        out_shape=jax.ShapeDtypeStruct(Y.shape, Y.dtype),
        input_output_aliases={10: 0},
        compiler_params=_params(2))(sink, Qa, Kd, Kd, Kd, Vd, Vd, Vd, Kdc, Vdc, Y)


def _att_bwd(Qa, Kd, Vd, Kdc, Vdc, sink, dY, blk0):
    L = Qa.shape[0]
    Lc = Kdc.shape[1]
    N = L // CHUNK
    nkv = Kd.shape[0]

    def body(sink_ref, q_ref, kp, kc_, kn, vp, vc_, vn, kctx, vctx, dy_ref,
             dq_ref, dkp, dkc_, dkn, dvp, dvc_, dvn, dkctx, dvctx, dsink_ref):
        g, n = pl.program_id(0), pl.program_id(1)

        @pl.when(n == 0)
        def _():
            dkctx[...] = jnp.zeros_like(dkctx)
            dvctx[...] = jnp.zeros_like(dvctx)
            dsink_ref[...] = jnp.zeros_like(dsink_ref)

        Kall = jnp.concatenate([kp[0], kc_[0], kn[0], kctx[0]], axis=0)
        Vall = jnp.concatenate([vp[0], vc_[0], vn[0], vctx[0]], axis=0)
        q4 = _stack_heads(q_ref)
        do4 = _stack_heads(dy_ref)
        p, p_snk, inv = _att_probs(q4, Kall, _att_valid(n, N, Lc), _sink_column(sink_ref, g))
        P = p * inv
        dp = _nt(do4, Vall)
        delta = jnp.sum(P * dp, axis=1, keepdims=True)
        ds = (P * (dp - delta)).astype(BF16)
        dsnk = -(p_snk * inv) * delta
        for h in range(GROUP):
            dsink_ref[0, h:h + 1, :] += _fsum(dsnk[h * CHUNK:(h + 1) * CHUNK])
        for pr, dq in enumerate(_unstack_heads(_nn(ds, Kall))):
            dq_ref[:, pr * LANES:(pr + 1) * LANES] = dq
        dK = _tn(ds, q4)
        dV = _tn(P.astype(BF16), do4)
        for j, (rk, rv) in enumerate([(dkp, dvp), (dkc_, dvc_), (dkn, dvn)]):
            rk[0] = dK[j * CHUNK:(j + 1) * CHUNK].astype(BF16)
            rv[0] = dV[j * CHUNK:(j + 1) * CHUNK].astype(BF16)
        dkctx[0] += dK[3 * CHUNK:]
        dvctx[0] += dV[3 * CHUNK:]

    q, kv, ctx = _att_specs(Lc, N)
    blk = pl.BlockSpec((1, CHUNK, LANES), lambda g, n: (g, n, 0))
    part = jax.ShapeDtypeStruct((nkv, L, LANES), BF16)
    cshape = jax.ShapeDtypeStruct((nkv, Lc, LANES), F32)
    return pl.pallas_call(
        body, name="att_bwd", grid=(nkv, N),
        in_specs=[pl.BlockSpec(memory_space=pltpu.SMEM), q, kv(-1), kv(0), kv(1), kv(-1), kv(0), kv(1), ctx, ctx,
                  pl.BlockSpec((CHUNK, 2 * LANES), lambda g, n: (n, blk0 + g))],
        out_specs=[q, blk, blk, blk, blk, blk, blk, ctx, ctx,
                   pl.BlockSpec((1, 8, LANES), lambda g, n: (g, 0, 0))],
        out_shape=[jax.ShapeDtypeStruct(Qa.shape, F32), part, part, part, part, part, part, cshape, cshape,
                   jax.ShapeDtypeStruct((nkv, 8, LANES), F32)],
        compiler_params=_params(2))(sink, Qa, Kd, Kd, Kd, Vd, Vd, Vd, Kdc, Vdc, dY)


def _bwd_proj(dx, gt, w, saved, G=None, U=None, name="bwd_proj"):
    M, D = dx.shape
    N = w.shape[0]
    swiglu = G is not None
    tm, tn = _tile(M, ROWS_PER_LATCH, 8), _tile(N, 256 if swiglu else 512)

    def body(*refs):
        if swiglu:
            dx_ref, gt_ref, w_ref, sv_ref, G_ref, U_ref, dG_ref, dU_ref, dz_ref, dgt_ref, zs = refs
        else:
            dx_ref, gt_ref, w_ref, sv_ref, dA_ref, dz_ref, dgt_ref, zs = refs
        i, j = pl.program_id(0), pl.program_id(1)

        @pl.when(jnp.logical_and(i == 0, j == 0))
        def _():
            dgt_ref[...] = jnp.zeros_like(dgt_ref)

        @pl.when(j == 0)
        def _():
            d = dx_ref[...]
            z = (d * gt_ref[...]).astype(BF16)
            zs[...] = z
            dz_ref[...] = z
            dgt_ref[...] += jnp.sum(d * sv_ref[...].astype(F32), axis=0, keepdims=True)

        dA = _nt(zs[...], w_ref[...])
        if swiglu:
            Gv = G_ref[...].astype(F32)
            Uv = U_ref[...].astype(F32)
            sg = _sigmoid(Gv)
            dU_ref[...] = (dA * Gv * sg).astype(BF16)
            dG_ref[...] = (dA * Uv * (sg * (1.0 + Gv * (1.0 - sg)))).astype(BF16)
        else:
            dA_ref[...] = dA.astype(BF16)

    row = pl.BlockSpec((tm, D), lambda i, j: (i, 0))
    vec = pl.BlockSpec((1, D), lambda i, j: (0, 0))
    tile = pl.BlockSpec((tm, tn), lambda i, j: (i, j))
    big = jax.ShapeDtypeStruct((M, N), BF16)
    in_specs = [row, vec, pl.BlockSpec((tn, D), lambda i, j: (j, 0)), row]
    args = [dx, gt, w, saved]
    if swiglu:
        in_specs += [tile, tile]
        args += [G, U]
        out_specs = [tile, tile, row, vec]
        out_shape = [big, big, jax.ShapeDtypeStruct((M, D), BF16), jax.ShapeDtypeStruct((1, D), F32)]
    else:
        out_specs = [tile, row, vec]
        out_shape = [big, jax.ShapeDtypeStruct((M, D), BF16), jax.ShapeDtypeStruct((1, D), F32)]
    return pl.pallas_call(
        body, name=name, grid=(M // tm, N // tn), in_specs=in_specs, out_specs=out_specs, out_shape=out_shape,
        scratch_shapes=[pltpu.VMEM((tm, D), BF16)], compiler_params=_params(2))(*args)


def _tn_matmul(pairs, name):
    Ka, Nb = pairs[0][0].shape[1], pairs[0][1].shape[1]
    tk, tn = _tile(Ka, 2048), _tile(Nb, 2048)
    tls, nks = [], []
    for a, _ in pairs:
        tl = _tile(a.shape[0], 512, 8)
        tls.append(tl)
        nks.append(a.shape[0] // tl)
    starts = [int(s) for s in np.cumsum([0] + nks[:-1])]
    nk = int(sum(nks))

    def body(*refs):
        out_ref, acc = refs[-2], refs[-1]
        k = pl.program_id(2)

        @pl.when(k == 0)
        def _():
            acc[...] = jnp.zeros_like(acc)

        for idx in range(len(pairs)):
            a_ref, b_ref = refs[2 * idx], refs[2 * idx + 1]

            @pl.when(jnp.logical_and(k >= starts[idx], k < starts[idx] + nks[idx]))
            def _():
                acc[...] += _tn(a_ref[...], b_ref[...])

        @pl.when(k == nk - 1)
        def _():
            out_ref[...] = acc[...].astype(BF16)

    in_specs, args = [], []
    for idx, (a, b) in enumerate(pairs):
        s0, n_ = starts[idx], nks[idx]
        in_specs.append(pl.BlockSpec((tls[idx], tk), lambda i, j, k, s0=s0, n_=n_: (jnp.clip(k - s0, 0, n_ - 1), i)))
        in_specs.append(pl.BlockSpec((tls[idx], tn), lambda i, j, k, s0=s0, n_=n_: (jnp.clip(k - s0, 0, n_ - 1), j)))
        args += [a, b]
    return pl.pallas_call(
        body, name=name, grid=(Ka // tk, Nb // tn, nk), in_specs=in_specs,
        out_specs=pl.BlockSpec((tk, tn), lambda i, j, k: (i, j)),
        out_shape=jax.ShapeDtypeStruct((Ka, Nb), BF16),
        scratch_shapes=[pltpu.VMEM((tk, tn), F32)], compiler_params=_params(3))(*args)


def _bwd_norm_mod(pairs, x, dres, g, sh, sc, name):
    M, D = x.shape
    K = pairs[0][0].shape[1]
    tm, tk = _tile(M, 512, 8), _tile(K, 1152 if len(pairs) == 1 else 512)
    nk = K // tk
    npair = len(pairs)
    has_res = dres is not None

    def body(*refs):
        pr = refs[:2 * npair]
        rest = refs[2 * npair:]
        if has_res:
            x_ref, dres_ref, g_ref, sh_ref, sc_ref, dx_ref, st_ref, acc = rest
        else:
            x_ref, g_ref, sh_ref, sc_ref, dx_ref, st_ref, acc = rest
        del sh_ref
        i, k = pl.program_id(0), pl.program_id(1)

        @pl.when(jnp.logical_and(i == 0, k == 0))
        def _():
            st_ref[...] = jnp.zeros_like(st_ref)

        @pl.when(k == 0)
        def _():
            acc[...] = jnp.zeros_like(acc)

        t = _nt(pr[1][...], pr[0][...])
        for idx in range(1, npair):
            t = t + _nt(pr[2 * idx + 1][...], pr[2 * idx][...])
        acc[...] += t

        @pl.when(k == nk - 1)
        def _():
            xv = x_ref[...]
            gv = g_ref[...]
            dh = acc[...].T
            r = lax.rsqrt(jnp.mean(xv * xv, axis=-1, keepdims=True) + NORM_EPS)
            xh = xv * r
            st_ref[0:1, :] += jnp.sum(dh, axis=0, keepdims=True)
            st_ref[1:2, :] += jnp.sum(dh * (xh * gv), axis=0, keepdims=True)
            dn = dh * (1.0 + sc_ref[...])
            st_ref[2:3, :] += jnp.sum(dn * xh, axis=0, keepdims=True)
            dxh = dn * gv
            d = r * (dxh - xh * jnp.mean(dxh * xh, axis=-1, keepdims=True))
            if has_res:
                d = d + dres_ref[...]
            dx_ref[...] = d

    row = pl.BlockSpec((tm, D), lambda i, k: (i, 0))
    vec = pl.BlockSpec((1, D), lambda i, k: (0, 0))
    in_specs, args = [], []
    for dA, w in pairs:
        in_specs += [pl.BlockSpec((tm, tk), lambda i, k: (i, k)), pl.BlockSpec((D, tk), lambda i, k: (0, k))]
        args += [dA, w]
    in_specs += [row] + ([row] if has_res else []) + [vec, vec, vec]
    args += [x] + ([dres] if has_res else []) + [g, sh, sc]
    return pl.pallas_call(
        body, name=name, grid=(M // tm, nk), in_specs=in_specs,
        out_specs=[row, pl.BlockSpec((8, D), lambda i, k: (0, 0))],
        out_shape=[jax.ShapeDtypeStruct((M, D), F32), jax.ShapeDtypeStruct((8, D), F32)],
        scratch_shapes=[pltpu.VMEM((D, tm), F32)], compiler_params=_params(2))(*args)


def _local_step(x, ctx, tgt, mod, modc, norm_mix, norm_ffn, norm_final, lg, sink, w_in, rest_weights, on_grads):
    L, D = x.shape
    Lc = ctx.shape[0]
    d_proj = w_in.shape[1]
    npairs = RET_HEADS // 2
    nkv = ATT_KV_HEADS
    nkvp = nkv // 2
    o_rq = 0
    o_rk = o_rq + RET_HEADS * RET_DK // LANES
    o_rv = o_rk + RET_HEADS * RET_DK // LANES
    o_rg = o_rv + RET_HEADS * RET_DV // LANES
    o_aq = o_rg + RET_HEADS * RET_DV // LANES
    o_ak = o_aq + ATT_HEADS * ATT_DH // LANES
    o_av = o_ak + nkv * ATT_DH // LANES
    assert (o_av + nkv * ATT_DH // LANES) * LANES == d_proj
    assert o_rv % 2 == 0 and o_rg % 2 == 0 and (RET_HEADS * RET_DV) % (2 * LANES) == 0
    rv_blk, rg_blk = o_rv // 2, o_rg // 2
    d_ret = RET_HEADS * RET_DV
    d_mix = d_ret + ATT_HEADS * ATT_DH
    att_blk = d_ret // (2 * LANES)
    k_scale = RET_DK ** -0.5
    a_scale = ATT_DH ** -0.5

    T = _rope_tables(L)
    Tc = dict(C=jnp.ones((Lc, LANES), F32), S=jnp.zeros((Lc, LANES), F32))
    row = lambda m, i: m[i:i + 1]
    sh_m, sc_m, gt_m, sh_f, sc_f, gt_f = [row(mod, i) for i in range(6)]
    sh_mc, sc_mc = row(modc, 0), row(modc, 1)

    P, hx = _norm_mod_matmul(x, norm_mix, sh_m, sc_m, w_in, "in_proj")
    Pc, hc = _norm_mod_matmul(ctx, norm_mix, sh_mc, sc_mc, w_in, "in_proj_ctx")
    nq = RET_HEADS * RET_DK // LANES
    Qr = _rope_cols(P, o_rq, nq, T["Cr"], T["Sr"], T["Rr"], 1.0, True, "rope_rq")
    Kr = _rope_cols(P, o_rk, nq, T["Cr"], T["Sr"], T["Rr"], k_scale, True, "rope_rk")
    Krc = _rope_cols(Pc, o_rk, nq, Tc["C"], Tc["S"], T["Rr"], k_scale, False, "scale_rk_ctx")
    Qa = _rope_cols(P, o_aq, ATT_HEADS * ATT_DH // LANES, T["Ca"], T["Sa"], T["Ra"], a_scale, True, "rope_aq")
    Kd = _dup_heads(P, o_ak, nkvp, T["Ca"], T["Sa"], T["Ra"], T["D0"], T["D1"], True, "dup_ak")
    Vd = _dup_heads(P, o_av, nkvp, T["Ca"], T["Sa"], T["Ra"], T["D0"], T["D1"], False, "dup_av")
    Kdc = _dup_heads(Pc, o_ak, nkvp, Tc["C"], Tc["S"], T["Ra"], T["D0"], T["D1"], False, "dup_ak_ctx")
    Vdc = _dup_heads(Pc, o_av, nkvp, Tc["C"], Tc["S"], T["Ra"], T["D0"], T["D1"], False, "dup_av_ctx")

    SF = _ret_states_fwd(Kr, P, Krc, Pc, lg, rv_blk, npairs)
    Y, SB = _ret_out_fwd(Qr, Kr, P, Krc, Pc, SF, lg, rv_blk, rg_blk, npairs, d_mix)
    Y = _att_fwd(Qa, Kd, Vd, Kdc, Vdc, sink, Y, att_blk)

    w_out, w_gate, w_up, w_down = rest_weights(Y)
    x1, O1 = _proj_residual(Y, w_out, x, gt_m, "out_proj")
    G, U, A, h2 = _ffn_in(x1, norm_ffn, sh_f, sc_f, w_gate, w_up)
    x2, Fo = _proj_residual(A, w_down, x1, gt_f, "ffn_out")
    dx2, loss, d_norm_final = _final(x2, norm_final, tgt)

    dG, dU, dz2, dgt_f = _bwd_proj(dx2, gt_f, w_down, Fo, G, U, name="ffn_out_bwd")
    g_w_down = _tn_matmul([(A, dz2)], "grad_w_down")
    tok = on_grads(["w_down"], [g_w_down])
    dx1, st_f = _bwd_norm_mod([(dG, w_gate), (dU, w_up)], x1, dx2, norm_ffn + tok, sh_f, sc_f, "ffn_in_bwd")
    g_w_gate = _tn_matmul([(h2, dG)], "grad_w_gate")
    g_w_up = _tn_matmul([(h2, dU)], "grad_w_up")
    tok = on_grads(["w_gate", "w_up"], [g_w_gate, g_w_up])
    dY, dz1, dgt_m = _bwd_proj(dx1, gt_m + tok, w_out, O1, name="out_proj_bwd")
    g_w_out = _tn_matmul([(Y, dz1)], "grad_w_out")
    tok = on_grads(["w_out"], [g_w_out])

    dQa, dKp, dKs, dKn, dVp, dVs, dVn, dKdc, dVdc, dsink = _att_bwd(Qa, Kd, Vd, Kdc, Vdc, sink + tok, dY, att_blk)
    dQr, dKr, dVr, dP, dO, dKc, dVc, acc1 = _ret_bwd1(Qr, Kr, P, Krc, Pc, SF, SB, dY, lg, rv_blk, rg_blk, npairs, d_proj)
    dKr, dP, dKc, dVc, acc2 = _ret_bwd2(Qr, Kr, P, Krc, Pc, SB, dO, dKr, dVr, dP, dKc, dVc, lg, rv_blk, npairs)

    dP = _unrope_cols(dQr, dP, o_rq, nq, T["Cr"], T["Sr"], T["RrT"], 1.0, True, "unrope_rq")
    dP = _unrope_cols(dKr, dP, o_rk, nq, T["Cr"], T["Sr"], T["RrT"], k_scale, True, "unrope_rk")
    dP = _unrope_cols(dQa, dP, o_aq, ATT_HEADS * ATT_DH // LANES, T["Ca"], T["Sa"], T["RaT"], a_scale, True, "unrope_aq")
    dP = _fold_heads([(dKs, 0), (dKp, 1), (dKn, -1)], dP, o_ak, nkvp, T["Ca"], T["Sa"], T["RaT"], T["D0T"], T["D1T"],
                     True, "fold_ak")
    dP = _fold_heads([(dVs, 0), (dVp, 1), (dVn, -1)], dP, o_av, nkvp, T["Ca"], T["Sa"], T["RaT"], T["D0T"], T["D1T"],
                     False, "fold_av")
    dPc = jnp.zeros((Lc, d_proj), BF16)
    dPc = _unrope_cols(dKc, dPc, o_rk, nq, Tc["C"], Tc["S"], T["RrT"], k_scale, False, "ctx_rk_bwd")
    dPc = _unrope_cols(dVc, dPc, o_rv, RET_HEADS * RET_DV // LANES, Tc["C"], Tc["S"], T["RrT"], 1.0, False, "ctx_rv_bwd")
    dPc = _fold_heads([(dKdc.astype(BF16), 0)], dPc, o_ak, nkvp, Tc["C"], Tc["S"], T["RaT"], T["D0T"], T["D1T"],
                      False, "fold_ak_ctx")
    dPc = _fold_heads([(dVdc.astype(BF16), 0)], dPc, o_av, nkvp, Tc["C"], Tc["S"], T["RaT"], T["D0T"], T["D1T"],
                      False, "fold_av_ctx")

    dx, st_m = _bwd_norm_mod([(dP, w_in)], x, dx1, norm_mix, sh_m, sc_m, "in_proj_bwd")
    _, st_mc = _bwd_norm_mod([(dPc, w_in)], ctx, None, norm_mix, sh_mc, sc_mc, "in_proj_ctx_bwd")
    g_w_in = _tn_matmul([(hx, dP), (hc, dPc)], "grad_w_in")
    on_grads(["w_in"], [g_w_in])

    a1 = acc1[:, :, :, 0].reshape(RET_HEADS, ACC_ROWS)
    a2 = acc2[:, :, :, 0].reshape(RET_HEADS, ACC_ROWS)
    dlam = (a1[:, 0] + a1[:, 2] + a1[:, 3] + a1[:, 4]) * lg[0]
    dmu = (a1[:, 1] + a1[:, 5] + a2[:, 0] + a2[:, 1]) * lg[1]
    d_sink = dsink[:, :4, 0].reshape(1, ATT_HEADS)

    nh = RET_HEADS
    assert 2 * nh + ATT_HEADS <= LOSS_LANE
    small = _pack_rows(
        [(st_m, 0, 2, 0, 0), (dgt_m, 0, 1, 2, 0), (st_f, 0, 2, 3, 0), (dgt_f, 0, 1, 5, 0), (st_mc, 0, 2, 6, 0),
         (st_m[2:3] + st_mc[2:3], 0, 1, 12, 0), (st_f, 2, 1, 13, 0), (d_norm_final, 0, 1, 14, 0),
         (dlam.reshape(1, nh), 0, 1, 15, 0), (dmu.reshape(1, nh), 0, 1, 15, nh), (d_sink, 0, 1, 15, 2 * nh),
         (loss[:, 0:1], 0, 1, 15, LOSS_LANE)], 16, D, "pack_small")
    return dict(grad_x=dx, small=small)


def _my_pos():
    return lax.axis_index("x"), lax.axis_index("y"), lax.axis_index("c")


def _other_chips(x, y):
    return [(1 - x, y), (x, 1 - y), (1 - x, 1 - y)]


def _remote(src, dst, ssem, rsem, dev):
    return pltpu.make_async_remote_copy(src_ref=src, dst_ref=dst, send_sem=ssem, recv_sem=rsem,
                                        device_id=dev, device_id_type=MESH)


def _allgather8(v, name):
    R, Cc = v.shape

    def body(v_ref, out_ref, send_sems, recv_sems):
        x, y, c = _my_pos()
        me = 4 * x + 2 * y + c
        out_ref[pl.ds(me, 1)] = v_ref[...][None]
        peers = []
        for j in range(1, N_DEV):
            peers.append((1 - x if (j >> 2) & 1 else x, 1 - y if (j >> 1) & 1 else y, 1 - c if j & 1 else c))
        copies = []
        for j, peer in enumerate(peers):
            cp = _remote(v_ref, out_ref.at[me], send_sems.at[j], recv_sems.at[j], peer)
            cp.start()
            copies.append(cp)
        for j, peer in enumerate(peers):
            pid = 4 * peer[0] + 2 * peer[1] + peer[2]
            _remote(v_ref, out_ref.at[pid], send_sems.at[j], recv_sems.at[j], peer).wait_recv()
        for cp in copies:
            cp.wait_send()

    return pl.pallas_call(
        body, name=name, out_shape=jax.ShapeDtypeStruct((N_DEV, R, Cc), v.dtype),
        in_specs=[pl.BlockSpec(memory_space=pltpu.VMEM)], out_specs=pl.BlockSpec(memory_space=pltpu.VMEM),
        scratch_shapes=[pltpu.SemaphoreType.DMA((N_DEV - 1,)), pltpu.SemaphoreType.DMA((N_DEV - 1,))])(v)


def _region(ref, k, half, shard_shape, axis):
    r, cs = shard_shape
    hr = r // 2
    if axis == 1:
        return ref.at[pl.ds(pl.multiple_of(half * hr, 16), hr), pl.ds(pl.multiple_of(k * cs, LANES), cs)]
    return ref.at[pl.ds(pl.multiple_of(k * r + half * hr, 16), hr), :]


def _full_shape(shard_shape, axis):
    r, cs = shard_shape
    return (r, N_CHIPS * cs) if axis == 1 else (N_CHIPS * r, cs)


def _allgather_weights(fulls, shapes, axes):
    nw = len(fulls)

    def body(*refs):
        out_refs = refs[nw:2 * nw]
        send1, recv1, send2, recv2 = refs[2 * nw:]
        x, y, c = _my_pos()
        k0 = 2 * x + y
        chips = _other_chips(x, y)
        sends = []
        for w in range(nw):
            own = _region(out_refs[w], k0, c, shapes[w], axes[w])
            for j, ch in enumerate(chips):
                cp = _remote(own, own, send1.at[w, j], recv1.at[w, j], (ch[0], ch[1], c))
                cp.start()
                sends.append(cp)
        for w in range(nw):
            for j, ch in enumerate(chips):
                kj = 2 * ch[0] + ch[1]
                got = _region(out_refs[w], kj, c, shapes[w], axes[w])
                _remote(got, got, send1.at[w, j], recv1.at[w, j], (ch[0], ch[1], c)).wait_recv()
                fw = _remote(got, got, send2.at[w, j], recv2.at[w, j], (x, y, 1 - c))
                fw.start()
                sends.append(fw)
        for w in range(nw):
            for j, ch in enumerate(chips):
                kj = 2 * ch[0] + ch[1]
                got = _region(out_refs[w], kj, 1 - c, shapes[w], axes[w])
                _remote(got, got, send2.at[w, j], recv2.at[w, j], (x, y, 1 - c)).wait_recv()
        for cp in sends:
            cp.wait_send()

    anyspec = pl.BlockSpec(memory_space=pl.ANY)
    sem = lambda: pltpu.SemaphoreType.DMA((nw, 3))
    return pl.pallas_call(
        body, name="allgather_weights",
        out_shape=[jax.ShapeDtypeStruct(f.shape, BF16) for f in fulls],
        in_specs=[anyspec] * nw, out_specs=[anyspec] * nw,
        input_output_aliases={w: w for w in range(nw)},
        scratch_shapes=[sem(), sem(), sem(), sem()])(*fulls)


def _half_pieces(ref, half, shard_shape, axis):
    r, cs = shard_shape
    hr = r // 2
    if axis == 1:
        return [ref.at[pl.ds(pl.multiple_of(half * hr, 16), hr), :]]
    return [ref.at[pl.ds(pl.multiple_of(k * r + half * hr, 16), hr), :] for k in range(N_CHIPS)]


def _rs_sibling(grads, shapes, axes, name):
    nw = len(grads)
    npc = max(1 if a == 1 else N_CHIPS for a in axes)

    def body(*refs):
        g_refs, out_refs = refs[:nw], refs[nw:2 * nw]
        send, recv = refs[2 * nw:]
        x, y, c = _my_pos()
        sib = (x, y, 1 - c)
        copies = []
        for w in range(nw):
            src = _half_pieces(g_refs[w], 1 - c, shapes[w], axes[w])
            dst = _half_pieces(out_refs[w], 1 - c, shapes[w], axes[w])
            for i, (s, d) in enumerate(zip(src, dst)):
                cp = _remote(s, d, send.at[w, i], recv.at[w, i], sib)
                cp.start()
                copies.append(cp)
        for w in range(nw):
            mine = _half_pieces(out_refs[w], c, shapes[w], axes[w])
            for i, d in enumerate(mine):
                _remote(d, d, send.at[w, i], recv.at[w, i], sib).wait_recv()
        for cp in copies:
            cp.wait_send()

    anyspec = pl.BlockSpec(memory_space=pl.ANY)
    return pl.pallas_call(
        body, name=name,
        out_shape=[jax.ShapeDtypeStruct(_full_shape(s, a), BF16) for s, a in zip(shapes, axes)],
        in_specs=[anyspec] * nw, out_specs=[anyspec] * nw,
        scratch_shapes=[pltpu.SemaphoreType.DMA((nw, npc)), pltpu.SemaphoreType.DMA((nw, npc))])(*grads)


def _half_block_spec(shard_shape, axis, tr):
    r, cs = shard_shape
    hr = r // 2
    if axis == 1:
        return pl.BlockSpec((tr, cs), lambda k, i, c_ref: (c_ref[0] * (hr // tr) + i, k))
    return pl.BlockSpec((tr, cs), lambda k, i, c_ref: (k * (r // tr) + c_ref[0] * (hr // tr) + i, 0))


def _add_halves(g, recv, cvec, shard_shape, axis, name):
    r, cs = shard_shape
    hr = r // 2
    tr = _tile(hr, 256, 16)

    def body(c_ref, a_ref, b_ref, o_ref):
        del c_ref
        o_ref[0] = (a_ref[...].astype(F32) + b_ref[...].astype(F32)).astype(BF16)

    spec = _half_block_spec(shard_shape, axis, tr)
    return pl.pallas_call(
        body, name=name,
        grid_spec=pltpu.PrefetchScalarGridSpec(
            num_scalar_prefetch=1, grid=(N_CHIPS, hr // tr), in_specs=[spec, spec],
            out_specs=pl.BlockSpec((1, tr, cs), lambda k, i, c_ref: (k, i, 0))),
        out_shape=jax.ShapeDtypeStruct((N_CHIPS, hr, cs), BF16),
        compiler_params=_params(2, False))(cvec, g, recv)


def _rs_chips(sums, name):
    nw = len(sums)

    def body(*refs):
        s_refs, out_refs = refs[:nw], refs[nw:2 * nw]
        send, recv = refs[2 * nw:]
        x, y, c = _my_pos()
        k0 = 2 * x + y
        chips = _other_chips(x, y)
        copies = []
        for w in range(nw):
            for j, ch in enumerate(chips):
                kj = 2 * ch[0] + ch[1]
                cp = _remote(s_refs[w].at[kj], out_refs[w].at[k0], send.at[w, j], recv.at[w, j], (ch[0], ch[1], c))
                cp.start()
                copies.append(cp)
        for w in range(nw):
            for j, ch in enumerate(chips):
                kj = 2 * ch[0] + ch[1]
                d = out_refs[w].at[kj]
                _remote(d, d, send.at[w, j], recv.at[w, j], (ch[0], ch[1], c)).wait_recv()
        for cp in copies:
            cp.wait_send()

    anyspec = pl.BlockSpec(memory_space=pl.ANY)
    return pl.pallas_call(
        body, name=name,
        out_shape=[jax.ShapeDtypeStruct(s.shape, BF16) for s in sums],
        in_specs=[anyspec] * nw, out_specs=[anyspec] * nw,
        scratch_shapes=[pltpu.SemaphoreType.DMA((nw, 3)), pltpu.SemaphoreType.DMA((nw, 3))])(*sums)


def _sum_chips(sums, landed, kc, name):
    _, hr, cs = sums.shape
    tr = _tile(hr, 256, 16)

    def body(kc_ref, own_ref, a_ref, b_ref, c_ref, o_ref):
        del kc_ref
        o_ref[...] = (own_ref[0].astype(F32) + a_ref[0].astype(F32)) + (b_ref[0].astype(F32) + c_ref[0].astype(F32))

    slot = lambda j: pl.BlockSpec((1, tr, cs), lambda i, kc_ref: ((kc_ref[0] + j) % N_CHIPS, i, 0))
    return pl.pallas_call(
        body, name=name,
        grid_spec=pltpu.PrefetchScalarGridSpec(
            num_scalar_prefetch=1, grid=(hr // tr,), in_specs=[slot(0), slot(1), slot(2), slot(3)],
            out_specs=pl.BlockSpec((tr, cs), lambda i, kc_ref: (kc_ref[1] * (hr // tr) + i, 0))),
        out_shape=jax.ShapeDtypeStruct((2 * hr, cs), F32),
        compiler_params=_params(1, False))(kc, sums, landed, landed, landed)


def _exchange_halves(shards):
    nw = len(shards)

    def body(*refs):
        out_refs = refs[nw:2 * nw]
        send, recv = refs[2 * nw:]
        x, y, c = _my_pos()
        sib = (x, y, 1 - c)
        copies = []
        for w in range(nw):
            hr = shards[w].shape[0] // 2
            mine = out_refs[w].at[pl.ds(pl.multiple_of(c * hr, 8), hr), :]
            cp = _remote(mine, mine, send.at[w], recv.at[w], sib)
            cp.start()
            copies.append(cp)
        for w in range(nw):
            hr = shards[w].shape[0] // 2
            other = out_refs[w].at[pl.ds(pl.multiple_of((1 - c) * hr, 8), hr), :]
            _remote(other, other, send.at[w], recv.at[w], sib).wait_recv()
        for cp in copies:
            cp.wait_send()

    anyspec = pl.BlockSpec(memory_space=pl.ANY)
    return pl.pallas_call(
        body, name="exchange_halves",
        out_shape=[jax.ShapeDtypeStruct(s.shape, F32) for s in shards],
        in_specs=[anyspec] * nw, out_specs=[anyspec] * nw,
        input_output_aliases={w: w for w in range(nw)},
        scratch_shapes=[pltpu.SemaphoreType.DMA((nw,)), pltpu.SemaphoreType.DMA((nw,))])(*shards)


def _cast_into_full(w, kc, axis, name):
    r, cs = w.shape
    tr = _tile(r, 256, 16)

    def body(kc_ref, w_ref, o_ref):
        del kc_ref
        o_ref[...] = w_ref[...].astype(BF16)

    if axis == 1:
        ospec = pl.BlockSpec((tr, cs), lambda i, kc_ref: (i, kc_ref[0]))
    else:
        ospec = pl.BlockSpec((tr, cs), lambda i, kc_ref: (kc_ref[0] * (r // tr) + i, 0))
    return pl.pallas_call(
        body, name=name,
        grid_spec=pltpu.PrefetchScalarGridSpec(
            num_scalar_prefetch=1, grid=(r // tr,), in_specs=[pl.BlockSpec((tr, cs), lambda i, kc_ref: (i, 0))],
            out_specs=ospec),
        out_shape=jax.ShapeDtypeStruct(_full_shape((r, cs), axis), BF16),
        compiler_params=_params(1, False))(kc, w)


def _adam_math(w, g, m, v):
    m2 = ADAM_B1 * m + (1.0 - ADAM_B1) * g
    v2 = ADAM_B2 * v + (1.0 - ADAM_B2) * (g * g)
    m_hat = m2 / (1.0 - ADAM_B1 ** ADAM_STEP)
    v_hat = v2 / (1.0 - ADAM_B2 ** ADAM_STEP)
    delta = -ADAM_LR * (m_hat / (jnp.sqrt(v_hat) + ADAM_EPS) + ADAM_WD * w)
    return delta, m2, v2


def _adam(w, g, m, v, name):
    r, cs = w.shape
    tr = _tile(r, 256, 8)

    def body(w_ref, g_ref, m_ref, v_ref, d_ref, m2_ref, v2_ref):
        d, m2, v2 = _adam_math(w_ref[...], g_ref[...], m_ref[...], v_ref[...])
        d_ref[...] = d
        m2_ref[...] = m2
        v2_ref[...] = v2

    spec = pl.BlockSpec((tr, cs), lambda i: (i, 0))
    shp = jax.ShapeDtypeStruct((r, cs), F32)
    return pl.pallas_call(body, name=name, grid=(r // tr,), in_specs=[spec] * 4, out_specs=[spec] * 3,
                          out_shape=[shp, shp, shp], compiler_params=_params(1, False))(w, g, m, v)


def _mod_rows(a16, w, b, name):
    D, n = w.shape
    tn = _tile(n, 512)

    def body(a_ref, w_ref, b_ref, o_ref):
        a = a_ref[...]
        o_ref[...] = _nn((a * _sigmoid(a)).astype(BF16), w_ref[...].astype(BF16)) + b_ref[...]

    return pl.pallas_call(
        body, name=name, grid=(n // tn,),
        in_specs=[pl.BlockSpec((16, D), lambda j: (0, 0)), pl.BlockSpec((D, tn), lambda j: (0, j)),
                  pl.BlockSpec((1, tn), lambda j: (0, j))],
        out_specs=pl.BlockSpec((16, tn), lambda j: (0, j)),
        out_shape=jax.ShapeDtypeStruct((16, n), F32), compiler_params=_params(1, False))(a16, w, b)


def _w_mod_update(a16, d16, w, m, v):
    D, n = w.shape
    tn = _tile(n, 256)

    def body(a_ref, d_ref, w_ref, m_ref, v_ref, g_ref, dl_ref, m2_ref, v2_ref, p_ref):
        @pl.when(pl.program_id(0) == 0)
        def _():
            p_ref[...] = jnp.zeros_like(p_ref)
        a = a_ref[...]
        db = d_ref[...].astype(BF16)
        wv = w_ref[...]
        g = _tn((a * _sigmoid(a)).astype(BF16), db)
        g_ref[...] = g
        d, m2, v2 = _adam_math(wv, g, m_ref[...], v_ref[...])
        dl_ref[...] = d
        m2_ref[...] = m2
        v2_ref[...] = v2
        p_ref[...] += _nt(db, wv.astype(BF16))

    wspec = pl.BlockSpec((D, tn), lambda j: (0, j))
    shp = jax.ShapeDtypeStruct((D, n), F32)
    return pl.pallas_call(
        body, name="w_mod_update", grid=(n // tn,),
        in_specs=[pl.BlockSpec((16, D), lambda j: (0, 0)), pl.BlockSpec((16, tn), lambda j: (0, j)), wspec, wspec, wspec],
        out_specs=[wspec, wspec, wspec, wspec, pl.BlockSpec((16, D), lambda j: (0, 0))],
        out_shape=[shp, shp, shp, shp, jax.ShapeDtypeStruct((16, D), F32)],
        compiler_params=_params(1))(a16, d16, w, m, v)


def _sum_devices(g8, name):
    _, R, Cc = g8.shape

    def body(g_ref, o_ref):
        t = g_ref[0]
        for d in range(1, N_DEV):
            t = t + g_ref[d]
        o_ref[...] = t

    return pl.pallas_call(body, name=name, out_shape=jax.ShapeDtypeStruct((R, Cc), F32))(g8)


def _c_ctx_grad(parts, c_ctx):
    D = c_ctx.shape[1]

    def body(p_ref, c_ref, o_ref):
        t = p_ref[0]
        for k in range(1, N_CHIPS):
            t = t + p_ref[2 * k]
        cv = c_ref[...]
        sg = _sigmoid(cv)
        o_ref[...] = t * (sg * (1.0 + cv * (1.0 - sg)))

    return pl.pallas_call(body, name="c_ctx_grad", out_shape=jax.ShapeDtypeStruct((1, D), F32))(parts, c_ctx)


def _pack_rows(items, nrows, width, name):
    arrays, plan = [], []
    for a, r0, nr, d0, c0 in items:
        for ai, b in enumerate(arrays):
            if b is a:
                break
        else:
            ai = len(arrays)
            arrays.append(a)
        plan.append((ai, r0, nr, d0, c0, a.shape[1]))

    def body(*refs):
        o_ref = refs[-1]
        o_ref[...] = jnp.zeros_like(o_ref)
        for ai, r0, nr, d0, c0, w in plan:
            o_ref[d0:d0 + nr, c0:c0 + w] = refs[ai][r0:r0 + nr, :]

    return pl.pallas_call(body, name=name, out_shape=jax.ShapeDtypeStruct((nrows, width), F32))(*arrays)


HBM_SPEC = pl.BlockSpec(memory_space=pltpu.HBM)
SEM_SPEC = pl.BlockSpec(memory_space=pltpu.SEMAPHORE)
SPLIT_PARAMS = pltpu.CompilerParams(has_side_effects=pltpu.SideEffectType.DATAFLOW_SIDE_EFFECTING)


def _in_hbm(a):
    return pltpu.with_memory_space_constraint(a, pltpu.HBM)


def _ag_chips_start(fulls, shapes, axes, after):
    nw = len(fulls)

    def body(*refs):
        in_refs, send, recv, token = refs[:nw], refs[nw + 1], refs[nw + 2], refs[-1]
        x, y, c = _my_pos()
        k0 = 2 * x + y
        for w in range(nw):
            own = _region(in_refs[w], k0, c, shapes[w], axes[w])
            for j, ch in enumerate(_other_chips(x, y)):
                _remote(own, own, send.at[3 * w + j], recv.at[3 * w + j], (ch[0], ch[1], c)).start()
        token[...] = jnp.zeros_like(token)

    return pl.pallas_call(
        body, name="ag_chips_start",
        out_shape=(pltpu.SemaphoreType.DMA((3 * nw,)), pltpu.SemaphoreType.DMA((3 * nw,)),
                   *[pltpu.HBM(f.shape, f.dtype) for f in fulls], jax.ShapeDtypeStruct((8, LANES), F32)),
        in_specs=[HBM_SPEC] * nw + [pl.BlockSpec(memory_space=pl.ANY)],
        out_specs=(SEM_SPEC, SEM_SPEC, *[HBM_SPEC] * nw, pl.BlockSpec(memory_space=pltpu.VMEM)),
        input_output_aliases={w: 2 + w for w in range(nw)},
        compiler_params=SPLIT_PARAMS)(*[_in_hbm(f) for f in fulls], after)


def _ag_chips_wait(send, recv, fulls, shapes, axes, after):
    nw = len(fulls)

    def body(*refs):
        in_refs, send_ref, recv_ref = refs[:nw], refs[nw], refs[nw + 1]
        x, y, c = _my_pos()
        k0 = 2 * x + y
        for w in range(nw):
            own = _region(in_refs[w], k0, c, shapes[w], axes[w])
            for j, ch in enumerate(_other_chips(x, y)):
                got = _region(in_refs[w], 2 * ch[0] + ch[1], c, shapes[w], axes[w])
                cp = _remote(own, got, send_ref.at[3 * w + j], recv_ref.at[3 * w + j], (ch[0], ch[1], c))
                cp.wait_send()
                cp.wait_recv()

    return pl.pallas_call(
        body, name="ag_chips_wait",
        out_shape=tuple(pltpu.HBM(f.shape, f.dtype) for f in fulls),
        in_specs=[HBM_SPEC] * nw + [SEM_SPEC, SEM_SPEC, pl.BlockSpec(memory_space=pl.ANY)],
        out_specs=tuple([HBM_SPEC] * nw),
        input_output_aliases={w: w for w in range(nw)},
        compiler_params=SPLIT_PARAMS)(*fulls, send, recv, after)


def _ag_forward(fulls, shapes, axes):
    nw = len(fulls)

    def body(*refs):
        out_refs = refs[nw:2 * nw]
        send, recv = refs[2 * nw:]
        x, y, c = _my_pos()
        sib = (x, y, 1 - c)
        chips = _other_chips(x, y)
        copies = []
        for w in range(nw):
            for j, ch in enumerate(chips):
                got = _region(out_refs[w], 2 * ch[0] + ch[1], c, shapes[w], axes[w])
                cp = _remote(got, got, send.at[w, j], recv.at[w, j], sib)
                cp.start()
                copies.append(cp)
        for w in range(nw):
            for j, ch in enumerate(chips):
                got = _region(out_refs[w], 2 * ch[0] + ch[1], 1 - c, shapes[w], axes[w])
                _remote(got, got, send.at[w, j], recv.at[w, j], sib).wait_recv()
        for cp in copies:
            cp.wait_send()

    anyspec = pl.BlockSpec(memory_space=pl.ANY)
    return pl.pallas_call(
        body, name="ag_forward",
        out_shape=[jax.ShapeDtypeStruct(f.shape, BF16) for f in fulls],
        in_specs=[anyspec] * nw, out_specs=[anyspec] * nw,
        input_output_aliases={w: w for w in range(nw)},
        scratch_shapes=[pltpu.SemaphoreType.DMA((nw, 3)), pltpu.SemaphoreType.DMA((nw, 3))])(*fulls)


def _rs_chips_start(sums, name):
    nw = len(sums)

    def body(*refs):
        s_refs, l_refs, send, recv, token = refs[:nw], refs[nw:2 * nw], refs[2 * nw], refs[2 * nw + 1], refs[-1]
        x, y, c = _my_pos()
        k0 = 2 * x + y
        for w in range(nw):
            for j, ch in enumerate(_other_chips(x, y)):
                _remote(s_refs[w].at[2 * ch[0] + ch[1]], l_refs[w].at[k0], send.at[3 * w + j], recv.at[3 * w + j],
                        (ch[0], ch[1], c)).start()
        token[...] = jnp.zeros_like(token)

    thru = [pltpu.HBM(s.shape, s.dtype) for s in sums]
    return pl.pallas_call(
        body, name=name,
        out_shape=(pltpu.SemaphoreType.DMA((3 * nw,)), pltpu.SemaphoreType.DMA((3 * nw,)), *thru, *thru,
                   jax.ShapeDtypeStruct((8, LANES), F32)),
        in_specs=[HBM_SPEC] * (2 * nw),
        out_specs=(SEM_SPEC, SEM_SPEC, *[HBM_SPEC] * (2 * nw), pl.BlockSpec(memory_space=pltpu.VMEM)),
        input_output_aliases={i: 2 + i for i in range(2 * nw)},
        compiler_params=SPLIT_PARAMS)(*[_in_hbm(s) for s in sums], *[_in_hbm(lax.empty(s.shape, s.dtype)) for s in sums])


def _rs_chips_wait(send, recv, sums, lands, after, name):
    nw = len(sums)

    def body(*refs):
        s_refs, l_refs, send_ref, recv_ref = refs[:nw], refs[nw:2 * nw], refs[2 * nw], refs[2 * nw + 1]
        x, y, c = _my_pos()
        for w in range(nw):
            for j, ch in enumerate(_other_chips(x, y)):
                kj = 2 * ch[0] + ch[1]
                cp = _remote(s_refs[w].at[kj], l_refs[w].at[kj], send_ref.at[3 * w + j], recv_ref.at[3 * w + j],
                             (ch[0], ch[1], c))
                cp.wait_send()
                cp.wait_recv()

    thru = tuple(pltpu.HBM(s.shape, s.dtype) for s in sums)
    return pl.pallas_call(
        body, name=name, out_shape=thru + thru,
        in_specs=[HBM_SPEC] * (2 * nw) + [SEM_SPEC, SEM_SPEC, pl.BlockSpec(memory_space=pl.ANY)],
        out_specs=tuple([HBM_SPEC] * (2 * nw)),
        input_output_aliases={i: i for i in range(2 * nw)},
        compiler_params=SPLIT_PARAMS)(*sums, *lands, send, recv, after)


LOSS_LANE = 64


def kernel(x, c, ctx, c_ctx, w_mod, b_mod, norm_mix, norm_ffn, w_in, ret_decay, attn_sink, w_out, w_gate, w_up, w_down, norm_final, loss_target, m_c_ctx, m_w_mod, m_b_mod, m_norm_mix, m_norm_ffn, m_w_in, m_ret_decay, m_attn_sink, m_w_out, m_w_gate, m_w_up, m_w_down, m_norm_final, v_c_ctx, v_w_mod, v_b_mod, v_norm_mix, v_norm_ffn, v_w_in, v_ret_decay, v_attn_sink, v_w_out, v_w_gate, v_w_up, v_w_down, v_norm_final):
    D = x.shape[-1]
    n3 = w_mod.shape[-1]
    xi, yi, ci = _my_pos()
    b = 4 * xi + 2 * yi + ci
    k0 = 2 * xi + yi
    cvec = jnp.reshape(ci, (1,)).astype(jnp.int32)
    kc = jnp.stack([k0, ci]).astype(jnp.int32)

    dense = [("w_in", w_in[0], 1), ("w_out", w_out[0], 0), ("w_gate", w_gate[0], 1), ("w_up", w_up[0], 1),
             ("w_down", w_down[0], 0)]
    axes = [a for _, _, a in dense]
    shapes = [w.shape for _, w, _ in dense]
    own16 = [_cast_into_full(w, kc, a, "cast_" + n) for n, w, a in dense]
    (f_in,) = _allgather_weights(own16[:1], shapes[:1], axes[:1])
    ag = _ag_chips_start(own16[1:], shapes[1:], axes[1:], f_in)
    ag_send, ag_recv, ag_thru, ag_tok = ag[0], ag[1], list(ag[2:-1]), ag[-1][0:1, 0:1]

    def rest_weights(after):
        landed_w = _ag_chips_wait(ag_send, ag_recv, ag_thru, shapes[1:], axes[1:], after)
        return _ag_forward(list(landed_w), shapes[1:], axes[1:])

    c_all = _allgather8(c, "gather_c").reshape(N_DEV, D)
    c_ctx2 = c_ctx.reshape(1, D)
    a16 = _pack_rows([(c_all, 0, N_DEV, 0, 0), (c_ctx2, 0, 1, N_DEV, 0)], 16, D, "pack_cond")
    b_cols = lax.dynamic_slice_in_dim(b_mod, k0 * n3, n3, axis=1)
    mod16 = _mod_rows(a16, w_mod[0], b_cols, "mod_rows")
    mod_all = _allgather8(mod16, "gather_mod")
    mine = jnp.stack([lax.dynamic_index_in_dim(mod_all, 2 * k + ci, 0, keepdims=False) for k in range(N_CHIPS)])
    mod = lax.dynamic_index_in_dim(mine, b, 1, keepdims=False).reshape(6, D)
    modc = mine[:, N_DEV].reshape(6, D)

    lg = -jnp.exp(ret_decay[0])

    index = {n: i for i, (n, _, _) in enumerate(dense)}
    pending, done = [], {}

    def on_grads(names, gs):
        ids = [index[n] for n in names]
        shp, axs = [shapes[i] for i in ids], [axes[i] for i in ids]
        from_sib = _rs_sibling(gs, shp, axs, "rs_sibling_" + names[0])
        sums = [_add_halves(g, r, cvec, s, a, "add_halves_" + n) for g, r, s, a, n in zip(gs, from_sib, shp, axs, names)]
        if names == ["w_in"]:
            for n, s, l in zip(names, sums, _rs_chips(sums, "rs_chips_" + names[0])):
                done[n] = (s, l)
            return None
        st = _rs_chips_start(sums, "rs_chips_start_" + names[0])
        nw = len(names)
        pending.append((names, st[0], st[1], list(st[2:2 + nw]), list(st[2 + nw:2 + 2 * nw])))
        return st[-1][0:1, 0:1]

    out = _local_step(x[0], ctx[0], loss_target[0], mod, modc, norm_mix + ag_tok, norm_ffn, norm_final.reshape(1, D), lg,
                      attn_sink, f_in, rest_weights, on_grads)

    for names, send, recv, sums, lands in pending:
        res = _rs_chips_wait(send, recv, sums, lands, out["grad_x"], "rs_chips_wait_" + names[0])
        for i, n in enumerate(names):
            done[n] = (res[i], res[len(names) + i])
    halves = [_sum_chips(done[n][0], done[n][1], kc, "sum_chips_" + n) for n, _, _ in dense]
    g_dense = _exchange_halves(halves)

    nh = 2 * RET_HEADS
    small_all = _allgather8(out["small"], "gather_small")
    tot = _sum_devices(small_all, "sum_small")
    g_b_mod = (tot[0:6] + tot[6:12]).reshape(1, 6 * D)
    dmodc_tot = tot[6:12].reshape(1, 6 * D)
    dmod_rows = small_all[:, 0:6].reshape(N_DEV, 6 * D)
    d16 = _pack_rows([(dmod_rows, 0, N_DEV, 0, 0), (dmodc_tot, 0, 1, N_DEV, 0)], 16, 6 * D, "pack_dmod")
    d16 = lax.dynamic_slice_in_dim(d16, k0 * n3, n3, axis=1)
    g_w_mod, dl_w_mod, m2_w_mod, v2_w_mod, part = _w_mod_update(a16, d16, w_mod[0], m_w_mod[0], v_w_mod[0])
    part_all = _allgather8(part[N_DEV:N_DEV + 1], "gather_c_ctx")
    g_c_ctx = _c_ctx_grad(part_all, c_ctx2)
    loss = tot[15, LOSS_LANE]
    g_ret_decay = tot[15, :nh].reshape(1, 2, RET_HEADS)
    g_sink = tot[15, nh:nh + ATT_HEADS].reshape(1, ATT_HEADS)

    def pack(cc, bm, nm, nf, nfin, rd, sk, name):
        rd2 = rd.reshape(2, RET_HEADS)
        return _pack_rows([(bm.reshape(6, D), 0, 6, 0, 0), (cc.reshape(1, D), 0, 1, 6, 0), (nm.reshape(1, D), 0, 1, 7, 0),
                           (nf.reshape(1, D), 0, 1, 8, 0), (nfin.reshape(1, D), 0, 1, 9, 0),
                           (rd2, 0, 1, 10, 0), (rd2, 1, 1, 10, RET_HEADS), (sk.reshape(1, ATT_HEADS), 0, 1, 10, nh)],
                          16, D, name)

    w_s = pack(c_ctx, b_mod, norm_mix, norm_ffn, norm_final, ret_decay, attn_sink, "pack_w")
    g_s = _pack_rows([(g_b_mod.reshape(6, D), 0, 6, 0, 0), (g_c_ctx, 0, 1, 6, 0), (tot, 12, 3, 7, 0),
                      (tot[15:16, 0:nh + ATT_HEADS], 0, 1, 10, 0)], 16, D, "pack_g")
    m_s = pack(m_c_ctx, m_b_mod, m_norm_mix, m_norm_ffn, m_norm_final, m_ret_decay, m_attn_sink, "pack_m")
    v_s = pack(v_c_ctx, v_b_mod, v_norm_mix, v_norm_ffn, v_norm_final, v_ret_decay, v_attn_sink, "pack_v")
    small_upd = _adam(w_s, g_s, m_s, v_s, "adam_small")

    def unpack(t):
        return dict(b_mod=t[0:6].reshape(1, 6 * D), c_ctx=t[6], norm_mix=t[7:8], norm_ffn=t[8:9], norm_final=t[9],
                    ret_decay=t[10, :nh].reshape(1, 2, RET_HEADS), attn_sink=t[10, nh:nh + ATT_HEADS].reshape(1, ATT_HEADS))

    dense_w = dict(w_in=(w_in, m_w_in, v_w_in), w_out=(w_out, m_w_out, v_w_out), w_gate=(w_gate, m_w_gate, v_w_gate),
                   w_up=(w_up, m_w_up, v_w_up), w_down=(w_down, m_w_down, v_w_down))
    grads = dict(unpack(g_s), w_mod=g_w_mod[None])
    upd = [dict(unpack(t)) for t in small_upd]
    upd[0]["w_mod"], upd[1]["w_mod"], upd[2]["w_mod"] = dl_w_mod[None], m2_w_mod[None], v2_w_mod[None]
    for (n, _, _), g in zip(dense, g_dense):
        w_, m_, v_ = dense_w[n]
        res = _adam(w_[0], g, m_[0], v_[0], "adam_" + n)
        grads[n] = g[None]
        for u, r_ in zip(upd, res):
            u[n] = r_[None]

    order = ['c_ctx', 'w_mod', 'b_mod', 'norm_mix', 'norm_ffn', 'w_in', 'ret_decay', 'attn_sink', 'w_out', 'w_gate',
             'w_up', 'w_down', 'norm_final']
    outs = [loss, out["grad_x"][None]] + [grads[n] for n in order]
    for u in upd:
        outs += [u[n] for n in order]
    return tuple(outs)
```

```python
import functools
import numpy as np
import jax
import jax.numpy as jnp
from jax import lax
from jax.experimental import pallas as pl
from jax.experimental.pallas import tpu as pltpu

F32 = jnp.float32
BF16 = jnp.bfloat16

RET_HEADS = 8
RET_DK = 64
RET_DV = 128
CHUNK = 128
ATT_HEADS = 16
ATT_KV_HEADS = 4
ATT_DH = 64
GRID_W = 64
ROPE_BASE = 10000.0
NORM_EPS = 1e-6
ADAM_LR = 0.001
ADAM_B1 = 0.9
ADAM_B2 = 0.999
ADAM_EPS = 1e-08
ADAM_WD = 0.01
ADAM_STEP = 10
NEG = -1e30
LANES = 128
VMEM_LIMIT = 56 * 1024 * 1024
ROWS_PER_LATCH = 1024
MESH = pl.DeviceIdType.MESH
N_CHIPS = 4
N_DEV = 8


def _nn(a, b):
    return jnp.dot(a, b, preferred_element_type=F32)


def _nt(a, b):
    return lax.dot_general(a, b, (((1,), (1,)), ((), ())), preferred_element_type=F32)


def _tn(a, b):
    return lax.dot_general(a, b, (((0,), (0,)), ((), ())), preferred_element_type=F32)


def _tile(n, pref, unit=LANES):
    t = min(n, pref)
    t -= t % unit
    while t > unit and n % t:
        t -= unit
    if t <= 0 or n % t:
        return n
    return t


def _params(ndim, vmem=True):
    return pltpu.CompilerParams(dimension_semantics=("arbitrary",) * ndim,
                                vmem_limit_bytes=VMEM_LIMIT if vmem else None)


def _sigmoid(x):
    return 0.5 * jnp.tanh(0.5 * x) + 0.5


def _fsum(x):
    return jnp.sum(jnp.sum(x, axis=1, keepdims=True), axis=0, keepdims=True)


def _rope_tables(L):
    lane = np.arange(LANES)
    d = lane % 64
    inv_r = jnp.asarray(ROPE_BASE, F32) ** (-jnp.arange(32, dtype=F32) / 32)
    t = jnp.arange(L)
    ang_r = t.astype(F32)[:, None] * jnp.tile(inv_r, LANES // 32)[None, :]
    Rr = np.zeros((LANES, LANES), np.float32)
    for l in range(LANES):
        if d[l] < 32:
            Rr[l + 32, l] = -1.0
        else:
            Rr[l - 32, l] = 1.0
    inv_a = jnp.asarray(ROPE_BASE, F32) ** (-jnp.arange(16, dtype=F32) / 16)
    rows = (t // GRID_W).astype(F32)
    cols = (t % GRID_W).astype(F32)
    dd = d % 32
    pos = jnp.where(jnp.asarray(d < 32)[None, :], rows[:, None], cols[:, None])
    ang_a = pos * jnp.tile(inv_a, LANES // 16)[None, :]
    Ra = np.zeros((LANES, LANES), np.float32)
    for l in range(LANES):
        if dd[l] < 16:
            Ra[l + 16, l] = -1.0
        else:
            Ra[l - 16, l] = 1.0
    D0 = np.zeros((LANES, LANES), np.float32)
    D1 = np.zeros((LANES, LANES), np.float32)
    for l in range(LANES):
        D0[l % 64, l] = 1.0
        D1[64 + l % 64, l] = 1.0
    return dict(
        Cr=jnp.cos(ang_r), Sr=jnp.sin(ang_r), Rr=jnp.asarray(Rr, BF16), RrT=jnp.asarray(Rr.T, BF16),
        Ca=jnp.cos(ang_a), Sa=jnp.sin(ang_a), Ra=jnp.asarray(Ra, BF16), RaT=jnp.asarray(Ra.T, BF16),
        D0=jnp.asarray(D0, BF16), D1=jnp.asarray(D1, BF16),
        D0T=jnp.asarray(D0.T, BF16), D1T=jnp.asarray(D1.T, BF16))


def _norm_mod(xf, g, sh, sc):
    r = lax.rsqrt(jnp.mean(xf * xf, axis=-1, keepdims=True) + NORM_EPS)
    return (xf * r * g) * (1.0 + sc) + sh


def _norm_mod_matmul(x, g, sh, sc, w, name):
    M, D = x.shape
    N = w.shape[1]
    tm, tn = _tile(M, ROWS_PER_LATCH, 8), _tile(N, 768)

    def body(x_ref, g_ref, sh_ref, sc_ref, w_ref, p_ref, h_ref, hs):
        @pl.when(pl.program_id(1) == 0)
        def _():
            hb = _norm_mod(x_ref[...], g_ref[...], sh_ref[...], sc_ref[...]).astype(BF16)
            hs[...] = hb
            h_ref[...] = hb
        p_ref[...] = _nn(hs[...], w_ref[...]).astype(BF16)

    vec = pl.BlockSpec((1, D), lambda i, j: (0, 0))
    return pl.pallas_call(
        body, name=name, grid=(M // tm, N // tn),
        in_specs=[pl.BlockSpec((tm, D), lambda i, j: (i, 0)), vec, vec, vec,
                  pl.BlockSpec((D, tn), lambda i, j: (0, j))],
        out_specs=[pl.BlockSpec((tm, tn), lambda i, j: (i, j)), pl.BlockSpec((tm, D), lambda i, j: (i, 0))],
        out_shape=[jax.ShapeDtypeStruct((M, N), BF16), jax.ShapeDtypeStruct((M, D), BF16)],
        scratch_shapes=[pltpu.VMEM((tm, D), BF16)],
        compiler_params=_params(2))(x, g, sh, sc, w)


def _proj_residual(a, w, xres, gt, name):
    M, K = a.shape
    N = w.shape[1]
    tm, tn = _tile(M, ROWS_PER_LATCH, 8), _tile(N, 1024 if K <= 2048 else 512)

    def body(a_ref, w_ref, x_ref, gt_ref, xo_ref, o_ref):
        o = _nn(a_ref[...], w_ref[...])
        o_ref[...] = o.astype(BF16)
        xo_ref[...] = x_ref[...] + gt_ref[...] * o

    return pl.pallas_call(
        body, name=name, grid=(M // tm, N // tn),
        in_specs=[pl.BlockSpec((tm, K), lambda i, j: (i, 0)), pl.BlockSpec((K, tn), lambda i, j: (0, j)),
                  pl.BlockSpec((tm, tn), lambda i, j: (i, j)), pl.BlockSpec((1, tn), lambda i, j: (0, j))],
        out_specs=[pl.BlockSpec((tm, tn), lambda i, j: (i, j)), pl.BlockSpec((tm, tn), lambda i, j: (i, j))],
        out_shape=[jax.ShapeDtypeStruct((M, N), F32), jax.ShapeDtypeStruct((M, N), BF16)],
        compiler_params=_params(2))(a, w, xres, gt)


def _ffn_in(x1, g, sh, sc, wg, wu):
    M, D = x1.shape
    N = wg.shape[1]
    tm, tn = _tile(M, ROWS_PER_LATCH, 8), _tile(N, 512)

    def body(x_ref, g_ref, sh_ref, sc_ref, wg_ref, wu_ref, G_ref, U_ref, A_ref, h_ref, hs):
        @pl.when(pl.program_id(1) == 0)
        def _():
            hb = _norm_mod(x_ref[...], g_ref[...], sh_ref[...], sc_ref[...]).astype(BF16)
            hs[...] = hb
            h_ref[...] = hb
        G = _nn(hs[...], wg_ref[...])
        U = _nn(hs[...], wu_ref[...])
        G_ref[...] = G.astype(BF16)
        U_ref[...] = U.astype(BF16)
        A_ref[...] = (G * _sigmoid(G) * U).astype(BF16)

    vec = pl.BlockSpec((1, D), lambda i, j: (0, 0))
    wspec = pl.BlockSpec((D, tn), lambda i, j: (0, j))
    ospec = pl.BlockSpec((tm, tn), lambda i, j: (i, j))
    big = jax.ShapeDtypeStruct((M, N), BF16)
    return pl.pallas_call(
        body, name="ffn_in", grid=(M // tm, N // tn),
        in_specs=[pl.BlockSpec((tm, D), lambda i, j: (i, 0)), vec, vec, vec, wspec, wspec],
        out_specs=[ospec, ospec, ospec, pl.BlockSpec((tm, D), lambda i, j: (i, 0))],
        out_shape=[big, big, big, jax.ShapeDtypeStruct((M, D), BF16)],
        scratch_shapes=[pltpu.VMEM((tm, D), BF16)],
        compiler_params=_params(2))(x1, g, sh, sc, wg, wu)


def _final(x2, gn, tgt):
    M, D = x2.shape
    tm = _tile(M, 256, 8)

    def body(x_ref, g_ref, t_ref, dx_ref, loss_ref, dg_ref):
        @pl.when(pl.program_id(0) == 0)
        def _():
            loss_ref[...] = jnp.zeros_like(loss_ref)
            dg_ref[...] = jnp.zeros_like(dg_ref)
        x = x_ref[...]
        g = g_ref[...]
        r = lax.rsqrt(jnp.mean(x * x, axis=-1, keepdims=True) + NORM_EPS)
        xh = x * r
        e = xh * g - t_ref[...]
        loss_ref[...] += (0.5 / D) * _fsum(e * e)
        dy = e * (1.0 / D)
        dg_ref[...] += jnp.sum(dy * xh, axis=0, keepdims=True)
        dxh = dy * g
        dx_ref[...] = r * (dxh - xh * jnp.mean(dxh * xh, axis=-1, keepdims=True))

    row = pl.BlockSpec((tm, D), lambda i: (i, 0))
    return pl.pallas_call(
        body, name="final_loss", grid=(M // tm,),
        in_specs=[row, pl.BlockSpec((1, D), lambda i: (0, 0)), row],
        out_specs=[row, pl.BlockSpec((1, LANES), lambda i: (0, 0)), pl.BlockSpec((1, D), lambda i: (0, 0))],
        out_shape=[jax.ShapeDtypeStruct((M, D), F32), jax.ShapeDtypeStruct((1, LANES), F32),
                   jax.ShapeDtypeStruct((1, D), F32)],
        compiler_params=_params(1))(x2, gn, tgt)


def _col_group(blk0, nblk):
    return int(np.gcd(blk0, nblk)) if blk0 else nblk


def _rope_cols(src, blk0, nblk, Ct, St, R, scale, rope, name):
    M = src.shape[0]
    tm = _tile(M, 512, 8)
    wb = _col_group(blk0, nblk)

    def body(x_ref, c_ref, s_ref, r_ref, o_ref):
        for j in range(wb):
            cols = slice(j * LANES, (j + 1) * LANES)
            x = x_ref[:, cols]
            xf = x.astype(F32)
            if rope:
                xf = xf * c_ref[...] + _nn(x.astype(BF16), r_ref[...]) * s_ref[...]
            o_ref[:, cols] = (xf * scale).astype(BF16)

    tab = pl.BlockSpec((tm, LANES), lambda i, j: (i, 0))
    return pl.pallas_call(
        body, name=name, grid=(M // tm, nblk // wb),
        in_specs=[pl.BlockSpec((tm, wb * LANES), lambda i, j: (i, blk0 // wb + j)), tab, tab,
                  pl.BlockSpec((LANES, LANES), lambda i, j: (0, 0))],
        out_specs=pl.BlockSpec((tm, wb * LANES), lambda i, j: (i, j)),
        out_shape=jax.ShapeDtypeStruct((M, nblk * LANES), BF16),
        compiler_params=_params(2, False))(src, Ct, St, R)


def _dup_heads(src, blk0, npair, Ct, St, R, D0, D1, rope, name):
    M = src.shape[0]
    tm = _tile(M, 512, 8)

    def body(x_ref, c_ref, s_ref, r_ref, d0_ref, d1_ref, o_ref):
        x = x_ref[...]
        if rope:
            x = (x.astype(F32) * c_ref[...] + _nn(x, r_ref[...]) * s_ref[...]).astype(BF16)
        o_ref[0] = _nn(x, d0_ref[...]).astype(BF16)
        o_ref[1] = _nn(x, d1_ref[...]).astype(BF16)

    tab = pl.BlockSpec((tm, LANES), lambda i, p: (i, 0))
    mat = pl.BlockSpec((LANES, LANES), lambda i, p: (0, 0))
    return pl.pallas_call(
        body, name=name, grid=(M // tm, npair),
        in_specs=[pl.BlockSpec((tm, LANES), lambda i, p: (i, blk0 + p)), tab, tab, mat, mat, mat],
        out_specs=pl.BlockSpec((2, tm, LANES), lambda i, p: (p, i, 0)),
        out_shape=jax.ShapeDtypeStruct((2 * npair, M, LANES), BF16),
        compiler_params=_params(2, False))(src, Ct, St, R, D0, D1)


def _unrope_cols(dsrc, dst, blk0, nblk, Ct, St, RT, scale, rope, name):
    M = dsrc.shape[0]
    tm = _tile(M, 512, 8)
    wb = _col_group(blk0, nblk)

    def body(x_ref, c_ref, s_ref, r_ref, dst_ref, o_ref):
        del dst_ref
        for j in range(wb):
            cols = slice(j * LANES, (j + 1) * LANES)
            xf = x_ref[:, cols].astype(F32)
            if rope:
                xf = xf * c_ref[...] + _nn((xf * s_ref[...]).astype(BF16), r_ref[...])
            o_ref[:, cols] = (xf * scale).astype(BF16)

    tab = pl.BlockSpec((tm, LANES), lambda i, j: (i, 0))
    return pl.pallas_call(
        body, name=name, grid=(M // tm, nblk // wb),
        in_specs=[pl.BlockSpec((tm, wb * LANES), lambda i, j: (i, j)), tab, tab,
                  pl.BlockSpec((LANES, LANES), lambda i, j: (0, 0)),
                  pl.BlockSpec(memory_space=pl.ANY)],
        out_specs=pl.BlockSpec((tm, wb * LANES), lambda i, j: (i, blk0 // wb + j)),
        out_shape=jax.ShapeDtypeStruct(dst.shape, dst.dtype),
        input_output_aliases={4: 0},
        compiler_params=_params(2, False))(dsrc, Ct, St, RT, dst)


def _fold_heads(parts, dst, blk0, npair, Ct, St, RT, D0T, D1T, rope, name):
    M = parts[0][0].shape[1]
    nb = M // CHUNK
    R = _tile(M, 1024, CHUNK)
    rb = R // CHUNK
    nrefs = sum(1 if s == 0 else 2 for _, s in parts)

    def body(*refs):
        part_refs = list(refs[:nrefs])
        c_ref, s_ref, r_ref, d0_ref, d1_ref, dst_ref, o_ref = refs[nrefs:]
        del dst_ref
        i = pl.program_id(0)
        tot = [jnp.zeros((R, LANES), F32), jnp.zeros((R, LANES), F32)]
        for _, shift in parts:
            main = part_refs.pop(0)
            if shift == 0:
                for e in range(2):
                    tot[e] = tot[e] + main[e].astype(F32)
                continue
            edge = part_refs.pop(0)
            ok = (i + 1) * rb <= nb - 1 if shift > 0 else i > 0
            for e in range(2):
                ed = jnp.where(ok, edge[e].astype(F32), 0.0)
                if rb == 1:
                    tot[e] = tot[e] + ed
                elif shift > 0:
                    tot[e] = tot[e] + jnp.concatenate([main[e, CHUNK:, :].astype(F32), ed], axis=0)
                else:
                    tot[e] = tot[e] + jnp.concatenate([ed, main[e, :R - CHUNK, :].astype(F32)], axis=0)
        f = _nn(tot[0].astype(BF16), d0_ref[...]) + _nn(tot[1].astype(BF16), d1_ref[...])
        if rope:
            f = f * c_ref[...] + _nn((f * s_ref[...]).astype(BF16), r_ref[...])
        o_ref[...] = f.astype(BF16)

    in_specs, args = [], []
    for a, shift in parts:
        assert shift in (-1, 0, 1)
        in_specs.append(pl.BlockSpec((2, R, LANES), lambda i, p: (p, i, 0)))
        args.append(a)
        if shift > 0:
            in_specs.append(pl.BlockSpec((2, CHUNK, LANES), lambda i, p: (p, jnp.minimum((i + 1) * rb, nb - 1), 0)))
            args.append(a)
        elif shift < 0:
            in_specs.append(pl.BlockSpec((2, CHUNK, LANES), lambda i, p: (p, jnp.maximum(i * rb - 1, 0), 0)))
            args.append(a)
    tab = pl.BlockSpec((R, LANES), lambda i, p: (i, 0))
    mat = pl.BlockSpec((LANES, LANES), lambda i, p: (0, 0))
    return pl.pallas_call(
        body, name=name, grid=(M // R, npair),
        in_specs=in_specs + [tab, tab, mat, mat, mat, pl.BlockSpec(memory_space=pl.ANY)],
        out_specs=pl.BlockSpec((R, LANES), lambda i, p: (i, blk0 + p)),
        out_shape=jax.ShapeDtypeStruct(dst.shape, dst.dtype),
        input_output_aliases={nrefs + 5: 0},
        compiler_params=_params(2, False))(*args, Ct, St, RT, D0T, D1T, dst)


def _head_masks():
    lane = lax.broadcasted_iota(jnp.int32, (1, LANES), 1)
    return [lane < 64, lane >= 64]


def _decay_vecs(lam, mu):
    i = lax.broadcasted_iota(jnp.int32, (CHUNK, 1), 0).astype(F32)
    return dict(qf=jnp.exp(lam * (i + 1.0)), kf=jnp.exp(lam * (CHUNK - 1.0 - i)),
                qb=jnp.exp(mu * (CHUNK - i)), kb=jnp.exp(mu * i),
                gf=jnp.exp(lam * float(CHUNK)), gb=jnp.exp(mu * float(CHUNK)), i=i)


def _decay_mask(lam, mu):
    r = lax.broadcasted_iota(jnp.int32, (CHUNK, CHUNK), 0)
    c = lax.broadcasted_iota(jnp.int32, (CHUNK, CHUNK), 1)
    rel = (r - c).astype(F32)
    low = rel >= 0.0
    mf = jnp.exp(lam * jnp.maximum(rel, 0.0))
    mb = jnp.exp(mu * jnp.maximum(-rel, 0.0))
    return jnp.where(low, mf, mb), rel, low


def _lam_of(lg_ref, row, idx):
    return jnp.full((1, 1), lg_ref[row, idx], F32)


def _group_index(pair_blk, npairs):
    assert pair_blk % npairs == 0
    return pair_blk // npairs


def _ret_states_fwd(Kr, P, Krc, Pc, lg, rv_blk, npairs):
    L = Kr.shape[0]
    Lc = Krc.shape[0]
    N, ncc = L // CHUNK, Lc // CHUNK
    rv_grp = _group_index(rv_blk, npairs)

    heads = [(p, h) for p in range(npairs) for h in range(2)]
    kcols = lambda p: slice(p * LANES, (p + 1) * LANES)
    vcols = lambda p, h: slice((2 * p + h) * LANES, (2 * p + h + 1) * LANES)

    def body(lg_ref, k_ref, v_ref, kc_ref, vc_ref, sf_ref, S):
        n = pl.program_id(0)
        masks = _head_masks()

        @pl.when(n == 0)
        def _():
            for p, h in heads:
                lam = _lam_of(lg_ref, 0, 2 * p + h)
                dv = _decay_vecs(lam, lam)
                s = jnp.zeros((LANES, LANES), F32)
                for cc in range(ncc):
                    rows = slice(cc * CHUNK, (cc + 1) * CHUNK)
                    kw = jnp.where(masks[h], kc_ref[rows, kcols(p)].astype(F32) * dv["kf"], 0.0).astype(BF16)
                    s = dv["gf"] * s + _tn(kw, vc_ref[rows, vcols(p, h)])
                S[p, h] = s

        for p, h in heads:
            lam = _lam_of(lg_ref, 0, 2 * p + h)
            dv = _decay_vecs(lam, lam)
            s = S[p, h]
            sf_ref[p, 0, h] = s.astype(BF16)
            kw = jnp.where(masks[h], k_ref[:, kcols(p)].astype(F32) * dv["kf"], 0.0).astype(BF16)
            S[p, h] = dv["gf"] * s + _tn(kw, v_ref[:, vcols(p, h)])

    wq, wv = npairs * LANES, npairs * 2 * LANES
    return pl.pallas_call(
        body, name="ret_states_fwd", grid=(N,),
        in_specs=[pl.BlockSpec(memory_space=pltpu.SMEM),
                  pl.BlockSpec((CHUNK, wq), lambda n: (n, 0)),
                  pl.BlockSpec((CHUNK, wv), lambda n: (n, rv_grp)),
                  pl.BlockSpec((Lc, wq), lambda n: (0, 0)),
                  pl.BlockSpec((Lc, wv), lambda n: (0, rv_grp))],
        out_specs=pl.BlockSpec((npairs, 1, 2, LANES, LANES), lambda n: (0, n, 0, 0, 0)),
        out_shape=jax.ShapeDtypeStruct((npairs, N, 2, LANES, LANES), BF16),
        scratch_shapes=[pltpu.VMEM((npairs, 2, LANES, LANES), F32)],
        compiler_params=_params(1, False))(lg, Kr, P, Krc, Pc)


def _ret_chunk_fwd(q, k, v, sf, sb, hm, lam, mu):
    dv = _decay_vecs(lam, mu)
    Mk, rel, low = _decay_mask(lam, mu)
    qm = jnp.where(hm, q, jnp.zeros_like(q))
    qmf = qm.astype(F32)
    A = _nt(qm, k)
    Am = A * Mk
    Amb = Am.astype(BF16)
    Qf = (qmf * dv["qf"]).astype(BF16)
    Qb = (qmf * dv["qb"]).astype(BF16)
    O = _nn(Amb, v) + _nn(Qf, sf) + _nn(Qb, sb)
    return dict(dv=dv, Mk=Mk, rel=rel, low=low, qm=qm, Am=Am, Amb=Amb, Qf=Qf, Qb=Qb, O=O)


def _ret_out_fwd(Qr, Kr, P, Krc, Pc, SF, lg, rv_blk, rg_blk, npairs, d_mix):
    L = Qr.shape[0]
    Lc = Krc.shape[0]
    N, ncc = L // CHUNK, Lc // CHUNK

    rv_grp, rg_grp = _group_index(rv_blk, npairs), _group_index(rg_blk, npairs)
    heads = [(p, h) for p in range(npairs) for h in range(2)]
    kcols = lambda p: slice(p * LANES, (p + 1) * LANES)
    vcols = lambda p, h: slice((2 * p + h) * LANES, (2 * p + h + 1) * LANES)

    def body(lg_ref, q_ref, k_ref, v_ref, g_ref, sf_ref, kc_ref, vc_ref, y_ref, sb_ref, S):
        n = pl.program_id(0)
        masks = _head_masks()

        @pl.when(n == 0)
        def _():
            for p, h in heads:
                mu = _lam_of(lg_ref, 1, 2 * p + h)
                dvb = _decay_vecs(mu, mu)
                s = jnp.zeros((LANES, LANES), F32)
                for cc in reversed(range(ncc)):
                    rows = slice(cc * CHUNK, (cc + 1) * CHUNK)
                    kw = jnp.where(masks[h], kc_ref[rows, kcols(p)].astype(F32) * dvb["kb"], 0.0).astype(BF16)
                    s = dvb["gb"] * s + _tn(kw, vc_ref[rows, vcols(p, h)])
                S[p, h] = s

        for p, h in heads:
            lam = _lam_of(lg_ref, 0, 2 * p + h)
            mu = _lam_of(lg_ref, 1, 2 * p + h)
            hm = masks[h]
            dvb = _decay_vecs(lam, mu)
            s = S[p, h]
            sbb = s.astype(BF16)
            sb_ref[p, 0, h] = sbb
            k = k_ref[:, kcols(p)]
            v = v_ref[:, vcols(p, h)]
            f = _ret_chunk_fwd(q_ref[:, kcols(p)], k, v, sf_ref[p, 0, h], sbb, hm, lam, mu)
            O = f["O"]
            r = lax.rsqrt(jnp.mean(O * O, axis=-1, keepdims=True) + NORM_EPS)
            g = g_ref[:, vcols(p, h)].astype(F32)
            y_ref[:, vcols(p, h)] = (O * r * (g * _sigmoid(g))).astype(BF16)
            kw = jnp.where(hm, k.astype(F32) * dvb["kb"], 0.0).astype(BF16)
            S[p, h] = dvb["gb"] * s + _tn(kw, v)

    rev = lambda n: N - 1 - n
    wq, wv = npairs * LANES, npairs * 2 * LANES
    st = pl.BlockSpec((npairs, 1, 2, LANES, LANES), lambda n: (0, rev(n), 0, 0, 0))
    return pl.pallas_call(
        body, name="ret_out_fwd", grid=(N,),
        in_specs=[pl.BlockSpec(memory_space=pltpu.SMEM),
                  pl.BlockSpec((CHUNK, wq), lambda n: (rev(n), 0)),
                  pl.BlockSpec((CHUNK, wq), lambda n: (rev(n), 0)),
                  pl.BlockSpec((CHUNK, wv), lambda n: (rev(n), rv_grp)),
                  pl.BlockSpec((CHUNK, wv), lambda n: (rev(n), rg_grp)),
                  st,
                  pl.BlockSpec((Lc, wq), lambda n: (0, 0)),
                  pl.BlockSpec((Lc, wv), lambda n: (0, rv_grp))],
        out_specs=[pl.BlockSpec((CHUNK, wv), lambda n: (rev(n), 0)), st],
        out_shape=[jax.ShapeDtypeStruct((L, d_mix), BF16),
                   jax.ShapeDtypeStruct((npairs, N, 2, LANES, LANES), BF16)],
        scratch_shapes=[pltpu.VMEM((npairs, 2, LANES, LANES), F32)],
        compiler_params=_params(1))(lg, Qr, Kr, P, P, SF, Krc, Pc)


ACC_ROWS = 8


def _ret_bwd1(Qr, Kr, P, Krc, Pc, SF, SB, dY, lg, rv_blk, rg_blk, npairs, d_proj):
    L = Qr.shape[0]
    Lc = Krc.shape[0]
    N, ncc = L // CHUNK, Lc // CHUNK
    rv_grp, rg_grp = _group_index(rv_blk, npairs), _group_index(rg_blk, npairs)
    heads = [(p, h) for p in range(npairs) for h in range(2)]
    kcols = lambda p: slice(p * LANES, (p + 1) * LANES)
    vcols = lambda p, h: slice((2 * p + h) * LANES, (2 * p + h + 1) * LANES)

    def body(lg_ref, q_ref, k_ref, v_ref, g_ref, sf_ref, sb_ref, dy_ref, kc_ref, vc_ref,
             dq_ref, dk_ref, dv_ref, dg_ref, do_ref, dkc_ref, dvc_ref, acc_ref, dS, T):
        n = pl.program_id(0)
        masks = _head_masks()

        @pl.when(n == 0)
        def _():
            dS[...] = jnp.zeros_like(dS)
            T[...] = jnp.zeros_like(T)
            acc_ref[...] = jnp.zeros_like(acc_ref)

        def head_main(p, h):
            lam = _lam_of(lg_ref, 0, 2 * p + h)
            mu = _lam_of(lg_ref, 1, 2 * p + h)
            hm = masks[h]
            hs = vcols(p, h)
            v = v_ref[:, hs]
            k = k_ref[:, kcols(p)]
            sf = sf_ref[p, 0, h]
            sb = sb_ref[p, 0, h]
            f = _ret_chunk_fwd(q_ref[:, kcols(p)], k, v, sf, sb, hm, lam, mu)
            dv_, O = f["dv"], f["O"]
            r = lax.rsqrt(jnp.mean(O * O, axis=-1, keepdims=True) + NORM_EPS)
            on = O * r
            g = g_ref[:, hs].astype(F32)
            sg = _sigmoid(g)
            dy = dy_ref[:, hs].astype(F32)
            dg_ref[:, hs] = (dy * on * (sg * (1.0 + g * (1.0 - sg)))).astype(BF16)
            don = dy * (g * sg)
            dO = r * (don - on * jnp.mean(don * on, axis=-1, keepdims=True))
            dOb = dO.astype(BF16)
            do_ref[:, hs] = dOb
            dAm = _nt(dOb, v)
            T[p, h] += dAm * f["Am"]
            dAb = (dAm * f["Mk"]).astype(BF16)
            km = jnp.where(hm, k, jnp.zeros_like(k))
            dq = _nn(dAb, km)
            dk = _tn(dAb, f["qm"])
            dvh = _tn(f["Amb"], dOb)
            dQf = _nt(dOb, sf)
            dQb = _nt(dOb, sb)
            dq = dq + dQf * dv_["qf"] + dQb * dv_["qb"]
            acc_ref[p, h, 0:1, :] += _fsum(dQf * f["Qf"].astype(F32) * (dv_["i"] + 1.0))
            acc_ref[p, h, 1:2, :] += _fsum(dQb * f["Qb"].astype(F32) * (CHUNK - dv_["i"]))
            dSh = dS[p, h]
            dSb_ = dSh.astype(BF16)
            Kf = (km.astype(F32) * dv_["kf"]).astype(BF16)
            dKf = _nt(v, dSb_)
            dk = dk + jnp.where(hm, dKf * dv_["kf"], 0.0)
            acc_ref[p, h, 2:3, :] += _fsum(jnp.where(hm, dKf, 0.0) * Kf.astype(F32) * (CHUNK - 1.0 - dv_["i"]))
            dvh = dvh + _nn(Kf, dSb_)
            acc_ref[p, h, 3:4, :] += float(CHUNK) * dv_["gf"] * _fsum(dSh * sf.astype(F32))
            dSh = dv_["gf"] * dSh + _tn(f["Qf"], dOb)
            dS[p, h] = dSh
            dv_ref[:, hs] = dvh
            return dq, dk

        for p in range(npairs):
            dq0, dk0 = head_main(p, 0)
            dq1, dk1 = head_main(p, 1)
            dq_ref[:, kcols(p)] = dq0 + dq1
            dk_ref[:, kcols(p)] = dk0 + dk1

        @pl.when(n == N - 1)
        def _():
            for p, h in heads:
                lam = _lam_of(lg_ref, 0, 2 * p + h)
                dv_ = _decay_vecs(lam, lam)
                hm = masks[h]
                hs = vcols(p, h)
                states = [jnp.zeros((LANES, LANES), F32)]
                kws = []
                for cc in range(ncc):
                    rows = slice(cc * CHUNK, (cc + 1) * CHUNK)
                    kw = jnp.where(hm, kc_ref[rows, kcols(p)].astype(F32) * dv_["kf"], 0.0).astype(BF16)
                    kws.append(kw)
                    states.append(dv_["gf"] * states[-1] + _tn(kw, vc_ref[rows, hs]))
                d = dS[p, h]
                for cc in reversed(range(ncc)):
                    db = d.astype(BF16)
                    rows = slice(cc * CHUNK, (cc + 1) * CHUNK)
                    dKf_c = jnp.where(hm, _nt(vc_ref[rows, hs], db), 0.0)
                    part = dKf_c * dv_["kf"]
                    if h == 0:
                        dkc_ref[rows, kcols(p)] = part
                    else:
                        dkc_ref[rows, kcols(p)] += part
                    acc_ref[p, h, 2:3, :] += _fsum(dKf_c * kws[cc].astype(F32) * (CHUNK - 1.0 - dv_["i"]))
                    dvc_ref[rows, hs] = _nn(kws[cc], db)
                    acc_ref[p, h, 3:4, :] += float(CHUNK) * dv_["gf"] * _fsum(d * states[cc])
                    d = dv_["gf"] * d
                _, rel, low = _decay_mask(lam, lam)
                Th = T[p, h]
                acc_ref[p, h, 4:5, :] += _fsum(jnp.where(low, Th * rel, 0.0))
                acc_ref[p, h, 5:6, :] += _fsum(jnp.where(low, 0.0, -Th * rel))

    rev = lambda n: N - 1 - n
    wq, wv = npairs * LANES, npairs * 2 * LANES
    st = pl.BlockSpec((npairs, 1, 2, LANES, LANES), lambda n: (0, rev(n), 0, 0, 0))
    pair = pl.BlockSpec((CHUNK, wq), lambda n: (rev(n), 0))
    wide = lambda grp: pl.BlockSpec((CHUNK, wv), lambda n: (rev(n), grp))
    return pl.pallas_call(
        body, name="ret_bwd_desc", grid=(N,),
        in_specs=[pl.BlockSpec(memory_space=pltpu.SMEM), pair, pair, wide(rv_grp), wide(rg_grp), st, st, wide(0),
                  pl.BlockSpec((Lc, wq), lambda n: (0, 0)),
                  pl.BlockSpec((Lc, wv), lambda n: (0, rv_grp))],
        out_specs=[pair, pair, wide(0), wide(rg_grp), wide(0),
                   pl.BlockSpec((Lc, wq), lambda n: (0, 0)),
                   pl.BlockSpec((Lc, wv), lambda n: (0, 0)),
                   pl.BlockSpec((npairs, 2, ACC_ROWS, LANES), lambda n: (0, 0, 0, 0))],
        out_shape=[jax.ShapeDtypeStruct((L, npairs * LANES), F32),
                   jax.ShapeDtypeStruct((L, npairs * LANES), F32),
                   jax.ShapeDtypeStruct((L, npairs * 2 * LANES), F32),
                   jax.ShapeDtypeStruct((L, d_proj), BF16),
                   jax.ShapeDtypeStruct((L, npairs * 2 * LANES), BF16),
                   jax.ShapeDtypeStruct((Lc, npairs * LANES), F32),
                   jax.ShapeDtypeStruct((Lc, npairs * 2 * LANES), F32),
                   jax.ShapeDtypeStruct((npairs, 2, ACC_ROWS, LANES), F32)],
        scratch_shapes=[pltpu.VMEM((npairs, 2, LANES, LANES), F32), pltpu.VMEM((npairs, 2, CHUNK, CHUNK), F32)],
        compiler_params=_params(1))(lg, Qr, Kr, P, P, SF, SB, dY, Krc, Pc)


def _ret_bwd2(Qr, Kr, P, Krc, Pc, SB, dO, dKr, dVp, dP, dKc, dVc, lg, rv_blk, npairs):
    L = Qr.shape[0]
    Lc = Krc.shape[0]
    N, ncc = L // CHUNK, Lc // CHUNK
    rv_grp = _group_index(rv_blk, npairs)
    heads = [(p, h) for p in range(npairs) for h in range(2)]
    kcols = lambda p: slice(p * LANES, (p + 1) * LANES)
    vcols = lambda p, h: slice((2 * p + h) * LANES, (2 * p + h + 1) * LANES)

    def body(lg_ref, q_ref, k_ref, v_ref, sb_ref, do_ref, dkin_ref, dvin_ref, kc_ref, vc_ref, dkcin_ref, dvcin_ref,
             dpin_ref, dk_ref, dv_ref, dkc_ref, dvc_ref, acc_ref, dS):
        del dpin_ref
        n = pl.program_id(0)
        masks = _head_masks()

        @pl.when(n == 0)
        def _():
            dS[...] = jnp.zeros_like(dS)
            acc_ref[...] = jnp.zeros_like(acc_ref)

        def head_main(p, h):
            mu = _lam_of(lg_ref, 1, 2 * p + h)
            hm = masks[h]
            hs = vcols(p, h)
            dv_ = _decay_vecs(mu, mu)
            v = v_ref[:, hs]
            k = k_ref[:, kcols(p)]
            q = q_ref[:, kcols(p)]
            dOb = do_ref[:, hs]
            km = jnp.where(hm, k, jnp.zeros_like(k)).astype(F32)
            Kb = (km * dv_["kb"]).astype(BF16)
            Qb = (jnp.where(hm, q, jnp.zeros_like(q)).astype(F32) * dv_["qb"]).astype(BF16)
            dSh = dS[p, h]
            dSb_ = dSh.astype(BF16)
            dKb = jnp.where(hm, _nt(v, dSb_), 0.0)
            acc_ref[p, h, 0:1, :] += _fsum(dKb * Kb.astype(F32) * dv_["i"])
            dv_ref[:, hs] = (dvin_ref[:, hs] + _nn(Kb, dSb_)).astype(BF16)
            acc_ref[p, h, 1:2, :] += float(CHUNK) * dv_["gb"] * _fsum(dSh * sb_ref[p, 0, h].astype(F32))
            dS[p, h] = dv_["gb"] * dSh + _tn(Qb, dOb)
            return dKb * dv_["kb"]

        for p in range(npairs):
            dk_ref[:, kcols(p)] = dkin_ref[:, kcols(p)] + head_main(p, 0) + head_main(p, 1)

        @pl.when(n == N - 1)
        def _():
            for p, h in heads:
                mu = _lam_of(lg_ref, 1, 2 * p + h)
                hm = masks[h]
                hs = vcols(p, h)
                dv_ = _decay_vecs(mu, mu)
                states = {}
                kws = {}
                s = jnp.zeros((LANES, LANES), F32)
                for cc in reversed(range(ncc)):
                    rows = slice(cc * CHUNK, (cc + 1) * CHUNK)
                    states[cc] = s
                    kw = jnp.where(hm, kc_ref[rows, kcols(p)].astype(F32) * dv_["kb"], 0.0).astype(BF16)
                    kws[cc] = kw
                    s = dv_["gb"] * s + _tn(kw, vc_ref[rows, hs])
                d = dS[p, h]
                for cc in range(ncc):
                    db = d.astype(BF16)
                    rows = slice(cc * CHUNK, (cc + 1) * CHUNK)
                    dKb_c = jnp.where(hm, _nt(vc_ref[rows, hs], db), 0.0)
                    part = dKb_c * dv_["kb"]
                    if h == 0:
                        dkc_ref[rows, kcols(p)] = dkcin_ref[rows, kcols(p)] + part
                    else:
                        dkc_ref[rows, kcols(p)] += part
                    acc_ref[p, h, 0:1, :] += _fsum(dKb_c * kws[cc].astype(F32) * dv_["i"])
                    dvc_ref[rows, hs] = dvcin_ref[rows, hs] + _nn(kws[cc], db)
                    acc_ref[p, h, 1:2, :] += float(CHUNK) * dv_["gb"] * _fsum(d * states[cc])
                    d = dv_["gb"] * d

    wq, wv = npairs * LANES, npairs * 2 * LANES
    st = pl.BlockSpec((npairs, 1, 2, LANES, LANES), lambda n: (0, n, 0, 0, 0))
    pair = pl.BlockSpec((CHUNK, wq), lambda n: (n, 0))
    wide = lambda grp: pl.BlockSpec((CHUNK, wv), lambda n: (n, grp))
    ckc = pl.BlockSpec((Lc, wq), lambda n: (0, 0))
    cvc = lambda grp: pl.BlockSpec((Lc, wv), lambda n: (0, grp))
    return pl.pallas_call(
        body, name="ret_bwd_asc", grid=(N,),
        in_specs=[pl.BlockSpec(memory_space=pltpu.SMEM), pair, pair, wide(rv_grp), st, wide(0), pair, wide(0),
                  ckc, cvc(rv_grp), ckc, cvc(0), pl.BlockSpec(memory_space=pl.ANY)],
        out_specs=[pair, wide(rv_grp), ckc, cvc(0),
                   pl.BlockSpec((npairs, 2, ACC_ROWS, LANES), lambda n: (0, 0, 0, 0))],
        out_shape=[jax.ShapeDtypeStruct(dKr.shape, F32),
                   jax.ShapeDtypeStruct(dP.shape, dP.dtype),
                   jax.ShapeDtypeStruct(dKc.shape, F32),
                   jax.ShapeDtypeStruct(dVc.shape, F32),
                   jax.ShapeDtypeStruct((npairs, 2, ACC_ROWS, LANES), F32)],
        input_output_aliases={12: 1},
        scratch_shapes=[pltpu.VMEM((npairs, 2, LANES, LANES), F32)],
        compiler_params=_params(1))(lg, Qr, Kr, P, SB, dO, dKr, dVp, Krc, Pc, dKc, dVc, dP)


GROUP = 4


def _att_valid(n, N, Lc):
    rows = GROUP * CHUNK
    row = lax.broadcasted_iota(jnp.int32, (rows, 3 * CHUNK + Lc), 0) % CHUNK
    col = lax.broadcasted_iota(jnp.int32, (rows, 3 * CHUNK + Lc), 1)
    ok = jnp.logical_and(col >= row, col <= row + 2 * CHUNK)
    ok = jnp.logical_and(ok, jnp.logical_or(col >= CHUNK, n > 0))
    ok = jnp.logical_and(ok, jnp.logical_or(col < 2 * CHUNK, n < N - 1))
    return jnp.logical_or(ok, col >= 3 * CHUNK)


def _stack_heads(ref):
    masks = _head_masks()
    tiles = []
    for pr in range(2):
        t = ref[:, pr * LANES:(pr + 1) * LANES]
        for a in range(2):
            tiles.append(jnp.where(masks[a], t, jnp.zeros_like(t)))
    return jnp.concatenate(tiles, axis=0)


def _unstack_heads(x4):
    m0 = _head_masks()[0]
    return [jnp.where(m0, x4[(2 * pr) * CHUNK:(2 * pr + 1) * CHUNK], x4[(2 * pr + 1) * CHUNK:(2 * pr + 2) * CHUNK])
            for pr in range(2)]


def _sink_column(sink_ref, g):
    row = lax.broadcasted_iota(jnp.int32, (GROUP * CHUNK, 1), 0) // CHUNK
    col = jnp.zeros((GROUP * CHUNK, 1), F32)
    for h in range(GROUP):
        col = jnp.where(row == h, sink_ref[0, g * GROUP + h], col)
    return col


def _att_probs(q4, Kall, valid, snk):
    s = jnp.where(valid, _nt(q4, Kall), NEG)
    mx = jnp.maximum(jnp.max(s, axis=1, keepdims=True), snk)
    p = jnp.exp(s - mx)
    p_snk = jnp.exp(snk - mx)
    inv = 1.0 / (jnp.sum(p, axis=1, keepdims=True) + p_snk)
    return p, p_snk, inv


def _att_specs(Lc, N):
    q = pl.BlockSpec((CHUNK, 2 * LANES), lambda g, n: (n, g))
    kv = lambda s: pl.BlockSpec((1, CHUNK, LANES), lambda g, n: (g, jnp.clip(n + s, 0, N - 1), 0))
    ctx = pl.BlockSpec((1, Lc, LANES), lambda g, n: (g, 0, 0))
    return q, kv, ctx


def _att_fwd(Qa, Kd, Vd, Kdc, Vdc, sink, Y, blk0):
    L = Qa.shape[0]
    Lc = Kdc.shape[1]
    N = L // CHUNK
    nkv = Kd.shape[0]

    def body(sink_ref, q_ref, kp, kc_, kn, vp, vc_, vn, kctx, vctx, y_in, o_ref):
        del y_in
        g, n = pl.program_id(0), pl.program_id(1)
        Kall = jnp.concatenate([kp[0], kc_[0], kn[0], kctx[0]], axis=0)
        Vall = jnp.concatenate([vp[0], vc_[0], vn[0], vctx[0]], axis=0)
        p, _, inv = _att_probs(_stack_heads(q_ref), Kall, _att_valid(n, N, Lc), _sink_column(sink_ref, g))
        o4 = _nn(p.astype(BF16), Vall) * inv
        for pr, o in enumerate(_unstack_heads(o4)):
            o_ref[:, pr * LANES:(pr + 1) * LANES] = o.astype(BF16)

    q, kv, ctx = _att_specs(Lc, N)
    return pl.pallas_call(
        body, name="att_fwd", grid=(nkv, N),
        in_specs=[pl.BlockSpec(memory_space=pltpu.SMEM), q, kv(-1), kv(0), kv(1), kv(-1), kv(0), kv(1), ctx, ctx,
                  pl.BlockSpec(memory_space=pl.ANY)],
        out_specs=pl.BlockSpec((CHUNK, 2 * LANES), lambda g, n: (n, blk0 + g)),
        out_shape=jax.ShapeDtypeStruct(Y.shape, Y.dtype),
        input_output_aliases={10: 0},
        compiler_params=_params(2))(sink, Qa, Kd, Kd, Kd, Vd, Vd, Vd, Kdc, Vdc, Y)


def _att_bwd(Qa, Kd, Vd, Kdc, Vdc, sink, dY, blk0):
    L = Qa.shape[0]
    Lc = Kdc.shape[1]
    N = L // CHUNK
    nkv = Kd.shape[0]

    def body(sink_ref, q_ref, kp, kc_, kn, vp, vc_, vn, kctx, vctx, dy_ref,
             dq_ref, dkp, dkc_, dkn, dvp, dvc_, dvn, dkctx, dvctx, dsink_ref):
        g, n = pl.program_id(0), pl.program_id(1)

        @pl.when(n == 0)
        def _():
            dkctx[...] = jnp.zeros_like(dkctx)
            dvctx[...] = jnp.zeros_like(dvctx)
            dsink_ref[...] = jnp.zeros_like(dsink_ref)

        Kall = jnp.concatenate([kp[0], kc_[0], kn[0], kctx[0]], axis=0)
        Vall = jnp.concatenate([vp[0], vc_[0], vn[0], vctx[0]], axis=0)
        q4 = _stack_heads(q_ref)
        do4 = _stack_heads(dy_ref)
        p, p_snk, inv = _att_probs(q4, Kall, _att_valid(n, N, Lc), _sink_column(sink_ref, g))
        P = p * inv
        dp = _nt(do4, Vall)
        delta = jnp.sum(P * dp, axis=1, keepdims=True)
        ds = (P * (dp - delta)).astype(BF16)
        dsnk = -(p_snk * inv) * delta
        for h in range(GROUP):
            dsink_ref[0, h:h + 1, :] += _fsum(dsnk[h * CHUNK:(h + 1) * CHUNK])
        for pr, dq in enumerate(_unstack_heads(_nn(ds, Kall))):
            dq_ref[:, pr * LANES:(pr + 1) * LANES] = dq
        dK = _tn(ds, q4)
        dV = _tn(P.astype(BF16), do4)
        for j, (rk, rv) in enumerate([(dkp, dvp), (dkc_, dvc_), (dkn, dvn)]):
            rk[0] = dK[j * CHUNK:(j + 1) * CHUNK].astype(BF16)
            rv[0] = dV[j * CHUNK:(j + 1) * CHUNK].astype(BF16)
        dkctx[0] += dK[3 * CHUNK:]
        dvctx[0] += dV[3 * CHUNK:]

    q, kv, ctx = _att_specs(Lc, N)
    blk = pl.BlockSpec((1, CHUNK, LANES), lambda g, n: (g, n, 0))
    part = jax.ShapeDtypeStruct((nkv, L, LANES), BF16)
    cshape = jax.ShapeDtypeStruct((nkv, Lc, LANES), F32)
    return pl.pallas_call(
        body, name="att_bwd", grid=(nkv, N),
        in_specs=[pl.BlockSpec(memory_space=pltpu.SMEM), q, kv(-1), kv(0), kv(1), kv(-1), kv(0), kv(1), ctx, ctx,
                  pl.BlockSpec((CHUNK, 2 * LANES), lambda g, n: (n, blk0 + g))],
        out_specs=[q, blk, blk, blk, blk, blk, blk, ctx, ctx,
                   pl.BlockSpec((1, 8, LANES), lambda g, n: (g, 0, 0))],
        out_shape=[jax.ShapeDtypeStruct(Qa.shape, F32), part, part, part, part, part, part, cshape, cshape,
                   jax.ShapeDtypeStruct((nkv, 8, LANES), F32)],
        compiler_params=_params(2))(sink, Qa, Kd, Kd, Kd, Vd, Vd, Vd, Kdc, Vdc, dY)


def _bwd_proj(dx, gt, wT, saved, G=None, U=None, name="bwd_proj"):
    M, D = dx.shape
    N = wT.shape[1]
    swiglu = G is not None
    tm, tn = _tile(M, ROWS_PER_LATCH, 8), _tile(N, 256 if swiglu else 512)

    def body(*refs):
        if swiglu:
            dx_ref, gt_ref, w_ref, sv_ref, G_ref, U_ref, dG_ref, dU_ref, dz_ref, dgt_ref, zs = refs
        else:
            dx_ref, gt_ref, w_ref, sv_ref, dA_ref, dz_ref, dgt_ref, zs = refs
        i, j = pl.program_id(0), pl.program_id(1)

        @pl.when(jnp.logical_and(i == 0, j == 0))
        def _():
            dgt_ref[...] = jnp.zeros_like(dgt_ref)

        @pl.when(j == 0)
        def _():
            d = dx_ref[...]
            z = (d * gt_ref[...]).astype(BF16)
            zs[...] = z
            dz_ref[...] = z
            dgt_ref[...] += jnp.sum(d * sv_ref[...].astype(F32), axis=0, keepdims=True)

        dA = _nn(zs[...], w_ref[...])
        if swiglu:
            Gv = G_ref[...].astype(F32)
            Uv = U_ref[...].astype(F32)
            sg = _sigmoid(Gv)
            dU_ref[...] = (dA * Gv * sg).astype(BF16)
            dG_ref[...] = (dA * Uv * (sg * (1.0 + Gv * (1.0 - sg)))).astype(BF16)
        else:
            dA_ref[...] = dA.astype(BF16)

    row = pl.BlockSpec((tm, D), lambda i, j: (i, 0))
    vec = pl.BlockSpec((1, D), lambda i, j: (0, 0))
    tile = pl.BlockSpec((tm, tn), lambda i, j: (i, j))
    big = jax.ShapeDtypeStruct((M, N), BF16)
    in_specs = [row, vec, pl.BlockSpec((D, tn), lambda i, j: (0, j)), row]
    args = [dx, gt, wT, saved]
    if swiglu:
        in_specs += [tile, tile]
        args += [G, U]
        out_specs = [tile, tile, row, vec]
        out_shape = [big, big, jax.ShapeDtypeStruct((M, D), BF16), jax.ShapeDtypeStruct((1, D), F32)]
    else:
        out_specs = [tile, row, vec]
        out_shape = [big, jax.ShapeDtypeStruct((M, D), BF16), jax.ShapeDtypeStruct((1, D), F32)]
    return pl.pallas_call(
        body, name=name, grid=(M // tm, N // tn), in_specs=in_specs, out_specs=out_specs, out_shape=out_shape,
        scratch_shapes=[pltpu.VMEM((tm, D), BF16)], compiler_params=_params(2))(*args)


def _tn_matmul(pairs, name):
    Ka, Nb = pairs[0][0].shape[1], pairs[0][1].shape[1]
    tk, tn = _tile(Ka, 2048), _tile(Nb, 2048)
    tls, nks = [], []
    for a, _ in pairs:
        tl = _tile(a.shape[0], 512, 8)
        tls.append(tl)
        nks.append(a.shape[0] // tl)
    starts = [int(s) for s in np.cumsum([0] + nks[:-1])]
    nk = int(sum(nks))

    def body(*refs):
        out_ref, acc = refs[-2], refs[-1]
        k = pl.program_id(2)

        @pl.when(k == 0)
        def _():
            acc[...] = jnp.zeros_like(acc)

        for idx in range(len(pairs)):
            a_ref, b_ref = refs[2 * idx], refs[2 * idx + 1]

            @pl.when(jnp.logical_and(k >= starts[idx], k < starts[idx] + nks[idx]))
            def _():
                acc[...] += _tn(a_ref[...], b_ref[...])

        @pl.when(k == nk - 1)
        def _():
            out_ref[...] = acc[...].astype(BF16)

    in_specs, args = [], []
    for idx, (a, b) in enumerate(pairs):
        s0, n_ = starts[idx], nks[idx]
        in_specs.append(pl.BlockSpec((tls[idx], tk), lambda i, j, k, s0=s0, n_=n_: (jnp.clip(k - s0, 0, n_ - 1), i)))
        in_specs.append(pl.BlockSpec((tls[idx], tn), lambda i, j, k, s0=s0, n_=n_: (jnp.clip(k - s0, 0, n_ - 1), j)))
        args += [a, b]
    return pl.pallas_call(
        body, name=name, grid=(Ka // tk, Nb // tn, nk), in_specs=in_specs,
        out_specs=pl.BlockSpec((tk, tn), lambda i, j, k: (i, j)),
        out_shape=jax.ShapeDtypeStruct((Ka, Nb), BF16),
        scratch_shapes=[pltpu.VMEM((tk, tn), F32)], compiler_params=_params(3))(*args)


def _bwd_norm_mod(pairs, x, dres, g, sh, sc, name):
    M, D = x.shape
    K = pairs[0][0].shape[1]
    tm, tk = _tile(M, 512, 8), _tile(K, 1152 if len(pairs) == 1 else 512)
    nk = K // tk
    npair = len(pairs)
    has_res = dres is not None

    def body(*refs):
        pr = refs[:2 * npair]
        rest = refs[2 * npair:]
        if has_res:
            x_ref, dres_ref, g_ref, sh_ref, sc_ref, dx_ref, st_ref, acc = rest
        else:
            x_ref, g_ref, sh_ref, sc_ref, dx_ref, st_ref, acc = rest
        del sh_ref
        i, k = pl.program_id(0), pl.program_id(1)

        @pl.when(jnp.logical_and(i == 0, k == 0))
        def _():
            st_ref[...] = jnp.zeros_like(st_ref)

        @pl.when(k == 0)
        def _():
            acc[...] = jnp.zeros_like(acc)

        t = _nt(pr[1][...], pr[0][...])
        for idx in range(1, npair):
            t = t + _nt(pr[2 * idx + 1][...], pr[2 * idx][...])
        acc[...] += t

        @pl.when(k == nk - 1)
        def _():
            xv = x_ref[...]
            gv = g_ref[...]
            dh = acc[...].T
            r = lax.rsqrt(jnp.mean(xv * xv, axis=-1, keepdims=True) + NORM_EPS)
            xh = xv * r
            st_ref[0:1, :] += jnp.sum(dh, axis=0, keepdims=True)
            st_ref[1:2, :] += jnp.sum(dh * (xh * gv), axis=0, keepdims=True)
            dn = dh * (1.0 + sc_ref[...])
            st_ref[2:3, :] += jnp.sum(dn * xh, axis=0, keepdims=True)
            dxh = dn * gv
            d = r * (dxh - xh * jnp.mean(dxh * xh, axis=-1, keepdims=True))
            if has_res:
                d = d + dres_ref[...]
            dx_ref[...] = d

    row = pl.BlockSpec((tm, D), lambda i, k: (i, 0))
    vec = pl.BlockSpec((1, D), lambda i, k: (0, 0))
    in_specs, args = [], []
    for dA, w in pairs:
        in_specs += [pl.BlockSpec((tm, tk), lambda i, k: (i, k)), pl.BlockSpec((D, tk), lambda i, k: (0, k))]
        args += [dA, w]
    in_specs += [row] + ([row] if has_res else []) + [vec, vec, vec]
    args += [x] + ([dres] if has_res else []) + [g, sh, sc]
    return pl.pallas_call(
        body, name=name, grid=(M // tm, nk), in_specs=in_specs,
        out_specs=[row, pl.BlockSpec((8, D), lambda i, k: (0, 0))],
        out_shape=[jax.ShapeDtypeStruct((M, D), F32), jax.ShapeDtypeStruct((8, D), F32)],
        scratch_shapes=[pltpu.VMEM((D, tm), F32)], compiler_params=_params(2))(*args)


def _local_step(x, ctx, tgt, mod, modc, norm_mix, norm_ffn, norm_final, lg, sink, w_in, rest_weights, on_grads):
    L, D = x.shape
    Lc = ctx.shape[0]
    d_proj = w_in.shape[1]
    npairs = RET_HEADS // 2
    nkv = ATT_KV_HEADS
    nkvp = nkv // 2
    o_rq = 0
    o_rk = o_rq + RET_HEADS * RET_DK // LANES
    o_rv = o_rk + RET_HEADS * RET_DK // LANES
    o_rg = o_rv + RET_HEADS * RET_DV // LANES
    o_aq = o_rg + RET_HEADS * RET_DV // LANES
    o_ak = o_aq + ATT_HEADS * ATT_DH // LANES
    o_av = o_ak + nkv * ATT_DH // LANES
    assert (o_av + nkv * ATT_DH // LANES) * LANES == d_proj
    assert o_rv % 2 == 0 and o_rg % 2 == 0 and (RET_HEADS * RET_DV) % (2 * LANES) == 0
    rv_blk, rg_blk = o_rv // 2, o_rg // 2
    d_ret = RET_HEADS * RET_DV
    d_mix = d_ret + ATT_HEADS * ATT_DH
    att_blk = d_ret // (2 * LANES)
    k_scale = RET_DK ** -0.5
    a_scale = ATT_DH ** -0.5

    T = _rope_tables(L)
    Tc = dict(C=jnp.ones((Lc, LANES), F32), S=jnp.zeros((Lc, LANES), F32))
    row = lambda m, i: m[i:i + 1]
    sh_m, sc_m, gt_m, sh_f, sc_f, gt_f = [row(mod, i) for i in range(6)]
    sh_mc, sc_mc = row(modc, 0), row(modc, 1)

    P, hx = _norm_mod_matmul(x, norm_mix, sh_m, sc_m, w_in, "in_proj")
    Pc, hc = _norm_mod_matmul(ctx, norm_mix, sh_mc, sc_mc, w_in, "in_proj_ctx")
    nq = RET_HEADS * RET_DK // LANES
    Qr = _rope_cols(P, o_rq, nq, T["Cr"], T["Sr"], T["Rr"], 1.0, True, "rope_rq")
    Kr = _rope_cols(P, o_rk, nq, T["Cr"], T["Sr"], T["Rr"], k_scale, True, "rope_rk")
    Krc = _rope_cols(Pc, o_rk, nq, Tc["C"], Tc["S"], T["Rr"], k_scale, False, "scale_rk_ctx")
    Qa = _rope_cols(P, o_aq, ATT_HEADS * ATT_DH // LANES, T["Ca"], T["Sa"], T["Ra"], a_scale, True, "rope_aq")
    Kd = _dup_heads(P, o_ak, nkvp, T["Ca"], T["Sa"], T["Ra"], T["D0"], T["D1"], True, "dup_ak")
    Vd = _dup_heads(P, o_av, nkvp, T["Ca"], T["Sa"], T["Ra"], T["D0"], T["D1"], False, "dup_av")
    Kdc = _dup_heads(Pc, o_ak, nkvp, Tc["C"], Tc["S"], T["Ra"], T["D0"], T["D1"], False, "dup_ak_ctx")
    Vdc = _dup_heads(Pc, o_av, nkvp, Tc["C"], Tc["S"], T["Ra"], T["D0"], T["D1"], False, "dup_av_ctx")

    SF = _ret_states_fwd(Kr, P, Krc, Pc, lg, rv_blk, npairs)
    Y, SB = _ret_out_fwd(Qr, Kr, P, Krc, Pc, SF, lg, rv_blk, rg_blk, npairs, d_mix)
    Y = _att_fwd(Qa, Kd, Vd, Kdc, Vdc, sink, Y, att_blk)

    w_out, w_gate, w_up, w_down = rest_weights(Y)
    x1, O1 = _proj_residual(Y, w_out, x, gt_m, "out_proj")
    G, U, A, h2 = _ffn_in(x1, norm_ffn, sh_f, sc_f, w_gate, w_up)
    x2, Fo = _proj_residual(A, w_down, x1, gt_f, "ffn_out")
    dx2, loss, d_norm_final = _final(x2, norm_final, tgt)

    dG, dU, dz2, dgt_f = _bwd_proj(dx2, gt_f, w_down.T, Fo, G, U, name="ffn_out_bwd")
    g_w_down = _tn_matmul([(A, dz2)], "grad_w_down")
    tok = on_grads(["w_down"], [g_w_down])
    dx1, st_f = _bwd_norm_mod([(dG, w_gate), (dU, w_up)], x1, dx2, norm_ffn + tok, sh_f, sc_f, "ffn_in_bwd")
    g_w_gate = _tn_matmul([(h2, dG)], "grad_w_gate")
    g_w_up = _tn_matmul([(h2, dU)], "grad_w_up")
    tok = on_grads(["w_gate", "w_up"], [g_w_gate, g_w_up])
    dY, dz1, dgt_m = _bwd_proj(dx1, gt_m + tok, w_out.T, O1, name="out_proj_bwd")
    g_w_out = _tn_matmul([(Y, dz1)], "grad_w_out")
    tok = on_grads(["w_out"], [g_w_out])

    dQa, dKp, dKs, dKn, dVp, dVs, dVn, dKdc, dVdc, dsink = _att_bwd(Qa, Kd, Vd, Kdc, Vdc, sink + tok, dY, att_blk)
    dQr, dKr, dVr, dP, dO, dKc, dVc, acc1 = _ret_bwd1(Qr, Kr, P, Krc, Pc, SF, SB, dY, lg, rv_blk, rg_blk, npairs, d_proj)
    dKr, dP, dKc, dVc, acc2 = _ret_bwd2(Qr, Kr, P, Krc, Pc, SB, dO, dKr, dVr, dP, dKc, dVc, lg, rv_blk, npairs)

    dP = _unrope_cols(dQr, dP, o_rq, nq, T["Cr"], T["Sr"], T["RrT"], 1.0, True, "unrope_rq")
    dP = _unrope_cols(dKr, dP, o_rk, nq, T["Cr"], T["Sr"], T["RrT"], k_scale, True, "unrope_rk")
    dP = _unrope_cols(dQa, dP, o_aq, ATT_HEADS * ATT_DH // LANES, T["Ca"], T["Sa"], T["RaT"], a_scale, True, "unrope_aq")
    dP = _fold_heads([(dKs, 0), (dKp, 1), (dKn, -1)], dP, o_ak, nkvp, T["Ca"], T["Sa"], T["RaT"], T["D0T"], T["D1T"],
                     True, "fold_ak")
    dP = _fold_heads([(dVs, 0), (dVp, 1), (dVn, -1)], dP, o_av, nkvp, T["Ca"], T["Sa"], T["RaT"], T["D0T"], T["D1T"],
                     False, "fold_av")
    dPc = jnp.zeros((Lc, d_proj), BF16)
    dPc = _unrope_cols(dKc, dPc, o_rk, nq, Tc["C"], Tc["S"], T["RrT"], k_scale, False, "ctx_rk_bwd")
    dPc = _unrope_cols(dVc, dPc, o_rv, RET_HEADS * RET_DV // LANES, Tc["C"], Tc["S"], T["RrT"], 1.0, False, "ctx_rv_bwd")
    dPc = _fold_heads([(dKdc.astype(BF16), 0)], dPc, o_ak, nkvp, Tc["C"], Tc["S"], T["RaT"], T["D0T"], T["D1T"],
                      False, "fold_ak_ctx")
    dPc = _fold_heads([(dVdc.astype(BF16), 0)], dPc, o_av, nkvp, Tc["C"], Tc["S"], T["RaT"], T["D0T"], T["D1T"],
                      False, "fold_av_ctx")

    dx, st_m = _bwd_norm_mod([(dP, w_in)], x, dx1, norm_mix, sh_m, sc_m, "in_proj_bwd")
    _, st_mc = _bwd_norm_mod([(dPc, w_in)], ctx, None, norm_mix, sh_mc, sc_mc, "in_proj_ctx_bwd")
    g_w_in = _tn_matmul([(hx, dP), (hc, dPc)], "grad_w_in")
    on_grads(["w_in"], [g_w_in])

    a1 = acc1[:, :, :, 0].reshape(RET_HEADS, ACC_ROWS)
    a2 = acc2[:, :, :, 0].reshape(RET_HEADS, ACC_ROWS)
    dlam = (a1[:, 0] + a1[:, 2] + a1[:, 3] + a1[:, 4]) * lg[0]
    dmu = (a1[:, 1] + a1[:, 5] + a2[:, 0] + a2[:, 1]) * lg[1]
    d_sink = dsink[:, :4, 0].reshape(1, ATT_HEADS)

    nh = RET_HEADS
    assert 2 * nh + ATT_HEADS <= LOSS_LANE
    small = _pack_rows(
        [(st_m, 0, 2, 0, 0), (dgt_m, 0, 1, 2, 0), (st_f, 0, 2, 3, 0), (dgt_f, 0, 1, 5, 0), (st_mc, 0, 2, 6, 0),
         (st_m[2:3] + st_mc[2:3], 0, 1, 12, 0), (st_f, 2, 1, 13, 0), (d_norm_final, 0, 1, 14, 0),
         (dlam.reshape(1, nh), 0, 1, 15, 0), (dmu.reshape(1, nh), 0, 1, 15, nh), (d_sink, 0, 1, 15, 2 * nh),
         (loss[:, 0:1], 0, 1, 15, LOSS_LANE)], 16, D, "pack_small")
    return dict(grad_x=dx, small=small)


def _my_pos():
    return lax.axis_index("x"), lax.axis_index("y"), lax.axis_index("c")


def _other_chips(x, y):
    return [(1 - x, y), (x, 1 - y), (1 - x, 1 - y)]


def _remote(src, dst, ssem, rsem, dev):
    return pltpu.make_async_remote_copy(src_ref=src, dst_ref=dst, send_sem=ssem, recv_sem=rsem,
                                        device_id=dev, device_id_type=MESH)


def _allgather8(v, name):
    R, Cc = v.shape

    def body(v_ref, out_ref, send_sems, recv_sems):
        x, y, c = _my_pos()
        me = 4 * x + 2 * y + c
        out_ref[pl.ds(me, 1)] = v_ref[...][None]
        peers = []
        for j in range(1, N_DEV):
            peers.append((1 - x if (j >> 2) & 1 else x, 1 - y if (j >> 1) & 1 else y, 1 - c if j & 1 else c))
        copies = []
        for j, peer in enumerate(peers):
            cp = _remote(v_ref, out_ref.at[me], send_sems.at[j], recv_sems.at[j], peer)
            cp.start()
            copies.append(cp)
        for j, peer in enumerate(peers):
            pid = 4 * peer[0] + 2 * peer[1] + peer[2]
            _remote(v_ref, out_ref.at[pid], send_sems.at[j], recv_sems.at[j], peer).wait_recv()
        for cp in copies:
            cp.wait_send()

    return pl.pallas_call(
        body, name=name, out_shape=jax.ShapeDtypeStruct((N_DEV, R, Cc), v.dtype),
        in_specs=[pl.BlockSpec(memory_space=pltpu.VMEM)], out_specs=pl.BlockSpec(memory_space=pltpu.VMEM),
        scratch_shapes=[pltpu.SemaphoreType.DMA((N_DEV - 1,)), pltpu.SemaphoreType.DMA((N_DEV - 1,))])(v)


def _region(ref, k, half, shard_shape, axis):
    r, cs = shard_shape
    hr = r // 2
    if axis == 1:
        return ref.at[pl.ds(pl.multiple_of(half * hr, 16), hr), pl.ds(pl.multiple_of(k * cs, LANES), cs)]
    return ref.at[pl.ds(pl.multiple_of(k * r + half * hr, 16), hr), :]


def _full_shape(shard_shape, axis):
    r, cs = shard_shape
    return (r, N_CHIPS * cs) if axis == 1 else (N_CHIPS * r, cs)


def _half_pieces(ref, half, shard_shape, axis):
    r, cs = shard_shape
    hr = r // 2
    if axis == 1:
        return [ref.at[pl.ds(pl.multiple_of(half * hr, 16), hr), :]]
    return [ref.at[pl.ds(pl.multiple_of(k * r + half * hr, 16), hr), :] for k in range(N_CHIPS)]


def _rs_sibling(grads, shapes, axes, name):
    nw = len(grads)
    npc = max(1 if a == 1 else N_CHIPS for a in axes)

    def body(*refs):
        g_refs, out_refs = refs[:nw], refs[nw:2 * nw]
        send, recv = refs[2 * nw:]
        x, y, c = _my_pos()
        sib = (x, y, 1 - c)
        copies = []
        for w in range(nw):
            src = _half_pieces(g_refs[w], 1 - c, shapes[w], axes[w])
            dst = _half_pieces(out_refs[w], 1 - c, shapes[w], axes[w])
            for i, (s, d) in enumerate(zip(src, dst)):
                cp = _remote(s, d, send.at[w, i], recv.at[w, i], sib)
                cp.start()
                copies.append(cp)
        for w in range(nw):
            mine = _half_pieces(out_refs[w], c, shapes[w], axes[w])
            for i, d in enumerate(mine):
                _remote(d, d, send.at[w, i], recv.at[w, i], sib).wait_recv()
        for cp in copies:
            cp.wait_send()

    anyspec = pl.BlockSpec(memory_space=pl.ANY)
    return pl.pallas_call(
        body, name=name,
        out_shape=[jax.ShapeDtypeStruct(_full_shape(s, a), BF16) for s, a in zip(shapes, axes)],
        in_specs=[anyspec] * nw, out_specs=[anyspec] * nw,
        scratch_shapes=[pltpu.SemaphoreType.DMA((nw, npc)), pltpu.SemaphoreType.DMA((nw, npc))])(*grads)


def _half_block_spec(shard_shape, axis, tr):
    r, cs = shard_shape
    hr = r // 2
    if axis == 1:
        return pl.BlockSpec((tr, cs), lambda k, i, c_ref: (c_ref[0] * (hr // tr) + i, k))
    return pl.BlockSpec((tr, cs), lambda k, i, c_ref: (k * (r // tr) + c_ref[0] * (hr // tr) + i, 0))


def _add_halves(g, recv, cvec, shard_shape, axis, name):
    r, cs = shard_shape
    hr = r // 2
    tr = _tile(hr, 256, 16)

    def body(c_ref, a_ref, b_ref, o_ref):
        del c_ref
        o_ref[0] = (a_ref[...].astype(F32) + b_ref[...].astype(F32)).astype(BF16)

    spec = _half_block_spec(shard_shape, axis, tr)
    return pl.pallas_call(
        body, name=name,
        grid_spec=pltpu.PrefetchScalarGridSpec(
            num_scalar_prefetch=1, grid=(N_CHIPS, hr // tr), in_specs=[spec, spec],
            out_specs=pl.BlockSpec((1, tr, cs), lambda k, i, c_ref: (k, i, 0))),
        out_shape=jax.ShapeDtypeStruct((N_CHIPS, hr, cs), BF16),
        compiler_params=_params(2, False))(cvec, g, recv)


def _sum_chips(sums, landed, kc, name):
    _, hr, cs = sums.shape
    tr = _tile(hr, 256, 16)

    def body(kc_ref, own_ref, a_ref, b_ref, c_ref, o_ref):
        del kc_ref
        o_ref[...] = (own_ref[0].astype(F32) + a_ref[0].astype(F32)) + (b_ref[0].astype(F32) + c_ref[0].astype(F32))

    slot = lambda j: pl.BlockSpec((1, tr, cs), lambda i, kc_ref: ((kc_ref[0] + j) % N_CHIPS, i, 0))
    return pl.pallas_call(
        body, name=name,
        grid_spec=pltpu.PrefetchScalarGridSpec(
            num_scalar_prefetch=1, grid=(hr // tr,), in_specs=[slot(0), slot(1), slot(2), slot(3)],
            out_specs=pl.BlockSpec((tr, cs), lambda i, kc_ref: (kc_ref[1] * (hr // tr) + i, 0))),
        out_shape=jax.ShapeDtypeStruct((2 * hr, cs), F32),
        compiler_params=_params(1, False))(kc, sums, landed, landed, landed)


def _exchange_halves(shards, name):
    nw = len(shards)

    def body(*refs):
        out_refs = refs[nw:2 * nw]
        send, recv = refs[2 * nw:]
        x, y, c = _my_pos()
        sib = (x, y, 1 - c)
        copies = []
        for w in range(nw):
            hr = shards[w].shape[0] // 2
            mine = out_refs[w].at[pl.ds(pl.multiple_of(c * hr, 8), hr), :]
            cp = _remote(mine, mine, send.at[w], recv.at[w], sib)
            cp.start()
            copies.append(cp)
        for w in range(nw):
            hr = shards[w].shape[0] // 2
            other = out_refs[w].at[pl.ds(pl.multiple_of((1 - c) * hr, 8), hr), :]
            _remote(other, other, send.at[w], recv.at[w], sib).wait_recv()
        for cp in copies:
            cp.wait_send()

    anyspec = pl.BlockSpec(memory_space=pl.ANY)
    return pl.pallas_call(
        body, name=name,
        out_shape=[jax.ShapeDtypeStruct(s.shape, F32) for s in shards],
        in_specs=[anyspec] * nw, out_specs=[anyspec] * nw,
        input_output_aliases={w: w for w in range(nw)},
        scratch_shapes=[pltpu.SemaphoreType.DMA((nw,)), pltpu.SemaphoreType.DMA((nw,))])(*shards)


def _cast_into_full(w, kc, axis, name):
    r, cs = w.shape
    tr = _tile(r, 256, 16)

    def body(kc_ref, w_ref, o_ref):
        del kc_ref
        o_ref[...] = w_ref[...].astype(BF16)

    if axis == 1:
        ospec = pl.BlockSpec((tr, cs), lambda i, kc_ref: (i, kc_ref[0]))
    else:
        ospec = pl.BlockSpec((tr, cs), lambda i, kc_ref: (kc_ref[0] * (r // tr) + i, 0))
    return pl.pallas_call(
        body, name=name,
        grid_spec=pltpu.PrefetchScalarGridSpec(
            num_scalar_prefetch=1, grid=(r // tr,), in_specs=[pl.BlockSpec((tr, cs), lambda i, kc_ref: (i, 0))],
            out_specs=ospec),
        out_shape=jax.ShapeDtypeStruct(_full_shape((r, cs), axis), BF16),
        compiler_params=_params(1, False))(kc, w)


def _adam_math(w, g, m, v):
    m2 = ADAM_B1 * m + (1.0 - ADAM_B1) * g
    v2 = ADAM_B2 * v + (1.0 - ADAM_B2) * (g * g)
    m_hat = m2 / (1.0 - ADAM_B1 ** ADAM_STEP)
    v_hat = v2 / (1.0 - ADAM_B2 ** ADAM_STEP)
    delta = -ADAM_LR * (m_hat / (jnp.sqrt(v_hat) + ADAM_EPS) + ADAM_WD * w)
    return delta, m2, v2


def _adam(w, g, m, v, name):
    r, cs = w.shape
    tr = _tile(r, 256, 8)

    def body(w_ref, g_ref, m_ref, v_ref, d_ref, m2_ref, v2_ref):
        d, m2, v2 = _adam_math(w_ref[...], g_ref[...], m_ref[...], v_ref[...])
        d_ref[...] = d
        m2_ref[...] = m2
        v2_ref[...] = v2

    spec = pl.BlockSpec((tr, cs), lambda i: (i, 0))
    shp = jax.ShapeDtypeStruct((r, cs), F32)
    return pl.pallas_call(body, name=name, grid=(r // tr,), in_specs=[spec] * 4, out_specs=[spec] * 3,
                          out_shape=[shp, shp, shp], compiler_params=_params(1, False))(w, g, m, v)


def _mod_rows(a16, w, b, name):
    D, n = w.shape
    tn = _tile(n, 512)

    def body(a_ref, w_ref, b_ref, o_ref):
        a = a_ref[...]
        o_ref[...] = _nn((a * _sigmoid(a)).astype(BF16), w_ref[...].astype(BF16)) + b_ref[...]

    return pl.pallas_call(
        body, name=name, grid=(n // tn,),
        in_specs=[pl.BlockSpec((16, D), lambda j: (0, 0)), pl.BlockSpec((D, tn), lambda j: (0, j)),
                  pl.BlockSpec((1, tn), lambda j: (0, j))],
        out_specs=pl.BlockSpec((16, tn), lambda j: (0, j)),
        out_shape=jax.ShapeDtypeStruct((16, n), F32), compiler_params=_params(1, False))(a16, w, b)


def _w_mod_update(a16, d16, w, m, v):
    D, n = w.shape
    tn = _tile(n, 256)

    def body(a_ref, d_ref, w_ref, m_ref, v_ref, g_ref, dl_ref, m2_ref, v2_ref, p_ref):
        @pl.when(pl.program_id(0) == 0)
        def _():
            p_ref[...] = jnp.zeros_like(p_ref)
        a = a_ref[...]
        db = d_ref[...].astype(BF16)
        wv = w_ref[...]
        g = _tn((a * _sigmoid(a)).astype(BF16), db)
        g_ref[...] = g
        d, m2, v2 = _adam_math(wv, g, m_ref[...], v_ref[...])
        dl_ref[...] = d
        m2_ref[...] = m2
        v2_ref[...] = v2
        p_ref[...] += _nt(db, wv.astype(BF16))

    wspec = pl.BlockSpec((D, tn), lambda j: (0, j))
    shp = jax.ShapeDtypeStruct((D, n), F32)
    return pl.pallas_call(
        body, name="w_mod_update", grid=(n // tn,),
        in_specs=[pl.BlockSpec((16, D), lambda j: (0, 0)), pl.BlockSpec((16, tn), lambda j: (0, j)), wspec, wspec, wspec],
        out_specs=[wspec, wspec, wspec, wspec, pl.BlockSpec((16, D), lambda j: (0, 0))],
        out_shape=[shp, shp, shp, shp, jax.ShapeDtypeStruct((16, D), F32)],
        compiler_params=_params(1))(a16, d16, w, m, v)


def _sum_devices(g8, name):
    _, R, Cc = g8.shape

    def body(g_ref, o_ref):
        t = g_ref[0]
        for d in range(1, N_DEV):
            t = t + g_ref[d]
        o_ref[...] = t

    return pl.pallas_call(body, name=name, out_shape=jax.ShapeDtypeStruct((R, Cc), F32))(g8)


def _c_ctx_grad(parts, c_ctx):
    D = c_ctx.shape[1]

    def body(p_ref, c_ref, o_ref):
        t = p_ref[0]
        for k in range(1, N_CHIPS):
            t = t + p_ref[2 * k]
        cv = c_ref[...]
        sg = _sigmoid(cv)
        o_ref[...] = t * (sg * (1.0 + cv * (1.0 - sg)))

    return pl.pallas_call(body, name="c_ctx_grad", out_shape=jax.ShapeDtypeStruct((1, D), F32))(parts, c_ctx)


def _pack_rows(items, nrows, width, name):
    arrays, plan = [], []
    for a, r0, nr, d0, c0 in items:
        for ai, b in enumerate(arrays):
            if b is a:
                break
        else:
            ai = len(arrays)
            arrays.append(a)
        plan.append((ai, r0, nr, d0, c0, a.shape[1]))

    def body(*refs):
        o_ref = refs[-1]
        o_ref[...] = jnp.zeros_like(o_ref)
        for ai, r0, nr, d0, c0, w in plan:
            o_ref[d0:d0 + nr, c0:c0 + w] = refs[ai][r0:r0 + nr, :]

    return pl.pallas_call(body, name=name, out_shape=jax.ShapeDtypeStruct((nrows, width), F32))(*arrays)


HBM_SPEC = pl.BlockSpec(memory_space=pltpu.HBM)
SEM_SPEC = pl.BlockSpec(memory_space=pltpu.SEMAPHORE)
SPLIT_PARAMS = pltpu.CompilerParams(has_side_effects=pltpu.SideEffectType.DATAFLOW_SIDE_EFFECTING)


def _in_hbm(a):
    return pltpu.with_memory_space_constraint(a, pltpu.HBM)


def _ag_chips_start(fulls, shapes, axes, after, name):
    nw = len(fulls)

    def body(*refs):
        in_refs, send, recv, token = refs[:nw], refs[nw + 1], refs[nw + 2], refs[-1]
        x, y, c = _my_pos()
        k0 = 2 * x + y
        for w in range(nw):
            own = _region(in_refs[w], k0, c, shapes[w], axes[w])
            for j, ch in enumerate(_other_chips(x, y)):
                _remote(own, own, send.at[3 * w + j], recv.at[3 * w + j], (ch[0], ch[1], c)).start()
        token[...] = jnp.zeros_like(token)

    return pl.pallas_call(
        body, name=name,
        out_shape=(pltpu.SemaphoreType.DMA((3 * nw,)), pltpu.SemaphoreType.DMA((3 * nw,)),
                   *[pltpu.HBM(f.shape, f.dtype) for f in fulls], jax.ShapeDtypeStruct((8, LANES), F32)),
        in_specs=[HBM_SPEC] * nw + [pl.BlockSpec(memory_space=pl.ANY)],
        out_specs=(SEM_SPEC, SEM_SPEC, *[HBM_SPEC] * nw, pl.BlockSpec(memory_space=pltpu.VMEM)),
        input_output_aliases={w: 2 + w for w in range(nw)},
        compiler_params=SPLIT_PARAMS)(*[_in_hbm(f) for f in fulls], after)


def _ag_chips_wait(send, recv, fulls, shapes, axes, after, name):
    nw = len(fulls)

    def body(*refs):
        in_refs, send_ref, recv_ref = refs[:nw], refs[nw], refs[nw + 1]
        x, y, c = _my_pos()
        k0 = 2 * x + y
        for w in range(nw):
            own = _region(in_refs[w], k0, c, shapes[w], axes[w])
            for j, ch in enumerate(_other_chips(x, y)):
                got = _region(in_refs[w], 2 * ch[0] + ch[1], c, shapes[w], axes[w])
                cp = _remote(own, got, send_ref.at[3 * w + j], recv_ref.at[3 * w + j], (ch[0], ch[1], c))
                cp.wait_send()
                cp.wait_recv()

    return pl.pallas_call(
        body, name=name,
        out_shape=tuple(pltpu.HBM(f.shape, f.dtype) for f in fulls),
        in_specs=[HBM_SPEC] * nw + [SEM_SPEC, SEM_SPEC, pl.BlockSpec(memory_space=pl.ANY)],
        out_specs=tuple([HBM_SPEC] * nw),
        input_output_aliases={w: w for w in range(nw)},
        compiler_params=SPLIT_PARAMS)(*fulls, send, recv, after)


def _ag_forward(fulls, shapes, axes, name):
    nw = len(fulls)

    def body(*refs):
        out_refs = refs[nw:2 * nw]
        send, recv = refs[2 * nw:]
        x, y, c = _my_pos()
        sib = (x, y, 1 - c)
        chips = _other_chips(x, y)
        copies = []
        for w in range(nw):
            for j, ch in enumerate(chips):
                got = _region(out_refs[w], 2 * ch[0] + ch[1], c, shapes[w], axes[w])
                cp = _remote(got, got, send.at[w, j], recv.at[w, j], sib)
                cp.start()
                copies.append(cp)
        for w in range(nw):
            for j, ch in enumerate(chips):
                got = _region(out_refs[w], 2 * ch[0] + ch[1], 1 - c, shapes[w], axes[w])
                _remote(got, got, send.at[w, j], recv.at[w, j], sib).wait_recv()
        for cp in copies:
            cp.wait_send()

    anyspec = pl.BlockSpec(memory_space=pl.ANY)
    return pl.pallas_call(
        body, name=name,
        out_shape=[jax.ShapeDtypeStruct(f.shape, BF16) for f in fulls],
        in_specs=[anyspec] * nw, out_specs=[anyspec] * nw,
        input_output_aliases={w: w for w in range(nw)},
        scratch_shapes=[pltpu.SemaphoreType.DMA((nw, 3)), pltpu.SemaphoreType.DMA((nw, 3))])(*fulls)


def _rs_chips_start(sums, name):
    nw = len(sums)

    def body(*refs):
        s_refs, l_refs, send, recv, token = refs[:nw], refs[nw:2 * nw], refs[2 * nw], refs[2 * nw + 1], refs[-1]
        x, y, c = _my_pos()
        k0 = 2 * x + y
        for w in range(nw):
            for j, ch in enumerate(_other_chips(x, y)):
                _remote(s_refs[w].at[2 * ch[0] + ch[1]], l_refs[w].at[k0], send.at[3 * w + j], recv.at[3 * w + j],
                        (ch[0], ch[1], c)).start()
        token[...] = jnp.zeros_like(token)

    thru = [pltpu.HBM(s.shape, s.dtype) for s in sums]
    return pl.pallas_call(
        body, name=name,
        out_shape=(pltpu.SemaphoreType.DMA((3 * nw,)), pltpu.SemaphoreType.DMA((3 * nw,)), *thru, *thru,
                   jax.ShapeDtypeStruct((8, LANES), F32)),
        in_specs=[HBM_SPEC] * (2 * nw),
        out_specs=(SEM_SPEC, SEM_SPEC, *[HBM_SPEC] * (2 * nw), pl.BlockSpec(memory_space=pltpu.VMEM)),
        input_output_aliases={i: 2 + i for i in range(2 * nw)},
        compiler_params=SPLIT_PARAMS)(*[_in_hbm(s) for s in sums], *[_in_hbm(lax.empty(s.shape, s.dtype)) for s in sums])


def _rs_chips_wait(send, recv, sums, lands, after, name):
    nw = len(sums)

    def body(*refs):
        s_refs, l_refs, send_ref, recv_ref = refs[:nw], refs[nw:2 * nw], refs[2 * nw], refs[2 * nw + 1]
        x, y, c = _my_pos()
        for w in range(nw):
            for j, ch in enumerate(_other_chips(x, y)):
                kj = 2 * ch[0] + ch[1]
                cp = _remote(s_refs[w].at[kj], l_refs[w].at[kj], send_ref.at[3 * w + j], recv_ref.at[3 * w + j],
                             (ch[0], ch[1], c))
                cp.wait_send()
                cp.wait_recv()

    thru = tuple(pltpu.HBM(s.shape, s.dtype) for s in sums)
    return pl.pallas_call(
        body, name=name, out_shape=thru + thru,
        in_specs=[HBM_SPEC] * (2 * nw) + [SEM_SPEC, SEM_SPEC, pl.BlockSpec(memory_space=pl.ANY)],
        out_specs=tuple([HBM_SPEC] * (2 * nw)),
        input_output_aliases={i: i for i in range(2 * nw)},
        compiler_params=SPLIT_PARAMS)(*sums, *lands, send, recv, after)


LOSS_LANE = 64


def kernel(x, c, ctx, c_ctx, w_mod, b_mod, norm_mix, norm_ffn, w_in, ret_decay, attn_sink, w_out, w_gate, w_up, w_down, norm_final, loss_target, m_c_ctx, m_w_mod, m_b_mod, m_norm_mix, m_norm_ffn, m_w_in, m_ret_decay, m_attn_sink, m_w_out, m_w_gate, m_w_up, m_w_down, m_norm_final, v_c_ctx, v_w_mod, v_b_mod, v_norm_mix, v_norm_ffn, v_w_in, v_ret_decay, v_attn_sink, v_w_out, v_w_gate, v_w_up, v_w_down, v_norm_final):
    D = x.shape[-1]
    n3 = w_mod.shape[-1]
    xi, yi, ci = _my_pos()
    b = 4 * xi + 2 * yi + ci
    k0 = 2 * xi + yi
    cvec = jnp.reshape(ci, (1,)).astype(jnp.int32)
    kc = jnp.stack([k0, ci]).astype(jnp.int32)

    dense = [("w_in", w_in[0], 1), ("w_out", w_out[0], 0), ("w_gate", w_gate[0], 1), ("w_up", w_up[0], 1),
             ("w_down", w_down[0], 0)]
    axes = [a for _, _, a in dense]
    shapes = [w.shape for _, w, _ in dense]
    own_in = _cast_into_full(dense[0][1], kc, axes[0], "cast_w_in")
    agi = _ag_chips_start([own_in], shapes[:1], axes[:1], own_in, "ag_in_start")
    agi_tok = agi[-1][0:1, 0:1]
    own16 = [_cast_into_full(w, kc, a, "cast_" + n) for n, w, a in dense[1:]]

    c_all = _allgather8(c + agi_tok, "gather_c").reshape(N_DEV, D)
    c_ctx2 = c_ctx.reshape(1, D)
    a16 = _pack_rows([(c_all, 0, N_DEV, 0, 0), (c_ctx2, 0, 1, N_DEV, 0)], 16, D, "pack_cond")
    b_cols = lax.dynamic_slice_in_dim(b_mod, k0 * n3, n3, axis=1)
    mod16 = _mod_rows(a16, w_mod[0], b_cols, "mod_rows")
    mod_all = _allgather8(mod16, "gather_mod")

    (f_in,) = _ag_forward(list(_ag_chips_wait(agi[0], agi[1], [agi[2]], shapes[:1], axes[:1], mod_all, "ag_in_wait")),
                          shapes[:1], axes[:1], "ag_in_forward")
    ag = _ag_chips_start(own16, shapes[1:], axes[1:], f_in, "ag_rest_start")
    ag_send, ag_recv, ag_thru, ag_tok = ag[0], ag[1], list(ag[2:-1]), ag[-1][0:1, 0:1]

    def rest_weights(after):
        landed_w = _ag_chips_wait(ag_send, ag_recv, ag_thru, shapes[1:], axes[1:], after, "ag_rest_wait")
        return _ag_forward(list(landed_w), shapes[1:], axes[1:], "ag_rest_forward")
    mine = jnp.stack([lax.dynamic_index_in_dim(mod_all, 2 * k + ci, 0, keepdims=False) for k in range(N_CHIPS)])
    mod = lax.dynamic_index_in_dim(mine, b, 1, keepdims=False).reshape(6, D)
    modc = mine[:, N_DEV].reshape(6, D)

    lg = -jnp.exp(ret_decay[0])

    index = {n: i for i, (n, _, _) in enumerate(dense)}
    pending, done = [], {}

    def on_grads(names, gs):
        ids = [index[n] for n in names]
        shp, axs = [shapes[i] for i in ids], [axes[i] for i in ids]
        from_sib = _rs_sibling(gs, shp, axs, "rs_sibling_" + names[0])
        sums = [_add_halves(g, r, cvec, s, a, "add_halves_" + n) for g, r, s, a, n in zip(gs, from_sib, shp, axs, names)]
        st = _rs_chips_start(sums, "rs_chips_start_" + names[0])
        nw = len(names)
        pending.append((names, st[0], st[1], list(st[2:2 + nw]), list(st[2 + nw:2 + 2 * nw])))
        return st[-1][0:1, 0:1]

    out = _local_step(x[0], ctx[0], loss_target[0], mod, modc, norm_mix + ag_tok, norm_ffn, norm_final.reshape(1, D), lg,
                      attn_sink, f_in, rest_weights, on_grads)

    def finish(group, after):
        names, send, recv, sums, lands = group
        res = _rs_chips_wait(send, recv, sums, lands, after, "rs_chips_wait_" + names[0])
        return [_sum_chips(res[i], res[len(names) + i], kc, "sum_chips_" + n) for i, n in enumerate(names)]

    assert pending[-1][0] == ["w_in"]
    rest_names = [n for g in pending[:-1] for n in g[0]]
    rest_halves = [h for g in pending[:-1] for h in finish(g, out["grad_x"])]
    g_rest = dict(zip(rest_names, _exchange_halves(rest_halves, "exchange_halves_rest")))

    nh = 2 * RET_HEADS
    small_all = _allgather8(out["small"], "gather_small")
    tot = _sum_devices(small_all, "sum_small")
    g_b_mod = (tot[0:6] + tot[6:12]).reshape(1, 6 * D)
    dmodc_tot = tot[6:12].reshape(1, 6 * D)
    dmod_rows = small_all[:, 0:6].reshape(N_DEV, 6 * D)
    d16 = _pack_rows([(dmod_rows, 0, N_DEV, 0, 0), (dmodc_tot, 0, 1, N_DEV, 0)], 16, 6 * D, "pack_dmod")
    d16 = lax.dynamic_slice_in_dim(d16, k0 * n3, n3, axis=1)
    g_w_mod, dl_w_mod, m2_w_mod, v2_w_mod, part = _w_mod_update(a16, d16, w_mod[0], m_w_mod[0], v_w_mod[0])
    part_all = _allgather8(part[N_DEV:N_DEV + 1], "gather_c_ctx")
    g_c_ctx = _c_ctx_grad(part_all, c_ctx2)
    loss = tot[15, LOSS_LANE]
    g_ret_decay = tot[15, :nh].reshape(1, 2, RET_HEADS)
    g_sink = tot[15, nh:nh + ATT_HEADS].reshape(1, ATT_HEADS)

    def pack(cc, bm, nm, nf, nfin, rd, sk, name):
        rd2 = rd.reshape(2, RET_HEADS)
        return _pack_rows([(bm.reshape(6, D), 0, 6, 0, 0), (cc.reshape(1, D), 0, 1, 6, 0), (nm.reshape(1, D), 0, 1, 7, 0),
                           (nf.reshape(1, D), 0, 1, 8, 0), (nfin.reshape(1, D), 0, 1, 9, 0),
                           (rd2, 0, 1, 10, 0), (rd2, 1, 1, 10, RET_HEADS), (sk.reshape(1, ATT_HEADS), 0, 1, 10, nh)],
                          16, D, name)

    w_s = pack(c_ctx, b_mod, norm_mix, norm_ffn, norm_final, ret_decay, attn_sink, "pack_w")
    g_s = _pack_rows([(g_b_mod.reshape(6, D), 0, 6, 0, 0), (g_c_ctx, 0, 1, 6, 0), (tot, 12, 3, 7, 0),
                      (tot[15:16, 0:nh + ATT_HEADS], 0, 1, 10, 0)], 16, D, "pack_g")
    m_s = pack(m_c_ctx, m_b_mod, m_norm_mix, m_norm_ffn, m_norm_final, m_ret_decay, m_attn_sink, "pack_m")
    v_s = pack(v_c_ctx, v_b_mod, v_norm_mix, v_norm_ffn, v_norm_final, v_ret_decay, v_attn_sink, "pack_v")
    small_upd = _adam(w_s, g_s, m_s, v_s, "adam_small")

    def unpack(t):
        return dict(b_mod=t[0:6].reshape(1, 6 * D), c_ctx=t[6], norm_mix=t[7:8], norm_ffn=t[8:9], norm_final=t[9],
                    ret_decay=t[10, :nh].reshape(1, 2, RET_HEADS), attn_sink=t[10, nh:nh + ATT_HEADS].reshape(1, ATT_HEADS))

    dense_w = dict(w_in=(w_in, m_w_in, v_w_in), w_out=(w_out, m_w_out, v_w_out), w_gate=(w_gate, m_w_gate, v_w_gate),
                   w_up=(w_up, m_w_up, v_w_up), w_down=(w_down, m_w_down, v_w_down))
    grads = dict(unpack(g_s), w_mod=g_w_mod[None])
    upd = [dict(unpack(t)) for t in small_upd]
    upd[0]["w_mod"], upd[1]["w_mod"], upd[2]["w_mod"] = dl_w_mod[None], m2_w_mod[None], v2_w_mod[None]
    def update(n, g):
        w_, m_, v_ = dense_w[n]
        res = _adam(w_[0], g, m_[0], v_[0], "adam_" + n)
        grads[n] = g[None]
        for u, r_ in zip(upd, res):
            u[n] = r_[None]
        return res[0]

    for n in rest_names:
        last = update(n, g_rest[n])
    (g_in,) = _exchange_halves(finish(pending[-1], last), "exchange_halves_in")
    update("w_in", g_in)

    order = ['c_ctx', 'w_mod', 'b_mod', 'norm_mix', 'norm_ffn', 'w_in', 'ret_decay', 'attn_sink', 'w_out', 'w_gate',
             'w_up', 'w_down', 'norm_final']
    outs = [loss, out["grad_x"][None]] + [grads[n] for n in order]
    for u in upd:
        outs += [u[n] for n in order]
    return tuple(outs)
```

```python
import functools
import numpy as np
import jax
import jax.numpy as jnp
from jax import lax
from jax.experimental import pallas as pl
from jax.experimental.pallas import tpu as pltpu

F32 = jnp.float32
BF16 = jnp.bfloat16

RET_HEADS = 8
RET_DK = 64
RET_DV = 128
CHUNK = 128
ATT_HEADS = 16
ATT_KV_HEADS = 4
ATT_DH = 64
GRID_W = 64
ROPE_BASE = 10000.0
NORM_EPS = 1e-6
ADAM_LR = 0.001
ADAM_B1 = 0.9
ADAM_B2 = 0.999
ADAM_EPS = 1e-08
ADAM_WD = 0.01
ADAM_STEP = 10
NEG = -1e30
LANES = 128
VMEM_LIMIT = 56 * 1024 * 1024
ROWS_PER_LATCH = 1024
MESH = pl.DeviceIdType.MESH
N_CHIPS = 4
N_DEV = 8


def _nn(a, b):
    return jnp.dot(a, b, preferred_element_type=F32)


def _nt(a, b):
    return lax.dot_general(a, b, (((1,), (1,)), ((), ())), preferred_element_type=F32)


def _tn(a, b):
    return lax.dot_general(a, b, (((0,), (0,)), ((), ())), preferred_element_type=F32)


def _tile(n, pref, unit=LANES):
    t = min(n, pref)
    t -= t % unit
    while t > unit and n % t:
        t -= unit
    if t <= 0 or n % t:
        return n
    return t


def _params(ndim, vmem=True):
    return pltpu.CompilerParams(dimension_semantics=("arbitrary",) * ndim,
                                vmem_limit_bytes=VMEM_LIMIT if vmem else None)


def _sigmoid(x):
    return 0.5 * jnp.tanh(0.5 * x) + 0.5


def _fsum(x):
    return jnp.sum(jnp.sum(x, axis=1, keepdims=True), axis=0, keepdims=True)


def _rope_tables(L):
    lane = np.arange(LANES)
    d = lane % 64
    inv_r = jnp.asarray(ROPE_BASE, F32) ** (-jnp.arange(32, dtype=F32) / 32)
    t = jnp.arange(L)
    ang_r = t.astype(F32)[:, None] * jnp.tile(inv_r, LANES // 32)[None, :]
    Rr = np.zeros((LANES, LANES), np.float32)
    for l in range(LANES):
        if d[l] < 32:
            Rr[l + 32, l] = -1.0
        else:
            Rr[l - 32, l] = 1.0
    inv_a = jnp.asarray(ROPE_BASE, F32) ** (-jnp.arange(16, dtype=F32) / 16)
    rows = (t // GRID_W).astype(F32)
    cols = (t % GRID_W).astype(F32)
    dd = d % 32
    pos = jnp.where(jnp.asarray(d < 32)[None, :], rows[:, None], cols[:, None])
    ang_a = pos * jnp.tile(inv_a, LANES // 16)[None, :]
    Ra = np.zeros((LANES, LANES), np.float32)
    for l in range(LANES):
        if dd[l] < 16:
            Ra[l + 16, l] = -1.0
        else:
            Ra[l - 16, l] = 1.0
    D0 = np.zeros((LANES, LANES), np.float32)
    D1 = np.zeros((LANES, LANES), np.float32)
    for l in range(LANES):
        D0[l % 64, l] = 1.0
        D1[64 + l % 64, l] = 1.0
    return dict(
        Cr=jnp.cos(ang_r), Sr=jnp.sin(ang_r), Rr=jnp.asarray(Rr, BF16), RrT=jnp.asarray(Rr.T, BF16),
        Ca=jnp.cos(ang_a), Sa=jnp.sin(ang_a), Ra=jnp.asarray(Ra, BF16), RaT=jnp.asarray(Ra.T, BF16),
        D0=jnp.asarray(D0, BF16), D1=jnp.asarray(D1, BF16),
        D0T=jnp.asarray(D0.T, BF16), D1T=jnp.asarray(D1.T, BF16))


def _norm_mod(xf, g, sh, sc):
    r = lax.rsqrt(jnp.mean(xf * xf, axis=-1, keepdims=True) + NORM_EPS)
    return (xf * r * g) * (1.0 + sc) + sh


def _norm_mod_matmul(x, g, sh, sc, w, name):
    M, D = x.shape
    N = w.shape[1]
    tm, tn = _tile(M, ROWS_PER_LATCH, 8), _tile(N, 768)

    def body(x_ref, g_ref, sh_ref, sc_ref, w_ref, p_ref, h_ref, hs):
        @pl.when(pl.program_id(1) == 0)
        def _():
            hb = _norm_mod(x_ref[...], g_ref[...], sh_ref[...], sc_ref[...]).astype(BF16)
            hs[...] = hb
            h_ref[...] = hb
        p_ref[...] = _nn(hs[...], w_ref[...]).astype(BF16)

    vec = pl.BlockSpec((1, D), lambda i, j: (0, 0))
    return pl.pallas_call(
        body, name=name, grid=(M // tm, N // tn),
        in_specs=[pl.BlockSpec((tm, D), lambda i, j: (i, 0)), vec, vec, vec,
                  pl.BlockSpec((D, tn), lambda i, j: (0, j))],
        out_specs=[pl.BlockSpec((tm, tn), lambda i, j: (i, j)), pl.BlockSpec((tm, D), lambda i, j: (i, 0))],
        out_shape=[jax.ShapeDtypeStruct((M, N), BF16), jax.ShapeDtypeStruct((M, D), BF16)],
        scratch_shapes=[pltpu.VMEM((tm, D), BF16)],
        compiler_params=_params(2))(x, g, sh, sc, w)


def _proj_residual(a, w, xres, gt, name):
    M, K = a.shape
    N = w.shape[1]
    tm, tn = _tile(M, ROWS_PER_LATCH, 8), _tile(N, 1024 if K <= 2048 else 512)

    def body(a_ref, w_ref, x_ref, gt_ref, xo_ref, o_ref):
        o = _nn(a_ref[...], w_ref[...])
        o_ref[...] = o.astype(BF16)
        xo_ref[...] = x_ref[...] + gt_ref[...] * o

    return pl.pallas_call(
        body, name=name, grid=(M // tm, N // tn),
        in_specs=[pl.BlockSpec((tm, K), lambda i, j: (i, 0)), pl.BlockSpec((K, tn), lambda i, j: (0, j)),
                  pl.BlockSpec((tm, tn), lambda i, j: (i, j)), pl.BlockSpec((1, tn), lambda i, j: (0, j))],
        out_specs=[pl.BlockSpec((tm, tn), lambda i, j: (i, j)), pl.BlockSpec((tm, tn), lambda i, j: (i, j))],
        out_shape=[jax.ShapeDtypeStruct((M, N), F32), jax.ShapeDtypeStruct((M, N), BF16)],
        compiler_params=_params(2))(a, w, xres, gt)


def _ffn_in(x1, g, sh, sc, wg, wu):
    M, D = x1.shape
    N = wg.shape[1]
    tm, tn = _tile(M, ROWS_PER_LATCH, 8), _tile(N, 512)

    def body(x_ref, g_ref, sh_ref, sc_ref, wg_ref, wu_ref, G_ref, U_ref, A_ref, h_ref, hs):
        @pl.when(pl.program_id(1) == 0)
        def _():
            hb = _norm_mod(x_ref[...], g_ref[...], sh_ref[...], sc_ref[...]).astype(BF16)
            hs[...] = hb
            h_ref[...] = hb
        G = _nn(hs[...], wg_ref[...])
        U = _nn(hs[...], wu_ref[...])
        G_ref[...] = G.astype(BF16)
        U_ref[...] = U.astype(BF16)
        A_ref[...] = (G * _sigmoid(G) * U).astype(BF16)

    vec = pl.BlockSpec((1, D), lambda i, j: (0, 0))
    wspec = pl.BlockSpec((D, tn), lambda i, j: (0, j))
    ospec = pl.BlockSpec((tm, tn), lambda i, j: (i, j))
    big = jax.ShapeDtypeStruct((M, N), BF16)
    return pl.pallas_call(
        body, name="ffn_in", grid=(M // tm, N // tn),
        in_specs=[pl.BlockSpec((tm, D), lambda i, j: (i, 0)), vec, vec, vec, wspec, wspec],
        out_specs=[ospec, ospec, ospec, pl.BlockSpec((tm, D), lambda i, j: (i, 0))],
        out_shape=[big, big, big, jax.ShapeDtypeStruct((M, D), BF16)],
        scratch_shapes=[pltpu.VMEM((tm, D), BF16)],
        compiler_params=_params(2))(x1, g, sh, sc, wg, wu)


def _final(x2, gn, tgt):
    M, D = x2.shape
    tm = _tile(M, 256, 8)

    def body(x_ref, g_ref, t_ref, dx_ref, loss_ref, dg_ref):
        @pl.when(pl.program_id(0) == 0)
        def _():
            loss_ref[...] = jnp.zeros_like(loss_ref)
            dg_ref[...] = jnp.zeros_like(dg_ref)
        x = x_ref[...]
        g = g_ref[...]
        r = lax.rsqrt(jnp.mean(x * x, axis=-1, keepdims=True) + NORM_EPS)
        xh = x * r
        e = xh * g - t_ref[...]
        loss_ref[...] += (0.5 / D) * _fsum(e * e)
        dy = e * (1.0 / D)
        dg_ref[...] += jnp.sum(dy * xh, axis=0, keepdims=True)
        dxh = dy * g
        dx_ref[...] = r * (dxh - xh * jnp.mean(dxh * xh, axis=-1, keepdims=True))

    row = pl.BlockSpec((tm, D), lambda i: (i, 0))
    return pl.pallas_call(
        body, name="final_loss", grid=(M // tm,),
        in_specs=[row, pl.BlockSpec((1, D), lambda i: (0, 0)), row],
        out_specs=[row, pl.BlockSpec((1, LANES), lambda i: (0, 0)), pl.BlockSpec((1, D), lambda i: (0, 0))],
        out_shape=[jax.ShapeDtypeStruct((M, D), F32), jax.ShapeDtypeStruct((1, LANES), F32),
                   jax.ShapeDtypeStruct((1, D), F32)],
        compiler_params=_params(1))(x2, gn, tgt)


def _col_group(blk0, nblk):
    return int(np.gcd(blk0, nblk)) if blk0 else nblk


def _rope_cols(src, blk0, nblk, Ct, St, R, scale, rope, name):
    M = src.shape[0]
    tm = _tile(M, 512, 8)
    wb = _col_group(blk0, nblk)

    def body(x_ref, c_ref, s_ref, r_ref, o_ref):
        for j in range(wb):
            cols = slice(j * LANES, (j + 1) * LANES)
            x = x_ref[:, cols]
            xf = x.astype(F32)
            if rope:
                xf = xf * c_ref[...] + _nn(x.astype(BF16), r_ref[...]) * s_ref[...]
            o_ref[:, cols] = (xf * scale).astype(BF16)

    tab = pl.BlockSpec((tm, LANES), lambda i, j: (i, 0))
    return pl.pallas_call(
        body, name=name, grid=(M // tm, nblk // wb),
        in_specs=[pl.BlockSpec((tm, wb * LANES), lambda i, j: (i, blk0 // wb + j)), tab, tab,
                  pl.BlockSpec((LANES, LANES), lambda i, j: (0, 0))],
        out_specs=pl.BlockSpec((tm, wb * LANES), lambda i, j: (i, j)),
        out_shape=jax.ShapeDtypeStruct((M, nblk * LANES), BF16),
        compiler_params=_params(2, False))(src, Ct, St, R)


def _dup_heads(src, blk0, npair, Ct, St, R, D0, D1, rope, name):
    M = src.shape[0]
    tm = _tile(M, 512, 8)

    def body(x_ref, c_ref, s_ref, r_ref, d0_ref, d1_ref, o_ref):
        x = x_ref[...]
        if rope:
            x = (x.astype(F32) * c_ref[...] + _nn(x, r_ref[...]) * s_ref[...]).astype(BF16)
        o_ref[0] = _nn(x, d0_ref[...]).astype(BF16)
        o_ref[1] = _nn(x, d1_ref[...]).astype(BF16)

    tab = pl.BlockSpec((tm, LANES), lambda i, p: (i, 0))
    mat = pl.BlockSpec((LANES, LANES), lambda i, p: (0, 0))
    return pl.pallas_call(
        body, name=name, grid=(M // tm, npair),
        in_specs=[pl.BlockSpec((tm, LANES), lambda i, p: (i, blk0 + p)), tab, tab, mat, mat, mat],
        out_specs=pl.BlockSpec((2, tm, LANES), lambda i, p: (p, i, 0)),
        out_shape=jax.ShapeDtypeStruct((2 * npair, M, LANES), BF16),
        compiler_params=_params(2, False))(src, Ct, St, R, D0, D1)


def _unrope_cols(dsrc, dst, blk0, nblk, Ct, St, RT, scale, rope, name):
    M = dsrc.shape[0]
    tm = _tile(M, 512, 8)
    wb = _col_group(blk0, nblk)

    def body(x_ref, c_ref, s_ref, r_ref, dst_ref, o_ref):
        del dst_ref
        for j in range(wb):
            cols = slice(j * LANES, (j + 1) * LANES)
            xf = x_ref[:, cols].astype(F32)
            if rope:
                xf = xf * c_ref[...] + _nn((xf * s_ref[...]).astype(BF16), r_ref[...])
            o_ref[:, cols] = (xf * scale).astype(BF16)

    tab = pl.BlockSpec((tm, LANES), lambda i, j: (i, 0))
    return pl.pallas_call(
        body, name=name, grid=(M // tm, nblk // wb),
        in_specs=[pl.BlockSpec((tm, wb * LANES), lambda i, j: (i, j)), tab, tab,
                  pl.BlockSpec((LANES, LANES), lambda i, j: (0, 0)),
                  pl.BlockSpec(memory_space=pl.ANY)],
        out_specs=pl.BlockSpec((tm, wb * LANES), lambda i, j: (i, blk0 // wb + j)),
        out_shape=jax.ShapeDtypeStruct(dst.shape, dst.dtype),
        input_output_aliases={4: 0},
        compiler_params=_params(2, False))(dsrc, Ct, St, RT, dst)


def _fold_heads(parts, dst, blk0, npair, Ct, St, RT, D0T, D1T, rope, name):
    M = parts[0][0].shape[1]
    nb = M // CHUNK
    R = _tile(M, 1024, CHUNK)
    rb = R // CHUNK
    nrefs = sum(1 if s == 0 else 2 for _, s in parts)

    def body(*refs):
        part_refs = list(refs[:nrefs])
        c_ref, s_ref, r_ref, d0_ref, d1_ref, dst_ref, o_ref = refs[nrefs:]
        del dst_ref
        i = pl.program_id(0)
        tot = [jnp.zeros((R, LANES), F32), jnp.zeros((R, LANES), F32)]
        for _, shift in parts:
            main = part_refs.pop(0)
            if shift == 0:
                for e in range(2):
                    tot[e] = tot[e] + main[e].astype(F32)
                continue
            edge = part_refs.pop(0)
            ok = (i + 1) * rb <= nb - 1 if shift > 0 else i > 0
            for e in range(2):
                ed = jnp.where(ok, edge[e].astype(F32), 0.0)
                if rb == 1:
                    tot[e] = tot[e] + ed
                elif shift > 0:
                    tot[e] = tot[e] + jnp.concatenate([main[e, CHUNK:, :].astype(F32), ed], axis=0)
                else:
                    tot[e] = tot[e] + jnp.concatenate([ed, main[e, :R - CHUNK, :].astype(F32)], axis=0)
        f = _nn(tot[0].astype(BF16), d0_ref[...]) + _nn(tot[1].astype(BF16), d1_ref[...])
        if rope:
            f = f * c_ref[...] + _nn((f * s_ref[...]).astype(BF16), r_ref[...])
        o_ref[...] = f.astype(BF16)

    in_specs, args = [], []
    for a, shift in parts:
        assert shift in (-1, 0, 1)
        in_specs.append(pl.BlockSpec((2, R, LANES), lambda i, p: (p, i, 0)))
        args.append(a)
        if shift > 0:
            in_specs.append(pl.BlockSpec((2, CHUNK, LANES), lambda i, p: (p, jnp.minimum((i + 1) * rb, nb - 1), 0)))
            args.append(a)
        elif shift < 0:
            in_specs.append(pl.BlockSpec((2, CHUNK, LANES), lambda i, p: (p, jnp.maximum(i * rb - 1, 0), 0)))
            args.append(a)
    tab = pl.BlockSpec((R, LANES), lambda i, p: (i, 0))
    mat = pl.BlockSpec((LANES, LANES), lambda i, p: (0, 0))
    return pl.pallas_call(
        body, name=name, grid=(M // R, npair),
        in_specs=in_specs + [tab, tab, mat, mat, mat, pl.BlockSpec(memory_space=pl.ANY)],
        out_specs=pl.BlockSpec((R, LANES), lambda i, p: (i, blk0 + p)),
        out_shape=jax.ShapeDtypeStruct(dst.shape, dst.dtype),
        input_output_aliases={nrefs + 5: 0},
        compiler_params=_params(2, False))(*args, Ct, St, RT, D0T, D1T, dst)


def _head_masks():
    lane = lax.broadcasted_iota(jnp.int32, (1, LANES), 1)
    return [lane < 64, lane >= 64]


def _decay_vecs(lam, mu):
    i = lax.broadcasted_iota(jnp.int32, (CHUNK, 1), 0).astype(F32)
    return dict(qf=jnp.exp(lam * (i + 1.0)), kf=jnp.exp(lam * (CHUNK - 1.0 - i)),
                qb=jnp.exp(mu * (CHUNK - i)), kb=jnp.exp(mu * i),
                gf=jnp.exp(lam * float(CHUNK)), gb=jnp.exp(mu * float(CHUNK)), i=i)


def _decay_mask(lam, mu):
    r = lax.broadcasted_iota(jnp.int32, (CHUNK, CHUNK), 0)
    c = lax.broadcasted_iota(jnp.int32, (CHUNK, CHUNK), 1)
    rel = (r - c).astype(F32)
    low = rel >= 0.0
    mf = jnp.exp(lam * jnp.maximum(rel, 0.0))
    mb = jnp.exp(mu * jnp.maximum(-rel, 0.0))
    return jnp.where(low, mf, mb), rel, low


def _lam_of(lg_ref, row, idx):
    return jnp.full((1, 1), lg_ref[row, idx], F32)


def _group_index(pair_blk, npairs):
    assert pair_blk % npairs == 0
    return pair_blk // npairs


def _ret_states_fwd(Kr, P, Krc, Pc, lg, rv_blk, npairs):
    L = Kr.shape[0]
    Lc = Krc.shape[0]
    N, ncc = L // CHUNK, Lc // CHUNK
    rv_grp = _group_index(rv_blk, npairs)

    heads = [(p, h) for p in range(npairs) for h in range(2)]
    kcols = lambda p: slice(p * LANES, (p + 1) * LANES)
    vcols = lambda p, h: slice((2 * p + h) * LANES, (2 * p + h + 1) * LANES)

    def body(lg_ref, k_ref, v_ref, kc_ref, vc_ref, sf_ref, S):
        n = pl.program_id(0)
        masks = _head_masks()

        @pl.when(n == 0)
        def _():
            for p, h in heads:
                lam = _lam_of(lg_ref, 0, 2 * p + h)
                dv = _decay_vecs(lam, lam)
                s = jnp.zeros((LANES, LANES), F32)
                for cc in range(ncc):
                    rows = slice(cc * CHUNK, (cc + 1) * CHUNK)
                    kw = jnp.where(masks[h], kc_ref[rows, kcols(p)].astype(F32) * dv["kf"], 0.0).astype(BF16)
                    s = dv["gf"] * s + _tn(kw, vc_ref[rows, vcols(p, h)])
                S[p, h] = s

        for p, h in heads:
            lam = _lam_of(lg_ref, 0, 2 * p + h)
            dv = _decay_vecs(lam, lam)
            s = S[p, h]
            sf_ref[p, 0, h] = s.astype(BF16)
            kw = jnp.where(masks[h], k_ref[:, kcols(p)].astype(F32) * dv["kf"], 0.0).astype(BF16)
            S[p, h] = dv["gf"] * s + _tn(kw, v_ref[:, vcols(p, h)])

    wq, wv = npairs * LANES, npairs * 2 * LANES
    return pl.pallas_call(
        body, name="ret_states_fwd", grid=(N,),
        in_specs=[pl.BlockSpec(memory_space=pltpu.SMEM),
                  pl.BlockSpec((CHUNK, wq), lambda n: (n, 0)),
                  pl.BlockSpec((CHUNK, wv), lambda n: (n, rv_grp)),
                  pl.BlockSpec((Lc, wq), lambda n: (0, 0)),
                  pl.BlockSpec((Lc, wv), lambda n: (0, rv_grp))],
        out_specs=pl.BlockSpec((npairs, 1, 2, LANES, LANES), lambda n: (0, n, 0, 0, 0)),
        out_shape=jax.ShapeDtypeStruct((npairs, N, 2, LANES, LANES), BF16),
        scratch_shapes=[pltpu.VMEM((npairs, 2, LANES, LANES), F32)],
        compiler_params=_params(1, False))(lg, Kr, P, Krc, Pc)


def _ret_chunk_fwd(q, k, v, sf, sb, hm, lam, mu):
    dv = _decay_vecs(lam, mu)
    Mk, rel, low = _decay_mask(lam, mu)
    qm = jnp.where(hm, q, jnp.zeros_like(q))
    qmf = qm.astype(F32)
    A = _nt(qm, k)
    Am = A * Mk
    Amb = Am.astype(BF16)
    Qf = (qmf * dv["qf"]).astype(BF16)
    Qb = (qmf * dv["qb"]).astype(BF16)
    O = _nn(Amb, v) + _nn(Qf, sf) + _nn(Qb, sb)
    return dict(dv=dv, Mk=Mk, rel=rel, low=low, qm=qm, Am=Am, Amb=Amb, Qf=Qf, Qb=Qb, O=O)


def _ret_out_fwd(Qr, Kr, P, Krc, Pc, SF, lg, rv_blk, rg_blk, npairs, d_mix):
    L = Qr.shape[0]
    Lc = Krc.shape[0]
    N, ncc = L // CHUNK, Lc // CHUNK

    rv_grp, rg_grp = _group_index(rv_blk, npairs), _group_index(rg_blk, npairs)
    heads = [(p, h) for p in range(npairs) for h in range(2)]
    kcols = lambda p: slice(p * LANES, (p + 1) * LANES)
    vcols = lambda p, h: slice((2 * p + h) * LANES, (2 * p + h + 1) * LANES)

    def body(lg_ref, q_ref, k_ref, v_ref, g_ref, sf_ref, kc_ref, vc_ref, y_ref, sb_ref, S):
        n = pl.program_id(0)
        masks = _head_masks()

        @pl.when(n == 0)
        def _():
            for p, h in heads:
                mu = _lam_of(lg_ref, 1, 2 * p + h)
                dvb = _decay_vecs(mu, mu)
                s = jnp.zeros((LANES, LANES), F32)
                for cc in reversed(range(ncc)):
                    rows = slice(cc * CHUNK, (cc + 1) * CHUNK)
                    kw = jnp.where(masks[h], kc_ref[rows, kcols(p)].astype(F32) * dvb["kb"], 0.0).astype(BF16)
                    s = dvb["gb"] * s + _tn(kw, vc_ref[rows, vcols(p, h)])
                S[p, h] = s

        for p, h in heads:
            lam = _lam_of(lg_ref, 0, 2 * p + h)
            mu = _lam_of(lg_ref, 1, 2 * p + h)
            hm = masks[h]
            dvb = _decay_vecs(lam, mu)
            s = S[p, h]
            sbb = s.astype(BF16)
            sb_ref[p, 0, h] = sbb
            k = k_ref[:, kcols(p)]
            v = v_ref[:, vcols(p, h)]
            f = _ret_chunk_fwd(q_ref[:, kcols(p)], k, v, sf_ref[p, 0, h], sbb, hm, lam, mu)
            O = f["O"]
            r = lax.rsqrt(jnp.mean(O * O, axis=-1, keepdims=True) + NORM_EPS)
            g = g_ref[:, vcols(p, h)].astype(F32)
            y_ref[:, vcols(p, h)] = (O * r * (g * _sigmoid(g))).astype(BF16)
            kw = jnp.where(hm, k.astype(F32) * dvb["kb"], 0.0).astype(BF16)
            S[p, h] = dvb["gb"] * s + _tn(kw, v)

    rev = lambda n: N - 1 - n
    wq, wv = npairs * LANES, npairs * 2 * LANES
    st = pl.BlockSpec((npairs, 1, 2, LANES, LANES), lambda n: (0, rev(n), 0, 0, 0))
    return pl.pallas_call(
        body, name="ret_out_fwd", grid=(N,),
        in_specs=[pl.BlockSpec(memory_space=pltpu.SMEM),
                  pl.BlockSpec((CHUNK, wq), lambda n: (rev(n), 0)),
                  pl.BlockSpec((CHUNK, wq), lambda n: (rev(n), 0)),
                  pl.BlockSpec((CHUNK, wv), lambda n: (rev(n), rv_grp)),
                  pl.BlockSpec((CHUNK, wv), lambda n: (rev(n), rg_grp)),
                  st,
                  pl.BlockSpec((Lc, wq), lambda n: (0, 0)),
                  pl.BlockSpec((Lc, wv), lambda n: (0, rv_grp))],
        out_specs=[pl.BlockSpec((CHUNK, wv), lambda n: (rev(n), 0)), st],
        out_shape=[jax.ShapeDtypeStruct((L, d_mix), BF16),
                   jax.ShapeDtypeStruct((npairs, N, 2, LANES, LANES), BF16)],
        scratch_shapes=[pltpu.VMEM((npairs, 2, LANES, LANES), F32)],
        compiler_params=_params(1))(lg, Qr, Kr, P, P, SF, Krc, Pc)


ACC_ROWS = 8


def _ret_bwd1(Qr, Kr, P, Krc, Pc, SF, SB, dY, lg, rv_blk, rg_blk, npairs, d_proj):
    L = Qr.shape[0]
    Lc = Krc.shape[0]
    N, ncc = L // CHUNK, Lc // CHUNK
    rv_grp, rg_grp = _group_index(rv_blk, npairs), _group_index(rg_blk, npairs)
    heads = [(p, h) for p in range(npairs) for h in range(2)]
    kcols = lambda p: slice(p * LANES, (p + 1) * LANES)
    vcols = lambda p, h: slice((2 * p + h) * LANES, (2 * p + h + 1) * LANES)

    def body(lg_ref, q_ref, k_ref, v_ref, g_ref, sf_ref, sb_ref, dy_ref, kc_ref, vc_ref,
             dq_ref, dk_ref, dv_ref, dg_ref, do_ref, dkc_ref, dvc_ref, acc_ref, dS, T):
        n = pl.program_id(0)
        masks = _head_masks()

        @pl.when(n == 0)
        def _():
            dS[...] = jnp.zeros_like(dS)
            T[...] = jnp.zeros_like(T)
            acc_ref[...] = jnp.zeros_like(acc_ref)

        def head_main(p, h):
            lam = _lam_of(lg_ref, 0, 2 * p + h)
            mu = _lam_of(lg_ref, 1, 2 * p + h)
            hm = masks[h]
            hs = vcols(p, h)
            v = v_ref[:, hs]
            k = k_ref[:, kcols(p)]
            sf = sf_ref[p, 0, h]
            sb = sb_ref[p, 0, h]
            f = _ret_chunk_fwd(q_ref[:, kcols(p)], k, v, sf, sb, hm, lam, mu)
            dv_, O = f["dv"], f["O"]
            r = lax.rsqrt(jnp.mean(O * O, axis=-1, keepdims=True) + NORM_EPS)
            on = O * r
            g = g_ref[:, hs].astype(F32)
            sg = _sigmoid(g)
            dy = dy_ref[:, hs].astype(F32)
            dg_ref[:, hs] = (dy * on * (sg * (1.0 + g * (1.0 - sg)))).astype(BF16)
            don = dy * (g * sg)
            dO = r * (don - on * jnp.mean(don * on, axis=-1, keepdims=True))
            dOb = dO.astype(BF16)
            do_ref[:, hs] = dOb
            dAm = _nt(dOb, v)
            T[p, h] += dAm * f["Am"]
            dAb = (dAm * f["Mk"]).astype(BF16)
            km = jnp.where(hm, k, jnp.zeros_like(k))
            dq = _nn(dAb, km)
            dk = _tn(dAb, f["qm"])
            dvh = _tn(f["Amb"], dOb)
            dQf = _nt(dOb, sf)
            dQb = _nt(dOb, sb)
            dq = dq + dQf * dv_["qf"] + dQb * dv_["qb"]
            acc_ref[p, h, 0:1, :] += _fsum(dQf * f["Qf"].astype(F32) * (dv_["i"] + 1.0))
            acc_ref[p, h, 1:2, :] += _fsum(dQb * f["Qb"].astype(F32) * (CHUNK - dv_["i"]))
            dSh = dS[p, h]
            dSb_ = dSh.astype(BF16)
            Kf = (km.astype(F32) * dv_["kf"]).astype(BF16)
            dKf = _nt(v, dSb_)
            dk = dk + jnp.where(hm, dKf * dv_["kf"], 0.0)
            acc_ref[p, h, 2:3, :] += _fsum(jnp.where(hm, dKf, 0.0) * Kf.astype(F32) * (CHUNK - 1.0 - dv_["i"]))
            dvh = dvh + _nn(Kf, dSb_)
            acc_ref[p, h, 3:4, :] += float(CHUNK) * dv_["gf"] * _fsum(dSh * sf.astype(F32))
            dSh = dv_["gf"] * dSh + _tn(f["Qf"], dOb)
            dS[p, h] = dSh
            dv_ref[:, hs] = dvh
            return dq, dk

        for p in range(npairs):
            dq0, dk0 = head_main(p, 0)
            dq1, dk1 = head_main(p, 1)
            dq_ref[:, kcols(p)] = dq0 + dq1
            dk_ref[:, kcols(p)] = dk0 + dk1

        @pl.when(n == N - 1)
        def _():
            for p, h in heads:
                lam = _lam_of(lg_ref, 0, 2 * p + h)
                dv_ = _decay_vecs(lam, lam)
                hm = masks[h]
                hs = vcols(p, h)
                states = [jnp.zeros((LANES, LANES), F32)]
                kws = []
                for cc in range(ncc):
                    rows = slice(cc * CHUNK, (cc + 1) * CHUNK)
                    kw = jnp.where(hm, kc_ref[rows, kcols(p)].astype(F32) * dv_["kf"], 0.0).astype(BF16)
                    kws.append(kw)
                    states.append(dv_["gf"] * states[-1] + _tn(kw, vc_ref[rows, hs]))
                d = dS[p, h]
                for cc in reversed(range(ncc)):
                    db = d.astype(BF16)
                    rows = slice(cc * CHUNK, (cc + 1) * CHUNK)
                    dKf_c = jnp.where(hm, _nt(vc_ref[rows, hs], db), 0.0)
                    part = dKf_c * dv_["kf"]
                    if h == 0:
                        dkc_ref[rows, kcols(p)] = part
                    else:
                        dkc_ref[rows, kcols(p)] += part
                    acc_ref[p, h, 2:3, :] += _fsum(dKf_c * kws[cc].astype(F32) * (CHUNK - 1.0 - dv_["i"]))
                    dvc_ref[rows, hs] = _nn(kws[cc], db)
                    acc_ref[p, h, 3:4, :] += float(CHUNK) * dv_["gf"] * _fsum(d * states[cc])
                    d = dv_["gf"] * d
                _, rel, low = _decay_mask(lam, lam)
                Th = T[p, h]
                acc_ref[p, h, 4:5, :] += _fsum(jnp.where(low, Th * rel, 0.0))
                acc_ref[p, h, 5:6, :] += _fsum(jnp.where(low, 0.0, -Th * rel))

    rev = lambda n: N - 1 - n
    wq, wv = npairs * LANES, npairs * 2 * LANES
    st = pl.BlockSpec((npairs, 1, 2, LANES, LANES), lambda n: (0, rev(n), 0, 0, 0))
    pair = pl.BlockSpec((CHUNK, wq), lambda n: (rev(n), 0))
    wide = lambda grp: pl.BlockSpec((CHUNK, wv), lambda n: (rev(n), grp))
    return pl.pallas_call(
        body, name="ret_bwd_desc", grid=(N,),
        in_specs=[pl.BlockSpec(memory_space=pltpu.SMEM), pair, pair, wide(rv_grp), wide(rg_grp), st, st, wide(0),
                  pl.BlockSpec((Lc, wq), lambda n: (0, 0)),
                  pl.BlockSpec((Lc, wv), lambda n: (0, rv_grp))],
        out_specs=[pair, pair, wide(0), wide(rg_grp), wide(0),
                   pl.BlockSpec((Lc, wq), lambda n: (0, 0)),
                   pl.BlockSpec((Lc, wv), lambda n: (0, 0)),
                   pl.BlockSpec((npairs, 2, ACC_ROWS, LANES), lambda n: (0, 0, 0, 0))],
        out_shape=[jax.ShapeDtypeStruct((L, npairs * LANES), F32),
                   jax.ShapeDtypeStruct((L, npairs * LANES), F32),
                   jax.ShapeDtypeStruct((L, npairs * 2 * LANES), F32),
                   jax.ShapeDtypeStruct((L, d_proj), BF16),
                   jax.ShapeDtypeStruct((L, npairs * 2 * LANES), BF16),
                   jax.ShapeDtypeStruct((Lc, npairs * LANES), F32),
                   jax.ShapeDtypeStruct((Lc, npairs * 2 * LANES), F32),
                   jax.ShapeDtypeStruct((npairs, 2, ACC_ROWS, LANES), F32)],
        scratch_shapes=[pltpu.VMEM((npairs, 2, LANES, LANES), F32), pltpu.VMEM((npairs, 2, CHUNK, CHUNK), F32)],
        compiler_params=_params(1))(lg, Qr, Kr, P, P, SF, SB, dY, Krc, Pc)


def _ret_bwd2(Qr, Kr, P, Krc, Pc, SB, dO, dKr, dVp, dP, dKc, dVc, lg, rv_blk, npairs):
    L = Qr.shape[0]
    Lc = Krc.shape[0]
    N, ncc = L // CHUNK, Lc // CHUNK
    rv_grp = _group_index(rv_blk, npairs)
    heads = [(p, h) for p in range(npairs) for h in range(2)]
    kcols = lambda p: slice(p * LANES, (p + 1) * LANES)
    vcols = lambda p, h: slice((2 * p + h) * LANES, (2 * p + h + 1) * LANES)

    def body(lg_ref, q_ref, k_ref, v_ref, sb_ref, do_ref, dkin_ref, dvin_ref, kc_ref, vc_ref, dkcin_ref, dvcin_ref,
             dpin_ref, dk_ref, dv_ref, dkc_ref, dvc_ref, acc_ref, dS):
        del dpin_ref
        n = pl.program_id(0)
        masks = _head_masks()

        @pl.when(n == 0)
        def _():
            dS[...] = jnp.zeros_like(dS)
            acc_ref[...] = jnp.zeros_like(acc_ref)

        def head_main(p, h):
            mu = _lam_of(lg_ref, 1, 2 * p + h)
            hm = masks[h]
            hs = vcols(p, h)
            dv_ = _decay_vecs(mu, mu)
            v = v_ref[:, hs]
            k = k_ref[:, kcols(p)]
            q = q_ref[:, kcols(p)]
            dOb = do_ref[:, hs]
            km = jnp.where(hm, k, jnp.zeros_like(k)).astype(F32)
            Kb = (km * dv_["kb"]).astype(BF16)
            Qb = (jnp.where(hm, q, jnp.zeros_like(q)).astype(F32) * dv_["qb"]).astype(BF16)
            dSh = dS[p, h]
            dSb_ = dSh.astype(BF16)
            dKb = jnp.where(hm, _nt(v, dSb_), 0.0)
            acc_ref[p, h, 0:1, :] += _fsum(dKb * Kb.astype(F32) * dv_["i"])
            dv_ref[:, hs] = (dvin_ref[:, hs] + _nn(Kb, dSb_)).astype(BF16)
            acc_ref[p, h, 1:2, :] += float(CHUNK) * dv_["gb"] * _fsum(dSh * sb_ref[p, 0, h].astype(F32))
            dS[p, h] = dv_["gb"] * dSh + _tn(Qb, dOb)
            return dKb * dv_["kb"]

        for p in range(npairs):
            dk_ref[:, kcols(p)] = dkin_ref[:, kcols(p)] + head_main(p, 0) + head_main(p, 1)

        @pl.when(n == N - 1)
        def _():
            for p, h in heads:
                mu = _lam_of(lg_ref, 1, 2 * p + h)
                hm = masks[h]
                hs = vcols(p, h)
                dv_ = _decay_vecs(mu, mu)
                states = {}
                kws = {}
                s = jnp.zeros((LANES, LANES), F32)
                for cc in reversed(range(ncc)):
                    rows = slice(cc * CHUNK, (cc + 1) * CHUNK)
                    states[cc] = s
                    kw = jnp.where(hm, kc_ref[rows, kcols(p)].astype(F32) * dv_["kb"], 0.0).astype(BF16)
                    kws[cc] = kw
                    s = dv_["gb"] * s + _tn(kw, vc_ref[rows, hs])
                d = dS[p, h]
                for cc in range(ncc):
                    db = d.astype(BF16)
                    rows = slice(cc * CHUNK, (cc + 1) * CHUNK)
                    dKb_c = jnp.where(hm, _nt(vc_ref[rows, hs], db), 0.0)
                    part = dKb_c * dv_["kb"]
                    if h == 0:
                        dkc_ref[rows, kcols(p)] = dkcin_ref[rows, kcols(p)] + part
                    else:
                        dkc_ref[rows, kcols(p)] += part
                    acc_ref[p, h, 0:1, :] += _fsum(dKb_c * kws[cc].astype(F32) * dv_["i"])
                    dvc_ref[rows, hs] = dvcin_ref[rows, hs] + _nn(kws[cc], db)
                    acc_ref[p, h, 1:2, :] += float(CHUNK) * dv_["gb"] * _fsum(d * states[cc])
                    d = dv_["gb"] * d

    wq, wv = npairs * LANES, npairs * 2 * LANES
    st = pl.BlockSpec((npairs, 1, 2, LANES, LANES), lambda n: (0, n, 0, 0, 0))
    pair = pl.BlockSpec((CHUNK, wq), lambda n: (n, 0))
    wide = lambda grp: pl.BlockSpec((CHUNK, wv), lambda n: (n, grp))
    ckc = pl.BlockSpec((Lc, wq), lambda n: (0, 0))
    cvc = lambda grp: pl.BlockSpec((Lc, wv), lambda n: (0, grp))
    return pl.pallas_call(
        body, name="ret_bwd_asc", grid=(N,),
        in_specs=[pl.BlockSpec(memory_space=pltpu.SMEM), pair, pair, wide(rv_grp), st, wide(0), pair, wide(0),
                  ckc, cvc(rv_grp), ckc, cvc(0), pl.BlockSpec(memory_space=pl.ANY)],
        out_specs=[pair, wide(rv_grp), ckc, cvc(0),
                   pl.BlockSpec((npairs, 2, ACC_ROWS, LANES), lambda n: (0, 0, 0, 0))],
        out_shape=[jax.ShapeDtypeStruct(dKr.shape, F32),
                   jax.ShapeDtypeStruct(dP.shape, dP.dtype),
                   jax.ShapeDtypeStruct(dKc.shape, F32),
                   jax.ShapeDtypeStruct(dVc.shape, F32),
                   jax.ShapeDtypeStruct((npairs, 2, ACC_ROWS, LANES), F32)],
        input_output_aliases={12: 1},
        scratch_shapes=[pltpu.VMEM((npairs, 2, LANES, LANES), F32)],
        compiler_params=_params(1))(lg, Qr, Kr, P, SB, dO, dKr, dVp, Krc, Pc, dKc, dVc, dP)


GROUP = 4


def _att_band(Lc):
    row = np.arange(GROUP * CHUNK)[:, None] % CHUNK
    col = np.arange(3 * CHUNK + Lc)[None, :]
    ok = ((col >= row) & (col <= row + 2 * CHUNK)) | (col >= 3 * CHUNK)
    return jnp.asarray(np.where(ok, 0.0, NEG), F32)


def _att_edge(n, N, Lc):
    col = lax.broadcasted_iota(jnp.int32, (1, 3 * CHUNK + Lc), 1)
    off = jnp.logical_or(jnp.logical_and(col < CHUNK, n == 0),
                         jnp.logical_and(jnp.logical_and(col >= 2 * CHUNK, col < 3 * CHUNK), n == N - 1))
    return jnp.where(off, NEG, 0.0)


def _stack_heads(ref):
    masks = _head_masks()
    tiles = []
    for pr in range(2):
        t = ref[:, pr * LANES:(pr + 1) * LANES]
        for a in range(2):
            tiles.append(jnp.where(masks[a], t, jnp.zeros_like(t)))
    return jnp.concatenate(tiles, axis=0)


def _unstack_heads(x4):
    m0 = _head_masks()[0]
    return [jnp.where(m0, x4[(2 * pr) * CHUNK:(2 * pr + 1) * CHUNK], x4[(2 * pr + 1) * CHUNK:(2 * pr + 2) * CHUNK])
            for pr in range(2)]


def _sink_column(sink_ref, g):
    row = lax.broadcasted_iota(jnp.int32, (GROUP * CHUNK, 1), 0) // CHUNK
    col = jnp.zeros((GROUP * CHUNK, 1), F32)
    for h in range(GROUP):
        col = jnp.where(row == h, sink_ref[0, g * GROUP + h], col)
    return col


def _att_probs(q4, Kall, bias, snk):
    s = _nt(q4, Kall) + bias
    mx = jnp.maximum(jnp.max(s, axis=1, keepdims=True), snk)
    p = jnp.exp(s - mx)
    p_snk = jnp.exp(snk - mx)
    inv = 1.0 / (jnp.sum(p, axis=1, keepdims=True) + p_snk)
    return p, p_snk, inv


def _att_specs(Lc, N):
    q = pl.BlockSpec((CHUNK, 2 * LANES), lambda g, n: (n, g))
    kv = lambda s: pl.BlockSpec((1, CHUNK, LANES), lambda g, n: (g, jnp.clip(n + s, 0, N - 1), 0))
    ctx = pl.BlockSpec((1, Lc, LANES), lambda g, n: (g, 0, 0))
    return q, kv, ctx


def _att_fwd(Qa, Kd, Vd, Kdc, Vdc, sink, Y, blk0):
    L = Qa.shape[0]
    Lc = Kdc.shape[1]
    N = L // CHUNK
    nkv = Kd.shape[0]

    def body(sink_ref, band_ref, q_ref, kp, kc_, kn, vp, vc_, vn, kctx, vctx, y_in, o_ref):
        del y_in
        g, n = pl.program_id(0), pl.program_id(1)
        Kall = jnp.concatenate([kp[0], kc_[0], kn[0], kctx[0]], axis=0)
        Vall = jnp.concatenate([vp[0], vc_[0], vn[0], vctx[0]], axis=0)
        bias = band_ref[...] + _att_edge(n, N, Lc)
        p, _, inv = _att_probs(_stack_heads(q_ref), Kall, bias, _sink_column(sink_ref, g))
        o4 = _nn(p.astype(BF16), Vall) * inv
        for pr, o in enumerate(_unstack_heads(o4)):
            o_ref[:, pr * LANES:(pr + 1) * LANES] = o.astype(BF16)

    q, kv, ctx = _att_specs(Lc, N)
    band = pl.BlockSpec((GROUP * CHUNK, 3 * CHUNK + Lc), lambda g, n: (0, 0))
    return pl.pallas_call(
        body, name="att_fwd", grid=(nkv, N),
        in_specs=[pl.BlockSpec(memory_space=pltpu.SMEM), band, q, kv(-1), kv(0), kv(1), kv(-1), kv(0), kv(1), ctx, ctx,
                  pl.BlockSpec(memory_space=pl.ANY)],
        out_specs=pl.BlockSpec((CHUNK, 2 * LANES), lambda g, n: (n, blk0 + g)),
        out_shape=jax.ShapeDtypeStruct(Y.shape, Y.dtype),
        input_output_aliases={11: 0},
        compiler_params=_params(2))(sink, _att_band(Lc), Qa, Kd, Kd, Kd, Vd, Vd, Vd, Kdc, Vdc, Y)


def _att_bwd(Qa, Kd, Vd, Kdc, Vdc, sink, dY, blk0):
    L = Qa.shape[0]
    Lc = Kdc.shape[1]
    N = L // CHUNK
    nkv = Kd.shape[0]

    def body(sink_ref, band_ref, q_ref, kp, kc_, kn, vp, vc_, vn, kctx, vctx, dy_ref,
             dq_ref, dkp, dkc_, dkn, dvp, dvc_, dvn, dkctx, dvctx, dsink_ref):
        g, n = pl.program_id(0), pl.program_id(1)

        @pl.when(n == 0)
        def _():
            dkctx[...] = jnp.zeros_like(dkctx)
            dvctx[...] = jnp.zeros_like(dvctx)
            dsink_ref[...] = jnp.zeros_like(dsink_ref)

        Kall = jnp.concatenate([kp[0], kc_[0], kn[0], kctx[0]], axis=0)
        Vall = jnp.concatenate([vp[0], vc_[0], vn[0], vctx[0]], axis=0)
        q4 = _stack_heads(q_ref)
        do4 = _stack_heads(dy_ref)
        p, p_snk, inv = _att_probs(q4, Kall, band_ref[...] + _att_edge(n, N, Lc), _sink_column(sink_ref, g))
        P = p * inv
        dp = _nt(do4, Vall)
        delta = jnp.sum(P * dp, axis=1, keepdims=True)
        ds = (P * (dp - delta)).astype(BF16)
        dsnk = -(p_snk * inv) * delta
        for h in range(GROUP):
            dsink_ref[0, h:h + 1, :] += _fsum(dsnk[h * CHUNK:(h + 1) * CHUNK])
        for pr, dq in enumerate(_unstack_heads(_nn(ds, Kall))):
            dq_ref[:, pr * LANES:(pr + 1) * LANES] = dq
        dK = _tn(ds, q4)
        dV = _tn(P.astype(BF16), do4)
        for j, (rk, rv) in enumerate([(dkp, dvp), (dkc_, dvc_), (dkn, dvn)]):
            rk[0] = dK[j * CHUNK:(j + 1) * CHUNK].astype(BF16)
            rv[0] = dV[j * CHUNK:(j + 1) * CHUNK].astype(BF16)
        dkctx[0] += dK[3 * CHUNK:]
        dvctx[0] += dV[3 * CHUNK:]

    q, kv, ctx = _att_specs(Lc, N)
    band = pl.BlockSpec((GROUP * CHUNK, 3 * CHUNK + Lc), lambda g, n: (0, 0))
    blk = pl.BlockSpec((1, CHUNK, LANES), lambda g, n: (g, n, 0))
    part = jax.ShapeDtypeStruct((nkv, L, LANES), BF16)
    cshape = jax.ShapeDtypeStruct((nkv, Lc, LANES), F32)
    return pl.pallas_call(
        body, name="att_bwd", grid=(nkv, N),
        in_specs=[pl.BlockSpec(memory_space=pltpu.SMEM), band, q, kv(-1), kv(0), kv(1), kv(-1), kv(0), kv(1), ctx, ctx,
                  pl.BlockSpec((CHUNK, 2 * LANES), lambda g, n: (n, blk0 + g))],
        out_specs=[q, blk, blk, blk, blk, blk, blk, ctx, ctx,
                   pl.BlockSpec((1, 8, LANES), lambda g, n: (g, 0, 0))],
        out_shape=[jax.ShapeDtypeStruct(Qa.shape, F32), part, part, part, part, part, part, cshape, cshape,
                   jax.ShapeDtypeStruct((nkv, 8, LANES), F32)],
        compiler_params=_params(2))(sink, _att_band(Lc), Qa, Kd, Kd, Kd, Vd, Vd, Vd, Kdc, Vdc, dY)


def _scale_rows(dx, gt, saved, name):
    M, D = dx.shape
    tm = _tile(M, 512, 8)

    def body(dx_ref, gt_ref, sv_ref, dz_ref, dgt_ref):
        @pl.when(pl.program_id(0) == 0)
        def _():
            dgt_ref[...] = jnp.zeros_like(dgt_ref)
        d = dx_ref[...]
        dz_ref[...] = (d * gt_ref[...]).astype(BF16)
        dgt_ref[...] += jnp.sum(d * sv_ref[...].astype(F32), axis=0, keepdims=True)

    row = pl.BlockSpec((tm, D), lambda i: (i, 0))
    vec = pl.BlockSpec((1, D), lambda i: (0, 0))
    return pl.pallas_call(
        body, name=name, grid=(M // tm,), in_specs=[row, vec, row], out_specs=[row, vec],
        out_shape=[jax.ShapeDtypeStruct((M, D), BF16), jax.ShapeDtypeStruct((1, D), F32)],
        compiler_params=_params(1))(dx, gt, saved)


def _bwd_proj(dz, w, G=None, U=None, name="bwd_proj"):
    M, D = dz.shape
    N = w.shape[0]
    swiglu = G is not None
    tm, tn = _tile(M, ROWS_PER_LATCH, 8), _tile(N, 512)

    def body(*refs):
        if swiglu:
            dz_ref, w_ref, G_ref, U_ref, dG_ref, dU_ref = refs
        else:
            dz_ref, w_ref, dA_ref = refs
        dA = _nt(dz_ref[...], w_ref[...])
        if swiglu:
            Gv = G_ref[...].astype(F32)
            Uv = U_ref[...].astype(F32)
            sg = _sigmoid(Gv)
            dU_ref[...] = (dA * Gv * sg).astype(BF16)
            dG_ref[...] = (dA * Uv * (sg * (1.0 + Gv * (1.0 - sg)))).astype(BF16)
        else:
            dA_ref[...] = dA.astype(BF16)

    row = pl.BlockSpec((tm, D), lambda i, j: (i, 0))
    tile = pl.BlockSpec((tm, tn), lambda i, j: (i, j))
    big = jax.ShapeDtypeStruct((M, N), BF16)
    in_specs = [row, pl.BlockSpec((tn, D), lambda i, j: (j, 0))]
    args = [dz, w]
    if swiglu:
        in_specs += [tile, tile]
        args += [G, U]
        out_specs, out_shape = [tile, tile], [big, big]
    else:
        out_specs, out_shape = tile, big
    return pl.pallas_call(
        body, name=name, grid=(M // tm, N // tn), in_specs=in_specs, out_specs=out_specs, out_shape=out_shape,
        compiler_params=_params(2))(*args)


def _tn_matmul(pairs, name):
    Ka, Nb = pairs[0][0].shape[1], pairs[0][1].shape[1]
    tk, tn = _tile(Ka, 2048), _tile(Nb, 2048)
    tls, nks = [], []
    for a, _ in pairs:
        tl = _tile(a.shape[0], 512, 8)
        tls.append(tl)
        nks.append(a.shape[0] // tl)
    starts = [int(s) for s in np.cumsum([0] + nks[:-1])]
    nk = int(sum(nks))

    def body(*refs):
        out_ref, acc = refs[-2], refs[-1]
        k = pl.program_id(2)

        @pl.when(k == 0)
        def _():
            acc[...] = jnp.zeros_like(acc)

        for idx in range(len(pairs)):
            a_ref, b_ref = refs[2 * idx], refs[2 * idx + 1]

            @pl.when(jnp.logical_and(k >= starts[idx], k < starts[idx] + nks[idx]))
            def _():
                acc[...] += _tn(a_ref[...], b_ref[...])

        @pl.when(k == nk - 1)
        def _():
            out_ref[...] = acc[...].astype(BF16)

    in_specs, args = [], []
    for idx, (a, b) in enumerate(pairs):
        s0, n_ = starts[idx], nks[idx]
        in_specs.append(pl.BlockSpec((tls[idx], tk), lambda i, j, k, s0=s0, n_=n_: (jnp.clip(k - s0, 0, n_ - 1), i)))
        in_specs.append(pl.BlockSpec((tls[idx], tn), lambda i, j, k, s0=s0, n_=n_: (jnp.clip(k - s0, 0, n_ - 1), j)))
        args += [a, b]
    return pl.pallas_call(
        body, name=name, grid=(Ka // tk, Nb // tn, nk), in_specs=in_specs,
        out_specs=pl.BlockSpec((tk, tn), lambda i, j, k: (i, j)),
        out_shape=jax.ShapeDtypeStruct((Ka, Nb), BF16),
        scratch_shapes=[pltpu.VMEM((tk, tn), F32)], compiler_params=_params(3))(*args)


def _bwd_norm_mod(pairs, x, dres, g, sh, sc, name):
    M, D = x.shape
    K = pairs[0][0].shape[1]
    tm, tk = _tile(M, 512, 8), _tile(K, 1152 if len(pairs) == 1 else 512)
    nk = K // tk
    npair = len(pairs)
    has_res = dres is not None

    def body(*refs):
        pr = refs[:2 * npair]
        rest = refs[2 * npair:]
        if has_res:
            x_ref, dres_ref, g_ref, sh_ref, sc_ref, dx_ref, st_ref, acc = rest
        else:
            x_ref, g_ref, sh_ref, sc_ref, dx_ref, st_ref, acc = rest
        del sh_ref
        i, k = pl.program_id(0), pl.program_id(1)

        @pl.when(jnp.logical_and(i == 0, k == 0))
        def _():
            st_ref[...] = jnp.zeros_like(st_ref)

        @pl.when(k == 0)
        def _():
            acc[...] = jnp.zeros_like(acc)

        t = _nt(pr[1][...], pr[0][...])
        for idx in range(1, npair):
            t = t + _nt(pr[2 * idx + 1][...], pr[2 * idx][...])
        acc[...] += t

        @pl.when(k == nk - 1)
        def _():
            xv = x_ref[...]
            gv = g_ref[...]
            dh = acc[...].T
            r = lax.rsqrt(jnp.mean(xv * xv, axis=-1, keepdims=True) + NORM_EPS)
            xh = xv * r
            st_ref[0:1, :] += jnp.sum(dh, axis=0, keepdims=True)
            st_ref[1:2, :] += jnp.sum(dh * (xh * gv), axis=0, keepdims=True)
            dn = dh * (1.0 + sc_ref[...])
            st_ref[2:3, :] += jnp.sum(dn * xh, axis=0, keepdims=True)
            dxh = dn * gv
            d = r * (dxh - xh * jnp.mean(dxh * xh, axis=-1, keepdims=True))
            if has_res:
                d = d + dres_ref[...]
            dx_ref[...] = d

    row = pl.BlockSpec((tm, D), lambda i, k: (i, 0))
    vec = pl.BlockSpec((1, D), lambda i, k: (0, 0))
    in_specs, args = [], []
    for dA, w in pairs:
        in_specs += [pl.BlockSpec((tm, tk), lambda i, k: (i, k)), pl.BlockSpec((D, tk), lambda i, k: (0, k))]
        args += [dA, w]
    in_specs += [row] + ([row] if has_res else []) + [vec, vec, vec]
    args += [x] + ([dres] if has_res else []) + [g, sh, sc]
    return pl.pallas_call(
        body, name=name, grid=(M // tm, nk), in_specs=in_specs,
        out_specs=[row, pl.BlockSpec((8, D), lambda i, k: (0, 0))],
        out_shape=[jax.ShapeDtypeStruct((M, D), F32), jax.ShapeDtypeStruct((8, D), F32)],
        scratch_shapes=[pltpu.VMEM((D, tm), F32)], compiler_params=_params(2))(*args)


def _local_step(x, ctx, tgt, mod, modc, norm_mix, norm_ffn, norm_final, lg, sink, w_in, rest_weights, on_grads):
    L, D = x.shape
    Lc = ctx.shape[0]
    d_proj = w_in.shape[1]
    npairs = RET_HEADS // 2
    nkv = ATT_KV_HEADS
    nkvp = nkv // 2
    o_rq = 0
    o_rk = o_rq + RET_HEADS * RET_DK // LANES
    o_rv = o_rk + RET_HEADS * RET_DK // LANES
    o_rg = o_rv + RET_HEADS * RET_DV // LANES
    o_aq = o_rg + RET_HEADS * RET_DV // LANES
    o_ak = o_aq + ATT_HEADS * ATT_DH // LANES
    o_av = o_ak + nkv * ATT_DH // LANES
    assert (o_av + nkv * ATT_DH // LANES) * LANES == d_proj
    assert o_rv % 2 == 0 and o_rg % 2 == 0 and (RET_HEADS * RET_DV) % (2 * LANES) == 0
    rv_blk, rg_blk = o_rv // 2, o_rg // 2
    d_ret = RET_HEADS * RET_DV
    d_mix = d_ret + ATT_HEADS * ATT_DH
    att_blk = d_ret // (2 * LANES)
    k_scale = RET_DK ** -0.5
    a_scale = ATT_DH ** -0.5

    T = _rope_tables(L)
    Tc = dict(C=jnp.ones((Lc, LANES), F32), S=jnp.zeros((Lc, LANES), F32))
    row = lambda m, i: m[i:i + 1]
    sh_m, sc_m, gt_m, sh_f, sc_f, gt_f = [row(mod, i) for i in range(6)]
    sh_mc, sc_mc = row(modc, 0), row(modc, 1)

    P, hx = _norm_mod_matmul(x, norm_mix, sh_m, sc_m, w_in, "in_proj")
    Pc, hc = _norm_mod_matmul(ctx, norm_mix, sh_mc, sc_mc, w_in, "in_proj_ctx")
    nq = RET_HEADS * RET_DK // LANES
    Qr = _rope_cols(P, o_rq, nq, T["Cr"], T["Sr"], T["Rr"], 1.0, True, "rope_rq")
    Kr = _rope_cols(P, o_rk, nq, T["Cr"], T["Sr"], T["Rr"], k_scale, True, "rope_rk")
    Krc = _rope_cols(Pc, o_rk, nq, Tc["C"], Tc["S"], T["Rr"], k_scale, False, "scale_rk_ctx")
    Qa = _rope_cols(P, o_aq, ATT_HEADS * ATT_DH // LANES, T["Ca"], T["Sa"], T["Ra"], a_scale, True, "rope_aq")
    Kd = _dup_heads(P, o_ak, nkvp, T["Ca"], T["Sa"], T["Ra"], T["D0"], T["D1"], True, "dup_ak")
    Vd = _dup_heads(P, o_av, nkvp, T["Ca"], T["Sa"], T["Ra"], T["D0"], T["D1"], False, "dup_av")
    Kdc = _dup_heads(Pc, o_ak, nkvp, Tc["C"], Tc["S"], T["Ra"], T["D0"], T["D1"], False, "dup_ak_ctx")
    Vdc = _dup_heads(Pc, o_av, nkvp, Tc["C"], Tc["S"], T["Ra"], T["D0"], T["D1"], False, "dup_av_ctx")

    SF = _ret_states_fwd(Kr, P, Krc, Pc, lg, rv_blk, npairs)
    Y, SB = _ret_out_fwd(Qr, Kr, P, Krc, Pc, SF, lg, rv_blk, rg_blk, npairs, d_mix)
    Y = _att_fwd(Qa, Kd, Vd, Kdc, Vdc, sink, Y, att_blk)

    w_out, w_gate, w_up, w_down = rest_weights(Y)
    x1, O1 = _proj_residual(Y, w_out, x, gt_m, "out_proj")
    G, U, A, h2 = _ffn_in(x1, norm_ffn, sh_f, sc_f, w_gate, w_up)
    x2, Fo = _proj_residual(A, w_down, x1, gt_f, "ffn_out")
    dx2, loss, d_norm_final = _final(x2, norm_final, tgt)

    dz2, dgt_f = _scale_rows(dx2, gt_f, Fo, "ffn_gate_bwd")
    dG, dU = _bwd_proj(dz2, w_down, G, U, name="ffn_out_bwd")
    g_w_down = _tn_matmul([(A, dz2)], "grad_w_down")
    tok = on_grads(["w_down"], [g_w_down])
    dx1, st_f = _bwd_norm_mod([(dG, w_gate), (dU, w_up)], x1, dx2, norm_ffn + tok, sh_f, sc_f, "ffn_in_bwd")
    g_w_gate = _tn_matmul([(h2, dG)], "grad_w_gate")
    g_w_up = _tn_matmul([(h2, dU)], "grad_w_up")
    tok = on_grads(["w_gate", "w_up"], [g_w_gate, g_w_up])
    dz1, dgt_m = _scale_rows(dx1, gt_m + tok, O1, "mix_gate_bwd")
    dY = _bwd_proj(dz1, w_out, name="out_proj_bwd")
    g_w_out = _tn_matmul([(Y, dz1)], "grad_w_out")
    tok = on_grads(["w_out"], [g_w_out])

    dQa, dKp, dKs, dKn, dVp, dVs, dVn, dKdc, dVdc, dsink = _att_bwd(Qa, Kd, Vd, Kdc, Vdc, sink + tok, dY, att_blk)
    dQr, dKr, dVr, dP, dO, dKc, dVc, acc1 = _ret_bwd1(Qr, Kr, P, Krc, Pc, SF, SB, dY, lg, rv_blk, rg_blk, npairs, d_proj)
    dKr, dP, dKc, dVc, acc2 = _ret_bwd2(Qr, Kr, P, Krc, Pc, SB, dO, dKr, dVr, dP, dKc, dVc, lg, rv_blk, npairs)

    dP = _unrope_cols(dQr, dP, o_rq, nq, T["Cr"], T["Sr"], T["RrT"], 1.0, True, "unrope_rq")
    dP = _unrope_cols(dKr, dP, o_rk, nq, T["Cr"], T["Sr"], T["RrT"], k_scale, True, "unrope_rk")
    dP = _unrope_cols(dQa, dP, o_aq, ATT_HEADS * ATT_DH // LANES, T["Ca"], T["Sa"], T["RaT"], a_scale, True, "unrope_aq")
    dP = _fold_heads([(dKs, 0), (dKp, 1), (dKn, -1)], dP, o_ak, nkvp, T["Ca"], T["Sa"], T["RaT"], T["D0T"], T["D1T"],
                     True, "fold_ak")
    dP = _fold_heads([(dVs, 0), (dVp, 1), (dVn, -1)], dP, o_av, nkvp, T["Ca"], T["Sa"], T["RaT"], T["D0T"], T["D1T"],
                     False, "fold_av")
    dPc = jnp.zeros((Lc, d_proj), BF16)
    dPc = _unrope_cols(dKc, dPc, o_rk, nq, Tc["C"], Tc["S"], T["RrT"], k_scale, False, "ctx_rk_bwd")
    dPc = _unrope_cols(dVc, dPc, o_rv, RET_HEADS * RET_DV // LANES, Tc["C"], Tc["S"], T["RrT"], 1.0, False, "ctx_rv_bwd")
    dPc = _fold_heads([(dKdc.astype(BF16), 0)], dPc, o_ak, nkvp, Tc["C"], Tc["S"], T["RaT"], T["D0T"], T["D1T"],
                      False, "fold_ak_ctx")
    dPc = _fold_heads([(dVdc.astype(BF16), 0)], dPc, o_av, nkvp, Tc["C"], Tc["S"], T["RaT"], T["D0T"], T["D1T"],
                      False, "fold_av_ctx")

    dx, st_m = _bwd_norm_mod([(dP, w_in)], x, dx1, norm_mix, sh_m, sc_m, "in_proj_bwd")
    _, st_mc = _bwd_norm_mod([(dPc, w_in)], ctx, None, norm_mix, sh_mc, sc_mc, "in_proj_ctx_bwd")
    g_w_in = _tn_matmul([(hx, dP), (hc, dPc)], "grad_w_in")
    on_grads(["w_in"], [g_w_in])

    a1 = acc1[:, :, :, 0].reshape(RET_HEADS, ACC_ROWS)
    a2 = acc2[:, :, :, 0].reshape(RET_HEADS, ACC_ROWS)
    dlam = (a1[:, 0] + a1[:, 2] + a1[:, 3] + a1[:, 4]) * lg[0]
    dmu = (a1[:, 1] + a1[:, 5] + a2[:, 0] + a2[:, 1]) * lg[1]
    d_sink = dsink[:, :4, 0].reshape(1, ATT_HEADS)

    nh = RET_HEADS
    assert 2 * nh + ATT_HEADS <= LOSS_LANE
    small = _pack_rows(
        [(st_m, 0, 2, 0, 0), (dgt_m, 0, 1, 2, 0), (st_f, 0, 2, 3, 0), (dgt_f, 0, 1, 5, 0), (st_mc, 0, 2, 6, 0),
         (st_m[2:3] + st_mc[2:3], 0, 1, 12, 0), (st_f, 2, 1, 13, 0), (d_norm_final, 0, 1, 14, 0),
         (dlam.reshape(1, nh), 0, 1, 15, 0), (dmu.reshape(1, nh), 0, 1, 15, nh), (d_sink, 0, 1, 15, 2 * nh),
         (loss[:, 0:1], 0, 1, 15, LOSS_LANE)], 16, D, "pack_small")
    return dict(grad_x=dx, small=small)


def _my_pos():
    return lax.axis_index("x"), lax.axis_index("y"), lax.axis_index("c")


def _other_chips(x, y):
    return [(1 - x, y), (x, 1 - y), (1 - x, 1 - y)]


def _remote(src, dst, ssem, rsem, dev):
    return pltpu.make_async_remote_copy(src_ref=src, dst_ref=dst, send_sem=ssem, recv_sem=rsem,
                                        device_id=dev, device_id_type=MESH)


def _allgather8(v, name):
    R, Cc = v.shape

    def body(v_ref, out_ref, send_sems, recv_sems):
        x, y, c = _my_pos()
        me = 4 * x + 2 * y + c
        out_ref[pl.ds(me, 1)] = v_ref[...][None]
        peers = []
        for j in range(1, N_DEV):
            peers.append((1 - x if (j >> 2) & 1 else x, 1 - y if (j >> 1) & 1 else y, 1 - c if j & 1 else c))
        copies = []
        for j, peer in enumerate(peers):
            cp = _remote(v_ref, out_ref.at[me], send_sems.at[j], recv_sems.at[j], peer)
            cp.start()
            copies.append(cp)
        for j, peer in enumerate(peers):
            pid = 4 * peer[0] + 2 * peer[1] + peer[2]
            _remote(v_ref, out_ref.at[pid], send_sems.at[j], recv_sems.at[j], peer).wait_recv()
        for cp in copies:
            cp.wait_send()

    return pl.pallas_call(
        body, name=name, out_shape=jax.ShapeDtypeStruct((N_DEV, R, Cc), v.dtype),
        in_specs=[pl.BlockSpec(memory_space=pltpu.VMEM)], out_specs=pl.BlockSpec(memory_space=pltpu.VMEM),
        scratch_shapes=[pltpu.SemaphoreType.DMA((N_DEV - 1,)), pltpu.SemaphoreType.DMA((N_DEV - 1,))])(v)


def _region(ref, k, half, shard_shape, axis):
    r, cs = shard_shape
    hr = r // 2
    if axis == 1:
        return ref.at[pl.ds(pl.multiple_of(half * hr, 16), hr), pl.ds(pl.multiple_of(k * cs, LANES), cs)]
    return ref.at[pl.ds(pl.multiple_of(k * r + half * hr, 16), hr), :]


def _full_shape(shard_shape, axis):
    r, cs = shard_shape
    return (r, N_CHIPS * cs) if axis == 1 else (N_CHIPS * r, cs)


def _half_pieces(ref, half, shard_shape, axis):
    r, cs = shard_shape
    hr = r // 2
    if axis == 1:
        return [ref.at[pl.ds(pl.multiple_of(half * hr, 16), hr), :]]
    return [ref.at[pl.ds(pl.multiple_of(k * r + half * hr, 16), hr), :] for k in range(N_CHIPS)]


def _rs_sibling(grads, shapes, axes, name):
    nw = len(grads)
    npc = max(1 if a == 1 else N_CHIPS for a in axes)

    def body(*refs):
        g_refs, out_refs = refs[:nw], refs[nw:2 * nw]
        send, recv = refs[2 * nw:]
        x, y, c = _my_pos()
        sib = (x, y, 1 - c)
        copies = []
        for w in range(nw):
            src = _half_pieces(g_refs[w], 1 - c, shapes[w], axes[w])
            dst = _half_pieces(out_refs[w], 1 - c, shapes[w], axes[w])
            for i, (s, d) in enumerate(zip(src, dst)):
                cp = _remote(s, d, send.at[w, i], recv.at[w, i], sib)
                cp.start()
                copies.append(cp)
        for w in range(nw):
            mine = _half_pieces(out_refs[w], c, shapes[w], axes[w])
            for i, d in enumerate(mine):
                _remote(d, d, send.at[w, i], recv.at[w, i], sib).wait_recv()
        for cp in copies:
            cp.wait_send()

    anyspec = pl.BlockSpec(memory_space=pl.ANY)
    return pl.pallas_call(
        body, name=name,
        out_shape=[jax.ShapeDtypeStruct(_full_shape(s, a), BF16) for s, a in zip(shapes, axes)],
        in_specs=[anyspec] * nw, out_specs=[anyspec] * nw,
        scratch_shapes=[pltpu.SemaphoreType.DMA((nw, npc)), pltpu.SemaphoreType.DMA((nw, npc))])(*grads)


def _half_block_spec(shard_shape, axis, tr):
    r, cs = shard_shape
    hr = r // 2
    if axis == 1:
        return pl.BlockSpec((tr, cs), lambda k, i, c_ref: (c_ref[0] * (hr // tr) + i, k))
    return pl.BlockSpec((tr, cs), lambda k, i, c_ref: (k * (r // tr) + c_ref[0] * (hr // tr) + i, 0))


def _add_halves(g, recv, cvec, shard_shape, axis, name):
    r, cs = shard_shape
    hr = r // 2
    tr = _tile(hr, 256, 16)

    def body(c_ref, a_ref, b_ref, o_ref):
        del c_ref
        o_ref[0] = (a_ref[...].astype(F32) + b_ref[...].astype(F32)).astype(BF16)

    spec = _half_block_spec(shard_shape, axis, tr)
    return pl.pallas_call(
        body, name=name,
        grid_spec=pltpu.PrefetchScalarGridSpec(
            num_scalar_prefetch=1, grid=(N_CHIPS, hr // tr), in_specs=[spec, spec],
            out_specs=pl.BlockSpec((1, tr, cs), lambda k, i, c_ref: (k, i, 0))),
        out_shape=jax.ShapeDtypeStruct((N_CHIPS, hr, cs), BF16),
        compiler_params=_params(2, False))(cvec, g, recv)


def _sum_chips(sums, landed, kc, name):
    _, hr, cs = sums.shape
    tr = _tile(hr, 256, 16)

    def body(kc_ref, own_ref, a_ref, b_ref, c_ref, o_ref):
        del kc_ref
        o_ref[...] = (own_ref[0].astype(F32) + a_ref[0].astype(F32)) + (b_ref[0].astype(F32) + c_ref[0].astype(F32))

    slot = lambda j: pl.BlockSpec((1, tr, cs), lambda i, kc_ref: ((kc_ref[0] + j) % N_CHIPS, i, 0))
    return pl.pallas_call(
        body, name=name,
        grid_spec=pltpu.PrefetchScalarGridSpec(
            num_scalar_prefetch=1, grid=(hr // tr,), in_specs=[slot(0), slot(1), slot(2), slot(3)],
            out_specs=pl.BlockSpec((tr, cs), lambda i, kc_ref: (kc_ref[1] * (hr // tr) + i, 0))),
        out_shape=jax.ShapeDtypeStruct((2 * hr, cs), F32),
        compiler_params=_params(1, False))(kc, sums, landed, landed, landed)


def _exchange_halves(shards, name):
    nw = len(shards)

    def body(*refs):
        out_refs = refs[nw:2 * nw]
        send, recv = refs[2 * nw:]
        x, y, c = _my_pos()
        sib = (x, y, 1 - c)
        copies = []
        for w in range(nw):
            hr = shards[w].shape[0] // 2
            mine = out_refs[w].at[pl.ds(pl.multiple_of(c * hr, 8), hr), :]
            cp = _remote(mine, mine, send.at[w], recv.at[w], sib)
            cp.start()
            copies.append(cp)
        for w in range(nw):
            hr = shards[w].shape[0] // 2
            other = out_refs[w].at[pl.ds(pl.multiple_of((1 - c) * hr, 8), hr), :]
            _remote(other, other, send.at[w], recv.at[w], sib).wait_recv()
        for cp in copies:
            cp.wait_send()

    anyspec = pl.BlockSpec(memory_space=pl.ANY)
    return pl.pallas_call(
        body, name=name,
        out_shape=[jax.ShapeDtypeStruct(s.shape, F32) for s in shards],
        in_specs=[anyspec] * nw, out_specs=[anyspec] * nw,
        input_output_aliases={w: w for w in range(nw)},
        scratch_shapes=[pltpu.SemaphoreType.DMA((nw,)), pltpu.SemaphoreType.DMA((nw,))])(*shards)


def _cast_into_full(w, kc, axis, name):
    r, cs = w.shape
    tr = _tile(r, 256, 16)

    def body(kc_ref, w_ref, o_ref):
        del kc_ref
        o_ref[...] = w_ref[...].astype(BF16)

    if axis == 1:
        ospec = pl.BlockSpec((tr, cs), lambda i, kc_ref: (i, kc_ref[0]))
    else:
        ospec = pl.BlockSpec((tr, cs), lambda i, kc_ref: (kc_ref[0] * (r // tr) + i, 0))
    return pl.pallas_call(
        body, name=name,
        grid_spec=pltpu.PrefetchScalarGridSpec(
            num_scalar_prefetch=1, grid=(r // tr,), in_specs=[pl.BlockSpec((tr, cs), lambda i, kc_ref: (i, 0))],
            out_specs=ospec),
        out_shape=jax.ShapeDtypeStruct(_full_shape((r, cs), axis), BF16),
        compiler_params=_params(1, False))(kc, w)


def _adam_math(w, g, m, v):
    m2 = ADAM_B1 * m + (1.0 - ADAM_B1) * g
    v2 = ADAM_B2 * v + (1.0 - ADAM_B2) * (g * g)
    m_hat = m2 / (1.0 - ADAM_B1 ** ADAM_STEP)
    v_hat = v2 / (1.0 - ADAM_B2 ** ADAM_STEP)
    delta = -ADAM_LR * (m_hat / (jnp.sqrt(v_hat) + ADAM_EPS) + ADAM_WD * w)
    return delta, m2, v2


def _adam(w, g, m, v, name):
    r, cs = w.shape
    tr = _tile(r, 256, 8)

    def body(w_ref, g_ref, m_ref, v_ref, d_ref, m2_ref, v2_ref):
        d, m2, v2 = _adam_math(w_ref[...], g_ref[...], m_ref[...], v_ref[...])
        d_ref[...] = d
        m2_ref[...] = m2
        v2_ref[...] = v2

    spec = pl.BlockSpec((tr, cs), lambda i: (i, 0))
    shp = jax.ShapeDtypeStruct((r, cs), F32)
    return pl.pallas_call(body, name=name, grid=(r // tr,), in_specs=[spec] * 4, out_specs=[spec] * 3,
                          out_shape=[shp, shp, shp], compiler_params=_params(1, False))(w, g, m, v)


def _mod_rows(a16, w, b, name):
    D, n = w.shape
    tn = _tile(n, 512)

    def body(a_ref, w_ref, b_ref, o_ref):
        a = a_ref[...]
        o_ref[...] = _nn((a * _sigmoid(a)).astype(BF16), w_ref[...].astype(BF16)) + b_ref[...]

    return pl.pallas_call(
        body, name=name, grid=(n // tn,),
        in_specs=[pl.BlockSpec((16, D), lambda j: (0, 0)), pl.BlockSpec((D, tn), lambda j: (0, j)),
                  pl.BlockSpec((1, tn), lambda j: (0, j))],
        out_specs=pl.BlockSpec((16, tn), lambda j: (0, j)),
        out_shape=jax.ShapeDtypeStruct((16, n), F32), compiler_params=_params(1, False))(a16, w, b)


def _w_mod_update(a16, d16, w, m, v):
    D, n = w.shape
    tn = _tile(n, 256)

    def body(a_ref, d_ref, w_ref, m_ref, v_ref, g_ref, dl_ref, m2_ref, v2_ref, p_ref):
        @pl.when(pl.program_id(0) == 0)
        def _():
            p_ref[...] = jnp.zeros_like(p_ref)
        a = a_ref[...]
        db = d_ref[...].astype(BF16)
        wv = w_ref[...]
        g = _tn((a * _sigmoid(a)).astype(BF16), db)
        g_ref[...] = g
        d, m2, v2 = _adam_math(wv, g, m_ref[...], v_ref[...])
        dl_ref[...] = d
        m2_ref[...] = m2
        v2_ref[...] = v2
        p_ref[...] += _nt(db, wv.astype(BF16))

    wspec = pl.BlockSpec((D, tn), lambda j: (0, j))
    shp = jax.ShapeDtypeStruct((D, n), F32)
    return pl.pallas_call(
        body, name="w_mod_update", grid=(n // tn,),
        in_specs=[pl.BlockSpec((16, D), lambda j: (0, 0)), pl.BlockSpec((16, tn), lambda j: (0, j)), wspec, wspec, wspec],
        out_specs=[wspec, wspec, wspec, wspec, pl.BlockSpec((16, D), lambda j: (0, 0))],
        out_shape=[shp, shp, shp, shp, jax.ShapeDtypeStruct((16, D), F32)],
        compiler_params=_params(1))(a16, d16, w, m, v)


def _sum_devices(g8, name):
    _, R, Cc = g8.shape

    def body(g_ref, o_ref):
        t = g_ref[0]
        for d in range(1, N_DEV):
            t = t + g_ref[d]
        o_ref[...] = t

    return pl.pallas_call(body, name=name, out_shape=jax.ShapeDtypeStruct((R, Cc), F32))(g8)


def _c_ctx_grad(parts, c_ctx):
    D = c_ctx.shape[1]

    def body(p_ref, c_ref, o_ref):
        t = p_ref[0]
        for k in range(1, N_CHIPS):
            t = t + p_ref[2 * k]
        cv = c_ref[...]
        sg = _sigmoid(cv)
        o_ref[...] = t * (sg * (1.0 + cv * (1.0 - sg)))

    return pl.pallas_call(body, name="c_ctx_grad", out_shape=jax.ShapeDtypeStruct((1, D), F32))(parts, c_ctx)


def _pack_rows(items, nrows, width, name):
    arrays, plan = [], []
    for a, r0, nr, d0, c0 in items:
        for ai, b in enumerate(arrays):
            if b is a:
                break
        else:
            ai = len(arrays)
            arrays.append(a)
        plan.append((ai, r0, nr, d0, c0, a.shape[1]))

    def body(*refs):
        o_ref = refs[-1]
        o_ref[...] = jnp.zeros_like(o_ref)
        for ai, r0, nr, d0, c0, w in plan:
            o_ref[d0:d0 + nr, c0:c0 + w] = refs[ai][r0:r0 + nr, :]

    return pl.pallas_call(body, name=name, out_shape=jax.ShapeDtypeStruct((nrows, width), F32))(*arrays)


HBM_SPEC = pl.BlockSpec(memory_space=pltpu.HBM)
SEM_SPEC = pl.BlockSpec(memory_space=pltpu.SEMAPHORE)
SPLIT_PARAMS = pltpu.CompilerParams(has_side_effects=pltpu.SideEffectType.DATAFLOW_SIDE_EFFECTING)


def _in_hbm(a):
    return pltpu.with_memory_space_constraint(a, pltpu.HBM)


def _ag_chips_start(fulls, shapes, axes, after, name):
    nw = len(fulls)

    def body(*refs):
        in_refs, send, recv, token = refs[:nw], refs[nw + 1], refs[nw + 2], refs[-1]
        x, y, c = _my_pos()
        k0 = 2 * x + y
        for w in range(nw):
            own = _region(in_refs[w], k0, c, shapes[w], axes[w])
            for j, ch in enumerate(_other_chips(x, y)):
                _remote(own, own, send.at[3 * w + j], recv.at[3 * w + j], (ch[0], ch[1], c)).start()
        token[...] = jnp.zeros_like(token)

    return pl.pallas_call(
        body, name=name,
        out_shape=(pltpu.SemaphoreType.DMA((3 * nw,)), pltpu.SemaphoreType.DMA((3 * nw,)),
                   *[pltpu.HBM(f.shape, f.dtype) for f in fulls], jax.ShapeDtypeStruct((8, LANES), F32)),
        in_specs=[HBM_SPEC] * nw + [pl.BlockSpec(memory_space=pl.ANY)],
        out_specs=(SEM_SPEC, SEM_SPEC, *[HBM_SPEC] * nw, pl.BlockSpec(memory_space=pltpu.VMEM)),
        input_output_aliases={w: 2 + w for w in range(nw)},
        compiler_params=SPLIT_PARAMS)(*[_in_hbm(f) for f in fulls], after)


def _ag_chips_wait(send, recv, fulls, shapes, axes, after, name):
    nw = len(fulls)

    def body(*refs):
        in_refs, send_ref, recv_ref = refs[:nw], refs[nw], refs[nw + 1]
        x, y, c = _my_pos()
        k0 = 2 * x + y
        for w in range(nw):
            own = _region(in_refs[w], k0, c, shapes[w], axes[w])
            for j, ch in enumerate(_other_chips(x, y)):
                got = _region(in_refs[w], 2 * ch[0] + ch[1], c, shapes[w], axes[w])
                cp = _remote(own, got, send_ref.at[3 * w + j], recv_ref.at[3 * w + j], (ch[0], ch[1], c))
                cp.wait_send()
                cp.wait_recv()

    return pl.pallas_call(
        body, name=name,
        out_shape=tuple(pltpu.HBM(f.shape, f.dtype) for f in fulls),
        in_specs=[HBM_SPEC] * nw + [SEM_SPEC, SEM_SPEC, pl.BlockSpec(memory_space=pl.ANY)],
        out_specs=tuple([HBM_SPEC] * nw),
        input_output_aliases={w: w for w in range(nw)},
        compiler_params=SPLIT_PARAMS)(*fulls, send, recv, after)


def _ag_forward(fulls, shapes, axes, name):
    nw = len(fulls)

    def body(*refs):
        out_refs = refs[nw:2 * nw]
        send, recv = refs[2 * nw:]
        x, y, c = _my_pos()
        sib = (x, y, 1 - c)
        chips = _other_chips(x, y)
        copies = []
        for w in range(nw):
            for j, ch in enumerate(chips):
                got = _region(out_refs[w], 2 * ch[0] + ch[1], c, shapes[w], axes[w])
                cp = _remote(got, got, send.at[w, j], recv.at[w, j], sib)
                cp.start()
                copies.append(cp)
        for w in range(nw):
            for j, ch in enumerate(chips):
                got = _region(out_refs[w], 2 * ch[0] + ch[1], 1 - c, shapes[w], axes[w])
                _remote(got, got, send.at[w, j], recv.at[w, j], sib).wait_recv()
        for cp in copies:
            cp.wait_send()

    anyspec = pl.BlockSpec(memory_space=pl.ANY)
    return pl.pallas_call(
        body, name=name,
        out_shape=[jax.ShapeDtypeStruct(f.shape, BF16) for f in fulls],
        in_specs=[anyspec] * nw, out_specs=[anyspec] * nw,
        input_output_aliases={w: w for w in range(nw)},
        scratch_shapes=[pltpu.SemaphoreType.DMA((nw, 3)), pltpu.SemaphoreType.DMA((nw, 3))])(*fulls)


def _rs_chips_start(sums, name):
    nw = len(sums)

    def body(*refs):
        s_refs, l_refs, send, recv, token = refs[:nw], refs[nw:2 * nw], refs[2 * nw], refs[2 * nw + 1], refs[-1]
        x, y, c = _my_pos()
        k0 = 2 * x + y
        for w in range(nw):
            for j, ch in enumerate(_other_chips(x, y)):
                _remote(s_refs[w].at[2 * ch[0] + ch[1]], l_refs[w].at[k0], send.at[3 * w + j], recv.at[3 * w + j],
                        (ch[0], ch[1], c)).start()
        token[...] = jnp.zeros_like(token)

    thru = [pltpu.HBM(s.shape, s.dtype) for s in sums]
    return pl.pallas_call(
        body, name=name,
        out_shape=(pltpu.SemaphoreType.DMA((3 * nw,)), pltpu.SemaphoreType.DMA((3 * nw,)), *thru, *thru,
                   jax.ShapeDtypeStruct((8, LANES), F32)),
        in_specs=[HBM_SPEC] * (2 * nw),
        out_specs=(SEM_SPEC, SEM_SPEC, *[HBM_SPEC] * (2 * nw), pl.BlockSpec(memory_space=pltpu.VMEM)),
        input_output_aliases={i: 2 + i for i in range(2 * nw)},
        compiler_params=SPLIT_PARAMS)(*[_in_hbm(s) for s in sums], *[_in_hbm(lax.empty(s.shape, s.dtype)) for s in sums])


def _rs_chips_wait(send, recv, sums, lands, after, name):
    nw = len(sums)

    def body(*refs):
        s_refs, l_refs, send_ref, recv_ref = refs[:nw], refs[nw:2 * nw], refs[2 * nw], refs[2 * nw + 1]
        x, y, c = _my_pos()
        for w in range(nw):
            for j, ch in enumerate(_other_chips(x, y)):
                kj = 2 * ch[0] + ch[1]
                cp = _remote(s_refs[w].at[kj], l_refs[w].at[kj], send_ref.at[3 * w + j], recv_ref.at[3 * w + j],
                             (ch[0], ch[1], c))
                cp.wait_send()
                cp.wait_recv()

    thru = tuple(pltpu.HBM(s.shape, s.dtype) for s in sums)
    return pl.pallas_call(
        body, name=name, out_shape=thru + thru,
        in_specs=[HBM_SPEC] * (2 * nw) + [SEM_SPEC, SEM_SPEC, pl.BlockSpec(memory_space=pl.ANY)],
        out_specs=tuple([HBM_SPEC] * (2 * nw)),
        input_output_aliases={i: i for i in range(2 * nw)},
        compiler_params=SPLIT_PARAMS)(*sums, *lands, send, recv, after)


LOSS_LANE = 64


def kernel(x, c, ctx, c_ctx, w_mod, b_mod, norm_mix, norm_ffn, w_in, ret_decay, attn_sink, w_out, w_gate, w_up, w_down, norm_final, loss_target, m_c_ctx, m_w_mod, m_b_mod, m_norm_mix, m_norm_ffn, m_w_in, m_ret_decay, m_attn_sink, m_w_out, m_w_gate, m_w_up, m_w_down, m_norm_final, v_c_ctx, v_w_mod, v_b_mod, v_norm_mix, v_norm_ffn, v_w_in, v_ret_decay, v_attn_sink, v_w_out, v_w_gate, v_w_up, v_w_down, v_norm_final):
    D = x.shape[-1]
    n3 = w_mod.shape[-1]
    xi, yi, ci = _my_pos()
    b = 4 * xi + 2 * yi + ci
    k0 = 2 * xi + yi
    cvec = jnp.reshape(ci, (1,)).astype(jnp.int32)
    kc = jnp.stack([k0, ci]).astype(jnp.int32)

    dense = [("w_in", w_in[0], 1), ("w_out", w_out[0], 0), ("w_gate", w_gate[0], 1), ("w_up", w_up[0], 1),
             ("w_down", w_down[0], 0)]
    axes = [a for _, _, a in dense]
    shapes = [w.shape for _, w, _ in dense]
    c_all = _allgather8(c, "gather_c").reshape(N_DEV, D)
    c_ctx2 = c_ctx.reshape(1, D)
    a16 = _pack_rows([(c_all, 0, N_DEV, 0, 0), (c_ctx2, 0, 1, N_DEV, 0)], 16, D, "pack_cond")
    b_cols = lax.dynamic_slice_in_dim(b_mod, k0 * n3, n3, axis=1)
    mod16 = _mod_rows(a16, w_mod[0], b_cols, "mod_rows")
    mod_all = _allgather8(mod16, "gather_mod")

    own_in = _cast_into_full(dense[0][1], kc, axes[0], "cast_w_in")
    agi = _ag_chips_start([own_in], shapes[:1], axes[:1], mod_all, "ag_in_start")
    own16 = [_cast_into_full(w, kc, a, "cast_" + n) for n, w, a in dense[1:]]
    (f_in,) = _ag_forward(list(_ag_chips_wait(agi[0], agi[1], [agi[2]], shapes[:1], axes[:1], own16[-1], "ag_in_wait")),
                          shapes[:1], axes[:1], "ag_in_forward")
    ag = _ag_chips_start(own16, shapes[1:], axes[1:], f_in, "ag_rest_start")
    ag_send, ag_recv, ag_thru, ag_tok = ag[0], ag[1], list(ag[2:-1]), ag[-1][0:1, 0:1]

    def rest_weights(after):
        landed_w = _ag_chips_wait(ag_send, ag_recv, ag_thru, shapes[1:], axes[1:], after, "ag_rest_wait")
        return _ag_forward(list(landed_w), shapes[1:], axes[1:], "ag_rest_forward")
    mine = jnp.stack([lax.dynamic_index_in_dim(mod_all, 2 * k + ci, 0, keepdims=False) for k in range(N_CHIPS)])
    mod = lax.dynamic_index_in_dim(mine, b, 1, keepdims=False).reshape(6, D)
    modc = mine[:, N_DEV].reshape(6, D)

    lg = -jnp.exp(ret_decay[0])

    index = {n: i for i, (n, _, _) in enumerate(dense)}
    pending, done = [], {}

    def on_grads(names, gs):
        ids = [index[n] for n in names]
        shp, axs = [shapes[i] for i in ids], [axes[i] for i in ids]
        from_sib = _rs_sibling(gs, shp, axs, "rs_sibling_" + names[0])
        sums = [_add_halves(g, r, cvec, s, a, "add_halves_" + n) for g, r, s, a, n in zip(gs, from_sib, shp, axs, names)]
        st = _rs_chips_start(sums, "rs_chips_start_" + names[0])
        nw = len(names)
        pending.append((names, st[0], st[1], list(st[2:2 + nw]), list(st[2 + nw:2 + 2 * nw])))
        return st[-1][0:1, 0:1]

    out = _local_step(x[0], ctx[0], loss_target[0], mod, modc, norm_mix + ag_tok, norm_ffn, norm_final.reshape(1, D), lg,
                      attn_sink, f_in, rest_weights, on_grads)

    def finish(group, after):
        names, send, recv, sums, lands = group
        res = _rs_chips_wait(send, recv, sums, lands, after, "rs_chips_wait_" + names[0])
        return [_sum_chips(res[i], res[len(names) + i], kc, "sum_chips_" + n) for i, n in enumerate(names)]

    assert pending[-1][0] == ["w_in"]
    rest_names = [n for g in pending[:-1] for n in g[0]]
    rest_halves = [h for g in pending[:-1] for h in finish(g, out["grad_x"])]
    g_rest = dict(zip(rest_names, _exchange_halves(rest_halves, "exchange_halves_rest")))

    nh = 2 * RET_HEADS
    small_all = _allgather8(out["small"], "gather_small")
    tot = _sum_devices(small_all, "sum_small")
    g_b_mod = (tot[0:6] + tot[6:12]).reshape(1, 6 * D)
    dmodc_tot = tot[6:12].reshape(1, 6 * D)
    dmod_rows = small_all[:, 0:6].reshape(N_DEV, 6 * D)
    d16 = _pack_rows([(dmod_rows, 0, N_DEV, 0, 0), (dmodc_tot, 0, 1, N_DEV, 0)], 16, 6 * D, "pack_dmod")
    d16 = lax.dynamic_slice_in_dim(d16, k0 * n3, n3, axis=1)
    g_w_mod, dl_w_mod, m2_w_mod, v2_w_mod, part = _w_mod_update(a16, d16, w_mod[0], m_w_mod[0], v_w_mod[0])
    part_all = _allgather8(part[N_DEV:N_DEV + 1], "gather_c_ctx")
    g_c_ctx = _c_ctx_grad(part_all, c_ctx2)
    loss = tot[15, LOSS_LANE]
    g_ret_decay = tot[15, :nh].reshape(1, 2, RET_HEADS)
    g_sink = tot[15, nh:nh + ATT_HEADS].reshape(1, ATT_HEADS)

    def pack(cc, bm, nm, nf, nfin, rd, sk, name):
        rd2 = rd.reshape(2, RET_HEADS)
        return _pack_rows([(bm.reshape(6, D), 0, 6, 0, 0), (cc.reshape(1, D), 0, 1, 6, 0), (nm.reshape(1, D), 0, 1, 7, 0),
                           (nf.reshape(1, D), 0, 1, 8, 0), (nfin.reshape(1, D), 0, 1, 9, 0),
                           (rd2, 0, 1, 10, 0), (rd2, 1, 1, 10, RET_HEADS), (sk.reshape(1, ATT_HEADS), 0, 1, 10, nh)],
                          16, D, name)

    w_s = pack(c_ctx, b_mod, norm_mix, norm_ffn, norm_final, ret_decay, attn_sink, "pack_w")
    g_s = _pack_rows([(g_b_mod.reshape(6, D), 0, 6, 0, 0), (g_c_ctx, 0, 1, 6, 0), (tot, 12, 3, 7, 0),
                      (tot[15:16, 0:nh + ATT_HEADS], 0, 1, 10, 0)], 16, D, "pack_g")
    m_s = pack(m_c_ctx, m_b_mod, m_norm_mix, m_norm_ffn, m_norm_final, m_ret_decay, m_attn_sink, "pack_m")
    v_s = pack(v_c_ctx, v_b_mod, v_norm_mix, v_norm_ffn, v_norm_final, v_ret_decay, v_attn_sink, "pack_v")
    small_upd = _adam(w_s, g_s, m_s, v_s, "adam_small")

    def unpack(t):
        return dict(b_mod=t[0:6].reshape(1, 6 * D), c_ctx=t[6], norm_mix=t[7:8], norm_ffn=t[8:9], norm_final=t[9],
                    ret_decay=t[10, :nh].reshape(1, 2, RET_HEADS), attn_sink=t[10, nh:nh + ATT_HEADS].reshape(1, ATT_HEADS))

    dense_w = dict(w_in=(w_in, m_w_in, v_w_in), w_out=(w_out, m_w_out, v_w_out), w_gate=(w_gate, m_w_gate, v_w_gate),
                   w_up=(w_up, m_w_up, v_w_up), w_down=(w_down, m_w_down, v_w_down))
    grads = dict(unpack(g_s), w_mod=g_w_mod[None])
    upd = [dict(unpack(t)) for t in small_upd]
    upd[0]["w_mod"], upd[1]["w_mod"], upd[2]["w_mod"] = dl_w_mod[None], m2_w_mod[None], v2_w_mod[None]
    def update(n, g):
        w_, m_, v_ = dense_w[n]
        res = _adam(w_[0], g, m_[0], v_[0], "adam_" + n)
        grads[n] = g[None]
        for u, r_ in zip(upd, res):
            u[n] = r_[None]
        return res[0]

    dep = small_upd[0][0:1, 0:1] + dl_w_mod[0:1, 0:1]
    for n in rest_names:
        dep = dep + update(n, g_rest[n])[0:1, 0:1]
    (g_in,) = _exchange_halves(finish(pending[-1], dep), "exchange_halves_in")
    update("w_in", g_in)

    order = ['c_ctx', 'w_mod', 'b_mod', 'norm_mix', 'norm_ffn', 'w_in', 'ret_decay', 'attn_sink', 'w_out', 'w_gate',
             'w_up', 'w_down', 'norm_final']
    outs = [loss, out["grad_x"][None]] + [grads[n] for n in order]
    for u in upd:
        outs += [u[n] for n in order]
    return tuple(outs)
```

```python
import functools
import numpy as np
import jax
import jax.numpy as jnp
from jax import lax
from jax.experimental import pallas as pl
from jax.experimental.pallas import tpu as pltpu

F32 = jnp.float32
BF16 = jnp.bfloat16

RET_HEADS = 8
RET_DK = 64
RET_DV = 128
CHUNK = 128
ATT_HEADS = 16
ATT_KV_HEADS = 4
ATT_DH = 64
GRID_W = 64
ROPE_BASE = 10000.0
NORM_EPS = 1e-6
ADAM_LR = 0.001
ADAM_B1 = 0.9
ADAM_B2 = 0.999
ADAM_EPS = 1e-08
ADAM_WD = 0.01
ADAM_STEP = 10
NEG = -1e30
LANES = 128
VMEM_LIMIT = 56 * 1024 * 1024
ROWS_PER_LATCH = 1024
MESH = pl.DeviceIdType.MESH
N_CHIPS = 4
N_DEV = 8


def _nn(a, b):
    return jnp.dot(a, b, preferred_element_type=F32)


def _nt(a, b):
    return lax.dot_general(a, b, (((1,), (1,)), ((), ())), preferred_element_type=F32)


def _tn(a, b):
    return lax.dot_general(a, b, (((0,), (0,)), ((), ())), preferred_element_type=F32)


def _tile(n, pref, unit=LANES):
    t = min(n, pref)
    t -= t % unit
    while t > unit and n % t:
        t -= unit
    if t <= 0 or n % t:
        return n
    return t


def _params(ndim, vmem=True):
    return pltpu.CompilerParams(dimension_semantics=("arbitrary",) * ndim,
                                vmem_limit_bytes=VMEM_LIMIT if vmem else None)


def _sigmoid(x):
    return 0.5 * jnp.tanh(0.5 * x) + 0.5


def _fsum(x):
    return jnp.sum(jnp.sum(x, axis=1, keepdims=True), axis=0, keepdims=True)


def _rope_tables(L):
    lane = np.arange(LANES)
    d = lane % 64
    inv_r = jnp.asarray(ROPE_BASE, F32) ** (-jnp.arange(32, dtype=F32) / 32)
    t = jnp.arange(L)
    ang_r = t.astype(F32)[:, None] * jnp.tile(inv_r, LANES // 32)[None, :]
    Rr = np.zeros((LANES, LANES), np.float32)
    for l in range(LANES):
        if d[l] < 32:
            Rr[l + 32, l] = -1.0
        else:
            Rr[l - 32, l] = 1.0
    inv_a = jnp.asarray(ROPE_BASE, F32) ** (-jnp.arange(16, dtype=F32) / 16)
    rows = (t // GRID_W).astype(F32)
    cols = (t % GRID_W).astype(F32)
    dd = d % 32
    pos = jnp.where(jnp.asarray(d < 32)[None, :], rows[:, None], cols[:, None])
    ang_a = pos * jnp.tile(inv_a, LANES // 16)[None, :]
    Ra = np.zeros((LANES, LANES), np.float32)
    for l in range(LANES):
        if dd[l] < 16:
            Ra[l + 16, l] = -1.0
        else:
            Ra[l - 16, l] = 1.0
    D0 = np.zeros((LANES, LANES), np.float32)
    D1 = np.zeros((LANES, LANES), np.float32)
    for l in range(LANES):
        D0[l % 64, l] = 1.0
        D1[64 + l % 64, l] = 1.0
    return dict(
        Cr=jnp.cos(ang_r), Sr=jnp.sin(ang_r), Rr=jnp.asarray(Rr, BF16), RrT=jnp.asarray(Rr.T, BF16),
        Ca=jnp.cos(ang_a), Sa=jnp.sin(ang_a), Ra=jnp.asarray(Ra, BF16), RaT=jnp.asarray(Ra.T, BF16),
        D0=jnp.asarray(D0, BF16), D1=jnp.asarray(D1, BF16),
        D0T=jnp.asarray(D0.T, BF16), D1T=jnp.asarray(D1.T, BF16))


def _norm_mod(xf, g, sh, sc):
    r = lax.rsqrt(jnp.mean(xf * xf, axis=-1, keepdims=True) + NORM_EPS)
    return (xf * r * g) * (1.0 + sc) + sh


def _norm_mod_matmul(x, g, sh, sc, w, name):
    M, D = x.shape
    N = w.shape[1]
    tm, tn = _tile(M, ROWS_PER_LATCH, 8), _tile(N, 768)

    def body(x_ref, g_ref, sh_ref, sc_ref, w_ref, p_ref, h_ref, hs):
        @pl.when(pl.program_id(1) == 0)
        def _():
            hb = _norm_mod(x_ref[...], g_ref[...], sh_ref[...], sc_ref[...]).astype(BF16)
            hs[...] = hb
            h_ref[...] = hb
        p_ref[...] = _nn(hs[...], w_ref[...]).astype(BF16)

    vec = pl.BlockSpec((1, D), lambda i, j: (0, 0))
    return pl.pallas_call(
        body, name=name, grid=(M // tm, N // tn),
        in_specs=[pl.BlockSpec((tm, D), lambda i, j: (i, 0)), vec, vec, vec,
                  pl.BlockSpec((D, tn), lambda i, j: (0, j))],
        out_specs=[pl.BlockSpec((tm, tn), lambda i, j: (i, j)), pl.BlockSpec((tm, D), lambda i, j: (i, 0))],
        out_shape=[jax.ShapeDtypeStruct((M, N), BF16), jax.ShapeDtypeStruct((M, D), BF16)],
        scratch_shapes=[pltpu.VMEM((tm, D), BF16)],
        compiler_params=_params(2))(x, g, sh, sc, w)


def _proj_residual(a, w, xres, gt, name):
    M, K = a.shape
    N = w.shape[1]
    tm, tn = _tile(M, ROWS_PER_LATCH, 8), _tile(N, 1024 if K <= 2048 else 512)

    def body(a_ref, w_ref, x_ref, gt_ref, xo_ref, o_ref):
        o = _nn(a_ref[...], w_ref[...])
        o_ref[...] = o.astype(BF16)
        xo_ref[...] = x_ref[...] + gt_ref[...] * o

    return pl.pallas_call(
        body, name=name, grid=(M // tm, N // tn),
        in_specs=[pl.BlockSpec((tm, K), lambda i, j: (i, 0)), pl.BlockSpec((K, tn), lambda i, j: (0, j)),
                  pl.BlockSpec((tm, tn), lambda i, j: (i, j)), pl.BlockSpec((1, tn), lambda i, j: (0, j))],
        out_specs=[pl.BlockSpec((tm, tn), lambda i, j: (i, j)), pl.BlockSpec((tm, tn), lambda i, j: (i, j))],
        out_shape=[jax.ShapeDtypeStruct((M, N), F32), jax.ShapeDtypeStruct((M, N), BF16)],
        compiler_params=_params(2))(a, w, xres, gt)


def _ffn_in(x1, g, sh, sc, wg, wu):
    M, D = x1.shape
    N = wg.shape[1]
    tm, tn = _tile(M, ROWS_PER_LATCH, 8), _tile(N, 512)

    def body(x_ref, g_ref, sh_ref, sc_ref, wg_ref, wu_ref, G_ref, U_ref, A_ref, h_ref, hs):
        @pl.when(pl.program_id(1) == 0)
        def _():
            hb = _norm_mod(x_ref[...], g_ref[...], sh_ref[...], sc_ref[...]).astype(BF16)
            hs[...] = hb
            h_ref[...] = hb
        G = _nn(hs[...], wg_ref[...])
        U = _nn(hs[...], wu_ref[...])
        G_ref[...] = G.astype(BF16)
        U_ref[...] = U.astype(BF16)
        A_ref[...] = (G * _sigmoid(G) * U).astype(BF16)

    vec = pl.BlockSpec((1, D), lambda i, j: (0, 0))
    wspec = pl.BlockSpec((D, tn), lambda i, j: (0, j))
    ospec = pl.BlockSpec((tm, tn), lambda i, j: (i, j))
    big = jax.ShapeDtypeStruct((M, N), BF16)
    return pl.pallas_call(
        body, name="ffn_in", grid=(M // tm, N // tn),
        in_specs=[pl.BlockSpec((tm, D), lambda i, j: (i, 0)), vec, vec, vec, wspec, wspec],
        out_specs=[ospec, ospec, ospec, pl.BlockSpec((tm, D), lambda i, j: (i, 0))],
        out_shape=[big, big, big, jax.ShapeDtypeStruct((M, D), BF16)],
        scratch_shapes=[pltpu.VMEM((tm, D), BF16)],
        compiler_params=_params(2))(x1, g, sh, sc, wg, wu)


def _final(x2, gn, tgt):
    M, D = x2.shape
    tm = _tile(M, 256, 8)

    def body(x_ref, g_ref, t_ref, dx_ref, loss_ref, dg_ref):
        @pl.when(pl.program_id(0) == 0)
        def _():
            loss_ref[...] = jnp.zeros_like(loss_ref)
            dg_ref[...] = jnp.zeros_like(dg_ref)
        x = x_ref[...]
        g = g_ref[...]
        r = lax.rsqrt(jnp.mean(x * x, axis=-1, keepdims=True) + NORM_EPS)
        xh = x * r
        e = xh * g - t_ref[...]
        loss_ref[...] += (0.5 / D) * _fsum(e * e)
        dy = e * (1.0 / D)
        dg_ref[...] += jnp.sum(dy * xh, axis=0, keepdims=True)
        dxh = dy * g
        dx_ref[...] = r * (dxh - xh * jnp.mean(dxh * xh, axis=-1, keepdims=True))

    row = pl.BlockSpec((tm, D), lambda i: (i, 0))
    return pl.pallas_call(
        body, name="final_loss", grid=(M // tm,),
        in_specs=[row, pl.BlockSpec((1, D), lambda i: (0, 0)), row],
        out_specs=[row, pl.BlockSpec((1, LANES), lambda i: (0, 0)), pl.BlockSpec((1, D), lambda i: (0, 0))],
        out_shape=[jax.ShapeDtypeStruct((M, D), F32), jax.ShapeDtypeStruct((1, LANES), F32),
                   jax.ShapeDtypeStruct((1, D), F32)],
        compiler_params=_params(1))(x2, gn, tgt)


def _col_group(blk0, nblk):
    return int(np.gcd(blk0, nblk)) if blk0 else nblk


def _rope_cols(src, blk0, nblk, Ct, St, R, scale, rope, name):
    M = src.shape[0]
    tm = _tile(M, 512, 8)
    wb = _col_group(blk0, nblk)

    def body(x_ref, c_ref, s_ref, r_ref, o_ref):
        for j in range(wb):
            cols = slice(j * LANES, (j + 1) * LANES)
            x = x_ref[:, cols]
            xf = x.astype(F32)
            if rope:
                xf = xf * c_ref[...] + _nn(x.astype(BF16), r_ref[...]) * s_ref[...]
            o_ref[:, cols] = (xf * scale).astype(BF16)

    tab = pl.BlockSpec((tm, LANES), lambda i, j: (i, 0))
    return pl.pallas_call(
        body, name=name, grid=(M // tm, nblk // wb),
        in_specs=[pl.BlockSpec((tm, wb * LANES), lambda i, j: (i, blk0 // wb + j)), tab, tab,
                  pl.BlockSpec((LANES, LANES), lambda i, j: (0, 0))],
        out_specs=pl.BlockSpec((tm, wb * LANES), lambda i, j: (i, j)),
        out_shape=jax.ShapeDtypeStruct((M, nblk * LANES), BF16),
        compiler_params=_params(2, False))(src, Ct, St, R)


def _dup_heads(src, blk0, npair, Ct, St, R, D0, D1, rope, name):
    M = src.shape[0]
    tm = _tile(M, 512, 8)

    def body(x_ref, c_ref, s_ref, r_ref, d0_ref, d1_ref, o_ref):
        x = x_ref[...]
        if rope:
            x = (x.astype(F32) * c_ref[...] + _nn(x, r_ref[...]) * s_ref[...]).astype(BF16)
        o_ref[0] = _nn(x, d0_ref[...]).astype(BF16)
        o_ref[1] = _nn(x, d1_ref[...]).astype(BF16)

    tab = pl.BlockSpec((tm, LANES), lambda i, p: (i, 0))
    mat = pl.BlockSpec((LANES, LANES), lambda i, p: (0, 0))
    return pl.pallas_call(
        body, name=name, grid=(M // tm, npair),
        in_specs=[pl.BlockSpec((tm, LANES), lambda i, p: (i, blk0 + p)), tab, tab, mat, mat, mat],
        out_specs=pl.BlockSpec((2, tm, LANES), lambda i, p: (p, i, 0)),
        out_shape=jax.ShapeDtypeStruct((2 * npair, M, LANES), BF16),
        compiler_params=_params(2, False))(src, Ct, St, R, D0, D1)


def _unrope_cols(dsrc, dst, blk0, nblk, Ct, St, RT, scale, rope, name):
    M = dsrc.shape[0]
    tm = _tile(M, 512, 8)
    wb = _col_group(blk0, nblk)

    def body(x_ref, c_ref, s_ref, r_ref, dst_ref, o_ref):
        del dst_ref
        for j in range(wb):
            cols = slice(j * LANES, (j + 1) * LANES)
            xf = x_ref[:, cols].astype(F32)
            if rope:
                xf = xf * c_ref[...] + _nn((xf * s_ref[...]).astype(BF16), r_ref[...])
            o_ref[:, cols] = (xf * scale).astype(BF16)

    tab = pl.BlockSpec((tm, LANES), lambda i, j: (i, 0))
    return pl.pallas_call(
        body, name=name, grid=(M // tm, nblk // wb),
        in_specs=[pl.BlockSpec((tm, wb * LANES), lambda i, j: (i, j)), tab, tab,
                  pl.BlockSpec((LANES, LANES), lambda i, j: (0, 0)),
                  pl.BlockSpec(memory_space=pl.ANY)],
        out_specs=pl.BlockSpec((tm, wb * LANES), lambda i, j: (i, blk0 // wb + j)),
        out_shape=jax.ShapeDtypeStruct(dst.shape, dst.dtype),
        input_output_aliases={4: 0},
        compiler_params=_params(2, False))(dsrc, Ct, St, RT, dst)


def _fold_heads(parts, dst, blk0, npair, Ct, St, RT, D0T, D1T, rope, name):
    M = parts[0][0].shape[1]
    nb = M // CHUNK
    R = _tile(M, 1024, CHUNK)
    rb = R // CHUNK
    nrefs = sum(1 if s == 0 else 2 for _, s in parts)

    def body(*refs):
        part_refs = list(refs[:nrefs])
        c_ref, s_ref, r_ref, d0_ref, d1_ref, dst_ref, o_ref = refs[nrefs:]
        del dst_ref
        i = pl.program_id(0)
        tot = [jnp.zeros((R, LANES), F32), jnp.zeros((R, LANES), F32)]
        for _, shift in parts:
            main = part_refs.pop(0)
            if shift == 0:
                for e in range(2):
                    tot[e] = tot[e] + main[e].astype(F32)
                continue
            edge = part_refs.pop(0)
            ok = (i + 1) * rb <= nb - 1 if shift > 0 else i > 0
            for e in range(2):
                ed = jnp.where(ok, edge[e].astype(F32), 0.0)
                if rb == 1:
                    tot[e] = tot[e] + ed
                elif shift > 0:
                    tot[e] = tot[e] + jnp.concatenate([main[e, CHUNK:, :].astype(F32), ed], axis=0)
                else:
                    tot[e] = tot[e] + jnp.concatenate([ed, main[e, :R - CHUNK, :].astype(F32)], axis=0)
        f = _nn(tot[0].astype(BF16), d0_ref[...]) + _nn(tot[1].astype(BF16), d1_ref[...])
        if rope:
            f = f * c_ref[...] + _nn((f * s_ref[...]).astype(BF16), r_ref[...])
        o_ref[...] = f.astype(BF16)

    in_specs, args = [], []
    for a, shift in parts:
        assert shift in (-1, 0, 1)
        in_specs.append(pl.BlockSpec((2, R, LANES), lambda i, p: (p, i, 0)))
        args.append(a)
        if shift > 0:
            in_specs.append(pl.BlockSpec((2, CHUNK, LANES), lambda i, p: (p, jnp.minimum((i + 1) * rb, nb - 1), 0)))
            args.append(a)
        elif shift < 0:
            in_specs.append(pl.BlockSpec((2, CHUNK, LANES), lambda i, p: (p, jnp.maximum(i * rb - 1, 0), 0)))
            args.append(a)
    tab = pl.BlockSpec((R, LANES), lambda i, p: (i, 0))
    mat = pl.BlockSpec((LANES, LANES), lambda i, p: (0, 0))
    return pl.pallas_call(
        body, name=name, grid=(M // R, npair),
        in_specs=in_specs + [tab, tab, mat, mat, mat, pl.BlockSpec(memory_space=pl.ANY)],
        out_specs=pl.BlockSpec((R, LANES), lambda i, p: (i, blk0 + p)),
        out_shape=jax.ShapeDtypeStruct(dst.shape, dst.dtype),
        input_output_aliases={nrefs + 5: 0},
        compiler_params=_params(2, False))(*args, Ct, St, RT, D0T, D1T, dst)


def _head_masks():
    lane = lax.broadcasted_iota(jnp.int32, (1, LANES), 1)
    return [lane < 64, lane >= 64]


def _decay_vecs(lam, mu):
    i = lax.broadcasted_iota(jnp.int32, (CHUNK, 1), 0).astype(F32)
    return dict(qf=jnp.exp(lam * (i + 1.0)), kf=jnp.exp(lam * (CHUNK - 1.0 - i)),
                qb=jnp.exp(mu * (CHUNK - i)), kb=jnp.exp(mu * i),
                gf=jnp.exp(lam * float(CHUNK)), gb=jnp.exp(mu * float(CHUNK)), i=i)


def _decay_mask(lam, mu):
    r = lax.broadcasted_iota(jnp.int32, (CHUNK, CHUNK), 0)
    c = lax.broadcasted_iota(jnp.int32, (CHUNK, CHUNK), 1)
    rel = (r - c).astype(F32)
    low = rel >= 0.0
    mf = jnp.exp(lam * jnp.maximum(rel, 0.0))
    mb = jnp.exp(mu * jnp.maximum(-rel, 0.0))
    return jnp.where(low, mf, mb), rel, low


def _lam_of(lg_ref, row, idx):
    return jnp.full((1, 1), lg_ref[row, idx], F32)


def _group_index(pair_blk, npairs):
    assert pair_blk % npairs == 0
    return pair_blk // npairs


def _ret_states_fwd(Kr, P, Krc, Pc, lg, rv_blk, npairs):
    L = Kr.shape[0]
    Lc = Krc.shape[0]
    N, ncc = L // CHUNK, Lc // CHUNK
    rv_grp = _group_index(rv_blk, npairs)

    heads = [(p, h) for p in range(npairs) for h in range(2)]
    kcols = lambda p: slice(p * LANES, (p + 1) * LANES)
    vcols = lambda p, h: slice((2 * p + h) * LANES, (2 * p + h + 1) * LANES)

    def body(lg_ref, k_ref, v_ref, kc_ref, vc_ref, sf_ref, S):
        n = pl.program_id(0)
        masks = _head_masks()

        @pl.when(n == 0)
        def _():
            for p, h in heads:
                lam = _lam_of(lg_ref, 0, 2 * p + h)
                dv = _decay_vecs(lam, lam)
                s = jnp.zeros((LANES, LANES), F32)
                for cc in range(ncc):
                    rows = slice(cc * CHUNK, (cc + 1) * CHUNK)
                    kw = jnp.where(masks[h], kc_ref[rows, kcols(p)].astype(F32) * dv["kf"], 0.0).astype(BF16)
                    s = dv["gf"] * s + _tn(kw, vc_ref[rows, vcols(p, h)])
                S[p, h] = s

        for p, h in heads:
            lam = _lam_of(lg_ref, 0, 2 * p + h)
            dv = _decay_vecs(lam, lam)
            s = S[p, h]
            sf_ref[p, 0, h] = s.astype(BF16)
            kw = jnp.where(masks[h], k_ref[:, kcols(p)].astype(F32) * dv["kf"], 0.0).astype(BF16)
            S[p, h] = dv["gf"] * s + _tn(kw, v_ref[:, vcols(p, h)])

    wq, wv = npairs * LANES, npairs * 2 * LANES
    return pl.pallas_call(
        body, name="ret_states_fwd", grid=(N,),
        in_specs=[pl.BlockSpec(memory_space=pltpu.SMEM),
                  pl.BlockSpec((CHUNK, wq), lambda n: (n, 0)),
                  pl.BlockSpec((CHUNK, wv), lambda n: (n, rv_grp)),
                  pl.BlockSpec((Lc, wq), lambda n: (0, 0)),
                  pl.BlockSpec((Lc, wv), lambda n: (0, rv_grp))],
        out_specs=pl.BlockSpec((npairs, 1, 2, LANES, LANES), lambda n: (0, n, 0, 0, 0)),
        out_shape=jax.ShapeDtypeStruct((npairs, N, 2, LANES, LANES), BF16),
        scratch_shapes=[pltpu.VMEM((npairs, 2, LANES, LANES), F32)],
        compiler_params=_params(1, False))(lg, Kr, P, Krc, Pc)


def _ret_chunk_fwd(q, k, v, sf, sb, hm, lam, mu, Mk):
    dv = _decay_vecs(lam, mu)
    qm = jnp.where(hm, q, jnp.zeros_like(q))
    qmf = qm.astype(F32)
    A = _nt(qm, k)
    Am = A * Mk
    Amb = Am.astype(BF16)
    Qf = (qmf * dv["qf"]).astype(BF16)
    Qb = (qmf * dv["qb"]).astype(BF16)
    O = _nn(Amb, v) + _nn(Qf, sf) + _nn(Qb, sb)
    return dict(dv=dv, Mk=Mk, qm=qm, Am=Am, Amb=Amb, Qf=Qf, Qb=Qb, O=O)


def _ret_out_fwd(Qr, Kr, P, Krc, Pc, SF, lg, rv_blk, rg_blk, npairs, d_mix):
    L = Qr.shape[0]
    Lc = Krc.shape[0]
    N, ncc = L // CHUNK, Lc // CHUNK

    rv_grp, rg_grp = _group_index(rv_blk, npairs), _group_index(rg_blk, npairs)
    heads = [(p, h) for p in range(npairs) for h in range(2)]
    kcols = lambda p: slice(p * LANES, (p + 1) * LANES)
    vcols = lambda p, h: slice((2 * p + h) * LANES, (2 * p + h + 1) * LANES)

    def body(lg_ref, q_ref, k_ref, v_ref, g_ref, sf_ref, kc_ref, vc_ref, y_ref, sb_ref, S, Mks):
        n = pl.program_id(0)
        masks = _head_masks()

        @pl.when(n == 0)
        def _():
            for p, h in heads:
                mu = _lam_of(lg_ref, 1, 2 * p + h)
                Mks[p, h] = _decay_mask(_lam_of(lg_ref, 0, 2 * p + h), mu)[0]
                dvb = _decay_vecs(mu, mu)
                s = jnp.zeros((LANES, LANES), F32)
                for cc in reversed(range(ncc)):
                    rows = slice(cc * CHUNK, (cc + 1) * CHUNK)
                    kw = jnp.where(masks[h], kc_ref[rows, kcols(p)].astype(F32) * dvb["kb"], 0.0).astype(BF16)
                    s = dvb["gb"] * s + _tn(kw, vc_ref[rows, vcols(p, h)])
                S[p, h] = s

        for p, h in heads:
            lam = _lam_of(lg_ref, 0, 2 * p + h)
            mu = _lam_of(lg_ref, 1, 2 * p + h)
            hm = masks[h]
            dvb = _decay_vecs(lam, mu)
            s = S[p, h]
            sbb = s.astype(BF16)
            sb_ref[p, 0, h] = sbb
            k = k_ref[:, kcols(p)]
            v = v_ref[:, vcols(p, h)]
            f = _ret_chunk_fwd(q_ref[:, kcols(p)], k, v, sf_ref[p, 0, h], sbb, hm, lam, mu, Mks[p, h])
            O = f["O"]
            r = lax.rsqrt(jnp.mean(O * O, axis=-1, keepdims=True) + NORM_EPS)
            g = g_ref[:, vcols(p, h)].astype(F32)
            y_ref[:, vcols(p, h)] = (O * r * (g * _sigmoid(g))).astype(BF16)
            kw = jnp.where(hm, k.astype(F32) * dvb["kb"], 0.0).astype(BF16)
            S[p, h] = dvb["gb"] * s + _tn(kw, v)

    rev = lambda n: N - 1 - n
    wq, wv = npairs * LANES, npairs * 2 * LANES
    st = pl.BlockSpec((npairs, 1, 2, LANES, LANES), lambda n: (0, rev(n), 0, 0, 0))
    return pl.pallas_call(
        body, name="ret_out_fwd", grid=(N,),
        in_specs=[pl.BlockSpec(memory_space=pltpu.SMEM),
                  pl.BlockSpec((CHUNK, wq), lambda n: (rev(n), 0)),
                  pl.BlockSpec((CHUNK, wq), lambda n: (rev(n), 0)),
                  pl.BlockSpec((CHUNK, wv), lambda n: (rev(n), rv_grp)),
                  pl.BlockSpec((CHUNK, wv), lambda n: (rev(n), rg_grp)),
                  st,
                  pl.BlockSpec((Lc, wq), lambda n: (0, 0)),
                  pl.BlockSpec((Lc, wv), lambda n: (0, rv_grp))],
        out_specs=[pl.BlockSpec((CHUNK, wv), lambda n: (rev(n), 0)), st],
        out_shape=[jax.ShapeDtypeStruct((L, d_mix), BF16),
                   jax.ShapeDtypeStruct((npairs, N, 2, LANES, LANES), BF16)],
        scratch_shapes=[pltpu.VMEM((npairs, 2, LANES, LANES), F32), pltpu.VMEM((npairs, 2, CHUNK, CHUNK), F32)],
        compiler_params=_params(1))(lg, Qr, Kr, P, P, SF, Krc, Pc)


ACC_ROWS = 8


def _ret_bwd1(Qr, Kr, P, Krc, Pc, SF, SB, dY, lg, rv_blk, rg_blk, npairs, d_proj):
    L = Qr.shape[0]
    Lc = Krc.shape[0]
    N, ncc = L // CHUNK, Lc // CHUNK
    rv_grp, rg_grp = _group_index(rv_blk, npairs), _group_index(rg_blk, npairs)
    heads = [(p, h) for p in range(npairs) for h in range(2)]
    kcols = lambda p: slice(p * LANES, (p + 1) * LANES)
    vcols = lambda p, h: slice((2 * p + h) * LANES, (2 * p + h + 1) * LANES)

    def body(lg_ref, q_ref, k_ref, v_ref, g_ref, sf_ref, sb_ref, dy_ref, kc_ref, vc_ref,
             dq_ref, dk_ref, dv_ref, dg_ref, do_ref, dkc_ref, dvc_ref, acc_ref, dS, T, Mks):
        n = pl.program_id(0)
        masks = _head_masks()

        @pl.when(n == 0)
        def _():
            dS[...] = jnp.zeros_like(dS)
            T[...] = jnp.zeros_like(T)
            acc_ref[...] = jnp.zeros_like(acc_ref)
            for p, h in heads:
                Mks[p, h] = _decay_mask(_lam_of(lg_ref, 0, 2 * p + h), _lam_of(lg_ref, 1, 2 * p + h))[0]

        def head_main(p, h):
            lam = _lam_of(lg_ref, 0, 2 * p + h)
            mu = _lam_of(lg_ref, 1, 2 * p + h)
            hm = masks[h]
            hs = vcols(p, h)
            v = v_ref[:, hs]
            k = k_ref[:, kcols(p)]
            sf = sf_ref[p, 0, h]
            sb = sb_ref[p, 0, h]
            f = _ret_chunk_fwd(q_ref[:, kcols(p)], k, v, sf, sb, hm, lam, mu, Mks[p, h])
            dv_, O = f["dv"], f["O"]
            r = lax.rsqrt(jnp.mean(O * O, axis=-1, keepdims=True) + NORM_EPS)
            on = O * r
            g = g_ref[:, hs].astype(F32)
            sg = _sigmoid(g)
            dy = dy_ref[:, hs].astype(F32)
            dg_ref[:, hs] = (dy * on * (sg * (1.0 + g * (1.0 - sg)))).astype(BF16)
            don = dy * (g * sg)
            dO = r * (don - on * jnp.mean(don * on, axis=-1, keepdims=True))
            dOb = dO.astype(BF16)
            do_ref[:, hs] = dOb
            dAm = _nt(dOb, v)
            T[p, h] += dAm * f["Am"]
            dAb = (dAm * f["Mk"]).astype(BF16)
            km = jnp.where(hm, k, jnp.zeros_like(k))
            dq = _nn(dAb, km)
            dk = _tn(dAb, f["qm"])
            dvh = _tn(f["Amb"], dOb)
            dQf = _nt(dOb, sf)
            dQb = _nt(dOb, sb)
            dq = dq + dQf * dv_["qf"] + dQb * dv_["qb"]
            acc_ref[p, h, 0:1, :] += _fsum(dQf * f["Qf"].astype(F32) * (dv_["i"] + 1.0))
            acc_ref[p, h, 1:2, :] += _fsum(dQb * f["Qb"].astype(F32) * (CHUNK - dv_["i"]))
            dSh = dS[p, h]
            dSb_ = dSh.astype(BF16)
            Kf = (km.astype(F32) * dv_["kf"]).astype(BF16)
            dKf = _nt(v, dSb_)
            dk = dk + jnp.where(hm, dKf * dv_["kf"], 0.0)
            acc_ref[p, h, 2:3, :] += _fsum(jnp.where(hm, dKf, 0.0) * Kf.astype(F32) * (CHUNK - 1.0 - dv_["i"]))
            dvh = dvh + _nn(Kf, dSb_)
            acc_ref[p, h, 3:4, :] += float(CHUNK) * dv_["gf"] * _fsum(dSh * sf.astype(F32))
            dSh = dv_["gf"] * dSh + _tn(f["Qf"], dOb)
            dS[p, h] = dSh
            dv_ref[:, hs] = dvh
            return dq, dk

        for p in range(npairs):
            dq0, dk0 = head_main(p, 0)
            dq1, dk1 = head_main(p, 1)
            dq_ref[:, kcols(p)] = dq0 + dq1
            dk_ref[:, kcols(p)] = dk0 + dk1

        @pl.when(n == N - 1)
        def _():
            for p, h in heads:
                lam = _lam_of(lg_ref, 0, 2 * p + h)
                dv_ = _decay_vecs(lam, lam)
                hm = masks[h]
                hs = vcols(p, h)
                states = [jnp.zeros((LANES, LANES), F32)]
                kws = []
                for cc in range(ncc):
                    rows = slice(cc * CHUNK, (cc + 1) * CHUNK)
                    kw = jnp.where(hm, kc_ref[rows, kcols(p)].astype(F32) * dv_["kf"], 0.0).astype(BF16)
                    kws.append(kw)
                    states.append(dv_["gf"] * states[-1] + _tn(kw, vc_ref[rows, hs]))
                d = dS[p, h]
                for cc in reversed(range(ncc)):
                    db = d.astype(BF16)
                    rows = slice(cc * CHUNK, (cc + 1) * CHUNK)
                    dKf_c = jnp.where(hm, _nt(vc_ref[rows, hs], db), 0.0)
                    part = dKf_c * dv_["kf"]
                    if h == 0:
                        dkc_ref[rows, kcols(p)] = part
                    else:
                        dkc_ref[rows, kcols(p)] += part
                    acc_ref[p, h, 2:3, :] += _fsum(dKf_c * kws[cc].astype(F32) * (CHUNK - 1.0 - dv_["i"]))
                    dvc_ref[rows, hs] = _nn(kws[cc], db)
                    acc_ref[p, h, 3:4, :] += float(CHUNK) * dv_["gf"] * _fsum(d * states[cc])
                    d = dv_["gf"] * d
                _, rel, low = _decay_mask(lam, lam)
                Th = T[p, h]
                acc_ref[p, h, 4:5, :] += _fsum(jnp.where(low, Th * rel, 0.0))
                acc_ref[p, h, 5:6, :] += _fsum(jnp.where(low, 0.0, -Th * rel))

    rev = lambda n: N - 1 - n
    wq, wv = npairs * LANES, npairs * 2 * LANES
    st = pl.BlockSpec((npairs, 1, 2, LANES, LANES), lambda n: (0, rev(n), 0, 0, 0))
    pair = pl.BlockSpec((CHUNK, wq), lambda n: (rev(n), 0))
    wide = lambda grp: pl.BlockSpec((CHUNK, wv), lambda n: (rev(n), grp))
    return pl.pallas_call(
        body, name="ret_bwd_desc", grid=(N,),
        in_specs=[pl.BlockSpec(memory_space=pltpu.SMEM), pair, pair, wide(rv_grp), wide(rg_grp), st, st, wide(0),
                  pl.BlockSpec((Lc, wq), lambda n: (0, 0)),
                  pl.BlockSpec((Lc, wv), lambda n: (0, rv_grp))],
        out_specs=[pair, pair, wide(0), wide(rg_grp), wide(0),
                   pl.BlockSpec((Lc, wq), lambda n: (0, 0)),
                   pl.BlockSpec((Lc, wv), lambda n: (0, 0)),
                   pl.BlockSpec((npairs, 2, ACC_ROWS, LANES), lambda n: (0, 0, 0, 0))],
        out_shape=[jax.ShapeDtypeStruct((L, npairs * LANES), F32),
                   jax.ShapeDtypeStruct((L, npairs * LANES), F32),
                   jax.ShapeDtypeStruct((L, npairs * 2 * LANES), F32),
                   jax.ShapeDtypeStruct((L, d_proj), BF16),
                   jax.ShapeDtypeStruct((L, npairs * 2 * LANES), BF16),
                   jax.ShapeDtypeStruct((Lc, npairs * LANES), F32),
                   jax.ShapeDtypeStruct((Lc, npairs * 2 * LANES), F32),
                   jax.ShapeDtypeStruct((npairs, 2, ACC_ROWS, LANES), F32)],
        scratch_shapes=[pltpu.VMEM((npairs, 2, LANES, LANES), F32), pltpu.VMEM((npairs, 2, CHUNK, CHUNK), F32),
                        pltpu.VMEM((npairs, 2, CHUNK, CHUNK), F32)],
        compiler_params=_params(1))(lg, Qr, Kr, P, P, SF, SB, dY, Krc, Pc)


def _ret_bwd2(Qr, Kr, P, Krc, Pc, SB, dO, dKr, dVp, dP, dKc, dVc, lg, rv_blk, npairs):
    L = Qr.shape[0]
    Lc = Krc.shape[0]
    N, ncc = L // CHUNK, Lc // CHUNK
    rv_grp = _group_index(rv_blk, npairs)
    heads = [(p, h) for p in range(npairs) for h in range(2)]
    kcols = lambda p: slice(p * LANES, (p + 1) * LANES)
    vcols = lambda p, h: slice((2 * p + h) * LANES, (2 * p + h + 1) * LANES)

    def body(lg_ref, q_ref, k_ref, v_ref, sb_ref, do_ref, dkin_ref, dvin_ref, kc_ref, vc_ref, dkcin_ref, dvcin_ref,
             dpin_ref, dk_ref, dv_ref, dkc_ref, dvc_ref, acc_ref, dS):
        del dpin_ref
        n = pl.program_id(0)
        masks = _head_masks()

        @pl.when(n == 0)
        def _():
            dS[...] = jnp.zeros_like(dS)
            acc_ref[...] = jnp.zeros_like(acc_ref)

        def head_main(p, h):
            mu = _lam_of(lg_ref, 1, 2 * p + h)
            hm = masks[h]
            hs = vcols(p, h)
            dv_ = _decay_vecs(mu, mu)
            v = v_ref[:, hs]
            k = k_ref[:, kcols(p)]
            q = q_ref[:, kcols(p)]
            dOb = do_ref[:, hs]
            km = jnp.where(hm, k, jnp.zeros_like(k)).astype(F32)
            Kb = (km * dv_["kb"]).astype(BF16)
            Qb = (jnp.where(hm, q, jnp.zeros_like(q)).astype(F32) * dv_["qb"]).astype(BF16)
            dSh = dS[p, h]
            dSb_ = dSh.astype(BF16)
            dKb = jnp.where(hm, _nt(v, dSb_), 0.0)
            acc_ref[p, h, 0:1, :] += _fsum(dKb * Kb.astype(F32) * dv_["i"])
            dv_ref[:, hs] = (dvin_ref[:, hs] + _nn(Kb, dSb_)).astype(BF16)
            acc_ref[p, h, 1:2, :] += float(CHUNK) * dv_["gb"] * _fsum(dSh * sb_ref[p, 0, h].astype(F32))
            dS[p, h] = dv_["gb"] * dSh + _tn(Qb, dOb)
            return dKb * dv_["kb"]

        for p in range(npairs):
            dk_ref[:, kcols(p)] = dkin_ref[:, kcols(p)] + head_main(p, 0) + head_main(p, 1)

        @pl.when(n == N - 1)
        def _():
            for p, h in heads:
                mu = _lam_of(lg_ref, 1, 2 * p + h)
                hm = masks[h]
                hs = vcols(p, h)
                dv_ = _decay_vecs(mu, mu)
                states = {}
                kws = {}
                s = jnp.zeros((LANES, LANES), F32)
                for cc in reversed(range(ncc)):
                    rows = slice(cc * CHUNK, (cc + 1) * CHUNK)
                    states[cc] = s
                    kw = jnp.where(hm, kc_ref[rows, kcols(p)].astype(F32) * dv_["kb"], 0.0).astype(BF16)
                    kws[cc] = kw
                    s = dv_["gb"] * s + _tn(kw, vc_ref[rows, hs])
                d = dS[p, h]
                for cc in range(ncc):
                    db = d.astype(BF16)
                    rows = slice(cc * CHUNK, (cc + 1) * CHUNK)
                    dKb_c = jnp.where(hm, _nt(vc_ref[rows, hs], db), 0.0)
                    part = dKb_c * dv_["kb"]
                    if h == 0:
                        dkc_ref[rows, kcols(p)] = dkcin_ref[rows, kcols(p)] + part
                    else:
                        dkc_ref[rows, kcols(p)] += part
                    acc_ref[p, h, 0:1, :] += _fsum(dKb_c * kws[cc].astype(F32) * dv_["i"])
                    dvc_ref[rows, hs] = dvcin_ref[rows, hs] + _nn(kws[cc], db)
                    acc_ref[p, h, 1:2, :] += float(CHUNK) * dv_["gb"] * _fsum(d * states[cc])
                    d = dv_["gb"] * d

    wq, wv = npairs * LANES, npairs * 2 * LANES
    st = pl.BlockSpec((npairs, 1, 2, LANES, LANES), lambda n: (0, n, 0, 0, 0))
    pair = pl.BlockSpec((CHUNK, wq), lambda n: (n, 0))
    wide = lambda grp: pl.BlockSpec((CHUNK, wv), lambda n: (n, grp))
    ckc = pl.BlockSpec((Lc, wq), lambda n: (0, 0))
    cvc = lambda grp: pl.BlockSpec((Lc, wv), lambda n: (0, grp))
    return pl.pallas_call(
        body, name="ret_bwd_asc", grid=(N,),
        in_specs=[pl.BlockSpec(memory_space=pltpu.SMEM), pair, pair, wide(rv_grp), st, wide(0), pair, wide(0),
                  ckc, cvc(rv_grp), ckc, cvc(0), pl.BlockSpec(memory_space=pl.ANY)],
        out_specs=[pair, wide(rv_grp), ckc, cvc(0),
                   pl.BlockSpec((npairs, 2, ACC_ROWS, LANES), lambda n: (0, 0, 0, 0))],
        out_shape=[jax.ShapeDtypeStruct(dKr.shape, F32),
                   jax.ShapeDtypeStruct(dP.shape, dP.dtype),
                   jax.ShapeDtypeStruct(dKc.shape, F32),
                   jax.ShapeDtypeStruct(dVc.shape, F32),
                   jax.ShapeDtypeStruct((npairs, 2, ACC_ROWS, LANES), F32)],
        input_output_aliases={12: 1},
        scratch_shapes=[pltpu.VMEM((npairs, 2, LANES, LANES), F32)],
        compiler_params=_params(1))(lg, Qr, Kr, P, SB, dO, dKr, dVp, Krc, Pc, dKc, dVc, dP)


GROUP = 4


def _att_band(Lc):
    row = np.arange(GROUP * CHUNK)[:, None] % CHUNK
    col = np.arange(3 * CHUNK + Lc)[None, :]
    ok = ((col >= row) & (col <= row + 2 * CHUNK)) | (col >= 3 * CHUNK)
    return jnp.asarray(np.where(ok, 0.0, NEG), F32)


def _att_edge(n, N, Lc):
    col = lax.broadcasted_iota(jnp.int32, (1, 3 * CHUNK + Lc), 1)
    off = jnp.logical_or(jnp.logical_and(col < CHUNK, n == 0),
                         jnp.logical_and(jnp.logical_and(col >= 2 * CHUNK, col < 3 * CHUNK), n == N - 1))
    return jnp.where(off, NEG, 0.0)


def _stack_heads(ref, gi):
    masks = _head_masks()
    tiles = []
    for pr in range(2):
        t = ref[:, (2 * gi + pr) * LANES:(2 * gi + pr + 1) * LANES]
        for a in range(2):
            tiles.append(jnp.where(masks[a], t, jnp.zeros_like(t)))
    return jnp.concatenate(tiles, axis=0)


def _unstack_heads(x4):
    m0 = _head_masks()[0]
    return [jnp.where(m0, x4[(2 * pr) * CHUNK:(2 * pr + 1) * CHUNK], x4[(2 * pr + 1) * CHUNK:(2 * pr + 2) * CHUNK])
            for pr in range(2)]


def _sink_column(sink_ref, g):
    row = lax.broadcasted_iota(jnp.int32, (GROUP * CHUNK, 1), 0) // CHUNK
    col = jnp.zeros((GROUP * CHUNK, 1), F32)
    for h in range(GROUP):
        col = jnp.where(row == h, sink_ref[0, g * GROUP + h], col)
    return col


def _att_probs(q4, Kall, bias, snk):
    s = _nt(q4, Kall) + bias
    mx = jnp.maximum(jnp.max(s, axis=1, keepdims=True), snk)
    p = jnp.exp(s - mx)
    p_snk = jnp.exp(snk - mx)
    inv = 1.0 / (jnp.sum(p, axis=1, keepdims=True) + p_snk)
    return p, p_snk, inv


def _att_groups_per_step(nkv, blk0):
    return 2 if nkv % 2 == 0 and blk0 % 2 == 0 else 1


def _att_specs(Lc, N, gps):
    q = pl.BlockSpec((CHUNK, gps * 2 * LANES), lambda g, n: (n, g))
    kv = lambda s: pl.BlockSpec((gps, CHUNK, LANES), lambda g, n: (g, jnp.clip(n + s, 0, N - 1), 0))
    ctx = pl.BlockSpec((gps, Lc, LANES), lambda g, n: (g, 0, 0))
    return q, kv, ctx


def _att_fwd(Qa, Kd, Vd, Kdc, Vdc, sink, Y, blk0):
    L = Qa.shape[0]
    Lc = Kdc.shape[1]
    N = L // CHUNK
    nkv = Kd.shape[0]
    gps = _att_groups_per_step(nkv, blk0)

    def body(sink_ref, band_ref, q_ref, kp, kc_, kn, vp, vc_, vn, kctx, vctx, y_in, o_ref):
        del y_in
        g, n = pl.program_id(0), pl.program_id(1)
        bias = band_ref[...] + _att_edge(n, N, Lc)
        for gi in range(gps):
            Kall = jnp.concatenate([kp[gi], kc_[gi], kn[gi], kctx[gi]], axis=0)
            Vall = jnp.concatenate([vp[gi], vc_[gi], vn[gi], vctx[gi]], axis=0)
            p, _, inv = _att_probs(_stack_heads(q_ref, gi), Kall, bias, _sink_column(sink_ref, g * gps + gi))
            o4 = _nn(p.astype(BF16), Vall) * inv
            for pr, o in enumerate(_unstack_heads(o4)):
                o_ref[:, (2 * gi + pr) * LANES:(2 * gi + pr + 1) * LANES] = o.astype(BF16)

    q, kv, ctx = _att_specs(Lc, N, gps)
    band = pl.BlockSpec((GROUP * CHUNK, 3 * CHUNK + Lc), lambda g, n: (0, 0))
    return pl.pallas_call(
        body, name="att_fwd", grid=(nkv // gps, N),
        in_specs=[pl.BlockSpec(memory_space=pltpu.SMEM), band, q, kv(-1), kv(0), kv(1), kv(-1), kv(0), kv(1), ctx, ctx,
                  pl.BlockSpec(memory_space=pl.ANY)],
        out_specs=pl.BlockSpec((CHUNK, gps * 2 * LANES), lambda g, n: (n, blk0 // gps + g)),
        out_shape=jax.ShapeDtypeStruct(Y.shape, Y.dtype),
        input_output_aliases={11: 0},
        compiler_params=_params(2))(sink, _att_band(Lc), Qa, Kd, Kd, Kd, Vd, Vd, Vd, Kdc, Vdc, Y)


def _att_bwd(Qa, Kd, Vd, Kdc, Vdc, sink, dY, blk0):
    L = Qa.shape[0]
    Lc = Kdc.shape[1]
    N = L // CHUNK
    nkv = Kd.shape[0]
    gps = _att_groups_per_step(nkv, blk0)

    def body(sink_ref, band_ref, q_ref, kp, kc_, kn, vp, vc_, vn, kctx, vctx, dy_ref,
             dq_ref, dkp, dkc_, dkn, dvp, dvc_, dvn, dkctx, dvctx, dsink_ref):
        g, n = pl.program_id(0), pl.program_id(1)

        @pl.when(n == 0)
        def _():
            dkctx[...] = jnp.zeros_like(dkctx)
            dvctx[...] = jnp.zeros_like(dvctx)
            dsink_ref[...] = jnp.zeros_like(dsink_ref)

        bias = band_ref[...] + _att_edge(n, N, Lc)
        for gi in range(gps):
            Kall = jnp.concatenate([kp[gi], kc_[gi], kn[gi], kctx[gi]], axis=0)
            Vall = jnp.concatenate([vp[gi], vc_[gi], vn[gi], vctx[gi]], axis=0)
            q4 = _stack_heads(q_ref, gi)
            do4 = _stack_heads(dy_ref, gi)
            p, p_snk, inv = _att_probs(q4, Kall, bias, _sink_column(sink_ref, g * gps + gi))
            P = p * inv
            dp = _nt(do4, Vall)
            delta = jnp.sum(P * dp, axis=1, keepdims=True)
            ds = (P * (dp - delta)).astype(BF16)
            dsnk = -(p_snk * inv) * delta
            for h in range(GROUP):
                dsink_ref[gi, h:h + 1, :] += _fsum(dsnk[h * CHUNK:(h + 1) * CHUNK])
            for pr, dq in enumerate(_unstack_heads(_nn(ds, Kall))):
                dq_ref[:, (2 * gi + pr) * LANES:(2 * gi + pr + 1) * LANES] = dq
            dK = _tn(ds, q4)
            dV = _tn(P.astype(BF16), do4)
            for j, (rk, rv) in enumerate([(dkp, dvp), (dkc_, dvc_), (dkn, dvn)]):
                rk[gi] = dK[j * CHUNK:(j + 1) * CHUNK].astype(BF16)
                rv[gi] = dV[j * CHUNK:(j + 1) * CHUNK].astype(BF16)
            dkctx[gi] += dK[3 * CHUNK:]
            dvctx[gi] += dV[3 * CHUNK:]

    q, kv, ctx = _att_specs(Lc, N, gps)
    band = pl.BlockSpec((GROUP * CHUNK, 3 * CHUNK + Lc), lambda g, n: (0, 0))
    blk = pl.BlockSpec((gps, CHUNK, LANES), lambda g, n: (g, n, 0))
    part = jax.ShapeDtypeStruct((nkv, L, LANES), BF16)
    cshape = jax.ShapeDtypeStruct((nkv, Lc, LANES), F32)
    return pl.pallas_call(
        body, name="att_bwd", grid=(nkv // gps, N),
        in_specs=[pl.BlockSpec(memory_space=pltpu.SMEM), band, q, kv(-1), kv(0), kv(1), kv(-1), kv(0), kv(1), ctx, ctx,
                  pl.BlockSpec((CHUNK, gps * 2 * LANES), lambda g, n: (n, blk0 // gps + g))],
        out_specs=[q, blk, blk, blk, blk, blk, blk, ctx, ctx,
                   pl.BlockSpec((gps, 8, LANES), lambda g, n: (g, 0, 0))],
        out_shape=[jax.ShapeDtypeStruct(Qa.shape, F32), part, part, part, part, part, part, cshape, cshape,
                   jax.ShapeDtypeStruct((nkv, 8, LANES), F32)],
        compiler_params=_params(2))(sink, _att_band(Lc), Qa, Kd, Kd, Kd, Vd, Vd, Vd, Kdc, Vdc, dY)


def _scale_rows(dx, gt, saved, name):
    M, D = dx.shape
    tm = _tile(M, 512, 8)

    def body(dx_ref, gt_ref, sv_ref, dz_ref, dgt_ref):
        @pl.when(pl.program_id(0) == 0)
        def _():
            dgt_ref[...] = jnp.zeros_like(dgt_ref)
        d = dx_ref[...]
        dz_ref[...] = (d * gt_ref[...]).astype(BF16)
        dgt_ref[...] += jnp.sum(d * sv_ref[...].astype(F32), axis=0, keepdims=True)

    row = pl.BlockSpec((tm, D), lambda i: (i, 0))
    vec = pl.BlockSpec((1, D), lambda i: (0, 0))
    return pl.pallas_call(
        body, name=name, grid=(M // tm,), in_specs=[row, vec, row], out_specs=[row, vec],
        out_shape=[jax.ShapeDtypeStruct((M, D), BF16), jax.ShapeDtypeStruct((1, D), F32)],
        compiler_params=_params(1))(dx, gt, saved)


def _bwd_proj(dz, w, G=None, U=None, name="bwd_proj"):
    M, D = dz.shape
    N = w.shape[0]
    swiglu = G is not None
    tm, tn = _tile(M, ROWS_PER_LATCH, 8), _tile(N, 512)

    def body(*refs):
        if swiglu:
            dz_ref, w_ref, G_ref, U_ref, dG_ref, dU_ref = refs
        else:
            dz_ref, w_ref, dA_ref = refs
        dA = _nt(dz_ref[...], w_ref[...])
        if swiglu:
            Gv = G_ref[...].astype(F32)
            Uv = U_ref[...].astype(F32)
            sg = _sigmoid(Gv)
            dU_ref[...] = (dA * Gv * sg).astype(BF16)
            dG_ref[...] = (dA * Uv * (sg * (1.0 + Gv * (1.0 - sg)))).astype(BF16)
        else:
            dA_ref[...] = dA.astype(BF16)

    row = pl.BlockSpec((tm, D), lambda i, j: (i, 0))
    tile = pl.BlockSpec((tm, tn), lambda i, j: (i, j))
    big = jax.ShapeDtypeStruct((M, N), BF16)
    in_specs = [row, pl.BlockSpec((tn, D), lambda i, j: (j, 0))]
    args = [dz, w]
    if swiglu:
        in_specs += [tile, tile]
        args += [G, U]
        out_specs, out_shape = [tile, tile], [big, big]
    else:
        out_specs, out_shape = tile, big
    return pl.pallas_call(
        body, name=name, grid=(M // tm, N // tn), in_specs=in_specs, out_specs=out_specs, out_shape=out_shape,
        compiler_params=_params(2))(*args)


def _tn_matmul(pairs, name):
    Ka, Nb = pairs[0][0].shape[1], pairs[0][1].shape[1]
    tk, tn = _tile(Ka, 2048), _tile(Nb, 2048)
    tls, nks = [], []
    for a, _ in pairs:
        tl = _tile(a.shape[0], 512, 8)
        tls.append(tl)
        nks.append(a.shape[0] // tl)
    starts = [int(s) for s in np.cumsum([0] + nks[:-1])]
    nk = int(sum(nks))

    def body(*refs):
        out_ref, acc = refs[-2], refs[-1]
        k = pl.program_id(2)

        @pl.when(k == 0)
        def _():
            acc[...] = jnp.zeros_like(acc)

        for idx in range(len(pairs)):
            a_ref, b_ref = refs[2 * idx], refs[2 * idx + 1]

            @pl.when(jnp.logical_and(k >= starts[idx], k < starts[idx] + nks[idx]))
            def _():
                acc[...] += _tn(a_ref[...], b_ref[...])

        @pl.when(k == nk - 1)
        def _():
            out_ref[...] = acc[...].astype(BF16)

    in_specs, args = [], []
    for idx, (a, b) in enumerate(pairs):
        s0, n_ = starts[idx], nks[idx]
        in_specs.append(pl.BlockSpec((tls[idx], tk), lambda i, j, k, s0=s0, n_=n_: (jnp.clip(k - s0, 0, n_ - 1), i)))
        in_specs.append(pl.BlockSpec((tls[idx], tn), lambda i, j, k, s0=s0, n_=n_: (jnp.clip(k - s0, 0, n_ - 1), j)))
        args += [a, b]
    return pl.pallas_call(
        body, name=name, grid=(Ka // tk, Nb // tn, nk), in_specs=in_specs,
        out_specs=pl.BlockSpec((tk, tn), lambda i, j, k: (i, j)),
        out_shape=jax.ShapeDtypeStruct((Ka, Nb), BF16),
        scratch_shapes=[pltpu.VMEM((tk, tn), F32)], compiler_params=_params(3))(*args)


def _bwd_norm_mod(pairs, x, dres, g, sh, sc, name):
    M, D = x.shape
    K = pairs[0][0].shape[1]
    tm, tk = _tile(M, 512, 8), _tile(K, 1152 if len(pairs) == 1 else 512)
    nk = K // tk
    npair = len(pairs)
    has_res = dres is not None

    def body(*refs):
        pr = refs[:2 * npair]
        rest = refs[2 * npair:]
        if has_res:
            x_ref, dres_ref, g_ref, sh_ref, sc_ref, dx_ref, st_ref, acc = rest
        else:
            x_ref, g_ref, sh_ref, sc_ref, dx_ref, st_ref, acc = rest
        del sh_ref
        i, k = pl.program_id(0), pl.program_id(1)

        @pl.when(jnp.logical_and(i == 0, k == 0))
        def _():
            st_ref[...] = jnp.zeros_like(st_ref)

        @pl.when(k == 0)
        def _():
            acc[...] = jnp.zeros_like(acc)

        t = _nt(pr[1][...], pr[0][...])
        for idx in range(1, npair):
            t = t + _nt(pr[2 * idx + 1][...], pr[2 * idx][...])
        acc[...] += t

        @pl.when(k == nk - 1)
        def _():
            xv = x_ref[...]
            gv = g_ref[...]
            dh = acc[...].T
            r = lax.rsqrt(jnp.mean(xv * xv, axis=-1, keepdims=True) + NORM_EPS)
            xh = xv * r
            st_ref[0:1, :] += jnp.sum(dh, axis=0, keepdims=True)
            st_ref[1:2, :] += jnp.sum(dh * (xh * gv), axis=0, keepdims=True)
            dn = dh * (1.0 + sc_ref[...])
            st_ref[2:3, :] += jnp.sum(dn * xh, axis=0, keepdims=True)
            dxh = dn * gv
            d = r * (dxh - xh * jnp.mean(dxh * xh, axis=-1, keepdims=True))
            if has_res:
                d = d + dres_ref[...]
            dx_ref[...] = d

    row = pl.BlockSpec((tm, D), lambda i, k: (i, 0))
    vec = pl.BlockSpec((1, D), lambda i, k: (0, 0))
    in_specs, args = [], []
    for dA, w in pairs:
        in_specs += [pl.BlockSpec((tm, tk), lambda i, k: (i, k)), pl.BlockSpec((D, tk), lambda i, k: (0, k))]
        args += [dA, w]
    in_specs += [row] + ([row] if has_res else []) + [vec, vec, vec]
    args += [x] + ([dres] if has_res else []) + [g, sh, sc]
    return pl.pallas_call(
        body, name=name, grid=(M // tm, nk), in_specs=in_specs,
        out_specs=[row, pl.BlockSpec((8, D), lambda i, k: (0, 0))],
        out_shape=[jax.ShapeDtypeStruct((M, D), F32), jax.ShapeDtypeStruct((8, D), F32)],
        scratch_shapes=[pltpu.VMEM((D, tm), F32)], compiler_params=_params(2))(*args)


def _local_step(x, ctx, tgt, mod, modc, norm_mix, norm_ffn, norm_final, lg, sink, w_in, rest_weights, on_grads):
    L, D = x.shape
    Lc = ctx.shape[0]
    d_proj = w_in.shape[1]
    npairs = RET_HEADS // 2
    nkv = ATT_KV_HEADS
    nkvp = nkv // 2
    o_rq = 0
    o_rk = o_rq + RET_HEADS * RET_DK // LANES
    o_rv = o_rk + RET_HEADS * RET_DK // LANES
    o_rg = o_rv + RET_HEADS * RET_DV // LANES
    o_aq = o_rg + RET_HEADS * RET_DV // LANES
    o_ak = o_aq + ATT_HEADS * ATT_DH // LANES
    o_av = o_ak + nkv * ATT_DH // LANES
    assert (o_av + nkv * ATT_DH // LANES) * LANES == d_proj
    assert o_rv % 2 == 0 and o_rg % 2 == 0 and (RET_HEADS * RET_DV) % (2 * LANES) == 0
    rv_blk, rg_blk = o_rv // 2, o_rg // 2
    d_ret = RET_HEADS * RET_DV
    d_mix = d_ret + ATT_HEADS * ATT_DH
    att_blk = d_ret // (2 * LANES)
    k_scale = RET_DK ** -0.5
    a_scale = ATT_DH ** -0.5

    T = _rope_tables(L)
    Tc = dict(C=jnp.ones((Lc, LANES), F32), S=jnp.zeros((Lc, LANES), F32))
    row = lambda m, i: m[i:i + 1]
    sh_m, sc_m, gt_m, sh_f, sc_f, gt_f = [row(mod, i) for i in range(6)]
    sh_mc, sc_mc = row(modc, 0), row(modc, 1)

    P, hx = _norm_mod_matmul(x, norm_mix, sh_m, sc_m, w_in, "in_proj")
    Pc, hc = _norm_mod_matmul(ctx, norm_mix, sh_mc, sc_mc, w_in, "in_proj_ctx")
    nq = RET_HEADS * RET_DK // LANES
    Qr = _rope_cols(P, o_rq, nq, T["Cr"], T["Sr"], T["Rr"], 1.0, True, "rope_rq")
    Kr = _rope_cols(P, o_rk, nq, T["Cr"], T["Sr"], T["Rr"], k_scale, True, "rope_rk")
    Krc = _rope_cols(Pc, o_rk, nq, Tc["C"], Tc["S"], T["Rr"], k_scale, False, "scale_rk_ctx")
    Qa = _rope_cols(P, o_aq, ATT_HEADS * ATT_DH // LANES, T["Ca"], T["Sa"], T["Ra"], a_scale, True, "rope_aq")
    Kd = _dup_heads(P, o_ak, nkvp, T["Ca"], T["Sa"], T["Ra"], T["D0"], T["D1"], True, "dup_ak")
    Vd = _dup_heads(P, o_av, nkvp, T["Ca"], T["Sa"], T["Ra"], T["D0"], T["D1"], False, "dup_av")
    Kdc = _dup_heads(Pc, o_ak, nkvp, Tc["C"], Tc["S"], T["Ra"], T["D0"], T["D1"], False, "dup_ak_ctx")
    Vdc = _dup_heads(Pc, o_av, nkvp, Tc["C"], Tc["S"], T["Ra"], T["D0"], T["D1"], False, "dup_av_ctx")

    SF = _ret_states_fwd(Kr, P, Krc, Pc, lg, rv_blk, npairs)
    Y, SB = _ret_out_fwd(Qr, Kr, P, Krc, Pc, SF, lg, rv_blk, rg_blk, npairs, d_mix)
    Y = _att_fwd(Qa, Kd, Vd, Kdc, Vdc, sink, Y, att_blk)

    w_out, w_gate, w_up, w_down = rest_weights(Y)
    x1, O1 = _proj_residual(Y, w_out, x, gt_m, "out_proj")
    G, U, A, h2 = _ffn_in(x1, norm_ffn, sh_f, sc_f, w_gate, w_up)
    x2, Fo = _proj_residual(A, w_down, x1, gt_f, "ffn_out")
    dx2, loss, d_norm_final = _final(x2, norm_final, tgt)

    dz2, dgt_f = _scale_rows(dx2, gt_f, Fo, "ffn_gate_bwd")
    dG, dU = _bwd_proj(dz2, w_down, G, U, name="ffn_out_bwd")
    g_w_down = _tn_matmul([(A, dz2)], "grad_w_down")
    tok = on_grads(["w_down"], [g_w_down])
    dx1, st_f = _bwd_norm_mod([(dG, w_gate), (dU, w_up)], x1, dx2, norm_ffn + tok, sh_f, sc_f, "ffn_in_bwd")
    g_w_gate = _tn_matmul([(h2, dG)], "grad_w_gate")
    g_w_up = _tn_matmul([(h2, dU)], "grad_w_up")
    tok = on_grads(["w_gate", "w_up"], [g_w_gate, g_w_up])
    dz1, dgt_m = _scale_rows(dx1, gt_m + tok, O1, "mix_gate_bwd")
    dY = _bwd_proj(dz1, w_out, name="out_proj_bwd")
    g_w_out = _tn_matmul([(Y, dz1)], "grad_w_out")
    tok = on_grads(["w_out"], [g_w_out])

    dQa, dKp, dKs, dKn, dVp, dVs, dVn, dKdc, dVdc, dsink = _att_bwd(Qa, Kd, Vd, Kdc, Vdc, sink + tok, dY, att_blk)
    dQr, dKr, dVr, dP, dO, dKc, dVc, acc1 = _ret_bwd1(Qr, Kr, P, Krc, Pc, SF, SB, dY, lg, rv_blk, rg_blk, npairs, d_proj)
    dKr, dP, dKc, dVc, acc2 = _ret_bwd2(Qr, Kr, P, Krc, Pc, SB, dO, dKr, dVr, dP, dKc, dVc, lg, rv_blk, npairs)

    dP = _unrope_cols(dQr, dP, o_rq, nq, T["Cr"], T["Sr"], T["RrT"], 1.0, True, "unrope_rq")
    dP = _unrope_cols(dKr, dP, o_rk, nq, T["Cr"], T["Sr"], T["RrT"], k_scale, True, "unrope_rk")
    dP = _unrope_cols(dQa, dP, o_aq, ATT_HEADS * ATT_DH // LANES, T["Ca"], T["Sa"], T["RaT"], a_scale, True, "unrope_aq")
    dP = _fold_heads([(dKs, 0), (dKp, 1), (dKn, -1)], dP, o_ak, nkvp, T["Ca"], T["Sa"], T["RaT"], T["D0T"], T["D1T"],
                     True, "fold_ak")
    dP = _fold_heads([(dVs, 0), (dVp, 1), (dVn, -1)], dP, o_av, nkvp, T["Ca"], T["Sa"], T["RaT"], T["D0T"], T["D1T"],
                     False, "fold_av")
    dPc = jnp.zeros((Lc, d_proj), BF16)
    dPc = _unrope_cols(dKc, dPc, o_rk, nq, Tc["C"], Tc["S"], T["RrT"], k_scale, False, "ctx_rk_bwd")
    dPc = _unrope_cols(dVc, dPc, o_rv, RET_HEADS * RET_DV // LANES, Tc["C"], Tc["S"], T["RrT"], 1.0, False, "ctx_rv_bwd")
    dPc = _fold_heads([(dKdc.astype(BF16), 0)], dPc, o_ak, nkvp, Tc["C"], Tc["S"], T["RaT"], T["D0T"], T["D1T"],
                      False, "fold_ak_ctx")
    dPc = _fold_heads([(dVdc.astype(BF16), 0)], dPc, o_av, nkvp, Tc["C"], Tc["S"], T["RaT"], T["D0T"], T["D1T"],
                      False, "fold_av_ctx")

    dx, st_m = _bwd_norm_mod([(dP, w_in)], x, dx1, norm_mix, sh_m, sc_m, "in_proj_bwd")
    _, st_mc = _bwd_norm_mod([(dPc, w_in)], ctx, None, norm_mix, sh_mc, sc_mc, "in_proj_ctx_bwd")
    g_w_in = _tn_matmul([(hx, dP), (hc, dPc)], "grad_w_in")
    on_grads(["w_in"], [g_w_in])

    a1 = acc1[:, :, :, 0].reshape(RET_HEADS, ACC_ROWS)
    a2 = acc2[:, :, :, 0].reshape(RET_HEADS, ACC_ROWS)
    dlam = (a1[:, 0] + a1[:, 2] + a1[:, 3] + a1[:, 4]) * lg[0]
    dmu = (a1[:, 1] + a1[:, 5] + a2[:, 0] + a2[:, 1]) * lg[1]
    d_sink = dsink[:, :4, 0].reshape(1, ATT_HEADS)

    nh = RET_HEADS
    assert 2 * nh + ATT_HEADS <= LOSS_LANE
    small = _pack_rows(
        [(st_m, 0, 2, 0, 0), (dgt_m, 0, 1, 2, 0), (st_f, 0, 2, 3, 0), (dgt_f, 0, 1, 5, 0), (st_mc, 0, 2, 6, 0),
         (st_m[2:3] + st_mc[2:3], 0, 1, 12, 0), (st_f, 2, 1, 13, 0), (d_norm_final, 0, 1, 14, 0),
         (dlam.reshape(1, nh), 0, 1, 15, 0), (dmu.reshape(1, nh), 0, 1, 15, nh), (d_sink, 0, 1, 15, 2 * nh),
         (loss[:, 0:1], 0, 1, 15, LOSS_LANE)], 16, D, "pack_small")
    return dict(grad_x=dx, small=small)


def _my_pos():
    return lax.axis_index("x"), lax.axis_index("y"), lax.axis_index("c")


def _other_chips(x, y):
    return [(1 - x, y), (x, 1 - y), (1 - x, 1 - y)]


def _remote(src, dst, ssem, rsem, dev):
    return pltpu.make_async_remote_copy(src_ref=src, dst_ref=dst, send_sem=ssem, recv_sem=rsem,
                                        device_id=dev, device_id_type=MESH)


def _allgather8(v, name):
    R, Cc = v.shape

    def body(v_ref, out_ref, send_sems, recv_sems):
        x, y, c = _my_pos()
        me = 4 * x + 2 * y + c
        out_ref[pl.ds(me, 1)] = v_ref[...][None]
        peers = []
        for j in range(1, N_DEV):
            peers.append((1 - x if (j >> 2) & 1 else x, 1 - y if (j >> 1) & 1 else y, 1 - c if j & 1 else c))
        copies = []
        for j, peer in enumerate(peers):
            cp = _remote(v_ref, out_ref.at[me], send_sems.at[j], recv_sems.at[j], peer)
            cp.start()
            copies.append(cp)
        for j, peer in enumerate(peers):
            pid = 4 * peer[0] + 2 * peer[1] + peer[2]
            _remote(v_ref, out_ref.at[pid], send_sems.at[j], recv_sems.at[j], peer).wait_recv()
        for cp in copies:
            cp.wait_send()

    return pl.pallas_call(
        body, name=name, out_shape=jax.ShapeDtypeStruct((N_DEV, R, Cc), v.dtype),
        in_specs=[pl.BlockSpec(memory_space=pltpu.VMEM)], out_specs=pl.BlockSpec(memory_space=pltpu.VMEM),
        scratch_shapes=[pltpu.SemaphoreType.DMA((N_DEV - 1,)), pltpu.SemaphoreType.DMA((N_DEV - 1,))])(v)


def _region(ref, k, half, shard_shape, axis):
    r, cs = shard_shape
    hr = r // 2
    if axis == 1:
        return ref.at[pl.ds(pl.multiple_of(half * hr, 16), hr), pl.ds(pl.multiple_of(k * cs, LANES), cs)]
    return ref.at[pl.ds(pl.multiple_of(k * r + half * hr, 16), hr), :]


def _full_shape(shard_shape, axis):
    r, cs = shard_shape
    return (r, N_CHIPS * cs) if axis == 1 else (N_CHIPS * r, cs)


def _half_pieces(ref, half, shard_shape, axis):
    r, cs = shard_shape
    hr = r // 2
    if axis == 1:
        return [ref.at[pl.ds(pl.multiple_of(half * hr, 16), hr), :]]
    return [ref.at[pl.ds(pl.multiple_of(k * r + half * hr, 16), hr), :] for k in range(N_CHIPS)]


def _rs_sibling(grads, shapes, axes, name):
    nw = len(grads)
    npc = max(1 if a == 1 else N_CHIPS for a in axes)

    def body(*refs):
        g_refs, out_refs = refs[:nw], refs[nw:2 * nw]
        send, recv = refs[2 * nw:]
        x, y, c = _my_pos()
        sib = (x, y, 1 - c)
        copies = []
        for w in range(nw):
            src = _half_pieces(g_refs[w], 1 - c, shapes[w], axes[w])
            dst = _half_pieces(out_refs[w], 1 - c, shapes[w], axes[w])
            for i, (s, d) in enumerate(zip(src, dst)):
                cp = _remote(s, d, send.at[w, i], recv.at[w, i], sib)
                cp.start()
                copies.append(cp)
        for w in range(nw):
            mine = _half_pieces(out_refs[w], c, shapes[w], axes[w])
            for i, d in enumerate(mine):
                _remote(d, d, send.at[w, i], recv.at[w, i], sib).wait_recv()
        for cp in copies:
            cp.wait_send()

    anyspec = pl.BlockSpec(memory_space=pl.ANY)
    return pl.pallas_call(
        body, name=name,
        out_shape=[jax.ShapeDtypeStruct(_full_shape(s, a), BF16) for s, a in zip(shapes, axes)],
        in_specs=[anyspec] * nw, out_specs=[anyspec] * nw,
        scratch_shapes=[pltpu.SemaphoreType.DMA((nw, npc)), pltpu.SemaphoreType.DMA((nw, npc))])(*grads)


def _half_block_spec(shard_shape, axis, tr):
    r, cs = shard_shape
    hr = r // 2
    if axis == 1:
        return pl.BlockSpec((tr, cs), lambda k, i, c_ref: (c_ref[0] * (hr // tr) + i, k))
    return pl.BlockSpec((tr, cs), lambda k, i, c_ref: (k * (r // tr) + c_ref[0] * (hr // tr) + i, 0))


def _add_halves(g, recv, cvec, shard_shape, axis, name):
    r, cs = shard_shape
    hr = r // 2
    tr = _tile(hr, 256, 16)

    def body(c_ref, a_ref, b_ref, o_ref):
        del c_ref
        o_ref[0] = (a_ref[...].astype(F32) + b_ref[...].astype(F32)).astype(BF16)

    spec = _half_block_spec(shard_shape, axis, tr)
    return pl.pallas_call(
        body, name=name,
        grid_spec=pltpu.PrefetchScalarGridSpec(
            num_scalar_prefetch=1, grid=(N_CHIPS, hr // tr), in_specs=[spec, spec],
            out_specs=pl.BlockSpec((1, tr, cs), lambda k, i, c_ref: (k, i, 0))),
        out_shape=jax.ShapeDtypeStruct((N_CHIPS, hr, cs), BF16),
        compiler_params=_params(2, False))(cvec, g, recv)


def _sum_chips(sums, landed, kc, name):
    _, hr, cs = sums.shape
    tr = _tile(hr, 256, 16)

    def body(kc_ref, own_ref, a_ref, b_ref, c_ref, o_ref):
        del kc_ref
        o_ref[...] = (own_ref[0].astype(F32) + a_ref[0].astype(F32)) + (b_ref[0].astype(F32) + c_ref[0].astype(F32))

    slot = lambda j: pl.BlockSpec((1, tr, cs), lambda i, kc_ref: ((kc_ref[0] + j) % N_CHIPS, i, 0))
    return pl.pallas_call(
        body, name=name,
        grid_spec=pltpu.PrefetchScalarGridSpec(
            num_scalar_prefetch=1, grid=(hr // tr,), in_specs=[slot(0), slot(1), slot(2), slot(3)],
            out_specs=pl.BlockSpec((tr, cs), lambda i, kc_ref: (kc_ref[1] * (hr // tr) + i, 0))),
        out_shape=jax.ShapeDtypeStruct((2 * hr, cs), F32),
        compiler_params=_params(1, False))(kc, sums, landed, landed, landed)


def _exchange_halves(shards, name):
    nw = len(shards)

    def body(*refs):
        out_refs = refs[nw:2 * nw]
        send, recv = refs[2 * nw:]
        x, y, c = _my_pos()
        sib = (x, y, 1 - c)
        copies = []
        for w in range(nw):
            hr = shards[w].shape[0] // 2
            mine = out_refs[w].at[pl.ds(pl.multiple_of(c * hr, 8), hr), :]
            cp = _remote(mine, mine, send.at[w], recv.at[w], sib)
            cp.start()
            copies.append(cp)
        for w in range(nw):
            hr = shards[w].shape[0] // 2
            other = out_refs[w].at[pl.ds(pl.multiple_of((1 - c) * hr, 8), hr), :]
            _remote(other, other, send.at[w], recv.at[w], sib).wait_recv()
        for cp in copies:
            cp.wait_send()

    anyspec = pl.BlockSpec(memory_space=pl.ANY)
    return pl.pallas_call(
        body, name=name,
        out_shape=[jax.ShapeDtypeStruct(s.shape, F32) for s in shards],
        in_specs=[anyspec] * nw, out_specs=[anyspec] * nw,
        input_output_aliases={w: w for w in range(nw)},
        scratch_shapes=[pltpu.SemaphoreType.DMA((nw,)), pltpu.SemaphoreType.DMA((nw,))])(*shards)


def _cast_into_full(w, kc, axis, name):
    r, cs = w.shape
    tr = _tile(r, 256, 16)

    def body(kc_ref, w_ref, o_ref):
        del kc_ref
        o_ref[...] = w_ref[...].astype(BF16)

    if axis == 1:
        ospec = pl.BlockSpec((tr, cs), lambda i, kc_ref: (i, kc_ref[0]))
    else:
        ospec = pl.BlockSpec((tr, cs), lambda i, kc_ref: (kc_ref[0] * (r // tr) + i, 0))
    return pl.pallas_call(
        body, name=name,
        grid_spec=pltpu.PrefetchScalarGridSpec(
            num_scalar_prefetch=1, grid=(r // tr,), in_specs=[pl.BlockSpec((tr, cs), lambda i, kc_ref: (i, 0))],
            out_specs=ospec),
        out_shape=jax.ShapeDtypeStruct(_full_shape((r, cs), axis), BF16),
        compiler_params=_params(1, False))(kc, w)


def _adam_math(w, g, m, v):
    m2 = ADAM_B1 * m + (1.0 - ADAM_B1) * g
    v2 = ADAM_B2 * v + (1.0 - ADAM_B2) * (g * g)
    m_hat = m2 / (1.0 - ADAM_B1 ** ADAM_STEP)
    v_hat = v2 / (1.0 - ADAM_B2 ** ADAM_STEP)
    delta = -ADAM_LR * (m_hat / (jnp.sqrt(v_hat) + ADAM_EPS) + ADAM_WD * w)
    return delta, m2, v2


def _adam(w, g, m, v, name):
    r, cs = w.shape
    tr = _tile(r, 256, 8)

    def body(w_ref, g_ref, m_ref, v_ref, d_ref, m2_ref, v2_ref):
        d, m2, v2 = _adam_math(w_ref[...], g_ref[...], m_ref[...], v_ref[...])
        d_ref[...] = d
        m2_ref[...] = m2
        v2_ref[...] = v2

    spec = pl.BlockSpec((tr, cs), lambda i: (i, 0))
    shp = jax.ShapeDtypeStruct((r, cs), F32)
    return pl.pallas_call(body, name=name, grid=(r // tr,), in_specs=[spec] * 4, out_specs=[spec] * 3,
                          out_shape=[shp, shp, shp], compiler_params=_params(1, False))(w, g, m, v)


def _mod_rows(a16, w, b, name):
    D, n = w.shape
    tn = _tile(n, 512)

    def body(a_ref, w_ref, b_ref, o_ref):
        a = a_ref[...]
        o_ref[...] = _nn((a * _sigmoid(a)).astype(BF16), w_ref[...].astype(BF16)) + b_ref[...]

    return pl.pallas_call(
        body, name=name, grid=(n // tn,),
        in_specs=[pl.BlockSpec((16, D), lambda j: (0, 0)), pl.BlockSpec((D, tn), lambda j: (0, j)),
                  pl.BlockSpec((1, tn), lambda j: (0, j))],
        out_specs=pl.BlockSpec((16, tn), lambda j: (0, j)),
        out_shape=jax.ShapeDtypeStruct((16, n), F32), compiler_params=_params(1, False))(a16, w, b)


def _w_mod_update(a16, d16, w, m, v):
    D, n = w.shape
    tn = _tile(n, 256)

    def body(a_ref, d_ref, w_ref, m_ref, v_ref, g_ref, dl_ref, m2_ref, v2_ref, p_ref):
        @pl.when(pl.program_id(0) == 0)
        def _():
            p_ref[...] = jnp.zeros_like(p_ref)
        a = a_ref[...]
        db = d_ref[...].astype(BF16)
        wv = w_ref[...]
        g = _tn((a * _sigmoid(a)).astype(BF16), db)
        g_ref[...] = g
        d, m2, v2 = _adam_math(wv, g, m_ref[...], v_ref[...])
        dl_ref[...] = d
        m2_ref[...] = m2
        v2_ref[...] = v2
        p_ref[...] += _nt(db, wv.astype(BF16))

    wspec = pl.BlockSpec((D, tn), lambda j: (0, j))
    shp = jax.ShapeDtypeStruct((D, n), F32)
    return pl.pallas_call(
        body, name="w_mod_update", grid=(n // tn,),
        in_specs=[pl.BlockSpec((16, D), lambda j: (0, 0)), pl.BlockSpec((16, tn), lambda j: (0, j)), wspec, wspec, wspec],
        out_specs=[wspec, wspec, wspec, wspec, pl.BlockSpec((16, D), lambda j: (0, 0))],
        out_shape=[shp, shp, shp, shp, jax.ShapeDtypeStruct((16, D), F32)],
        compiler_params=_params(1))(a16, d16, w, m, v)


def _sum_devices(g8, name):
    _, R, Cc = g8.shape

    def body(g_ref, o_ref):
        t = g_ref[0]
        for d in range(1, N_DEV):
            t = t + g_ref[d]
        o_ref[...] = t

    return pl.pallas_call(body, name=name, out_shape=jax.ShapeDtypeStruct((R, Cc), F32))(g8)


def _c_ctx_grad(parts, c_ctx):
    D = c_ctx.shape[1]

    def body(p_ref, c_ref, o_ref):
        t = p_ref[0]
        for k in range(1, N_CHIPS):
            t = t + p_ref[2 * k]
        cv = c_ref[...]
        sg = _sigmoid(cv)
        o_ref[...] = t * (sg * (1.0 + cv * (1.0 - sg)))

    return pl.pallas_call(body, name="c_ctx_grad", out_shape=jax.ShapeDtypeStruct((1, D), F32))(parts, c_ctx)


def _pack_rows(items, nrows, width, name):
    arrays, plan = [], []
    for a, r0, nr, d0, c0 in items:
        for ai, b in enumerate(arrays):
            if b is a:
                break
        else:
            ai = len(arrays)
            arrays.append(a)
        plan.append((ai, r0, nr, d0, c0, a.shape[1]))

    def body(*refs):
        o_ref = refs[-1]
        o_ref[...] = jnp.zeros_like(o_ref)
        for ai, r0, nr, d0, c0, w in plan:
            o_ref[d0:d0 + nr, c0:c0 + w] = refs[ai][r0:r0 + nr, :]

    return pl.pallas_call(body, name=name, out_shape=jax.ShapeDtypeStruct((nrows, width), F32))(*arrays)


HBM_SPEC = pl.BlockSpec(memory_space=pltpu.HBM)
SEM_SPEC = pl.BlockSpec(memory_space=pltpu.SEMAPHORE)
SPLIT_PARAMS = pltpu.CompilerParams(has_side_effects=pltpu.SideEffectType.DATAFLOW_SIDE_EFFECTING)


def _in_hbm(a):
    return pltpu.with_memory_space_constraint(a, pltpu.HBM)


def _ag_chips_start(fulls, shapes, axes, after, name):
    nw = len(fulls)

    def body(*refs):
        in_refs, send, recv, token = refs[:nw], refs[nw + 1], refs[nw + 2], refs[-1]
        x, y, c = _my_pos()
        k0 = 2 * x + y
        for w in range(nw):
            own = _region(in_refs[w], k0, c, shapes[w], axes[w])
            for j, ch in enumerate(_other_chips(x, y)):
                _remote(own, own, send.at[3 * w + j], recv.at[3 * w + j], (ch[0], ch[1], c)).start()
        token[...] = jnp.zeros_like(token)

    return pl.pallas_call(
        body, name=name,
        out_shape=(pltpu.SemaphoreType.DMA((3 * nw,)), pltpu.SemaphoreType.DMA((3 * nw,)),
                   *[pltpu.HBM(f.shape, f.dtype) for f in fulls], jax.ShapeDtypeStruct((8, LANES), F32)),
        in_specs=[HBM_SPEC] * nw + [pl.BlockSpec(memory_space=pl.ANY)],
        out_specs=(SEM_SPEC, SEM_SPEC, *[HBM_SPEC] * nw, pl.BlockSpec(memory_space=pltpu.VMEM)),
        input_output_aliases={w: 2 + w for w in range(nw)},
        compiler_params=SPLIT_PARAMS)(*[_in_hbm(f) for f in fulls], after)


def _ag_chips_wait(send, recv, fulls, shapes, axes, after, name):
    nw = len(fulls)

    def body(*refs):
        in_refs, send_ref, recv_ref = refs[:nw], refs[nw], refs[nw + 1]
        x, y, c = _my_pos()
        k0 = 2 * x + y
        for w in range(nw):
            own = _region(in_refs[w], k0, c, shapes[w], axes[w])
            for j, ch in enumerate(_other_chips(x, y)):
                got = _region(in_refs[w], 2 * ch[0] + ch[1], c, shapes[w], axes[w])
                cp = _remote(own, got, send_ref.at[3 * w + j], recv_ref.at[3 * w + j], (ch[0], ch[1], c))
                cp.wait_send()
                cp.wait_recv()

    return pl.pallas_call(
        body, name=name,
        out_shape=tuple(pltpu.HBM(f.shape, f.dtype) for f in fulls),
        in_specs=[HBM_SPEC] * nw + [SEM_SPEC, SEM_SPEC, pl.BlockSpec(memory_space=pl.ANY)],
        out_specs=tuple([HBM_SPEC] * nw),
        input_output_aliases={w: w for w in range(nw)},
        compiler_params=SPLIT_PARAMS)(*fulls, send, recv, after)


def _ag_forward(fulls, shapes, axes, name):
    nw = len(fulls)

    def body(*refs):
        out_refs = refs[nw:2 * nw]
        send, recv = refs[2 * nw:]
        x, y, c = _my_pos()
        sib = (x, y, 1 - c)
        chips = _other_chips(x, y)
        copies = []
        for w in range(nw):
            for j, ch in enumerate(chips):
                got = _region(out_refs[w], 2 * ch[0] + ch[1], c, shapes[w], axes[w])
                cp = _remote(got, got, send.at[w, j], recv.at[w, j], sib)
                cp.start()
                copies.append(cp)
        for w in range(nw):
            for j, ch in enumerate(chips):
                got = _region(out_refs[w], 2 * ch[0] + ch[1], 1 - c, shapes[w], axes[w])
                _remote(got, got, send.at[w, j], recv.at[w, j], sib).wait_recv()
        for cp in copies:
            cp.wait_send()

    anyspec = pl.BlockSpec(memory_space=pl.ANY)
    return pl.pallas_call(
        body, name=name,
        out_shape=[jax.ShapeDtypeStruct(f.shape, BF16) for f in fulls],
        in_specs=[anyspec] * nw, out_specs=[anyspec] * nw,
        input_output_aliases={w: w for w in range(nw)},
        scratch_shapes=[pltpu.SemaphoreType.DMA((nw, 3)), pltpu.SemaphoreType.DMA((nw, 3))])(*fulls)


def _rs_chips_start(sums, name):
    nw = len(sums)

    def body(*refs):
        s_refs, l_refs, send, recv, token = refs[:nw], refs[nw:2 * nw], refs[2 * nw], refs[2 * nw + 1], refs[-1]
        x, y, c = _my_pos()
        k0 = 2 * x + y
        for w in range(nw):
            for j, ch in enumerate(_other_chips(x, y)):
                _remote(s_refs[w].at[2 * ch[0] + ch[1]], l_refs[w].at[k0], send.at[3 * w + j], recv.at[3 * w + j],
                        (ch[0], ch[1], c)).start()
        token[...] = jnp.zeros_like(token)

    thru = [pltpu.HBM(s.shape, s.dtype) for s in sums]
    return pl.pallas_call(
        body, name=name,
        out_shape=(pltpu.SemaphoreType.DMA((3 * nw,)), pltpu.SemaphoreType.DMA((3 * nw,)), *thru, *thru,
                   jax.ShapeDtypeStruct((8, LANES), F32)),
        in_specs=[HBM_SPEC] * (2 * nw),
        out_specs=(SEM_SPEC, SEM_SPEC, *[HBM_SPEC] * (2 * nw), pl.BlockSpec(memory_space=pltpu.VMEM)),
        input_output_aliases={i: 2 + i for i in range(2 * nw)},
        compiler_params=SPLIT_PARAMS)(*[_in_hbm(s) for s in sums], *[_in_hbm(lax.empty(s.shape, s.dtype)) for s in sums])


def _rs_chips_wait(send, recv, sums, lands, after, name):
    nw = len(sums)

    def body(*refs):
        s_refs, l_refs, send_ref, recv_ref = refs[:nw], refs[nw:2 * nw], refs[2 * nw], refs[2 * nw + 1]
        x, y, c = _my_pos()
        for w in range(nw):
            for j, ch in enumerate(_other_chips(x, y)):
                kj = 2 * ch[0] + ch[1]
                cp = _remote(s_refs[w].at[kj], l_refs[w].at[kj], send_ref.at[3 * w + j], recv_ref.at[3 * w + j],
                             (ch[0], ch[1], c))
                cp.wait_send()
                cp.wait_recv()

    thru = tuple(pltpu.HBM(s.shape, s.dtype) for s in sums)
    return pl.pallas_call(
        body, name=name, out_shape=thru + thru,
        in_specs=[HBM_SPEC] * (2 * nw) + [SEM_SPEC, SEM_SPEC, pl.BlockSpec(memory_space=pl.ANY)],
        out_specs=tuple([HBM_SPEC] * (2 * nw)),
        input_output_aliases={i: i for i in range(2 * nw)},
        compiler_params=SPLIT_PARAMS)(*sums, *lands, send, recv, after)


LOSS_LANE = 64


def kernel(x, c, ctx, c_ctx, w_mod, b_mod, norm_mix, norm_ffn, w_in, ret_decay, attn_sink, w_out, w_gate, w_up, w_down, norm_final, loss_target, m_c_ctx, m_w_mod, m_b_mod, m_norm_mix, m_norm_ffn, m_w_in, m_ret_decay, m_attn_sink, m_w_out, m_w_gate, m_w_up, m_w_down, m_norm_final, v_c_ctx, v_w_mod, v_b_mod, v_norm_mix, v_norm_ffn, v_w_in, v_ret_decay, v_attn_sink, v_w_out, v_w_gate, v_w_up, v_w_down, v_norm_final):
    D = x.shape[-1]
    n3 = w_mod.shape[-1]
    xi, yi, ci = _my_pos()
    b = 4 * xi + 2 * yi + ci
    k0 = 2 * xi + yi
    cvec = jnp.reshape(ci, (1,)).astype(jnp.int32)
    kc = jnp.stack([k0, ci]).astype(jnp.int32)

    dense = [("w_in", w_in[0], 1), ("w_out", w_out[0], 0), ("w_gate", w_gate[0], 1), ("w_up", w_up[0], 1),
             ("w_down", w_down[0], 0)]
    axes = [a for _, _, a in dense]
    shapes = [w.shape for _, w, _ in dense]
    c_all = _allgather8(c, "gather_c").reshape(N_DEV, D)
    c_ctx2 = c_ctx.reshape(1, D)
    a16 = _pack_rows([(c_all, 0, N_DEV, 0, 0), (c_ctx2, 0, 1, N_DEV, 0)], 16, D, "pack_cond")
    b_cols = lax.dynamic_slice_in_dim(b_mod, k0 * n3, n3, axis=1)
    mod16 = _mod_rows(a16, w_mod[0], b_cols, "mod_rows")
    mod_all = _allgather8(mod16, "gather_mod")

    own_in = _cast_into_full(dense[0][1], kc, axes[0], "cast_w_in")
    agi = _ag_chips_start([own_in], shapes[:1], axes[:1], mod_all, "ag_in_start")
    own16 = [_cast_into_full(w, kc, a, "cast_" + n) for n, w, a in dense[1:]]
    (f_in,) = _ag_forward(list(_ag_chips_wait(agi[0], agi[1], [agi[2]], shapes[:1], axes[:1], own16[-1], "ag_in_wait")),
                          shapes[:1], axes[:1], "ag_in_forward")
    ag = _ag_chips_start(own16, shapes[1:], axes[1:], f_in, "ag_rest_start")
    ag_send, ag_recv, ag_thru, ag_tok = ag[0], ag[1], list(ag[2:-1]), ag[-1][0:1, 0:1]

    def rest_weights(after):
        landed_w = _ag_chips_wait(ag_send, ag_recv, ag_thru, shapes[1:], axes[1:], after, "ag_rest_wait")
        return _ag_forward(list(landed_w), shapes[1:], axes[1:], "ag_rest_forward")
    mine = jnp.stack([lax.dynamic_index_in_dim(mod_all, 2 * k + ci, 0, keepdims=False) for k in range(N_CHIPS)])
    mod = lax.dynamic_index_in_dim(mine, b, 1, keepdims=False).reshape(6, D)
    modc = mine[:, N_DEV].reshape(6, D)

    lg = -jnp.exp(ret_decay[0])

    index = {n: i for i, (n, _, _) in enumerate(dense)}
    pending, done = [], {}

    def on_grads(names, gs):
        ids = [index[n] for n in names]
        shp, axs = [shapes[i] for i in ids], [axes[i] for i in ids]
        from_sib = _rs_sibling(gs, shp, axs, "rs_sibling_" + names[0])
        sums = [_add_halves(g, r, cvec, s, a, "add_halves_" + n) for g, r, s, a, n in zip(gs, from_sib, shp, axs, names)]
        st = _rs_chips_start(sums, "rs_chips_start_" + names[0])
        nw = len(names)
        pending.append((names, st[0], st[1], list(st[2:2 + nw]), list(st[2 + nw:2 + 2 * nw])))
        return st[-1][0:1, 0:1]

    out = _local_step(x[0], ctx[0], loss_target[0], mod, modc, norm_mix + ag_tok, norm_ffn, norm_final.reshape(1, D), lg,
                      attn_sink, f_in, rest_weights, on_grads)

    def finish(group, after):
        names, send, recv, sums, lands = group
        res = _rs_chips_wait(send, recv, sums, lands, after, "rs_chips_wait_" + names[0])
        return [_sum_chips(res[i], res[len(names) + i], kc, "sum_chips_" + n) for i, n in enumerate(names)]

    assert pending[-1][0] == ["w_in"]
    rest_names = [n for g in pending[:-1] for n in g[0]]
    rest_halves = [h for g in pending[:-1] for h in finish(g, out["grad_x"])]
    g_rest = dict(zip(rest_names, _exchange_halves(rest_halves, "exchange_halves_rest")))

    nh = 2 * RET_HEADS
    small_all = _allgather8(out["small"], "gather_small")
    tot = _sum_devices(small_all, "sum_small")
    g_b_mod = (tot[0:6] + tot[6:12]).reshape(1, 6 * D)
    dmodc_tot = tot[6:12].reshape(1, 6 * D)
    dmod_rows = small_all[:, 0:6].reshape(N_DEV, 6 * D)
    d16 = _pack_rows([(dmod_rows, 0, N_DEV, 0, 0), (dmodc_tot, 0, 1, N_DEV, 0)], 16, 6 * D, "pack_dmod")
    d16 = lax.dynamic_slice_in_dim(d16, k0 * n3, n3, axis=1)
    g_w_mod, dl_w_mod, m2_w_mod, v2_w_mod, part = _w_mod_update(a16, d16, w_mod[0], m_w_mod[0], v_w_mod[0])
    part_all = _allgather8(part[N_DEV:N_DEV + 1], "gather_c_ctx")
    g_c_ctx = _c_ctx_grad(part_all, c_ctx2)
    loss = tot[15, LOSS_LANE]
    g_ret_decay = tot[15, :nh].reshape(1, 2, RET_HEADS)
    g_sink = tot[15, nh:nh + ATT_HEADS].reshape(1, ATT_HEADS)

    def pack(cc, bm, nm, nf, nfin, rd, sk, name):
        rd2 = rd.reshape(2, RET_HEADS)
        return _pack_rows([(bm.reshape(6, D), 0, 6, 0, 0), (cc.reshape(1, D), 0, 1, 6, 0), (nm.reshape(1, D), 0, 1, 7, 0),
                           (nf.reshape(1, D), 0, 1, 8, 0), (nfin.reshape(1, D), 0, 1, 9, 0),
                           (rd2, 0, 1, 10, 0), (rd2, 1, 1, 10, RET_HEADS), (sk.reshape(1, ATT_HEADS), 0, 1, 10, nh)],
                          16, D, name)

    w_s = pack(c_ctx, b_mod, norm_mix, norm_ffn, norm_final, ret_decay, attn_sink, "pack_w")
    g_s = _pack_rows([(g_b_mod.reshape(6, D), 0, 6, 0, 0), (g_c_ctx, 0, 1, 6, 0), (tot, 12, 3, 7, 0),
                      (tot[15:16, 0:nh + ATT_HEADS], 0, 1, 10, 0)], 16, D, "pack_g")
    m_s = pack(m_c_ctx, m_b_mod, m_norm_mix, m_norm_ffn, m_norm_final, m_ret_decay, m_attn_sink, "pack_m")
    v_s = pack(v_c_ctx, v_b_mod, v_norm_mix, v_norm_ffn, v_norm_final, v_ret_decay, v_attn_sink, "pack_v")
    small_upd = _adam(w_s, g_s, m_s, v_s, "adam_small")

    def unpack(t):
        return dict(b_mod=t[0:6].reshape(1, 6 * D), c_ctx=t[6], norm_mix=t[7:8], norm_ffn=t[8:9], norm_final=t[9],
                    ret_decay=t[10, :nh].reshape(1, 2, RET_HEADS), attn_sink=t[10, nh:nh + ATT_HEADS].reshape(1, ATT_HEADS))

    dense_w = dict(w_in=(w_in, m_w_in, v_w_in), w_out=(w_out, m_w_out, v_w_out), w_gate=(w_gate, m_w_gate, v_w_gate),
                   w_up=(w_up, m_w_up, v_w_up), w_down=(w_down, m_w_down, v_w_down))
    grads = dict(unpack(g_s), w_mod=g_w_mod[None])
    upd = [dict(unpack(t)) for t in small_upd]
    upd[0]["w_mod"], upd[1]["w_mod"], upd[2]["w_mod"] = dl_w_mod[None], m2_w_mod[None], v2_w_mod[None]
    def update(n, g):
        w_, m_, v_ = dense_w[n]
        res = _adam(w_[0], g, m_[0], v_[0], "adam_" + n)
        grads[n] = g[None]
        for u, r_ in zip(upd, res):
            u[n] = r_[None]
        return res[0]

    dep = small_upd[0][0:1, 0:1] + dl_w_mod[0:1, 0:1]
    for n in rest_names:
        dep = dep + update(n, g_rest[n])[0:1, 0:1]
    (g_in,) = _exchange_halves(finish(pending[-1], dep), "exchange_halves_in")
    update("w_in", g_in)

    order = ['c_ctx', 'w_mod', 'b_mod', 'norm_mix', 'norm_ffn', 'w_in', 'ret_decay', 'attn_sink', 'w_out', 'w_gate',
             'w_up', 'w_down', 'norm_final']
    outs = [loss, out["grad_x"][None]] + [grads[n] for n in order]
    for u in upd:
        outs += [u[n] for n in order]
    return tuple(outs)
```

```python
import functools
import numpy as np
import jax
import jax.numpy as jnp
from jax import lax
from jax.experimental import pallas as pl
from jax.experimental.pallas import tpu as pltpu

F32 = jnp.float32
BF16 = jnp.bfloat16

RET_HEADS = 8
RET_DK = 64
RET_DV = 128
CHUNK = 128
ATT_HEADS = 16
ATT_KV_HEADS = 4
ATT_DH = 64
GRID_W = 64
ROPE_BASE = 10000.0
NORM_EPS = 1e-6
ADAM_LR = 0.001
ADAM_B1 = 0.9
ADAM_B2 = 0.999
ADAM_EPS = 1e-08
ADAM_WD = 0.01
ADAM_STEP = 10
NEG = -1e30
LANES = 128
VMEM_LIMIT = 56 * 1024 * 1024
ROWS_PER_LATCH = 1024
MESH = pl.DeviceIdType.MESH
N_CHIPS = 4
N_DEV = 8


def _nn(a, b):
    return jnp.dot(a, b, preferred_element_type=F32)


def _nt(a, b):
    return lax.dot_general(a, b, (((1,), (1,)), ((), ())), preferred_element_type=F32)


def _tn(a, b):
    return lax.dot_general(a, b, (((0,), (0,)), ((), ())), preferred_element_type=F32)


def _tile(n, pref, unit=LANES):
    t = min(n, pref)
    t -= t % unit
    while t > unit and n % t:
        t -= unit
    if t <= 0 or n % t:
        return n
    return t


def _params(ndim, vmem=True):
    return pltpu.CompilerParams(dimension_semantics=("arbitrary",) * ndim,
                                vmem_limit_bytes=VMEM_LIMIT if vmem else None)


def _sigmoid(x):
    return 0.5 * jnp.tanh(0.5 * x) + 0.5


def _fsum(x):
    return jnp.sum(jnp.sum(x, axis=0, keepdims=True), axis=1, keepdims=True)


def _rope_tables(L):
    lane = np.arange(LANES)
    d = lane % 64
    inv_r = jnp.asarray(ROPE_BASE, F32) ** (-jnp.arange(32, dtype=F32) / 32)
    t = jnp.arange(L)
    ang_r = t.astype(F32)[:, None] * jnp.tile(inv_r, LANES // 32)[None, :]
    Rr = np.zeros((LANES, LANES), np.float32)
    for l in range(LANES):
        if d[l] < 32:
            Rr[l + 32, l] = -1.0
        else:
            Rr[l - 32, l] = 1.0
    inv_a = jnp.asarray(ROPE_BASE, F32) ** (-jnp.arange(16, dtype=F32) / 16)
    rows = (t // GRID_W).astype(F32)
    cols = (t % GRID_W).astype(F32)
    dd = d % 32
    pos = jnp.where(jnp.asarray(d < 32)[None, :], rows[:, None], cols[:, None])
    ang_a = pos * jnp.tile(inv_a, LANES // 16)[None, :]
    Ra = np.zeros((LANES, LANES), np.float32)
    for l in range(LANES):
        if dd[l] < 16:
            Ra[l + 16, l] = -1.0
        else:
            Ra[l - 16, l] = 1.0
    D0 = np.zeros((LANES, LANES), np.float32)
    D1 = np.zeros((LANES, LANES), np.float32)
    for l in range(LANES):
        D0[l % 64, l] = 1.0
        D1[64 + l % 64, l] = 1.0
    return dict(
        Cr=jnp.cos(ang_r), Sr=jnp.sin(ang_r), Rr=jnp.asarray(Rr, BF16), RrT=jnp.asarray(Rr.T, BF16),
        Ca=jnp.cos(ang_a), Sa=jnp.sin(ang_a), Ra=jnp.asarray(Ra, BF16), RaT=jnp.asarray(Ra.T, BF16),
        D0=jnp.asarray(D0, BF16), D1=jnp.asarray(D1, BF16),
        D0T=jnp.asarray(D0.T, BF16), D1T=jnp.asarray(D1.T, BF16))


def _norm_mod(xf, g, sh, sc):
    r = lax.rsqrt(jnp.mean(xf * xf, axis=-1, keepdims=True) + NORM_EPS)
    return (xf * r * g) * (1.0 + sc) + sh


def _norm_mod_matmul(x, g, sh, sc, w, name):
    M, D = x.shape
    N = w.shape[1]
    tm, tn = _tile(M, ROWS_PER_LATCH, 8), _tile(N, 768)

    def body(x_ref, g_ref, sh_ref, sc_ref, w_ref, p_ref, h_ref, hs):
        @pl.when(pl.program_id(1) == 0)
        def _():
            hb = _norm_mod(x_ref[...], g_ref[...], sh_ref[...], sc_ref[...]).astype(BF16)
            hs[...] = hb
            h_ref[...] = hb
        p_ref[...] = _nn(hs[...], w_ref[...]).astype(BF16)

    vec = pl.BlockSpec((1, D), lambda i, j: (0, 0))
    return pl.pallas_call(
        body, name=name, grid=(M // tm, N // tn),
        in_specs=[pl.BlockSpec((tm, D), lambda i, j: (i, 0)), vec, vec, vec,
                  pl.BlockSpec((D, tn), lambda i, j: (0, j))],
        out_specs=[pl.BlockSpec((tm, tn), lambda i, j: (i, j)), pl.BlockSpec((tm, D), lambda i, j: (i, 0))],
        out_shape=[jax.ShapeDtypeStruct((M, N), BF16), jax.ShapeDtypeStruct((M, D), BF16)],
        scratch_shapes=[pltpu.VMEM((tm, D), BF16)],
        compiler_params=_params(2))(x, g, sh, sc, w)


def _proj_residual(a, w, xres, gt, name):
    M, K = a.shape
    N = w.shape[1]
    tm, tn = _tile(M, ROWS_PER_LATCH, 8), _tile(N, 1024 if K <= 2048 else 512)

    def body(a_ref, w_ref, x_ref, gt_ref, xo_ref, o_ref):
        o = _nn(a_ref[...], w_ref[...])
        o_ref[...] = o.astype(BF16)
        xo_ref[...] = x_ref[...] + gt_ref[...] * o

    return pl.pallas_call(
        body, name=name, grid=(M // tm, N // tn),
        in_specs=[pl.BlockSpec((tm, K), lambda i, j: (i, 0)), pl.BlockSpec((K, tn), lambda i, j: (0, j)),
                  pl.BlockSpec((tm, tn), lambda i, j: (i, j)), pl.BlockSpec((1, tn), lambda i, j: (0, j))],
        out_specs=[pl.BlockSpec((tm, tn), lambda i, j: (i, j)), pl.BlockSpec((tm, tn), lambda i, j: (i, j))],
        out_shape=[jax.ShapeDtypeStruct((M, N), F32), jax.ShapeDtypeStruct((M, N), BF16)],
        compiler_params=_params(2))(a, w, xres, gt)


def _ffn_in(x1, g, sh, sc, wg, wu):
    M, D = x1.shape
    N = wg.shape[1]
    tm, tn = _tile(M, ROWS_PER_LATCH, 8), _tile(N, 512)

    def body(x_ref, g_ref, sh_ref, sc_ref, wg_ref, wu_ref, G_ref, U_ref, A_ref, h_ref, hs):
        @pl.when(pl.program_id(1) == 0)
        def _():
            hb = _norm_mod(x_ref[...], g_ref[...], sh_ref[...], sc_ref[...]).astype(BF16)
            hs[...] = hb
            h_ref[...] = hb
        G = _nn(hs[...], wg_ref[...])
        U = _nn(hs[...], wu_ref[...])
        G_ref[...] = G.astype(BF16)
        U_ref[...] = U.astype(BF16)
        A_ref[...] = (G * _sigmoid(G) * U).astype(BF16)

    vec = pl.BlockSpec((1, D), lambda i, j: (0, 0))
    wspec = pl.BlockSpec((D, tn), lambda i, j: (0, j))
    ospec = pl.BlockSpec((tm, tn), lambda i, j: (i, j))
    big = jax.ShapeDtypeStruct((M, N), BF16)
    return pl.pallas_call(
        body, name="ffn_in", grid=(M // tm, N // tn),
        in_specs=[pl.BlockSpec((tm, D), lambda i, j: (i, 0)), vec, vec, vec, wspec, wspec],
        out_specs=[ospec, ospec, ospec, pl.BlockSpec((tm, D), lambda i, j: (i, 0))],
        out_shape=[big, big, big, jax.ShapeDtypeStruct((M, D), BF16)],
        scratch_shapes=[pltpu.VMEM((tm, D), BF16)],
        compiler_params=_params(2))(x1, g, sh, sc, wg, wu)


def _final(x2, gn, tgt, gt, saved):
    M, D = x2.shape
    tm = _tile(M, 256, 8)

    def body(x_ref, g_ref, t_ref, gt_ref, sv_ref, dx_ref, loss_ref, dg_ref, dz_ref, dgt_ref):
        @pl.when(pl.program_id(0) == 0)
        def _():
            loss_ref[...] = jnp.zeros_like(loss_ref)
            dg_ref[...] = jnp.zeros_like(dg_ref)
            dgt_ref[...] = jnp.zeros_like(dgt_ref)
        x = x_ref[...]
        g = g_ref[...]
        r = lax.rsqrt(jnp.mean(x * x, axis=-1, keepdims=True) + NORM_EPS)
        xh = x * r
        e = xh * g - t_ref[...]
        loss_ref[...] += (0.5 / D) * _fsum(e * e)
        dy = e * (1.0 / D)
        dg_ref[...] += jnp.sum(dy * xh, axis=0, keepdims=True)
        dxh = dy * g
        d = r * (dxh - xh * jnp.mean(dxh * xh, axis=-1, keepdims=True))
        dx_ref[...] = d
        dz_ref[...] = (d * gt_ref[...]).astype(BF16)
        dgt_ref[...] += jnp.sum(d * sv_ref[...].astype(F32), axis=0, keepdims=True)

    row = pl.BlockSpec((tm, D), lambda i: (i, 0))
    vec = pl.BlockSpec((1, D), lambda i: (0, 0))
    return pl.pallas_call(
        body, name="final_loss", grid=(M // tm,),
        in_specs=[row, vec, row, vec, row],
        out_specs=[row, pl.BlockSpec((1, LANES), lambda i: (0, 0)), vec, row, vec],
        out_shape=[jax.ShapeDtypeStruct((M, D), F32), jax.ShapeDtypeStruct((1, LANES), F32),
                   jax.ShapeDtypeStruct((1, D), F32), jax.ShapeDtypeStruct((M, D), BF16),
                   jax.ShapeDtypeStruct((1, D), F32)],
        compiler_params=_params(1))(x2, gn, tgt, gt, saved)


def _col_group(blk0, nblk):
    return int(np.gcd(blk0, nblk)) if blk0 else nblk


def _rope_cols(src, blk0, nblk, Ct, St, R, scale, rope, name):
    M = src.shape[0]
    tm = _tile(M, 512, 8)
    wb = _col_group(blk0, nblk)

    def body(x_ref, c_ref, s_ref, r_ref, o_ref):
        for j in range(wb):
            cols = slice(j * LANES, (j + 1) * LANES)
            x = x_ref[:, cols]
            xf = x.astype(F32)
            if rope:
                xf = xf * c_ref[...] + _nn(x.astype(BF16), r_ref[...]) * s_ref[...]
            o_ref[:, cols] = (xf * scale).astype(BF16)

    tab = pl.BlockSpec((tm, LANES), lambda i, j: (i, 0))
    return pl.pallas_call(
        body, name=name, grid=(M // tm, nblk // wb),
        in_specs=[pl.BlockSpec((tm, wb * LANES), lambda i, j: (i, blk0 // wb + j)), tab, tab,
                  pl.BlockSpec((LANES, LANES), lambda i, j: (0, 0))],
        out_specs=pl.BlockSpec((tm, wb * LANES), lambda i, j: (i, j)),
        out_shape=jax.ShapeDtypeStruct((M, nblk * LANES), BF16),
        compiler_params=_params(2, False))(src, Ct, St, R)


def _dup_heads(src, blk0, npair, Ct, St, R, D0, D1, rope, name):
    M = src.shape[0]
    tm = _tile(M, 512, 8)

    def body(x_ref, c_ref, s_ref, r_ref, d0_ref, d1_ref, o_ref):
        x = x_ref[...]
        if rope:
            x = (x.astype(F32) * c_ref[...] + _nn(x, r_ref[...]) * s_ref[...]).astype(BF16)
        o_ref[0] = _nn(x, d0_ref[...]).astype(BF16)
        o_ref[1] = _nn(x, d1_ref[...]).astype(BF16)

    tab = pl.BlockSpec((tm, LANES), lambda i, p: (i, 0))
    mat = pl.BlockSpec((LANES, LANES), lambda i, p: (0, 0))
    return pl.pallas_call(
        body, name=name, grid=(M // tm, npair),
        in_specs=[pl.BlockSpec((tm, LANES), lambda i, p: (i, blk0 + p)), tab, tab, mat, mat, mat],
        out_specs=pl.BlockSpec((2, tm, LANES), lambda i, p: (p, i, 0)),
        out_shape=jax.ShapeDtypeStruct((2 * npair, M, LANES), BF16),
        compiler_params=_params(2, False))(src, Ct, St, R, D0, D1)


def _unrope_cols(dsrc, dst, blk0, nblk, Ct, St, RT, scale, rope, name):
    M = dsrc.shape[0]
    tm = _tile(M, 512, 8)
    wb = _col_group(blk0, nblk)

    def body(x_ref, c_ref, s_ref, r_ref, dst_ref, o_ref):
        del dst_ref
        for j in range(wb):
            cols = slice(j * LANES, (j + 1) * LANES)
            xf = x_ref[:, cols].astype(F32)
            if rope:
                xf = xf * c_ref[...] + _nn((xf * s_ref[...]).astype(BF16), r_ref[...])
            o_ref[:, cols] = (xf * scale).astype(BF16)

    tab = pl.BlockSpec((tm, LANES), lambda i, j: (i, 0))
    return pl.pallas_call(
        body, name=name, grid=(M // tm, nblk // wb),
        in_specs=[pl.BlockSpec((tm, wb * LANES), lambda i, j: (i, j)), tab, tab,
                  pl.BlockSpec((LANES, LANES), lambda i, j: (0, 0)),
                  pl.BlockSpec(memory_space=pl.ANY)],
        out_specs=pl.BlockSpec((tm, wb * LANES), lambda i, j: (i, blk0 // wb + j)),
        out_shape=jax.ShapeDtypeStruct(dst.shape, dst.dtype),
        input_output_aliases={4: 0},
        compiler_params=_params(2, False))(dsrc, Ct, St, RT, dst)


def _fold_heads(parts, dst, blk0, npair, Ct, St, RT, D0T, D1T, rope, name):
    M = parts[0][0].shape[1]
    nb = M // CHUNK
    R = _tile(M, 1024, CHUNK)
    rb = R // CHUNK
    nrefs = sum(1 if s == 0 else 2 for _, s in parts)

    def body(*refs):
        part_refs = list(refs[:nrefs])
        c_ref, s_ref, r_ref, d0_ref, d1_ref, dst_ref, o_ref = refs[nrefs:]
        del dst_ref
        i = pl.program_id(0)
        tot = [jnp.zeros((R, LANES), F32), jnp.zeros((R, LANES), F32)]
        for _, shift in parts:
            main = part_refs.pop(0)
            if shift == 0:
                for e in range(2):
                    tot[e] = tot[e] + main[e].astype(F32)
                continue
            edge = part_refs.pop(0)
            ok = (i + 1) * rb <= nb - 1 if shift > 0 else i > 0
            for e in range(2):
                ed = jnp.where(ok, edge[e].astype(F32), 0.0)
                if rb == 1:
                    tot[e] = tot[e] + ed
                elif shift > 0:
                    tot[e] = tot[e] + jnp.concatenate([main[e, CHUNK:, :].astype(F32), ed], axis=0)
                else:
                    tot[e] = tot[e] + jnp.concatenate([ed, main[e, :R - CHUNK, :].astype(F32)], axis=0)
        f = _nn(tot[0].astype(BF16), d0_ref[...]) + _nn(tot[1].astype(BF16), d1_ref[...])
        if rope:
            f = f * c_ref[...] + _nn((f * s_ref[...]).astype(BF16), r_ref[...])
        o_ref[...] = f.astype(BF16)

    in_specs, args = [], []
    for a, shift in parts:
        assert shift in (-1, 0, 1)
        in_specs.append(pl.BlockSpec((2, R, LANES), lambda i, p: (p, i, 0)))
        args.append(a)
        if shift > 0:
            in_specs.append(pl.BlockSpec((2, CHUNK, LANES), lambda i, p: (p, jnp.minimum((i + 1) * rb, nb - 1), 0)))
            args.append(a)
        elif shift < 0:
            in_specs.append(pl.BlockSpec((2, CHUNK, LANES), lambda i, p: (p, jnp.maximum(i * rb - 1, 0), 0)))
            args.append(a)
    tab = pl.BlockSpec((R, LANES), lambda i, p: (i, 0))
    mat = pl.BlockSpec((LANES, LANES), lambda i, p: (0, 0))
    return pl.pallas_call(
        body, name=name, grid=(M // R, npair),
        in_specs=in_specs + [tab, tab, mat, mat, mat, pl.BlockSpec(memory_space=pl.ANY)],
        out_specs=pl.BlockSpec((R, LANES), lambda i, p: (i, blk0 + p)),
        out_shape=jax.ShapeDtypeStruct(dst.shape, dst.dtype),
        input_output_aliases={nrefs + 5: 0},
        compiler_params=_params(2, False))(*args, Ct, St, RT, D0T, D1T, dst)


def _head_masks():
    lane = lax.broadcasted_iota(jnp.int32, (1, LANES), 1)
    return [lane < 64, lane >= 64]


def _decay_vecs(lam, mu):
    i = lax.broadcasted_iota(jnp.int32, (CHUNK, 1), 0).astype(F32)
    return dict(qf=jnp.exp(lam * (i + 1.0)), kf=jnp.exp(lam * (CHUNK - 1.0 - i)),
                qb=jnp.exp(mu * (CHUNK - i)), kb=jnp.exp(mu * i),
                gf=jnp.exp(lam * float(CHUNK)), gb=jnp.exp(mu * float(CHUNK)), i=i)


def _decay_mask(lam, mu):
    r = lax.broadcasted_iota(jnp.int32, (CHUNK, CHUNK), 0)
    c = lax.broadcasted_iota(jnp.int32, (CHUNK, CHUNK), 1)
    rel = (r - c).astype(F32)
    low = rel >= 0.0
    mf = jnp.exp(lam * jnp.maximum(rel, 0.0))
    mb = jnp.exp(mu * jnp.maximum(-rel, 0.0))
    return jnp.where(low, mf, mb), rel, low


def _lam_of(lg_ref, row, idx):
    return jnp.full((1, 1), lg_ref[row, idx], F32)


def _group_index(pair_blk, npairs):
    assert pair_blk % npairs == 0
    return pair_blk // npairs


def _ret_states_fwd(Kr, P, Krc, Pc, lg, rv_blk, npairs):
    L = Kr.shape[0]
    Lc = Krc.shape[0]
    N, ncc = L // CHUNK, Lc // CHUNK
    rv_grp = _group_index(rv_blk, npairs)

    heads = [(p, h) for p in range(npairs) for h in range(2)]
    kcols = lambda p: slice(p * LANES, (p + 1) * LANES)
    vcols = lambda p, h: slice((2 * p + h) * LANES, (2 * p + h + 1) * LANES)

    def body(lg_ref, k_ref, v_ref, kc_ref, vc_ref, sf_ref, S):
        n = pl.program_id(0)
        masks = _head_masks()

        @pl.when(n == 0)
        def _():
            for p, h in heads:
                lam = _lam_of(lg_ref, 0, 2 * p + h)
                dv = _decay_vecs(lam, lam)
                s = jnp.zeros((LANES, LANES), F32)
                for cc in range(ncc):
                    rows = slice(cc * CHUNK, (cc + 1) * CHUNK)
                    kw = jnp.where(masks[h], kc_ref[rows, kcols(p)].astype(F32) * dv["kf"], 0.0).astype(BF16)
                    s = dv["gf"] * s + _tn(kw, vc_ref[rows, vcols(p, h)])
                S[p, h] = s

        for p, h in heads:
            lam = _lam_of(lg_ref, 0, 2 * p + h)
            dv = _decay_vecs(lam, lam)
            s = S[p, h]
            sf_ref[p, 0, h] = s.astype(BF16)
            kw = jnp.where(masks[h], k_ref[:, kcols(p)].astype(F32) * dv["kf"], 0.0).astype(BF16)
            S[p, h] = dv["gf"] * s + _tn(kw, v_ref[:, vcols(p, h)])

    wq, wv = npairs * LANES, npairs * 2 * LANES
    return pl.pallas_call(
        body, name="ret_states_fwd", grid=(N,),
        in_specs=[pl.BlockSpec(memory_space=pltpu.SMEM),
                  pl.BlockSpec((CHUNK, wq), lambda n: (n, 0)),
                  pl.BlockSpec((CHUNK, wv), lambda n: (n, rv_grp)),
                  pl.BlockSpec((Lc, wq), lambda n: (0, 0)),
                  pl.BlockSpec((Lc, wv), lambda n: (0, rv_grp))],
        out_specs=pl.BlockSpec((npairs, 1, 2, LANES, LANES), lambda n: (0, n, 0, 0, 0)),
        out_shape=jax.ShapeDtypeStruct((npairs, N, 2, LANES, LANES), BF16),
        scratch_shapes=[pltpu.VMEM((npairs, 2, LANES, LANES), F32)],
        compiler_params=_params(1, False))(lg, Kr, P, Krc, Pc)


def _ret_chunk_fwd(q, k, v, sf, sb, hm, lam, mu, Mk):
    dv = _decay_vecs(lam, mu)
    qm = jnp.where(hm, q, jnp.zeros_like(q))
    qmf = qm.astype(F32)
    A = _nt(qm, k)
    Am = A * Mk
    Amb = Am.astype(BF16)
    Qf = (qmf * dv["qf"]).astype(BF16)
    Qb = (qmf * dv["qb"]).astype(BF16)
    O = _nn(Amb, v) + _nn(Qf, sf) + _nn(Qb, sb)
    return dict(dv=dv, Mk=Mk, qm=qm, Am=Am, Amb=Amb, Qf=Qf, Qb=Qb, O=O)


def _ret_out_fwd(Qr, Kr, P, Krc, Pc, SF, lg, rv_blk, rg_blk, npairs, d_mix):
    L = Qr.shape[0]
    Lc = Krc.shape[0]
    N, ncc = L // CHUNK, Lc // CHUNK

    rv_grp, rg_grp = _group_index(rv_blk, npairs), _group_index(rg_blk, npairs)
    heads = [(p, h) for p in range(npairs) for h in range(2)]
    kcols = lambda p: slice(p * LANES, (p + 1) * LANES)
    vcols = lambda p, h: slice((2 * p + h) * LANES, (2 * p + h + 1) * LANES)

    def body(lg_ref, q_ref, k_ref, v_ref, g_ref, sf_ref, kc_ref, vc_ref, y_ref, sb_ref, S, Mks):
        n = pl.program_id(0)
        masks = _head_masks()

        @pl.when(n == 0)
        def _():
            for p, h in heads:
                mu = _lam_of(lg_ref, 1, 2 * p + h)
                Mks[p, h] = _decay_mask(_lam_of(lg_ref, 0, 2 * p + h), mu)[0]
                dvb = _decay_vecs(mu, mu)
                s = jnp.zeros((LANES, LANES), F32)
                for cc in reversed(range(ncc)):
                    rows = slice(cc * CHUNK, (cc + 1) * CHUNK)
                    kw = jnp.where(masks[h], kc_ref[rows, kcols(p)].astype(F32) * dvb["kb"], 0.0).astype(BF16)
                    s = dvb["gb"] * s + _tn(kw, vc_ref[rows, vcols(p, h)])
                S[p, h] = s

        for p, h in heads:
            lam = _lam_of(lg_ref, 0, 2 * p + h)
            mu = _lam_of(lg_ref, 1, 2 * p + h)
            hm = masks[h]
            dvb = _decay_vecs(lam, mu)
            s = S[p, h]
            sbb = s.astype(BF16)
            sb_ref[p, 0, h] = sbb
            k = k_ref[:, kcols(p)]
            v = v_ref[:, vcols(p, h)]
            f = _ret_chunk_fwd(q_ref[:, kcols(p)], k, v, sf_ref[p, 0, h], sbb, hm, lam, mu, Mks[p, h])
            O = f["O"]
            r = lax.rsqrt(jnp.mean(O * O, axis=-1, keepdims=True) + NORM_EPS)
            g = g_ref[:, vcols(p, h)].astype(F32)
            y_ref[:, vcols(p, h)] = (O * r * (g * _sigmoid(g))).astype(BF16)
            kw = jnp.where(hm, k.astype(F32) * dvb["kb"], 0.0).astype(BF16)
            S[p, h] = dvb["gb"] * s + _tn(kw, v)

    rev = lambda n: N - 1 - n
    wq, wv = npairs * LANES, npairs * 2 * LANES
    st = pl.BlockSpec((npairs, 1, 2, LANES, LANES), lambda n: (0, rev(n), 0, 0, 0))
    return pl.pallas_call(
        body, name="ret_out_fwd", grid=(N,),
        in_specs=[pl.BlockSpec(memory_space=pltpu.SMEM),
                  pl.BlockSpec((CHUNK, wq), lambda n: (rev(n), 0)),
                  pl.BlockSpec((CHUNK, wq), lambda n: (rev(n), 0)),
                  pl.BlockSpec((CHUNK, wv), lambda n: (rev(n), rv_grp)),
                  pl.BlockSpec((CHUNK, wv), lambda n: (rev(n), rg_grp)),
                  st,
                  pl.BlockSpec((Lc, wq), lambda n: (0, 0)),
                  pl.BlockSpec((Lc, wv), lambda n: (0, rv_grp))],
        out_specs=[pl.BlockSpec((CHUNK, wv), lambda n: (rev(n), 0)), st],
        out_shape=[jax.ShapeDtypeStruct((L, d_mix), BF16),
                   jax.ShapeDtypeStruct((npairs, N, 2, LANES, LANES), BF16)],
        scratch_shapes=[pltpu.VMEM((npairs, 2, LANES, LANES), F32), pltpu.VMEM((npairs, 2, CHUNK, CHUNK), F32)],
        compiler_params=_params(1))(lg, Qr, Kr, P, P, SF, Krc, Pc)


ACC_ROWS = 8


def _ret_bwd1(Qr, Kr, P, Krc, Pc, SF, SB, dY, lg, rv_blk, rg_blk, npairs, d_proj):
    L = Qr.shape[0]
    Lc = Krc.shape[0]
    N, ncc = L // CHUNK, Lc // CHUNK
    rv_grp, rg_grp = _group_index(rv_blk, npairs), _group_index(rg_blk, npairs)
    heads = [(p, h) for p in range(npairs) for h in range(2)]
    kcols = lambda p: slice(p * LANES, (p + 1) * LANES)
    vcols = lambda p, h: slice((2 * p + h) * LANES, (2 * p + h + 1) * LANES)

    def body(lg_ref, q_ref, k_ref, v_ref, g_ref, sf_ref, sb_ref, dy_ref, kc_ref, vc_ref,
             dq_ref, dk_ref, dv_ref, dg_ref, do_ref, dkc_ref, dvc_ref, acc_ref, dS, T, Mks):
        n = pl.program_id(0)
        masks = _head_masks()

        @pl.when(n == 0)
        def _():
            dS[...] = jnp.zeros_like(dS)
            T[...] = jnp.zeros_like(T)
            acc_ref[...] = jnp.zeros_like(acc_ref)
            for p, h in heads:
                Mks[p, h] = _decay_mask(_lam_of(lg_ref, 0, 2 * p + h), _lam_of(lg_ref, 1, 2 * p + h))[0]

        def head_main(p, h):
            lam = _lam_of(lg_ref, 0, 2 * p + h)
            mu = _lam_of(lg_ref, 1, 2 * p + h)
            hm = masks[h]
            hs = vcols(p, h)
            v = v_ref[:, hs]
            k = k_ref[:, kcols(p)]
            sf = sf_ref[p, 0, h]
            sb = sb_ref[p, 0, h]
            f = _ret_chunk_fwd(q_ref[:, kcols(p)], k, v, sf, sb, hm, lam, mu, Mks[p, h])
            dv_, O = f["dv"], f["O"]
            r = lax.rsqrt(jnp.mean(O * O, axis=-1, keepdims=True) + NORM_EPS)
            on = O * r
            g = g_ref[:, hs].astype(F32)
            sg = _sigmoid(g)
            dy = dy_ref[:, hs].astype(F32)
            dg_ref[:, hs] = (dy * on * (sg * (1.0 + g * (1.0 - sg)))).astype(BF16)
            don = dy * (g * sg)
            dO = r * (don - on * jnp.mean(don * on, axis=-1, keepdims=True))
            dOb = dO.astype(BF16)
            do_ref[:, hs] = dOb
            dAm = _nt(dOb, v)
            T[p, h] += dAm * f["Am"]
            dAb = (dAm * f["Mk"]).astype(BF16)
            km = jnp.where(hm, k, jnp.zeros_like(k))
            dq = _nn(dAb, km)
            dk = _tn(dAb, f["qm"])
            dvh = _tn(f["Amb"], dOb)
            dQf = _nt(dOb, sf)
            dQb = _nt(dOb, sb)
            dq = dq + dQf * dv_["qf"] + dQb * dv_["qb"]
            acc_ref[p, h, 0:1, :] += _fsum(dQf * f["Qf"].astype(F32) * (dv_["i"] + 1.0))
            acc_ref[p, h, 1:2, :] += _fsum(dQb * f["Qb"].astype(F32) * (CHUNK - dv_["i"]))
            dSh = dS[p, h]
            dSb_ = dSh.astype(BF16)
            Kf = (km.astype(F32) * dv_["kf"]).astype(BF16)
            dKf = _nt(v, dSb_)
            dk = dk + jnp.where(hm, dKf * dv_["kf"], 0.0)
            acc_ref[p, h, 2:3, :] += _fsum(jnp.where(hm, dKf, 0.0) * Kf.astype(F32) * (CHUNK - 1.0 - dv_["i"]))
            dvh = dvh + _nn(Kf, dSb_)
            acc_ref[p, h, 3:4, :] += float(CHUNK) * dv_["gf"] * _fsum(dSh * sf.astype(F32))
            dSh = dv_["gf"] * dSh + _tn(f["Qf"], dOb)
            dS[p, h] = dSh
            dv_ref[:, hs] = dvh
            return dq, dk

        for p in range(npairs):
            dq0, dk0 = head_main(p, 0)
            dq1, dk1 = head_main(p, 1)
            dq_ref[:, kcols(p)] = dq0 + dq1
            dk_ref[:, kcols(p)] = dk0 + dk1

        @pl.when(n == N - 1)
        def _():
            for p, h in heads:
                lam = _lam_of(lg_ref, 0, 2 * p + h)
                dv_ = _decay_vecs(lam, lam)
                hm = masks[h]
                hs = vcols(p, h)
                states = [jnp.zeros((LANES, LANES), F32)]
                kws = []
                for cc in range(ncc):
                    rows = slice(cc * CHUNK, (cc + 1) * CHUNK)
                    kw = jnp.where(hm, kc_ref[rows, kcols(p)].astype(F32) * dv_["kf"], 0.0).astype(BF16)
                    kws.append(kw)
                    states.append(dv_["gf"] * states[-1] + _tn(kw, vc_ref[rows, hs]))
                d = dS[p, h]
                for cc in reversed(range(ncc)):
                    db = d.astype(BF16)
                    rows = slice(cc * CHUNK, (cc + 1) * CHUNK)
                    dKf_c = jnp.where(hm, _nt(vc_ref[rows, hs], db), 0.0)
                    part = dKf_c * dv_["kf"]
                    if h == 0:
                        dkc_ref[rows, kcols(p)] = part
                    else:
                        dkc_ref[rows, kcols(p)] += part
                    acc_ref[p, h, 2:3, :] += _fsum(dKf_c * kws[cc].astype(F32) * (CHUNK - 1.0 - dv_["i"]))
                    dvc_ref[rows, hs] = _nn(kws[cc], db)
                    acc_ref[p, h, 3:4, :] += float(CHUNK) * dv_["gf"] * _fsum(d * states[cc])
                    d = dv_["gf"] * d
                _, rel, low = _decay_mask(lam, lam)
                Th = T[p, h]
                acc_ref[p, h, 4:5, :] += _fsum(jnp.where(low, Th * rel, 0.0))
                acc_ref[p, h, 5:6, :] += _fsum(jnp.where(low, 0.0, -Th * rel))

    rev = lambda n: N - 1 - n
    wq, wv = npairs * LANES, npairs * 2 * LANES
    st = pl.BlockSpec((npairs, 1, 2, LANES, LANES), lambda n: (0, rev(n), 0, 0, 0))
    pair = pl.BlockSpec((CHUNK, wq), lambda n: (rev(n), 0))
    wide = lambda grp: pl.BlockSpec((CHUNK, wv), lambda n: (rev(n), grp))
    return pl.pallas_call(
        body, name="ret_bwd_desc", grid=(N,),
        in_specs=[pl.BlockSpec(memory_space=pltpu.SMEM), pair, pair, wide(rv_grp), wide(rg_grp), st, st, wide(0),
                  pl.BlockSpec((Lc, wq), lambda n: (0, 0)),
                  pl.BlockSpec((Lc, wv), lambda n: (0, rv_grp))],
        out_specs=[pair, pair, wide(0), wide(rg_grp), wide(0),
                   pl.BlockSpec((Lc, wq), lambda n: (0, 0)),
                   pl.BlockSpec((Lc, wv), lambda n: (0, 0)),
                   pl.BlockSpec((npairs, 2, ACC_ROWS, LANES), lambda n: (0, 0, 0, 0))],
        out_shape=[jax.ShapeDtypeStruct((L, npairs * LANES), F32),
                   jax.ShapeDtypeStruct((L, npairs * LANES), F32),
                   jax.ShapeDtypeStruct((L, npairs * 2 * LANES), F32),
                   jax.ShapeDtypeStruct((L, d_proj), BF16),
                   jax.ShapeDtypeStruct((L, npairs * 2 * LANES), BF16),
                   jax.ShapeDtypeStruct((Lc, npairs * LANES), F32),
                   jax.ShapeDtypeStruct((Lc, npairs * 2 * LANES), F32),
                   jax.ShapeDtypeStruct((npairs, 2, ACC_ROWS, LANES), F32)],
        scratch_shapes=[pltpu.VMEM((npairs, 2, LANES, LANES), F32), pltpu.VMEM((npairs, 2, CHUNK, CHUNK), F32),
                        pltpu.VMEM((npairs, 2, CHUNK, CHUNK), F32)],
        compiler_params=_params(1))(lg, Qr, Kr, P, P, SF, SB, dY, Krc, Pc)


def _ret_bwd2(Qr, Kr, P, Krc, Pc, SB, dO, dKr, dVp, dP, dKc, dVc, lg, rv_blk, npairs):
    L = Qr.shape[0]
    Lc = Krc.shape[0]
    N, ncc = L // CHUNK, Lc // CHUNK
    rv_grp = _group_index(rv_blk, npairs)
    heads = [(p, h) for p in range(npairs) for h in range(2)]
    kcols = lambda p: slice(p * LANES, (p + 1) * LANES)
    vcols = lambda p, h: slice((2 * p + h) * LANES, (2 * p + h + 1) * LANES)

    def body(lg_ref, q_ref, k_ref, v_ref, sb_ref, do_ref, dkin_ref, dvin_ref, kc_ref, vc_ref, dkcin_ref, dvcin_ref,
             dpin_ref, dk_ref, dv_ref, dkc_ref, dvc_ref, acc_ref, dS):
        del dpin_ref
        n = pl.program_id(0)
        masks = _head_masks()

        @pl.when(n == 0)
        def _():
            dS[...] = jnp.zeros_like(dS)
            acc_ref[...] = jnp.zeros_like(acc_ref)

        def head_main(p, h):
            mu = _lam_of(lg_ref, 1, 2 * p + h)
            hm = masks[h]
            hs = vcols(p, h)
            dv_ = _decay_vecs(mu, mu)
            v = v_ref[:, hs]
            k = k_ref[:, kcols(p)]
            q = q_ref[:, kcols(p)]
            dOb = do_ref[:, hs]
            km = jnp.where(hm, k, jnp.zeros_like(k)).astype(F32)
            Kb = (km * dv_["kb"]).astype(BF16)
            Qb = (jnp.where(hm, q, jnp.zeros_like(q)).astype(F32) * dv_["qb"]).astype(BF16)
            dSh = dS[p, h]
            dSb_ = dSh.astype(BF16)
            dKb = jnp.where(hm, _nt(v, dSb_), 0.0)
            acc_ref[p, h, 0:1, :] += _fsum(dKb * Kb.astype(F32) * dv_["i"])
            dv_ref[:, hs] = (dvin_ref[:, hs] + _nn(Kb, dSb_)).astype(BF16)
            acc_ref[p, h, 1:2, :] += float(CHUNK) * dv_["gb"] * _fsum(dSh * sb_ref[p, 0, h].astype(F32))
            dS[p, h] = dv_["gb"] * dSh + _tn(Qb, dOb)
            return dKb * dv_["kb"]

        for p in range(npairs):
            dk_ref[:, kcols(p)] = dkin_ref[:, kcols(p)] + head_main(p, 0) + head_main(p, 1)

        @pl.when(n == N - 1)
        def _():
            for p, h in heads:
                mu = _lam_of(lg_ref, 1, 2 * p + h)
                hm = masks[h]
                hs = vcols(p, h)
                dv_ = _decay_vecs(mu, mu)
                states = {}
                kws = {}
                s = jnp.zeros((LANES, LANES), F32)
                for cc in reversed(range(ncc)):
                    rows = slice(cc * CHUNK, (cc + 1) * CHUNK)
                    states[cc] = s
                    kw = jnp.where(hm, kc_ref[rows, kcols(p)].astype(F32) * dv_["kb"], 0.0).astype(BF16)
                    kws[cc] = kw
                    s = dv_["gb"] * s + _tn(kw, vc_ref[rows, hs])
                d = dS[p, h]
                for cc in range(ncc):
                    db = d.astype(BF16)
                    rows = slice(cc * CHUNK, (cc + 1) * CHUNK)
                    dKb_c = jnp.where(hm, _nt(vc_ref[rows, hs], db), 0.0)
                    part = dKb_c * dv_["kb"]
                    if h == 0:
                        dkc_ref[rows, kcols(p)] = dkcin_ref[rows, kcols(p)] + part
                    else:
                        dkc_ref[rows, kcols(p)] += part
                    acc_ref[p, h, 0:1, :] += _fsum(dKb_c * kws[cc].astype(F32) * dv_["i"])
                    dvc_ref[rows, hs] = dvcin_ref[rows, hs] + _nn(kws[cc], db)
                    acc_ref[p, h, 1:2, :] += float(CHUNK) * dv_["gb"] * _fsum(d * states[cc])
                    d = dv_["gb"] * d

    wq, wv = npairs * LANES, npairs * 2 * LANES
    st = pl.BlockSpec((npairs, 1, 2, LANES, LANES), lambda n: (0, n, 0, 0, 0))
    pair = pl.BlockSpec((CHUNK, wq), lambda n: (n, 0))
    wide = lambda grp: pl.BlockSpec((CHUNK, wv), lambda n: (n, grp))
    ckc = pl.BlockSpec((Lc, wq), lambda n: (0, 0))
    cvc = lambda grp: pl.BlockSpec((Lc, wv), lambda n: (0, grp))
    return pl.pallas_call(
        body, name="ret_bwd_asc", grid=(N,),
        in_specs=[pl.BlockSpec(memory_space=pltpu.SMEM), pair, pair, wide(rv_grp), st, wide(0), pair, wide(0),
                  ckc, cvc(rv_grp), ckc, cvc(0), pl.BlockSpec(memory_space=pl.ANY)],
        out_specs=[pair, wide(rv_grp), ckc, cvc(0),
                   pl.BlockSpec((npairs, 2, ACC_ROWS, LANES), lambda n: (0, 0, 0, 0))],
        out_shape=[jax.ShapeDtypeStruct(dKr.shape, F32),
                   jax.ShapeDtypeStruct(dP.shape, dP.dtype),
                   jax.ShapeDtypeStruct(dKc.shape, F32),
                   jax.ShapeDtypeStruct(dVc.shape, F32),
                   jax.ShapeDtypeStruct((npairs, 2, ACC_ROWS, LANES), F32)],
        input_output_aliases={12: 1},
        scratch_shapes=[pltpu.VMEM((npairs, 2, LANES, LANES), F32)],
        compiler_params=_params(1))(lg, Qr, Kr, P, SB, dO, dKr, dVp, Krc, Pc, dKc, dVc, dP)


GROUP = 4


def _att_band(Lc):
    row = np.arange(GROUP * CHUNK)[:, None] % CHUNK
    col = np.arange(3 * CHUNK + Lc)[None, :]
    ok = ((col >= row) & (col <= row + 2 * CHUNK)) | (col >= 3 * CHUNK)
    return jnp.asarray(np.where(ok, 0.0, NEG), F32)


def _att_edge(n, N, Lc):
    col = lax.broadcasted_iota(jnp.int32, (1, 3 * CHUNK + Lc), 1)
    off = jnp.logical_or(jnp.logical_and(col < CHUNK, n == 0),
                         jnp.logical_and(jnp.logical_and(col >= 2 * CHUNK, col < 3 * CHUNK), n == N - 1))
    return jnp.where(off, NEG, 0.0)


def _stack_heads(ref, gi):
    masks = _head_masks()
    tiles = []
    for pr in range(2):
        t = ref[:, (2 * gi + pr) * LANES:(2 * gi + pr + 1) * LANES]
        for a in range(2):
            tiles.append(jnp.where(masks[a], t, jnp.zeros_like(t)))
    return jnp.concatenate(tiles, axis=0)


def _unstack_heads(x4):
    m0 = _head_masks()[0]
    return [jnp.where(m0, x4[(2 * pr) * CHUNK:(2 * pr + 1) * CHUNK], x4[(2 * pr + 1) * CHUNK:(2 * pr + 2) * CHUNK])
            for pr in range(2)]


def _sink_column(sink_ref, g):
    row = lax.broadcasted_iota(jnp.int32, (GROUP * CHUNK, 1), 0) // CHUNK
    col = jnp.zeros((GROUP * CHUNK, 1), F32)
    for h in range(GROUP):
        col = jnp.where(row == h, sink_ref[0, g * GROUP + h], col)
    return col


def _att_probs(q4, Kall, bias, snk):
    s = _nt(q4, Kall) + bias
    mx = jnp.maximum(jnp.max(s, axis=1, keepdims=True), snk)
    p = jnp.exp(s - mx)
    p_snk = jnp.exp(snk - mx)
    inv = 1.0 / (jnp.sum(p, axis=1, keepdims=True) + p_snk)
    return p, p_snk, inv


def _att_groups_per_step(nkv, blk0):
    return 2 if nkv % 2 == 0 and blk0 % 2 == 0 else 1


def _att_specs(Lc, N, gps):
    q = pl.BlockSpec((CHUNK, gps * 2 * LANES), lambda g, n: (n, g))
    kv = lambda s: pl.BlockSpec((gps, CHUNK, LANES), lambda g, n: (g, jnp.clip(n + s, 0, N - 1), 0))
    ctx = pl.BlockSpec((gps, Lc, LANES), lambda g, n: (g, 0, 0))
    return q, kv, ctx


def _att_fwd(Qa, Kd, Vd, Kdc, Vdc, sink, Y, blk0):
    L = Qa.shape[0]
    Lc = Kdc.shape[1]
    N = L // CHUNK
    nkv = Kd.shape[0]
    gps = _att_groups_per_step(nkv, blk0)

    def body(sink_ref, band_ref, q_ref, kp, kc_, kn, vp, vc_, vn, kctx, vctx, y_in, o_ref):
        del y_in
        g, n = pl.program_id(0), pl.program_id(1)
        bias = band_ref[...] + _att_edge(n, N, Lc)
        for gi in range(gps):
            Kall = jnp.concatenate([kp[gi], kc_[gi], kn[gi], kctx[gi]], axis=0)
            Vall = jnp.concatenate([vp[gi], vc_[gi], vn[gi], vctx[gi]], axis=0)
            p, _, inv = _att_probs(_stack_heads(q_ref, gi), Kall, bias, _sink_column(sink_ref, g * gps + gi))
            o4 = _nn(p.astype(BF16), Vall) * inv
            for pr, o in enumerate(_unstack_heads(o4)):
                o_ref[:, (2 * gi + pr) * LANES:(2 * gi + pr + 1) * LANES] = o.astype(BF16)

    q, kv, ctx = _att_specs(Lc, N, gps)
    band = pl.BlockSpec((GROUP * CHUNK, 3 * CHUNK + Lc), lambda g, n: (0, 0))
    return pl.pallas_call(
        body, name="att_fwd", grid=(nkv // gps, N),
        in_specs=[pl.BlockSpec(memory_space=pltpu.SMEM), band, q, kv(-1), kv(0), kv(1), kv(-1), kv(0), kv(1), ctx, ctx,
                  pl.BlockSpec(memory_space=pl.ANY)],
        out_specs=pl.BlockSpec((CHUNK, gps * 2 * LANES), lambda g, n: (n, blk0 // gps + g)),
        out_shape=jax.ShapeDtypeStruct(Y.shape, Y.dtype),
        input_output_aliases={11: 0},
        compiler_params=_params(2))(sink, _att_band(Lc), Qa, Kd, Kd, Kd, Vd, Vd, Vd, Kdc, Vdc, Y)


def _att_bwd(Qa, Kd, Vd, Kdc, Vdc, sink, dY, blk0):
    L = Qa.shape[0]
    Lc = Kdc.shape[1]
    N = L // CHUNK
    nkv = Kd.shape[0]
    gps = _att_groups_per_step(nkv, blk0)

    def body(sink_ref, band_ref, q_ref, kp, kc_, kn, vp, vc_, vn, kctx, vctx, dy_ref,
             dq_ref, dkp, dkc_, dkn, dvp, dvc_, dvn, dkctx, dvctx, dsink_ref):
        g, n = pl.program_id(0), pl.program_id(1)

        @pl.when(n == 0)
        def _():
            dkctx[...] = jnp.zeros_like(dkctx)
            dvctx[...] = jnp.zeros_like(dvctx)
            dsink_ref[...] = jnp.zeros_like(dsink_ref)

        bias = band_ref[...] + _att_edge(n, N, Lc)
        for gi in range(gps):
            Kall = jnp.concatenate([kp[gi], kc_[gi], kn[gi], kctx[gi]], axis=0)
            Vall = jnp.concatenate([vp[gi], vc_[gi], vn[gi], vctx[gi]], axis=0)
            q4 = _stack_heads(q_ref, gi)
            do4 = _stack_heads(dy_ref, gi)
            p, p_snk, inv = _att_probs(q4, Kall, bias, _sink_column(sink_ref, g * gps + gi))
            P = p * inv
            dp = _nt(do4, Vall)
            delta = jnp.sum(P * dp, axis=1, keepdims=True)
            ds = (P * (dp - delta)).astype(BF16)
            dsnk = -(p_snk * inv) * delta
            for h in range(GROUP):
                dsink_ref[gi, h:h + 1, :] += _fsum(dsnk[h * CHUNK:(h + 1) * CHUNK])
            for pr, dq in enumerate(_unstack_heads(_nn(ds, Kall))):
                dq_ref[:, (2 * gi + pr) * LANES:(2 * gi + pr + 1) * LANES] = dq
            dK = _tn(ds, q4)
            dV = _tn(P.astype(BF16), do4)
            for j, (rk, rv) in enumerate([(dkp, dvp), (dkc_, dvc_), (dkn, dvn)]):
                rk[gi] = dK[j * CHUNK:(j + 1) * CHUNK].astype(BF16)
                rv[gi] = dV[j * CHUNK:(j + 1) * CHUNK].astype(BF16)
            dkctx[gi] += dK[3 * CHUNK:]
            dvctx[gi] += dV[3 * CHUNK:]

    q, kv, ctx = _att_specs(Lc, N, gps)
    band = pl.BlockSpec((GROUP * CHUNK, 3 * CHUNK + Lc), lambda g, n: (0, 0))
    blk = pl.BlockSpec((gps, CHUNK, LANES), lambda g, n: (g, n, 0))
    part = jax.ShapeDtypeStruct((nkv, L, LANES), BF16)
    cshape = jax.ShapeDtypeStruct((nkv, Lc, LANES), F32)
    return pl.pallas_call(
        body, name="att_bwd", grid=(nkv // gps, N),
        in_specs=[pl.BlockSpec(memory_space=pltpu.SMEM), band, q, kv(-1), kv(0), kv(1), kv(-1), kv(0), kv(1), ctx, ctx,
                  pl.BlockSpec((CHUNK, gps * 2 * LANES), lambda g, n: (n, blk0 // gps + g))],
        out_specs=[q, blk, blk, blk, blk, blk, blk, ctx, ctx,
                   pl.BlockSpec((gps, 8, LANES), lambda g, n: (g, 0, 0))],
        out_shape=[jax.ShapeDtypeStruct(Qa.shape, F32), part, part, part, part, part, part, cshape, cshape,
                   jax.ShapeDtypeStruct((nkv, 8, LANES), F32)],
        compiler_params=_params(2))(sink, _att_band(Lc), Qa, Kd, Kd, Kd, Vd, Vd, Vd, Kdc, Vdc, dY)


def _scale_rows(dx, gt, saved, name):
    M, D = dx.shape
    tm = _tile(M, 512, 8)

    def body(dx_ref, gt_ref, sv_ref, dz_ref, dgt_ref):
        @pl.when(pl.program_id(0) == 0)
        def _():
            dgt_ref[...] = jnp.zeros_like(dgt_ref)
        d = dx_ref[...]
        dz_ref[...] = (d * gt_ref[...]).astype(BF16)
        dgt_ref[...] += jnp.sum(d * sv_ref[...].astype(F32), axis=0, keepdims=True)

    row = pl.BlockSpec((tm, D), lambda i: (i, 0))
    vec = pl.BlockSpec((1, D), lambda i: (0, 0))
    return pl.pallas_call(
        body, name=name, grid=(M // tm,), in_specs=[row, vec, row], out_specs=[row, vec],
        out_shape=[jax.ShapeDtypeStruct((M, D), BF16), jax.ShapeDtypeStruct((1, D), F32)],
        compiler_params=_params(1))(dx, gt, saved)


def _bwd_proj(dz, w, G=None, U=None, name="bwd_proj"):
    M, D = dz.shape
    N = w.shape[0]
    swiglu = G is not None
    tm, tn = _tile(M, ROWS_PER_LATCH, 8), _tile(N, 512)

    def body(*refs):
        if swiglu:
            dz_ref, w_ref, G_ref, U_ref, dG_ref, dU_ref = refs
        else:
            dz_ref, w_ref, dA_ref = refs
        dA = _nt(dz_ref[...], w_ref[...])
        if swiglu:
            Gv = G_ref[...].astype(F32)
            Uv = U_ref[...].astype(F32)
            sg = _sigmoid(Gv)
            dU_ref[...] = (dA * Gv * sg).astype(BF16)
            dG_ref[...] = (dA * Uv * (sg * (1.0 + Gv * (1.0 - sg)))).astype(BF16)
        else:
            dA_ref[...] = dA.astype(BF16)

    row = pl.BlockSpec((tm, D), lambda i, j: (i, 0))
    tile = pl.BlockSpec((tm, tn), lambda i, j: (i, j))
    big = jax.ShapeDtypeStruct((M, N), BF16)
    in_specs = [row, pl.BlockSpec((tn, D), lambda i, j: (j, 0))]
    args = [dz, w]
    if swiglu:
        in_specs += [tile, tile]
        args += [G, U]
        out_specs, out_shape = [tile, tile], [big, big]
    else:
        out_specs, out_shape = tile, big
    return pl.pallas_call(
        body, name=name, grid=(M // tm, N // tn), in_specs=in_specs, out_specs=out_specs, out_shape=out_shape,
        compiler_params=_params(2))(*args)


def _tn_matmul(pairs, name):
    Ka, Nb = pairs[0][0].shape[1], pairs[0][1].shape[1]
    tk, tn = _tile(Ka, 2048), _tile(Nb, 2048)
    tls, nks = [], []
    for a, _ in pairs:
        tl = _tile(a.shape[0], 512, 8)
        tls.append(tl)
        nks.append(a.shape[0] // tl)
    starts = [int(s) for s in np.cumsum([0] + nks[:-1])]
    nk = int(sum(nks))

    def body(*refs):
        out_ref, acc = refs[-2], refs[-1]
        k = pl.program_id(2)

        @pl.when(k == 0)
        def _():
            acc[...] = jnp.zeros_like(acc)

        for idx in range(len(pairs)):
            a_ref, b_ref = refs[2 * idx], refs[2 * idx + 1]

            @pl.when(jnp.logical_and(k >= starts[idx], k < starts[idx] + nks[idx]))
            def _():
                acc[...] += _tn(a_ref[...], b_ref[...])

        @pl.when(k == nk - 1)
        def _():
            out_ref[...] = acc[...].astype(BF16)

    in_specs, args = [], []
    for idx, (a, b) in enumerate(pairs):
        s0, n_ = starts[idx], nks[idx]
        in_specs.append(pl.BlockSpec((tls[idx], tk), lambda i, j, k, s0=s0, n_=n_: (jnp.clip(k - s0, 0, n_ - 1), i)))
        in_specs.append(pl.BlockSpec((tls[idx], tn), lambda i, j, k, s0=s0, n_=n_: (jnp.clip(k - s0, 0, n_ - 1), j)))
        args += [a, b]
    return pl.pallas_call(
        body, name=name, grid=(Ka // tk, Nb // tn, nk), in_specs=in_specs,
        out_specs=pl.BlockSpec((tk, tn), lambda i, j, k: (i, j)),
        out_shape=jax.ShapeDtypeStruct((Ka, Nb), BF16),
        scratch_shapes=[pltpu.VMEM((tk, tn), F32)], compiler_params=_params(3))(*args)


def _bwd_norm_mod(pairs, x, dres, g, sh, sc, name):
    M, D = x.shape
    K = pairs[0][0].shape[1]
    tm, tk = _tile(M, 512, 8), _tile(K, 1152 if len(pairs) == 1 else 512)
    nk = K // tk
    npair = len(pairs)
    has_res = dres is not None

    def body(*refs):
        pr = refs[:2 * npair]
        rest = refs[2 * npair:]
        if has_res:
            x_ref, dres_ref, g_ref, sh_ref, sc_ref, dx_ref, st_ref, acc = rest
        else:
            x_ref, g_ref, sh_ref, sc_ref, dx_ref, st_ref, acc = rest
        del sh_ref
        i, k = pl.program_id(0), pl.program_id(1)

        @pl.when(jnp.logical_and(i == 0, k == 0))
        def _():
            st_ref[...] = jnp.zeros_like(st_ref)

        @pl.when(k == 0)
        def _():
            acc[...] = jnp.zeros_like(acc)

        t = _nt(pr[1][...], pr[0][...])
        for idx in range(1, npair):
            t = t + _nt(pr[2 * idx + 1][...], pr[2 * idx][...])
        acc[...] += t

        @pl.when(k == nk - 1)
        def _():
            xv = x_ref[...]
            gv = g_ref[...]
            dh = acc[...].T
            r = lax.rsqrt(jnp.mean(xv * xv, axis=-1, keepdims=True) + NORM_EPS)
            xh = xv * r
            st_ref[0:1, :] += jnp.sum(dh, axis=0, keepdims=True)
            st_ref[1:2, :] += jnp.sum(dh * (xh * gv), axis=0, keepdims=True)
            dn = dh * (1.0 + sc_ref[...])
            st_ref[2:3, :] += jnp.sum(dn * xh, axis=0, keepdims=True)
            dxh = dn * gv
            d = r * (dxh - xh * jnp.mean(dxh * xh, axis=-1, keepdims=True))
            if has_res:
                d = d + dres_ref[...]
            dx_ref[...] = d

    row = pl.BlockSpec((tm, D), lambda i, k: (i, 0))
    vec = pl.BlockSpec((1, D), lambda i, k: (0, 0))
    in_specs, args = [], []
    for dA, w in pairs:
        in_specs += [pl.BlockSpec((tm, tk), lambda i, k: (i, k)), pl.BlockSpec((D, tk), lambda i, k: (0, k))]
        args += [dA, w]
    in_specs += [row] + ([row] if has_res else []) + [vec, vec, vec]
    args += [x] + ([dres] if has_res else []) + [g, sh, sc]
    return pl.pallas_call(
        body, name=name, grid=(M // tm, nk), in_specs=in_specs,
        out_specs=[row, pl.BlockSpec((8, D), lambda i, k: (0, 0))],
        out_shape=[jax.ShapeDtypeStruct((M, D), F32), jax.ShapeDtypeStruct((8, D), F32)],
        scratch_shapes=[pltpu.VMEM((D, tm), F32)], compiler_params=_params(2))(*args)


def _local_step(x, ctx, tgt, mod, modc, norm_mix, norm_ffn, norm_final, lg, sink, w_in, rest_weights, on_grads):
    L, D = x.shape
    Lc = ctx.shape[0]
    d_proj = w_in.shape[1]
    npairs = RET_HEADS // 2
    nkv = ATT_KV_HEADS
    nkvp = nkv // 2
    o_rq = 0
    o_rk = o_rq + RET_HEADS * RET_DK // LANES
    o_rv = o_rk + RET_HEADS * RET_DK // LANES
    o_rg = o_rv + RET_HEADS * RET_DV // LANES
    o_aq = o_rg + RET_HEADS * RET_DV // LANES
    o_ak = o_aq + ATT_HEADS * ATT_DH // LANES
    o_av = o_ak + nkv * ATT_DH // LANES
    assert (o_av + nkv * ATT_DH // LANES) * LANES == d_proj
    assert o_rv % 2 == 0 and o_rg % 2 == 0 and (RET_HEADS * RET_DV) % (2 * LANES) == 0
    rv_blk, rg_blk = o_rv // 2, o_rg // 2
    d_ret = RET_HEADS * RET_DV
    d_mix = d_ret + ATT_HEADS * ATT_DH
    att_blk = d_ret // (2 * LANES)
    k_scale = RET_DK ** -0.5
    a_scale = ATT_DH ** -0.5

    T = _rope_tables(L)
    Tc = dict(C=jnp.ones((Lc, LANES), F32), S=jnp.zeros((Lc, LANES), F32))
    row = lambda m, i: m[i:i + 1]
    sh_m, sc_m, gt_m, sh_f, sc_f, gt_f = [row(mod, i) for i in range(6)]
    sh_mc, sc_mc = row(modc, 0), row(modc, 1)

    P, hx = _norm_mod_matmul(x, norm_mix, sh_m, sc_m, w_in, "in_proj")
    Pc, hc = _norm_mod_matmul(ctx, norm_mix, sh_mc, sc_mc, w_in, "in_proj_ctx")
    nq = RET_HEADS * RET_DK // LANES
    Qr = _rope_cols(P, o_rq, nq, T["Cr"], T["Sr"], T["Rr"], 1.0, True, "rope_rq")
    Kr = _rope_cols(P, o_rk, nq, T["Cr"], T["Sr"], T["Rr"], k_scale, True, "rope_rk")
    Krc = _rope_cols(Pc, o_rk, nq, Tc["C"], Tc["S"], T["Rr"], k_scale, False, "scale_rk_ctx")
    Qa = _rope_cols(P, o_aq, ATT_HEADS * ATT_DH // LANES, T["Ca"], T["Sa"], T["Ra"], a_scale, True, "rope_aq")
    Kd = _dup_heads(P, o_ak, nkvp, T["Ca"], T["Sa"], T["Ra"], T["D0"], T["D1"], True, "dup_ak")
    Vd = _dup_heads(P, o_av, nkvp, T["Ca"], T["Sa"], T["Ra"], T["D0"], T["D1"], False, "dup_av")
    Kdc = _dup_heads(Pc, o_ak, nkvp, Tc["C"], Tc["S"], T["Ra"], T["D0"], T["D1"], False, "dup_ak_ctx")
    Vdc = _dup_heads(Pc, o_av, nkvp, Tc["C"], Tc["S"], T["Ra"], T["D0"], T["D1"], False, "dup_av_ctx")

    SF = _ret_states_fwd(Kr, P, Krc, Pc, lg, rv_blk, npairs)
    Y, SB = _ret_out_fwd(Qr, Kr, P, Krc, Pc, SF, lg, rv_blk, rg_blk, npairs, d_mix)
    Y = _att_fwd(Qa, Kd, Vd, Kdc, Vdc, sink, Y, att_blk)

    w_out, w_gate, w_up, w_down = rest_weights(Y)
    x1, O1 = _proj_residual(Y, w_out, x, gt_m, "out_proj")
    G, U, A, h2 = _ffn_in(x1, norm_ffn, sh_f, sc_f, w_gate, w_up)
    x2, Fo = _proj_residual(A, w_down, x1, gt_f, "ffn_out")
    dx2, loss, d_norm_final, dz2, dgt_f = _final(x2, norm_final, tgt, gt_f, Fo)

    dG, dU = _bwd_proj(dz2, w_down, G, U, name="ffn_out_bwd")
    g_w_down = _tn_matmul([(A, dz2)], "grad_w_down")
    tok = on_grads(["w_down"], [g_w_down])
    dx1, st_f = _bwd_norm_mod([(dG, w_gate), (dU, w_up)], x1, dx2, norm_ffn + tok, sh_f, sc_f, "ffn_in_bwd")
    g_w_gate = _tn_matmul([(h2, dG)], "grad_w_gate")
    g_w_up = _tn_matmul([(h2, dU)], "grad_w_up")
    tok = on_grads(["w_gate", "w_up"], [g_w_gate, g_w_up])
    dz1, dgt_m = _scale_rows(dx1, gt_m + tok, O1, "mix_gate_bwd")
    dY = _bwd_proj(dz1, w_out, name="out_proj_bwd")
    g_w_out = _tn_matmul([(Y, dz1)], "grad_w_out")
    tok = on_grads(["w_out"], [g_w_out])

    dQa, dKp, dKs, dKn, dVp, dVs, dVn, dKdc, dVdc, dsink = _att_bwd(Qa, Kd, Vd, Kdc, Vdc, sink + tok, dY, att_blk)
    dQr, dKr, dVr, dP, dO, dKc, dVc, acc1 = _ret_bwd1(Qr, Kr, P, Krc, Pc, SF, SB, dY, lg, rv_blk, rg_blk, npairs, d_proj)
    dKr, dP, dKc, dVc, acc2 = _ret_bwd2(Qr, Kr, P, Krc, Pc, SB, dO, dKr, dVr, dP, dKc, dVc, lg, rv_blk, npairs)

    dP = _unrope_cols(dQr, dP, o_rq, nq, T["Cr"], T["Sr"], T["RrT"], 1.0, True, "unrope_rq")
    dP = _unrope_cols(dKr, dP, o_rk, nq, T["Cr"], T["Sr"], T["RrT"], k_scale, True, "unrope_rk")
    dP = _unrope_cols(dQa, dP, o_aq, ATT_HEADS * ATT_DH // LANES, T["Ca"], T["Sa"], T["RaT"], a_scale, True, "unrope_aq")
    dP = _fold_heads([(dKs, 0), (dKp, 1), (dKn, -1)], dP, o_ak, nkvp, T["Ca"], T["Sa"], T["RaT"], T["D0T"], T["D1T"],
                     True, "fold_ak")
    dP = _fold_heads([(dVs, 0), (dVp, 1), (dVn, -1)], dP, o_av, nkvp, T["Ca"], T["Sa"], T["RaT"], T["D0T"], T["D1T"],
                     False, "fold_av")
    dPc = jnp.zeros((Lc, d_proj), BF16)
    dPc = _unrope_cols(dKc, dPc, o_rk, nq, Tc["C"], Tc["S"], T["RrT"], k_scale, False, "ctx_rk_bwd")
    dPc = _unrope_cols(dVc, dPc, o_rv, RET_HEADS * RET_DV // LANES, Tc["C"], Tc["S"], T["RrT"], 1.0, False, "ctx_rv_bwd")
    dPc = _fold_heads([(dKdc.astype(BF16), 0)], dPc, o_ak, nkvp, Tc["C"], Tc["S"], T["RaT"], T["D0T"], T["D1T"],
                      False, "fold_ak_ctx")
    dPc = _fold_heads([(dVdc.astype(BF16), 0)], dPc, o_av, nkvp, Tc["C"], Tc["S"], T["RaT"], T["D0T"], T["D1T"],
                      False, "fold_av_ctx")

    dx, st_m = _bwd_norm_mod([(dP, w_in)], x, dx1, norm_mix, sh_m, sc_m, "in_proj_bwd")
    _, st_mc = _bwd_norm_mod([(dPc, w_in)], ctx, None, norm_mix, sh_mc, sc_mc, "in_proj_ctx_bwd")
    g_w_in = _tn_matmul([(hx, dP), (hc, dPc)], "grad_w_in")
    on_grads(["w_in"], [g_w_in])

    a1 = acc1[:, :, :, 0].reshape(RET_HEADS, ACC_ROWS)
    a2 = acc2[:, :, :, 0].reshape(RET_HEADS, ACC_ROWS)
    dlam = (a1[:, 0] + a1[:, 2] + a1[:, 3] + a1[:, 4]) * lg[0]
    dmu = (a1[:, 1] + a1[:, 5] + a2[:, 0] + a2[:, 1]) * lg[1]
    d_sink = dsink[:, :4, 0].reshape(1, ATT_HEADS)

    nh = RET_HEADS
    assert 2 * nh + ATT_HEADS <= LOSS_LANE
    small = _pack_rows(
        [(st_m, 0, 2, 0, 0), (dgt_m, 0, 1, 2, 0), (st_f, 0, 2, 3, 0), (dgt_f, 0, 1, 5, 0), (st_mc, 0, 2, 6, 0),
         (st_m[2:3] + st_mc[2:3], 0, 1, 12, 0), (st_f, 2, 1, 13, 0), (d_norm_final, 0, 1, 14, 0),
         (dlam.reshape(1, nh), 0, 1, 15, 0), (dmu.reshape(1, nh), 0, 1, 15, nh), (d_sink, 0, 1, 15, 2 * nh),
         (loss[:, 0:1], 0, 1, 15, LOSS_LANE)], 16, D, "pack_small")
    return dict(grad_x=dx, small=small)


def _my_pos():
    return lax.axis_index("x"), lax.axis_index("y"), lax.axis_index("c")


def _other_chips(x, y):
    return [(1 - x, y), (x, 1 - y), (1 - x, 1 - y)]


def _remote(src, dst, ssem, rsem, dev):
    return pltpu.make_async_remote_copy(src_ref=src, dst_ref=dst, send_sem=ssem, recv_sem=rsem,
                                        device_id=dev, device_id_type=MESH)


def _allgather8(v, name):
    R, Cc = v.shape

    def body(v_ref, out_ref, send_sems, recv_sems):
        x, y, c = _my_pos()
        me = 4 * x + 2 * y + c
        out_ref[pl.ds(me, 1)] = v_ref[...][None]
        peers = []
        for j in range(1, N_DEV):
            peers.append((1 - x if (j >> 2) & 1 else x, 1 - y if (j >> 1) & 1 else y, 1 - c if j & 1 else c))
        copies = []
        for j, peer in enumerate(peers):
            cp = _remote(v_ref, out_ref.at[me], send_sems.at[j], recv_sems.at[j], peer)
            cp.start()
            copies.append(cp)
        for j, peer in enumerate(peers):
            pid = 4 * peer[0] + 2 * peer[1] + peer[2]
            _remote(v_ref, out_ref.at[pid], send_sems.at[j], recv_sems.at[j], peer).wait_recv()
        for cp in copies:
            cp.wait_send()

    return pl.pallas_call(
        body, name=name, out_shape=jax.ShapeDtypeStruct((N_DEV, R, Cc), v.dtype),
        in_specs=[pl.BlockSpec(memory_space=pltpu.VMEM)], out_specs=pl.BlockSpec(memory_space=pltpu.VMEM),
        scratch_shapes=[pltpu.SemaphoreType.DMA((N_DEV - 1,)), pltpu.SemaphoreType.DMA((N_DEV - 1,))])(v)


def _region(ref, k, half, shard_shape, axis):
    r, cs = shard_shape
    hr = r // 2
    if axis == 1:
        return ref.at[pl.ds(pl.multiple_of(half * hr, 16), hr), pl.ds(pl.multiple_of(k * cs, LANES), cs)]
    return ref.at[pl.ds(pl.multiple_of(k * r + half * hr, 16), hr), :]


def _full_shape(shard_shape, axis):
    r, cs = shard_shape
    return (r, N_CHIPS * cs) if axis == 1 else (N_CHIPS * r, cs)


def _half_pieces(ref, half, shard_shape, axis):
    r, cs = shard_shape
    hr = r // 2
    if axis == 1:
        return [ref.at[pl.ds(pl.multiple_of(half * hr, 16), hr), :]]
    return [ref.at[pl.ds(pl.multiple_of(k * r + half * hr, 16), hr), :] for k in range(N_CHIPS)]


def _half_block_spec(shard_shape, axis, tr):
    r, cs = shard_shape
    hr = r // 2
    if axis == 1:
        return pl.BlockSpec((tr, cs), lambda k, i, c_ref: (c_ref[0] * (hr // tr) + i, k))
    return pl.BlockSpec((tr, cs), lambda k, i, c_ref: (k * (r // tr) + c_ref[0] * (hr // tr) + i, 0))


def _add_halves(g, recv, cvec, shard_shape, axis, name):
    r, cs = shard_shape
    hr = r // 2
    tr = _tile(hr, 256, 16)

    def body(c_ref, a_ref, b_ref, o_ref):
        del c_ref
        o_ref[0] = (a_ref[...].astype(F32) + b_ref[...].astype(F32)).astype(BF16)

    spec = _half_block_spec(shard_shape, axis, tr)
    return pl.pallas_call(
        body, name=name,
        grid_spec=pltpu.PrefetchScalarGridSpec(
            num_scalar_prefetch=1, grid=(N_CHIPS, hr // tr), in_specs=[spec, spec],
            out_specs=pl.BlockSpec((1, tr, cs), lambda k, i, c_ref: (k, i, 0))),
        out_shape=jax.ShapeDtypeStruct((N_CHIPS, hr, cs), BF16),
        compiler_params=_params(2, False))(cvec, g, recv)


def _sum_chips(sums, landed, kc, name):
    _, hr, cs = sums.shape
    tr = _tile(hr, 256, 16)

    def body(kc_ref, own_ref, a_ref, b_ref, c_ref, o_ref):
        del kc_ref
        o_ref[...] = (own_ref[0].astype(F32) + a_ref[0].astype(F32)) + (b_ref[0].astype(F32) + c_ref[0].astype(F32))

    slot = lambda j: pl.BlockSpec((1, tr, cs), lambda i, kc_ref: ((kc_ref[0] + j) % N_CHIPS, i, 0))
    return pl.pallas_call(
        body, name=name,
        grid_spec=pltpu.PrefetchScalarGridSpec(
            num_scalar_prefetch=1, grid=(hr // tr,), in_specs=[slot(0), slot(1), slot(2), slot(3)],
            out_specs=pl.BlockSpec((tr, cs), lambda i, kc_ref: (kc_ref[1] * (hr // tr) + i, 0))),
        out_shape=jax.ShapeDtypeStruct((2 * hr, cs), F32),
        compiler_params=_params(1, False))(kc, sums, landed, landed, landed)


def _exchange_halves(shards, name):
    nw = len(shards)

    def body(*refs):
        out_refs = refs[nw:2 * nw]
        send, recv = refs[2 * nw:]
        x, y, c = _my_pos()
        sib = (x, y, 1 - c)
        copies = []
        for w in range(nw):
            hr = shards[w].shape[0] // 2
            mine = out_refs[w].at[pl.ds(pl.multiple_of(c * hr, 8), hr), :]
            cp = _remote(mine, mine, send.at[w], recv.at[w], sib)
            cp.start()
            copies.append(cp)
        for w in range(nw):
            hr = shards[w].shape[0] // 2
            other = out_refs[w].at[pl.ds(pl.multiple_of((1 - c) * hr, 8), hr), :]
            _remote(other, other, send.at[w], recv.at[w], sib).wait_recv()
        for cp in copies:
            cp.wait_send()

    anyspec = pl.BlockSpec(memory_space=pl.ANY)
    return pl.pallas_call(
        body, name=name,
        out_shape=[jax.ShapeDtypeStruct(s.shape, F32) for s in shards],
        in_specs=[anyspec] * nw, out_specs=[anyspec] * nw,
        input_output_aliases={w: w for w in range(nw)},
        scratch_shapes=[pltpu.SemaphoreType.DMA((nw,)), pltpu.SemaphoreType.DMA((nw,))])(*shards)


def _cast_into_full(w, kc, axis, name):
    r, cs = w.shape
    tr = _tile(r, 256, 16)

    def body(kc_ref, w_ref, o_ref):
        del kc_ref
        o_ref[...] = w_ref[...].astype(BF16)

    if axis == 1:
        ospec = pl.BlockSpec((tr, cs), lambda i, kc_ref: (i, kc_ref[0]))
    else:
        ospec = pl.BlockSpec((tr, cs), lambda i, kc_ref: (kc_ref[0] * (r // tr) + i, 0))
    return pl.pallas_call(
        body, name=name,
        grid_spec=pltpu.PrefetchScalarGridSpec(
            num_scalar_prefetch=1, grid=(r // tr,), in_specs=[pl.BlockSpec((tr, cs), lambda i, kc_ref: (i, 0))],
            out_specs=ospec),
        out_shape=jax.ShapeDtypeStruct(_full_shape((r, cs), axis), BF16),
        compiler_params=_params(1, False))(kc, w)


def _adam_math(w, g, m, v):
    m2 = ADAM_B1 * m + (1.0 - ADAM_B1) * g
    v2 = ADAM_B2 * v + (1.0 - ADAM_B2) * (g * g)
    m_hat = m2 / (1.0 - ADAM_B1 ** ADAM_STEP)
    v_hat = v2 / (1.0 - ADAM_B2 ** ADAM_STEP)
    delta = -ADAM_LR * (m_hat / (jnp.sqrt(v_hat) + ADAM_EPS) + ADAM_WD * w)
    return delta, m2, v2


def _adam(w, g, m, v, name):
    r, cs = w.shape
    tr = _tile(r, 256, 8)

    def body(w_ref, g_ref, m_ref, v_ref, d_ref, m2_ref, v2_ref):
        d, m2, v2 = _adam_math(w_ref[...], g_ref[...], m_ref[...], v_ref[...])
        d_ref[...] = d
        m2_ref[...] = m2
        v2_ref[...] = v2

    spec = pl.BlockSpec((tr, cs), lambda i: (i, 0))
    shp = jax.ShapeDtypeStruct((r, cs), F32)
    return pl.pallas_call(body, name=name, grid=(r // tr,), in_specs=[spec] * 4, out_specs=[spec] * 3,
                          out_shape=[shp, shp, shp], compiler_params=_params(1, False))(w, g, m, v)


def _mod_rows(a16, w, b, name):
    D, n = w.shape
    tn = _tile(n, 512)

    def body(a_ref, w_ref, b_ref, o_ref):
        a = a_ref[...]
        o_ref[...] = _nn((a * _sigmoid(a)).astype(BF16), w_ref[...].astype(BF16)) + b_ref[...]

    return pl.pallas_call(
        body, name=name, grid=(n // tn,),
        in_specs=[pl.BlockSpec((16, D), lambda j: (0, 0)), pl.BlockSpec((D, tn), lambda j: (0, j)),
                  pl.BlockSpec((1, tn), lambda j: (0, j))],
        out_specs=pl.BlockSpec((16, tn), lambda j: (0, j)),
        out_shape=jax.ShapeDtypeStruct((16, n), F32), compiler_params=_params(1, False))(a16, w, b)


def _w_mod_update(a16, d16, w, m, v):
    D, n = w.shape
    tn = _tile(n, 256)

    def body(a_ref, d_ref, w_ref, m_ref, v_ref, g_ref, dl_ref, m2_ref, v2_ref, p_ref):
        @pl.when(pl.program_id(0) == 0)
        def _():
            p_ref[...] = jnp.zeros_like(p_ref)
        a = a_ref[...]
        db = d_ref[...].astype(BF16)
        wv = w_ref[...]
        g = _tn((a * _sigmoid(a)).astype(BF16), db)
        g_ref[...] = g
        d, m2, v2 = _adam_math(wv, g, m_ref[...], v_ref[...])
        dl_ref[...] = d
        m2_ref[...] = m2
        v2_ref[...] = v2
        p_ref[...] += _nt(db, wv.astype(BF16))

    wspec = pl.BlockSpec((D, tn), lambda j: (0, j))
    shp = jax.ShapeDtypeStruct((D, n), F32)
    return pl.pallas_call(
        body, name="w_mod_update", grid=(n // tn,),
        in_specs=[pl.BlockSpec((16, D), lambda j: (0, 0)), pl.BlockSpec((16, tn), lambda j: (0, j)), wspec, wspec, wspec],
        out_specs=[wspec, wspec, wspec, wspec, pl.BlockSpec((16, D), lambda j: (0, 0))],
        out_shape=[shp, shp, shp, shp, jax.ShapeDtypeStruct((16, D), F32)],
        compiler_params=_params(1))(a16, d16, w, m, v)


def _sum_devices(g8, name):
    _, R, Cc = g8.shape

    def body(g_ref, o_ref):
        t = g_ref[0]
        for d in range(1, N_DEV):
            t = t + g_ref[d]
        o_ref[...] = t

    return pl.pallas_call(body, name=name, out_shape=jax.ShapeDtypeStruct((R, Cc), F32))(g8)


def _c_ctx_grad(parts, c_ctx):
    D = c_ctx.shape[1]

    def body(p_ref, c_ref, o_ref):
        t = p_ref[0]
        for k in range(1, N_CHIPS):
            t = t + p_ref[2 * k]
        cv = c_ref[...]
        sg = _sigmoid(cv)
        o_ref[...] = t * (sg * (1.0 + cv * (1.0 - sg)))

    return pl.pallas_call(body, name="c_ctx_grad", out_shape=jax.ShapeDtypeStruct((1, D), F32))(parts, c_ctx)


def _pack_rows(items, nrows, width, name):
    arrays, plan = [], []
    for a, r0, nr, d0, c0 in items:
        for ai, b in enumerate(arrays):
            if b is a:
                break
        else:
            ai = len(arrays)
            arrays.append(a)
        plan.append((ai, r0, nr, d0, c0, a.shape[1]))

    def body(*refs):
        o_ref = refs[-1]
        o_ref[...] = jnp.zeros_like(o_ref)
        for ai, r0, nr, d0, c0, w in plan:
            o_ref[d0:d0 + nr, c0:c0 + w] = refs[ai][r0:r0 + nr, :]

    return pl.pallas_call(body, name=name, out_shape=jax.ShapeDtypeStruct((nrows, width), F32))(*arrays)


HBM_SPEC = pl.BlockSpec(memory_space=pltpu.HBM)
SEM_SPEC = pl.BlockSpec(memory_space=pltpu.SEMAPHORE)
SPLIT_PARAMS = pltpu.CompilerParams(has_side_effects=pltpu.SideEffectType.DATAFLOW_SIDE_EFFECTING)


def _in_hbm(a):
    return pltpu.with_memory_space_constraint(a, pltpu.HBM)


def _ag_chips_start(fulls, shapes, axes, after, name):
    nw = len(fulls)

    def body(*refs):
        in_refs, send, recv, token = refs[:nw], refs[nw + 1], refs[nw + 2], refs[-1]
        x, y, c = _my_pos()
        k0 = 2 * x + y
        for w in range(nw):
            own = _region(in_refs[w], k0, c, shapes[w], axes[w])
            for j, ch in enumerate(_other_chips(x, y)):
                _remote(own, own, send.at[3 * w + j], recv.at[3 * w + j], (ch[0], ch[1], c)).start()
        token[...] = jnp.zeros_like(token)

    return pl.pallas_call(
        body, name=name,
        out_shape=(pltpu.SemaphoreType.DMA((3 * nw,)), pltpu.SemaphoreType.DMA((3 * nw,)),
                   *[pltpu.HBM(f.shape, f.dtype) for f in fulls], jax.ShapeDtypeStruct((8, LANES), F32)),
        in_specs=[HBM_SPEC] * nw + [pl.BlockSpec(memory_space=pl.ANY)],
        out_specs=(SEM_SPEC, SEM_SPEC, *[HBM_SPEC] * nw, pl.BlockSpec(memory_space=pltpu.VMEM)),
        input_output_aliases={w: 2 + w for w in range(nw)},
        compiler_params=SPLIT_PARAMS)(*[_in_hbm(f) for f in fulls], after)


def _ag_chips_wait(send, recv, fulls, shapes, axes, after, name):
    nw = len(fulls)

    def body(*refs):
        in_refs, send_ref, recv_ref = refs[:nw], refs[nw], refs[nw + 1]
        x, y, c = _my_pos()
        k0 = 2 * x + y
        for w in range(nw):
            own = _region(in_refs[w], k0, c, shapes[w], axes[w])
            for j, ch in enumerate(_other_chips(x, y)):
                got = _region(in_refs[w], 2 * ch[0] + ch[1], c, shapes[w], axes[w])
                cp = _remote(own, got, send_ref.at[3 * w + j], recv_ref.at[3 * w + j], (ch[0], ch[1], c))
                cp.wait_send()
                cp.wait_recv()

    return pl.pallas_call(
        body, name=name,
        out_shape=tuple(pltpu.HBM(f.shape, f.dtype) for f in fulls),
        in_specs=[HBM_SPEC] * nw + [SEM_SPEC, SEM_SPEC, pl.BlockSpec(memory_space=pl.ANY)],
        out_specs=tuple([HBM_SPEC] * nw),
        input_output_aliases={w: w for w in range(nw)},
        compiler_params=SPLIT_PARAMS)(*fulls, send, recv, after)


def _ag_forward(fulls, shapes, axes, name):
    nw = len(fulls)

    def body(*refs):
        out_refs = refs[nw:2 * nw]
        send, recv = refs[2 * nw:]
        x, y, c = _my_pos()
        sib = (x, y, 1 - c)
        chips = _other_chips(x, y)
        copies = []
        for w in range(nw):
            for j, ch in enumerate(chips):
                got = _region(out_refs[w], 2 * ch[0] + ch[1], c, shapes[w], axes[w])
                cp = _remote(got, got, send.at[w, j], recv.at[w, j], sib)
                cp.start()
                copies.append(cp)
        for w in range(nw):
            for j, ch in enumerate(chips):
                got = _region(out_refs[w], 2 * ch[0] + ch[1], 1 - c, shapes[w], axes[w])
                _remote(got, got, send.at[w, j], recv.at[w, j], sib).wait_recv()
        for cp in copies:
            cp.wait_send()

    anyspec = pl.BlockSpec(memory_space=pl.ANY)
    return pl.pallas_call(
        body, name=name,
        out_shape=[jax.ShapeDtypeStruct(f.shape, BF16) for f in fulls],
        in_specs=[anyspec] * nw, out_specs=[anyspec] * nw,
        input_output_aliases={w: w for w in range(nw)},
        scratch_shapes=[pltpu.SemaphoreType.DMA((nw, 3)), pltpu.SemaphoreType.DMA((nw, 3))])(*fulls)


def _rs_sibling_start(grads, shapes, axes, name):
    nw = len(grads)
    npc = max(1 if a == 1 else N_CHIPS for a in axes)

    def body(*refs):
        g_refs, l_refs, send, recv, token = refs[:nw], refs[nw:2 * nw], refs[2 * nw], refs[2 * nw + 1], refs[-1]
        x, y, c = _my_pos()
        for w in range(nw):
            src = _half_pieces(g_refs[w], 1 - c, shapes[w], axes[w])
            dst = _half_pieces(l_refs[w], 1 - c, shapes[w], axes[w])
            for i, (s, d) in enumerate(zip(src, dst)):
                _remote(s, d, send.at[npc * w + i], recv.at[npc * w + i], (x, y, 1 - c)).start()
        token[...] = jnp.zeros_like(token)

    thru = [pltpu.HBM(g.shape, g.dtype) for g in grads]
    return pl.pallas_call(
        body, name=name,
        out_shape=(pltpu.SemaphoreType.DMA((npc * nw,)), pltpu.SemaphoreType.DMA((npc * nw,)), *thru, *thru,
                   jax.ShapeDtypeStruct((8, LANES), F32)),
        in_specs=[HBM_SPEC] * (2 * nw),
        out_specs=(SEM_SPEC, SEM_SPEC, *[HBM_SPEC] * (2 * nw), pl.BlockSpec(memory_space=pltpu.VMEM)),
        input_output_aliases={i: 2 + i for i in range(2 * nw)},
        compiler_params=SPLIT_PARAMS)(*[_in_hbm(g) for g in grads], *[_in_hbm(lax.empty(g.shape, g.dtype)) for g in grads])


def _rs_sibling_wait(send, recv, grads, lands, shapes, axes, after, name):
    nw = len(grads)
    npc = max(1 if a == 1 else N_CHIPS for a in axes)

    def body(*refs):
        g_refs, l_refs, send_ref, recv_ref = refs[:nw], refs[nw:2 * nw], refs[2 * nw], refs[2 * nw + 1]
        x, y, c = _my_pos()
        for w in range(nw):
            sent = _half_pieces(g_refs[w], 1 - c, shapes[w], axes[w])
            mine = _half_pieces(l_refs[w], c, shapes[w], axes[w])
            for i, (s, d) in enumerate(zip(sent, mine)):
                cp = _remote(s, d, send_ref.at[npc * w + i], recv_ref.at[npc * w + i], (x, y, 1 - c))
                cp.wait_send()
                cp.wait_recv()

    thru = tuple(pltpu.HBM(g.shape, g.dtype) for g in grads)
    return pl.pallas_call(
        body, name=name, out_shape=thru + thru,
        in_specs=[HBM_SPEC] * (2 * nw) + [SEM_SPEC, SEM_SPEC, pl.BlockSpec(memory_space=pl.ANY)],
        out_specs=tuple([HBM_SPEC] * (2 * nw)),
        input_output_aliases={i: i for i in range(2 * nw)},
        compiler_params=SPLIT_PARAMS)(*grads, *lands, send, recv, after)


def _rs_chips_start(sums, name):
    nw = len(sums)

    def body(*refs):
        s_refs, l_refs, send, recv, token = refs[:nw], refs[nw:2 * nw], refs[2 * nw], refs[2 * nw + 1], refs[-1]
        x, y, c = _my_pos()
        k0 = 2 * x + y
        for w in range(nw):
            for j, ch in enumerate(_other_chips(x, y)):
                _remote(s_refs[w].at[2 * ch[0] + ch[1]], l_refs[w].at[k0], send.at[3 * w + j], recv.at[3 * w + j],
                        (ch[0], ch[1], c)).start()
        token[...] = jnp.zeros_like(token)

    thru = [pltpu.HBM(s.shape, s.dtype) for s in sums]
    return pl.pallas_call(
        body, name=name,
        out_shape=(pltpu.SemaphoreType.DMA((3 * nw,)), pltpu.SemaphoreType.DMA((3 * nw,)), *thru, *thru,
                   jax.ShapeDtypeStruct((8, LANES), F32)),
        in_specs=[HBM_SPEC] * (2 * nw),
        out_specs=(SEM_SPEC, SEM_SPEC, *[HBM_SPEC] * (2 * nw), pl.BlockSpec(memory_space=pltpu.VMEM)),
        input_output_aliases={i: 2 + i for i in range(2 * nw)},
        compiler_params=SPLIT_PARAMS)(*[_in_hbm(s) for s in sums], *[_in_hbm(lax.empty(s.shape, s.dtype)) for s in sums])


def _rs_chips_wait(send, recv, sums, lands, after, name):
    nw = len(sums)

    def body(*refs):
        s_refs, l_refs, send_ref, recv_ref = refs[:nw], refs[nw:2 * nw], refs[2 * nw], refs[2 * nw + 1]
        x, y, c = _my_pos()
        for w in range(nw):
            for j, ch in enumerate(_other_chips(x, y)):
                kj = 2 * ch[0] + ch[1]
                cp = _remote(s_refs[w].at[kj], l_refs[w].at[kj], send_ref.at[3 * w + j], recv_ref.at[3 * w + j],
                             (ch[0], ch[1], c))
                cp.wait_send()
                cp.wait_recv()

    thru = tuple(pltpu.HBM(s.shape, s.dtype) for s in sums)
    return pl.pallas_call(
        body, name=name, out_shape=thru + thru,
        in_specs=[HBM_SPEC] * (2 * nw) + [SEM_SPEC, SEM_SPEC, pl.BlockSpec(memory_space=pl.ANY)],
        out_specs=tuple([HBM_SPEC] * (2 * nw)),
        input_output_aliases={i: i for i in range(2 * nw)},
        compiler_params=SPLIT_PARAMS)(*sums, *lands, send, recv, after)


LOSS_LANE = 64


def kernel(x, c, ctx, c_ctx, w_mod, b_mod, norm_mix, norm_ffn, w_in, ret_decay, attn_sink, w_out, w_gate, w_up, w_down, norm_final, loss_target, m_c_ctx, m_w_mod, m_b_mod, m_norm_mix, m_norm_ffn, m_w_in, m_ret_decay, m_attn_sink, m_w_out, m_w_gate, m_w_up, m_w_down, m_norm_final, v_c_ctx, v_w_mod, v_b_mod, v_norm_mix, v_norm_ffn, v_w_in, v_ret_decay, v_attn_sink, v_w_out, v_w_gate, v_w_up, v_w_down, v_norm_final):
    D = x.shape[-1]
    n3 = w_mod.shape[-1]
    xi, yi, ci = _my_pos()
    b = 4 * xi + 2 * yi + ci
    k0 = 2 * xi + yi
    cvec = jnp.reshape(ci, (1,)).astype(jnp.int32)
    kc = jnp.stack([k0, ci]).astype(jnp.int32)

    dense = [("w_in", w_in[0], 1), ("w_out", w_out[0], 0), ("w_gate", w_gate[0], 1), ("w_up", w_up[0], 1),
             ("w_down", w_down[0], 0)]
    axes = [a for _, _, a in dense]
    shapes = [w.shape for _, w, _ in dense]
    c_all = _allgather8(c, "gather_c").reshape(N_DEV, D)
    c_ctx2 = c_ctx.reshape(1, D)
    a16 = _pack_rows([(c_all, 0, N_DEV, 0, 0), (c_ctx2, 0, 1, N_DEV, 0)], 16, D, "pack_cond")
    b_cols = lax.dynamic_slice_in_dim(b_mod, k0 * n3, n3, axis=1)
    mod16 = _mod_rows(a16, w_mod[0], b_cols, "mod_rows")
    mod_all = _allgather8(mod16, "gather_mod")

    own_in = _cast_into_full(dense[0][1], kc, axes[0], "cast_w_in")
    agi = _ag_chips_start([own_in], shapes[:1], axes[:1], mod_all, "ag_in_start")
    own16 = [_cast_into_full(w, kc, a, "cast_" + n) for n, w, a in dense[1:]]
    (f_in,) = _ag_forward(list(_ag_chips_wait(agi[0], agi[1], [agi[2]], shapes[:1], axes[:1], own16[-1], "ag_in_wait")),
                          shapes[:1], axes[:1], "ag_in_forward")
    ag = _ag_chips_start(own16, shapes[1:], axes[1:], f_in, "ag_rest_start")
    ag_send, ag_recv, ag_thru, ag_tok = ag[0], ag[1], list(ag[2:-1]), ag[-1][0:1, 0:1]

    def rest_weights(after):
        landed_w = _ag_chips_wait(ag_send, ag_recv, ag_thru, shapes[1:], axes[1:], after, "ag_rest_wait")
        return _ag_forward(list(landed_w), shapes[1:], axes[1:], "ag_rest_forward")
    mine = jnp.stack([lax.dynamic_index_in_dim(mod_all, 2 * k + ci, 0, keepdims=False) for k in range(N_CHIPS)])
    mod = lax.dynamic_index_in_dim(mine, b, 1, keepdims=False).reshape(6, D)
    modc = mine[:, N_DEV].reshape(6, D)

    lg = -jnp.exp(ret_decay[0])

    index = {n: i for i, (n, _, _) in enumerate(dense)}
    pending, done = [], {}

    sib = []

    def finish_sibling(after):
        names, shp, axs, st = sib.pop()
        nw = len(names)
        res = _rs_sibling_wait(st[0], st[1], list(st[2:2 + nw]), list(st[2 + nw:2 + 2 * nw]), shp, axs, after,
                               "rs_sibling_wait_" + names[0])
        sums = [_add_halves(res[i], res[nw + i], cvec, s, a, "add_halves_" + n)
                for i, (s, a, n) in enumerate(zip(shp, axs, names))]
        ch = _rs_chips_start(sums, "rs_chips_start_" + names[0])
        pending.append((names, ch[0], ch[1], list(ch[2:2 + nw]), list(ch[2 + nw:2 + 2 * nw])))
        return ch[-1][0:1, 0:1]

    def on_grads(names, gs):
        ids = [index[n] for n in names]
        shp, axs = [shapes[i] for i in ids], [axes[i] for i in ids]
        tok = finish_sibling(gs[0]) if sib else 0.0
        st = _rs_sibling_start(gs, shp, axs, "rs_sibling_start_" + names[0])
        sib.append((names, shp, axs, st))
        return st[-1][0:1, 0:1] + tok

    out = _local_step(x[0], ctx[0], loss_target[0], mod, modc, norm_mix + ag_tok, norm_ffn, norm_final.reshape(1, D), lg,
                      attn_sink, f_in, rest_weights, on_grads)

    def finish(group, after):
        names, send, recv, sums, lands = group
        res = _rs_chips_wait(send, recv, sums, lands, after, "rs_chips_wait_" + names[0])
        return [_sum_chips(res[i], res[len(names) + i], kc, "sum_chips_" + n) for i, n in enumerate(names)]

    finish_sibling(out["grad_x"])
    assert pending[-1][0] == ["w_in"]
    rest_names = [n for g in pending[:-1] for n in g[0]]
    rest_halves = [h for g in pending[:-1] for h in finish(g, out["grad_x"])]
    g_rest = dict(zip(rest_names, _exchange_halves(rest_halves, "exchange_halves_rest")))

    nh = 2 * RET_HEADS
    small_all = _allgather8(out["small"], "gather_small")
    tot = _sum_devices(small_all, "sum_small")
    g_b_mod = (tot[0:6] + tot[6:12]).reshape(1, 6 * D)
    dmodc_tot = tot[6:12].reshape(1, 6 * D)
    dmod_rows = small_all[:, 0:6].reshape(N_DEV, 6 * D)
    d16 = _pack_rows([(dmod_rows, 0, N_DEV, 0, 0), (dmodc_tot, 0, 1, N_DEV, 0)], 16, 6 * D, "pack_dmod")
    d16 = lax.dynamic_slice_in_dim(d16, k0 * n3, n3, axis=1)
    g_w_mod, dl_w_mod, m2_w_mod, v2_w_mod, part = _w_mod_update(a16, d16, w_mod[0], m_w_mod[0], v_w_mod[0])
    part_all = _allgather8(part[N_DEV:N_DEV + 1], "gather_c_ctx")
    g_c_ctx = _c_ctx_grad(part_all, c_ctx2)
    loss = tot[15, LOSS_LANE]
    g_ret_decay = tot[15, :nh].reshape(1, 2, RET_HEADS)
    g_sink = tot[15, nh:nh + ATT_HEADS].reshape(1, ATT_HEADS)

    def pack(cc, bm, nm, nf, nfin, rd, sk, name):
        rd2 = rd.reshape(2, RET_HEADS)
        return _pack_rows([(bm.reshape(6, D), 0, 6, 0, 0), (cc.reshape(1, D), 0, 1, 6, 0), (nm.reshape(1, D), 0, 1, 7, 0),
                           (nf.reshape(1, D), 0, 1, 8, 0), (nfin.reshape(1, D), 0, 1, 9, 0),
                           (rd2, 0, 1, 10, 0), (rd2, 1, 1, 10, RET_HEADS), (sk.reshape(1, ATT_HEADS), 0, 1, 10, nh)],
                          16, D, name)

    w_s = pack(c_ctx, b_mod, norm_mix, norm_ffn, norm_final, ret_decay, attn_sink, "pack_w")
    g_s = _pack_rows([(g_b_mod.reshape(6, D), 0, 6, 0, 0), (g_c_ctx, 0, 1, 6, 0), (tot, 12, 3, 7, 0),
                      (tot[15:16, 0:nh + ATT_HEADS], 0, 1, 10, 0)], 16, D, "pack_g")
    m_s = pack(m_c_ctx, m_b_mod, m_norm_mix, m_norm_ffn, m_norm_final, m_ret_decay, m_attn_sink, "pack_m")
    v_s = pack(v_c_ctx, v_b_mod, v_norm_mix, v_norm_ffn, v_norm_final, v_ret_decay, v_attn_sink, "pack_v")
    small_upd = _adam(w_s, g_s, m_s, v_s, "adam_small")

    def unpack(t):
        return dict(b_mod=t[0:6].reshape(1, 6 * D), c_ctx=t[6], norm_mix=t[7:8], norm_ffn=t[8:9], norm_final=t[9],
                    ret_decay=t[10, :nh].reshape(1, 2, RET_HEADS), attn_sink=t[10, nh:nh + ATT_HEADS].reshape(1, ATT_HEADS))

    dense_w = dict(w_in=(w_in, m_w_in, v_w_in), w_out=(w_out, m_w_out, v_w_out), w_gate=(w_gate, m_w_gate, v_w_gate),
                   w_up=(w_up, m_w_up, v_w_up), w_down=(w_down, m_w_down, v_w_down))
    grads = dict(unpack(g_s), w_mod=g_w_mod[None])
    upd = [dict(unpack(t)) for t in small_upd]
    upd[0]["w_mod"], upd[1]["w_mod"], upd[2]["w_mod"] = dl_w_mod[None], m2_w_mod[None], v2_w_mod[None]
    def update(n, g):
        w_, m_, v_ = dense_w[n]
        res = _adam(w_[0], g, m_[0], v_[0], "adam_" + n)
        grads[n] = g[None]
        for u, r_ in zip(upd, res):
            u[n] = r_[None]
        return res[0]

    dep = small_upd[0][0:1, 0:1] + dl_w_mod[0:1, 0:1]
    for n in rest_names:
        dep = dep + update(n, g_rest[n])[0:1, 0:1]
    (g_in,) = _exchange_halves(finish(pending[-1], dep), "exchange_halves_in")
    update("w_in", g_in)

    order = ['c_ctx', 'w_mod', 'b_mod', 'norm_mix', 'norm_ffn', 'w_in', 'ret_decay', 'attn_sink', 'w_out', 'w_gate',
             'w_up', 'w_down', 'norm_final']
    outs = [loss, out["grad_x"][None]] + [grads[n] for n in order]
    for u in upd:
        outs += [u[n] for n in order]
    return tuple(outs)
```

```python
import functools
import numpy as np
import jax
import jax.numpy as jnp
from jax import lax
from jax.experimental import pallas as pl
from jax.experimental.pallas import tpu as pltpu

F32 = jnp.float32
BF16 = jnp.bfloat16

RET_HEADS = 8
RET_DK = 64
RET_DV = 128
CHUNK = 128
ATT_HEADS = 16
ATT_KV_HEADS = 4
ATT_DH = 64
GRID_W = 64
ROPE_BASE = 10000.0
NORM_EPS = 1e-6
ADAM_LR = 0.001
ADAM_B1 = 0.9
ADAM_B2 = 0.999
ADAM_EPS = 1e-08
ADAM_WD = 0.01
ADAM_STEP = 10
NEG = -1e30
LANES = 128
VMEM_LIMIT = 56 * 1024 * 1024
ROWS_PER_LATCH = 1024
MESH = pl.DeviceIdType.MESH
N_CHIPS = 4
N_DEV = 8


def _nn(a, b):
    return jnp.dot(a, b, preferred_element_type=F32)


def _nt(a, b):
    return lax.dot_general(a, b, (((1,), (1,)), ((), ())), preferred_element_type=F32)


def _tn(a, b):
    return lax.dot_general(a, b, (((0,), (0,)), ((), ())), preferred_element_type=F32)


def _tile(n, pref, unit=LANES):
    t = min(n, pref)
    t -= t % unit
    while t > unit and n % t:
        t -= unit
    if t <= 0 or n % t:
        return n
    return t


def _params(ndim, vmem=True):
    return pltpu.CompilerParams(dimension_semantics=("arbitrary",) * ndim,
                                vmem_limit_bytes=VMEM_LIMIT if vmem else None)


def _sigmoid(x):
    return 0.5 * jnp.tanh(0.5 * x) + 0.5


def _fsum(x):
    return jnp.sum(jnp.sum(x, axis=0, keepdims=True), axis=1, keepdims=True)


def _rope_tables(L):
    lane = np.arange(LANES)
    d = lane % 64
    inv_r = jnp.asarray(ROPE_BASE, F32) ** (-jnp.arange(32, dtype=F32) / 32)
    t = jnp.arange(L)
    ang_r = t.astype(F32)[:, None] * jnp.tile(inv_r, LANES // 32)[None, :]
    Rr = np.zeros((LANES, LANES), np.float32)
    for l in range(LANES):
        if d[l] < 32:
            Rr[l + 32, l] = -1.0
        else:
            Rr[l - 32, l] = 1.0
    inv_a = jnp.asarray(ROPE_BASE, F32) ** (-jnp.arange(16, dtype=F32) / 16)
    rows = (t // GRID_W).astype(F32)
    cols = (t % GRID_W).astype(F32)
    dd = d % 32
    pos = jnp.where(jnp.asarray(d < 32)[None, :], rows[:, None], cols[:, None])
    ang_a = pos * jnp.tile(inv_a, LANES // 16)[None, :]
    Ra = np.zeros((LANES, LANES), np.float32)
    for l in range(LANES):
        if dd[l] < 16:
            Ra[l + 16, l] = -1.0
        else:
            Ra[l - 16, l] = 1.0
    D0 = np.zeros((LANES, LANES), np.float32)
    D1 = np.zeros((LANES, LANES), np.float32)
    for l in range(LANES):
        D0[l % 64, l] = 1.0
        D1[64 + l % 64, l] = 1.0
    return dict(
        Cr=jnp.cos(ang_r), Sr=jnp.sin(ang_r), Rr=jnp.asarray(Rr, BF16), RrT=jnp.asarray(Rr.T, BF16),
        Ca=jnp.cos(ang_a), Sa=jnp.sin(ang_a), Ra=jnp.asarray(Ra, BF16), RaT=jnp.asarray(Ra.T, BF16),
        D0=jnp.asarray(D0, BF16), D1=jnp.asarray(D1, BF16),
        D0T=jnp.asarray(D0.T, BF16), D1T=jnp.asarray(D1.T, BF16))


def _norm_mod(xf, g, sh, sc):
    r = lax.rsqrt(jnp.mean(xf * xf, axis=-1, keepdims=True) + NORM_EPS)
    return (xf * r * g) * (1.0 + sc) + sh


def _norm_mod_matmul(x, g, sh, sc, w, name):
    M, D = x.shape
    N = w.shape[1]
    tm, tn = _tile(M, ROWS_PER_LATCH, 8), _tile(N, 768)

    def body(x_ref, g_ref, sh_ref, sc_ref, w_ref, p_ref, h_ref, hs):
        @pl.when(pl.program_id(1) == 0)
        def _():
            hb = _norm_mod(x_ref[...], g_ref[...], sh_ref[...], sc_ref[...]).astype(BF16)
            hs[...] = hb
            h_ref[...] = hb
        p_ref[...] = _nn(hs[...], w_ref[...]).astype(BF16)

    vec = pl.BlockSpec((1, D), lambda i, j: (0, 0))
    return pl.pallas_call(
        body, name=name, grid=(M // tm, N // tn),
        in_specs=[pl.BlockSpec((tm, D), lambda i, j: (i, 0)), vec, vec, vec,
                  pl.BlockSpec((D, tn), lambda i, j: (0, j))],
        out_specs=[pl.BlockSpec((tm, tn), lambda i, j: (i, j)), pl.BlockSpec((tm, D), lambda i, j: (i, 0))],
        out_shape=[jax.ShapeDtypeStruct((M, N), BF16), jax.ShapeDtypeStruct((M, D), BF16)],
        scratch_shapes=[pltpu.VMEM((tm, D), BF16)],
        compiler_params=_params(2))(x, g, sh, sc, w)


def _proj_residual(a, w, xres, gt, name):
    M, K = a.shape
    N = w.shape[1]
    tm, tn = _tile(M, ROWS_PER_LATCH, 8), _tile(N, 1024 if K <= 2048 else 512)

    def body(a_ref, w_ref, x_ref, gt_ref, xo_ref, o_ref):
        o = _nn(a_ref[...], w_ref[...])
        o_ref[...] = o.astype(BF16)
        xo_ref[...] = x_ref[...] + gt_ref[...] * o

    return pl.pallas_call(
        body, name=name, grid=(M // tm, N // tn),
        in_specs=[pl.BlockSpec((tm, K), lambda i, j: (i, 0)), pl.BlockSpec((K, tn), lambda i, j: (0, j)),
                  pl.BlockSpec((tm, tn), lambda i, j: (i, j)), pl.BlockSpec((1, tn), lambda i, j: (0, j))],
        out_specs=[pl.BlockSpec((tm, tn), lambda i, j: (i, j)), pl.BlockSpec((tm, tn), lambda i, j: (i, j))],
        out_shape=[jax.ShapeDtypeStruct((M, N), F32), jax.ShapeDtypeStruct((M, N), BF16)],
        compiler_params=_params(2))(a, w, xres, gt)


def _ffn_in(x1, g, sh, sc, wg, wu):
    M, D = x1.shape
    N = wg.shape[1]
    tm, tn = _tile(M, ROWS_PER_LATCH, 8), _tile(N, 512)

    def body(x_ref, g_ref, sh_ref, sc_ref, wg_ref, wu_ref, G_ref, U_ref, A_ref, h_ref, hs):
        @pl.when(pl.program_id(1) == 0)
        def _():
            hb = _norm_mod(x_ref[...], g_ref[...], sh_ref[...], sc_ref[...]).astype(BF16)
            hs[...] = hb
            h_ref[...] = hb
        G = _nn(hs[...], wg_ref[...])
        U = _nn(hs[...], wu_ref[...])
        G_ref[...] = G.astype(BF16)
        U_ref[...] = U.astype(BF16)
        A_ref[...] = (G * _sigmoid(G) * U).astype(BF16)

    vec = pl.BlockSpec((1, D), lambda i, j: (0, 0))
    wspec = pl.BlockSpec((D, tn), lambda i, j: (0, j))
    ospec = pl.BlockSpec((tm, tn), lambda i, j: (i, j))
    big = jax.ShapeDtypeStruct((M, N), BF16)
    return pl.pallas_call(
        body, name="ffn_in", grid=(M // tm, N // tn),
        in_specs=[pl.BlockSpec((tm, D), lambda i, j: (i, 0)), vec, vec, vec, wspec, wspec],
        out_specs=[ospec, ospec, ospec, pl.BlockSpec((tm, D), lambda i, j: (i, 0))],
        out_shape=[big, big, big, jax.ShapeDtypeStruct((M, D), BF16)],
        scratch_shapes=[pltpu.VMEM((tm, D), BF16)],
        compiler_params=_params(2))(x1, g, sh, sc, wg, wu)


def _final(x2, gn, tgt, gt, saved):
    M, D = x2.shape
    tm = _tile(M, 256, 8)

    def body(x_ref, g_ref, t_ref, gt_ref, sv_ref, dx_ref, loss_ref, dg_ref, dz_ref, dgt_ref):
        @pl.when(pl.program_id(0) == 0)
        def _():
            loss_ref[...] = jnp.zeros_like(loss_ref)
            dg_ref[...] = jnp.zeros_like(dg_ref)
            dgt_ref[...] = jnp.zeros_like(dgt_ref)
        x = x_ref[...]
        g = g_ref[...]
        r = lax.rsqrt(jnp.mean(x * x, axis=-1, keepdims=True) + NORM_EPS)
        xh = x * r
        e = xh * g - t_ref[...]
        loss_ref[...] += (0.5 / D) * _fsum(e * e)
        dy = e * (1.0 / D)
        dg_ref[...] += jnp.sum(dy * xh, axis=0, keepdims=True)
        dxh = dy * g
        d = r * (dxh - xh * jnp.mean(dxh * xh, axis=-1, keepdims=True))
        dx_ref[...] = d
        dz_ref[...] = (d * gt_ref[...]).astype(BF16)
        dgt_ref[...] += jnp.sum(d * sv_ref[...].astype(F32), axis=0, keepdims=True)

    row = pl.BlockSpec((tm, D), lambda i: (i, 0))
    vec = pl.BlockSpec((1, D), lambda i: (0, 0))
    return pl.pallas_call(
        body, name="final_loss", grid=(M // tm,),
        in_specs=[row, vec, row, vec, row],
        out_specs=[row, pl.BlockSpec((1, LANES), lambda i: (0, 0)), vec, row, vec],
        out_shape=[jax.ShapeDtypeStruct((M, D), F32), jax.ShapeDtypeStruct((1, LANES), F32),
                   jax.ShapeDtypeStruct((1, D), F32), jax.ShapeDtypeStruct((M, D), BF16),
                   jax.ShapeDtypeStruct((1, D), F32)],
        compiler_params=_params(1))(x2, gn, tgt, gt, saved)


def _col_group(blk0, nblk):
    return int(np.gcd(blk0, nblk)) if blk0 else nblk


def _rope_cols(src, blk0, nblk, Ct, St, R, scale, rope, name):
    M = src.shape[0]
    tm = _tile(M, 512, 8)
    wb = _col_group(blk0, nblk)

    def body(x_ref, c_ref, s_ref, r_ref, o_ref):
        for j in range(wb):
            cols = slice(j * LANES, (j + 1) * LANES)
            x = x_ref[:, cols]
            xf = x.astype(F32)
            if rope:
                xf = xf * c_ref[...] + _nn(x.astype(BF16), r_ref[...]) * s_ref[...]
            o_ref[:, cols] = (xf * scale).astype(BF16)

    tab = pl.BlockSpec((tm, LANES), lambda i, j: (i, 0))
    return pl.pallas_call(
        body, name=name, grid=(M // tm, nblk // wb),
        in_specs=[pl.BlockSpec((tm, wb * LANES), lambda i, j: (i, blk0 // wb + j)), tab, tab,
                  pl.BlockSpec((LANES, LANES), lambda i, j: (0, 0))],
        out_specs=pl.BlockSpec((tm, wb * LANES), lambda i, j: (i, j)),
        out_shape=jax.ShapeDtypeStruct((M, nblk * LANES), BF16),
        compiler_params=_params(2, False))(src, Ct, St, R)


def _dup_heads(src, blk0, npair, Ct, St, R, D0, D1, rope, name):
    M = src.shape[0]
    tm = _tile(M, 512, 8)

    def body(x_ref, c_ref, s_ref, r_ref, d0_ref, d1_ref, o_ref):
        x = x_ref[...]
        if rope:
            x = (x.astype(F32) * c_ref[...] + _nn(x, r_ref[...]) * s_ref[...]).astype(BF16)
        o_ref[0] = _nn(x, d0_ref[...]).astype(BF16)
        o_ref[1] = _nn(x, d1_ref[...]).astype(BF16)

    tab = pl.BlockSpec((tm, LANES), lambda i, p: (i, 0))
    mat = pl.BlockSpec((LANES, LANES), lambda i, p: (0, 0))
    return pl.pallas_call(
        body, name=name, grid=(M // tm, npair),
        in_specs=[pl.BlockSpec((tm, LANES), lambda i, p: (i, blk0 + p)), tab, tab, mat, mat, mat],
        out_specs=pl.BlockSpec((2, tm, LANES), lambda i, p: (p, i, 0)),
        out_shape=jax.ShapeDtypeStruct((2 * npair, M, LANES), BF16),
        compiler_params=_params(2, False))(src, Ct, St, R, D0, D1)


def _unrope_cols(dsrc, dst, blk0, nblk, Ct, St, RT, scale, rope, name):
    M = dsrc.shape[0]
    tm = _tile(M, 512, 8)
    wb = _col_group(blk0, nblk)

    def body(x_ref, c_ref, s_ref, r_ref, dst_ref, o_ref):
        del dst_ref
        for j in range(wb):
            cols = slice(j * LANES, (j + 1) * LANES)
            xf = x_ref[:, cols].astype(F32)
            if rope:
                xf = xf * c_ref[...] + _nn((xf * s_ref[...]).astype(BF16), r_ref[...])
            o_ref[:, cols] = (xf * scale).astype(BF16)

    tab = pl.BlockSpec((tm, LANES), lambda i, j: (i, 0))
    return pl.pallas_call(
        body, name=name, grid=(M // tm, nblk // wb),
        in_specs=[pl.BlockSpec((tm, wb * LANES), lambda i, j: (i, j)), tab, tab,
                  pl.BlockSpec((LANES, LANES), lambda i, j: (0, 0)),
                  pl.BlockSpec(memory_space=pl.ANY)],
        out_specs=pl.BlockSpec((tm, wb * LANES), lambda i, j: (i, blk0 // wb + j)),
        out_shape=jax.ShapeDtypeStruct(dst.shape, dst.dtype),
        input_output_aliases={4: 0},
        compiler_params=_params(2, False))(dsrc, Ct, St, RT, dst)


def _fold_heads(parts, dst, blk0, npair, Ct, St, RT, D0T, D1T, rope, name):
    M = parts[0][0].shape[1]
    nb = M // CHUNK
    R = _tile(M, 1024, CHUNK)
    rb = R // CHUNK
    nrefs = sum(1 if s == 0 else 2 for _, s in parts)

    def body(*refs):
        part_refs = list(refs[:nrefs])
        c_ref, s_ref, r_ref, d0_ref, d1_ref, dst_ref, o_ref = refs[nrefs:]
        del dst_ref
        i = pl.program_id(0)
        tot = [jnp.zeros((R, LANES), F32), jnp.zeros((R, LANES), F32)]
        for _, shift in parts:
            main = part_refs.pop(0)
            if shift == 0:
                for e in range(2):
                    tot[e] = tot[e] + main[e].astype(F32)
                continue
            edge = part_refs.pop(0)
            ok = (i + 1) * rb <= nb - 1 if shift > 0 else i > 0
            for e in range(2):
                ed = jnp.where(ok, edge[e].astype(F32), 0.0)
                if rb == 1:
                    tot[e] = tot[e] + ed
                elif shift > 0:
                    tot[e] = tot[e] + jnp.concatenate([main[e, CHUNK:, :].astype(F32), ed], axis=0)
                else:
                    tot[e] = tot[e] + jnp.concatenate([ed, main[e, :R - CHUNK, :].astype(F32)], axis=0)
        f = _nn(tot[0].astype(BF16), d0_ref[...]) + _nn(tot[1].astype(BF16), d1_ref[...])
        if rope:
            f = f * c_ref[...] + _nn((f * s_ref[...]).astype(BF16), r_ref[...])
        o_ref[...] = f.astype(BF16)

    in_specs, args = [], []
    for a, shift in parts:
        assert shift in (-1, 0, 1)
        in_specs.append(pl.BlockSpec((2, R, LANES), lambda i, p: (p, i, 0)))
        args.append(a)
        if shift > 0:
            in_specs.append(pl.BlockSpec((2, CHUNK, LANES), lambda i, p: (p, jnp.minimum((i + 1) * rb, nb - 1), 0)))
            args.append(a)
        elif shift < 0:
            in_specs.append(pl.BlockSpec((2, CHUNK, LANES), lambda i, p: (p, jnp.maximum(i * rb - 1, 0), 0)))
            args.append(a)
    tab = pl.BlockSpec((R, LANES), lambda i, p: (i, 0))
    mat = pl.BlockSpec((LANES, LANES), lambda i, p: (0, 0))
    return pl.pallas_call(
        body, name=name, grid=(M // R, npair),
        in_specs=in_specs + [tab, tab, mat, mat, mat, pl.BlockSpec(memory_space=pl.ANY)],
        out_specs=pl.BlockSpec((R, LANES), lambda i, p: (i, blk0 + p)),
        out_shape=jax.ShapeDtypeStruct(dst.shape, dst.dtype),
        input_output_aliases={nrefs + 5: 0},
        compiler_params=_params(2, False))(*args, Ct, St, RT, D0T, D1T, dst)


def _head_masks():
    lane = lax.broadcasted_iota(jnp.int32, (1, LANES), 1)
    return [lane < 64, lane >= 64]


def _decay_vecs(lam, mu):
    i = lax.broadcasted_iota(jnp.int32, (CHUNK, 1), 0).astype(F32)
    return dict(qf=jnp.exp(lam * (i + 1.0)), kf=jnp.exp(lam * (CHUNK - 1.0 - i)),
                qb=jnp.exp(mu * (CHUNK - i)), kb=jnp.exp(mu * i),
                gf=jnp.exp(lam * float(CHUNK)), gb=jnp.exp(mu * float(CHUNK)), i=i)


def _decay_mask(lam, mu):
    r = lax.broadcasted_iota(jnp.int32, (CHUNK, CHUNK), 0)
    c = lax.broadcasted_iota(jnp.int32, (CHUNK, CHUNK), 1)
    rel = (r - c).astype(F32)
    low = rel >= 0.0
    mf = jnp.exp(lam * jnp.maximum(rel, 0.0))
    mb = jnp.exp(mu * jnp.maximum(-rel, 0.0))
    return jnp.where(low, mf, mb), rel, low


def _lam_of(lg_ref, row, idx):
    return jnp.full((1, 1), lg_ref[row, idx], F32)


def _group_index(pair_blk, npairs):
    assert pair_blk % npairs == 0
    return pair_blk // npairs


def _ret_states_fwd(Kr, P, Krc, Pc, lg, rv_blk, npairs):
    L = Kr.shape[0]
    Lc = Krc.shape[0]
    N, ncc = L // CHUNK, Lc // CHUNK
    rv_grp = _group_index(rv_blk, npairs)

    heads = [(p, h) for p in range(npairs) for h in range(2)]
    kcols = lambda p: slice(p * LANES, (p + 1) * LANES)
    vcols = lambda p, h: slice((2 * p + h) * LANES, (2 * p + h + 1) * LANES)

    def body(lg_ref, k_ref, v_ref, kc_ref, vc_ref, sf_ref, S):
        n = pl.program_id(0)
        masks = _head_masks()

        @pl.when(n == 0)
        def _():
            for p, h in heads:
                lam = _lam_of(lg_ref, 0, 2 * p + h)
                dv = _decay_vecs(lam, lam)
                s = jnp.zeros((LANES, LANES), F32)
                for cc in range(ncc):
                    rows = slice(cc * CHUNK, (cc + 1) * CHUNK)
                    kw = jnp.where(masks[h], kc_ref[rows, kcols(p)].astype(F32) * dv["kf"], 0.0).astype(BF16)
                    s = dv["gf"] * s + _tn(kw, vc_ref[rows, vcols(p, h)])
                S[p, h] = s

        for p, h in heads:
            lam = _lam_of(lg_ref, 0, 2 * p + h)
            dv = _decay_vecs(lam, lam)
            s = S[p, h]
            sf_ref[p, 0, h] = s.astype(BF16)
            kw = jnp.where(masks[h], k_ref[:, kcols(p)].astype(F32) * dv["kf"], 0.0).astype(BF16)
            S[p, h] = dv["gf"] * s + _tn(kw, v_ref[:, vcols(p, h)])

    wq, wv = npairs * LANES, npairs * 2 * LANES
    return pl.pallas_call(
        body, name="ret_states_fwd", grid=(N,),
        in_specs=[pl.BlockSpec(memory_space=pltpu.SMEM),
                  pl.BlockSpec((CHUNK, wq), lambda n: (n, 0)),
                  pl.BlockSpec((CHUNK, wv), lambda n: (n, rv_grp)),
                  pl.BlockSpec((Lc, wq), lambda n: (0, 0)),
                  pl.BlockSpec((Lc, wv), lambda n: (0, rv_grp))],
        out_specs=pl.BlockSpec((npairs, 1, 2, LANES, LANES), lambda n: (0, n, 0, 0, 0)),
        out_shape=jax.ShapeDtypeStruct((npairs, N, 2, LANES, LANES), BF16),
        scratch_shapes=[pltpu.VMEM((npairs, 2, LANES, LANES), F32)],
        compiler_params=_params(1, False))(lg, Kr, P, Krc, Pc)


def _ret_chunk_fwd(q, k, v, sf, sb, hm, lam, mu, Mk):
    dv = _decay_vecs(lam, mu)
    qm = jnp.where(hm, q, jnp.zeros_like(q))
    qmf = qm.astype(F32)
    A = _nt(qm, k)
    Am = A * Mk
    Amb = Am.astype(BF16)
    Qf = (qmf * dv["qf"]).astype(BF16)
    Qb = (qmf * dv["qb"]).astype(BF16)
    O = _nn(Amb, v) + _nn(Qf, sf) + _nn(Qb, sb)
    return dict(dv=dv, Mk=Mk, qm=qm, Am=Am, Amb=Amb, Qf=Qf, Qb=Qb, O=O)


def _ret_out_fwd(Qr, Kr, P, Krc, Pc, SF, lg, rv_blk, rg_blk, npairs, d_mix):
    L = Qr.shape[0]
    Lc = Krc.shape[0]
    N, ncc = L // CHUNK, Lc // CHUNK

    rv_grp, rg_grp = _group_index(rv_blk, npairs), _group_index(rg_blk, npairs)
    heads = [(p, h) for p in range(npairs) for h in range(2)]
    kcols = lambda p: slice(p * LANES, (p + 1) * LANES)
    vcols = lambda p, h: slice((2 * p + h) * LANES, (2 * p + h + 1) * LANES)

    def body(lg_ref, q_ref, k_ref, v_ref, g_ref, sf_ref, kc_ref, vc_ref, y_ref, sb_ref, S, Mks):
        n = pl.program_id(0)
        masks = _head_masks()

        @pl.when(n == 0)
        def _():
            for p, h in heads:
                mu = _lam_of(lg_ref, 1, 2 * p + h)
                Mks[p, h] = _decay_mask(_lam_of(lg_ref, 0, 2 * p + h), mu)[0]
                dvb = _decay_vecs(mu, mu)
                s = jnp.zeros((LANES, LANES), F32)
                for cc in reversed(range(ncc)):
                    rows = slice(cc * CHUNK, (cc + 1) * CHUNK)
                    kw = jnp.where(masks[h], kc_ref[rows, kcols(p)].astype(F32) * dvb["kb"], 0.0).astype(BF16)
                    s = dvb["gb"] * s + _tn(kw, vc_ref[rows, vcols(p, h)])
                S[p, h] = s

        for p, h in heads:
            lam = _lam_of(lg_ref, 0, 2 * p + h)
            mu = _lam_of(lg_ref, 1, 2 * p + h)
            hm = masks[h]
            dvb = _decay_vecs(lam, mu)
            s = S[p, h]
            sbb = s.astype(BF16)
            sb_ref[p, 0, h] = sbb
            k = k_ref[:, kcols(p)]
            v = v_ref[:, vcols(p, h)]
            f = _ret_chunk_fwd(q_ref[:, kcols(p)], k, v, sf_ref[p, 0, h], sbb, hm, lam, mu, Mks[p, h])
            O = f["O"]
            r = lax.rsqrt(jnp.mean(O * O, axis=-1, keepdims=True) + NORM_EPS)
            g = g_ref[:, vcols(p, h)].astype(F32)
            y_ref[:, vcols(p, h)] = (O * r * (g * _sigmoid(g))).astype(BF16)
            kw = jnp.where(hm, k.astype(F32) * dvb["kb"], 0.0).astype(BF16)
            S[p, h] = dvb["gb"] * s + _tn(kw, v)

    rev = lambda n: N - 1 - n
    wq, wv = npairs * LANES, npairs * 2 * LANES
    st = pl.BlockSpec((npairs, 1, 2, LANES, LANES), lambda n: (0, rev(n), 0, 0, 0))
    return pl.pallas_call(
        body, name="ret_out_fwd", grid=(N,),
        in_specs=[pl.BlockSpec(memory_space=pltpu.SMEM),
                  pl.BlockSpec((CHUNK, wq), lambda n: (rev(n), 0)),
                  pl.BlockSpec((CHUNK, wq), lambda n: (rev(n), 0)),
                  pl.BlockSpec((CHUNK, wv), lambda n: (rev(n), rv_grp)),
                  pl.BlockSpec((CHUNK, wv), lambda n: (rev(n), rg_grp)),
                  st,
                  pl.BlockSpec((Lc, wq), lambda n: (0, 0)),
                  pl.BlockSpec((Lc, wv), lambda n: (0, rv_grp))],
        out_specs=[pl.BlockSpec((CHUNK, wv), lambda n: (rev(n), 0)), st],
        out_shape=[jax.ShapeDtypeStruct((L, d_mix), BF16),
                   jax.ShapeDtypeStruct((npairs, N, 2, LANES, LANES), BF16)],
        scratch_shapes=[pltpu.VMEM((npairs, 2, LANES, LANES), F32), pltpu.VMEM((npairs, 2, CHUNK, CHUNK), F32)],
        compiler_params=_params(1))(lg, Qr, Kr, P, P, SF, Krc, Pc)


ACC_ROWS = 8


def _ret_bwd1(Qr, Kr, P, Krc, Pc, SF, SB, dY, lg, rv_blk, rg_blk, npairs, d_proj):
    L = Qr.shape[0]
    Lc = Krc.shape[0]
    N, ncc = L // CHUNK, Lc // CHUNK
    rv_grp, rg_grp = _group_index(rv_blk, npairs), _group_index(rg_blk, npairs)
    heads = [(p, h) for p in range(npairs) for h in range(2)]
    kcols = lambda p: slice(p * LANES, (p + 1) * LANES)
    vcols = lambda p, h: slice((2 * p + h) * LANES, (2 * p + h + 1) * LANES)

    def body(lg_ref, q_ref, k_ref, v_ref, g_ref, sf_ref, sb_ref, dy_ref, kc_ref, vc_ref,
             dq_ref, dk_ref, dv_ref, dg_ref, do_ref, dkc_ref, dvc_ref, acc_ref, dS, T, Mks):
        n = pl.program_id(0)
        masks = _head_masks()

        @pl.when(n == 0)
        def _():
            dS[...] = jnp.zeros_like(dS)
            T[...] = jnp.zeros_like(T)
            acc_ref[...] = jnp.zeros_like(acc_ref)
            for p, h in heads:
                Mks[p, h] = _decay_mask(_lam_of(lg_ref, 0, 2 * p + h), _lam_of(lg_ref, 1, 2 * p + h))[0]

        def head_main(p, h):
            lam = _lam_of(lg_ref, 0, 2 * p + h)
            mu = _lam_of(lg_ref, 1, 2 * p + h)
            hm = masks[h]
            hs = vcols(p, h)
            v = v_ref[:, hs]
            k = k_ref[:, kcols(p)]
            sf = sf_ref[p, 0, h]
            sb = sb_ref[p, 0, h]
            f = _ret_chunk_fwd(q_ref[:, kcols(p)], k, v, sf, sb, hm, lam, mu, Mks[p, h])
            dv_, O = f["dv"], f["O"]
            r = lax.rsqrt(jnp.mean(O * O, axis=-1, keepdims=True) + NORM_EPS)
            on = O * r
            g = g_ref[:, hs].astype(F32)
            sg = _sigmoid(g)
            dy = dy_ref[:, hs].astype(F32)
            dg_ref[:, hs] = (dy * on * (sg * (1.0 + g * (1.0 - sg)))).astype(BF16)
            don = dy * (g * sg)
            dO = r * (don - on * jnp.mean(don * on, axis=-1, keepdims=True))
            dOb = dO.astype(BF16)
            do_ref[:, hs] = dOb
            dAm = _nt(dOb, v)
            T[p, h] += dAm * f["Am"]
            dAb = (dAm * f["Mk"]).astype(BF16)
            km = jnp.where(hm, k, jnp.zeros_like(k))
            dq = _nn(dAb, km)
            dk = _tn(dAb, f["qm"])
            dvh = _tn(f["Amb"], dOb)
            dQf = _nt(dOb, sf)
            dQb = _nt(dOb, sb)
            dq = dq + dQf * dv_["qf"] + dQb * dv_["qb"]
            acc_ref[p, h, 0:1, :] += _fsum(dQf * f["Qf"].astype(F32) * (dv_["i"] + 1.0))
            acc_ref[p, h, 1:2, :] += _fsum(dQb * f["Qb"].astype(F32) * (CHUNK - dv_["i"]))
            dSh = dS[p, h]
            dSb_ = dSh.astype(BF16)
            Kf = (km.astype(F32) * dv_["kf"]).astype(BF16)
            dKf = _nt(v, dSb_)
            dk = dk + jnp.where(hm, dKf * dv_["kf"], 0.0)
            acc_ref[p, h, 2:3, :] += _fsum(jnp.where(hm, dKf, 0.0) * Kf.astype(F32) * (CHUNK - 1.0 - dv_["i"]))
            dvh = dvh + _nn(Kf, dSb_)
            acc_ref[p, h, 3:4, :] += float(CHUNK) * dv_["gf"] * _fsum(dSh * sf.astype(F32))
            dSh = dv_["gf"] * dSh + _tn(f["Qf"], dOb)
            dS[p, h] = dSh
            dv_ref[:, hs] = dvh
            return dq, dk

        for p in range(npairs):
            dq0, dk0 = head_main(p, 0)
            dq1, dk1 = head_main(p, 1)
            dq_ref[:, kcols(p)] = dq0 + dq1
            dk_ref[:, kcols(p)] = dk0 + dk1

        @pl.when(n == N - 1)
        def _():
            for p, h in heads:
                lam = _lam_of(lg_ref, 0, 2 * p + h)
                dv_ = _decay_vecs(lam, lam)
                hm = masks[h]
                hs = vcols(p, h)
                states = [jnp.zeros((LANES, LANES), F32)]
                kws = []
                for cc in range(ncc):
                    rows = slice(cc * CHUNK, (cc + 1) * CHUNK)
                    kw = jnp.where(hm, kc_ref[rows, kcols(p)].astype(F32) * dv_["kf"], 0.0).astype(BF16)
                    kws.append(kw)
                    states.append(dv_["gf"] * states[-1] + _tn(kw, vc_ref[rows, hs]))
                d = dS[p, h]
                for cc in reversed(range(ncc)):
                    db = d.astype(BF16)
                    rows = slice(cc * CHUNK, (cc + 1) * CHUNK)
                    dKf_c = jnp.where(hm, _nt(vc_ref[rows, hs], db), 0.0)
                    part = dKf_c * dv_["kf"]
                    if h == 0:
                        dkc_ref[rows, kcols(p)] = part
                    else:
                        dkc_ref[rows, kcols(p)] += part
                    acc_ref[p, h, 2:3, :] += _fsum(dKf_c * kws[cc].astype(F32) * (CHUNK - 1.0 - dv_["i"]))
                    dvc_ref[rows, hs] = _nn(kws[cc], db)
                    acc_ref[p, h, 3:4, :] += float(CHUNK) * dv_["gf"] * _fsum(d * states[cc])
                    d = dv_["gf"] * d
                _, rel, low = _decay_mask(lam, lam)
                Th = T[p, h]
                acc_ref[p, h, 4:5, :] += _fsum(jnp.where(low, Th * rel, 0.0))
                acc_ref[p, h, 5:6, :] += _fsum(jnp.where(low, 0.0, -Th * rel))

    rev = lambda n: N - 1 - n
    wq, wv = npairs * LANES, npairs * 2 * LANES
    st = pl.BlockSpec((npairs, 1, 2, LANES, LANES), lambda n: (0, rev(n), 0, 0, 0))
    pair = pl.BlockSpec((CHUNK, wq), lambda n: (rev(n), 0))
    wide = lambda grp: pl.BlockSpec((CHUNK, wv), lambda n: (rev(n), grp))
    return pl.pallas_call(
        body, name="ret_bwd_desc", grid=(N,),
        in_specs=[pl.BlockSpec(memory_space=pltpu.SMEM), pair, pair, wide(rv_grp), wide(rg_grp), st, st, wide(0),
                  pl.BlockSpec((Lc, wq), lambda n: (0, 0)),
                  pl.BlockSpec((Lc, wv), lambda n: (0, rv_grp))],
        out_specs=[pair, pair, wide(0), wide(rg_grp), wide(0),
                   pl.BlockSpec((Lc, wq), lambda n: (0, 0)),
                   pl.BlockSpec((Lc, wv), lambda n: (0, 0)),
                   pl.BlockSpec((npairs, 2, ACC_ROWS, LANES), lambda n: (0, 0, 0, 0))],
        out_shape=[jax.ShapeDtypeStruct((L, npairs * LANES), F32),
                   jax.ShapeDtypeStruct((L, npairs * LANES), F32),
                   jax.ShapeDtypeStruct((L, npairs * 2 * LANES), F32),
                   jax.ShapeDtypeStruct((L, d_proj), BF16),
                   jax.ShapeDtypeStruct((L, npairs * 2 * LANES), BF16),
                   jax.ShapeDtypeStruct((Lc, npairs * LANES), F32),
                   jax.ShapeDtypeStruct((Lc, npairs * 2 * LANES), F32),
                   jax.ShapeDtypeStruct((npairs, 2, ACC_ROWS, LANES), F32)],
        scratch_shapes=[pltpu.VMEM((npairs, 2, LANES, LANES), F32), pltpu.VMEM((npairs, 2, CHUNK, CHUNK), F32),
                        pltpu.VMEM((npairs, 2, CHUNK, CHUNK), F32)],
        compiler_params=_params(1))(lg, Qr, Kr, P, P, SF, SB, dY, Krc, Pc)


def _ret_bwd2(Qr, Kr, P, Krc, Pc, SB, dO, dKr, dVp, dP, dKc, dVc, lg, rv_blk, npairs):
    L = Qr.shape[0]
    Lc = Krc.shape[0]
    N, ncc = L // CHUNK, Lc // CHUNK
    rv_grp = _group_index(rv_blk, npairs)
    heads = [(p, h) for p in range(npairs) for h in range(2)]
    kcols = lambda p: slice(p * LANES, (p + 1) * LANES)
    vcols = lambda p, h: slice((2 * p + h) * LANES, (2 * p + h + 1) * LANES)

    def body(lg_ref, q_ref, k_ref, v_ref, sb_ref, do_ref, dkin_ref, dvin_ref, kc_ref, vc_ref, dkcin_ref, dvcin_ref,
             dpin_ref, dk_ref, dv_ref, dkc_ref, dvc_ref, acc_ref, dS):
        del dpin_ref
        n = pl.program_id(0)
        masks = _head_masks()

        @pl.when(n == 0)
        def _():
            dS[...] = jnp.zeros_like(dS)
            acc_ref[...] = jnp.zeros_like(acc_ref)

        def head_main(p, h):
            mu = _lam_of(lg_ref, 1, 2 * p + h)
            hm = masks[h]
            hs = vcols(p, h)
            dv_ = _decay_vecs(mu, mu)
            v = v_ref[:, hs]
            k = k_ref[:, kcols(p)]
            q = q_ref[:, kcols(p)]
            dOb = do_ref[:, hs]
            km = jnp.where(hm, k, jnp.zeros_like(k)).astype(F32)
            Kb = (km * dv_["kb"]).astype(BF16)
            Qb = (jnp.where(hm, q, jnp.zeros_like(q)).astype(F32) * dv_["qb"]).astype(BF16)
            dSh = dS[p, h]
            dSb_ = dSh.astype(BF16)
            dKb = jnp.where(hm, _nt(v, dSb_), 0.0)
            acc_ref[p, h, 0:1, :] += _fsum(dKb * Kb.astype(F32) * dv_["i"])
            dv_ref[:, hs] = (dvin_ref[:, hs] + _nn(Kb, dSb_)).astype(BF16)
            acc_ref[p, h, 1:2, :] += float(CHUNK) * dv_["gb"] * _fsum(dSh * sb_ref[p, 0, h].astype(F32))
            dS[p, h] = dv_["gb"] * dSh + _tn(Qb, dOb)
            return dKb * dv_["kb"]

        for p in range(npairs):
            dk_ref[:, kcols(p)] = dkin_ref[:, kcols(p)] + head_main(p, 0) + head_main(p, 1)

        @pl.when(n == N - 1)
        def _():
            for p, h in heads:
                mu = _lam_of(lg_ref, 1, 2 * p + h)
                hm = masks[h]
                hs = vcols(p, h)
                dv_ = _decay_vecs(mu, mu)
                states = {}
                kws = {}
                s = jnp.zeros((LANES, LANES), F32)
                for cc in reversed(range(ncc)):
                    rows = slice(cc * CHUNK, (cc + 1) * CHUNK)
                    states[cc] = s
                    kw = jnp.where(hm, kc_ref[rows, kcols(p)].astype(F32) * dv_["kb"], 0.0).astype(BF16)
                    kws[cc] = kw
                    s = dv_["gb"] * s + _tn(kw, vc_ref[rows, hs])
                d = dS[p, h]
                for cc in range(ncc):
                    db = d.astype(BF16)
                    rows = slice(cc * CHUNK, (cc + 1) * CHUNK)
                    dKb_c = jnp.where(hm, _nt(vc_ref[rows, hs], db), 0.0)
                    part = dKb_c * dv_["kb"]
                    if h == 0:
                        dkc_ref[rows, kcols(p)] = dkcin_ref[rows, kcols(p)] + part
                    else:
                        dkc_ref[rows, kcols(p)] += part
                    acc_ref[p, h, 0:1, :] += _fsum(dKb_c * kws[cc].astype(F32) * dv_["i"])
                    dvc_ref[rows, hs] = dvcin_ref[rows, hs] + _nn(kws[cc], db)
                    acc_ref[p, h, 1:2, :] += float(CHUNK) * dv_["gb"] * _fsum(d * states[cc])
                    d = dv_["gb"] * d

    wq, wv = npairs * LANES, npairs * 2 * LANES
    st = pl.BlockSpec((npairs, 1, 2, LANES, LANES), lambda n: (0, n, 0, 0, 0))
    pair = pl.BlockSpec((CHUNK, wq), lambda n: (n, 0))
    wide = lambda grp: pl.BlockSpec((CHUNK, wv), lambda n: (n, grp))
    ckc = pl.BlockSpec((Lc, wq), lambda n: (0, 0))
    cvc = lambda grp: pl.BlockSpec((Lc, wv), lambda n: (0, grp))
    return pl.pallas_call(
        body, name="ret_bwd_asc", grid=(N,),
        in_specs=[pl.BlockSpec(memory_space=pltpu.SMEM), pair, pair, wide(rv_grp), st, wide(0), pair, wide(0),
                  ckc, cvc(rv_grp), ckc, cvc(0), pl.BlockSpec(memory_space=pl.ANY)],
        out_specs=[pair, wide(rv_grp), ckc, cvc(0),
                   pl.BlockSpec((npairs, 2, ACC_ROWS, LANES), lambda n: (0, 0, 0, 0))],
        out_shape=[jax.ShapeDtypeStruct(dKr.shape, F32),
                   jax.ShapeDtypeStruct(dP.shape, dP.dtype),
                   jax.ShapeDtypeStruct(dKc.shape, F32),
                   jax.ShapeDtypeStruct(dVc.shape, F32),
                   jax.ShapeDtypeStruct((npairs, 2, ACC_ROWS, LANES), F32)],
        input_output_aliases={12: 1},
        scratch_shapes=[pltpu.VMEM((npairs, 2, LANES, LANES), F32)],
        compiler_params=_params(1))(lg, Qr, Kr, P, SB, dO, dKr, dVp, Krc, Pc, dKc, dVc, dP)


GROUP = 4


def _att_band(Lc):
    row = np.arange(GROUP * CHUNK)[:, None] % CHUNK
    col = np.arange(3 * CHUNK + Lc)[None, :]
    ok = ((col >= row) & (col <= row + 2 * CHUNK)) | (col >= 3 * CHUNK)
    return jnp.asarray(np.where(ok, 0.0, NEG), F32)


def _att_edge(n, N, Lc):
    col = lax.broadcasted_iota(jnp.int32, (1, 3 * CHUNK + Lc), 1)
    off = jnp.logical_or(jnp.logical_and(col < CHUNK, n == 0),
                         jnp.logical_and(jnp.logical_and(col >= 2 * CHUNK, col < 3 * CHUNK), n == N - 1))
    return jnp.where(off, NEG, 0.0)


def _stack_heads(ref, gi):
    masks = _head_masks()
    tiles = []
    for pr in range(2):
        t = ref[:, (2 * gi + pr) * LANES:(2 * gi + pr + 1) * LANES]
        for a in range(2):
            tiles.append(jnp.where(masks[a], t, jnp.zeros_like(t)))
    return jnp.concatenate(tiles, axis=0)


def _unstack_heads(x4):
    m0 = _head_masks()[0]
    return [jnp.where(m0, x4[(2 * pr) * CHUNK:(2 * pr + 1) * CHUNK], x4[(2 * pr + 1) * CHUNK:(2 * pr + 2) * CHUNK])
            for pr in range(2)]


def _sink_column(sink_ref, g):
    row = lax.broadcasted_iota(jnp.int32, (GROUP * CHUNK, 1), 0) // CHUNK
    col = jnp.zeros((GROUP * CHUNK, 1), F32)
    for h in range(GROUP):
        col = jnp.where(row == h, sink_ref[0, g * GROUP + h], col)
    return col


def _att_probs(q4, Kall, bias, snk):
    s = _nt(q4, Kall) + bias
    mx = jnp.maximum(jnp.max(s, axis=1, keepdims=True), snk)
    p = jnp.exp(s - mx)
    p_snk = jnp.exp(snk - mx)
    inv = 1.0 / (jnp.sum(p, axis=1, keepdims=True) + p_snk)
    return p, p_snk, inv


def _att_groups_per_step(nkv, blk0):
    return 2 if nkv % 2 == 0 and blk0 % 2 == 0 else 1


def _att_specs(Lc, N, gps):
    q = pl.BlockSpec((CHUNK, gps * 2 * LANES), lambda g, n: (n, g))
    kv = lambda s: pl.BlockSpec((gps, CHUNK, LANES), lambda g, n: (g, jnp.clip(n + s, 0, N - 1), 0))
    ctx = pl.BlockSpec((gps, Lc, LANES), lambda g, n: (g, 0, 0))
    return q, kv, ctx


def _att_fwd(Qa, Kd, Vd, Kdc, Vdc, sink, Y, blk0):
    L = Qa.shape[0]
    Lc = Kdc.shape[1]
    N = L // CHUNK
    nkv = Kd.shape[0]
    gps = _att_groups_per_step(nkv, blk0)

    def body(sink_ref, band_ref, q_ref, kp, kc_, kn, vp, vc_, vn, kctx, vctx, y_in, o_ref):
        del y_in
        g, n = pl.program_id(0), pl.program_id(1)
        bias = band_ref[...] + _att_edge(n, N, Lc)
        for gi in range(gps):
            Kall = jnp.concatenate([kp[gi], kc_[gi], kn[gi], kctx[gi]], axis=0)
            Vall = jnp.concatenate([vp[gi], vc_[gi], vn[gi], vctx[gi]], axis=0)
            p, _, inv = _att_probs(_stack_heads(q_ref, gi), Kall, bias, _sink_column(sink_ref, g * gps + gi))
            o4 = _nn(p.astype(BF16), Vall) * inv
            for pr, o in enumerate(_unstack_heads(o4)):
                o_ref[:, (2 * gi + pr) * LANES:(2 * gi + pr + 1) * LANES] = o.astype(BF16)

    q, kv, ctx = _att_specs(Lc, N, gps)
    band = pl.BlockSpec((GROUP * CHUNK, 3 * CHUNK + Lc), lambda g, n: (0, 0))
    return pl.pallas_call(
        body, name="att_fwd", grid=(nkv // gps, N),
        in_specs=[pl.BlockSpec(memory_space=pltpu.SMEM), band, q, kv(-1), kv(0), kv(1), kv(-1), kv(0), kv(1), ctx, ctx,
                  pl.BlockSpec(memory_space=pl.ANY)],
        out_specs=pl.BlockSpec((CHUNK, gps * 2 * LANES), lambda g, n: (n, blk0 // gps + g)),
        out_shape=jax.ShapeDtypeStruct(Y.shape, Y.dtype),
        input_output_aliases={11: 0},
        compiler_params=_params(2))(sink, _att_band(Lc), Qa, Kd, Kd, Kd, Vd, Vd, Vd, Kdc, Vdc, Y)


def _att_bwd(Qa, Kd, Vd, Kdc, Vdc, sink, dY, blk0):
    L = Qa.shape[0]
    Lc = Kdc.shape[1]
    N = L // CHUNK
    nkv = Kd.shape[0]
    gps = _att_groups_per_step(nkv, blk0)

    def body(sink_ref, band_ref, q_ref, kp, kc_, kn, vp, vc_, vn, kctx, vctx, dy_ref,
             dq_ref, dkp, dkc_, dkn, dvp, dvc_, dvn, dkctx, dvctx, dsink_ref):
        g, n = pl.program_id(0), pl.program_id(1)

        @pl.when(n == 0)
        def _():
            dkctx[...] = jnp.zeros_like(dkctx)
            dvctx[...] = jnp.zeros_like(dvctx)
            dsink_ref[...] = jnp.zeros_like(dsink_ref)

        bias = band_ref[...] + _att_edge(n, N, Lc)
        for gi in range(gps):
            Kall = jnp.concatenate([kp[gi], kc_[gi], kn[gi], kctx[gi]], axis=0)
            Vall = jnp.concatenate([vp[gi], vc_[gi], vn[gi], vctx[gi]], axis=0)
            q4 = _stack_heads(q_ref, gi)
            do4 = _stack_heads(dy_ref, gi)
            p, p_snk, inv = _att_probs(q4, Kall, bias, _sink_column(sink_ref, g * gps + gi))
            P = p * inv
            dp = _nt(do4, Vall)
            delta = jnp.sum(P * dp, axis=1, keepdims=True)
            ds = (P * (dp - delta)).astype(BF16)
            dsnk = -(p_snk * inv) * delta
            for h in range(GROUP):
                dsink_ref[gi, h:h + 1, :] += _fsum(dsnk[h * CHUNK:(h + 1) * CHUNK])
            for pr, dq in enumerate(_unstack_heads(_nn(ds, Kall))):
                dq_ref[:, (2 * gi + pr) * LANES:(2 * gi + pr + 1) * LANES] = dq
            dK = _tn(ds, q4)
            dV = _tn(P.astype(BF16), do4)
            for j, (rk, rv) in enumerate([(dkp, dvp), (dkc_, dvc_), (dkn, dvn)]):
                rk[gi] = dK[j * CHUNK:(j + 1) * CHUNK].astype(BF16)
                rv[gi] = dV[j * CHUNK:(j + 1) * CHUNK].astype(BF16)
            dkctx[gi] += dK[3 * CHUNK:]
            dvctx[gi] += dV[3 * CHUNK:]

    q, kv, ctx = _att_specs(Lc, N, gps)
    band = pl.BlockSpec((GROUP * CHUNK, 3 * CHUNK + Lc), lambda g, n: (0, 0))
    blk = pl.BlockSpec((gps, CHUNK, LANES), lambda g, n: (g, n, 0))
    part = jax.ShapeDtypeStruct((nkv, L, LANES), BF16)
    cshape = jax.ShapeDtypeStruct((nkv, Lc, LANES), F32)
    return pl.pallas_call(
        body, name="att_bwd", grid=(nkv // gps, N),
        in_specs=[pl.BlockSpec(memory_space=pltpu.SMEM), band, q, kv(-1), kv(0), kv(1), kv(-1), kv(0), kv(1), ctx, ctx,
                  pl.BlockSpec((CHUNK, gps * 2 * LANES), lambda g, n: (n, blk0 // gps + g))],
        out_specs=[q, blk, blk, blk, blk, blk, blk, ctx, ctx,
                   pl.BlockSpec((gps, 8, LANES), lambda g, n: (g, 0, 0))],
        out_shape=[jax.ShapeDtypeStruct(Qa.shape, F32), part, part, part, part, part, part, cshape, cshape,
                   jax.ShapeDtypeStruct((nkv, 8, LANES), F32)],
        compiler_params=_params(2))(sink, _att_band(Lc), Qa, Kd, Kd, Kd, Vd, Vd, Vd, Kdc, Vdc, dY)


def _scale_rows(dx, gt, saved, name):
    M, D = dx.shape
    tm = _tile(M, 512, 8)

    def body(dx_ref, gt_ref, sv_ref, dz_ref, dgt_ref):
        @pl.when(pl.program_id(0) == 0)
        def _():
            dgt_ref[...] = jnp.zeros_like(dgt_ref)
        d = dx_ref[...]
        dz_ref[...] = (d * gt_ref[...]).astype(BF16)
        dgt_ref[...] += jnp.sum(d * sv_ref[...].astype(F32), axis=0, keepdims=True)

    row = pl.BlockSpec((tm, D), lambda i: (i, 0))
    vec = pl.BlockSpec((1, D), lambda i: (0, 0))
    return pl.pallas_call(
        body, name=name, grid=(M // tm,), in_specs=[row, vec, row], out_specs=[row, vec],
        out_shape=[jax.ShapeDtypeStruct((M, D), BF16), jax.ShapeDtypeStruct((1, D), F32)],
        compiler_params=_params(1))(dx, gt, saved)


def _bwd_proj(dz, w, G=None, U=None, name="bwd_proj"):
    M, D = dz.shape
    N = w.shape[0]
    swiglu = G is not None
    tm, tn = _tile(M, ROWS_PER_LATCH, 8), _tile(N, 512)

    def body(*refs):
        if swiglu:
            dz_ref, w_ref, G_ref, U_ref, dG_ref, dU_ref = refs
        else:
            dz_ref, w_ref, dA_ref = refs
        dA = _nt(dz_ref[...], w_ref[...])
        if swiglu:
            Gv = G_ref[...].astype(F32)
            Uv = U_ref[...].astype(F32)
            sg = _sigmoid(Gv)
            dU_ref[...] = (dA * Gv * sg).astype(BF16)
            dG_ref[...] = (dA * Uv * (sg * (1.0 + Gv * (1.0 - sg)))).astype(BF16)
        else:
            dA_ref[...] = dA.astype(BF16)

    row = pl.BlockSpec((tm, D), lambda i, j: (i, 0))
    tile = pl.BlockSpec((tm, tn), lambda i, j: (i, j))
    big = jax.ShapeDtypeStruct((M, N), BF16)
    in_specs = [row, pl.BlockSpec((tn, D), lambda i, j: (j, 0))]
    args = [dz, w]
    if swiglu:
        in_specs += [tile, tile]
        args += [G, U]
        out_specs, out_shape = [tile, tile], [big, big]
    else:
        out_specs, out_shape = tile, big
    return pl.pallas_call(
        body, name=name, grid=(M // tm, N // tn), in_specs=in_specs, out_specs=out_specs, out_shape=out_shape,
        compiler_params=_params(2))(*args)


def _tn_matmul(pairs, name):
    Ka, Nb = pairs[0][0].shape[1], pairs[0][1].shape[1]
    tk, tn = _tile(Ka, 2048), _tile(Nb, 2048)
    tls, nks = [], []
    for a, _ in pairs:
        tl = _tile(a.shape[0], 512, 8)
        tls.append(tl)
        nks.append(a.shape[0] // tl)
    starts = [int(s) for s in np.cumsum([0] + nks[:-1])]
    nk = int(sum(nks))

    def body(*refs):
        out_ref, acc = refs[-2], refs[-1]
        k = pl.program_id(2)

        @pl.when(k == 0)
        def _():
            acc[...] = jnp.zeros_like(acc)

        for idx in range(len(pairs)):
            a_ref, b_ref = refs[2 * idx], refs[2 * idx + 1]

            @pl.when(jnp.logical_and(k >= starts[idx], k < starts[idx] + nks[idx]))
            def _():
                acc[...] += _tn(a_ref[...], b_ref[...])

        @pl.when(k == nk - 1)
        def _():
            out_ref[...] = acc[...].astype(BF16)

    in_specs, args = [], []
    for idx, (a, b) in enumerate(pairs):
        s0, n_ = starts[idx], nks[idx]
        in_specs.append(pl.BlockSpec((tls[idx], tk), lambda i, j, k, s0=s0, n_=n_: (jnp.clip(k - s0, 0, n_ - 1), i)))
        in_specs.append(pl.BlockSpec((tls[idx], tn), lambda i, j, k, s0=s0, n_=n_: (jnp.clip(k - s0, 0, n_ - 1), j)))
        args += [a, b]
    return pl.pallas_call(
        body, name=name, grid=(Ka // tk, Nb // tn, nk), in_specs=in_specs,
        out_specs=pl.BlockSpec((tk, tn), lambda i, j, k: (i, j)),
        out_shape=jax.ShapeDtypeStruct((Ka, Nb), BF16),
        scratch_shapes=[pltpu.VMEM((tk, tn), F32)], compiler_params=_params(3))(*args)


def _bwd_norm_mod(pairs, x, dres, g, sh, sc, name):
    M, D = x.shape
    K = pairs[0][0].shape[1]
    tm, tk = _tile(M, 512, 8), _tile(K, 1152 if len(pairs) == 1 else 512)
    nk = K // tk
    npair = len(pairs)
    has_res = dres is not None

    def body(*refs):
        pr = refs[:2 * npair]
        rest = refs[2 * npair:]
        if has_res:
            x_ref, dres_ref, g_ref, sh_ref, sc_ref, dx_ref, st_ref, acc = rest
        else:
            x_ref, g_ref, sh_ref, sc_ref, dx_ref, st_ref, acc = rest
        del sh_ref
        i, k = pl.program_id(0), pl.program_id(1)

        @pl.when(jnp.logical_and(i == 0, k == 0))
        def _():
            st_ref[...] = jnp.zeros_like(st_ref)

        @pl.when(k == 0)
        def _():
            acc[...] = jnp.zeros_like(acc)

        t = _nt(pr[1][...], pr[0][...])
        for idx in range(1, npair):
            t = t + _nt(pr[2 * idx + 1][...], pr[2 * idx][...])
        acc[...] += t

        @pl.when(k == nk - 1)
        def _():
            xv = x_ref[...]
            gv = g_ref[...]
            dh = acc[...].T
            r = lax.rsqrt(jnp.mean(xv * xv, axis=-1, keepdims=True) + NORM_EPS)
            xh = xv * r
            st_ref[0:1, :] += jnp.sum(dh, axis=0, keepdims=True)
            st_ref[1:2, :] += jnp.sum(dh * (xh * gv), axis=0, keepdims=True)
            dn = dh * (1.0 + sc_ref[...])
            st_ref[2:3, :] += jnp.sum(dn * xh, axis=0, keepdims=True)
            dxh = dn * gv
            d = r * (dxh - xh * jnp.mean(dxh * xh, axis=-1, keepdims=True))
            if has_res:
                d = d + dres_ref[...]
            dx_ref[...] = d

    row = pl.BlockSpec((tm, D), lambda i, k: (i, 0))
    vec = pl.BlockSpec((1, D), lambda i, k: (0, 0))
    in_specs, args = [], []
    for dA, w in pairs:
        in_specs += [pl.BlockSpec((tm, tk), lambda i, k: (i, k)), pl.BlockSpec((D, tk), lambda i, k: (0, k))]
        args += [dA, w]
    in_specs += [row] + ([row] if has_res else []) + [vec, vec, vec]
    args += [x] + ([dres] if has_res else []) + [g, sh, sc]
    return pl.pallas_call(
        body, name=name, grid=(M // tm, nk), in_specs=in_specs,
        out_specs=[row, pl.BlockSpec((8, D), lambda i, k: (0, 0))],
        out_shape=[jax.ShapeDtypeStruct((M, D), F32), jax.ShapeDtypeStruct((8, D), F32)],
        scratch_shapes=[pltpu.VMEM((D, tm), F32)], compiler_params=_params(2))(*args)


def _local_step(x, ctx, tgt, mod, modc, norm_mix, norm_ffn, norm_final, lg, sink, w_in, rest_weights, on_grads, flush):
    L, D = x.shape
    Lc = ctx.shape[0]
    d_proj = w_in.shape[1]
    npairs = RET_HEADS // 2
    nkv = ATT_KV_HEADS
    nkvp = nkv // 2
    o_rq = 0
    o_rk = o_rq + RET_HEADS * RET_DK // LANES
    o_rv = o_rk + RET_HEADS * RET_DK // LANES
    o_rg = o_rv + RET_HEADS * RET_DV // LANES
    o_aq = o_rg + RET_HEADS * RET_DV // LANES
    o_ak = o_aq + ATT_HEADS * ATT_DH // LANES
    o_av = o_ak + nkv * ATT_DH // LANES
    assert (o_av + nkv * ATT_DH // LANES) * LANES == d_proj
    assert o_rv % 2 == 0 and o_rg % 2 == 0 and (RET_HEADS * RET_DV) % (2 * LANES) == 0
    rv_blk, rg_blk = o_rv // 2, o_rg // 2
    d_ret = RET_HEADS * RET_DV
    d_mix = d_ret + ATT_HEADS * ATT_DH
    att_blk = d_ret // (2 * LANES)
    k_scale = RET_DK ** -0.5
    a_scale = ATT_DH ** -0.5

    T = _rope_tables(L)
    Tc = dict(C=jnp.ones((Lc, LANES), F32), S=jnp.zeros((Lc, LANES), F32))
    row = lambda m, i: m[i:i + 1]
    sh_m, sc_m, gt_m, sh_f, sc_f, gt_f = [row(mod, i) for i in range(6)]
    sh_mc, sc_mc = row(modc, 0), row(modc, 1)

    P, hx = _norm_mod_matmul(x, norm_mix, sh_m, sc_m, w_in, "in_proj")
    Pc, hc = _norm_mod_matmul(ctx, norm_mix, sh_mc, sc_mc, w_in, "in_proj_ctx")
    nq = RET_HEADS * RET_DK // LANES
    Qr = _rope_cols(P, o_rq, nq, T["Cr"], T["Sr"], T["Rr"], 1.0, True, "rope_rq")
    Kr = _rope_cols(P, o_rk, nq, T["Cr"], T["Sr"], T["Rr"], k_scale, True, "rope_rk")
    Krc = _rope_cols(Pc, o_rk, nq, Tc["C"], Tc["S"], T["Rr"], k_scale, False, "scale_rk_ctx")
    Qa = _rope_cols(P, o_aq, ATT_HEADS * ATT_DH // LANES, T["Ca"], T["Sa"], T["Ra"], a_scale, True, "rope_aq")
    Kd = _dup_heads(P, o_ak, nkvp, T["Ca"], T["Sa"], T["Ra"], T["D0"], T["D1"], True, "dup_ak")
    Vd = _dup_heads(P, o_av, nkvp, T["Ca"], T["Sa"], T["Ra"], T["D0"], T["D1"], False, "dup_av")
    Kdc = _dup_heads(Pc, o_ak, nkvp, Tc["C"], Tc["S"], T["Ra"], T["D0"], T["D1"], False, "dup_ak_ctx")
    Vdc = _dup_heads(Pc, o_av, nkvp, Tc["C"], Tc["S"], T["Ra"], T["D0"], T["D1"], False, "dup_av_ctx")

    SF = _ret_states_fwd(Kr, P, Krc, Pc, lg, rv_blk, npairs)
    Y, SB = _ret_out_fwd(Qr, Kr, P, Krc, Pc, SF, lg, rv_blk, rg_blk, npairs, d_mix)
    Y = _att_fwd(Qa, Kd, Vd, Kdc, Vdc, sink, Y, att_blk)

    w_out, w_gate, w_up, w_down = rest_weights(Y)
    x1, O1 = _proj_residual(Y, w_out, x, gt_m, "out_proj")
    G, U, A, h2 = _ffn_in(x1, norm_ffn, sh_f, sc_f, w_gate, w_up)
    x2, Fo = _proj_residual(A, w_down, x1, gt_f, "ffn_out")
    dx2, loss, d_norm_final, dz2, dgt_f = _final(x2, norm_final, tgt, gt_f, Fo)

    dG, dU = _bwd_proj(dz2, w_down, G, U, name="ffn_out_bwd")
    g_w_down = _tn_matmul([(A, dz2)], "grad_w_down")
    tok = on_grads(["w_down"], [g_w_down])
    dx1, st_f = _bwd_norm_mod([(dG, w_gate), (dU, w_up)], x1, dx2, norm_ffn + tok, sh_f, sc_f, "ffn_in_bwd")
    tok = flush(dx1)
    g_w_gate = _tn_matmul([(h2, dG)], "grad_w_gate")
    g_w_up = _tn_matmul([(h2, dU)], "grad_w_up")
    tok = tok + on_grads(["w_gate", "w_up"], [g_w_gate, g_w_up])
    dz1, dgt_m = _scale_rows(dx1, gt_m + tok, O1, "mix_gate_bwd")
    dY = _bwd_proj(dz1, w_out, name="out_proj_bwd")
    tok = flush(dY)
    g_w_out = _tn_matmul([(Y, dz1)], "grad_w_out")
    tok = tok + on_grads(["w_out"], [g_w_out])

    dQa, dKp, dKs, dKn, dVp, dVs, dVn, dKdc, dVdc, dsink = _att_bwd(Qa, Kd, Vd, Kdc, Vdc, sink + tok, dY, att_blk)
    tok = flush(dQa)
    dQr, dKr, dVr, dP, dO, dKc, dVc, acc1 = _ret_bwd1(Qr, Kr, P, Krc, Pc, SF, SB, dY, lg + tok, rv_blk, rg_blk, npairs,
                                                      d_proj)
    dKr, dP, dKc, dVc, acc2 = _ret_bwd2(Qr, Kr, P, Krc, Pc, SB, dO, dKr, dVr, dP, dKc, dVc, lg, rv_blk, npairs)

    dP = _unrope_cols(dQr, dP, o_rq, nq, T["Cr"], T["Sr"], T["RrT"], 1.0, True, "unrope_rq")
    dP = _unrope_cols(dKr, dP, o_rk, nq, T["Cr"], T["Sr"], T["RrT"], k_scale, True, "unrope_rk")
    dP = _unrope_cols(dQa, dP, o_aq, ATT_HEADS * ATT_DH // LANES, T["Ca"], T["Sa"], T["RaT"], a_scale, True, "unrope_aq")
    dP = _fold_heads([(dKs, 0), (dKp, 1), (dKn, -1)], dP, o_ak, nkvp, T["Ca"], T["Sa"], T["RaT"], T["D0T"], T["D1T"],
                     True, "fold_ak")
    dP = _fold_heads([(dVs, 0), (dVp, 1), (dVn, -1)], dP, o_av, nkvp, T["Ca"], T["Sa"], T["RaT"], T["D0T"], T["D1T"],
                     False, "fold_av")
    dPc = jnp.zeros((Lc, d_proj), BF16)
    dPc = _unrope_cols(dKc, dPc, o_rk, nq, Tc["C"], Tc["S"], T["RrT"], k_scale, False, "ctx_rk_bwd")
    dPc = _unrope_cols(dVc, dPc, o_rv, RET_HEADS * RET_DV // LANES, Tc["C"], Tc["S"], T["RrT"], 1.0, False, "ctx_rv_bwd")
    dPc = _fold_heads([(dKdc.astype(BF16), 0)], dPc, o_ak, nkvp, Tc["C"], Tc["S"], T["RaT"], T["D0T"], T["D1T"],
                      False, "fold_ak_ctx")
    dPc = _fold_heads([(dVdc.astype(BF16), 0)], dPc, o_av, nkvp, Tc["C"], Tc["S"], T["RaT"], T["D0T"], T["D1T"],
                      False, "fold_av_ctx")

    dx, st_m = _bwd_norm_mod([(dP, w_in)], x, dx1, norm_mix, sh_m, sc_m, "in_proj_bwd")
    _, st_mc = _bwd_norm_mod([(dPc, w_in)], ctx, None, norm_mix, sh_mc, sc_mc, "in_proj_ctx_bwd")
    g_w_in = _tn_matmul([(hx, dP), (hc, dPc)], "grad_w_in")
    on_grads(["w_in"], [g_w_in])

    a1 = acc1[:, :, :, 0].reshape(RET_HEADS, ACC_ROWS)
    a2 = acc2[:, :, :, 0].reshape(RET_HEADS, ACC_ROWS)
    dlam = (a1[:, 0] + a1[:, 2] + a1[:, 3] + a1[:, 4]) * lg[0]
    dmu = (a1[:, 1] + a1[:, 5] + a2[:, 0] + a2[:, 1]) * lg[1]
    d_sink = dsink[:, :4, 0].reshape(1, ATT_HEADS)

    nh = RET_HEADS
    assert 2 * nh + ATT_HEADS <= LOSS_LANE
    small = _pack_rows(
        [(st_m, 0, 2, 0, 0), (dgt_m, 0, 1, 2, 0), (st_f, 0, 2, 3, 0), (dgt_f, 0, 1, 5, 0), (st_mc, 0, 2, 6, 0),
         (st_m[2:3] + st_mc[2:3], 0, 1, 12, 0), (st_f, 2, 1, 13, 0), (d_norm_final, 0, 1, 14, 0),
         (dlam.reshape(1, nh), 0, 1, 15, 0), (dmu.reshape(1, nh), 0, 1, 15, nh), (d_sink, 0, 1, 15, 2 * nh),
         (loss[:, 0:1], 0, 1, 15, LOSS_LANE)], 16, D, "pack_small")
    return dict(grad_x=dx, small=small)


def _my_pos():
    return lax.axis_index("x"), lax.axis_index("y"), lax.axis_index("c")


def _other_chips(x, y):
    return [(1 - x, y), (x, 1 - y), (1 - x, 1 - y)]


def _remote(src, dst, ssem, rsem, dev):
    return pltpu.make_async_remote_copy(src_ref=src, dst_ref=dst, send_sem=ssem, recv_sem=rsem,
                                        device_id=dev, device_id_type=MESH)


def _allgather8(v, name):
    R, Cc = v.shape

    def body(v_ref, out_ref, send_sems, recv_sems):
        x, y, c = _my_pos()
        me = 4 * x + 2 * y + c
        out_ref[pl.ds(me, 1)] = v_ref[...][None]
        peers = []
        for j in range(1, N_DEV):
            peers.append((1 - x if (j >> 2) & 1 else x, 1 - y if (j >> 1) & 1 else y, 1 - c if j & 1 else c))
        copies = []
        for j, peer in enumerate(peers):
            cp = _remote(v_ref, out_ref.at[me], send_sems.at[j], recv_sems.at[j], peer)
            cp.start()
            copies.append(cp)
        for j, peer in enumerate(peers):
            pid = 4 * peer[0] + 2 * peer[1] + peer[2]
            _remote(v_ref, out_ref.at[pid], send_sems.at[j], recv_sems.at[j], peer).wait_recv()
        for cp in copies:
            cp.wait_send()

    return pl.pallas_call(
        body, name=name, out_shape=jax.ShapeDtypeStruct((N_DEV, R, Cc), v.dtype),
        in_specs=[pl.BlockSpec(memory_space=pltpu.VMEM)], out_specs=pl.BlockSpec(memory_space=pltpu.VMEM),
        scratch_shapes=[pltpu.SemaphoreType.DMA((N_DEV - 1,)), pltpu.SemaphoreType.DMA((N_DEV - 1,))])(v)


def _region(ref, k, half, shard_shape, axis):
    r, cs = shard_shape
    hr = r // 2
    if axis == 1:
        return ref.at[pl.ds(pl.multiple_of(half * hr, 16), hr), pl.ds(pl.multiple_of(k * cs, LANES), cs)]
    return ref.at[pl.ds(pl.multiple_of(k * r + half * hr, 16), hr), :]


def _full_shape(shard_shape, axis):
    r, cs = shard_shape
    return (r, N_CHIPS * cs) if axis == 1 else (N_CHIPS * r, cs)


def _half_pieces(ref, half, shard_shape, axis):
    r, cs = shard_shape
    hr = r // 2
    if axis == 1:
        return [ref.at[pl.ds(pl.multiple_of(half * hr, 16), hr), :]]
    return [ref.at[pl.ds(pl.multiple_of(k * r + half * hr, 16), hr), :] for k in range(N_CHIPS)]


def _half_block_spec(shard_shape, axis, tr):
    r, cs = shard_shape
    hr = r // 2
    if axis == 1:
        return pl.BlockSpec((tr, cs), lambda k, i, c_ref: (c_ref[0] * (hr // tr) + i, k))
    return pl.BlockSpec((tr, cs), lambda k, i, c_ref: (k * (r // tr) + c_ref[0] * (hr // tr) + i, 0))


def _add_halves(g, recv, cvec, shard_shape, axis, name):
    r, cs = shard_shape
    hr = r // 2
    tr = _tile(hr, 256, 16)

    def body(c_ref, a_ref, b_ref, o_ref):
        del c_ref
        o_ref[0] = (a_ref[...].astype(F32) + b_ref[...].astype(F32)).astype(BF16)

    spec = _half_block_spec(shard_shape, axis, tr)
    return pl.pallas_call(
        body, name=name,
        grid_spec=pltpu.PrefetchScalarGridSpec(
            num_scalar_prefetch=1, grid=(N_CHIPS, hr // tr), in_specs=[spec, spec],
            out_specs=pl.BlockSpec((1, tr, cs), lambda k, i, c_ref: (k, i, 0))),
        out_shape=jax.ShapeDtypeStruct((N_CHIPS, hr, cs), BF16),
        compiler_params=_params(2, False))(cvec, g, recv)


def _sum_chips(sums, landed, kc, name):
    _, hr, cs = sums.shape
    tr = _tile(hr, 256, 16)

    def body(kc_ref, own_ref, a_ref, b_ref, c_ref, o_ref):
        del kc_ref
        o_ref[...] = (own_ref[0].astype(F32) + a_ref[0].astype(F32)) + (b_ref[0].astype(F32) + c_ref[0].astype(F32))

    slot = lambda j: pl.BlockSpec((1, tr, cs), lambda i, kc_ref: ((kc_ref[0] + j) % N_CHIPS, i, 0))
    return pl.pallas_call(
        body, name=name,
        grid_spec=pltpu.PrefetchScalarGridSpec(
            num_scalar_prefetch=1, grid=(hr // tr,), in_specs=[slot(0), slot(1), slot(2), slot(3)],
            out_specs=pl.BlockSpec((tr, cs), lambda i, kc_ref: (kc_ref[1] * (hr // tr) + i, 0))),
        out_shape=jax.ShapeDtypeStruct((2 * hr, cs), F32),
        compiler_params=_params(1, False))(kc, sums, landed, landed, landed)


def _exchange_halves(shards, name):
    nw = len(shards)

    def body(*refs):
        out_refs = refs[nw:2 * nw]
        send, recv = refs[2 * nw:]
        x, y, c = _my_pos()
        sib = (x, y, 1 - c)
        copies = []
        for w in range(nw):
            hr = shards[w].shape[0] // 2
            mine = out_refs[w].at[pl.ds(pl.multiple_of(c * hr, 8), hr), :]
            cp = _remote(mine, mine, send.at[w], recv.at[w], sib)
            cp.start()
            copies.append(cp)
        for w in range(nw):
            hr = shards[w].shape[0] // 2
            other = out_refs[w].at[pl.ds(pl.multiple_of((1 - c) * hr, 8), hr), :]
            _remote(other, other, send.at[w], recv.at[w], sib).wait_recv()
        for cp in copies:
            cp.wait_send()

    anyspec = pl.BlockSpec(memory_space=pl.ANY)
    return pl.pallas_call(
        body, name=name,
        out_shape=[jax.ShapeDtypeStruct(s.shape, F32) for s in shards],
        in_specs=[anyspec] * nw, out_specs=[anyspec] * nw,
        input_output_aliases={w: w for w in range(nw)},
        scratch_shapes=[pltpu.SemaphoreType.DMA((nw,)), pltpu.SemaphoreType.DMA((nw,))])(*shards)


def _cast_into_full(w, kc, axis, name):
    r, cs = w.shape
    tr = _tile(r, 256, 16)

    def body(kc_ref, w_ref, o_ref):
        del kc_ref
        o_ref[...] = w_ref[...].astype(BF16)

    if axis == 1:
        ospec = pl.BlockSpec((tr, cs), lambda i, kc_ref: (i, kc_ref[0]))
    else:
        ospec = pl.BlockSpec((tr, cs), lambda i, kc_ref: (kc_ref[0] * (r // tr) + i, 0))
    return pl.pallas_call(
        body, name=name,
        grid_spec=pltpu.PrefetchScalarGridSpec(
            num_scalar_prefetch=1, grid=(r // tr,), in_specs=[pl.BlockSpec((tr, cs), lambda i, kc_ref: (i, 0))],
            out_specs=ospec),
        out_shape=jax.ShapeDtypeStruct(_full_shape((r, cs), axis), BF16),
        compiler_params=_params(1, False))(kc, w)


def _adam_math(w, g, m, v):
    m2 = ADAM_B1 * m + (1.0 - ADAM_B1) * g
    v2 = ADAM_B2 * v + (1.0 - ADAM_B2) * (g * g)
    m_hat = m2 / (1.0 - ADAM_B1 ** ADAM_STEP)
    v_hat = v2 / (1.0 - ADAM_B2 ** ADAM_STEP)
    delta = -ADAM_LR * (m_hat / (jnp.sqrt(v_hat) + ADAM_EPS) + ADAM_WD * w)
    return delta, m2, v2


def _adam(w, g, m, v, name):
    r, cs = w.shape
    tr = _tile(r, 256, 8)

    def body(w_ref, g_ref, m_ref, v_ref, d_ref, m2_ref, v2_ref):
        d, m2, v2 = _adam_math(w_ref[...], g_ref[...], m_ref[...], v_ref[...])
        d_ref[...] = d
        m2_ref[...] = m2
        v2_ref[...] = v2

    spec = pl.BlockSpec((tr, cs), lambda i: (i, 0))
    shp = jax.ShapeDtypeStruct((r, cs), F32)
    return pl.pallas_call(body, name=name, grid=(r // tr,), in_specs=[spec] * 4, out_specs=[spec] * 3,
                          out_shape=[shp, shp, shp], compiler_params=_params(1, False))(w, g, m, v)


def _mod_rows(a16, w, b, name):
    D, n = w.shape
    tn = _tile(n, 512)

    def body(a_ref, w_ref, b_ref, o_ref):
        a = a_ref[...]
        o_ref[...] = _nn((a * _sigmoid(a)).astype(BF16), w_ref[...].astype(BF16)) + b_ref[...]

    return pl.pallas_call(
        body, name=name, grid=(n // tn,),
        in_specs=[pl.BlockSpec((16, D), lambda j: (0, 0)), pl.BlockSpec((D, tn), lambda j: (0, j)),
                  pl.BlockSpec((1, tn), lambda j: (0, j))],
        out_specs=pl.BlockSpec((16, tn), lambda j: (0, j)),
        out_shape=jax.ShapeDtypeStruct((16, n), F32), compiler_params=_params(1, False))(a16, w, b)


def _w_mod_update(a16, d16, w, m, v):
    D, n = w.shape
    tn = _tile(n, 256)

    def body(a_ref, d_ref, w_ref, m_ref, v_ref, g_ref, dl_ref, m2_ref, v2_ref, p_ref):
        @pl.when(pl.program_id(0) == 0)
        def _():
            p_ref[...] = jnp.zeros_like(p_ref)
        a = a_ref[...]
        db = d_ref[...].astype(BF16)
        wv = w_ref[...]
        g = _tn((a * _sigmoid(a)).astype(BF16), db)
        g_ref[...] = g
        d, m2, v2 = _adam_math(wv, g, m_ref[...], v_ref[...])
        dl_ref[...] = d
        m2_ref[...] = m2
        v2_ref[...] = v2
        p_ref[...] += _nt(db, wv.astype(BF16))

    wspec = pl.BlockSpec((D, tn), lambda j: (0, j))
    shp = jax.ShapeDtypeStruct((D, n), F32)
    return pl.pallas_call(
        body, name="w_mod_update", grid=(n // tn,),
        in_specs=[pl.BlockSpec((16, D), lambda j: (0, 0)), pl.BlockSpec((16, tn), lambda j: (0, j)), wspec, wspec, wspec],
        out_specs=[wspec, wspec, wspec, wspec, pl.BlockSpec((16, D), lambda j: (0, 0))],
        out_shape=[shp, shp, shp, shp, jax.ShapeDtypeStruct((16, D), F32)],
        compiler_params=_params(1))(a16, d16, w, m, v)


def _sum_devices(g8, name):
    _, R, Cc = g8.shape

    def body(g_ref, o_ref):
        t = g_ref[0]
        for d in range(1, N_DEV):
            t = t + g_ref[d]
        o_ref[...] = t

    return pl.pallas_call(body, name=name, out_shape=jax.ShapeDtypeStruct((R, Cc), F32))(g8)


def _c_ctx_grad(parts, c_ctx):
    D = c_ctx.shape[1]

    def body(p_ref, c_ref, o_ref):
        t = p_ref[0]
        for k in range(1, N_CHIPS):
            t = t + p_ref[2 * k]
        cv = c_ref[...]
        sg = _sigmoid(cv)
        o_ref[...] = t * (sg * (1.0 + cv * (1.0 - sg)))

    return pl.pallas_call(body, name="c_ctx_grad", out_shape=jax.ShapeDtypeStruct((1, D), F32))(parts, c_ctx)


def _pack_rows(items, nrows, width, name):
    arrays, plan = [], []
    for a, r0, nr, d0, c0 in items:
        for ai, b in enumerate(arrays):
            if b is a:
                break
        else:
            ai = len(arrays)
            arrays.append(a)
        plan.append((ai, r0, nr, d0, c0, a.shape[1]))

    def body(*refs):
        o_ref = refs[-1]
        o_ref[...] = jnp.zeros_like(o_ref)
        for ai, r0, nr, d0, c0, w in plan:
            o_ref[d0:d0 + nr, c0:c0 + w] = refs[ai][r0:r0 + nr, :]

    return pl.pallas_call(body, name=name, out_shape=jax.ShapeDtypeStruct((nrows, width), F32))(*arrays)


HBM_SPEC = pl.BlockSpec(memory_space=pltpu.HBM)
SEM_SPEC = pl.BlockSpec(memory_space=pltpu.SEMAPHORE)
SPLIT_PARAMS = pltpu.CompilerParams(has_side_effects=pltpu.SideEffectType.DATAFLOW_SIDE_EFFECTING)


def _in_hbm(a):
    return pltpu.with_memory_space_constraint(a, pltpu.HBM)


def _ag_chips_start(fulls, shapes, axes, after, name):
    nw = len(fulls)

    def body(*refs):
        in_refs, send, recv, token = refs[:nw], refs[nw + 1], refs[nw + 2], refs[-1]
        x, y, c = _my_pos()
        k0 = 2 * x + y
        for w in range(nw):
            own = _region(in_refs[w], k0, c, shapes[w], axes[w])
            for j, ch in enumerate(_other_chips(x, y)):
                _remote(own, own, send.at[3 * w + j], recv.at[3 * w + j], (ch[0], ch[1], c)).start()
        token[...] = jnp.zeros_like(token)

    return pl.pallas_call(
        body, name=name,
        out_shape=(pltpu.SemaphoreType.DMA((3 * nw,)), pltpu.SemaphoreType.DMA((3 * nw,)),
                   *[pltpu.HBM(f.shape, f.dtype) for f in fulls], jax.ShapeDtypeStruct((8, LANES), F32)),
        in_specs=[HBM_SPEC] * nw + [pl.BlockSpec(memory_space=pl.ANY)],
        out_specs=(SEM_SPEC, SEM_SPEC, *[HBM_SPEC] * nw, pl.BlockSpec(memory_space=pltpu.VMEM)),
        input_output_aliases={w: 2 + w for w in range(nw)},
        compiler_params=SPLIT_PARAMS)(*[_in_hbm(f) for f in fulls], after)


def _ag_chips_wait(send, recv, fulls, shapes, axes, after, name):
    nw = len(fulls)

    def body(*refs):
        in_refs, send_ref, recv_ref = refs[:nw], refs[nw], refs[nw + 1]
        x, y, c = _my_pos()
        k0 = 2 * x + y
        for w in range(nw):
            own = _region(in_refs[w], k0, c, shapes[w], axes[w])
            for j, ch in enumerate(_other_chips(x, y)):
                got = _region(in_refs[w], 2 * ch[0] + ch[1], c, shapes[w], axes[w])
                cp = _remote(own, got, send_ref.at[3 * w + j], recv_ref.at[3 * w + j], (ch[0], ch[1], c))
                cp.wait_send()
                cp.wait_recv()

    return pl.pallas_call(
        body, name=name,
        out_shape=tuple(pltpu.HBM(f.shape, f.dtype) for f in fulls),
        in_specs=[HBM_SPEC] * nw + [SEM_SPEC, SEM_SPEC, pl.BlockSpec(memory_space=pl.ANY)],
        out_specs=tuple([HBM_SPEC] * nw),
        input_output_aliases={w: w for w in range(nw)},
        compiler_params=SPLIT_PARAMS)(*fulls, send, recv, after)


def _ag_forward(fulls, shapes, axes, name):
    nw = len(fulls)

    def body(*refs):
        out_refs = refs[nw:2 * nw]
        send, recv = refs[2 * nw:]
        x, y, c = _my_pos()
        sib = (x, y, 1 - c)
        chips = _other_chips(x, y)
        copies = []
        for w in range(nw):
            for j, ch in enumerate(chips):
                got = _region(out_refs[w], 2 * ch[0] + ch[1], c, shapes[w], axes[w])
                cp = _remote(got, got, send.at[w, j], recv.at[w, j], sib)
                cp.start()
                copies.append(cp)
        for w in range(nw):
            for j, ch in enumerate(chips):
                got = _region(out_refs[w], 2 * ch[0] + ch[1], 1 - c, shapes[w], axes[w])
                _remote(got, got, send.at[w, j], recv.at[w, j], sib).wait_recv()
        for cp in copies:
            cp.wait_send()

    anyspec = pl.BlockSpec(memory_space=pl.ANY)
    return pl.pallas_call(
        body, name=name,
        out_shape=[jax.ShapeDtypeStruct(f.shape, BF16) for f in fulls],
        in_specs=[anyspec] * nw, out_specs=[anyspec] * nw,
        input_output_aliases={w: w for w in range(nw)},
        scratch_shapes=[pltpu.SemaphoreType.DMA((nw, 3)), pltpu.SemaphoreType.DMA((nw, 3))])(*fulls)


def _rs_sibling_start(grads, shapes, axes, name):
    nw = len(grads)
    npc = max(1 if a == 1 else N_CHIPS for a in axes)

    def body(*refs):
        g_refs, l_refs, send, recv, token = refs[:nw], refs[nw:2 * nw], refs[2 * nw], refs[2 * nw + 1], refs[-1]
        x, y, c = _my_pos()
        for w in range(nw):
            src = _half_pieces(g_refs[w], 1 - c, shapes[w], axes[w])
            dst = _half_pieces(l_refs[w], 1 - c, shapes[w], axes[w])
            for i, (s, d) in enumerate(zip(src, dst)):
                _remote(s, d, send.at[npc * w + i], recv.at[npc * w + i], (x, y, 1 - c)).start()
        token[...] = jnp.zeros_like(token)

    thru = [pltpu.HBM(g.shape, g.dtype) for g in grads]
    return pl.pallas_call(
        body, name=name,
        out_shape=(pltpu.SemaphoreType.DMA((npc * nw,)), pltpu.SemaphoreType.DMA((npc * nw,)), *thru, *thru,
                   jax.ShapeDtypeStruct((8, LANES), F32)),
        in_specs=[HBM_SPEC] * (2 * nw),
        out_specs=(SEM_SPEC, SEM_SPEC, *[HBM_SPEC] * (2 * nw), pl.BlockSpec(memory_space=pltpu.VMEM)),
        input_output_aliases={i: 2 + i for i in range(2 * nw)},
        compiler_params=SPLIT_PARAMS)(*[_in_hbm(g) for g in grads], *[_in_hbm(lax.empty(g.shape, g.dtype)) for g in grads])


def _rs_sibling_wait(send, recv, grads, lands, shapes, axes, after, name):
    nw = len(grads)
    npc = max(1 if a == 1 else N_CHIPS for a in axes)

    def body(*refs):
        g_refs, l_refs, send_ref, recv_ref = refs[:nw], refs[nw:2 * nw], refs[2 * nw], refs[2 * nw + 1]
        x, y, c = _my_pos()
        for w in range(nw):
            sent = _half_pieces(g_refs[w], 1 - c, shapes[w], axes[w])
            mine = _half_pieces(l_refs[w], c, shapes[w], axes[w])
            for i, (s, d) in enumerate(zip(sent, mine)):
                cp = _remote(s, d, send_ref.at[npc * w + i], recv_ref.at[npc * w + i], (x, y, 1 - c))
                cp.wait_send()
                cp.wait_recv()

    thru = tuple(pltpu.HBM(g.shape, g.dtype) for g in grads)
    return pl.pallas_call(
        body, name=name, out_shape=thru + thru,
        in_specs=[HBM_SPEC] * (2 * nw) + [SEM_SPEC, SEM_SPEC, pl.BlockSpec(memory_space=pl.ANY)],
        out_specs=tuple([HBM_SPEC] * (2 * nw)),
        input_output_aliases={i: i for i in range(2 * nw)},
        compiler_params=SPLIT_PARAMS)(*grads, *lands, send, recv, after)


def _rs_chips_start(sums, name):
    nw = len(sums)

    def body(*refs):
        s_refs, l_refs, send, recv, token = refs[:nw], refs[nw:2 * nw], refs[2 * nw], refs[2 * nw + 1], refs[-1]
        x, y, c = _my_pos()
        k0 = 2 * x + y
        for w in range(nw):
            for j, ch in enumerate(_other_chips(x, y)):
                _remote(s_refs[w].at[2 * ch[0] + ch[1]], l_refs[w].at[k0], send.at[3 * w + j], recv.at[3 * w + j],
                        (ch[0], ch[1], c)).start()
        token[...] = jnp.zeros_like(token)

    thru = [pltpu.HBM(s.shape, s.dtype) for s in sums]
    return pl.pallas_call(
        body, name=name,
        out_shape=(pltpu.SemaphoreType.DMA((3 * nw,)), pltpu.SemaphoreType.DMA((3 * nw,)), *thru, *thru,
                   jax.ShapeDtypeStruct((8, LANES), F32)),
        in_specs=[HBM_SPEC] * (2 * nw),
        out_specs=(SEM_SPEC, SEM_SPEC, *[HBM_SPEC] * (2 * nw), pl.BlockSpec(memory_space=pltpu.VMEM)),
        input_output_aliases={i: 2 + i for i in range(2 * nw)},
        compiler_params=SPLIT_PARAMS)(*[_in_hbm(s) for s in sums], *[_in_hbm(lax.empty(s.shape, s.dtype)) for s in sums])


def _rs_chips_wait(send, recv, sums, lands, after, name):
    nw = len(sums)

    def body(*refs):
        s_refs, l_refs, send_ref, recv_ref = refs[:nw], refs[nw:2 * nw], refs[2 * nw], refs[2 * nw + 1]
        x, y, c = _my_pos()
        for w in range(nw):
            for j, ch in enumerate(_other_chips(x, y)):
                kj = 2 * ch[0] + ch[1]
                cp = _remote(s_refs[w].at[kj], l_refs[w].at[kj], send_ref.at[3 * w + j], recv_ref.at[3 * w + j],
                             (ch[0], ch[1], c))
                cp.wait_send()
                cp.wait_recv()

    thru = tuple(pltpu.HBM(s.shape, s.dtype) for s in sums)
    return pl.pallas_call(
        body, name=name, out_shape=thru + thru,
        in_specs=[HBM_SPEC] * (2 * nw) + [SEM_SPEC, SEM_SPEC, pl.BlockSpec(memory_space=pl.ANY)],
        out_specs=tuple([HBM_SPEC] * (2 * nw)),
        input_output_aliases={i: i for i in range(2 * nw)},
        compiler_params=SPLIT_PARAMS)(*sums, *lands, send, recv, after)


LOSS_LANE = 64


def kernel(x, c, ctx, c_ctx, w_mod, b_mod, norm_mix, norm_ffn, w_in, ret_decay, attn_sink, w_out, w_gate, w_up, w_down, norm_final, loss_target, m_c_ctx, m_w_mod, m_b_mod, m_norm_mix, m_norm_ffn, m_w_in, m_ret_decay, m_attn_sink, m_w_out, m_w_gate, m_w_up, m_w_down, m_norm_final, v_c_ctx, v_w_mod, v_b_mod, v_norm_mix, v_norm_ffn, v_w_in, v_ret_decay, v_attn_sink, v_w_out, v_w_gate, v_w_up, v_w_down, v_norm_final):
    D = x.shape[-1]
    n3 = w_mod.shape[-1]
    xi, yi, ci = _my_pos()
    b = 4 * xi + 2 * yi + ci
    k0 = 2 * xi + yi
    cvec = jnp.reshape(ci, (1,)).astype(jnp.int32)
    kc = jnp.stack([k0, ci]).astype(jnp.int32)

    dense = [("w_in", w_in[0], 1), ("w_out", w_out[0], 0), ("w_gate", w_gate[0], 1), ("w_up", w_up[0], 1),
             ("w_down", w_down[0], 0)]
    axes = [a for _, _, a in dense]
    shapes = [w.shape for _, w, _ in dense]
    c_all = _allgather8(c, "gather_c").reshape(N_DEV, D)
    c_ctx2 = c_ctx.reshape(1, D)
    a16 = _pack_rows([(c_all, 0, N_DEV, 0, 0), (c_ctx2, 0, 1, N_DEV, 0)], 16, D, "pack_cond")
    b_cols = lax.dynamic_slice_in_dim(b_mod, k0 * n3, n3, axis=1)
    mod16 = _mod_rows(a16, w_mod[0], b_cols, "mod_rows")
    mod_all = _allgather8(mod16, "gather_mod")

    own_in = _cast_into_full(dense[0][1], kc, axes[0], "cast_w_in")
    agi = _ag_chips_start([own_in], shapes[:1], axes[:1], mod_all, "ag_in_start")
    own16 = [_cast_into_full(w, kc, a, "cast_" + n) for n, w, a in dense[1:]]
    (f_in,) = _ag_forward(list(_ag_chips_wait(agi[0], agi[1], [agi[2]], shapes[:1], axes[:1], own16[-1], "ag_in_wait")),
                          shapes[:1], axes[:1], "ag_in_forward")
    ag = _ag_chips_start(own16, shapes[1:], axes[1:], f_in, "ag_rest_start")
    ag_send, ag_recv, ag_thru, ag_tok = ag[0], ag[1], list(ag[2:-1]), ag[-1][0:1, 0:1]

    def rest_weights(after):
        landed_w = _ag_chips_wait(ag_send, ag_recv, ag_thru, shapes[1:], axes[1:], after, "ag_rest_wait")
        return _ag_forward(list(landed_w), shapes[1:], axes[1:], "ag_rest_forward")
    mine = jnp.stack([lax.dynamic_index_in_dim(mod_all, 2 * k + ci, 0, keepdims=False) for k in range(N_CHIPS)])
    mod = lax.dynamic_index_in_dim(mine, b, 1, keepdims=False).reshape(6, D)
    modc = mine[:, N_DEV].reshape(6, D)

    lg = -jnp.exp(ret_decay[0])

    index = {n: i for i, (n, _, _) in enumerate(dense)}
    pending, done = [], {}

    sib = []

    def finish_sibling(after):
        names, shp, axs, st = sib.pop()
        nw = len(names)
        res = _rs_sibling_wait(st[0], st[1], list(st[2:2 + nw]), list(st[2 + nw:2 + 2 * nw]), shp, axs, after,
                               "rs_sibling_wait_" + names[0])
        sums = [_add_halves(res[i], res[nw + i], cvec, s, a, "add_halves_" + n)
                for i, (s, a, n) in enumerate(zip(shp, axs, names))]
        ch = _rs_chips_start(sums, "rs_chips_start_" + names[0])
        pending.append((names, ch[0], ch[1], list(ch[2:2 + nw]), list(ch[2 + nw:2 + 2 * nw])))
        return ch[-1][0:1, 0:1]

    def on_grads(names, gs):
        ids = [index[n] for n in names]
        shp, axs = [shapes[i] for i in ids], [axes[i] for i in ids]
        st = _rs_sibling_start(gs, shp, axs, "rs_sibling_start_" + names[0])
        sib.append((names, shp, axs, st))
        return st[-1][0:1, 0:1]

    out = _local_step(x[0], ctx[0], loss_target[0], mod, modc, norm_mix + ag_tok, norm_ffn, norm_final.reshape(1, D), lg,
                      attn_sink, f_in, rest_weights, on_grads, finish_sibling)

    def finish(group, after):
        names, send, recv, sums, lands = group
        res = _rs_chips_wait(send, recv, sums, lands, after, "rs_chips_wait_" + names[0])
        return [_sum_chips(res[i], res[len(names) + i], kc, "sum_chips_" + n) for i, n in enumerate(names)]

    finish_sibling(out["grad_x"])
    assert pending[-1][0] == ["w_in"]
    rest_names = [n for g in pending[:-1] for n in g[0]]
    rest_halves = [h for g in pending[:-1] for h in finish(g, out["grad_x"])]
    g_rest = dict(zip(rest_names, _exchange_halves(rest_halves, "exchange_halves_rest")))

    nh = 2 * RET_HEADS
    small_all = _allgather8(out["small"], "gather_small")
    tot = _sum_devices(small_all, "sum_small")
    g_b_mod = (tot[0:6] + tot[6:12]).reshape(1, 6 * D)
    dmodc_tot = tot[6:12].reshape(1, 6 * D)
    dmod_rows = small_all[:, 0:6].reshape(N_DEV, 6 * D)
    d16 = _pack_rows([(dmod_rows, 0, N_DEV, 0, 0), (dmodc_tot, 0, 1, N_DEV, 0)], 16, 6 * D, "pack_dmod")
    d16 = lax.dynamic_slice_in_dim(d16, k0 * n3, n3, axis=1)
    g_w_mod, dl_w_mod, m2_w_mod, v2_w_mod, part = _w_mod_update(a16, d16, w_mod[0], m_w_mod[0], v_w_mod[0])
    part_all = _allgather8(part[N_DEV:N_DEV + 1], "gather_c_ctx")
    g_c_ctx = _c_ctx_grad(part_all, c_ctx2)
    loss = tot[15, LOSS_LANE]
    g_ret_decay = tot[15, :nh].reshape(1, 2, RET_HEADS)
    g_sink = tot[15, nh:nh + ATT_HEADS].reshape(1, ATT_HEADS)

    def pack(cc, bm, nm, nf, nfin, rd, sk, name):
        rd2 = rd.reshape(2, RET_HEADS)
        return _pack_rows([(bm.reshape(6, D), 0, 6, 0, 0), (cc.reshape(1, D), 0, 1, 6, 0), (nm.reshape(1, D), 0, 1, 7, 0),
                           (nf.reshape(1, D), 0, 1, 8, 0), (nfin.reshape(1, D), 0, 1, 9, 0),
                           (rd2, 0, 1, 10, 0), (rd2, 1, 1, 10, RET_HEADS), (sk.reshape(1, ATT_HEADS), 0, 1, 10, nh)],
                          16, D, name)

    w_s = pack(c_ctx, b_mod, norm_mix, norm_ffn, norm_final, ret_decay, attn_sink, "pack_w")
    g_s = _pack_rows([(g_b_mod.reshape(6, D), 0, 6, 0, 0), (g_c_ctx, 0, 1, 6, 0), (tot, 12, 3, 7, 0),
                      (tot[15:16, 0:nh + ATT_HEADS], 0, 1, 10, 0)], 16, D, "pack_g")
    m_s = pack(m_c_ctx, m_b_mod, m_norm_mix, m_norm_ffn, m_norm_final, m_ret_decay, m_attn_sink, "pack_m")
    v_s = pack(v_c_ctx, v_b_mod, v_norm_mix, v_norm_ffn, v_norm_final, v_ret_decay, v_attn_sink, "pack_v")
    small_upd = _adam(w_s, g_s, m_s, v_s, "adam_small")

    def unpack(t):
        return dict(b_mod=t[0:6].reshape(1, 6 * D), c_ctx=t[6], norm_mix=t[7:8], norm_ffn=t[8:9], norm_final=t[9],
                    ret_decay=t[10, :nh].reshape(1, 2, RET_HEADS), attn_sink=t[10, nh:nh + ATT_HEADS].reshape(1, ATT_HEADS))

    dense_w = dict(w_in=(w_in, m_w_in, v_w_in), w_out=(w_out, m_w_out, v_w_out), w_gate=(w_gate, m_w_gate, v_w_gate),
                   w_up=(w_up, m_w_up, v_w_up), w_down=(w_down, m_w_down, v_w_down))
    grads = dict(unpack(g_s), w_mod=g_w_mod[None])
    upd = [dict(unpack(t)) for t in small_upd]
    upd[0]["w_mod"], upd[1]["w_mod"], upd[2]["w_mod"] = dl_w_mod[None], m2_w_mod[None], v2_w_mod[None]
    def update(n, g):
        w_, m_, v_ = dense_w[n]
        res = _adam(w_[0], g, m_[0], v_[0], "adam_" + n)
        grads[n] = g[None]
        for u, r_ in zip(upd, res):
            u[n] = r_[None]
        return res[0]

    dep = small_upd[0][0:1, 0:1] + dl_w_mod[0:1, 0:1]
    for n in rest_names:
        dep = dep + update(n, g_rest[n])[0:1, 0:1]
    (g_in,) = _exchange_halves(finish(pending[-1], dep), "exchange_halves_in")
    update("w_in", g_in)

    order = ['c_ctx', 'w_mod', 'b_mod', 'norm_mix', 'norm_ffn', 'w_in', 'ret_decay', 'attn_sink', 'w_out', 'w_gate',
             'w_up', 'w_down', 'norm_final']
    outs = [loss, out["grad_x"][None]] + [grads[n] for n in order]
    for u in upd:
        outs += [u[n] for n in order]
    return tuple(outs)
```

```python
import functools
import numpy as np
import jax
import jax.numpy as jnp
from jax import lax
from jax.experimental import pallas as pl
from jax.experimental.pallas import tpu as pltpu

F32 = jnp.float32
BF16 = jnp.bfloat16

RET_HEADS = 8
RET_DK = 64
RET_DV = 128
CHUNK = 128
ATT_HEADS = 16
ATT_KV_HEADS = 4
ATT_DH = 64
GRID_W = 64
ROPE_BASE = 10000.0
NORM_EPS = 1e-6
ADAM_LR = 0.001
ADAM_B1 = 0.9
ADAM_B2 = 0.999
ADAM_EPS = 1e-08
ADAM_WD = 0.01
ADAM_STEP = 10
NEG = -1e30
LANES = 128
VMEM_LIMIT = 56 * 1024 * 1024
ROWS_PER_LATCH = 1024
MESH = pl.DeviceIdType.MESH
N_CHIPS = 4
N_DEV = 8


def _nn(a, b):
    return jnp.dot(a, b, preferred_element_type=F32)


def _nt(a, b):
    return lax.dot_general(a, b, (((1,), (1,)), ((), ())), preferred_element_type=F32)


def _tn(a, b):
    return lax.dot_general(a, b, (((0,), (0,)), ((), ())), preferred_element_type=F32)


def _tile(n, pref, unit=LANES):
    t = min(n, pref)
    t -= t % unit
    while t > unit and n % t:
        t -= unit
    if t <= 0 or n % t:
        return n
    return t


def _params(ndim, vmem=True):
    return pltpu.CompilerParams(dimension_semantics=("arbitrary",) * ndim,
                                vmem_limit_bytes=VMEM_LIMIT if vmem else None)


def _sigmoid(x):
    return 0.5 * jnp.tanh(0.5 * x) + 0.5


def _fsum(x):
    return jnp.sum(jnp.sum(x, axis=0, keepdims=True), axis=1, keepdims=True)


def _rope_tables(L):
    lane = np.arange(LANES)
    d = lane % 64
    inv_r = jnp.asarray(ROPE_BASE, F32) ** (-jnp.arange(32, dtype=F32) / 32)
    t = jnp.arange(L)
    ang_r = t.astype(F32)[:, None] * jnp.tile(inv_r, LANES // 32)[None, :]
    Rr = np.zeros((LANES, LANES), np.float32)
    for l in range(LANES):
        if d[l] < 32:
            Rr[l + 32, l] = -1.0
        else:
            Rr[l - 32, l] = 1.0
    inv_a = jnp.asarray(ROPE_BASE, F32) ** (-jnp.arange(16, dtype=F32) / 16)
    rows = (t // GRID_W).astype(F32)
    cols = (t % GRID_W).astype(F32)
    dd = d % 32
    pos = jnp.where(jnp.asarray(d < 32)[None, :], rows[:, None], cols[:, None])
    ang_a = pos * jnp.tile(inv_a, LANES // 16)[None, :]
    Ra = np.zeros((LANES, LANES), np.float32)
    for l in range(LANES):
        if dd[l] < 16:
            Ra[l + 16, l] = -1.0
        else:
            Ra[l - 16, l] = 1.0
    D0 = np.zeros((LANES, LANES), np.float32)
    D1 = np.zeros((LANES, LANES), np.float32)
    for l in range(LANES):
        D0[l % 64, l] = 1.0
        D1[64 + l % 64, l] = 1.0
    return dict(
        Cr=jnp.cos(ang_r), Sr=jnp.sin(ang_r), Rr=jnp.asarray(Rr, BF16), RrT=jnp.asarray(Rr.T, BF16),
        Ca=jnp.cos(ang_a), Sa=jnp.sin(ang_a), Ra=jnp.asarray(Ra, BF16), RaT=jnp.asarray(Ra.T, BF16),
        D0=jnp.asarray(D0, BF16), D1=jnp.asarray(D1, BF16),
        D0T=jnp.asarray(D0.T, BF16), D1T=jnp.asarray(D1.T, BF16))


def _norm_mod(xf, g, sh, sc):
    r = lax.rsqrt(jnp.mean(xf * xf, axis=-1, keepdims=True) + NORM_EPS)
    return (xf * r * g) * (1.0 + sc) + sh


def _norm_mod_matmul(x, g, sh, sc, w, name):
    M, D = x.shape
    N = w.shape[1]
    tm, tn = _tile(M, ROWS_PER_LATCH, 8), _tile(N, 768)

    def body(x_ref, g_ref, sh_ref, sc_ref, w_ref, p_ref, h_ref, hs):
        @pl.when(pl.program_id(1) == 0)
        def _():
            hb = _norm_mod(x_ref[...], g_ref[...], sh_ref[...], sc_ref[...]).astype(BF16)
            hs[...] = hb
            h_ref[...] = hb
        p_ref[...] = _nn(hs[...], w_ref[...]).astype(BF16)

    vec = pl.BlockSpec((1, D), lambda i, j: (0, 0))
    return pl.pallas_call(
        body, name=name, grid=(M // tm, N // tn),
        in_specs=[pl.BlockSpec((tm, D), lambda i, j: (i, 0)), vec, vec, vec,
                  pl.BlockSpec((D, tn), lambda i, j: (0, j))],
        out_specs=[pl.BlockSpec((tm, tn), lambda i, j: (i, j)), pl.BlockSpec((tm, D), lambda i, j: (i, 0))],
        out_shape=[jax.ShapeDtypeStruct((M, N), BF16), jax.ShapeDtypeStruct((M, D), BF16)],
        scratch_shapes=[pltpu.VMEM((tm, D), BF16)],
        compiler_params=_params(2))(x, g, sh, sc, w)


def _proj_residual(a, w, xres, gt, name):
    M, K = a.shape
    N = w.shape[1]
    tm, tn = _tile(M, ROWS_PER_LATCH, 8), _tile(N, 1024 if K <= 2048 else 512)

    def body(a_ref, w_ref, x_ref, gt_ref, xo_ref, o_ref):
        o = _nn(a_ref[...], w_ref[...])
        o_ref[...] = o.astype(BF16)
        xo_ref[...] = x_ref[...] + gt_ref[...] * o

    return pl.pallas_call(
        body, name=name, grid=(M // tm, N // tn),
        in_specs=[pl.BlockSpec((tm, K), lambda i, j: (i, 0)), pl.BlockSpec((K, tn), lambda i, j: (0, j)),
                  pl.BlockSpec((tm, tn), lambda i, j: (i, j)), pl.BlockSpec((1, tn), lambda i, j: (0, j))],
        out_specs=[pl.BlockSpec((tm, tn), lambda i, j: (i, j)), pl.BlockSpec((tm, tn), lambda i, j: (i, j))],
        out_shape=[jax.ShapeDtypeStruct((M, N), F32), jax.ShapeDtypeStruct((M, N), BF16)],
        compiler_params=_params(2))(a, w, xres, gt)


def _ffn_in(x1, g, sh, sc, wg, wu):
    M, D = x1.shape
    N = wg.shape[1]
    tm, tn = _tile(M, ROWS_PER_LATCH, 8), _tile(N, 512)

    def body(x_ref, g_ref, sh_ref, sc_ref, wg_ref, wu_ref, G_ref, U_ref, A_ref, h_ref, hs):
        @pl.when(pl.program_id(1) == 0)
        def _():
            hb = _norm_mod(x_ref[...], g_ref[...], sh_ref[...], sc_ref[...]).astype(BF16)
            hs[...] = hb
            h_ref[...] = hb
        G = _nn(hs[...], wg_ref[...])
        U = _nn(hs[...], wu_ref[...])
        G_ref[...] = G.astype(BF16)
        U_ref[...] = U.astype(BF16)
        A_ref[...] = (G * _sigmoid(G) * U).astype(BF16)

    vec = pl.BlockSpec((1, D), lambda i, j: (0, 0))
    wspec = pl.BlockSpec((D, tn), lambda i, j: (0, j))
    ospec = pl.BlockSpec((tm, tn), lambda i, j: (i, j))
    big = jax.ShapeDtypeStruct((M, N), BF16)
    return pl.pallas_call(
        body, name="ffn_in", grid=(M // tm, N // tn),
        in_specs=[pl.BlockSpec((tm, D), lambda i, j: (i, 0)), vec, vec, vec, wspec, wspec],
        out_specs=[ospec, ospec, ospec, pl.BlockSpec((tm, D), lambda i, j: (i, 0))],
        out_shape=[big, big, big, jax.ShapeDtypeStruct((M, D), BF16)],
        scratch_shapes=[pltpu.VMEM((tm, D), BF16)],
        compiler_params=_params(2))(x1, g, sh, sc, wg, wu)


def _final(x2, gn, tgt, gt, saved):
    M, D = x2.shape
    tm = _tile(M, 256, 8)

    def body(x_ref, g_ref, t_ref, gt_ref, sv_ref, dx_ref, loss_ref, dg_ref, dz_ref, dgt_ref):
        @pl.when(pl.program_id(0) == 0)
        def _():
            loss_ref[...] = jnp.zeros_like(loss_ref)
            dg_ref[...] = jnp.zeros_like(dg_ref)
            dgt_ref[...] = jnp.zeros_like(dgt_ref)
        x = x_ref[...]
        g = g_ref[...]
        r = lax.rsqrt(jnp.mean(x * x, axis=-1, keepdims=True) + NORM_EPS)
        xh = x * r
        e = xh * g - t_ref[...]
        loss_ref[...] += (0.5 / D) * _fsum(e * e)
        dy = e * (1.0 / D)
        dg_ref[...] += jnp.sum(dy * xh, axis=0, keepdims=True)
        dxh = dy * g
        d = r * (dxh - xh * jnp.mean(dxh * xh, axis=-1, keepdims=True))
        dx_ref[...] = d
        dz_ref[...] = (d * gt_ref[...]).astype(BF16)
        dgt_ref[...] += jnp.sum(d * sv_ref[...].astype(F32), axis=0, keepdims=True)

    row = pl.BlockSpec((tm, D), lambda i: (i, 0))
    vec = pl.BlockSpec((1, D), lambda i: (0, 0))
    return pl.pallas_call(
        body, name="final_loss", grid=(M // tm,),
        in_specs=[row, vec, row, vec, row],
        out_specs=[row, pl.BlockSpec((1, LANES), lambda i: (0, 0)), vec, row, vec],
        out_shape=[jax.ShapeDtypeStruct((M, D), F32), jax.ShapeDtypeStruct((1, LANES), F32),
                   jax.ShapeDtypeStruct((1, D), F32), jax.ShapeDtypeStruct((M, D), BF16),
                   jax.ShapeDtypeStruct((1, D), F32)],
        compiler_params=_params(1))(x2, gn, tgt, gt, saved)


def _col_group(blk0, nblk):
    return int(np.gcd(blk0, nblk)) if blk0 else nblk


def _rope_cols(src, blk0, nblk, Ct, St, R, scale, rope, name):
    M = src.shape[0]
    tm = _tile(M, 512, 8)
    wb = _col_group(blk0, nblk)

    def body(x_ref, c_ref, s_ref, r_ref, o_ref):
        for j in range(wb):
            cols = slice(j * LANES, (j + 1) * LANES)
            x = x_ref[:, cols]
            xf = x.astype(F32)
            if rope:
                xf = xf * c_ref[...] + _nn(x.astype(BF16), r_ref[...]) * s_ref[...]
            o_ref[:, cols] = (xf * scale).astype(BF16)

    tab = pl.BlockSpec((tm, LANES), lambda i, j: (i, 0))
    return pl.pallas_call(
        body, name=name, grid=(M // tm, nblk // wb),
        in_specs=[pl.BlockSpec((tm, wb * LANES), lambda i, j: (i, blk0 // wb + j)), tab, tab,
                  pl.BlockSpec((LANES, LANES), lambda i, j: (0, 0))],
        out_specs=pl.BlockSpec((tm, wb * LANES), lambda i, j: (i, j)),
        out_shape=jax.ShapeDtypeStruct((M, nblk * LANES), BF16),
        compiler_params=_params(2, False))(src, Ct, St, R)


def _dup_heads(src, blk0, npair, Ct, St, R, D0, D1, rope, name):
    M = src.shape[0]
    tm = _tile(M, 512, 8)

    def body(x_ref, c_ref, s_ref, r_ref, d0_ref, d1_ref, o_ref):
        x = x_ref[...]
        if rope:
            x = (x.astype(F32) * c_ref[...] + _nn(x, r_ref[...]) * s_ref[...]).astype(BF16)
        o_ref[0] = _nn(x, d0_ref[...]).astype(BF16)
        o_ref[1] = _nn(x, d1_ref[...]).astype(BF16)

    tab = pl.BlockSpec((tm, LANES), lambda i, p: (i, 0))
    mat = pl.BlockSpec((LANES, LANES), lambda i, p: (0, 0))
    return pl.pallas_call(
        body, name=name, grid=(M // tm, npair),
        in_specs=[pl.BlockSpec((tm, LANES), lambda i, p: (i, blk0 + p)), tab, tab, mat, mat, mat],
        out_specs=pl.BlockSpec((2, tm, LANES), lambda i, p: (p, i, 0)),
        out_shape=jax.ShapeDtypeStruct((2 * npair, M, LANES), BF16),
        compiler_params=_params(2, False))(src, Ct, St, R, D0, D1)


def _unrope_cols(dsrc, dst, blk0, nblk, Ct, St, RT, scale, rope, name):
    M = dsrc.shape[0]
    tm = _tile(M, 512, 8)
    wb = _col_group(blk0, nblk)

    def body(x_ref, c_ref, s_ref, r_ref, dst_ref, o_ref):
        del dst_ref
        for j in range(wb):
            cols = slice(j * LANES, (j + 1) * LANES)
            xf = x_ref[:, cols].astype(F32)
            if rope:
                xf = xf * c_ref[...] + _nn((xf * s_ref[...]).astype(BF16), r_ref[...])
            o_ref[:, cols] = (xf * scale).astype(BF16)

    tab = pl.BlockSpec((tm, LANES), lambda i, j: (i, 0))
    return pl.pallas_call(
        body, name=name, grid=(M // tm, nblk // wb),
        in_specs=[pl.BlockSpec((tm, wb * LANES), lambda i, j: (i, j)), tab, tab,
                  pl.BlockSpec((LANES, LANES), lambda i, j: (0, 0)),
                  pl.BlockSpec(memory_space=pl.ANY)],
        out_specs=pl.BlockSpec((tm, wb * LANES), lambda i, j: (i, blk0 // wb + j)),
        out_shape=jax.ShapeDtypeStruct(dst.shape, dst.dtype),
        input_output_aliases={4: 0},
        compiler_params=_params(2, False))(dsrc, Ct, St, RT, dst)


def _fold_heads(parts, dst, blk0, npair, Ct, St, RT, D0T, D1T, rope, name):
    M = parts[0][0].shape[1]
    nb = M // CHUNK
    R = _tile(M, 1024, CHUNK)
    rb = R // CHUNK
    nrefs = sum(1 if s == 0 else 2 for _, s in parts)

    def body(*refs):
        part_refs = list(refs[:nrefs])
        c_ref, s_ref, r_ref, d0_ref, d1_ref, dst_ref, o_ref = refs[nrefs:]
        del dst_ref
        i = pl.program_id(0)
        tot = [jnp.zeros((R, LANES), F32), jnp.zeros((R, LANES), F32)]
        for _, shift in parts:
            main = part_refs.pop(0)
            if shift == 0:
                for e in range(2):
                    tot[e] = tot[e] + main[e].astype(F32)
                continue
            edge = part_refs.pop(0)
            ok = (i + 1) * rb <= nb - 1 if shift > 0 else i > 0
            for e in range(2):
                ed = jnp.where(ok, edge[e].astype(F32), 0.0)
                if rb == 1:
                    tot[e] = tot[e] + ed
                elif shift > 0:
                    tot[e] = tot[e] + jnp.concatenate([main[e, CHUNK:, :].astype(F32), ed], axis=0)
                else:
                    tot[e] = tot[e] + jnp.concatenate([ed, main[e, :R - CHUNK, :].astype(F32)], axis=0)
        f = _nn(tot[0].astype(BF16), d0_ref[...]) + _nn(tot[1].astype(BF16), d1_ref[...])
        if rope:
            f = f * c_ref[...] + _nn((f * s_ref[...]).astype(BF16), r_ref[...])
        o_ref[...] = f.astype(BF16)

    in_specs, args = [], []
    for a, shift in parts:
        assert shift in (-1, 0, 1)
        in_specs.append(pl.BlockSpec((2, R, LANES), lambda i, p: (p, i, 0)))
        args.append(a)
        if shift > 0:
            in_specs.append(pl.BlockSpec((2, CHUNK, LANES), lambda i, p: (p, jnp.minimum((i + 1) * rb, nb - 1), 0)))
            args.append(a)
        elif shift < 0:
            in_specs.append(pl.BlockSpec((2, CHUNK, LANES), lambda i, p: (p, jnp.maximum(i * rb - 1, 0), 0)))
            args.append(a)
    tab = pl.BlockSpec((R, LANES), lambda i, p: (i, 0))
    mat = pl.BlockSpec((LANES, LANES), lambda i, p: (0, 0))
    return pl.pallas_call(
        body, name=name, grid=(M // R, npair),
        in_specs=in_specs + [tab, tab, mat, mat, mat, pl.BlockSpec(memory_space=pl.ANY)],
        out_specs=pl.BlockSpec((R, LANES), lambda i, p: (i, blk0 + p)),
        out_shape=jax.ShapeDtypeStruct(dst.shape, dst.dtype),
        input_output_aliases={nrefs + 5: 0},
        compiler_params=_params(2, False))(*args, Ct, St, RT, D0T, D1T, dst)


def _head_masks():
    lane = lax.broadcasted_iota(jnp.int32, (1, LANES), 1)
    return [lane < 64, lane >= 64]


def _decay_vecs(lam, mu):
    i = lax.broadcasted_iota(jnp.int32, (CHUNK, 1), 0).astype(F32)
    return dict(qf=jnp.exp(lam * (i + 1.0)), kf=jnp.exp(lam * (CHUNK - 1.0 - i)),
                qb=jnp.exp(mu * (CHUNK - i)), kb=jnp.exp(mu * i),
                gf=jnp.exp(lam * float(CHUNK)), gb=jnp.exp(mu * float(CHUNK)), i=i)


def _decay_mask(lam, mu):
    r = lax.broadcasted_iota(jnp.int32, (CHUNK, CHUNK), 0)
    c = lax.broadcasted_iota(jnp.int32, (CHUNK, CHUNK), 1)
    rel = (r - c).astype(F32)
    low = rel >= 0.0
    mf = jnp.exp(lam * jnp.maximum(rel, 0.0))
    mb = jnp.exp(mu * jnp.maximum(-rel, 0.0))
    return jnp.where(low, mf, mb), rel, low


def _lam_of(lg_ref, row, idx):
    return jnp.full((1, 1), lg_ref[row, idx], F32)


def _group_index(pair_blk, npairs):
    assert pair_blk % npairs == 0
    return pair_blk // npairs


def _ret_states_fwd(Kr, P, Krc, Pc, lg, rv_blk, npairs):
    L = Kr.shape[0]
    Lc = Krc.shape[0]
    N, ncc = L // CHUNK, Lc // CHUNK
    rv_grp = _group_index(rv_blk, npairs)

    heads = [(p, h) for p in range(npairs) for h in range(2)]
    kcols = lambda p: slice(p * LANES, (p + 1) * LANES)
    vcols = lambda p, h: slice((2 * p + h) * LANES, (2 * p + h + 1) * LANES)

    def body(lg_ref, k_ref, v_ref, kc_ref, vc_ref, sf_ref, S):
        n = pl.program_id(0)
        masks = _head_masks()

        @pl.when(n == 0)
        def _():
            for p, h in heads:
                lam = _lam_of(lg_ref, 0, 2 * p + h)
                dv = _decay_vecs(lam, lam)
                s = jnp.zeros((LANES, LANES), F32)
                for cc in range(ncc):
                    rows = slice(cc * CHUNK, (cc + 1) * CHUNK)
                    kw = jnp.where(masks[h], kc_ref[rows, kcols(p)].astype(F32) * dv["kf"], 0.0).astype(BF16)
                    s = dv["gf"] * s + _tn(kw, vc_ref[rows, vcols(p, h)])
                S[p, h] = s

        for p, h in heads:
            lam = _lam_of(lg_ref, 0, 2 * p + h)
            dv = _decay_vecs(lam, lam)
            s = S[p, h]
            sf_ref[p, 0, h] = s.astype(BF16)
            kw = jnp.where(masks[h], k_ref[:, kcols(p)].astype(F32) * dv["kf"], 0.0).astype(BF16)
            S[p, h] = dv["gf"] * s + _tn(kw, v_ref[:, vcols(p, h)])

    wq, wv = npairs * LANES, npairs * 2 * LANES
    return pl.pallas_call(
        body, name="ret_states_fwd", grid=(N,),
        in_specs=[pl.BlockSpec(memory_space=pltpu.SMEM),
                  pl.BlockSpec((CHUNK, wq), lambda n: (n, 0)),
                  pl.BlockSpec((CHUNK, wv), lambda n: (n, rv_grp)),
                  pl.BlockSpec((Lc, wq), lambda n: (0, 0)),
                  pl.BlockSpec((Lc, wv), lambda n: (0, rv_grp))],
        out_specs=pl.BlockSpec((npairs, 1, 2, LANES, LANES), lambda n: (0, n, 0, 0, 0)),
        out_shape=jax.ShapeDtypeStruct((npairs, N, 2, LANES, LANES), BF16),
        scratch_shapes=[pltpu.VMEM((npairs, 2, LANES, LANES), F32)],
        compiler_params=_params(1, False))(lg, Kr, P, Krc, Pc)


def _ret_chunk_fwd(q, k, v, sf, sb, hm, lam, mu, Mk):
    dv = _decay_vecs(lam, mu)
    qm = jnp.where(hm, q, jnp.zeros_like(q))
    qmf = qm.astype(F32)
    A = _nt(qm, k)
    Am = A * Mk
    Amb = Am.astype(BF16)
    Qf = (qmf * dv["qf"]).astype(BF16)
    Qb = (qmf * dv["qb"]).astype(BF16)
    O = _nn(Amb, v) + _nn(Qf, sf) + _nn(Qb, sb)
    return dict(dv=dv, Mk=Mk, qm=qm, Am=Am, Amb=Amb, Qf=Qf, Qb=Qb, O=O)


def _ret_out_fwd(Qr, Kr, P, Krc, Pc, SF, lg, rv_blk, rg_blk, npairs, d_mix):
    L = Qr.shape[0]
    Lc = Krc.shape[0]
    N, ncc = L // CHUNK, Lc // CHUNK

    rv_grp, rg_grp = _group_index(rv_blk, npairs), _group_index(rg_blk, npairs)
    heads = [(p, h) for p in range(npairs) for h in range(2)]
    kcols = lambda p: slice(p * LANES, (p + 1) * LANES)
    vcols = lambda p, h: slice((2 * p + h) * LANES, (2 * p + h + 1) * LANES)

    def body(lg_ref, q_ref, k_ref, v_ref, g_ref, sf_ref, kc_ref, vc_ref, y_ref, sb_ref, S, Mks):
        n = pl.program_id(0)
        masks = _head_masks()

        @pl.when(n == 0)
        def _():
            for p, h in heads:
                mu = _lam_of(lg_ref, 1, 2 * p + h)
                Mks[p, h] = _decay_mask(_lam_of(lg_ref, 0, 2 * p + h), mu)[0]
                dvb = _decay_vecs(mu, mu)
                s = jnp.zeros((LANES, LANES), F32)
                for cc in reversed(range(ncc)):
                    rows = slice(cc * CHUNK, (cc + 1) * CHUNK)
                    kw = jnp.where(masks[h], kc_ref[rows, kcols(p)].astype(F32) * dvb["kb"], 0.0).astype(BF16)
                    s = dvb["gb"] * s + _tn(kw, vc_ref[rows, vcols(p, h)])
                S[p, h] = s

        for p, h in heads:
            lam = _lam_of(lg_ref, 0, 2 * p + h)
            mu = _lam_of(lg_ref, 1, 2 * p + h)
            hm = masks[h]
            dvb = _decay_vecs(lam, mu)
            s = S[p, h]
            sbb = s.astype(BF16)
            sb_ref[p, 0, h] = sbb
            k = k_ref[:, kcols(p)]
            v = v_ref[:, vcols(p, h)]
            f = _ret_chunk_fwd(q_ref[:, kcols(p)], k, v, sf_ref[p, 0, h], sbb, hm, lam, mu, Mks[p, h])
            O = f["O"]
            r = lax.rsqrt(jnp.mean(O * O, axis=-1, keepdims=True) + NORM_EPS)
            g = g_ref[:, vcols(p, h)].astype(F32)
            y_ref[:, vcols(p, h)] = (O * r * (g * _sigmoid(g))).astype(BF16)
            kw = jnp.where(hm, k.astype(F32) * dvb["kb"], 0.0).astype(BF16)
            S[p, h] = dvb["gb"] * s + _tn(kw, v)

    rev = lambda n: N - 1 - n
    wq, wv = npairs * LANES, npairs * 2 * LANES
    st = pl.BlockSpec((npairs, 1, 2, LANES, LANES), lambda n: (0, rev(n), 0, 0, 0))
    return pl.pallas_call(
        body, name="ret_out_fwd", grid=(N,),
        in_specs=[pl.BlockSpec(memory_space=pltpu.SMEM),
                  pl.BlockSpec((CHUNK, wq), lambda n: (rev(n), 0)),
                  pl.BlockSpec((CHUNK, wq), lambda n: (rev(n), 0)),
                  pl.BlockSpec((CHUNK, wv), lambda n: (rev(n), rv_grp)),
                  pl.BlockSpec((CHUNK, wv), lambda n: (rev(n), rg_grp)),
                  st,
                  pl.BlockSpec((Lc, wq), lambda n: (0, 0)),
                  pl.BlockSpec((Lc, wv), lambda n: (0, rv_grp))],
        out_specs=[pl.BlockSpec((CHUNK, wv), lambda n: (rev(n), 0)), st],
        out_shape=[jax.ShapeDtypeStruct((L, d_mix), BF16),
                   jax.ShapeDtypeStruct((npairs, N, 2, LANES, LANES), BF16)],
        scratch_shapes=[pltpu.VMEM((npairs, 2, LANES, LANES), F32), pltpu.VMEM((npairs, 2, CHUNK, CHUNK), F32)],
        compiler_params=_params(1))(lg, Qr, Kr, P, P, SF, Krc, Pc)


ACC_ROWS = 8


def _ret_bwd1(Qr, Kr, P, Krc, Pc, SF, SB, dY, lg, rv_blk, rg_blk, npairs, d_proj):
    L = Qr.shape[0]
    Lc = Krc.shape[0]
    N, ncc = L // CHUNK, Lc // CHUNK
    rv_grp, rg_grp = _group_index(rv_blk, npairs), _group_index(rg_blk, npairs)
    heads = [(p, h) for p in range(npairs) for h in range(2)]
    kcols = lambda p: slice(p * LANES, (p + 1) * LANES)
    vcols = lambda p, h: slice((2 * p + h) * LANES, (2 * p + h + 1) * LANES)

    def body(lg_ref, q_ref, k_ref, v_ref, g_ref, sf_ref, sb_ref, dy_ref, kc_ref, vc_ref,
             dq_ref, dk_ref, dv_ref, dg_ref, do_ref, dkc_ref, dvc_ref, acc_ref, dS, T, Mks):
        n = pl.program_id(0)
        masks = _head_masks()

        @pl.when(n == 0)
        def _():
            dS[...] = jnp.zeros_like(dS)
            T[...] = jnp.zeros_like(T)
            acc_ref[...] = jnp.zeros_like(acc_ref)
            for p, h in heads:
                Mks[p, h] = _decay_mask(_lam_of(lg_ref, 0, 2 * p + h), _lam_of(lg_ref, 1, 2 * p + h))[0]

        def head_main(p, h):
            lam = _lam_of(lg_ref, 0, 2 * p + h)
            mu = _lam_of(lg_ref, 1, 2 * p + h)
            hm = masks[h]
            hs = vcols(p, h)
            v = v_ref[:, hs]
            k = k_ref[:, kcols(p)]
            sf = sf_ref[p, 0, h]
            sb = sb_ref[p, 0, h]
            f = _ret_chunk_fwd(q_ref[:, kcols(p)], k, v, sf, sb, hm, lam, mu, Mks[p, h])
            dv_, O = f["dv"], f["O"]
            r = lax.rsqrt(jnp.mean(O * O, axis=-1, keepdims=True) + NORM_EPS)
            on = O * r
            g = g_ref[:, hs].astype(F32)
            sg = _sigmoid(g)
            dy = dy_ref[:, hs].astype(F32)
            dg_ref[:, hs] = (dy * on * (sg * (1.0 + g * (1.0 - sg)))).astype(BF16)
            don = dy * (g * sg)
            dO = r * (don - on * jnp.mean(don * on, axis=-1, keepdims=True))
            dOb = dO.astype(BF16)
            do_ref[:, hs] = dOb
            dAm = _nt(dOb, v)
            T[p, h] += dAm * f["Am"]
            dAb = (dAm * f["Mk"]).astype(BF16)
            km = jnp.where(hm, k, jnp.zeros_like(k))
            dq = _nn(dAb, km)
            dk = _tn(dAb, f["qm"])
            dvh = _tn(f["Amb"], dOb)
            dQf = _nt(dOb, sf)
            dQb = _nt(dOb, sb)
            dq = dq + dQf * dv_["qf"] + dQb * dv_["qb"]
            acc_ref[p, h, 0:1, :] += _fsum(dQf * f["Qf"].astype(F32) * (dv_["i"] + 1.0))
            acc_ref[p, h, 1:2, :] += _fsum(dQb * f["Qb"].astype(F32) * (CHUNK - dv_["i"]))
            dSh = dS[p, h]
            dSb_ = dSh.astype(BF16)
            Kf = (km.astype(F32) * dv_["kf"]).astype(BF16)
            dKf = _nt(v, dSb_)
            dk = dk + jnp.where(hm, dKf * dv_["kf"], 0.0)
            acc_ref[p, h, 2:3, :] += _fsum(jnp.where(hm, dKf, 0.0) * Kf.astype(F32) * (CHUNK - 1.0 - dv_["i"]))
            dvh = dvh + _nn(Kf, dSb_)
            acc_ref[p, h, 3:4, :] += float(CHUNK) * dv_["gf"] * _fsum(dSh * sf.astype(F32))
            dSh = dv_["gf"] * dSh + _tn(f["Qf"], dOb)
            dS[p, h] = dSh
            dv_ref[:, hs] = dvh
            return dq, dk

        for p in range(npairs):
            dq0, dk0 = head_main(p, 0)
            dq1, dk1 = head_main(p, 1)
            dq_ref[:, kcols(p)] = dq0 + dq1
            dk_ref[:, kcols(p)] = dk0 + dk1

        @pl.when(n == N - 1)
        def _():
            for p, h in heads:
                lam = _lam_of(lg_ref, 0, 2 * p + h)
                dv_ = _decay_vecs(lam, lam)
                hm = masks[h]
                hs = vcols(p, h)
                states = [jnp.zeros((LANES, LANES), F32)]
                kws = []
                for cc in range(ncc):
                    rows = slice(cc * CHUNK, (cc + 1) * CHUNK)
                    kw = jnp.where(hm, kc_ref[rows, kcols(p)].astype(F32) * dv_["kf"], 0.0).astype(BF16)
                    kws.append(kw)
                    states.append(dv_["gf"] * states[-1] + _tn(kw, vc_ref[rows, hs]))
                d = dS[p, h]
                for cc in reversed(range(ncc)):
                    db = d.astype(BF16)
                    rows = slice(cc * CHUNK, (cc + 1) * CHUNK)
                    dKf_c = jnp.where(hm, _nt(vc_ref[rows, hs], db), 0.0)
                    part = dKf_c * dv_["kf"]
                    if h == 0:
                        dkc_ref[rows, kcols(p)] = part
                    else:
                        dkc_ref[rows, kcols(p)] += part
                    acc_ref[p, h, 2:3, :] += _fsum(dKf_c * kws[cc].astype(F32) * (CHUNK - 1.0 - dv_["i"]))
                    dvc_ref[rows, hs] = _nn(kws[cc], db)
                    acc_ref[p, h, 3:4, :] += float(CHUNK) * dv_["gf"] * _fsum(d * states[cc])
                    d = dv_["gf"] * d
                _, rel, low = _decay_mask(lam, lam)
                Th = T[p, h]
                acc_ref[p, h, 4:5, :] += _fsum(jnp.where(low, Th * rel, 0.0))
                acc_ref[p, h, 5:6, :] += _fsum(jnp.where(low, 0.0, -Th * rel))

    rev = lambda n: N - 1 - n
    wq, wv = npairs * LANES, npairs * 2 * LANES
    st = pl.BlockSpec((npairs, 1, 2, LANES, LANES), lambda n: (0, rev(n), 0, 0, 0))
    pair = pl.BlockSpec((CHUNK, wq), lambda n: (rev(n), 0))
    wide = lambda grp: pl.BlockSpec((CHUNK, wv), lambda n: (rev(n), grp))
    return pl.pallas_call(
        body, name="ret_bwd_desc", grid=(N,),
        in_specs=[pl.BlockSpec(memory_space=pltpu.SMEM), pair, pair, wide(rv_grp), wide(rg_grp), st, st, wide(0),
                  pl.BlockSpec((Lc, wq), lambda n: (0, 0)),
                  pl.BlockSpec((Lc, wv), lambda n: (0, rv_grp))],
        out_specs=[pair, pair, wide(0), wide(rg_grp), wide(0),
                   pl.BlockSpec((Lc, wq), lambda n: (0, 0)),
                   pl.BlockSpec((Lc, wv), lambda n: (0, 0)),
                   pl.BlockSpec((npairs, 2, ACC_ROWS, LANES), lambda n: (0, 0, 0, 0))],
        out_shape=[jax.ShapeDtypeStruct((L, npairs * LANES), F32),
                   jax.ShapeDtypeStruct((L, npairs * LANES), F32),
                   jax.ShapeDtypeStruct((L, npairs * 2 * LANES), F32),
                   jax.ShapeDtypeStruct((L, d_proj), BF16),
                   jax.ShapeDtypeStruct((L, npairs * 2 * LANES), BF16),
                   jax.ShapeDtypeStruct((Lc, npairs * LANES), F32),
                   jax.ShapeDtypeStruct((Lc, npairs * 2 * LANES), F32),
                   jax.ShapeDtypeStruct((npairs, 2, ACC_ROWS, LANES), F32)],
        scratch_shapes=[pltpu.VMEM((npairs, 2, LANES, LANES), F32), pltpu.VMEM((npairs, 2, CHUNK, CHUNK), F32),
                        pltpu.VMEM((npairs, 2, CHUNK, CHUNK), F32)],
        compiler_params=_params(1))(lg, Qr, Kr, P, P, SF, SB, dY, Krc, Pc)


def _ret_bwd2(Qr, Kr, P, Krc, Pc, SB, dO, dKr, dVp, dP, dKc, dVc, lg, rv_blk, npairs):
    L = Qr.shape[0]
    Lc = Krc.shape[0]
    N, ncc = L // CHUNK, Lc // CHUNK
    rv_grp = _group_index(rv_blk, npairs)
    heads = [(p, h) for p in range(npairs) for h in range(2)]
    kcols = lambda p: slice(p * LANES, (p + 1) * LANES)
    vcols = lambda p, h: slice((2 * p + h) * LANES, (2 * p + h + 1) * LANES)

    def body(lg_ref, q_ref, k_ref, v_ref, sb_ref, do_ref, dkin_ref, dvin_ref, kc_ref, vc_ref, dkcin_ref, dvcin_ref,
             dpin_ref, dk_ref, dv_ref, dkc_ref, dvc_ref, acc_ref, dS):
        del dpin_ref
        n = pl.program_id(0)
        masks = _head_masks()

        @pl.when(n == 0)
        def _():
            dS[...] = jnp.zeros_like(dS)
            acc_ref[...] = jnp.zeros_like(acc_ref)

        def head_main(p, h):
            mu = _lam_of(lg_ref, 1, 2 * p + h)
            hm = masks[h]
            hs = vcols(p, h)
            dv_ = _decay_vecs(mu, mu)
            v = v_ref[:, hs]
            k = k_ref[:, kcols(p)]
            q = q_ref[:, kcols(p)]
            dOb = do_ref[:, hs]
            km = jnp.where(hm, k, jnp.zeros_like(k)).astype(F32)
            Kb = (km * dv_["kb"]).astype(BF16)
            Qb = (jnp.where(hm, q, jnp.zeros_like(q)).astype(F32) * dv_["qb"]).astype(BF16)
            dSh = dS[p, h]
            dSb_ = dSh.astype(BF16)
            dKb = jnp.where(hm, _nt(v, dSb_), 0.0)
            acc_ref[p, h, 0:1, :] += _fsum(dKb * Kb.astype(F32) * dv_["i"])
            dv_ref[:, hs] = (dvin_ref[:, hs] + _nn(Kb, dSb_)).astype(BF16)
            acc_ref[p, h, 1:2, :] += float(CHUNK) * dv_["gb"] * _fsum(dSh * sb_ref[p, 0, h].astype(F32))
            dS[p, h] = dv_["gb"] * dSh + _tn(Qb, dOb)
            return dKb * dv_["kb"]

        for p in range(npairs):
            dk_ref[:, kcols(p)] = dkin_ref[:, kcols(p)] + head_main(p, 0) + head_main(p, 1)

        @pl.when(n == N - 1)
        def _():
            for p, h in heads:
                mu = _lam_of(lg_ref, 1, 2 * p + h)
                hm = masks[h]
                hs = vcols(p, h)
                dv_ = _decay_vecs(mu, mu)
                states = {}
                kws = {}
                s = jnp.zeros((LANES, LANES), F32)
                for cc in reversed(range(ncc)):
                    rows = slice(cc * CHUNK, (cc + 1) * CHUNK)
                    states[cc] = s
                    kw = jnp.where(hm, kc_ref[rows, kcols(p)].astype(F32) * dv_["kb"], 0.0).astype(BF16)
                    kws[cc] = kw
                    s = dv_["gb"] * s + _tn(kw, vc_ref[rows, hs])
                d = dS[p, h]
                for cc in range(ncc):
                    db = d.astype(BF16)
                    rows = slice(cc * CHUNK, (cc + 1) * CHUNK)
                    dKb_c = jnp.where(hm, _nt(vc_ref[rows, hs], db), 0.0)
                    part = dKb_c * dv_["kb"]
                    if h == 0:
                        dkc_ref[rows, kcols(p)] = dkcin_ref[rows, kcols(p)] + part
                    else:
                        dkc_ref[rows, kcols(p)] += part
                    acc_ref[p, h, 0:1, :] += _fsum(dKb_c * kws[cc].astype(F32) * dv_["i"])
                    dvc_ref[rows, hs] = dvcin_ref[rows, hs] + _nn(kws[cc], db)
                    acc_ref[p, h, 1:2, :] += float(CHUNK) * dv_["gb"] * _fsum(d * states[cc])
                    d = dv_["gb"] * d

    wq, wv = npairs * LANES, npairs * 2 * LANES
    st = pl.BlockSpec((npairs, 1, 2, LANES, LANES), lambda n: (0, n, 0, 0, 0))
    pair = pl.BlockSpec((CHUNK, wq), lambda n: (n, 0))
    wide = lambda grp: pl.BlockSpec((CHUNK, wv), lambda n: (n, grp))
    ckc = pl.BlockSpec((Lc, wq), lambda n: (0, 0))
    cvc = lambda grp: pl.BlockSpec((Lc, wv), lambda n: (0, grp))
    return pl.pallas_call(
        body, name="ret_bwd_asc", grid=(N,),
        in_specs=[pl.BlockSpec(memory_space=pltpu.SMEM), pair, pair, wide(rv_grp), st, wide(0), pair, wide(0),
                  ckc, cvc(rv_grp), ckc, cvc(0), pl.BlockSpec(memory_space=pl.ANY)],
        out_specs=[pair, wide(rv_grp), ckc, cvc(0),
                   pl.BlockSpec((npairs, 2, ACC_ROWS, LANES), lambda n: (0, 0, 0, 0))],
        out_shape=[jax.ShapeDtypeStruct(dKr.shape, F32),
                   jax.ShapeDtypeStruct(dP.shape, dP.dtype),
                   jax.ShapeDtypeStruct(dKc.shape, F32),
                   jax.ShapeDtypeStruct(dVc.shape, F32),
                   jax.ShapeDtypeStruct((npairs, 2, ACC_ROWS, LANES), F32)],
        input_output_aliases={12: 1},
        scratch_shapes=[pltpu.VMEM((npairs, 2, LANES, LANES), F32)],
        compiler_params=_params(1))(lg, Qr, Kr, P, SB, dO, dKr, dVp, Krc, Pc, dKc, dVc, dP)


GROUP = 4


def _att_band(Lc):
    row = np.arange(GROUP * CHUNK)[:, None] % CHUNK
    col = np.arange(3 * CHUNK + Lc)[None, :]
    ok = ((col >= row) & (col <= row + 2 * CHUNK)) | (col >= 3 * CHUNK)
    return jnp.asarray(np.where(ok, 0.0, NEG), F32)


def _att_edge(n, N, Lc):
    col = lax.broadcasted_iota(jnp.int32, (1, 3 * CHUNK + Lc), 1)
    off = jnp.logical_or(jnp.logical_and(col < CHUNK, n == 0),
                         jnp.logical_and(jnp.logical_and(col >= 2 * CHUNK, col < 3 * CHUNK), n == N - 1))
    return jnp.where(off, NEG, 0.0)


def _stack_heads(ref, gi):
    masks = _head_masks()
    tiles = []
    for pr in range(2):
        t = ref[:, (2 * gi + pr) * LANES:(2 * gi + pr + 1) * LANES]
        for a in range(2):
            tiles.append(jnp.where(masks[a], t, jnp.zeros_like(t)))
    return jnp.concatenate(tiles, axis=0)


def _unstack_heads(x4):
    m0 = _head_masks()[0]
    return [jnp.where(m0, x4[(2 * pr) * CHUNK:(2 * pr + 1) * CHUNK], x4[(2 * pr + 1) * CHUNK:(2 * pr + 2) * CHUNK])
            for pr in range(2)]


def _sink_column(sink_ref, g):
    row = lax.broadcasted_iota(jnp.int32, (GROUP * CHUNK, 1), 0) // CHUNK
    col = jnp.zeros((GROUP * CHUNK, 1), F32)
    for h in range(GROUP):
        col = jnp.where(row == h, sink_ref[0, g * GROUP + h], col)
    return col


def _att_probs(q4, Kall, bias, snk):
    s = _nt(q4, Kall) + bias
    mx = jnp.maximum(jnp.max(s, axis=1, keepdims=True), snk)
    p = jnp.exp(s - mx)
    p_snk = jnp.exp(snk - mx)
    inv = 1.0 / (jnp.sum(p, axis=1, keepdims=True) + p_snk)
    return p, p_snk, inv


def _att_groups_per_step(nkv, blk0):
    return 2 if nkv % 2 == 0 and blk0 % 2 == 0 else 1


def _att_specs(Lc, N, gps):
    q = pl.BlockSpec((CHUNK, gps * 2 * LANES), lambda g, n: (n, g))
    kv = lambda s: pl.BlockSpec((gps, CHUNK, LANES), lambda g, n: (g, jnp.clip(n + s, 0, N - 1), 0))
    ctx = pl.BlockSpec((gps, Lc, LANES), lambda g, n: (g, 0, 0))
    return q, kv, ctx


def _att_fwd(Qa, Kd, Vd, Kdc, Vdc, sink, Y, blk0):
    L = Qa.shape[0]
    Lc = Kdc.shape[1]
    N = L // CHUNK
    nkv = Kd.shape[0]
    gps = _att_groups_per_step(nkv, blk0)

    def body(sink_ref, band_ref, q_ref, kp, kc_, kn, vp, vc_, vn, kctx, vctx, y_in, o_ref):
        del y_in
        g, n = pl.program_id(0), pl.program_id(1)
        bias = band_ref[...] + _att_edge(n, N, Lc)
        for gi in range(gps):
            Kall = jnp.concatenate([kp[gi], kc_[gi], kn[gi], kctx[gi]], axis=0)
            Vall = jnp.concatenate([vp[gi], vc_[gi], vn[gi], vctx[gi]], axis=0)
            p, _, inv = _att_probs(_stack_heads(q_ref, gi), Kall, bias, _sink_column(sink_ref, g * gps + gi))
            o4 = _nn(p.astype(BF16), Vall) * inv
            for pr, o in enumerate(_unstack_heads(o4)):
                o_ref[:, (2 * gi + pr) * LANES:(2 * gi + pr + 1) * LANES] = o.astype(BF16)

    q, kv, ctx = _att_specs(Lc, N, gps)
    band = pl.BlockSpec((GROUP * CHUNK, 3 * CHUNK + Lc), lambda g, n: (0, 0))
    return pl.pallas_call(
        body, name="att_fwd", grid=(nkv // gps, N),
        in_specs=[pl.BlockSpec(memory_space=pltpu.SMEM), band, q, kv(-1), kv(0), kv(1), kv(-1), kv(0), kv(1), ctx, ctx,
                  pl.BlockSpec(memory_space=pl.ANY)],
        out_specs=pl.BlockSpec((CHUNK, gps * 2 * LANES), lambda g, n: (n, blk0 // gps + g)),
        out_shape=jax.ShapeDtypeStruct(Y.shape, Y.dtype),
        input_output_aliases={11: 0},
        compiler_params=_params(2))(sink, _att_band(Lc), Qa, Kd, Kd, Kd, Vd, Vd, Vd, Kdc, Vdc, Y)


def _att_bwd(Qa, Kd, Vd, Kdc, Vdc, sink, dY, blk0):
    L = Qa.shape[0]
    Lc = Kdc.shape[1]
    N = L // CHUNK
    nkv = Kd.shape[0]
    gps = _att_groups_per_step(nkv, blk0)

    def body(sink_ref, band_ref, q_ref, kp, kc_, kn, vp, vc_, vn, kctx, vctx, dy_ref,
             dq_ref, dkp, dkc_, dkn, dvp, dvc_, dvn, dkctx, dvctx, dsink_ref):
        g, n = pl.program_id(0), pl.program_id(1)

        @pl.when(n == 0)
        def _():
            dkctx[...] = jnp.zeros_like(dkctx)
            dvctx[...] = jnp.zeros_like(dvctx)
            dsink_ref[...] = jnp.zeros_like(dsink_ref)

        bias = band_ref[...] + _att_edge(n, N, Lc)
        for gi in range(gps):
            Kall = jnp.concatenate([kp[gi], kc_[gi], kn[gi], kctx[gi]], axis=0)
            Vall = jnp.concatenate([vp[gi], vc_[gi], vn[gi], vctx[gi]], axis=0)
            q4 = _stack_heads(q_ref, gi)
            do4 = _stack_heads(dy_ref, gi)
            p, p_snk, inv = _att_probs(q4, Kall, bias, _sink_column(sink_ref, g * gps + gi))
            P = p * inv
            dp = _nt(do4, Vall)
            delta = jnp.sum(P * dp, axis=1, keepdims=True)
            ds = (P * (dp - delta)).astype(BF16)
            dsnk = -(p_snk * inv) * delta
            for h in range(GROUP):
                dsink_ref[gi, h:h + 1, :] += _fsum(dsnk[h * CHUNK:(h + 1) * CHUNK])
            for pr, dq in enumerate(_unstack_heads(_nn(ds, Kall))):
                dq_ref[:, (2 * gi + pr) * LANES:(2 * gi + pr + 1) * LANES] = dq
            dK = _tn(ds, q4)
            dV = _tn(P.astype(BF16), do4)
            for j, (rk, rv) in enumerate([(dkp, dvp), (dkc_, dvc_), (dkn, dvn)]):
                rk[gi] = dK[j * CHUNK:(j + 1) * CHUNK].astype(BF16)
                rv[gi] = dV[j * CHUNK:(j + 1) * CHUNK].astype(BF16)
            dkctx[gi] += dK[3 * CHUNK:]
            dvctx[gi] += dV[3 * CHUNK:]

    q, kv, ctx = _att_specs(Lc, N, gps)
    band = pl.BlockSpec((GROUP * CHUNK, 3 * CHUNK + Lc), lambda g, n: (0, 0))
    blk = pl.BlockSpec((gps, CHUNK, LANES), lambda g, n: (g, n, 0))
    part = jax.ShapeDtypeStruct((nkv, L, LANES), BF16)
    cshape = jax.ShapeDtypeStruct((nkv, Lc, LANES), F32)
    return pl.pallas_call(
        body, name="att_bwd", grid=(nkv // gps, N),
        in_specs=[pl.BlockSpec(memory_space=pltpu.SMEM), band, q, kv(-1), kv(0), kv(1), kv(-1), kv(0), kv(1), ctx, ctx,
                  pl.BlockSpec((CHUNK, gps * 2 * LANES), lambda g, n: (n, blk0 // gps + g))],
        out_specs=[q, blk, blk, blk, blk, blk, blk, ctx, ctx,
                   pl.BlockSpec((gps, 8, LANES), lambda g, n: (g, 0, 0))],
        out_shape=[jax.ShapeDtypeStruct(Qa.shape, F32), part, part, part, part, part, part, cshape, cshape,
                   jax.ShapeDtypeStruct((nkv, 8, LANES), F32)],
        compiler_params=_params(2))(sink, _att_band(Lc), Qa, Kd, Kd, Kd, Vd, Vd, Vd, Kdc, Vdc, dY)


def _scale_rows(dx, gt, saved, name):
    M, D = dx.shape
    tm = _tile(M, 512, 8)

    def body(dx_ref, gt_ref, sv_ref, dz_ref, dgt_ref):
        @pl.when(pl.program_id(0) == 0)
        def _():
            dgt_ref[...] = jnp.zeros_like(dgt_ref)
        d = dx_ref[...]
        dz_ref[...] = (d * gt_ref[...]).astype(BF16)
        dgt_ref[...] += jnp.sum(d * sv_ref[...].astype(F32), axis=0, keepdims=True)

    row = pl.BlockSpec((tm, D), lambda i: (i, 0))
    vec = pl.BlockSpec((1, D), lambda i: (0, 0))
    return pl.pallas_call(
        body, name=name, grid=(M // tm,), in_specs=[row, vec, row], out_specs=[row, vec],
        out_shape=[jax.ShapeDtypeStruct((M, D), BF16), jax.ShapeDtypeStruct((1, D), F32)],
        compiler_params=_params(1))(dx, gt, saved)


def _bwd_proj(dz, w, G=None, U=None, name="bwd_proj"):
    M, D = dz.shape
    N = w.shape[0]
    swiglu = G is not None
    tm, tn = _tile(M, ROWS_PER_LATCH, 8), _tile(N, 512)

    def body(*refs):
        if swiglu:
            dz_ref, w_ref, G_ref, U_ref, dG_ref, dU_ref = refs
        else:
            dz_ref, w_ref, dA_ref = refs
        dA = _nt(dz_ref[...], w_ref[...])
        if swiglu:
            Gv = G_ref[...].astype(F32)
            Uv = U_ref[...].astype(F32)
            sg = _sigmoid(Gv)
            dU_ref[...] = (dA * Gv * sg).astype(BF16)
            dG_ref[...] = (dA * Uv * (sg * (1.0 + Gv * (1.0 - sg)))).astype(BF16)
        else:
            dA_ref[...] = dA.astype(BF16)

    row = pl.BlockSpec((tm, D), lambda i, j: (i, 0))
    tile = pl.BlockSpec((tm, tn), lambda i, j: (i, j))
    big = jax.ShapeDtypeStruct((M, N), BF16)
    in_specs = [row, pl.BlockSpec((tn, D), lambda i, j: (j, 0))]
    args = [dz, w]
    if swiglu:
        in_specs += [tile, tile]
        args += [G, U]
        out_specs, out_shape = [tile, tile], [big, big]
    else:
        out_specs, out_shape = tile, big
    return pl.pallas_call(
        body, name=name, grid=(M // tm, N // tn), in_specs=in_specs, out_specs=out_specs, out_shape=out_shape,
        compiler_params=_params(2))(*args)


def _tn_matmul(pairs, name):
    Ka, Nb = pairs[0][0].shape[1], pairs[0][1].shape[1]
    tk, tn = _tile(Ka, 2048), _tile(Nb, 2048)
    tls, nks = [], []
    for a, _ in pairs:
        tl = _tile(a.shape[0], 512, 8)
        tls.append(tl)
        nks.append(a.shape[0] // tl)
    starts = [int(s) for s in np.cumsum([0] + nks[:-1])]
    nk = int(sum(nks))

    def body(*refs):
        out_ref, acc = refs[-2], refs[-1]
        k = pl.program_id(2)

        @pl.when(k == 0)
        def _():
            acc[...] = jnp.zeros_like(acc)

        for idx in range(len(pairs)):
            a_ref, b_ref = refs[2 * idx], refs[2 * idx + 1]

            @pl.when(jnp.logical_and(k >= starts[idx], k < starts[idx] + nks[idx]))
            def _():
                acc[...] += _tn(a_ref[...], b_ref[...])

        @pl.when(k == nk - 1)
        def _():
            out_ref[...] = acc[...].astype(BF16)

    in_specs, args = [], []
    for idx, (a, b) in enumerate(pairs):
        s0, n_ = starts[idx], nks[idx]
        in_specs.append(pl.BlockSpec((tls[idx], tk), lambda i, j, k, s0=s0, n_=n_: (jnp.clip(k - s0, 0, n_ - 1), i)))
        in_specs.append(pl.BlockSpec((tls[idx], tn), lambda i, j, k, s0=s0, n_=n_: (jnp.clip(k - s0, 0, n_ - 1), j)))
        args += [a, b]
    return pl.pallas_call(
        body, name=name, grid=(Ka // tk, Nb // tn, nk), in_specs=in_specs,
        out_specs=pl.BlockSpec((tk, tn), lambda i, j, k: (i, j)),
        out_shape=jax.ShapeDtypeStruct((Ka, Nb), BF16),
        scratch_shapes=[pltpu.VMEM((tk, tn), F32)], compiler_params=_params(3))(*args)


def _bwd_norm_mod(pairs, x, dres, g, sh, sc, name):
    M, D = x.shape
    K = pairs[0][0].shape[1]
    tm, tk = _tile(M, 512, 8), _tile(K, 1152 if len(pairs) == 1 else 512)
    nk = K // tk
    npair = len(pairs)
    has_res = dres is not None

    def body(*refs):
        pr = refs[:2 * npair]
        rest = refs[2 * npair:]
        if has_res:
            x_ref, dres_ref, g_ref, sh_ref, sc_ref, dx_ref, st_ref, acc = rest
        else:
            x_ref, g_ref, sh_ref, sc_ref, dx_ref, st_ref, acc = rest
        del sh_ref
        i, k = pl.program_id(0), pl.program_id(1)

        @pl.when(jnp.logical_and(i == 0, k == 0))
        def _():
            st_ref[...] = jnp.zeros_like(st_ref)

        @pl.when(k == 0)
        def _():
            acc[...] = jnp.zeros_like(acc)

        t = _nt(pr[1][...], pr[0][...])
        for idx in range(1, npair):
            t = t + _nt(pr[2 * idx + 1][...], pr[2 * idx][...])
        acc[...] += t

        @pl.when(k == nk - 1)
        def _():
            xv = x_ref[...]
            gv = g_ref[...]
            dh = acc[...].T
            r = lax.rsqrt(jnp.mean(xv * xv, axis=-1, keepdims=True) + NORM_EPS)
            xh = xv * r
            st_ref[0:1, :] += jnp.sum(dh, axis=0, keepdims=True)
            st_ref[1:2, :] += jnp.sum(dh * (xh * gv), axis=0, keepdims=True)
            dn = dh * (1.0 + sc_ref[...])
            st_ref[2:3, :] += jnp.sum(dn * xh, axis=0, keepdims=True)
            dxh = dn * gv
            d = r * (dxh - xh * jnp.mean(dxh * xh, axis=-1, keepdims=True))
            if has_res:
                d = d + dres_ref[...]
            dx_ref[...] = d

    row = pl.BlockSpec((tm, D), lambda i, k: (i, 0))
    vec = pl.BlockSpec((1, D), lambda i, k: (0, 0))
    in_specs, args = [], []
    for dA, w in pairs:
        in_specs += [pl.BlockSpec((tm, tk), lambda i, k: (i, k)), pl.BlockSpec((D, tk), lambda i, k: (0, k))]
        args += [dA, w]
    in_specs += [row] + ([row] if has_res else []) + [vec, vec, vec]
    args += [x] + ([dres] if has_res else []) + [g, sh, sc]
    return pl.pallas_call(
        body, name=name, grid=(M // tm, nk), in_specs=in_specs,
        out_specs=[row, pl.BlockSpec((8, D), lambda i, k: (0, 0))],
        out_shape=[jax.ShapeDtypeStruct((M, D), F32), jax.ShapeDtypeStruct((8, D), F32)],
        scratch_shapes=[pltpu.VMEM((D, tm), F32)], compiler_params=_params(2))(*args)


def _local_step(x, ctx, tgt, mod, modc, norm_mix, norm_ffn, norm_final, lg, sink, w_in, rest_weights, on_grads, flush):
    L, D = x.shape
    Lc = ctx.shape[0]
    d_proj = w_in.shape[1]
    npairs = RET_HEADS // 2
    nkv = ATT_KV_HEADS
    nkvp = nkv // 2
    o_rq = 0
    o_rk = o_rq + RET_HEADS * RET_DK // LANES
    o_rv = o_rk + RET_HEADS * RET_DK // LANES
    o_rg = o_rv + RET_HEADS * RET_DV // LANES
    o_aq = o_rg + RET_HEADS * RET_DV // LANES
    o_ak = o_aq + ATT_HEADS * ATT_DH // LANES
    o_av = o_ak + nkv * ATT_DH // LANES
    assert (o_av + nkv * ATT_DH // LANES) * LANES == d_proj
    assert o_rv % 2 == 0 and o_rg % 2 == 0 and (RET_HEADS * RET_DV) % (2 * LANES) == 0
    rv_blk, rg_blk = o_rv // 2, o_rg // 2
    d_ret = RET_HEADS * RET_DV
    d_mix = d_ret + ATT_HEADS * ATT_DH
    att_blk = d_ret // (2 * LANES)
    k_scale = RET_DK ** -0.5
    a_scale = ATT_DH ** -0.5

    T = _rope_tables(L)
    Tc = dict(C=jnp.ones((Lc, LANES), F32), S=jnp.zeros((Lc, LANES), F32))
    row = lambda m, i: m[i:i + 1]
    sh_m, sc_m, gt_m, sh_f, sc_f, gt_f = [row(mod, i) for i in range(6)]
    sh_mc, sc_mc = row(modc, 0), row(modc, 1)

    P, hx = _norm_mod_matmul(x, norm_mix, sh_m, sc_m, w_in, "in_proj")
    Pc, hc = _norm_mod_matmul(ctx, norm_mix, sh_mc, sc_mc, w_in, "in_proj_ctx")
    nq = RET_HEADS * RET_DK // LANES
    Qr = _rope_cols(P, o_rq, nq, T["Cr"], T["Sr"], T["Rr"], 1.0, True, "rope_rq")
    Kr = _rope_cols(P, o_rk, nq, T["Cr"], T["Sr"], T["Rr"], k_scale, True, "rope_rk")
    Krc = _rope_cols(Pc, o_rk, nq, Tc["C"], Tc["S"], T["Rr"], k_scale, False, "scale_rk_ctx")
    Qa = _rope_cols(P, o_aq, ATT_HEADS * ATT_DH // LANES, T["Ca"], T["Sa"], T["Ra"], a_scale, True, "rope_aq")
    Kd = _dup_heads(P, o_ak, nkvp, T["Ca"], T["Sa"], T["Ra"], T["D0"], T["D1"], True, "dup_ak")
    Vd = _dup_heads(P, o_av, nkvp, T["Ca"], T["Sa"], T["Ra"], T["D0"], T["D1"], False, "dup_av")
    Kdc = _dup_heads(Pc, o_ak, nkvp, Tc["C"], Tc["S"], T["Ra"], T["D0"], T["D1"], False, "dup_ak_ctx")
    Vdc = _dup_heads(Pc, o_av, nkvp, Tc["C"], Tc["S"], T["Ra"], T["D0"], T["D1"], False, "dup_av_ctx")

    SF = _ret_states_fwd(Kr, P, Krc, Pc, lg, rv_blk, npairs)
    Y, SB = _ret_out_fwd(Qr, Kr, P, Krc, Pc, SF, lg, rv_blk, rg_blk, npairs, d_mix)
    Y = _att_fwd(Qa, Kd, Vd, Kdc, Vdc, sink, Y, att_blk)

    w_out, w_gate, w_up, w_down = rest_weights(Y)
    x1, O1 = _proj_residual(Y, w_out, x, gt_m, "out_proj")
    G, U, A, h2 = _ffn_in(x1, norm_ffn, sh_f, sc_f, w_gate, w_up)
    x2, Fo = _proj_residual(A, w_down, x1, gt_f, "ffn_out")
    dx2, loss, d_norm_final, dz2, dgt_f = _final(x2, norm_final, tgt, gt_f, Fo)

    dG, dU = _bwd_proj(dz2, w_down, G, U, name="ffn_out_bwd")
    g_w_down = _tn_matmul([(A, dz2)], "grad_w_down")
    tok = on_grads(["w_down"], [g_w_down])
    dx1, st_f = _bwd_norm_mod([(dG, w_gate), (dU, w_up)], x1, dx2, norm_ffn + tok, sh_f, sc_f, "ffn_in_bwd")
    tok = flush(dx1)
    g_w_gate = _tn_matmul([(h2, dG)], "grad_w_gate")
    g_w_up = _tn_matmul([(h2, dU)], "grad_w_up")
    tok = tok + on_grads(["w_gate", "w_up"], [g_w_gate, g_w_up])
    dz1, dgt_m = _scale_rows(dx1, gt_m + tok, O1, "mix_gate_bwd")
    dY = _bwd_proj(dz1, w_out, name="out_proj_bwd")
    tok = flush(dY)
    g_w_out = _tn_matmul([(Y, dz1)], "grad_w_out")
    tok = tok + on_grads(["w_out"], [g_w_out])

    dQa, dKp, dKs, dKn, dVp, dVs, dVn, dKdc, dVdc, dsink = _att_bwd(Qa, Kd, Vd, Kdc, Vdc, sink + tok, dY, att_blk)
    tok = flush(dQa)
    dQr, dKr, dVr, dP, dO, dKc, dVc, acc1 = _ret_bwd1(Qr, Kr, P, Krc, Pc, SF, SB, dY, lg + tok, rv_blk, rg_blk, npairs,
                                                      d_proj)
    dKr, dP, dKc, dVc, acc2 = _ret_bwd2(Qr, Kr, P, Krc, Pc, SB, dO, dKr, dVr, dP, dKc, dVc, lg, rv_blk, npairs)

    dP = _unrope_cols(dQr, dP, o_rq, nq, T["Cr"], T["Sr"], T["RrT"], 1.0, True, "unrope_rq")
    dP = _unrope_cols(dKr, dP, o_rk, nq, T["Cr"], T["Sr"], T["RrT"], k_scale, True, "unrope_rk")
    dP = _unrope_cols(dQa, dP, o_aq, ATT_HEADS * ATT_DH // LANES, T["Ca"], T["Sa"], T["RaT"], a_scale, True, "unrope_aq")
    dP = _fold_heads([(dKs, 0), (dKp, 1), (dKn, -1)], dP, o_ak, nkvp, T["Ca"], T["Sa"], T["RaT"], T["D0T"], T["D1T"],
                     True, "fold_ak")
    dP = _fold_heads([(dVs, 0), (dVp, 1), (dVn, -1)], dP, o_av, nkvp, T["Ca"], T["Sa"], T["RaT"], T["D0T"], T["D1T"],
                     False, "fold_av")
    dPc = jnp.zeros((Lc, d_proj), BF16)
    dPc = _unrope_cols(dKc, dPc, o_rk, nq, Tc["C"], Tc["S"], T["RrT"], k_scale, False, "ctx_rk_bwd")
    dPc = _unrope_cols(dVc, dPc, o_rv, RET_HEADS * RET_DV // LANES, Tc["C"], Tc["S"], T["RrT"], 1.0, False, "ctx_rv_bwd")
    dPc = _fold_heads([(dKdc.astype(BF16), 0)], dPc, o_ak, nkvp, Tc["C"], Tc["S"], T["RaT"], T["D0T"], T["D1T"],
                      False, "fold_ak_ctx")
    dPc = _fold_heads([(dVdc.astype(BF16), 0)], dPc, o_av, nkvp, Tc["C"], Tc["S"], T["RaT"], T["D0T"], T["D1T"],
                      False, "fold_av_ctx")

    dx, st_m = _bwd_norm_mod([(dP, w_in)], x, dx1, norm_mix, sh_m, sc_m, "in_proj_bwd")
    _, st_mc = _bwd_norm_mod([(dPc, w_in)], ctx, None, norm_mix, sh_mc, sc_mc, "in_proj_ctx_bwd")
    g_w_in = _tn_matmul([(hx, dP), (hc, dPc)], "grad_w_in")
    on_grads(["w_in"], [g_w_in])

    a1 = acc1[:, :, :, 0].reshape(RET_HEADS, ACC_ROWS)
    a2 = acc2[:, :, :, 0].reshape(RET_HEADS, ACC_ROWS)
    dlam = (a1[:, 0] + a1[:, 2] + a1[:, 3] + a1[:, 4]) * lg[0]
    dmu = (a1[:, 1] + a1[:, 5] + a2[:, 0] + a2[:, 1]) * lg[1]
    d_sink = dsink[:, :4, 0].reshape(1, ATT_HEADS)

    nh = RET_HEADS
    assert 2 * nh + ATT_HEADS <= LOSS_LANE
    small = _pack_rows(
        [(st_m, 0, 2, 0, 0), (dgt_m, 0, 1, 2, 0), (st_f, 0, 2, 3, 0), (dgt_f, 0, 1, 5, 0), (st_mc, 0, 2, 6, 0),
         (st_m[2:3] + st_mc[2:3], 0, 1, 12, 0), (st_f, 2, 1, 13, 0), (d_norm_final, 0, 1, 14, 0),
         (dlam.reshape(1, nh), 0, 1, 15, 0), (dmu.reshape(1, nh), 0, 1, 15, nh), (d_sink, 0, 1, 15, 2 * nh),
         (loss[:, 0:1], 0, 1, 15, LOSS_LANE)], 16, D, "pack_small")
    return dict(grad_x=dx, small=small)


def _my_pos():
    return lax.axis_index("x"), lax.axis_index("y"), lax.axis_index("c")


def _other_chips(x, y):
    return [(1 - x, y), (x, 1 - y), (1 - x, 1 - y)]


def _remote(src, dst, ssem, rsem, dev):
    return pltpu.make_async_remote_copy(src_ref=src, dst_ref=dst, send_sem=ssem, recv_sem=rsem,
                                        device_id=dev, device_id_type=MESH)


def _allgather8(v, name):
    R, Cc = v.shape

    def body(v_ref, out_ref, send_sems, recv_sems):
        x, y, c = _my_pos()
        me = 4 * x + 2 * y + c
        out_ref[pl.ds(me, 1)] = v_ref[...][None]
        peers = []
        for j in range(1, N_DEV):
            peers.append((1 - x if (j >> 2) & 1 else x, 1 - y if (j >> 1) & 1 else y, 1 - c if j & 1 else c))
        copies = []
        for j, peer in enumerate(peers):
            cp = _remote(v_ref, out_ref.at[me], send_sems.at[j], recv_sems.at[j], peer)
            cp.start()
            copies.append(cp)
        for j, peer in enumerate(peers):
            pid = 4 * peer[0] + 2 * peer[1] + peer[2]
            _remote(v_ref, out_ref.at[pid], send_sems.at[j], recv_sems.at[j], peer).wait_recv()
        for cp in copies:
            cp.wait_send()

    return pl.pallas_call(
        body, name=name, out_shape=jax.ShapeDtypeStruct((N_DEV, R, Cc), v.dtype),
        in_specs=[pl.BlockSpec(memory_space=pltpu.VMEM)], out_specs=pl.BlockSpec(memory_space=pltpu.VMEM),
        scratch_shapes=[pltpu.SemaphoreType.DMA((N_DEV - 1,)), pltpu.SemaphoreType.DMA((N_DEV - 1,))])(v)


def _region(ref, k, half, shard_shape, axis):
    r, cs = shard_shape
    hr = r // 2
    if axis == 1:
        return ref.at[pl.ds(pl.multiple_of(half * hr, 16), hr), pl.ds(pl.multiple_of(k * cs, LANES), cs)]
    return ref.at[pl.ds(pl.multiple_of(k * r + half * hr, 16), hr), :]


def _full_shape(shard_shape, axis):
    r, cs = shard_shape
    return (r, N_CHIPS * cs) if axis == 1 else (N_CHIPS * r, cs)


def _half_pieces(ref, half, shard_shape, axis):
    r, cs = shard_shape
    hr = r // 2
    if axis == 1:
        return [ref.at[pl.ds(pl.multiple_of(half * hr, 16), hr), :]]
    return [ref.at[pl.ds(pl.multiple_of(k * r + half * hr, 16), hr), :] for k in range(N_CHIPS)]


def _half_block_spec(shard_shape, axis, tr):
    r, cs = shard_shape
    hr = r // 2
    if axis == 1:
        return pl.BlockSpec((tr, cs), lambda k, i, c_ref: (c_ref[0] * (hr // tr) + i, k))
    return pl.BlockSpec((tr, cs), lambda k, i, c_ref: (k * (r // tr) + c_ref[0] * (hr // tr) + i, 0))


def _add_halves(g, recv, cvec, shard_shape, axis, name):
    r, cs = shard_shape
    hr = r // 2
    tr = _tile(hr, 256, 16)

    def body(c_ref, a_ref, b_ref, o_ref):
        del c_ref
        o_ref[0] = (a_ref[...].astype(F32) + b_ref[...].astype(F32)).astype(BF16)

    spec = _half_block_spec(shard_shape, axis, tr)
    return pl.pallas_call(
        body, name=name,
        grid_spec=pltpu.PrefetchScalarGridSpec(
            num_scalar_prefetch=1, grid=(N_CHIPS, hr // tr), in_specs=[spec, spec],
            out_specs=pl.BlockSpec((1, tr, cs), lambda k, i, c_ref: (k, i, 0))),
        out_shape=jax.ShapeDtypeStruct((N_CHIPS, hr, cs), BF16),
        compiler_params=_params(2, False))(cvec, g, recv)


def _sum_chips(sums, landed, kc, name):
    _, hr, cs = sums.shape
    tr = _tile(hr, 256, 16)

    def body(kc_ref, own_ref, a_ref, b_ref, c_ref, o_ref):
        del kc_ref
        o_ref[...] = (own_ref[0].astype(F32) + a_ref[0].astype(F32)) + (b_ref[0].astype(F32) + c_ref[0].astype(F32))

    slot = lambda j: pl.BlockSpec((1, tr, cs), lambda i, kc_ref: ((kc_ref[0] + j) % N_CHIPS, i, 0))
    return pl.pallas_call(
        body, name=name,
        grid_spec=pltpu.PrefetchScalarGridSpec(
            num_scalar_prefetch=1, grid=(hr // tr,), in_specs=[slot(0), slot(1), slot(2), slot(3)],
            out_specs=pl.BlockSpec((tr, cs), lambda i, kc_ref: (kc_ref[1] * (hr // tr) + i, 0))),
        out_shape=jax.ShapeDtypeStruct((2 * hr, cs), F32),
        compiler_params=_params(1, False))(kc, sums, landed, landed, landed)


def _exchange_halves(shards, name):
    nw = len(shards)

    def body(*refs):
        out_refs = refs[nw:2 * nw]
        send, recv = refs[2 * nw:]
        x, y, c = _my_pos()
        sib = (x, y, 1 - c)
        copies = []
        for w in range(nw):
            hr = shards[w].shape[0] // 2
            mine = out_refs[w].at[pl.ds(pl.multiple_of(c * hr, 8), hr), :]
            cp = _remote(mine, mine, send.at[w], recv.at[w], sib)
            cp.start()
            copies.append(cp)
        for w in range(nw):
            hr = shards[w].shape[0] // 2
            other = out_refs[w].at[pl.ds(pl.multiple_of((1 - c) * hr, 8), hr), :]
            _remote(other, other, send.at[w], recv.at[w], sib).wait_recv()
        for cp in copies:
            cp.wait_send()

    anyspec = pl.BlockSpec(memory_space=pl.ANY)
    return pl.pallas_call(
        body, name=name,
        out_shape=[jax.ShapeDtypeStruct(s.shape, F32) for s in shards],
        in_specs=[anyspec] * nw, out_specs=[anyspec] * nw,
        input_output_aliases={w: w for w in range(nw)},
        scratch_shapes=[pltpu.SemaphoreType.DMA((nw,)), pltpu.SemaphoreType.DMA((nw,))])(*shards)


def _cast_into_full(w, kc, axis, name):
    r, cs = w.shape
    tr = _tile(r, 256, 16)

    def body(kc_ref, w_ref, o_ref):
        del kc_ref
        o_ref[...] = w_ref[...].astype(BF16)

    if axis == 1:
        ospec = pl.BlockSpec((tr, cs), lambda i, kc_ref: (i, kc_ref[0]))
    else:
        ospec = pl.BlockSpec((tr, cs), lambda i, kc_ref: (kc_ref[0] * (r // tr) + i, 0))
    return pl.pallas_call(
        body, name=name,
        grid_spec=pltpu.PrefetchScalarGridSpec(
            num_scalar_prefetch=1, grid=(r // tr,), in_specs=[pl.BlockSpec((tr, cs), lambda i, kc_ref: (i, 0))],
            out_specs=ospec),
        out_shape=jax.ShapeDtypeStruct(_full_shape((r, cs), axis), BF16),
        compiler_params=_params(1, False))(kc, w)


def _adam_math(w, g, m, v):
    m2 = ADAM_B1 * m + (1.0 - ADAM_B1) * g
    v2 = ADAM_B2 * v + (1.0 - ADAM_B2) * (g * g)
    m_hat = m2 / (1.0 - ADAM_B1 ** ADAM_STEP)
    v_hat = v2 / (1.0 - ADAM_B2 ** ADAM_STEP)
    delta = -ADAM_LR * (m_hat / (jnp.sqrt(v_hat) + ADAM_EPS) + ADAM_WD * w)
    return delta, m2, v2


def _adam(w, g, m, v, name):
    r, cs = w.shape
    tr = _tile(r, 256, 8)

    def body(w_ref, g_ref, m_ref, v_ref, d_ref, m2_ref, v2_ref):
        d, m2, v2 = _adam_math(w_ref[...], g_ref[...], m_ref[...], v_ref[...])
        d_ref[...] = d
        m2_ref[...] = m2
        v2_ref[...] = v2

    spec = pl.BlockSpec((tr, cs), lambda i: (i, 0))
    shp = jax.ShapeDtypeStruct((r, cs), F32)
    return pl.pallas_call(body, name=name, grid=(r // tr,), in_specs=[spec] * 4, out_specs=[spec] * 3,
                          out_shape=[shp, shp, shp], compiler_params=_params(1, False))(w, g, m, v)


def _mod_rows(a16, w, b, name):
    D, n = w.shape
    tn = _tile(n, 512)

    def body(a_ref, w_ref, b_ref, o_ref):
        a = a_ref[...]
        o_ref[...] = _nn((a * _sigmoid(a)).astype(BF16), w_ref[...].astype(BF16)) + b_ref[...]

    return pl.pallas_call(
        body, name=name, grid=(n // tn,),
        in_specs=[pl.BlockSpec((16, D), lambda j: (0, 0)), pl.BlockSpec((D, tn), lambda j: (0, j)),
                  pl.BlockSpec((1, tn), lambda j: (0, j))],
        out_specs=pl.BlockSpec((16, tn), lambda j: (0, j)),
        out_shape=jax.ShapeDtypeStruct((16, n), F32), compiler_params=_params(1, False))(a16, w, b)


def _w_mod_update(a16, d16, w, m, v):
    D, n = w.shape
    tn = _tile(n, 256)

    def body(a_ref, d_ref, w_ref, m_ref, v_ref, g_ref, dl_ref, m2_ref, v2_ref, p_ref):
        @pl.when(pl.program_id(0) == 0)
        def _():
            p_ref[...] = jnp.zeros_like(p_ref)
        a = a_ref[...]
        db = d_ref[...].astype(BF16)
        wv = w_ref[...]
        g = _tn((a * _sigmoid(a)).astype(BF16), db)
        g_ref[...] = g
        d, m2, v2 = _adam_math(wv, g, m_ref[...], v_ref[...])
        dl_ref[...] = d
        m2_ref[...] = m2
        v2_ref[...] = v2
        p_ref[...] += _nt(db, wv.astype(BF16))

    wspec = pl.BlockSpec((D, tn), lambda j: (0, j))
    shp = jax.ShapeDtypeStruct((D, n), F32)
    return pl.pallas_call(
        body, name="w_mod_update", grid=(n // tn,),
        in_specs=[pl.BlockSpec((16, D), lambda j: (0, 0)), pl.BlockSpec((16, tn), lambda j: (0, j)), wspec, wspec, wspec],
        out_specs=[wspec, wspec, wspec, wspec, pl.BlockSpec((16, D), lambda j: (0, 0))],
        out_shape=[shp, shp, shp, shp, jax.ShapeDtypeStruct((16, D), F32)],
        compiler_params=_params(1))(a16, d16, w, m, v)


def _sum_devices(g8, name):
    _, R, Cc = g8.shape

    def body(g_ref, o_ref):
        t = g_ref[0]
        for d in range(1, N_DEV):
            t = t + g_ref[d]
        o_ref[...] = t

    return pl.pallas_call(body, name=name, out_shape=jax.ShapeDtypeStruct((R, Cc), F32))(g8)


def _c_ctx_grad(parts, c_ctx):
    D = c_ctx.shape[1]

    def body(p_ref, c_ref, o_ref):
        t = p_ref[0]
        for k in range(1, N_CHIPS):
            t = t + p_ref[2 * k]
        cv = c_ref[...]
        sg = _sigmoid(cv)
        o_ref[...] = t * (sg * (1.0 + cv * (1.0 - sg)))

    return pl.pallas_call(body, name="c_ctx_grad", out_shape=jax.ShapeDtypeStruct((1, D), F32))(parts, c_ctx)


def _pack_rows(items, nrows, width, name):
    arrays, plan = [], []
    for a, r0, nr, d0, c0 in items:
        for ai, b in enumerate(arrays):
            if b is a:
                break
        else:
            ai = len(arrays)
            arrays.append(a)
        plan.append((ai, r0, nr, d0, c0, a.shape[1]))

    def body(*refs):
        o_ref = refs[-1]
        o_ref[...] = jnp.zeros_like(o_ref)
        for ai, r0, nr, d0, c0, w in plan:
            o_ref[d0:d0 + nr, c0:c0 + w] = refs[ai][r0:r0 + nr, :]

    return pl.pallas_call(body, name=name, out_shape=jax.ShapeDtypeStruct((nrows, width), F32))(*arrays)


HBM_SPEC = pl.BlockSpec(memory_space=pltpu.HBM)
SEM_SPEC = pl.BlockSpec(memory_space=pltpu.SEMAPHORE)
SPLIT_PARAMS = pltpu.CompilerParams(has_side_effects=pltpu.SideEffectType.DATAFLOW_SIDE_EFFECTING)


def _in_hbm(a):
    return pltpu.with_memory_space_constraint(a, pltpu.HBM)


def _ag_chips_start(fulls, shapes, axes, after, name):
    nw = len(fulls)

    def body(*refs):
        in_refs, send, recv, token = refs[:nw], refs[nw + 1], refs[nw + 2], refs[-1]
        x, y, c = _my_pos()
        k0 = 2 * x + y
        for w in range(nw):
            own = _region(in_refs[w], k0, c, shapes[w], axes[w])
            for j, ch in enumerate(_other_chips(x, y)):
                _remote(own, own, send.at[3 * w + j], recv.at[3 * w + j], (ch[0], ch[1], c)).start()
        token[...] = jnp.zeros_like(token)

    return pl.pallas_call(
        body, name=name,
        out_shape=(pltpu.SemaphoreType.DMA((3 * nw,)), pltpu.SemaphoreType.DMA((3 * nw,)),
                   *[pltpu.HBM(f.shape, f.dtype) for f in fulls], jax.ShapeDtypeStruct((8, LANES), F32)),
        in_specs=[HBM_SPEC] * nw + [pl.BlockSpec(memory_space=pl.ANY)],
        out_specs=(SEM_SPEC, SEM_SPEC, *[HBM_SPEC] * nw, pl.BlockSpec(memory_space=pltpu.VMEM)),
        input_output_aliases={w: 2 + w for w in range(nw)},
        compiler_params=SPLIT_PARAMS)(*[_in_hbm(f) for f in fulls], after)


def _ag_chips_wait(send, recv, fulls, shapes, axes, after, name):
    nw = len(fulls)

    def body(*refs):
        in_refs, send_ref, recv_ref = refs[:nw], refs[nw], refs[nw + 1]
        x, y, c = _my_pos()
        k0 = 2 * x + y
        for w in range(nw):
            own = _region(in_refs[w], k0, c, shapes[w], axes[w])
            for j, ch in enumerate(_other_chips(x, y)):
                got = _region(in_refs[w], 2 * ch[0] + ch[1], c, shapes[w], axes[w])
                cp = _remote(own, got, send_ref.at[3 * w + j], recv_ref.at[3 * w + j], (ch[0], ch[1], c))
                cp.wait_send()
                cp.wait_recv()

    return pl.pallas_call(
        body, name=name,
        out_shape=tuple(pltpu.HBM(f.shape, f.dtype) for f in fulls),
        in_specs=[HBM_SPEC] * nw + [SEM_SPEC, SEM_SPEC, pl.BlockSpec(memory_space=pl.ANY)],
        out_specs=tuple([HBM_SPEC] * nw),
        input_output_aliases={w: w for w in range(nw)},
        compiler_params=SPLIT_PARAMS)(*fulls, send, recv, after)


def _ag_forward(fulls, shapes, axes, name):
    nw = len(fulls)

    def body(*refs):
        out_refs = refs[nw:2 * nw]
        send, recv = refs[2 * nw:]
        x, y, c = _my_pos()
        sib = (x, y, 1 - c)
        chips = _other_chips(x, y)
        copies = []
        for w in range(nw):
            for j, ch in enumerate(chips):
                got = _region(out_refs[w], 2 * ch[0] + ch[1], c, shapes[w], axes[w])
                cp = _remote(got, got, send.at[w, j], recv.at[w, j], sib)
                cp.start()
                copies.append(cp)
        for w in range(nw):
            for j, ch in enumerate(chips):
                got = _region(out_refs[w], 2 * ch[0] + ch[1], 1 - c, shapes[w], axes[w])
                _remote(got, got, send.at[w, j], recv.at[w, j], sib).wait_recv()
        for cp in copies:
            cp.wait_send()

    anyspec = pl.BlockSpec(memory_space=pl.ANY)
    return pl.pallas_call(
        body, name=name,
        out_shape=[jax.ShapeDtypeStruct(f.shape, BF16) for f in fulls],
        in_specs=[anyspec] * nw, out_specs=[anyspec] * nw,
        input_output_aliases={w: w for w in range(nw)},
        scratch_shapes=[pltpu.SemaphoreType.DMA((nw, 3)), pltpu.SemaphoreType.DMA((nw, 3))])(*fulls)


def _rs_sibling_start(grads, shapes, axes, name):
    nw = len(grads)
    npc = max(1 if a == 1 else N_CHIPS for a in axes)

    def body(*refs):
        g_refs, l_refs, send, recv, token = refs[:nw], refs[nw:2 * nw], refs[2 * nw], refs[2 * nw + 1], refs[-1]
        x, y, c = _my_pos()
        for w in range(nw):
            src = _half_pieces(g_refs[w], 1 - c, shapes[w], axes[w])
            dst = _half_pieces(l_refs[w], 1 - c, shapes[w], axes[w])
            for i, (s, d) in enumerate(zip(src, dst)):
                _remote(s, d, send.at[npc * w + i], recv.at[npc * w + i], (x, y, 1 - c)).start()
        token[...] = jnp.zeros_like(token)

    thru = [pltpu.HBM(g.shape, g.dtype) for g in grads]
    return pl.pallas_call(
        body, name=name,
        out_shape=(pltpu.SemaphoreType.DMA((npc * nw,)), pltpu.SemaphoreType.DMA((npc * nw,)), *thru, *thru,
                   jax.ShapeDtypeStruct((8, LANES), F32)),
        in_specs=[HBM_SPEC] * (2 * nw),
        out_specs=(SEM_SPEC, SEM_SPEC, *[HBM_SPEC] * (2 * nw), pl.BlockSpec(memory_space=pltpu.VMEM)),
        input_output_aliases={i: 2 + i for i in range(2 * nw)},
        compiler_params=SPLIT_PARAMS)(*[_in_hbm(g) for g in grads], *[_in_hbm(lax.empty(g.shape, g.dtype)) for g in grads])


def _rs_sibling_wait(send, recv, grads, lands, shapes, axes, after, name):
    nw = len(grads)
    npc = max(1 if a == 1 else N_CHIPS for a in axes)

    def body(*refs):
        g_refs, l_refs, send_ref, recv_ref = refs[:nw], refs[nw:2 * nw], refs[2 * nw], refs[2 * nw + 1]
        x, y, c = _my_pos()
        for w in range(nw):
            sent = _half_pieces(g_refs[w], 1 - c, shapes[w], axes[w])
            mine = _half_pieces(l_refs[w], c, shapes[w], axes[w])
            for i, (s, d) in enumerate(zip(sent, mine)):
                cp = _remote(s, d, send_ref.at[npc * w + i], recv_ref.at[npc * w + i], (x, y, 1 - c))
                cp.wait_send()
                cp.wait_recv()

    thru = tuple(pltpu.HBM(g.shape, g.dtype) for g in grads)
    return pl.pallas_call(
        body, name=name, out_shape=thru + thru,
        in_specs=[HBM_SPEC] * (2 * nw) + [SEM_SPEC, SEM_SPEC, pl.BlockSpec(memory_space=pl.ANY)],
        out_specs=tuple([HBM_SPEC] * (2 * nw)),
        input_output_aliases={i: i for i in range(2 * nw)},
        compiler_params=SPLIT_PARAMS)(*grads, *lands, send, recv, after)


def _rs_chips_start(sums, name):
    nw = len(sums)

    def body(*refs):
        s_refs, l_refs, send, recv, token = refs[:nw], refs[nw:2 * nw], refs[2 * nw], refs[2 * nw + 1], refs[-1]
        x, y, c = _my_pos()
        k0 = 2 * x + y
        for w in range(nw):
            for j, ch in enumerate(_other_chips(x, y)):
                _remote(s_refs[w].at[2 * ch[0] + ch[1]], l_refs[w].at[k0], send.at[3 * w + j], recv.at[3 * w + j],
                        (ch[0], ch[1], c)).start()
        token[...] = jnp.zeros_like(token)

    thru = [pltpu.HBM(s.shape, s.dtype) for s in sums]
    return pl.pallas_call(
        body, name=name,
        out_shape=(pltpu.SemaphoreType.DMA((3 * nw,)), pltpu.SemaphoreType.DMA((3 * nw,)), *thru, *thru,
                   jax.ShapeDtypeStruct((8, LANES), F32)),
        in_specs=[HBM_SPEC] * (2 * nw),
        out_specs=(SEM_SPEC, SEM_SPEC, *[HBM_SPEC] * (2 * nw), pl.BlockSpec(memory_space=pltpu.VMEM)),
        input_output_aliases={i: 2 + i for i in range(2 * nw)},
        compiler_params=SPLIT_PARAMS)(*[_in_hbm(s) for s in sums], *[_in_hbm(lax.empty(s.shape, s.dtype)) for s in sums])


def _rs_chips_wait(send, recv, sums, lands, after, name):
    nw = len(sums)

    def body(*refs):
        s_refs, l_refs, send_ref, recv_ref = refs[:nw], refs[nw:2 * nw], refs[2 * nw], refs[2 * nw + 1]
        x, y, c = _my_pos()
        for w in range(nw):
            for j, ch in enumerate(_other_chips(x, y)):
                kj = 2 * ch[0] + ch[1]
                cp = _remote(s_refs[w].at[kj], l_refs[w].at[kj], send_ref.at[3 * w + j], recv_ref.at[3 * w + j],
                             (ch[0], ch[1], c))
                cp.wait_send()
                cp.wait_recv()

    thru = tuple(pltpu.HBM(s.shape, s.dtype) for s in sums)
    return pl.pallas_call(
        body, name=name, out_shape=thru + thru,
        in_specs=[HBM_SPEC] * (2 * nw) + [SEM_SPEC, SEM_SPEC, pl.BlockSpec(memory_space=pl.ANY)],
        out_specs=tuple([HBM_SPEC] * (2 * nw)),
        input_output_aliases={i: i for i in range(2 * nw)},
        compiler_params=SPLIT_PARAMS)(*sums, *lands, send, recv, after)


LOSS_LANE = 64


def kernel(x, c, ctx, c_ctx, w_mod, b_mod, norm_mix, norm_ffn, w_in, ret_decay, attn_sink, w_out, w_gate, w_up, w_down, norm_final, loss_target, m_c_ctx, m_w_mod, m_b_mod, m_norm_mix, m_norm_ffn, m_w_in, m_ret_decay, m_attn_sink, m_w_out, m_w_gate, m_w_up, m_w_down, m_norm_final, v_c_ctx, v_w_mod, v_b_mod, v_norm_mix, v_norm_ffn, v_w_in, v_ret_decay, v_attn_sink, v_w_out, v_w_gate, v_w_up, v_w_down, v_norm_final):
    D = x.shape[-1]
    n3 = w_mod.shape[-1]
    xi, yi, ci = _my_pos()
    b = 4 * xi + 2 * yi + ci
    k0 = 2 * xi + yi
    cvec = jnp.reshape(ci, (1,)).astype(jnp.int32)
    kc = jnp.stack([k0, ci]).astype(jnp.int32)

    dense = [("w_in", w_in[0], 1), ("w_out", w_out[0], 0), ("w_gate", w_gate[0], 1), ("w_up", w_up[0], 1),
             ("w_down", w_down[0], 0)]
    axes = [a for _, _, a in dense]
    shapes = [w.shape for _, w, _ in dense]
    c_all = _allgather8(c, "gather_c").reshape(N_DEV, D)
    c_ctx2 = c_ctx.reshape(1, D)
    a16 = _pack_rows([(c_all, 0, N_DEV, 0, 0), (c_ctx2, 0, 1, N_DEV, 0)], 16, D, "pack_cond")
    b_cols = lax.dynamic_slice_in_dim(b_mod, k0 * n3, n3, axis=1)
    mod16 = _mod_rows(a16, w_mod[0], b_cols, "mod_rows")
    mod_all = _allgather8(mod16, "gather_mod")

    own_in = _cast_into_full(dense[0][1], kc, axes[0], "cast_w_in")
    agi = _ag_chips_start([own_in], shapes[:1], axes[:1], mod_all, "ag_in_start")
    own16 = [_cast_into_full(w, kc, a, "cast_" + n) for n, w, a in dense[1:]]
    (f_in,) = _ag_forward(list(_ag_chips_wait(agi[0], agi[1], [agi[2]], shapes[:1], axes[:1], own16[-1], "ag_in_wait")),
                          shapes[:1], axes[:1], "ag_in_forward")
    ag = _ag_chips_start(own16, shapes[1:], axes[1:], f_in, "ag_rest_start")
    ag_send, ag_recv, ag_thru, ag_tok = ag[0], ag[1], list(ag[2:-1]), ag[-1][0:1, 0:1]

    def rest_weights(after):
        landed_w = _ag_chips_wait(ag_send, ag_recv, ag_thru, shapes[1:], axes[1:], after, "ag_rest_wait")
        return _ag_forward(list(landed_w), shapes[1:], axes[1:], "ag_rest_forward")
    mine = jnp.stack([lax.dynamic_index_in_dim(mod_all, 2 * k + ci, 0, keepdims=False) for k in range(N_CHIPS)])
    mod = lax.dynamic_index_in_dim(mine, b, 1, keepdims=False).reshape(6, D)
    modc = mine[:, N_DEV].reshape(6, D)

    lg = -jnp.exp(ret_decay[0])

    index = {n: i for i, (n, _, _) in enumerate(dense)}
    pending, done = [], {}

    sib = []

    def finish_sibling(after):
        names, shp, axs, st = sib.pop()
        nw = len(names)
        res = _rs_sibling_wait(st[0], st[1], list(st[2:2 + nw]), list(st[2 + nw:2 + 2 * nw]), shp, axs, after,
                               "rs_sibling_wait_" + names[0])
        sums = [_add_halves(res[i], res[nw + i], cvec, s, a, "add_halves_" + n)
                for i, (s, a, n) in enumerate(zip(shp, axs, names))]
        ch = _rs_chips_start(sums, "rs_chips_start_" + names[0])
        pending.append((names, ch[0], ch[1], list(ch[2:2 + nw]), list(ch[2 + nw:2 + 2 * nw])))
        return ch[-1][0:1, 0:1]

    def on_grads(names, gs):
        ids = [index[n] for n in names]
        shp, axs = [shapes[i] for i in ids], [axes[i] for i in ids]
        st = _rs_sibling_start(gs, shp, axs, "rs_sibling_start_" + names[0])
        sib.append((names, shp, axs, st))
        return st[-1][0:1, 0:1]

    out = _local_step(x[0], ctx[0], loss_target[0], mod, modc, norm_mix + ag_tok, norm_ffn, norm_final.reshape(1, D), lg,
                      attn_sink, f_in, rest_weights, on_grads, finish_sibling)

    def finish(group, after):
        names, send, recv, sums, lands = group
        res = _rs_chips_wait(send, recv, sums, lands, after, "rs_chips_wait_" + names[0])
        return [_sum_chips(res[i], res[len(names) + i], kc, "sum_chips_" + n) for i, n in enumerate(names)]

    tok_in = finish_sibling(out["grad_x"])
    assert pending[-1][0] == ["w_in"]
    rest_names = [n for g in pending[:-1] for n in g[0]]
    after_in = out["small"][0:8, 0:LANES] + tok_in
    rest_halves = [h for g in pending[:-1] for h in finish(g, after_in)]
    g_rest = dict(zip(rest_names, _exchange_halves(rest_halves, "exchange_halves_rest")))

    nh = 2 * RET_HEADS
    small_all = _allgather8(out["small"], "gather_small")
    tot = _sum_devices(small_all, "sum_small")
    g_b_mod = (tot[0:6] + tot[6:12]).reshape(1, 6 * D)
    dmodc_tot = tot[6:12].reshape(1, 6 * D)
    dmod_rows = small_all[:, 0:6].reshape(N_DEV, 6 * D)
    d16 = _pack_rows([(dmod_rows, 0, N_DEV, 0, 0), (dmodc_tot, 0, 1, N_DEV, 0)], 16, 6 * D, "pack_dmod")
    d16 = lax.dynamic_slice_in_dim(d16, k0 * n3, n3, axis=1)
    g_w_mod, dl_w_mod, m2_w_mod, v2_w_mod, part = _w_mod_update(a16, d16, w_mod[0], m_w_mod[0], v_w_mod[0])
    part_all = _allgather8(part[N_DEV:N_DEV + 1], "gather_c_ctx")
    g_c_ctx = _c_ctx_grad(part_all, c_ctx2)
    loss = tot[15, LOSS_LANE]
    g_ret_decay = tot[15, :nh].reshape(1, 2, RET_HEADS)
    g_sink = tot[15, nh:nh + ATT_HEADS].reshape(1, ATT_HEADS)

    def pack(cc, bm, nm, nf, nfin, rd, sk, name):
        rd2 = rd.reshape(2, RET_HEADS)
        return _pack_rows([(bm.reshape(6, D), 0, 6, 0, 0), (cc.reshape(1, D), 0, 1, 6, 0), (nm.reshape(1, D), 0, 1, 7, 0),
                           (nf.reshape(1, D), 0, 1, 8, 0), (nfin.reshape(1, D), 0, 1, 9, 0),
                           (rd2, 0, 1, 10, 0), (rd2, 1, 1, 10, RET_HEADS), (sk.reshape(1, ATT_HEADS), 0, 1, 10, nh)],
                          16, D, name)

    w_s = pack(c_ctx, b_mod, norm_mix, norm_ffn, norm_final, ret_decay, attn_sink, "pack_w")
    g_s = _pack_rows([(g_b_mod.reshape(6, D), 0, 6, 0, 0), (g_c_ctx, 0, 1, 6, 0), (tot, 12, 3, 7, 0),
                      (tot[15:16, 0:nh + ATT_HEADS], 0, 1, 10, 0)], 16, D, "pack_g")
    m_s = pack(m_c_ctx, m_b_mod, m_norm_mix, m_norm_ffn, m_norm_final, m_ret_decay, m_attn_sink, "pack_m")
    v_s = pack(v_c_ctx, v_b_mod, v_norm_mix, v_norm_ffn, v_norm_final, v_ret_decay, v_attn_sink, "pack_v")
    small_upd = _adam(w_s, g_s, m_s, v_s, "adam_small")

    def unpack(t):
        return dict(b_mod=t[0:6].reshape(1, 6 * D), c_ctx=t[6], norm_mix=t[7:8], norm_ffn=t[8:9], norm_final=t[9],
                    ret_decay=t[10, :nh].reshape(1, 2, RET_HEADS), attn_sink=t[10, nh:nh + ATT_HEADS].reshape(1, ATT_HEADS))

    dense_w = dict(w_in=(w_in, m_w_in, v_w_in), w_out=(w_out, m_w_out, v_w_out), w_gate=(w_gate, m_w_gate, v_w_gate),
                   w_up=(w_up, m_w_up, v_w_up), w_down=(w_down, m_w_down, v_w_down))
    grads = dict(unpack(g_s), w_mod=g_w_mod[None])
    upd = [dict(unpack(t)) for t in small_upd]
    upd[0]["w_mod"], upd[1]["w_mod"], upd[2]["w_mod"] = dl_w_mod[None], m2_w_mod[None], v2_w_mod[None]
    def update(n, g):
        w_, m_, v_ = dense_w[n]
        res = _adam(w_[0], g, m_[0], v_[0], "adam_" + n)
        grads[n] = g[None]
        for u, r_ in zip(upd, res):
            u[n] = r_[None]
        return res[0]

    dep = small_upd[0][0:1, 0:1] + dl_w_mod[0:1, 0:1]
    for n in rest_names:
        dep = dep + update(n, g_rest[n])[0:1, 0:1]
    (g_in,) = _exchange_halves(finish(pending[-1], dep), "exchange_halves_in")
    update("w_in", g_in)

    order = ['c_ctx', 'w_mod', 'b_mod', 'norm_mix', 'norm_ffn', 'w_in', 'ret_decay', 'attn_sink', 'w_out', 'w_gate',
             'w_up', 'w_down', 'norm_final']
    outs = [loss, out["grad_x"][None]] + [grads[n] for n in order]
    for u in upd:
        outs += [u[n] for n in order]
    return tuple(outs)
```

```python
import numpy as np
import jax
import jax.numpy as jnp
from jax import lax
from jax.experimental import pallas as pl
from jax.experimental.pallas import tpu as pltpu

F32 = jnp.float32
BF16 = jnp.bfloat16

RET_HEADS = 8
RET_DK = 64
RET_DV = 128
CHUNK = 128
ATT_HEADS = 16
ATT_KV_HEADS = 4
ATT_DH = 64
GRID_W = 64
ROPE_BASE = 10000.0
NORM_EPS = 1e-6
ADAM_LR = 0.001
ADAM_B1 = 0.9
ADAM_B2 = 0.999
ADAM_EPS = 1e-08
ADAM_WD = 0.01
ADAM_STEP = 10
NEG = -1e30
LANES = 128
VMEM_LIMIT = 56 * 1024 * 1024
ROWS_PER_LATCH = 1024
MESH = pl.DeviceIdType.MESH
N_CHIPS = 4
N_DEV = 8


def _nn(a, b):
    return jnp.dot(a, b, preferred_element_type=F32)


def _nt(a, b):
    return lax.dot_general(a, b, (((1,), (1,)), ((), ())), preferred_element_type=F32)


def _tn(a, b):
    return lax.dot_general(a, b, (((0,), (0,)), ((), ())), preferred_element_type=F32)


def _tile(n, pref, unit=LANES):
    t = min(n, pref)
    t -= t % unit
    while t > unit and n % t:
        t -= unit
    if t <= 0 or n % t:
        return n
    return t


def _params(ndim, vmem=True):
    return pltpu.CompilerParams(dimension_semantics=("arbitrary",) * ndim,
                                vmem_limit_bytes=VMEM_LIMIT if vmem else None)


def _sigmoid(x):
    return 0.5 * jnp.tanh(0.5 * x) + 0.5


def _fsum(x):
    return jnp.sum(jnp.sum(x, axis=0, keepdims=True), axis=1, keepdims=True)


def _rope_tables(L):
    lane = np.arange(LANES)
    d = lane % 64
    inv_r = jnp.asarray(ROPE_BASE, F32) ** (-jnp.arange(32, dtype=F32) / 32)
    t = jnp.arange(L)
    ang_r = t.astype(F32)[:, None] * jnp.tile(inv_r, LANES // 32)[None, :]
    Rr = np.zeros((LANES, LANES), np.float32)
    for l in range(LANES):
        if d[l] < 32:
            Rr[l + 32, l] = -1.0
        else:
            Rr[l - 32, l] = 1.0
    inv_a = jnp.asarray(ROPE_BASE, F32) ** (-jnp.arange(16, dtype=F32) / 16)
    rows = (t // GRID_W).astype(F32)
    cols = (t % GRID_W).astype(F32)
    dd = d % 32
    pos = jnp.where(jnp.asarray(d < 32)[None, :], rows[:, None], cols[:, None])
    ang_a = pos * jnp.tile(inv_a, LANES // 16)[None, :]
    Ra = np.zeros((LANES, LANES), np.float32)
    for l in range(LANES):
        if dd[l] < 16:
            Ra[l + 16, l] = -1.0
        else:
            Ra[l - 16, l] = 1.0
    D0 = np.zeros((LANES, LANES), np.float32)
    D1 = np.zeros((LANES, LANES), np.float32)
    for l in range(LANES):
        D0[l % 64, l] = 1.0
        D1[64 + l % 64, l] = 1.0
    return dict(
        Cr=jnp.cos(ang_r), Sr=jnp.sin(ang_r), Rr=jnp.asarray(Rr, BF16), RrT=jnp.asarray(Rr.T, BF16),
        Ca=jnp.cos(ang_a), Sa=jnp.sin(ang_a), Ra=jnp.asarray(Ra, BF16), RaT=jnp.asarray(Ra.T, BF16),
        D0=jnp.asarray(D0, BF16), D1=jnp.asarray(D1, BF16),
        D0T=jnp.asarray(D0.T, BF16), D1T=jnp.asarray(D1.T, BF16))


def _norm_mod(xf, g, sh, sc):
    r = lax.rsqrt(jnp.mean(xf * xf, axis=-1, keepdims=True) + NORM_EPS)
    return (xf * r * g) * (1.0 + sc) + sh


def _norm_mod_matmul(x, g, sh, sc, w, name):
    M, D = x.shape
    N = w.shape[1]
    tm, tn = _tile(M, ROWS_PER_LATCH, 8), _tile(N, 768)

    def body(x_ref, g_ref, sh_ref, sc_ref, w_ref, p_ref, h_ref, hs):
        @pl.when(pl.program_id(1) == 0)
        def _():
            hb = _norm_mod(x_ref[...], g_ref[...], sh_ref[...], sc_ref[...]).astype(BF16)
            hs[...] = hb
            h_ref[...] = hb
        p_ref[...] = _nn(hs[...], w_ref[...]).astype(BF16)

    vec = pl.BlockSpec((1, D), lambda i, j: (0, 0))
    return pl.pallas_call(
        body, name=name, grid=(M // tm, N // tn),
        in_specs=[pl.BlockSpec((tm, D), lambda i, j: (i, 0)), vec, vec, vec,
                  pl.BlockSpec((D, tn), lambda i, j: (0, j))],
        out_specs=[pl.BlockSpec((tm, tn), lambda i, j: (i, j)), pl.BlockSpec((tm, D), lambda i, j: (i, 0))],
        out_shape=[jax.ShapeDtypeStruct((M, N), BF16), jax.ShapeDtypeStruct((M, D), BF16)],
        scratch_shapes=[pltpu.VMEM((tm, D), BF16)],
        compiler_params=_params(2))(x, g, sh, sc, w)


def _proj_residual(a, w, xres, gt, name):
    M, K = a.shape
    N = w.shape[1]
    tm, tn = _tile(M, ROWS_PER_LATCH, 8), _tile(N, 1024 if K <= 2048 else 512)

    def body(a_ref, w_ref, x_ref, gt_ref, xo_ref, o_ref):
        o = _nn(a_ref[...], w_ref[...])
        o_ref[...] = o.astype(BF16)
        xo_ref[...] = x_ref[...] + gt_ref[...] * o

    return pl.pallas_call(
        body, name=name, grid=(M // tm, N // tn),
        in_specs=[pl.BlockSpec((tm, K), lambda i, j: (i, 0)), pl.BlockSpec((K, tn), lambda i, j: (0, j)),
                  pl.BlockSpec((tm, tn), lambda i, j: (i, j)), pl.BlockSpec((1, tn), lambda i, j: (0, j))],
        out_specs=[pl.BlockSpec((tm, tn), lambda i, j: (i, j)), pl.BlockSpec((tm, tn), lambda i, j: (i, j))],
        out_shape=[jax.ShapeDtypeStruct((M, N), F32), jax.ShapeDtypeStruct((M, N), BF16)],
        compiler_params=_params(2))(a, w, xres, gt)


def _ffn_in(x1, g, sh, sc, wg, wu):
    M, D = x1.shape
    N = wg.shape[1]
    tm, tn = _tile(M, ROWS_PER_LATCH, 8), _tile(N, 512)

    def body(x_ref, g_ref, sh_ref, sc_ref, wg_ref, wu_ref, G_ref, U_ref, A_ref, h_ref, hs):
        @pl.when(pl.program_id(1) == 0)
        def _():
            hb = _norm_mod(x_ref[...], g_ref[...], sh_ref[...], sc_ref[...]).astype(BF16)
            hs[...] = hb
            h_ref[...] = hb
        G = _nn(hs[...], wg_ref[...])
        U = _nn(hs[...], wu_ref[...])
        G_ref[...] = G.astype(BF16)
        U_ref[...] = U.astype(BF16)
        A_ref[...] = (G * _sigmoid(G) * U).astype(BF16)

    vec = pl.BlockSpec((1, D), lambda i, j: (0, 0))
    wspec = pl.BlockSpec((D, tn), lambda i, j: (0, j))
    ospec = pl.BlockSpec((tm, tn), lambda i, j: (i, j))
    big = jax.ShapeDtypeStruct((M, N), BF16)
    return pl.pallas_call(
        body, name="ffn_in", grid=(M // tm, N // tn),
        in_specs=[pl.BlockSpec((tm, D), lambda i, j: (i, 0)), vec, vec, vec, wspec, wspec],
        out_specs=[ospec, ospec, ospec, pl.BlockSpec((tm, D), lambda i, j: (i, 0))],
        out_shape=[big, big, big, jax.ShapeDtypeStruct((M, D), BF16)],
        scratch_shapes=[pltpu.VMEM((tm, D), BF16)],
        compiler_params=_params(2))(x1, g, sh, sc, wg, wu)


def _final(x2, gn, tgt, gt, saved):
    M, D = x2.shape
    tm = _tile(M, 256, 8)

    def body(x_ref, g_ref, t_ref, gt_ref, sv_ref, dx_ref, loss_ref, dg_ref, dz_ref, dgt_ref):
        @pl.when(pl.program_id(0) == 0)
        def _():
            loss_ref[...] = jnp.zeros_like(loss_ref)
            dg_ref[...] = jnp.zeros_like(dg_ref)
            dgt_ref[...] = jnp.zeros_like(dgt_ref)
        x = x_ref[...]
        g = g_ref[...]
        r = lax.rsqrt(jnp.mean(x * x, axis=-1, keepdims=True) + NORM_EPS)
        xh = x * r
        e = xh * g - t_ref[...]
        loss_ref[...] += (0.5 / D) * _fsum(e * e)
        dy = e * (1.0 / D)
        dg_ref[...] += jnp.sum(dy * xh, axis=0, keepdims=True)
        dxh = dy * g
        d = r * (dxh - xh * jnp.mean(dxh * xh, axis=-1, keepdims=True))
        dx_ref[...] = d
        dz_ref[...] = (d * gt_ref[...]).astype(BF16)
        dgt_ref[...] += jnp.sum(d * sv_ref[...].astype(F32), axis=0, keepdims=True)

    row = pl.BlockSpec((tm, D), lambda i: (i, 0))
    vec = pl.BlockSpec((1, D), lambda i: (0, 0))
    return pl.pallas_call(
        body, name="final_loss", grid=(M // tm,),
        in_specs=[row, vec, row, vec, row],
        out_specs=[row, pl.BlockSpec((1, LANES), lambda i: (0, 0)), vec, row, vec],
        out_shape=[jax.ShapeDtypeStruct((M, D), F32), jax.ShapeDtypeStruct((1, LANES), F32),
                   jax.ShapeDtypeStruct((1, D), F32), jax.ShapeDtypeStruct((M, D), BF16),
                   jax.ShapeDtypeStruct((1, D), F32)],
        compiler_params=_params(1))(x2, gn, tgt, gt, saved)


def _col_group(blk0, nblk):
    return int(np.gcd(blk0, nblk)) if blk0 else nblk


def _rope_cols(src, blk0, nblk, Ct, St, R, scale, rope, name):
    M = src.shape[0]
    tm = _tile(M, 512, 8)
    wb = _col_group(blk0, nblk)

    def body(x_ref, c_ref, s_ref, r_ref, o_ref):
        for j in range(wb):
            cols = slice(j * LANES, (j + 1) * LANES)
            x = x_ref[:, cols]
            xf = x.astype(F32)
            if rope:
                xf = xf * c_ref[...] + _nn(x.astype(BF16), r_ref[...]) * s_ref[...]
            o_ref[:, cols] = (xf * scale).astype(BF16)

    tab = pl.BlockSpec((tm, LANES), lambda i, j: (i, 0))
    return pl.pallas_call(
        body, name=name, grid=(M // tm, nblk // wb),
        in_specs=[pl.BlockSpec((tm, wb * LANES), lambda i, j: (i, blk0 // wb + j)), tab, tab,
                  pl.BlockSpec((LANES, LANES), lambda i, j: (0, 0))],
        out_specs=pl.BlockSpec((tm, wb * LANES), lambda i, j: (i, j)),
        out_shape=jax.ShapeDtypeStruct((M, nblk * LANES), BF16),
        compiler_params=_params(2, False))(src, Ct, St, R)


def _dup_heads(src, blk0, npair, Ct, St, R, D0, D1, rope, name):
    M = src.shape[0]
    tm = _tile(M, 512, 8)

    def body(x_ref, c_ref, s_ref, r_ref, d0_ref, d1_ref, o_ref):
        x = x_ref[...]
        if rope:
            x = (x.astype(F32) * c_ref[...] + _nn(x, r_ref[...]) * s_ref[...]).astype(BF16)
        o_ref[0] = _nn(x, d0_ref[...]).astype(BF16)
        o_ref[1] = _nn(x, d1_ref[...]).astype(BF16)

    tab = pl.BlockSpec((tm, LANES), lambda i, p: (i, 0))
    mat = pl.BlockSpec((LANES, LANES), lambda i, p: (0, 0))
    return pl.pallas_call(
        body, name=name, grid=(M // tm, npair),
        in_specs=[pl.BlockSpec((tm, LANES), lambda i, p: (i, blk0 + p)), tab, tab, mat, mat, mat],
        out_specs=pl.BlockSpec((2, tm, LANES), lambda i, p: (p, i, 0)),
        out_shape=jax.ShapeDtypeStruct((2 * npair, M, LANES), BF16),
        compiler_params=_params(2, False))(src, Ct, St, R, D0, D1)


def _unrope_cols(dsrc, dst, blk0, nblk, Ct, St, RT, scale, rope, name):
    M = dsrc.shape[0]
    tm = _tile(M, 512, 8)
    wb = _col_group(blk0, nblk)

    def body(x_ref, c_ref, s_ref, r_ref, dst_ref, o_ref):
        del dst_ref
        for j in range(wb):
            cols = slice(j * LANES, (j + 1) * LANES)
            xf = x_ref[:, cols].astype(F32)
            if rope:
                xf = xf * c_ref[...] + _nn((xf * s_ref[...]).astype(BF16), r_ref[...])
            o_ref[:, cols] = (xf * scale).astype(BF16)

    tab = pl.BlockSpec((tm, LANES), lambda i, j: (i, 0))
    return pl.pallas_call(
        body, name=name, grid=(M // tm, nblk // wb),
        in_specs=[pl.BlockSpec((tm, wb * LANES), lambda i, j: (i, j)), tab, tab,
                  pl.BlockSpec((LANES, LANES), lambda i, j: (0, 0)),
                  pl.BlockSpec(memory_space=pl.ANY)],
        out_specs=pl.BlockSpec((tm, wb * LANES), lambda i, j: (i, blk0 // wb + j)),
        out_shape=jax.ShapeDtypeStruct(dst.shape, dst.dtype),
        input_output_aliases={4: 0},
        compiler_params=_params(2, False))(dsrc, Ct, St, RT, dst)


def _fold_heads(parts, dst, blk0, npair, Ct, St, RT, D0T, D1T, rope, name):
    M = parts[0][0].shape[1]
    nb = M // CHUNK
    R = _tile(M, 1024, CHUNK)
    rb = R // CHUNK
    nrefs = sum(1 if s == 0 else 2 for _, s in parts)

    def body(*refs):
        part_refs = list(refs[:nrefs])
        c_ref, s_ref, r_ref, d0_ref, d1_ref, dst_ref, o_ref = refs[nrefs:]
        del dst_ref
        i = pl.program_id(0)
        tot = [jnp.zeros((R, LANES), F32), jnp.zeros((R, LANES), F32)]
        for _, shift in parts:
            main = part_refs.pop(0)
            if shift == 0:
                for e in range(2):
                    tot[e] = tot[e] + main[e].astype(F32)
                continue
            edge = part_refs.pop(0)
            ok = (i + 1) * rb <= nb - 1 if shift > 0 else i > 0
            for e in range(2):
                ed = jnp.where(ok, edge[e].astype(F32), 0.0)
                if rb == 1:
                    tot[e] = tot[e] + ed
                elif shift > 0:
                    tot[e] = tot[e] + jnp.concatenate([main[e, CHUNK:, :].astype(F32), ed], axis=0)
                else:
                    tot[e] = tot[e] + jnp.concatenate([ed, main[e, :R - CHUNK, :].astype(F32)], axis=0)
        f = _nn(tot[0].astype(BF16), d0_ref[...]) + _nn(tot[1].astype(BF16), d1_ref[...])
        if rope:
            f = f * c_ref[...] + _nn((f * s_ref[...]).astype(BF16), r_ref[...])
        o_ref[...] = f.astype(BF16)

    in_specs, args = [], []
    for a, shift in parts:
        assert shift in (-1, 0, 1)
        in_specs.append(pl.BlockSpec((2, R, LANES), lambda i, p: (p, i, 0)))
        args.append(a)
        if shift > 0:
            in_specs.append(pl.BlockSpec((2, CHUNK, LANES), lambda i, p: (p, jnp.minimum((i + 1) * rb, nb - 1), 0)))
            args.append(a)
        elif shift < 0:
            in_specs.append(pl.BlockSpec((2, CHUNK, LANES), lambda i, p: (p, jnp.maximum(i * rb - 1, 0), 0)))
            args.append(a)
    tab = pl.BlockSpec((R, LANES), lambda i, p: (i, 0))
    mat = pl.BlockSpec((LANES, LANES), lambda i, p: (0, 0))
    return pl.pallas_call(
        body, name=name, grid=(M // R, npair),
        in_specs=in_specs + [tab, tab, mat, mat, mat, pl.BlockSpec(memory_space=pl.ANY)],
        out_specs=pl.BlockSpec((R, LANES), lambda i, p: (i, blk0 + p)),
        out_shape=jax.ShapeDtypeStruct(dst.shape, dst.dtype),
        input_output_aliases={nrefs + 5: 0},
        compiler_params=_params(2, False))(*args, Ct, St, RT, D0T, D1T, dst)


def _head_masks():
    lane = lax.broadcasted_iota(jnp.int32, (1, LANES), 1)
    return [lane < 64, lane >= 64]


def _decay_vecs(lam, mu):
    i = lax.broadcasted_iota(jnp.int32, (CHUNK, 1), 0).astype(F32)
    return dict(qf=jnp.exp(lam * (i + 1.0)), kf=jnp.exp(lam * (CHUNK - 1.0 - i)),
                qb=jnp.exp(mu * (CHUNK - i)), kb=jnp.exp(mu * i),
                gf=jnp.exp(lam * float(CHUNK)), gb=jnp.exp(mu * float(CHUNK)), i=i)


def _decay_mask(lam, mu):
    r = lax.broadcasted_iota(jnp.int32, (CHUNK, CHUNK), 0)
    c = lax.broadcasted_iota(jnp.int32, (CHUNK, CHUNK), 1)
    rel = (r - c).astype(F32)
    low = rel >= 0.0
    mf = jnp.exp(lam * jnp.maximum(rel, 0.0))
    mb = jnp.exp(mu * jnp.maximum(-rel, 0.0))
    return jnp.where(low, mf, mb), rel, low


def _lam_of(lg_ref, row, idx):
    return jnp.full((1, 1), lg_ref[row, idx], F32)


def _group_index(pair_blk, npairs):
    assert pair_blk % npairs == 0
    return pair_blk // npairs


def _ret_states_fwd(Kr, P, Krc, Pc, lg, rv_blk, npairs):
    L = Kr.shape[0]
    Lc = Krc.shape[0]
    N, ncc = L // CHUNK, Lc // CHUNK
    rv_grp = _group_index(rv_blk, npairs)

    heads = [(p, h) for p in range(npairs) for h in range(2)]
    kcols = lambda p: slice(p * LANES, (p + 1) * LANES)
    vcols = lambda p, h: slice((2 * p + h) * LANES, (2 * p + h + 1) * LANES)

    def body(lg_ref, k_ref, v_ref, kc_ref, vc_ref, sf_ref, S):
        n = pl.program_id(0)
        masks = _head_masks()

        @pl.when(n == 0)
        def _():
            for p, h in heads:
                lam = _lam_of(lg_ref, 0, 2 * p + h)
                dv = _decay_vecs(lam, lam)
                s = jnp.zeros((LANES, LANES), F32)
                for cc in range(ncc):
                    rows = slice(cc * CHUNK, (cc + 1) * CHUNK)
                    kw = jnp.where(masks[h], kc_ref[rows, kcols(p)].astype(F32) * dv["kf"], 0.0).astype(BF16)
                    s = dv["gf"] * s + _tn(kw, vc_ref[rows, vcols(p, h)])
                S[p, h] = s

        for p, h in heads:
            lam = _lam_of(lg_ref, 0, 2 * p + h)
            dv = _decay_vecs(lam, lam)
            s = S[p, h]
            sf_ref[p, 0, h] = s.astype(BF16)
            kw = jnp.where(masks[h], k_ref[:, kcols(p)].astype(F32) * dv["kf"], 0.0).astype(BF16)
            S[p, h] = dv["gf"] * s + _tn(kw, v_ref[:, vcols(p, h)])

    wq, wv = npairs * LANES, npairs * 2 * LANES
    return pl.pallas_call(
        body, name="ret_states_fwd", grid=(N,),
        in_specs=[pl.BlockSpec(memory_space=pltpu.SMEM),
                  pl.BlockSpec((CHUNK, wq), lambda n: (n, 0)),
                  pl.BlockSpec((CHUNK, wv), lambda n: (n, rv_grp)),
                  pl.BlockSpec((Lc, wq), lambda n: (0, 0)),
                  pl.BlockSpec((Lc, wv), lambda n: (0, rv_grp))],
        out_specs=pl.BlockSpec((npairs, 1, 2, LANES, LANES), lambda n: (0, n, 0, 0, 0)),
        out_shape=jax.ShapeDtypeStruct((npairs, N, 2, LANES, LANES), BF16),
        scratch_shapes=[pltpu.VMEM((npairs, 2, LANES, LANES), F32)],
        compiler_params=_params(1, False))(lg, Kr, P, Krc, Pc)


def _ret_chunk_fwd(q, k, v, sf, sb, hm, lam, mu, Mk):
    dv = _decay_vecs(lam, mu)
    qm = jnp.where(hm, q, jnp.zeros_like(q))
    qmf = qm.astype(F32)
    A = _nt(qm, k)
    Am = A * Mk
    Amb = Am.astype(BF16)
    Qf = (qmf * dv["qf"]).astype(BF16)
    Qb = (qmf * dv["qb"]).astype(BF16)
    O = _nn(Amb, v) + _nn(Qf, sf) + _nn(Qb, sb)
    return dict(dv=dv, Mk=Mk, qm=qm, Am=Am, Amb=Amb, Qf=Qf, Qb=Qb, O=O)


def _ret_out_fwd(Qr, Kr, P, Krc, Pc, SF, lg, rv_blk, rg_blk, npairs, d_mix):
    L = Qr.shape[0]
    Lc = Krc.shape[0]
    N, ncc = L // CHUNK, Lc // CHUNK

    rv_grp, rg_grp = _group_index(rv_blk, npairs), _group_index(rg_blk, npairs)
    heads = [(p, h) for p in range(npairs) for h in range(2)]
    kcols = lambda p: slice(p * LANES, (p + 1) * LANES)
    vcols = lambda p, h: slice((2 * p + h) * LANES, (2 * p + h + 1) * LANES)

    def body(lg_ref, q_ref, k_ref, v_ref, g_ref, sf_ref, kc_ref, vc_ref, y_ref, sb_ref, S, Mks):
        n = pl.program_id(0)
        masks = _head_masks()

        @pl.when(n == 0)
        def _():
            for p, h in heads:
                mu = _lam_of(lg_ref, 1, 2 * p + h)
                Mks[p, h] = _decay_mask(_lam_of(lg_ref, 0, 2 * p + h), mu)[0]
                dvb = _decay_vecs(mu, mu)
                s = jnp.zeros((LANES, LANES), F32)
                for cc in reversed(range(ncc)):
                    rows = slice(cc * CHUNK, (cc + 1) * CHUNK)
                    kw = jnp.where(masks[h], kc_ref[rows, kcols(p)].astype(F32) * dvb["kb"], 0.0).astype(BF16)
                    s = dvb["gb"] * s + _tn(kw, vc_ref[rows, vcols(p, h)])
                S[p, h] = s

        for p, h in heads:
            lam = _lam_of(lg_ref, 0, 2 * p + h)
            mu = _lam_of(lg_ref, 1, 2 * p + h)
            hm = masks[h]
            dvb = _decay_vecs(lam, mu)
            s = S[p, h]
            sbb = s.astype(BF16)
            sb_ref[p, 0, h] = sbb
            k = k_ref[:, kcols(p)]
            v = v_ref[:, vcols(p, h)]
            f = _ret_chunk_fwd(q_ref[:, kcols(p)], k, v, sf_ref[p, 0, h], sbb, hm, lam, mu, Mks[p, h])
            O = f["O"]
            r = lax.rsqrt(jnp.mean(O * O, axis=-1, keepdims=True) + NORM_EPS)
            g = g_ref[:, vcols(p, h)].astype(F32)
            y_ref[:, vcols(p, h)] = (O * r * (g * _sigmoid(g))).astype(BF16)
            kw = jnp.where(hm, k.astype(F32) * dvb["kb"], 0.0).astype(BF16)
            S[p, h] = dvb["gb"] * s + _tn(kw, v)

    rev = lambda n: N - 1 - n
    wq, wv = npairs * LANES, npairs * 2 * LANES
    st = pl.BlockSpec((npairs, 1, 2, LANES, LANES), lambda n: (0, rev(n), 0, 0, 0))
    return pl.pallas_call(
        body, name="ret_out_fwd", grid=(N,),
        in_specs=[pl.BlockSpec(memory_space=pltpu.SMEM),
                  pl.BlockSpec((CHUNK, wq), lambda n: (rev(n), 0)),
                  pl.BlockSpec((CHUNK, wq), lambda n: (rev(n), 0)),
                  pl.BlockSpec((CHUNK, wv), lambda n: (rev(n), rv_grp)),
                  pl.BlockSpec((CHUNK, wv), lambda n: (rev(n), rg_grp)),
                  st,
                  pl.BlockSpec((Lc, wq), lambda n: (0, 0)),
                  pl.BlockSpec((Lc, wv), lambda n: (0, rv_grp))],
        out_specs=[pl.BlockSpec((CHUNK, wv), lambda n: (rev(n), 0)), st],
        out_shape=[jax.ShapeDtypeStruct((L, d_mix), BF16),
                   jax.ShapeDtypeStruct((npairs, N, 2, LANES, LANES), BF16)],
        scratch_shapes=[pltpu.VMEM((npairs, 2, LANES, LANES), F32), pltpu.VMEM((npairs, 2, CHUNK, CHUNK), F32)],
        compiler_params=_params(1))(lg, Qr, Kr, P, P, SF, Krc, Pc)


ACC_ROWS = 8


def _ret_bwd1(Qr, Kr, P, Krc, Pc, SF, SB, dY, lg, rv_blk, rg_blk, npairs, d_proj):
    L = Qr.shape[0]
    Lc = Krc.shape[0]
    N, ncc = L // CHUNK, Lc // CHUNK
    rv_grp, rg_grp = _group_index(rv_blk, npairs), _group_index(rg_blk, npairs)
    heads = [(p, h) for p in range(npairs) for h in range(2)]
    kcols = lambda p: slice(p * LANES, (p + 1) * LANES)
    vcols = lambda p, h: slice((2 * p + h) * LANES, (2 * p + h + 1) * LANES)

    def body(lg_ref, q_ref, k_ref, v_ref, g_ref, sf_ref, sb_ref, dy_ref, kc_ref, vc_ref,
             dq_ref, dk_ref, dv_ref, dg_ref, do_ref, dkc_ref, dvc_ref, acc_ref, dS, T, Mks):
        n = pl.program_id(0)
        masks = _head_masks()

        @pl.when(n == 0)
        def _():
            dS[...] = jnp.zeros_like(dS)
            T[...] = jnp.zeros_like(T)
            acc_ref[...] = jnp.zeros_like(acc_ref)
            for p, h in heads:
                Mks[p, h] = _decay_mask(_lam_of(lg_ref, 0, 2 * p + h), _lam_of(lg_ref, 1, 2 * p + h))[0]

        def head_main(p, h):
            lam = _lam_of(lg_ref, 0, 2 * p + h)
            mu = _lam_of(lg_ref, 1, 2 * p + h)
            hm = masks[h]
            hs = vcols(p, h)
            v = v_ref[:, hs]
            k = k_ref[:, kcols(p)]
            sf = sf_ref[p, 0, h]
            sb = sb_ref[p, 0, h]
            f = _ret_chunk_fwd(q_ref[:, kcols(p)], k, v, sf, sb, hm, lam, mu, Mks[p, h])
            dv_, O = f["dv"], f["O"]
            r = lax.rsqrt(jnp.mean(O * O, axis=-1, keepdims=True) + NORM_EPS)
            on = O * r
            g = g_ref[:, hs].astype(F32)
            sg = _sigmoid(g)
            dy = dy_ref[:, hs].astype(F32)
            dg_ref[:, hs] = (dy * on * (sg * (1.0 + g * (1.0 - sg)))).astype(BF16)
            don = dy * (g * sg)
            dO = r * (don - on * jnp.mean(don * on, axis=-1, keepdims=True))
            dOb = dO.astype(BF16)
            do_ref[:, hs] = dOb
            dAm = _nt(dOb, v)
            T[p, h] += dAm * f["Am"]
            dAb = (dAm * f["Mk"]).astype(BF16)
            km = jnp.where(hm, k, jnp.zeros_like(k))
            dq = _nn(dAb, km)
            dk = _tn(dAb, f["qm"])
            dvh = _tn(f["Amb"], dOb)
            dQf = _nt(dOb, sf)
            dQb = _nt(dOb, sb)
            dq = dq + dQf * dv_["qf"] + dQb * dv_["qb"]
            acc_ref[p, h, 0:1, :] += _fsum(dQf * f["Qf"].astype(F32) * (dv_["i"] + 1.0))
            acc_ref[p, h, 1:2, :] += _fsum(dQb * f["Qb"].astype(F32) * (CHUNK - dv_["i"]))
            dSh = dS[p, h]
            dSb_ = dSh.astype(BF16)
            Kf = (km.astype(F32) * dv_["kf"]).astype(BF16)
            dKf = _nt(v, dSb_)
            dk = dk + jnp.where(hm, dKf * dv_["kf"], 0.0)
            acc_ref[p, h, 2:3, :] += _fsum(jnp.where(hm, dKf, 0.0) * Kf.astype(F32) * (CHUNK - 1.0 - dv_["i"]))
            dvh = dvh + _nn(Kf, dSb_)
            acc_ref[p, h, 3:4, :] += float(CHUNK) * dv_["gf"] * _fsum(dSh * sf.astype(F32))
            dSh = dv_["gf"] * dSh + _tn(f["Qf"], dOb)
            dS[p, h] = dSh
            dv_ref[:, hs] = dvh
            return dq, dk

        for p in range(npairs):
            dq0, dk0 = head_main(p, 0)
            dq1, dk1 = head_main(p, 1)
            dq_ref[:, kcols(p)] = dq0 + dq1
            dk_ref[:, kcols(p)] = dk0 + dk1

        @pl.when(n == N - 1)
        def _():
            for p, h in heads:
                lam = _lam_of(lg_ref, 0, 2 * p + h)
                dv_ = _decay_vecs(lam, lam)
                hm = masks[h]
                hs = vcols(p, h)
                states = [jnp.zeros((LANES, LANES), F32)]
                kws = []
                for cc in range(ncc):
                    rows = slice(cc * CHUNK, (cc + 1) * CHUNK)
                    kw = jnp.where(hm, kc_ref[rows, kcols(p)].astype(F32) * dv_["kf"], 0.0).astype(BF16)
                    kws.append(kw)
                    states.append(dv_["gf"] * states[-1] + _tn(kw, vc_ref[rows, hs]))
                d = dS[p, h]
                for cc in reversed(range(ncc)):
                    db = d.astype(BF16)
                    rows = slice(cc * CHUNK, (cc + 1) * CHUNK)
                    dKf_c = jnp.where(hm, _nt(vc_ref[rows, hs], db), 0.0)
                    part = dKf_c * dv_["kf"]
                    if h == 0:
                        dkc_ref[rows, kcols(p)] = part
                    else:
                        dkc_ref[rows, kcols(p)] += part
                    acc_ref[p, h, 2:3, :] += _fsum(dKf_c * kws[cc].astype(F32) * (CHUNK - 1.0 - dv_["i"]))
                    dvc_ref[rows, hs] = _nn(kws[cc], db)
                    acc_ref[p, h, 3:4, :] += float(CHUNK) * dv_["gf"] * _fsum(d * states[cc])
                    d = dv_["gf"] * d
                _, rel, low = _decay_mask(lam, lam)
                Th = T[p, h]
                acc_ref[p, h, 4:5, :] += _fsum(jnp.where(low, Th * rel, 0.0))
                acc_ref[p, h, 5:6, :] += _fsum(jnp.where(low, 0.0, -Th * rel))

    rev = lambda n: N - 1 - n
    wq, wv = npairs * LANES, npairs * 2 * LANES
    st = pl.BlockSpec((npairs, 1, 2, LANES, LANES), lambda n: (0, rev(n), 0, 0, 0))
    pair = pl.BlockSpec((CHUNK, wq), lambda n: (rev(n), 0))
    wide = lambda grp: pl.BlockSpec((CHUNK, wv), lambda n: (rev(n), grp))
    return pl.pallas_call(
        body, name="ret_bwd_desc", grid=(N,),
        in_specs=[pl.BlockSpec(memory_space=pltpu.SMEM), pair, pair, wide(rv_grp), wide(rg_grp), st, st, wide(0),
                  pl.BlockSpec((Lc, wq), lambda n: (0, 0)),
                  pl.BlockSpec((Lc, wv), lambda n: (0, rv_grp))],
        out_specs=[pair, pair, wide(0), wide(rg_grp), wide(0),
                   pl.BlockSpec((Lc, wq), lambda n: (0, 0)),
                   pl.BlockSpec((Lc, wv), lambda n: (0, 0)),
                   pl.BlockSpec((npairs, 2, ACC_ROWS, LANES), lambda n: (0, 0, 0, 0))],
        out_shape=[jax.ShapeDtypeStruct((L, npairs * LANES), F32),
                   jax.ShapeDtypeStruct((L, npairs * LANES), F32),
                   jax.ShapeDtypeStruct((L, npairs * 2 * LANES), F32),
                   jax.ShapeDtypeStruct((L, d_proj), BF16),
                   jax.ShapeDtypeStruct((L, npairs * 2 * LANES), BF16),
                   jax.ShapeDtypeStruct((Lc, npairs * LANES), F32),
                   jax.ShapeDtypeStruct((Lc, npairs * 2 * LANES), F32),
                   jax.ShapeDtypeStruct((npairs, 2, ACC_ROWS, LANES), F32)],
        scratch_shapes=[pltpu.VMEM((npairs, 2, LANES, LANES), F32), pltpu.VMEM((npairs, 2, CHUNK, CHUNK), F32),
                        pltpu.VMEM((npairs, 2, CHUNK, CHUNK), F32)],
        compiler_params=_params(1))(lg, Qr, Kr, P, P, SF, SB, dY, Krc, Pc)


def _ret_bwd2(Qr, Kr, P, Krc, Pc, SB, dO, dKr, dVp, dP, dKc, dVc, lg, rv_blk, npairs):
    L = Qr.shape[0]
    Lc = Krc.shape[0]
    N, ncc = L // CHUNK, Lc // CHUNK
    rv_grp = _group_index(rv_blk, npairs)
    heads = [(p, h) for p in range(npairs) for h in range(2)]
    kcols = lambda p: slice(p * LANES, (p + 1) * LANES)
    vcols = lambda p, h: slice((2 * p + h) * LANES, (2 * p + h + 1) * LANES)

    def body(lg_ref, q_ref, k_ref, v_ref, sb_ref, do_ref, dkin_ref, dvin_ref, kc_ref, vc_ref, dkcin_ref, dvcin_ref,
             dpin_ref, dk_ref, dv_ref, dkc_ref, dvc_ref, acc_ref, dS):
        del dpin_ref
        n = pl.program_id(0)
        masks = _head_masks()

        @pl.when(n == 0)
        def _():
            dS[...] = jnp.zeros_like(dS)
            acc_ref[...] = jnp.zeros_like(acc_ref)

        def head_main(p, h):
            mu = _lam_of(lg_ref, 1, 2 * p + h)
            hm = masks[h]
            hs = vcols(p, h)
            dv_ = _decay_vecs(mu, mu)
            v = v_ref[:, hs]
            k = k_ref[:, kcols(p)]
            q = q_ref[:, kcols(p)]
            dOb = do_ref[:, hs]
            km = jnp.where(hm, k, jnp.zeros_like(k)).astype(F32)
            Kb = (km * dv_["kb"]).astype(BF16)
            Qb = (jnp.where(hm, q, jnp.zeros_like(q)).astype(F32) * dv_["qb"]).astype(BF16)
            dSh = dS[p, h]
            dSb_ = dSh.astype(BF16)
            dKb = jnp.where(hm, _nt(v, dSb_), 0.0)
            acc_ref[p, h, 0:1, :] += _fsum(dKb * Kb.astype(F32) * dv_["i"])
            dv_ref[:, hs] = (dvin_ref[:, hs] + _nn(Kb, dSb_)).astype(BF16)
            acc_ref[p, h, 1:2, :] += float(CHUNK) * dv_["gb"] * _fsum(dSh * sb_ref[p, 0, h].astype(F32))
            dS[p, h] = dv_["gb"] * dSh + _tn(Qb, dOb)
            return dKb * dv_["kb"]

        for p in range(npairs):
            dk_ref[:, kcols(p)] = dkin_ref[:, kcols(p)] + head_main(p, 0) + head_main(p, 1)

        @pl.when(n == N - 1)
        def _():
            for p, h in heads:
                mu = _lam_of(lg_ref, 1, 2 * p + h)
                hm = masks[h]
                hs = vcols(p, h)
                dv_ = _decay_vecs(mu, mu)
                states = {}
                kws = {}
                s = jnp.zeros((LANES, LANES), F32)
                for cc in reversed(range(ncc)):
                    rows = slice(cc * CHUNK, (cc + 1) * CHUNK)
                    states[cc] = s
                    kw = jnp.where(hm, kc_ref[rows, kcols(p)].astype(F32) * dv_["kb"], 0.0).astype(BF16)
                    kws[cc] = kw
                    s = dv_["gb"] * s + _tn(kw, vc_ref[rows, hs])
                d = dS[p, h]
                for cc in range(ncc):
                    db = d.astype(BF16)
                    rows = slice(cc * CHUNK, (cc + 1) * CHUNK)
                    dKb_c = jnp.where(hm, _nt(vc_ref[rows, hs], db), 0.0)
                    part = dKb_c * dv_["kb"]
                    if h == 0:
                        dkc_ref[rows, kcols(p)] = dkcin_ref[rows, kcols(p)] + part
                    else:
                        dkc_ref[rows, kcols(p)] += part
                    acc_ref[p, h, 0:1, :] += _fsum(dKb_c * kws[cc].astype(F32) * dv_["i"])
                    dvc_ref[rows, hs] = dvcin_ref[rows, hs] + _nn(kws[cc], db)
                    acc_ref[p, h, 1:2, :] += float(CHUNK) * dv_["gb"] * _fsum(d * states[cc])
                    d = dv_["gb"] * d

    wq, wv = npairs * LANES, npairs * 2 * LANES
    st = pl.BlockSpec((npairs, 1, 2, LANES, LANES), lambda n: (0, n, 0, 0, 0))
    pair = pl.BlockSpec((CHUNK, wq), lambda n: (n, 0))
    wide = lambda grp: pl.BlockSpec((CHUNK, wv), lambda n: (n, grp))
    ckc = pl.BlockSpec((Lc, wq), lambda n: (0, 0))
    cvc = lambda grp: pl.BlockSpec((Lc, wv), lambda n: (0, grp))
    return pl.pallas_call(
        body, name="ret_bwd_asc", grid=(N,),
        in_specs=[pl.BlockSpec(memory_space=pltpu.SMEM), pair, pair, wide(rv_grp), st, wide(0), pair, wide(0),
                  ckc, cvc(rv_grp), ckc, cvc(0), pl.BlockSpec(memory_space=pl.ANY)],
        out_specs=[pair, wide(rv_grp), ckc, cvc(0),
                   pl.BlockSpec((npairs, 2, ACC_ROWS, LANES), lambda n: (0, 0, 0, 0))],
        out_shape=[jax.ShapeDtypeStruct(dKr.shape, F32),
                   jax.ShapeDtypeStruct(dP.shape, dP.dtype),
                   jax.ShapeDtypeStruct(dKc.shape, F32),
                   jax.ShapeDtypeStruct(dVc.shape, F32),
                   jax.ShapeDtypeStruct((npairs, 2, ACC_ROWS, LANES), F32)],
        input_output_aliases={12: 1},
        scratch_shapes=[pltpu.VMEM((npairs, 2, LANES, LANES), F32)],
        compiler_params=_params(1))(lg, Qr, Kr, P, SB, dO, dKr, dVp, Krc, Pc, dKc, dVc, dP)


GROUP = 4


def _att_band(Lc):
    row = np.arange(GROUP * CHUNK)[:, None] % CHUNK
    col = np.arange(3 * CHUNK + Lc)[None, :]
    ok = ((col >= row) & (col <= row + 2 * CHUNK)) | (col >= 3 * CHUNK)
    return jnp.asarray(np.where(ok, 0.0, NEG), F32)


def _att_edge(n, N, Lc):
    col = lax.broadcasted_iota(jnp.int32, (1, 3 * CHUNK + Lc), 1)
    off = jnp.logical_or(jnp.logical_and(col < CHUNK, n == 0),
                         jnp.logical_and(jnp.logical_and(col >= 2 * CHUNK, col < 3 * CHUNK), n == N - 1))
    return jnp.where(off, NEG, 0.0)


def _stack_heads(ref, gi):
    masks = _head_masks()
    tiles = []
    for pr in range(2):
        t = ref[:, (2 * gi + pr) * LANES:(2 * gi + pr + 1) * LANES]
        for a in range(2):
            tiles.append(jnp.where(masks[a], t, jnp.zeros_like(t)))
    return jnp.concatenate(tiles, axis=0)


def _unstack_heads(x4):
    m0 = _head_masks()[0]
    return [jnp.where(m0, x4[(2 * pr) * CHUNK:(2 * pr + 1) * CHUNK], x4[(2 * pr + 1) * CHUNK:(2 * pr + 2) * CHUNK])
            for pr in range(2)]


def _sink_column(sink_ref, g):
    row = lax.broadcasted_iota(jnp.int32, (GROUP * CHUNK, 1), 0) // CHUNK
    col = jnp.zeros((GROUP * CHUNK, 1), F32)
    for h in range(GROUP):
        col = jnp.where(row == h, sink_ref[0, g * GROUP + h], col)
    return col


def _att_probs(q4, Kall, bias, snk):
    s = _nt(q4, Kall) + bias
    mx = jnp.maximum(jnp.max(s, axis=1, keepdims=True), snk)
    p = jnp.exp(s - mx)
    p_snk = jnp.exp(snk - mx)
    inv = 1.0 / (jnp.sum(p, axis=1, keepdims=True) + p_snk)
    return p, p_snk, inv


def _att_groups_per_step(nkv, blk0):
    return 2 if nkv % 2 == 0 and blk0 % 2 == 0 else 1


def _att_specs(Lc, N, gps):
    q = pl.BlockSpec((CHUNK, gps * 2 * LANES), lambda g, n: (n, g))
    kv = lambda s: pl.BlockSpec((gps, CHUNK, LANES), lambda g, n: (g, jnp.clip(n + s, 0, N - 1), 0))
    ctx = pl.BlockSpec((gps, Lc, LANES), lambda g, n: (g, 0, 0))
    return q, kv, ctx


def _att_fwd(Qa, Kd, Vd, Kdc, Vdc, sink, Y, blk0):
    L = Qa.shape[0]
    Lc = Kdc.shape[1]
    N = L // CHUNK
    nkv = Kd.shape[0]
    gps = _att_groups_per_step(nkv, blk0)

    def body(sink_ref, band_ref, q_ref, kp, kc_, kn, vp, vc_, vn, kctx, vctx, y_in, o_ref):
        del y_in
        g, n = pl.program_id(0), pl.program_id(1)
        bias = band_ref[...] + _att_edge(n, N, Lc)
        for gi in range(gps):
            Kall = jnp.concatenate([kp[gi], kc_[gi], kn[gi], kctx[gi]], axis=0)
            Vall = jnp.concatenate([vp[gi], vc_[gi], vn[gi], vctx[gi]], axis=0)
            p, _, inv = _att_probs(_stack_heads(q_ref, gi), Kall, bias, _sink_column(sink_ref, g * gps + gi))
            o4 = _nn(p.astype(BF16), Vall) * inv
            for pr, o in enumerate(_unstack_heads(o4)):
                o_ref[:, (2 * gi + pr) * LANES:(2 * gi + pr + 1) * LANES] = o.astype(BF16)

    q, kv, ctx = _att_specs(Lc, N, gps)
    band = pl.BlockSpec((GROUP * CHUNK, 3 * CHUNK + Lc), lambda g, n: (0, 0))
    return pl.pallas_call(
        body, name="att_fwd", grid=(nkv // gps, N),
        in_specs=[pl.BlockSpec(memory_space=pltpu.SMEM), band, q, kv(-1), kv(0), kv(1), kv(-1), kv(0), kv(1), ctx, ctx,
                  pl.BlockSpec(memory_space=pl.ANY)],
        out_specs=pl.BlockSpec((CHUNK, gps * 2 * LANES), lambda g, n: (n, blk0 // gps + g)),
        out_shape=jax.ShapeDtypeStruct(Y.shape, Y.dtype),
        input_output_aliases={11: 0},
        compiler_params=_params(2))(sink, _att_band(Lc), Qa, Kd, Kd, Kd, Vd, Vd, Vd, Kdc, Vdc, Y)


def _att_bwd(Qa, Kd, Vd, Kdc, Vdc, sink, dY, blk0):
    L = Qa.shape[0]
    Lc = Kdc.shape[1]
    N = L // CHUNK
    nkv = Kd.shape[0]
    gps = _att_groups_per_step(nkv, blk0)

    def body(sink_ref, band_ref, q_ref, kp, kc_, kn, vp, vc_, vn, kctx, vctx, dy_ref,
             dq_ref, dkp, dkc_, dkn, dvp, dvc_, dvn, dkctx, dvctx, dsink_ref):
        g, n = pl.program_id(0), pl.program_id(1)

        @pl.when(n == 0)
        def _():
            dkctx[...] = jnp.zeros_like(dkctx)
            dvctx[...] = jnp.zeros_like(dvctx)
            dsink_ref[...] = jnp.zeros_like(dsink_ref)

        bias = band_ref[...] + _att_edge(n, N, Lc)
        for gi in range(gps):
            Kall = jnp.concatenate([kp[gi], kc_[gi], kn[gi], kctx[gi]], axis=0)
            Vall = jnp.concatenate([vp[gi], vc_[gi], vn[gi], vctx[gi]], axis=0)
            q4 = _stack_heads(q_ref, gi)
            do4 = _stack_heads(dy_ref, gi)
            p, p_snk, inv = _att_probs(q4, Kall, bias, _sink_column(sink_ref, g * gps + gi))
            P = p * inv
            dp = _nt(do4, Vall)
            delta = jnp.sum(P * dp, axis=1, keepdims=True)
            ds = (P * (dp - delta)).astype(BF16)
            dsnk = -(p_snk * inv) * delta
            for h in range(GROUP):
                dsink_ref[gi, h:h + 1, :] += _fsum(dsnk[h * CHUNK:(h + 1) * CHUNK])
            for pr, dq in enumerate(_unstack_heads(_nn(ds, Kall))):
                dq_ref[:, (2 * gi + pr) * LANES:(2 * gi + pr + 1) * LANES] = dq
            dK = _tn(ds, q4)
            dV = _tn(P.astype(BF16), do4)
            for j, (rk, rv) in enumerate([(dkp, dvp), (dkc_, dvc_), (dkn, dvn)]):
                rk[gi] = dK[j * CHUNK:(j + 1) * CHUNK].astype(BF16)
                rv[gi] = dV[j * CHUNK:(j + 1) * CHUNK].astype(BF16)
            dkctx[gi] += dK[3 * CHUNK:]
            dvctx[gi] += dV[3 * CHUNK:]

    q, kv, ctx = _att_specs(Lc, N, gps)
    band = pl.BlockSpec((GROUP * CHUNK, 3 * CHUNK + Lc), lambda g, n: (0, 0))
    blk = pl.BlockSpec((gps, CHUNK, LANES), lambda g, n: (g, n, 0))
    part = jax.ShapeDtypeStruct((nkv, L, LANES), BF16)
    cshape = jax.ShapeDtypeStruct((nkv, Lc, LANES), F32)
    return pl.pallas_call(
        body, name="att_bwd", grid=(nkv // gps, N),
        in_specs=[pl.BlockSpec(memory_space=pltpu.SMEM), band, q, kv(-1), kv(0), kv(1), kv(-1), kv(0), kv(1), ctx, ctx,
                  pl.BlockSpec((CHUNK, gps * 2 * LANES), lambda g, n: (n, blk0 // gps + g))],
        out_specs=[q, blk, blk, blk, blk, blk, blk, ctx, ctx,
                   pl.BlockSpec((gps, 8, LANES), lambda g, n: (g, 0, 0))],
        out_shape=[jax.ShapeDtypeStruct(Qa.shape, F32), part, part, part, part, part, part, cshape, cshape,
                   jax.ShapeDtypeStruct((nkv, 8, LANES), F32)],
        compiler_params=_params(2))(sink, _att_band(Lc), Qa, Kd, Kd, Kd, Vd, Vd, Vd, Kdc, Vdc, dY)


def _scale_rows(dx, gt, saved, name):
    M, D = dx.shape
    tm = _tile(M, 512, 8)

    def body(dx_ref, gt_ref, sv_ref, dz_ref, dgt_ref):
        @pl.when(pl.program_id(0) == 0)
        def _():
            dgt_ref[...] = jnp.zeros_like(dgt_ref)
        d = dx_ref[...]
        dz_ref[...] = (d * gt_ref[...]).astype(BF16)
        dgt_ref[...] += jnp.sum(d * sv_ref[...].astype(F32), axis=0, keepdims=True)

    row = pl.BlockSpec((tm, D), lambda i: (i, 0))
    vec = pl.BlockSpec((1, D), lambda i: (0, 0))
    return pl.pallas_call(
        body, name=name, grid=(M // tm,), in_specs=[row, vec, row], out_specs=[row, vec],
        out_shape=[jax.ShapeDtypeStruct((M, D), BF16), jax.ShapeDtypeStruct((1, D), F32)],
        compiler_params=_params(1))(dx, gt, saved)


def _bwd_proj(dz, w, G=None, U=None, name="bwd_proj"):
    M, D = dz.shape
    N = w.shape[0]
    swiglu = G is not None
    tm, tn = _tile(M, ROWS_PER_LATCH, 8), _tile(N, 512)

    def body(*refs):
        if swiglu:
            dz_ref, w_ref, G_ref, U_ref, dG_ref, dU_ref = refs
        else:
            dz_ref, w_ref, dA_ref = refs
        dA = _nt(dz_ref[...], w_ref[...])
        if swiglu:
            Gv = G_ref[...].astype(F32)
            Uv = U_ref[...].astype(F32)
            sg = _sigmoid(Gv)
            dU_ref[...] = (dA * Gv * sg).astype(BF16)
            dG_ref[...] = (dA * Uv * (sg * (1.0 + Gv * (1.0 - sg)))).astype(BF16)
        else:
            dA_ref[...] = dA.astype(BF16)

    row = pl.BlockSpec((tm, D), lambda i, j: (i, 0))
    tile = pl.BlockSpec((tm, tn), lambda i, j: (i, j))
    big = jax.ShapeDtypeStruct((M, N), BF16)
    in_specs = [row, pl.BlockSpec((tn, D), lambda i, j: (j, 0))]
    args = [dz, w]
    if swiglu:
        in_specs += [tile, tile]
        args += [G, U]
        out_specs, out_shape = [tile, tile], [big, big]
    else:
        out_specs, out_shape = tile, big
    return pl.pallas_call(
        body, name=name, grid=(M // tm, N // tn), in_specs=in_specs, out_specs=out_specs, out_shape=out_shape,
        compiler_params=_params(2))(*args)


def _tn_matmul(pairs, name):
    Ka, Nb = pairs[0][0].shape[1], pairs[0][1].shape[1]
    tk, tn = _tile(Ka, 2048), _tile(Nb, 2048)
    tls, nks = [], []
    for a, _ in pairs:
        tl = _tile(a.shape[0], 1024, 8)
        tls.append(tl)
        nks.append(a.shape[0] // tl)
    starts = [int(s) for s in np.cumsum([0] + nks[:-1])]
    nk = int(sum(nks))

    def body(*refs):
        out_ref, acc = refs[-2], refs[-1]
        k = pl.program_id(2)

        @pl.when(k == 0)
        def _():
            acc[...] = jnp.zeros_like(acc)

        for idx in range(len(pairs)):
            a_ref, b_ref = refs[2 * idx], refs[2 * idx + 1]

            @pl.when(jnp.logical_and(k >= starts[idx], k < starts[idx] + nks[idx]))
            def _():
                acc[...] += _tn(a_ref[...], b_ref[...])

        @pl.when(k == nk - 1)
        def _():
            out_ref[...] = acc[...].astype(BF16)

    in_specs, args = [], []
    for idx, (a, b) in enumerate(pairs):
        s0, n_ = starts[idx], nks[idx]
        in_specs.append(pl.BlockSpec((tls[idx], tk), lambda i, j, k, s0=s0, n_=n_: (jnp.clip(k - s0, 0, n_ - 1), i)))
        in_specs.append(pl.BlockSpec((tls[idx], tn), lambda i, j, k, s0=s0, n_=n_: (jnp.clip(k - s0, 0, n_ - 1), j)))
        args += [a, b]
    return pl.pallas_call(
        body, name=name, grid=(Ka // tk, Nb // tn, nk), in_specs=in_specs,
        out_specs=pl.BlockSpec((tk, tn), lambda i, j, k: (i, j)),
        out_shape=jax.ShapeDtypeStruct((Ka, Nb), BF16),
        scratch_shapes=[pltpu.VMEM((tk, tn), F32)], compiler_params=_params(3))(*args)


def _bwd_norm_mod(pairs, x, dres, g, sh, sc, name):
    M, D = x.shape
    K = pairs[0][0].shape[1]
    tm, tk = _tile(M, 512, 8), _tile(K, 1152 if len(pairs) == 1 else 512)
    nk = K // tk
    npair = len(pairs)
    has_res = dres is not None

    def body(*refs):
        pr = refs[:2 * npair]
        rest = refs[2 * npair:]
        if has_res:
            x_ref, dres_ref, g_ref, sh_ref, sc_ref, dx_ref, st_ref, acc = rest
        else:
            x_ref, g_ref, sh_ref, sc_ref, dx_ref, st_ref, acc = rest
        del sh_ref
        i, k = pl.program_id(0), pl.program_id(1)

        @pl.when(jnp.logical_and(i == 0, k == 0))
        def _():
            st_ref[...] = jnp.zeros_like(st_ref)

        @pl.when(k == 0)
        def _():
            acc[...] = jnp.zeros_like(acc)

        t = _nt(pr[1][...], pr[0][...])
        for idx in range(1, npair):
            t = t + _nt(pr[2 * idx + 1][...], pr[2 * idx][...])
        acc[...] += t

        @pl.when(k == nk - 1)
        def _():
            xv = x_ref[...]
            gv = g_ref[...]
            dh = acc[...].T
            r = lax.rsqrt(jnp.mean(xv * xv, axis=-1, keepdims=True) + NORM_EPS)
            xh = xv * r
            st_ref[0:1, :] += jnp.sum(dh, axis=0, keepdims=True)
            st_ref[1:2, :] += jnp.sum(dh * (xh * gv), axis=0, keepdims=True)
            dn = dh * (1.0 + sc_ref[...])
            st_ref[2:3, :] += jnp.sum(dn * xh, axis=0, keepdims=True)
            dxh = dn * gv
            d = r * (dxh - xh * jnp.mean(dxh * xh, axis=-1, keepdims=True))
            if has_res:
                d = d + dres_ref[...]
            dx_ref[...] = d

    row = pl.BlockSpec((tm, D), lambda i, k: (i, 0))
    vec = pl.BlockSpec((1, D), lambda i, k: (0, 0))
    in_specs, args = [], []
    for dA, w in pairs:
        in_specs += [pl.BlockSpec((tm, tk), lambda i, k: (i, k)), pl.BlockSpec((D, tk), lambda i, k: (0, k))]
        args += [dA, w]
    in_specs += [row] + ([row] if has_res else []) + [vec, vec, vec]
    args += [x] + ([dres] if has_res else []) + [g, sh, sc]
    return pl.pallas_call(
        body, name=name, grid=(M // tm, nk), in_specs=in_specs,
        out_specs=[row, pl.BlockSpec((8, D), lambda i, k: (0, 0))],
        out_shape=[jax.ShapeDtypeStruct((M, D), F32), jax.ShapeDtypeStruct((8, D), F32)],
        scratch_shapes=[pltpu.VMEM((D, tm), F32)], compiler_params=_params(2))(*args)


def _local_step(x, ctx, tgt, mod, modc, norm_mix, norm_ffn, norm_final, lg, sink, w_in, rest_weights, on_grads, flush):
    L, D = x.shape
    Lc = ctx.shape[0]
    d_proj = w_in.shape[1]
    npairs = RET_HEADS // 2
    nkv = ATT_KV_HEADS
    nkvp = nkv // 2
    o_rq = 0
    o_rk = o_rq + RET_HEADS * RET_DK // LANES
    o_rv = o_rk + RET_HEADS * RET_DK // LANES
    o_rg = o_rv + RET_HEADS * RET_DV // LANES
    o_aq = o_rg + RET_HEADS * RET_DV // LANES
    o_ak = o_aq + ATT_HEADS * ATT_DH // LANES
    o_av = o_ak + nkv * ATT_DH // LANES
    assert (o_av + nkv * ATT_DH // LANES) * LANES == d_proj
    assert o_rv % 2 == 0 and o_rg % 2 == 0 and (RET_HEADS * RET_DV) % (2 * LANES) == 0
    rv_blk, rg_blk = o_rv // 2, o_rg // 2
    d_ret = RET_HEADS * RET_DV
    d_mix = d_ret + ATT_HEADS * ATT_DH
    att_blk = d_ret // (2 * LANES)
    k_scale = RET_DK ** -0.5
    a_scale = ATT_DH ** -0.5

    T = _rope_tables(L)
    Tc = dict(C=jnp.ones((Lc, LANES), F32), S=jnp.zeros((Lc, LANES), F32))
    row = lambda m, i: m[i:i + 1]
    sh_m, sc_m, gt_m, sh_f, sc_f, gt_f = [row(mod, i) for i in range(6)]
    sh_mc, sc_mc = row(modc, 0), row(modc, 1)

    P, hx = _norm_mod_matmul(x, norm_mix, sh_m, sc_m, w_in, "in_proj")
    Pc, hc = _norm_mod_matmul(ctx, norm_mix, sh_mc, sc_mc, w_in, "in_proj_ctx")
    nq = RET_HEADS * RET_DK // LANES
    Qr = _rope_cols(P, o_rq, nq, T["Cr"], T["Sr"], T["Rr"], 1.0, True, "rope_rq")
    Kr = _rope_cols(P, o_rk, nq, T["Cr"], T["Sr"], T["Rr"], k_scale, True, "rope_rk")
    Krc = _rope_cols(Pc, o_rk, nq, Tc["C"], Tc["S"], T["Rr"], k_scale, False, "scale_rk_ctx")
    Qa = _rope_cols(P, o_aq, ATT_HEADS * ATT_DH // LANES, T["Ca"], T["Sa"], T["Ra"], a_scale, True, "rope_aq")
    Kd = _dup_heads(P, o_ak, nkvp, T["Ca"], T["Sa"], T["Ra"], T["D0"], T["D1"], True, "dup_ak")
    Vd = _dup_heads(P, o_av, nkvp, T["Ca"], T["Sa"], T["Ra"], T["D0"], T["D1"], False, "dup_av")
    Kdc = _dup_heads(Pc, o_ak, nkvp, Tc["C"], Tc["S"], T["Ra"], T["D0"], T["D1"], False, "dup_ak_ctx")
    Vdc = _dup_heads(Pc, o_av, nkvp, Tc["C"], Tc["S"], T["Ra"], T["D0"], T["D1"], False, "dup_av_ctx")

    SF = _ret_states_fwd(Kr, P, Krc, Pc, lg, rv_blk, npairs)
    Y, SB = _ret_out_fwd(Qr, Kr, P, Krc, Pc, SF, lg, rv_blk, rg_blk, npairs, d_mix)
    Y = _att_fwd(Qa, Kd, Vd, Kdc, Vdc, sink, Y, att_blk)

    w_out, w_gate, w_up, w_down = rest_weights(Y)
    x1, O1 = _proj_residual(Y, w_out, x, gt_m, "out_proj")
    G, U, A, h2 = _ffn_in(x1, norm_ffn, sh_f, sc_f, w_gate, w_up)
    x2, Fo = _proj_residual(A, w_down, x1, gt_f, "ffn_out")
    dx2, loss, d_norm_final, dz2, dgt_f = _final(x2, norm_final, tgt, gt_f, Fo)

    dG, dU = _bwd_proj(dz2, w_down, G, U, name="ffn_out_bwd")
    g_w_down = _tn_matmul([(A, dz2)], "grad_w_down")
    tok = on_grads(["w_down"], [g_w_down])
    dx1, st_f = _bwd_norm_mod([(dG, w_gate), (dU, w_up)], x1, dx2, norm_ffn + tok, sh_f, sc_f, "ffn_in_bwd")
    tok = flush(dx1)
    g_w_gate = _tn_matmul([(h2, dG)], "grad_w_gate")
    g_w_up = _tn_matmul([(h2, dU)], "grad_w_up")
    tok = tok + on_grads(["w_gate", "w_up"], [g_w_gate, g_w_up])
    dz1, dgt_m = _scale_rows(dx1, gt_m + tok, O1, "mix_gate_bwd")
    dY = _bwd_proj(dz1, w_out, name="out_proj_bwd")
    tok = flush(dY)
    g_w_out = _tn_matmul([(Y, dz1)], "grad_w_out")
    tok = tok + on_grads(["w_out"], [g_w_out])

    dQa, dKp, dKs, dKn, dVp, dVs, dVn, dKdc, dVdc, dsink = _att_bwd(Qa, Kd, Vd, Kdc, Vdc, sink + tok, dY, att_blk)
    tok = flush(dQa)
    dQr, dKr, dVr, dP, dO, dKc, dVc, acc1 = _ret_bwd1(Qr, Kr, P, Krc, Pc, SF, SB, dY, lg + tok, rv_blk, rg_blk, npairs,
                                                      d_proj)
    dKr, dP, dKc, dVc, acc2 = _ret_bwd2(Qr, Kr, P, Krc, Pc, SB, dO, dKr, dVr, dP, dKc, dVc, lg, rv_blk, npairs)

    dP = _unrope_cols(dQr, dP, o_rq, nq, T["Cr"], T["Sr"], T["RrT"], 1.0, True, "unrope_rq")
    dP = _unrope_cols(dKr, dP, o_rk, nq, T["Cr"], T["Sr"], T["RrT"], k_scale, True, "unrope_rk")
    dP = _unrope_cols(dQa, dP, o_aq, ATT_HEADS * ATT_DH // LANES, T["Ca"], T["Sa"], T["RaT"], a_scale, True, "unrope_aq")
    dP = _fold_heads([(dKs, 0), (dKp, 1), (dKn, -1)], dP, o_ak, nkvp, T["Ca"], T["Sa"], T["RaT"], T["D0T"], T["D1T"],
                     True, "fold_ak")
    dP = _fold_heads([(dVs, 0), (dVp, 1), (dVn, -1)], dP, o_av, nkvp, T["Ca"], T["Sa"], T["RaT"], T["D0T"], T["D1T"],
                     False, "fold_av")
    dPc = jnp.zeros((Lc, d_proj), BF16)
    dPc = _unrope_cols(dKc, dPc, o_rk, nq, Tc["C"], Tc["S"], T["RrT"], k_scale, False, "ctx_rk_bwd")
    dPc = _unrope_cols(dVc, dPc, o_rv, RET_HEADS * RET_DV // LANES, Tc["C"], Tc["S"], T["RrT"], 1.0, False, "ctx_rv_bwd")
    dPc = _fold_heads([(dKdc.astype(BF16), 0)], dPc, o_ak, nkvp, Tc["C"], Tc["S"], T["RaT"], T["D0T"], T["D1T"],
                      False, "fold_ak_ctx")
    dPc = _fold_heads([(dVdc.astype(BF16), 0)], dPc, o_av, nkvp, Tc["C"], Tc["S"], T["RaT"], T["D0T"], T["D1T"],
                      False, "fold_av_ctx")

    dx, st_m = _bwd_norm_mod([(dP, w_in)], x, dx1, norm_mix, sh_m, sc_m, "in_proj_bwd")
    _, st_mc = _bwd_norm_mod([(dPc, w_in)], ctx, None, norm_mix, sh_mc, sc_mc, "in_proj_ctx_bwd")
    g_w_in = _tn_matmul([(hx, dP), (hc, dPc)], "grad_w_in")
    on_grads(["w_in"], [g_w_in])

    a1 = acc1[:, :, :, 0].reshape(RET_HEADS, ACC_ROWS)
    a2 = acc2[:, :, :, 0].reshape(RET_HEADS, ACC_ROWS)
    dlam = (a1[:, 0] + a1[:, 2] + a1[:, 3] + a1[:, 4]) * lg[0]
    dmu = (a1[:, 1] + a1[:, 5] + a2[:, 0] + a2[:, 1]) * lg[1]
    d_sink = dsink[:, :4, 0].reshape(1, ATT_HEADS)

    nh = RET_HEADS
    assert 2 * nh + ATT_HEADS <= LOSS_LANE
    small = _pack_rows(
        [(st_m, 0, 2, 0, 0), (dgt_m, 0, 1, 2, 0), (st_f, 0, 2, 3, 0), (dgt_f, 0, 1, 5, 0), (st_mc, 0, 2, 6, 0),
         (st_m[2:3] + st_mc[2:3], 0, 1, 12, 0), (st_f, 2, 1, 13, 0), (d_norm_final, 0, 1, 14, 0),
         (dlam.reshape(1, nh), 0, 1, 15, 0), (dmu.reshape(1, nh), 0, 1, 15, nh), (d_sink, 0, 1, 15, 2 * nh),
         (loss[:, 0:1], 0, 1, 15, LOSS_LANE)], 16, D, "pack_small")
    return dict(grad_x=dx, small=small)


def _my_pos():
    return lax.axis_index("x"), lax.axis_index("y"), lax.axis_index("c")


def _other_chips(x, y):
    return [(1 - x, y), (x, 1 - y), (1 - x, 1 - y)]


def _remote(src, dst, ssem, rsem, dev):
    return pltpu.make_async_remote_copy(src_ref=src, dst_ref=dst, send_sem=ssem, recv_sem=rsem,
                                        device_id=dev, device_id_type=MESH)


def _allgather8(v, name):
    R, Cc = v.shape

    def body(v_ref, out_ref, send_sems, recv_sems):
        x, y, c = _my_pos()
        me = 4 * x + 2 * y + c
        out_ref[pl.ds(me, 1)] = v_ref[...][None]
        peers = []
        for j in range(1, N_DEV):
            peers.append((1 - x if (j >> 2) & 1 else x, 1 - y if (j >> 1) & 1 else y, 1 - c if j & 1 else c))
        copies = []
        for j, peer in enumerate(peers):
            cp = _remote(v_ref, out_ref.at[me], send_sems.at[j], recv_sems.at[j], peer)
            cp.start()
            copies.append(cp)
        for j, peer in enumerate(peers):
            pid = 4 * peer[0] + 2 * peer[1] + peer[2]
            _remote(v_ref, out_ref.at[pid], send_sems.at[j], recv_sems.at[j], peer).wait_recv()
        for cp in copies:
            cp.wait_send()

    return pl.pallas_call(
        body, name=name, out_shape=jax.ShapeDtypeStruct((N_DEV, R, Cc), v.dtype),
        in_specs=[pl.BlockSpec(memory_space=pltpu.VMEM)], out_specs=pl.BlockSpec(memory_space=pltpu.VMEM),
        scratch_shapes=[pltpu.SemaphoreType.DMA((N_DEV - 1,)), pltpu.SemaphoreType.DMA((N_DEV - 1,))])(v)


def _region(ref, k, half, shard_shape, axis):
    r, cs = shard_shape
    hr = r // 2
    if axis == 1:
        return ref.at[pl.ds(pl.multiple_of(half * hr, 16), hr), pl.ds(pl.multiple_of(k * cs, LANES), cs)]
    return ref.at[pl.ds(pl.multiple_of(k * r + half * hr, 16), hr), :]


def _full_shape(shard_shape, axis):
    r, cs = shard_shape
    return (r, N_CHIPS * cs) if axis == 1 else (N_CHIPS * r, cs)


def _half_pieces(ref, half, shard_shape, axis):
    r, cs = shard_shape
    hr = r // 2
    if axis == 1:
        return [ref.at[pl.ds(pl.multiple_of(half * hr, 16), hr), :]]
    return [ref.at[pl.ds(pl.multiple_of(k * r + half * hr, 16), hr), :] for k in range(N_CHIPS)]


def _half_block_spec(shard_shape, axis, tr):
    r, cs = shard_shape
    hr = r // 2
    if axis == 1:
        return pl.BlockSpec((tr, cs), lambda k, i, c_ref: (c_ref[0] * (hr // tr) + i, k))
    return pl.BlockSpec((tr, cs), lambda k, i, c_ref: (k * (r // tr) + c_ref[0] * (hr // tr) + i, 0))


def _add_halves(g, recv, cvec, shard_shape, axis, name):
    r, cs = shard_shape
    hr = r // 2
    tr = _tile(hr, 256, 16)

    def body(c_ref, a_ref, b_ref, o_ref):
        del c_ref
        o_ref[0] = (a_ref[...].astype(F32) + b_ref[...].astype(F32)).astype(BF16)

    spec = _half_block_spec(shard_shape, axis, tr)
    return pl.pallas_call(
        body, name=name,
        grid_spec=pltpu.PrefetchScalarGridSpec(
            num_scalar_prefetch=1, grid=(N_CHIPS, hr // tr), in_specs=[spec, spec],
            out_specs=pl.BlockSpec((1, tr, cs), lambda k, i, c_ref: (k, i, 0))),
        out_shape=jax.ShapeDtypeStruct((N_CHIPS, hr, cs), BF16),
        compiler_params=_params(2, False))(cvec, g, recv)


def _sum_chips(sums, landed, kc, name):
    _, hr, cs = sums.shape
    tr = _tile(hr, 256, 16)

    def body(kc_ref, own_ref, a_ref, b_ref, c_ref, o_ref):
        del kc_ref
        o_ref[...] = (own_ref[0].astype(F32) + a_ref[0].astype(F32)) + (b_ref[0].astype(F32) + c_ref[0].astype(F32))

    slot = lambda j: pl.BlockSpec((1, tr, cs), lambda i, kc_ref: ((kc_ref[0] + j) % N_CHIPS, i, 0))
    return pl.pallas_call(
        body, name=name,
        grid_spec=pltpu.PrefetchScalarGridSpec(
            num_scalar_prefetch=1, grid=(hr // tr,), in_specs=[slot(0), slot(1), slot(2), slot(3)],
            out_specs=pl.BlockSpec((tr, cs), lambda i, kc_ref: (kc_ref[1] * (hr // tr) + i, 0))),
        out_shape=jax.ShapeDtypeStruct((2 * hr, cs), F32),
        compiler_params=_params(1, False))(kc, sums, landed, landed, landed)


def _exchange_halves(shards, name):
    nw = len(shards)

    def body(*refs):
        out_refs = refs[nw:2 * nw]
        send, recv = refs[2 * nw:]
        x, y, c = _my_pos()
        sib = (x, y, 1 - c)
        copies = []
        for w in range(nw):
            hr = shards[w].shape[0] // 2
            mine = out_refs[w].at[pl.ds(pl.multiple_of(c * hr, 8), hr), :]
            cp = _remote(mine, mine, send.at[w], recv.at[w], sib)
            cp.start()
            copies.append(cp)
        for w in range(nw):
            hr = shards[w].shape[0] // 2
            other = out_refs[w].at[pl.ds(pl.multiple_of((1 - c) * hr, 8), hr), :]
            _remote(other, other, send.at[w], recv.at[w], sib).wait_recv()
        for cp in copies:
            cp.wait_send()

    anyspec = pl.BlockSpec(memory_space=pl.ANY)
    return pl.pallas_call(
        body, name=name,
        out_shape=[jax.ShapeDtypeStruct(s.shape, F32) for s in shards],
        in_specs=[anyspec] * nw, out_specs=[anyspec] * nw,
        input_output_aliases={w: w for w in range(nw)},
        scratch_shapes=[pltpu.SemaphoreType.DMA((nw,)), pltpu.SemaphoreType.DMA((nw,))])(*shards)


def _cast_into_full(w, kc, axis, name):
    r, cs = w.shape
    tr = _tile(r, 256, 16)

    def body(kc_ref, w_ref, o_ref):
        del kc_ref
        o_ref[...] = w_ref[...].astype(BF16)

    if axis == 1:
        ospec = pl.BlockSpec((tr, cs), lambda i, kc_ref: (i, kc_ref[0]))
    else:
        ospec = pl.BlockSpec((tr, cs), lambda i, kc_ref: (kc_ref[0] * (r // tr) + i, 0))
    return pl.pallas_call(
        body, name=name,
        grid_spec=pltpu.PrefetchScalarGridSpec(
            num_scalar_prefetch=1, grid=(r // tr,), in_specs=[pl.BlockSpec((tr, cs), lambda i, kc_ref: (i, 0))],
            out_specs=ospec),
        out_shape=jax.ShapeDtypeStruct(_full_shape((r, cs), axis), BF16),
        compiler_params=_params(1, False))(kc, w)


def _adam_math(w, g, m, v):
    m2 = ADAM_B1 * m + (1.0 - ADAM_B1) * g
    v2 = ADAM_B2 * v + (1.0 - ADAM_B2) * (g * g)
    m_hat = m2 / (1.0 - ADAM_B1 ** ADAM_STEP)
    v_hat = v2 / (1.0 - ADAM_B2 ** ADAM_STEP)
    delta = -ADAM_LR * (m_hat / (jnp.sqrt(v_hat) + ADAM_EPS) + ADAM_WD * w)
    return delta, m2, v2


def _adam(w, g, m, v, name):
    r, cs = w.shape
    tr = _tile(r, 256, 8)

    def body(w_ref, g_ref, m_ref, v_ref, d_ref, m2_ref, v2_ref):
        d, m2, v2 = _adam_math(w_ref[...], g_ref[...], m_ref[...], v_ref[...])
        d_ref[...] = d
        m2_ref[...] = m2
        v2_ref[...] = v2

    spec = pl.BlockSpec((tr, cs), lambda i: (i, 0))
    shp = jax.ShapeDtypeStruct((r, cs), F32)
    return pl.pallas_call(body, name=name, grid=(r // tr,), in_specs=[spec] * 4, out_specs=[spec] * 3,
                          out_shape=[shp, shp, shp], compiler_params=_params(1, False))(w, g, m, v)


def _mod_rows(a16, w, b, name):
    D, n = w.shape
    tn = _tile(n, 512)

    def body(a_ref, w_ref, b_ref, o_ref):
        a = a_ref[...]
        o_ref[...] = _nn((a * _sigmoid(a)).astype(BF16), w_ref[...].astype(BF16)) + b_ref[...]

    return pl.pallas_call(
        body, name=name, grid=(n // tn,),
        in_specs=[pl.BlockSpec((16, D), lambda j: (0, 0)), pl.BlockSpec((D, tn), lambda j: (0, j)),
                  pl.BlockSpec((1, tn), lambda j: (0, j))],
        out_specs=pl.BlockSpec((16, tn), lambda j: (0, j)),
        out_shape=jax.ShapeDtypeStruct((16, n), F32), compiler_params=_params(1, False))(a16, w, b)


def _w_mod_update(a16, d16, w, m, v):
    D, n = w.shape
    tn = _tile(n, 256)

    def body(a_ref, d_ref, w_ref, m_ref, v_ref, g_ref, dl_ref, m2_ref, v2_ref, p_ref):
        @pl.when(pl.program_id(0) == 0)
        def _():
            p_ref[...] = jnp.zeros_like(p_ref)
        a = a_ref[...]
        db = d_ref[...].astype(BF16)
        wv = w_ref[...]
        g = _tn((a * _sigmoid(a)).astype(BF16), db)
        g_ref[...] = g
        d, m2, v2 = _adam_math(wv, g, m_ref[...], v_ref[...])
        dl_ref[...] = d
        m2_ref[...] = m2
        v2_ref[...] = v2
        p_ref[...] += _nt(db, wv.astype(BF16))

    wspec = pl.BlockSpec((D, tn), lambda j: (0, j))
    shp = jax.ShapeDtypeStruct((D, n), F32)
    return pl.pallas_call(
        body, name="w_mod_update", grid=(n // tn,),
        in_specs=[pl.BlockSpec((16, D), lambda j: (0, 0)), pl.BlockSpec((16, tn), lambda j: (0, j)), wspec, wspec, wspec],
        out_specs=[wspec, wspec, wspec, wspec, pl.BlockSpec((16, D), lambda j: (0, 0))],
        out_shape=[shp, shp, shp, shp, jax.ShapeDtypeStruct((16, D), F32)],
        compiler_params=_params(1))(a16, d16, w, m, v)


def _sum_devices(g8, name):
    _, R, Cc = g8.shape

    def body(g_ref, o_ref):
        t = g_ref[0]
        for d in range(1, N_DEV):
            t = t + g_ref[d]
        o_ref[...] = t

    return pl.pallas_call(body, name=name, out_shape=jax.ShapeDtypeStruct((R, Cc), F32))(g8)


def _c_ctx_grad(parts, c_ctx):
    D = c_ctx.shape[1]

    def body(p_ref, c_ref, o_ref):
        t = p_ref[0]
        for k in range(1, N_CHIPS):
            t = t + p_ref[2 * k]
        cv = c_ref[...]
        sg = _sigmoid(cv)
        o_ref[...] = t * (sg * (1.0 + cv * (1.0 - sg)))

    return pl.pallas_call(body, name="c_ctx_grad", out_shape=jax.ShapeDtypeStruct((1, D), F32))(parts, c_ctx)


def _pack_rows(items, nrows, width, name):
    arrays, plan = [], []
    for a, r0, nr, d0, c0 in items:
        for ai, b in enumerate(arrays):
            if b is a:
                break
        else:
            ai = len(arrays)
            arrays.append(a)
        plan.append((ai, r0, nr, d0, c0, a.shape[1]))

    def body(*refs):
        o_ref = refs[-1]
        o_ref[...] = jnp.zeros_like(o_ref)
        for ai, r0, nr, d0, c0, w in plan:
            o_ref[d0:d0 + nr, c0:c0 + w] = refs[ai][r0:r0 + nr, :]

    return pl.pallas_call(body, name=name, out_shape=jax.ShapeDtypeStruct((nrows, width), F32))(*arrays)


HBM_SPEC = pl.BlockSpec(memory_space=pltpu.HBM)
SEM_SPEC = pl.BlockSpec(memory_space=pltpu.SEMAPHORE)
SPLIT_PARAMS = pltpu.CompilerParams(has_side_effects=pltpu.SideEffectType.DATAFLOW_SIDE_EFFECTING)


def _in_hbm(a):
    return pltpu.with_memory_space_constraint(a, pltpu.HBM)


def _ag_chips_start(fulls, shapes, axes, after, name):
    nw = len(fulls)

    def body(*refs):
        in_refs, send, recv, token = refs[:nw], refs[nw + 1], refs[nw + 2], refs[-1]
        x, y, c = _my_pos()
        k0 = 2 * x + y
        for w in range(nw):
            own = _region(in_refs[w], k0, c, shapes[w], axes[w])
            for j, ch in enumerate(_other_chips(x, y)):
                _remote(own, own, send.at[3 * w + j], recv.at[3 * w + j], (ch[0], ch[1], c)).start()
        token[...] = jnp.zeros_like(token)

    return pl.pallas_call(
        body, name=name,
        out_shape=(pltpu.SemaphoreType.DMA((3 * nw,)), pltpu.SemaphoreType.DMA((3 * nw,)),
                   *[pltpu.HBM(f.shape, f.dtype) for f in fulls], jax.ShapeDtypeStruct((8, LANES), F32)),
        in_specs=[HBM_SPEC] * nw + [pl.BlockSpec(memory_space=pl.ANY)],
        out_specs=(SEM_SPEC, SEM_SPEC, *[HBM_SPEC] * nw, pl.BlockSpec(memory_space=pltpu.VMEM)),
        input_output_aliases={w: 2 + w for w in range(nw)},
        compiler_params=SPLIT_PARAMS)(*[_in_hbm(f) for f in fulls], after)


def _ag_chips_wait(send, recv, fulls, shapes, axes, after, name):
    nw = len(fulls)

    def body(*refs):
        in_refs, send_ref, recv_ref = refs[:nw], refs[nw], refs[nw + 1]
        x, y, c = _my_pos()
        k0 = 2 * x + y
        for w in range(nw):
            own = _region(in_refs[w], k0, c, shapes[w], axes[w])
            for j, ch in enumerate(_other_chips(x, y)):
                got = _region(in_refs[w], 2 * ch[0] + ch[1], c, shapes[w], axes[w])
                cp = _remote(own, got, send_ref.at[3 * w + j], recv_ref.at[3 * w + j], (ch[0], ch[1], c))
                cp.wait_send()
                cp.wait_recv()

    return pl.pallas_call(
        body, name=name,
        out_shape=tuple(pltpu.HBM(f.shape, f.dtype) for f in fulls),
        in_specs=[HBM_SPEC] * nw + [SEM_SPEC, SEM_SPEC, pl.BlockSpec(memory_space=pl.ANY)],
        out_specs=tuple([HBM_SPEC] * nw),
        input_output_aliases={w: w for w in range(nw)},
        compiler_params=SPLIT_PARAMS)(*fulls, send, recv, after)


def _ag_forward(fulls, shapes, axes, name):
    nw = len(fulls)

    def body(*refs):
        out_refs = refs[nw:2 * nw]
        send, recv = refs[2 * nw:]
        x, y, c = _my_pos()
        sib = (x, y, 1 - c)
        chips = _other_chips(x, y)
        copies = []
        for w in range(nw):
            for j, ch in enumerate(chips):
                got = _region(out_refs[w], 2 * ch[0] + ch[1], c, shapes[w], axes[w])
                cp = _remote(got, got, send.at[w, j], recv.at[w, j], sib)
                cp.start()
                copies.append(cp)
        for w in range(nw):
            for j, ch in enumerate(chips):
                got = _region(out_refs[w], 2 * ch[0] + ch[1], 1 - c, shapes[w], axes[w])
                _remote(got, got, send.at[w, j], recv.at[w, j], sib).wait_recv()
        for cp in copies:
            cp.wait_send()

    anyspec = pl.BlockSpec(memory_space=pl.ANY)
    return pl.pallas_call(
        body, name=name,
        out_shape=[jax.ShapeDtypeStruct(f.shape, BF16) for f in fulls],
        in_specs=[anyspec] * nw, out_specs=[anyspec] * nw,
        input_output_aliases={w: w for w in range(nw)},
        scratch_shapes=[pltpu.SemaphoreType.DMA((nw, 3)), pltpu.SemaphoreType.DMA((nw, 3))])(*fulls)


def _rs_sibling_start(grads, shapes, axes, name):
    nw = len(grads)
    npc = max(1 if a == 1 else N_CHIPS for a in axes)

    def body(*refs):
        g_refs, l_refs, send, recv, token = refs[:nw], refs[nw:2 * nw], refs[2 * nw], refs[2 * nw + 1], refs[-1]
        x, y, c = _my_pos()
        for w in range(nw):
            src = _half_pieces(g_refs[w], 1 - c, shapes[w], axes[w])
            dst = _half_pieces(l_refs[w], 1 - c, shapes[w], axes[w])
            for i, (s, d) in enumerate(zip(src, dst)):
                _remote(s, d, send.at[npc * w + i], recv.at[npc * w + i], (x, y, 1 - c)).start()
        token[...] = jnp.zeros_like(token)

    thru = [pltpu.HBM(g.shape, g.dtype) for g in grads]
    return pl.pallas_call(
        body, name=name,
        out_shape=(pltpu.SemaphoreType.DMA((npc * nw,)), pltpu.SemaphoreType.DMA((npc * nw,)), *thru, *thru,
                   jax.ShapeDtypeStruct((8, LANES), F32)),
        in_specs=[HBM_SPEC] * (2 * nw),
        out_specs=(SEM_SPEC, SEM_SPEC, *[HBM_SPEC] * (2 * nw), pl.BlockSpec(memory_space=pltpu.VMEM)),
        input_output_aliases={i: 2 + i for i in range(2 * nw)},
        compiler_params=SPLIT_PARAMS)(*[_in_hbm(g) for g in grads], *[_in_hbm(lax.empty(g.shape, g.dtype)) for g in grads])


def _rs_sibling_wait(send, recv, grads, lands, shapes, axes, after, name):
    nw = len(grads)
    npc = max(1 if a == 1 else N_CHIPS for a in axes)

    def body(*refs):
        g_refs, l_refs, send_ref, recv_ref = refs[:nw], refs[nw:2 * nw], refs[2 * nw], refs[2 * nw + 1]
        x, y, c = _my_pos()
        for w in range(nw):
            sent = _half_pieces(g_refs[w], 1 - c, shapes[w], axes[w])
            mine = _half_pieces(l_refs[w], c, shapes[w], axes[w])
            for i, (s, d) in enumerate(zip(sent, mine)):
                cp = _remote(s, d, send_ref.at[npc * w + i], recv_ref.at[npc * w + i], (x, y, 1 - c))
                cp.wait_send()
                cp.wait_recv()

    thru = tuple(pltpu.HBM(g.shape, g.dtype) for g in grads)
    return pl.pallas_call(
        body, name=name, out_shape=thru + thru,
        in_specs=[HBM_SPEC] * (2 * nw) + [SEM_SPEC, SEM_SPEC, pl.BlockSpec(memory_space=pl.ANY)],
        out_specs=tuple([HBM_SPEC] * (2 * nw)),
        input_output_aliases={i: i for i in range(2 * nw)},
        compiler_params=SPLIT_PARAMS)(*grads, *lands, send, recv, after)


def _rs_chips_start(sums, name):
    nw = len(sums)

    def body(*refs):
        s_refs, l_refs, send, recv, token = refs[:nw], refs[nw:2 * nw], refs[2 * nw], refs[2 * nw + 1], refs[-1]
        x, y, c = _my_pos()
        k0 = 2 * x + y
        for w in range(nw):
            for j, ch in enumerate(_other_chips(x, y)):
                _remote(s_refs[w].at[2 * ch[0] + ch[1]], l_refs[w].at[k0], send.at[3 * w + j], recv.at[3 * w + j],
                        (ch[0], ch[1], c)).start()
        token[...] = jnp.zeros_like(token)

    thru = [pltpu.HBM(s.shape, s.dtype) for s in sums]
    return pl.pallas_call(
        body, name=name,
        out_shape=(pltpu.SemaphoreType.DMA((3 * nw,)), pltpu.SemaphoreType.DMA((3 * nw,)), *thru, *thru,
                   jax.ShapeDtypeStruct((8, LANES), F32)),
        in_specs=[HBM_SPEC] * (2 * nw),
        out_specs=(SEM_SPEC, SEM_SPEC, *[HBM_SPEC] * (2 * nw), pl.BlockSpec(memory_space=pltpu.VMEM)),
        input_output_aliases={i: 2 + i for i in range(2 * nw)},
        compiler_params=SPLIT_PARAMS)(*[_in_hbm(s) for s in sums], *[_in_hbm(lax.empty(s.shape, s.dtype)) for s in sums])


def _rs_chips_wait(send, recv, sums, lands, after, name):
    nw = len(sums)

    def body(*refs):
        s_refs, l_refs, send_ref, recv_ref = refs[:nw], refs[nw:2 * nw], refs[2 * nw], refs[2 * nw + 1]
        x, y, c = _my_pos()
        for w in range(nw):
            for j, ch in enumerate(_other_chips(x, y)):
                kj = 2 * ch[0] + ch[1]
                cp = _remote(s_refs[w].at[kj], l_refs[w].at[kj], send_ref.at[3 * w + j], recv_ref.at[3 * w + j],
                             (ch[0], ch[1], c))
                cp.wait_send()
                cp.wait_recv()

    thru = tuple(pltpu.HBM(s.shape, s.dtype) for s in sums)
    return pl.pallas_call(
        body, name=name, out_shape=thru + thru,
        in_specs=[HBM_SPEC] * (2 * nw) + [SEM_SPEC, SEM_SPEC, pl.BlockSpec(memory_space=pl.ANY)],
        out_specs=tuple([HBM_SPEC] * (2 * nw)),
        input_output_aliases={i: i for i in range(2 * nw)},
        compiler_params=SPLIT_PARAMS)(*sums, *lands, send, recv, after)


LOSS_LANE = 64


def kernel(x, c, ctx, c_ctx, w_mod, b_mod, norm_mix, norm_ffn, w_in, ret_decay, attn_sink, w_out, w_gate, w_up, w_down, norm_final, loss_target, m_c_ctx, m_w_mod, m_b_mod, m_norm_mix, m_norm_ffn, m_w_in, m_ret_decay, m_attn_sink, m_w_out, m_w_gate, m_w_up, m_w_down, m_norm_final, v_c_ctx, v_w_mod, v_b_mod, v_norm_mix, v_norm_ffn, v_w_in, v_ret_decay, v_attn_sink, v_w_out, v_w_gate, v_w_up, v_w_down, v_norm_final):
    D = x.shape[-1]
    n3 = w_mod.shape[-1]
    xi, yi, ci = _my_pos()
    b = 4 * xi + 2 * yi + ci
    k0 = 2 * xi + yi
    cvec = jnp.reshape(ci, (1,)).astype(jnp.int32)
    kc = jnp.stack([k0, ci]).astype(jnp.int32)

    dense = [("w_in", w_in[0], 1), ("w_out", w_out[0], 0), ("w_gate", w_gate[0], 1), ("w_up", w_up[0], 1),
             ("w_down", w_down[0], 0)]
    axes = [a for _, _, a in dense]
    shapes = [w.shape for _, w, _ in dense]
    c_all = _allgather8(c, "gather_c").reshape(N_DEV, D)
    c_ctx2 = c_ctx.reshape(1, D)
    a16 = _pack_rows([(c_all, 0, N_DEV, 0, 0), (c_ctx2, 0, 1, N_DEV, 0)], 16, D, "pack_cond")
    b_cols = lax.dynamic_slice_in_dim(b_mod, k0 * n3, n3, axis=1)
    mod16 = _mod_rows(a16, w_mod[0], b_cols, "mod_rows")
    mod_all = _allgather8(mod16, "gather_mod")

    own_in = _cast_into_full(dense[0][1], kc, axes[0], "cast_w_in")
    agi = _ag_chips_start([own_in], shapes[:1], axes[:1], mod_all, "ag_in_start")
    own16 = [_cast_into_full(w, kc, a, "cast_" + n) for n, w, a in dense[1:]]
    (f_in,) = _ag_forward(list(_ag_chips_wait(agi[0], agi[1], [agi[2]], shapes[:1], axes[:1], own16[-1], "ag_in_wait")),
                          shapes[:1], axes[:1], "ag_in_forward")
    ag = _ag_chips_start(own16, shapes[1:], axes[1:], f_in, "ag_rest_start")
    ag_send, ag_recv, ag_thru, ag_tok = ag[0], ag[1], list(ag[2:-1]), ag[-1][0:1, 0:1]

    def rest_weights(after):
        landed_w = _ag_chips_wait(ag_send, ag_recv, ag_thru, shapes[1:], axes[1:], after, "ag_rest_wait")
        return _ag_forward(list(landed_w), shapes[1:], axes[1:], "ag_rest_forward")
    mine = jnp.stack([lax.dynamic_index_in_dim(mod_all, 2 * k + ci, 0, keepdims=False) for k in range(N_CHIPS)])
    mod = lax.dynamic_index_in_dim(mine, b, 1, keepdims=False).reshape(6, D)
    modc = mine[:, N_DEV].reshape(6, D)

    lg = -jnp.exp(ret_decay[0])

    index = {n: i for i, (n, _, _) in enumerate(dense)}
    pending, done = [], {}

    sib = []

    def finish_sibling(after):
        names, shp, axs, st = sib.pop()
        nw = len(names)
        res = _rs_sibling_wait(st[0], st[1], list(st[2:2 + nw]), list(st[2 + nw:2 + 2 * nw]), shp, axs, after,
                               "rs_sibling_wait_" + names[0])
        sums = [_add_halves(res[i], res[nw + i], cvec, s, a, "add_halves_" + n)
                for i, (s, a, n) in enumerate(zip(shp, axs, names))]
        ch = _rs_chips_start(sums, "rs_chips_start_" + names[0])
        pending.append((names, ch[0], ch[1], list(ch[2:2 + nw]), list(ch[2 + nw:2 + 2 * nw])))
        return ch[-1][0:1, 0:1]

    def on_grads(names, gs):
        ids = [index[n] for n in names]
        shp, axs = [shapes[i] for i in ids], [axes[i] for i in ids]
        st = _rs_sibling_start(gs, shp, axs, "rs_sibling_start_" + names[0])
        sib.append((names, shp, axs, st))
        return st[-1][0:1, 0:1]

    out = _local_step(x[0], ctx[0], loss_target[0], mod, modc, norm_mix + ag_tok, norm_ffn, norm_final.reshape(1, D), lg,
                      attn_sink, f_in, rest_weights, on_grads, finish_sibling)

    def finish(group, after):
        names, send, recv, sums, lands = group
        res = _rs_chips_wait(send, recv, sums, lands, after, "rs_chips_wait_" + names[0])
        return [_sum_chips(res[i], res[len(names) + i], kc, "sum_chips_" + n) for i, n in enumerate(names)]

    tok_in = finish_sibling(out["grad_x"])
    assert pending[-1][0] == ["w_in"]
    rest_names = [n for g in pending[:-1] for n in g[0]]
    after_in = out["small"][0:8, 0:LANES] + tok_in
    rest_halves = [h for g in pending[:-1] for h in finish(g, after_in)]
    g_rest = dict(zip(rest_names, _exchange_halves(rest_halves, "exchange_halves_rest")))

    nh = 2 * RET_HEADS
    small_all = _allgather8(out["small"], "gather_small")
    tot = _sum_devices(small_all, "sum_small")
    g_b_mod = (tot[0:6] + tot[6:12]).reshape(1, 6 * D)
    dmodc_tot = tot[6:12].reshape(1, 6 * D)
    dmod_rows = small_all[:, 0:6].reshape(N_DEV, 6 * D)
    d16 = _pack_rows([(dmod_rows, 0, N_DEV, 0, 0), (dmodc_tot, 0, 1, N_DEV, 0)], 16, 6 * D, "pack_dmod")
    d16 = lax.dynamic_slice_in_dim(d16, k0 * n3, n3, axis=1)
    g_w_mod, dl_w_mod, m2_w_mod, v2_w_mod, part = _w_mod_update(a16, d16, w_mod[0], m_w_mod[0], v_w_mod[0])
    part_all = _allgather8(part[N_DEV:N_DEV + 1], "gather_c_ctx")
    g_c_ctx = _c_ctx_grad(part_all, c_ctx2)
    loss = tot[15, LOSS_LANE]

    def pack(cc, bm, nm, nf, nfin, rd, sk, name):
        rd2 = rd.reshape(2, RET_HEADS)
        return _pack_rows([(bm.reshape(6, D), 0, 6, 0, 0), (cc.reshape(1, D), 0, 1, 6, 0), (nm.reshape(1, D), 0, 1, 7, 0),
                           (nf.reshape(1, D), 0, 1, 8, 0), (nfin.reshape(1, D), 0, 1, 9, 0),
                           (rd2, 0, 1, 10, 0), (rd2, 1, 1, 10, RET_HEADS), (sk.reshape(1, ATT_HEADS), 0, 1, 10, nh)],
                          16, D, name)

    w_s = pack(c_ctx, b_mod, norm_mix, norm_ffn, norm_final, ret_decay, attn_sink, "pack_w")
    g_s = _pack_rows([(g_b_mod.reshape(6, D), 0, 6, 0, 0), (g_c_ctx, 0, 1, 6, 0), (tot, 12, 3, 7, 0),
                      (tot[15:16, 0:nh + ATT_HEADS], 0, 1, 10, 0)], 16, D, "pack_g")
    m_s = pack(m_c_ctx, m_b_mod, m_norm_mix, m_norm_ffn, m_norm_final, m_ret_decay, m_attn_sink, "pack_m")
    v_s = pack(v_c_ctx, v_b_mod, v_norm_mix, v_norm_ffn, v_norm_final, v_ret_decay, v_attn_sink, "pack_v")
    small_upd = _adam(w_s, g_s, m_s, v_s, "adam_small")

    def unpack(t):
        return dict(b_mod=t[0:6].reshape(1, 6 * D), c_ctx=t[6], norm_mix=t[7:8], norm_ffn=t[8:9], norm_final=t[9],
                    ret_decay=t[10, :nh].reshape(1, 2, RET_HEADS), attn_sink=t[10, nh:nh + ATT_HEADS].reshape(1, ATT_HEADS))

    dense_w = dict(w_in=(w_in, m_w_in, v_w_in), w_out=(w_out, m_w_out, v_w_out), w_gate=(w_gate, m_w_gate, v_w_gate),
                   w_up=(w_up, m_w_up, v_w_up), w_down=(w_down, m_w_down, v_w_down))
    grads = dict(unpack(g_s), w_mod=g_w_mod[None])
    upd = [dict(unpack(t)) for t in small_upd]
    upd[0]["w_mod"], upd[1]["w_mod"], upd[2]["w_mod"] = dl_w_mod[None], m2_w_mod[None], v2_w_mod[None]
    def update(n, g):
        w_, m_, v_ = dense_w[n]
        res = _adam(w_[0], g, m_[0], v_[0], "adam_" + n)
        grads[n] = g[None]
        for u, r_ in zip(upd, res):
            u[n] = r_[None]
        return res[0]

    dep = small_upd[0][0:1, 0:1] + dl_w_mod[0:1, 0:1]
    for n in rest_names:
        dep = dep + update(n, g_rest[n])[0:1, 0:1]
    (g_in,) = _exchange_halves(finish(pending[-1], dep), "exchange_halves_in")
    update("w_in", g_in)

    order = ['c_ctx', 'w_mod', 'b_mod', 'norm_mix', 'norm_ffn', 'w_in', 'ret_decay', 'attn_sink', 'w_out', 'w_gate',
             'w_up', 'w_down', 'norm_final']
    outs = [loss, out["grad_x"][None]] + [grads[n] for n in order]
    for u in upd:
        outs += [u[n] for n in order]
    return tuple(outs)
```

```python
import numpy as np
import jax
import jax.numpy as jnp
from jax import lax
from jax.experimental import pallas as pl
from jax.experimental.pallas import tpu as pltpu

F32 = jnp.float32
BF16 = jnp.bfloat16

RET_HEADS = 8
RET_DK = 64
RET_DV = 128
CHUNK = 128
ATT_HEADS = 16
ATT_KV_HEADS = 4
ATT_DH = 64
GRID_W = 64
ROPE_BASE = 10000.0
NORM_EPS = 1e-6
ADAM_LR = 0.001
ADAM_B1 = 0.9
ADAM_B2 = 0.999
ADAM_EPS = 1e-08
ADAM_WD = 0.01
ADAM_STEP = 10
NEG = -1e30
LANES = 128
VMEM_LIMIT = 56 * 1024 * 1024
ROWS_PER_LATCH = 1024
MESH = pl.DeviceIdType.MESH
N_CHIPS = 4
N_DEV = 8


def _nn(a, b):
    return jnp.dot(a, b, preferred_element_type=F32)


def _nt(a, b):
    return lax.dot_general(a, b, (((1,), (1,)), ((), ())), preferred_element_type=F32)


def _tn(a, b):
    return lax.dot_general(a, b, (((0,), (0,)), ((), ())), preferred_element_type=F32)


def _tile(n, pref, unit=LANES):
    t = min(n, pref)
    t -= t % unit
    while t > unit and n % t:
        t -= unit
    if t <= 0 or n % t:
        return n
    return t


def _params(ndim, vmem=True):
    return pltpu.CompilerParams(dimension_semantics=("arbitrary",) * ndim,
                                vmem_limit_bytes=VMEM_LIMIT if vmem else None)


def _sigmoid(x):
    return 0.5 * jnp.tanh(0.5 * x) + 0.5


def _fsum(x):
    return jnp.sum(jnp.sum(x, axis=0, keepdims=True), axis=1, keepdims=True)


def _rope_tables(L):
    lane = np.arange(LANES)
    d = lane % 64
    inv_r = jnp.asarray(ROPE_BASE, F32) ** (-jnp.arange(32, dtype=F32) / 32)
    t = jnp.arange(L)
    ang_r = t.astype(F32)[:, None] * jnp.tile(inv_r, LANES // 32)[None, :]
    Rr = np.zeros((LANES, LANES), np.float32)
    for l in range(LANES):
        if d[l] < 32:
            Rr[l + 32, l] = -1.0
        else:
            Rr[l - 32, l] = 1.0
    inv_a = jnp.asarray(ROPE_BASE, F32) ** (-jnp.arange(16, dtype=F32) / 16)
    rows = (t // GRID_W).astype(F32)
    cols = (t % GRID_W).astype(F32)
    dd = d % 32
    pos = jnp.where(jnp.asarray(d < 32)[None, :], rows[:, None], cols[:, None])
    ang_a = pos * jnp.tile(inv_a, LANES // 16)[None, :]
    Ra = np.zeros((LANES, LANES), np.float32)
    for l in range(LANES):
        if dd[l] < 16:
            Ra[l + 16, l] = -1.0
        else:
            Ra[l - 16, l] = 1.0
    D0 = np.zeros((LANES, LANES), np.float32)
    D1 = np.zeros((LANES, LANES), np.float32)
    for l in range(LANES):
        D0[l % 64, l] = 1.0
        D1[64 + l % 64, l] = 1.0
    return dict(
        Cr=jnp.cos(ang_r), Sr=jnp.sin(ang_r), Rr=jnp.asarray(Rr, BF16), RrT=jnp.asarray(Rr.T, BF16),
        Ca=jnp.cos(ang_a), Sa=jnp.sin(ang_a), Ra=jnp.asarray(Ra, BF16), RaT=jnp.asarray(Ra.T, BF16),
        D0=jnp.asarray(D0, BF16), D1=jnp.asarray(D1, BF16),
        D0T=jnp.asarray(D0.T, BF16), D1T=jnp.asarray(D1.T, BF16))


def _norm_mod(xf, g, sh, sc):
    r = lax.rsqrt(jnp.mean(xf * xf, axis=-1, keepdims=True) + NORM_EPS)
    return (xf * r * g) * (1.0 + sc) + sh


def _norm_mod_matmul(x, g, sh, sc, w, name):
    M, D = x.shape
    N = w.shape[1]
    tm, tn = _tile(M, ROWS_PER_LATCH, 8), _tile(N, 768)

    def body(x_ref, g_ref, sh_ref, sc_ref, w_ref, p_ref, h_ref, hs):
        @pl.when(pl.program_id(1) == 0)
        def _():
            hb = _norm_mod(x_ref[...], g_ref[...], sh_ref[...], sc_ref[...]).astype(BF16)
            hs[...] = hb
            h_ref[...] = hb
        p_ref[...] = _nn(hs[...], w_ref[...]).astype(BF16)

    vec = pl.BlockSpec((1, D), lambda i, j: (0, 0))
    return pl.pallas_call(
        body, name=name, grid=(M // tm, N // tn),
        in_specs=[pl.BlockSpec((tm, D), lambda i, j: (i, 0)), vec, vec, vec,
                  pl.BlockSpec((D, tn), lambda i, j: (0, j))],
        out_specs=[pl.BlockSpec((tm, tn), lambda i, j: (i, j)), pl.BlockSpec((tm, D), lambda i, j: (i, 0))],
        out_shape=[jax.ShapeDtypeStruct((M, N), BF16), jax.ShapeDtypeStruct((M, D), BF16)],
        scratch_shapes=[pltpu.VMEM((tm, D), BF16)],
        compiler_params=_params(2))(x, g, sh, sc, w)


def _proj_residual(a, w, xres, gt, name):
    M, K = a.shape
    N = w.shape[1]
    tm, tn = _tile(M, ROWS_PER_LATCH, 8), _tile(N, 1024 if K <= 2048 else 512)

    def body(a_ref, w_ref, x_ref, gt_ref, xo_ref, o_ref):
        o = _nn(a_ref[...], w_ref[...])
        o_ref[...] = o.astype(BF16)
        xo_ref[...] = x_ref[...] + gt_ref[...] * o

    return pl.pallas_call(
        body, name=name, grid=(M // tm, N // tn),
        in_specs=[pl.BlockSpec((tm, K), lambda i, j: (i, 0)), pl.BlockSpec((K, tn), lambda i, j: (0, j)),
                  pl.BlockSpec((tm, tn), lambda i, j: (i, j)), pl.BlockSpec((1, tn), lambda i, j: (0, j))],
        out_specs=[pl.BlockSpec((tm, tn), lambda i, j: (i, j)), pl.BlockSpec((tm, tn), lambda i, j: (i, j))],
        out_shape=[jax.ShapeDtypeStruct((M, N), F32), jax.ShapeDtypeStruct((M, N), BF16)],
        compiler_params=_params(2))(a, w, xres, gt)


def _ffn_in(x1, g, sh, sc, wg, wu):
    M, D = x1.shape
    N = wg.shape[1]
    tm, tn = _tile(M, ROWS_PER_LATCH, 8), _tile(N, 512)

    def body(x_ref, g_ref, sh_ref, sc_ref, wg_ref, wu_ref, G_ref, U_ref, A_ref, h_ref, hs):
        @pl.when(pl.program_id(1) == 0)
        def _():
            hb = _norm_mod(x_ref[...], g_ref[...], sh_ref[...], sc_ref[...]).astype(BF16)
            hs[...] = hb
            h_ref[...] = hb
        G = _nn(hs[...], wg_ref[...])
        U = _nn(hs[...], wu_ref[...])
        G_ref[...] = G.astype(BF16)
        U_ref[...] = U.astype(BF16)
        A_ref[...] = (G * _sigmoid(G) * U).astype(BF16)

    vec = pl.BlockSpec((1, D), lambda i, j: (0, 0))
    wspec = pl.BlockSpec((D, tn), lambda i, j: (0, j))
    ospec = pl.BlockSpec((tm, tn), lambda i, j: (i, j))
    big = jax.ShapeDtypeStruct((M, N), BF16)
    return pl.pallas_call(
        body, name="ffn_in", grid=(M // tm, N // tn),
        in_specs=[pl.BlockSpec((tm, D), lambda i, j: (i, 0)), vec, vec, vec, wspec, wspec],
        out_specs=[ospec, ospec, ospec, pl.BlockSpec((tm, D), lambda i, j: (i, 0))],
        out_shape=[big, big, big, jax.ShapeDtypeStruct((M, D), BF16)],
        scratch_shapes=[pltpu.VMEM((tm, D), BF16)],
        compiler_params=_params(2))(x1, g, sh, sc, wg, wu)


def _final(x2, gn, tgt, gt, saved):
    M, D = x2.shape
    tm = _tile(M, 256, 8)

    def body(x_ref, g_ref, t_ref, gt_ref, sv_ref, dx_ref, loss_ref, dg_ref, dz_ref, dgt_ref):
        @pl.when(pl.program_id(0) == 0)
        def _():
            loss_ref[...] = jnp.zeros_like(loss_ref)
            dg_ref[...] = jnp.zeros_like(dg_ref)
            dgt_ref[...] = jnp.zeros_like(dgt_ref)
        x = x_ref[...]
        g = g_ref[...]
        r = lax.rsqrt(jnp.mean(x * x, axis=-1, keepdims=True) + NORM_EPS)
        xh = x * r
        e = xh * g - t_ref[...]
        loss_ref[...] += (0.5 / D) * _fsum(e * e)
        dy = e * (1.0 / D)
        dg_ref[...] += jnp.sum(dy * xh, axis=0, keepdims=True)
        dxh = dy * g
        d = r * (dxh - xh * jnp.mean(dxh * xh, axis=-1, keepdims=True))
        dx_ref[...] = d
        dz_ref[...] = (d * gt_ref[...]).astype(BF16)
        dgt_ref[...] += jnp.sum(d * sv_ref[...].astype(F32), axis=0, keepdims=True)

    row = pl.BlockSpec((tm, D), lambda i: (i, 0))
    vec = pl.BlockSpec((1, D), lambda i: (0, 0))
    return pl.pallas_call(
        body, name="final_loss", grid=(M // tm,),
        in_specs=[row, vec, row, vec, row],
        out_specs=[row, pl.BlockSpec((1, LANES), lambda i: (0, 0)), vec, row, vec],
        out_shape=[jax.ShapeDtypeStruct((M, D), F32), jax.ShapeDtypeStruct((1, LANES), F32),
                   jax.ShapeDtypeStruct((1, D), F32), jax.ShapeDtypeStruct((M, D), BF16),
                   jax.ShapeDtypeStruct((1, D), F32)],
        compiler_params=_params(1))(x2, gn, tgt, gt, saved)


def _col_group(blk0, nblk):
    return int(np.gcd(blk0, nblk)) if blk0 else nblk


def _rope_cols(src, blk0, nblk, Ct, St, R, scale, rope, name):
    M = src.shape[0]
    tm = _tile(M, 512, 8)
    wb = _col_group(blk0, nblk)

    def body(x_ref, c_ref, s_ref, r_ref, o_ref):
        for j in range(wb):
            cols = slice(j * LANES, (j + 1) * LANES)
            x = x_ref[:, cols]
            xf = x.astype(F32)
            if rope:
                xf = xf * c_ref[...] + _nn(x.astype(BF16), r_ref[...]) * s_ref[...]
            o_ref[:, cols] = (xf * scale).astype(BF16)

    tab = pl.BlockSpec((tm, LANES), lambda i, j: (i, 0))
    return pl.pallas_call(
        body, name=name, grid=(M // tm, nblk // wb),
        in_specs=[pl.BlockSpec((tm, wb * LANES), lambda i, j: (i, blk0 // wb + j)), tab, tab,
                  pl.BlockSpec((LANES, LANES), lambda i, j: (0, 0))],
        out_specs=pl.BlockSpec((tm, wb * LANES), lambda i, j: (i, j)),
        out_shape=jax.ShapeDtypeStruct((M, nblk * LANES), BF16),
        compiler_params=_params(2, False))(src, Ct, St, R)


def _dup_heads(src, blk0, npair, Ct, St, R, D0, D1, rope, name):
    M = src.shape[0]
    tm = _tile(M, 512, 8)

    def body(x_ref, c_ref, s_ref, r_ref, d0_ref, d1_ref, o_ref):
        x = x_ref[...]
        if rope:
            x = (x.astype(F32) * c_ref[...] + _nn(x, r_ref[...]) * s_ref[...]).astype(BF16)
        o_ref[0] = _nn(x, d0_ref[...]).astype(BF16)
        o_ref[1] = _nn(x, d1_ref[...]).astype(BF16)

    tab = pl.BlockSpec((tm, LANES), lambda i, p: (i, 0))
    mat = pl.BlockSpec((LANES, LANES), lambda i, p: (0, 0))
    return pl.pallas_call(
        body, name=name, grid=(M // tm, npair),
        in_specs=[pl.BlockSpec((tm, LANES), lambda i, p: (i, blk0 + p)), tab, tab, mat, mat, mat],
        out_specs=pl.BlockSpec((2, tm, LANES), lambda i, p: (p, i, 0)),
        out_shape=jax.ShapeDtypeStruct((2 * npair, M, LANES), BF16),
        compiler_params=_params(2, False))(src, Ct, St, R, D0, D1)


def _unrope_cols(dsrc, dst, blk0, nblk, Ct, St, RT, scale, rope, name):
    M = dsrc.shape[0]
    tm = _tile(M, 512, 8)
    wb = _col_group(blk0, nblk)

    def body(x_ref, c_ref, s_ref, r_ref, dst_ref, o_ref):
        del dst_ref
        for j in range(wb):
            cols = slice(j * LANES, (j + 1) * LANES)
            xf = x_ref[:, cols].astype(F32)
            if rope:
                xf = xf * c_ref[...] + _nn((xf * s_ref[...]).astype(BF16), r_ref[...])
            o_ref[:, cols] = (xf * scale).astype(BF16)

    tab = pl.BlockSpec((tm, LANES), lambda i, j: (i, 0))
    return pl.pallas_call(
        body, name=name, grid=(M // tm, nblk // wb),
        in_specs=[pl.BlockSpec((tm, wb * LANES), lambda i, j: (i, j)), tab, tab,
                  pl.BlockSpec((LANES, LANES), lambda i, j: (0, 0)),
                  pl.BlockSpec(memory_space=pl.ANY)],
        out_specs=pl.BlockSpec((tm, wb * LANES), lambda i, j: (i, blk0 // wb + j)),
        out_shape=jax.ShapeDtypeStruct(dst.shape, dst.dtype),
        input_output_aliases={4: 0},
        compiler_params=_params(2, False))(dsrc, Ct, St, RT, dst)


def _fold_heads(parts, dst, blk0, npair, Ct, St, RT, D0T, D1T, rope, name):
    M = parts[0][0].shape[1]
    nb = M // CHUNK
    R = _tile(M, 1024, CHUNK)
    rb = R // CHUNK
    nrefs = sum(1 if s == 0 else 2 for _, s in parts)

    def body(*refs):
        part_refs = list(refs[:nrefs])
        c_ref, s_ref, r_ref, d0_ref, d1_ref, dst_ref, o_ref = refs[nrefs:]
        del dst_ref
        i = pl.program_id(0)
        tot = [jnp.zeros((R, LANES), F32), jnp.zeros((R, LANES), F32)]
        for _, shift in parts:
            main = part_refs.pop(0)
            if shift == 0:
                for e in range(2):
                    tot[e] = tot[e] + main[e].astype(F32)
                continue
            edge = part_refs.pop(0)
            ok = (i + 1) * rb <= nb - 1 if shift > 0 else i > 0
            for e in range(2):
                ed = jnp.where(ok, edge[e].astype(F32), 0.0)
                if rb == 1:
                    tot[e] = tot[e] + ed
                elif shift > 0:
                    tot[e] = tot[e] + jnp.concatenate([main[e, CHUNK:, :].astype(F32), ed], axis=0)
                else:
                    tot[e] = tot[e] + jnp.concatenate([ed, main[e, :R - CHUNK, :].astype(F32)], axis=0)
        f = _nn(tot[0].astype(BF16), d0_ref[...]) + _nn(tot[1].astype(BF16), d1_ref[...])
        if rope:
            f = f * c_ref[...] + _nn((f * s_ref[...]).astype(BF16), r_ref[...])
        o_ref[...] = f.astype(BF16)

    in_specs, args = [], []
    for a, shift in parts:
        assert shift in (-1, 0, 1)
        in_specs.append(pl.BlockSpec((2, R, LANES), lambda i, p: (p, i, 0)))
        args.append(a)
        if shift > 0:
            in_specs.append(pl.BlockSpec((2, CHUNK, LANES), lambda i, p: (p, jnp.minimum((i + 1) * rb, nb - 1), 0)))
            args.append(a)
        elif shift < 0:
            in_specs.append(pl.BlockSpec((2, CHUNK, LANES), lambda i, p: (p, jnp.maximum(i * rb - 1, 0), 0)))
            args.append(a)
    tab = pl.BlockSpec((R, LANES), lambda i, p: (i, 0))
    mat = pl.BlockSpec((LANES, LANES), lambda i, p: (0, 0))
    return pl.pallas_call(
        body, name=name, grid=(M // R, npair),
        in_specs=in_specs + [tab, tab, mat, mat, mat, pl.BlockSpec(memory_space=pl.ANY)],
        out_specs=pl.BlockSpec((R, LANES), lambda i, p: (i, blk0 + p)),
        out_shape=jax.ShapeDtypeStruct(dst.shape, dst.dtype),
        input_output_aliases={nrefs + 5: 0},
        compiler_params=_params(2, False))(*args, Ct, St, RT, D0T, D1T, dst)


def _head_masks():
    lane = lax.broadcasted_iota(jnp.int32, (1, LANES), 1)
    return [lane < 64, lane >= 64]


def _decay_vecs(lam, mu):
    i = lax.broadcasted_iota(jnp.int32, (CHUNK, 1), 0).astype(F32)
    return dict(qf=jnp.exp(lam * (i + 1.0)), kf=jnp.exp(lam * (CHUNK - 1.0 - i)),
                qb=jnp.exp(mu * (CHUNK - i)), kb=jnp.exp(mu * i),
                gf=jnp.exp(lam * float(CHUNK)), gb=jnp.exp(mu * float(CHUNK)), i=i)


def _decay_mask(lam, mu):
    r = lax.broadcasted_iota(jnp.int32, (CHUNK, CHUNK), 0)
    c = lax.broadcasted_iota(jnp.int32, (CHUNK, CHUNK), 1)
    rel = (r - c).astype(F32)
    low = rel >= 0.0
    mf = jnp.exp(lam * jnp.maximum(rel, 0.0))
    mb = jnp.exp(mu * jnp.maximum(-rel, 0.0))
    return jnp.where(low, mf, mb), rel, low


def _lam_of(lg_ref, row, idx):
    return jnp.full((1, 1), lg_ref[row, idx], F32)


def _group_index(pair_blk, npairs):
    assert pair_blk % npairs == 0
    return pair_blk // npairs


def _ret_states_fwd(Kr, P, Krc, Pc, lg, rv_blk, npairs):
    L = Kr.shape[0]
    Lc = Krc.shape[0]
    N, ncc = L // CHUNK, Lc // CHUNK
    rv_grp = _group_index(rv_blk, npairs)

    heads = [(p, h) for p in range(npairs) for h in range(2)]
    kcols = lambda p: slice(p * LANES, (p + 1) * LANES)
    vcols = lambda p, h: slice((2 * p + h) * LANES, (2 * p + h + 1) * LANES)

    def body(lg_ref, k_ref, v_ref, kc_ref, vc_ref, sf_ref, S):
        n = pl.program_id(0)
        masks = _head_masks()

        @pl.when(n == 0)
        def _():
            for p, h in heads:
                lam = _lam_of(lg_ref, 0, 2 * p + h)
                dv = _decay_vecs(lam, lam)
                s = jnp.zeros((LANES, LANES), F32)
                for cc in range(ncc):
                    rows = slice(cc * CHUNK, (cc + 1) * CHUNK)
                    kw = jnp.where(masks[h], kc_ref[rows, kcols(p)].astype(F32) * dv["kf"], 0.0).astype(BF16)
                    s = dv["gf"] * s + _tn(kw, vc_ref[rows, vcols(p, h)])
                S[p, h] = s

        for p, h in heads:
            lam = _lam_of(lg_ref, 0, 2 * p + h)
            dv = _decay_vecs(lam, lam)
            s = S[p, h]
            sf_ref[p, 0, h] = s.astype(BF16)
            kw = jnp.where(masks[h], k_ref[:, kcols(p)].astype(F32) * dv["kf"], 0.0).astype(BF16)
            S[p, h] = dv["gf"] * s + _tn(kw, v_ref[:, vcols(p, h)])

    wq, wv = npairs * LANES, npairs * 2 * LANES
    return pl.pallas_call(
        body, name="ret_states_fwd", grid=(N,),
        in_specs=[pl.BlockSpec(memory_space=pltpu.SMEM),
                  pl.BlockSpec((CHUNK, wq), lambda n: (n, 0)),
                  pl.BlockSpec((CHUNK, wv), lambda n: (n, rv_grp)),
                  pl.BlockSpec((Lc, wq), lambda n: (0, 0)),
                  pl.BlockSpec((Lc, wv), lambda n: (0, rv_grp))],
        out_specs=pl.BlockSpec((npairs, 1, 2, LANES, LANES), lambda n: (0, n, 0, 0, 0)),
        out_shape=jax.ShapeDtypeStruct((npairs, N, 2, LANES, LANES), BF16),
        scratch_shapes=[pltpu.VMEM((npairs, 2, LANES, LANES), F32)],
        compiler_params=_params(1, False))(lg, Kr, P, Krc, Pc)


def _ret_chunk_fwd(q, k, v, sf, sb, hm, lam, mu, Mk):
    dv = _decay_vecs(lam, mu)
    qm = jnp.where(hm, q, jnp.zeros_like(q))
    qmf = qm.astype(F32)
    A = _nt(qm, k)
    Am = A * Mk
    Amb = Am.astype(BF16)
    Qf = (qmf * dv["qf"]).astype(BF16)
    Qb = (qmf * dv["qb"]).astype(BF16)
    O = _nn(Amb, v) + _nn(Qf, sf) + _nn(Qb, sb)
    return dict(dv=dv, Mk=Mk, qm=qm, Am=Am, Amb=Amb, Qf=Qf, Qb=Qb, O=O)


def _ret_out_fwd(Qr, Kr, P, Krc, Pc, SF, lg, rv_blk, rg_blk, npairs, d_mix):
    L = Qr.shape[0]
    Lc = Krc.shape[0]
    N, ncc = L // CHUNK, Lc // CHUNK

    rv_grp, rg_grp = _group_index(rv_blk, npairs), _group_index(rg_blk, npairs)
    heads = [(p, h) for p in range(npairs) for h in range(2)]
    kcols = lambda p: slice(p * LANES, (p + 1) * LANES)
    vcols = lambda p, h: slice((2 * p + h) * LANES, (2 * p + h + 1) * LANES)

    def body(lg_ref, q_ref, k_ref, v_ref, g_ref, sf_ref, kc_ref, vc_ref, y_ref, sb_ref, S, Mks):
        n = pl.program_id(0)
        masks = _head_masks()

        @pl.when(n == 0)
        def _():
            for p, h in heads:
                mu = _lam_of(lg_ref, 1, 2 * p + h)
                Mks[p, h] = _decay_mask(_lam_of(lg_ref, 0, 2 * p + h), mu)[0]
                dvb = _decay_vecs(mu, mu)
                s = jnp.zeros((LANES, LANES), F32)
                for cc in reversed(range(ncc)):
                    rows = slice(cc * CHUNK, (cc + 1) * CHUNK)
                    kw = jnp.where(masks[h], kc_ref[rows, kcols(p)].astype(F32) * dvb["kb"], 0.0).astype(BF16)
                    s = dvb["gb"] * s + _tn(kw, vc_ref[rows, vcols(p, h)])
                S[p, h] = s

        for p, h in heads:
            lam = _lam_of(lg_ref, 0, 2 * p + h)
            mu = _lam_of(lg_ref, 1, 2 * p + h)
            hm = masks[h]
            dvb = _decay_vecs(lam, mu)
            s = S[p, h]
            sbb = s.astype(BF16)
            sb_ref[p, 0, h] = sbb
            k = k_ref[:, kcols(p)]
            v = v_ref[:, vcols(p, h)]
            f = _ret_chunk_fwd(q_ref[:, kcols(p)], k, v, sf_ref[p, 0, h], sbb, hm, lam, mu, Mks[p, h])
            O = f["O"]
            r = lax.rsqrt(jnp.mean(O * O, axis=-1, keepdims=True) + NORM_EPS)
            g = g_ref[:, vcols(p, h)].astype(F32)
            y_ref[:, vcols(p, h)] = (O * r * (g * _sigmoid(g))).astype(BF16)
            kw = jnp.where(hm, k.astype(F32) * dvb["kb"], 0.0).astype(BF16)
            S[p, h] = dvb["gb"] * s + _tn(kw, v)

    rev = lambda n: N - 1 - n
    wq, wv = npairs * LANES, npairs * 2 * LANES
    st = pl.BlockSpec((npairs, 1, 2, LANES, LANES), lambda n: (0, rev(n), 0, 0, 0))
    return pl.pallas_call(
        body, name="ret_out_fwd", grid=(N,),
        in_specs=[pl.BlockSpec(memory_space=pltpu.SMEM),
                  pl.BlockSpec((CHUNK, wq), lambda n: (rev(n), 0)),
                  pl.BlockSpec((CHUNK, wq), lambda n: (rev(n), 0)),
                  pl.BlockSpec((CHUNK, wv), lambda n: (rev(n), rv_grp)),
                  pl.BlockSpec((CHUNK, wv), lambda n: (rev(n), rg_grp)),
                  st,
                  pl.BlockSpec((Lc, wq), lambda n: (0, 0)),
                  pl.BlockSpec((Lc, wv), lambda n: (0, rv_grp))],
        out_specs=[pl.BlockSpec((CHUNK, wv), lambda n: (rev(n), 0)), st],
        out_shape=[jax.ShapeDtypeStruct((L, d_mix), BF16),
                   jax.ShapeDtypeStruct((npairs, N, 2, LANES, LANES), BF16)],
        scratch_shapes=[pltpu.VMEM((npairs, 2, LANES, LANES), F32), pltpu.VMEM((npairs, 2, CHUNK, CHUNK), F32)],
        compiler_params=_params(1))(lg, Qr, Kr, P, P, SF, Krc, Pc)


ACC_ROWS = 8


def _ret_bwd1(Qr, Kr, P, Krc, Pc, SF, SB, dY, lg, rv_blk, rg_blk, npairs, d_proj):
    L = Qr.shape[0]
    Lc = Krc.shape[0]
    N, ncc = L // CHUNK, Lc // CHUNK
    rv_grp, rg_grp = _group_index(rv_blk, npairs), _group_index(rg_blk, npairs)
    heads = [(p, h) for p in range(npairs) for h in range(2)]
    kcols = lambda p: slice(p * LANES, (p + 1) * LANES)
    vcols = lambda p, h: slice((2 * p + h) * LANES, (2 * p + h + 1) * LANES)

    def body(lg_ref, q_ref, k_ref, v_ref, g_ref, sf_ref, sb_ref, dy_ref, kc_ref, vc_ref,
             dq_ref, dk_ref, dv_ref, dg_ref, do_ref, dkc_ref, dvc_ref, acc_ref, dS, T, Mks):
        n = pl.program_id(0)
        masks = _head_masks()

        @pl.when(n == 0)
        def _():
            dS[...] = jnp.zeros_like(dS)
            T[...] = jnp.zeros_like(T)
            acc_ref[...] = jnp.zeros_like(acc_ref)
            for p, h in heads:
                Mks[p, h] = _decay_mask(_lam_of(lg_ref, 0, 2 * p + h), _lam_of(lg_ref, 1, 2 * p + h))[0]

        def head_main(p, h):
            lam = _lam_of(lg_ref, 0, 2 * p + h)
            mu = _lam_of(lg_ref, 1, 2 * p + h)
            hm = masks[h]
            hs = vcols(p, h)
            v = v_ref[:, hs]
            k = k_ref[:, kcols(p)]
            sf = sf_ref[p, 0, h]
            sb = sb_ref[p, 0, h]
            f = _ret_chunk_fwd(q_ref[:, kcols(p)], k, v, sf, sb, hm, lam, mu, Mks[p, h])
            dv_, O = f["dv"], f["O"]
            r = lax.rsqrt(jnp.mean(O * O, axis=-1, keepdims=True) + NORM_EPS)
            on = O * r
            g = g_ref[:, hs].astype(F32)
            sg = _sigmoid(g)
            dy = dy_ref[:, hs].astype(F32)
            dg_ref[:, hs] = (dy * on * (sg * (1.0 + g * (1.0 - sg)))).astype(BF16)
            don = dy * (g * sg)
            dO = r * (don - on * jnp.mean(don * on, axis=-1, keepdims=True))
            dOb = dO.astype(BF16)
            do_ref[:, hs] = dOb
            dAm = _nt(dOb, v)
            T[p, h] += dAm * f["Am"]
            dAb = (dAm * f["Mk"]).astype(BF16)
            km = jnp.where(hm, k, jnp.zeros_like(k))
            dq = _nn(dAb, km)
            dk = _tn(dAb, f["qm"])
            dvh = _tn(f["Amb"], dOb)
            dQf = _nt(dOb, sf)
            dQb = _nt(dOb, sb)
            dq = dq + dQf * dv_["qf"] + dQb * dv_["qb"]
            acc_ref[p, h, 0:1, :] += _fsum(dQf * f["Qf"].astype(F32) * (dv_["i"] + 1.0))
            acc_ref[p, h, 1:2, :] += _fsum(dQb * f["Qb"].astype(F32) * (CHUNK - dv_["i"]))
            dSh = dS[p, h]
            dSb_ = dSh.astype(BF16)
            Kf = (km.astype(F32) * dv_["kf"]).astype(BF16)
            dKf = _nt(v, dSb_)
            dk = dk + jnp.where(hm, dKf * dv_["kf"], 0.0)
            acc_ref[p, h, 2:3, :] += _fsum(jnp.where(hm, dKf, 0.0) * Kf.astype(F32) * (CHUNK - 1.0 - dv_["i"]))
            dvh = dvh + _nn(Kf, dSb_)
            acc_ref[p, h, 3:4, :] += float(CHUNK) * dv_["gf"] * _fsum(dSh * sf.astype(F32))
            dSh = dv_["gf"] * dSh + _tn(f["Qf"], dOb)
            dS[p, h] = dSh
            dv_ref[:, hs] = dvh
            return dq, dk

        for p in range(npairs):
            dq0, dk0 = head_main(p, 0)
            dq1, dk1 = head_main(p, 1)
            dq_ref[:, kcols(p)] = dq0 + dq1
            dk_ref[:, kcols(p)] = dk0 + dk1

        @pl.when(n == N - 1)
        def _():
            for p, h in heads:
                lam = _lam_of(lg_ref, 0, 2 * p + h)
                dv_ = _decay_vecs(lam, lam)
                hm = masks[h]
                hs = vcols(p, h)
                states = [jnp.zeros((LANES, LANES), F32)]
                kws = []
                for cc in range(ncc):
                    rows = slice(cc * CHUNK, (cc + 1) * CHUNK)
                    kw = jnp.where(hm, kc_ref[rows, kcols(p)].astype(F32) * dv_["kf"], 0.0).astype(BF16)
                    kws.append(kw)
                    states.append(dv_["gf"] * states[-1] + _tn(kw, vc_ref[rows, hs]))
                d = dS[p, h]
                for cc in reversed(range(ncc)):
                    db = d.astype(BF16)
                    rows = slice(cc * CHUNK, (cc + 1) * CHUNK)
                    dKf_c = jnp.where(hm, _nt(vc_ref[rows, hs], db), 0.0)
                    part = dKf_c * dv_["kf"]
                    if h == 0:
                        dkc_ref[rows, kcols(p)] = part
                    else:
                        dkc_ref[rows, kcols(p)] += part
                    acc_ref[p, h, 2:3, :] += _fsum(dKf_c * kws[cc].astype(F32) * (CHUNK - 1.0 - dv_["i"]))
                    dvc_ref[rows, hs] = _nn(kws[cc], db)
                    acc_ref[p, h, 3:4, :] += float(CHUNK) * dv_["gf"] * _fsum(d * states[cc])
                    d = dv_["gf"] * d
                _, rel, low = _decay_mask(lam, lam)
                Th = T[p, h]
                acc_ref[p, h, 4:5, :] += _fsum(jnp.where(low, Th * rel, 0.0))
                acc_ref[p, h, 5:6, :] += _fsum(jnp.where(low, 0.0, -Th * rel))

    rev = lambda n: N - 1 - n
    wq, wv = npairs * LANES, npairs * 2 * LANES
    st = pl.BlockSpec((npairs, 1, 2, LANES, LANES), lambda n: (0, rev(n), 0, 0, 0))
    pair = pl.BlockSpec((CHUNK, wq), lambda n: (rev(n), 0))
    wide = lambda grp: pl.BlockSpec((CHUNK, wv), lambda n: (rev(n), grp))
    return pl.pallas_call(
        body, name="ret_bwd_desc", grid=(N,),
        in_specs=[pl.BlockSpec(memory_space=pltpu.SMEM), pair, pair, wide(rv_grp), wide(rg_grp), st, st, wide(0),
                  pl.BlockSpec((Lc, wq), lambda n: (0, 0)),
                  pl.BlockSpec((Lc, wv), lambda n: (0, rv_grp))],
        out_specs=[pair, pair, wide(0), wide(rg_grp), wide(0),
                   pl.BlockSpec((Lc, wq), lambda n: (0, 0)),
                   pl.BlockSpec((Lc, wv), lambda n: (0, 0)),
                   pl.BlockSpec((npairs, 2, ACC_ROWS, LANES), lambda n: (0, 0, 0, 0))],
        out_shape=[jax.ShapeDtypeStruct((L, npairs * LANES), F32),
                   jax.ShapeDtypeStruct((L, npairs * LANES), F32),
                   jax.ShapeDtypeStruct((L, npairs * 2 * LANES), F32),
                   jax.ShapeDtypeStruct((L, d_proj), BF16),
                   jax.ShapeDtypeStruct((L, npairs * 2 * LANES), BF16),
                   jax.ShapeDtypeStruct((Lc, npairs * LANES), F32),
                   jax.ShapeDtypeStruct((Lc, npairs * 2 * LANES), F32),
                   jax.ShapeDtypeStruct((npairs, 2, ACC_ROWS, LANES), F32)],
        scratch_shapes=[pltpu.VMEM((npairs, 2, LANES, LANES), F32), pltpu.VMEM((npairs, 2, CHUNK, CHUNK), F32),
                        pltpu.VMEM((npairs, 2, CHUNK, CHUNK), F32)],
        compiler_params=_params(1))(lg, Qr, Kr, P, P, SF, SB, dY, Krc, Pc)


def _ret_bwd2(Qr, Kr, P, Krc, Pc, SB, dO, dKr, dVp, dP, dKc, dVc, lg, rv_blk, npairs):
    L = Qr.shape[0]
    Lc = Krc.shape[0]
    N, ncc = L // CHUNK, Lc // CHUNK
    rv_grp = _group_index(rv_blk, npairs)
    heads = [(p, h) for p in range(npairs) for h in range(2)]
    kcols = lambda p: slice(p * LANES, (p + 1) * LANES)
    vcols = lambda p, h: slice((2 * p + h) * LANES, (2 * p + h + 1) * LANES)

    def body(lg_ref, q_ref, k_ref, v_ref, sb_ref, do_ref, dkin_ref, dvin_ref, kc_ref, vc_ref, dkcin_ref, dvcin_ref,
             dpin_ref, dk_ref, dv_ref, dkc_ref, dvc_ref, acc_ref, dS):
        del dpin_ref
        n = pl.program_id(0)
        masks = _head_masks()

        @pl.when(n == 0)
        def _():
            dS[...] = jnp.zeros_like(dS)
            acc_ref[...] = jnp.zeros_like(acc_ref)

        def head_main(p, h):
            mu = _lam_of(lg_ref, 1, 2 * p + h)
            hm = masks[h]
            hs = vcols(p, h)
            dv_ = _decay_vecs(mu, mu)
            v = v_ref[:, hs]
            k = k_ref[:, kcols(p)]
            q = q_ref[:, kcols(p)]
            dOb = do_ref[:, hs]
            km = jnp.where(hm, k, jnp.zeros_like(k)).astype(F32)
            Kb = (km * dv_["kb"]).astype(BF16)
            Qb = (jnp.where(hm, q, jnp.zeros_like(q)).astype(F32) * dv_["qb"]).astype(BF16)
            dSh = dS[p, h]
            dSb_ = dSh.astype(BF16)
            dKb = jnp.where(hm, _nt(v, dSb_), 0.0)
            acc_ref[p, h, 0:1, :] += _fsum(dKb * Kb.astype(F32) * dv_["i"])
            dv_ref[:, hs] = (dvin_ref[:, hs] + _nn(Kb, dSb_)).astype(BF16)
            acc_ref[p, h, 1:2, :] += float(CHUNK) * dv_["gb"] * _fsum(dSh * sb_ref[p, 0, h].astype(F32))
            dS[p, h] = dv_["gb"] * dSh + _tn(Qb, dOb)
            return dKb * dv_["kb"]

        for p in range(npairs):
            dk_ref[:, kcols(p)] = dkin_ref[:, kcols(p)] + head_main(p, 0) + head_main(p, 1)

        @pl.when(n == N - 1)
        def _():
            for p, h in heads:
                mu = _lam_of(lg_ref, 1, 2 * p + h)
                hm = masks[h]
                hs = vcols(p, h)
                dv_ = _decay_vecs(mu, mu)
                states = {}
                kws = {}
                s = jnp.zeros((LANES, LANES), F32)
                for cc in reversed(range(ncc)):
                    rows = slice(cc * CHUNK, (cc + 1) * CHUNK)
                    states[cc] = s
                    kw = jnp.where(hm, kc_ref[rows, kcols(p)].astype(F32) * dv_["kb"], 0.0).astype(BF16)
                    kws[cc] = kw
                    s = dv_["gb"] * s + _tn(kw, vc_ref[rows, hs])
                d = dS[p, h]
                for cc in range(ncc):
                    db = d.astype(BF16)
                    rows = slice(cc * CHUNK, (cc + 1) * CHUNK)
                    dKb_c = jnp.where(hm, _nt(vc_ref[rows, hs], db), 0.0)
                    part = dKb_c * dv_["kb"]
                    if h == 0:
                        dkc_ref[rows, kcols(p)] = dkcin_ref[rows, kcols(p)] + part
                    else:
                        dkc_ref[rows, kcols(p)] += part
                    acc_ref[p, h, 0:1, :] += _fsum(dKb_c * kws[cc].astype(F32) * dv_["i"])
                    dvc_ref[rows, hs] = dvcin_ref[rows, hs] + _nn(kws[cc], db)
                    acc_ref[p, h, 1:2, :] += float(CHUNK) * dv_["gb"] * _fsum(d * states[cc])
                    d = dv_["gb"] * d

    wq, wv = npairs * LANES, npairs * 2 * LANES
    st = pl.BlockSpec((npairs, 1, 2, LANES, LANES), lambda n: (0, n, 0, 0, 0))
    pair = pl.BlockSpec((CHUNK, wq), lambda n: (n, 0))
    wide = lambda grp: pl.BlockSpec((CHUNK, wv), lambda n: (n, grp))
    ckc = pl.BlockSpec((Lc, wq), lambda n: (0, 0))
    cvc = lambda grp: pl.BlockSpec((Lc, wv), lambda n: (0, grp))
    return pl.pallas_call(
        body, name="ret_bwd_asc", grid=(N,),
        in_specs=[pl.BlockSpec(memory_space=pltpu.SMEM), pair, pair, wide(rv_grp), st, wide(0), pair, wide(0),
                  ckc, cvc(rv_grp), ckc, cvc(0), pl.BlockSpec(memory_space=pl.ANY)],
        out_specs=[pair, wide(rv_grp), ckc, cvc(0),
                   pl.BlockSpec((npairs, 2, ACC_ROWS, LANES), lambda n: (0, 0, 0, 0))],
        out_shape=[jax.ShapeDtypeStruct(dKr.shape, F32),
                   jax.ShapeDtypeStruct(dP.shape, dP.dtype),
                   jax.ShapeDtypeStruct(dKc.shape, F32),
                   jax.ShapeDtypeStruct(dVc.shape, F32),
                   jax.ShapeDtypeStruct((npairs, 2, ACC_ROWS, LANES), F32)],
        input_output_aliases={12: 1},
        scratch_shapes=[pltpu.VMEM((npairs, 2, LANES, LANES), F32)],
        compiler_params=_params(1))(lg, Qr, Kr, P, SB, dO, dKr, dVp, Krc, Pc, dKc, dVc, dP)


GROUP = 4


def _att_band(Lc):
    row = np.arange(GROUP * CHUNK)[:, None] % CHUNK
    col = np.arange(3 * CHUNK + Lc)[None, :]
    ok = ((col >= row) & (col <= row + 2 * CHUNK)) | (col >= 3 * CHUNK)
    return jnp.asarray(np.where(ok, 0.0, NEG), F32)


def _att_edge(n, N, Lc):
    col = lax.broadcasted_iota(jnp.int32, (1, 3 * CHUNK + Lc), 1)
    off = jnp.logical_or(jnp.logical_and(col < CHUNK, n == 0),
                         jnp.logical_and(jnp.logical_and(col >= 2 * CHUNK, col < 3 * CHUNK), n == N - 1))
    return jnp.where(off, NEG, 0.0)


def _stack_heads(ref, gi):
    masks = _head_masks()
    tiles = []
    for pr in range(2):
        t = ref[:, (2 * gi + pr) * LANES:(2 * gi + pr + 1) * LANES]
        for a in range(2):
            tiles.append(jnp.where(masks[a], t, jnp.zeros_like(t)))
    return jnp.concatenate(tiles, axis=0)


def _unstack_heads(x4):
    m0 = _head_masks()[0]
    return [jnp.where(m0, x4[(2 * pr) * CHUNK:(2 * pr + 1) * CHUNK], x4[(2 * pr + 1) * CHUNK:(2 * pr + 2) * CHUNK])
            for pr in range(2)]


def _sink_column(sink_ref, g):
    row = lax.broadcasted_iota(jnp.int32, (GROUP * CHUNK, 1), 0) // CHUNK
    col = jnp.zeros((GROUP * CHUNK, 1), F32)
    for h in range(GROUP):
        col = jnp.where(row == h, sink_ref[0, g * GROUP + h], col)
    return col


def _att_probs(q4, Kall, bias, snk):
    s = _nt(q4, Kall) + bias
    mx = jnp.maximum(jnp.max(s, axis=1, keepdims=True), snk)
    p = jnp.exp(s - mx)
    p_snk = jnp.exp(snk - mx)
    inv = 1.0 / (jnp.sum(p, axis=1, keepdims=True) + p_snk)
    return p, p_snk, inv


def _att_groups_per_step(nkv, blk0):
    for gps in (4, 2):
        if nkv % gps == 0 and blk0 % gps == 0:
            return gps
    return 1


def _att_specs(Lc, N, gps):
    q = pl.BlockSpec((CHUNK, gps * 2 * LANES), lambda g, n: (n, g))
    kv = lambda s: pl.BlockSpec((gps, CHUNK, LANES), lambda g, n: (g, jnp.clip(n + s, 0, N - 1), 0))
    ctx = pl.BlockSpec((gps, Lc, LANES), lambda g, n: (g, 0, 0))
    return q, kv, ctx


def _att_fwd(Qa, Kd, Vd, Kdc, Vdc, sink, Y, blk0):
    L = Qa.shape[0]
    Lc = Kdc.shape[1]
    N = L // CHUNK
    nkv = Kd.shape[0]
    gps = _att_groups_per_step(nkv, blk0)

    def body(sink_ref, band_ref, q_ref, kp, kc_, kn, vp, vc_, vn, kctx, vctx, y_in, o_ref):
        del y_in
        g, n = pl.program_id(0), pl.program_id(1)
        bias = band_ref[...] + _att_edge(n, N, Lc)
        for gi in range(gps):
            Kall = jnp.concatenate([kp[gi], kc_[gi], kn[gi], kctx[gi]], axis=0)
            Vall = jnp.concatenate([vp[gi], vc_[gi], vn[gi], vctx[gi]], axis=0)
            p, _, inv = _att_probs(_stack_heads(q_ref, gi), Kall, bias, _sink_column(sink_ref, g * gps + gi))
            o4 = _nn(p.astype(BF16), Vall) * inv
            for pr, o in enumerate(_unstack_heads(o4)):
                o_ref[:, (2 * gi + pr) * LANES:(2 * gi + pr + 1) * LANES] = o.astype(BF16)

    q, kv, ctx = _att_specs(Lc, N, gps)
    band = pl.BlockSpec((GROUP * CHUNK, 3 * CHUNK + Lc), lambda g, n: (0, 0))
    return pl.pallas_call(
        body, name="att_fwd", grid=(nkv // gps, N),
        in_specs=[pl.BlockSpec(memory_space=pltpu.SMEM), band, q, kv(-1), kv(0), kv(1), kv(-1), kv(0), kv(1), ctx, ctx,
                  pl.BlockSpec(memory_space=pl.ANY)],
        out_specs=pl.BlockSpec((CHUNK, gps * 2 * LANES), lambda g, n: (n, blk0 // gps + g)),
        out_shape=jax.ShapeDtypeStruct(Y.shape, Y.dtype),
        input_output_aliases={11: 0},
        compiler_params=_params(2))(sink, _att_band(Lc), Qa, Kd, Kd, Kd, Vd, Vd, Vd, Kdc, Vdc, Y)


def _att_bwd(Qa, Kd, Vd, Kdc, Vdc, sink, dY, blk0):
    L = Qa.shape[0]
    Lc = Kdc.shape[1]
    N = L // CHUNK
    nkv = Kd.shape[0]
    gps = _att_groups_per_step(nkv, blk0)

    def body(sink_ref, band_ref, q_ref, kp, kc_, kn, vp, vc_, vn, kctx, vctx, dy_ref,
             dq_ref, dkp, dkc_, dkn, dvp, dvc_, dvn, dkctx, dvctx, dsink_ref):
        g, n = pl.program_id(0), pl.program_id(1)

        @pl.when(n == 0)
        def _():
            dkctx[...] = jnp.zeros_like(dkctx)
            dvctx[...] = jnp.zeros_like(dvctx)
            dsink_ref[...] = jnp.zeros_like(dsink_ref)

        bias = band_ref[...] + _att_edge(n, N, Lc)
        for gi in range(gps):
            Kall = jnp.concatenate([kp[gi], kc_[gi], kn[gi], kctx[gi]], axis=0)
            Vall = jnp.concatenate([vp[gi], vc_[gi], vn[gi], vctx[gi]], axis=0)
            q4 = _stack_heads(q_ref, gi)
            do4 = _stack_heads(dy_ref, gi)
            p, p_snk, inv = _att_probs(q4, Kall, bias, _sink_column(sink_ref, g * gps + gi))
            P = p * inv
            dp = _nt(do4, Vall)
            delta = jnp.sum(P * dp, axis=1, keepdims=True)
            ds = (P * (dp - delta)).astype(BF16)
            dsnk = -(p_snk * inv) * delta
            for h in range(GROUP):
                dsink_ref[gi, h:h + 1, :] += _fsum(dsnk[h * CHUNK:(h + 1) * CHUNK])
            for pr, dq in enumerate(_unstack_heads(_nn(ds, Kall))):
                dq_ref[:, (2 * gi + pr) * LANES:(2 * gi + pr + 1) * LANES] = dq
            dK = _tn(ds, q4)
            dV = _tn(P.astype(BF16), do4)
            for j, (rk, rv) in enumerate([(dkp, dvp), (dkc_, dvc_), (dkn, dvn)]):
                rk[gi] = dK[j * CHUNK:(j + 1) * CHUNK].astype(BF16)
                rv[gi] = dV[j * CHUNK:(j + 1) * CHUNK].astype(BF16)
            dkctx[gi] += dK[3 * CHUNK:]
            dvctx[gi] += dV[3 * CHUNK:]

    q, kv, ctx = _att_specs(Lc, N, gps)
    band = pl.BlockSpec((GROUP * CHUNK, 3 * CHUNK + Lc), lambda g, n: (0, 0))
    blk = pl.BlockSpec((gps, CHUNK, LANES), lambda g, n: (g, n, 0))
    part = jax.ShapeDtypeStruct((nkv, L, LANES), BF16)
    cshape = jax.ShapeDtypeStruct((nkv, Lc, LANES), F32)
    return pl.pallas_call(
        body, name="att_bwd", grid=(nkv // gps, N),
        in_specs=[pl.BlockSpec(memory_space=pltpu.SMEM), band, q, kv(-1), kv(0), kv(1), kv(-1), kv(0), kv(1), ctx, ctx,
                  pl.BlockSpec((CHUNK, gps * 2 * LANES), lambda g, n: (n, blk0 // gps + g))],
        out_specs=[q, blk, blk, blk, blk, blk, blk, ctx, ctx,
                   pl.BlockSpec((gps, 8, LANES), lambda g, n: (g, 0, 0))],
        out_shape=[jax.ShapeDtypeStruct(Qa.shape, F32), part, part, part, part, part, part, cshape, cshape,
                   jax.ShapeDtypeStruct((nkv, 8, LANES), F32)],
        compiler_params=_params(2))(sink, _att_band(Lc), Qa, Kd, Kd, Kd, Vd, Vd, Vd, Kdc, Vdc, dY)


def _scale_rows(dx, gt, saved, name):
    M, D = dx.shape
    tm = _tile(M, 512, 8)

    def body(dx_ref, gt_ref, sv_ref, dz_ref, dgt_ref):
        @pl.when(pl.program_id(0) == 0)
        def _():
            dgt_ref[...] = jnp.zeros_like(dgt_ref)
        d = dx_ref[...]
        dz_ref[...] = (d * gt_ref[...]).astype(BF16)
        dgt_ref[...] += jnp.sum(d * sv_ref[...].astype(F32), axis=0, keepdims=True)

    row = pl.BlockSpec((tm, D), lambda i: (i, 0))
    vec = pl.BlockSpec((1, D), lambda i: (0, 0))
    return pl.pallas_call(
        body, name=name, grid=(M // tm,), in_specs=[row, vec, row], out_specs=[row, vec],
        out_shape=[jax.ShapeDtypeStruct((M, D), BF16), jax.ShapeDtypeStruct((1, D), F32)],
        compiler_params=_params(1))(dx, gt, saved)


def _bwd_proj(dz, w, G=None, U=None, name="bwd_proj"):
    M, D = dz.shape
    N = w.shape[0]
    swiglu = G is not None
    tm, tn = _tile(M, ROWS_PER_LATCH, 8), _tile(N, 512)

    def body(*refs):
        if swiglu:
            dz_ref, w_ref, G_ref, U_ref, dG_ref, dU_ref = refs
        else:
            dz_ref, w_ref, dA_ref = refs
        dA = _nt(dz_ref[...], w_ref[...])
        if swiglu:
            Gv = G_ref[...].astype(F32)
            Uv = U_ref[...].astype(F32)
            sg = _sigmoid(Gv)
            dU_ref[...] = (dA * Gv * sg).astype(BF16)
            dG_ref[...] = (dA * Uv * (sg * (1.0 + Gv * (1.0 - sg)))).astype(BF16)
        else:
            dA_ref[...] = dA.astype(BF16)

    row = pl.BlockSpec((tm, D), lambda i, j: (i, 0))
    tile = pl.BlockSpec((tm, tn), lambda i, j: (i, j))
    big = jax.ShapeDtypeStruct((M, N), BF16)
    in_specs = [row, pl.BlockSpec((tn, D), lambda i, j: (j, 0))]
    args = [dz, w]
    if swiglu:
        in_specs += [tile, tile]
        args += [G, U]
        out_specs, out_shape = [tile, tile], [big, big]
    else:
        out_specs, out_shape = tile, big
    return pl.pallas_call(
        body, name=name, grid=(M // tm, N // tn), in_specs=in_specs, out_specs=out_specs, out_shape=out_shape,
        compiler_params=_params(2))(*args)


def _tn_matmul(pairs, name):
    Ka, Nb = pairs[0][0].shape[1], pairs[0][1].shape[1]
    tk, tn = _tile(Ka, 2048), _tile(Nb, 2048)
    tls, nks = [], []
    for a, _ in pairs:
        tl = _tile(a.shape[0], 1024, 8)
        tls.append(tl)
        nks.append(a.shape[0] // tl)
    starts = [int(s) for s in np.cumsum([0] + nks[:-1])]
    nk = int(sum(nks))

    def body(*refs):
        out_ref, acc = refs[-2], refs[-1]
        k = pl.program_id(2)

        @pl.when(k == 0)
        def _():
            acc[...] = jnp.zeros_like(acc)

        for idx in range(len(pairs)):
            a_ref, b_ref = refs[2 * idx], refs[2 * idx + 1]

            @pl.when(jnp.logical_and(k >= starts[idx], k < starts[idx] + nks[idx]))
            def _():
                acc[...] += _tn(a_ref[...], b_ref[...])

        @pl.when(k == nk - 1)
        def _():
            out_ref[...] = acc[...].astype(BF16)

    in_specs, args = [], []
    for idx, (a, b) in enumerate(pairs):
        s0, n_ = starts[idx], nks[idx]
        in_specs.append(pl.BlockSpec((tls[idx], tk), lambda i, j, k, s0=s0, n_=n_: (jnp.clip(k - s0, 0, n_ - 1), i)))
        in_specs.append(pl.BlockSpec((tls[idx], tn), lambda i, j, k, s0=s0, n_=n_: (jnp.clip(k - s0, 0, n_ - 1), j)))
        args += [a, b]
    return pl.pallas_call(
        body, name=name, grid=(Ka // tk, Nb // tn, nk), in_specs=in_specs,
        out_specs=pl.BlockSpec((tk, tn), lambda i, j, k: (i, j)),
        out_shape=jax.ShapeDtypeStruct((Ka, Nb), BF16),
        scratch_shapes=[pltpu.VMEM((tk, tn), F32)], compiler_params=_params(3))(*args)


def _bwd_norm_mod(pairs, x, dres, g, sh, sc, name):
    M, D = x.shape
    K = pairs[0][0].shape[1]
    tm, tk = _tile(M, 512, 8), _tile(K, 1152 if len(pairs) == 1 else 512)
    nk = K // tk
    npair = len(pairs)
    has_res = dres is not None

    def body(*refs):
        pr = refs[:2 * npair]
        rest = refs[2 * npair:]
        if has_res:
            x_ref, dres_ref, g_ref, sh_ref, sc_ref, dx_ref, st_ref, acc = rest
        else:
            x_ref, g_ref, sh_ref, sc_ref, dx_ref, st_ref, acc = rest
        del sh_ref
        i, k = pl.program_id(0), pl.program_id(1)

        @pl.when(jnp.logical_and(i == 0, k == 0))
        def _():
            st_ref[...] = jnp.zeros_like(st_ref)

        @pl.when(k == 0)
        def _():
            acc[...] = jnp.zeros_like(acc)

        t = _nt(pr[1][...], pr[0][...])
        for idx in range(1, npair):
            t = t + _nt(pr[2 * idx + 1][...], pr[2 * idx][...])
        acc[...] += t

        @pl.when(k == nk - 1)
        def _():
            xv = x_ref[...]
            gv = g_ref[...]
            dh = acc[...].T
            r = lax.rsqrt(jnp.mean(xv * xv, axis=-1, keepdims=True) + NORM_EPS)
            xh = xv * r
            st_ref[0:1, :] += jnp.sum(dh, axis=0, keepdims=True)
            st_ref[1:2, :] += jnp.sum(dh * (xh * gv), axis=0, keepdims=True)
            dn = dh * (1.0 + sc_ref[...])
            st_ref[2:3, :] += jnp.sum(dn * xh, axis=0, keepdims=True)
            dxh = dn * gv
            d = r * (dxh - xh * jnp.mean(dxh * xh, axis=-1, keepdims=True))
            if has_res:
                d = d + dres_ref[...]
            dx_ref[...] = d

    row = pl.BlockSpec((tm, D), lambda i, k: (i, 0))
    vec = pl.BlockSpec((1, D), lambda i, k: (0, 0))
    in_specs, args = [], []
    for dA, w in pairs:
        in_specs += [pl.BlockSpec((tm, tk), lambda i, k: (i, k)), pl.BlockSpec((D, tk), lambda i, k: (0, k))]
        args += [dA, w]
    in_specs += [row] + ([row] if has_res else []) + [vec, vec, vec]
    args += [x] + ([dres] if has_res else []) + [g, sh, sc]
    return pl.pallas_call(
        body, name=name, grid=(M // tm, nk), in_specs=in_specs,
        out_specs=[row, pl.BlockSpec((8, D), lambda i, k: (0, 0))],
        out_shape=[jax.ShapeDtypeStruct((M, D), F32), jax.ShapeDtypeStruct((8, D), F32)],
        scratch_shapes=[pltpu.VMEM((D, tm), F32)], compiler_params=_params(2))(*args)


def _local_step(x, ctx, tgt, mod, modc, norm_mix, norm_ffn, norm_final, lg, sink, w_in, rest_weights, on_grads, flush):
    L, D = x.shape
    Lc = ctx.shape[0]
    d_proj = w_in.shape[1]
    npairs = RET_HEADS // 2
    nkv = ATT_KV_HEADS
    nkvp = nkv // 2
    o_rq = 0
    o_rk = o_rq + RET_HEADS * RET_DK // LANES
    o_rv = o_rk + RET_HEADS * RET_DK // LANES
    o_rg = o_rv + RET_HEADS * RET_DV // LANES
    o_aq = o_rg + RET_HEADS * RET_DV // LANES
    o_ak = o_aq + ATT_HEADS * ATT_DH // LANES
    o_av = o_ak + nkv * ATT_DH // LANES
    assert (o_av + nkv * ATT_DH // LANES) * LANES == d_proj
    assert o_rv % 2 == 0 and o_rg % 2 == 0 and (RET_HEADS * RET_DV) % (2 * LANES) == 0
    rv_blk, rg_blk = o_rv // 2, o_rg // 2
    d_ret = RET_HEADS * RET_DV
    d_mix = d_ret + ATT_HEADS * ATT_DH
    att_blk = d_ret // (2 * LANES)
    k_scale = RET_DK ** -0.5
    a_scale = ATT_DH ** -0.5

    T = _rope_tables(L)
    Tc = dict(C=jnp.ones((Lc, LANES), F32), S=jnp.zeros((Lc, LANES), F32))
    row = lambda m, i: m[i:i + 1]
    sh_m, sc_m, gt_m, sh_f, sc_f, gt_f = [row(mod, i) for i in range(6)]
    sh_mc, sc_mc = row(modc, 0), row(modc, 1)

    P, hx = _norm_mod_matmul(x, norm_mix, sh_m, sc_m, w_in, "in_proj")
    Pc, hc = _norm_mod_matmul(ctx, norm_mix, sh_mc, sc_mc, w_in, "in_proj_ctx")
    nq = RET_HEADS * RET_DK // LANES
    Qr = _rope_cols(P, o_rq, nq, T["Cr"], T["Sr"], T["Rr"], 1.0, True, "rope_rq")
    Kr = _rope_cols(P, o_rk, nq, T["Cr"], T["Sr"], T["Rr"], k_scale, True, "rope_rk")
    Krc = _rope_cols(Pc, o_rk, nq, Tc["C"], Tc["S"], T["Rr"], k_scale, False, "scale_rk_ctx")
    Qa = _rope_cols(P, o_aq, ATT_HEADS * ATT_DH // LANES, T["Ca"], T["Sa"], T["Ra"], a_scale, True, "rope_aq")
    Kd = _dup_heads(P, o_ak, nkvp, T["Ca"], T["Sa"], T["Ra"], T["D0"], T["D1"], True, "dup_ak")
    Vd = _dup_heads(P, o_av, nkvp, T["Ca"], T["Sa"], T["Ra"], T["D0"], T["D1"], False, "dup_av")
    Kdc = _dup_heads(Pc, o_ak, nkvp, Tc["C"], Tc["S"], T["Ra"], T["D0"], T["D1"], False, "dup_ak_ctx")
    Vdc = _dup_heads(Pc, o_av, nkvp, Tc["C"], Tc["S"], T["Ra"], T["D0"], T["D1"], False, "dup_av_ctx")

    SF = _ret_states_fwd(Kr, P, Krc, Pc, lg, rv_blk, npairs)
    Y, SB = _ret_out_fwd(Qr, Kr, P, Krc, Pc, SF, lg, rv_blk, rg_blk, npairs, d_mix)
    Y = _att_fwd(Qa, Kd, Vd, Kdc, Vdc, sink, Y, att_blk)

    w_out, w_gate, w_up, w_down = rest_weights(Y)
    x1, O1 = _proj_residual(Y, w_out, x, gt_m, "out_proj")
    G, U, A, h2 = _ffn_in(x1, norm_ffn, sh_f, sc_f, w_gate, w_up)
    x2, Fo = _proj_residual(A, w_down, x1, gt_f, "ffn_out")
    dx2, loss, d_norm_final, dz2, dgt_f = _final(x2, norm_final, tgt, gt_f, Fo)

    dG, dU = _bwd_proj(dz2, w_down, G, U, name="ffn_out_bwd")
    g_w_down = _tn_matmul([(A, dz2)], "grad_w_down")
    tok = on_grads(["w_down"], [g_w_down])
    dx1, st_f = _bwd_norm_mod([(dG, w_gate), (dU, w_up)], x1, dx2, norm_ffn + tok, sh_f, sc_f, "ffn_in_bwd")
    tok = flush(dx1)
    g_w_gate = _tn_matmul([(h2, dG)], "grad_w_gate")
    g_w_up = _tn_matmul([(h2, dU)], "grad_w_up")
    tok = tok + on_grads(["w_gate", "w_up"], [g_w_gate, g_w_up])
    dz1, dgt_m = _scale_rows(dx1, gt_m + tok, O1, "mix_gate_bwd")
    dY = _bwd_proj(dz1, w_out, name="out_proj_bwd")
    tok = flush(dY)
    g_w_out = _tn_matmul([(Y, dz1)], "grad_w_out")
    tok = tok + on_grads(["w_out"], [g_w_out])

    dQa, dKp, dKs, dKn, dVp, dVs, dVn, dKdc, dVdc, dsink = _att_bwd(Qa, Kd, Vd, Kdc, Vdc, sink + tok, dY, att_blk)
    tok = flush(dQa)
    dQr, dKr, dVr, dP, dO, dKc, dVc, acc1 = _ret_bwd1(Qr, Kr, P, Krc, Pc, SF, SB, dY, lg + tok, rv_blk, rg_blk, npairs,
                                                      d_proj)
    dKr, dP, dKc, dVc, acc2 = _ret_bwd2(Qr, Kr, P, Krc, Pc, SB, dO, dKr, dVr, dP, dKc, dVc, lg, rv_blk, npairs)

    dP = _unrope_cols(dQr, dP, o_rq, nq, T["Cr"], T["Sr"], T["RrT"], 1.0, True, "unrope_rq")
    dP = _unrope_cols(dKr, dP, o_rk, nq, T["Cr"], T["Sr"], T["RrT"], k_scale, True, "unrope_rk")
    dP = _unrope_cols(dQa, dP, o_aq, ATT_HEADS * ATT_DH // LANES, T["Ca"], T["Sa"], T["RaT"], a_scale, True, "unrope_aq")
    dP = _fold_heads([(dKs, 0), (dKp, 1), (dKn, -1)], dP, o_ak, nkvp, T["Ca"], T["Sa"], T["RaT"], T["D0T"], T["D1T"],
                     True, "fold_ak")
    dP = _fold_heads([(dVs, 0), (dVp, 1), (dVn, -1)], dP, o_av, nkvp, T["Ca"], T["Sa"], T["RaT"], T["D0T"], T["D1T"],
                     False, "fold_av")
    dPc = jnp.zeros((Lc, d_proj), BF16)
    dPc = _unrope_cols(dKc, dPc, o_rk, nq, Tc["C"], Tc["S"], T["RrT"], k_scale, False, "ctx_rk_bwd")
    dPc = _unrope_cols(dVc, dPc, o_rv, RET_HEADS * RET_DV // LANES, Tc["C"], Tc["S"], T["RrT"], 1.0, False, "ctx_rv_bwd")
    dPc = _fold_heads([(dKdc.astype(BF16), 0)], dPc, o_ak, nkvp, Tc["C"], Tc["S"], T["RaT"], T["D0T"], T["D1T"],
                      False, "fold_ak_ctx")
    dPc = _fold_heads([(dVdc.astype(BF16), 0)], dPc, o_av, nkvp, Tc["C"], Tc["S"], T["RaT"], T["D0T"], T["D1T"],
                      False, "fold_av_ctx")

    dx, st_m = _bwd_norm_mod([(dP, w_in)], x, dx1, norm_mix, sh_m, sc_m, "in_proj_bwd")
    _, st_mc = _bwd_norm_mod([(dPc, w_in)], ctx, None, norm_mix, sh_mc, sc_mc, "in_proj_ctx_bwd")
    g_w_in = _tn_matmul([(hx, dP), (hc, dPc)], "grad_w_in")
    on_grads(["w_in"], [g_w_in])

    a1 = acc1[:, :, :, 0].reshape(RET_HEADS, ACC_ROWS)
    a2 = acc2[:, :, :, 0].reshape(RET_HEADS, ACC_ROWS)
    dlam = (a1[:, 0] + a1[:, 2] + a1[:, 3] + a1[:, 4]) * lg[0]
    dmu = (a1[:, 1] + a1[:, 5] + a2[:, 0] + a2[:, 1]) * lg[1]
    d_sink = dsink[:, :4, 0].reshape(1, ATT_HEADS)

    nh = RET_HEADS
    assert 2 * nh + ATT_HEADS <= LOSS_LANE
    small = _pack_rows(
        [(st_m, 0, 2, 0, 0), (dgt_m, 0, 1, 2, 0), (st_f, 0, 2, 3, 0), (dgt_f, 0, 1, 5, 0), (st_mc, 0, 2, 6, 0),
         (st_m[2:3] + st_mc[2:3], 0, 1, 12, 0), (st_f, 2, 1, 13, 0), (d_norm_final, 0, 1, 14, 0),
         (dlam.reshape(1, nh), 0, 1, 15, 0), (dmu.reshape(1, nh), 0, 1, 15, nh), (d_sink, 0, 1, 15, 2 * nh),
         (loss[:, 0:1], 0, 1, 15, LOSS_LANE)], 16, D, "pack_small")
    return dict(grad_x=dx, small=small)


def _my_pos():
    return lax.axis_index("x"), lax.axis_index("y"), lax.axis_index("c")


def _other_chips(x, y):
    return [(1 - x, y), (x, 1 - y), (1 - x, 1 - y)]


def _remote(src, dst, ssem, rsem, dev):
    return pltpu.make_async_remote_copy(src_ref=src, dst_ref=dst, send_sem=ssem, recv_sem=rsem,
                                        device_id=dev, device_id_type=MESH)


def _allgather8(v, name):
    R, Cc = v.shape

    def body(v_ref, out_ref, send_sems, recv_sems):
        x, y, c = _my_pos()
        me = 4 * x + 2 * y + c
        out_ref[pl.ds(me, 1)] = v_ref[...][None]
        peers = []
        for j in range(1, N_DEV):
            peers.append((1 - x if (j >> 2) & 1 else x, 1 - y if (j >> 1) & 1 else y, 1 - c if j & 1 else c))
        copies = []
        for j, peer in enumerate(peers):
            cp = _remote(v_ref, out_ref.at[me], send_sems.at[j], recv_sems.at[j], peer)
            cp.start()
            copies.append(cp)
        for j, peer in enumerate(peers):
            pid = 4 * peer[0] + 2 * peer[1] + peer[2]
            _remote(v_ref, out_ref.at[pid], send_sems.at[j], recv_sems.at[j], peer).wait_recv()
        for cp in copies:
            cp.wait_send()

    return pl.pallas_call(
        body, name=name, out_shape=jax.ShapeDtypeStruct((N_DEV, R, Cc), v.dtype),
        in_specs=[pl.BlockSpec(memory_space=pltpu.VMEM)], out_specs=pl.BlockSpec(memory_space=pltpu.VMEM),
        scratch_shapes=[pltpu.SemaphoreType.DMA((N_DEV - 1,)), pltpu.SemaphoreType.DMA((N_DEV - 1,))])(v)


def _region(ref, k, half, shard_shape, axis):
    r, cs = shard_shape
    hr = r // 2
    if axis == 1:
        return ref.at[pl.ds(pl.multiple_of(half * hr, 16), hr), pl.ds(pl.multiple_of(k * cs, LANES), cs)]
    return ref.at[pl.ds(pl.multiple_of(k * r + half * hr, 16), hr), :]


def _full_shape(shard_shape, axis):
    r, cs = shard_shape
    return (r, N_CHIPS * cs) if axis == 1 else (N_CHIPS * r, cs)


def _half_pieces(ref, half, shard_shape, axis):
    r, cs = shard_shape
    hr = r // 2
    if axis == 1:
        return [ref.at[pl.ds(pl.multiple_of(half * hr, 16), hr), :]]
    return [ref.at[pl.ds(pl.multiple_of(k * r + half * hr, 16), hr), :] for k in range(N_CHIPS)]


def _half_block_spec(shard_shape, axis, tr):
    r, cs = shard_shape
    hr = r // 2
    if axis == 1:
        return pl.BlockSpec((tr, cs), lambda k, i, c_ref: (c_ref[0] * (hr // tr) + i, k))
    return pl.BlockSpec((tr, cs), lambda k, i, c_ref: (k * (r // tr) + c_ref[0] * (hr // tr) + i, 0))


def _add_halves(g, recv, cvec, shard_shape, axis, name):
    r, cs = shard_shape
    hr = r // 2
    tr = _tile(hr, 256, 16)

    def body(c_ref, a_ref, b_ref, o_ref):
        del c_ref
        o_ref[0] = (a_ref[...].astype(F32) + b_ref[...].astype(F32)).astype(BF16)

    spec = _half_block_spec(shard_shape, axis, tr)
    return pl.pallas_call(
        body, name=name,
        grid_spec=pltpu.PrefetchScalarGridSpec(
            num_scalar_prefetch=1, grid=(N_CHIPS, hr // tr), in_specs=[spec, spec],
            out_specs=pl.BlockSpec((1, tr, cs), lambda k, i, c_ref: (k, i, 0))),
        out_shape=jax.ShapeDtypeStruct((N_CHIPS, hr, cs), BF16),
        compiler_params=_params(2, False))(cvec, g, recv)


def _sum_chips(sums, landed, kc, name):
    _, hr, cs = sums.shape
    tr = _tile(hr, 256, 16)

    def body(kc_ref, own_ref, a_ref, b_ref, c_ref, o_ref):
        del kc_ref
        o_ref[...] = (own_ref[0].astype(F32) + a_ref[0].astype(F32)) + (b_ref[0].astype(F32) + c_ref[0].astype(F32))

    slot = lambda j: pl.BlockSpec((1, tr, cs), lambda i, kc_ref: ((kc_ref[0] + j) % N_CHIPS, i, 0))
    return pl.pallas_call(
        body, name=name,
        grid_spec=pltpu.PrefetchScalarGridSpec(
            num_scalar_prefetch=1, grid=(hr // tr,), in_specs=[slot(0), slot(1), slot(2), slot(3)],
            out_specs=pl.BlockSpec((tr, cs), lambda i, kc_ref: (kc_ref[1] * (hr // tr) + i, 0))),
        out_shape=jax.ShapeDtypeStruct((2 * hr, cs), F32),
        compiler_params=_params(1, False))(kc, sums, landed, landed, landed)


def _exchange_halves(shards, name):
    nw = len(shards)

    def body(*refs):
        out_refs = refs[nw:2 * nw]
        send, recv = refs[2 * nw:]
        x, y, c = _my_pos()
        sib = (x, y, 1 - c)
        copies = []
        for w in range(nw):
            hr = shards[w].shape[0] // 2
            mine = out_refs[w].at[pl.ds(pl.multiple_of(c * hr, 8), hr), :]
            cp = _remote(mine, mine, send.at[w], recv.at[w], sib)
            cp.start()
            copies.append(cp)
        for w in range(nw):
            hr = shards[w].shape[0] // 2
            other = out_refs[w].at[pl.ds(pl.multiple_of((1 - c) * hr, 8), hr), :]
            _remote(other, other, send.at[w], recv.at[w], sib).wait_recv()
        for cp in copies:
            cp.wait_send()

    anyspec = pl.BlockSpec(memory_space=pl.ANY)
    return pl.pallas_call(
        body, name=name,
        out_shape=[jax.ShapeDtypeStruct(s.shape, F32) for s in shards],
        in_specs=[anyspec] * nw, out_specs=[anyspec] * nw,
        input_output_aliases={w: w for w in range(nw)},
        scratch_shapes=[pltpu.SemaphoreType.DMA((nw,)), pltpu.SemaphoreType.DMA((nw,))])(*shards)


def _cast_into_full(w, kc, axis, name):
    r, cs = w.shape
    tr = _tile(r, 256, 16)

    def body(kc_ref, w_ref, o_ref):
        del kc_ref
        o_ref[...] = w_ref[...].astype(BF16)

    if axis == 1:
        ospec = pl.BlockSpec((tr, cs), lambda i, kc_ref: (i, kc_ref[0]))
    else:
        ospec = pl.BlockSpec((tr, cs), lambda i, kc_ref: (kc_ref[0] * (r // tr) + i, 0))
    return pl.pallas_call(
        body, name=name,
        grid_spec=pltpu.PrefetchScalarGridSpec(
            num_scalar_prefetch=1, grid=(r // tr,), in_specs=[pl.BlockSpec((tr, cs), lambda i, kc_ref: (i, 0))],
            out_specs=ospec),
        out_shape=jax.ShapeDtypeStruct(_full_shape((r, cs), axis), BF16),
        compiler_params=_params(1, False))(kc, w)


def _adam_math(w, g, m, v):
    m2 = ADAM_B1 * m + (1.0 - ADAM_B1) * g
    v2 = ADAM_B2 * v + (1.0 - ADAM_B2) * (g * g)
    m_hat = m2 / (1.0 - ADAM_B1 ** ADAM_STEP)
    v_hat = v2 / (1.0 - ADAM_B2 ** ADAM_STEP)
    delta = -ADAM_LR * (m_hat / (jnp.sqrt(v_hat) + ADAM_EPS) + ADAM_WD * w)
    return delta, m2, v2


def _adam(w, g, m, v, name):
    r, cs = w.shape
    tr = _tile(r, 256, 8)

    def body(w_ref, g_ref, m_ref, v_ref, d_ref, m2_ref, v2_ref):
        d, m2, v2 = _adam_math(w_ref[...], g_ref[...], m_ref[...], v_ref[...])
        d_ref[...] = d
        m2_ref[...] = m2
        v2_ref[...] = v2

    spec = pl.BlockSpec((tr, cs), lambda i: (i, 0))
    shp = jax.ShapeDtypeStruct((r, cs), F32)
    return pl.pallas_call(body, name=name, grid=(r // tr,), in_specs=[spec] * 4, out_specs=[spec] * 3,
                          out_shape=[shp, shp, shp], compiler_params=_params(1, False))(w, g, m, v)


def _mod_rows(a16, w, b, name):
    D, n = w.shape
    tn = _tile(n, 512)

    def body(a_ref, w_ref, b_ref, o_ref):
        a = a_ref[...]
        o_ref[...] = _nn((a * _sigmoid(a)).astype(BF16), w_ref[...].astype(BF16)) + b_ref[...]

    return pl.pallas_call(
        body, name=name, grid=(n // tn,),
        in_specs=[pl.BlockSpec((16, D), lambda j: (0, 0)), pl.BlockSpec((D, tn), lambda j: (0, j)),
                  pl.BlockSpec((1, tn), lambda j: (0, j))],
        out_specs=pl.BlockSpec((16, tn), lambda j: (0, j)),
        out_shape=jax.ShapeDtypeStruct((16, n), F32), compiler_params=_params(1, False))(a16, w, b)


def _w_mod_update(a16, d16, w, m, v):
    D, n = w.shape
    tn = _tile(n, 256)

    def body(a_ref, d_ref, w_ref, m_ref, v_ref, g_ref, dl_ref, m2_ref, v2_ref, p_ref):
        @pl.when(pl.program_id(0) == 0)
        def _():
            p_ref[...] = jnp.zeros_like(p_ref)
        a = a_ref[...]
        db = d_ref[...].astype(BF16)
        wv = w_ref[...]
        g = _tn((a * _sigmoid(a)).astype(BF16), db)
        g_ref[...] = g
        d, m2, v2 = _adam_math(wv, g, m_ref[...], v_ref[...])
        dl_ref[...] = d
        m2_ref[...] = m2
        v2_ref[...] = v2
        p_ref[...] += _nt(db, wv.astype(BF16))

    wspec = pl.BlockSpec((D, tn), lambda j: (0, j))
    shp = jax.ShapeDtypeStruct((D, n), F32)
    return pl.pallas_call(
        body, name="w_mod_update", grid=(n // tn,),
        in_specs=[pl.BlockSpec((16, D), lambda j: (0, 0)), pl.BlockSpec((16, tn), lambda j: (0, j)), wspec, wspec, wspec],
        out_specs=[wspec, wspec, wspec, wspec, pl.BlockSpec((16, D), lambda j: (0, 0))],
        out_shape=[shp, shp, shp, shp, jax.ShapeDtypeStruct((16, D), F32)],
        compiler_params=_params(1))(a16, d16, w, m, v)


def _sum_devices(g8, name):
    _, R, Cc = g8.shape

    def body(g_ref, o_ref):
        t = g_ref[0]
        for d in range(1, N_DEV):
            t = t + g_ref[d]
        o_ref[...] = t

    return pl.pallas_call(body, name=name, out_shape=jax.ShapeDtypeStruct((R, Cc), F32))(g8)


def _c_ctx_grad(parts, c_ctx):
    D = c_ctx.shape[1]

    def body(p_ref, c_ref, o_ref):
        t = p_ref[0]
        for k in range(1, N_CHIPS):
            t = t + p_ref[2 * k]
        cv = c_ref[...]
        sg = _sigmoid(cv)
        o_ref[...] = t * (sg * (1.0 + cv * (1.0 - sg)))

    return pl.pallas_call(body, name="c_ctx_grad", out_shape=jax.ShapeDtypeStruct((1, D), F32))(parts, c_ctx)


def _pack_rows(items, nrows, width, name):
    arrays, plan = [], []
    for a, r0, nr, d0, c0 in items:
        for ai, b in enumerate(arrays):
            if b is a:
                break
        else:
            ai = len(arrays)
            arrays.append(a)
        plan.append((ai, r0, nr, d0, c0, a.shape[1]))

    def body(*refs):
        o_ref = refs[-1]
        o_ref[...] = jnp.zeros_like(o_ref)
        for ai, r0, nr, d0, c0, w in plan:
            o_ref[d0:d0 + nr, c0:c0 + w] = refs[ai][r0:r0 + nr, :]

    return pl.pallas_call(body, name=name, out_shape=jax.ShapeDtypeStruct((nrows, width), F32))(*arrays)


HBM_SPEC = pl.BlockSpec(memory_space=pltpu.HBM)
SEM_SPEC = pl.BlockSpec(memory_space=pltpu.SEMAPHORE)
SPLIT_PARAMS = pltpu.CompilerParams(has_side_effects=pltpu.SideEffectType.DATAFLOW_SIDE_EFFECTING)


def _in_hbm(a):
    return pltpu.with_memory_space_constraint(a, pltpu.HBM)


def _ag_chips_start(fulls, shapes, axes, after, name):
    nw = len(fulls)

    def body(*refs):
        in_refs, send, recv, token = refs[:nw], refs[nw + 1], refs[nw + 2], refs[-1]
        x, y, c = _my_pos()
        k0 = 2 * x + y
        for w in range(nw):
            own = _region(in_refs[w], k0, c, shapes[w], axes[w])
            for j, ch in enumerate(_other_chips(x, y)):
                _remote(own, own, send.at[3 * w + j], recv.at[3 * w + j], (ch[0], ch[1], c)).start()
        token[...] = jnp.zeros_like(token)

    return pl.pallas_call(
        body, name=name,
        out_shape=(pltpu.SemaphoreType.DMA((3 * nw,)), pltpu.SemaphoreType.DMA((3 * nw,)),
                   *[pltpu.HBM(f.shape, f.dtype) for f in fulls], jax.ShapeDtypeStruct((8, LANES), F32)),
        in_specs=[HBM_SPEC] * nw + [pl.BlockSpec(memory_space=pl.ANY)],
        out_specs=(SEM_SPEC, SEM_SPEC, *[HBM_SPEC] * nw, pl.BlockSpec(memory_space=pltpu.VMEM)),
        input_output_aliases={w: 2 + w for w in range(nw)},
        compiler_params=SPLIT_PARAMS)(*[_in_hbm(f) for f in fulls], after)


def _ag_chips_wait(send, recv, fulls, shapes, axes, after, name):
    nw = len(fulls)

    def body(*refs):
        in_refs, send_ref, recv_ref = refs[:nw], refs[nw], refs[nw + 1]
        x, y, c = _my_pos()
        k0 = 2 * x + y
        for w in range(nw):
            own = _region(in_refs[w], k0, c, shapes[w], axes[w])
            for j, ch in enumerate(_other_chips(x, y)):
                got = _region(in_refs[w], 2 * ch[0] + ch[1], c, shapes[w], axes[w])
                cp = _remote(own, got, send_ref.at[3 * w + j], recv_ref.at[3 * w + j], (ch[0], ch[1], c))
                cp.wait_send()
                cp.wait_recv()

    return pl.pallas_call(
        body, name=name,
        out_shape=tuple(pltpu.HBM(f.shape, f.dtype) for f in fulls),
        in_specs=[HBM_SPEC] * nw + [SEM_SPEC, SEM_SPEC, pl.BlockSpec(memory_space=pl.ANY)],
        out_specs=tuple([HBM_SPEC] * nw),
        input_output_aliases={w: w for w in range(nw)},
        compiler_params=SPLIT_PARAMS)(*fulls, send, recv, after)


def _ag_forward(fulls, shapes, axes, name):
    nw = len(fulls)

    def body(*refs):
        out_refs = refs[nw:2 * nw]
        send, recv = refs[2 * nw:]
        x, y, c = _my_pos()
        sib = (x, y, 1 - c)
        chips = _other_chips(x, y)
        copies = []
        for w in range(nw):
            for j, ch in enumerate(chips):
                got = _region(out_refs[w], 2 * ch[0] + ch[1], c, shapes[w], axes[w])
                cp = _remote(got, got, send.at[w, j], recv.at[w, j], sib)
                cp.start()
                copies.append(cp)
        for w in range(nw):
            for j, ch in enumerate(chips):
                got = _region(out_refs[w], 2 * ch[0] + ch[1], 1 - c, shapes[w], axes[w])
                _remote(got, got, send.at[w, j], recv.at[w, j], sib).wait_recv()
        for cp in copies:
            cp.wait_send()

    anyspec = pl.BlockSpec(memory_space=pl.ANY)
    return pl.pallas_call(
        body, name=name,
        out_shape=[jax.ShapeDtypeStruct(f.shape, BF16) for f in fulls],
        in_specs=[anyspec] * nw, out_specs=[anyspec] * nw,
        input_output_aliases={w: w for w in range(nw)},
        scratch_shapes=[pltpu.SemaphoreType.DMA((nw, 3)), pltpu.SemaphoreType.DMA((nw, 3))])(*fulls)


def _rs_sibling_start(grads, shapes, axes, name):
    nw = len(grads)
    npc = max(1 if a == 1 else N_CHIPS for a in axes)

    def body(*refs):
        g_refs, l_refs, send, recv, token = refs[:nw], refs[nw:2 * nw], refs[2 * nw], refs[2 * nw + 1], refs[-1]
        x, y, c = _my_pos()
        for w in range(nw):
            src = _half_pieces(g_refs[w], 1 - c, shapes[w], axes[w])
            dst = _half_pieces(l_refs[w], 1 - c, shapes[w], axes[w])
            for i, (s, d) in enumerate(zip(src, dst)):
                _remote(s, d, send.at[npc * w + i], recv.at[npc * w + i], (x, y, 1 - c)).start()
        token[...] = jnp.zeros_like(token)

    thru = [pltpu.HBM(g.shape, g.dtype) for g in grads]
    return pl.pallas_call(
        body, name=name,
        out_shape=(pltpu.SemaphoreType.DMA((npc * nw,)), pltpu.SemaphoreType.DMA((npc * nw,)), *thru, *thru,
                   jax.ShapeDtypeStruct((8, LANES), F32)),
        in_specs=[HBM_SPEC] * (2 * nw),
        out_specs=(SEM_SPEC, SEM_SPEC, *[HBM_SPEC] * (2 * nw), pl.BlockSpec(memory_space=pltpu.VMEM)),
        input_output_aliases={i: 2 + i for i in range(2 * nw)},
        compiler_params=SPLIT_PARAMS)(*[_in_hbm(g) for g in grads], *[_in_hbm(lax.empty(g.shape, g.dtype)) for g in grads])


def _rs_sibling_wait(send, recv, grads, lands, shapes, axes, after, name):
    nw = len(grads)
    npc = max(1 if a == 1 else N_CHIPS for a in axes)

    def body(*refs):
        g_refs, l_refs, send_ref, recv_ref = refs[:nw], refs[nw:2 * nw], refs[2 * nw], refs[2 * nw + 1]
        x, y, c = _my_pos()
        for w in range(nw):
            sent = _half_pieces(g_refs[w], 1 - c, shapes[w], axes[w])
            mine = _half_pieces(l_refs[w], c, shapes[w], axes[w])
            for i, (s, d) in enumerate(zip(sent, mine)):
                cp = _remote(s, d, send_ref.at[npc * w + i], recv_ref.at[npc * w + i], (x, y, 1 - c))
                cp.wait_send()
                cp.wait_recv()

    thru = tuple(pltpu.HBM(g.shape, g.dtype) for g in grads)
    return pl.pallas_call(
        body, name=name, out_shape=thru + thru,
        in_specs=[HBM_SPEC] * (2 * nw) + [SEM_SPEC, SEM_SPEC, pl.BlockSpec(memory_space=pl.ANY)],
        out_specs=tuple([HBM_SPEC] * (2 * nw)),
        input_output_aliases={i: i for i in range(2 * nw)},
        compiler_params=SPLIT_PARAMS)(*grads, *lands, send, recv, after)


def _rs_chips_start(sums, name):
    nw = len(sums)

    def body(*refs):
        s_refs, l_refs, send, recv, token = refs[:nw], refs[nw:2 * nw], refs[2 * nw], refs[2 * nw + 1], refs[-1]
        x, y, c = _my_pos()
        k0 = 2 * x + y
        for w in range(nw):
            for j, ch in enumerate(_other_chips(x, y)):
                _remote(s_refs[w].at[2 * ch[0] + ch[1]], l_refs[w].at[k0], send.at[3 * w + j], recv.at[3 * w + j],
                        (ch[0], ch[1], c)).start()
        token[...] = jnp.zeros_like(token)

    thru = [pltpu.HBM(s.shape, s.dtype) for s in sums]
    return pl.pallas_call(
        body, name=name,
        out_shape=(pltpu.SemaphoreType.DMA((3 * nw,)), pltpu.SemaphoreType.DMA((3 * nw,)), *thru, *thru,
                   jax.ShapeDtypeStruct((8, LANES), F32)),
        in_specs=[HBM_SPEC] * (2 * nw),
        out_specs=(SEM_SPEC, SEM_SPEC, *[HBM_SPEC] * (2 * nw), pl.BlockSpec(memory_space=pltpu.VMEM)),
        input_output_aliases={i: 2 + i for i in range(2 * nw)},
        compiler_params=SPLIT_PARAMS)(*[_in_hbm(s) for s in sums], *[_in_hbm(lax.empty(s.shape, s.dtype)) for s in sums])


def _rs_chips_wait(send, recv, sums, lands, after, name):
    nw = len(sums)

    def body(*refs):
        s_refs, l_refs, send_ref, recv_ref = refs[:nw], refs[nw:2 * nw], refs[2 * nw], refs[2 * nw + 1]
        x, y, c = _my_pos()
        for w in range(nw):
            for j, ch in enumerate(_other_chips(x, y)):
                kj = 2 * ch[0] + ch[1]
                cp = _remote(s_refs[w].at[kj], l_refs[w].at[kj], send_ref.at[3 * w + j], recv_ref.at[3 * w + j],
                             (ch[0], ch[1], c))
                cp.wait_send()
                cp.wait_recv()

    thru = tuple(pltpu.HBM(s.shape, s.dtype) for s in sums)
    return pl.pallas_call(
        body, name=name, out_shape=thru + thru,
        in_specs=[HBM_SPEC] * (2 * nw) + [SEM_SPEC, SEM_SPEC, pl.BlockSpec(memory_space=pl.ANY)],
        out_specs=tuple([HBM_SPEC] * (2 * nw)),
        input_output_aliases={i: i for i in range(2 * nw)},
        compiler_params=SPLIT_PARAMS)(*sums, *lands, send, recv, after)


LOSS_LANE = 64


def kernel(x, c, ctx, c_ctx, w_mod, b_mod, norm_mix, norm_ffn, w_in, ret_decay, attn_sink, w_out, w_gate, w_up, w_down, norm_final, loss_target, m_c_ctx, m_w_mod, m_b_mod, m_norm_mix, m_norm_ffn, m_w_in, m_ret_decay, m_attn_sink, m_w_out, m_w_gate, m_w_up, m_w_down, m_norm_final, v_c_ctx, v_w_mod, v_b_mod, v_norm_mix, v_norm_ffn, v_w_in, v_ret_decay, v_attn_sink, v_w_out, v_w_gate, v_w_up, v_w_down, v_norm_final):
    D = x.shape[-1]
    n3 = w_mod.shape[-1]
    xi, yi, ci = _my_pos()
    b = 4 * xi + 2 * yi + ci
    k0 = 2 * xi + yi
    cvec = jnp.reshape(ci, (1,)).astype(jnp.int32)
    kc = jnp.stack([k0, ci]).astype(jnp.int32)

    dense = [("w_in", w_in[0], 1), ("w_out", w_out[0], 0), ("w_gate", w_gate[0], 1), ("w_up", w_up[0], 1),
             ("w_down", w_down[0], 0)]
    axes = [a for _, _, a in dense]
    shapes = [w.shape for _, w, _ in dense]
    c_all = _allgather8(c, "gather_c").reshape(N_DEV, D)
    c_ctx2 = c_ctx.reshape(1, D)
    a16 = _pack_rows([(c_all, 0, N_DEV, 0, 0), (c_ctx2, 0, 1, N_DEV, 0)], 16, D, "pack_cond")
    b_cols = lax.dynamic_slice_in_dim(b_mod, k0 * n3, n3, axis=1)
    mod16 = _mod_rows(a16, w_mod[0], b_cols, "mod_rows")
    mod_all = _allgather8(mod16, "gather_mod")

    own_in = _cast_into_full(dense[0][1], kc, axes[0], "cast_w_in")
    agi = _ag_chips_start([own_in], shapes[:1], axes[:1], mod_all, "ag_in_start")
    own16 = [_cast_into_full(w, kc, a, "cast_" + n) for n, w, a in dense[1:]]
    (f_in,) = _ag_forward(list(_ag_chips_wait(agi[0], agi[1], [agi[2]], shapes[:1], axes[:1], own16[-1], "ag_in_wait")),
                          shapes[:1], axes[:1], "ag_in_forward")
    ag = _ag_chips_start(own16, shapes[1:], axes[1:], f_in, "ag_rest_start")
    ag_send, ag_recv, ag_thru, ag_tok = ag[0], ag[1], list(ag[2:-1]), ag[-1][0:1, 0:1]

    def rest_weights(after):
        landed_w = _ag_chips_wait(ag_send, ag_recv, ag_thru, shapes[1:], axes[1:], after, "ag_rest_wait")
        return _ag_forward(list(landed_w), shapes[1:], axes[1:], "ag_rest_forward")
    mine = jnp.stack([lax.dynamic_index_in_dim(mod_all, 2 * k + ci, 0, keepdims=False) for k in range(N_CHIPS)])
    mod = lax.dynamic_index_in_dim(mine, b, 1, keepdims=False).reshape(6, D)
    modc = mine[:, N_DEV].reshape(6, D)

    lg = -jnp.exp(ret_decay[0])

    index = {n: i for i, (n, _, _) in enumerate(dense)}
    pending, done = [], {}

    sib = []

    def finish_sibling(after):
        names, shp, axs, st = sib.pop()
        nw = len(names)
        res = _rs_sibling_wait(st[0], st[1], list(st[2:2 + nw]), list(st[2 + nw:2 + 2 * nw]), shp, axs, after,
                               "rs_sibling_wait_" + names[0])
        sums = [_add_halves(res[i], res[nw + i], cvec, s, a, "add_halves_" + n)
                for i, (s, a, n) in enumerate(zip(shp, axs, names))]
        ch = _rs_chips_start(sums, "rs_chips_start_" + names[0])
        pending.append((names, ch[0], ch[1], list(ch[2:2 + nw]), list(ch[2 + nw:2 + 2 * nw])))
        return ch[-1][0:1, 0:1]

    def on_grads(names, gs):
        ids = [index[n] for n in names]
        shp, axs = [shapes[i] for i in ids], [axes[i] for i in ids]
        st = _rs_sibling_start(gs, shp, axs, "rs_sibling_start_" + names[0])
        sib.append((names, shp, axs, st))
        return st[-1][0:1, 0:1]

    out = _local_step(x[0], ctx[0], loss_target[0], mod, modc, norm_mix + ag_tok, norm_ffn, norm_final.reshape(1, D), lg,
                      attn_sink, f_in, rest_weights, on_grads, finish_sibling)

    def finish(group, after):
        names, send, recv, sums, lands = group
        res = _rs_chips_wait(send, recv, sums, lands, after, "rs_chips_wait_" + names[0])
        return [_sum_chips(res[i], res[len(names) + i], kc, "sum_chips_" + n) for i, n in enumerate(names)]

    tok_in = finish_sibling(out["grad_x"])
    assert pending[-1][0] == ["w_in"]
    rest_names = [n for g in pending[:-1] for n in g[0]]
    after_in = out["small"][0:8, 0:LANES] + tok_in
    rest_halves = [h for g in pending[:-1] for h in finish(g, after_in)]
    g_rest = dict(zip(rest_names, _exchange_halves(rest_halves, "exchange_halves_rest")))

    nh = 2 * RET_HEADS
    small_all = _allgather8(out["small"], "gather_small")
    tot = _sum_devices(small_all, "sum_small")
    g_b_mod = (tot[0:6] + tot[6:12]).reshape(1, 6 * D)
    dmodc_tot = tot[6:12].reshape(1, 6 * D)
    dmod_rows = small_all[:, 0:6].reshape(N_DEV, 6 * D)
    d16 = _pack_rows([(dmod_rows, 0, N_DEV, 0, 0), (dmodc_tot, 0, 1, N_DEV, 0)], 16, 6 * D, "pack_dmod")
    d16 = lax.dynamic_slice_in_dim(d16, k0 * n3, n3, axis=1)
    g_w_mod, dl_w_mod, m2_w_mod, v2_w_mod, part = _w_mod_update(a16, d16, w_mod[0], m_w_mod[0], v_w_mod[0])
    part_all = _allgather8(part[N_DEV:N_DEV + 1], "gather_c_ctx")
    g_c_ctx = _c_ctx_grad(part_all, c_ctx2)
    loss = tot[15, LOSS_LANE]

    def pack(cc, bm, nm, nf, nfin, rd, sk, name):
        rd2 = rd.reshape(2, RET_HEADS)
        return _pack_rows([(bm.reshape(6, D), 0, 6, 0, 0), (cc.reshape(1, D), 0, 1, 6, 0), (nm.reshape(1, D), 0, 1, 7, 0),
                           (nf.reshape(1, D), 0, 1, 8, 0), (nfin.reshape(1, D), 0, 1, 9, 0),
                           (rd2, 0, 1, 10, 0), (rd2, 1, 1, 10, RET_HEADS), (sk.reshape(1, ATT_HEADS), 0, 1, 10, nh)],
                          16, D, name)

    w_s = pack(c_ctx, b_mod, norm_mix, norm_ffn, norm_final, ret_decay, attn_sink, "pack_w")
    g_s = _pack_rows([(g_b_mod.reshape(6, D), 0, 6, 0, 0), (g_c_ctx, 0, 1, 6, 0), (tot, 12, 3, 7, 0),
                      (tot[15:16, 0:nh + ATT_HEADS], 0, 1, 10, 0)], 16, D, "pack_g")
    m_s = pack(m_c_ctx, m_b_mod, m_norm_mix, m_norm_ffn, m_norm_final, m_ret_decay, m_attn_sink, "pack_m")
    v_s = pack(v_c_ctx, v_b_mod, v_norm_mix, v_norm_ffn, v_norm_final, v_ret_decay, v_attn_sink, "pack_v")
    small_upd = _adam(w_s, g_s, m_s, v_s, "adam_small")

    def unpack(t):
        return dict(b_mod=t[0:6].reshape(1, 6 * D), c_ctx=t[6], norm_mix=t[7:8], norm_ffn=t[8:9], norm_final=t[9],
                    ret_decay=t[10, :nh].reshape(1, 2, RET_HEADS), attn_sink=t[10, nh:nh + ATT_HEADS].reshape(1, ATT_HEADS))

    dense_w = dict(w_in=(w_in, m_w_in, v_w_in), w_out=(w_out, m_w_out, v_w_out), w_gate=(w_gate, m_w_gate, v_w_gate),
                   w_up=(w_up, m_w_up, v_w_up), w_down=(w_down, m_w_down, v_w_down))
    grads = dict(unpack(g_s), w_mod=g_w_mod[None])
    upd = [dict(unpack(t)) for t in small_upd]
    upd[0]["w_mod"], upd[1]["w_mod"], upd[2]["w_mod"] = dl_w_mod[None], m2_w_mod[None], v2_w_mod[None]
    def update(n, g):
        w_, m_, v_ = dense_w[n]
        res = _adam(w_[0], g, m_[0], v_[0], "adam_" + n)
        grads[n] = g[None]
        for u, r_ in zip(upd, res):
            u[n] = r_[None]
        return res[0]

    dep = small_upd[0][0:1, 0:1] + dl_w_mod[0:1, 0:1]
    for n in rest_names:
        dep = dep + update(n, g_rest[n])[0:1, 0:1]
    (g_in,) = _exchange_halves(finish(pending[-1], dep), "exchange_halves_in")
    update("w_in", g_in)

    order = ['c_ctx', 'w_mod', 'b_mod', 'norm_mix', 'norm_ffn', 'w_in', 'ret_decay', 'attn_sink', 'w_out', 'w_gate',
             'w_up', 'w_down', 'norm_final']
    outs = [loss, out["grad_x"][None]] + [grads[n] for n in order]
    for u in upd:
        outs += [u[n] for n in order]
    return tuple(outs)
```

```python
import numpy as np
import jax
import jax.numpy as jnp
from jax import lax
from jax.experimental import pallas as pl
from jax.experimental.pallas import tpu as pltpu

F32 = jnp.float32
BF16 = jnp.bfloat16

RET_HEADS = 8
RET_DK = 64
RET_DV = 128
CHUNK = 128
ATT_HEADS = 16
ATT_KV_HEADS = 4
ATT_DH = 64
GRID_W = 64
ROPE_BASE = 10000.0
NORM_EPS = 1e-6
ADAM_LR = 0.001
ADAM_B1 = 0.9
ADAM_B2 = 0.999
ADAM_EPS = 1e-08
ADAM_WD = 0.01
ADAM_STEP = 10
NEG = -1e30
LANES = 128
VMEM_LIMIT = 56 * 1024 * 1024
ROWS_PER_LATCH = 1024
MESH = pl.DeviceIdType.MESH
N_CHIPS = 4
N_DEV = 8


def _nn(a, b):
    return jnp.dot(a, b, preferred_element_type=F32)


def _nt(a, b):
    return lax.dot_general(a, b, (((1,), (1,)), ((), ())), preferred_element_type=F32)


def _tn(a, b):
    return lax.dot_general(a, b, (((0,), (0,)), ((), ())), preferred_element_type=F32)


def _tile(n, pref, unit=LANES):
    t = min(n, pref)
    t -= t % unit
    while t > unit and n % t:
        t -= unit
    if t <= 0 or n % t:
        return n
    return t


def _params(ndim, vmem=True):
    return pltpu.CompilerParams(dimension_semantics=("arbitrary",) * ndim,
                                vmem_limit_bytes=VMEM_LIMIT if vmem else None)


def _sigmoid(x):
    return 0.5 * jnp.tanh(0.5 * x) + 0.5


def _fsum(x):
    return jnp.sum(jnp.sum(x, axis=0, keepdims=True), axis=1, keepdims=True)


def _rope_tables(L):
    lane = np.arange(LANES)
    d = lane % 64
    inv_r = jnp.asarray(ROPE_BASE, F32) ** (-jnp.arange(32, dtype=F32) / 32)
    t = jnp.arange(L)
    ang_r = t.astype(F32)[:, None] * jnp.tile(inv_r, LANES // 32)[None, :]
    Rr = np.zeros((LANES, LANES), np.float32)
    for l in range(LANES):
        if d[l] < 32:
            Rr[l + 32, l] = -1.0
        else:
            Rr[l - 32, l] = 1.0
    inv_a = jnp.asarray(ROPE_BASE, F32) ** (-jnp.arange(16, dtype=F32) / 16)
    rows = (t // GRID_W).astype(F32)
    cols = (t % GRID_W).astype(F32)
    dd = d % 32
    pos = jnp.where(jnp.asarray(d < 32)[None, :], rows[:, None], cols[:, None])
    ang_a = pos * jnp.tile(inv_a, LANES // 16)[None, :]
    Ra = np.zeros((LANES, LANES), np.float32)
    for l in range(LANES):
        if dd[l] < 16:
            Ra[l + 16, l] = -1.0
        else:
            Ra[l - 16, l] = 1.0
    D0 = np.zeros((LANES, LANES), np.float32)
    D1 = np.zeros((LANES, LANES), np.float32)
    for l in range(LANES):
        D0[l % 64, l] = 1.0
        D1[64 + l % 64, l] = 1.0
    return dict(
        Cr=jnp.cos(ang_r), Sr=jnp.sin(ang_r), Rr=jnp.asarray(Rr, BF16), RrT=jnp.asarray(Rr.T, BF16),
        Ca=jnp.cos(ang_a), Sa=jnp.sin(ang_a), Ra=jnp.asarray(Ra, BF16), RaT=jnp.asarray(Ra.T, BF16),
        D0=jnp.asarray(D0, BF16), D1=jnp.asarray(D1, BF16),
        D0T=jnp.asarray(D0.T, BF16), D1T=jnp.asarray(D1.T, BF16))


def _norm_mod(xf, g, sh, sc):
    r = lax.rsqrt(jnp.mean(xf * xf, axis=-1, keepdims=True) + NORM_EPS)
    return (xf * r * g) * (1.0 + sc) + sh


def _norm_mod_matmul(x, g, sh, sc, w, name):
    M, D = x.shape
    N = w.shape[1]
    tm, tn = _tile(M, ROWS_PER_LATCH, 8), _tile(N, 768)

    def body(x_ref, g_ref, sh_ref, sc_ref, w_ref, p_ref, h_ref, hs):
        @pl.when(pl.program_id(1) == 0)
        def _():
            hb = _norm_mod(x_ref[...], g_ref[...], sh_ref[...], sc_ref[...]).astype(BF16)
            hs[...] = hb
            h_ref[...] = hb
        p_ref[...] = _nn(hs[...], w_ref[...]).astype(BF16)

    vec = pl.BlockSpec((1, D), lambda i, j: (0, 0))
    return pl.pallas_call(
        body, name=name, grid=(M // tm, N // tn),
        in_specs=[pl.BlockSpec((tm, D), lambda i, j: (i, 0)), vec, vec, vec,
                  pl.BlockSpec((D, tn), lambda i, j: (0, j))],
        out_specs=[pl.BlockSpec((tm, tn), lambda i, j: (i, j)), pl.BlockSpec((tm, D), lambda i, j: (i, 0))],
        out_shape=[jax.ShapeDtypeStruct((M, N), BF16), jax.ShapeDtypeStruct((M, D), BF16)],
        scratch_shapes=[pltpu.VMEM((tm, D), BF16)],
        compiler_params=_params(2))(x, g, sh, sc, w)


def _proj_residual(a, w, xres, gt, name):
    M, K = a.shape
    N = w.shape[1]
    tm, tn = _tile(M, ROWS_PER_LATCH, 8), _tile(N, 1024 if K <= 2048 else 512)

    def body(a_ref, w_ref, x_ref, gt_ref, xo_ref, o_ref):
        o = _nn(a_ref[...], w_ref[...])
        o_ref[...] = o.astype(BF16)
        xo_ref[...] = x_ref[...] + gt_ref[...] * o

    return pl.pallas_call(
        body, name=name, grid=(M // tm, N // tn),
        in_specs=[pl.BlockSpec((tm, K), lambda i, j: (i, 0)), pl.BlockSpec((K, tn), lambda i, j: (0, j)),
                  pl.BlockSpec((tm, tn), lambda i, j: (i, j)), pl.BlockSpec((1, tn), lambda i, j: (0, j))],
        out_specs=[pl.BlockSpec((tm, tn), lambda i, j: (i, j)), pl.BlockSpec((tm, tn), lambda i, j: (i, j))],
        out_shape=[jax.ShapeDtypeStruct((M, N), F32), jax.ShapeDtypeStruct((M, N), BF16)],
        compiler_params=_params(2))(a, w, xres, gt)


def _ffn_in(x1, g, sh, sc, wg, wu):
    M, D = x1.shape
    N = wg.shape[1]
    tm, tn = _tile(M, ROWS_PER_LATCH, 8), _tile(N, 512)

    def body(x_ref, g_ref, sh_ref, sc_ref, wg_ref, wu_ref, G_ref, U_ref, A_ref, h_ref, hs):
        @pl.when(pl.program_id(1) == 0)
        def _():
            hb = _norm_mod(x_ref[...], g_ref[...], sh_ref[...], sc_ref[...]).astype(BF16)
            hs[...] = hb
            h_ref[...] = hb
        G = _nn(hs[...], wg_ref[...])
        U = _nn(hs[...], wu_ref[...])
        G_ref[...] = G.astype(BF16)
        U_ref[...] = U.astype(BF16)
        A_ref[...] = (G * _sigmoid(G) * U).astype(BF16)

    vec = pl.BlockSpec((1, D), lambda i, j: (0, 0))
    wspec = pl.BlockSpec((D, tn), lambda i, j: (0, j))
    ospec = pl.BlockSpec((tm, tn), lambda i, j: (i, j))
    big = jax.ShapeDtypeStruct((M, N), BF16)
    return pl.pallas_call(
        body, name="ffn_in", grid=(M // tm, N // tn),
        in_specs=[pl.BlockSpec((tm, D), lambda i, j: (i, 0)), vec, vec, vec, wspec, wspec],
        out_specs=[ospec, ospec, ospec, pl.BlockSpec((tm, D), lambda i, j: (i, 0))],
        out_shape=[big, big, big, jax.ShapeDtypeStruct((M, D), BF16)],
        scratch_shapes=[pltpu.VMEM((tm, D), BF16)],
        compiler_params=_params(2))(x1, g, sh, sc, wg, wu)


def _final(x2, gn, tgt, gt, saved):
    M, D = x2.shape
    tm = _tile(M, 256, 8)

    def body(x_ref, g_ref, t_ref, gt_ref, sv_ref, dx_ref, loss_ref, dg_ref, dz_ref, dgt_ref):
        @pl.when(pl.program_id(0) == 0)
        def _():
            loss_ref[...] = jnp.zeros_like(loss_ref)
            dg_ref[...] = jnp.zeros_like(dg_ref)
            dgt_ref[...] = jnp.zeros_like(dgt_ref)
        x = x_ref[...]
        g = g_ref[...]
        r = lax.rsqrt(jnp.mean(x * x, axis=-1, keepdims=True) + NORM_EPS)
        xh = x * r
        e = xh * g - t_ref[...]
        loss_ref[...] += (0.5 / D) * _fsum(e * e)
        dy = e * (1.0 / D)
        dg_ref[...] += jnp.sum(dy * xh, axis=0, keepdims=True)
        dxh = dy * g
        d = r * (dxh - xh * jnp.mean(dxh * xh, axis=-1, keepdims=True))
        dx_ref[...] = d
        dz_ref[...] = (d * gt_ref[...]).astype(BF16)
        dgt_ref[...] += jnp.sum(d * sv_ref[...].astype(F32), axis=0, keepdims=True)

    row = pl.BlockSpec((tm, D), lambda i: (i, 0))
    vec = pl.BlockSpec((1, D), lambda i: (0, 0))
    return pl.pallas_call(
        body, name="final_loss", grid=(M // tm,),
        in_specs=[row, vec, row, vec, row],
        out_specs=[row, pl.BlockSpec((1, LANES), lambda i: (0, 0)), vec, row, vec],
        out_shape=[jax.ShapeDtypeStruct((M, D), F32), jax.ShapeDtypeStruct((1, LANES), F32),
                   jax.ShapeDtypeStruct((1, D), F32), jax.ShapeDtypeStruct((M, D), BF16),
                   jax.ShapeDtypeStruct((1, D), F32)],
        compiler_params=_params(1))(x2, gn, tgt, gt, saved)


def _col_group(blk0, nblk):
    return int(np.gcd(blk0, nblk)) if blk0 else nblk


def _rope_cols(src, blk0, nblk, Ct, St, R, scale, rope, name):
    M = src.shape[0]
    tm = _tile(M, 512, 8)
    wb = _col_group(blk0, nblk)

    def body(x_ref, c_ref, s_ref, r_ref, o_ref):
        for j in range(wb):
            cols = slice(j * LANES, (j + 1) * LANES)
            x = x_ref[:, cols]
            xf = x.astype(F32)
            if rope:
                xf = xf * c_ref[...] + _nn(x.astype(BF16), r_ref[...]) * s_ref[...]
            o_ref[:, cols] = (xf * scale).astype(BF16)

    tab = pl.BlockSpec((tm, LANES), lambda i, j: (i, 0))
    return pl.pallas_call(
        body, name=name, grid=(M // tm, nblk // wb),
        in_specs=[pl.BlockSpec((tm, wb * LANES), lambda i, j: (i, blk0 // wb + j)), tab, tab,
                  pl.BlockSpec((LANES, LANES), lambda i, j: (0, 0))],
        out_specs=pl.BlockSpec((tm, wb * LANES), lambda i, j: (i, j)),
        out_shape=jax.ShapeDtypeStruct((M, nblk * LANES), BF16),
        compiler_params=_params(2, False))(src, Ct, St, R)


def _dup_heads(src, blk0, npair, Ct, St, R, D0, D1, rope, name):
    M = src.shape[0]
    tm = _tile(M, 512, 8)

    def body(x_ref, c_ref, s_ref, r_ref, d0_ref, d1_ref, o_ref):
        x = x_ref[...]
        if rope:
            x = (x.astype(F32) * c_ref[...] + _nn(x, r_ref[...]) * s_ref[...]).astype(BF16)
        o_ref[0] = _nn(x, d0_ref[...]).astype(BF16)
        o_ref[1] = _nn(x, d1_ref[...]).astype(BF16)

    tab = pl.BlockSpec((tm, LANES), lambda i, p: (i, 0))
    mat = pl.BlockSpec((LANES, LANES), lambda i, p: (0, 0))
    return pl.pallas_call(
        body, name=name, grid=(M // tm, npair),
        in_specs=[pl.BlockSpec((tm, LANES), lambda i, p: (i, blk0 + p)), tab, tab, mat, mat, mat],
        out_specs=pl.BlockSpec((2, tm, LANES), lambda i, p: (p, i, 0)),
        out_shape=jax.ShapeDtypeStruct((2 * npair, M, LANES), BF16),
        compiler_params=_params(2, False))(src, Ct, St, R, D0, D1)


def _unrope_cols(dsrc, dst, blk0, nblk, Ct, St, RT, scale, rope, name):
    M = dsrc.shape[0]
    tm = _tile(M, 512, 8)
    wb = _col_group(blk0, nblk)

    def body(x_ref, c_ref, s_ref, r_ref, dst_ref, o_ref):
        del dst_ref
        for j in range(wb):
            cols = slice(j * LANES, (j + 1) * LANES)
            xf = x_ref[:, cols].astype(F32)
            if rope:
                xf = xf * c_ref[...] + _nn((xf * s_ref[...]).astype(BF16), r_ref[...])
            o_ref[:, cols] = (xf * scale).astype(BF16)

    tab = pl.BlockSpec((tm, LANES), lambda i, j: (i, 0))
    return pl.pallas_call(
        body, name=name, grid=(M // tm, nblk // wb),
        in_specs=[pl.BlockSpec((tm, wb * LANES), lambda i, j: (i, j)), tab, tab,
                  pl.BlockSpec((LANES, LANES), lambda i, j: (0, 0)),
                  pl.BlockSpec(memory_space=pl.ANY)],
        out_specs=pl.BlockSpec((tm, wb * LANES), lambda i, j: (i, blk0 // wb + j)),
        out_shape=jax.ShapeDtypeStruct(dst.shape, dst.dtype),
        input_output_aliases={4: 0},
        compiler_params=_params(2, False))(dsrc, Ct, St, RT, dst)


def _fold_heads(parts, dst, blk0, npair, Ct, St, RT, D0T, D1T, rope, name):
    M = parts[0][0].shape[1]
    nb = M // CHUNK
    R = _tile(M, 1024, CHUNK)
    rb = R // CHUNK
    nrefs = sum(1 if s == 0 else 2 for _, s in parts)

    def body(*refs):
        part_refs = list(refs[:nrefs])
        c_ref, s_ref, r_ref, d0_ref, d1_ref, dst_ref, o_ref = refs[nrefs:]
        del dst_ref
        i = pl.program_id(0)
        tot = [jnp.zeros((R, LANES), F32), jnp.zeros((R, LANES), F32)]
        for _, shift in parts:
            main = part_refs.pop(0)
            if shift == 0:
                for e in range(2):
                    tot[e] = tot[e] + main[e].astype(F32)
                continue
            edge = part_refs.pop(0)
            ok = (i + 1) * rb <= nb - 1 if shift > 0 else i > 0
            for e in range(2):
                ed = jnp.where(ok, edge[e].astype(F32), 0.0)
                if rb == 1:
                    tot[e] = tot[e] + ed
                elif shift > 0:
                    tot[e] = tot[e] + jnp.concatenate([main[e, CHUNK:, :].astype(F32), ed], axis=0)
                else:
                    tot[e] = tot[e] + jnp.concatenate([ed, main[e, :R - CHUNK, :].astype(F32)], axis=0)
        f = _nn(tot[0].astype(BF16), d0_ref[...]) + _nn(tot[1].astype(BF16), d1_ref[...])
        if rope:
            f = f * c_ref[...] + _nn((f * s_ref[...]).astype(BF16), r_ref[...])
        o_ref[...] = f.astype(BF16)

    in_specs, args = [], []
    for a, shift in parts:
        assert shift in (-1, 0, 1)
        in_specs.append(pl.BlockSpec((2, R, LANES), lambda i, p: (p, i, 0)))
        args.append(a)
        if shift > 0:
            in_specs.append(pl.BlockSpec((2, CHUNK, LANES), lambda i, p: (p, jnp.minimum((i + 1) * rb, nb - 1), 0)))
            args.append(a)
        elif shift < 0:
            in_specs.append(pl.BlockSpec((2, CHUNK, LANES), lambda i, p: (p, jnp.maximum(i * rb - 1, 0), 0)))
            args.append(a)
    tab = pl.BlockSpec((R, LANES), lambda i, p: (i, 0))
    mat = pl.BlockSpec((LANES, LANES), lambda i, p: (0, 0))
    return pl.pallas_call(
        body, name=name, grid=(M // R, npair),
        in_specs=in_specs + [tab, tab, mat, mat, mat, pl.BlockSpec(memory_space=pl.ANY)],
        out_specs=pl.BlockSpec((R, LANES), lambda i, p: (i, blk0 + p)),
        out_shape=jax.ShapeDtypeStruct(dst.shape, dst.dtype),
        input_output_aliases={nrefs + 5: 0},
        compiler_params=_params(2, False))(*args, Ct, St, RT, D0T, D1T, dst)


def _head_masks():
    lane = lax.broadcasted_iota(jnp.int32, (1, LANES), 1)
    return [lane < 64, lane >= 64]


def _decay_vecs(lam, mu):
    i = lax.broadcasted_iota(jnp.int32, (CHUNK, 1), 0).astype(F32)
    return dict(qf=jnp.exp(lam * (i + 1.0)), kf=jnp.exp(lam * (CHUNK - 1.0 - i)),
                qb=jnp.exp(mu * (CHUNK - i)), kb=jnp.exp(mu * i),
                gf=jnp.exp(lam * float(CHUNK)), gb=jnp.exp(mu * float(CHUNK)), i=i)


def _decay_mask(lam, mu):
    r = lax.broadcasted_iota(jnp.int32, (CHUNK, CHUNK), 0)
    c = lax.broadcasted_iota(jnp.int32, (CHUNK, CHUNK), 1)
    rel = (r - c).astype(F32)
    low = rel >= 0.0
    mf = jnp.exp(lam * jnp.maximum(rel, 0.0))
    mb = jnp.exp(mu * jnp.maximum(-rel, 0.0))
    return jnp.where(low, mf, mb), rel, low


def _lam_of(lg_ref, row, idx):
    return jnp.full((1, 1), lg_ref[row, idx], F32)


def _group_index(pair_blk, npairs):
    assert pair_blk % npairs == 0
    return pair_blk // npairs


def _ret_states_fwd(Kr, P, Krc, Pc, lg, rv_blk, npairs):
    L = Kr.shape[0]
    Lc = Krc.shape[0]
    N, ncc = L // CHUNK, Lc // CHUNK
    rv_grp = _group_index(rv_blk, npairs)

    heads = [(p, h) for p in range(npairs) for h in range(2)]
    kcols = lambda p: slice(p * LANES, (p + 1) * LANES)
    vcols = lambda p, h: slice((2 * p + h) * LANES, (2 * p + h + 1) * LANES)

    def body(lg_ref, k_ref, v_ref, kc_ref, vc_ref, sf_ref, S):
        n = pl.program_id(0)
        masks = _head_masks()

        @pl.when(n == 0)
        def _():
            for p, h in heads:
                lam = _lam_of(lg_ref, 0, 2 * p + h)
                dv = _decay_vecs(lam, lam)
                s = jnp.zeros((LANES, LANES), F32)
                for cc in range(ncc):
                    rows = slice(cc * CHUNK, (cc + 1) * CHUNK)
                    kw = jnp.where(masks[h], kc_ref[rows, kcols(p)].astype(F32) * dv["kf"], 0.0).astype(BF16)
                    s = dv["gf"] * s + _tn(kw, vc_ref[rows, vcols(p, h)])
                S[p, h] = s

        for p, h in heads:
            lam = _lam_of(lg_ref, 0, 2 * p + h)
            dv = _decay_vecs(lam, lam)
            s = S[p, h]
            sf_ref[p, 0, h] = s.astype(BF16)
            kw = jnp.where(masks[h], k_ref[:, kcols(p)].astype(F32) * dv["kf"], 0.0).astype(BF16)
            S[p, h] = dv["gf"] * s + _tn(kw, v_ref[:, vcols(p, h)])

    wq, wv = npairs * LANES, npairs * 2 * LANES
    return pl.pallas_call(
        body, name="ret_states_fwd", grid=(N,),
        in_specs=[pl.BlockSpec(memory_space=pltpu.SMEM),
                  pl.BlockSpec((CHUNK, wq), lambda n: (n, 0)),
                  pl.BlockSpec((CHUNK, wv), lambda n: (n, rv_grp)),
                  pl.BlockSpec((Lc, wq), lambda n: (0, 0)),
                  pl.BlockSpec((Lc, wv), lambda n: (0, rv_grp))],
        out_specs=pl.BlockSpec((npairs, 1, 2, LANES, LANES), lambda n: (0, n, 0, 0, 0)),
        out_shape=jax.ShapeDtypeStruct((npairs, N, 2, LANES, LANES), BF16),
        scratch_shapes=[pltpu.VMEM((npairs, 2, LANES, LANES), F32)],
        compiler_params=_params(1, False))(lg, Kr, P, Krc, Pc)


def _ret_chunk_fwd(q, k, v, sf, sb, hm, lam, mu, Mk):
    dv = _decay_vecs(lam, mu)
    qm = jnp.where(hm, q, jnp.zeros_like(q))
    qmf = qm.astype(F32)
    A = _nt(qm, k)
    Am = A * Mk
    Amb = Am.astype(BF16)
    Qf = (qmf * dv["qf"]).astype(BF16)
    Qb = (qmf * dv["qb"]).astype(BF16)
    O = _nn(Amb, v) + _nn(Qf, sf) + _nn(Qb, sb)
    return dict(dv=dv, Mk=Mk, qm=qm, Am=Am, Amb=Amb, Qf=Qf, Qb=Qb, O=O)


def _ret_out_fwd(Qr, Kr, P, Krc, Pc, SF, lg, rv_blk, rg_blk, npairs, d_mix):
    L = Qr.shape[0]
    Lc = Krc.shape[0]
    N, ncc = L // CHUNK, Lc // CHUNK

    rv_grp, rg_grp = _group_index(rv_blk, npairs), _group_index(rg_blk, npairs)
    heads = [(p, h) for p in range(npairs) for h in range(2)]
    kcols = lambda p: slice(p * LANES, (p + 1) * LANES)
    vcols = lambda p, h: slice((2 * p + h) * LANES, (2 * p + h + 1) * LANES)

    def body(lg_ref, q_ref, k_ref, v_ref, g_ref, sf_ref, kc_ref, vc_ref, y_ref, sb_ref, S, Mks):
        n = pl.program_id(0)
        masks = _head_masks()

        @pl.when(n == 0)
        def _():
            for p, h in heads:
                mu = _lam_of(lg_ref, 1, 2 * p + h)
                Mks[p, h] = _decay_mask(_lam_of(lg_ref, 0, 2 * p + h), mu)[0]
                dvb = _decay_vecs(mu, mu)
                s = jnp.zeros((LANES, LANES), F32)
                for cc in reversed(range(ncc)):
                    rows = slice(cc * CHUNK, (cc + 1) * CHUNK)
                    kw = jnp.where(masks[h], kc_ref[rows, kcols(p)].astype(F32) * dvb["kb"], 0.0).astype(BF16)
                    s = dvb["gb"] * s + _tn(kw, vc_ref[rows, vcols(p, h)])
                S[p, h] = s

        for p, h in heads:
            lam = _lam_of(lg_ref, 0, 2 * p + h)
            mu = _lam_of(lg_ref, 1, 2 * p + h)
            hm = masks[h]
            dvb = _decay_vecs(lam, mu)
            s = S[p, h]
            sbb = s.astype(BF16)
            sb_ref[p, 0, h] = sbb
            k = k_ref[:, kcols(p)]
            v = v_ref[:, vcols(p, h)]
            f = _ret_chunk_fwd(q_ref[:, kcols(p)], k, v, sf_ref[p, 0, h], sbb, hm, lam, mu, Mks[p, h])
            O = f["O"]
            r = lax.rsqrt(jnp.mean(O * O, axis=-1, keepdims=True) + NORM_EPS)
            g = g_ref[:, vcols(p, h)].astype(F32)
            y_ref[:, vcols(p, h)] = (O * r * (g * _sigmoid(g))).astype(BF16)
            kw = jnp.where(hm, k.astype(F32) * dvb["kb"], 0.0).astype(BF16)
            S[p, h] = dvb["gb"] * s + _tn(kw, v)

    rev = lambda n: N - 1 - n
    wq, wv = npairs * LANES, npairs * 2 * LANES
    st = pl.BlockSpec((npairs, 1, 2, LANES, LANES), lambda n: (0, rev(n), 0, 0, 0))
    return pl.pallas_call(
        body, name="ret_out_fwd", grid=(N,),
        in_specs=[pl.BlockSpec(memory_space=pltpu.SMEM),
                  pl.BlockSpec((CHUNK, wq), lambda n: (rev(n), 0)),
                  pl.BlockSpec((CHUNK, wq), lambda n: (rev(n), 0)),
                  pl.BlockSpec((CHUNK, wv), lambda n: (rev(n), rv_grp)),
                  pl.BlockSpec((CHUNK, wv), lambda n: (rev(n), rg_grp)),
                  st,
                  pl.BlockSpec((Lc, wq), lambda n: (0, 0)),
                  pl.BlockSpec((Lc, wv), lambda n: (0, rv_grp))],
        out_specs=[pl.BlockSpec((CHUNK, wv), lambda n: (rev(n), 0)), st],
        out_shape=[jax.ShapeDtypeStruct((L, d_mix), BF16),
                   jax.ShapeDtypeStruct((npairs, N, 2, LANES, LANES), BF16)],
        scratch_shapes=[pltpu.VMEM((npairs, 2, LANES, LANES), F32), pltpu.VMEM((npairs, 2, CHUNK, CHUNK), F32)],
        compiler_params=_params(1))(lg, Qr, Kr, P, P, SF, Krc, Pc)


ACC_ROWS = 8


def _ret_bwd1(Qr, Kr, P, Krc, Pc, SF, SB, dY, lg, rv_blk, rg_blk, npairs, d_proj):
    L = Qr.shape[0]
    Lc = Krc.shape[0]
    N, ncc = L // CHUNK, Lc // CHUNK
    rv_grp, rg_grp = _group_index(rv_blk, npairs), _group_index(rg_blk, npairs)
    heads = [(p, h) for p in range(npairs) for h in range(2)]
    kcols = lambda p: slice(p * LANES, (p + 1) * LANES)
    vcols = lambda p, h: slice((2 * p + h) * LANES, (2 * p + h + 1) * LANES)

    def body(lg_ref, q_ref, k_ref, v_ref, g_ref, sf_ref, sb_ref, dy_ref, kc_ref, vc_ref,
             dq_ref, dk_ref, dv_ref, dg_ref, do_ref, dkc_ref, dvc_ref, acc_ref, dS, T, Mks):
        n = pl.program_id(0)
        masks = _head_masks()

        @pl.when(n == 0)
        def _():
            dS[...] = jnp.zeros_like(dS)
            T[...] = jnp.zeros_like(T)
            acc_ref[...] = jnp.zeros_like(acc_ref)
            for p, h in heads:
                Mks[p, h] = _decay_mask(_lam_of(lg_ref, 0, 2 * p + h), _lam_of(lg_ref, 1, 2 * p + h))[0]

        def head_main(p, h):
            lam = _lam_of(lg_ref, 0, 2 * p + h)
            mu = _lam_of(lg_ref, 1, 2 * p + h)
            hm = masks[h]
            hs = vcols(p, h)
            v = v_ref[:, hs]
            k = k_ref[:, kcols(p)]
            sf = sf_ref[p, 0, h]
            sb = sb_ref[p, 0, h]
            f = _ret_chunk_fwd(q_ref[:, kcols(p)], k, v, sf, sb, hm, lam, mu, Mks[p, h])
            dv_, O = f["dv"], f["O"]
            r = lax.rsqrt(jnp.mean(O * O, axis=-1, keepdims=True) + NORM_EPS)
            on = O * r
            g = g_ref[:, hs].astype(F32)
            sg = _sigmoid(g)
            dy = dy_ref[:, hs].astype(F32)
            dg_ref[:, hs] = (dy * on * (sg * (1.0 + g * (1.0 - sg)))).astype(BF16)
            don = dy * (g * sg)
            dO = r * (don - on * jnp.mean(don * on, axis=-1, keepdims=True))
            dOb = dO.astype(BF16)
            do_ref[:, hs] = dOb
            dAm = _nt(dOb, v)
            T[p, h] += dAm * f["Am"]
            dAb = (dAm * f["Mk"]).astype(BF16)
            km = jnp.where(hm, k, jnp.zeros_like(k))
            dq = _nn(dAb, km)
            dk = _tn(dAb, f["qm"])
            dvh = _tn(f["Amb"], dOb)
            dQf = _nt(dOb, sf)
            dQb = _nt(dOb, sb)
            dq = dq + dQf * dv_["qf"] + dQb * dv_["qb"]
            acc_ref[p, h, 0:1, :] += _fsum(dQf * f["Qf"].astype(F32) * (dv_["i"] + 1.0))
            acc_ref[p, h, 1:2, :] += _fsum(dQb * f["Qb"].astype(F32) * (CHUNK - dv_["i"]))
            dSh = dS[p, h]
            dSb_ = dSh.astype(BF16)
            Kf = (km.astype(F32) * dv_["kf"]).astype(BF16)
            dKf = _nt(v, dSb_)
            dk = dk + jnp.where(hm, dKf * dv_["kf"], 0.0)
            acc_ref[p, h, 2:3, :] += _fsum(jnp.where(hm, dKf, 0.0) * Kf.astype(F32) * (CHUNK - 1.0 - dv_["i"]))
            dvh = dvh + _nn(Kf, dSb_)
            acc_ref[p, h, 3:4, :] += float(CHUNK) * dv_["gf"] * _fsum(dSh * sf.astype(F32))
            dSh = dv_["gf"] * dSh + _tn(f["Qf"], dOb)
            dS[p, h] = dSh
            dv_ref[:, hs] = dvh
            return dq, dk

        for p in range(npairs):
            dq0, dk0 = head_main(p, 0)
            dq1, dk1 = head_main(p, 1)
            dq_ref[:, kcols(p)] = dq0 + dq1
            dk_ref[:, kcols(p)] = dk0 + dk1

        @pl.when(n == N - 1)
        def _():
            for p, h in heads:
                lam = _lam_of(lg_ref, 0, 2 * p + h)
                dv_ = _decay_vecs(lam, lam)
                hm = masks[h]
                hs = vcols(p, h)
                states = [jnp.zeros((LANES, LANES), F32)]
                kws = []
                for cc in range(ncc):
                    rows = slice(cc * CHUNK, (cc + 1) * CHUNK)
                    kw = jnp.where(hm, kc_ref[rows, kcols(p)].astype(F32) * dv_["kf"], 0.0).astype(BF16)
                    kws.append(kw)
                    states.append(dv_["gf"] * states[-1] + _tn(kw, vc_ref[rows, hs]))
                d = dS[p, h]
                for cc in reversed(range(ncc)):
                    db = d.astype(BF16)
                    rows = slice(cc * CHUNK, (cc + 1) * CHUNK)
                    dKf_c = jnp.where(hm, _nt(vc_ref[rows, hs], db), 0.0)
                    part = dKf_c * dv_["kf"]
                    if h == 0:
                        dkc_ref[rows, kcols(p)] = part
                    else:
                        dkc_ref[rows, kcols(p)] += part
                    acc_ref[p, h, 2:3, :] += _fsum(dKf_c * kws[cc].astype(F32) * (CHUNK - 1.0 - dv_["i"]))
                    dvc_ref[rows, hs] = _nn(kws[cc], db)
                    acc_ref[p, h, 3:4, :] += float(CHUNK) * dv_["gf"] * _fsum(d * states[cc])
                    d = dv_["gf"] * d
                _, rel, low = _decay_mask(lam, lam)
                Th = T[p, h]
                acc_ref[p, h, 4:5, :] += _fsum(jnp.where(low, Th * rel, 0.0))
                acc_ref[p, h, 5:6, :] += _fsum(jnp.where(low, 0.0, -Th * rel))

    rev = lambda n: N - 1 - n
    wq, wv = npairs * LANES, npairs * 2 * LANES
    st = pl.BlockSpec((npairs, 1, 2, LANES, LANES), lambda n: (0, rev(n), 0, 0, 0))
    pair = pl.BlockSpec((CHUNK, wq), lambda n: (rev(n), 0))
    wide = lambda grp: pl.BlockSpec((CHUNK, wv), lambda n: (rev(n), grp))
    return pl.pallas_call(
        body, name="ret_bwd_desc", grid=(N,),
        in_specs=[pl.BlockSpec(memory_space=pltpu.SMEM), pair, pair, wide(rv_grp), wide(rg_grp), st, st, wide(0),
                  pl.BlockSpec((Lc, wq), lambda n: (0, 0)),
                  pl.BlockSpec((Lc, wv), lambda n: (0, rv_grp))],
        out_specs=[pair, pair, wide(0), wide(rg_grp), wide(0),
                   pl.BlockSpec((Lc, wq), lambda n: (0, 0)),
                   pl.BlockSpec((Lc, wv), lambda n: (0, 0)),
                   pl.BlockSpec((npairs, 2, ACC_ROWS, LANES), lambda n: (0, 0, 0, 0))],
        out_shape=[jax.ShapeDtypeStruct((L, npairs * LANES), F32),
                   jax.ShapeDtypeStruct((L, npairs * LANES), F32),
                   jax.ShapeDtypeStruct((L, npairs * 2 * LANES), F32),
                   jax.ShapeDtypeStruct((L, d_proj), BF16),
                   jax.ShapeDtypeStruct((L, npairs * 2 * LANES), BF16),
                   jax.ShapeDtypeStruct((Lc, npairs * LANES), F32),
                   jax.ShapeDtypeStruct((Lc, npairs * 2 * LANES), F32),
                   jax.ShapeDtypeStruct((npairs, 2, ACC_ROWS, LANES), F32)],
        scratch_shapes=[pltpu.VMEM((npairs, 2, LANES, LANES), F32), pltpu.VMEM((npairs, 2, CHUNK, CHUNK), F32),
                        pltpu.VMEM((npairs, 2, CHUNK, CHUNK), F32)],
        compiler_params=_params(1))(lg, Qr, Kr, P, P, SF, SB, dY, Krc, Pc)


def _ret_bwd2(Qr, Kr, P, Krc, Pc, SB, dO, dKr, dVp, dP, dKc, dVc, lg, rv_blk, npairs):
    L = Qr.shape[0]
    Lc = Krc.shape[0]
    N, ncc = L // CHUNK, Lc // CHUNK
    rv_grp = _group_index(rv_blk, npairs)
    heads = [(p, h) for p in range(npairs) for h in range(2)]
    kcols = lambda p: slice(p * LANES, (p + 1) * LANES)
    vcols = lambda p, h: slice((2 * p + h) * LANES, (2 * p + h + 1) * LANES)

    def body(lg_ref, q_ref, k_ref, v_ref, sb_ref, do_ref, dkin_ref, dvin_ref, kc_ref, vc_ref, dkcin_ref, dvcin_ref,
             dpin_ref, dk_ref, dv_ref, dkc_ref, dvc_ref, acc_ref, dS):
        del dpin_ref
        n = pl.program_id(0)
        masks = _head_masks()

        @pl.when(n == 0)
        def _():
            dS[...] = jnp.zeros_like(dS)
            acc_ref[...] = jnp.zeros_like(acc_ref)

        def head_main(p, h):
            mu = _lam_of(lg_ref, 1, 2 * p + h)
            hm = masks[h]
            hs = vcols(p, h)
            dv_ = _decay_vecs(mu, mu)
            v = v_ref[:, hs]
            k = k_ref[:, kcols(p)]
            q = q_ref[:, kcols(p)]
            dOb = do_ref[:, hs]
            km = jnp.where(hm, k, jnp.zeros_like(k)).astype(F32)
            Kb = (km * dv_["kb"]).astype(BF16)
            Qb = (jnp.where(hm, q, jnp.zeros_like(q)).astype(F32) * dv_["qb"]).astype(BF16)
            dSh = dS[p, h]
            dSb_ = dSh.astype(BF16)
            dKb = jnp.where(hm, _nt(v, dSb_), 0.0)
            acc_ref[p, h, 0:1, :] += _fsum(dKb * Kb.astype(F32) * dv_["i"])
            dv_ref[:, hs] = (dvin_ref[:, hs] + _nn(Kb, dSb_)).astype(BF16)
            acc_ref[p, h, 1:2, :] += float(CHUNK) * dv_["gb"] * _fsum(dSh * sb_ref[p, 0, h].astype(F32))
            dS[p, h] = dv_["gb"] * dSh + _tn(Qb, dOb)
            return dKb * dv_["kb"]

        for p in range(npairs):
            dk_ref[:, kcols(p)] = dkin_ref[:, kcols(p)] + head_main(p, 0) + head_main(p, 1)

        @pl.when(n == N - 1)
        def _():
            for p, h in heads:
                mu = _lam_of(lg_ref, 1, 2 * p + h)
                hm = masks[h]
                hs = vcols(p, h)
                dv_ = _decay_vecs(mu, mu)
                states = {}
                kws = {}
                s = jnp.zeros((LANES, LANES), F32)
                for cc in reversed(range(ncc)):
                    rows = slice(cc * CHUNK, (cc + 1) * CHUNK)
                    states[cc] = s
                    kw = jnp.where(hm, kc_ref[rows, kcols(p)].astype(F32) * dv_["kb"], 0.0).astype(BF16)
                    kws[cc] = kw
                    s = dv_["gb"] * s + _tn(kw, vc_ref[rows, hs])
                d = dS[p, h]
                for cc in range(ncc):
                    db = d.astype(BF16)
                    rows = slice(cc * CHUNK, (cc + 1) * CHUNK)
                    dKb_c = jnp.where(hm, _nt(vc_ref[rows, hs], db), 0.0)
                    part = dKb_c * dv_["kb"]
                    if h == 0:
                        dkc_ref[rows, kcols(p)] = dkcin_ref[rows, kcols(p)] + part
                    else:
                        dkc_ref[rows, kcols(p)] += part
                    acc_ref[p, h, 0:1, :] += _fsum(dKb_c * kws[cc].astype(F32) * dv_["i"])
                    dvc_ref[rows, hs] = dvcin_ref[rows, hs] + _nn(kws[cc], db)
                    acc_ref[p, h, 1:2, :] += float(CHUNK) * dv_["gb"] * _fsum(d * states[cc])
                    d = dv_["gb"] * d

    wq, wv = npairs * LANES, npairs * 2 * LANES
    st = pl.BlockSpec((npairs, 1, 2, LANES, LANES), lambda n: (0, n, 0, 0, 0))
    pair = pl.BlockSpec((CHUNK, wq), lambda n: (n, 0))
    wide = lambda grp: pl.BlockSpec((CHUNK, wv), lambda n: (n, grp))
    ckc = pl.BlockSpec((Lc, wq), lambda n: (0, 0))
    cvc = lambda grp: pl.BlockSpec((Lc, wv), lambda n: (0, grp))
    return pl.pallas_call(
        body, name="ret_bwd_asc", grid=(N,),
        in_specs=[pl.BlockSpec(memory_space=pltpu.SMEM), pair, pair, wide(rv_grp), st, wide(0), pair, wide(0),
                  ckc, cvc(rv_grp), ckc, cvc(0), pl.BlockSpec(memory_space=pl.ANY)],
        out_specs=[pair, wide(rv_grp), ckc, cvc(0),
                   pl.BlockSpec((npairs, 2, ACC_ROWS, LANES), lambda n: (0, 0, 0, 0))],
        out_shape=[jax.ShapeDtypeStruct(dKr.shape, F32),
                   jax.ShapeDtypeStruct(dP.shape, dP.dtype),
                   jax.ShapeDtypeStruct(dKc.shape, F32),
                   jax.ShapeDtypeStruct(dVc.shape, F32),
                   jax.ShapeDtypeStruct((npairs, 2, ACC_ROWS, LANES), F32)],
        input_output_aliases={12: 1},
        scratch_shapes=[pltpu.VMEM((npairs, 2, LANES, LANES), F32)],
        compiler_params=_params(1))(lg, Qr, Kr, P, SB, dO, dKr, dVp, Krc, Pc, dKc, dVc, dP)


GROUP = 4


def _att_band(Lc):
    row = np.arange(GROUP * CHUNK)[:, None] % CHUNK
    col = np.arange(3 * CHUNK + Lc)[None, :]
    ok = ((col >= row) & (col <= row + 2 * CHUNK)) | (col >= 3 * CHUNK)
    return jnp.asarray(np.where(ok, 0.0, NEG), F32)


def _att_edge(n, N, Lc):
    col = lax.broadcasted_iota(jnp.int32, (1, 3 * CHUNK + Lc), 1)
    off = jnp.logical_or(jnp.logical_and(col < CHUNK, n == 0),
                         jnp.logical_and(jnp.logical_and(col >= 2 * CHUNK, col < 3 * CHUNK), n == N - 1))
    return jnp.where(off, NEG, 0.0)


def _stack_heads(ref, gi):
    masks = _head_masks()
    tiles = []
    for pr in range(2):
        t = ref[:, (2 * gi + pr) * LANES:(2 * gi + pr + 1) * LANES]
        for a in range(2):
            tiles.append(jnp.where(masks[a], t, jnp.zeros_like(t)))
    return jnp.concatenate(tiles, axis=0)


def _unstack_heads(x4):
    m0 = _head_masks()[0]
    return [jnp.where(m0, x4[(2 * pr) * CHUNK:(2 * pr + 1) * CHUNK], x4[(2 * pr + 1) * CHUNK:(2 * pr + 2) * CHUNK])
            for pr in range(2)]


def _sink_column(sink_ref, g):
    row = lax.broadcasted_iota(jnp.int32, (GROUP * CHUNK, 1), 0) // CHUNK
    col = jnp.zeros((GROUP * CHUNK, 1), F32)
    for h in range(GROUP):
        col = jnp.where(row == h, sink_ref[0, g * GROUP + h], col)
    return col


def _att_probs(q4, Kall, bias, snk):
    s = _nt(q4, Kall) + bias
    mx = jnp.maximum(jnp.max(s, axis=1, keepdims=True), snk)
    p = jnp.exp(s - mx)
    p_snk = jnp.exp(snk - mx)
    inv = 1.0 / (jnp.sum(p, axis=1, keepdims=True) + p_snk)
    return p, p_snk, inv


def _att_groups_per_step(nkv, blk0):
    for gps in (4, 2):
        if nkv % gps == 0 and blk0 % gps == 0:
            return gps
    return 1


def _att_specs(Lc, N, gps):
    q = pl.BlockSpec((CHUNK, gps * 2 * LANES), lambda g, n: (n, g))
    kv = lambda s: pl.BlockSpec((gps, CHUNK, LANES), lambda g, n: (g, jnp.clip(n + s, 0, N - 1), 0))
    ctx = pl.BlockSpec((gps, Lc, LANES), lambda g, n: (g, 0, 0))
    return q, kv, ctx


def _att_fwd(Qa, Kd, Vd, Kdc, Vdc, sink, Y, blk0):
    L = Qa.shape[0]
    Lc = Kdc.shape[1]
    N = L // CHUNK
    nkv = Kd.shape[0]
    gps = _att_groups_per_step(nkv, blk0)

    def body(sink_ref, band_ref, q_ref, kp, kc_, kn, vp, vc_, vn, kctx, vctx, y_in, o_ref):
        del y_in
        g, n = pl.program_id(0), pl.program_id(1)
        bias = band_ref[...] + _att_edge(n, N, Lc)
        for gi in range(gps):
            Kall = jnp.concatenate([kp[gi], kc_[gi], kn[gi], kctx[gi]], axis=0)
            Vall = jnp.concatenate([vp[gi], vc_[gi], vn[gi], vctx[gi]], axis=0)
            p, _, inv = _att_probs(_stack_heads(q_ref, gi), Kall, bias, _sink_column(sink_ref, g * gps + gi))
            o4 = _nn(p.astype(BF16), Vall) * inv
            for pr, o in enumerate(_unstack_heads(o4)):
                o_ref[:, (2 * gi + pr) * LANES:(2 * gi + pr + 1) * LANES] = o.astype(BF16)

    q, kv, ctx = _att_specs(Lc, N, gps)
    band = pl.BlockSpec((GROUP * CHUNK, 3 * CHUNK + Lc), lambda g, n: (0, 0))
    return pl.pallas_call(
        body, name="att_fwd", grid=(nkv // gps, N),
        in_specs=[pl.BlockSpec(memory_space=pltpu.SMEM), band, q, kv(-1), kv(0), kv(1), kv(-1), kv(0), kv(1), ctx, ctx,
                  pl.BlockSpec(memory_space=pl.ANY)],
        out_specs=pl.BlockSpec((CHUNK, gps * 2 * LANES), lambda g, n: (n, blk0 // gps + g)),
        out_shape=jax.ShapeDtypeStruct(Y.shape, Y.dtype),
        input_output_aliases={11: 0},
        compiler_params=_params(2))(sink, _att_band(Lc), Qa, Kd, Kd, Kd, Vd, Vd, Vd, Kdc, Vdc, Y)


def _att_bwd(Qa, Kd, Vd, Kdc, Vdc, sink, dY, blk0):
    L = Qa.shape[0]
    Lc = Kdc.shape[1]
    N = L // CHUNK
    nkv = Kd.shape[0]
    gps = _att_groups_per_step(nkv, blk0)

    def body(sink_ref, band_ref, q_ref, kp, kc_, kn, vp, vc_, vn, kctx, vctx, dy_ref,
             dq_ref, dkp, dkc_, dkn, dvp, dvc_, dvn, dkctx, dvctx, dsink_ref):
        g, n = pl.program_id(0), pl.program_id(1)

        @pl.when(n == 0)
        def _():
            dkctx[...] = jnp.zeros_like(dkctx)
            dvctx[...] = jnp.zeros_like(dvctx)
            dsink_ref[...] = jnp.zeros_like(dsink_ref)

        bias = band_ref[...] + _att_edge(n, N, Lc)
        for gi in range(gps):
            Kall = jnp.concatenate([kp[gi], kc_[gi], kn[gi], kctx[gi]], axis=0)
            Vall = jnp.concatenate([vp[gi], vc_[gi], vn[gi], vctx[gi]], axis=0)
            q4 = _stack_heads(q_ref, gi)
            do4 = _stack_heads(dy_ref, gi)
            p, p_snk, inv = _att_probs(q4, Kall, bias, _sink_column(sink_ref, g * gps + gi))
            P = p * inv
            dp = _nt(do4, Vall)
            delta = jnp.sum(P * dp, axis=1, keepdims=True)
            ds = (P * (dp - delta)).astype(BF16)
            dsnk = -(p_snk * inv) * delta
            for h in range(GROUP):
                dsink_ref[gi, h:h + 1, :] += _fsum(dsnk[h * CHUNK:(h + 1) * CHUNK])
            for pr, dq in enumerate(_unstack_heads(_nn(ds, Kall))):
                dq_ref[:, (2 * gi + pr) * LANES:(2 * gi + pr + 1) * LANES] = dq
            dK = _tn(ds, q4)
            dV = _tn(P.astype(BF16), do4)
            for j, (rk, rv) in enumerate([(dkp, dvp), (dkc_, dvc_), (dkn, dvn)]):
                rk[gi] = dK[j * CHUNK:(j + 1) * CHUNK].astype(BF16)
                rv[gi] = dV[j * CHUNK:(j + 1) * CHUNK].astype(BF16)
            dkctx[gi] += dK[3 * CHUNK:]
            dvctx[gi] += dV[3 * CHUNK:]

    q, kv, ctx = _att_specs(Lc, N, gps)
    band = pl.BlockSpec((GROUP * CHUNK, 3 * CHUNK + Lc), lambda g, n: (0, 0))
    blk = pl.BlockSpec((gps, CHUNK, LANES), lambda g, n: (g, n, 0))
    part = jax.ShapeDtypeStruct((nkv, L, LANES), BF16)
    cshape = jax.ShapeDtypeStruct((nkv, Lc, LANES), F32)
    return pl.pallas_call(
        body, name="att_bwd", grid=(nkv // gps, N),
        in_specs=[pl.BlockSpec(memory_space=pltpu.SMEM), band, q, kv(-1), kv(0), kv(1), kv(-1), kv(0), kv(1), ctx, ctx,
                  pl.BlockSpec((CHUNK, gps * 2 * LANES), lambda g, n: (n, blk0 // gps + g))],
        out_specs=[q, blk, blk, blk, blk, blk, blk, ctx, ctx,
                   pl.BlockSpec((gps, 8, LANES), lambda g, n: (g, 0, 0))],
        out_shape=[jax.ShapeDtypeStruct(Qa.shape, F32), part, part, part, part, part, part, cshape, cshape,
                   jax.ShapeDtypeStruct((nkv, 8, LANES), F32)],
        compiler_params=_params(2))(sink, _att_band(Lc), Qa, Kd, Kd, Kd, Vd, Vd, Vd, Kdc, Vdc, dY)


def _scale_rows(dx, gt, saved, name):
    M, D = dx.shape
    tm = _tile(M, 512, 8)

    def body(dx_ref, gt_ref, sv_ref, dz_ref, dgt_ref):
        @pl.when(pl.program_id(0) == 0)
        def _():
            dgt_ref[...] = jnp.zeros_like(dgt_ref)
        d = dx_ref[...]
        dz_ref[...] = (d * gt_ref[...]).astype(BF16)
        dgt_ref[...] += jnp.sum(d * sv_ref[...].astype(F32), axis=0, keepdims=True)

    row = pl.BlockSpec((tm, D), lambda i: (i, 0))
    vec = pl.BlockSpec((1, D), lambda i: (0, 0))
    return pl.pallas_call(
        body, name=name, grid=(M // tm,), in_specs=[row, vec, row], out_specs=[row, vec],
        out_shape=[jax.ShapeDtypeStruct((M, D), BF16), jax.ShapeDtypeStruct((1, D), F32)],
        compiler_params=_params(1))(dx, gt, saved)


def _bwd_proj(dz, w, G=None, U=None, name="bwd_proj"):
    M, D = dz.shape
    N = w.shape[0]
    swiglu = G is not None
    tm, tn = _tile(M, ROWS_PER_LATCH, 8), _tile(N, 512)

    def body(*refs):
        if swiglu:
            dz_ref, w_ref, G_ref, U_ref, dG_ref, dU_ref = refs
        else:
            dz_ref, w_ref, dA_ref = refs
        dA = _nt(dz_ref[...], w_ref[...])
        if swiglu:
            Gv = G_ref[...].astype(F32)
            Uv = U_ref[...].astype(F32)
            sg = _sigmoid(Gv)
            dU_ref[...] = (dA * Gv * sg).astype(BF16)
            dG_ref[...] = (dA * Uv * (sg * (1.0 + Gv * (1.0 - sg)))).astype(BF16)
        else:
            dA_ref[...] = dA.astype(BF16)

    row = pl.BlockSpec((tm, D), lambda i, j: (i, 0))
    tile = pl.BlockSpec((tm, tn), lambda i, j: (i, j))
    big = jax.ShapeDtypeStruct((M, N), BF16)
    in_specs = [row, pl.BlockSpec((tn, D), lambda i, j: (j, 0))]
    args = [dz, w]
    if swiglu:
        in_specs += [tile, tile]
        args += [G, U]
        out_specs, out_shape = [tile, tile], [big, big]
    else:
        out_specs, out_shape = tile, big
    return pl.pallas_call(
        body, name=name, grid=(M // tm, N // tn), in_specs=in_specs, out_specs=out_specs, out_shape=out_shape,
        compiler_params=_params(2))(*args)


def _tn_matmul(pairs, name):
    Ka, Nb = pairs[0][0].shape[1], pairs[0][1].shape[1]
    tk, tn = _tile(Ka, 2048), _tile(Nb, 2048)
    tls, nks = [], []
    for a, _ in pairs:
        tl = _tile(a.shape[0], 1024, 8)
        tls.append(tl)
        nks.append(a.shape[0] // tl)
    starts = [int(s) for s in np.cumsum([0] + nks[:-1])]
    nk = int(sum(nks))

    def body(*refs):
        out_ref, acc = refs[-2], refs[-1]
        k = pl.program_id(2)

        @pl.when(k == 0)
        def _():
            acc[...] = jnp.zeros_like(acc)

        for idx in range(len(pairs)):
            a_ref, b_ref = refs[2 * idx], refs[2 * idx + 1]

            @pl.when(jnp.logical_and(k >= starts[idx], k < starts[idx] + nks[idx]))
            def _():
                acc[...] += _tn(a_ref[...], b_ref[...])

        @pl.when(k == nk - 1)
        def _():
            out_ref[...] = acc[...].astype(BF16)

    in_specs, args = [], []
    for idx, (a, b) in enumerate(pairs):
        s0, n_ = starts[idx], nks[idx]
        in_specs.append(pl.BlockSpec((tls[idx], tk), lambda i, j, k, s0=s0, n_=n_: (jnp.clip(k - s0, 0, n_ - 1), i)))
        in_specs.append(pl.BlockSpec((tls[idx], tn), lambda i, j, k, s0=s0, n_=n_: (jnp.clip(k - s0, 0, n_ - 1), j)))
        args += [a, b]
    return pl.pallas_call(
        body, name=name, grid=(Ka // tk, Nb // tn, nk), in_specs=in_specs,
        out_specs=pl.BlockSpec((tk, tn), lambda i, j, k: (i, j)),
        out_shape=jax.ShapeDtypeStruct((Ka, Nb), BF16),
        scratch_shapes=[pltpu.VMEM((tk, tn), F32)], compiler_params=_params(3))(*args)


def _bwd_norm_mod(pairs, x, dres, g, sh, sc, name):
    M, D = x.shape
    K = pairs[0][0].shape[1]
    tm, tk = _tile(M, 512, 8), _tile(K, 1152 if len(pairs) == 1 else 512)
    nk = K // tk
    npair = len(pairs)
    has_res = dres is not None

    def body(*refs):
        pr = refs[:2 * npair]
        rest = refs[2 * npair:]
        if has_res:
            x_ref, dres_ref, g_ref, sh_ref, sc_ref, dx_ref, st_ref, acc = rest
        else:
            x_ref, g_ref, sh_ref, sc_ref, dx_ref, st_ref, acc = rest
        del sh_ref
        i, k = pl.program_id(0), pl.program_id(1)

        @pl.when(jnp.logical_and(i == 0, k == 0))
        def _():
            st_ref[...] = jnp.zeros_like(st_ref)

        @pl.when(k == 0)
        def _():
            acc[...] = jnp.zeros_like(acc)

        t = _nt(pr[1][...], pr[0][...])
        for idx in range(1, npair):
            t = t + _nt(pr[2 * idx + 1][...], pr[2 * idx][...])
        acc[...] += t

        @pl.when(k == nk - 1)
        def _():
            xv = x_ref[...]
            gv = g_ref[...]
            dh = acc[...].T
            r = lax.rsqrt(jnp.mean(xv * xv, axis=-1, keepdims=True) + NORM_EPS)
            xh = xv * r
            st_ref[0:1, :] += jnp.sum(dh, axis=0, keepdims=True)
            st_ref[1:2, :] += jnp.sum(dh * (xh * gv), axis=0, keepdims=True)
            dn = dh * (1.0 + sc_ref[...])
            st_ref[2:3, :] += jnp.sum(dn * xh, axis=0, keepdims=True)
            dxh = dn * gv
            d = r * (dxh - xh * jnp.mean(dxh * xh, axis=-1, keepdims=True))
            if has_res:
                d = d + dres_ref[...]
            dx_ref[...] = d

    row = pl.BlockSpec((tm, D), lambda i, k: (i, 0))
    vec = pl.BlockSpec((1, D), lambda i, k: (0, 0))
    in_specs, args = [], []
    for dA, w in pairs:
        in_specs += [pl.BlockSpec((tm, tk), lambda i, k: (i, k)), pl.BlockSpec((D, tk), lambda i, k: (0, k))]
        args += [dA, w]
    in_specs += [row] + ([row] if has_res else []) + [vec, vec, vec]
    args += [x] + ([dres] if has_res else []) + [g, sh, sc]
    return pl.pallas_call(
        body, name=name, grid=(M // tm, nk), in_specs=in_specs,
        out_specs=[row, pl.BlockSpec((8, D), lambda i, k: (0, 0))],
        out_shape=[jax.ShapeDtypeStruct((M, D), F32), jax.ShapeDtypeStruct((8, D), F32)],
        scratch_shapes=[pltpu.VMEM((D, tm), F32)], compiler_params=_params(2))(*args)


def _local_step(x, ctx, tgt, mod, modc, norm_mix, norm_ffn, norm_final, lg, sink, T, w_in, rest_weights, on_grads, flush):
    L, D = x.shape
    Lc = ctx.shape[0]
    d_proj = w_in.shape[1]
    npairs = RET_HEADS // 2
    nkv = ATT_KV_HEADS
    nkvp = nkv // 2
    o_rq = 0
    o_rk = o_rq + RET_HEADS * RET_DK // LANES
    o_rv = o_rk + RET_HEADS * RET_DK // LANES
    o_rg = o_rv + RET_HEADS * RET_DV // LANES
    o_aq = o_rg + RET_HEADS * RET_DV // LANES
    o_ak = o_aq + ATT_HEADS * ATT_DH // LANES
    o_av = o_ak + nkv * ATT_DH // LANES
    assert (o_av + nkv * ATT_DH // LANES) * LANES == d_proj
    assert o_rv % 2 == 0 and o_rg % 2 == 0 and (RET_HEADS * RET_DV) % (2 * LANES) == 0
    rv_blk, rg_blk = o_rv // 2, o_rg // 2
    d_ret = RET_HEADS * RET_DV
    d_mix = d_ret + ATT_HEADS * ATT_DH
    att_blk = d_ret // (2 * LANES)
    k_scale = RET_DK ** -0.5
    a_scale = ATT_DH ** -0.5

    Tc = dict(C=jnp.ones((Lc, LANES), F32), S=jnp.zeros((Lc, LANES), F32))
    row = lambda m, i: m[i:i + 1]
    sh_m, sc_m, gt_m, sh_f, sc_f, gt_f = [row(mod, i) for i in range(6)]
    sh_mc, sc_mc = row(modc, 0), row(modc, 1)

    P, hx = _norm_mod_matmul(x, norm_mix, sh_m, sc_m, w_in, "in_proj")
    Pc, hc = _norm_mod_matmul(ctx, norm_mix, sh_mc, sc_mc, w_in, "in_proj_ctx")
    nq = RET_HEADS * RET_DK // LANES
    Qr = _rope_cols(P, o_rq, nq, T["Cr"], T["Sr"], T["Rr"], 1.0, True, "rope_rq")
    Kr = _rope_cols(P, o_rk, nq, T["Cr"], T["Sr"], T["Rr"], k_scale, True, "rope_rk")
    Krc = _rope_cols(Pc, o_rk, nq, Tc["C"], Tc["S"], T["Rr"], k_scale, False, "scale_rk_ctx")
    Qa = _rope_cols(P, o_aq, ATT_HEADS * ATT_DH // LANES, T["Ca"], T["Sa"], T["Ra"], a_scale, True, "rope_aq")
    Kd = _dup_heads(P, o_ak, nkvp, T["Ca"], T["Sa"], T["Ra"], T["D0"], T["D1"], True, "dup_ak")
    Vd = _dup_heads(P, o_av, nkvp, T["Ca"], T["Sa"], T["Ra"], T["D0"], T["D1"], False, "dup_av")
    Kdc = _dup_heads(Pc, o_ak, nkvp, Tc["C"], Tc["S"], T["Ra"], T["D0"], T["D1"], False, "dup_ak_ctx")
    Vdc = _dup_heads(Pc, o_av, nkvp, Tc["C"], Tc["S"], T["Ra"], T["D0"], T["D1"], False, "dup_av_ctx")

    SF = _ret_states_fwd(Kr, P, Krc, Pc, lg, rv_blk, npairs)
    Y, SB = _ret_out_fwd(Qr, Kr, P, Krc, Pc, SF, lg, rv_blk, rg_blk, npairs, d_mix)
    Y = _att_fwd(Qa, Kd, Vd, Kdc, Vdc, sink, Y, att_blk)

    w_out, w_gate, w_up, w_down = rest_weights(Y)
    x1, O1 = _proj_residual(Y, w_out, x, gt_m, "out_proj")
    G, U, A, h2 = _ffn_in(x1, norm_ffn, sh_f, sc_f, w_gate, w_up)
    x2, Fo = _proj_residual(A, w_down, x1, gt_f, "ffn_out")
    dx2, loss, d_norm_final, dz2, dgt_f = _final(x2, norm_final, tgt, gt_f, Fo)

    dG, dU = _bwd_proj(dz2, w_down, G, U, name="ffn_out_bwd")
    g_w_down = _tn_matmul([(A, dz2)], "grad_w_down")
    tok = on_grads(["w_down"], [g_w_down])
    dx1, st_f = _bwd_norm_mod([(dG, w_gate), (dU, w_up)], x1, dx2, norm_ffn + tok, sh_f, sc_f, "ffn_in_bwd")
    tok = flush(dx1)
    g_w_gate = _tn_matmul([(h2, dG)], "grad_w_gate")
    g_w_up = _tn_matmul([(h2, dU)], "grad_w_up")
    tok = tok + on_grads(["w_gate", "w_up"], [g_w_gate, g_w_up])
    dz1, dgt_m = _scale_rows(dx1, gt_m + tok, O1, "mix_gate_bwd")
    dY = _bwd_proj(dz1, w_out, name="out_proj_bwd")
    tok = flush(dY)
    g_w_out = _tn_matmul([(Y, dz1)], "grad_w_out")
    tok = tok + on_grads(["w_out"], [g_w_out])

    dQa, dKp, dKs, dKn, dVp, dVs, dVn, dKdc, dVdc, dsink = _att_bwd(Qa, Kd, Vd, Kdc, Vdc, sink + tok, dY, att_blk)
    tok = flush(dQa)
    dQr, dKr, dVr, dP, dO, dKc, dVc, acc1 = _ret_bwd1(Qr, Kr, P, Krc, Pc, SF, SB, dY, lg + tok, rv_blk, rg_blk, npairs,
                                                      d_proj)
    dKr, dP, dKc, dVc, acc2 = _ret_bwd2(Qr, Kr, P, Krc, Pc, SB, dO, dKr, dVr, dP, dKc, dVc, lg, rv_blk, npairs)

    dP = _unrope_cols(dQr, dP, o_rq, nq, T["Cr"], T["Sr"], T["RrT"], 1.0, True, "unrope_rq")
    dP = _unrope_cols(dKr, dP, o_rk, nq, T["Cr"], T["Sr"], T["RrT"], k_scale, True, "unrope_rk")
    dP = _unrope_cols(dQa, dP, o_aq, ATT_HEADS * ATT_DH // LANES, T["Ca"], T["Sa"], T["RaT"], a_scale, True, "unrope_aq")
    dP = _fold_heads([(dKs, 0), (dKp, 1), (dKn, -1)], dP, o_ak, nkvp, T["Ca"], T["Sa"], T["RaT"], T["D0T"], T["D1T"],
                     True, "fold_ak")
    dP = _fold_heads([(dVs, 0), (dVp, 1), (dVn, -1)], dP, o_av, nkvp, T["Ca"], T["Sa"], T["RaT"], T["D0T"], T["D1T"],
                     False, "fold_av")
    dPc = jnp.zeros((Lc, d_proj), BF16)
    dPc = _unrope_cols(dKc, dPc, o_rk, nq, Tc["C"], Tc["S"], T["RrT"], k_scale, False, "ctx_rk_bwd")
    dPc = _unrope_cols(dVc, dPc, o_rv, RET_HEADS * RET_DV // LANES, Tc["C"], Tc["S"], T["RrT"], 1.0, False, "ctx_rv_bwd")
    dPc = _fold_heads([(dKdc.astype(BF16), 0)], dPc, o_ak, nkvp, Tc["C"], Tc["S"], T["RaT"], T["D0T"], T["D1T"],
                      False, "fold_ak_ctx")
    dPc = _fold_heads([(dVdc.astype(BF16), 0)], dPc, o_av, nkvp, Tc["C"], Tc["S"], T["RaT"], T["D0T"], T["D1T"],
                      False, "fold_av_ctx")

    dx, st_m = _bwd_norm_mod([(dP, w_in)], x, dx1, norm_mix, sh_m, sc_m, "in_proj_bwd")
    _, st_mc = _bwd_norm_mod([(dPc, w_in)], ctx, None, norm_mix, sh_mc, sc_mc, "in_proj_ctx_bwd")
    g_w_in = _tn_matmul([(hx, dP), (hc, dPc)], "grad_w_in")
    on_grads(["w_in"], [g_w_in])

    a1 = acc1[:, :, :, 0].reshape(RET_HEADS, ACC_ROWS)
    a2 = acc2[:, :, :, 0].reshape(RET_HEADS, ACC_ROWS)
    dlam = (a1[:, 0] + a1[:, 2] + a1[:, 3] + a1[:, 4]) * lg[0]
    dmu = (a1[:, 1] + a1[:, 5] + a2[:, 0] + a2[:, 1]) * lg[1]
    d_sink = dsink[:, :4, 0].reshape(1, ATT_HEADS)

    nh = RET_HEADS
    assert 2 * nh + ATT_HEADS <= LOSS_LANE
    small = _pack_rows(
        [(st_m, 0, 2, 0, 0), (dgt_m, 0, 1, 2, 0), (st_f, 0, 2, 3, 0), (dgt_f, 0, 1, 5, 0), (st_mc, 0, 2, 6, 0),
         (st_m[2:3] + st_mc[2:3], 0, 1, 12, 0), (st_f, 2, 1, 13, 0), (d_norm_final, 0, 1, 14, 0),
         (dlam.reshape(1, nh), 0, 1, 15, 0), (dmu.reshape(1, nh), 0, 1, 15, nh), (d_sink, 0, 1, 15, 2 * nh),
         (loss[:, 0:1], 0, 1, 15, LOSS_LANE)], 16, D, "pack_small")
    return dict(grad_x=dx, small=small)


def _my_pos():
    return lax.axis_index("x"), lax.axis_index("y"), lax.axis_index("c")


def _other_chips(x, y):
    return [(1 - x, y), (x, 1 - y), (1 - x, 1 - y)]


def _remote(src, dst, ssem, rsem, dev):
    return pltpu.make_async_remote_copy(src_ref=src, dst_ref=dst, send_sem=ssem, recv_sem=rsem,
                                        device_id=dev, device_id_type=MESH)


def _allgather8(v, name):
    R, Cc = v.shape

    def body(v_ref, out_ref, send_sems, recv_sems):
        x, y, c = _my_pos()
        me = 4 * x + 2 * y + c
        out_ref[pl.ds(me, 1)] = v_ref[...][None]
        peers = []
        for j in range(1, N_DEV):
            peers.append((1 - x if (j >> 2) & 1 else x, 1 - y if (j >> 1) & 1 else y, 1 - c if j & 1 else c))
        copies = []
        for j, peer in enumerate(peers):
            cp = _remote(v_ref, out_ref.at[me], send_sems.at[j], recv_sems.at[j], peer)
            cp.start()
            copies.append(cp)
        for j, peer in enumerate(peers):
            pid = 4 * peer[0] + 2 * peer[1] + peer[2]
            _remote(v_ref, out_ref.at[pid], send_sems.at[j], recv_sems.at[j], peer).wait_recv()
        for cp in copies:
            cp.wait_send()

    return pl.pallas_call(
        body, name=name, out_shape=jax.ShapeDtypeStruct((N_DEV, R, Cc), v.dtype),
        in_specs=[pl.BlockSpec(memory_space=pltpu.VMEM)], out_specs=pl.BlockSpec(memory_space=pltpu.VMEM),
        scratch_shapes=[pltpu.SemaphoreType.DMA((N_DEV - 1,)), pltpu.SemaphoreType.DMA((N_DEV - 1,))])(v)


def _region(ref, k, half, shard_shape, axis):
    r, cs = shard_shape
    hr = r // 2
    if axis == 1:
        return ref.at[pl.ds(pl.multiple_of(half * hr, 16), hr), pl.ds(pl.multiple_of(k * cs, LANES), cs)]
    return ref.at[pl.ds(pl.multiple_of(k * r + half * hr, 16), hr), :]


def _full_shape(shard_shape, axis):
    r, cs = shard_shape
    return (r, N_CHIPS * cs) if axis == 1 else (N_CHIPS * r, cs)


def _half_pieces(ref, half, shard_shape, axis):
    r, cs = shard_shape
    hr = r // 2
    if axis == 1:
        return [ref.at[pl.ds(pl.multiple_of(half * hr, 16), hr), :]]
    return [ref.at[pl.ds(pl.multiple_of(k * r + half * hr, 16), hr), :] for k in range(N_CHIPS)]


def _half_block_spec(shard_shape, axis, tr):
    r, cs = shard_shape
    hr = r // 2
    if axis == 1:
        return pl.BlockSpec((tr, cs), lambda k, i, c_ref: (c_ref[0] * (hr // tr) + i, k))
    return pl.BlockSpec((tr, cs), lambda k, i, c_ref: (k * (r // tr) + c_ref[0] * (hr // tr) + i, 0))


def _add_halves(g, recv, cvec, shard_shape, axis, name):
    r, cs = shard_shape
    hr = r // 2
    tr = _tile(hr, 256, 16)

    def body(c_ref, a_ref, b_ref, o_ref):
        del c_ref
        o_ref[0] = (a_ref[...].astype(F32) + b_ref[...].astype(F32)).astype(BF16)

    spec = _half_block_spec(shard_shape, axis, tr)
    return pl.pallas_call(
        body, name=name,
        grid_spec=pltpu.PrefetchScalarGridSpec(
            num_scalar_prefetch=1, grid=(N_CHIPS, hr // tr), in_specs=[spec, spec],
            out_specs=pl.BlockSpec((1, tr, cs), lambda k, i, c_ref: (k, i, 0))),
        out_shape=jax.ShapeDtypeStruct((N_CHIPS, hr, cs), BF16),
        compiler_params=_params(2, False))(cvec, g, recv)


def _sum_chips(sums, landed, kc, name):
    _, hr, cs = sums.shape
    tr = _tile(hr, 256, 16)

    def body(kc_ref, own_ref, a_ref, b_ref, c_ref, o_ref):
        del kc_ref
        o_ref[...] = (own_ref[0].astype(F32) + a_ref[0].astype(F32)) + (b_ref[0].astype(F32) + c_ref[0].astype(F32))

    slot = lambda j: pl.BlockSpec((1, tr, cs), lambda i, kc_ref: ((kc_ref[0] + j) % N_CHIPS, i, 0))
    return pl.pallas_call(
        body, name=name,
        grid_spec=pltpu.PrefetchScalarGridSpec(
            num_scalar_prefetch=1, grid=(hr // tr,), in_specs=[slot(0), slot(1), slot(2), slot(3)],
            out_specs=pl.BlockSpec((tr, cs), lambda i, kc_ref: (kc_ref[1] * (hr // tr) + i, 0))),
        out_shape=jax.ShapeDtypeStruct((2 * hr, cs), F32),
        compiler_params=_params(1, False))(kc, sums, landed, landed, landed)


def _exchange_halves(shards, name):
    nw = len(shards)

    def body(*refs):
        out_refs = refs[nw:2 * nw]
        send, recv = refs[2 * nw:]
        x, y, c = _my_pos()
        sib = (x, y, 1 - c)
        copies = []
        for w in range(nw):
            hr = shards[w].shape[0] // 2
            mine = out_refs[w].at[pl.ds(pl.multiple_of(c * hr, 8), hr), :]
            cp = _remote(mine, mine, send.at[w], recv.at[w], sib)
            cp.start()
            copies.append(cp)
        for w in range(nw):
            hr = shards[w].shape[0] // 2
            other = out_refs[w].at[pl.ds(pl.multiple_of((1 - c) * hr, 8), hr), :]
            _remote(other, other, send.at[w], recv.at[w], sib).wait_recv()
        for cp in copies:
            cp.wait_send()

    anyspec = pl.BlockSpec(memory_space=pl.ANY)
    return pl.pallas_call(
        body, name=name,
        out_shape=[jax.ShapeDtypeStruct(s.shape, F32) for s in shards],
        in_specs=[anyspec] * nw, out_specs=[anyspec] * nw,
        input_output_aliases={w: w for w in range(nw)},
        scratch_shapes=[pltpu.SemaphoreType.DMA((nw,)), pltpu.SemaphoreType.DMA((nw,))])(*shards)


def _cast_into_full(w, kc, axis, name, after=None):
    r, cs = w.shape
    tr = _tile(r, 256, 16)
    tied = after is not None

    def body(kc_ref, w_ref, *rest):
        del kc_ref
        if tied:
            rest[1][...] = w_ref[...].astype(BF16)
            rest[2][...] = jnp.zeros_like(rest[2])
        else:
            rest[0][...] = w_ref[...].astype(BF16)

    if axis == 1:
        ospec = pl.BlockSpec((tr, cs), lambda i, kc_ref: (i, kc_ref[0]))
    else:
        ospec = pl.BlockSpec((tr, cs), lambda i, kc_ref: (kc_ref[0] * (r // tr) + i, 0))
    tok = pl.BlockSpec((8, LANES), lambda i, kc_ref: (0, 0))
    full = jax.ShapeDtypeStruct(_full_shape((r, cs), axis), BF16)
    in_specs, args = [pl.BlockSpec((tr, cs), lambda i, kc_ref: (i, 0))], [w]
    if tied:
        in_specs.append(tok)
        args.append(after)
    return pl.pallas_call(
        body, name=name,
        grid_spec=pltpu.PrefetchScalarGridSpec(num_scalar_prefetch=1, grid=(r // tr,), in_specs=in_specs,
                                               out_specs=[ospec, tok] if tied else ospec),
        out_shape=[full, jax.ShapeDtypeStruct((8, LANES), F32)] if tied else full,
        compiler_params=_params(1, False))(kc, *args)


def _adam_math(w, g, m, v):
    m2 = ADAM_B1 * m + (1.0 - ADAM_B1) * g
    v2 = ADAM_B2 * v + (1.0 - ADAM_B2) * (g * g)
    m_hat = m2 / (1.0 - ADAM_B1 ** ADAM_STEP)
    v_hat = v2 / (1.0 - ADAM_B2 ** ADAM_STEP)
    delta = -ADAM_LR * (m_hat / (jnp.sqrt(v_hat) + ADAM_EPS) + ADAM_WD * w)
    return delta, m2, v2


def _adam(w, g, m, v, name):
    r, cs = w.shape
    tr = _tile(r, 256, 8)

    def body(w_ref, g_ref, m_ref, v_ref, d_ref, m2_ref, v2_ref):
        d, m2, v2 = _adam_math(w_ref[...], g_ref[...], m_ref[...], v_ref[...])
        d_ref[...] = d
        m2_ref[...] = m2
        v2_ref[...] = v2

    spec = pl.BlockSpec((tr, cs), lambda i: (i, 0))
    shp = jax.ShapeDtypeStruct((r, cs), F32)
    return pl.pallas_call(body, name=name, grid=(r // tr,), in_specs=[spec] * 4, out_specs=[spec] * 3,
                          out_shape=[shp, shp, shp], compiler_params=_params(1, False))(w, g, m, v)


def _mod_rows(a16, w, b, name):
    D, n = w.shape
    tn = _tile(n, 512)

    def body(a_ref, w_ref, b_ref, o_ref):
        a = a_ref[...]
        o_ref[...] = _nn((a * _sigmoid(a)).astype(BF16), w_ref[...].astype(BF16)) + b_ref[...]

    return pl.pallas_call(
        body, name=name, grid=(n // tn,),
        in_specs=[pl.BlockSpec((16, D), lambda j: (0, 0)), pl.BlockSpec((D, tn), lambda j: (0, j)),
                  pl.BlockSpec((1, tn), lambda j: (0, j))],
        out_specs=pl.BlockSpec((16, tn), lambda j: (0, j)),
        out_shape=jax.ShapeDtypeStruct((16, n), F32), compiler_params=_params(1, False))(a16, w, b)


def _w_mod_update(a16, d16, w, m, v):
    D, n = w.shape
    tn = _tile(n, 256)

    def body(a_ref, d_ref, w_ref, m_ref, v_ref, g_ref, dl_ref, m2_ref, v2_ref, p_ref):
        @pl.when(pl.program_id(0) == 0)
        def _():
            p_ref[...] = jnp.zeros_like(p_ref)
        a = a_ref[...]
        db = d_ref[...].astype(BF16)
        wv = w_ref[...]
        g = _tn((a * _sigmoid(a)).astype(BF16), db)
        g_ref[...] = g
        d, m2, v2 = _adam_math(wv, g, m_ref[...], v_ref[...])
        dl_ref[...] = d
        m2_ref[...] = m2
        v2_ref[...] = v2
        p_ref[...] += _nt(db, wv.astype(BF16))

    wspec = pl.BlockSpec((D, tn), lambda j: (0, j))
    shp = jax.ShapeDtypeStruct((D, n), F32)
    return pl.pallas_call(
        body, name="w_mod_update", grid=(n // tn,),
        in_specs=[pl.BlockSpec((16, D), lambda j: (0, 0)), pl.BlockSpec((16, tn), lambda j: (0, j)), wspec, wspec, wspec],
        out_specs=[wspec, wspec, wspec, wspec, pl.BlockSpec((16, D), lambda j: (0, 0))],
        out_shape=[shp, shp, shp, shp, jax.ShapeDtypeStruct((16, D), F32)],
        compiler_params=_params(1))(a16, d16, w, m, v)


def _sum_devices(g8, name):
    _, R, Cc = g8.shape

    def body(g_ref, o_ref):
        t = g_ref[0]
        for d in range(1, N_DEV):
            t = t + g_ref[d]
        o_ref[...] = t

    return pl.pallas_call(body, name=name, out_shape=jax.ShapeDtypeStruct((R, Cc), F32))(g8)


def _c_ctx_grad(parts, c_ctx):
    D = c_ctx.shape[1]

    def body(p_ref, c_ref, o_ref):
        t = p_ref[0]
        for k in range(1, N_CHIPS):
            t = t + p_ref[2 * k]
        cv = c_ref[...]
        sg = _sigmoid(cv)
        o_ref[...] = t * (sg * (1.0 + cv * (1.0 - sg)))

    return pl.pallas_call(body, name="c_ctx_grad", out_shape=jax.ShapeDtypeStruct((1, D), F32))(parts, c_ctx)


def _pack_rows(items, nrows, width, name):
    arrays, plan = [], []
    for a, r0, nr, d0, c0 in items:
        for ai, b in enumerate(arrays):
            if b is a:
                break
        else:
            ai = len(arrays)
            arrays.append(a)
        plan.append((ai, r0, nr, d0, c0, a.shape[1]))

    def body(*refs):
        o_ref = refs[-1]
        o_ref[...] = jnp.zeros_like(o_ref)
        for ai, r0, nr, d0, c0, w in plan:
            o_ref[d0:d0 + nr, c0:c0 + w] = refs[ai][r0:r0 + nr, :]

    return pl.pallas_call(body, name=name, out_shape=jax.ShapeDtypeStruct((nrows, width), F32))(*arrays)


HBM_SPEC = pl.BlockSpec(memory_space=pltpu.HBM)
SEM_SPEC = pl.BlockSpec(memory_space=pltpu.SEMAPHORE)
SPLIT_PARAMS = pltpu.CompilerParams(has_side_effects=pltpu.SideEffectType.DATAFLOW_SIDE_EFFECTING)


def _in_hbm(a):
    return pltpu.with_memory_space_constraint(a, pltpu.HBM)


def _ag_chips_start(fulls, shapes, axes, after, name):
    nw = len(fulls)

    def body(*refs):
        in_refs, send, recv, token = refs[:nw], refs[nw + 1], refs[nw + 2], refs[-1]
        x, y, c = _my_pos()
        k0 = 2 * x + y
        for w in range(nw):
            own = _region(in_refs[w], k0, c, shapes[w], axes[w])
            for j, ch in enumerate(_other_chips(x, y)):
                _remote(own, own, send.at[3 * w + j], recv.at[3 * w + j], (ch[0], ch[1], c)).start()
        token[...] = jnp.zeros_like(token)

    return pl.pallas_call(
        body, name=name,
        out_shape=(pltpu.SemaphoreType.DMA((3 * nw,)), pltpu.SemaphoreType.DMA((3 * nw,)),
                   *[pltpu.HBM(f.shape, f.dtype) for f in fulls], jax.ShapeDtypeStruct((8, LANES), F32)),
        in_specs=[HBM_SPEC] * nw + [pl.BlockSpec(memory_space=pl.ANY)],
        out_specs=(SEM_SPEC, SEM_SPEC, *[HBM_SPEC] * nw, pl.BlockSpec(memory_space=pltpu.VMEM)),
        input_output_aliases={w: 2 + w for w in range(nw)},
        compiler_params=SPLIT_PARAMS)(*[_in_hbm(f) for f in fulls], after)


def _ag_chips_wait(send, recv, fulls, shapes, axes, after, name):
    nw = len(fulls)

    def body(*refs):
        in_refs, send_ref, recv_ref = refs[:nw], refs[nw], refs[nw + 1]
        x, y, c = _my_pos()
        k0 = 2 * x + y
        for w in range(nw):
            own = _region(in_refs[w], k0, c, shapes[w], axes[w])
            for j, ch in enumerate(_other_chips(x, y)):
                got = _region(in_refs[w], 2 * ch[0] + ch[1], c, shapes[w], axes[w])
                cp = _remote(own, got, send_ref.at[3 * w + j], recv_ref.at[3 * w + j], (ch[0], ch[1], c))
                cp.wait_send()
                cp.wait_recv()

    return pl.pallas_call(
        body, name=name,
        out_shape=tuple(pltpu.HBM(f.shape, f.dtype) for f in fulls),
        in_specs=[HBM_SPEC] * nw + [SEM_SPEC, SEM_SPEC, pl.BlockSpec(memory_space=pl.ANY)],
        out_specs=tuple([HBM_SPEC] * nw),
        input_output_aliases={w: w for w in range(nw)},
        compiler_params=SPLIT_PARAMS)(*fulls, send, recv, after)


def _ag_forward(fulls, shapes, axes, name):
    nw = len(fulls)

    def body(*refs):
        out_refs = refs[nw:2 * nw]
        send, recv = refs[2 * nw:]
        x, y, c = _my_pos()
        sib = (x, y, 1 - c)
        chips = _other_chips(x, y)
        copies = []
        for w in range(nw):
            for j, ch in enumerate(chips):
                got = _region(out_refs[w], 2 * ch[0] + ch[1], c, shapes[w], axes[w])
                cp = _remote(got, got, send.at[w, j], recv.at[w, j], sib)
                cp.start()
                copies.append(cp)
        for w in range(nw):
            for j, ch in enumerate(chips):
                got = _region(out_refs[w], 2 * ch[0] + ch[1], 1 - c, shapes[w], axes[w])
                _remote(got, got, send.at[w, j], recv.at[w, j], sib).wait_recv()
        for cp in copies:
            cp.wait_send()

    anyspec = pl.BlockSpec(memory_space=pl.ANY)
    return pl.pallas_call(
        body, name=name,
        out_shape=[jax.ShapeDtypeStruct(f.shape, BF16) for f in fulls],
        in_specs=[anyspec] * nw, out_specs=[anyspec] * nw,
        input_output_aliases={w: w for w in range(nw)},
        scratch_shapes=[pltpu.SemaphoreType.DMA((nw, 3)), pltpu.SemaphoreType.DMA((nw, 3))])(*fulls)


def _rs_sibling_start(grads, shapes, axes, name):
    nw = len(grads)
    npc = max(1 if a == 1 else N_CHIPS for a in axes)

    def body(*refs):
        g_refs, l_refs, send, recv, token = refs[:nw], refs[nw:2 * nw], refs[2 * nw], refs[2 * nw + 1], refs[-1]
        x, y, c = _my_pos()
        for w in range(nw):
            src = _half_pieces(g_refs[w], 1 - c, shapes[w], axes[w])
            dst = _half_pieces(l_refs[w], 1 - c, shapes[w], axes[w])
            for i, (s, d) in enumerate(zip(src, dst)):
                _remote(s, d, send.at[npc * w + i], recv.at[npc * w + i], (x, y, 1 - c)).start()
        token[...] = jnp.zeros_like(token)

    thru = [pltpu.HBM(g.shape, g.dtype) for g in grads]
    return pl.pallas_call(
        body, name=name,
        out_shape=(pltpu.SemaphoreType.DMA((npc * nw,)), pltpu.SemaphoreType.DMA((npc * nw,)), *thru, *thru,
                   jax.ShapeDtypeStruct((8, LANES), F32)),
        in_specs=[HBM_SPEC] * (2 * nw),
        out_specs=(SEM_SPEC, SEM_SPEC, *[HBM_SPEC] * (2 * nw), pl.BlockSpec(memory_space=pltpu.VMEM)),
        input_output_aliases={i: 2 + i for i in range(2 * nw)},
        compiler_params=SPLIT_PARAMS)(*[_in_hbm(g) for g in grads], *[_in_hbm(lax.empty(g.shape, g.dtype)) for g in grads])


def _rs_sibling_wait(send, recv, grads, lands, shapes, axes, after, name):
    nw = len(grads)
    npc = max(1 if a == 1 else N_CHIPS for a in axes)

    def body(*refs):
        g_refs, l_refs, send_ref, recv_ref = refs[:nw], refs[nw:2 * nw], refs[2 * nw], refs[2 * nw + 1]
        x, y, c = _my_pos()
        for w in range(nw):
            sent = _half_pieces(g_refs[w], 1 - c, shapes[w], axes[w])
            mine = _half_pieces(l_refs[w], c, shapes[w], axes[w])
            for i, (s, d) in enumerate(zip(sent, mine)):
                cp = _remote(s, d, send_ref.at[npc * w + i], recv_ref.at[npc * w + i], (x, y, 1 - c))
                cp.wait_send()
                cp.wait_recv()

    thru = tuple(pltpu.HBM(g.shape, g.dtype) for g in grads)
    return pl.pallas_call(
        body, name=name, out_shape=thru + thru,
        in_specs=[HBM_SPEC] * (2 * nw) + [SEM_SPEC, SEM_SPEC, pl.BlockSpec(memory_space=pl.ANY)],
        out_specs=tuple([HBM_SPEC] * (2 * nw)),
        input_output_aliases={i: i for i in range(2 * nw)},
        compiler_params=SPLIT_PARAMS)(*grads, *lands, send, recv, after)


def _rs_chips_start(sums, name):
    nw = len(sums)

    def body(*refs):
        s_refs, l_refs, send, recv, token = refs[:nw], refs[nw:2 * nw], refs[2 * nw], refs[2 * nw + 1], refs[-1]
        x, y, c = _my_pos()
        k0 = 2 * x + y
        for w in range(nw):
            for j, ch in enumerate(_other_chips(x, y)):
                _remote(s_refs[w].at[2 * ch[0] + ch[1]], l_refs[w].at[k0], send.at[3 * w + j], recv.at[3 * w + j],
                        (ch[0], ch[1], c)).start()
        token[...] = jnp.zeros_like(token)

    thru = [pltpu.HBM(s.shape, s.dtype) for s in sums]
    return pl.pallas_call(
        body, name=name,
        out_shape=(pltpu.SemaphoreType.DMA((3 * nw,)), pltpu.SemaphoreType.DMA((3 * nw,)), *thru, *thru,
                   jax.ShapeDtypeStruct((8, LANES), F32)),
        in_specs=[HBM_SPEC] * (2 * nw),
        out_specs=(SEM_SPEC, SEM_SPEC, *[HBM_SPEC] * (2 * nw), pl.BlockSpec(memory_space=pltpu.VMEM)),
        input_output_aliases={i: 2 + i for i in range(2 * nw)},
        compiler_params=SPLIT_PARAMS)(*[_in_hbm(s) for s in sums], *[_in_hbm(lax.empty(s.shape, s.dtype)) for s in sums])


def _rs_chips_wait(send, recv, sums, lands, after, name):
    nw = len(sums)

    def body(*refs):
        s_refs, l_refs, send_ref, recv_ref = refs[:nw], refs[nw:2 * nw], refs[2 * nw], refs[2 * nw + 1]
        x, y, c = _my_pos()
        for w in range(nw):
            for j, ch in enumerate(_other_chips(x, y)):
                kj = 2 * ch[0] + ch[1]
                cp = _remote(s_refs[w].at[kj], l_refs[w].at[kj], send_ref.at[3 * w + j], recv_ref.at[3 * w + j],
                             (ch[0], ch[1], c))
                cp.wait_send()
                cp.wait_recv()

    thru = tuple(pltpu.HBM(s.shape, s.dtype) for s in sums)
    return pl.pallas_call(
        body, name=name, out_shape=thru + thru,
        in_specs=[HBM_SPEC] * (2 * nw) + [SEM_SPEC, SEM_SPEC, pl.BlockSpec(memory_space=pl.ANY)],
        out_specs=tuple([HBM_SPEC] * (2 * nw)),
        input_output_aliases={i: i for i in range(2 * nw)},
        compiler_params=SPLIT_PARAMS)(*sums, *lands, send, recv, after)


LOSS_LANE = 64


def kernel(x, c, ctx, c_ctx, w_mod, b_mod, norm_mix, norm_ffn, w_in, ret_decay, attn_sink, w_out, w_gate, w_up, w_down, norm_final, loss_target, m_c_ctx, m_w_mod, m_b_mod, m_norm_mix, m_norm_ffn, m_w_in, m_ret_decay, m_attn_sink, m_w_out, m_w_gate, m_w_up, m_w_down, m_norm_final, v_c_ctx, v_w_mod, v_b_mod, v_norm_mix, v_norm_ffn, v_w_in, v_ret_decay, v_attn_sink, v_w_out, v_w_gate, v_w_up, v_w_down, v_norm_final):
    D = x.shape[-1]
    n3 = w_mod.shape[-1]
    xi, yi, ci = _my_pos()
    b = 4 * xi + 2 * yi + ci
    k0 = 2 * xi + yi
    cvec = jnp.reshape(ci, (1,)).astype(jnp.int32)
    kc = jnp.stack([k0, ci]).astype(jnp.int32)

    dense = [("w_in", w_in[0], 1), ("w_out", w_out[0], 0), ("w_gate", w_gate[0], 1), ("w_up", w_up[0], 1),
             ("w_down", w_down[0], 0)]
    axes = [a for _, _, a in dense]
    shapes = [w.shape for _, w, _ in dense]
    c_all = _allgather8(c, "gather_c").reshape(N_DEV, D)
    c_ctx2 = c_ctx.reshape(1, D)
    a16 = _pack_rows([(c_all, 0, N_DEV, 0, 0), (c_ctx2, 0, 1, N_DEV, 0)], 16, D, "pack_cond")
    b_cols = lax.dynamic_slice_in_dim(b_mod, k0 * n3, n3, axis=1)
    mod16 = _mod_rows(a16, w_mod[0], b_cols, "mod_rows")
    mod_all = _allgather8(mod16, "gather_mod")

    own_in = _cast_into_full(dense[0][1], kc, axes[0], "cast_w_in")
    agi = _ag_chips_start([own_in], shapes[:1], axes[:1], mod_all, "ag_in_start")
    casts = [_cast_into_full(w, kc, a, "cast_" + n, after=agi[-1]) for n, w, a in dense[1:]]
    own16 = [cst[0] for cst in casts]
    tables = _rope_tables(x.shape[1])
    behind = tables["Cr"][0:8] + tables["Sr"][0:8] + tables["Ca"][0:8] + tables["Sa"][0:8] + sum(cst[1] for cst in casts)
    (f_in,) = _ag_forward(list(_ag_chips_wait(agi[0], agi[1], [agi[2]], shapes[:1], axes[:1], behind, "ag_in_wait")),
                          shapes[:1], axes[:1], "ag_in_forward")
    ag = _ag_chips_start(own16, shapes[1:], axes[1:], f_in, "ag_rest_start")
    ag_send, ag_recv, ag_thru, ag_tok = ag[0], ag[1], list(ag[2:-1]), ag[-1][0:1, 0:1]

    def rest_weights(after):
        landed_w = _ag_chips_wait(ag_send, ag_recv, ag_thru, shapes[1:], axes[1:], after, "ag_rest_wait")
        return _ag_forward(list(landed_w), shapes[1:], axes[1:], "ag_rest_forward")
    mine = jnp.stack([lax.dynamic_index_in_dim(mod_all, 2 * k + ci, 0, keepdims=False) for k in range(N_CHIPS)])
    mod = lax.dynamic_index_in_dim(mine, b, 1, keepdims=False).reshape(6, D)
    modc = mine[:, N_DEV].reshape(6, D)

    lg = -jnp.exp(ret_decay[0])

    index = {n: i for i, (n, _, _) in enumerate(dense)}
    pending, done = [], {}

    sib = []

    def finish_sibling(after):
        names, shp, axs, st = sib.pop()
        nw = len(names)
        res = _rs_sibling_wait(st[0], st[1], list(st[2:2 + nw]), list(st[2 + nw:2 + 2 * nw]), shp, axs, after,
                               "rs_sibling_wait_" + names[0])
        sums = [_add_halves(res[i], res[nw + i], cvec, s, a, "add_halves_" + n)
                for i, (s, a, n) in enumerate(zip(shp, axs, names))]
        ch = _rs_chips_start(sums, "rs_chips_start_" + names[0])
        pending.append((names, ch[0], ch[1], list(ch[2:2 + nw]), list(ch[2 + nw:2 + 2 * nw])))
        return ch[-1][0:1, 0:1]

    def on_grads(names, gs):
        ids = [index[n] for n in names]
        shp, axs = [shapes[i] for i in ids], [axes[i] for i in ids]
        st = _rs_sibling_start(gs, shp, axs, "rs_sibling_start_" + names[0])
        sib.append((names, shp, axs, st))
        return st[-1][0:1, 0:1]

    out = _local_step(x[0], ctx[0], loss_target[0], mod, modc, norm_mix + ag_tok, norm_ffn, norm_final.reshape(1, D), lg,
                      attn_sink, tables, f_in, rest_weights, on_grads, finish_sibling)

    def finish(group, after):
        names, send, recv, sums, lands = group
        res = _rs_chips_wait(send, recv, sums, lands, after, "rs_chips_wait_" + names[0])
        return [_sum_chips(res[i], res[len(names) + i], kc, "sum_chips_" + n) for i, n in enumerate(names)]

    tok_in = finish_sibling(out["grad_x"])
    assert pending[-1][0] == ["w_in"]
    rest_names = [n for g in pending[:-1] for n in g[0]]
    after_in = out["small"][0:8, 0:LANES] + tok_in
    rest_halves = [h for g in pending[:-1] for h in finish(g, after_in)]
    g_rest = dict(zip(rest_names, _exchange_halves(rest_halves, "exchange_halves_rest")))

    nh = 2 * RET_HEADS
    small_all = _allgather8(out["small"], "gather_small")
    tot = _sum_devices(small_all, "sum_small")
    g_b_mod = (tot[0:6] + tot[6:12]).reshape(1, 6 * D)
    dmodc_tot = tot[6:12].reshape(1, 6 * D)
    dmod_rows = small_all[:, 0:6].reshape(N_DEV, 6 * D)
    d16 = _pack_rows([(dmod_rows, 0, N_DEV, 0, 0), (dmodc_tot, 0, 1, N_DEV, 0)], 16, 6 * D, "pack_dmod")
    d16 = lax.dynamic_slice_in_dim(d16, k0 * n3, n3, axis=1)
    g_w_mod, dl_w_mod, m2_w_mod, v2_w_mod, part = _w_mod_update(a16, d16, w_mod[0], m_w_mod[0], v_w_mod[0])
    part_all = _allgather8(part[N_DEV:N_DEV + 1], "gather_c_ctx")
    g_c_ctx = _c_ctx_grad(part_all, c_ctx2)
    loss = tot[15, LOSS_LANE]

    def pack(cc, bm, nm, nf, nfin, rd, sk, name):
        rd2 = rd.reshape(2, RET_HEADS)
        return _pack_rows([(bm.reshape(6, D), 0, 6, 0, 0), (cc.reshape(1, D), 0, 1, 6, 0), (nm.reshape(1, D), 0, 1, 7, 0),
                           (nf.reshape(1, D), 0, 1, 8, 0), (nfin.reshape(1, D), 0, 1, 9, 0),
                           (rd2, 0, 1, 10, 0), (rd2, 1, 1, 10, RET_HEADS), (sk.reshape(1, ATT_HEADS), 0, 1, 10, nh)],
                          16, D, name)

    w_s = pack(c_ctx, b_mod, norm_mix, norm_ffn, norm_final, ret_decay, attn_sink, "pack_w")
    g_s = _pack_rows([(g_b_mod.reshape(6, D), 0, 6, 0, 0), (g_c_ctx, 0, 1, 6, 0), (tot, 12, 3, 7, 0),
                      (tot[15:16, 0:nh + ATT_HEADS], 0, 1, 10, 0)], 16, D, "pack_g")
    m_s = pack(m_c_ctx, m_b_mod, m_norm_mix, m_norm_ffn, m_norm_final, m_ret_decay, m_attn_sink, "pack_m")
    v_s = pack(v_c_ctx, v_b_mod, v_norm_mix, v_norm_ffn, v_norm_final, v_ret_decay, v_attn_sink, "pack_v")
    small_upd = _adam(w_s, g_s, m_s, v_s, "adam_small")

    def unpack(t):
        return dict(b_mod=t[0:6].reshape(1, 6 * D), c_ctx=t[6], norm_mix=t[7:8], norm_ffn=t[8:9], norm_final=t[9],
                    ret_decay=t[10, :nh].reshape(1, 2, RET_HEADS), attn_sink=t[10, nh:nh + ATT_HEADS].reshape(1, ATT_HEADS))

    dense_w = dict(w_in=(w_in, m_w_in, v_w_in), w_out=(w_out, m_w_out, v_w_out), w_gate=(w_gate, m_w_gate, v_w_gate),
                   w_up=(w_up, m_w_up, v_w_up), w_down=(w_down, m_w_down, v_w_down))
    grads = dict(unpack(g_s), w_mod=g_w_mod[None])
    upd = [dict(unpack(t)) for t in small_upd]
    upd[0]["w_mod"], upd[1]["w_mod"], upd[2]["w_mod"] = dl_w_mod[None], m2_w_mod[None], v2_w_mod[None]
    def update(n, g):
        w_, m_, v_ = dense_w[n]
        res = _adam(w_[0], g, m_[0], v_[0], "adam_" + n)
        grads[n] = g[None]
        for u, r_ in zip(upd, res):
            u[n] = r_[None]
        return res[0]

    dep = small_upd[0][0:1, 0:1] + dl_w_mod[0:1, 0:1]
    for n in rest_names:
        dep = dep + update(n, g_rest[n])[0:1, 0:1]
    (g_in,) = _exchange_halves(finish(pending[-1], dep), "exchange_halves_in")
    update("w_in", g_in)

    order = ['c_ctx', 'w_mod', 'b_mod', 'norm_mix', 'norm_ffn', 'w_in', 'ret_decay', 'attn_sink', 'w_out', 'w_gate',
             'w_up', 'w_down', 'norm_final']
    outs = [loss, out["grad_x"][None]] + [grads[n] for n in order]
    for u in upd:
        outs += [u[n] for n in order]
    return tuple(outs)
```

```python
import numpy as np
import jax
import jax.numpy as jnp
from jax import lax
from jax.experimental import pallas as pl
from jax.experimental.pallas import tpu as pltpu

F32 = jnp.float32
BF16 = jnp.bfloat16

RET_HEADS = 8
RET_DK = 64
RET_DV = 128
CHUNK = 128
ATT_HEADS = 16
ATT_KV_HEADS = 4
ATT_DH = 64
GRID_W = 64
ROPE_BASE = 10000.0
NORM_EPS = 1e-6
ADAM_LR = 0.001
ADAM_B1 = 0.9
ADAM_B2 = 0.999
ADAM_EPS = 1e-08
ADAM_WD = 0.01
ADAM_STEP = 10
NEG = -1e30
LANES = 128
VMEM_LIMIT = 56 * 1024 * 1024
ROWS_PER_LATCH = 1024
MESH = pl.DeviceIdType.MESH
N_CHIPS = 4
N_DEV = 8


def _nn(a, b):
    return jnp.dot(a, b, preferred_element_type=F32)


def _nt(a, b):
    return lax.dot_general(a, b, (((1,), (1,)), ((), ())), preferred_element_type=F32)


def _tn(a, b):
    return lax.dot_general(a, b, (((0,), (0,)), ((), ())), preferred_element_type=F32)


def _tile(n, pref, unit=LANES):
    t = min(n, pref)
    t -= t % unit
    while t > unit and n % t:
        t -= unit
    if t <= 0 or n % t:
        return n
    return t


def _params(ndim, vmem=True):
    return pltpu.CompilerParams(dimension_semantics=("arbitrary",) * ndim,
                                vmem_limit_bytes=VMEM_LIMIT if vmem else None)


def _sigmoid(x):
    return 0.5 * jnp.tanh(0.5 * x) + 0.5


def _fsum(x):
    return jnp.sum(jnp.sum(x, axis=0, keepdims=True), axis=1, keepdims=True)


def _rope_tables(L):
    lane = np.arange(LANES)
    d = lane % 64
    inv_r = jnp.asarray(ROPE_BASE, F32) ** (-jnp.arange(32, dtype=F32) / 32)
    t = jnp.arange(L)
    ang_r = t.astype(F32)[:, None] * jnp.tile(inv_r, LANES // 32)[None, :]
    Rr = np.zeros((LANES, LANES), np.float32)
    for l in range(LANES):
        if d[l] < 32:
            Rr[l + 32, l] = -1.0
        else:
            Rr[l - 32, l] = 1.0
    inv_a = jnp.asarray(ROPE_BASE, F32) ** (-jnp.arange(16, dtype=F32) / 16)
    rows = (t // GRID_W).astype(F32)
    cols = (t % GRID_W).astype(F32)
    dd = d % 32
    pos = jnp.where(jnp.asarray(d < 32)[None, :], rows[:, None], cols[:, None])
    ang_a = pos * jnp.tile(inv_a, LANES // 16)[None, :]
    Ra = np.zeros((LANES, LANES), np.float32)
    for l in range(LANES):
        if dd[l] < 16:
            Ra[l + 16, l] = -1.0
        else:
            Ra[l - 16, l] = 1.0
    D0 = np.zeros((LANES, LANES), np.float32)
    D1 = np.zeros((LANES, LANES), np.float32)
    for l in range(LANES):
        D0[l % 64, l] = 1.0
        D1[64 + l % 64, l] = 1.0
    return dict(
        Cr=jnp.cos(ang_r), Sr=jnp.sin(ang_r), Rr=jnp.asarray(Rr, BF16), RrT=jnp.asarray(Rr.T, BF16),
        Ca=jnp.cos(ang_a), Sa=jnp.sin(ang_a), Ra=jnp.asarray(Ra, BF16), RaT=jnp.asarray(Ra.T, BF16),
        D0=jnp.asarray(D0, BF16), D1=jnp.asarray(D1, BF16),
        D0T=jnp.asarray(D0.T, BF16), D1T=jnp.asarray(D1.T, BF16))


def _norm_mod(xf, g, sh, sc):
    r = lax.rsqrt(jnp.mean(xf * xf, axis=-1, keepdims=True) + NORM_EPS)
    return (xf * r * g) * (1.0 + sc) + sh


def _norm_mod_matmul(x, g, sh, sc, w, name):
    M, D = x.shape
    N = w.shape[1]
    tm, tn = _tile(M, ROWS_PER_LATCH, 8), _tile(N, 768)

    def body(x_ref, g_ref, sh_ref, sc_ref, w_ref, p_ref, h_ref, hs):
        @pl.when(pl.program_id(1) == 0)
        def _():
            hb = _norm_mod(x_ref[...], g_ref[...], sh_ref[...], sc_ref[...]).astype(BF16)
            hs[...] = hb
            h_ref[...] = hb
        p_ref[...] = _nn(hs[...], w_ref[...]).astype(BF16)

    vec = pl.BlockSpec((1, D), lambda i, j: (0, 0))
    return pl.pallas_call(
        body, name=name, grid=(M // tm, N // tn),
        in_specs=[pl.BlockSpec((tm, D), lambda i, j: (i, 0)), vec, vec, vec,
                  pl.BlockSpec((D, tn), lambda i, j: (0, j))],
        out_specs=[pl.BlockSpec((tm, tn), lambda i, j: (i, j)), pl.BlockSpec((tm, D), lambda i, j: (i, 0))],
        out_shape=[jax.ShapeDtypeStruct((M, N), BF16), jax.ShapeDtypeStruct((M, D), BF16)],
        scratch_shapes=[pltpu.VMEM((tm, D), BF16)],
        compiler_params=_params(2))(x, g, sh, sc, w)


def _proj_residual(a, w, xres, gt, name):
    M, K = a.shape
    N = w.shape[1]
    tm, tn = _tile(M, ROWS_PER_LATCH, 8), _tile(N, 1024 if K <= 2048 else 512)

    def body(a_ref, w_ref, x_ref, gt_ref, xo_ref, o_ref):
        o = _nn(a_ref[...], w_ref[...])
        o_ref[...] = o.astype(BF16)
        xo_ref[...] = x_ref[...] + gt_ref[...] * o

    return pl.pallas_call(
        body, name=name, grid=(M // tm, N // tn),
        in_specs=[pl.BlockSpec((tm, K), lambda i, j: (i, 0)), pl.BlockSpec((K, tn), lambda i, j: (0, j)),
                  pl.BlockSpec((tm, tn), lambda i, j: (i, j)), pl.BlockSpec((1, tn), lambda i, j: (0, j))],
        out_specs=[pl.BlockSpec((tm, tn), lambda i, j: (i, j)), pl.BlockSpec((tm, tn), lambda i, j: (i, j))],
        out_shape=[jax.ShapeDtypeStruct((M, N), F32), jax.ShapeDtypeStruct((M, N), BF16)],
        compiler_params=_params(2))(a, w, xres, gt)


def _ffn_in(x1, g, sh, sc, wg, wu):
    M, D = x1.shape
    N = wg.shape[1]
    tm, tn = _tile(M, ROWS_PER_LATCH, 8), _tile(N, 512)

    def body(x_ref, g_ref, sh_ref, sc_ref, wg_ref, wu_ref, G_ref, U_ref, A_ref, h_ref, hs):
        @pl.when(pl.program_id(1) == 0)
        def _():
            hb = _norm_mod(x_ref[...], g_ref[...], sh_ref[...], sc_ref[...]).astype(BF16)
            hs[...] = hb
            h_ref[...] = hb
        G = _nn(hs[...], wg_ref[...])
        U = _nn(hs[...], wu_ref[...])
        G_ref[...] = G.astype(BF16)
        U_ref[...] = U.astype(BF16)
        A_ref[...] = (G * _sigmoid(G) * U).astype(BF16)

    vec = pl.BlockSpec((1, D), lambda i, j: (0, 0))
    wspec = pl.BlockSpec((D, tn), lambda i, j: (0, j))
    ospec = pl.BlockSpec((tm, tn), lambda i, j: (i, j))
    big = jax.ShapeDtypeStruct((M, N), BF16)
    return pl.pallas_call(
        body, name="ffn_in", grid=(M // tm, N // tn),
        in_specs=[pl.BlockSpec((tm, D), lambda i, j: (i, 0)), vec, vec, vec, wspec, wspec],
        out_specs=[ospec, ospec, ospec, pl.BlockSpec((tm, D), lambda i, j: (i, 0))],
        out_shape=[big, big, big, jax.ShapeDtypeStruct((M, D), BF16)],
        scratch_shapes=[pltpu.VMEM((tm, D), BF16)],
        compiler_params=_params(2))(x1, g, sh, sc, wg, wu)


def _final(x2, gn, tgt, gt, saved):
    M, D = x2.shape
    tm = _tile(M, 256, 8)

    def body(x_ref, g_ref, t_ref, gt_ref, sv_ref, dx_ref, loss_ref, dg_ref, dz_ref, dgt_ref):
        @pl.when(pl.program_id(0) == 0)
        def _():
            loss_ref[...] = jnp.zeros_like(loss_ref)
            dg_ref[...] = jnp.zeros_like(dg_ref)
            dgt_ref[...] = jnp.zeros_like(dgt_ref)
        x = x_ref[...]
        g = g_ref[...]
        r = lax.rsqrt(jnp.mean(x * x, axis=-1, keepdims=True) + NORM_EPS)
        xh = x * r
        e = xh * g - t_ref[...]
        loss_ref[...] += (0.5 / D) * _fsum(e * e)
        dy = e * (1.0 / D)
        dg_ref[...] += jnp.sum(dy * xh, axis=0, keepdims=True)
        dxh = dy * g
        d = r * (dxh - xh * jnp.mean(dxh * xh, axis=-1, keepdims=True))
        dx_ref[...] = d
        dz_ref[...] = (d * gt_ref[...]).astype(BF16)
        dgt_ref[...] += jnp.sum(d * sv_ref[...].astype(F32), axis=0, keepdims=True)

    row = pl.BlockSpec((tm, D), lambda i: (i, 0))
    vec = pl.BlockSpec((1, D), lambda i: (0, 0))
    return pl.pallas_call(
        body, name="final_loss", grid=(M // tm,),
        in_specs=[row, vec, row, vec, row],
        out_specs=[row, pl.BlockSpec((1, LANES), lambda i: (0, 0)), vec, row, vec],
        out_shape=[jax.ShapeDtypeStruct((M, D), F32), jax.ShapeDtypeStruct((1, LANES), F32),
                   jax.ShapeDtypeStruct((1, D), F32), jax.ShapeDtypeStruct((M, D), BF16),
                   jax.ShapeDtypeStruct((1, D), F32)],
        compiler_params=_params(1))(x2, gn, tgt, gt, saved)


def _col_group(blk0, nblk):
    return int(np.gcd(blk0, nblk)) if blk0 else nblk


def _rope_cols(src, blk0, nblk, Ct, St, R, scale, rope, name):
    M = src.shape[0]
    tm = _tile(M, 512, 8)
    wb = _col_group(blk0, nblk)

    def body(x_ref, c_ref, s_ref, r_ref, o_ref):
        for j in range(wb):
            cols = slice(j * LANES, (j + 1) * LANES)
            x = x_ref[:, cols]
            xf = x.astype(F32)
            if rope:
                xf = xf * c_ref[...] + _nn(x.astype(BF16), r_ref[...]) * s_ref[...]
            o_ref[:, cols] = (xf * scale).astype(BF16)

    tab = pl.BlockSpec((tm, LANES), lambda i, j: (i, 0))
    return pl.pallas_call(
        body, name=name, grid=(M // tm, nblk // wb),
        in_specs=[pl.BlockSpec((tm, wb * LANES), lambda i, j: (i, blk0 // wb + j)), tab, tab,
                  pl.BlockSpec((LANES, LANES), lambda i, j: (0, 0))],
        out_specs=pl.BlockSpec((tm, wb * LANES), lambda i, j: (i, j)),
        out_shape=jax.ShapeDtypeStruct((M, nblk * LANES), BF16),
        compiler_params=_params(2, False))(src, Ct, St, R)


def _dup_heads(src, blk0, npair, Ct, St, R, D0, D1, rope, name):
    M = src.shape[0]
    tm = _tile(M, 512, 8)

    def body(x_ref, c_ref, s_ref, r_ref, d0_ref, d1_ref, o_ref):
        x = x_ref[...]
        if rope:
            x = (x.astype(F32) * c_ref[...] + _nn(x, r_ref[...]) * s_ref[...]).astype(BF16)
        o_ref[0] = _nn(x, d0_ref[...]).astype(BF16)
        o_ref[1] = _nn(x, d1_ref[...]).astype(BF16)

    tab = pl.BlockSpec((tm, LANES), lambda i, p: (i, 0))
    mat = pl.BlockSpec((LANES, LANES), lambda i, p: (0, 0))
    return pl.pallas_call(
        body, name=name, grid=(M // tm, npair),
        in_specs=[pl.BlockSpec((tm, LANES), lambda i, p: (i, blk0 + p)), tab, tab, mat, mat, mat],
        out_specs=pl.BlockSpec((2, tm, LANES), lambda i, p: (p, i, 0)),
        out_shape=jax.ShapeDtypeStruct((2 * npair, M, LANES), BF16),
        compiler_params=_params(2, False))(src, Ct, St, R, D0, D1)


def _unrope_cols(dsrc, dst, blk0, nblk, Ct, St, RT, scale, rope, name):
    M = dsrc.shape[0]
    tm = _tile(M, 512, 8)
    wb = _col_group(blk0, nblk)

    def body(x_ref, c_ref, s_ref, r_ref, dst_ref, o_ref):
        del dst_ref
        for j in range(wb):
            cols = slice(j * LANES, (j + 1) * LANES)
            xf = x_ref[:, cols].astype(F32)
            if rope:
                xf = xf * c_ref[...] + _nn((xf * s_ref[...]).astype(BF16), r_ref[...])
            o_ref[:, cols] = (xf * scale).astype(BF16)

    tab = pl.BlockSpec((tm, LANES), lambda i, j: (i, 0))
    return pl.pallas_call(
        body, name=name, grid=(M // tm, nblk // wb),
        in_specs=[pl.BlockSpec((tm, wb * LANES), lambda i, j: (i, j)), tab, tab,
                  pl.BlockSpec((LANES, LANES), lambda i, j: (0, 0)),
                  pl.BlockSpec(memory_space=pl.ANY)],
        out_specs=pl.BlockSpec((tm, wb * LANES), lambda i, j: (i, blk0 // wb + j)),
        out_shape=jax.ShapeDtypeStruct(dst.shape, dst.dtype),
        input_output_aliases={4: 0},
        compiler_params=_params(2, False))(dsrc, Ct, St, RT, dst)


def _fold_heads(parts, dst, blk0, npair, Ct, St, RT, D0T, D1T, rope, name):
    M = parts[0][0].shape[1]
    nb = M // CHUNK
    R = _tile(M, 1024, CHUNK)
    rb = R // CHUNK
    nrefs = sum(1 if s == 0 else 2 for _, s in parts)

    def body(*refs):
        part_refs = list(refs[:nrefs])
        c_ref, s_ref, r_ref, d0_ref, d1_ref, dst_ref, o_ref = refs[nrefs:]
        del dst_ref
        i = pl.program_id(0)
        tot = [jnp.zeros((R, LANES), F32), jnp.zeros((R, LANES), F32)]
        for _, shift in parts:
            main = part_refs.pop(0)
            if shift == 0:
                for e in range(2):
                    tot[e] = tot[e] + main[e].astype(F32)
                continue
            edge = part_refs.pop(0)
            ok = (i + 1) * rb <= nb - 1 if shift > 0 else i > 0
            for e in range(2):
                ed = jnp.where(ok, edge[e].astype(F32), 0.0)
                if rb == 1:
                    tot[e] = tot[e] + ed
                elif shift > 0:
                    tot[e] = tot[e] + jnp.concatenate([main[e, CHUNK:, :].astype(F32), ed], axis=0)
                else:
                    tot[e] = tot[e] + jnp.concatenate([ed, main[e, :R - CHUNK, :].astype(F32)], axis=0)
        f = _nn(tot[0].astype(BF16), d0_ref[...]) + _nn(tot[1].astype(BF16), d1_ref[...])
        if rope:
            f = f * c_ref[...] + _nn((f * s_ref[...]).astype(BF16), r_ref[...])
        o_ref[...] = f.astype(BF16)

    in_specs, args = [], []
    for a, shift in parts:
        assert shift in (-1, 0, 1)
        in_specs.append(pl.BlockSpec((2, R, LANES), lambda i, p: (p, i, 0)))
        args.append(a)
        if shift > 0:
            in_specs.append(pl.BlockSpec((2, CHUNK, LANES), lambda i, p: (p, jnp.minimum((i + 1) * rb, nb - 1), 0)))
            args.append(a)
        elif shift < 0:
            in_specs.append(pl.BlockSpec((2, CHUNK, LANES), lambda i, p: (p, jnp.maximum(i * rb - 1, 0), 0)))
            args.append(a)
    tab = pl.BlockSpec((R, LANES), lambda i, p: (i, 0))
    mat = pl.BlockSpec((LANES, LANES), lambda i, p: (0, 0))
    return pl.pallas_call(
        body, name=name, grid=(M // R, npair),
        in_specs=in_specs + [tab, tab, mat, mat, mat, pl.BlockSpec(memory_space=pl.ANY)],
        out_specs=pl.BlockSpec((R, LANES), lambda i, p: (i, blk0 + p)),
        out_shape=jax.ShapeDtypeStruct(dst.shape, dst.dtype),
        input_output_aliases={nrefs + 5: 0},
        compiler_params=_params(2, False))(*args, Ct, St, RT, D0T, D1T, dst)


def _head_masks():
    lane = lax.broadcasted_iota(jnp.int32, (1, LANES), 1)
    return [lane < 64, lane >= 64]


def _decay_vecs(lam, mu):
    i = lax.broadcasted_iota(jnp.int32, (CHUNK, 1), 0).astype(F32)
    return dict(qf=jnp.exp(lam * (i + 1.0)), kf=jnp.exp(lam * (CHUNK - 1.0 - i)),
                qb=jnp.exp(mu * (CHUNK - i)), kb=jnp.exp(mu * i),
                gf=jnp.exp(lam * float(CHUNK)), gb=jnp.exp(mu * float(CHUNK)), i=i)


def _decay_mask(lam, mu):
    r = lax.broadcasted_iota(jnp.int32, (CHUNK, CHUNK), 0)
    c = lax.broadcasted_iota(jnp.int32, (CHUNK, CHUNK), 1)
    rel = (r - c).astype(F32)
    low = rel >= 0.0
    mf = jnp.exp(lam * jnp.maximum(rel, 0.0))
    mb = jnp.exp(mu * jnp.maximum(-rel, 0.0))
    return jnp.where(low, mf, mb), rel, low


def _lam_of(lg_ref, row, idx):
    return jnp.full((1, 1), lg_ref[row, idx], F32)


def _group_index(pair_blk, npairs):
    assert pair_blk % npairs == 0
    return pair_blk // npairs


def _ret_states_fwd(Kr, P, Krc, Pc, lg, rv_blk, npairs):
    L = Kr.shape[0]
    Lc = Krc.shape[0]
    N, ncc = L // CHUNK, Lc // CHUNK
    rv_grp = _group_index(rv_blk, npairs)

    heads = [(p, h) for p in range(npairs) for h in range(2)]
    kcols = lambda p: slice(p * LANES, (p + 1) * LANES)
    vcols = lambda p, h: slice((2 * p + h) * LANES, (2 * p + h + 1) * LANES)

    def body(lg_ref, k_ref, v_ref, kc_ref, vc_ref, sf_ref, S):
        n = pl.program_id(0)
        masks = _head_masks()

        @pl.when(n == 0)
        def _():
            for p, h in heads:
                lam = _lam_of(lg_ref, 0, 2 * p + h)
                dv = _decay_vecs(lam, lam)
                s = jnp.zeros((LANES, LANES), F32)
                for cc in range(ncc):
                    rows = slice(cc * CHUNK, (cc + 1) * CHUNK)
                    kw = jnp.where(masks[h], kc_ref[rows, kcols(p)].astype(F32) * dv["kf"], 0.0).astype(BF16)
                    s = dv["gf"] * s + _tn(kw, vc_ref[rows, vcols(p, h)])
                S[p, h] = s

        for p, h in heads:
            lam = _lam_of(lg_ref, 0, 2 * p + h)
            dv = _decay_vecs(lam, lam)
            s = S[p, h]
            sf_ref[p, 0, h] = s.astype(BF16)
            kw = jnp.where(masks[h], k_ref[:, kcols(p)].astype(F32) * dv["kf"], 0.0).astype(BF16)
            S[p, h] = dv["gf"] * s + _tn(kw, v_ref[:, vcols(p, h)])

    wq, wv = npairs * LANES, npairs * 2 * LANES
    return pl.pallas_call(
        body, name="ret_states_fwd", grid=(N,),
        in_specs=[pl.BlockSpec(memory_space=pltpu.SMEM),
                  pl.BlockSpec((CHUNK, wq), lambda n: (n, 0)),
                  pl.BlockSpec((CHUNK, wv), lambda n: (n, rv_grp)),
                  pl.BlockSpec((Lc, wq), lambda n: (0, 0)),
                  pl.BlockSpec((Lc, wv), lambda n: (0, rv_grp))],
        out_specs=pl.BlockSpec((npairs, 1, 2, LANES, LANES), lambda n: (0, n, 0, 0, 0)),
        out_shape=jax.ShapeDtypeStruct((npairs, N, 2, LANES, LANES), BF16),
        scratch_shapes=[pltpu.VMEM((npairs, 2, LANES, LANES), F32)],
        compiler_params=_params(1, False))(lg, Kr, P, Krc, Pc)


def _ret_chunk_fwd(q, k, v, sf, sb, hm, lam, mu, Mk):
    dv = _decay_vecs(lam, mu)
    qm = jnp.where(hm, q, jnp.zeros_like(q))
    qmf = qm.astype(F32)
    A = _nt(qm, k)
    Am = A * Mk
    Amb = Am.astype(BF16)
    Qf = (qmf * dv["qf"]).astype(BF16)
    Qb = (qmf * dv["qb"]).astype(BF16)
    O = _nn(Amb, v) + _nn(Qf, sf) + _nn(Qb, sb)
    return dict(dv=dv, Mk=Mk, qm=qm, Am=Am, Amb=Amb, Qf=Qf, Qb=Qb, O=O)


def _ret_out_fwd(Qr, Kr, P, Krc, Pc, SF, lg, rv_blk, rg_blk, npairs, d_mix):
    L = Qr.shape[0]
    Lc = Krc.shape[0]
    N, ncc = L // CHUNK, Lc // CHUNK

    rv_grp, rg_grp = _group_index(rv_blk, npairs), _group_index(rg_blk, npairs)
    heads = [(p, h) for p in range(npairs) for h in range(2)]
    kcols = lambda p: slice(p * LANES, (p + 1) * LANES)
    vcols = lambda p, h: slice((2 * p + h) * LANES, (2 * p + h + 1) * LANES)

    def body(lg_ref, q_ref, k_ref, v_ref, g_ref, sf_ref, kc_ref, vc_ref, y_ref, sb_ref, S, Mks):
        n = pl.program_id(0)
        masks = _head_masks()

        @pl.when(n == 0)
        def _():
            for p, h in heads:
                mu = _lam_of(lg_ref, 1, 2 * p + h)
                Mks[p, h] = _decay_mask(_lam_of(lg_ref, 0, 2 * p + h), mu)[0]
                dvb = _decay_vecs(mu, mu)
                s = jnp.zeros((LANES, LANES), F32)
                for cc in reversed(range(ncc)):
                    rows = slice(cc * CHUNK, (cc + 1) * CHUNK)
                    kw = jnp.where(masks[h], kc_ref[rows, kcols(p)].astype(F32) * dvb["kb"], 0.0).astype(BF16)
                    s = dvb["gb"] * s + _tn(kw, vc_ref[rows, vcols(p, h)])
                S[p, h] = s

        for p, h in heads:
            lam = _lam_of(lg_ref, 0, 2 * p + h)
            mu = _lam_of(lg_ref, 1, 2 * p + h)
            hm = masks[h]
            dvb = _decay_vecs(lam, mu)
            s = S[p, h]
            sbb = s.astype(BF16)
            sb_ref[p, 0, h] = sbb
            k = k_ref[:, kcols(p)]
            v = v_ref[:, vcols(p, h)]
            f = _ret_chunk_fwd(q_ref[:, kcols(p)], k, v, sf_ref[p, 0, h], sbb, hm, lam, mu, Mks[p, h])
            O = f["O"]
            r = lax.rsqrt(jnp.mean(O * O, axis=-1, keepdims=True) + NORM_EPS)
            g = g_ref[:, vcols(p, h)].astype(F32)
            y_ref[:, vcols(p, h)] = (O * r * (g * _sigmoid(g))).astype(BF16)
            kw = jnp.where(hm, k.astype(F32) * dvb["kb"], 0.0).astype(BF16)
            S[p, h] = dvb["gb"] * s + _tn(kw, v)

    rev = lambda n: N - 1 - n
    wq, wv = npairs * LANES, npairs * 2 * LANES
    st = pl.BlockSpec((npairs, 1, 2, LANES, LANES), lambda n: (0, rev(n), 0, 0, 0))
    return pl.pallas_call(
        body, name="ret_out_fwd", grid=(N,),
        in_specs=[pl.BlockSpec(memory_space=pltpu.SMEM),
                  pl.BlockSpec((CHUNK, wq), lambda n: (rev(n), 0)),
                  pl.BlockSpec((CHUNK, wq), lambda n: (rev(n), 0)),
                  pl.BlockSpec((CHUNK, wv), lambda n: (rev(n), rv_grp)),
                  pl.BlockSpec((CHUNK, wv), lambda n: (rev(n), rg_grp)),
                  st,
                  pl.BlockSpec((Lc, wq), lambda n: (0, 0)),
                  pl.BlockSpec((Lc, wv), lambda n: (0, rv_grp))],
        out_specs=[pl.BlockSpec((CHUNK, wv), lambda n: (rev(n), 0)), st],
        out_shape=[jax.ShapeDtypeStruct((L, d_mix), BF16),
                   jax.ShapeDtypeStruct((npairs, N, 2, LANES, LANES), BF16)],
        scratch_shapes=[pltpu.VMEM((npairs, 2, LANES, LANES), F32), pltpu.VMEM((npairs, 2, CHUNK, CHUNK), F32)],
        compiler_params=_params(1))(lg, Qr, Kr, P, P, SF, Krc, Pc)


ACC_ROWS = 8


def _ret_bwd1(Qr, Kr, P, Krc, Pc, SF, SB, dY, lg, rv_blk, rg_blk, npairs, d_proj):
    L = Qr.shape[0]
    Lc = Krc.shape[0]
    N, ncc = L // CHUNK, Lc // CHUNK
    rv_grp, rg_grp = _group_index(rv_blk, npairs), _group_index(rg_blk, npairs)
    heads = [(p, h) for p in range(npairs) for h in range(2)]
    kcols = lambda p: slice(p * LANES, (p + 1) * LANES)
    vcols = lambda p, h: slice((2 * p + h) * LANES, (2 * p + h + 1) * LANES)

    def body(lg_ref, q_ref, k_ref, v_ref, g_ref, sf_ref, sb_ref, dy_ref, kc_ref, vc_ref,
             dq_ref, dk_ref, dv_ref, dg_ref, do_ref, dkc_ref, dvc_ref, acc_ref, dS, T, Mks):
        n = pl.program_id(0)
        masks = _head_masks()

        @pl.when(n == 0)
        def _():
            dS[...] = jnp.zeros_like(dS)
            T[...] = jnp.zeros_like(T)
            acc_ref[...] = jnp.zeros_like(acc_ref)
            for p, h in heads:
                Mks[p, h] = _decay_mask(_lam_of(lg_ref, 0, 2 * p + h), _lam_of(lg_ref, 1, 2 * p + h))[0]

        def head_main(p, h):
            lam = _lam_of(lg_ref, 0, 2 * p + h)
            mu = _lam_of(lg_ref, 1, 2 * p + h)
            hm = masks[h]
            hs = vcols(p, h)
            v = v_ref[:, hs]
            k = k_ref[:, kcols(p)]
            sf = sf_ref[p, 0, h]
            sb = sb_ref[p, 0, h]
            f = _ret_chunk_fwd(q_ref[:, kcols(p)], k, v, sf, sb, hm, lam, mu, Mks[p, h])
            dv_, O = f["dv"], f["O"]
            r = lax.rsqrt(jnp.mean(O * O, axis=-1, keepdims=True) + NORM_EPS)
            on = O * r
            g = g_ref[:, hs].astype(F32)
            sg = _sigmoid(g)
            dy = dy_ref[:, hs].astype(F32)
            dg_ref[:, hs] = (dy * on * (sg * (1.0 + g * (1.0 - sg)))).astype(BF16)
            don = dy * (g * sg)
            dO = r * (don - on * jnp.mean(don * on, axis=-1, keepdims=True))
            dOb = dO.astype(BF16)
            do_ref[:, hs] = dOb
            dAm = _nt(dOb, v)
            T[p, h] += dAm * f["Am"]
            dAb = (dAm * f["Mk"]).astype(BF16)
            km = jnp.where(hm, k, jnp.zeros_like(k))
            dq = _nn(dAb, km)
            dk = _tn(dAb, f["qm"])
            dvh = _tn(f["Amb"], dOb)
            dQf = _nt(dOb, sf)
            dQb = _nt(dOb, sb)
            dq = dq + dQf * dv_["qf"] + dQb * dv_["qb"]
            acc_ref[p, h, 0:1, :] += _fsum(dQf * f["Qf"].astype(F32) * (dv_["i"] + 1.0))
            acc_ref[p, h, 1:2, :] += _fsum(dQb * f["Qb"].astype(F32) * (CHUNK - dv_["i"]))
            dSh = dS[p, h]
            dSb_ = dSh.astype(BF16)
            Kf = (km.astype(F32) * dv_["kf"]).astype(BF16)
            dKf = _nt(v, dSb_)
            dk = dk + jnp.where(hm, dKf * dv_["kf"], 0.0)
            acc_ref[p, h, 2:3, :] += _fsum(jnp.where(hm, dKf, 0.0) * Kf.astype(F32) * (CHUNK - 1.0 - dv_["i"]))
            dvh = dvh + _nn(Kf, dSb_)
            acc_ref[p, h, 3:4, :] += float(CHUNK) * dv_["gf"] * _fsum(dSh * sf.astype(F32))
            dSh = dv_["gf"] * dSh + _tn(f["Qf"], dOb)
            dS[p, h] = dSh
            dv_ref[:, hs] = dvh
            return dq, dk

        for p in range(npairs):
            dq0, dk0 = head_main(p, 0)
            dq1, dk1 = head_main(p, 1)
            dq_ref[:, kcols(p)] = dq0 + dq1
            dk_ref[:, kcols(p)] = dk0 + dk1

        @pl.when(n == N - 1)
        def _():
            for p, h in heads:
                lam = _lam_of(lg_ref, 0, 2 * p + h)
                dv_ = _decay_vecs(lam, lam)
                hm = masks[h]
                hs = vcols(p, h)
                states = [jnp.zeros((LANES, LANES), F32)]
                kws = []
                for cc in range(ncc):
                    rows = slice(cc * CHUNK, (cc + 1) * CHUNK)
                    kw = jnp.where(hm, kc_ref[rows, kcols(p)].astype(F32) * dv_["kf"], 0.0).astype(BF16)
                    kws.append(kw)
                    states.append(dv_["gf"] * states[-1] + _tn(kw, vc_ref[rows, hs]))
                d = dS[p, h]
                for cc in reversed(range(ncc)):
                    db = d.astype(BF16)
                    rows = slice(cc * CHUNK, (cc + 1) * CHUNK)
                    dKf_c = jnp.where(hm, _nt(vc_ref[rows, hs], db), 0.0)
                    part = dKf_c * dv_["kf"]
                    if h == 0:
                        dkc_ref[rows, kcols(p)] = part
                    else:
                        dkc_ref[rows, kcols(p)] += part
                    acc_ref[p, h, 2:3, :] += _fsum(dKf_c * kws[cc].astype(F32) * (CHUNK - 1.0 - dv_["i"]))
                    dvc_ref[rows, hs] = _nn(kws[cc], db)
                    acc_ref[p, h, 3:4, :] += float(CHUNK) * dv_["gf"] * _fsum(d * states[cc])
                    d = dv_["gf"] * d
                _, rel, low = _decay_mask(lam, lam)
                Th = T[p, h]
                acc_ref[p, h, 4:5, :] += _fsum(jnp.where(low, Th * rel, 0.0))
                acc_ref[p, h, 5:6, :] += _fsum(jnp.where(low, 0.0, -Th * rel))

    rev = lambda n: N - 1 - n
    wq, wv = npairs * LANES, npairs * 2 * LANES
    st = pl.BlockSpec((npairs, 1, 2, LANES, LANES), lambda n: (0, rev(n), 0, 0, 0))
    pair = pl.BlockSpec((CHUNK, wq), lambda n: (rev(n), 0))
    wide = lambda grp: pl.BlockSpec((CHUNK, wv), lambda n: (rev(n), grp))
    return pl.pallas_call(
        body, name="ret_bwd_desc", grid=(N,),
        in_specs=[pl.BlockSpec(memory_space=pltpu.SMEM), pair, pair, wide(rv_grp), wide(rg_grp), st, st, wide(0),
                  pl.BlockSpec((Lc, wq), lambda n: (0, 0)),
                  pl.BlockSpec((Lc, wv), lambda n: (0, rv_grp))],
        out_specs=[pair, pair, wide(0), wide(rg_grp), wide(0),
                   pl.BlockSpec((Lc, wq), lambda n: (0, 0)),
                   pl.BlockSpec((Lc, wv), lambda n: (0, 0)),
                   pl.BlockSpec((npairs, 2, ACC_ROWS, LANES), lambda n: (0, 0, 0, 0))],
        out_shape=[jax.ShapeDtypeStruct((L, npairs * LANES), F32),
                   jax.ShapeDtypeStruct((L, npairs * LANES), F32),
                   jax.ShapeDtypeStruct((L, npairs * 2 * LANES), F32),
                   jax.ShapeDtypeStruct((L, d_proj), BF16),
                   jax.ShapeDtypeStruct((L, npairs * 2 * LANES), BF16),
                   jax.ShapeDtypeStruct((Lc, npairs * LANES), F32),
                   jax.ShapeDtypeStruct((Lc, npairs * 2 * LANES), F32),
                   jax.ShapeDtypeStruct((npairs, 2, ACC_ROWS, LANES), F32)],
        scratch_shapes=[pltpu.VMEM((npairs, 2, LANES, LANES), F32), pltpu.VMEM((npairs, 2, CHUNK, CHUNK), F32),
                        pltpu.VMEM((npairs, 2, CHUNK, CHUNK), F32)],
        compiler_params=_params(1))(lg, Qr, Kr, P, P, SF, SB, dY, Krc, Pc)


def _ret_bwd2(Qr, Kr, P, Krc, Pc, SB, dO, dKr, dVp, dP, dKc, dVc, lg, rv_blk, npairs):
    L = Qr.shape[0]
    Lc = Krc.shape[0]
    N, ncc = L // CHUNK, Lc // CHUNK
    rv_grp = _group_index(rv_blk, npairs)
    heads = [(p, h) for p in range(npairs) for h in range(2)]
    kcols = lambda p: slice(p * LANES, (p + 1) * LANES)
    vcols = lambda p, h: slice((2 * p + h) * LANES, (2 * p + h + 1) * LANES)

    def body(lg_ref, q_ref, k_ref, v_ref, sb_ref, do_ref, dkin_ref, dvin_ref, kc_ref, vc_ref, dkcin_ref, dvcin_ref,
             dpin_ref, dk_ref, dv_ref, dkc_ref, dvc_ref, acc_ref, dS):
        del dpin_ref
        n = pl.program_id(0)
        masks = _head_masks()

        @pl.when(n == 0)
        def _():
            dS[...] = jnp.zeros_like(dS)
            acc_ref[...] = jnp.zeros_like(acc_ref)

        def head_main(p, h):
            mu = _lam_of(lg_ref, 1, 2 * p + h)
            hm = masks[h]
            hs = vcols(p, h)
            dv_ = _decay_vecs(mu, mu)
            v = v_ref[:, hs]
            k = k_ref[:, kcols(p)]
            q = q_ref[:, kcols(p)]
            dOb = do_ref[:, hs]
            km = jnp.where(hm, k, jnp.zeros_like(k)).astype(F32)
            Kb = (km * dv_["kb"]).astype(BF16)
            Qb = (jnp.where(hm, q, jnp.zeros_like(q)).astype(F32) * dv_["qb"]).astype(BF16)
            dSh = dS[p, h]
            dSb_ = dSh.astype(BF16)
            dKb = jnp.where(hm, _nt(v, dSb_), 0.0)
            acc_ref[p, h, 0:1, :] += _fsum(dKb * Kb.astype(F32) * dv_["i"])
            dv_ref[:, hs] = (dvin_ref[:, hs] + _nn(Kb, dSb_)).astype(BF16)
            acc_ref[p, h, 1:2, :] += float(CHUNK) * dv_["gb"] * _fsum(dSh * sb_ref[p, 0, h].astype(F32))
            dS[p, h] = dv_["gb"] * dSh + _tn(Qb, dOb)
            return dKb * dv_["kb"]

        for p in range(npairs):
            dk_ref[:, kcols(p)] = dkin_ref[:, kcols(p)] + head_main(p, 0) + head_main(p, 1)

        @pl.when(n == N - 1)
        def _():
            for p, h in heads:
                mu = _lam_of(lg_ref, 1, 2 * p + h)
                hm = masks[h]
                hs = vcols(p, h)
                dv_ = _decay_vecs(mu, mu)
                states = {}
                kws = {}
                s = jnp.zeros((LANES, LANES), F32)
                for cc in reversed(range(ncc)):
                    rows = slice(cc * CHUNK, (cc + 1) * CHUNK)
                    states[cc] = s
                    kw = jnp.where(hm, kc_ref[rows, kcols(p)].astype(F32) * dv_["kb"], 0.0).astype(BF16)
                    kws[cc] = kw
                    s = dv_["gb"] * s + _tn(kw, vc_ref[rows, hs])
                d = dS[p, h]
                for cc in range(ncc):
                    db = d.astype(BF16)
                    rows = slice(cc * CHUNK, (cc + 1) * CHUNK)
                    dKb_c = jnp.where(hm, _nt(vc_ref[rows, hs], db), 0.0)
                    part = dKb_c * dv_["kb"]
                    if h == 0:
                        dkc_ref[rows, kcols(p)] = dkcin_ref[rows, kcols(p)] + part
                    else:
                        dkc_ref[rows, kcols(p)] += part
                    acc_ref[p, h, 0:1, :] += _fsum(dKb_c * kws[cc].astype(F32) * dv_["i"])
                    dvc_ref[rows, hs] = dvcin_ref[rows, hs] + _nn(kws[cc], db)
                    acc_ref[p, h, 1:2, :] += float(CHUNK) * dv_["gb"] * _fsum(d * states[cc])
                    d = dv_["gb"] * d

    wq, wv = npairs * LANES, npairs * 2 * LANES
    st = pl.BlockSpec((npairs, 1, 2, LANES, LANES), lambda n: (0, n, 0, 0, 0))
    pair = pl.BlockSpec((CHUNK, wq), lambda n: (n, 0))
    wide = lambda grp: pl.BlockSpec((CHUNK, wv), lambda n: (n, grp))
    ckc = pl.BlockSpec((Lc, wq), lambda n: (0, 0))
    cvc = lambda grp: pl.BlockSpec((Lc, wv), lambda n: (0, grp))
    return pl.pallas_call(
        body, name="ret_bwd_asc", grid=(N,),
        in_specs=[pl.BlockSpec(memory_space=pltpu.SMEM), pair, pair, wide(rv_grp), st, wide(0), pair, wide(0),
                  ckc, cvc(rv_grp), ckc, cvc(0), pl.BlockSpec(memory_space=pl.ANY)],
        out_specs=[pair, wide(rv_grp), ckc, cvc(0),
                   pl.BlockSpec((npairs, 2, ACC_ROWS, LANES), lambda n: (0, 0, 0, 0))],
        out_shape=[jax.ShapeDtypeStruct(dKr.shape, F32),
                   jax.ShapeDtypeStruct(dP.shape, dP.dtype),
                   jax.ShapeDtypeStruct(dKc.shape, F32),
                   jax.ShapeDtypeStruct(dVc.shape, F32),
                   jax.ShapeDtypeStruct((npairs, 2, ACC_ROWS, LANES), F32)],
        input_output_aliases={12: 1},
        scratch_shapes=[pltpu.VMEM((npairs, 2, LANES, LANES), F32)],
        compiler_params=_params(1))(lg, Qr, Kr, P, SB, dO, dKr, dVp, Krc, Pc, dKc, dVc, dP)


GROUP = 4


def _att_band(Lc):
    row = np.arange(GROUP * CHUNK)[:, None] % CHUNK
    col = np.arange(3 * CHUNK + Lc)[None, :]
    ok = ((col >= row) & (col <= row + 2 * CHUNK)) | (col >= 3 * CHUNK)
    return jnp.asarray(np.where(ok, 0.0, NEG), F32)


def _att_edge(n, N, Lc):
    col = lax.broadcasted_iota(jnp.int32, (1, 3 * CHUNK + Lc), 1)
    off = jnp.logical_or(jnp.logical_and(col < CHUNK, n == 0),
                         jnp.logical_and(jnp.logical_and(col >= 2 * CHUNK, col < 3 * CHUNK), n == N - 1))
    return jnp.where(off, NEG, 0.0)


def _stack_heads(ref, gi):
    masks = _head_masks()
    tiles = []
    for pr in range(2):
        t = ref[:, (2 * gi + pr) * LANES:(2 * gi + pr + 1) * LANES]
        for a in range(2):
            tiles.append(jnp.where(masks[a], t, jnp.zeros_like(t)))
    return jnp.concatenate(tiles, axis=0)


def _unstack_heads(x4):
    m0 = _head_masks()[0]
    return [jnp.where(m0, x4[(2 * pr) * CHUNK:(2 * pr + 1) * CHUNK], x4[(2 * pr + 1) * CHUNK:(2 * pr + 2) * CHUNK])
            for pr in range(2)]


def _sink_column(sink_ref, g):
    row = lax.broadcasted_iota(jnp.int32, (GROUP * CHUNK, 1), 0) // CHUNK
    col = jnp.zeros((GROUP * CHUNK, 1), F32)
    for h in range(GROUP):
        col = jnp.where(row == h, sink_ref[0, g * GROUP + h], col)
    return col


def _att_probs(q4, Kall, bias, snk):
    s = _nt(q4, Kall) + bias
    mx = jnp.maximum(jnp.max(s, axis=1, keepdims=True), snk)
    p = jnp.exp(s - mx)
    p_snk = jnp.exp(snk - mx)
    inv = 1.0 / (jnp.sum(p, axis=1, keepdims=True) + p_snk)
    return p, p_snk, inv


def _att_groups_per_step(nkv, blk0):
    for gps in (4, 2):
        if nkv % gps == 0 and blk0 % gps == 0:
            return gps
    return 1


def _att_specs(Lc, N, gps):
    q = pl.BlockSpec((CHUNK, gps * 2 * LANES), lambda g, n: (n, g))
    kv = lambda s: pl.BlockSpec((gps, CHUNK, LANES), lambda g, n: (g, jnp.clip(n + s, 0, N - 1), 0))
    ctx = pl.BlockSpec((gps, Lc, LANES), lambda g, n: (g, 0, 0))
    return q, kv, ctx


def _att_fwd(Qa, Kd, Vd, Kdc, Vdc, sink, Y, blk0):
    L = Qa.shape[0]
    Lc = Kdc.shape[1]
    N = L // CHUNK
    nkv = Kd.shape[0]
    gps = _att_groups_per_step(nkv, blk0)

    def body(sink_ref, band_ref, q_ref, kp, kc_, kn, vp, vc_, vn, kctx, vctx, y_in, o_ref):
        del y_in
        g, n = pl.program_id(0), pl.program_id(1)
        bias = band_ref[...] + _att_edge(n, N, Lc)
        for gi in range(gps):
            Kall = jnp.concatenate([kp[gi], kc_[gi], kn[gi], kctx[gi]], axis=0)
            Vall = jnp.concatenate([vp[gi], vc_[gi], vn[gi], vctx[gi]], axis=0)
            p, _, inv = _att_probs(_stack_heads(q_ref, gi), Kall, bias, _sink_column(sink_ref, g * gps + gi))
            o4 = _nn(p.astype(BF16), Vall) * inv
            for pr, o in enumerate(_unstack_heads(o4)):
                o_ref[:, (2 * gi + pr) * LANES:(2 * gi + pr + 1) * LANES] = o.astype(BF16)

    q, kv, ctx = _att_specs(Lc, N, gps)
    band = pl.BlockSpec((GROUP * CHUNK, 3 * CHUNK + Lc), lambda g, n: (0, 0))
    return pl.pallas_call(
        body, name="att_fwd", grid=(nkv // gps, N),
        in_specs=[pl.BlockSpec(memory_space=pltpu.SMEM), band, q, kv(-1), kv(0), kv(1), kv(-1), kv(0), kv(1), ctx, ctx,
                  pl.BlockSpec(memory_space=pl.ANY)],
        out_specs=pl.BlockSpec((CHUNK, gps * 2 * LANES), lambda g, n: (n, blk0 // gps + g)),
        out_shape=jax.ShapeDtypeStruct(Y.shape, Y.dtype),
        input_output_aliases={11: 0},
        compiler_params=_params(2))(sink, _att_band(Lc), Qa, Kd, Kd, Kd, Vd, Vd, Vd, Kdc, Vdc, Y)


def _att_bwd(Qa, Kd, Vd, Kdc, Vdc, sink, dY, blk0):
    L = Qa.shape[0]
    Lc = Kdc.shape[1]
    N = L // CHUNK
    nkv = Kd.shape[0]
    gps = _att_groups_per_step(nkv, blk0)

    def body(sink_ref, band_ref, q_ref, kp, kc_, kn, vp, vc_, vn, kctx, vctx, dy_ref,
             dq_ref, dkp, dkc_, dkn, dvp, dvc_, dvn, dkctx, dvctx, dsink_ref):
        g, n = pl.program_id(0), pl.program_id(1)

        @pl.when(n == 0)
        def _():
            dkctx[...] = jnp.zeros_like(dkctx)
            dvctx[...] = jnp.zeros_like(dvctx)
            dsink_ref[...] = jnp.zeros_like(dsink_ref)

        bias = band_ref[...] + _att_edge(n, N, Lc)
        for gi in range(gps):
            Kall = jnp.concatenate([kp[gi], kc_[gi], kn[gi], kctx[gi]], axis=0)
            Vall = jnp.concatenate([vp[gi], vc_[gi], vn[gi], vctx[gi]], axis=0)
            q4 = _stack_heads(q_ref, gi)
            do4 = _stack_heads(dy_ref, gi)
            p, p_snk, inv = _att_probs(q4, Kall, bias, _sink_column(sink_ref, g * gps + gi))
            P = p * inv
            dp = _nt(do4, Vall)
            delta = jnp.sum(P * dp, axis=1, keepdims=True)
            ds = (P * (dp - delta)).astype(BF16)
            dsnk = -(p_snk * inv) * delta
            for h in range(GROUP):
                dsink_ref[gi, h:h + 1, :] += _fsum(dsnk[h * CHUNK:(h + 1) * CHUNK])
            for pr, dq in enumerate(_unstack_heads(_nn(ds, Kall))):
                dq_ref[:, (2 * gi + pr) * LANES:(2 * gi + pr + 1) * LANES] = dq
            dK = _tn(ds, q4)
            dV = _tn(P.astype(BF16), do4)
            for j, (rk, rv) in enumerate([(dkp, dvp), (dkc_, dvc_), (dkn, dvn)]):
                rk[gi] = dK[j * CHUNK:(j + 1) * CHUNK].astype(BF16)
                rv[gi] = dV[j * CHUNK:(j + 1) * CHUNK].astype(BF16)
            dkctx[gi] += dK[3 * CHUNK:]
            dvctx[gi] += dV[3 * CHUNK:]

    q, kv, ctx = _att_specs(Lc, N, gps)
    band = pl.BlockSpec((GROUP * CHUNK, 3 * CHUNK + Lc), lambda g, n: (0, 0))
    blk = pl.BlockSpec((gps, CHUNK, LANES), lambda g, n: (g, n, 0))
    part = jax.ShapeDtypeStruct((nkv, L, LANES), BF16)
    cshape = jax.ShapeDtypeStruct((nkv, Lc, LANES), F32)
    return pl.pallas_call(
        body, name="att_bwd", grid=(nkv // gps, N),
        in_specs=[pl.BlockSpec(memory_space=pltpu.SMEM), band, q, kv(-1), kv(0), kv(1), kv(-1), kv(0), kv(1), ctx, ctx,
                  pl.BlockSpec((CHUNK, gps * 2 * LANES), lambda g, n: (n, blk0 // gps + g))],
        out_specs=[q, blk, blk, blk, blk, blk, blk, ctx, ctx,
                   pl.BlockSpec((gps, 8, LANES), lambda g, n: (g, 0, 0))],
        out_shape=[jax.ShapeDtypeStruct(Qa.shape, F32), part, part, part, part, part, part, cshape, cshape,
                   jax.ShapeDtypeStruct((nkv, 8, LANES), F32)],
        compiler_params=_params(2))(sink, _att_band(Lc), Qa, Kd, Kd, Kd, Vd, Vd, Vd, Kdc, Vdc, dY)


def _scale_rows(dx, gt, saved, name):
    M, D = dx.shape
    tm = _tile(M, 512, 8)

    def body(dx_ref, gt_ref, sv_ref, dz_ref, dgt_ref):
        @pl.when(pl.program_id(0) == 0)
        def _():
            dgt_ref[...] = jnp.zeros_like(dgt_ref)
        d = dx_ref[...]
        dz_ref[...] = (d * gt_ref[...]).astype(BF16)
        dgt_ref[...] += jnp.sum(d * sv_ref[...].astype(F32), axis=0, keepdims=True)

    row = pl.BlockSpec((tm, D), lambda i: (i, 0))
    vec = pl.BlockSpec((1, D), lambda i: (0, 0))
    return pl.pallas_call(
        body, name=name, grid=(M // tm,), in_specs=[row, vec, row], out_specs=[row, vec],
        out_shape=[jax.ShapeDtypeStruct((M, D), BF16), jax.ShapeDtypeStruct((1, D), F32)],
        compiler_params=_params(1))(dx, gt, saved)


def _bwd_proj(dz, w, G=None, U=None, name="bwd_proj"):
    M, D = dz.shape
    N = w.shape[0]
    swiglu = G is not None
    tm, tn = _tile(M, ROWS_PER_LATCH, 8), _tile(N, 512)

    def body(*refs):
        if swiglu:
            dz_ref, w_ref, G_ref, U_ref, dG_ref, dU_ref = refs
        else:
            dz_ref, w_ref, dA_ref = refs
        dA = _nt(dz_ref[...], w_ref[...])
        if swiglu:
            Gv = G_ref[...].astype(F32)
            Uv = U_ref[...].astype(F32)
            sg = _sigmoid(Gv)
            dU_ref[...] = (dA * Gv * sg).astype(BF16)
            dG_ref[...] = (dA * Uv * (sg * (1.0 + Gv * (1.0 - sg)))).astype(BF16)
        else:
            dA_ref[...] = dA.astype(BF16)

    row = pl.BlockSpec((tm, D), lambda i, j: (i, 0))
    tile = pl.BlockSpec((tm, tn), lambda i, j: (i, j))
    big = jax.ShapeDtypeStruct((M, N), BF16)
    in_specs = [row, pl.BlockSpec((tn, D), lambda i, j: (j, 0))]
    args = [dz, w]
    if swiglu:
        in_specs += [tile, tile]
        args += [G, U]
        out_specs, out_shape = [tile, tile], [big, big]
    else:
        out_specs, out_shape = tile, big
    return pl.pallas_call(
        body, name=name, grid=(M // tm, N // tn), in_specs=in_specs, out_specs=out_specs, out_shape=out_shape,
        compiler_params=_params(2))(*args)


def _tn_matmul(pairs, name):
    Ka, Nb = pairs[0][0].shape[1], pairs[0][1].shape[1]
    tk, tn = _tile(Ka, 2048), _tile(Nb, 2048)
    tls, nks = [], []
    for a, _ in pairs:
        tl = _tile(a.shape[0], 1024, 8)
        tls.append(tl)
        nks.append(a.shape[0] // tl)
    starts = [int(s) for s in np.cumsum([0] + nks[:-1])]
    nk = int(sum(nks))

    def body(*refs):
        out_ref, acc = refs[-2], refs[-1]
        k = pl.program_id(2)

        @pl.when(k == 0)
        def _():
            acc[...] = jnp.zeros_like(acc)

        for idx in range(len(pairs)):
            a_ref, b_ref = refs[2 * idx], refs[2 * idx + 1]

            @pl.when(jnp.logical_and(k >= starts[idx], k < starts[idx] + nks[idx]))
            def _():
                acc[...] += _tn(a_ref[...], b_ref[...])

        @pl.when(k == nk - 1)
        def _():
            out_ref[...] = acc[...].astype(BF16)

    in_specs, args = [], []
    for idx, (a, b) in enumerate(pairs):
        s0, n_ = starts[idx], nks[idx]
        in_specs.append(pl.BlockSpec((tls[idx], tk), lambda i, j, k, s0=s0, n_=n_: (jnp.clip(k - s0, 0, n_ - 1), i)))
        in_specs.append(pl.BlockSpec((tls[idx], tn), lambda i, j, k, s0=s0, n_=n_: (jnp.clip(k - s0, 0, n_ - 1), j)))
        args += [a, b]
    return pl.pallas_call(
        body, name=name, grid=(Ka // tk, Nb // tn, nk), in_specs=in_specs,
        out_specs=pl.BlockSpec((tk, tn), lambda i, j, k: (i, j)),
        out_shape=jax.ShapeDtypeStruct((Ka, Nb), BF16),
        scratch_shapes=[pltpu.VMEM((tk, tn), F32)], compiler_params=_params(3))(*args)


def _bwd_norm_mod(pairs, x, dres, g, sh, sc, name):
    M, D = x.shape
    K = pairs[0][0].shape[1]
    tm, tk = _tile(M, 512, 8), _tile(K, 1152 if len(pairs) == 1 else 512)
    nk = K // tk
    npair = len(pairs)
    has_res = dres is not None

    def body(*refs):
        pr = refs[:2 * npair]
        rest = refs[2 * npair:]
        if has_res:
            x_ref, dres_ref, g_ref, sh_ref, sc_ref, dx_ref, st_ref, acc = rest
        else:
            x_ref, g_ref, sh_ref, sc_ref, dx_ref, st_ref, acc = rest
        del sh_ref
        i, k = pl.program_id(0), pl.program_id(1)

        @pl.when(jnp.logical_and(i == 0, k == 0))
        def _():
            st_ref[...] = jnp.zeros_like(st_ref)

        @pl.when(k == 0)
        def _():
            acc[...] = jnp.zeros_like(acc)

        t = _nt(pr[1][...], pr[0][...])
        for idx in range(1, npair):
            t = t + _nt(pr[2 * idx + 1][...], pr[2 * idx][...])
        acc[...] += t

        @pl.when(k == nk - 1)
        def _():
            xv = x_ref[...]
            gv = g_ref[...]
            dh = acc[...].T
            r = lax.rsqrt(jnp.mean(xv * xv, axis=-1, keepdims=True) + NORM_EPS)
            xh = xv * r
            st_ref[0:1, :] += jnp.sum(dh, axis=0, keepdims=True)
            st_ref[1:2, :] += jnp.sum(dh * (xh * gv), axis=0, keepdims=True)
            dn = dh * (1.0 + sc_ref[...])
            st_ref[2:3, :] += jnp.sum(dn * xh, axis=0, keepdims=True)
            dxh = dn * gv
            d = r * (dxh - xh * jnp.mean(dxh * xh, axis=-1, keepdims=True))
            if has_res:
                d = d + dres_ref[...]
            dx_ref[...] = d

    row = pl.BlockSpec((tm, D), lambda i, k: (i, 0))
    vec = pl.BlockSpec((1, D), lambda i, k: (0, 0))
    in_specs, args = [], []
    for dA, w in pairs:
        in_specs += [pl.BlockSpec((tm, tk), lambda i, k: (i, k)), pl.BlockSpec((D, tk), lambda i, k: (0, k))]
        args += [dA, w]
    in_specs += [row] + ([row] if has_res else []) + [vec, vec, vec]
    args += [x] + ([dres] if has_res else []) + [g, sh, sc]
    return pl.pallas_call(
        body, name=name, grid=(M // tm, nk), in_specs=in_specs,
        out_specs=[row, pl.BlockSpec((8, D), lambda i, k: (0, 0))],
        out_shape=[jax.ShapeDtypeStruct((M, D), F32), jax.ShapeDtypeStruct((8, D), F32)],
        scratch_shapes=[pltpu.VMEM((D, tm), F32)], compiler_params=_params(2))(*args)


def _local_step(x, ctx, tgt, mod, modc, norm_mix, norm_ffn, norm_final, lg, sink, T, w_in, rest_weights, on_grads, flush):
    L, D = x.shape
    Lc = ctx.shape[0]
    d_proj = w_in.shape[1]
    npairs = RET_HEADS // 2
    nkv = ATT_KV_HEADS
    nkvp = nkv // 2
    o_rq = 0
    o_rk = o_rq + RET_HEADS * RET_DK // LANES
    o_rv = o_rk + RET_HEADS * RET_DK // LANES
    o_rg = o_rv + RET_HEADS * RET_DV // LANES
    o_aq = o_rg + RET_HEADS * RET_DV // LANES
    o_ak = o_aq + ATT_HEADS * ATT_DH // LANES
    o_av = o_ak + nkv * ATT_DH // LANES
    assert (o_av + nkv * ATT_DH // LANES) * LANES == d_proj
    assert o_rv % 2 == 0 and o_rg % 2 == 0 and (RET_HEADS * RET_DV) % (2 * LANES) == 0
    rv_blk, rg_blk = o_rv // 2, o_rg // 2
    d_ret = RET_HEADS * RET_DV
    d_mix = d_ret + ATT_HEADS * ATT_DH
    att_blk = d_ret // (2 * LANES)
    k_scale = RET_DK ** -0.5
    a_scale = ATT_DH ** -0.5

    Tc = dict(C=jnp.ones((Lc, LANES), F32), S=jnp.zeros((Lc, LANES), F32))
    row = lambda m, i: m[i:i + 1]
    sh_m, sc_m, gt_m, sh_f, sc_f, gt_f = [row(mod, i) for i in range(6)]
    sh_mc, sc_mc = row(modc, 0), row(modc, 1)

    P, hx = _norm_mod_matmul(x, norm_mix, sh_m, sc_m, w_in, "in_proj")
    Pc, hc = _norm_mod_matmul(ctx, norm_mix, sh_mc, sc_mc, w_in, "in_proj_ctx")
    nq = RET_HEADS * RET_DK // LANES
    Qr = _rope_cols(P, o_rq, nq, T["Cr"], T["Sr"], T["Rr"], 1.0, True, "rope_rq")
    Kr = _rope_cols(P, o_rk, nq, T["Cr"], T["Sr"], T["Rr"], k_scale, True, "rope_rk")
    Krc = _rope_cols(Pc, o_rk, nq, Tc["C"], Tc["S"], T["Rr"], k_scale, False, "scale_rk_ctx")
    Qa = _rope_cols(P, o_aq, ATT_HEADS * ATT_DH // LANES, T["Ca"], T["Sa"], T["Ra"], a_scale, True, "rope_aq")
    Kd = _dup_heads(P, o_ak, nkvp, T["Ca"], T["Sa"], T["Ra"], T["D0"], T["D1"], True, "dup_ak")
    Vd = _dup_heads(P, o_av, nkvp, T["Ca"], T["Sa"], T["Ra"], T["D0"], T["D1"], False, "dup_av")
    Kdc = _dup_heads(Pc, o_ak, nkvp, Tc["C"], Tc["S"], T["Ra"], T["D0"], T["D1"], False, "dup_ak_ctx")
    Vdc = _dup_heads(Pc, o_av, nkvp, Tc["C"], Tc["S"], T["Ra"], T["D0"], T["D1"], False, "dup_av_ctx")

    SF = _ret_states_fwd(Kr, P, Krc, Pc, lg, rv_blk, npairs)
    Y, SB = _ret_out_fwd(Qr, Kr, P, Krc, Pc, SF, lg, rv_blk, rg_blk, npairs, d_mix)
    tok = rest_weights(Y)
    Y = _att_fwd(Qa, Kd, Vd, Kdc, Vdc, sink + tok, Y, att_blk)

    w_out, w_gate, w_up, w_down = rest_weights(Y)
    x1, O1 = _proj_residual(Y, w_out, x, gt_m, "out_proj")
    G, U, A, h2 = _ffn_in(x1, norm_ffn, sh_f, sc_f, w_gate, w_up)
    x2, Fo = _proj_residual(A, w_down, x1, gt_f, "ffn_out")
    dx2, loss, d_norm_final, dz2, dgt_f = _final(x2, norm_final, tgt, gt_f, Fo)

    dG, dU = _bwd_proj(dz2, w_down, G, U, name="ffn_out_bwd")
    g_w_down = _tn_matmul([(A, dz2)], "grad_w_down")
    tok = on_grads(["w_down"], [g_w_down])
    dx1, st_f = _bwd_norm_mod([(dG, w_gate), (dU, w_up)], x1, dx2, norm_ffn + tok, sh_f, sc_f, "ffn_in_bwd")
    tok = flush(dx1)
    g_w_gate = _tn_matmul([(h2, dG)], "grad_w_gate")
    g_w_up = _tn_matmul([(h2, dU)], "grad_w_up")
    tok = tok + on_grads(["w_gate", "w_up"], [g_w_gate, g_w_up])
    dz1, dgt_m = _scale_rows(dx1, gt_m + tok, O1, "mix_gate_bwd")
    dY = _bwd_proj(dz1, w_out, name="out_proj_bwd")
    tok = flush(dY)
    g_w_out = _tn_matmul([(Y, dz1)], "grad_w_out")
    tok = tok + on_grads(["w_out"], [g_w_out])

    dQa, dKp, dKs, dKn, dVp, dVs, dVn, dKdc, dVdc, dsink = _att_bwd(Qa, Kd, Vd, Kdc, Vdc, sink + tok, dY, att_blk)
    tok = flush(dQa)
    dQr, dKr, dVr, dP, dO, dKc, dVc, acc1 = _ret_bwd1(Qr, Kr, P, Krc, Pc, SF, SB, dY, lg + tok, rv_blk, rg_blk, npairs,
                                                      d_proj)
    dKr, dP, dKc, dVc, acc2 = _ret_bwd2(Qr, Kr, P, Krc, Pc, SB, dO, dKr, dVr, dP, dKc, dVc, lg, rv_blk, npairs)

    dP = _unrope_cols(dQr, dP, o_rq, nq, T["Cr"], T["Sr"], T["RrT"], 1.0, True, "unrope_rq")
    dP = _unrope_cols(dKr, dP, o_rk, nq, T["Cr"], T["Sr"], T["RrT"], k_scale, True, "unrope_rk")
    dP = _unrope_cols(dQa, dP, o_aq, ATT_HEADS * ATT_DH // LANES, T["Ca"], T["Sa"], T["RaT"], a_scale, True, "unrope_aq")
    dP = _fold_heads([(dKs, 0), (dKp, 1), (dKn, -1)], dP, o_ak, nkvp, T["Ca"], T["Sa"], T["RaT"], T["D0T"], T["D1T"],
                     True, "fold_ak")
    dP = _fold_heads([(dVs, 0), (dVp, 1), (dVn, -1)], dP, o_av, nkvp, T["Ca"], T["Sa"], T["RaT"], T["D0T"], T["D1T"],
                     False, "fold_av")
    dPc = jnp.zeros((Lc, d_proj), BF16)
    dPc = _unrope_cols(dKc, dPc, o_rk, nq, Tc["C"], Tc["S"], T["RrT"], k_scale, False, "ctx_rk_bwd")
    dPc = _unrope_cols(dVc, dPc, o_rv, RET_HEADS * RET_DV // LANES, Tc["C"], Tc["S"], T["RrT"], 1.0, False, "ctx_rv_bwd")
    dPc = _fold_heads([(dKdc.astype(BF16), 0)], dPc, o_ak, nkvp, Tc["C"], Tc["S"], T["RaT"], T["D0T"], T["D1T"],
                      False, "fold_ak_ctx")
    dPc = _fold_heads([(dVdc.astype(BF16), 0)], dPc, o_av, nkvp, Tc["C"], Tc["S"], T["RaT"], T["D0T"], T["D1T"],
                      False, "fold_av_ctx")

    dx, st_m = _bwd_norm_mod([(dP, w_in)], x, dx1, norm_mix, sh_m, sc_m, "in_proj_bwd")
    _, st_mc = _bwd_norm_mod([(dPc, w_in)], ctx, None, norm_mix, sh_mc, sc_mc, "in_proj_ctx_bwd")
    g_w_in = _tn_matmul([(hx, dP), (hc, dPc)], "grad_w_in")
    on_grads(["w_in"], [g_w_in])

    a1 = acc1[:, :, :, 0].reshape(RET_HEADS, ACC_ROWS)
    a2 = acc2[:, :, :, 0].reshape(RET_HEADS, ACC_ROWS)
    dlam = (a1[:, 0] + a1[:, 2] + a1[:, 3] + a1[:, 4]) * lg[0]
    dmu = (a1[:, 1] + a1[:, 5] + a2[:, 0] + a2[:, 1]) * lg[1]
    d_sink = dsink[:, :4, 0].reshape(1, ATT_HEADS)

    nh = RET_HEADS
    assert 2 * nh + ATT_HEADS <= LOSS_LANE
    small = _pack_rows(
        [(st_m, 0, 2, 0, 0), (dgt_m, 0, 1, 2, 0), (st_f, 0, 2, 3, 0), (dgt_f, 0, 1, 5, 0), (st_mc, 0, 2, 6, 0),
         (st_m[2:3] + st_mc[2:3], 0, 1, 12, 0), (st_f, 2, 1, 13, 0), (d_norm_final, 0, 1, 14, 0),
         (dlam.reshape(1, nh), 0, 1, 15, 0), (dmu.reshape(1, nh), 0, 1, 15, nh), (d_sink, 0, 1, 15, 2 * nh),
         (loss[:, 0:1], 0, 1, 15, LOSS_LANE)], 16, D, "pack_small")
    return dict(grad_x=dx, small=small)


def _my_pos():
    return lax.axis_index("x"), lax.axis_index("y"), lax.axis_index("c")


def _other_chips(x, y):
    return [(1 - x, y), (x, 1 - y), (1 - x, 1 - y)]


def _remote(src, dst, ssem, rsem, dev):
    return pltpu.make_async_remote_copy(src_ref=src, dst_ref=dst, send_sem=ssem, recv_sem=rsem,
                                        device_id=dev, device_id_type=MESH)


def _allgather8(v, name):
    R, Cc = v.shape

    def body(v_ref, out_ref, send_sems, recv_sems):
        x, y, c = _my_pos()
        me = 4 * x + 2 * y + c
        out_ref[pl.ds(me, 1)] = v_ref[...][None]
        peers = []
        for j in range(1, N_DEV):
            peers.append((1 - x if (j >> 2) & 1 else x, 1 - y if (j >> 1) & 1 else y, 1 - c if j & 1 else c))
        copies = []
        for j, peer in enumerate(peers):
            cp = _remote(v_ref, out_ref.at[me], send_sems.at[j], recv_sems.at[j], peer)
            cp.start()
            copies.append(cp)
        for j, peer in enumerate(peers):
            pid = 4 * peer[0] + 2 * peer[1] + peer[2]
            _remote(v_ref, out_ref.at[pid], send_sems.at[j], recv_sems.at[j], peer).wait_recv()
        for cp in copies:
            cp.wait_send()

    return pl.pallas_call(
        body, name=name, out_shape=jax.ShapeDtypeStruct((N_DEV, R, Cc), v.dtype),
        in_specs=[pl.BlockSpec(memory_space=pltpu.VMEM)], out_specs=pl.BlockSpec(memory_space=pltpu.VMEM),
        scratch_shapes=[pltpu.SemaphoreType.DMA((N_DEV - 1,)), pltpu.SemaphoreType.DMA((N_DEV - 1,))])(v)


def _region(ref, k, half, shard_shape, axis):
    r, cs = shard_shape
    hr = r // 2
    if axis == 1:
        return ref.at[pl.ds(pl.multiple_of(half * hr, 16), hr), pl.ds(pl.multiple_of(k * cs, LANES), cs)]
    return ref.at[pl.ds(pl.multiple_of(k * r + half * hr, 16), hr), :]


def _full_shape(shard_shape, axis):
    r, cs = shard_shape
    return (r, N_CHIPS * cs) if axis == 1 else (N_CHIPS * r, cs)


def _half_pieces(ref, half, shard_shape, axis):
    r, cs = shard_shape
    hr = r // 2
    if axis == 1:
        return [ref.at[pl.ds(pl.multiple_of(half * hr, 16), hr), :]]
    return [ref.at[pl.ds(pl.multiple_of(k * r + half * hr, 16), hr), :] for k in range(N_CHIPS)]


def _half_block_spec(shard_shape, axis, tr):
    r, cs = shard_shape
    hr = r // 2
    if axis == 1:
        return pl.BlockSpec((tr, cs), lambda k, i, c_ref: (c_ref[0] * (hr // tr) + i, k))
    return pl.BlockSpec((tr, cs), lambda k, i, c_ref: (k * (r // tr) + c_ref[0] * (hr // tr) + i, 0))


def _add_halves(g, recv, cvec, shard_shape, axis, name):
    r, cs = shard_shape
    hr = r // 2
    tr = _tile(hr, 256, 16)

    def body(c_ref, a_ref, b_ref, o_ref):
        del c_ref
        o_ref[0] = (a_ref[...].astype(F32) + b_ref[...].astype(F32)).astype(BF16)

    spec = _half_block_spec(shard_shape, axis, tr)
    return pl.pallas_call(
        body, name=name,
        grid_spec=pltpu.PrefetchScalarGridSpec(
            num_scalar_prefetch=1, grid=(N_CHIPS, hr // tr), in_specs=[spec, spec],
            out_specs=pl.BlockSpec((1, tr, cs), lambda k, i, c_ref: (k, i, 0))),
        out_shape=jax.ShapeDtypeStruct((N_CHIPS, hr, cs), BF16),
        compiler_params=_params(2, False))(cvec, g, recv)


def _sum_chips(sums, landed, kc, name):
    _, hr, cs = sums.shape
    tr = _tile(hr, 256, 16)

    def body(kc_ref, own_ref, a_ref, b_ref, c_ref, o_ref):
        del kc_ref
        o_ref[...] = (own_ref[0].astype(F32) + a_ref[0].astype(F32)) + (b_ref[0].astype(F32) + c_ref[0].astype(F32))

    slot = lambda j: pl.BlockSpec((1, tr, cs), lambda i, kc_ref: ((kc_ref[0] + j) % N_CHIPS, i, 0))
    return pl.pallas_call(
        body, name=name,
        grid_spec=pltpu.PrefetchScalarGridSpec(
            num_scalar_prefetch=1, grid=(hr // tr,), in_specs=[slot(0), slot(1), slot(2), slot(3)],
            out_specs=pl.BlockSpec((tr, cs), lambda i, kc_ref: (kc_ref[1] * (hr // tr) + i, 0))),
        out_shape=jax.ShapeDtypeStruct((2 * hr, cs), F32),
        compiler_params=_params(1, False))(kc, sums, landed, landed, landed)


def _exchange_halves(shards, name):
    nw = len(shards)

    def body(*refs):
        out_refs = refs[nw:2 * nw]
        send, recv = refs[2 * nw:]
        x, y, c = _my_pos()
        sib = (x, y, 1 - c)
        copies = []
        for w in range(nw):
            hr = shards[w].shape[0] // 2
            mine = out_refs[w].at[pl.ds(pl.multiple_of(c * hr, 8), hr), :]
            cp = _remote(mine, mine, send.at[w], recv.at[w], sib)
            cp.start()
            copies.append(cp)
        for w in range(nw):
            hr = shards[w].shape[0] // 2
            other = out_refs[w].at[pl.ds(pl.multiple_of((1 - c) * hr, 8), hr), :]
            _remote(other, other, send.at[w], recv.at[w], sib).wait_recv()
        for cp in copies:
            cp.wait_send()

    anyspec = pl.BlockSpec(memory_space=pl.ANY)
    return pl.pallas_call(
        body, name=name,
        out_shape=[jax.ShapeDtypeStruct(s.shape, F32) for s in shards],
        in_specs=[anyspec] * nw, out_specs=[anyspec] * nw,
        input_output_aliases={w: w for w in range(nw)},
        scratch_shapes=[pltpu.SemaphoreType.DMA((nw,)), pltpu.SemaphoreType.DMA((nw,))])(*shards)


def _cast_into_full(w, kc, axis, name, after=None):
    r, cs = w.shape
    tr = _tile(r, 256, 16)
    tied = after is not None

    def body(kc_ref, w_ref, *rest):
        del kc_ref
        if tied:
            rest[1][...] = w_ref[...].astype(BF16)
            rest[2][...] = jnp.zeros_like(rest[2])
        else:
            rest[0][...] = w_ref[...].astype(BF16)

    if axis == 1:
        ospec = pl.BlockSpec((tr, cs), lambda i, kc_ref: (i, kc_ref[0]))
    else:
        ospec = pl.BlockSpec((tr, cs), lambda i, kc_ref: (kc_ref[0] * (r // tr) + i, 0))
    tok = pl.BlockSpec((8, LANES), lambda i, kc_ref: (0, 0))
    full = jax.ShapeDtypeStruct(_full_shape((r, cs), axis), BF16)
    in_specs, args = [pl.BlockSpec((tr, cs), lambda i, kc_ref: (i, 0))], [w]
    if tied:
        in_specs.append(tok)
        args.append(after)
    return pl.pallas_call(
        body, name=name,
        grid_spec=pltpu.PrefetchScalarGridSpec(num_scalar_prefetch=1, grid=(r // tr,), in_specs=in_specs,
                                               out_specs=[ospec, tok] if tied else ospec),
        out_shape=[full, jax.ShapeDtypeStruct((8, LANES), F32)] if tied else full,
        compiler_params=_params(1, False))(kc, *args)


def _adam_math(w, g, m, v):
    m2 = ADAM_B1 * m + (1.0 - ADAM_B1) * g
    v2 = ADAM_B2 * v + (1.0 - ADAM_B2) * (g * g)
    m_hat = m2 / (1.0 - ADAM_B1 ** ADAM_STEP)
    v_hat = v2 / (1.0 - ADAM_B2 ** ADAM_STEP)
    delta = -ADAM_LR * (m_hat / (jnp.sqrt(v_hat) + ADAM_EPS) + ADAM_WD * w)
    return delta, m2, v2


def _adam(w, g, m, v, name):
    r, cs = w.shape
    tr = _tile(r, 256, 8)

    def body(w_ref, g_ref, m_ref, v_ref, d_ref, m2_ref, v2_ref):
        d, m2, v2 = _adam_math(w_ref[...], g_ref[...], m_ref[...], v_ref[...])
        d_ref[...] = d
        m2_ref[...] = m2
        v2_ref[...] = v2

    spec = pl.BlockSpec((tr, cs), lambda i: (i, 0))
    shp = jax.ShapeDtypeStruct((r, cs), F32)
    return pl.pallas_call(body, name=name, grid=(r // tr,), in_specs=[spec] * 4, out_specs=[spec] * 3,
                          out_shape=[shp, shp, shp], compiler_params=_params(1, False))(w, g, m, v)


def _mod_rows(a16, w, b, name):
    D, n = w.shape
    tn = _tile(n, 512)

    def body(a_ref, w_ref, b_ref, o_ref):
        a = a_ref[...]
        o_ref[...] = _nn((a * _sigmoid(a)).astype(BF16), w_ref[...].astype(BF16)) + b_ref[...]

    return pl.pallas_call(
        body, name=name, grid=(n // tn,),
        in_specs=[pl.BlockSpec((16, D), lambda j: (0, 0)), pl.BlockSpec((D, tn), lambda j: (0, j)),
                  pl.BlockSpec((1, tn), lambda j: (0, j))],
        out_specs=pl.BlockSpec((16, tn), lambda j: (0, j)),
        out_shape=jax.ShapeDtypeStruct((16, n), F32), compiler_params=_params(1, False))(a16, w, b)


def _w_mod_update(a16, d16, w, m, v):
    D, n = w.shape
    tn = _tile(n, 256)

    def body(a_ref, d_ref, w_ref, m_ref, v_ref, g_ref, dl_ref, m2_ref, v2_ref, p_ref):
        @pl.when(pl.program_id(0) == 0)
        def _():
            p_ref[...] = jnp.zeros_like(p_ref)
        a = a_ref[...]
        db = d_ref[...].astype(BF16)
        wv = w_ref[...]
        g = _tn((a * _sigmoid(a)).astype(BF16), db)
        g_ref[...] = g
        d, m2, v2 = _adam_math(wv, g, m_ref[...], v_ref[...])
        dl_ref[...] = d
        m2_ref[...] = m2
        v2_ref[...] = v2
        p_ref[...] += _nt(db, wv.astype(BF16))

    wspec = pl.BlockSpec((D, tn), lambda j: (0, j))
    shp = jax.ShapeDtypeStruct((D, n), F32)
    return pl.pallas_call(
        body, name="w_mod_update", grid=(n // tn,),
        in_specs=[pl.BlockSpec((16, D), lambda j: (0, 0)), pl.BlockSpec((16, tn), lambda j: (0, j)), wspec, wspec, wspec],
        out_specs=[wspec, wspec, wspec, wspec, pl.BlockSpec((16, D), lambda j: (0, 0))],
        out_shape=[shp, shp, shp, shp, jax.ShapeDtypeStruct((16, D), F32)],
        compiler_params=_params(1))(a16, d16, w, m, v)


def _sum_devices(g8, name):
    _, R, Cc = g8.shape

    def body(g_ref, o_ref):
        t = g_ref[0]
        for d in range(1, N_DEV):
            t = t + g_ref[d]
        o_ref[...] = t

    return pl.pallas_call(body, name=name, out_shape=jax.ShapeDtypeStruct((R, Cc), F32))(g8)


def _c_ctx_grad(parts, c_ctx):
    D = c_ctx.shape[1]

    def body(p_ref, c_ref, o_ref):
        t = p_ref[0]
        for k in range(1, N_CHIPS):
            t = t + p_ref[2 * k]
        cv = c_ref[...]
        sg = _sigmoid(cv)
        o_ref[...] = t * (sg * (1.0 + cv * (1.0 - sg)))

    return pl.pallas_call(body, name="c_ctx_grad", out_shape=jax.ShapeDtypeStruct((1, D), F32))(parts, c_ctx)


def _pack_rows(items, nrows, width, name):
    arrays, plan = [], []
    for a, r0, nr, d0, c0 in items:
        for ai, b in enumerate(arrays):
            if b is a:
                break
        else:
            ai = len(arrays)
            arrays.append(a)
        plan.append((ai, r0, nr, d0, c0, a.shape[1]))

    def body(*refs):
        o_ref = refs[-1]
        o_ref[...] = jnp.zeros_like(o_ref)
        for ai, r0, nr, d0, c0, w in plan:
            o_ref[d0:d0 + nr, c0:c0 + w] = refs[ai][r0:r0 + nr, :]

    return pl.pallas_call(body, name=name, out_shape=jax.ShapeDtypeStruct((nrows, width), F32))(*arrays)


HBM_SPEC = pl.BlockSpec(memory_space=pltpu.HBM)
SEM_SPEC = pl.BlockSpec(memory_space=pltpu.SEMAPHORE)
SPLIT_PARAMS = pltpu.CompilerParams(has_side_effects=pltpu.SideEffectType.DATAFLOW_SIDE_EFFECTING)


def _in_hbm(a):
    return pltpu.with_memory_space_constraint(a, pltpu.HBM)


def _ag_chips_start(fulls, shapes, axes, after, name):
    nw = len(fulls)

    def body(*refs):
        in_refs, send, recv, token = refs[:nw], refs[nw + 1], refs[nw + 2], refs[-1]
        x, y, c = _my_pos()
        k0 = 2 * x + y
        for w in range(nw):
            own = _region(in_refs[w], k0, c, shapes[w], axes[w])
            for j, ch in enumerate(_other_chips(x, y)):
                _remote(own, own, send.at[3 * w + j], recv.at[3 * w + j], (ch[0], ch[1], c)).start()
        token[...] = jnp.zeros_like(token)

    return pl.pallas_call(
        body, name=name,
        out_shape=(pltpu.SemaphoreType.DMA((3 * nw,)), pltpu.SemaphoreType.DMA((3 * nw,)),
                   *[pltpu.HBM(f.shape, f.dtype) for f in fulls], jax.ShapeDtypeStruct((8, LANES), F32)),
        in_specs=[HBM_SPEC] * nw + [pl.BlockSpec(memory_space=pl.ANY)],
        out_specs=(SEM_SPEC, SEM_SPEC, *[HBM_SPEC] * nw, pl.BlockSpec(memory_space=pltpu.VMEM)),
        input_output_aliases={w: 2 + w for w in range(nw)},
        compiler_params=SPLIT_PARAMS)(*[_in_hbm(f) for f in fulls], after)


def _ag_chips_wait(send, recv, fulls, shapes, axes, after, name):
    nw = len(fulls)

    def body(*refs):
        in_refs, send_ref, recv_ref = refs[:nw], refs[nw], refs[nw + 1]
        x, y, c = _my_pos()
        k0 = 2 * x + y
        for w in range(nw):
            own = _region(in_refs[w], k0, c, shapes[w], axes[w])
            for j, ch in enumerate(_other_chips(x, y)):
                got = _region(in_refs[w], 2 * ch[0] + ch[1], c, shapes[w], axes[w])
                cp = _remote(own, got, send_ref.at[3 * w + j], recv_ref.at[3 * w + j], (ch[0], ch[1], c))
                cp.wait_send()
                cp.wait_recv()

    return pl.pallas_call(
        body, name=name,
        out_shape=tuple(pltpu.HBM(f.shape, f.dtype) for f in fulls),
        in_specs=[HBM_SPEC] * nw + [SEM_SPEC, SEM_SPEC, pl.BlockSpec(memory_space=pl.ANY)],
        out_specs=tuple([HBM_SPEC] * nw),
        input_output_aliases={w: w for w in range(nw)},
        compiler_params=SPLIT_PARAMS)(*fulls, send, recv, after)


def _ag_forward(fulls, shapes, axes, name):
    nw = len(fulls)

    def body(*refs):
        out_refs = refs[nw:2 * nw]
        send, recv = refs[2 * nw:]
        x, y, c = _my_pos()
        sib = (x, y, 1 - c)
        chips = _other_chips(x, y)
        copies = []
        for w in range(nw):
            for j, ch in enumerate(chips):
                got = _region(out_refs[w], 2 * ch[0] + ch[1], c, shapes[w], axes[w])
                cp = _remote(got, got, send.at[w, j], recv.at[w, j], sib)
                cp.start()
                copies.append(cp)
        for w in range(nw):
            for j, ch in enumerate(chips):
                got = _region(out_refs[w], 2 * ch[0] + ch[1], 1 - c, shapes[w], axes[w])
                _remote(got, got, send.at[w, j], recv.at[w, j], sib).wait_recv()
        for cp in copies:
            cp.wait_send()

    anyspec = pl.BlockSpec(memory_space=pl.ANY)
    return pl.pallas_call(
        body, name=name,
        out_shape=[jax.ShapeDtypeStruct(f.shape, BF16) for f in fulls],
        in_specs=[anyspec] * nw, out_specs=[anyspec] * nw,
        input_output_aliases={w: w for w in range(nw)},
        scratch_shapes=[pltpu.SemaphoreType.DMA((nw, 3)), pltpu.SemaphoreType.DMA((nw, 3))])(*fulls)


def _ag_forward_start(fulls, shapes, axes, name):
    nw = len(fulls)

    def body(*refs):
        in_refs, send, recv, token = refs[:nw], refs[nw], refs[nw + 1], refs[-1]
        x, y, c = _my_pos()
        for w in range(nw):
            for j, ch in enumerate(_other_chips(x, y)):
                got = _region(in_refs[w], 2 * ch[0] + ch[1], c, shapes[w], axes[w])
                _remote(got, got, send.at[3 * w + j], recv.at[3 * w + j], (x, y, 1 - c)).start()
        token[...] = jnp.zeros_like(token)

    return pl.pallas_call(
        body, name=name,
        out_shape=(pltpu.SemaphoreType.DMA((3 * nw,)), pltpu.SemaphoreType.DMA((3 * nw,)),
                   *[pltpu.HBM(f.shape, f.dtype) for f in fulls], jax.ShapeDtypeStruct((8, LANES), F32)),
        in_specs=[HBM_SPEC] * nw,
        out_specs=(SEM_SPEC, SEM_SPEC, *[HBM_SPEC] * nw, pl.BlockSpec(memory_space=pltpu.VMEM)),
        input_output_aliases={w: 2 + w for w in range(nw)},
        compiler_params=SPLIT_PARAMS)(*[_in_hbm(f) for f in fulls])


def _ag_forward_wait(send, recv, fulls, shapes, axes, after, name):
    nw = len(fulls)

    def body(*refs):
        in_refs, send_ref, recv_ref = refs[:nw], refs[nw], refs[nw + 1]
        x, y, c = _my_pos()
        for w in range(nw):
            for j, ch in enumerate(_other_chips(x, y)):
                kj = 2 * ch[0] + ch[1]
                sent = _region(in_refs[w], kj, c, shapes[w], axes[w])
                got = _region(in_refs[w], kj, 1 - c, shapes[w], axes[w])
                cp = _remote(sent, got, send_ref.at[3 * w + j], recv_ref.at[3 * w + j], (x, y, 1 - c))
                cp.wait_send()
                cp.wait_recv()

    return pl.pallas_call(
        body, name=name,
        out_shape=tuple(pltpu.HBM(f.shape, f.dtype) for f in fulls),
        in_specs=[HBM_SPEC] * nw + [SEM_SPEC, SEM_SPEC, pl.BlockSpec(memory_space=pl.ANY)],
        out_specs=tuple([HBM_SPEC] * nw),
        input_output_aliases={w: w for w in range(nw)},
        compiler_params=SPLIT_PARAMS)(*fulls, send, recv, after)


def _rs_sibling_start(grads, shapes, axes, name):
    nw = len(grads)
    npc = max(1 if a == 1 else N_CHIPS for a in axes)

    def body(*refs):
        g_refs, l_refs, send, recv, token = refs[:nw], refs[nw:2 * nw], refs[2 * nw], refs[2 * nw + 1], refs[-1]
        x, y, c = _my_pos()
        for w in range(nw):
            src = _half_pieces(g_refs[w], 1 - c, shapes[w], axes[w])
            dst = _half_pieces(l_refs[w], 1 - c, shapes[w], axes[w])
            for i, (s, d) in enumerate(zip(src, dst)):
                _remote(s, d, send.at[npc * w + i], recv.at[npc * w + i], (x, y, 1 - c)).start()
        token[...] = jnp.zeros_like(token)

    thru = [pltpu.HBM(g.shape, g.dtype) for g in grads]
    return pl.pallas_call(
        body, name=name,
        out_shape=(pltpu.SemaphoreType.DMA((npc * nw,)), pltpu.SemaphoreType.DMA((npc * nw,)), *thru, *thru,
                   jax.ShapeDtypeStruct((8, LANES), F32)),
        in_specs=[HBM_SPEC] * (2 * nw),
        out_specs=(SEM_SPEC, SEM_SPEC, *[HBM_SPEC] * (2 * nw), pl.BlockSpec(memory_space=pltpu.VMEM)),
        input_output_aliases={i: 2 + i for i in range(2 * nw)},
        compiler_params=SPLIT_PARAMS)(*[_in_hbm(g) for g in grads], *[_in_hbm(lax.empty(g.shape, g.dtype)) for g in grads])


def _rs_sibling_wait(send, recv, grads, lands, shapes, axes, after, name):
    nw = len(grads)
    npc = max(1 if a == 1 else N_CHIPS for a in axes)

    def body(*refs):
        g_refs, l_refs, send_ref, recv_ref = refs[:nw], refs[nw:2 * nw], refs[2 * nw], refs[2 * nw + 1]
        x, y, c = _my_pos()
        for w in range(nw):
            sent = _half_pieces(g_refs[w], 1 - c, shapes[w], axes[w])
            mine = _half_pieces(l_refs[w], c, shapes[w], axes[w])
            for i, (s, d) in enumerate(zip(sent, mine)):
                cp = _remote(s, d, send_ref.at[npc * w + i], recv_ref.at[npc * w + i], (x, y, 1 - c))
                cp.wait_send()
                cp.wait_recv()

    thru = tuple(pltpu.HBM(g.shape, g.dtype) for g in grads)
    return pl.pallas_call(
        body, name=name, out_shape=thru + thru,
        in_specs=[HBM_SPEC] * (2 * nw) + [SEM_SPEC, SEM_SPEC, pl.BlockSpec(memory_space=pl.ANY)],
        out_specs=tuple([HBM_SPEC] * (2 * nw)),
        input_output_aliases={i: i for i in range(2 * nw)},
        compiler_params=SPLIT_PARAMS)(*grads, *lands, send, recv, after)


def _rs_chips_start(sums, name):
    nw = len(sums)

    def body(*refs):
        s_refs, l_refs, send, recv, token = refs[:nw], refs[nw:2 * nw], refs[2 * nw], refs[2 * nw + 1], refs[-1]
        x, y, c = _my_pos()
        k0 = 2 * x + y
        for w in range(nw):
            for j, ch in enumerate(_other_chips(x, y)):
                _remote(s_refs[w].at[2 * ch[0] + ch[1]], l_refs[w].at[k0], send.at[3 * w + j], recv.at[3 * w + j],
                        (ch[0], ch[1], c)).start()
        token[...] = jnp.zeros_like(token)

    thru = [pltpu.HBM(s.shape, s.dtype) for s in sums]
    return pl.pallas_call(
        body, name=name,
        out_shape=(pltpu.SemaphoreType.DMA((3 * nw,)), pltpu.SemaphoreType.DMA((3 * nw,)), *thru, *thru,
                   jax.ShapeDtypeStruct((8, LANES), F32)),
        in_specs=[HBM_SPEC] * (2 * nw),
        out_specs=(SEM_SPEC, SEM_SPEC, *[HBM_SPEC] * (2 * nw), pl.BlockSpec(memory_space=pltpu.VMEM)),
        input_output_aliases={i: 2 + i for i in range(2 * nw)},
        compiler_params=SPLIT_PARAMS)(*[_in_hbm(s) for s in sums], *[_in_hbm(lax.empty(s.shape, s.dtype)) for s in sums])


def _rs_chips_wait(send, recv, sums, lands, after, name):
    nw = len(sums)

    def body(*refs):
        s_refs, l_refs, send_ref, recv_ref = refs[:nw], refs[nw:2 * nw], refs[2 * nw], refs[2 * nw + 1]
        x, y, c = _my_pos()
        for w in range(nw):
            for j, ch in enumerate(_other_chips(x, y)):
                kj = 2 * ch[0] + ch[1]
                cp = _remote(s_refs[w].at[kj], l_refs[w].at[kj], send_ref.at[3 * w + j], recv_ref.at[3 * w + j],
                             (ch[0], ch[1], c))
                cp.wait_send()
                cp.wait_recv()

    thru = tuple(pltpu.HBM(s.shape, s.dtype) for s in sums)
    return pl.pallas_call(
        body, name=name, out_shape=thru + thru,
        in_specs=[HBM_SPEC] * (2 * nw) + [SEM_SPEC, SEM_SPEC, pl.BlockSpec(memory_space=pl.ANY)],
        out_specs=tuple([HBM_SPEC] * (2 * nw)),
        input_output_aliases={i: i for i in range(2 * nw)},
        compiler_params=SPLIT_PARAMS)(*sums, *lands, send, recv, after)


LOSS_LANE = 64


def kernel(x, c, ctx, c_ctx, w_mod, b_mod, norm_mix, norm_ffn, w_in, ret_decay, attn_sink, w_out, w_gate, w_up, w_down, norm_final, loss_target, m_c_ctx, m_w_mod, m_b_mod, m_norm_mix, m_norm_ffn, m_w_in, m_ret_decay, m_attn_sink, m_w_out, m_w_gate, m_w_up, m_w_down, m_norm_final, v_c_ctx, v_w_mod, v_b_mod, v_norm_mix, v_norm_ffn, v_w_in, v_ret_decay, v_attn_sink, v_w_out, v_w_gate, v_w_up, v_w_down, v_norm_final):
    D = x.shape[-1]
    n3 = w_mod.shape[-1]
    xi, yi, ci = _my_pos()
    b = 4 * xi + 2 * yi + ci
    k0 = 2 * xi + yi
    cvec = jnp.reshape(ci, (1,)).astype(jnp.int32)
    kc = jnp.stack([k0, ci]).astype(jnp.int32)

    dense = [("w_in", w_in[0], 1), ("w_out", w_out[0], 0), ("w_gate", w_gate[0], 1), ("w_up", w_up[0], 1),
             ("w_down", w_down[0], 0)]
    axes = [a for _, _, a in dense]
    shapes = [w.shape for _, w, _ in dense]
    c_all = _allgather8(c, "gather_c").reshape(N_DEV, D)
    c_ctx2 = c_ctx.reshape(1, D)
    a16 = _pack_rows([(c_all, 0, N_DEV, 0, 0), (c_ctx2, 0, 1, N_DEV, 0)], 16, D, "pack_cond")
    b_cols = lax.dynamic_slice_in_dim(b_mod, k0 * n3, n3, axis=1)
    mod16 = _mod_rows(a16, w_mod[0], b_cols, "mod_rows")
    mod_all = _allgather8(mod16, "gather_mod")

    own_in = _cast_into_full(dense[0][1], kc, axes[0], "cast_w_in")
    agi = _ag_chips_start([own_in], shapes[:1], axes[:1], mod_all, "ag_in_start")
    casts = [_cast_into_full(w, kc, a, "cast_" + n, after=agi[-1]) for n, w, a in dense[1:]]
    own16 = [cst[0] for cst in casts]
    tables = _rope_tables(x.shape[1])
    behind = tables["Cr"][0:8] + tables["Sr"][0:8] + tables["Ca"][0:8] + tables["Sa"][0:8] + sum(cst[1] for cst in casts)
    (f_in,) = _ag_forward(list(_ag_chips_wait(agi[0], agi[1], [agi[2]], shapes[:1], axes[:1], behind, "ag_in_wait")),
                          shapes[:1], axes[:1], "ag_in_forward")
    ag = _ag_chips_start(own16, shapes[1:], axes[1:], f_in, "ag_rest_start")
    ag_send, ag_recv, ag_thru, ag_tok = ag[0], ag[1], list(ag[2:-1]), ag[-1][0:1, 0:1]

    fwd = []

    def rest_weights(after):
        if not fwd:
            landed_w = _ag_chips_wait(ag_send, ag_recv, ag_thru, shapes[1:], axes[1:], after, "ag_rest_wait")
            fwd.append(_ag_forward_start(list(landed_w), shapes[1:], axes[1:], "ag_rest_forward_start"))
            return fwd[0][-1][0:1, 0:1]
        st = fwd[0]
        return _ag_forward_wait(st[0], st[1], list(st[2:-1]), shapes[1:], axes[1:], after, "ag_rest_forward_wait")
    mine = jnp.stack([lax.dynamic_index_in_dim(mod_all, 2 * k + ci, 0, keepdims=False) for k in range(N_CHIPS)])
    mod = lax.dynamic_index_in_dim(mine, b, 1, keepdims=False).reshape(6, D)
    modc = mine[:, N_DEV].reshape(6, D)

    lg = -jnp.exp(ret_decay[0])

    index = {n: i for i, (n, _, _) in enumerate(dense)}
    pending, done = [], {}

    sib = []

    def finish_sibling(after):
        names, shp, axs, st = sib.pop()
        nw = len(names)
        res = _rs_sibling_wait(st[0], st[1], list(st[2:2 + nw]), list(st[2 + nw:2 + 2 * nw]), shp, axs, after,
                               "rs_sibling_wait_" + names[0])
        sums = [_add_halves(res[i], res[nw + i], cvec, s, a, "add_halves_" + n)
                for i, (s, a, n) in enumerate(zip(shp, axs, names))]
        ch = _rs_chips_start(sums, "rs_chips_start_" + names[0])
        pending.append((names, ch[0], ch[1], list(ch[2:2 + nw]), list(ch[2 + nw:2 + 2 * nw])))
        return ch[-1][0:1, 0:1]

    def on_grads(names, gs):
        ids = [index[n] for n in names]
        shp, axs = [shapes[i] for i in ids], [axes[i] for i in ids]
        st = _rs_sibling_start(gs, shp, axs, "rs_sibling_start_" + names[0])
        sib.append((names, shp, axs, st))
        return st[-1][0:1, 0:1]

    out = _local_step(x[0], ctx[0], loss_target[0], mod, modc, norm_mix + ag_tok, norm_ffn, norm_final.reshape(1, D), lg,
                      attn_sink, tables, f_in, rest_weights, on_grads, finish_sibling)

    def finish(group, after):
        names, send, recv, sums, lands = group
        res = _rs_chips_wait(send, recv, sums, lands, after, "rs_chips_wait_" + names[0])
        return [_sum_chips(res[i], res[len(names) + i], kc, "sum_chips_" + n) for i, n in enumerate(names)]

    tok_in = finish_sibling(out["grad_x"])
    assert pending[-1][0] == ["w_in"]
    rest_names = [n for g in pending[:-1] for n in g[0]]
    after_in = out["small"][0:8, 0:LANES] + tok_in
    rest_halves = [h for g in pending[:-1] for h in finish(g, after_in)]
    g_rest = dict(zip(rest_names, _exchange_halves(rest_halves, "exchange_halves_rest")))

    nh = 2 * RET_HEADS
    small_all = _allgather8(out["small"], "gather_small")
    tot = _sum_devices(small_all, "sum_small")
    g_b_mod = (tot[0:6] + tot[6:12]).reshape(1, 6 * D)
    dmodc_tot = tot[6:12].reshape(1, 6 * D)
    dmod_rows = small_all[:, 0:6].reshape(N_DEV, 6 * D)
    d16 = _pack_rows([(dmod_rows, 0, N_DEV, 0, 0), (dmodc_tot, 0, 1, N_DEV, 0)], 16, 6 * D, "pack_dmod")
    d16 = lax.dynamic_slice_in_dim(d16, k0 * n3, n3, axis=1)
    g_w_mod, dl_w_mod, m2_w_mod, v2_w_mod, part = _w_mod_update(a16, d16, w_mod[0], m_w_mod[0], v_w_mod[0])
    part_all = _allgather8(part[N_DEV:N_DEV + 1], "gather_c_ctx")
    g_c_ctx = _c_ctx_grad(part_all, c_ctx2)
    loss = tot[15, LOSS_LANE]

    def pack(cc, bm, nm, nf, nfin, rd, sk, name):
        rd2 = rd.reshape(2, RET_HEADS)
        return _pack_rows([(bm.reshape(6, D), 0, 6, 0, 0), (cc.reshape(1, D), 0, 1, 6, 0), (nm.reshape(1, D), 0, 1, 7, 0),
                           (nf.reshape(1, D), 0, 1, 8, 0), (nfin.reshape(1, D), 0, 1, 9, 0),
                           (rd2, 0, 1, 10, 0), (rd2, 1, 1, 10, RET_HEADS), (sk.reshape(1, ATT_HEADS), 0, 1, 10, nh)],
                          16, D, name)

    w_s = pack(c_ctx, b_mod, norm_mix, norm_ffn, norm_final, ret_decay, attn_sink, "pack_w")
    g_s = _pack_rows([(g_b_mod.reshape(6, D), 0, 6, 0, 0), (g_c_ctx, 0, 1, 6, 0), (tot, 12, 3, 7, 0),
                      (tot[15:16, 0:nh + ATT_HEADS], 0, 1, 10, 0)], 16, D, "pack_g")
    m_s = pack(m_c_ctx, m_b_mod, m_norm_mix, m_norm_ffn, m_norm_final, m_ret_decay, m_attn_sink, "pack_m")
    v_s = pack(v_c_ctx, v_b_mod, v_norm_mix, v_norm_ffn, v_norm_final, v_ret_decay, v_attn_sink, "pack_v")
    small_upd = _adam(w_s, g_s, m_s, v_s, "adam_small")

    def unpack(t):
        return dict(b_mod=t[0:6].reshape(1, 6 * D), c_ctx=t[6], norm_mix=t[7:8], norm_ffn=t[8:9], norm_final=t[9],
                    ret_decay=t[10, :nh].reshape(1, 2, RET_HEADS), attn_sink=t[10, nh:nh + ATT_HEADS].reshape(1, ATT_HEADS))

    dense_w = dict(w_in=(w_in, m_w_in, v_w_in), w_out=(w_out, m_w_out, v_w_out), w_gate=(w_gate, m_w_gate, v_w_gate),
                   w_up=(w_up, m_w_up, v_w_up), w_down=(w_down, m_w_down, v_w_down))
    grads = dict(unpack(g_s), w_mod=g_w_mod[None])
    upd = [dict(unpack(t)) for t in small_upd]
    upd[0]["w_mod"], upd[1]["w_mod"], upd[2]["w_mod"] = dl_w_mod[None], m2_w_mod[None], v2_w_mod[None]
    def update(n, g):
        w_, m_, v_ = dense_w[n]
        res = _adam(w_[0], g, m_[0], v_[0], "adam_" + n)
        grads[n] = g[None]
        for u, r_ in zip(upd, res):
            u[n] = r_[None]
        return res[0]

    dep = small_upd[0][0:1, 0:1] + dl_w_mod[0:1, 0:1]
    for n in rest_names:
        dep = dep + update(n, g_rest[n])[0:1, 0:1]
    (g_in,) = _exchange_halves(finish(pending[-1], dep), "exchange_halves_in")
    update("w_in", g_in)

    order = ['c_ctx', 'w_mod', 'b_mod', 'norm_mix', 'norm_ffn', 'w_in', 'ret_decay', 'attn_sink', 'w_out', 'w_gate',
             'w_up', 'w_down', 'norm_final']
    outs = [loss, out["grad_x"][None]] + [grads[n] for n in order]
    for u in upd:
        outs += [u[n] for n in order]
    return tuple(outs)
```

```python
import numpy as np
import jax
import jax.numpy as jnp
from jax import lax
from jax.experimental import pallas as pl
from jax.experimental.pallas import tpu as pltpu

F32 = jnp.float32
BF16 = jnp.bfloat16

RET_HEADS = 8
RET_DK = 64
RET_DV = 128
CHUNK = 128
ATT_HEADS = 16
ATT_KV_HEADS = 4
ATT_DH = 64
GRID_W = 64
ROPE_BASE = 10000.0
NORM_EPS = 1e-6
ADAM_LR = 0.001
ADAM_B1 = 0.9
ADAM_B2 = 0.999
ADAM_EPS = 1e-08
ADAM_WD = 0.01
ADAM_STEP = 10
NEG = -1e30
LANES = 128
VMEM_LIMIT = 56 * 1024 * 1024
ROWS_PER_LATCH = 1024
MESH = pl.DeviceIdType.MESH
N_CHIPS = 4
N_DEV = 8


def _nn(a, b):
    return jnp.dot(a, b, preferred_element_type=F32)


def _nt(a, b):
    return lax.dot_general(a, b, (((1,), (1,)), ((), ())), preferred_element_type=F32)


def _tn(a, b):
    return lax.dot_general(a, b, (((0,), (0,)), ((), ())), preferred_element_type=F32)


def _tile(n, pref, unit=LANES):
    t = min(n, pref)
    t -= t % unit
    while t > unit and n % t:
        t -= unit
    if t <= 0 or n % t:
        return n
    return t


def _params(ndim, vmem=True):
    return pltpu.CompilerParams(dimension_semantics=("arbitrary",) * ndim,
                                vmem_limit_bytes=VMEM_LIMIT if vmem else None)


def _sigmoid(x):
    return 0.5 * jnp.tanh(0.5 * x) + 0.5


def _fsum(x):
    return jnp.sum(jnp.sum(x, axis=0, keepdims=True), axis=1, keepdims=True)


def _rope_tables(L):
    lane = np.arange(LANES)
    d = lane % 64
    inv_r = jnp.asarray(ROPE_BASE, F32) ** (-jnp.arange(32, dtype=F32) / 32)
    t = jnp.arange(L)
    ang_r = t.astype(F32)[:, None] * jnp.tile(inv_r, LANES // 32)[None, :]
    Rr = np.zeros((LANES, LANES), np.float32)
    for l in range(LANES):
        if d[l] < 32:
            Rr[l + 32, l] = -1.0
        else:
            Rr[l - 32, l] = 1.0
    inv_a = jnp.asarray(ROPE_BASE, F32) ** (-jnp.arange(16, dtype=F32) / 16)
    rows = (t // GRID_W).astype(F32)
    cols = (t % GRID_W).astype(F32)
    dd = d % 32
    pos = jnp.where(jnp.asarray(d < 32)[None, :], rows[:, None], cols[:, None])
    ang_a = pos * jnp.tile(inv_a, LANES // 16)[None, :]
    Ra = np.zeros((LANES, LANES), np.float32)
    for l in range(LANES):
        if dd[l] < 16:
            Ra[l + 16, l] = -1.0
        else:
            Ra[l - 16, l] = 1.0
    D0 = np.zeros((LANES, LANES), np.float32)
    D1 = np.zeros((LANES, LANES), np.float32)
    for l in range(LANES):
        D0[l % 64, l] = 1.0
        D1[64 + l % 64, l] = 1.0
    return dict(
        Cr=jnp.cos(ang_r), Sr=jnp.sin(ang_r), Rr=jnp.asarray(Rr, BF16), RrT=jnp.asarray(Rr.T, BF16),
        Ca=jnp.cos(ang_a), Sa=jnp.sin(ang_a), Ra=jnp.asarray(Ra, BF16), RaT=jnp.asarray(Ra.T, BF16),
        D0=jnp.asarray(D0, BF16), D1=jnp.asarray(D1, BF16),
        D0T=jnp.asarray(D0.T, BF16), D1T=jnp.asarray(D1.T, BF16))


def _norm_mod(xf, g, sh, sc):
    r = lax.rsqrt(jnp.mean(xf * xf, axis=-1, keepdims=True) + NORM_EPS)
    return (xf * r * g) * (1.0 + sc) + sh


def _norm_mod_matmul(x, g, sh, sc, w, name):
    M, D = x.shape
    N = w.shape[1]
    tm, tn = _tile(M, ROWS_PER_LATCH, 8), _tile(N, 768)

    def body(x_ref, g_ref, sh_ref, sc_ref, w_ref, p_ref, h_ref, hs):
        @pl.when(pl.program_id(1) == 0)
        def _():
            hb = _norm_mod(x_ref[...], g_ref[...], sh_ref[...], sc_ref[...]).astype(BF16)
            hs[...] = hb
            h_ref[...] = hb
        p_ref[...] = _nn(hs[...], w_ref[...]).astype(BF16)

    vec = pl.BlockSpec((1, D), lambda i, j: (0, 0))
    return pl.pallas_call(
        body, name=name, grid=(M // tm, N // tn),
        in_specs=[pl.BlockSpec((tm, D), lambda i, j: (i, 0)), vec, vec, vec,
                  pl.BlockSpec((D, tn), lambda i, j: (0, j))],
        out_specs=[pl.BlockSpec((tm, tn), lambda i, j: (i, j)), pl.BlockSpec((tm, D), lambda i, j: (i, 0))],
        out_shape=[jax.ShapeDtypeStruct((M, N), BF16), jax.ShapeDtypeStruct((M, D), BF16)],
        scratch_shapes=[pltpu.VMEM((tm, D), BF16)],
        compiler_params=_params(2))(x, g, sh, sc, w)


def _proj_residual(a, w, xres, gt, name):
    M, K = a.shape
    N = w.shape[1]
    tm, tn = _tile(M, ROWS_PER_LATCH, 8), _tile(N, 1024 if K <= 2048 else 512)

    def body(a_ref, w_ref, x_ref, gt_ref, xo_ref, o_ref):
        o = _nn(a_ref[...], w_ref[...])
        o_ref[...] = o.astype(BF16)
        xo_ref[...] = x_ref[...] + gt_ref[...] * o

    return pl.pallas_call(
        body, name=name, grid=(M // tm, N // tn),
        in_specs=[pl.BlockSpec((tm, K), lambda i, j: (i, 0)), pl.BlockSpec((K, tn), lambda i, j: (0, j)),
                  pl.BlockSpec((tm, tn), lambda i, j: (i, j)), pl.BlockSpec((1, tn), lambda i, j: (0, j))],
        out_specs=[pl.BlockSpec((tm, tn), lambda i, j: (i, j)), pl.BlockSpec((tm, tn), lambda i, j: (i, j))],
        out_shape=[jax.ShapeDtypeStruct((M, N), F32), jax.ShapeDtypeStruct((M, N), BF16)],
        compiler_params=_params(2))(a, w, xres, gt)


def _ffn_in(x1, g, sh, sc, wg, wu):
    M, D = x1.shape
    N = wg.shape[1]
    tm, tn = _tile(M, ROWS_PER_LATCH, 8), _tile(N, 512)

    def body(x_ref, g_ref, sh_ref, sc_ref, wg_ref, wu_ref, G_ref, U_ref, A_ref, h_ref, hs):
        @pl.when(pl.program_id(1) == 0)
        def _():
            hb = _norm_mod(x_ref[...], g_ref[...], sh_ref[...], sc_ref[...]).astype(BF16)
            hs[...] = hb
            h_ref[...] = hb
        G = _nn(hs[...], wg_ref[...])
        U = _nn(hs[...], wu_ref[...])
        G_ref[...] = G.astype(BF16)
        U_ref[...] = U.astype(BF16)
        A_ref[...] = (G * _sigmoid(G) * U).astype(BF16)

    vec = pl.BlockSpec((1, D), lambda i, j: (0, 0))
    wspec = pl.BlockSpec((D, tn), lambda i, j: (0, j))
    ospec = pl.BlockSpec((tm, tn), lambda i, j: (i, j))
    big = jax.ShapeDtypeStruct((M, N), BF16)
    return pl.pallas_call(
        body, name="ffn_in", grid=(M // tm, N // tn),
        in_specs=[pl.BlockSpec((tm, D), lambda i, j: (i, 0)), vec, vec, vec, wspec, wspec],
        out_specs=[ospec, ospec, ospec, pl.BlockSpec((tm, D), lambda i, j: (i, 0))],
        out_shape=[big, big, big, jax.ShapeDtypeStruct((M, D), BF16)],
        scratch_shapes=[pltpu.VMEM((tm, D), BF16)],
        compiler_params=_params(2))(x1, g, sh, sc, wg, wu)


def _final(x2, gn, tgt, gt, saved):
    M, D = x2.shape
    tm = _tile(M, 256, 8)

    def body(x_ref, g_ref, t_ref, gt_ref, sv_ref, dx_ref, loss_ref, dg_ref, dz_ref, dgt_ref):
        @pl.when(pl.program_id(0) == 0)
        def _():
            loss_ref[...] = jnp.zeros_like(loss_ref)
            dg_ref[...] = jnp.zeros_like(dg_ref)
            dgt_ref[...] = jnp.zeros_like(dgt_ref)
        x = x_ref[...]
        g = g_ref[...]
        r = lax.rsqrt(jnp.mean(x * x, axis=-1, keepdims=True) + NORM_EPS)
        xh = x * r
        e = xh * g - t_ref[...]
        loss_ref[...] += (0.5 / D) * _fsum(e * e)
        dy = e * (1.0 / D)
        dg_ref[...] += jnp.sum(dy * xh, axis=0, keepdims=True)
        dxh = dy * g
        d = r * (dxh - xh * jnp.mean(dxh * xh, axis=-1, keepdims=True))
        dx_ref[...] = d
        dz_ref[...] = (d * gt_ref[...]).astype(BF16)
        dgt_ref[...] += jnp.sum(d * sv_ref[...].astype(F32), axis=0, keepdims=True)

    row = pl.BlockSpec((tm, D), lambda i: (i, 0))
    vec = pl.BlockSpec((1, D), lambda i: (0, 0))
    return pl.pallas_call(
        body, name="final_loss", grid=(M // tm,),
        in_specs=[row, vec, row, vec, row],
        out_specs=[row, pl.BlockSpec((1, LANES), lambda i: (0, 0)), vec, row, vec],
        out_shape=[jax.ShapeDtypeStruct((M, D), F32), jax.ShapeDtypeStruct((1, LANES), F32),
                   jax.ShapeDtypeStruct((1, D), F32), jax.ShapeDtypeStruct((M, D), BF16),
                   jax.ShapeDtypeStruct((1, D), F32)],
        compiler_params=_params(1))(x2, gn, tgt, gt, saved)


def _col_group(blk0, nblk):
    return int(np.gcd(blk0, nblk)) if blk0 else nblk


def _rope_cols(src, blk0, nblk, Ct, St, R, scale, rope, name):
    M = src.shape[0]
    tm = _tile(M, 512, 8)
    wb = _col_group(blk0, nblk)

    def body(x_ref, c_ref, s_ref, r_ref, o_ref):
        for j in range(wb):
            cols = slice(j * LANES, (j + 1) * LANES)
            x = x_ref[:, cols]
            xf = x.astype(F32)
            if rope:
                xf = xf * c_ref[...] + _nn(x.astype(BF16), r_ref[...]) * s_ref[...]
            o_ref[:, cols] = (xf * scale).astype(BF16)

    tab = pl.BlockSpec((tm, LANES), lambda i, j: (i, 0))
    return pl.pallas_call(
        body, name=name, grid=(M // tm, nblk // wb),
        in_specs=[pl.BlockSpec((tm, wb * LANES), lambda i, j: (i, blk0 // wb + j)), tab, tab,
                  pl.BlockSpec((LANES, LANES), lambda i, j: (0, 0))],
        out_specs=pl.BlockSpec((tm, wb * LANES), lambda i, j: (i, j)),
        out_shape=jax.ShapeDtypeStruct((M, nblk * LANES), BF16),
        compiler_params=_params(2, False))(src, Ct, St, R)


def _dup_heads(src, blk0, npair, Ct, St, R, D0, D1, rope, name):
    M = src.shape[0]
    tm = _tile(M, 512, 8)

    def body(x_ref, c_ref, s_ref, r_ref, d0_ref, d1_ref, o_ref):
        x = x_ref[...]
        if rope:
            x = (x.astype(F32) * c_ref[...] + _nn(x, r_ref[...]) * s_ref[...]).astype(BF16)
        o_ref[0] = _nn(x, d0_ref[...]).astype(BF16)
        o_ref[1] = _nn(x, d1_ref[...]).astype(BF16)

    tab = pl.BlockSpec((tm, LANES), lambda i, p: (i, 0))
    mat = pl.BlockSpec((LANES, LANES), lambda i, p: (0, 0))
    return pl.pallas_call(
        body, name=name, grid=(M // tm, npair),
        in_specs=[pl.BlockSpec((tm, LANES), lambda i, p: (i, blk0 + p)), tab, tab, mat, mat, mat],
        out_specs=pl.BlockSpec((2, tm, LANES), lambda i, p: (p, i, 0)),
        out_shape=jax.ShapeDtypeStruct((2 * npair, M, LANES), BF16),
        compiler_params=_params(2, False))(src, Ct, St, R, D0, D1)


def _unrope_cols(dsrc, dst, blk0, nblk, Ct, St, RT, scale, rope, name):
    M = dsrc.shape[0]
    tm = _tile(M, 512, 8)
    wb = _col_group(blk0, nblk)

    def body(x_ref, c_ref, s_ref, r_ref, dst_ref, o_ref):
        del dst_ref
        for j in range(wb):
            cols = slice(j * LANES, (j + 1) * LANES)
            xf = x_ref[:, cols].astype(F32)
            if rope:
                xf = xf * c_ref[...] + _nn((xf * s_ref[...]).astype(BF16), r_ref[...])
            o_ref[:, cols] = (xf * scale).astype(BF16)

    tab = pl.BlockSpec((tm, LANES), lambda i, j: (i, 0))
    return pl.pallas_call(
        body, name=name, grid=(M // tm, nblk // wb),
        in_specs=[pl.BlockSpec((tm, wb * LANES), lambda i, j: (i, j)), tab, tab,
                  pl.BlockSpec((LANES, LANES), lambda i, j: (0, 0)),
                  pl.BlockSpec(memory_space=pl.ANY)],
        out_specs=pl.BlockSpec((tm, wb * LANES), lambda i, j: (i, blk0 // wb + j)),
        out_shape=jax.ShapeDtypeStruct(dst.shape, dst.dtype),
        input_output_aliases={4: 0},
        compiler_params=_params(2, False))(dsrc, Ct, St, RT, dst)


def _fold_heads(parts, dst, blk0, npair, Ct, St, RT, D0T, D1T, rope, name):
    M = parts[0][0].shape[1]
    nb = M // CHUNK
    R = _tile(M, 1024, CHUNK)
    rb = R // CHUNK
    nrefs = sum(1 if s == 0 else 2 for _, s in parts)

    def body(*refs):
        part_refs = list(refs[:nrefs])
        c_ref, s_ref, r_ref, d0_ref, d1_ref, dst_ref, o_ref = refs[nrefs:]
        del dst_ref
        i = pl.program_id(0)
        tot = [jnp.zeros((R, LANES), F32), jnp.zeros((R, LANES), F32)]
        for _, shift in parts:
            main = part_refs.pop(0)
            if shift == 0:
                for e in range(2):
                    tot[e] = tot[e] + main[e].astype(F32)
                continue
            edge = part_refs.pop(0)
            ok = (i + 1) * rb <= nb - 1 if shift > 0 else i > 0
            for e in range(2):
                ed = jnp.where(ok, edge[e].astype(F32), 0.0)
                if rb == 1:
                    tot[e] = tot[e] + ed
                elif shift > 0:
                    tot[e] = tot[e] + jnp.concatenate([main[e, CHUNK:, :].astype(F32), ed], axis=0)
                else:
                    tot[e] = tot[e] + jnp.concatenate([ed, main[e, :R - CHUNK, :].astype(F32)], axis=0)
        f = _nn(tot[0].astype(BF16), d0_ref[...]) + _nn(tot[1].astype(BF16), d1_ref[...])
        if rope:
            f = f * c_ref[...] + _nn((f * s_ref[...]).astype(BF16), r_ref[...])
        o_ref[...] = f.astype(BF16)

    in_specs, args = [], []
    for a, shift in parts:
        assert shift in (-1, 0, 1)
        in_specs.append(pl.BlockSpec((2, R, LANES), lambda i, p: (p, i, 0)))
        args.append(a)
        if shift > 0:
            in_specs.append(pl.BlockSpec((2, CHUNK, LANES), lambda i, p: (p, jnp.minimum((i + 1) * rb, nb - 1), 0)))
            args.append(a)
        elif shift < 0:
            in_specs.append(pl.BlockSpec((2, CHUNK, LANES), lambda i, p: (p, jnp.maximum(i * rb - 1, 0), 0)))
            args.append(a)
    tab = pl.BlockSpec((R, LANES), lambda i, p: (i, 0))
    mat = pl.BlockSpec((LANES, LANES), lambda i, p: (0, 0))
    return pl.pallas_call(
        body, name=name, grid=(M // R, npair),
        in_specs=in_specs + [tab, tab, mat, mat, mat, pl.BlockSpec(memory_space=pl.ANY)],
        out_specs=pl.BlockSpec((R, LANES), lambda i, p: (i, blk0 + p)),
        out_shape=jax.ShapeDtypeStruct(dst.shape, dst.dtype),
        input_output_aliases={nrefs + 5: 0},
        compiler_params=_params(2, False))(*args, Ct, St, RT, D0T, D1T, dst)


def _head_masks():
    lane = lax.broadcasted_iota(jnp.int32, (1, LANES), 1)
    return [lane < 64, lane >= 64]


def _decay_vecs(lam, mu):
    i = lax.broadcasted_iota(jnp.int32, (CHUNK, 1), 0).astype(F32)
    return dict(qf=jnp.exp(lam * (i + 1.0)), kf=jnp.exp(lam * (CHUNK - 1.0 - i)),
                qb=jnp.exp(mu * (CHUNK - i)), kb=jnp.exp(mu * i),
                gf=jnp.exp(lam * float(CHUNK)), gb=jnp.exp(mu * float(CHUNK)), i=i)


def _decay_mask(lam, mu):
    r = lax.broadcasted_iota(jnp.int32, (CHUNK, CHUNK), 0)
    c = lax.broadcasted_iota(jnp.int32, (CHUNK, CHUNK), 1)
    rel = (r - c).astype(F32)
    low = rel >= 0.0
    mf = jnp.exp(lam * jnp.maximum(rel, 0.0))
    mb = jnp.exp(mu * jnp.maximum(-rel, 0.0))
    return jnp.where(low, mf, mb), rel, low


def _lam_of(lg_ref, row, idx):
    return jnp.full((1, 1), lg_ref[row, idx], F32)


def _group_index(pair_blk, npairs):
    assert pair_blk % npairs == 0
    return pair_blk // npairs


def _ret_states_fwd(Kr, P, Krc, Pc, lg, rv_blk, npairs):
    L = Kr.shape[0]
    Lc = Krc.shape[0]
    N, ncc = L // CHUNK, Lc // CHUNK
    rv_grp = _group_index(rv_blk, npairs)

    heads = [(p, h) for p in range(npairs) for h in range(2)]
    kcols = lambda p: slice(p * LANES, (p + 1) * LANES)
    vcols = lambda p, h: slice((2 * p + h) * LANES, (2 * p + h + 1) * LANES)

    def body(lg_ref, k_ref, v_ref, kc_ref, vc_ref, sf_ref, S):
        n = pl.program_id(0)
        masks = _head_masks()

        @pl.when(n == 0)
        def _():
            for p, h in heads:
                lam = _lam_of(lg_ref, 0, 2 * p + h)
                dv = _decay_vecs(lam, lam)
                s = jnp.zeros((LANES, LANES), F32)
                for cc in range(ncc):
                    rows = slice(cc * CHUNK, (cc + 1) * CHUNK)
                    kw = jnp.where(masks[h], kc_ref[rows, kcols(p)].astype(F32) * dv["kf"], 0.0).astype(BF16)
                    s = dv["gf"] * s + _tn(kw, vc_ref[rows, vcols(p, h)])
                S[p, h] = s

        for p, h in heads:
            lam = _lam_of(lg_ref, 0, 2 * p + h)
            dv = _decay_vecs(lam, lam)
            s = S[p, h]
            sf_ref[p, 0, h] = s.astype(BF16)
            kw = jnp.where(masks[h], k_ref[:, kcols(p)].astype(F32) * dv["kf"], 0.0).astype(BF16)
            S[p, h] = dv["gf"] * s + _tn(kw, v_ref[:, vcols(p, h)])

    wq, wv = npairs * LANES, npairs * 2 * LANES
    return pl.pallas_call(
        body, name="ret_states_fwd", grid=(N,),
        in_specs=[pl.BlockSpec(memory_space=pltpu.SMEM),
                  pl.BlockSpec((CHUNK, wq), lambda n: (n, 0)),
                  pl.BlockSpec((CHUNK, wv), lambda n: (n, rv_grp)),
                  pl.BlockSpec((Lc, wq), lambda n: (0, 0)),
                  pl.BlockSpec((Lc, wv), lambda n: (0, rv_grp))],
        out_specs=pl.BlockSpec((npairs, 1, 2, LANES, LANES), lambda n: (0, n, 0, 0, 0)),
        out_shape=jax.ShapeDtypeStruct((npairs, N, 2, LANES, LANES), BF16),
        scratch_shapes=[pltpu.VMEM((npairs, 2, LANES, LANES), F32)],
        compiler_params=_params(1, False))(lg, Kr, P, Krc, Pc)


def _ret_chunk_fwd(q, k, v, sf, sb, hm, lam, mu, Mk):
    dv = _decay_vecs(lam, mu)
    qm = jnp.where(hm, q, jnp.zeros_like(q))
    qmf = qm.astype(F32)
    A = _nt(qm, k)
    Am = A * Mk
    Amb = Am.astype(BF16)
    Qf = (qmf * dv["qf"]).astype(BF16)
    Qb = (qmf * dv["qb"]).astype(BF16)
    O = _nn(Amb, v) + _nn(Qf, sf) + _nn(Qb, sb)
    return dict(dv=dv, Mk=Mk, qm=qm, Am=Am, Amb=Amb, Qf=Qf, Qb=Qb, O=O)


def _ret_out_fwd(Qr, Kr, P, Krc, Pc, SF, lg, rv_blk, rg_blk, npairs, d_mix):
    L = Qr.shape[0]
    Lc = Krc.shape[0]
    N, ncc = L // CHUNK, Lc // CHUNK

    rv_grp, rg_grp = _group_index(rv_blk, npairs), _group_index(rg_blk, npairs)
    heads = [(p, h) for p in range(npairs) for h in range(2)]
    kcols = lambda p: slice(p * LANES, (p + 1) * LANES)
    vcols = lambda p, h: slice((2 * p + h) * LANES, (2 * p + h + 1) * LANES)

    def body(lg_ref, q_ref, k_ref, v_ref, g_ref, sf_ref, kc_ref, vc_ref, y_ref, sb_ref, S, Mks):
        n = pl.program_id(0)
        masks = _head_masks()

        @pl.when(n == 0)
        def _():
            for p, h in heads:
                mu = _lam_of(lg_ref, 1, 2 * p + h)
                Mks[p, h] = _decay_mask(_lam_of(lg_ref, 0, 2 * p + h), mu)[0]
                dvb = _decay_vecs(mu, mu)
                s = jnp.zeros((LANES, LANES), F32)
                for cc in reversed(range(ncc)):
                    rows = slice(cc * CHUNK, (cc + 1) * CHUNK)
                    kw = jnp.where(masks[h], kc_ref[rows, kcols(p)].astype(F32) * dvb["kb"], 0.0).astype(BF16)
                    s = dvb["gb"] * s + _tn(kw, vc_ref[rows, vcols(p, h)])
                S[p, h] = s

        for p, h in heads:
            lam = _lam_of(lg_ref, 0, 2 * p + h)
            mu = _lam_of(lg_ref, 1, 2 * p + h)
            hm = masks[h]
            dvb = _decay_vecs(lam, mu)
            s = S[p, h]
            sbb = s.astype(BF16)
            sb_ref[p, 0, h] = sbb
            k = k_ref[:, kcols(p)]
            v = v_ref[:, vcols(p, h)]
            f = _ret_chunk_fwd(q_ref[:, kcols(p)], k, v, sf_ref[p, 0, h], sbb, hm, lam, mu, Mks[p, h])
            O = f["O"]
            r = lax.rsqrt(jnp.mean(O * O, axis=-1, keepdims=True) + NORM_EPS)
            g = g_ref[:, vcols(p, h)].astype(F32)
            y_ref[:, vcols(p, h)] = (O * r * (g * _sigmoid(g))).astype(BF16)
            kw = jnp.where(hm, k.astype(F32) * dvb["kb"], 0.0).astype(BF16)
            S[p, h] = dvb["gb"] * s + _tn(kw, v)

    rev = lambda n: N - 1 - n
    wq, wv = npairs * LANES, npairs * 2 * LANES
    st = pl.BlockSpec((npairs, 1, 2, LANES, LANES), lambda n: (0, rev(n), 0, 0, 0))
    return pl.pallas_call(
        body, name="ret_out_fwd", grid=(N,),
        in_specs=[pl.BlockSpec(memory_space=pltpu.SMEM),
                  pl.BlockSpec((CHUNK, wq), lambda n: (rev(n), 0)),
                  pl.BlockSpec((CHUNK, wq), lambda n: (rev(n), 0)),
                  pl.BlockSpec((CHUNK, wv), lambda n: (rev(n), rv_grp)),
                  pl.BlockSpec((CHUNK, wv), lambda n: (rev(n), rg_grp)),
                  st,
                  pl.BlockSpec((Lc, wq), lambda n: (0, 0)),
                  pl.BlockSpec((Lc, wv), lambda n: (0, rv_grp))],
        out_specs=[pl.BlockSpec((CHUNK, wv), lambda n: (rev(n), 0)), st],
        out_shape=[jax.ShapeDtypeStruct((L, d_mix), BF16),
                   jax.ShapeDtypeStruct((npairs, N, 2, LANES, LANES), BF16)],
        scratch_shapes=[pltpu.VMEM((npairs, 2, LANES, LANES), F32), pltpu.VMEM((npairs, 2, CHUNK, CHUNK), F32)],
        compiler_params=_params(1))(lg, Qr, Kr, P, P, SF, Krc, Pc)


ACC_ROWS = 8


def _ret_bwd1(Qr, Kr, P, Krc, Pc, SF, SB, dY, lg, rv_blk, rg_blk, npairs, d_proj):
    L = Qr.shape[0]
    Lc = Krc.shape[0]
    N, ncc = L // CHUNK, Lc // CHUNK
    rv_grp, rg_grp = _group_index(rv_blk, npairs), _group_index(rg_blk, npairs)
    heads = [(p, h) for p in range(npairs) for h in range(2)]
    kcols = lambda p: slice(p * LANES, (p + 1) * LANES)
    vcols = lambda p, h: slice((2 * p + h) * LANES, (2 * p + h + 1) * LANES)

    def body(lg_ref, q_ref, k_ref, v_ref, g_ref, sf_ref, sb_ref, dy_ref, kc_ref, vc_ref,
             dq_ref, dk_ref, dv_ref, dg_ref, do_ref, dkc_ref, dvc_ref, acc_ref, dS, T, Mks):
        n = pl.program_id(0)
        masks = _head_masks()

        @pl.when(n == 0)
        def _():
            dS[...] = jnp.zeros_like(dS)
            T[...] = jnp.zeros_like(T)
            acc_ref[...] = jnp.zeros_like(acc_ref)
            for p, h in heads:
                Mks[p, h] = _decay_mask(_lam_of(lg_ref, 0, 2 * p + h), _lam_of(lg_ref, 1, 2 * p + h))[0]

        def head_main(p, h):
            lam = _lam_of(lg_ref, 0, 2 * p + h)
            mu = _lam_of(lg_ref, 1, 2 * p + h)
            hm = masks[h]
            hs = vcols(p, h)
            v = v_ref[:, hs]
            k = k_ref[:, kcols(p)]
            sf = sf_ref[p, 0, h]
            sb = sb_ref[p, 0, h]
            f = _ret_chunk_fwd(q_ref[:, kcols(p)], k, v, sf, sb, hm, lam, mu, Mks[p, h])
            dv_, O = f["dv"], f["O"]
            r = lax.rsqrt(jnp.mean(O * O, axis=-1, keepdims=True) + NORM_EPS)
            on = O * r
            g = g_ref[:, hs].astype(F32)
            sg = _sigmoid(g)
            dy = dy_ref[:, hs].astype(F32)
            dg_ref[:, hs] = (dy * on * (sg * (1.0 + g * (1.0 - sg)))).astype(BF16)
            don = dy * (g * sg)
            dO = r * (don - on * jnp.mean(don * on, axis=-1, keepdims=True))
            dOb = dO.astype(BF16)
            do_ref[:, hs] = dOb
            dAm = _nt(dOb, v)
            T[p, h] += dAm * f["Am"]
            dAb = (dAm * f["Mk"]).astype(BF16)
            km = jnp.where(hm, k, jnp.zeros_like(k))
            dq = _nn(dAb, km)
            dk = _tn(dAb, f["qm"])
            dvh = _tn(f["Amb"], dOb)
            dQf = _nt(dOb, sf)
            dQb = _nt(dOb, sb)
            dq = dq + dQf * dv_["qf"] + dQb * dv_["qb"]
            acc_ref[p, h, 0:1, :] += _fsum(dQf * f["Qf"].astype(F32) * (dv_["i"] + 1.0))
            acc_ref[p, h, 1:2, :] += _fsum(dQb * f["Qb"].astype(F32) * (CHUNK - dv_["i"]))
            dSh = dS[p, h]
            dSb_ = dSh.astype(BF16)
            Kf = (km.astype(F32) * dv_["kf"]).astype(BF16)
            dKf = _nt(v, dSb_)
            dk = dk + jnp.where(hm, dKf * dv_["kf"], 0.0)
            acc_ref[p, h, 2:3, :] += _fsum(jnp.where(hm, dKf, 0.0) * Kf.astype(F32) * (CHUNK - 1.0 - dv_["i"]))
            dvh = dvh + _nn(Kf, dSb_)
            acc_ref[p, h, 3:4, :] += float(CHUNK) * dv_["gf"] * _fsum(dSh * sf.astype(F32))
            dSh = dv_["gf"] * dSh + _tn(f["Qf"], dOb)
            dS[p, h] = dSh
            dv_ref[:, hs] = dvh
            return dq, dk

        for p in range(npairs):
            dq0, dk0 = head_main(p, 0)
            dq1, dk1 = head_main(p, 1)
            dq_ref[:, kcols(p)] = dq0 + dq1
            dk_ref[:, kcols(p)] = dk0 + dk1

        @pl.when(n == N - 1)
        def _():
            for p, h in heads:
                lam = _lam_of(lg_ref, 0, 2 * p + h)
                dv_ = _decay_vecs(lam, lam)
                hm = masks[h]
                hs = vcols(p, h)
                states = [jnp.zeros((LANES, LANES), F32)]
                kws = []
                for cc in range(ncc):
                    rows = slice(cc * CHUNK, (cc + 1) * CHUNK)
                    kw = jnp.where(hm, kc_ref[rows, kcols(p)].astype(F32) * dv_["kf"], 0.0).astype(BF16)
                    kws.append(kw)
                    states.append(dv_["gf"] * states[-1] + _tn(kw, vc_ref[rows, hs]))
                d = dS[p, h]
                for cc in reversed(range(ncc)):
                    db = d.astype(BF16)
                    rows = slice(cc * CHUNK, (cc + 1) * CHUNK)
                    dKf_c = jnp.where(hm, _nt(vc_ref[rows, hs], db), 0.0)
                    part = dKf_c * dv_["kf"]
                    if h == 0:
                        dkc_ref[rows, kcols(p)] = part
                    else:
                        dkc_ref[rows, kcols(p)] += part
                    acc_ref[p, h, 2:3, :] += _fsum(dKf_c * kws[cc].astype(F32) * (CHUNK - 1.0 - dv_["i"]))
                    dvc_ref[rows, hs] = _nn(kws[cc], db)
                    acc_ref[p, h, 3:4, :] += float(CHUNK) * dv_["gf"] * _fsum(d * states[cc])
                    d = dv_["gf"] * d
                _, rel, low = _decay_mask(lam, lam)
                Th = T[p, h]
                acc_ref[p, h, 4:5, :] += _fsum(jnp.where(low, Th * rel, 0.0))
                acc_ref[p, h, 5:6, :] += _fsum(jnp.where(low, 0.0, -Th * rel))

    rev = lambda n: N - 1 - n
    wq, wv = npairs * LANES, npairs * 2 * LANES
    st = pl.BlockSpec((npairs, 1, 2, LANES, LANES), lambda n: (0, rev(n), 0, 0, 0))
    pair = pl.BlockSpec((CHUNK, wq), lambda n: (rev(n), 0))
    wide = lambda grp: pl.BlockSpec((CHUNK, wv), lambda n: (rev(n), grp))
    return pl.pallas_call(
        body, name="ret_bwd_desc", grid=(N,),
        in_specs=[pl.BlockSpec(memory_space=pltpu.SMEM), pair, pair, wide(rv_grp), wide(rg_grp), st, st, wide(0),
                  pl.BlockSpec((Lc, wq), lambda n: (0, 0)),
                  pl.BlockSpec((Lc, wv), lambda n: (0, rv_grp))],
        out_specs=[pair, pair, wide(0), wide(rg_grp), wide(0),
                   pl.BlockSpec((Lc, wq), lambda n: (0, 0)),
                   pl.BlockSpec((Lc, wv), lambda n: (0, 0)),
                   pl.BlockSpec((npairs, 2, ACC_ROWS, LANES), lambda n: (0, 0, 0, 0))],
        out_shape=[jax.ShapeDtypeStruct((L, npairs * LANES), F32),
                   jax.ShapeDtypeStruct((L, npairs * LANES), F32),
                   jax.ShapeDtypeStruct((L, npairs * 2 * LANES), F32),
                   jax.ShapeDtypeStruct((L, d_proj), BF16),
                   jax.ShapeDtypeStruct((L, npairs * 2 * LANES), BF16),
                   jax.ShapeDtypeStruct((Lc, npairs * LANES), F32),
                   jax.ShapeDtypeStruct((Lc, npairs * 2 * LANES), F32),
                   jax.ShapeDtypeStruct((npairs, 2, ACC_ROWS, LANES), F32)],
        scratch_shapes=[pltpu.VMEM((npairs, 2, LANES, LANES), F32), pltpu.VMEM((npairs, 2, CHUNK, CHUNK), F32),
                        pltpu.VMEM((npairs, 2, CHUNK, CHUNK), F32)],
        compiler_params=_params(1))(lg, Qr, Kr, P, P, SF, SB, dY, Krc, Pc)


def _ret_bwd2(Qr, Kr, P, Krc, Pc, SB, dO, dKr, dVp, dP, dKc, dVc, lg, rv_blk, npairs):
    L = Qr.shape[0]
    Lc = Krc.shape[0]
    N, ncc = L // CHUNK, Lc // CHUNK
    rv_grp = _group_index(rv_blk, npairs)
    heads = [(p, h) for p in range(npairs) for h in range(2)]
    kcols = lambda p: slice(p * LANES, (p + 1) * LANES)
    vcols = lambda p, h: slice((2 * p + h) * LANES, (2 * p + h + 1) * LANES)

    def body(lg_ref, q_ref, k_ref, v_ref, sb_ref, do_ref, dkin_ref, dvin_ref, kc_ref, vc_ref, dkcin_ref, dvcin_ref,
             dpin_ref, dk_ref, dv_ref, dkc_ref, dvc_ref, acc_ref, dS):
        del dpin_ref
        n = pl.program_id(0)
        masks = _head_masks()

        @pl.when(n == 0)
        def _():
            dS[...] = jnp.zeros_like(dS)
            acc_ref[...] = jnp.zeros_like(acc_ref)

        def head_main(p, h):
            mu = _lam_of(lg_ref, 1, 2 * p + h)
            hm = masks[h]
            hs = vcols(p, h)
            dv_ = _decay_vecs(mu, mu)
            v = v_ref[:, hs]
            k = k_ref[:, kcols(p)]
            q = q_ref[:, kcols(p)]
            dOb = do_ref[:, hs]
            km = jnp.where(hm, k, jnp.zeros_like(k)).astype(F32)
            Kb = (km * dv_["kb"]).astype(BF16)
            Qb = (jnp.where(hm, q, jnp.zeros_like(q)).astype(F32) * dv_["qb"]).astype(BF16)
            dSh = dS[p, h]
            dSb_ = dSh.astype(BF16)
            dKb = jnp.where(hm, _nt(v, dSb_), 0.0)
            acc_ref[p, h, 0:1, :] += _fsum(dKb * Kb.astype(F32) * dv_["i"])
            dv_ref[:, hs] = (dvin_ref[:, hs] + _nn(Kb, dSb_)).astype(BF16)
            acc_ref[p, h, 1:2, :] += float(CHUNK) * dv_["gb"] * _fsum(dSh * sb_ref[p, 0, h].astype(F32))
            dS[p, h] = dv_["gb"] * dSh + _tn(Qb, dOb)
            return dKb * dv_["kb"]

        for p in range(npairs):
            dk_ref[:, kcols(p)] = dkin_ref[:, kcols(p)] + head_main(p, 0) + head_main(p, 1)

        @pl.when(n == N - 1)
        def _():
            for p, h in heads:
                mu = _lam_of(lg_ref, 1, 2 * p + h)
                hm = masks[h]
                hs = vcols(p, h)
                dv_ = _decay_vecs(mu, mu)
                states = {}
                kws = {}
                s = jnp.zeros((LANES, LANES), F32)
                for cc in reversed(range(ncc)):
                    rows = slice(cc * CHUNK, (cc + 1) * CHUNK)
                    states[cc] = s
                    kw = jnp.where(hm, kc_ref[rows, kcols(p)].astype(F32) * dv_["kb"], 0.0).astype(BF16)
                    kws[cc] = kw
                    s = dv_["gb"] * s + _tn(kw, vc_ref[rows, hs])
                d = dS[p, h]
                for cc in range(ncc):
                    db = d.astype(BF16)
                    rows = slice(cc * CHUNK, (cc + 1) * CHUNK)
                    dKb_c = jnp.where(hm, _nt(vc_ref[rows, hs], db), 0.0)
                    part = dKb_c * dv_["kb"]
                    if h == 0:
                        dkc_ref[rows, kcols(p)] = dkcin_ref[rows, kcols(p)] + part
                    else:
                        dkc_ref[rows, kcols(p)] += part
                    acc_ref[p, h, 0:1, :] += _fsum(dKb_c * kws[cc].astype(F32) * dv_["i"])
                    dvc_ref[rows, hs] = dvcin_ref[rows, hs] + _nn(kws[cc], db)
                    acc_ref[p, h, 1:2, :] += float(CHUNK) * dv_["gb"] * _fsum(d * states[cc])
                    d = dv_["gb"] * d

    wq, wv = npairs * LANES, npairs * 2 * LANES
    st = pl.BlockSpec((npairs, 1, 2, LANES, LANES), lambda n: (0, n, 0, 0, 0))
    pair = pl.BlockSpec((CHUNK, wq), lambda n: (n, 0))
    wide = lambda grp: pl.BlockSpec((CHUNK, wv), lambda n: (n, grp))
    ckc = pl.BlockSpec((Lc, wq), lambda n: (0, 0))
    cvc = lambda grp: pl.BlockSpec((Lc, wv), lambda n: (0, grp))
    return pl.pallas_call(
        body, name="ret_bwd_asc", grid=(N,),
        in_specs=[pl.BlockSpec(memory_space=pltpu.SMEM), pair, pair, wide(rv_grp), st, wide(0), pair, wide(0),
                  ckc, cvc(rv_grp), ckc, cvc(0), pl.BlockSpec(memory_space=pl.ANY)],
        out_specs=[pair, wide(rv_grp), ckc, cvc(0),
                   pl.BlockSpec((npairs, 2, ACC_ROWS, LANES), lambda n: (0, 0, 0, 0))],
        out_shape=[jax.ShapeDtypeStruct(dKr.shape, F32),
                   jax.ShapeDtypeStruct(dP.shape, dP.dtype),
                   jax.ShapeDtypeStruct(dKc.shape, F32),
                   jax.ShapeDtypeStruct(dVc.shape, F32),
                   jax.ShapeDtypeStruct((npairs, 2, ACC_ROWS, LANES), F32)],
        input_output_aliases={12: 1},
        scratch_shapes=[pltpu.VMEM((npairs, 2, LANES, LANES), F32)],
        compiler_params=_params(1))(lg, Qr, Kr, P, SB, dO, dKr, dVp, Krc, Pc, dKc, dVc, dP)


GROUP = 4


def _att_band(Lc):
    row = np.arange(GROUP * CHUNK)[:, None] % CHUNK
    col = np.arange(3 * CHUNK + Lc)[None, :]
    ok = ((col >= row) & (col <= row + 2 * CHUNK)) | (col >= 3 * CHUNK)
    return jnp.asarray(np.where(ok, 0.0, NEG), F32)


def _att_edge(n, N, Lc):
    col = lax.broadcasted_iota(jnp.int32, (1, 3 * CHUNK + Lc), 1)
    off = jnp.logical_or(jnp.logical_and(col < CHUNK, n == 0),
                         jnp.logical_and(jnp.logical_and(col >= 2 * CHUNK, col < 3 * CHUNK), n == N - 1))
    return jnp.where(off, NEG, 0.0)


def _stack_heads(ref, gi):
    masks = _head_masks()
    tiles = []
    for pr in range(2):
        t = ref[:, (2 * gi + pr) * LANES:(2 * gi + pr + 1) * LANES]
        for a in range(2):
            tiles.append(jnp.where(masks[a], t, jnp.zeros_like(t)))
    return jnp.concatenate(tiles, axis=0)


def _unstack_heads(x4):
    m0 = _head_masks()[0]
    return [jnp.where(m0, x4[(2 * pr) * CHUNK:(2 * pr + 1) * CHUNK], x4[(2 * pr + 1) * CHUNK:(2 * pr + 2) * CHUNK])
            for pr in range(2)]


def _sink_column(sink_ref, g):
    row = lax.broadcasted_iota(jnp.int32, (GROUP * CHUNK, 1), 0) // CHUNK
    col = jnp.zeros((GROUP * CHUNK, 1), F32)
    for h in range(GROUP):
        col = jnp.where(row == h, sink_ref[0, g * GROUP + h], col)
    return col


def _att_probs(q4, Kall, bias, snk):
    s = _nt(q4, Kall) + bias
    mx = jnp.maximum(jnp.max(s, axis=1, keepdims=True), snk)
    p = jnp.exp(s - mx)
    p_snk = jnp.exp(snk - mx)
    inv = 1.0 / (jnp.sum(p, axis=1, keepdims=True) + p_snk)
    return p, p_snk, inv


def _att_groups_per_step(nkv, blk0):
    for gps in (4, 2):
        if nkv % gps == 0 and blk0 % gps == 0:
            return gps
    return 1


def _att_specs(Lc, N, gps):
    q = pl.BlockSpec((CHUNK, gps * 2 * LANES), lambda g, n: (n, g))
    kv = lambda s: pl.BlockSpec((gps, CHUNK, LANES), lambda g, n: (g, jnp.clip(n + s, 0, N - 1), 0))
    ctx = pl.BlockSpec((gps, Lc, LANES), lambda g, n: (g, 0, 0))
    return q, kv, ctx


def _att_fwd(Qa, Kd, Vd, Kdc, Vdc, sink, Y, blk0):
    L = Qa.shape[0]
    Lc = Kdc.shape[1]
    N = L // CHUNK
    nkv = Kd.shape[0]
    gps = _att_groups_per_step(nkv, blk0)

    def body(sink_ref, band_ref, q_ref, kp, kc_, kn, vp, vc_, vn, kctx, vctx, y_in, o_ref):
        del y_in
        g, n = pl.program_id(0), pl.program_id(1)
        bias = band_ref[...] + _att_edge(n, N, Lc)
        for gi in range(gps):
            Kall = jnp.concatenate([kp[gi], kc_[gi], kn[gi], kctx[gi]], axis=0)
            Vall = jnp.concatenate([vp[gi], vc_[gi], vn[gi], vctx[gi]], axis=0)
            p, _, inv = _att_probs(_stack_heads(q_ref, gi), Kall, bias, _sink_column(sink_ref, g * gps + gi))
            o4 = _nn(p.astype(BF16), Vall) * inv
            for pr, o in enumerate(_unstack_heads(o4)):
                o_ref[:, (2 * gi + pr) * LANES:(2 * gi + pr + 1) * LANES] = o.astype(BF16)

    q, kv, ctx = _att_specs(Lc, N, gps)
    band = pl.BlockSpec((GROUP * CHUNK, 3 * CHUNK + Lc), lambda g, n: (0, 0))
    return pl.pallas_call(
        body, name="att_fwd", grid=(nkv // gps, N),
        in_specs=[pl.BlockSpec(memory_space=pltpu.SMEM), band, q, kv(-1), kv(0), kv(1), kv(-1), kv(0), kv(1), ctx, ctx,
                  pl.BlockSpec(memory_space=pl.ANY)],
        out_specs=pl.BlockSpec((CHUNK, gps * 2 * LANES), lambda g, n: (n, blk0 // gps + g)),
        out_shape=jax.ShapeDtypeStruct(Y.shape, Y.dtype),
        input_output_aliases={11: 0},
        compiler_params=_params(2))(sink, _att_band(Lc), Qa, Kd, Kd, Kd, Vd, Vd, Vd, Kdc, Vdc, Y)


def _att_bwd(Qa, Kd, Vd, Kdc, Vdc, sink, dY, blk0):
    L = Qa.shape[0]
    Lc = Kdc.shape[1]
    N = L // CHUNK
    nkv = Kd.shape[0]
    gps = _att_groups_per_step(nkv, blk0)

    def body(sink_ref, band_ref, q_ref, kp, kc_, kn, vp, vc_, vn, kctx, vctx, dy_ref,
             dq_ref, dkp, dkc_, dkn, dvp, dvc_, dvn, dkctx, dvctx, dsink_ref):
        g, n = pl.program_id(0), pl.program_id(1)

        @pl.when(n == 0)
        def _():
            dkctx[...] = jnp.zeros_like(dkctx)
            dvctx[...] = jnp.zeros_like(dvctx)
            dsink_ref[...] = jnp.zeros_like(dsink_ref)

        bias = band_ref[...] + _att_edge(n, N, Lc)
        for gi in range(gps):
            Kall = jnp.concatenate([kp[gi], kc_[gi], kn[gi], kctx[gi]], axis=0)
            Vall = jnp.concatenate([vp[gi], vc_[gi], vn[gi], vctx[gi]], axis=0)
            q4 = _stack_heads(q_ref, gi)
            do4 = _stack_heads(dy_ref, gi)
            p, p_snk, inv = _att_probs(q4, Kall, bias, _sink_column(sink_ref, g * gps + gi))
            P = p * inv
            dp = _nt(do4, Vall)
            delta = jnp.sum(P * dp, axis=1, keepdims=True)
            ds = (P * (dp - delta)).astype(BF16)
            dsnk = -(p_snk * inv) * delta
            for h in range(GROUP):
                dsink_ref[gi, h:h + 1, :] += _fsum(dsnk[h * CHUNK:(h + 1) * CHUNK])
            for pr, dq in enumerate(_unstack_heads(_nn(ds, Kall))):
                dq_ref[:, (2 * gi + pr) * LANES:(2 * gi + pr + 1) * LANES] = dq
            dK = _tn(ds, q4)
            dV = _tn(P.astype(BF16), do4)
            for j, (rk, rv) in enumerate([(dkp, dvp), (dkc_, dvc_), (dkn, dvn)]):
                rk[gi] = dK[j * CHUNK:(j + 1) * CHUNK].astype(BF16)
                rv[gi] = dV[j * CHUNK:(j + 1) * CHUNK].astype(BF16)
            dkctx[gi] += dK[3 * CHUNK:]
            dvctx[gi] += dV[3 * CHUNK:]

    q, kv, ctx = _att_specs(Lc, N, gps)
    band = pl.BlockSpec((GROUP * CHUNK, 3 * CHUNK + Lc), lambda g, n: (0, 0))
    blk = pl.BlockSpec((gps, CHUNK, LANES), lambda g, n: (g, n, 0))
    part = jax.ShapeDtypeStruct((nkv, L, LANES), BF16)
    cshape = jax.ShapeDtypeStruct((nkv, Lc, LANES), F32)
    return pl.pallas_call(
        body, name="att_bwd", grid=(nkv // gps, N),
        in_specs=[pl.BlockSpec(memory_space=pltpu.SMEM), band, q, kv(-1), kv(0), kv(1), kv(-1), kv(0), kv(1), ctx, ctx,
                  pl.BlockSpec((CHUNK, gps * 2 * LANES), lambda g, n: (n, blk0 // gps + g))],
        out_specs=[q, blk, blk, blk, blk, blk, blk, ctx, ctx,
                   pl.BlockSpec((gps, 8, LANES), lambda g, n: (g, 0, 0))],
        out_shape=[jax.ShapeDtypeStruct(Qa.shape, F32), part, part, part, part, part, part, cshape, cshape,
                   jax.ShapeDtypeStruct((nkv, 8, LANES), F32)],
        compiler_params=_params(2))(sink, _att_band(Lc), Qa, Kd, Kd, Kd, Vd, Vd, Vd, Kdc, Vdc, dY)


def _scale_rows(dx, gt, saved, name):
    M, D = dx.shape
    tm = _tile(M, 512, 8)

    def body(dx_ref, gt_ref, sv_ref, dz_ref, dgt_ref):
        @pl.when(pl.program_id(0) == 0)
        def _():
            dgt_ref[...] = jnp.zeros_like(dgt_ref)
        d = dx_ref[...]
        dz_ref[...] = (d * gt_ref[...]).astype(BF16)
        dgt_ref[...] += jnp.sum(d * sv_ref[...].astype(F32), axis=0, keepdims=True)

    row = pl.BlockSpec((tm, D), lambda i: (i, 0))
    vec = pl.BlockSpec((1, D), lambda i: (0, 0))
    return pl.pallas_call(
        body, name=name, grid=(M // tm,), in_specs=[row, vec, row], out_specs=[row, vec],
        out_shape=[jax.ShapeDtypeStruct((M, D), BF16), jax.ShapeDtypeStruct((1, D), F32)],
        compiler_params=_params(1))(dx, gt, saved)


def _bwd_proj(dz, w, G=None, U=None, name="bwd_proj"):
    M, D = dz.shape
    N = w.shape[0]
    swiglu = G is not None
    tm, tn = _tile(M, ROWS_PER_LATCH, 8), _tile(N, 512)

    def body(*refs):
        if swiglu:
            dz_ref, w_ref, G_ref, U_ref, dG_ref, dU_ref = refs
        else:
            dz_ref, w_ref, dA_ref = refs
        dA = _nt(dz_ref[...], w_ref[...])
        if swiglu:
            Gv = G_ref[...].astype(F32)
            Uv = U_ref[...].astype(F32)
            sg = _sigmoid(Gv)
            dU_ref[...] = (dA * Gv * sg).astype(BF16)
            dG_ref[...] = (dA * Uv * (sg * (1.0 + Gv * (1.0 - sg)))).astype(BF16)
        else:
            dA_ref[...] = dA.astype(BF16)

    row = pl.BlockSpec((tm, D), lambda i, j: (i, 0))
    tile = pl.BlockSpec((tm, tn), lambda i, j: (i, j))
    big = jax.ShapeDtypeStruct((M, N), BF16)
    in_specs = [row, pl.BlockSpec((tn, D), lambda i, j: (j, 0))]
    args = [dz, w]
    if swiglu:
        in_specs += [tile, tile]
        args += [G, U]
        out_specs, out_shape = [tile, tile], [big, big]
    else:
        out_specs, out_shape = tile, big
    return pl.pallas_call(
        body, name=name, grid=(M // tm, N // tn), in_specs=in_specs, out_specs=out_specs, out_shape=out_shape,
        compiler_params=_params(2))(*args)


def _tn_matmul(pairs, name):
    Ka, Nb = pairs[0][0].shape[1], pairs[0][1].shape[1]
    tk, tn = _tile(Ka, 2048), _tile(Nb, 2048)
    tls, nks = [], []
    for a, _ in pairs:
        tl = _tile(a.shape[0], 1024, 8)
        tls.append(tl)
        nks.append(a.shape[0] // tl)
    starts = [int(s) for s in np.cumsum([0] + nks[:-1])]
    nk = int(sum(nks))

    def body(*refs):
        out_ref, acc = refs[-2], refs[-1]
        k = pl.program_id(2)

        @pl.when(k == 0)
        def _():
            acc[...] = jnp.zeros_like(acc)

        for idx in range(len(pairs)):
            a_ref, b_ref = refs[2 * idx], refs[2 * idx + 1]

            @pl.when(jnp.logical_and(k >= starts[idx], k < starts[idx] + nks[idx]))
            def _():
                acc[...] += _tn(a_ref[...], b_ref[...])

        @pl.when(k == nk - 1)
        def _():
            out_ref[...] = acc[...].astype(BF16)

    in_specs, args = [], []
    for idx, (a, b) in enumerate(pairs):
        s0, n_ = starts[idx], nks[idx]
        in_specs.append(pl.BlockSpec((tls[idx], tk), lambda i, j, k, s0=s0, n_=n_: (jnp.clip(k - s0, 0, n_ - 1), i)))
        in_specs.append(pl.BlockSpec((tls[idx], tn), lambda i, j, k, s0=s0, n_=n_: (jnp.clip(k - s0, 0, n_ - 1), j)))
        args += [a, b]
    return pl.pallas_call(
        body, name=name, grid=(Ka // tk, Nb // tn, nk), in_specs=in_specs,
        out_specs=pl.BlockSpec((tk, tn), lambda i, j, k: (i, j)),
        out_shape=jax.ShapeDtypeStruct((Ka, Nb), BF16),
        scratch_shapes=[pltpu.VMEM((tk, tn), F32)], compiler_params=_params(3))(*args)


def _bwd_norm_mod(pairs, x, dres, g, sh, sc, name):
    M, D = x.shape
    K = pairs[0][0].shape[1]
    tm, tk = _tile(M, 512, 8), _tile(K, 1152 if len(pairs) == 1 else 512)
    nk = K // tk
    npair = len(pairs)
    has_res = dres is not None

    def body(*refs):
        pr = refs[:2 * npair]
        rest = refs[2 * npair:]
        if has_res:
            x_ref, dres_ref, g_ref, sh_ref, sc_ref, dx_ref, st_ref, acc = rest
        else:
            x_ref, g_ref, sh_ref, sc_ref, dx_ref, st_ref, acc = rest
        del sh_ref
        i, k = pl.program_id(0), pl.program_id(1)

        @pl.when(jnp.logical_and(i == 0, k == 0))
        def _():
            st_ref[...] = jnp.zeros_like(st_ref)

        @pl.when(k == 0)
        def _():
            acc[...] = jnp.zeros_like(acc)

        t = _nt(pr[1][...], pr[0][...])
        for idx in range(1, npair):
            t = t + _nt(pr[2 * idx + 1][...], pr[2 * idx][...])
        acc[...] += t

        @pl.when(k == nk - 1)
        def _():
            xv = x_ref[...]
            gv = g_ref[...]
            dh = acc[...].T
            r = lax.rsqrt(jnp.mean(xv * xv, axis=-1, keepdims=True) + NORM_EPS)
            xh = xv * r
            st_ref[0:1, :] += jnp.sum(dh, axis=0, keepdims=True)
            st_ref[1:2, :] += jnp.sum(dh * (xh * gv), axis=0, keepdims=True)
            dn = dh * (1.0 + sc_ref[...])
            st_ref[2:3, :] += jnp.sum(dn * xh, axis=0, keepdims=True)
            dxh = dn * gv
            d = r * (dxh - xh * jnp.mean(dxh * xh, axis=-1, keepdims=True))
            if has_res:
                d = d + dres_ref[...]
            dx_ref[...] = d

    row = pl.BlockSpec((tm, D), lambda i, k: (i, 0))
    vec = pl.BlockSpec((1, D), lambda i, k: (0, 0))
    in_specs, args = [], []
    for dA, w in pairs:
        in_specs += [pl.BlockSpec((tm, tk), lambda i, k: (i, k)), pl.BlockSpec((D, tk), lambda i, k: (0, k))]
        args += [dA, w]
    in_specs += [row] + ([row] if has_res else []) + [vec, vec, vec]
    args += [x] + ([dres] if has_res else []) + [g, sh, sc]
    return pl.pallas_call(
        body, name=name, grid=(M // tm, nk), in_specs=in_specs,
        out_specs=[row, pl.BlockSpec((8, D), lambda i, k: (0, 0))],
        out_shape=[jax.ShapeDtypeStruct((M, D), F32), jax.ShapeDtypeStruct((8, D), F32)],
        scratch_shapes=[pltpu.VMEM((D, tm), F32)], compiler_params=_params(2))(*args)


def _local_step(x, ctx, tgt, mod, modc, norm_mix, norm_ffn, norm_final, lg, sink, T, w_in, rest_weights, on_grads, flush):
    L, D = x.shape
    Lc = ctx.shape[0]
    d_proj = w_in.shape[1]
    npairs = RET_HEADS // 2
    nkv = ATT_KV_HEADS
    nkvp = nkv // 2
    o_rq = 0
    o_rk = o_rq + RET_HEADS * RET_DK // LANES
    o_rv = o_rk + RET_HEADS * RET_DK // LANES
    o_rg = o_rv + RET_HEADS * RET_DV // LANES
    o_aq = o_rg + RET_HEADS * RET_DV // LANES
    o_ak = o_aq + ATT_HEADS * ATT_DH // LANES
    o_av = o_ak + nkv * ATT_DH // LANES
    assert (o_av + nkv * ATT_DH // LANES) * LANES == d_proj
    assert o_rv % 2 == 0 and o_rg % 2 == 0 and (RET_HEADS * RET_DV) % (2 * LANES) == 0
    rv_blk, rg_blk = o_rv // 2, o_rg // 2
    d_ret = RET_HEADS * RET_DV
    d_mix = d_ret + ATT_HEADS * ATT_DH
    att_blk = d_ret // (2 * LANES)
    k_scale = RET_DK ** -0.5
    a_scale = ATT_DH ** -0.5

    Tc = dict(C=jnp.ones((Lc, LANES), F32), S=jnp.zeros((Lc, LANES), F32))
    row = lambda m, i: m[i:i + 1]
    sh_m, sc_m, gt_m, sh_f, sc_f, gt_f = [row(mod, i) for i in range(6)]
    sh_mc, sc_mc = row(modc, 0), row(modc, 1)

    P, hx = _norm_mod_matmul(x, norm_mix, sh_m, sc_m, w_in, "in_proj")
    Pc, hc = _norm_mod_matmul(ctx, norm_mix, sh_mc, sc_mc, w_in, "in_proj_ctx")
    nq = RET_HEADS * RET_DK // LANES
    Qr = _rope_cols(P, o_rq, nq, T["Cr"], T["Sr"], T["Rr"], 1.0, True, "rope_rq")
    Kr = _rope_cols(P, o_rk, nq, T["Cr"], T["Sr"], T["Rr"], k_scale, True, "rope_rk")
    Krc = _rope_cols(Pc, o_rk, nq, Tc["C"], Tc["S"], T["Rr"], k_scale, False, "scale_rk_ctx")
    Qa = _rope_cols(P, o_aq, ATT_HEADS * ATT_DH // LANES, T["Ca"], T["Sa"], T["Ra"], a_scale, True, "rope_aq")
    Kd = _dup_heads(P, o_ak, nkvp, T["Ca"], T["Sa"], T["Ra"], T["D0"], T["D1"], True, "dup_ak")
    Vd = _dup_heads(P, o_av, nkvp, T["Ca"], T["Sa"], T["Ra"], T["D0"], T["D1"], False, "dup_av")
    Kdc = _dup_heads(Pc, o_ak, nkvp, Tc["C"], Tc["S"], T["Ra"], T["D0"], T["D1"], False, "dup_ak_ctx")
    Vdc = _dup_heads(Pc, o_av, nkvp, Tc["C"], Tc["S"], T["Ra"], T["D0"], T["D1"], False, "dup_av_ctx")

    SF = _ret_states_fwd(Kr, P, Krc, Pc, lg, rv_blk, npairs)
    Y, SB = _ret_out_fwd(Qr, Kr, P, Krc, Pc, SF, lg, rv_blk, rg_blk, npairs, d_mix)
    tok = rest_weights(Y)
    Y = _att_fwd(Qa, Kd, Vd, Kdc, Vdc, sink + tok, Y, att_blk)

    w_out, w_gate, w_up, w_down = rest_weights(Y)
    x1, O1 = _proj_residual(Y, w_out, x, gt_m, "out_proj")
    G, U, A, h2 = _ffn_in(x1, norm_ffn, sh_f, sc_f, w_gate, w_up)
    x2, Fo = _proj_residual(A, w_down, x1, gt_f, "ffn_out")
    dx2, loss, d_norm_final, dz2, dgt_f = _final(x2, norm_final, tgt, gt_f, Fo)

    dG, dU = _bwd_proj(dz2, w_down, G, U, name="ffn_out_bwd")
    g_w_down = _tn_matmul([(A, dz2)], "grad_w_down")
    tok = on_grads(["w_down"], [g_w_down])
    dx1, st_f = _bwd_norm_mod([(dG, w_gate), (dU, w_up)], x1, dx2, norm_ffn + tok, sh_f, sc_f, "ffn_in_bwd")
    tok = flush(dx1)
    g_w_gate = _tn_matmul([(h2, dG)], "grad_w_gate")
    g_w_up = _tn_matmul([(h2, dU)], "grad_w_up")
    tok = tok + on_grads(["w_gate", "w_up"], [g_w_gate, g_w_up])
    dz1, dgt_m = _scale_rows(dx1, gt_m + tok, O1, "mix_gate_bwd")
    dY = _bwd_proj(dz1, w_out, name="out_proj_bwd")
    tok = flush(dY)
    g_w_out = _tn_matmul([(Y, dz1)], "grad_w_out")
    tok = tok + on_grads(["w_out"], [g_w_out])

    dQa, dKp, dKs, dKn, dVp, dVs, dVn, dKdc, dVdc, dsink = _att_bwd(Qa, Kd, Vd, Kdc, Vdc, sink + tok, dY, att_blk)
    tok = flush(dQa)
    dQr, dKr, dVr, dP, dO, dKc, dVc, acc1 = _ret_bwd1(Qr, Kr, P, Krc, Pc, SF, SB, dY, lg + tok, rv_blk, rg_blk, npairs,
                                                      d_proj)
    dKr, dP, dKc, dVc, acc2 = _ret_bwd2(Qr, Kr, P, Krc, Pc, SB, dO, dKr, dVr, dP, dKc, dVc, lg, rv_blk, npairs)

    dP = _unrope_cols(dQr, dP, o_rq, nq, T["Cr"], T["Sr"], T["RrT"], 1.0, True, "unrope_rq")
    dP = _unrope_cols(dKr, dP, o_rk, nq, T["Cr"], T["Sr"], T["RrT"], k_scale, True, "unrope_rk")
    dP = _unrope_cols(dQa, dP, o_aq, ATT_HEADS * ATT_DH // LANES, T["Ca"], T["Sa"], T["RaT"], a_scale, True, "unrope_aq")
    dP = _fold_heads([(dKs, 0), (dKp, 1), (dKn, -1)], dP, o_ak, nkvp, T["Ca"], T["Sa"], T["RaT"], T["D0T"], T["D1T"],
                     True, "fold_ak")
    dP = _fold_heads([(dVs, 0), (dVp, 1), (dVn, -1)], dP, o_av, nkvp, T["Ca"], T["Sa"], T["RaT"], T["D0T"], T["D1T"],
                     False, "fold_av")
    dPc = jnp.zeros((Lc, d_proj), BF16)
    dPc = _unrope_cols(dKc, dPc, o_rk, nq, Tc["C"], Tc["S"], T["RrT"], k_scale, False, "ctx_rk_bwd")
    dPc = _unrope_cols(dVc, dPc, o_rv, RET_HEADS * RET_DV // LANES, Tc["C"], Tc["S"], T["RrT"], 1.0, False, "ctx_rv_bwd")
    dPc = _fold_heads([(dKdc.astype(BF16), 0)], dPc, o_ak, nkvp, Tc["C"], Tc["S"], T["RaT"], T["D0T"], T["D1T"],
                      False, "fold_ak_ctx")
    dPc = _fold_heads([(dVdc.astype(BF16), 0)], dPc, o_av, nkvp, Tc["C"], Tc["S"], T["RaT"], T["D0T"], T["D1T"],
                      False, "fold_av_ctx")

    dx, st_m = _bwd_norm_mod([(dP, w_in)], x, dx1, norm_mix, sh_m, sc_m, "in_proj_bwd")
    _, st_mc = _bwd_norm_mod([(dPc, w_in)], ctx, None, norm_mix, sh_mc, sc_mc, "in_proj_ctx_bwd")
    g_w_in = _tn_matmul([(hx, dP), (hc, dPc)], "grad_w_in")
    on_grads(["w_in"], [g_w_in])

    a1 = acc1[:, :, :, 0].reshape(RET_HEADS, ACC_ROWS)
    a2 = acc2[:, :, :, 0].reshape(RET_HEADS, ACC_ROWS)
    dlam = (a1[:, 0] + a1[:, 2] + a1[:, 3] + a1[:, 4]) * lg[0]
    dmu = (a1[:, 1] + a1[:, 5] + a2[:, 0] + a2[:, 1]) * lg[1]
    d_sink = dsink[:, :4, 0].reshape(1, ATT_HEADS)

    nh = RET_HEADS
    assert 2 * nh + ATT_HEADS <= LOSS_LANE
    small = _pack_rows(
        [(st_m, 0, 2, 0, 0), (dgt_m, 0, 1, 2, 0), (st_f, 0, 2, 3, 0), (dgt_f, 0, 1, 5, 0), (st_mc, 0, 2, 6, 0),
         (st_m[2:3] + st_mc[2:3], 0, 1, 12, 0), (st_f, 2, 1, 13, 0), (d_norm_final, 0, 1, 14, 0),
         (dlam.reshape(1, nh), 0, 1, 15, 0), (dmu.reshape(1, nh), 0, 1, 15, nh), (d_sink, 0, 1, 15, 2 * nh),
         (loss[:, 0:1], 0, 1, 15, LOSS_LANE)], 16, D, "pack_small")
    return dict(grad_x=dx, small=small)


def _my_pos():
    return lax.axis_index("x"), lax.axis_index("y"), lax.axis_index("c")


def _other_chips(x, y):
    return [(1 - x, y), (x, 1 - y), (1 - x, 1 - y)]


def _remote(src, dst, ssem, rsem, dev):
    return pltpu.make_async_remote_copy(src_ref=src, dst_ref=dst, send_sem=ssem, recv_sem=rsem,
                                        device_id=dev, device_id_type=MESH)


def _allgather8(v, name):
    R, Cc = v.shape

    def body(v_ref, out_ref, send_sems, recv_sems):
        x, y, c = _my_pos()
        me = 4 * x + 2 * y + c
        out_ref[pl.ds(me, 1)] = v_ref[...][None]
        peers = []
        for j in range(1, N_DEV):
            peers.append((1 - x if (j >> 2) & 1 else x, 1 - y if (j >> 1) & 1 else y, 1 - c if j & 1 else c))
        copies = []
        for j, peer in enumerate(peers):
            cp = _remote(v_ref, out_ref.at[me], send_sems.at[j], recv_sems.at[j], peer)
            cp.start()
            copies.append(cp)
        for j, peer in enumerate(peers):
            pid = 4 * peer[0] + 2 * peer[1] + peer[2]
            _remote(v_ref, out_ref.at[pid], send_sems.at[j], recv_sems.at[j], peer).wait_recv()
        for cp in copies:
            cp.wait_send()

    return pl.pallas_call(
        body, name=name, out_shape=jax.ShapeDtypeStruct((N_DEV, R, Cc), v.dtype),
        in_specs=[pl.BlockSpec(memory_space=pltpu.VMEM)], out_specs=pl.BlockSpec(memory_space=pltpu.VMEM),
        scratch_shapes=[pltpu.SemaphoreType.DMA((N_DEV - 1,)), pltpu.SemaphoreType.DMA((N_DEV - 1,))])(v)


def _region(ref, k, half, shard_shape, axis):
    r, cs = shard_shape
    hr = r // 2
    if axis == 1:
        return ref.at[pl.ds(pl.multiple_of(half * hr, 16), hr), pl.ds(pl.multiple_of(k * cs, LANES), cs)]
    return ref.at[pl.ds(pl.multiple_of(k * r + half * hr, 16), hr), :]


def _full_shape(shard_shape, axis):
    r, cs = shard_shape
    return (r, N_CHIPS * cs) if axis == 1 else (N_CHIPS * r, cs)


def _half_pieces(ref, half, shard_shape, axis):
    r, cs = shard_shape
    hr = r // 2
    if axis == 1:
        return [ref.at[pl.ds(pl.multiple_of(half * hr, 16), hr), :]]
    return [ref.at[pl.ds(pl.multiple_of(k * r + half * hr, 16), hr), :] for k in range(N_CHIPS)]


def _half_block_spec(shard_shape, axis, tr):
    r, cs = shard_shape
    hr = r // 2
    if axis == 1:
        return pl.BlockSpec((tr, cs), lambda k, i, c_ref: (c_ref[0] * (hr // tr) + i, k))
    return pl.BlockSpec((tr, cs), lambda k, i, c_ref: (k * (r // tr) + c_ref[0] * (hr // tr) + i, 0))


def _add_halves(g, recv, cvec, shard_shape, axis, name):
    r, cs = shard_shape
    hr = r // 2
    tr = _tile(hr, 256, 16)

    def body(c_ref, a_ref, b_ref, o_ref):
        del c_ref
        o_ref[0] = (a_ref[...].astype(F32) + b_ref[...].astype(F32)).astype(BF16)

    spec = _half_block_spec(shard_shape, axis, tr)
    return pl.pallas_call(
        body, name=name,
        grid_spec=pltpu.PrefetchScalarGridSpec(
            num_scalar_prefetch=1, grid=(N_CHIPS, hr // tr), in_specs=[spec, spec],
            out_specs=pl.BlockSpec((1, tr, cs), lambda k, i, c_ref: (k, i, 0))),
        out_shape=jax.ShapeDtypeStruct((N_CHIPS, hr, cs), BF16),
        compiler_params=_params(2, False))(cvec, g, recv)


def _sum_chips(sums, landed, kc, name):
    _, hr, cs = sums.shape
    tr = _tile(hr, 256, 16)

    def body(kc_ref, own_ref, a_ref, b_ref, c_ref, o_ref):
        del kc_ref
        o_ref[...] = (own_ref[0].astype(F32) + a_ref[0].astype(F32)) + (b_ref[0].astype(F32) + c_ref[0].astype(F32))

    slot = lambda j: pl.BlockSpec((1, tr, cs), lambda i, kc_ref: ((kc_ref[0] + j) % N_CHIPS, i, 0))
    return pl.pallas_call(
        body, name=name,
        grid_spec=pltpu.PrefetchScalarGridSpec(
            num_scalar_prefetch=1, grid=(hr // tr,), in_specs=[slot(0), slot(1), slot(2), slot(3)],
            out_specs=pl.BlockSpec((tr, cs), lambda i, kc_ref: (kc_ref[1] * (hr // tr) + i, 0))),
        out_shape=jax.ShapeDtypeStruct((2 * hr, cs), F32),
        compiler_params=_params(1, False))(kc, sums, landed, landed, landed)


def _exchange_halves(shards, name):
    nw = len(shards)

    def body(*refs):
        out_refs = refs[nw:2 * nw]
        send, recv = refs[2 * nw:]
        x, y, c = _my_pos()
        sib = (x, y, 1 - c)
        copies = []
        for w in range(nw):
            hr = shards[w].shape[0] // 2
            mine = out_refs[w].at[pl.ds(pl.multiple_of(c * hr, 8), hr), :]
            cp = _remote(mine, mine, send.at[w], recv.at[w], sib)
            cp.start()
            copies.append(cp)
        for w in range(nw):
            hr = shards[w].shape[0] // 2
            other = out_refs[w].at[pl.ds(pl.multiple_of((1 - c) * hr, 8), hr), :]
            _remote(other, other, send.at[w], recv.at[w], sib).wait_recv()
        for cp in copies:
            cp.wait_send()

    anyspec = pl.BlockSpec(memory_space=pl.ANY)
    return pl.pallas_call(
        body, name=name,
        out_shape=[jax.ShapeDtypeStruct(s.shape, F32) for s in shards],
        in_specs=[anyspec] * nw, out_specs=[anyspec] * nw,
        input_output_aliases={w: w for w in range(nw)},
        scratch_shapes=[pltpu.SemaphoreType.DMA((nw,)), pltpu.SemaphoreType.DMA((nw,))])(*shards)


def _cast_into_full(w, kc, axis, name, after=None):
    r, cs = w.shape
    tr = _tile(r, 256, 16)
    tied = after is not None

    def body(kc_ref, w_ref, *rest):
        del kc_ref
        if tied:
            rest[1][...] = w_ref[...].astype(BF16)
            rest[2][...] = jnp.zeros_like(rest[2])
        else:
            rest[0][...] = w_ref[...].astype(BF16)

    if axis == 1:
        ospec = pl.BlockSpec((tr, cs), lambda i, kc_ref: (i, kc_ref[0]))
    else:
        ospec = pl.BlockSpec((tr, cs), lambda i, kc_ref: (kc_ref[0] * (r // tr) + i, 0))
    tok = pl.BlockSpec((8, LANES), lambda i, kc_ref: (0, 0))
    full = jax.ShapeDtypeStruct(_full_shape((r, cs), axis), BF16)
    in_specs, args = [pl.BlockSpec((tr, cs), lambda i, kc_ref: (i, 0))], [w]
    if tied:
        in_specs.append(tok)
        args.append(after)
    return pl.pallas_call(
        body, name=name,
        grid_spec=pltpu.PrefetchScalarGridSpec(num_scalar_prefetch=1, grid=(r // tr,), in_specs=in_specs,
                                               out_specs=[ospec, tok] if tied else ospec),
        out_shape=[full, jax.ShapeDtypeStruct((8, LANES), F32)] if tied else full,
        compiler_params=_params(1, False))(kc, *args)


def _adam_math(w, g, m, v):
    m2 = ADAM_B1 * m + (1.0 - ADAM_B1) * g
    v2 = ADAM_B2 * v + (1.0 - ADAM_B2) * (g * g)
    m_hat = m2 / (1.0 - ADAM_B1 ** ADAM_STEP)
    v_hat = v2 / (1.0 - ADAM_B2 ** ADAM_STEP)
    delta = -ADAM_LR * (m_hat / (jnp.sqrt(v_hat) + ADAM_EPS) + ADAM_WD * w)
    return delta, m2, v2


def _adam(w, g, m, v, name):
    r, cs = w.shape
    tr = _tile(r, 256, 8)

    def body(w_ref, g_ref, m_ref, v_ref, d_ref, m2_ref, v2_ref):
        d, m2, v2 = _adam_math(w_ref[...], g_ref[...], m_ref[...], v_ref[...])
        d_ref[...] = d
        m2_ref[...] = m2
        v2_ref[...] = v2

    spec = pl.BlockSpec((tr, cs), lambda i: (i, 0))
    shp = jax.ShapeDtypeStruct((r, cs), F32)
    return pl.pallas_call(body, name=name, grid=(r // tr,), in_specs=[spec] * 4, out_specs=[spec] * 3,
                          out_shape=[shp, shp, shp], compiler_params=_params(1, False))(w, g, m, v)


def _mod_rows(a16, w, b, name):
    D, n = w.shape
    tn = _tile(n, 512)

    def body(a_ref, w_ref, b_ref, o_ref):
        a = a_ref[...]
        o_ref[...] = _nn((a * _sigmoid(a)).astype(BF16), w_ref[...].astype(BF16)) + b_ref[...]

    return pl.pallas_call(
        body, name=name, grid=(n // tn,),
        in_specs=[pl.BlockSpec((16, D), lambda j: (0, 0)), pl.BlockSpec((D, tn), lambda j: (0, j)),
                  pl.BlockSpec((1, tn), lambda j: (0, j))],
        out_specs=pl.BlockSpec((16, tn), lambda j: (0, j)),
        out_shape=jax.ShapeDtypeStruct((16, n), F32), compiler_params=_params(1, False))(a16, w, b)


def _w_mod_update(a16, d16, w, m, v):
    D, n = w.shape
    tn = _tile(n, 256)

    def body(a_ref, d_ref, w_ref, m_ref, v_ref, g_ref, dl_ref, m2_ref, v2_ref, p_ref):
        @pl.when(pl.program_id(0) == 0)
        def _():
            p_ref[...] = jnp.zeros_like(p_ref)
        a = a_ref[...]
        db = d_ref[...].astype(BF16)
        wv = w_ref[...]
        g = _tn((a * _sigmoid(a)).astype(BF16), db)
        g_ref[...] = g
        d, m2, v2 = _adam_math(wv, g, m_ref[...], v_ref[...])
        dl_ref[...] = d
        m2_ref[...] = m2
        v2_ref[...] = v2
        p_ref[...] += _nt(db, wv.astype(BF16))

    wspec = pl.BlockSpec((D, tn), lambda j: (0, j))
    shp = jax.ShapeDtypeStruct((D, n), F32)
    return pl.pallas_call(
        body, name="w_mod_update", grid=(n // tn,),
        in_specs=[pl.BlockSpec((16, D), lambda j: (0, 0)), pl.BlockSpec((16, tn), lambda j: (0, j)), wspec, wspec, wspec],
        out_specs=[wspec, wspec, wspec, wspec, pl.BlockSpec((16, D), lambda j: (0, 0))],
        out_shape=[shp, shp, shp, shp, jax.ShapeDtypeStruct((16, D), F32)],
        compiler_params=_params(1))(a16, d16, w, m, v)


def _sum_devices(g8, name):
    _, R, Cc = g8.shape

    def body(g_ref, o_ref):
        t = g_ref[0]
        for d in range(1, N_DEV):
            t = t + g_ref[d]
        o_ref[...] = t

    return pl.pallas_call(body, name=name, out_shape=jax.ShapeDtypeStruct((R, Cc), F32))(g8)


def _c_ctx_grad(parts, c_ctx):
    D = c_ctx.shape[1]

    def body(p_ref, c_ref, o_ref):
        t = p_ref[0]
        for k in range(1, N_CHIPS):
            t = t + p_ref[2 * k]
        cv = c_ref[...]
        sg = _sigmoid(cv)
        o_ref[...] = t * (sg * (1.0 + cv * (1.0 - sg)))

    return pl.pallas_call(body, name="c_ctx_grad", out_shape=jax.ShapeDtypeStruct((1, D), F32))(parts, c_ctx)


def _pack_rows(items, nrows, width, name):
    arrays, plan = [], []
    for a, r0, nr, d0, c0 in items:
        for ai, b in enumerate(arrays):
            if b is a:
                break
        else:
            ai = len(arrays)
            arrays.append(a)
        plan.append((ai, r0, nr, d0, c0, a.shape[1]))

    def body(*refs):
        o_ref = refs[-1]
        o_ref[...] = jnp.zeros_like(o_ref)
        for ai, r0, nr, d0, c0, w in plan:
            o_ref[d0:d0 + nr, c0:c0 + w] = refs[ai][r0:r0 + nr, :]

    return pl.pallas_call(body, name=name, out_shape=jax.ShapeDtypeStruct((nrows, width), F32))(*arrays)


HBM_SPEC = pl.BlockSpec(memory_space=pltpu.HBM)
SEM_SPEC = pl.BlockSpec(memory_space=pltpu.SEMAPHORE)
SPLIT_PARAMS = pltpu.CompilerParams(has_side_effects=pltpu.SideEffectType.DATAFLOW_SIDE_EFFECTING)


def _in_hbm(a):
    return pltpu.with_memory_space_constraint(a, pltpu.HBM)


def _ag_chips_start(fulls, shapes, axes, after, name):
    nw = len(fulls)

    def body(*refs):
        in_refs, send, recv, token = refs[:nw], refs[nw + 1], refs[nw + 2], refs[-1]
        x, y, c = _my_pos()
        k0 = 2 * x + y
        for w in range(nw):
            own = _region(in_refs[w], k0, c, shapes[w], axes[w])
            for j, ch in enumerate(_other_chips(x, y)):
                _remote(own, own, send.at[3 * w + j], recv.at[3 * w + j], (ch[0], ch[1], c)).start()
        token[...] = jnp.zeros_like(token)

    return pl.pallas_call(
        body, name=name,
        out_shape=(pltpu.SemaphoreType.DMA((3 * nw,)), pltpu.SemaphoreType.DMA((3 * nw,)),
                   *[pltpu.HBM(f.shape, f.dtype) for f in fulls], jax.ShapeDtypeStruct((8, LANES), F32)),
        in_specs=[HBM_SPEC] * nw + [pl.BlockSpec(memory_space=pl.ANY)],
        out_specs=(SEM_SPEC, SEM_SPEC, *[HBM_SPEC] * nw, pl.BlockSpec(memory_space=pltpu.VMEM)),
        input_output_aliases={w: 2 + w for w in range(nw)},
        compiler_params=SPLIT_PARAMS)(*[_in_hbm(f) for f in fulls], after)


def _ag_chips_wait(send, recv, fulls, shapes, axes, after, name):
    nw = len(fulls)

    def body(*refs):
        in_refs, send_ref, recv_ref = refs[:nw], refs[nw], refs[nw + 1]
        x, y, c = _my_pos()
        k0 = 2 * x + y
        for w in range(nw):
            own = _region(in_refs[w], k0, c, shapes[w], axes[w])
            for j, ch in enumerate(_other_chips(x, y)):
                got = _region(in_refs[w], 2 * ch[0] + ch[1], c, shapes[w], axes[w])
                cp = _remote(own, got, send_ref.at[3 * w + j], recv_ref.at[3 * w + j], (ch[0], ch[1], c))
                cp.wait_send()
                cp.wait_recv()

    return pl.pallas_call(
        body, name=name,
        out_shape=tuple(pltpu.HBM(f.shape, f.dtype) for f in fulls),
        in_specs=[HBM_SPEC] * nw + [SEM_SPEC, SEM_SPEC, pl.BlockSpec(memory_space=pl.ANY)],
        out_specs=tuple([HBM_SPEC] * nw),
        input_output_aliases={w: w for w in range(nw)},
        compiler_params=SPLIT_PARAMS)(*fulls, send, recv, after)


def _ag_forward(fulls, shapes, axes, name):
    nw = len(fulls)

    def body(*refs):
        out_refs = refs[nw:2 * nw]
        send, recv = refs[2 * nw:]
        x, y, c = _my_pos()
        sib = (x, y, 1 - c)
        chips = _other_chips(x, y)
        copies = []
        for w in range(nw):
            for j, ch in enumerate(chips):
                got = _region(out_refs[w], 2 * ch[0] + ch[1], c, shapes[w], axes[w])
                cp = _remote(got, got, send.at[w, j], recv.at[w, j], sib)
                cp.start()
                copies.append(cp)
        for w in range(nw):
            for j, ch in enumerate(chips):
                got = _region(out_refs[w], 2 * ch[0] + ch[1], 1 - c, shapes[w], axes[w])
                _remote(got, got, send.at[w, j], recv.at[w, j], sib).wait_recv()
        for cp in copies:
            cp.wait_send()

    anyspec = pl.BlockSpec(memory_space=pl.ANY)
    return pl.pallas_call(
        body, name=name,
        out_shape=[jax.ShapeDtypeStruct(f.shape, BF16) for f in fulls],
        in_specs=[anyspec] * nw, out_specs=[anyspec] * nw,
        input_output_aliases={w: w for w in range(nw)},
        scratch_shapes=[pltpu.SemaphoreType.DMA((nw, 3)), pltpu.SemaphoreType.DMA((nw, 3))])(*fulls)


def _ag_forward_start(fulls, shapes, axes, name):
    nw = len(fulls)

    def body(*refs):
        in_refs, send, recv, token = refs[:nw], refs[nw], refs[nw + 1], refs[-1]
        x, y, c = _my_pos()
        for w in range(nw):
            for j, ch in enumerate(_other_chips(x, y)):
                got = _region(in_refs[w], 2 * ch[0] + ch[1], c, shapes[w], axes[w])
                _remote(got, got, send.at[3 * w + j], recv.at[3 * w + j], (x, y, 1 - c)).start()
        token[...] = jnp.zeros_like(token)

    return pl.pallas_call(
        body, name=name,
        out_shape=(pltpu.SemaphoreType.DMA((3 * nw,)), pltpu.SemaphoreType.DMA((3 * nw,)),
                   *[pltpu.HBM(f.shape, f.dtype) for f in fulls], jax.ShapeDtypeStruct((8, LANES), F32)),
        in_specs=[HBM_SPEC] * nw,
        out_specs=(SEM_SPEC, SEM_SPEC, *[HBM_SPEC] * nw, pl.BlockSpec(memory_space=pltpu.VMEM)),
        input_output_aliases={w: 2 + w for w in range(nw)},
        compiler_params=SPLIT_PARAMS)(*[_in_hbm(f) for f in fulls])


def _ag_forward_wait(send, recv, fulls, shapes, axes, after, name):
    nw = len(fulls)

    def body(*refs):
        in_refs, send_ref, recv_ref = refs[:nw], refs[nw], refs[nw + 1]
        x, y, c = _my_pos()
        for w in range(nw):
            for j, ch in enumerate(_other_chips(x, y)):
                kj = 2 * ch[0] + ch[1]
                sent = _region(in_refs[w], kj, c, shapes[w], axes[w])
                got = _region(in_refs[w], kj, 1 - c, shapes[w], axes[w])
                cp = _remote(sent, got, send_ref.at[3 * w + j], recv_ref.at[3 * w + j], (x, y, 1 - c))
                cp.wait_send()
                cp.wait_recv()

    return pl.pallas_call(
        body, name=name,
        out_shape=tuple(pltpu.HBM(f.shape, f.dtype) for f in fulls),
        in_specs=[HBM_SPEC] * nw + [SEM_SPEC, SEM_SPEC, pl.BlockSpec(memory_space=pl.ANY)],
        out_specs=tuple([HBM_SPEC] * nw),
        input_output_aliases={w: w for w in range(nw)},
        compiler_params=SPLIT_PARAMS)(*fulls, send, recv, after)


def _exchange_halves_start(shards, name):
    nw = len(shards)

    def body(*refs):
        in_refs, send, recv, token = refs[:nw], refs[nw], refs[nw + 1], refs[-1]
        x, y, c = _my_pos()
        for w in range(nw):
            hr = shards[w].shape[0] // 2
            mine = in_refs[w].at[pl.ds(pl.multiple_of(c * hr, 8), hr), :]
            _remote(mine, mine, send.at[w], recv.at[w], (x, y, 1 - c)).start()
        token[...] = jnp.zeros_like(token)

    return pl.pallas_call(
        body, name=name,
        out_shape=(pltpu.SemaphoreType.DMA((nw,)), pltpu.SemaphoreType.DMA((nw,)),
                   *[pltpu.HBM(s.shape, s.dtype) for s in shards], jax.ShapeDtypeStruct((8, LANES), F32)),
        in_specs=[HBM_SPEC] * nw,
        out_specs=(SEM_SPEC, SEM_SPEC, *[HBM_SPEC] * nw, pl.BlockSpec(memory_space=pltpu.VMEM)),
        input_output_aliases={w: 2 + w for w in range(nw)},
        compiler_params=SPLIT_PARAMS)(*[_in_hbm(s) for s in shards])


def _exchange_halves_wait(send, recv, shards, after, name):
    nw = len(shards)

    def body(*refs):
        in_refs, send_ref, recv_ref = refs[:nw], refs[nw], refs[nw + 1]
        x, y, c = _my_pos()
        for w in range(nw):
            hr = shards[w].shape[0] // 2
            mine = in_refs[w].at[pl.ds(pl.multiple_of(c * hr, 8), hr), :]
            other = in_refs[w].at[pl.ds(pl.multiple_of((1 - c) * hr, 8), hr), :]
            cp = _remote(mine, other, send_ref.at[w], recv_ref.at[w], (x, y, 1 - c))
            cp.wait_send()
            cp.wait_recv()

    return pl.pallas_call(
        body, name=name,
        out_shape=tuple(pltpu.HBM(s.shape, s.dtype) for s in shards),
        in_specs=[HBM_SPEC] * nw + [SEM_SPEC, SEM_SPEC, pl.BlockSpec(memory_space=pl.ANY)],
        out_specs=tuple([HBM_SPEC] * nw),
        input_output_aliases={w: w for w in range(nw)},
        compiler_params=SPLIT_PARAMS)(*shards, send, recv, after)


def _rs_sibling_start(grads, shapes, axes, name):
    nw = len(grads)
    npc = max(1 if a == 1 else N_CHIPS for a in axes)

    def body(*refs):
        g_refs, l_refs, send, recv, token = refs[:nw], refs[nw:2 * nw], refs[2 * nw], refs[2 * nw + 1], refs[-1]
        x, y, c = _my_pos()
        for w in range(nw):
            src = _half_pieces(g_refs[w], 1 - c, shapes[w], axes[w])
            dst = _half_pieces(l_refs[w], 1 - c, shapes[w], axes[w])
            for i, (s, d) in enumerate(zip(src, dst)):
                _remote(s, d, send.at[npc * w + i], recv.at[npc * w + i], (x, y, 1 - c)).start()
        token[...] = jnp.zeros_like(token)

    thru = [pltpu.HBM(g.shape, g.dtype) for g in grads]
    return pl.pallas_call(
        body, name=name,
        out_shape=(pltpu.SemaphoreType.DMA((npc * nw,)), pltpu.SemaphoreType.DMA((npc * nw,)), *thru, *thru,
                   jax.ShapeDtypeStruct((8, LANES), F32)),
        in_specs=[HBM_SPEC] * (2 * nw),
        out_specs=(SEM_SPEC, SEM_SPEC, *[HBM_SPEC] * (2 * nw), pl.BlockSpec(memory_space=pltpu.VMEM)),
        input_output_aliases={i: 2 + i for i in range(2 * nw)},
        compiler_params=SPLIT_PARAMS)(*[_in_hbm(g) for g in grads], *[_in_hbm(lax.empty(g.shape, g.dtype)) for g in grads])


def _rs_sibling_wait(send, recv, grads, lands, shapes, axes, after, name):
    nw = len(grads)
    npc = max(1 if a == 1 else N_CHIPS for a in axes)

    def body(*refs):
        g_refs, l_refs, send_ref, recv_ref = refs[:nw], refs[nw:2 * nw], refs[2 * nw], refs[2 * nw + 1]
        x, y, c = _my_pos()
        for w in range(nw):
            sent = _half_pieces(g_refs[w], 1 - c, shapes[w], axes[w])
            mine = _half_pieces(l_refs[w], c, shapes[w], axes[w])
            for i, (s, d) in enumerate(zip(sent, mine)):
                cp = _remote(s, d, send_ref.at[npc * w + i], recv_ref.at[npc * w + i], (x, y, 1 - c))
                cp.wait_send()
                cp.wait_recv()

    thru = tuple(pltpu.HBM(g.shape, g.dtype) for g in grads)
    return pl.pallas_call(
        body, name=name, out_shape=thru + thru,
        in_specs=[HBM_SPEC] * (2 * nw) + [SEM_SPEC, SEM_SPEC, pl.BlockSpec(memory_space=pl.ANY)],
        out_specs=tuple([HBM_SPEC] * (2 * nw)),
        input_output_aliases={i: i for i in range(2 * nw)},
        compiler_params=SPLIT_PARAMS)(*grads, *lands, send, recv, after)


def _rs_chips_start(sums, name):
    nw = len(sums)

    def body(*refs):
        s_refs, l_refs, send, recv, token = refs[:nw], refs[nw:2 * nw], refs[2 * nw], refs[2 * nw + 1], refs[-1]
        x, y, c = _my_pos()
        k0 = 2 * x + y
        for w in range(nw):
            for j, ch in enumerate(_other_chips(x, y)):
                _remote(s_refs[w].at[2 * ch[0] + ch[1]], l_refs[w].at[k0], send.at[3 * w + j], recv.at[3 * w + j],
                        (ch[0], ch[1], c)).start()
        token[...] = jnp.zeros_like(token)

    thru = [pltpu.HBM(s.shape, s.dtype) for s in sums]
    return pl.pallas_call(
        body, name=name,
        out_shape=(pltpu.SemaphoreType.DMA((3 * nw,)), pltpu.SemaphoreType.DMA((3 * nw,)), *thru, *thru,
                   jax.ShapeDtypeStruct((8, LANES), F32)),
        in_specs=[HBM_SPEC] * (2 * nw),
        out_specs=(SEM_SPEC, SEM_SPEC, *[HBM_SPEC] * (2 * nw), pl.BlockSpec(memory_space=pltpu.VMEM)),
        input_output_aliases={i: 2 + i for i in range(2 * nw)},
        compiler_params=SPLIT_PARAMS)(*[_in_hbm(s) for s in sums], *[_in_hbm(lax.empty(s.shape, s.dtype)) for s in sums])


def _rs_chips_wait(send, recv, sums, lands, after, name):
    nw = len(sums)

    def body(*refs):
        s_refs, l_refs, send_ref, recv_ref = refs[:nw], refs[nw:2 * nw], refs[2 * nw], refs[2 * nw + 1]
        x, y, c = _my_pos()
        for w in range(nw):
            for j, ch in enumerate(_other_chips(x, y)):
                kj = 2 * ch[0] + ch[1]
                cp = _remote(s_refs[w].at[kj], l_refs[w].at[kj], send_ref.at[3 * w + j], recv_ref.at[3 * w + j],
                             (ch[0], ch[1], c))
                cp.wait_send()
                cp.wait_recv()

    thru = tuple(pltpu.HBM(s.shape, s.dtype) for s in sums)
    return pl.pallas_call(
        body, name=name, out_shape=thru + thru,
        in_specs=[HBM_SPEC] * (2 * nw) + [SEM_SPEC, SEM_SPEC, pl.BlockSpec(memory_space=pl.ANY)],
        out_specs=tuple([HBM_SPEC] * (2 * nw)),
        input_output_aliases={i: i for i in range(2 * nw)},
        compiler_params=SPLIT_PARAMS)(*sums, *lands, send, recv, after)


LOSS_LANE = 64


def kernel(x, c, ctx, c_ctx, w_mod, b_mod, norm_mix, norm_ffn, w_in, ret_decay, attn_sink, w_out, w_gate, w_up, w_down, norm_final, loss_target, m_c_ctx, m_w_mod, m_b_mod, m_norm_mix, m_norm_ffn, m_w_in, m_ret_decay, m_attn_sink, m_w_out, m_w_gate, m_w_up, m_w_down, m_norm_final, v_c_ctx, v_w_mod, v_b_mod, v_norm_mix, v_norm_ffn, v_w_in, v_ret_decay, v_attn_sink, v_w_out, v_w_gate, v_w_up, v_w_down, v_norm_final):
    D = x.shape[-1]
    n3 = w_mod.shape[-1]
    xi, yi, ci = _my_pos()
    b = 4 * xi + 2 * yi + ci
    k0 = 2 * xi + yi
    cvec = jnp.reshape(ci, (1,)).astype(jnp.int32)
    kc = jnp.stack([k0, ci]).astype(jnp.int32)

    dense = [("w_in", w_in[0], 1), ("w_out", w_out[0], 0), ("w_gate", w_gate[0], 1), ("w_up", w_up[0], 1),
             ("w_down", w_down[0], 0)]
    axes = [a for _, _, a in dense]
    shapes = [w.shape for _, w, _ in dense]
    c_all = _allgather8(c, "gather_c").reshape(N_DEV, D)
    c_ctx2 = c_ctx.reshape(1, D)
    a16 = _pack_rows([(c_all, 0, N_DEV, 0, 0), (c_ctx2, 0, 1, N_DEV, 0)], 16, D, "pack_cond")
    b_cols = lax.dynamic_slice_in_dim(b_mod, k0 * n3, n3, axis=1)
    mod16 = _mod_rows(a16, w_mod[0], b_cols, "mod_rows")
    mod_all = _allgather8(mod16, "gather_mod")

    own_in = _cast_into_full(dense[0][1], kc, axes[0], "cast_w_in")
    agi = _ag_chips_start([own_in], shapes[:1], axes[:1], mod_all, "ag_in_start")
    casts = [_cast_into_full(w, kc, a, "cast_" + n, after=agi[-1]) for n, w, a in dense[1:]]
    own16 = [cst[0] for cst in casts]
    tables = _rope_tables(x.shape[1])
    behind = tables["Cr"][0:8] + tables["Sr"][0:8] + tables["Ca"][0:8] + tables["Sa"][0:8] + sum(cst[1] for cst in casts)
    (f_in,) = _ag_forward(list(_ag_chips_wait(agi[0], agi[1], [agi[2]], shapes[:1], axes[:1], behind, "ag_in_wait")),
                          shapes[:1], axes[:1], "ag_in_forward")
    ag = _ag_chips_start(own16, shapes[1:], axes[1:], f_in, "ag_rest_start")
    ag_send, ag_recv, ag_thru, ag_tok = ag[0], ag[1], list(ag[2:-1]), ag[-1][0:1, 0:1]

    fwd = []

    def rest_weights(after):
        if not fwd:
            landed_w = _ag_chips_wait(ag_send, ag_recv, ag_thru, shapes[1:], axes[1:], after, "ag_rest_wait")
            fwd.append(_ag_forward_start(list(landed_w), shapes[1:], axes[1:], "ag_rest_forward_start"))
            return fwd[0][-1][0:1, 0:1]
        st = fwd[0]
        return _ag_forward_wait(st[0], st[1], list(st[2:-1]), shapes[1:], axes[1:], after, "ag_rest_forward_wait")
    mine = jnp.stack([lax.dynamic_index_in_dim(mod_all, 2 * k + ci, 0, keepdims=False) for k in range(N_CHIPS)])
    mod = lax.dynamic_index_in_dim(mine, b, 1, keepdims=False).reshape(6, D)
    modc = mine[:, N_DEV].reshape(6, D)

    lg = -jnp.exp(ret_decay[0])

    index = {n: i for i, (n, _, _) in enumerate(dense)}
    pending, done = [], {}

    sib = []

    def finish_sibling(after):
        names, shp, axs, st = sib.pop()
        nw = len(names)
        res = _rs_sibling_wait(st[0], st[1], list(st[2:2 + nw]), list(st[2 + nw:2 + 2 * nw]), shp, axs, after,
                               "rs_sibling_wait_" + names[0])
        sums = [_add_halves(res[i], res[nw + i], cvec, s, a, "add_halves_" + n)
                for i, (s, a, n) in enumerate(zip(shp, axs, names))]
        ch = _rs_chips_start(sums, "rs_chips_start_" + names[0])
        pending.append((names, ch[0], ch[1], list(ch[2:2 + nw]), list(ch[2 + nw:2 + 2 * nw])))
        return ch[-1][0:1, 0:1]

    def on_grads(names, gs):
        ids = [index[n] for n in names]
        shp, axs = [shapes[i] for i in ids], [axes[i] for i in ids]
        st = _rs_sibling_start(gs, shp, axs, "rs_sibling_start_" + names[0])
        sib.append((names, shp, axs, st))
        return st[-1][0:1, 0:1]

    out = _local_step(x[0], ctx[0], loss_target[0], mod, modc, norm_mix + ag_tok, norm_ffn, norm_final.reshape(1, D), lg,
                      attn_sink, tables, f_in, rest_weights, on_grads, finish_sibling)

    def finish(group, after):
        names, send, recv, sums, lands = group
        res = _rs_chips_wait(send, recv, sums, lands, after, "rs_chips_wait_" + names[0])
        return [_sum_chips(res[i], res[len(names) + i], kc, "sum_chips_" + n) for i, n in enumerate(names)]

    tok_in = finish_sibling(out["grad_x"])
    assert pending[-1][0] == ["w_in"]
    rest_names = [n for g in pending[:-1] for n in g[0]]
    after_in = out["small"][0:8, 0:LANES] + tok_in
    rest_halves = [h for g in pending[:-1] for h in finish(g, after_in)]
    xch = _exchange_halves_start(rest_halves, "exchange_halves_rest_start")

    nh = 2 * RET_HEADS
    small_all = _allgather8(out["small"] + xch[-1][0:1, 0:1], "gather_small")
    tot = _sum_devices(small_all, "sum_small")
    g_b_mod = (tot[0:6] + tot[6:12]).reshape(1, 6 * D)
    dmodc_tot = tot[6:12].reshape(1, 6 * D)
    dmod_rows = small_all[:, 0:6].reshape(N_DEV, 6 * D)
    d16 = _pack_rows([(dmod_rows, 0, N_DEV, 0, 0), (dmodc_tot, 0, 1, N_DEV, 0)], 16, 6 * D, "pack_dmod")
    d16 = lax.dynamic_slice_in_dim(d16, k0 * n3, n3, axis=1)
    g_w_mod, dl_w_mod, m2_w_mod, v2_w_mod, part = _w_mod_update(a16, d16, w_mod[0], m_w_mod[0], v_w_mod[0])
    part_all = _allgather8(part[N_DEV:N_DEV + 1], "gather_c_ctx")
    g_c_ctx = _c_ctx_grad(part_all, c_ctx2)
    loss = tot[15, LOSS_LANE]

    def pack(cc, bm, nm, nf, nfin, rd, sk, name):
        rd2 = rd.reshape(2, RET_HEADS)
        return _pack_rows([(bm.reshape(6, D), 0, 6, 0, 0), (cc.reshape(1, D), 0, 1, 6, 0), (nm.reshape(1, D), 0, 1, 7, 0),
                           (nf.reshape(1, D), 0, 1, 8, 0), (nfin.reshape(1, D), 0, 1, 9, 0),
                           (rd2, 0, 1, 10, 0), (rd2, 1, 1, 10, RET_HEADS), (sk.reshape(1, ATT_HEADS), 0, 1, 10, nh)],
                          16, D, name)

    w_s = pack(c_ctx, b_mod, norm_mix, norm_ffn, norm_final, ret_decay, attn_sink, "pack_w")
    g_s = _pack_rows([(g_b_mod.reshape(6, D), 0, 6, 0, 0), (g_c_ctx, 0, 1, 6, 0), (tot, 12, 3, 7, 0),
                      (tot[15:16, 0:nh + ATT_HEADS], 0, 1, 10, 0)], 16, D, "pack_g")
    m_s = pack(m_c_ctx, m_b_mod, m_norm_mix, m_norm_ffn, m_norm_final, m_ret_decay, m_attn_sink, "pack_m")
    v_s = pack(v_c_ctx, v_b_mod, v_norm_mix, v_norm_ffn, v_norm_final, v_ret_decay, v_attn_sink, "pack_v")
    small_upd = _adam(w_s, g_s, m_s, v_s, "adam_small")

    def unpack(t):
        return dict(b_mod=t[0:6].reshape(1, 6 * D), c_ctx=t[6], norm_mix=t[7:8], norm_ffn=t[8:9], norm_final=t[9],
                    ret_decay=t[10, :nh].reshape(1, 2, RET_HEADS), attn_sink=t[10, nh:nh + ATT_HEADS].reshape(1, ATT_HEADS))

    dense_w = dict(w_in=(w_in, m_w_in, v_w_in), w_out=(w_out, m_w_out, v_w_out), w_gate=(w_gate, m_w_gate, v_w_gate),
                   w_up=(w_up, m_w_up, v_w_up), w_down=(w_down, m_w_down, v_w_down))
    grads = dict(unpack(g_s), w_mod=g_w_mod[None])
    upd = [dict(unpack(t)) for t in small_upd]
    upd[0]["w_mod"], upd[1]["w_mod"], upd[2]["w_mod"] = dl_w_mod[None], m2_w_mod[None], v2_w_mod[None]
    def update(n, g):
        w_, m_, v_ = dense_w[n]
        res = _adam(w_[0], g, m_[0], v_[0], "adam_" + n)
        grads[n] = g[None]
        for u, r_ in zip(upd, res):
            u[n] = r_[None]
        return res[0]

    dep = small_upd[0][0:8, 0:LANES] + dl_w_mod[0:8, 0:LANES]
    g_rest = dict(zip(rest_names, _exchange_halves_wait(xch[0], xch[1], list(xch[2:-1]), dep, "exchange_halves_rest_wait")))
    dep = dep[0:1, 0:1]
    for n in rest_names:
        dep = dep + update(n, g_rest[n])[0:1, 0:1]
    (g_in,) = _exchange_halves(finish(pending[-1], dep), "exchange_halves_in")
    update("w_in", g_in)

    order = ['c_ctx', 'w_mod', 'b_mod', 'norm_mix', 'norm_ffn', 'w_in', 'ret_decay', 'attn_sink', 'w_out', 'w_gate',
             'w_up', 'w_down', 'norm_final']
    outs = [loss, out["grad_x"][None]] + [grads[n] for n in order]
    for u in upd:
        outs += [u[n] for n in order]
    return tuple(outs)
```
